```python
import jax, jax.numpy as jnp
from jax import lax
import numpy as np

D_MODEL = 2048
BATCH = 8
SEQ = 2048
DEPTH = 2

HEAD_DIM = 128
N_MIX_HEADS = D_MODEL // HEAD_DIM
A_HEADS = N_MIX_HEADS // 4
B_HEADS = (N_MIX_HEADS - A_HEADS) // 2
C_HEADS = N_MIX_HEADS - A_HEADS - B_HEADS
C_KV_HEADS = 2
A_KEY_DIM = 128
A_CHUNK = 16
DILATED_PATTERNS = ((128, 1), (512, 4), (2048, 16))
C_WINDOW = 128
BLOCK = 128
ROPE_THETA = 500000.0
ROPE_DIM = HEAD_DIM // 4
D_FF = 5632
CONV_WIDTH = 3
LN_EPS = 1e-5
ALPHA = (2 * DEPTH) ** 0.25
BETA = (8 * DEPTH) ** -0.25

A_KEY_WIDTH = A_HEADS * A_KEY_DIM
A_WIDTH = A_HEADS * HEAD_DIM
B_WIDTH = B_HEADS * HEAD_DIM
C_WIDTH = C_HEADS * HEAD_DIM
C_KV_WIDTH = C_KV_HEADS * HEAD_DIM
MIX_WIDTH = A_WIDTH + B_WIDTH + C_WIDTH
PROJ_SIZES = (A_KEY_WIDTH, A_KEY_WIDTH, A_WIDTH, A_WIDTH,
              B_WIDTH, B_WIDTH, B_WIDTH,
              C_WIDTH, C_KV_WIDTH, C_KV_WIDTH)
VALUE_BLOCKS = (2, 6, 9)
IN_WIDTH = sum(PROJ_SIZES)

kernel_name = "hybrid_hgrn2_dilated_swa_sink_convffn"


def layer_norm(x, g, b):
    xf = x.astype(jnp.float32)
    mu = jnp.mean(xf, -1, keepdims=True)
    var = jnp.mean(jnp.square(xf - mu), -1, keepdims=True)
    return ((xf - mu) * lax.rsqrt(var + LN_EPS) * g.astype(jnp.float32) + b.astype(jnp.float32)).astype(x.dtype)


def rms_norm(x, g):
    xf = x.astype(jnp.float32)
    return xf * lax.rsqrt(jnp.mean(jnp.square(xf), -1, keepdims=True) + LN_EPS) * g.astype(jnp.float32)


def rope_tables(seq):
    inv = ROPE_THETA ** (-jnp.arange(0, ROPE_DIM, 2, dtype=jnp.float32) / ROPE_DIM)
    ang = jnp.arange(seq, dtype=jnp.float32)[:, None] * inv[None, :]
    return jnp.cos(ang), jnp.sin(ang)


def partial_rope(x, cos, sin):
    half = ROPE_DIM // 2
    xr = x[..., :ROPE_DIM].astype(jnp.float32)
    x1, x2 = xr[..., :half], xr[..., half:]
    rot = jnp.concatenate([x1 * cos - x2 * sin, x2 * cos + x1 * sin], -1)
    return jnp.concatenate([rot.astype(x.dtype), x[..., ROPE_DIM:]], -1)


def banded_attention(q, k, v, max_lag, sink=None):
    b, h, L, dh = q.shape
    nb = -(-L // BLOCK)
    pad = nb * BLOCK - L
    padw = ((0, 0), (0, 0), (0, pad), (0, 0))
    qb = jnp.pad(q, padw).reshape(b, h, nb, BLOCK, dh)
    kb = jnp.pad(k, padw).reshape(b, h, nb, BLOCK, dh)
    vb = jnp.pad(v, padw).reshape(b, h, nb, BLOCK, dh)
    shift = ((0, 0), (0, 0), (1, 0), (0, 0), (0, 0))
    kk = jnp.concatenate([jnp.pad(kb, shift)[:, :, :-1], kb], axis=3)
    vv = jnp.concatenate([jnp.pad(vb, shift)[:, :, :-1], vb], axis=3)
    s = jnp.einsum('bhnqd,bhnkd->bhnqk', qb, kk, preferred_element_type=jnp.float32) * (dh ** -0.5)
    r = jnp.arange(BLOCK)[:, None]
    c = jnp.arange(2 * BLOCK)[None, :]
    lag = BLOCK + r - c
    kpos = (jnp.arange(nb)[:, None, None] - 1) * BLOCK + c[None]
    valid = ((lag >= 0) & (lag <= max_lag))[None] & (kpos >= 0)
    s = jnp.where(valid, s, -jnp.inf)
    m = jnp.max(s, -1)
    if sink is not None:
        sk = sink.astype(jnp.float32)[None, :, None, None]
        m = jnp.maximum(m, sk)
    e = jnp.exp(s - m[..., None])
    denom = jnp.sum(e, -1)
    if sink is not None:
        denom = denom + jnp.exp(sk - m)
    out = jnp.einsum('bhnqk,bhnkd->bhnqd', e, vv.astype(jnp.float32)) / denom[..., None]
    lse = m + jnp.log(denom)
    out = out.reshape(b, h, nb * BLOCK, dh)[:, :, :L].astype(q.dtype)
    return out, lse.reshape(b, h, nb * BLOCK)[:, :, :L]


def dilated_attention(q, k, v):
    b, h, S, dh = q.shape
    outs, lses = [], []
    for window, dil in DILATED_PATTERNS:
        L = S // dil
        def split(t):
            return t.reshape(b, h, L, dil, dh).transpose(0, 1, 3, 2, 4).reshape(b, h * dil, L, dh)
        o, lse = banded_attention(split(q), split(k), split(v), window // dil)
        outs.append(o.reshape(b, h, dil, L, dh).transpose(0, 1, 3, 2, 4).reshape(b, h, S, dh))
        lses.append(lse.reshape(b, h, dil, L).transpose(0, 1, 3, 2).reshape(b, h, S))
    w = jax.nn.softmax(jnp.stack(lses, 0), axis=0)
    out = jnp.einsum('pbhs,pbhsd->bhsd', w, jnp.stack(outs, 0).astype(jnp.float32))
    return out.astype(q.dtype)


def hgrn2(q, f_logit, i, g, lb, norm_w):
    b, S, _ = q.shape
    n = S // A_CHUNK
    def chunks(t, dim):
        return t.astype(jnp.float32).reshape(b, n, A_CHUNK, A_HEADS, dim).transpose(0, 3, 1, 2, 4)
    qc = jax.nn.silu(chunks(q, A_KEY_DIM))
    fg = lb.astype(jnp.float32).reshape(A_HEADS, 1, 1, A_KEY_DIM) + \
        (1.0 - lb.astype(jnp.float32).reshape(A_HEADS, 1, 1, A_KEY_DIM)) * jax.nn.sigmoid(chunks(f_logit, A_KEY_DIM))
    kc = 1.0 - fg
    vc = chunks(i, HEAD_DIM)
    bcum = jnp.cumsum(jnp.log(fg), axis=3)
    causal = jnp.tril(jnp.ones((A_CHUNK, A_CHUNK), dtype=bool))
    diff = bcum[:, :, :, :, None, :] - bcum[:, :, :, None, :, :]
    decay = jnp.exp(jnp.where(causal[:, :, None], diff, -jnp.inf))
    att = jnp.einsum('bhnik,bhnjk,bhnijk->bhnij', qc, kc, decay)
    o_intra = jnp.einsum('bhnij,bhnjv->bhniv', att, vc)
    blast = bcum[:, :, :, -1:, :]
    upd = jnp.einsum('bhnck,bhncv->bhnkv', kc * jnp.exp(blast - bcum), vc)
    cdec = jnp.exp(blast[:, :, :, 0, :])
    def step(state, inp):
        dec, u = inp
        return dec[..., None] * state + u, state
    s0 = jnp.zeros((b, A_HEADS, A_KEY_DIM, HEAD_DIM), jnp.float32)
    _, s_prev = lax.scan(step, s0, (jnp.moveaxis(cdec, 2, 0), jnp.moveaxis(upd, 2, 0)))
    s_prev = jnp.moveaxis(s_prev, 0, 2)
    o_inter = jnp.einsum('bhnck,bhnkv->bhncv', qc * jnp.exp(bcum), s_prev)
    o = (o_intra + o_inter).transpose(0, 2, 3, 1, 4).reshape(b, S, A_HEADS, HEAD_DIM)
    o = rms_norm(o, norm_w).reshape(b, S, A_WIDTH)
    return (o * jax.nn.silu(g.astype(jnp.float32))).astype(q.dtype)


def mixer_sublayer(x, w_in, lb, a_norm_w, c_sink, w_out, cos, sin):
    b, S, _ = x.shape
    proj = x @ w_in
    idx = [int(t) for t in np.cumsum(PROJ_SIZES)[:-1]]
    qA, fA, iA, gA, qB, kB, vB, qC, kC, vC = jnp.split(proj, idx, axis=-1)
    def heads(t, nh):
        return t.reshape(b, S, nh, HEAD_DIM).transpose(0, 2, 1, 3)
    def merge(t):
        return t.transpose(0, 2, 1, 3).reshape(b, S, -1)
    oA = hgrn2(qA, fA, iA, gA, lb, a_norm_w)
    oB = dilated_attention(partial_rope(heads(qB, B_HEADS), cos, sin),
                           partial_rope(heads(kB, B_HEADS), cos, sin), heads(vB, B_HEADS))
    rep = C_HEADS // C_KV_HEADS
    kCh = jnp.repeat(partial_rope(heads(kC, C_KV_HEADS), cos, sin), rep, axis=1)
    vCh = jnp.repeat(heads(vC, C_KV_HEADS), rep, axis=1)
    oC, _ = banded_attention(partial_rope(heads(qC, C_HEADS), cos, sin), kCh, vCh, C_WINDOW - 1, c_sink)
    mixed = jnp.concatenate([oA, merge(oB), merge(oC)], axis=-1)
    return mixed @ w_out


def conv_ffn(x, w_gate, w_up, conv_w, conv_b, w_down):
    S = x.shape[1]
    g = x @ w_gate
    u = x @ w_up
    gp = jnp.pad(g, ((0, 0), (CONV_WIDTH - 1, 0), (0, 0)))
    gc = conv_b
    for j in range(CONV_WIDTH):
        gc = gc + conv_w[j] * gp[:, j:j + S]
    return (jax.nn.silu(gc) * u) @ w_down


def _fwd_setup_inputs(seed: int = 0) -> dict:
    key = jax.random.key(seed)
    ks = jax.random.split(key, 24)
    f32 = jnp.float32
    x = jax.random.normal(ks[0], (BATCH, SEQ, D_MODEL), f32)
    blocks = []
    for j, size in enumerate(PROJ_SIZES):
        w = jax.random.normal(ks[1 + j], (DEPTH, D_MODEL, size), f32) * D_MODEL ** -0.5
        if j in VALUE_BLOCKS:
            w = w * BETA
        blocks.append(w)
    w_in = jnp.concatenate(blocks, axis=-1)
    lb_logits = jax.random.normal(ks[11], (DEPTH, A_KEY_WIDTH), f32)
    a_norm_w = 1.0 + 0.02 * jax.random.normal(ks[12], (DEPTH, HEAD_DIM), f32)
    c_sinks = jax.random.normal(ks[13], (DEPTH, C_HEADS), f32)
    w_out = jax.random.normal(ks[14], (DEPTH, MIX_WIDTH, D_MODEL), f32) * (MIX_WIDTH ** -0.5) * BETA
    ln1_g = 1.0 + 0.02 * jax.random.normal(ks[15], (DEPTH, D_MODEL), f32)
    ln1_b = 0.02 * jax.random.normal(ks[16], (DEPTH, D_MODEL), f32)
    w_gate = jax.random.normal(ks[17], (DEPTH, D_MODEL, D_FF), f32) * D_MODEL ** -0.5
    w_up = jax.random.normal(ks[18], (DEPTH, D_MODEL, D_FF), f32) * (D_MODEL ** -0.5) * BETA
    conv_w = jax.random.normal(ks[19], (DEPTH, CONV_WIDTH, D_FF), f32) * CONV_WIDTH ** -0.5
    conv_b = 0.02 * jax.random.normal(ks[20], (DEPTH, D_FF), f32)
    w_down = jax.random.normal(ks[21], (DEPTH, D_FF, D_MODEL), f32) * (D_FF ** -0.5) * BETA
    ln2_g = 1.0 + 0.02 * jax.random.normal(ks[22], (DEPTH, D_MODEL), f32)
    ln2_b = 0.02 * jax.random.normal(ks[23], (DEPTH, D_MODEL), f32)
    return {"x": x, "w_in": w_in, "lb_logits": lb_logits, "a_norm_w": a_norm_w, "c_sinks": c_sinks,
            "w_out": w_out, "ln1_g": ln1_g, "ln1_b": ln1_b, "w_gate": w_gate, "w_up": w_up,
            "conv_w": conv_w, "conv_b": conv_b, "w_down": w_down, "ln2_g": ln2_g, "ln2_b": ln2_b}


def _fwd_reference(x, w_in, lb_logits, a_norm_w, c_sinks, w_out, ln1_g, ln1_b, w_gate, w_up,
              conv_w, conv_b, w_down, ln2_g, ln2_b):
    cos, sin = rope_tables(x.shape[1])
    lbs = jnp.cumsum(jax.nn.softmax(lb_logits.astype(jnp.float32), axis=0), axis=0)
    lbs = lbs - lbs[0]
    for l in range(DEPTH):
        y = mixer_sublayer(x, w_in[l], lbs[l], a_norm_w[l], c_sinks[l], w_out[l], cos, sin)
        x = layer_norm(ALPHA * x + y, ln1_g[l], ln1_b[l])
        y = conv_ffn(x, w_gate[l], w_up[l], conv_w[l], conv_b[l], w_down[l])
        x = layer_norm(ALPHA * x + y, ln2_g[l], ln2_b[l])
    return x


import jax as _jax
import jax.numpy as _jnp

TWIN_FORMAT = 'train_step'
FWD_PARAMS = ['x', 'w_in', 'lb_logits', 'a_norm_w', 'c_sinks', 'w_out', 'ln1_g', 'ln1_b', 'w_gate', 'w_up', 'conv_w', 'conv_b', 'w_down', 'ln2_g', 'ln2_b']
TWIN_WEIGHTS = ['w_in', 'lb_logits', 'a_norm_w', 'c_sinks', 'w_out', 'ln1_g', 'ln1_b', 'w_gate', 'w_up', 'conv_w', 'conv_b', 'w_down', 'ln2_g', 'ln2_b']
TWIN_DIFF_INPUT = 'x'
TWIN_INPUTS = ['x', 'w_in', 'lb_logits', 'a_norm_w', 'c_sinks', 'w_out', 'ln1_g', 'ln1_b', 'w_gate', 'w_up', 'conv_w', 'conv_b', 'w_down', 'ln2_g', 'ln2_b', 'loss_target', 'm_w_in', 'm_lb_logits', 'm_a_norm_w', 'm_c_sinks', 'm_w_out', 'm_ln1_g', 'm_ln1_b', 'm_w_gate', 'm_w_up', 'm_conv_w', 'm_conv_b', 'm_w_down', 'm_ln2_g', 'm_ln2_b', 'v_w_in', 'v_lb_logits', 'v_a_norm_w', 'v_c_sinks', 'v_w_out', 'v_ln1_g', 'v_ln1_b', 'v_w_gate', 'v_w_up', 'v_conv_w', 'v_conv_b', 'v_w_down', 'v_ln2_g', 'v_ln2_b']
TWIN_OUTPUTS = ['loss', 'grad_x', 'grad_w_in', 'grad_lb_logits', 'grad_a_norm_w', 'grad_c_sinks', 'grad_w_out', 'grad_ln1_g', 'grad_ln1_b', 'grad_w_gate', 'grad_w_up', 'grad_conv_w', 'grad_conv_b', 'grad_w_down', 'grad_ln2_g', 'grad_ln2_b', 'delta_w_in', 'delta_lb_logits', 'delta_a_norm_w', 'delta_c_sinks', 'delta_w_out', 'delta_ln1_g', 'delta_ln1_b', 'delta_w_gate', 'delta_w_up', 'delta_conv_w', 'delta_conv_b', 'delta_w_down', 'delta_ln2_g', 'delta_ln2_b', 'new_m_w_in', 'new_m_lb_logits', 'new_m_a_norm_w', 'new_m_c_sinks', 'new_m_w_out', 'new_m_ln1_g', 'new_m_ln1_b', 'new_m_w_gate', 'new_m_w_up', 'new_m_conv_w', 'new_m_conv_b', 'new_m_w_down', 'new_m_ln2_g', 'new_m_ln2_b', 'new_v_w_in', 'new_v_lb_logits', 'new_v_a_norm_w', 'new_v_c_sinks', 'new_v_w_out', 'new_v_ln1_g', 'new_v_ln1_b', 'new_v_w_gate', 'new_v_w_up', 'new_v_conv_w', 'new_v_conv_b', 'new_v_w_down', 'new_v_ln2_g', 'new_v_ln2_b']
TWIN_LEAF_KINDS = {'loss': 'loss', 'grad_x': 'grad_x', 'grad_w_in': 'grad_w', 'grad_lb_logits': 'grad_w', 'grad_a_norm_w': 'grad_w', 'grad_c_sinks': 'grad_w', 'grad_w_out': 'grad_w', 'grad_ln1_g': 'grad_w', 'grad_ln1_b': 'grad_w', 'grad_w_gate': 'grad_w', 'grad_w_up': 'grad_w', 'grad_conv_w': 'grad_w', 'grad_conv_b': 'grad_w', 'grad_w_down': 'grad_w', 'grad_ln2_g': 'grad_w', 'grad_ln2_b': 'grad_w', 'delta_w_in': 'delta_w', 'delta_lb_logits': 'delta_w', 'delta_a_norm_w': 'delta_w', 'delta_c_sinks': 'delta_w', 'delta_w_out': 'delta_w', 'delta_ln1_g': 'delta_w', 'delta_ln1_b': 'delta_w', 'delta_w_gate': 'delta_w', 'delta_w_up': 'delta_w', 'delta_conv_w': 'delta_w', 'delta_conv_b': 'delta_w', 'delta_w_down': 'delta_w', 'delta_ln2_g': 'delta_w', 'delta_ln2_b': 'delta_w', 'new_m_w_in': 'new_m', 'new_m_lb_logits': 'new_m', 'new_m_a_norm_w': 'new_m', 'new_m_c_sinks': 'new_m', 'new_m_w_out': 'new_m', 'new_m_ln1_g': 'new_m', 'new_m_ln1_b': 'new_m', 'new_m_w_gate': 'new_m', 'new_m_w_up': 'new_m', 'new_m_conv_w': 'new_m', 'new_m_conv_b': 'new_m', 'new_m_w_down': 'new_m', 'new_m_ln2_g': 'new_m', 'new_m_ln2_b': 'new_m', 'new_v_w_in': 'new_v', 'new_v_lb_logits': 'new_v', 'new_v_a_norm_w': 'new_v', 'new_v_c_sinks': 'new_v', 'new_v_w_out': 'new_v', 'new_v_ln1_g': 'new_v', 'new_v_ln1_b': 'new_v', 'new_v_w_gate': 'new_v', 'new_v_w_up': 'new_v', 'new_v_conv_w': 'new_v', 'new_v_conv_b': 'new_v', 'new_v_w_down': 'new_v', 'new_v_ln2_g': 'new_v', 'new_v_ln2_b': 'new_v'}


def _forward(args):
    return _fwd_reference(*[args[k] for k in FWD_PARAMS])


def _output_shape():
    out = _jax.eval_shape(lambda: _forward(_fwd_setup_inputs(0)))
    return out.shape, out.dtype

N_MICROBATCH = 1
ADAM_LR = 0.001
ADAM_B1 = 0.9
ADAM_B2 = 0.999
ADAM_EPS = 1e-08
ADAM_WD = 0.01
ADAM_STEP = 10
PER_EXAMPLE_BATCH_AXIS = {'x': 0, 'loss_target': 0}
SHARED_INPUTS = []
_WEIGHT_DTYPES = {'w_in': _jnp.float32, 'lb_logits': _jnp.float32, 'a_norm_w': _jnp.float32, 'c_sinks': _jnp.float32, 'w_out': _jnp.float32, 'ln1_g': _jnp.float32, 'ln1_b': _jnp.float32, 'w_gate': _jnp.float32, 'w_up': _jnp.float32, 'conv_w': _jnp.float32, 'conv_b': _jnp.float32, 'w_down': _jnp.float32, 'ln2_g': _jnp.float32, 'ln2_b': _jnp.float32}
MOMENT_SCALE = {'w_in': 9.567082e-03, 'lb_logits': 1.115428e-03, 'a_norm_w': 2.753221e-02, 'c_sinks': 3.351644e-03, 'w_out': 1.400142e-02, 'ln1_g': 2.720179e-01, 'ln1_b': 1.470713e-01, 'w_gate': 4.170132e-03, 'w_up': 8.111540e-03, 'conv_w': 4.256200e-03, 'conv_b': 4.047647e-03, 'w_down': 1.346642e-02, 'ln2_g': 5.673188e+00, 'ln2_b': 2.209858e-01}


def _to_microbatches(a, axis):
    t = _jnp.moveaxis(a, axis, 0)
    t = t.reshape((N_MICROBATCH, t.shape[0] // N_MICROBATCH) + t.shape[1:])
    return _jnp.moveaxis(t, 1, axis + 1)


def setup_inputs(seed: int = 0) -> dict:
    inp = _fwd_setup_inputs(seed)
    key = _jax.random.fold_in(_jax.random.key(seed), 7919)
    shape, _ = _output_shape()
    out = dict(inp)
    out["loss_target"] = _jax.random.normal(_jax.random.fold_in(key, 0), shape, _jnp.float32)
    for i, name in enumerate(TWIN_WEIGHTS):
        w = inp[name].astype(_jnp.float32)
        if MOMENT_SCALE is None:
            s = _jnp.sqrt(_jnp.mean(_jnp.square(w)) + 1e-30)
        else:
            s = MOMENT_SCALE[name]
        km, kv = _jax.random.split(_jax.random.fold_in(key, i + 1))
        out[name] = w
        out["m_" + name] = s * _jax.random.normal(km, w.shape, _jnp.float32)
        out["v_" + name] = (s * s) * _jax.random.uniform(kv, w.shape, _jnp.float32, 0.5, 1.5)
    if N_MICROBATCH > 1:
        for name, axis in PER_EXAMPLE_BATCH_AXIS.items():
            out[name] = _to_microbatches(out[name], axis)
    return {'x': out['x'], 'w_in': out['w_in'], 'lb_logits': out['lb_logits'], 'a_norm_w': out['a_norm_w'], 'c_sinks': out['c_sinks'], 'w_out': out['w_out'], 'ln1_g': out['ln1_g'], 'ln1_b': out['ln1_b'], 'w_gate': out['w_gate'], 'w_up': out['w_up'], 'conv_w': out['conv_w'], 'conv_b': out['conv_b'], 'w_down': out['w_down'], 'ln2_g': out['ln2_g'], 'ln2_b': out['ln2_b'], 'loss_target': out['loss_target'], 'm_w_in': out['m_w_in'], 'm_lb_logits': out['m_lb_logits'], 'm_a_norm_w': out['m_a_norm_w'], 'm_c_sinks': out['m_c_sinks'], 'm_w_out': out['m_w_out'], 'm_ln1_g': out['m_ln1_g'], 'm_ln1_b': out['m_ln1_b'], 'm_w_gate': out['m_w_gate'], 'm_w_up': out['m_w_up'], 'm_conv_w': out['m_conv_w'], 'm_conv_b': out['m_conv_b'], 'm_w_down': out['m_w_down'], 'm_ln2_g': out['m_ln2_g'], 'm_ln2_b': out['m_ln2_b'], 'v_w_in': out['v_w_in'], 'v_lb_logits': out['v_lb_logits'], 'v_a_norm_w': out['v_a_norm_w'], 'v_c_sinks': out['v_c_sinks'], 'v_w_out': out['v_w_out'], 'v_ln1_g': out['v_ln1_g'], 'v_ln1_b': out['v_ln1_b'], 'v_w_gate': out['v_w_gate'], 'v_w_up': out['v_w_up'], 'v_conv_w': out['v_conv_w'], 'v_conv_b': out['v_conv_b'], 'v_w_down': out['v_w_down'], 'v_ln2_g': out['v_ln2_g'], 'v_ln2_b': out['v_ln2_b']}


def _loss(weights, diff, rest, loss_target):
    with _jax.named_scope("forward"):
        args = {**rest, TWIN_DIFF_INPUT: diff, **{k: w.astype(_WEIGHT_DTYPES[k]) for k, w in weights.items()}}
        y = _forward(args)
    with _jax.named_scope("loss_head"):
        err = _jnp.square(y.astype(_jnp.float32) - loss_target)
        return 0.5 * _jnp.sum(_jnp.mean(err, axis=-1)) if err.ndim else 0.5 * err


def _adamw(w, g, m, v):
    m = ADAM_B1 * m + (1.0 - ADAM_B1) * g
    v = ADAM_B2 * v + (1.0 - ADAM_B2) * _jnp.square(g)
    m_hat = m / (1.0 - ADAM_B1 ** ADAM_STEP)
    v_hat = v / (1.0 - ADAM_B2 ** ADAM_STEP)
    delta = -ADAM_LR * (m_hat / (_jnp.sqrt(v_hat) + ADAM_EPS) + ADAM_WD * w)
    return delta, m, v


def reference(x, w_in, lb_logits, a_norm_w, c_sinks, w_out, ln1_g, ln1_b, w_gate, w_up, conv_w, conv_b, w_down, ln2_g, ln2_b, loss_target, m_w_in, m_lb_logits, m_a_norm_w, m_c_sinks, m_w_out, m_ln1_g, m_ln1_b, m_w_gate, m_w_up, m_conv_w, m_conv_b, m_w_down, m_ln2_g, m_ln2_b, v_w_in, v_lb_logits, v_a_norm_w, v_c_sinks, v_w_out, v_ln1_g, v_ln1_b, v_w_gate, v_w_up, v_conv_w, v_conv_b, v_w_down, v_ln2_g, v_ln2_b):
    given = dict(x=x, w_in=w_in, lb_logits=lb_logits, a_norm_w=a_norm_w, c_sinks=c_sinks, w_out=w_out, ln1_g=ln1_g, ln1_b=ln1_b, w_gate=w_gate, w_up=w_up, conv_w=conv_w, conv_b=conv_b, w_down=w_down, ln2_g=ln2_g, ln2_b=ln2_b, loss_target=loss_target, m_w_in=m_w_in, m_lb_logits=m_lb_logits, m_a_norm_w=m_a_norm_w, m_c_sinks=m_c_sinks, m_w_out=m_w_out, m_ln1_g=m_ln1_g, m_ln1_b=m_ln1_b, m_w_gate=m_w_gate, m_w_up=m_w_up, m_conv_w=m_conv_w, m_conv_b=m_conv_b, m_w_down=m_w_down, m_ln2_g=m_ln2_g, m_ln2_b=m_ln2_b, v_w_in=v_w_in, v_lb_logits=v_lb_logits, v_a_norm_w=v_a_norm_w, v_c_sinks=v_c_sinks, v_w_out=v_w_out, v_ln1_g=v_ln1_g, v_ln1_b=v_ln1_b, v_w_gate=v_w_gate, v_w_up=v_w_up, v_conv_w=v_conv_w, v_conv_b=v_conv_b, v_w_down=v_w_down, v_ln2_g=v_ln2_g, v_ln2_b=v_ln2_b)
    weights = {n: given[n] for n in TWIN_WEIGHTS}
    shared = {n: given[n] for n in SHARED_INPUTS}
    per_example = {n: given[n] for n in ['x']}
    grad_fn = _jax.value_and_grad(_loss, argnums=(0, 1))

    def one_microbatch(ex, loss_target):
        ex = dict(ex)
        diff = ex.pop(TWIN_DIFF_INPUT)
        return grad_fn(weights, diff, {**shared, **ex}, loss_target)

    if N_MICROBATCH == 1:
        loss, (grad_w, grad_x) = one_microbatch(per_example, given["loss_target"])
    else:
        def body(carry, xs):
            loss_sum, grad_sum = carry
            l_k, (gw_k, gx_k) = one_microbatch(xs[0], xs[1])
            with _jax.named_scope("update"):
                return (loss_sum + l_k, _jax.tree.map(_jnp.add, grad_sum, gw_k)), gx_k

        init = (_jnp.zeros((), _jnp.float32), _jax.tree.map(_jnp.zeros_like, weights))
        (loss, grad_w), grad_x = _jax.lax.scan(body, init, (per_example, given["loss_target"]))
    with _jax.named_scope("update"):
        delta_w, new_m, new_v = {}, {}, {}
        for n in TWIN_WEIGHTS:
            delta_w[n], new_m[n], new_v[n] = _adamw(weights[n], grad_w[n], given["m_" + n], given["v_" + n])
    return (loss, grad_x, *[grad_w[n] for n in TWIN_WEIGHTS], *[delta_w[n] for n in TWIN_WEIGHTS],
            *[new_m[n] for n in TWIN_WEIGHTS], *[new_v[n] for n in TWIN_WEIGHTS])
```

```python
import functools

import jax
import jax.numpy as jnp
import numpy as np
from jax import lax
from jax.experimental import pallas as pl
from jax.experimental.pallas import tpu as pltpu

D_MODEL = 2048
SEQ = 2048
DEPTH = 2
HEAD_DIM = 128
A_HEADS = 4
B_HEADS = 6
C_HEADS = 6
C_KV_HEADS = 2
A_CHUNK = 16
DILATIONS = (1, 4, 16)
BLOCK = 128
ROPE_THETA = 500000.0
ROPE_DIM = 32
D_FF = 5632
IN_WIDTH = 5632
LN_EPS = 1e-5
ALPHA = (2 * DEPTH) ** 0.25
N_DEV = 8
SHARD_COLS = IN_WIDTH // N_DEV

ADAM_LR = 0.001
ADAM_B1 = 0.9
ADAM_B2 = 0.999
ADAM_EPS = 1e-08
ADAM_WD = 0.01
ADAM_STEP = 10

A_COLS = 16
QKV_COLS = 28
QB0, KB0, VB0, QC0, KC0, VC0 = 0, 6, 12, 18, 24, 26

_MXU_DTYPE = jnp.bfloat16
_NEG = -1e30
_VMEM_LIMIT = 56 * 2 ** 20

_F32 = jnp.float32


def _sigmoid(x):
    return 1.0 / (1.0 + jnp.exp(-x))


def _cparams(**kw):
    return pltpu.CompilerParams(vmem_limit_bytes=_VMEM_LIMIT, **kw)


def _mm(a, b, *, ta=False, tb=False, tm, tn, out_dtype=_F32, add=None, add_scale=1.0, name):
    K = a.shape[0] if ta else a.shape[1]
    M = a.shape[1] if ta else a.shape[0]
    N = b.shape[0] if tb else b.shape[1]
    assert (b.shape[1] if tb else b.shape[0]) == K and M % tm == 0 and N % tn == 0
    dn = (((0 if ta else 1,), (1 if tb else 0,)), ((), ()))

    def body(*refs):
        if add is None:
            a_ref, b_ref, o_ref = refs
        else:
            a_ref, b_ref, add_ref, o_ref = refs
        r = lax.dot_general(a_ref[...], b_ref[...], dn, preferred_element_type=_F32)
        if add is not None:
            r = r + add_scale * add_ref[...]
        o_ref[...] = r.astype(o_ref.dtype)

    a_spec = pl.BlockSpec((K, tm), lambda i, j: (0, i)) if ta else pl.BlockSpec((tm, K), lambda i, j: (i, 0))
    b_spec = pl.BlockSpec((tn, K), lambda i, j: (j, 0)) if tb else pl.BlockSpec((K, tn), lambda i, j: (0, j))
    o_spec = pl.BlockSpec((tm, tn), lambda i, j: (i, j))
    in_specs = [a_spec, b_spec] + ([o_spec] if add is not None else [])
    args = (a, b) + ((add,) if add is not None else ())
    return pl.pallas_call(
        body, grid=(M // tm, N // tn), in_specs=in_specs, out_specs=o_spec,
        out_shape=jax.ShapeDtypeStruct((M, N), out_dtype), name=name,
        compiler_params=_cparams(dimension_semantics=("parallel", "parallel")),
    )(*args)


def _mm_tn_slabs(a, b, *, tm, name):
    K, M = a.shape
    assert b.shape == (N_DEV, K, SHARD_COLS) and M % tm == 0

    def body(a_ref, b_ref, o_ref):
        o_ref[...] = lax.dot_general(a_ref[...], b_ref[...], _TN, preferred_element_type=_F32)

    return pl.pallas_call(
        body, grid=(M // tm, N_DEV),
        in_specs=[pl.BlockSpec((K, tm), lambda i, j: (0, i)), pl.BlockSpec((None, K, SHARD_COLS), lambda i, j: (j, 0, 0))],
        out_specs=pl.BlockSpec((None, tm, SHARD_COLS), lambda i, j: (j, i, 0)),
        out_shape=jax.ShapeDtypeStruct((N_DEV, M, SHARD_COLS), _F32), name=name,
        compiler_params=_cparams(dimension_semantics=("parallel", "parallel")),
    )(a, b)


def _to_slabs(m):
    return m.reshape(m.shape[0], N_DEV, SHARD_COLS).transpose(1, 0, 2)


def _from_slabs(s):
    return s.transpose(1, 0, 2).reshape(s.shape[1], N_DEV * SHARD_COLS)


def _ln_fwd(x, y, g, b, *, name):
    tm = 256

    def body(x_ref, y_ref, g_ref, b_ref, z_ref, o_ref, ob_ref):
        z = ALPHA * x_ref[...] + y_ref[...]
        mu = jnp.mean(z, axis=-1, keepdims=True)
        zc = z - mu
        var = jnp.mean(zc * zc, axis=-1, keepdims=True)
        o = zc * lax.rsqrt(var + LN_EPS) * g_ref[...] + b_ref[...]
        z_ref[...] = z
        o_ref[...] = o
        ob_ref[...] = o.astype(ob_ref.dtype)

    row = pl.BlockSpec((tm, D_MODEL), lambda i: (i, 0))
    vec = pl.BlockSpec((1, D_MODEL), lambda i: (0, 0))
    return pl.pallas_call(
        body, grid=(SEQ // tm,), in_specs=[row, row, vec, vec], out_specs=[row, row, row],
        out_shape=[jax.ShapeDtypeStruct((SEQ, D_MODEL), _F32), jax.ShapeDtypeStruct((SEQ, D_MODEL), _F32),
                   jax.ShapeDtypeStruct((SEQ, D_MODEL), _MXU_DTYPE)],
        name=name, compiler_params=_cparams(dimension_semantics=("parallel",)),
    )(x, y, g.reshape(1, D_MODEL), b.reshape(1, D_MODEL))


def _ln_bwd(z, d_a, d_res, g, *, name):
    tm = 256

    def body(*refs):
        if d_res is None:
            z_ref, da_ref, g_ref, dz_ref, dzb_ref, dg_ref, db_ref = refs
            dout = da_ref[...]
        else:
            z_ref, da_ref, dr_ref, g_ref, dz_ref, dzb_ref, dg_ref, db_ref = refs
            dout = da_ref[...] + ALPHA * dr_ref[...]
        z = z_ref[...]
        mu = jnp.mean(z, axis=-1, keepdims=True)
        zc = z - mu
        var = jnp.mean(zc * zc, axis=-1, keepdims=True)
        rstd = lax.rsqrt(var + LN_EPS)
        xh = zc * rstd
        dxh = dout * g_ref[...]
        m1 = jnp.mean(dxh, axis=-1, keepdims=True)
        m2 = jnp.mean(dxh * xh, axis=-1, keepdims=True)
        dz = rstd * (dxh - m1 - xh * m2)
        dz_ref[...] = dz
        dzb_ref[...] = dz.astype(dzb_ref.dtype)

        @pl.when(pl.program_id(0) == 0)
        def _():
            dg_ref[...] = jnp.zeros_like(dg_ref)
            db_ref[...] = jnp.zeros_like(db_ref)

        dg_ref[0:1, :] += jnp.sum(dout * xh, axis=0, keepdims=True)
        db_ref[0:1, :] += jnp.sum(dout, axis=0, keepdims=True)

    row = pl.BlockSpec((tm, D_MODEL), lambda i: (i, 0))
    vec = pl.BlockSpec((1, D_MODEL), lambda i: (0, 0))
    acc = pl.BlockSpec((8, D_MODEL), lambda i: (0, 0))
    ins = [z, d_a] + ([d_res] if d_res is not None else []) + [g.reshape(1, D_MODEL)]
    in_specs = [row, row] + ([row] if d_res is not None else []) + [vec]
    dz, dzb, dg, db = pl.pallas_call(
        body, grid=(SEQ // tm,), in_specs=in_specs, out_specs=[row, row, acc, acc],
        out_shape=[jax.ShapeDtypeStruct((SEQ, D_MODEL), _F32), jax.ShapeDtypeStruct((SEQ, D_MODEL), _MXU_DTYPE),
                   jax.ShapeDtypeStruct((8, D_MODEL), _F32), jax.ShapeDtypeStruct((8, D_MODEL), _F32)],
        name=name, compiler_params=_cparams(dimension_semantics=("arbitrary",)),
    )(*ins)
    return dz, dzb, dg[0], db[0]


def _loss_head(y, target, *, name):
    tm = 256

    def body(y_ref, t_ref, d_ref, l_ref):
        e = y_ref[...] - t_ref[...]
        d_ref[...] = e * (1.0 / D_MODEL)

        @pl.when(pl.program_id(0) == 0)
        def _():
            l_ref[...] = jnp.zeros_like(l_ref)

        l_ref[...] += (0.5 / D_MODEL) * jnp.sum(e * e)

    row = pl.BlockSpec((tm, D_MODEL), lambda i: (i, 0))
    d, l = pl.pallas_call(
        body, grid=(SEQ // tm,), in_specs=[row, row], out_specs=[row, pl.BlockSpec((8, 128), lambda i: (0, 0))],
        out_shape=[jax.ShapeDtypeStruct((SEQ, D_MODEL), _F32), jax.ShapeDtypeStruct((8, 128), _F32)],
        name=name, compiler_params=_cparams(dimension_semantics=("arbitrary",)),
    )(y, target)
    return l[0, 0], d


_CONV_TN = 256


def _shift_down(v, k, rows):
    return jnp.where(rows >= k, pltpu.roll(v, k, axis=0), 0.0)


def _shift_up(v, k, rows):
    return jnp.where(rows < SEQ - k, pltpu.roll(v, SEQ - k, axis=0), 0.0)


def _conv_gate_fwd(g, u, conv_w, conv_b, *, name):
    def body(g_ref, u_ref, w_ref, b_ref, h_ref):
        gv = g_ref[...]
        rows = lax.broadcasted_iota(jnp.int32, gv.shape, 0)
        w = w_ref[...]
        gc = b_ref[...] + w[2:3, :] * gv + w[1:2, :] * _shift_down(gv, 1, rows) + w[0:1, :] * _shift_down(gv, 2, rows)
        h_ref[...] = (gc * _sigmoid(gc) * u_ref[...]).astype(h_ref.dtype)

    col = pl.BlockSpec((SEQ, _CONV_TN), lambda j: (0, j))
    return pl.pallas_call(
        body, grid=(D_FF // _CONV_TN,),
        in_specs=[col, col, pl.BlockSpec((3, _CONV_TN), lambda j: (0, j)), pl.BlockSpec((1, _CONV_TN), lambda j: (0, j))],
        out_specs=col, out_shape=jax.ShapeDtypeStruct((SEQ, D_FF), _MXU_DTYPE), name=name,
        compiler_params=_cparams(dimension_semantics=("parallel",)),
    )(g, u, conv_w, conv_b.reshape(1, D_FF))


def _conv_gate_bwd(dh, g, u, conv_w, conv_b, *, name):
    def body(dh_ref, g_ref, u_ref, w_ref, b_ref, dg_ref, du_ref, dw_ref, db_ref):
        gv = g_ref[...]
        rows = lax.broadcasted_iota(jnp.int32, gv.shape, 0)
        w = w_ref[...]
        g1 = _shift_down(gv, 1, rows)
        g2 = _shift_down(gv, 2, rows)
        gc = b_ref[...] + w[2:3, :] * gv + w[1:2, :] * g1 + w[0:1, :] * g2
        sg = _sigmoid(gc)
        dh = dh_ref[...]
        du_ref[...] = (dh * (gc * sg)).astype(du_ref.dtype)
        dgc = dh * u_ref[...] * (sg * (1.0 + gc * (1.0 - sg)))
        dg = w[2:3, :] * dgc + w[1:2, :] * _shift_up(dgc, 1, rows) + w[0:1, :] * _shift_up(dgc, 2, rows)
        dg_ref[...] = dg.astype(dg_ref.dtype)
        dw_ref[0:1, :] = jnp.sum(dgc * g2, axis=0, keepdims=True)
        dw_ref[1:2, :] = jnp.sum(dgc * g1, axis=0, keepdims=True)
        dw_ref[2:3, :] = jnp.sum(dgc * gv, axis=0, keepdims=True)
        db_ref[...] = jnp.sum(dgc, axis=0, keepdims=True)

    col = pl.BlockSpec((SEQ, _CONV_TN), lambda j: (0, j))
    w3 = pl.BlockSpec((3, _CONV_TN), lambda j: (0, j))
    w1 = pl.BlockSpec((1, _CONV_TN), lambda j: (0, j))
    dg, du, dw, db = pl.pallas_call(
        body, grid=(D_FF // _CONV_TN,), in_specs=[col, col, col, w3, w1], out_specs=[col, col, w3, w1],
        out_shape=[jax.ShapeDtypeStruct((SEQ, D_FF), _MXU_DTYPE), jax.ShapeDtypeStruct((SEQ, D_FF), _MXU_DTYPE),
                   jax.ShapeDtypeStruct((3, D_FF), _F32), jax.ShapeDtypeStruct((1, D_FF), _F32)],
        name=name, compiler_params=_cparams(dimension_semantics=("parallel",)),
    )(dh, g, u, conv_w, conv_b.reshape(1, D_FF))
    return dg, du, dw, db[0]


def _rope_tables():
    half = ROPE_DIM // 2
    inv = ROPE_THETA ** (-jnp.arange(0, ROPE_DIM, 2, dtype=_F32) / ROPE_DIM)
    ang = jnp.arange(SEQ, dtype=_F32)[:, None] * inv[None, :]
    cos, sin = jnp.cos(ang), jnp.sin(ang)
    rest = HEAD_DIM - ROPE_DIM
    c = jnp.concatenate([cos, cos, jnp.ones((SEQ, rest), _F32)], axis=1)
    s1 = jnp.concatenate([-sin, jnp.zeros((SEQ, HEAD_DIM - half), _F32)], axis=1)
    s2 = jnp.concatenate([jnp.zeros((SEQ, half), _F32), sin, jnp.zeros((SEQ, rest), _F32)], axis=1)
    return c, s1, s2


def _rope_apply(x, c, s1, s2):
    return x * c + pltpu.roll(x, HEAD_DIM - ROPE_DIM // 2, axis=1) * s1 + pltpu.roll(x, ROPE_DIM // 2, axis=1) * s2


def _rope_transpose(d, c, s1, s2):
    half = ROPE_DIM // 2
    return d * c + pltpu.roll(d * s1, half, axis=1) + pltpu.roll(d * s2, HEAD_DIM - half, axis=1)


def _is_rope_block(j):
    return (j < VB0) | ((j >= QC0) & (j < VC0))


def _qkv_prep(proj, tabs, *, name):
    tm = 512

    def body(p_ref, c_ref, s1_ref, s2_ref, o_ref):
        j = pl.program_id(1)
        x = p_ref[...]
        r = _rope_apply(x, c_ref[...], s1_ref[...], s2_ref[...])
        o_ref[...] = jnp.where(_is_rope_block(j), r, x).astype(o_ref.dtype)

    tab = pl.BlockSpec((tm, HEAD_DIM), lambda i, j: (i, 0))
    return pl.pallas_call(
        body, grid=(SEQ // tm, QKV_COLS),
        in_specs=[pl.BlockSpec((tm, HEAD_DIM), lambda i, j: (i, A_COLS + j)), tab, tab, tab],
        out_specs=pl.BlockSpec((tm, HEAD_DIM), lambda i, j: (i, j)),
        out_shape=jax.ShapeDtypeStruct((SEQ, QKV_COLS * HEAD_DIM), _MXU_DTYPE), name=name,
        compiler_params=_cparams(dimension_semantics=("parallel", "parallel")),
    )(proj, *tabs)


def _qkv_grad_finish(dq1, dk1, dv1, dq4, dk4, dv4, dq16, dk16, dv16, dqc, dkc, dvc, tabs, *, name):
    tm = 512

    def body(a1, a4, a16, k1, k4, k16, v1, v4, v16, qc, kc, vc, c_ref, s1_ref, s2_ref, o_ref):
        c, s1, s2 = c_ref[...], s1_ref[...], s2_ref[...]
        for h in range(B_HEADS):
            sl = slice(h * HEAD_DIM, (h + 1) * HEAD_DIM)
            o_ref[:, (QB0 + h) * HEAD_DIM:(QB0 + h + 1) * HEAD_DIM] = _rope_transpose(
                a1[:, sl] + a4[:, sl] + a16[:, sl], c, s1, s2).astype(o_ref.dtype)
            o_ref[:, (KB0 + h) * HEAD_DIM:(KB0 + h + 1) * HEAD_DIM] = _rope_transpose(
                k1[:, sl] + k4[:, sl] + k16[:, sl], c, s1, s2).astype(o_ref.dtype)
            o_ref[:, (VB0 + h) * HEAD_DIM:(VB0 + h + 1) * HEAD_DIM] = (v1[:, sl] + v4[:, sl] + v16[:, sl]).astype(o_ref.dtype)
            o_ref[:, (QC0 + h) * HEAD_DIM:(QC0 + h + 1) * HEAD_DIM] = _rope_transpose(qc[:, sl], c, s1, s2).astype(o_ref.dtype)
        for h in range(C_KV_HEADS):
            sl = slice(h * HEAD_DIM, (h + 1) * HEAD_DIM)
            o_ref[:, (KC0 + h) * HEAD_DIM:(KC0 + h + 1) * HEAD_DIM] = _rope_transpose(kc[:, sl], c, s1, s2).astype(o_ref.dtype)
            o_ref[:, (VC0 + h) * HEAD_DIM:(VC0 + h + 1) * HEAD_DIM] = vc[:, sl].astype(o_ref.dtype)

    wb = pl.BlockSpec((tm, B_HEADS * HEAD_DIM), lambda i: (i, 0))
    wkv = pl.BlockSpec((tm, C_KV_HEADS * HEAD_DIM), lambda i: (i, 0))
    tab = pl.BlockSpec((tm, HEAD_DIM), lambda i: (i, 0))
    return pl.pallas_call(
        body, grid=(SEQ // tm,), in_specs=[wb] * 10 + [wkv, wkv, tab, tab, tab],
        out_specs=pl.BlockSpec((tm, QKV_COLS * HEAD_DIM), lambda i: (i, 0)),
        out_shape=jax.ShapeDtypeStruct((SEQ, QKV_COLS * HEAD_DIM), _MXU_DTYPE), name=name,
        compiler_params=_cparams(dimension_semantics=("parallel",)),
    )(dq1, dq4, dq16, dk1, dk4, dk16, dv1, dv4, dv16, dqc, dkc, dvc, *tabs)


_NT = (((1,), (1,)), ((), ()))
_TN = (((0,), (0,)), ((), ()))
_SCALE = HEAD_DIM ** -0.5


def _band_scores(q, kp, kc, n, lag_off):
    sp = lax.dot_general(q, kp, _NT, preferred_element_type=_F32) * _SCALE
    sc = lax.dot_general(q, kc, _NT, preferred_element_type=_F32) * _SCALE
    row = lax.broadcasted_iota(jnp.int32, (BLOCK, BLOCK), 0)
    col = lax.broadcasted_iota(jnp.int32, (BLOCK, BLOCK), 1)
    sp = jnp.where((col >= row + lag_off) & (n > 0), sp, _NEG)
    sc = jnp.where(col <= row, sc, _NEG)
    return sp, sc


def _attn_specs(dil, q0, k0, v0, kv_group):
    def q_map(h, r, n):
        return (n, r * QKV_COLS + q0 + h)

    def kv_map(base, prev):
        def f(h, r, n):
            return (jnp.maximum(n - 1, 0) if prev else n, r * QKV_COLS + base + h // kv_group)
        return f

    blk = (BLOCK, HEAD_DIM)
    return [pl.BlockSpec(blk, q_map), pl.BlockSpec(blk, kv_map(k0, True)), pl.BlockSpec(blk, kv_map(k0, False)),
            pl.BlockSpec(blk, kv_map(v0, True)), pl.BlockSpec(blk, kv_map(v0, False))]


def _attn_fwd(qkv, dil, *, heads, q0, k0, v0, kv_group, lag_off, sink, name):
    L = SEQ // dil
    nb = L // BLOCK
    qkv_r = qkv.reshape(L, dil * QKV_COLS * HEAD_DIM)

    def body(*refs):
        if sink is None:
            q_ref, kp_ref, kc_ref, vp_ref, vc_ref, o_ref, lse_ref = refs
        else:
            q_ref, kp_ref, kc_ref, vp_ref, vc_ref, sk_ref, o_ref, lse_ref = refs
        n = pl.program_id(2)
        sp, sc = _band_scores(q_ref[...], kp_ref[...], kc_ref[...], n, lag_off)
        m = jnp.maximum(jnp.max(sp, axis=1, keepdims=True), jnp.max(sc, axis=1, keepdims=True))
        if sink is not None:
            sk = sk_ref[0][:, 0:1]
            m = jnp.maximum(m, sk)
        pp = jnp.exp(sp - m)
        pc = jnp.exp(sc - m)
        den = jnp.sum(pp, axis=1, keepdims=True) + jnp.sum(pc, axis=1, keepdims=True)
        if sink is not None:
            den = den + jnp.exp(sk - m)
        acc = jnp.dot(pp.astype(_MXU_DTYPE), vp_ref[...], preferred_element_type=_F32)
        acc = acc + jnp.dot(pc.astype(_MXU_DTYPE), vc_ref[...], preferred_element_type=_F32)
        o_ref[...] = acc / den
        lse_ref[...] = jnp.broadcast_to(m + jnp.log(den), (BLOCK, HEAD_DIM))

    in_specs = _attn_specs(dil, q0, k0, v0, kv_group)
    args = [qkv_r] * 5
    if sink is not None:
        in_specs.append(pl.BlockSpec((1, 1, HEAD_DIM), lambda h, r, n: (h, 0, 0)))
        args.append(jnp.broadcast_to(sink.reshape(heads, 1, 1), (heads, 1, HEAD_DIM)))
    o_spec = pl.BlockSpec((BLOCK, HEAD_DIM), lambda h, r, n: (n, r * heads + h))
    shape = jax.ShapeDtypeStruct((L, dil * heads * HEAD_DIM), _F32)
    o, lse = pl.pallas_call(
        body, grid=(heads, dil, nb), in_specs=in_specs, out_specs=[o_spec, o_spec], out_shape=[shape, shape],
        name=name, compiler_params=_cparams(dimension_semantics=("parallel", "parallel", "parallel")),
    )(*args)
    return o.reshape(SEQ, heads * HEAD_DIM), lse.reshape(SEQ, heads * HEAD_DIM)


def _attn_bwd(qkv, dmixed, o, lse, dil, *, heads, q0, k0, v0, kv_group, lag_off, do0, sink, name):
    L = SEQ // dil
    nb = L // BLOCK
    kvh = heads // kv_group
    qkv_r = qkv.reshape(L, dil * QKV_COLS * HEAD_DIM)
    dm_r = dmixed.reshape(L, dil * D_MODEL)
    o_r = o.reshape(L, dil * heads * HEAD_DIM)
    lse_r = lse.reshape(L, dil * heads * HEAD_DIM)

    def body(*refs):
        if sink is None:
            q_ref, kp_ref, kc_ref, vp_ref, vc_ref, do_ref, o_ref, lse_ref, dq_ref, dk_ref, dv_ref = refs
        else:
            q_ref, kp_ref, kc_ref, vp_ref, vc_ref, do_ref, o_ref, lse_ref, sk_ref, dq_ref, dk_ref, dv_ref, dsk_ref = refs
        g = pl.program_id(0)
        n = pl.program_id(2)
        q, kp, kc, vp, vc = q_ref[...], kp_ref[...], kc_ref[...], vp_ref[...], vc_ref[...]
        do = do_ref[...]
        delta = jnp.sum(do * o_ref[...], axis=1, keepdims=True)
        lse_c = lse_ref[:, 0:1]
        sp, sc = _band_scores(q, kp, kc, n, lag_off)
        pp = jnp.exp(sp - lse_c)
        pc = jnp.exp(sc - lse_c)
        dob = do.astype(_MXU_DTYPE)
        dsp = (pp * (lax.dot_general(dob, vp, _NT, preferred_element_type=_F32) - delta) * _SCALE).astype(_MXU_DTYPE)
        dsc = (pc * (lax.dot_general(dob, vc, _NT, preferred_element_type=_F32) - delta) * _SCALE).astype(_MXU_DTYPE)
        dq_ref[...] = (jnp.dot(dsp, kp, preferred_element_type=_F32) + jnp.dot(dsc, kc, preferred_element_type=_F32))

        @pl.when((n == 0) & (g % kv_group == 0))
        def _():
            dk_ref[...] = jnp.zeros_like(dk_ref)
            dv_ref[...] = jnp.zeros_like(dv_ref)

        prev = pl.ds(pl.multiple_of(jnp.maximum(n - 1, 0) * BLOCK, BLOCK), BLOCK)
        cur = pl.ds(pl.multiple_of(n * BLOCK, BLOCK), BLOCK)
        dk_ref[prev, :] += lax.dot_general(dsp, q, _TN, preferred_element_type=_F32)
        dv_ref[prev, :] += lax.dot_general(pp.astype(_MXU_DTYPE), dob, _TN, preferred_element_type=_F32)
        dk_ref[cur, :] += lax.dot_general(dsc, q, _TN, preferred_element_type=_F32)
        dv_ref[cur, :] += lax.dot_general(pc.astype(_MXU_DTYPE), dob, _TN, preferred_element_type=_F32)
        if sink is not None:
            @pl.when(n == 0)
            def _():
                dsk_ref[...] = jnp.zeros_like(dsk_ref)

            sk = sk_ref[0][:, 0:1]
            dsk_ref[...] += jnp.sum(-delta * jnp.exp(sk - lse_c))

    in_specs = _attn_specs(dil, q0, k0, v0, kv_group)
    blk = (BLOCK, HEAD_DIM)
    in_specs.append(pl.BlockSpec(blk, lambda h, r, n: (n, r * (D_MODEL // HEAD_DIM) + do0 + h)))
    in_specs.append(pl.BlockSpec(blk, lambda h, r, n: (n, r * heads + h)))
    in_specs.append(pl.BlockSpec(blk, lambda h, r, n: (n, r * heads + h)))
    args = [qkv_r] * 5 + [dm_r, o_r, lse_r]
    if sink is not None:
        in_specs.append(pl.BlockSpec((1, 1, HEAD_DIM), lambda h, r, n: (h, 0, 0)))
        args.append(jnp.broadcast_to(sink.reshape(heads, 1, 1), (heads, 1, HEAD_DIM)))
    kv_spec = pl.BlockSpec((L, HEAD_DIM), lambda h, r, n: (0, r * kvh + h // kv_group))
    out_specs = [pl.BlockSpec(blk, lambda h, r, n: (n, r * heads + h)), kv_spec, kv_spec]
    out_shape = [jax.ShapeDtypeStruct((L, dil * heads * HEAD_DIM), _F32),
                 jax.ShapeDtypeStruct((L, dil * kvh * HEAD_DIM), _F32),
                 jax.ShapeDtypeStruct((L, dil * kvh * HEAD_DIM), _F32)]
    if sink is not None:
        out_specs.append(pl.BlockSpec((1, 8, HEAD_DIM), lambda h, r, n: (h, 0, 0)))
        out_shape.append(jax.ShapeDtypeStruct((heads, 8, HEAD_DIM), _F32))
    res = pl.pallas_call(
        body, grid=(heads, dil, nb), in_specs=in_specs, out_specs=out_specs, out_shape=out_shape, name=name,
        compiler_params=_cparams(dimension_semantics=("arbitrary", "arbitrary", "arbitrary")),
    )(*args)
    dq = res[0].reshape(SEQ, heads * HEAD_DIM)
    dk = res[1].reshape(SEQ, kvh * HEAD_DIM)
    dv = res[2].reshape(SEQ, kvh * HEAD_DIM)
    if sink is not None:
        return dq, dk, dv, res[3][:, 0, 0]
    return dq, dk, dv


def _dilated_mix(os_, lses, *, name):
    tm = 256

    def body(o1, o4, o16, l1, l4, l16, o_ref, lse_ref):
        a, b, c = l1[...], l4[...], l16[...]
        m = jnp.maximum(jnp.maximum(a, b), c)
        ea, eb, ec = jnp.exp(a - m), jnp.exp(b - m), jnp.exp(c - m)
        den = ea + eb + ec
        o_ref[...] = (ea * o1[...] + eb * o4[...] + ec * o16[...]) / den
        lse_ref[...] = m + jnp.log(den)

    w = B_HEADS * HEAD_DIM
    spec = pl.BlockSpec((tm, w), lambda i: (i, 0))
    shape = jax.ShapeDtypeStruct((SEQ, w), _F32)
    return pl.pallas_call(
        body, grid=(SEQ // tm,), in_specs=[spec] * 6, out_specs=[spec, spec], out_shape=[shape, shape], name=name,
        compiler_params=_cparams(dimension_semantics=("parallel",)),
    )(*os_, *lses)


_HG_TILE = 128
_HG_CHUNKS = _HG_TILE // A_CHUNK
_HG_TILES = SEQ // _HG_TILE
_HI = lax.Precision.HIGHEST


def _chunk_tri():
    i = np.arange(_HG_TILE)
    return jnp.asarray(((i[:, None] // A_CHUNK == i[None, :] // A_CHUNK) & (i[None, :] <= i[:, None])).astype(np.float32))


def _layer_lb(lb_ref, layer):
    if layer == 0:
        return jnp.zeros((1, HEAD_DIM), _F32)
    lg = lb_ref[...]
    m = jnp.max(lg, axis=0, keepdims=True)
    e = jnp.exp(lg - m)
    return e[1:2, :] / jnp.sum(e, axis=0, keepdims=True)


def _hgrn_gates(q, fr, lb):
    sgq = _sigmoid(q)
    sg = _sigmoid(fr)
    f = lb + (1.0 - lb) * sg
    return sgq, q * sgq, sg, f, 1.0 - f


def _hgrn_fwd(proj, lb_logits, norm_w, layer, *, name):
    tri = _chunk_tri()

    def body(q_ref, f_ref, i_ref, g_ref, lb_ref, nw_ref, tri_ref, o_ref, raw_ref, st_ref, state):
        @pl.when(pl.program_id(1) == 0)
        def _():
            state[...] = jnp.zeros_like(state)

        lb = _layer_lb(lb_ref, layer)
        _, qs, _, f, k = _hgrn_gates(q_ref[...], f_ref[...], lb)
        v = i_ref[...]
        b = jnp.dot(tri_ref[...], jnp.log(f), precision=_HI, preferred_element_type=_F32)
        eb = jnp.exp(b)
        ridx = lax.broadcasted_iota(jnp.int32, (A_CHUNK, HEAD_DIM), 0)
        outs = []
        for c in range(_HG_CHUNKS):
            sl = slice(c * A_CHUNK, (c + 1) * A_CHUNK)
            bc, qc, kc, vc = b[sl], qs[sl], k[sl], v[sl]
            bl = bc[A_CHUNK - 1:A_CHUNK]
            st = state[...]
            st_ref[0, c] = st
            o_c = lax.dot_general((qc * eb[sl]).astype(_MXU_DTYPE), st.astype(_MXU_DTYPE), _NT, preferred_element_type=_F32)
            rows = []
            for i in range(A_CHUNK):
                di = jnp.exp(jnp.where(ridx <= i, bc[i:i + 1] - bc, _NEG))
                a = jnp.sum(qc[i:i + 1] * kc * di, axis=1, keepdims=True)
                rows.append(jnp.sum(a * vc, axis=0, keepdims=True))
            outs.append(o_c + jnp.concatenate(rows, axis=0))
            kt = (kc * jnp.exp(bl - bc)).astype(_MXU_DTYPE)
            state[...] = st * jnp.exp(bl) + lax.dot_general(vc.astype(_MXU_DTYPE), kt, _TN, preferred_element_type=_F32)
        o = jnp.concatenate(outs, axis=0)
        raw_ref[...] = o
        r = lax.rsqrt(jnp.mean(o * o, axis=-1, keepdims=True) + LN_EPS)
        g = g_ref[...]
        o_ref[...] = o * r * nw_ref[...] * (g * _sigmoid(g))

    blk = (_HG_TILE, HEAD_DIM)

    def col(base):
        return pl.BlockSpec(blk, lambda h, t: (t, base + h))

    o_spec = pl.BlockSpec(blk, lambda h, t: (t, h))
    o_shape = jax.ShapeDtypeStruct((SEQ, A_HEADS * HEAD_DIM), _F32)
    return pl.pallas_call(
        body, grid=(A_HEADS, _HG_TILES),
        in_specs=[col(0), col(4), col(8), col(12), pl.BlockSpec((DEPTH, HEAD_DIM), lambda h, t: (0, h)),
                  pl.BlockSpec((1, HEAD_DIM), lambda h, t: (0, 0)), pl.BlockSpec(blk, lambda h, t: (0, 0))],
        out_specs=[o_spec, o_spec, pl.BlockSpec((1, _HG_CHUNKS, HEAD_DIM, HEAD_DIM), lambda h, t: (h, t, 0, 0))],
        out_shape=[o_shape, o_shape, jax.ShapeDtypeStruct((A_HEADS, SEQ // A_CHUNK, HEAD_DIM, HEAD_DIM), _F32)],
        scratch_shapes=[pltpu.VMEM((HEAD_DIM, HEAD_DIM), _F32)], name=name,
        compiler_params=_cparams(dimension_semantics=("parallel", "arbitrary")),
    )(proj, proj, proj, proj, lb_logits, norm_w.reshape(1, HEAD_DIM), tri)


def _hgrn_bwd(proj, lb_logits, norm_w, raw, states, dmixed, layer, *, name):
    tri = _chunk_tri()
    triu = tri.T

    def body(q_ref, f_ref, i_ref, g_ref, lb_ref, nw_ref, tri_ref, triu_ref, raw_ref, do_ref, st_ref,
             dq_ref, df_ref, di_ref, dg_ref, dnw_ref, dlb_ref, dstate):
        @pl.when(pl.program_id(1) == 0)
        def _():
            dstate[...] = jnp.zeros_like(dstate)
            dlb_ref[...] = jnp.zeros_like(dlb_ref)

        @pl.when((pl.program_id(0) == 0) & (pl.program_id(1) == 0))
        def _():
            dnw_ref[...] = jnp.zeros_like(dnw_ref)

        lb = _layer_lb(lb_ref, layer)
        q = q_ref[...]
        sgq, qs, sg, f, k = _hgrn_gates(q, f_ref[...], lb)
        v = i_ref[...]
        b = jnp.dot(tri_ref[...], jnp.log(f), precision=_HI, preferred_element_type=_F32)
        eb = jnp.exp(b)
        g = g_ref[...]
        nw = nw_ref[...]
        o = raw_ref[...]
        dout = do_ref[...]
        sgg = _sigmoid(g)
        r = lax.rsqrt(jnp.mean(o * o, axis=-1, keepdims=True) + LN_EPS)
        dg_ref[...] = (dout * (o * r * nw) * (sgg * (1.0 + g * (1.0 - sgg)))).astype(dg_ref.dtype)
        don = dout * (g * sgg)
        dnw_ref[0:1, :] += jnp.sum(don * o * r, axis=0, keepdims=True)
        dy = don * nw
        do_raw = r * dy - o * (r * r * r) * jnp.mean(o * dy, axis=-1, keepdims=True)

        ridx = lax.broadcasted_iota(jnp.int32, (A_CHUNK, HEAD_DIM), 0)
        dqs_t, dk_t, db_t, dv_t = [None] * _HG_CHUNKS, [None] * _HG_CHUNKS, [None] * _HG_CHUNKS, [None] * _HG_CHUNKS
        for c in reversed(range(_HG_CHUNKS)):
            sl = slice(c * A_CHUNK, (c + 1) * A_CHUNK)
            bc, qc, kc, vc, doc = b[sl], qs[sl], k[sl], v[sl], do_raw[sl]
            bl = bc[A_CHUNK - 1:A_CHUNK]
            ebc = eb[sl]
            ebl = jnp.exp(bl - bc)
            lam = jnp.exp(bl)
            qt = qc * ebc
            kt = kc * ebl
            dst = dstate[...]
            stp = st_ref[0, c]
            dob = doc.astype(_MXU_DTYPE)
            dstb = dst.astype(_MXU_DTYPE)
            dqt = jnp.dot(dob, stp.astype(_MXU_DTYPE), preferred_element_type=_F32)
            dkt = jnp.dot(vc.astype(_MXU_DTYPE), dstb, preferred_element_type=_F32)
            dv = lax.dot_general(kt.astype(_MXU_DTYPE), dstb, _NT, preferred_element_type=_F32)
            dlam = jnp.sum(stp * dst, axis=0, keepdims=True)
            dstate[...] = dst * lam + lax.dot_general(dob, qt.astype(_MXU_DTYPE), _TN, preferred_element_type=_F32)
            dqs_rows = []
            dk_in = jnp.zeros((A_CHUNK, HEAD_DIM), _F32)
            for i in range(A_CHUNK):
                di = jnp.exp(jnp.where(ridx <= i, bc[i:i + 1] - bc, _NEG))
                qi = qc[i:i + 1]
                doi = doc[i:i + 1]
                w = kc * di
                a = jnp.sum(qi * w, axis=1, keepdims=True)
                dv = dv + a * doi
                da = jnp.sum(doi * vc, axis=1, keepdims=True)
                dqs_rows.append(jnp.sum(da * w, axis=0, keepdims=True))
                dk_in = dk_in + da * (qi * di)
            dqs_in = jnp.concatenate(dqs_rows, axis=0)
            dbl = jnp.sum(dkt * kt, axis=0, keepdims=True) + dlam * lam
            db = qc * dqs_in - kc * dk_in + dqt * qt - dkt * kt
            db_t[c] = db + jnp.where(ridx == A_CHUNK - 1, dbl, 0.0)
            dqs_t[c] = dqs_in + dqt * ebc
            dk_t[c] = dk_in + dkt * ebl
            dv_t[c] = dv
        dqs = jnp.concatenate(dqs_t, axis=0)
        dk = jnp.concatenate(dk_t, axis=0)
        db = jnp.concatenate(db_t, axis=0)
        di_ref[...] = jnp.concatenate(dv_t, axis=0).astype(di_ref.dtype)
        dlogf = jnp.dot(triu_ref[...], db, precision=_HI, preferred_element_type=_F32)
        df = dlogf / f - dk
        df_ref[...] = (df * (1.0 - lb) * sg * (1.0 - sg)).astype(df_ref.dtype)
        dlb_ref[0, 0:1, :] += jnp.sum(df * (1.0 - sg), axis=0, keepdims=True)
        dq_ref[...] = (dqs * (sgq * (1.0 + q * (1.0 - sgq)))).astype(dq_ref.dtype)

    blk = (_HG_TILE, HEAD_DIM)
    last = _HG_TILES - 1

    def col(base):
        return pl.BlockSpec(blk, lambda h, t: (last - t, base + h))

    tri_spec = pl.BlockSpec(blk, lambda h, t: (0, 0))
    acc_spec = pl.BlockSpec((1, 8, HEAD_DIM), lambda h, t: (h, 0, 0))
    acc_shape = jax.ShapeDtypeStruct((A_HEADS, 8, HEAD_DIM), _F32)
    dq, df, di, dg, dnw, dlb = pl.pallas_call(
        body, grid=(A_HEADS, _HG_TILES),
        in_specs=[col(0), col(4), col(8), col(12), pl.BlockSpec((DEPTH, HEAD_DIM), lambda h, t: (0, h)),
                  pl.BlockSpec((1, HEAD_DIM), lambda h, t: (0, 0)), tri_spec, tri_spec, col(0), col(0),
                  pl.BlockSpec((1, _HG_CHUNKS, HEAD_DIM, HEAD_DIM), lambda h, t: (h, last - t, 0, 0))],
        out_specs=[col(0), col(0), col(0), col(0), pl.BlockSpec((8, HEAD_DIM), lambda h, t: (0, 0)), acc_spec],
        out_shape=[jax.ShapeDtypeStruct((SEQ, A_HEADS * HEAD_DIM), _MXU_DTYPE)] * 4
        + [jax.ShapeDtypeStruct((8, HEAD_DIM), _F32), acc_shape],
        scratch_shapes=[pltpu.VMEM((HEAD_DIM, HEAD_DIM), _F32)], name=name,
        compiler_params=_cparams(dimension_semantics=("arbitrary", "arbitrary")),
    )(proj, proj, proj, proj, lb_logits, norm_w.reshape(1, HEAD_DIM), tri, triu, raw, dmixed, states)
    return dq, df, di, dg, dnw[0], dlb[:, 0, :].reshape(A_HEADS * HEAD_DIM)


def _exchange(arrays, scatter, *, name):
    n = len(arrays)
    n_peer = N_DEV - 1

    def body(*refs):
        ins, outs = refs[:n], refs[n:2 * n]
        send_sems, recv_sems, loc_sems = refs[2 * n:]
        x, y, c = lax.axis_index("x"), lax.axis_index("y"), lax.axis_index("c")
        me = 4 * x + 2 * y + c
        local = []
        for a in range(n):
            cp = pltpu.make_async_copy(ins[a].at[me] if scatter else ins[a], outs[a].at[me], loc_sems.at[a])
            cp.start()
            local.append(cp)

        def peer(k):
            px = jnp.bitwise_xor(x, (k >> 2) & 1)
            py = jnp.bitwise_xor(y, (k >> 1) & 1)
            pc = jnp.bitwise_xor(c, k & 1)
            return (px, py, pc), 4 * px + 2 * py + pc

        def copy(a, k):
            dev, pid = peer(k)
            return pltpu.make_async_remote_copy(
                src_ref=ins[a].at[pid] if scatter else ins[a], dst_ref=outs[a].at[me],
                send_sem=send_sems.at[a * n_peer + k - 1], recv_sem=recv_sems.at[a * n_peer + k - 1],
                device_id=dev, device_id_type=pl.DeviceIdType.MESH)

        def arrival(a, k):
            dev, pid = peer(k)
            return pltpu.make_async_remote_copy(
                src_ref=ins[a].at[pid] if scatter else ins[a], dst_ref=outs[a].at[pid],
                send_sem=send_sems.at[a * n_peer + k - 1], recv_sem=recv_sems.at[a * n_peer + k - 1],
                device_id=dev, device_id_type=pl.DeviceIdType.MESH)

        sends = [copy(a, k) for k in range(1, N_DEV) for a in range(n)]
        for cp in sends:
            cp.start()
        for k in range(1, N_DEV):
            for a in range(n):
                arrival(a, k).wait_recv()
        for cp in sends:
            cp.wait_send()
        for cp in local:
            cp.wait()

    def out_shape(a):
        blk = a.shape[1:] if scatter else a.shape
        return jax.ShapeDtypeStruct((N_DEV,) + tuple(blk), a.dtype)

    any_spec = pl.BlockSpec(memory_space=pl.ANY)
    return pl.pallas_call(
        body, in_specs=[any_spec] * n, out_specs=[any_spec] * n, out_shape=[out_shape(a) for a in arrays],
        scratch_shapes=[pltpu.SemaphoreType.DMA((n * n_peer,)), pltpu.SemaphoreType.DMA((n * n_peer,)),
                        pltpu.SemaphoreType.DMA((n,))],
        name=name, compiler_params=pltpu.CompilerParams(has_side_effects=True),
    )(*arrays)


_C1 = 1.0 - ADAM_B1 ** ADAM_STEP
_C2 = 1.0 - ADAM_B2 ** ADAM_STEP


def _adamw_math(g, w, m, v):
    m = ADAM_B1 * m + (1.0 - ADAM_B1) * g
    v = ADAM_B2 * v + (1.0 - ADAM_B2) * (g * g)
    delta = -ADAM_LR * ((m / _C1) / (jnp.sqrt(v / _C2) + ADAM_EPS) + ADAM_WD * w)
    return delta, m, v


def _adamw_reduce(parts, w, m, v, *, name):
    R, C = w.shape
    tr = max(t for t in range(8, R + 1, 8) if R % t == 0 and t * C <= 192 * 1024)

    def body(p_ref, w_ref, m_ref, v_ref, g_ref, d_ref, nm_ref, nv_ref):
        g = p_ref[0]
        for s in range(1, N_DEV):
            g = g + p_ref[s]
        d, nm, nv = _adamw_math(g, w_ref[...], m_ref[...], v_ref[...])
        g_ref[...] = g
        d_ref[...] = d
        nm_ref[...] = nm
        nv_ref[...] = nv

    blk = pl.BlockSpec((tr, C), lambda i: (i, 0))
    shape = jax.ShapeDtypeStruct((R, C), _F32)
    return pl.pallas_call(
        body, grid=(R // tr,), in_specs=[pl.BlockSpec((N_DEV, tr, C), lambda i: (0, i, 0)), blk, blk, blk],
        out_specs=[blk] * 4, out_shape=[shape] * 4, name=name,
        compiler_params=_cparams(dimension_semantics=("parallel",)),
    )(parts, w, m, v)


_PACK_LANES = 128
_LAYER_ROWS = 248
_LB_ROWS = (A_HEADS * HEAD_DIM) // _PACK_LANES


def _small_reduce(parts, lb_logits, *, name):
    rows = DEPTH * _LAYER_ROWS

    def body(p_ref, lg_ref, o_ref):
        g = p_ref[0]
        for s in range(1, N_DEV):
            g = g + p_ref[s]
        o_ref[...] = g
        lg = lg_ref[...]
        e = jnp.exp(lg - jnp.max(lg, axis=0, keepdims=True))
        p = e / jnp.sum(e, axis=0, keepdims=True)
        d1 = g[_LAYER_ROWS:_LAYER_ROWS + _LB_ROWS, :] * p[0] * p[1]
        o_ref[0:_LB_ROWS, :] = -d1
        o_ref[_LAYER_ROWS:_LAYER_ROWS + _LB_ROWS, :] = d1

    return pl.pallas_call(
        body, out_shape=jax.ShapeDtypeStruct((rows, _PACK_LANES), _F32), name=name,
        compiler_params=_cparams(),
    )(parts, lb_logits.reshape(DEPTH, _LB_ROWS, _PACK_LANES))


def _adamw_small(g, w, m, v, *, name):
    def body(g_ref, w_ref, m_ref, v_ref, d_ref, nm_ref, nv_ref):
        d, nm, nv = _adamw_math(g_ref[...], w_ref[...], m_ref[...], v_ref[...])
        d_ref[...] = d
        nm_ref[...] = nm
        nv_ref[...] = nv

    shape = jax.ShapeDtypeStruct(g.shape, _F32)
    return pl.pallas_call(body, out_shape=[shape] * 3, name=name, compiler_params=_cparams())(g, w, m, v)


def _pack(vectors, rows):
    flat = jnp.concatenate([v.reshape(-1).astype(_F32) for v in vectors])
    return jnp.pad(flat, (0, rows * _PACK_LANES - flat.shape[0])).reshape(rows, _PACK_LANES)


def _unpack(packed, shapes):
    flat = packed.reshape(-1)
    out, at = [], 0
    for s in shapes:
        size = int(np.prod(s))
        out.append(flat[at:at + size].reshape(s))
        at += size
    return out


def _layer_fwd(x, xb, w, lb_logits, a_norm_w, c_sink, ln1_g, ln1_b, conv_w, conv_b, ln2_g, ln2_b, tabs, l):
    proj = _mm(xb, w["in"], tm=1024, tn=512, name=f"proj_{l}")
    qkv = _qkv_prep(proj, tabs, name=f"qkv_prep_{l}")
    o_a, raw, states = _hgrn_fwd(proj, lb_logits, a_norm_w, l, name=f"hgrn_fwd_{l}")
    os_, lses = [], []
    for d in DILATIONS:
        o, lse = _attn_fwd(qkv, d, heads=B_HEADS, q0=QB0, k0=KB0, v0=VB0, kv_group=1, lag_off=0, sink=None,
                           name=f"dilated_fwd_{d}_{l}")
        os_.append(o)
        lses.append(lse)
    o_b, lse_b = _dilated_mix(os_, lses, name=f"dilated_mix_{l}")
    o_c, lse_c = _attn_fwd(qkv, 1, heads=C_HEADS, q0=QC0, k0=KC0, v0=VC0, kv_group=C_HEADS // C_KV_HEADS, lag_off=1,
                           sink=c_sink, name=f"swa_fwd_{l}")
    mixed = jnp.concatenate([o_a, o_b, o_c], axis=1).astype(_MXU_DTYPE)
    y = _mm(mixed, w["out"], tm=1024, tn=512, name=f"mix_out_{l}")
    z1, x1, x1b = _ln_fwd(x, y, ln1_g, ln1_b, name=f"ln1_fwd_{l}")
    g = _mm(x1b, w["gate"], tm=1024, tn=512, name=f"ffn_gate_{l}")
    u = _mm(x1b, w["up"], tm=1024, tn=512, name=f"ffn_up_{l}")
    hb = _conv_gate_fwd(g, u, conv_w, conv_b, name=f"conv_gate_fwd_{l}")
    y2 = _mm(hb, w["down"], tm=512, tn=512, name=f"ffn_down_{l}")
    z2, x2, x2b = _ln_fwd(x1, y2, ln2_g, ln2_b, name=f"ln2_fwd_{l}")
    res = dict(xb=xb, proj=proj, qkv=qkv, raw=raw, states=states, o_b=o_b, lse_b=lse_b, o_c=o_c, lse_c=lse_c,
               mixed=mixed, z1=z1, x1b=x1b, g=g, u=u, hb=hb, z2=z2)
    return x2, x2b, res


def _layer_bwd(dx2, res, w, lb_logits, a_norm_w, c_sink, ln1_g, conv_w, conv_b, ln2_g, tabs, l):
    dz2, dz2b, d_ln2_g, d_ln2_b = _ln_bwd(res["z2"], dx2, None, ln2_g, name=f"ln2_bwd_{l}")
    dh = _mm(dz2b, w["down"], tb=True, tm=1024, tn=512, name=f"ffn_down_dx_{l}")
    d_w_down = _mm(res["hb"], dz2b, ta=True, tm=512, tn=512, name=f"ffn_down_dw_{l}")
    dg, du, d_conv_w, d_conv_b = _conv_gate_bwd(dh, res["g"], res["u"], conv_w, conv_b, name=f"conv_gate_bwd_{l}")
    t = _mm(dg, w["gate"], tb=True, tm=512, tn=512, name=f"ffn_gate_dx_{l}")
    dx1 = _mm(du, w["up"], tb=True, tm=512, tn=512, add=t, name=f"ffn_up_dx_{l}")
    d_w_gate = _mm_tn_slabs(res["x1b"], _to_slabs(dg), tm=1024, name=f"ffn_gate_dw_{l}")
    d_w_up = _mm_tn_slabs(res["x1b"], _to_slabs(du), tm=1024, name=f"ffn_up_dw_{l}")
    dz1, dz1b, d_ln1_g, d_ln1_b = _ln_bwd(res["z1"], dx1, dz2, ln1_g, name=f"ln1_bwd_{l}")
    dmixed = _mm(dz1b, w["out"], tb=True, tm=1024, tn=512, name=f"mix_out_dx_{l}")
    d_w_out = _mm(res["mixed"], dz1b, ta=True, tm=1024, tn=512, name=f"mix_out_dw_{l}")
    dq_a, df_a, di_a, dg_a, d_norm_w, d_lb = _hgrn_bwd(res["proj"], lb_logits, a_norm_w, res["raw"], res["states"],
                                                      dmixed, l, name=f"hgrn_bwd_{l}")
    grads = []
    for d in DILATIONS:
        grads += list(_attn_bwd(res["qkv"], dmixed, res["o_b"], res["lse_b"], d, heads=B_HEADS, q0=QB0, k0=KB0, v0=VB0,
                                kv_group=1, lag_off=0, do0=A_HEADS, sink=None, name=f"dilated_bwd_{d}_{l}"))
    dq_c, dk_c, dv_c, d_sink = _attn_bwd(res["qkv"], dmixed, res["o_c"], res["lse_c"], 1, heads=C_HEADS, q0=QC0, k0=KC0,
                                         v0=VC0, kv_group=C_HEADS // C_KV_HEADS, lag_off=1, do0=A_HEADS + B_HEADS,
                                         sink=c_sink, name=f"swa_bwd_{l}")
    dqkv = _qkv_grad_finish(*grads, dq_c, dk_c, dv_c, tabs, name=f"qkv_grad_{l}")
    dproj = jnp.concatenate([dq_a, df_a, di_a, dg_a, dqkv], axis=1)
    dx = _mm(dproj, w["in"], tb=True, tm=512, tn=512, add=dz1, add_scale=ALPHA, name=f"proj_dx_{l}")
    d_w_in = _mm_tn_slabs(res["xb"], _to_slabs(dproj), tm=1024, name=f"proj_dw_{l}")
    big = dict(w_in=d_w_in, w_gate=d_w_gate, w_up=d_w_up,
               w_out=d_w_out.reshape(N_DEV, D_MODEL // N_DEV, D_MODEL), w_down=d_w_down.reshape(N_DEV, D_FF // N_DEV, D_MODEL))
    small = [d_lb, d_norm_w, jnp.pad(d_sink, (0, _PACK_LANES - C_HEADS)), d_ln1_g, d_ln1_b, d_ln2_g, d_ln2_b, d_conv_b,
             d_conv_w]
    return dx, big, small


_BIG = ("w_in", "w_gate", "w_up", "w_out", "w_down")


def kernel(x, w_in, lb_logits, a_norm_w, c_sinks, w_out, ln1_g, ln1_b, w_gate, w_up, conv_w, conv_b, w_down, ln2_g, ln2_b, loss_target, m_w_in, m_lb_logits, m_a_norm_w, m_c_sinks, m_w_out, m_ln1_g, m_ln1_b, m_w_gate, m_w_up, m_conv_w, m_conv_b, m_w_down, m_ln2_g, m_ln2_b, v_w_in, v_lb_logits, v_a_norm_w, v_c_sinks, v_w_out, v_ln1_g, v_ln1_b, v_w_gate, v_w_up, v_conv_w, v_conv_b, v_w_down, v_ln2_g, v_ln2_b):
    weights = dict(w_in=w_in, lb_logits=lb_logits, a_norm_w=a_norm_w, c_sinks=c_sinks, w_out=w_out, ln1_g=ln1_g, ln1_b=ln1_b,
                   w_gate=w_gate, w_up=w_up, conv_w=conv_w, conv_b=conv_b, w_down=w_down, ln2_g=ln2_g, ln2_b=ln2_b)
    mom1 = dict(w_in=m_w_in, lb_logits=m_lb_logits, a_norm_w=m_a_norm_w, c_sinks=m_c_sinks, w_out=m_w_out, ln1_g=m_ln1_g,
                ln1_b=m_ln1_b, w_gate=m_w_gate, w_up=m_w_up, conv_w=m_conv_w, conv_b=m_conv_b, w_down=m_w_down, ln2_g=m_ln2_g,
                ln2_b=m_ln2_b)
    mom2 = dict(w_in=v_w_in, lb_logits=v_lb_logits, a_norm_w=v_a_norm_w, c_sinks=v_c_sinks, w_out=v_w_out, ln1_g=v_ln1_g,
                ln1_b=v_ln1_b, w_gate=v_w_gate, w_up=v_w_up, conv_w=v_conv_w, conv_b=v_conv_b, w_down=v_w_down, ln2_g=v_ln2_g,
                ln2_b=v_ln2_b)
    me = 4 * lax.axis_index("x") + 2 * lax.axis_index("y") + lax.axis_index("c")
    tabs = _rope_tables()

    full = []
    for l in range(DEPTH):
        g_in, g_gate, g_up, g_out, g_down, g_conv = _exchange(
            [weights[n][l].astype(_MXU_DTYPE) for n in _BIG] + [conv_w[l]], False, name=f"gather_weights_{l}")
        full.append(dict({"in": _from_slabs(g_in), "gate": _from_slabs(g_gate), "up": _from_slabs(g_up),
                          "out": g_out.reshape(D_MODEL, D_MODEL), "down": g_down.reshape(D_FF, D_MODEL)},
                         conv=_from_slabs(g_conv)))

    xs = x[0]
    xb = xs.astype(_MXU_DTYPE)
    saved = []
    for l in range(DEPTH):
        xs, xb, res = _layer_fwd(xs, xb, full[l], lb_logits, a_norm_w[l], c_sinks[l], ln1_g[l], ln1_b[l], full[l]["conv"],
                                 conv_b[l], ln2_g[l], ln2_b[l], tabs, l)
        saved.append(res)
    loss_part, dx = _loss_head(xs, loss_target[0], name="loss_head")
    loss = lax.psum(loss_part, ("x", "y", "c"))

    big_out = {n: [None] * DEPTH for n in _BIG}
    small_parts = [None] * DEPTH
    for l in reversed(range(DEPTH)):
        dx, big, small = _layer_bwd(dx, saved[l], full[l], lb_logits, a_norm_w[l], c_sinks[l], ln1_g[l], full[l]["conv"],
                                    conv_b[l], ln2_g[l], tabs, l)
        parts = _exchange([big[n] for n in _BIG], True, name=f"exchange_grads_{l}")
        for n, p in zip(_BIG, parts):
            big_out[n][l] = _adamw_reduce(p, weights[n][l], mom1[n][l], mom2[n][l], name=f"adamw_{n}_{l}")
        small_parts[l] = _pack(small, _LAYER_ROWS)
    gathered, = _exchange([jnp.concatenate(small_parts, axis=0)], False, name="gather_small_grads")
    g_small = _small_reduce(gathered, lb_logits, name="small_grads")

    per_layer = [(A_HEADS * HEAD_DIM,), (HEAD_DIM,), (_PACK_LANES,), (D_MODEL,), (D_MODEL,), (D_MODEL,), (D_MODEL,), (D_FF,),
                 (3, D_FF)]
    names = ("lb_logits", "a_norm_w", "c_sinks", "ln1_g", "ln1_b", "ln2_g", "ln2_b", "conv_b", "conv_w")
    grads = {n: [] for n in names}
    for l in range(DEPTH):
        for n, t in zip(names, _unpack(g_small[l * _LAYER_ROWS:(l + 1) * _LAYER_ROWS], per_layer)):
            grads[n].append(t)
    grads = {n: jnp.stack(t) for n, t in grads.items()}
    grads["c_sinks"] = grads["c_sinks"][:, :C_HEADS]
    grads["conv_w"] = lax.dynamic_slice_in_dim(grads["conv_w"], me * SHARD_COLS, SHARD_COLS, axis=2)
    shapes = [grads[n].shape for n in names]
    rows = -(-sum(int(np.prod(s)) for s in shapes) // (8 * _PACK_LANES)) * 8
    d_s, m_s, v_s = _adamw_small(_pack([grads[n] for n in names], rows), _pack([weights[n] for n in names], rows),
                                 _pack([mom1[n] for n in names], rows), _pack([mom2[n] for n in names], rows),
                                 name="adamw_small")
    delta = dict(zip(names, _unpack(d_s, shapes)))
    new_m = dict(zip(names, _unpack(m_s, shapes)))
    new_v = dict(zip(names, _unpack(v_s, shapes)))
    for n in _BIG:
        g, d, nm, nv = (jnp.stack(t) for t in zip(*big_out[n]))
        grads[n], delta[n], new_m[n], new_v[n] = g, d, nm, nv

    order = ("w_in", "lb_logits", "a_norm_w", "c_sinks", "w_out", "ln1_g", "ln1_b", "w_gate", "w_up", "conv_w", "conv_b",
             "w_down", "ln2_g", "ln2_b")
    return (loss, dx[None], *[grads[n] for n in order], *[delta[n] for n in order], *[new_m[n] for n in order],
            *[new_v[n] for n in order])
```

```python
import functools

import jax
import jax.numpy as jnp
import numpy as np
from jax import lax
from jax.experimental import pallas as pl
from jax.experimental.pallas import tpu as pltpu

D_MODEL = 2048
SEQ = 2048
DEPTH = 2
HEAD_DIM = 128
A_HEADS = 4
B_HEADS = 6
C_HEADS = 6
C_KV_HEADS = 2
A_CHUNK = 16
DILATIONS = (1, 4, 16)
BLOCK = 128
ROPE_THETA = 500000.0
ROPE_DIM = 32
D_FF = 5632
IN_WIDTH = 5632
LN_EPS = 1e-5
ALPHA = (2 * DEPTH) ** 0.25
N_DEV = 8
SHARD_COLS = IN_WIDTH // N_DEV

ADAM_LR = 0.001
ADAM_B1 = 0.9
ADAM_B2 = 0.999
ADAM_EPS = 1e-08
ADAM_WD = 0.01
ADAM_STEP = 10

A_COLS = 16
QKV_COLS = 28
QB0, KB0, VB0, QC0, KC0, VC0 = 0, 6, 12, 18, 24, 26

_MXU_DTYPE = jnp.bfloat16
_GRAD_DTYPE = jnp.bfloat16
_NEG = -1e30
_VMEM_LIMIT = 56 * 2 ** 20

_F32 = jnp.float32


def _sigmoid(x):
    return 1.0 / (1.0 + jnp.exp(-x))


def _cparams(**kw):
    return pltpu.CompilerParams(vmem_limit_bytes=_VMEM_LIMIT, **kw)


def _mm(a, b, *, ta=False, tb=False, tm, tn, out_dtype=_F32, add=None, add_scale=1.0, name):
    K = a.shape[0] if ta else a.shape[1]
    M = a.shape[1] if ta else a.shape[0]
    N = b.shape[0] if tb else b.shape[1]
    assert (b.shape[1] if tb else b.shape[0]) == K and M % tm == 0 and N % tn == 0
    dn = (((0 if ta else 1,), (1 if tb else 0,)), ((), ()))

    def body(*refs):
        if add is None:
            a_ref, b_ref, o_ref = refs
        else:
            a_ref, b_ref, add_ref, o_ref = refs
        r = lax.dot_general(a_ref[...], b_ref[...], dn, preferred_element_type=_F32)
        if add is not None:
            r = r + add_scale * add_ref[...]
        o_ref[...] = r.astype(o_ref.dtype)

    a_spec = pl.BlockSpec((K, tm), lambda i, j: (0, i)) if ta else pl.BlockSpec((tm, K), lambda i, j: (i, 0))
    b_spec = pl.BlockSpec((tn, K), lambda i, j: (j, 0)) if tb else pl.BlockSpec((K, tn), lambda i, j: (0, j))
    o_spec = pl.BlockSpec((tm, tn), lambda i, j: (i, j))
    in_specs = [a_spec, b_spec] + ([o_spec] if add is not None else [])
    args = (a, b) + ((add,) if add is not None else ())
    return pl.pallas_call(
        body, grid=(M // tm, N // tn), in_specs=in_specs, out_specs=o_spec,
        out_shape=jax.ShapeDtypeStruct((M, N), out_dtype), name=name,
        compiler_params=_cparams(dimension_semantics=("parallel", "parallel")),
    )(*args)


def _mm_tn_slabs(a, b, *, tm, name):
    K, M = a.shape
    assert b.shape == (N_DEV, K, SHARD_COLS) and M % tm == 0

    def body(a_ref, b_ref, o_ref):
        o_ref[...] = lax.dot_general(a_ref[...], b_ref[...], _TN, preferred_element_type=_F32).astype(o_ref.dtype)

    return pl.pallas_call(
        body, grid=(M // tm, N_DEV),
        in_specs=[pl.BlockSpec((K, tm), lambda i, j: (0, i)), pl.BlockSpec((None, K, SHARD_COLS), lambda i, j: (j, 0, 0))],
        out_specs=pl.BlockSpec((None, tm, SHARD_COLS), lambda i, j: (j, i, 0)),
        out_shape=jax.ShapeDtypeStruct((N_DEV, M, SHARD_COLS), _GRAD_DTYPE), name=name,
        compiler_params=_cparams(dimension_semantics=("parallel", "parallel")),
    )(a, b)


def _to_slabs(m):
    return m.reshape(m.shape[0], N_DEV, SHARD_COLS).transpose(1, 0, 2)


def _from_slabs(s):
    return s.transpose(1, 0, 2).reshape(s.shape[1], N_DEV * SHARD_COLS)


def _ln_fwd(x, y, g, b, *, name):
    tm = 256

    def body(x_ref, y_ref, g_ref, b_ref, z_ref, o_ref, ob_ref):
        z = ALPHA * x_ref[...] + y_ref[...]
        mu = jnp.mean(z, axis=-1, keepdims=True)
        zc = z - mu
        var = jnp.mean(zc * zc, axis=-1, keepdims=True)
        o = zc * lax.rsqrt(var + LN_EPS) * g_ref[...] + b_ref[...]
        z_ref[...] = z
        o_ref[...] = o
        ob_ref[...] = o.astype(ob_ref.dtype)

    row = pl.BlockSpec((tm, D_MODEL), lambda i: (i, 0))
    vec = pl.BlockSpec((1, D_MODEL), lambda i: (0, 0))
    return pl.pallas_call(
        body, grid=(SEQ // tm,), in_specs=[row, row, vec, vec], out_specs=[row, row, row],
        out_shape=[jax.ShapeDtypeStruct((SEQ, D_MODEL), _F32), jax.ShapeDtypeStruct((SEQ, D_MODEL), _F32),
                   jax.ShapeDtypeStruct((SEQ, D_MODEL), _MXU_DTYPE)],
        name=name, compiler_params=_cparams(dimension_semantics=("parallel",)),
    )(x, y, g.reshape(1, D_MODEL), b.reshape(1, D_MODEL))


def _ln_bwd(z, d_a, d_res, g, *, name):
    tm = 256

    def body(*refs):
        if d_res is None:
            z_ref, da_ref, g_ref, dz_ref, dzb_ref, dg_ref, db_ref = refs
            dout = da_ref[...]
        else:
            z_ref, da_ref, dr_ref, g_ref, dz_ref, dzb_ref, dg_ref, db_ref = refs
            dout = da_ref[...] + ALPHA * dr_ref[...]
        z = z_ref[...]
        mu = jnp.mean(z, axis=-1, keepdims=True)
        zc = z - mu
        var = jnp.mean(zc * zc, axis=-1, keepdims=True)
        rstd = lax.rsqrt(var + LN_EPS)
        xh = zc * rstd
        dxh = dout * g_ref[...]
        m1 = jnp.mean(dxh, axis=-1, keepdims=True)
        m2 = jnp.mean(dxh * xh, axis=-1, keepdims=True)
        dz = rstd * (dxh - m1 - xh * m2)
        dz_ref[...] = dz
        dzb_ref[...] = dz.astype(dzb_ref.dtype)

        @pl.when(pl.program_id(0) == 0)
        def _():
            dg_ref[...] = jnp.zeros_like(dg_ref)
            db_ref[...] = jnp.zeros_like(db_ref)

        dg_ref[0:1, :] += jnp.sum(dout * xh, axis=0, keepdims=True)
        db_ref[0:1, :] += jnp.sum(dout, axis=0, keepdims=True)

    row = pl.BlockSpec((tm, D_MODEL), lambda i: (i, 0))
    vec = pl.BlockSpec((1, D_MODEL), lambda i: (0, 0))
    acc = pl.BlockSpec((8, D_MODEL), lambda i: (0, 0))
    ins = [z, d_a] + ([d_res] if d_res is not None else []) + [g.reshape(1, D_MODEL)]
    in_specs = [row, row] + ([row] if d_res is not None else []) + [vec]
    dz, dzb, dg, db = pl.pallas_call(
        body, grid=(SEQ // tm,), in_specs=in_specs, out_specs=[row, row, acc, acc],
        out_shape=[jax.ShapeDtypeStruct((SEQ, D_MODEL), _F32), jax.ShapeDtypeStruct((SEQ, D_MODEL), _MXU_DTYPE),
                   jax.ShapeDtypeStruct((8, D_MODEL), _F32), jax.ShapeDtypeStruct((8, D_MODEL), _F32)],
        name=name, compiler_params=_cparams(dimension_semantics=("arbitrary",)),
    )(*ins)
    return dz, dzb, dg[0], db[0]


def _loss_head(y, target, *, name):
    tm = 256

    def body(y_ref, t_ref, d_ref, l_ref):
        e = y_ref[...] - t_ref[...]
        d_ref[...] = e * (1.0 / D_MODEL)

        @pl.when(pl.program_id(0) == 0)
        def _():
            l_ref[...] = jnp.zeros_like(l_ref)

        l_ref[...] += (0.5 / D_MODEL) * jnp.sum(e * e)

    row = pl.BlockSpec((tm, D_MODEL), lambda i: (i, 0))
    d, l = pl.pallas_call(
        body, grid=(SEQ // tm,), in_specs=[row, row], out_specs=[row, pl.BlockSpec((8, 128), lambda i: (0, 0))],
        out_shape=[jax.ShapeDtypeStruct((SEQ, D_MODEL), _F32), jax.ShapeDtypeStruct((8, 128), _F32)],
        name=name, compiler_params=_cparams(dimension_semantics=("arbitrary",)),
    )(y, target)
    return l[0, 0], d


_CONV_TN = 256


def _shift_down(v, k, rows):
    return jnp.where(rows >= k, pltpu.roll(v, k, axis=0), 0.0)


def _shift_up(v, k, rows):
    return jnp.where(rows < SEQ - k, pltpu.roll(v, SEQ - k, axis=0), 0.0)


def _conv_gate_fwd(g, u, conv_w, conv_b, *, name):
    def body(g_ref, u_ref, w_ref, b_ref, h_ref):
        gv = g_ref[...]
        rows = lax.broadcasted_iota(jnp.int32, gv.shape, 0)
        w = w_ref[...]
        gc = b_ref[...] + w[2:3, :] * gv + w[1:2, :] * _shift_down(gv, 1, rows) + w[0:1, :] * _shift_down(gv, 2, rows)
        h_ref[...] = (gc * _sigmoid(gc) * u_ref[...]).astype(h_ref.dtype)

    col = pl.BlockSpec((SEQ, _CONV_TN), lambda j: (0, j))
    return pl.pallas_call(
        body, grid=(D_FF // _CONV_TN,),
        in_specs=[col, col, pl.BlockSpec((3, _CONV_TN), lambda j: (0, j)), pl.BlockSpec((1, _CONV_TN), lambda j: (0, j))],
        out_specs=col, out_shape=jax.ShapeDtypeStruct((SEQ, D_FF), _MXU_DTYPE), name=name,
        compiler_params=_cparams(dimension_semantics=("parallel",)),
    )(g, u, conv_w, conv_b.reshape(1, D_FF))


def _conv_gate_bwd(dh, g, u, conv_w, conv_b, *, name):
    def body(dh_ref, g_ref, u_ref, w_ref, b_ref, dg_ref, du_ref, dw_ref, db_ref):
        gv = g_ref[...]
        rows = lax.broadcasted_iota(jnp.int32, gv.shape, 0)
        w = w_ref[...]
        g1 = _shift_down(gv, 1, rows)
        g2 = _shift_down(gv, 2, rows)
        gc = b_ref[...] + w[2:3, :] * gv + w[1:2, :] * g1 + w[0:1, :] * g2
        sg = _sigmoid(gc)
        dh = dh_ref[...]
        du_ref[...] = (dh * (gc * sg)).astype(du_ref.dtype)
        dgc = dh * u_ref[...] * (sg * (1.0 + gc * (1.0 - sg)))
        dg = w[2:3, :] * dgc + w[1:2, :] * _shift_up(dgc, 1, rows) + w[0:1, :] * _shift_up(dgc, 2, rows)
        dg_ref[...] = dg.astype(dg_ref.dtype)
        dw_ref[0:1, :] = jnp.sum(dgc * g2, axis=0, keepdims=True)
        dw_ref[1:2, :] = jnp.sum(dgc * g1, axis=0, keepdims=True)
        dw_ref[2:3, :] = jnp.sum(dgc * gv, axis=0, keepdims=True)
        db_ref[...] = jnp.sum(dgc, axis=0, keepdims=True)

    col = pl.BlockSpec((SEQ, _CONV_TN), lambda j: (0, j))
    w3 = pl.BlockSpec((3, _CONV_TN), lambda j: (0, j))
    w1 = pl.BlockSpec((1, _CONV_TN), lambda j: (0, j))
    dg, du, dw, db = pl.pallas_call(
        body, grid=(D_FF // _CONV_TN,), in_specs=[col, col, col, w3, w1], out_specs=[col, col, w3, w1],
        out_shape=[jax.ShapeDtypeStruct((SEQ, D_FF), _MXU_DTYPE), jax.ShapeDtypeStruct((SEQ, D_FF), _MXU_DTYPE),
                   jax.ShapeDtypeStruct((3, D_FF), _F32), jax.ShapeDtypeStruct((1, D_FF), _F32)],
        name=name, compiler_params=_cparams(dimension_semantics=("parallel",)),
    )(dh, g, u, conv_w, conv_b.reshape(1, D_FF))
    return dg, du, dw, db[0]


def _rope_tables():
    half = ROPE_DIM // 2
    inv = ROPE_THETA ** (-jnp.arange(0, ROPE_DIM, 2, dtype=_F32) / ROPE_DIM)
    ang = jnp.arange(SEQ, dtype=_F32)[:, None] * inv[None, :]
    cos, sin = jnp.cos(ang), jnp.sin(ang)
    rest = HEAD_DIM - ROPE_DIM
    c = jnp.concatenate([cos, cos, jnp.ones((SEQ, rest), _F32)], axis=1)
    s1 = jnp.concatenate([-sin, jnp.zeros((SEQ, HEAD_DIM - half), _F32)], axis=1)
    s2 = jnp.concatenate([jnp.zeros((SEQ, half), _F32), sin, jnp.zeros((SEQ, rest), _F32)], axis=1)
    return c, s1, s2


def _rope_apply(x, c, s1, s2):
    return x * c + pltpu.roll(x, HEAD_DIM - ROPE_DIM // 2, axis=1) * s1 + pltpu.roll(x, ROPE_DIM // 2, axis=1) * s2


def _rope_transpose(d, c, s1, s2):
    half = ROPE_DIM // 2
    return d * c + pltpu.roll(d * s1, half, axis=1) + pltpu.roll(d * s2, HEAD_DIM - half, axis=1)


def _is_rope_block(j):
    return (j < VB0) | ((j >= QC0) & (j < VC0))


def _qkv_prep(proj, tabs, *, name):
    tm = 512

    def body(p_ref, c_ref, s1_ref, s2_ref, o_ref):
        j = pl.program_id(1)
        x = p_ref[...]
        r = _rope_apply(x, c_ref[...], s1_ref[...], s2_ref[...])
        o_ref[...] = jnp.where(_is_rope_block(j), r, x).astype(o_ref.dtype)

    tab = pl.BlockSpec((tm, HEAD_DIM), lambda i, j: (i, 0))
    return pl.pallas_call(
        body, grid=(SEQ // tm, QKV_COLS),
        in_specs=[pl.BlockSpec((tm, HEAD_DIM), lambda i, j: (i, A_COLS + j)), tab, tab, tab],
        out_specs=pl.BlockSpec((tm, HEAD_DIM), lambda i, j: (i, j)),
        out_shape=jax.ShapeDtypeStruct((SEQ, QKV_COLS * HEAD_DIM), _MXU_DTYPE), name=name,
        compiler_params=_cparams(dimension_semantics=("parallel", "parallel")),
    )(proj, *tabs)


def _qkv_grad_finish(dq1, dk1, dv1, dq4, dk4, dv4, dq16, dk16, dv16, dqc, dkc, dvc, tabs, *, name):
    tm = 512

    def body(a1, a4, a16, k1, k4, k16, v1, v4, v16, qc, kc, vc, c_ref, s1_ref, s2_ref, o_ref):
        c, s1, s2 = c_ref[...], s1_ref[...], s2_ref[...]
        for h in range(B_HEADS):
            sl = slice(h * HEAD_DIM, (h + 1) * HEAD_DIM)
            o_ref[:, (QB0 + h) * HEAD_DIM:(QB0 + h + 1) * HEAD_DIM] = _rope_transpose(
                a1[:, sl] + a4[:, sl] + a16[:, sl], c, s1, s2).astype(o_ref.dtype)
            o_ref[:, (KB0 + h) * HEAD_DIM:(KB0 + h + 1) * HEAD_DIM] = _rope_transpose(
                k1[:, sl] + k4[:, sl] + k16[:, sl], c, s1, s2).astype(o_ref.dtype)
            o_ref[:, (VB0 + h) * HEAD_DIM:(VB0 + h + 1) * HEAD_DIM] = (v1[:, sl] + v4[:, sl] + v16[:, sl]).astype(o_ref.dtype)
            o_ref[:, (QC0 + h) * HEAD_DIM:(QC0 + h + 1) * HEAD_DIM] = _rope_transpose(qc[:, sl], c, s1, s2).astype(o_ref.dtype)
        for h in range(C_KV_HEADS):
            sl = slice(h * HEAD_DIM, (h + 1) * HEAD_DIM)
            o_ref[:, (KC0 + h) * HEAD_DIM:(KC0 + h + 1) * HEAD_DIM] = _rope_transpose(kc[:, sl], c, s1, s2).astype(o_ref.dtype)
            o_ref[:, (VC0 + h) * HEAD_DIM:(VC0 + h + 1) * HEAD_DIM] = vc[:, sl].astype(o_ref.dtype)

    wb = pl.BlockSpec((tm, B_HEADS * HEAD_DIM), lambda i: (i, 0))
    wkv = pl.BlockSpec((tm, C_KV_HEADS * HEAD_DIM), lambda i: (i, 0))
    tab = pl.BlockSpec((tm, HEAD_DIM), lambda i: (i, 0))
    return pl.pallas_call(
        body, grid=(SEQ // tm,), in_specs=[wb] * 10 + [wkv, wkv, tab, tab, tab],
        out_specs=pl.BlockSpec((tm, QKV_COLS * HEAD_DIM), lambda i: (i, 0)),
        out_shape=jax.ShapeDtypeStruct((SEQ, QKV_COLS * HEAD_DIM), _MXU_DTYPE), name=name,
        compiler_params=_cparams(dimension_semantics=("parallel",)),
    )(dq1, dq4, dq16, dk1, dk4, dk16, dv1, dv4, dv16, dqc, dkc, dvc, *tabs)


_NT = (((1,), (1,)), ((), ()))
_TN = (((0,), (0,)), ((), ()))
_SCALE = HEAD_DIM ** -0.5


def _band_scores(q, kp, kc, n, lag_off):
    sp = lax.dot_general(q, kp, _NT, preferred_element_type=_F32) * _SCALE
    sc = lax.dot_general(q, kc, _NT, preferred_element_type=_F32) * _SCALE
    row = lax.broadcasted_iota(jnp.int32, (BLOCK, BLOCK), 0)
    col = lax.broadcasted_iota(jnp.int32, (BLOCK, BLOCK), 1)
    sp = jnp.where((col >= row + lag_off) & (n > 0), sp, _NEG)
    sc = jnp.where(col <= row, sc, _NEG)
    return sp, sc


def _attn_specs(dil, q0, k0, v0, kv_group):
    def q_map(h, r, n):
        return (n, r * QKV_COLS + q0 + h)

    def kv_map(base, prev):
        def f(h, r, n):
            return (jnp.maximum(n - 1, 0) if prev else n, r * QKV_COLS + base + h // kv_group)
        return f

    blk = (BLOCK, HEAD_DIM)
    return [pl.BlockSpec(blk, q_map), pl.BlockSpec(blk, kv_map(k0, True)), pl.BlockSpec(blk, kv_map(k0, False)),
            pl.BlockSpec(blk, kv_map(v0, True)), pl.BlockSpec(blk, kv_map(v0, False))]


def _attn_fwd(qkv, dil, *, heads, q0, k0, v0, kv_group, lag_off, sink, name):
    L = SEQ // dil
    nb = L // BLOCK
    qkv_r = qkv.reshape(L, dil * QKV_COLS * HEAD_DIM)

    def body(*refs):
        if sink is None:
            q_ref, kp_ref, kc_ref, vp_ref, vc_ref, o_ref, lse_ref = refs
        else:
            q_ref, kp_ref, kc_ref, vp_ref, vc_ref, sk_ref, o_ref, lse_ref = refs
        n = pl.program_id(2)
        sp, sc = _band_scores(q_ref[...], kp_ref[...], kc_ref[...], n, lag_off)
        m = jnp.maximum(jnp.max(sp, axis=1, keepdims=True), jnp.max(sc, axis=1, keepdims=True))
        if sink is not None:
            sk = sk_ref[0][:, 0:1]
            m = jnp.maximum(m, sk)
        pp = jnp.exp(sp - m)
        pc = jnp.exp(sc - m)
        den = jnp.sum(pp, axis=1, keepdims=True) + jnp.sum(pc, axis=1, keepdims=True)
        if sink is not None:
            den = den + jnp.exp(sk - m)
        acc = jnp.dot(pp.astype(_MXU_DTYPE), vp_ref[...], preferred_element_type=_F32)
        acc = acc + jnp.dot(pc.astype(_MXU_DTYPE), vc_ref[...], preferred_element_type=_F32)
        o_ref[...] = acc / den
        lse_ref[...] = jnp.broadcast_to(m + jnp.log(den), (BLOCK, HEAD_DIM))

    in_specs = _attn_specs(dil, q0, k0, v0, kv_group)
    args = [qkv_r] * 5
    if sink is not None:
        in_specs.append(pl.BlockSpec((1, 1, HEAD_DIM), lambda h, r, n: (h, 0, 0)))
        args.append(jnp.broadcast_to(sink.reshape(heads, 1, 1), (heads, 1, HEAD_DIM)))
    o_spec = pl.BlockSpec((BLOCK, HEAD_DIM), lambda h, r, n: (n, r * heads + h))
    shape = jax.ShapeDtypeStruct((L, dil * heads * HEAD_DIM), _F32)
    o, lse = pl.pallas_call(
        body, grid=(heads, dil, nb), in_specs=in_specs, out_specs=[o_spec, o_spec], out_shape=[shape, shape],
        name=name, compiler_params=_cparams(dimension_semantics=("parallel", "parallel", "parallel")),
    )(*args)
    return o.reshape(SEQ, heads * HEAD_DIM), lse.reshape(SEQ, heads * HEAD_DIM)


def _attn_bwd(qkv, dmixed, o, lse, dil, *, heads, q0, k0, v0, kv_group, lag_off, do0, sink, name):
    L = SEQ // dil
    nb = L // BLOCK
    kvh = heads // kv_group
    qkv_r = qkv.reshape(L, dil * QKV_COLS * HEAD_DIM)
    dm_r = dmixed.reshape(L, dil * D_MODEL)
    o_r = o.reshape(L, dil * heads * HEAD_DIM)
    lse_r = lse.reshape(L, dil * heads * HEAD_DIM)

    def body(*refs):
        if sink is None:
            q_ref, kp_ref, kc_ref, vp_ref, vc_ref, do_ref, o_ref, lse_ref, dq_ref, dk_ref, dv_ref = refs
        else:
            q_ref, kp_ref, kc_ref, vp_ref, vc_ref, do_ref, o_ref, lse_ref, sk_ref, dq_ref, dk_ref, dv_ref, dsk_ref = refs
        g = pl.program_id(0)
        n = pl.program_id(2)
        q, kp, kc, vp, vc = q_ref[...], kp_ref[...], kc_ref[...], vp_ref[...], vc_ref[...]
        do = do_ref[...]
        delta = jnp.sum(do * o_ref[...], axis=1, keepdims=True)
        lse_c = lse_ref[:, 0:1]
        sp, sc = _band_scores(q, kp, kc, n, lag_off)
        pp = jnp.exp(sp - lse_c)
        pc = jnp.exp(sc - lse_c)
        dob = do.astype(_MXU_DTYPE)
        dsp = (pp * (lax.dot_general(dob, vp, _NT, preferred_element_type=_F32) - delta) * _SCALE).astype(_MXU_DTYPE)
        dsc = (pc * (lax.dot_general(dob, vc, _NT, preferred_element_type=_F32) - delta) * _SCALE).astype(_MXU_DTYPE)
        dq_ref[...] = (jnp.dot(dsp, kp, preferred_element_type=_F32) + jnp.dot(dsc, kc, preferred_element_type=_F32))

        @pl.when((n == 0) & (g % kv_group == 0))
        def _():
            dk_ref[...] = jnp.zeros_like(dk_ref)
            dv_ref[...] = jnp.zeros_like(dv_ref)

        prev = pl.ds(pl.multiple_of(jnp.maximum(n - 1, 0) * BLOCK, BLOCK), BLOCK)
        cur = pl.ds(pl.multiple_of(n * BLOCK, BLOCK), BLOCK)
        dk_ref[prev, :] += lax.dot_general(dsp, q, _TN, preferred_element_type=_F32)
        dv_ref[prev, :] += lax.dot_general(pp.astype(_MXU_DTYPE), dob, _TN, preferred_element_type=_F32)
        dk_ref[cur, :] += lax.dot_general(dsc, q, _TN, preferred_element_type=_F32)
        dv_ref[cur, :] += lax.dot_general(pc.astype(_MXU_DTYPE), dob, _TN, preferred_element_type=_F32)
        if sink is not None:
            @pl.when(n == 0)
            def _():
                dsk_ref[...] = jnp.zeros_like(dsk_ref)

            sk = sk_ref[0][:, 0:1]
            dsk_ref[...] += jnp.sum(-delta * jnp.exp(sk - lse_c))

    in_specs = _attn_specs(dil, q0, k0, v0, kv_group)
    blk = (BLOCK, HEAD_DIM)
    in_specs.append(pl.BlockSpec(blk, lambda h, r, n: (n, r * (D_MODEL // HEAD_DIM) + do0 + h)))
    in_specs.append(pl.BlockSpec(blk, lambda h, r, n: (n, r * heads + h)))
    in_specs.append(pl.BlockSpec(blk, lambda h, r, n: (n, r * heads + h)))
    args = [qkv_r] * 5 + [dm_r, o_r, lse_r]
    if sink is not None:
        in_specs.append(pl.BlockSpec((1, 1, HEAD_DIM), lambda h, r, n: (h, 0, 0)))
        args.append(jnp.broadcast_to(sink.reshape(heads, 1, 1), (heads, 1, HEAD_DIM)))
    kv_spec = pl.BlockSpec((L, HEAD_DIM), lambda h, r, n: (0, r * kvh + h // kv_group))
    out_specs = [pl.BlockSpec(blk, lambda h, r, n: (n, r * heads + h)), kv_spec, kv_spec]
    out_shape = [jax.ShapeDtypeStruct((L, dil * heads * HEAD_DIM), _F32),
                 jax.ShapeDtypeStruct((L, dil * kvh * HEAD_DIM), _F32),
                 jax.ShapeDtypeStruct((L, dil * kvh * HEAD_DIM), _F32)]
    if sink is not None:
        out_specs.append(pl.BlockSpec((1, 8, HEAD_DIM), lambda h, r, n: (h, 0, 0)))
        out_shape.append(jax.ShapeDtypeStruct((heads, 8, HEAD_DIM), _F32))
    res = pl.pallas_call(
        body, grid=(heads, dil, nb), in_specs=in_specs, out_specs=out_specs, out_shape=out_shape, name=name,
        compiler_params=_cparams(dimension_semantics=("arbitrary", "arbitrary", "arbitrary")),
    )(*args)
    dq = res[0].reshape(SEQ, heads * HEAD_DIM)
    dk = res[1].reshape(SEQ, kvh * HEAD_DIM)
    dv = res[2].reshape(SEQ, kvh * HEAD_DIM)
    if sink is not None:
        return dq, dk, dv, res[3][:, 0, 0]
    return dq, dk, dv


def _dilated_mix(os_, lses, *, name):
    tm = 256

    def body(o1, o4, o16, l1, l4, l16, o_ref, lse_ref):
        a, b, c = l1[...], l4[...], l16[...]
        m = jnp.maximum(jnp.maximum(a, b), c)
        ea, eb, ec = jnp.exp(a - m), jnp.exp(b - m), jnp.exp(c - m)
        den = ea + eb + ec
        o_ref[...] = (ea * o1[...] + eb * o4[...] + ec * o16[...]) / den
        lse_ref[...] = m + jnp.log(den)

    w = B_HEADS * HEAD_DIM
    spec = pl.BlockSpec((tm, w), lambda i: (i, 0))
    shape = jax.ShapeDtypeStruct((SEQ, w), _F32)
    return pl.pallas_call(
        body, grid=(SEQ // tm,), in_specs=[spec] * 6, out_specs=[spec, spec], out_shape=[shape, shape], name=name,
        compiler_params=_cparams(dimension_semantics=("parallel",)),
    )(*os_, *lses)


_HG_TILE = 128
_HG_CHUNKS = _HG_TILE // A_CHUNK
_HG_TILES = SEQ // _HG_TILE
_HI = lax.Precision.HIGHEST


def _chunk_tri():
    i = np.arange(_HG_TILE)
    return jnp.asarray(((i[:, None] // A_CHUNK == i[None, :] // A_CHUNK) & (i[None, :] <= i[:, None])).astype(np.float32))


def _layer_lb(lb_ref, layer):
    if layer == 0:
        return jnp.zeros((1, HEAD_DIM), _F32)
    lg = lb_ref[...]
    m = jnp.max(lg, axis=0, keepdims=True)
    e = jnp.exp(lg - m)
    return e[1:2, :] / jnp.sum(e, axis=0, keepdims=True)


def _hgrn_gates(q, fr, lb):
    sgq = _sigmoid(q)
    sg = _sigmoid(fr)
    f = lb + (1.0 - lb) * sg
    return sgq, q * sgq, sg, f, 1.0 - f


def _hgrn_fwd(proj, lb_logits, norm_w, layer, *, name):
    tri = _chunk_tri()

    def body(q_ref, f_ref, i_ref, g_ref, lb_ref, nw_ref, tri_ref, o_ref, raw_ref, st_ref, state):
        @pl.when(pl.program_id(1) == 0)
        def _():
            state[...] = jnp.zeros_like(state)

        lb = _layer_lb(lb_ref, layer)
        _, qs, _, f, k = _hgrn_gates(q_ref[...], f_ref[...], lb)
        v = i_ref[...]
        b = jnp.dot(tri_ref[...], jnp.log(f), precision=_HI, preferred_element_type=_F32)
        eb = jnp.exp(b)
        ridx = lax.broadcasted_iota(jnp.int32, (A_CHUNK, HEAD_DIM), 0)
        outs = []
        for c in range(_HG_CHUNKS):
            sl = slice(c * A_CHUNK, (c + 1) * A_CHUNK)
            bc, qc, kc, vc = b[sl], qs[sl], k[sl], v[sl]
            bl = bc[A_CHUNK - 1:A_CHUNK]
            st = state[...]
            st_ref[0, c] = st
            o_c = lax.dot_general((qc * eb[sl]).astype(_MXU_DTYPE), st.astype(_MXU_DTYPE), _NT, preferred_element_type=_F32)
            rows = []
            for i in range(A_CHUNK):
                di = jnp.exp(jnp.where(ridx <= i, bc[i:i + 1] - bc, _NEG))
                a = jnp.sum(qc[i:i + 1] * kc * di, axis=1, keepdims=True)
                rows.append(jnp.sum(a * vc, axis=0, keepdims=True))
            outs.append(o_c + jnp.concatenate(rows, axis=0))
            kt = (kc * jnp.exp(bl - bc)).astype(_MXU_DTYPE)
            state[...] = st * jnp.exp(bl) + lax.dot_general(vc.astype(_MXU_DTYPE), kt, _TN, preferred_element_type=_F32)
        o = jnp.concatenate(outs, axis=0)
        raw_ref[...] = o
        r = lax.rsqrt(jnp.mean(o * o, axis=-1, keepdims=True) + LN_EPS)
        g = g_ref[...]
        o_ref[...] = o * r * nw_ref[...] * (g * _sigmoid(g))

    blk = (_HG_TILE, HEAD_DIM)

    def col(base):
        return pl.BlockSpec(blk, lambda h, t: (t, base + h))

    o_spec = pl.BlockSpec(blk, lambda h, t: (t, h))
    o_shape = jax.ShapeDtypeStruct((SEQ, A_HEADS * HEAD_DIM), _F32)
    return pl.pallas_call(
        body, grid=(A_HEADS, _HG_TILES),
        in_specs=[col(0), col(4), col(8), col(12), pl.BlockSpec((DEPTH, HEAD_DIM), lambda h, t: (0, h)),
                  pl.BlockSpec((1, HEAD_DIM), lambda h, t: (0, 0)), pl.BlockSpec(blk, lambda h, t: (0, 0))],
        out_specs=[o_spec, o_spec, pl.BlockSpec((1, _HG_CHUNKS, HEAD_DIM, HEAD_DIM), lambda h, t: (h, t, 0, 0))],
        out_shape=[o_shape, o_shape, jax.ShapeDtypeStruct((A_HEADS, SEQ // A_CHUNK, HEAD_DIM, HEAD_DIM), _F32)],
        scratch_shapes=[pltpu.VMEM((HEAD_DIM, HEAD_DIM), _F32)], name=name,
        compiler_params=_cparams(dimension_semantics=("parallel", "arbitrary")),
    )(proj, proj, proj, proj, lb_logits, norm_w.reshape(1, HEAD_DIM), tri)


def _hgrn_bwd(proj, lb_logits, norm_w, raw, states, dmixed, layer, *, name):
    tri = _chunk_tri()
    triu = tri.T

    def body(q_ref, f_ref, i_ref, g_ref, lb_ref, nw_ref, tri_ref, triu_ref, raw_ref, do_ref, st_ref,
             dq_ref, df_ref, di_ref, dg_ref, dnw_ref, dlb_ref, dstate):
        @pl.when(pl.program_id(1) == 0)
        def _():
            dstate[...] = jnp.zeros_like(dstate)
            dlb_ref[...] = jnp.zeros_like(dlb_ref)

        @pl.when((pl.program_id(0) == 0) & (pl.program_id(1) == 0))
        def _():
            dnw_ref[...] = jnp.zeros_like(dnw_ref)

        lb = _layer_lb(lb_ref, layer)
        q = q_ref[...]
        sgq, qs, sg, f, k = _hgrn_gates(q, f_ref[...], lb)
        v = i_ref[...]
        b = jnp.dot(tri_ref[...], jnp.log(f), precision=_HI, preferred_element_type=_F32)
        eb = jnp.exp(b)
        g = g_ref[...]
        nw = nw_ref[...]
        o = raw_ref[...]
        dout = do_ref[...]
        sgg = _sigmoid(g)
        r = lax.rsqrt(jnp.mean(o * o, axis=-1, keepdims=True) + LN_EPS)
        dg_ref[...] = (dout * (o * r * nw) * (sgg * (1.0 + g * (1.0 - sgg)))).astype(dg_ref.dtype)
        don = dout * (g * sgg)
        dnw_ref[0:1, :] += jnp.sum(don * o * r, axis=0, keepdims=True)
        dy = don * nw
        do_raw = r * dy - o * (r * r * r) * jnp.mean(o * dy, axis=-1, keepdims=True)

        ridx = lax.broadcasted_iota(jnp.int32, (A_CHUNK, HEAD_DIM), 0)
        dqs_t, dk_t, db_t, dv_t = [None] * _HG_CHUNKS, [None] * _HG_CHUNKS, [None] * _HG_CHUNKS, [None] * _HG_CHUNKS
        for c in reversed(range(_HG_CHUNKS)):
            sl = slice(c * A_CHUNK, (c + 1) * A_CHUNK)
            bc, qc, kc, vc, doc = b[sl], qs[sl], k[sl], v[sl], do_raw[sl]
            bl = bc[A_CHUNK - 1:A_CHUNK]
            ebc = eb[sl]
            ebl = jnp.exp(bl - bc)
            lam = jnp.exp(bl)
            qt = qc * ebc
            kt = kc * ebl
            dst = dstate[...]
            stp = st_ref[0, c]
            dob = doc.astype(_MXU_DTYPE)
            dstb = dst.astype(_MXU_DTYPE)
            dqt = jnp.dot(dob, stp.astype(_MXU_DTYPE), preferred_element_type=_F32)
            dkt = jnp.dot(vc.astype(_MXU_DTYPE), dstb, preferred_element_type=_F32)
            dv = lax.dot_general(kt.astype(_MXU_DTYPE), dstb, _NT, preferred_element_type=_F32)
            dlam = jnp.sum(stp * dst, axis=0, keepdims=True)
            dstate[...] = dst * lam + lax.dot_general(dob, qt.astype(_MXU_DTYPE), _TN, preferred_element_type=_F32)
            dqs_rows = []
            dk_in = jnp.zeros((A_CHUNK, HEAD_DIM), _F32)
            for i in range(A_CHUNK):
                di = jnp.exp(jnp.where(ridx <= i, bc[i:i + 1] - bc, _NEG))
                qi = qc[i:i + 1]
                doi = doc[i:i + 1]
                w = kc * di
                a = jnp.sum(qi * w, axis=1, keepdims=True)
                dv = dv + a * doi
                da = jnp.sum(doi * vc, axis=1, keepdims=True)
                dqs_rows.append(jnp.sum(da * w, axis=0, keepdims=True))
                dk_in = dk_in + da * (qi * di)
            dqs_in = jnp.concatenate(dqs_rows, axis=0)
            dbl = jnp.sum(dkt * kt, axis=0, keepdims=True) + dlam * lam
            db = qc * dqs_in - kc * dk_in + dqt * qt - dkt * kt
            db_t[c] = db + jnp.where(ridx == A_CHUNK - 1, dbl, 0.0)
            dqs_t[c] = dqs_in + dqt * ebc
            dk_t[c] = dk_in + dkt * ebl
            dv_t[c] = dv
        dqs = jnp.concatenate(dqs_t, axis=0)
        dk = jnp.concatenate(dk_t, axis=0)
        db = jnp.concatenate(db_t, axis=0)
        di_ref[...] = jnp.concatenate(dv_t, axis=0).astype(di_ref.dtype)
        dlogf = jnp.dot(triu_ref[...], db, precision=_HI, preferred_element_type=_F32)
        df = dlogf / f - dk
        df_ref[...] = (df * (1.0 - lb) * sg * (1.0 - sg)).astype(df_ref.dtype)
        dlb_ref[0, 0:1, :] += jnp.sum(df * (1.0 - sg), axis=0, keepdims=True)
        dq_ref[...] = (dqs * (sgq * (1.0 + q * (1.0 - sgq)))).astype(dq_ref.dtype)

    blk = (_HG_TILE, HEAD_DIM)
    last = _HG_TILES - 1

    def col(base):
        return pl.BlockSpec(blk, lambda h, t: (last - t, base + h))

    tri_spec = pl.BlockSpec(blk, lambda h, t: (0, 0))
    acc_spec = pl.BlockSpec((1, 8, HEAD_DIM), lambda h, t: (h, 0, 0))
    acc_shape = jax.ShapeDtypeStruct((A_HEADS, 8, HEAD_DIM), _F32)
    dq, df, di, dg, dnw, dlb = pl.pallas_call(
        body, grid=(A_HEADS, _HG_TILES),
        in_specs=[col(0), col(4), col(8), col(12), pl.BlockSpec((DEPTH, HEAD_DIM), lambda h, t: (0, h)),
                  pl.BlockSpec((1, HEAD_DIM), lambda h, t: (0, 0)), tri_spec, tri_spec, col(0), col(0),
                  pl.BlockSpec((1, _HG_CHUNKS, HEAD_DIM, HEAD_DIM), lambda h, t: (h, last - t, 0, 0))],
        out_specs=[col(0), col(0), col(0), col(0), pl.BlockSpec((8, HEAD_DIM), lambda h, t: (0, 0)), acc_spec],
        out_shape=[jax.ShapeDtypeStruct((SEQ, A_HEADS * HEAD_DIM), _MXU_DTYPE)] * 4
        + [jax.ShapeDtypeStruct((8, HEAD_DIM), _F32), acc_shape],
        scratch_shapes=[pltpu.VMEM((HEAD_DIM, HEAD_DIM), _F32)], name=name,
        compiler_params=_cparams(dimension_semantics=("arbitrary", "arbitrary")),
    )(proj, proj, proj, proj, lb_logits, norm_w.reshape(1, HEAD_DIM), tri, triu, raw, dmixed, states)
    return dq, df, di, dg, dnw[0], dlb[:, 0, :].reshape(A_HEADS * HEAD_DIM)


def _exchange(arrays, scatter, *, name):
    n = len(arrays)
    n_peer = N_DEV - 1

    def body(*refs):
        ins, outs = refs[:n], refs[n:2 * n]
        send_sems, recv_sems, loc_sems = refs[2 * n:]
        x, y, c = lax.axis_index("x"), lax.axis_index("y"), lax.axis_index("c")
        me = 4 * x + 2 * y + c
        local = []
        for a in range(n):
            cp = pltpu.make_async_copy(ins[a].at[me] if scatter else ins[a], outs[a].at[me], loc_sems.at[a])
            cp.start()
            local.append(cp)

        def peer(k):
            px = jnp.bitwise_xor(x, (k >> 2) & 1)
            py = jnp.bitwise_xor(y, (k >> 1) & 1)
            pc = jnp.bitwise_xor(c, k & 1)
            return (px, py, pc), 4 * px + 2 * py + pc

        def copy(a, k):
            dev, pid = peer(k)
            return pltpu.make_async_remote_copy(
                src_ref=ins[a].at[pid] if scatter else ins[a], dst_ref=outs[a].at[me],
                send_sem=send_sems.at[a * n_peer + k - 1], recv_sem=recv_sems.at[a * n_peer + k - 1],
                device_id=dev, device_id_type=pl.DeviceIdType.MESH)

        def arrival(a, k):
            dev, pid = peer(k)
            return pltpu.make_async_remote_copy(
                src_ref=ins[a].at[pid] if scatter else ins[a], dst_ref=outs[a].at[pid],
                send_sem=send_sems.at[a * n_peer + k - 1], recv_sem=recv_sems.at[a * n_peer + k - 1],
                device_id=dev, device_id_type=pl.DeviceIdType.MESH)

        sends = [copy(a, k) for k in range(1, N_DEV) for a in range(n)]
        for cp in sends:
            cp.start()
        for k in range(1, N_DEV):
            for a in range(n):
                arrival(a, k).wait_recv()
        for cp in sends:
            cp.wait_send()
        for cp in local:
            cp.wait()

    def out_shape(a):
        blk = a.shape[1:] if scatter else a.shape
        return jax.ShapeDtypeStruct((N_DEV,) + tuple(blk), a.dtype)

    any_spec = pl.BlockSpec(memory_space=pl.ANY)
    return pl.pallas_call(
        body, in_specs=[any_spec] * n, out_specs=[any_spec] * n, out_shape=[out_shape(a) for a in arrays],
        scratch_shapes=[pltpu.SemaphoreType.DMA((n * n_peer,)), pltpu.SemaphoreType.DMA((n * n_peer,)),
                        pltpu.SemaphoreType.DMA((n,))],
        name=name, compiler_params=pltpu.CompilerParams(has_side_effects=True),
    )(*arrays)


N_CHIP = N_DEV // 2
_MESH_ID = pl.DeviceIdType.MESH


def _place():
    x, y, c = lax.axis_index("x"), lax.axis_index("y"), lax.axis_index("c")
    chips = [(1 - x, y), (x, 1 - y), (1 - x, 1 - y)]
    return x, y, c, 2 * x + y, chips


def _gather_blocks(arrays, *, name):
    n = len(arrays)

    def body(*refs):
        ins, outs = refs[:n], refs[n:2 * n]
        send_sems, recv_sems, loc_sems = refs[2 * n:]
        x, y, c, _, chips = _place()
        me = 4 * x + 2 * y + c
        sibling = (x, y, 1 - c)

        def slot(px, py, pc):
            return 4 * px + 2 * py + pc

        def copy(a, k, block, to, src=None):
            dst = outs[a].at[slot(*block)]
            return pltpu.make_async_remote_copy(
                src_ref=dst if src is None else src, dst_ref=dst, send_sem=send_sems.at[7 * a + k],
                recv_sem=recv_sems.at[7 * a + k], device_id=to, device_id_type=_MESH_ID)

        local = [pltpu.make_async_copy(ins[a], outs[a].at[me], loc_sems.at[a]) for a in range(n)]
        for cp in local:
            cp.start()
        first = []
        for a in range(n):
            first.append(copy(a, 0, (x, y, c), sibling, src=ins[a]))
            first += [copy(a, 1 + j, (x, y, c), (*chip, c), src=ins[a]) for j, chip in enumerate(chips)]
        for cp in first:
            cp.start()
        passed = []
        for a in range(n):
            for j, chip in enumerate(chips):
                copy(a, 1 + j, (*chip, c), (x, y, c)).wait_recv()
                fwd = copy(a, 4 + j, (*chip, c), sibling)
                fwd.start()
                passed.append(fwd)
        for a in range(n):
            copy(a, 0, sibling, (x, y, c)).wait_recv()
            for j, chip in enumerate(chips):
                copy(a, 4 + j, (*chip, 1 - c), (x, y, c)).wait_recv()
        for cp in first + passed:
            cp.wait_send()
        for cp in local:
            cp.wait()

    any_spec = pl.BlockSpec(memory_space=pl.ANY)
    return pl.pallas_call(
        body, in_specs=[any_spec] * n, out_specs=[any_spec] * n,
        out_shape=[jax.ShapeDtypeStruct((N_DEV,) + a.shape, a.dtype) for a in arrays],
        scratch_shapes=[pltpu.SemaphoreType.DMA((7 * n,)), pltpu.SemaphoreType.DMA((7 * n,)), pltpu.SemaphoreType.DMA((n,))],
        name=name, compiler_params=pltpu.CompilerParams(has_side_effects=True),
    )(*arrays)


def _sibling_swap(arrays, *, name):
    n = len(arrays)

    def body(*refs):
        ins, outs = refs[:n], refs[n:2 * n]
        send_sems, recv_sems = refs[2 * n:]
        x, y, c, _, _ = _place()
        copies = [pltpu.make_async_remote_copy(
            src_ref=ins[a].at[:, 1 - c], dst_ref=outs[a], send_sem=send_sems.at[a], recv_sem=recv_sems.at[a],
            device_id=(x, y, 1 - c), device_id_type=_MESH_ID) for a in range(n)]
        for cp in copies:
            cp.start()
        for cp in copies:
            cp.wait()

    any_spec = pl.BlockSpec(memory_space=pl.ANY)
    return pl.pallas_call(
        body, in_specs=[any_spec] * n, out_specs=[any_spec] * n,
        out_shape=[jax.ShapeDtypeStruct((N_CHIP,) + a.shape[2:], a.dtype) for a in arrays],
        scratch_shapes=[pltpu.SemaphoreType.DMA((n,)), pltpu.SemaphoreType.DMA((n,))],
        name=name, compiler_params=pltpu.CompilerParams(has_side_effects=True),
    )(*arrays)


def _pair_add(mine, theirs, core, *, name):
    _, _, R, C = mine.shape
    tr = max(t for t in range(16, R + 1, 16) if R % t == 0 and t * C <= 512 * 1024)

    def body(core_ref, m_ref, t_ref, o_ref):
        del core_ref
        o_ref[...] = (m_ref[...].astype(_F32) + t_ref[...].astype(_F32)).astype(o_ref.dtype)

    grid_spec = pltpu.PrefetchScalarGridSpec(
        num_scalar_prefetch=1, grid=(N_CHIP, R // tr),
        in_specs=[pl.BlockSpec((None, None, tr, C), lambda q, i, core: (q, core[0], i, 0)),
                  pl.BlockSpec((None, tr, C), lambda q, i, core: (q, i, 0))],
        out_specs=pl.BlockSpec((None, tr, C), lambda q, i, core: (q, i, 0)))
    return pl.pallas_call(
        body, grid_spec=grid_spec, out_shape=jax.ShapeDtypeStruct((N_CHIP, R, C), mine.dtype), name=name,
        compiler_params=_cparams(dimension_semantics=("parallel", "parallel")),
    )(core.reshape(1), mine, theirs)


def _chip_exchange(arrays, *, name):
    n = len(arrays)

    def body(*refs):
        ins, outs = refs[:n], refs[n:2 * n]
        send_sems, recv_sems, loc_sems = refs[2 * n:]
        x, y, c, p, chips = _place()
        local = [pltpu.make_async_copy(ins[a].at[p], outs[a].at[p], loc_sems.at[a]) for a in range(n)]
        for cp in local:
            cp.start()

        def copy(a, j, landing):
            qx, qy = chips[j]
            q = 2 * qx + qy
            return pltpu.make_async_remote_copy(
                src_ref=ins[a].at[q], dst_ref=outs[a].at[q if landing else p], send_sem=send_sems.at[3 * a + j],
                recv_sem=recv_sems.at[3 * a + j], device_id=(qx, qy, c), device_id_type=_MESH_ID)

        sends = [copy(a, j, False) for a in range(n) for j in range(3)]
        for cp in sends:
            cp.start()
        for a in range(n):
            for j in range(3):
                copy(a, j, True).wait_recv()
        for cp in sends:
            cp.wait_send()
        for cp in local:
            cp.wait()

    any_spec = pl.BlockSpec(memory_space=pl.ANY)
    return pl.pallas_call(
        body, in_specs=[any_spec] * n, out_specs=[any_spec] * n,
        out_shape=[jax.ShapeDtypeStruct(a.shape, a.dtype) for a in arrays],
        scratch_shapes=[pltpu.SemaphoreType.DMA((3 * n,)), pltpu.SemaphoreType.DMA((3 * n,)), pltpu.SemaphoreType.DMA((n,))],
        name=name, compiler_params=pltpu.CompilerParams(has_side_effects=True),
    )(*arrays)


_C1 = 1.0 - ADAM_B1 ** ADAM_STEP
_C2 = 1.0 - ADAM_B2 ** ADAM_STEP


def _adamw_math(g, w, m, v):
    m = ADAM_B1 * m + (1.0 - ADAM_B1) * g
    v = ADAM_B2 * v + (1.0 - ADAM_B2) * (g * g)
    delta = -ADAM_LR * ((m / _C1) / (jnp.sqrt(v / _C2) + ADAM_EPS) + ADAM_WD * w)
    return delta, m, v


def _adamw_reduce(parts, w, m, v, *, name):
    R, C = w.shape
    n_parts = parts.shape[0]
    tr = max(t for t in range(16, R + 1, 16) if R % t == 0 and t * C <= 256 * 1024)

    def body(p_ref, w_ref, m_ref, v_ref, g_ref, d_ref, nm_ref, nv_ref):
        g = p_ref[0].astype(_F32)
        for s in range(1, n_parts):
            g = g + p_ref[s].astype(_F32)
        d, nm, nv = _adamw_math(g, w_ref[...], m_ref[...], v_ref[...])
        g_ref[...] = g
        d_ref[...] = d
        nm_ref[...] = nm
        nv_ref[...] = nv

    blk = pl.BlockSpec((tr, C), lambda i: (i, 0))
    shape = jax.ShapeDtypeStruct((R, C), _F32)
    return pl.pallas_call(
        body, grid=(R // tr,), in_specs=[pl.BlockSpec((n_parts, tr, C), lambda i: (0, i, 0)), blk, blk, blk],
        out_specs=[blk] * 4, out_shape=[shape] * 4, name=name,
        compiler_params=_cparams(dimension_semantics=("parallel",)),
    )(parts, w, m, v)


_PACK_LANES = 128
_LAYER_ROWS = 248
_LB_ROWS = (A_HEADS * HEAD_DIM) // _PACK_LANES


def _small_reduce(parts, lb_logits, *, name):
    rows = DEPTH * _LAYER_ROWS

    def body(p_ref, lg_ref, o_ref):
        g = p_ref[0]
        for s in range(1, N_DEV):
            g = g + p_ref[s]
        o_ref[...] = g
        lg = lg_ref[...]
        e = jnp.exp(lg - jnp.max(lg, axis=0, keepdims=True))
        p = e / jnp.sum(e, axis=0, keepdims=True)
        d1 = g[_LAYER_ROWS:_LAYER_ROWS + _LB_ROWS, :] * p[0] * p[1]
        o_ref[0:_LB_ROWS, :] = -d1
        o_ref[_LAYER_ROWS:_LAYER_ROWS + _LB_ROWS, :] = d1

    return pl.pallas_call(
        body, out_shape=jax.ShapeDtypeStruct((rows, _PACK_LANES), _F32), name=name,
        compiler_params=_cparams(),
    )(parts, lb_logits.reshape(DEPTH, _LB_ROWS, _PACK_LANES))


def _adamw_small(g, w, m, v, *, name):
    def body(g_ref, w_ref, m_ref, v_ref, d_ref, nm_ref, nv_ref):
        d, nm, nv = _adamw_math(g_ref[...], w_ref[...], m_ref[...], v_ref[...])
        d_ref[...] = d
        nm_ref[...] = nm
        nv_ref[...] = nv

    shape = jax.ShapeDtypeStruct(g.shape, _F32)
    return pl.pallas_call(body, out_shape=[shape] * 3, name=name, compiler_params=_cparams())(g, w, m, v)


def _pack(vectors, rows):
    flat = jnp.concatenate([v.reshape(-1).astype(_F32) for v in vectors])
    return jnp.pad(flat, (0, rows * _PACK_LANES - flat.shape[0])).reshape(rows, _PACK_LANES)


def _unpack(packed, shapes):
    flat = packed.reshape(-1)
    out, at = [], 0
    for s in shapes:
        size = int(np.prod(s))
        out.append(flat[at:at + size].reshape(s))
        at += size
    return out


def _layer_fwd(x, xb, w, lb_logits, a_norm_w, c_sink, ln1_g, ln1_b, conv_w, conv_b, ln2_g, ln2_b, tabs, l):
    proj = _mm(xb, w["in"], tm=1024, tn=512, name=f"proj_{l}")
    qkv = _qkv_prep(proj, tabs, name=f"qkv_prep_{l}")
    o_a, raw, states = _hgrn_fwd(proj, lb_logits, a_norm_w, l, name=f"hgrn_fwd_{l}")
    os_, lses = [], []
    for d in DILATIONS:
        o, lse = _attn_fwd(qkv, d, heads=B_HEADS, q0=QB0, k0=KB0, v0=VB0, kv_group=1, lag_off=0, sink=None,
                           name=f"dilated_fwd_{d}_{l}")
        os_.append(o)
        lses.append(lse)
    o_b, lse_b = _dilated_mix(os_, lses, name=f"dilated_mix_{l}")
    o_c, lse_c = _attn_fwd(qkv, 1, heads=C_HEADS, q0=QC0, k0=KC0, v0=VC0, kv_group=C_HEADS // C_KV_HEADS, lag_off=1,
                           sink=c_sink, name=f"swa_fwd_{l}")
    mixed = jnp.concatenate([o_a, o_b, o_c], axis=1).astype(_MXU_DTYPE)
    y = _mm(mixed, w["out"], tm=1024, tn=512, name=f"mix_out_{l}")
    z1, x1, x1b = _ln_fwd(x, y, ln1_g, ln1_b, name=f"ln1_fwd_{l}")
    g = _mm(x1b, w["gate"], tm=1024, tn=512, name=f"ffn_gate_{l}")
    u = _mm(x1b, w["up"], tm=1024, tn=512, name=f"ffn_up_{l}")
    hb = _conv_gate_fwd(g, u, conv_w, conv_b, name=f"conv_gate_fwd_{l}")
    y2 = _mm(hb, w["down"], tm=512, tn=512, name=f"ffn_down_{l}")
    z2, x2, x2b = _ln_fwd(x1, y2, ln2_g, ln2_b, name=f"ln2_fwd_{l}")
    res = dict(xb=xb, proj=proj, qkv=qkv, raw=raw, states=states, o_b=o_b, lse_b=lse_b, o_c=o_c, lse_c=lse_c,
               mixed=mixed, z1=z1, x1b=x1b, g=g, u=u, hb=hb, z2=z2)
    return x2, x2b, res


def _layer_bwd(dx2, res, w, lb_logits, a_norm_w, c_sink, ln1_g, conv_w, conv_b, ln2_g, tabs, l):
    dz2, dz2b, d_ln2_g, d_ln2_b = _ln_bwd(res["z2"], dx2, None, ln2_g, name=f"ln2_bwd_{l}")
    dh = _mm(dz2b, w["down"], tb=True, tm=1024, tn=512, name=f"ffn_down_dx_{l}")
    d_w_down = _mm(res["hb"], dz2b, ta=True, tm=512, tn=512, out_dtype=_GRAD_DTYPE, name=f"ffn_down_dw_{l}")
    dg, du, d_conv_w, d_conv_b = _conv_gate_bwd(dh, res["g"], res["u"], conv_w, conv_b, name=f"conv_gate_bwd_{l}")
    t = _mm(dg, w["gate"], tb=True, tm=512, tn=512, name=f"ffn_gate_dx_{l}")
    dx1 = _mm(du, w["up"], tb=True, tm=512, tn=512, add=t, name=f"ffn_up_dx_{l}")
    d_w_gate = _mm_tn_slabs(res["x1b"], _to_slabs(dg), tm=1024, name=f"ffn_gate_dw_{l}")
    d_w_up = _mm_tn_slabs(res["x1b"], _to_slabs(du), tm=1024, name=f"ffn_up_dw_{l}")
    dz1, dz1b, d_ln1_g, d_ln1_b = _ln_bwd(res["z1"], dx1, dz2, ln1_g, name=f"ln1_bwd_{l}")
    dmixed = _mm(dz1b, w["out"], tb=True, tm=1024, tn=512, name=f"mix_out_dx_{l}")
    d_w_out = _mm(res["mixed"], dz1b, ta=True, tm=1024, tn=512, out_dtype=_GRAD_DTYPE, name=f"mix_out_dw_{l}")
    dq_a, df_a, di_a, dg_a, d_norm_w, d_lb = _hgrn_bwd(res["proj"], lb_logits, a_norm_w, res["raw"], res["states"],
                                                      dmixed, l, name=f"hgrn_bwd_{l}")
    grads = []
    for d in DILATIONS:
        grads += list(_attn_bwd(res["qkv"], dmixed, res["o_b"], res["lse_b"], d, heads=B_HEADS, q0=QB0, k0=KB0, v0=VB0,
                                kv_group=1, lag_off=0, do0=A_HEADS, sink=None, name=f"dilated_bwd_{d}_{l}"))
    dq_c, dk_c, dv_c, d_sink = _attn_bwd(res["qkv"], dmixed, res["o_c"], res["lse_c"], 1, heads=C_HEADS, q0=QC0, k0=KC0,
                                         v0=VC0, kv_group=C_HEADS // C_KV_HEADS, lag_off=1, do0=A_HEADS + B_HEADS,
                                         sink=c_sink, name=f"swa_bwd_{l}")
    dqkv = _qkv_grad_finish(*grads, dq_c, dk_c, dv_c, tabs, name=f"qkv_grad_{l}")
    dproj = jnp.concatenate([dq_a, df_a, di_a, dg_a, dqkv], axis=1)
    dx = _mm(dproj, w["in"], tb=True, tm=512, tn=512, add=dz1, add_scale=ALPHA, name=f"proj_dx_{l}")
    d_w_in = _mm_tn_slabs(res["xb"], _to_slabs(dproj), tm=1024, name=f"proj_dw_{l}")
    big = dict(w_in=d_w_in, w_gate=d_w_gate, w_up=d_w_up,
               w_out=d_w_out.reshape(N_DEV, D_MODEL // N_DEV, D_MODEL), w_down=d_w_down.reshape(N_DEV, D_FF // N_DEV, D_MODEL))
    small = [d_lb, d_norm_w, jnp.pad(d_sink, (0, _PACK_LANES - C_HEADS)), d_ln1_g, d_ln1_b, d_ln2_g, d_ln2_b, d_conv_b,
             d_conv_w]
    return dx, big, small


_BIG = ("w_in", "w_gate", "w_up", "w_out", "w_down")


def kernel(x, w_in, lb_logits, a_norm_w, c_sinks, w_out, ln1_g, ln1_b, w_gate, w_up, conv_w, conv_b, w_down, ln2_g, ln2_b, loss_target, m_w_in, m_lb_logits, m_a_norm_w, m_c_sinks, m_w_out, m_ln1_g, m_ln1_b, m_w_gate, m_w_up, m_conv_w, m_conv_b, m_w_down, m_ln2_g, m_ln2_b, v_w_in, v_lb_logits, v_a_norm_w, v_c_sinks, v_w_out, v_ln1_g, v_ln1_b, v_w_gate, v_w_up, v_conv_w, v_conv_b, v_w_down, v_ln2_g, v_ln2_b):
    weights = dict(w_in=w_in, lb_logits=lb_logits, a_norm_w=a_norm_w, c_sinks=c_sinks, w_out=w_out, ln1_g=ln1_g, ln1_b=ln1_b,
                   w_gate=w_gate, w_up=w_up, conv_w=conv_w, conv_b=conv_b, w_down=w_down, ln2_g=ln2_g, ln2_b=ln2_b)
    mom1 = dict(w_in=m_w_in, lb_logits=m_lb_logits, a_norm_w=m_a_norm_w, c_sinks=m_c_sinks, w_out=m_w_out, ln1_g=m_ln1_g,
                ln1_b=m_ln1_b, w_gate=m_w_gate, w_up=m_w_up, conv_w=m_conv_w, conv_b=m_conv_b, w_down=m_w_down, ln2_g=m_ln2_g,
                ln2_b=m_ln2_b)
    mom2 = dict(w_in=v_w_in, lb_logits=v_lb_logits, a_norm_w=v_a_norm_w, c_sinks=v_c_sinks, w_out=v_w_out, ln1_g=v_ln1_g,
                ln1_b=v_ln1_b, w_gate=v_w_gate, w_up=v_w_up, conv_w=v_conv_w, conv_b=v_conv_b, w_down=v_w_down, ln2_g=v_ln2_g,
                ln2_b=v_ln2_b)
    core = lax.axis_index("c").astype(jnp.int32)
    me = 4 * lax.axis_index("x") + 2 * lax.axis_index("y") + core
    tabs = _rope_tables()

    full = []
    for l in range(DEPTH):
        g_in, g_gate, g_up, g_out, g_down, g_conv = _gather_blocks(
            [weights[n][l].astype(_MXU_DTYPE) for n in _BIG] + [conv_w[l]], name=f"gather_weights_{l}")
        full.append(dict({"in": _from_slabs(g_in), "gate": _from_slabs(g_gate), "up": _from_slabs(g_up),
                          "out": g_out.reshape(D_MODEL, D_MODEL), "down": g_down.reshape(D_FF, D_MODEL)},
                         conv=_from_slabs(g_conv)))

    xs = x[0]
    xb = xs.astype(_MXU_DTYPE)
    saved = []
    for l in range(DEPTH):
        xs, xb, res = _layer_fwd(xs, xb, full[l], lb_logits, a_norm_w[l], c_sinks[l], ln1_g[l], ln1_b[l], full[l]["conv"],
                                 conv_b[l], ln2_g[l], ln2_b[l], tabs, l)
        saved.append(res)
    loss_part, dx = _loss_head(xs, loss_target[0], name="loss_head")
    loss = lax.psum(loss_part, ("x", "y", "c"))

    big_out = {n: [None] * DEPTH for n in _BIG}
    small_parts = [None] * DEPTH
    for l in reversed(range(DEPTH)):
        dx, big, small = _layer_bwd(dx, saved[l], full[l], lb_logits, a_norm_w[l], c_sinks[l], ln1_g[l], full[l]["conv"],
                                    conv_b[l], ln2_g[l], tabs, l)
        mine = [big[n].reshape((N_CHIP, 2) + big[n].shape[1:]) for n in _BIG]
        theirs = _sibling_swap(mine, name=f"swap_grads_{l}")
        sums = [_pair_add(a, b, core, name=f"pair_add_{n}_{l}") for n, a, b in zip(_BIG, mine, theirs)]
        parts = _chip_exchange(sums, name=f"exchange_grads_{l}")
        for n, p in zip(_BIG, parts):
            big_out[n][l] = _adamw_reduce(p, weights[n][l], mom1[n][l], mom2[n][l], name=f"adamw_{n}_{l}")
        small_parts[l] = _pack(small, _LAYER_ROWS)
    gathered, = _exchange([jnp.concatenate(small_parts, axis=0)], False, name="gather_small_grads")
    g_small = _small_reduce(gathered, lb_logits, name="small_grads")

    per_layer = [(A_HEADS * HEAD_DIM,), (HEAD_DIM,), (_PACK_LANES,), (D_MODEL,), (D_MODEL,), (D_MODEL,), (D_MODEL,), (D_FF,),
                 (3, D_FF)]
    names = ("lb_logits", "a_norm_w", "c_sinks", "ln1_g", "ln1_b", "ln2_g", "ln2_b", "conv_b", "conv_w")
    grads = {n: [] for n in names}
    for l in range(DEPTH):
        for n, t in zip(names, _unpack(g_small[l * _LAYER_ROWS:(l + 1) * _LAYER_ROWS], per_layer)):
            grads[n].append(t)
    grads = {n: jnp.stack(t) for n, t in grads.items()}
    grads["c_sinks"] = grads["c_sinks"][:, :C_HEADS]
    grads["conv_w"] = lax.dynamic_slice_in_dim(grads["conv_w"], me * SHARD_COLS, SHARD_COLS, axis=2)
    shapes = [grads[n].shape for n in names]
    rows = -(-sum(int(np.prod(s)) for s in shapes) // (8 * _PACK_LANES)) * 8
    d_s, m_s, v_s = _adamw_small(_pack([grads[n] for n in names], rows), _pack([weights[n] for n in names], rows),
                                 _pack([mom1[n] for n in names], rows), _pack([mom2[n] for n in names], rows),
                                 name="adamw_small")
    delta = dict(zip(names, _unpack(d_s, shapes)))
    new_m = dict(zip(names, _unpack(m_s, shapes)))
    new_v = dict(zip(names, _unpack(v_s, shapes)))
    for n in _BIG:
        g, d, nm, nv = (jnp.stack(t) for t in zip(*big_out[n]))
        grads[n], delta[n], new_m[n], new_v[n] = g, d, nm, nv

    order = ("w_in", "lb_logits", "a_norm_w", "c_sinks", "w_out", "ln1_g", "ln1_b", "w_gate", "w_up", "conv_w", "conv_b",
             "w_down", "ln2_g", "ln2_b")
    return (loss, dx[None], *[grads[n] for n in order], *[delta[n] for n in order], *[new_m[n] for n in order],
            *[new_v[n] for n in order])
```

```python
import functools

import jax
import jax.numpy as jnp
import numpy as np
from jax import lax
from jax.experimental import pallas as pl
from jax.experimental.pallas import tpu as pltpu

D_MODEL = 2048
SEQ = 2048
DEPTH = 2
HEAD_DIM = 128
A_HEADS = 4
B_HEADS = 6
C_HEADS = 6
C_KV_HEADS = 2
A_CHUNK = 16
DILATIONS = (1, 4, 16)
BLOCK = 128
ROPE_THETA = 500000.0
ROPE_DIM = 32
D_FF = 5632
IN_WIDTH = 5632
LN_EPS = 1e-5
ALPHA = (2 * DEPTH) ** 0.25
N_DEV = 8
SHARD_COLS = IN_WIDTH // N_DEV

ADAM_LR = 0.001
ADAM_B1 = 0.9
ADAM_B2 = 0.999
ADAM_EPS = 1e-08
ADAM_WD = 0.01
ADAM_STEP = 10

A_COLS = 16
QKV_COLS = 28
QB0, KB0, VB0, QC0, KC0, VC0 = 0, 6, 12, 18, 24, 26

_MXU_DTYPE = jnp.bfloat16
_GRAD_DTYPE = jnp.bfloat16
_NEG = -1e30
_VMEM_LIMIT = 56 * 2 ** 20

_F32 = jnp.float32


def _sigmoid(x):
    return 1.0 / (1.0 + jnp.exp(-x))


def _cparams(**kw):
    return pltpu.CompilerParams(vmem_limit_bytes=_VMEM_LIMIT, **kw)


def _mm(a, b, *, ta=False, tb=False, tm, tn, out_dtype=_F32, add=None, add_scale=1.0, after=(), name):
    K = a.shape[0] if ta else a.shape[1]
    M = a.shape[1] if ta else a.shape[0]
    N = b.shape[0] if tb else b.shape[1]
    assert (b.shape[1] if tb else b.shape[0]) == K and M % tm == 0 and N % tn == 0
    dn = (((0 if ta else 1,), (1 if tb else 0,)), ((), ()))

    def body(*refs):
        a_ref, b_ref = refs[:2]
        o_ref = refs[-1]
        r = lax.dot_general(a_ref[...], b_ref[...], dn, preferred_element_type=_F32)
        if add is not None:
            r = r + add_scale * refs[2][...]
        o_ref[...] = r.astype(o_ref.dtype)

    a_spec = pl.BlockSpec((K, tm), lambda i, j: (0, i)) if ta else pl.BlockSpec((tm, K), lambda i, j: (i, 0))
    b_spec = pl.BlockSpec((tn, K), lambda i, j: (j, 0)) if tb else pl.BlockSpec((K, tn), lambda i, j: (0, j))
    o_spec = pl.BlockSpec((tm, tn), lambda i, j: (i, j))
    in_specs = [a_spec, b_spec] + ([o_spec] if add is not None else []) + [pl.BlockSpec(memory_space=pl.ANY)] * len(after)
    args = (a, b) + ((add,) if add is not None else ()) + tuple(after)
    return pl.pallas_call(
        body, grid=(M // tm, N // tn), in_specs=in_specs, out_specs=o_spec,
        out_shape=jax.ShapeDtypeStruct((M, N), out_dtype), name=name,
        compiler_params=_cparams(dimension_semantics=("parallel", "parallel")),
    )(*args)


def _mm_tn_slabs(a, b, *, tm, name):
    K, M = a.shape
    assert b.shape == (N_DEV, K, SHARD_COLS) and M % tm == 0

    def body(a_ref, b_ref, o_ref):
        o_ref[...] = lax.dot_general(a_ref[...], b_ref[...], _TN, preferred_element_type=_F32).astype(o_ref.dtype)

    return pl.pallas_call(
        body, grid=(M // tm, N_DEV),
        in_specs=[pl.BlockSpec((K, tm), lambda i, j: (0, i)), pl.BlockSpec((None, K, SHARD_COLS), lambda i, j: (j, 0, 0))],
        out_specs=pl.BlockSpec((None, tm, SHARD_COLS), lambda i, j: (j, i, 0)),
        out_shape=jax.ShapeDtypeStruct((N_DEV, M, SHARD_COLS), _GRAD_DTYPE), name=name,
        compiler_params=_cparams(dimension_semantics=("parallel", "parallel")),
    )(a, b)


def _to_slabs(m):
    return m.reshape(m.shape[0], N_DEV, SHARD_COLS).transpose(1, 0, 2)


def _from_slabs(s):
    return s.transpose(1, 0, 2).reshape(s.shape[1], N_DEV * SHARD_COLS)


def _ln_fwd(x, y, g, b, *, name):
    tm = 256

    def body(x_ref, y_ref, g_ref, b_ref, z_ref, o_ref, ob_ref):
        z = ALPHA * x_ref[...] + y_ref[...]
        mu = jnp.mean(z, axis=-1, keepdims=True)
        zc = z - mu
        var = jnp.mean(zc * zc, axis=-1, keepdims=True)
        o = zc * lax.rsqrt(var + LN_EPS) * g_ref[...] + b_ref[...]
        z_ref[...] = z
        o_ref[...] = o
        ob_ref[...] = o.astype(ob_ref.dtype)

    row = pl.BlockSpec((tm, D_MODEL), lambda i: (i, 0))
    vec = pl.BlockSpec((1, D_MODEL), lambda i: (0, 0))
    return pl.pallas_call(
        body, grid=(SEQ // tm,), in_specs=[row, row, vec, vec], out_specs=[row, row, row],
        out_shape=[jax.ShapeDtypeStruct((SEQ, D_MODEL), _F32), jax.ShapeDtypeStruct((SEQ, D_MODEL), _F32),
                   jax.ShapeDtypeStruct((SEQ, D_MODEL), _MXU_DTYPE)],
        name=name, compiler_params=_cparams(dimension_semantics=("parallel",)),
    )(x, y, g.reshape(1, D_MODEL), b.reshape(1, D_MODEL))


def _ln_bwd(z, d_a, d_res, g, *, name):
    tm = 256

    def body(*refs):
        if d_res is None:
            z_ref, da_ref, g_ref, dz_ref, dzb_ref, dg_ref, db_ref = refs
            dout = da_ref[...]
        else:
            z_ref, da_ref, dr_ref, g_ref, dz_ref, dzb_ref, dg_ref, db_ref = refs
            dout = da_ref[...] + ALPHA * dr_ref[...]
        z = z_ref[...]
        mu = jnp.mean(z, axis=-1, keepdims=True)
        zc = z - mu
        var = jnp.mean(zc * zc, axis=-1, keepdims=True)
        rstd = lax.rsqrt(var + LN_EPS)
        xh = zc * rstd
        dxh = dout * g_ref[...]
        m1 = jnp.mean(dxh, axis=-1, keepdims=True)
        m2 = jnp.mean(dxh * xh, axis=-1, keepdims=True)
        dz = rstd * (dxh - m1 - xh * m2)
        dz_ref[...] = dz
        dzb_ref[...] = dz.astype(dzb_ref.dtype)

        @pl.when(pl.program_id(0) == 0)
        def _():
            dg_ref[...] = jnp.zeros_like(dg_ref)
            db_ref[...] = jnp.zeros_like(db_ref)

        dg_ref[0:1, :] += jnp.sum(dout * xh, axis=0, keepdims=True)
        db_ref[0:1, :] += jnp.sum(dout, axis=0, keepdims=True)

    row = pl.BlockSpec((tm, D_MODEL), lambda i: (i, 0))
    vec = pl.BlockSpec((1, D_MODEL), lambda i: (0, 0))
    acc = pl.BlockSpec((8, D_MODEL), lambda i: (0, 0))
    ins = [z, d_a] + ([d_res] if d_res is not None else []) + [g.reshape(1, D_MODEL)]
    in_specs = [row, row] + ([row] if d_res is not None else []) + [vec]
    dz, dzb, dg, db = pl.pallas_call(
        body, grid=(SEQ // tm,), in_specs=in_specs, out_specs=[row, row, acc, acc],
        out_shape=[jax.ShapeDtypeStruct((SEQ, D_MODEL), _F32), jax.ShapeDtypeStruct((SEQ, D_MODEL), _MXU_DTYPE),
                   jax.ShapeDtypeStruct((8, D_MODEL), _F32), jax.ShapeDtypeStruct((8, D_MODEL), _F32)],
        name=name, compiler_params=_cparams(dimension_semantics=("arbitrary",)),
    )(*ins)
    return dz, dzb, dg[0], db[0]


def _loss_head(y, target, *, name):
    tm = 256

    def body(y_ref, t_ref, d_ref, l_ref):
        e = y_ref[...] - t_ref[...]
        d_ref[...] = e * (1.0 / D_MODEL)

        @pl.when(pl.program_id(0) == 0)
        def _():
            l_ref[...] = jnp.zeros_like(l_ref)

        l_ref[...] += (0.5 / D_MODEL) * jnp.sum(e * e)

    row = pl.BlockSpec((tm, D_MODEL), lambda i: (i, 0))
    d, l = pl.pallas_call(
        body, grid=(SEQ // tm,), in_specs=[row, row], out_specs=[row, pl.BlockSpec((8, 128), lambda i: (0, 0))],
        out_shape=[jax.ShapeDtypeStruct((SEQ, D_MODEL), _F32), jax.ShapeDtypeStruct((8, 128), _F32)],
        name=name, compiler_params=_cparams(dimension_semantics=("arbitrary",)),
    )(y, target)
    return l[0, 0], d


_CONV_TN = 256


def _shift_down(v, k, rows):
    return jnp.where(rows >= k, pltpu.roll(v, k, axis=0), 0.0)


def _shift_up(v, k, rows):
    return jnp.where(rows < SEQ - k, pltpu.roll(v, SEQ - k, axis=0), 0.0)


def _conv_gate_fwd(g, u, conv_w, conv_b, *, name):
    def body(g_ref, u_ref, w_ref, b_ref, h_ref):
        gv = g_ref[...]
        rows = lax.broadcasted_iota(jnp.int32, gv.shape, 0)
        w = w_ref[...]
        gc = b_ref[...] + w[2:3, :] * gv + w[1:2, :] * _shift_down(gv, 1, rows) + w[0:1, :] * _shift_down(gv, 2, rows)
        h_ref[...] = (gc * _sigmoid(gc) * u_ref[...]).astype(h_ref.dtype)

    col = pl.BlockSpec((SEQ, _CONV_TN), lambda j: (0, j))
    return pl.pallas_call(
        body, grid=(D_FF // _CONV_TN,),
        in_specs=[col, col, pl.BlockSpec((3, _CONV_TN), lambda j: (0, j)), pl.BlockSpec((1, _CONV_TN), lambda j: (0, j))],
        out_specs=col, out_shape=jax.ShapeDtypeStruct((SEQ, D_FF), _MXU_DTYPE), name=name,
        compiler_params=_cparams(dimension_semantics=("parallel",)),
    )(g, u, conv_w, conv_b.reshape(1, D_FF))


def _conv_gate_bwd(dh, g, u, conv_w, conv_b, *, name):
    def body(dh_ref, g_ref, u_ref, w_ref, b_ref, dg_ref, du_ref, dw_ref, db_ref):
        gv = g_ref[...]
        rows = lax.broadcasted_iota(jnp.int32, gv.shape, 0)
        w = w_ref[...]
        g1 = _shift_down(gv, 1, rows)
        g2 = _shift_down(gv, 2, rows)
        gc = b_ref[...] + w[2:3, :] * gv + w[1:2, :] * g1 + w[0:1, :] * g2
        sg = _sigmoid(gc)
        dh = dh_ref[...]
        du_ref[...] = (dh * (gc * sg)).astype(du_ref.dtype)
        dgc = dh * u_ref[...] * (sg * (1.0 + gc * (1.0 - sg)))
        dg = w[2:3, :] * dgc + w[1:2, :] * _shift_up(dgc, 1, rows) + w[0:1, :] * _shift_up(dgc, 2, rows)
        dg_ref[...] = dg.astype(dg_ref.dtype)
        dw_ref[0:1, :] = jnp.sum(dgc * g2, axis=0, keepdims=True)
        dw_ref[1:2, :] = jnp.sum(dgc * g1, axis=0, keepdims=True)
        dw_ref[2:3, :] = jnp.sum(dgc * gv, axis=0, keepdims=True)
        db_ref[...] = jnp.sum(dgc, axis=0, keepdims=True)

    col = pl.BlockSpec((SEQ, _CONV_TN), lambda j: (0, j))
    w3 = pl.BlockSpec((3, _CONV_TN), lambda j: (0, j))
    w1 = pl.BlockSpec((1, _CONV_TN), lambda j: (0, j))
    dg, du, dw, db = pl.pallas_call(
        body, grid=(D_FF // _CONV_TN,), in_specs=[col, col, col, w3, w1], out_specs=[col, col, w3, w1],
        out_shape=[jax.ShapeDtypeStruct((SEQ, D_FF), _MXU_DTYPE), jax.ShapeDtypeStruct((SEQ, D_FF), _MXU_DTYPE),
                   jax.ShapeDtypeStruct((3, D_FF), _F32), jax.ShapeDtypeStruct((1, D_FF), _F32)],
        name=name, compiler_params=_cparams(dimension_semantics=("parallel",)),
    )(dh, g, u, conv_w, conv_b.reshape(1, D_FF))
    return dg, du, dw, db[0]


def _rope_tables():
    half = ROPE_DIM // 2
    inv = ROPE_THETA ** (-jnp.arange(0, ROPE_DIM, 2, dtype=_F32) / ROPE_DIM)
    ang = jnp.arange(SEQ, dtype=_F32)[:, None] * inv[None, :]
    cos, sin = jnp.cos(ang), jnp.sin(ang)
    rest = HEAD_DIM - ROPE_DIM
    c = jnp.concatenate([cos, cos, jnp.ones((SEQ, rest), _F32)], axis=1)
    s1 = jnp.concatenate([-sin, jnp.zeros((SEQ, HEAD_DIM - half), _F32)], axis=1)
    s2 = jnp.concatenate([jnp.zeros((SEQ, half), _F32), sin, jnp.zeros((SEQ, rest), _F32)], axis=1)
    return c, s1, s2


def _rope_apply(x, c, s1, s2):
    return x * c + pltpu.roll(x, HEAD_DIM - ROPE_DIM // 2, axis=1) * s1 + pltpu.roll(x, ROPE_DIM // 2, axis=1) * s2


def _rope_transpose(d, c, s1, s2):
    half = ROPE_DIM // 2
    return d * c + pltpu.roll(d * s1, half, axis=1) + pltpu.roll(d * s2, HEAD_DIM - half, axis=1)


def _is_rope_block(j):
    return (j < VB0) | ((j >= QC0) & (j < VC0))


def _qkv_prep(proj, tabs, *, name):
    tm = 512

    def body(p_ref, c_ref, s1_ref, s2_ref, o_ref):
        j = pl.program_id(1)
        x = p_ref[...]
        r = _rope_apply(x, c_ref[...], s1_ref[...], s2_ref[...])
        o_ref[...] = jnp.where(_is_rope_block(j), r, x).astype(o_ref.dtype)

    tab = pl.BlockSpec((tm, HEAD_DIM), lambda i, j: (i, 0))
    return pl.pallas_call(
        body, grid=(SEQ // tm, QKV_COLS),
        in_specs=[pl.BlockSpec((tm, HEAD_DIM), lambda i, j: (i, A_COLS + j)), tab, tab, tab],
        out_specs=pl.BlockSpec((tm, HEAD_DIM), lambda i, j: (i, j)),
        out_shape=jax.ShapeDtypeStruct((SEQ, QKV_COLS * HEAD_DIM), _MXU_DTYPE), name=name,
        compiler_params=_cparams(dimension_semantics=("parallel", "parallel")),
    )(proj, *tabs)


def _qkv_grad_finish(dq1, dk1, dv1, dq4, dk4, dv4, dq16, dk16, dv16, dqc, dkc, dvc, tabs, *, name):
    tm = 512

    def body(a1, a4, a16, k1, k4, k16, v1, v4, v16, qc, kc, vc, c_ref, s1_ref, s2_ref, o_ref):
        c, s1, s2 = c_ref[...], s1_ref[...], s2_ref[...]
        for h in range(B_HEADS):
            sl = slice(h * HEAD_DIM, (h + 1) * HEAD_DIM)
            o_ref[:, (QB0 + h) * HEAD_DIM:(QB0 + h + 1) * HEAD_DIM] = _rope_transpose(
                a1[:, sl] + a4[:, sl] + a16[:, sl], c, s1, s2).astype(o_ref.dtype)
            o_ref[:, (KB0 + h) * HEAD_DIM:(KB0 + h + 1) * HEAD_DIM] = _rope_transpose(
                k1[:, sl] + k4[:, sl] + k16[:, sl], c, s1, s2).astype(o_ref.dtype)
            o_ref[:, (VB0 + h) * HEAD_DIM:(VB0 + h + 1) * HEAD_DIM] = (v1[:, sl] + v4[:, sl] + v16[:, sl]).astype(o_ref.dtype)
            o_ref[:, (QC0 + h) * HEAD_DIM:(QC0 + h + 1) * HEAD_DIM] = _rope_transpose(qc[:, sl], c, s1, s2).astype(o_ref.dtype)
        for h in range(C_KV_HEADS):
            sl = slice(h * HEAD_DIM, (h + 1) * HEAD_DIM)
            o_ref[:, (KC0 + h) * HEAD_DIM:(KC0 + h + 1) * HEAD_DIM] = _rope_transpose(kc[:, sl], c, s1, s2).astype(o_ref.dtype)
            o_ref[:, (VC0 + h) * HEAD_DIM:(VC0 + h + 1) * HEAD_DIM] = vc[:, sl].astype(o_ref.dtype)

    wb = pl.BlockSpec((tm, B_HEADS * HEAD_DIM), lambda i: (i, 0))
    wkv = pl.BlockSpec((tm, C_KV_HEADS * HEAD_DIM), lambda i: (i, 0))
    tab = pl.BlockSpec((tm, HEAD_DIM), lambda i: (i, 0))
    return pl.pallas_call(
        body, grid=(SEQ // tm,), in_specs=[wb] * 10 + [wkv, wkv, tab, tab, tab],
        out_specs=pl.BlockSpec((tm, QKV_COLS * HEAD_DIM), lambda i: (i, 0)),
        out_shape=jax.ShapeDtypeStruct((SEQ, QKV_COLS * HEAD_DIM), _MXU_DTYPE), name=name,
        compiler_params=_cparams(dimension_semantics=("parallel",)),
    )(dq1, dq4, dq16, dk1, dk4, dk16, dv1, dv4, dv16, dqc, dkc, dvc, *tabs)


_NT = (((1,), (1,)), ((), ()))
_TN = (((0,), (0,)), ((), ()))
_SCALE = HEAD_DIM ** -0.5


def _band_scores(q, kp, kc, n, lag_off):
    sp = lax.dot_general(q, kp, _NT, preferred_element_type=_F32) * _SCALE
    sc = lax.dot_general(q, kc, _NT, preferred_element_type=_F32) * _SCALE
    row = lax.broadcasted_iota(jnp.int32, (BLOCK, BLOCK), 0)
    col = lax.broadcasted_iota(jnp.int32, (BLOCK, BLOCK), 1)
    sp = jnp.where((col >= row + lag_off) & (n > 0), sp, _NEG)
    sc = jnp.where(col <= row, sc, _NEG)
    return sp, sc


def _attn_specs(dil, q0, k0, v0, kv_group):
    def q_map(h, r, n):
        return (n, r * QKV_COLS + q0 + h)

    def kv_map(base, prev):
        def f(h, r, n):
            return (jnp.maximum(n - 1, 0) if prev else n, r * QKV_COLS + base + h // kv_group)
        return f

    blk = (BLOCK, HEAD_DIM)
    return [pl.BlockSpec(blk, q_map), pl.BlockSpec(blk, kv_map(k0, True)), pl.BlockSpec(blk, kv_map(k0, False)),
            pl.BlockSpec(blk, kv_map(v0, True)), pl.BlockSpec(blk, kv_map(v0, False))]


def _attn_fwd(qkv, dil, *, heads, q0, k0, v0, kv_group, lag_off, sink, name):
    L = SEQ // dil
    nb = L // BLOCK
    qkv_r = qkv.reshape(L, dil * QKV_COLS * HEAD_DIM)

    def body(*refs):
        if sink is None:
            q_ref, kp_ref, kc_ref, vp_ref, vc_ref, o_ref, lse_ref = refs
        else:
            q_ref, kp_ref, kc_ref, vp_ref, vc_ref, sk_ref, o_ref, lse_ref = refs
        n = pl.program_id(2)
        sp, sc = _band_scores(q_ref[...], kp_ref[...], kc_ref[...], n, lag_off)
        m = jnp.maximum(jnp.max(sp, axis=1, keepdims=True), jnp.max(sc, axis=1, keepdims=True))
        if sink is not None:
            sk = sk_ref[0][:, 0:1]
            m = jnp.maximum(m, sk)
        pp = jnp.exp(sp - m)
        pc = jnp.exp(sc - m)
        den = jnp.sum(pp, axis=1, keepdims=True) + jnp.sum(pc, axis=1, keepdims=True)
        if sink is not None:
            den = den + jnp.exp(sk - m)
        acc = jnp.dot(pp.astype(_MXU_DTYPE), vp_ref[...], preferred_element_type=_F32)
        acc = acc + jnp.dot(pc.astype(_MXU_DTYPE), vc_ref[...], preferred_element_type=_F32)
        o_ref[...] = acc / den
        lse_ref[...] = jnp.broadcast_to(m + jnp.log(den), (BLOCK, HEAD_DIM))

    in_specs = _attn_specs(dil, q0, k0, v0, kv_group)
    args = [qkv_r] * 5
    if sink is not None:
        in_specs.append(pl.BlockSpec((1, 1, HEAD_DIM), lambda h, r, n: (h, 0, 0)))
        args.append(jnp.broadcast_to(sink.reshape(heads, 1, 1), (heads, 1, HEAD_DIM)))
    o_spec = pl.BlockSpec((BLOCK, HEAD_DIM), lambda h, r, n: (n, r * heads + h))
    shape = jax.ShapeDtypeStruct((L, dil * heads * HEAD_DIM), _F32)
    o, lse = pl.pallas_call(
        body, grid=(heads, dil, nb), in_specs=in_specs, out_specs=[o_spec, o_spec], out_shape=[shape, shape],
        name=name, compiler_params=_cparams(dimension_semantics=("parallel", "parallel", "parallel")),
    )(*args)
    return o.reshape(SEQ, heads * HEAD_DIM), lse.reshape(SEQ, heads * HEAD_DIM)


def _attn_bwd(qkv, dmixed, o, lse, dil, *, heads, q0, k0, v0, kv_group, lag_off, do0, sink, name):
    L = SEQ // dil
    nb = L // BLOCK
    kvh = heads // kv_group
    qkv_r = qkv.reshape(L, dil * QKV_COLS * HEAD_DIM)
    dm_r = dmixed.reshape(L, dil * D_MODEL)
    o_r = o.reshape(L, dil * heads * HEAD_DIM)
    lse_r = lse.reshape(L, dil * heads * HEAD_DIM)

    def body(*refs):
        if sink is None:
            q_ref, kp_ref, kc_ref, vp_ref, vc_ref, do_ref, o_ref, lse_ref, dq_ref, dk_ref, dv_ref = refs
        else:
            q_ref, kp_ref, kc_ref, vp_ref, vc_ref, do_ref, o_ref, lse_ref, sk_ref, dq_ref, dk_ref, dv_ref, dsk_ref = refs
        g = pl.program_id(0)
        n = pl.program_id(2)
        q, kp, kc, vp, vc = q_ref[...], kp_ref[...], kc_ref[...], vp_ref[...], vc_ref[...]
        do = do_ref[...]
        delta = jnp.sum(do * o_ref[...], axis=1, keepdims=True)
        lse_c = lse_ref[:, 0:1]
        sp, sc = _band_scores(q, kp, kc, n, lag_off)
        pp = jnp.exp(sp - lse_c)
        pc = jnp.exp(sc - lse_c)
        dob = do.astype(_MXU_DTYPE)
        dsp = (pp * (lax.dot_general(dob, vp, _NT, preferred_element_type=_F32) - delta) * _SCALE).astype(_MXU_DTYPE)
        dsc = (pc * (lax.dot_general(dob, vc, _NT, preferred_element_type=_F32) - delta) * _SCALE).astype(_MXU_DTYPE)
        dq_ref[...] = (jnp.dot(dsp, kp, preferred_element_type=_F32) + jnp.dot(dsc, kc, preferred_element_type=_F32))

        @pl.when((n == 0) & (g % kv_group == 0))
        def _():
            dk_ref[...] = jnp.zeros_like(dk_ref)
            dv_ref[...] = jnp.zeros_like(dv_ref)

        prev = pl.ds(pl.multiple_of(jnp.maximum(n - 1, 0) * BLOCK, BLOCK), BLOCK)
        cur = pl.ds(pl.multiple_of(n * BLOCK, BLOCK), BLOCK)
        dk_ref[prev, :] += lax.dot_general(dsp, q, _TN, preferred_element_type=_F32)
        dv_ref[prev, :] += lax.dot_general(pp.astype(_MXU_DTYPE), dob, _TN, preferred_element_type=_F32)
        dk_ref[cur, :] += lax.dot_general(dsc, q, _TN, preferred_element_type=_F32)
        dv_ref[cur, :] += lax.dot_general(pc.astype(_MXU_DTYPE), dob, _TN, preferred_element_type=_F32)
        if sink is not None:
            @pl.when(n == 0)
            def _():
                dsk_ref[...] = jnp.zeros_like(dsk_ref)

            sk = sk_ref[0][:, 0:1]
            dsk_ref[...] += jnp.sum(-delta * jnp.exp(sk - lse_c))

    in_specs = _attn_specs(dil, q0, k0, v0, kv_group)
    blk = (BLOCK, HEAD_DIM)
    in_specs.append(pl.BlockSpec(blk, lambda h, r, n: (n, r * (D_MODEL // HEAD_DIM) + do0 + h)))
    in_specs.append(pl.BlockSpec(blk, lambda h, r, n: (n, r * heads + h)))
    in_specs.append(pl.BlockSpec(blk, lambda h, r, n: (n, r * heads + h)))
    args = [qkv_r] * 5 + [dm_r, o_r, lse_r]
    if sink is not None:
        in_specs.append(pl.BlockSpec((1, 1, HEAD_DIM), lambda h, r, n: (h, 0, 0)))
        args.append(jnp.broadcast_to(sink.reshape(heads, 1, 1), (heads, 1, HEAD_DIM)))
    kv_spec = pl.BlockSpec((L, HEAD_DIM), lambda h, r, n: (0, r * kvh + h // kv_group))
    out_specs = [pl.BlockSpec(blk, lambda h, r, n: (n, r * heads + h)), kv_spec, kv_spec]
    out_shape = [jax.ShapeDtypeStruct((L, dil * heads * HEAD_DIM), _F32),
                 jax.ShapeDtypeStruct((L, dil * kvh * HEAD_DIM), _F32),
                 jax.ShapeDtypeStruct((L, dil * kvh * HEAD_DIM), _F32)]
    if sink is not None:
        out_specs.append(pl.BlockSpec((1, 8, HEAD_DIM), lambda h, r, n: (h, 0, 0)))
        out_shape.append(jax.ShapeDtypeStruct((heads, 8, HEAD_DIM), _F32))
    res = pl.pallas_call(
        body, grid=(heads, dil, nb), in_specs=in_specs, out_specs=out_specs, out_shape=out_shape, name=name,
        compiler_params=_cparams(dimension_semantics=("arbitrary", "arbitrary", "arbitrary")),
    )(*args)
    dq = res[0].reshape(SEQ, heads * HEAD_DIM)
    dk = res[1].reshape(SEQ, kvh * HEAD_DIM)
    dv = res[2].reshape(SEQ, kvh * HEAD_DIM)
    if sink is not None:
        return dq, dk, dv, res[3][:, 0, 0]
    return dq, dk, dv


def _dilated_mix(os_, lses, *, name):
    tm = 256

    def body(o1, o4, o16, l1, l4, l16, o_ref, lse_ref):
        a, b, c = l1[...], l4[...], l16[...]
        m = jnp.maximum(jnp.maximum(a, b), c)
        ea, eb, ec = jnp.exp(a - m), jnp.exp(b - m), jnp.exp(c - m)
        den = ea + eb + ec
        o_ref[...] = (ea * o1[...] + eb * o4[...] + ec * o16[...]) / den
        lse_ref[...] = m + jnp.log(den)

    w = B_HEADS * HEAD_DIM
    spec = pl.BlockSpec((tm, w), lambda i: (i, 0))
    shape = jax.ShapeDtypeStruct((SEQ, w), _F32)
    return pl.pallas_call(
        body, grid=(SEQ // tm,), in_specs=[spec] * 6, out_specs=[spec, spec], out_shape=[shape, shape], name=name,
        compiler_params=_cparams(dimension_semantics=("parallel",)),
    )(*os_, *lses)


_HG_TILE = 128
_HG_CHUNKS = _HG_TILE // A_CHUNK
_HG_TILES = SEQ // _HG_TILE
_HI = lax.Precision.HIGHEST


def _chunk_tri():
    i = np.arange(_HG_TILE)
    return jnp.asarray(((i[:, None] // A_CHUNK == i[None, :] // A_CHUNK) & (i[None, :] <= i[:, None])).astype(np.float32))


def _layer_lb(lb_ref, layer):
    if layer == 0:
        return jnp.zeros((1, HEAD_DIM), _F32)
    lg = lb_ref[...]
    m = jnp.max(lg, axis=0, keepdims=True)
    e = jnp.exp(lg - m)
    return e[1:2, :] / jnp.sum(e, axis=0, keepdims=True)


def _hgrn_gates(q, fr, lb):
    sgq = _sigmoid(q)
    sg = _sigmoid(fr)
    f = lb + (1.0 - lb) * sg
    return sgq, q * sgq, sg, f, 1.0 - f


def _hgrn_fwd(proj, lb_logits, norm_w, layer, *, name):
    tri = _chunk_tri()

    def body(q_ref, f_ref, i_ref, g_ref, lb_ref, nw_ref, tri_ref, o_ref, raw_ref, st_ref, state):
        @pl.when(pl.program_id(1) == 0)
        def _():
            state[...] = jnp.zeros_like(state)

        lb = _layer_lb(lb_ref, layer)
        _, qs, _, f, k = _hgrn_gates(q_ref[...], f_ref[...], lb)
        v = i_ref[...]
        b = jnp.dot(tri_ref[...], jnp.log(f), precision=_HI, preferred_element_type=_F32)
        eb = jnp.exp(b)
        ridx = lax.broadcasted_iota(jnp.int32, (A_CHUNK, HEAD_DIM), 0)
        outs = []
        for c in range(_HG_CHUNKS):
            sl = slice(c * A_CHUNK, (c + 1) * A_CHUNK)
            bc, qc, kc, vc = b[sl], qs[sl], k[sl], v[sl]
            bl = bc[A_CHUNK - 1:A_CHUNK]
            st = state[...]
            st_ref[0, c] = st
            o_c = lax.dot_general((qc * eb[sl]).astype(_MXU_DTYPE), st.astype(_MXU_DTYPE), _NT, preferred_element_type=_F32)
            rows = []
            for i in range(A_CHUNK):
                di = jnp.exp(jnp.where(ridx <= i, bc[i:i + 1] - bc, _NEG))
                a = jnp.sum(qc[i:i + 1] * kc * di, axis=1, keepdims=True)
                rows.append(jnp.sum(a * vc, axis=0, keepdims=True))
            outs.append(o_c + jnp.concatenate(rows, axis=0))
            kt = (kc * jnp.exp(bl - bc)).astype(_MXU_DTYPE)
            state[...] = st * jnp.exp(bl) + lax.dot_general(vc.astype(_MXU_DTYPE), kt, _TN, preferred_element_type=_F32)
        o = jnp.concatenate(outs, axis=0)
        raw_ref[...] = o
        r = lax.rsqrt(jnp.mean(o * o, axis=-1, keepdims=True) + LN_EPS)
        g = g_ref[...]
        o_ref[...] = o * r * nw_ref[...] * (g * _sigmoid(g))

    blk = (_HG_TILE, HEAD_DIM)

    def col(base):
        return pl.BlockSpec(blk, lambda h, t: (t, base + h))

    o_spec = pl.BlockSpec(blk, lambda h, t: (t, h))
    o_shape = jax.ShapeDtypeStruct((SEQ, A_HEADS * HEAD_DIM), _F32)
    return pl.pallas_call(
        body, grid=(A_HEADS, _HG_TILES),
        in_specs=[col(0), col(4), col(8), col(12), pl.BlockSpec((DEPTH, HEAD_DIM), lambda h, t: (0, h)),
                  pl.BlockSpec((1, HEAD_DIM), lambda h, t: (0, 0)), pl.BlockSpec(blk, lambda h, t: (0, 0))],
        out_specs=[o_spec, o_spec, pl.BlockSpec((1, _HG_CHUNKS, HEAD_DIM, HEAD_DIM), lambda h, t: (h, t, 0, 0))],
        out_shape=[o_shape, o_shape, jax.ShapeDtypeStruct((A_HEADS, SEQ // A_CHUNK, HEAD_DIM, HEAD_DIM), _F32)],
        scratch_shapes=[pltpu.VMEM((HEAD_DIM, HEAD_DIM), _F32)], name=name,
        compiler_params=_cparams(dimension_semantics=("parallel", "arbitrary")),
    )(proj, proj, proj, proj, lb_logits, norm_w.reshape(1, HEAD_DIM), tri)


def _hgrn_bwd(proj, lb_logits, norm_w, raw, states, dmixed, layer, *, name):
    tri = _chunk_tri()
    triu = tri.T

    def body(q_ref, f_ref, i_ref, g_ref, lb_ref, nw_ref, tri_ref, triu_ref, raw_ref, do_ref, st_ref,
             dq_ref, df_ref, di_ref, dg_ref, dnw_ref, dlb_ref, dstate):
        @pl.when(pl.program_id(1) == 0)
        def _():
            dstate[...] = jnp.zeros_like(dstate)
            dlb_ref[...] = jnp.zeros_like(dlb_ref)

        @pl.when((pl.program_id(0) == 0) & (pl.program_id(1) == 0))
        def _():
            dnw_ref[...] = jnp.zeros_like(dnw_ref)

        lb = _layer_lb(lb_ref, layer)
        q = q_ref[...]
        sgq, qs, sg, f, k = _hgrn_gates(q, f_ref[...], lb)
        v = i_ref[...]
        b = jnp.dot(tri_ref[...], jnp.log(f), precision=_HI, preferred_element_type=_F32)
        eb = jnp.exp(b)
        g = g_ref[...]
        nw = nw_ref[...]
        o = raw_ref[...]
        dout = do_ref[...]
        sgg = _sigmoid(g)
        r = lax.rsqrt(jnp.mean(o * o, axis=-1, keepdims=True) + LN_EPS)
        dg_ref[...] = (dout * (o * r * nw) * (sgg * (1.0 + g * (1.0 - sgg)))).astype(dg_ref.dtype)
        don = dout * (g * sgg)
        dnw_ref[0:1, :] += jnp.sum(don * o * r, axis=0, keepdims=True)
        dy = don * nw
        do_raw = r * dy - o * (r * r * r) * jnp.mean(o * dy, axis=-1, keepdims=True)

        ridx = lax.broadcasted_iota(jnp.int32, (A_CHUNK, HEAD_DIM), 0)
        dqs_t, dk_t, db_t, dv_t = [None] * _HG_CHUNKS, [None] * _HG_CHUNKS, [None] * _HG_CHUNKS, [None] * _HG_CHUNKS
        for c in reversed(range(_HG_CHUNKS)):
            sl = slice(c * A_CHUNK, (c + 1) * A_CHUNK)
            bc, qc, kc, vc, doc = b[sl], qs[sl], k[sl], v[sl], do_raw[sl]
            bl = bc[A_CHUNK - 1:A_CHUNK]
            ebc = eb[sl]
            ebl = jnp.exp(bl - bc)
            lam = jnp.exp(bl)
            qt = qc * ebc
            kt = kc * ebl
            dst = dstate[...]
            stp = st_ref[0, c]
            dob = doc.astype(_MXU_DTYPE)
            dstb = dst.astype(_MXU_DTYPE)
            dqt = jnp.dot(dob, stp.astype(_MXU_DTYPE), preferred_element_type=_F32)
            dkt = jnp.dot(vc.astype(_MXU_DTYPE), dstb, preferred_element_type=_F32)
            dv = lax.dot_general(kt.astype(_MXU_DTYPE), dstb, _NT, preferred_element_type=_F32)
            dlam = jnp.sum(stp * dst, axis=0, keepdims=True)
            dstate[...] = dst * lam + lax.dot_general(dob, qt.astype(_MXU_DTYPE), _TN, preferred_element_type=_F32)
            dqs_rows = []
            dk_in = jnp.zeros((A_CHUNK, HEAD_DIM), _F32)
            for i in range(A_CHUNK):
                di = jnp.exp(jnp.where(ridx <= i, bc[i:i + 1] - bc, _NEG))
                qi = qc[i:i + 1]
                doi = doc[i:i + 1]
                w = kc * di
                a = jnp.sum(qi * w, axis=1, keepdims=True)
                dv = dv + a * doi
                da = jnp.sum(doi * vc, axis=1, keepdims=True)
                dqs_rows.append(jnp.sum(da * w, axis=0, keepdims=True))
                dk_in = dk_in + da * (qi * di)
            dqs_in = jnp.concatenate(dqs_rows, axis=0)
            dbl = jnp.sum(dkt * kt, axis=0, keepdims=True) + dlam * lam
            db = qc * dqs_in - kc * dk_in + dqt * qt - dkt * kt
            db_t[c] = db + jnp.where(ridx == A_CHUNK - 1, dbl, 0.0)
            dqs_t[c] = dqs_in + dqt * ebc
            dk_t[c] = dk_in + dkt * ebl
            dv_t[c] = dv
        dqs = jnp.concatenate(dqs_t, axis=0)
        dk = jnp.concatenate(dk_t, axis=0)
        db = jnp.concatenate(db_t, axis=0)
        di_ref[...] = jnp.concatenate(dv_t, axis=0).astype(di_ref.dtype)
        dlogf = jnp.dot(triu_ref[...], db, precision=_HI, preferred_element_type=_F32)
        df = dlogf / f - dk
        df_ref[...] = (df * (1.0 - lb) * sg * (1.0 - sg)).astype(df_ref.dtype)
        dlb_ref[0, 0:1, :] += jnp.sum(df * (1.0 - sg), axis=0, keepdims=True)
        dq_ref[...] = (dqs * (sgq * (1.0 + q * (1.0 - sgq)))).astype(dq_ref.dtype)

    blk = (_HG_TILE, HEAD_DIM)
    last = _HG_TILES - 1

    def col(base):
        return pl.BlockSpec(blk, lambda h, t: (last - t, base + h))

    tri_spec = pl.BlockSpec(blk, lambda h, t: (0, 0))
    acc_spec = pl.BlockSpec((1, 8, HEAD_DIM), lambda h, t: (h, 0, 0))
    acc_shape = jax.ShapeDtypeStruct((A_HEADS, 8, HEAD_DIM), _F32)
    dq, df, di, dg, dnw, dlb = pl.pallas_call(
        body, grid=(A_HEADS, _HG_TILES),
        in_specs=[col(0), col(4), col(8), col(12), pl.BlockSpec((DEPTH, HEAD_DIM), lambda h, t: (0, h)),
                  pl.BlockSpec((1, HEAD_DIM), lambda h, t: (0, 0)), tri_spec, tri_spec, col(0), col(0),
                  pl.BlockSpec((1, _HG_CHUNKS, HEAD_DIM, HEAD_DIM), lambda h, t: (h, last - t, 0, 0))],
        out_specs=[col(0), col(0), col(0), col(0), pl.BlockSpec((8, HEAD_DIM), lambda h, t: (0, 0)), acc_spec],
        out_shape=[jax.ShapeDtypeStruct((SEQ, A_HEADS * HEAD_DIM), _MXU_DTYPE)] * 4
        + [jax.ShapeDtypeStruct((8, HEAD_DIM), _F32), acc_shape],
        scratch_shapes=[pltpu.VMEM((HEAD_DIM, HEAD_DIM), _F32)], name=name,
        compiler_params=_cparams(dimension_semantics=("arbitrary", "arbitrary")),
    )(proj, proj, proj, proj, lb_logits, norm_w.reshape(1, HEAD_DIM), tri, triu, raw, dmixed, states)
    return dq, df, di, dg, dnw[0], dlb[:, 0, :].reshape(A_HEADS * HEAD_DIM)


def _exchange(arrays, scatter, *, name):
    n = len(arrays)
    n_peer = N_DEV - 1

    def body(*refs):
        ins, outs = refs[:n], refs[n:2 * n]
        send_sems, recv_sems, loc_sems = refs[2 * n:]
        x, y, c = lax.axis_index("x"), lax.axis_index("y"), lax.axis_index("c")
        me = 4 * x + 2 * y + c
        local = []
        for a in range(n):
            cp = pltpu.make_async_copy(ins[a].at[me] if scatter else ins[a], outs[a].at[me], loc_sems.at[a])
            cp.start()
            local.append(cp)

        def peer(k):
            px = jnp.bitwise_xor(x, (k >> 2) & 1)
            py = jnp.bitwise_xor(y, (k >> 1) & 1)
            pc = jnp.bitwise_xor(c, k & 1)
            return (px, py, pc), 4 * px + 2 * py + pc

        def copy(a, k):
            dev, pid = peer(k)
            return pltpu.make_async_remote_copy(
                src_ref=ins[a].at[pid] if scatter else ins[a], dst_ref=outs[a].at[me],
                send_sem=send_sems.at[a * n_peer + k - 1], recv_sem=recv_sems.at[a * n_peer + k - 1],
                device_id=dev, device_id_type=pl.DeviceIdType.MESH)

        def arrival(a, k):
            dev, pid = peer(k)
            return pltpu.make_async_remote_copy(
                src_ref=ins[a].at[pid] if scatter else ins[a], dst_ref=outs[a].at[pid],
                send_sem=send_sems.at[a * n_peer + k - 1], recv_sem=recv_sems.at[a * n_peer + k - 1],
                device_id=dev, device_id_type=pl.DeviceIdType.MESH)

        sends = [copy(a, k) for k in range(1, N_DEV) for a in range(n)]
        for cp in sends:
            cp.start()
        for k in range(1, N_DEV):
            for a in range(n):
                arrival(a, k).wait_recv()
        for cp in sends:
            cp.wait_send()
        for cp in local:
            cp.wait()

    def out_shape(a):
        blk = a.shape[1:] if scatter else a.shape
        return jax.ShapeDtypeStruct((N_DEV,) + tuple(blk), a.dtype)

    any_spec = pl.BlockSpec(memory_space=pl.ANY)
    return pl.pallas_call(
        body, in_specs=[any_spec] * n, out_specs=[any_spec] * n, out_shape=[out_shape(a) for a in arrays],
        scratch_shapes=[pltpu.SemaphoreType.DMA((n * n_peer,)), pltpu.SemaphoreType.DMA((n * n_peer,)),
                        pltpu.SemaphoreType.DMA((n,))],
        name=name, compiler_params=pltpu.CompilerParams(has_side_effects=True),
    )(*arrays)


N_CHIP = N_DEV // 2
_MESH_ID = pl.DeviceIdType.MESH


def _place():
    x, y, c = lax.axis_index("x"), lax.axis_index("y"), lax.axis_index("c")
    chips = [(1 - x, y), (x, 1 - y), (1 - x, 1 - y)]
    return x, y, c, 2 * x + y, chips


def _gather_blocks(arrays, *, name):
    n = len(arrays)

    def body(*refs):
        ins, outs = refs[:n], refs[n:2 * n]
        send_sems, recv_sems, loc_sems = refs[2 * n:]
        x, y, c, _, chips = _place()
        me = 4 * x + 2 * y + c
        sibling = (x, y, 1 - c)

        def slot(px, py, pc):
            return 4 * px + 2 * py + pc

        def copy(a, k, block, to, src=None):
            dst = outs[a].at[slot(*block)]
            return pltpu.make_async_remote_copy(
                src_ref=dst if src is None else src, dst_ref=dst, send_sem=send_sems.at[7 * a + k],
                recv_sem=recv_sems.at[7 * a + k], device_id=to, device_id_type=_MESH_ID)

        local = [pltpu.make_async_copy(ins[a], outs[a].at[me], loc_sems.at[a]) for a in range(n)]
        for cp in local:
            cp.start()
        first = []
        for a in range(n):
            first.append(copy(a, 0, (x, y, c), sibling, src=ins[a]))
            first += [copy(a, 1 + j, (x, y, c), (*chip, c), src=ins[a]) for j, chip in enumerate(chips)]
        for cp in first:
            cp.start()
        passed = []
        for a in range(n):
            for j, chip in enumerate(chips):
                copy(a, 1 + j, (*chip, c), (x, y, c)).wait_recv()
                fwd = copy(a, 4 + j, (*chip, c), sibling)
                fwd.start()
                passed.append(fwd)
        for a in range(n):
            copy(a, 0, sibling, (x, y, c)).wait_recv()
            for j, chip in enumerate(chips):
                copy(a, 4 + j, (*chip, 1 - c), (x, y, c)).wait_recv()
        for cp in first + passed:
            cp.wait_send()
        for cp in local:
            cp.wait()

    any_spec = pl.BlockSpec(memory_space=pl.ANY)
    return pl.pallas_call(
        body, in_specs=[any_spec] * n, out_specs=[any_spec] * n,
        out_shape=[jax.ShapeDtypeStruct((N_DEV,) + a.shape, a.dtype) for a in arrays],
        scratch_shapes=[pltpu.SemaphoreType.DMA((7 * n,)), pltpu.SemaphoreType.DMA((7 * n,)), pltpu.SemaphoreType.DMA((n,))],
        name=name, compiler_params=pltpu.CompilerParams(has_side_effects=True),
    )(*arrays)


def _sibling_swap(arrays, *, name):
    n = len(arrays)

    def body(*refs):
        ins, outs = refs[:n], refs[n:2 * n]
        send_sems, recv_sems = refs[2 * n:]
        x, y, c, _, _ = _place()
        copies = [pltpu.make_async_remote_copy(
            src_ref=ins[a].at[:, 1 - c], dst_ref=outs[a], send_sem=send_sems.at[a], recv_sem=recv_sems.at[a],
            device_id=(x, y, 1 - c), device_id_type=_MESH_ID) for a in range(n)]
        for cp in copies:
            cp.start()
        for cp in copies:
            cp.wait()

    any_spec = pl.BlockSpec(memory_space=pl.ANY)
    return pl.pallas_call(
        body, in_specs=[any_spec] * n, out_specs=[any_spec] * n,
        out_shape=[jax.ShapeDtypeStruct((N_CHIP,) + a.shape[2:], a.dtype) for a in arrays],
        scratch_shapes=[pltpu.SemaphoreType.DMA((n,)), pltpu.SemaphoreType.DMA((n,))],
        name=name, compiler_params=pltpu.CompilerParams(has_side_effects=True),
    )(*arrays)


def _pair_add(mine, theirs, core, *, name):
    _, _, R, C = mine.shape
    tr = max(t for t in range(16, R + 1, 16) if R % t == 0 and t * C <= 512 * 1024)

    def body(core_ref, m_ref, t_ref, o_ref):
        del core_ref
        o_ref[...] = (m_ref[...].astype(_F32) + t_ref[...].astype(_F32)).astype(o_ref.dtype)

    grid_spec = pltpu.PrefetchScalarGridSpec(
        num_scalar_prefetch=1, grid=(N_CHIP, R // tr),
        in_specs=[pl.BlockSpec((None, None, tr, C), lambda q, i, core: (q, core[0], i, 0)),
                  pl.BlockSpec((None, tr, C), lambda q, i, core: (q, i, 0))],
        out_specs=pl.BlockSpec((None, tr, C), lambda q, i, core: (q, i, 0)))
    return pl.pallas_call(
        body, grid_spec=grid_spec, out_shape=jax.ShapeDtypeStruct((N_CHIP, R, C), mine.dtype), name=name,
        compiler_params=_cparams(dimension_semantics=("parallel", "parallel")),
    )(core.reshape(1), mine, theirs)


def _chip_exchange(arrays, *, name):
    n = len(arrays)

    def body(*refs):
        ins, outs = refs[:n], refs[n:2 * n]
        send_sems, recv_sems, loc_sems = refs[2 * n:]
        x, y, c, p, chips = _place()
        local = [pltpu.make_async_copy(ins[a].at[p], outs[a].at[p], loc_sems.at[a]) for a in range(n)]
        for cp in local:
            cp.start()

        def copy(a, j, landing):
            qx, qy = chips[j]
            q = 2 * qx + qy
            return pltpu.make_async_remote_copy(
                src_ref=ins[a].at[q], dst_ref=outs[a].at[q if landing else p], send_sem=send_sems.at[3 * a + j],
                recv_sem=recv_sems.at[3 * a + j], device_id=(qx, qy, c), device_id_type=_MESH_ID)

        sends = [copy(a, j, False) for a in range(n) for j in range(3)]
        for cp in sends:
            cp.start()
        for a in range(n):
            for j in range(3):
                copy(a, j, True).wait_recv()
        for cp in sends:
            cp.wait_send()
        for cp in local:
            cp.wait()

    any_spec = pl.BlockSpec(memory_space=pl.ANY)
    return pl.pallas_call(
        body, in_specs=[any_spec] * n, out_specs=[any_spec] * n,
        out_shape=[jax.ShapeDtypeStruct(a.shape, a.dtype) for a in arrays],
        scratch_shapes=[pltpu.SemaphoreType.DMA((3 * n,)), pltpu.SemaphoreType.DMA((3 * n,)), pltpu.SemaphoreType.DMA((n,))],
        name=name, compiler_params=pltpu.CompilerParams(has_side_effects=True),
    )(*arrays)


_HBM = pl.BlockSpec(memory_space=pltpu.HBM)
_SEM = pl.BlockSpec(memory_space=pltpu.SEMAPHORE)
_TOKEN = pl.BlockSpec(memory_space=pltpu.VMEM)
_DATAFLOW = pltpu.SideEffectType.DATAFLOW_SIDE_EFFECTING


def _hbm(a):
    return pltpu.HBM(a.shape, a.dtype)


def _token_shape():
    return jax.ShapeDtypeStruct((8, 128), _F32)


def _dev_slot(px, py, pc):
    return 4 * px + 2 * py + pc


def _gather_start(blocks, landings, *, name):
    n = len(blocks)

    def body(*refs):
        ins, lands = refs[:n], refs[n:2 * n]
        send_sems, d2d_sems, ici_sems = refs[2 * n:2 * n + 3]
        token = refs[-1]
        x, y, c, _, chips = _place()
        for a in range(n):
            dst = lands[a].at[_dev_slot(x, y, c)]
            pltpu.make_async_remote_copy(src_ref=ins[a], dst_ref=dst, send_sem=send_sems.at[4 * a], recv_sem=d2d_sems.at[a],
                                         device_id=(x, y, 1 - c), device_id_type=_MESH_ID).start()
            for j, chip in enumerate(chips):
                pltpu.make_async_remote_copy(src_ref=ins[a], dst_ref=dst, send_sem=send_sems.at[4 * a + 1 + j],
                                             recv_sem=ici_sems.at[3 * a + j], device_id=(*chip, c),
                                             device_id_type=_MESH_ID).start()
        token[...] = jnp.zeros_like(token)

    res = pl.pallas_call(
        body, name=name, in_specs=[_HBM] * (2 * n),
        out_shape=(pltpu.SemaphoreType.DMA((4 * n,)), pltpu.SemaphoreType.DMA((n,)), pltpu.SemaphoreType.DMA((3 * n,)),
                   *[_hbm(b) for b in blocks], *[_hbm(b) for b in landings], _token_shape()),
        out_specs=(_SEM, _SEM, _SEM, *[_HBM] * (2 * n), _TOKEN),
        input_output_aliases={i: 3 + i for i in range(2 * n)},
        compiler_params=pltpu.CompilerParams(has_side_effects=_DATAFLOW),
    )(*[pltpu.with_memory_space_constraint(b, pltpu.HBM) for b in blocks],
      *[pltpu.with_memory_space_constraint(b, pltpu.HBM) for b in landings])
    return res[0], res[1], res[2], list(res[3:3 + n]), list(res[3 + n:3 + 2 * n]), res[-1]


def _gather_forward(landings, ici_sems, after, *, name):
    n = len(landings)

    def body(*refs):
        lands = refs[:n]
        ici = refs[n]
        f_send, f_recv = refs[n + 2], refs[n + 3]
        token = refs[-1]
        x, y, c, _, chips = _place()
        for a in range(n):
            for j, chip in enumerate(chips):
                blk = lands[a].at[_dev_slot(*chip, c)]
                pltpu.make_async_remote_copy(src_ref=blk, dst_ref=blk, send_sem=f_send.at[3 * a + j], recv_sem=ici.at[3 * a + j],
                                             device_id=(*chip, c), device_id_type=_MESH_ID).wait_recv()
                pltpu.make_async_remote_copy(src_ref=blk, dst_ref=blk, send_sem=f_send.at[3 * a + j], recv_sem=f_recv.at[3 * a + j],
                                             device_id=(x, y, 1 - c), device_id_type=_MESH_ID).start()
        token[...] = jnp.zeros_like(token)

    res = pl.pallas_call(
        body, name=name, in_specs=[_HBM] * n + [_SEM, pl.BlockSpec(memory_space=pl.ANY)],
        out_shape=(pltpu.SemaphoreType.DMA((3 * n,)), pltpu.SemaphoreType.DMA((3 * n,)), *[_hbm(b) for b in landings], _token_shape()),
        out_specs=(_SEM, _SEM, *[_HBM] * n, _TOKEN),
        input_output_aliases={i: 2 + i for i in range(n)},
        compiler_params=pltpu.CompilerParams(has_side_effects=_DATAFLOW),
    )(*landings, ici_sems, after)
    return res[0], res[1], list(res[2:2 + n]), res[-1]


def _gather_wait(blocks, landings, send_sems, d2d_sems, f_send, f_recv, after, *, name):
    n = len(landings)

    def body(*refs):
        ins, lands = refs[:n], refs[n:2 * n]
        send, d2d, fs, fr = refs[2 * n:2 * n + 4]
        x, y, c, _, chips = _place()
        me = (x, y, c)
        for a in range(n):
            own = lands[a].at[_dev_slot(x, y, 1 - c)]
            pltpu.make_async_remote_copy(src_ref=ins[a], dst_ref=own, send_sem=send.at[4 * a], recv_sem=d2d.at[a],
                                         device_id=me, device_id_type=_MESH_ID).wait_recv()
            for j, chip in enumerate(chips):
                blk = lands[a].at[_dev_slot(*chip, 1 - c)]
                pltpu.make_async_remote_copy(src_ref=blk, dst_ref=blk, send_sem=fs.at[3 * a + j], recv_sem=fr.at[3 * a + j],
                                             device_id=me, device_id_type=_MESH_ID).wait_recv()
            for k in range(4):
                pltpu.make_async_remote_copy(src_ref=ins[a], dst_ref=own, send_sem=send.at[4 * a + k], recv_sem=d2d.at[a],
                                             device_id=me, device_id_type=_MESH_ID).wait_send()
            for j in range(3):
                pltpu.make_async_remote_copy(src_ref=own, dst_ref=own, send_sem=fs.at[3 * a + j], recv_sem=fr.at[3 * a + j],
                                             device_id=me, device_id_type=_MESH_ID).wait_send()

    res = pl.pallas_call(
        body, name=name, in_specs=[_HBM] * (2 * n) + [_SEM] * 4 + [pl.BlockSpec(memory_space=pl.ANY)],
        out_shape=(*[_hbm(b) for b in blocks], *[_hbm(b) for b in landings]), out_specs=tuple([_HBM] * (2 * n)),
        input_output_aliases={i: i for i in range(2 * n)},
        compiler_params=pltpu.CompilerParams(has_side_effects=_DATAFLOW),
    )(*blocks, *landings, send_sems, d2d_sems, f_send, f_recv, after)
    return list(res[n:])


def _chip_exchange_start(sums, landings, *, name):
    n = len(sums)

    def body(*refs):
        ins, lands = refs[:n], refs[n:2 * n]
        send_sems, recv_sems = refs[2 * n:2 * n + 2]
        token = refs[-1]
        _, _, c, p, chips = _place()
        for a in range(n):
            for j, (qx, qy) in enumerate(chips):
                pltpu.make_async_remote_copy(src_ref=ins[a].at[2 * qx + qy], dst_ref=lands[a].at[p], send_sem=send_sems.at[3 * a + j],
                                             recv_sem=recv_sems.at[3 * a + j], device_id=(qx, qy, c), device_id_type=_MESH_ID).start()
        token[...] = jnp.zeros_like(token)

    res = pl.pallas_call(
        body, name=name, in_specs=[_HBM] * (2 * n),
        out_shape=(pltpu.SemaphoreType.DMA((3 * n,)), pltpu.SemaphoreType.DMA((3 * n,)),
                   *[_hbm(b) for b in sums], *[_hbm(b) for b in landings], _token_shape()),
        out_specs=(_SEM, _SEM, *[_HBM] * (2 * n), _TOKEN),
        input_output_aliases={i: 2 + i for i in range(2 * n)},
        compiler_params=pltpu.CompilerParams(has_side_effects=_DATAFLOW),
    )(*[pltpu.with_memory_space_constraint(b, pltpu.HBM) for b in sums],
      *[pltpu.with_memory_space_constraint(b, pltpu.HBM) for b in landings])
    return res[0], res[1], list(res[2:2 + n]), list(res[2 + n:2 + 2 * n]), res[-1]


def _chip_exchange_wait(sums, landings, send_sems, recv_sems, after, *, name):
    n = len(sums)

    def body(*refs):
        ins, lands = refs[:n], refs[n:2 * n]
        send, recv = refs[2 * n:2 * n + 2]
        x, y, c, _, chips = _place()
        for a in range(n):
            for j, (qx, qy) in enumerate(chips):
                q = 2 * qx + qy
                cp = pltpu.make_async_remote_copy(src_ref=ins[a].at[q], dst_ref=lands[a].at[q], send_sem=send.at[3 * a + j],
                                                  recv_sem=recv.at[3 * a + j], device_id=(x, y, c), device_id_type=_MESH_ID)
                cp.wait_recv()
                cp.wait_send()

    res = pl.pallas_call(
        body, name=name, in_specs=[_HBM] * (2 * n) + [_SEM] * 2 + [pl.BlockSpec(memory_space=pl.ANY)],
        out_shape=(*[_hbm(b) for b in sums], *[_hbm(b) for b in landings]), out_specs=tuple([_HBM] * (2 * n)),
        input_output_aliases={i: i for i in range(2 * n)},
        compiler_params=pltpu.CompilerParams(has_side_effects=_DATAFLOW),
    )(*sums, *landings, send_sems, recv_sems, after)
    return list(res[:n]), list(res[n:])


_C1 = 1.0 - ADAM_B1 ** ADAM_STEP
_C2 = 1.0 - ADAM_B2 ** ADAM_STEP


def _adamw_math(g, w, m, v):
    m = ADAM_B1 * m + (1.0 - ADAM_B1) * g
    v = ADAM_B2 * v + (1.0 - ADAM_B2) * (g * g)
    delta = -ADAM_LR * ((m / _C1) / (jnp.sqrt(v / _C2) + ADAM_EPS) + ADAM_WD * w)
    return delta, m, v


def _adamw_reduce(landed, sums, chip, w, m, v, *, name):
    R, C = w.shape
    tr = max(t for t in range(16, R + 1, 16) if R % t == 0 and t * C <= 256 * 1024)

    def body(chip_ref, p_ref, own_ref, w_ref, m_ref, v_ref, g_ref, d_ref, nm_ref, nv_ref):
        own = own_ref[...].astype(_F32)
        g = jnp.where(chip_ref[0] == 0, own, p_ref[0].astype(_F32))
        for q in range(1, N_CHIP):
            g = g + jnp.where(chip_ref[0] == q, own, p_ref[q].astype(_F32))
        d, nm, nv = _adamw_math(g, w_ref[...], m_ref[...], v_ref[...])
        g_ref[...] = g
        d_ref[...] = d
        nm_ref[...] = nm
        nv_ref[...] = nv

    blk = pl.BlockSpec((tr, C), lambda i, chip: (i, 0))
    shape = jax.ShapeDtypeStruct((R, C), _F32)
    grid_spec = pltpu.PrefetchScalarGridSpec(
        num_scalar_prefetch=1, grid=(R // tr,),
        in_specs=[pl.BlockSpec((N_CHIP, tr, C), lambda i, chip: (0, i, 0)),
                  pl.BlockSpec((None, tr, C), lambda i, chip: (chip[0], i, 0)), blk, blk, blk],
        out_specs=[blk] * 4)
    return pl.pallas_call(
        body, grid_spec=grid_spec, out_shape=[shape] * 4, name=name,
        compiler_params=_cparams(dimension_semantics=("parallel",)),
    )(chip.reshape(1), landed, sums, w, m, v)


_PACK_LANES = 128
_LAYER_ROWS = 248
_LB_ROWS = (A_HEADS * HEAD_DIM) // _PACK_LANES


def _small_reduce(parts, lb_logits, *, name):
    rows = DEPTH * _LAYER_ROWS

    def body(p_ref, lg_ref, o_ref):
        g = p_ref[0]
        for s in range(1, N_DEV):
            g = g + p_ref[s]
        o_ref[...] = g
        lg = lg_ref[...]
        e = jnp.exp(lg - jnp.max(lg, axis=0, keepdims=True))
        p = e / jnp.sum(e, axis=0, keepdims=True)
        d1 = g[_LAYER_ROWS:_LAYER_ROWS + _LB_ROWS, :] * p[0] * p[1]
        o_ref[0:_LB_ROWS, :] = -d1
        o_ref[_LAYER_ROWS:_LAYER_ROWS + _LB_ROWS, :] = d1

    return pl.pallas_call(
        body, out_shape=jax.ShapeDtypeStruct((rows, _PACK_LANES), _F32), name=name,
        compiler_params=_cparams(),
    )(parts, lb_logits.reshape(DEPTH, _LB_ROWS, _PACK_LANES))


def _adamw_small(g, w, m, v, *, name):
    def body(g_ref, w_ref, m_ref, v_ref, d_ref, nm_ref, nv_ref):
        d, nm, nv = _adamw_math(g_ref[...], w_ref[...], m_ref[...], v_ref[...])
        d_ref[...] = d
        nm_ref[...] = nm
        nv_ref[...] = nv

    shape = jax.ShapeDtypeStruct(g.shape, _F32)
    return pl.pallas_call(body, out_shape=[shape] * 3, name=name, compiler_params=_cparams())(g, w, m, v)


def _pack(vectors, rows):
    flat = jnp.concatenate([v.reshape(-1).astype(_F32) for v in vectors])
    return jnp.pad(flat, (0, rows * _PACK_LANES - flat.shape[0])).reshape(rows, _PACK_LANES)


def _unpack(packed, shapes):
    flat = packed.reshape(-1)
    out, at = [], 0
    for s in shapes:
        size = int(np.prod(s))
        out.append(flat[at:at + size].reshape(s))
        at += size
    return out


_BIG = ("w_in", "w_gate", "w_up", "w_out", "w_down")


def _full_weight(name, g):
    if name == "w_out":
        return g.reshape(D_MODEL, D_MODEL)
    if name == "w_down":
        return g.reshape(D_FF, D_MODEL)
    return _from_slabs(g)


class _WeightGather:
    def __init__(self, names, blocks, me, tag):
        self.names, self.tag = names, tag
        landings = [lax.dynamic_update_index_in_dim(lax.empty((N_DEV,) + b.shape, b.dtype), b[None], me, 0) for b in blocks]
        self.send, self.d2d, self.ici, self.blocks, self.lands, self.token = _gather_start(
            blocks, landings, name=f"gather_start_{tag}")

    def forward(self, after):
        self.f_send, self.f_recv, self.lands, token = _gather_forward(self.lands, self.ici, after, name=f"gather_forward_{self.tag}")
        return token

    def wait(self, after):
        got = _gather_wait(self.blocks, self.lands, self.send, self.d2d, self.f_send, self.f_recv, after,
                           name=f"gather_wait_{self.tag}")
        return {n: _full_weight(n, g) for n, g in zip(self.names, got)}


class _LayerWeights:
    def __init__(self, ready, pending=None, forwards=(), tokens=()):
        self.ready, self.pending, self.forwards, self._tokens = dict(ready), pending, list(forwards), list(tokens)

    def at(self, point, after):
        for when, gather in self.forwards:
            if when == point:
                self._tokens.append(gather.forward(after))

    def tokens(self):
        out, self._tokens = self._tokens, []
        return out

    def get(self, name, after):
        if name not in self.ready:
            self.ready.update(self.pending.wait(after))
        return self.ready[name]


def _layer_fwd(x, xb, ws, lb_logits, a_norm_w, c_sink, ln1_g, ln1_b, conv_b, ln2_g, ln2_b, tabs, l):
    proj = _mm(xb, ws.get("w_in", xb), tm=1024, tn=512, after=ws.tokens(), name=f"proj_{l}")
    qkv = _qkv_prep(proj, tabs, name=f"qkv_prep_{l}")
    o_a, raw, states = _hgrn_fwd(proj, lb_logits, a_norm_w, l, name=f"hgrn_fwd_{l}")
    os_, lses = [], []
    for d in DILATIONS:
        o, lse = _attn_fwd(qkv, d, heads=B_HEADS, q0=QB0, k0=KB0, v0=VB0, kv_group=1, lag_off=0, sink=None,
                           name=f"dilated_fwd_{d}_{l}")
        os_.append(o)
        lses.append(lse)
    ws.at("dilated", os_[-1])
    o_b, lse_b = _dilated_mix(os_, lses, name=f"dilated_mix_{l}")
    o_c, lse_c = _attn_fwd(qkv, 1, heads=C_HEADS, q0=QC0, k0=KC0, v0=VC0, kv_group=C_HEADS // C_KV_HEADS, lag_off=1,
                           sink=c_sink, name=f"swa_fwd_{l}")
    mixed = jnp.concatenate([o_a, o_b, o_c], axis=1).astype(_MXU_DTYPE)
    y = _mm(mixed, ws.get("w_out", mixed), tm=1024, tn=512, after=ws.tokens(), name=f"mix_out_{l}")
    z1, x1, x1b = _ln_fwd(x, y, ln1_g, ln1_b, name=f"ln1_fwd_{l}")
    g = _mm(x1b, ws.get("w_gate", x1b), tm=1024, tn=512, name=f"ffn_gate_{l}")
    u = _mm(x1b, ws.get("w_up", x1b), tm=1024, tn=512, name=f"ffn_up_{l}")
    hb = _conv_gate_fwd(g, u, ws.get("conv_w", u), conv_b, name=f"conv_gate_fwd_{l}")
    ws.at("conv", hb)
    y2 = _mm(hb, ws.get("w_down", hb), tm=512, tn=512, after=ws.tokens(), name=f"ffn_down_{l}")
    z2, x2, x2b = _ln_fwd(x1, y2, ln2_g, ln2_b, name=f"ln2_fwd_{l}")
    res = dict(xb=xb, proj=proj, qkv=qkv, raw=raw, states=states, o_b=o_b, lse_b=lse_b, o_c=o_c, lse_c=lse_c,
               mixed=mixed, z1=z1, x1b=x1b, g=g, u=u, hb=hb, z2=z2)
    return x2, x2b, res


class _GradExchange:
    def __init__(self, core, chip):
        self.core, self.chip, self.groups, self._tokens = core, chip, [], []

    def launch(self, names, slabs, l, tag):
        mine = [s.reshape((N_CHIP, 2) + s.shape[1:]) for s in slabs]
        theirs = _sibling_swap(mine, name=f"swap_grads_{tag}")
        sums = [_pair_add(a, b, self.core, name=f"pair_add_{n}_{l}") for n, a, b in zip(names, mine, theirs)]
        landings = [lax.empty(s.shape, s.dtype) for s in sums]
        send, recv, sums, landings, token = _chip_exchange_start(sums, landings, name=f"exchange_start_{tag}")
        self.groups.append((names, l, tag, send, recv, sums, landings))
        self._tokens.append(token)

    def tokens(self):
        out, self._tokens = self._tokens, []
        return out

    def finish(self, weights, mom1, mom2, after):
        out = {}
        after = list(after) + self.tokens()
        for names, l, tag, send, recv, sums, landings in self.groups:
            sums, landings = _chip_exchange_wait(sums, landings, send, recv, after[-1], name=f"exchange_wait_{tag}")
            for n, s, landed in zip(names, sums, landings):
                out[n, l] = _adamw_reduce(landed, s, self.chip, weights[n][l], mom1[n][l], mom2[n][l], name=f"adamw_{n}_{l}")
                after = [out[n, l][0]]
        return out


def _layer_bwd(dx2, res, w, lb_logits, a_norm_w, c_sink, ln1_g, conv_b, ln2_g, tabs, exchange, l):
    dz2, dz2b, d_ln2_g, d_ln2_b = _ln_bwd(res["z2"], dx2, None, ln2_g, name=f"ln2_bwd_{l}")
    dh = _mm(dz2b, w["w_down"], tb=True, tm=1024, tn=512, after=exchange.tokens(), name=f"ffn_down_dx_{l}")
    d_w_down = _mm(res["hb"], dz2b, ta=True, tm=512, tn=512, out_dtype=_GRAD_DTYPE, name=f"ffn_down_dw_{l}")
    dg, du, d_conv_w, d_conv_b = _conv_gate_bwd(dh, res["g"], res["u"], w["conv_w"], conv_b, name=f"conv_gate_bwd_{l}")
    t = _mm(dg, w["w_gate"], tb=True, tm=512, tn=512, name=f"ffn_gate_dx_{l}")
    dx1 = _mm(du, w["w_up"], tb=True, tm=512, tn=512, add=t, name=f"ffn_up_dx_{l}")
    d_w_gate = _mm_tn_slabs(res["x1b"], _to_slabs(dg), tm=1024, name=f"ffn_gate_dw_{l}")
    d_w_up = _mm_tn_slabs(res["x1b"], _to_slabs(du), tm=1024, name=f"ffn_up_dw_{l}")
    exchange.launch(("w_down", "w_gate", "w_up"), [d_w_down.reshape(N_DEV, D_FF // N_DEV, D_MODEL), d_w_gate, d_w_up], l, f"ffn_{l}")
    dz1, dz1b, d_ln1_g, d_ln1_b = _ln_bwd(res["z1"], dx1, dz2, ln1_g, name=f"ln1_bwd_{l}")
    dmixed = _mm(dz1b, w["w_out"], tb=True, tm=1024, tn=512, after=exchange.tokens(), name=f"mix_out_dx_{l}")
    d_w_out = _mm(res["mixed"], dz1b, ta=True, tm=1024, tn=512, out_dtype=_GRAD_DTYPE, name=f"mix_out_dw_{l}")
    dq_a, df_a, di_a, dg_a, d_norm_w, d_lb = _hgrn_bwd(res["proj"], lb_logits, a_norm_w, res["raw"], res["states"],
                                                      dmixed, l, name=f"hgrn_bwd_{l}")
    grads = []
    for d in DILATIONS:
        grads += list(_attn_bwd(res["qkv"], dmixed, res["o_b"], res["lse_b"], d, heads=B_HEADS, q0=QB0, k0=KB0, v0=VB0,
                                kv_group=1, lag_off=0, do0=A_HEADS, sink=None, name=f"dilated_bwd_{d}_{l}"))
    dq_c, dk_c, dv_c, d_sink = _attn_bwd(res["qkv"], dmixed, res["o_c"], res["lse_c"], 1, heads=C_HEADS, q0=QC0, k0=KC0,
                                         v0=VC0, kv_group=C_HEADS // C_KV_HEADS, lag_off=1, do0=A_HEADS + B_HEADS,
                                         sink=c_sink, name=f"swa_bwd_{l}")
    dqkv = _qkv_grad_finish(*grads, dq_c, dk_c, dv_c, tabs, name=f"qkv_grad_{l}")
    dproj = jnp.concatenate([dq_a, df_a, di_a, dg_a, dqkv], axis=1)
    dx = _mm(dproj, w["w_in"], tb=True, tm=512, tn=512, add=dz1, add_scale=ALPHA, name=f"proj_dx_{l}")
    d_w_in = _mm_tn_slabs(res["xb"], _to_slabs(dproj), tm=1024, name=f"proj_dw_{l}")
    exchange.launch(("w_out", "w_in"), [d_w_out.reshape(N_DEV, D_MODEL // N_DEV, D_MODEL), d_w_in], l, f"mix_{l}")
    small = [d_lb, d_norm_w, jnp.pad(d_sink, (0, _PACK_LANES - C_HEADS)), d_ln1_g, d_ln1_b, d_ln2_g, d_ln2_b, d_conv_b,
             d_conv_w]
    return dx, small


def kernel(x, w_in, lb_logits, a_norm_w, c_sinks, w_out, ln1_g, ln1_b, w_gate, w_up, conv_w, conv_b, w_down, ln2_g, ln2_b, loss_target, m_w_in, m_lb_logits, m_a_norm_w, m_c_sinks, m_w_out, m_ln1_g, m_ln1_b, m_w_gate, m_w_up, m_conv_w, m_conv_b, m_w_down, m_ln2_g, m_ln2_b, v_w_in, v_lb_logits, v_a_norm_w, v_c_sinks, v_w_out, v_ln1_g, v_ln1_b, v_w_gate, v_w_up, v_conv_w, v_conv_b, v_w_down, v_ln2_g, v_ln2_b):
    weights = dict(w_in=w_in, lb_logits=lb_logits, a_norm_w=a_norm_w, c_sinks=c_sinks, w_out=w_out, ln1_g=ln1_g, ln1_b=ln1_b,
                   w_gate=w_gate, w_up=w_up, conv_w=conv_w, conv_b=conv_b, w_down=w_down, ln2_g=ln2_g, ln2_b=ln2_b)
    mom1 = dict(w_in=m_w_in, lb_logits=m_lb_logits, a_norm_w=m_a_norm_w, c_sinks=m_c_sinks, w_out=m_w_out, ln1_g=m_ln1_g,
                ln1_b=m_ln1_b, w_gate=m_w_gate, w_up=m_w_up, conv_w=m_conv_w, conv_b=m_conv_b, w_down=m_w_down, ln2_g=m_ln2_g,
                ln2_b=m_ln2_b)
    mom2 = dict(w_in=v_w_in, lb_logits=v_lb_logits, a_norm_w=v_a_norm_w, c_sinks=v_c_sinks, w_out=v_w_out, ln1_g=v_ln1_g,
                ln1_b=v_ln1_b, w_gate=v_w_gate, w_up=v_w_up, conv_w=v_conv_w, conv_b=v_conv_b, w_down=v_w_down, ln2_g=v_ln2_g,
                ln2_b=v_ln2_b)
    core = lax.axis_index("c").astype(jnp.int32)
    me = 4 * lax.axis_index("x") + 2 * lax.axis_index("y") + core
    tabs = _rope_tables()

    chip = (2 * lax.axis_index("x") + lax.axis_index("y")).astype(jnp.int32)

    def block(n, l):
        return conv_w[l] if n == "conv_w" else weights[n][l].astype(_MXU_DTYPE)

    first, = _gather_blocks([block("w_in", 0)], name="gather_w_in_0")
    rest0 = ("w_out", "w_gate", "w_up", "conv_w", "w_down")
    all1 = ("w_in",) + rest0
    gather0 = _WeightGather(rest0, [block(n, 0) for n in rest0], me, "rest_0")
    gather1 = _WeightGather(all1, [block(n, 1) for n in all1], me, "all_1")
    layer_ws = [_LayerWeights({"w_in": _full_weight("w_in", first)}, gather0, [("dilated", gather0), ("conv", gather1)],
                              [gather0.token, gather1.token]),
                _LayerWeights({}, gather1)]

    xs = x[0]
    xb = xs.astype(_MXU_DTYPE)
    saved = []
    for l in range(DEPTH):
        xs, xb, res = _layer_fwd(xs, xb, layer_ws[l], lb_logits, a_norm_w[l], c_sinks[l], ln1_g[l], ln1_b[l], conv_b[l],
                                 ln2_g[l], ln2_b[l], tabs, l)
        saved.append(res)
    loss_part, dx = _loss_head(xs, loss_target[0], name="loss_head")
    loss = lax.psum(loss_part, ("x", "y", "c"))

    exchange = _GradExchange(core, chip)
    small_parts = [None] * DEPTH
    for l in reversed(range(DEPTH)):
        dx, small = _layer_bwd(dx, saved[l], layer_ws[l].ready, lb_logits, a_norm_w[l], c_sinks[l], ln1_g[l], conv_b[l],
                               ln2_g[l], tabs, exchange, l)
        small_parts[l] = _pack(small, _LAYER_ROWS)
    updated = exchange.finish(weights, mom1, mom2, [dx])
    big_out = {n: [updated[n, l] for l in range(DEPTH)] for n in _BIG}
    gathered, = _exchange([jnp.concatenate(small_parts, axis=0)], False, name="gather_small_grads")
    g_small = _small_reduce(gathered, lb_logits, name="small_grads")

    per_layer = [(A_HEADS * HEAD_DIM,), (HEAD_DIM,), (_PACK_LANES,), (D_MODEL,), (D_MODEL,), (D_MODEL,), (D_MODEL,), (D_FF,),
                 (3, D_FF)]
    names = ("lb_logits", "a_norm_w", "c_sinks", "ln1_g", "ln1_b", "ln2_g", "ln2_b", "conv_b", "conv_w")
    grads = {n: [] for n in names}
    for l in range(DEPTH):
        for n, t in zip(names, _unpack(g_small[l * _LAYER_ROWS:(l + 1) * _LAYER_ROWS], per_layer)):
            grads[n].append(t)
    grads = {n: jnp.stack(t) for n, t in grads.items()}
    grads["c_sinks"] = grads["c_sinks"][:, :C_HEADS]
    grads["conv_w"] = lax.dynamic_slice_in_dim(grads["conv_w"], me * SHARD_COLS, SHARD_COLS, axis=2)
    shapes = [grads[n].shape for n in names]
    rows = -(-sum(int(np.prod(s)) for s in shapes) // (8 * _PACK_LANES)) * 8
    d_s, m_s, v_s = _adamw_small(_pack([grads[n] for n in names], rows), _pack([weights[n] for n in names], rows),
                                 _pack([mom1[n] for n in names], rows), _pack([mom2[n] for n in names], rows),
                                 name="adamw_small")
    delta = dict(zip(names, _unpack(d_s, shapes)))
    new_m = dict(zip(names, _unpack(m_s, shapes)))
    new_v = dict(zip(names, _unpack(v_s, shapes)))
    for n in _BIG:
        g, d, nm, nv = (jnp.stack(t) for t in zip(*big_out[n]))
        grads[n], delta[n], new_m[n], new_v[n] = g, d, nm, nv

    order = ("w_in", "lb_logits", "a_norm_w", "c_sinks", "w_out", "ln1_g", "ln1_b", "w_gate", "w_up", "conv_w", "conv_b",
             "w_down", "ln2_g", "ln2_b")
    return (loss, dx[None], *[grads[n] for n in order], *[delta[n] for n in order], *[new_m[n] for n in order],
            *[new_v[n] for n in order])
```

```python
import functools

import jax
import jax.numpy as jnp
import numpy as np
from jax import lax
from jax.experimental import pallas as pl
from jax.experimental.pallas import tpu as pltpu

D_MODEL = 2048
SEQ = 2048
DEPTH = 2
HEAD_DIM = 128
A_HEADS = 4
B_HEADS = 6
C_HEADS = 6
C_KV_HEADS = 2
A_CHUNK = 16
DILATIONS = (1, 4, 16)
BLOCK = 128
ROPE_THETA = 500000.0
ROPE_DIM = 32
D_FF = 5632
IN_WIDTH = 5632
LN_EPS = 1e-5
ALPHA = (2 * DEPTH) ** 0.25
N_DEV = 8
SHARD_COLS = IN_WIDTH // N_DEV

ADAM_LR = 0.001
ADAM_B1 = 0.9
ADAM_B2 = 0.999
ADAM_EPS = 1e-08
ADAM_WD = 0.01
ADAM_STEP = 10

A_COLS = 16
QKV_COLS = 28
QB0, KB0, VB0, QC0, KC0, VC0 = 0, 6, 12, 18, 24, 26

_MXU_DTYPE = jnp.bfloat16
_GRAD_DTYPE = jnp.bfloat16
_NEG = -1e30
_VMEM_LIMIT = 56 * 2 ** 20

_F32 = jnp.float32


def _sigmoid(x):
    return 1.0 / (1.0 + jnp.exp(-x))


def _cparams(**kw):
    return pltpu.CompilerParams(vmem_limit_bytes=_VMEM_LIMIT, **kw)


def _mm(a, b, *, ta=False, tb=False, tm, tn, out_dtype=_F32, add=None, add_scale=1.0, after=(), name):
    K = a.shape[0] if ta else a.shape[1]
    M = a.shape[1] if ta else a.shape[0]
    N = b.shape[0] if tb else b.shape[1]
    assert (b.shape[1] if tb else b.shape[0]) == K and M % tm == 0 and N % tn == 0
    dn = (((0 if ta else 1,), (1 if tb else 0,)), ((), ()))

    def body(*refs):
        a_ref, b_ref = refs[:2]
        o_ref = refs[-1]
        r = lax.dot_general(a_ref[...], b_ref[...], dn, preferred_element_type=_F32)
        if add is not None:
            r = r + add_scale * refs[2][...]
        o_ref[...] = r.astype(o_ref.dtype)

    a_spec = pl.BlockSpec((K, tm), lambda i, j: (0, i)) if ta else pl.BlockSpec((tm, K), lambda i, j: (i, 0))
    b_spec = pl.BlockSpec((tn, K), lambda i, j: (j, 0)) if tb else pl.BlockSpec((K, tn), lambda i, j: (0, j))
    o_spec = pl.BlockSpec((tm, tn), lambda i, j: (i, j))
    in_specs = [a_spec, b_spec] + ([o_spec] if add is not None else []) + [pl.BlockSpec(memory_space=pl.ANY)] * len(after)
    args = (a, b) + ((add,) if add is not None else ()) + tuple(after)
    return pl.pallas_call(
        body, grid=(M // tm, N // tn), in_specs=in_specs, out_specs=o_spec,
        out_shape=jax.ShapeDtypeStruct((M, N), out_dtype), name=name,
        compiler_params=_cparams(dimension_semantics=("parallel", "parallel")),
    )(*args)


_PAIR = 2 * SHARD_COLS


def _mm_tn_slabs(a, b, *, tm, name):
    K, M = a.shape
    assert b.shape == (K, N_DEV * SHARD_COLS) and M % tm == 0

    def body(a_ref, b_ref, o_ref):
        a_blk = a_ref[...]
        for s in range(2):
            o_ref[s] = lax.dot_general(a_blk, b_ref[:, s * SHARD_COLS:(s + 1) * SHARD_COLS], _TN,
                                       preferred_element_type=_F32).astype(o_ref.dtype)

    return pl.pallas_call(
        body, grid=(M // tm, N_DEV // 2),
        in_specs=[pl.BlockSpec((K, tm), lambda i, p: (0, i)), pl.BlockSpec((K, _PAIR), lambda i, p: (0, p))],
        out_specs=pl.BlockSpec((2, tm, SHARD_COLS), lambda i, p: (p, i, 0)),
        out_shape=jax.ShapeDtypeStruct((N_DEV, M, SHARD_COLS), _GRAD_DTYPE), name=name,
        compiler_params=_cparams(dimension_semantics=("parallel", "parallel")),
    )(a, b)


def _mm_w_slabs(a, w, *, tm, after=(), name):
    M, K = a.shape
    assert w.shape == (N_DEV, K, SHARD_COLS) and M % tm == 0

    def body(a_ref, w_ref, *rest):
        o_ref = rest[-1]
        a_blk = a_ref[...]
        for s in range(2):
            o_ref[:, s * SHARD_COLS:(s + 1) * SHARD_COLS] = jnp.dot(a_blk, w_ref[s], preferred_element_type=_F32)

    return pl.pallas_call(
        body, grid=(M // tm, N_DEV // 2),
        in_specs=[pl.BlockSpec((tm, K), lambda i, p: (i, 0)), pl.BlockSpec((2, K, SHARD_COLS), lambda i, p: (p, 0, 0))]
        + [pl.BlockSpec(memory_space=pl.ANY)] * len(after),
        out_specs=pl.BlockSpec((tm, _PAIR), lambda i, p: (i, p)),
        out_shape=jax.ShapeDtypeStruct((M, N_DEV * SHARD_COLS), _F32), name=name,
        compiler_params=_cparams(dimension_semantics=("parallel", "parallel")),
    )(a, w, *after)


def _mm_nt_w_slabs(a, w, *, tm, tn, add=None, add_scale=1.0, after=(), name):
    M = a.shape[0]
    N = w.shape[1]
    assert a.shape[1] == N_DEV * SHARD_COLS and w.shape[0] == N_DEV and M % tm == 0 and N % tn == 0

    def body(a_ref, w_ref, *rest):
        o_ref = rest[-1]
        acc = add_scale * rest[0][...] if add is not None else None
        for j in range(N_DEV):
            t = lax.dot_general(a_ref[:, j * SHARD_COLS:(j + 1) * SHARD_COLS], w_ref[j], _NT, preferred_element_type=_F32)
            acc = t if acc is None else acc + t
        o_ref[...] = acc

    o_spec = pl.BlockSpec((tm, tn), lambda i, j: (i, j))
    return pl.pallas_call(
        body, grid=(M // tm, N // tn),
        in_specs=[pl.BlockSpec((tm, N_DEV * SHARD_COLS), lambda i, j: (i, 0)),
                  pl.BlockSpec((N_DEV, tn, SHARD_COLS), lambda i, j: (0, j, 0))]
        + ([o_spec] if add is not None else []) + [pl.BlockSpec(memory_space=pl.ANY)] * len(after),
        out_specs=o_spec, out_shape=jax.ShapeDtypeStruct((M, N), _F32), name=name,
        compiler_params=_cparams(dimension_semantics=("parallel", "parallel")),
    )(a, w, *((add,) if add is not None else ()), *after)


def _ln_fwd(x, y, g, b, *, name):
    tm = 256

    def body(x_ref, y_ref, g_ref, b_ref, z_ref, o_ref, ob_ref):
        z = ALPHA * x_ref[...] + y_ref[...]
        mu = jnp.mean(z, axis=-1, keepdims=True)
        zc = z - mu
        var = jnp.mean(zc * zc, axis=-1, keepdims=True)
        o = zc * lax.rsqrt(var + LN_EPS) * g_ref[...] + b_ref[...]
        z_ref[...] = z
        o_ref[...] = o
        ob_ref[...] = o.astype(ob_ref.dtype)

    row = pl.BlockSpec((tm, D_MODEL), lambda i: (i, 0))
    vec = pl.BlockSpec((1, D_MODEL), lambda i: (0, 0))
    return pl.pallas_call(
        body, grid=(SEQ // tm,), in_specs=[row, row, vec, vec], out_specs=[row, row, row],
        out_shape=[jax.ShapeDtypeStruct((SEQ, D_MODEL), _F32), jax.ShapeDtypeStruct((SEQ, D_MODEL), _F32),
                   jax.ShapeDtypeStruct((SEQ, D_MODEL), _MXU_DTYPE)],
        name=name, compiler_params=_cparams(dimension_semantics=("parallel",)),
    )(x, y, g.reshape(1, D_MODEL), b.reshape(1, D_MODEL))


def _ln_bwd(z, d_a, d_res, g, *, name):
    tm = 256

    def body(*refs):
        if d_res is None:
            z_ref, da_ref, g_ref, dz_ref, dzb_ref, dg_ref, db_ref = refs
            dout = da_ref[...]
        else:
            z_ref, da_ref, dr_ref, g_ref, dz_ref, dzb_ref, dg_ref, db_ref = refs
            dout = da_ref[...] + ALPHA * dr_ref[...]
        z = z_ref[...]
        mu = jnp.mean(z, axis=-1, keepdims=True)
        zc = z - mu
        var = jnp.mean(zc * zc, axis=-1, keepdims=True)
        rstd = lax.rsqrt(var + LN_EPS)
        xh = zc * rstd
        dxh = dout * g_ref[...]
        m1 = jnp.mean(dxh, axis=-1, keepdims=True)
        m2 = jnp.mean(dxh * xh, axis=-1, keepdims=True)
        dz = rstd * (dxh - m1 - xh * m2)
        dz_ref[...] = dz
        dzb_ref[...] = dz.astype(dzb_ref.dtype)

        @pl.when(pl.program_id(0) == 0)
        def _():
            dg_ref[...] = jnp.zeros_like(dg_ref)
            db_ref[...] = jnp.zeros_like(db_ref)

        dg_ref[0:1, :] += jnp.sum(dout * xh, axis=0, keepdims=True)
        db_ref[0:1, :] += jnp.sum(dout, axis=0, keepdims=True)

    row = pl.BlockSpec((tm, D_MODEL), lambda i: (i, 0))
    vec = pl.BlockSpec((1, D_MODEL), lambda i: (0, 0))
    acc = pl.BlockSpec((8, D_MODEL), lambda i: (0, 0))
    ins = [z, d_a] + ([d_res] if d_res is not None else []) + [g.reshape(1, D_MODEL)]
    in_specs = [row, row] + ([row] if d_res is not None else []) + [vec]
    dz, dzb, dg, db = pl.pallas_call(
        body, grid=(SEQ // tm,), in_specs=in_specs, out_specs=[row, row, acc, acc],
        out_shape=[jax.ShapeDtypeStruct((SEQ, D_MODEL), _F32), jax.ShapeDtypeStruct((SEQ, D_MODEL), _MXU_DTYPE),
                   jax.ShapeDtypeStruct((8, D_MODEL), _F32), jax.ShapeDtypeStruct((8, D_MODEL), _F32)],
        name=name, compiler_params=_cparams(dimension_semantics=("arbitrary",)),
    )(*ins)
    return dz, dzb, dg[0], db[0]


def _loss_head(y, target, *, name):
    tm = 256

    def body(y_ref, t_ref, d_ref, l_ref):
        e = y_ref[...] - t_ref[...]
        d_ref[...] = e * (1.0 / D_MODEL)

        @pl.when(pl.program_id(0) == 0)
        def _():
            l_ref[...] = jnp.zeros_like(l_ref)

        l_ref[...] += (0.5 / D_MODEL) * jnp.sum(e * e)

    row = pl.BlockSpec((tm, D_MODEL), lambda i: (i, 0))
    d, l = pl.pallas_call(
        body, grid=(SEQ // tm,), in_specs=[row, row], out_specs=[row, pl.BlockSpec((8, 128), lambda i: (0, 0))],
        out_shape=[jax.ShapeDtypeStruct((SEQ, D_MODEL), _F32), jax.ShapeDtypeStruct((8, 128), _F32)],
        name=name, compiler_params=_cparams(dimension_semantics=("arbitrary",)),
    )(y, target)
    return l[0, 0], d


_CONV_TN = 256


def _shift_down(v, k, rows):
    return jnp.where(rows >= k, pltpu.roll(v, k, axis=0), 0.0)


def _shift_up(v, k, rows):
    return jnp.where(rows < SEQ - k, pltpu.roll(v, SEQ - k, axis=0), 0.0)


def _conv_gate_fwd(g, u, conv_w, conv_b, *, name):
    def body(g_ref, u_ref, w_ref, b_ref, h_ref):
        gv = g_ref[...]
        rows = lax.broadcasted_iota(jnp.int32, gv.shape, 0)
        w = w_ref[...]
        gc = b_ref[...] + w[2:3, :] * gv + w[1:2, :] * _shift_down(gv, 1, rows) + w[0:1, :] * _shift_down(gv, 2, rows)
        h_ref[...] = (gc * _sigmoid(gc) * u_ref[...]).astype(h_ref.dtype)

    col = pl.BlockSpec((SEQ, _CONV_TN), lambda j: (0, j))
    return pl.pallas_call(
        body, grid=(D_FF // _CONV_TN,),
        in_specs=[col, col, pl.BlockSpec((3, _CONV_TN), lambda j: (0, j)), pl.BlockSpec((1, _CONV_TN), lambda j: (0, j))],
        out_specs=col, out_shape=jax.ShapeDtypeStruct((SEQ, D_FF), _MXU_DTYPE), name=name,
        compiler_params=_cparams(dimension_semantics=("parallel",)),
    )(g, u, conv_w, conv_b.reshape(1, D_FF))


def _conv_gate_bwd(dh, g, u, conv_w, conv_b, *, name):
    def body(dh_ref, g_ref, u_ref, w_ref, b_ref, dg_ref, du_ref, dw_ref, db_ref):
        gv = g_ref[...]
        rows = lax.broadcasted_iota(jnp.int32, gv.shape, 0)
        w = w_ref[...]
        g1 = _shift_down(gv, 1, rows)
        g2 = _shift_down(gv, 2, rows)
        gc = b_ref[...] + w[2:3, :] * gv + w[1:2, :] * g1 + w[0:1, :] * g2
        sg = _sigmoid(gc)
        dh = dh_ref[...]
        du_ref[...] = (dh * (gc * sg)).astype(du_ref.dtype)
        dgc = dh * u_ref[...] * (sg * (1.0 + gc * (1.0 - sg)))
        dg = w[2:3, :] * dgc + w[1:2, :] * _shift_up(dgc, 1, rows) + w[0:1, :] * _shift_up(dgc, 2, rows)
        dg_ref[...] = dg.astype(dg_ref.dtype)
        dw_ref[0:1, :] = jnp.sum(dgc * g2, axis=0, keepdims=True)
        dw_ref[1:2, :] = jnp.sum(dgc * g1, axis=0, keepdims=True)
        dw_ref[2:3, :] = jnp.sum(dgc * gv, axis=0, keepdims=True)
        db_ref[...] = jnp.sum(dgc, axis=0, keepdims=True)

    col = pl.BlockSpec((SEQ, _CONV_TN), lambda j: (0, j))
    w3 = pl.BlockSpec((3, _CONV_TN), lambda j: (0, j))
    w1 = pl.BlockSpec((1, _CONV_TN), lambda j: (0, j))
    dg, du, dw, db = pl.pallas_call(
        body, grid=(D_FF // _CONV_TN,), in_specs=[col, col, col, w3, w1], out_specs=[col, col, w3, w1],
        out_shape=[jax.ShapeDtypeStruct((SEQ, D_FF), _MXU_DTYPE), jax.ShapeDtypeStruct((SEQ, D_FF), _MXU_DTYPE),
                   jax.ShapeDtypeStruct((3, D_FF), _F32), jax.ShapeDtypeStruct((1, D_FF), _F32)],
        name=name, compiler_params=_cparams(dimension_semantics=("parallel",)),
    )(dh, g, u, conv_w, conv_b.reshape(1, D_FF))
    return dg, du, dw, db[0]


def _rope_tables():
    half = ROPE_DIM // 2
    inv = ROPE_THETA ** (-jnp.arange(0, ROPE_DIM, 2, dtype=_F32) / ROPE_DIM)
    ang = jnp.arange(SEQ, dtype=_F32)[:, None] * inv[None, :]
    cos, sin = jnp.cos(ang), jnp.sin(ang)
    rest = HEAD_DIM - ROPE_DIM
    c = jnp.concatenate([cos, cos, jnp.ones((SEQ, rest), _F32)], axis=1)
    s1 = jnp.concatenate([-sin, jnp.zeros((SEQ, HEAD_DIM - half), _F32)], axis=1)
    s2 = jnp.concatenate([jnp.zeros((SEQ, half), _F32), sin, jnp.zeros((SEQ, rest), _F32)], axis=1)
    return c, s1, s2


def _rope_apply(x, c, s1, s2):
    return x * c + pltpu.roll(x, HEAD_DIM - ROPE_DIM // 2, axis=1) * s1 + pltpu.roll(x, ROPE_DIM // 2, axis=1) * s2


def _rope_transpose(d, c, s1, s2):
    half = ROPE_DIM // 2
    return d * c + pltpu.roll(d * s1, half, axis=1) + pltpu.roll(d * s2, HEAD_DIM - half, axis=1)


def _is_rope_block(j):
    return (j < VB0) | ((j >= QC0) & (j < VC0))


def _qkv_prep(proj, tabs, *, name):
    tm = 512

    def body(p_ref, c_ref, s1_ref, s2_ref, o_ref):
        j = pl.program_id(1)
        x = p_ref[...]
        r = _rope_apply(x, c_ref[...], s1_ref[...], s2_ref[...])
        o_ref[...] = jnp.where(_is_rope_block(j), r, x).astype(o_ref.dtype)

    tab = pl.BlockSpec((tm, HEAD_DIM), lambda i, j: (i, 0))
    return pl.pallas_call(
        body, grid=(SEQ // tm, QKV_COLS),
        in_specs=[pl.BlockSpec((tm, HEAD_DIM), lambda i, j: (i, A_COLS + j)), tab, tab, tab],
        out_specs=pl.BlockSpec((tm, HEAD_DIM), lambda i, j: (i, j)),
        out_shape=jax.ShapeDtypeStruct((SEQ, QKV_COLS * HEAD_DIM), _MXU_DTYPE), name=name,
        compiler_params=_cparams(dimension_semantics=("parallel", "parallel")),
    )(proj, *tabs)


def _qkv_grad_finish(dq1, dk1, dv1, dq4, dk4, dv4, dq16, dk16, dv16, dqc, dkc, dvc, tabs, *, name):
    tm = 512

    def body(a1, a4, a16, k1, k4, k16, v1, v4, v16, qc, kc, vc, c_ref, s1_ref, s2_ref, o_ref):
        c, s1, s2 = c_ref[...], s1_ref[...], s2_ref[...]
        for h in range(B_HEADS):
            sl = slice(h * HEAD_DIM, (h + 1) * HEAD_DIM)
            o_ref[:, (QB0 + h) * HEAD_DIM:(QB0 + h + 1) * HEAD_DIM] = _rope_transpose(
                a1[:, sl] + a4[:, sl] + a16[:, sl], c, s1, s2).astype(o_ref.dtype)
            o_ref[:, (KB0 + h) * HEAD_DIM:(KB0 + h + 1) * HEAD_DIM] = _rope_transpose(
                k1[:, sl] + k4[:, sl] + k16[:, sl], c, s1, s2).astype(o_ref.dtype)
            o_ref[:, (VB0 + h) * HEAD_DIM:(VB0 + h + 1) * HEAD_DIM] = (v1[:, sl] + v4[:, sl] + v16[:, sl]).astype(o_ref.dtype)
            o_ref[:, (QC0 + h) * HEAD_DIM:(QC0 + h + 1) * HEAD_DIM] = _rope_transpose(qc[:, sl], c, s1, s2).astype(o_ref.dtype)
        for h in range(C_KV_HEADS):
            sl = slice(h * HEAD_DIM, (h + 1) * HEAD_DIM)
            o_ref[:, (KC0 + h) * HEAD_DIM:(KC0 + h + 1) * HEAD_DIM] = _rope_transpose(kc[:, sl], c, s1, s2).astype(o_ref.dtype)
            o_ref[:, (VC0 + h) * HEAD_DIM:(VC0 + h + 1) * HEAD_DIM] = vc[:, sl].astype(o_ref.dtype)

    wb = pl.BlockSpec((tm, B_HEADS * HEAD_DIM), lambda i: (i, 0))
    wkv = pl.BlockSpec((tm, C_KV_HEADS * HEAD_DIM), lambda i: (i, 0))
    tab = pl.BlockSpec((tm, HEAD_DIM), lambda i: (i, 0))
    return pl.pallas_call(
        body, grid=(SEQ // tm,), in_specs=[wb] * 10 + [wkv, wkv, tab, tab, tab],
        out_specs=pl.BlockSpec((tm, QKV_COLS * HEAD_DIM), lambda i: (i, 0)),
        out_shape=jax.ShapeDtypeStruct((SEQ, QKV_COLS * HEAD_DIM), _MXU_DTYPE), name=name,
        compiler_params=_cparams(dimension_semantics=("parallel",)),
    )(dq1, dq4, dq16, dk1, dk4, dk16, dv1, dv4, dv16, dqc, dkc, dvc, *tabs)


_NT = (((1,), (1,)), ((), ()))
_TN = (((0,), (0,)), ((), ()))
_SCALE = HEAD_DIM ** -0.5


def _band_scores(q, kp, kc, n, lag_off):
    sp = lax.dot_general(q, kp, _NT, preferred_element_type=_F32) * _SCALE
    sc = lax.dot_general(q, kc, _NT, preferred_element_type=_F32) * _SCALE
    row = lax.broadcasted_iota(jnp.int32, (BLOCK, BLOCK), 0)
    col = lax.broadcasted_iota(jnp.int32, (BLOCK, BLOCK), 1)
    sp = jnp.where((col >= row + lag_off) & (n > 0), sp, _NEG)
    sc = jnp.where(col <= row, sc, _NEG)
    return sp, sc


def _attn_specs(dil, q0, k0, v0, kv_group):
    def q_map(h, r, n):
        return (n, r * QKV_COLS + q0 + h)

    def kv_map(base, prev):
        def f(h, r, n):
            return (jnp.maximum(n - 1, 0) if prev else n, r * QKV_COLS + base + h // kv_group)
        return f

    blk = (BLOCK, HEAD_DIM)
    return [pl.BlockSpec(blk, q_map), pl.BlockSpec(blk, kv_map(k0, True)), pl.BlockSpec(blk, kv_map(k0, False)),
            pl.BlockSpec(blk, kv_map(v0, True)), pl.BlockSpec(blk, kv_map(v0, False))]


def _attn_fwd(qkv, dil, *, heads, q0, k0, v0, kv_group, lag_off, sink, name):
    L = SEQ // dil
    nb = L // BLOCK
    qkv_r = qkv.reshape(L, dil * QKV_COLS * HEAD_DIM)

    def body(*refs):
        if sink is None:
            q_ref, kp_ref, kc_ref, vp_ref, vc_ref, o_ref, lse_ref = refs
        else:
            q_ref, kp_ref, kc_ref, vp_ref, vc_ref, sk_ref, o_ref, lse_ref = refs
        n = pl.program_id(2)
        sp, sc = _band_scores(q_ref[...], kp_ref[...], kc_ref[...], n, lag_off)
        m = jnp.maximum(jnp.max(sp, axis=1, keepdims=True), jnp.max(sc, axis=1, keepdims=True))
        if sink is not None:
            sk = sk_ref[0][:, 0:1]
            m = jnp.maximum(m, sk)
        pp = jnp.exp(sp - m)
        pc = jnp.exp(sc - m)
        den = jnp.sum(pp, axis=1, keepdims=True) + jnp.sum(pc, axis=1, keepdims=True)
        if sink is not None:
            den = den + jnp.exp(sk - m)
        acc = jnp.dot(pp.astype(_MXU_DTYPE), vp_ref[...], preferred_element_type=_F32)
        acc = acc + jnp.dot(pc.astype(_MXU_DTYPE), vc_ref[...], preferred_element_type=_F32)
        o_ref[...] = acc / den
        lse_ref[...] = jnp.broadcast_to(m + jnp.log(den), (BLOCK, HEAD_DIM))

    in_specs = _attn_specs(dil, q0, k0, v0, kv_group)
    args = [qkv_r] * 5
    if sink is not None:
        in_specs.append(pl.BlockSpec((1, 1, HEAD_DIM), lambda h, r, n: (h, 0, 0)))
        args.append(jnp.broadcast_to(sink.reshape(heads, 1, 1), (heads, 1, HEAD_DIM)))
    o_spec = pl.BlockSpec((BLOCK, HEAD_DIM), lambda h, r, n: (n, r * heads + h))
    shape = jax.ShapeDtypeStruct((L, dil * heads * HEAD_DIM), _F32)
    o, lse = pl.pallas_call(
        body, grid=(heads, dil, nb), in_specs=in_specs, out_specs=[o_spec, o_spec], out_shape=[shape, shape],
        name=name, compiler_params=_cparams(dimension_semantics=("parallel", "parallel", "parallel")),
    )(*args)
    return o.reshape(SEQ, heads * HEAD_DIM), lse.reshape(SEQ, heads * HEAD_DIM)


def _attn_bwd(qkv, dmixed, o, lse, dil, *, heads, q0, k0, v0, kv_group, lag_off, do0, sink, name):
    L = SEQ // dil
    nb = L // BLOCK
    kvh = heads // kv_group
    qkv_r = qkv.reshape(L, dil * QKV_COLS * HEAD_DIM)
    dm_r = dmixed.reshape(L, dil * D_MODEL)
    o_r = o.reshape(L, dil * heads * HEAD_DIM)
    lse_r = lse.reshape(L, dil * heads * HEAD_DIM)

    def body(*refs):
        if sink is None:
            q_ref, kp_ref, kc_ref, vp_ref, vc_ref, do_ref, o_ref, lse_ref, dq_ref, dk_ref, dv_ref = refs
        else:
            q_ref, kp_ref, kc_ref, vp_ref, vc_ref, do_ref, o_ref, lse_ref, sk_ref, dq_ref, dk_ref, dv_ref, dsk_ref = refs
        g = pl.program_id(0)
        n = pl.program_id(2)
        q, kp, kc, vp, vc = q_ref[...], kp_ref[...], kc_ref[...], vp_ref[...], vc_ref[...]
        do = do_ref[...]
        delta = jnp.sum(do * o_ref[...], axis=1, keepdims=True)
        lse_c = lse_ref[:, 0:1]
        sp, sc = _band_scores(q, kp, kc, n, lag_off)
        pp = jnp.exp(sp - lse_c)
        pc = jnp.exp(sc - lse_c)
        dob = do.astype(_MXU_DTYPE)
        dsp = (pp * (lax.dot_general(dob, vp, _NT, preferred_element_type=_F32) - delta) * _SCALE).astype(_MXU_DTYPE)
        dsc = (pc * (lax.dot_general(dob, vc, _NT, preferred_element_type=_F32) - delta) * _SCALE).astype(_MXU_DTYPE)
        dq_ref[...] = (jnp.dot(dsp, kp, preferred_element_type=_F32) + jnp.dot(dsc, kc, preferred_element_type=_F32))

        @pl.when((n == 0) & (g % kv_group == 0))
        def _():
            dk_ref[...] = jnp.zeros_like(dk_ref)
            dv_ref[...] = jnp.zeros_like(dv_ref)

        prev = pl.ds(pl.multiple_of(jnp.maximum(n - 1, 0) * BLOCK, BLOCK), BLOCK)
        cur = pl.ds(pl.multiple_of(n * BLOCK, BLOCK), BLOCK)
        dk_ref[prev, :] += lax.dot_general(dsp, q, _TN, preferred_element_type=_F32)
        dv_ref[prev, :] += lax.dot_general(pp.astype(_MXU_DTYPE), dob, _TN, preferred_element_type=_F32)
        dk_ref[cur, :] += lax.dot_general(dsc, q, _TN, preferred_element_type=_F32)
        dv_ref[cur, :] += lax.dot_general(pc.astype(_MXU_DTYPE), dob, _TN, preferred_element_type=_F32)
        if sink is not None:
            @pl.when(n == 0)
            def _():
                dsk_ref[...] = jnp.zeros_like(dsk_ref)

            sk = sk_ref[0][:, 0:1]
            dsk_ref[...] += jnp.sum(-delta * jnp.exp(sk - lse_c))

    in_specs = _attn_specs(dil, q0, k0, v0, kv_group)
    blk = (BLOCK, HEAD_DIM)
    in_specs.append(pl.BlockSpec(blk, lambda h, r, n: (n, r * (D_MODEL // HEAD_DIM) + do0 + h)))
    in_specs.append(pl.BlockSpec(blk, lambda h, r, n: (n, r * heads + h)))
    in_specs.append(pl.BlockSpec(blk, lambda h, r, n: (n, r * heads + h)))
    args = [qkv_r] * 5 + [dm_r, o_r, lse_r]
    if sink is not None:
        in_specs.append(pl.BlockSpec((1, 1, HEAD_DIM), lambda h, r, n: (h, 0, 0)))
        args.append(jnp.broadcast_to(sink.reshape(heads, 1, 1), (heads, 1, HEAD_DIM)))
    kv_spec = pl.BlockSpec((L, HEAD_DIM), lambda h, r, n: (0, r * kvh + h // kv_group))
    out_specs = [pl.BlockSpec(blk, lambda h, r, n: (n, r * heads + h)), kv_spec, kv_spec]
    out_shape = [jax.ShapeDtypeStruct((L, dil * heads * HEAD_DIM), _F32),
                 jax.ShapeDtypeStruct((L, dil * kvh * HEAD_DIM), _F32),
                 jax.ShapeDtypeStruct((L, dil * kvh * HEAD_DIM), _F32)]
    if sink is not None:
        out_specs.append(pl.BlockSpec((1, 8, HEAD_DIM), lambda h, r, n: (h, 0, 0)))
        out_shape.append(jax.ShapeDtypeStruct((heads, 8, HEAD_DIM), _F32))
    res = pl.pallas_call(
        body, grid=(heads, dil, nb), in_specs=in_specs, out_specs=out_specs, out_shape=out_shape, name=name,
        compiler_params=_cparams(dimension_semantics=("arbitrary", "arbitrary", "arbitrary")),
    )(*args)
    dq = res[0].reshape(SEQ, heads * HEAD_DIM)
    dk = res[1].reshape(SEQ, kvh * HEAD_DIM)
    dv = res[2].reshape(SEQ, kvh * HEAD_DIM)
    if sink is not None:
        return dq, dk, dv, res[3][:, 0, 0]
    return dq, dk, dv


def _dilated_mix(os_, lses, *, name):
    tm = 256

    def body(o1, o4, o16, l1, l4, l16, o_ref, lse_ref):
        a, b, c = l1[...], l4[...], l16[...]
        m = jnp.maximum(jnp.maximum(a, b), c)
        ea, eb, ec = jnp.exp(a - m), jnp.exp(b - m), jnp.exp(c - m)
        den = ea + eb + ec
        o_ref[...] = (ea * o1[...] + eb * o4[...] + ec * o16[...]) / den
        lse_ref[...] = m + jnp.log(den)

    w = B_HEADS * HEAD_DIM
    spec = pl.BlockSpec((tm, w), lambda i: (i, 0))
    shape = jax.ShapeDtypeStruct((SEQ, w), _F32)
    return pl.pallas_call(
        body, grid=(SEQ // tm,), in_specs=[spec] * 6, out_specs=[spec, spec], out_shape=[shape, shape], name=name,
        compiler_params=_cparams(dimension_semantics=("parallel",)),
    )(*os_, *lses)


_HG_TILE = 128
_HG_CHUNKS = _HG_TILE // A_CHUNK
_HG_TILES = SEQ // _HG_TILE
_HI = lax.Precision.HIGHEST


def _chunk_tri():
    i = np.arange(_HG_TILE)
    return jnp.asarray(((i[:, None] // A_CHUNK == i[None, :] // A_CHUNK) & (i[None, :] <= i[:, None])).astype(np.float32))


def _layer_lb(lb_ref, layer):
    if layer == 0:
        return jnp.zeros((1, HEAD_DIM), _F32)
    lg = lb_ref[...]
    m = jnp.max(lg, axis=0, keepdims=True)
    e = jnp.exp(lg - m)
    return e[1:2, :] / jnp.sum(e, axis=0, keepdims=True)


def _hgrn_gates(q, fr, lb):
    sgq = _sigmoid(q)
    sg = _sigmoid(fr)
    f = lb + (1.0 - lb) * sg
    return sgq, q * sgq, sg, f, 1.0 - f


def _hgrn_fwd(proj, lb_logits, norm_w, layer, *, name):
    tri = _chunk_tri()

    def body(q_ref, f_ref, i_ref, g_ref, lb_ref, nw_ref, tri_ref, o_ref, raw_ref, st_ref, state):
        @pl.when(pl.program_id(1) == 0)
        def _():
            state[...] = jnp.zeros_like(state)

        lb = _layer_lb(lb_ref, layer)
        _, qs, _, f, k = _hgrn_gates(q_ref[...], f_ref[...], lb)
        v = i_ref[...]
        b = jnp.dot(tri_ref[...], jnp.log(f), precision=_HI, preferred_element_type=_F32)
        eb = jnp.exp(b)
        ridx = lax.broadcasted_iota(jnp.int32, (A_CHUNK, HEAD_DIM), 0)
        outs = []
        for c in range(_HG_CHUNKS):
            sl = slice(c * A_CHUNK, (c + 1) * A_CHUNK)
            bc, qc, kc, vc = b[sl], qs[sl], k[sl], v[sl]
            bl = bc[A_CHUNK - 1:A_CHUNK]
            st = state[...]
            st_ref[0, c] = st
            o_c = lax.dot_general((qc * eb[sl]).astype(_MXU_DTYPE), st.astype(_MXU_DTYPE), _NT, preferred_element_type=_F32)
            rows = []
            for i in range(A_CHUNK):
                di = jnp.exp(jnp.where(ridx <= i, bc[i:i + 1] - bc, _NEG))
                a = jnp.sum(qc[i:i + 1] * kc * di, axis=1, keepdims=True)
                rows.append(jnp.sum(a * vc, axis=0, keepdims=True))
            outs.append(o_c + jnp.concatenate(rows, axis=0))
            kt = (kc * jnp.exp(bl - bc)).astype(_MXU_DTYPE)
            state[...] = st * jnp.exp(bl) + lax.dot_general(vc.astype(_MXU_DTYPE), kt, _TN, preferred_element_type=_F32)
        o = jnp.concatenate(outs, axis=0)
        raw_ref[...] = o
        r = lax.rsqrt(jnp.mean(o * o, axis=-1, keepdims=True) + LN_EPS)
        g = g_ref[...]
        o_ref[...] = o * r * nw_ref[...] * (g * _sigmoid(g))

    blk = (_HG_TILE, HEAD_DIM)

    def col(base):
        return pl.BlockSpec(blk, lambda h, t: (t, base + h))

    o_spec = pl.BlockSpec(blk, lambda h, t: (t, h))
    o_shape = jax.ShapeDtypeStruct((SEQ, A_HEADS * HEAD_DIM), _F32)
    return pl.pallas_call(
        body, grid=(A_HEADS, _HG_TILES),
        in_specs=[col(0), col(4), col(8), col(12), pl.BlockSpec((DEPTH, HEAD_DIM), lambda h, t: (0, h)),
                  pl.BlockSpec((1, HEAD_DIM), lambda h, t: (0, 0)), pl.BlockSpec(blk, lambda h, t: (0, 0))],
        out_specs=[o_spec, o_spec, pl.BlockSpec((1, _HG_CHUNKS, HEAD_DIM, HEAD_DIM), lambda h, t: (h, t, 0, 0))],
        out_shape=[o_shape, o_shape, jax.ShapeDtypeStruct((A_HEADS, SEQ // A_CHUNK, HEAD_DIM, HEAD_DIM), _F32)],
        scratch_shapes=[pltpu.VMEM((HEAD_DIM, HEAD_DIM), _F32)], name=name,
        compiler_params=_cparams(dimension_semantics=("parallel", "arbitrary")),
    )(proj, proj, proj, proj, lb_logits, norm_w.reshape(1, HEAD_DIM), tri)


def _hgrn_bwd(proj, lb_logits, norm_w, raw, states, dmixed, layer, *, name):
    tri = _chunk_tri()
    triu = tri.T

    def body(q_ref, f_ref, i_ref, g_ref, lb_ref, nw_ref, tri_ref, triu_ref, raw_ref, do_ref, st_ref,
             dq_ref, df_ref, di_ref, dg_ref, dnw_ref, dlb_ref, dstate):
        @pl.when(pl.program_id(1) == 0)
        def _():
            dstate[...] = jnp.zeros_like(dstate)
            dlb_ref[...] = jnp.zeros_like(dlb_ref)

        @pl.when((pl.program_id(0) == 0) & (pl.program_id(1) == 0))
        def _():
            dnw_ref[...] = jnp.zeros_like(dnw_ref)

        lb = _layer_lb(lb_ref, layer)
        q = q_ref[...]
        sgq, qs, sg, f, k = _hgrn_gates(q, f_ref[...], lb)
        v = i_ref[...]
        b = jnp.dot(tri_ref[...], jnp.log(f), precision=_HI, preferred_element_type=_F32)
        eb = jnp.exp(b)
        g = g_ref[...]
        nw = nw_ref[...]
        o = raw_ref[...]
        dout = do_ref[...]
        sgg = _sigmoid(g)
        r = lax.rsqrt(jnp.mean(o * o, axis=-1, keepdims=True) + LN_EPS)
        dg_ref[...] = (dout * (o * r * nw) * (sgg * (1.0 + g * (1.0 - sgg)))).astype(dg_ref.dtype)
        don = dout * (g * sgg)
        dnw_ref[0:1, :] += jnp.sum(don * o * r, axis=0, keepdims=True)
        dy = don * nw
        do_raw = r * dy - o * (r * r * r) * jnp.mean(o * dy, axis=-1, keepdims=True)

        ridx = lax.broadcasted_iota(jnp.int32, (A_CHUNK, HEAD_DIM), 0)
        dqs_t, dk_t, db_t, dv_t = [None] * _HG_CHUNKS, [None] * _HG_CHUNKS, [None] * _HG_CHUNKS, [None] * _HG_CHUNKS
        for c in reversed(range(_HG_CHUNKS)):
            sl = slice(c * A_CHUNK, (c + 1) * A_CHUNK)
            bc, qc, kc, vc, doc = b[sl], qs[sl], k[sl], v[sl], do_raw[sl]
            bl = bc[A_CHUNK - 1:A_CHUNK]
            ebc = eb[sl]
            ebl = jnp.exp(bl - bc)
            lam = jnp.exp(bl)
            qt = qc * ebc
            kt = kc * ebl
            dst = dstate[...]
            stp = st_ref[0, c]
            dob = doc.astype(_MXU_DTYPE)
            dstb = dst.astype(_MXU_DTYPE)
            dqt = jnp.dot(dob, stp.astype(_MXU_DTYPE), preferred_element_type=_F32)
            dkt = jnp.dot(vc.astype(_MXU_DTYPE), dstb, preferred_element_type=_F32)
            dv = lax.dot_general(kt.astype(_MXU_DTYPE), dstb, _NT, preferred_element_type=_F32)
            dlam = jnp.sum(stp * dst, axis=0, keepdims=True)
            dstate[...] = dst * lam + lax.dot_general(dob, qt.astype(_MXU_DTYPE), _TN, preferred_element_type=_F32)
            dqs_rows = []
            dk_in = jnp.zeros((A_CHUNK, HEAD_DIM), _F32)
            for i in range(A_CHUNK):
                di = jnp.exp(jnp.where(ridx <= i, bc[i:i + 1] - bc, _NEG))
                qi = qc[i:i + 1]
                doi = doc[i:i + 1]
                w = kc * di
                a = jnp.sum(qi * w, axis=1, keepdims=True)
                dv = dv + a * doi
                da = jnp.sum(doi * vc, axis=1, keepdims=True)
                dqs_rows.append(jnp.sum(da * w, axis=0, keepdims=True))
                dk_in = dk_in + da * (qi * di)
            dqs_in = jnp.concatenate(dqs_rows, axis=0)
            dbl = jnp.sum(dkt * kt, axis=0, keepdims=True) + dlam * lam
            db = qc * dqs_in - kc * dk_in + dqt * qt - dkt * kt
            db_t[c] = db + jnp.where(ridx == A_CHUNK - 1, dbl, 0.0)
            dqs_t[c] = dqs_in + dqt * ebc
            dk_t[c] = dk_in + dkt * ebl
            dv_t[c] = dv
        dqs = jnp.concatenate(dqs_t, axis=0)
        dk = jnp.concatenate(dk_t, axis=0)
        db = jnp.concatenate(db_t, axis=0)
        di_ref[...] = jnp.concatenate(dv_t, axis=0).astype(di_ref.dtype)
        dlogf = jnp.dot(triu_ref[...], db, precision=_HI, preferred_element_type=_F32)
        df = dlogf / f - dk
        df_ref[...] = (df * (1.0 - lb) * sg * (1.0 - sg)).astype(df_ref.dtype)
        dlb_ref[0, 0:1, :] += jnp.sum(df * (1.0 - sg), axis=0, keepdims=True)
        dq_ref[...] = (dqs * (sgq * (1.0 + q * (1.0 - sgq)))).astype(dq_ref.dtype)

    blk = (_HG_TILE, HEAD_DIM)
    last = _HG_TILES - 1

    def col(base):
        return pl.BlockSpec(blk, lambda h, t: (last - t, base + h))

    tri_spec = pl.BlockSpec(blk, lambda h, t: (0, 0))
    acc_spec = pl.BlockSpec((1, 8, HEAD_DIM), lambda h, t: (h, 0, 0))
    acc_shape = jax.ShapeDtypeStruct((A_HEADS, 8, HEAD_DIM), _F32)
    dq, df, di, dg, dnw, dlb = pl.pallas_call(
        body, grid=(A_HEADS, _HG_TILES),
        in_specs=[col(0), col(4), col(8), col(12), pl.BlockSpec((DEPTH, HEAD_DIM), lambda h, t: (0, h)),
                  pl.BlockSpec((1, HEAD_DIM), lambda h, t: (0, 0)), tri_spec, tri_spec, col(0), col(0),
                  pl.BlockSpec((1, _HG_CHUNKS, HEAD_DIM, HEAD_DIM), lambda h, t: (h, last - t, 0, 0))],
        out_specs=[col(0), col(0), col(0), col(0), pl.BlockSpec((8, HEAD_DIM), lambda h, t: (0, 0)), acc_spec],
        out_shape=[jax.ShapeDtypeStruct((SEQ, A_HEADS * HEAD_DIM), _MXU_DTYPE)] * 4
        + [jax.ShapeDtypeStruct((8, HEAD_DIM), _F32), acc_shape],
        scratch_shapes=[pltpu.VMEM((HEAD_DIM, HEAD_DIM), _F32)], name=name,
        compiler_params=_cparams(dimension_semantics=("arbitrary", "arbitrary")),
    )(proj, proj, proj, proj, lb_logits, norm_w.reshape(1, HEAD_DIM), tri, triu, raw, dmixed, states)
    return dq, df, di, dg, dnw[0], dlb[:, 0, :].reshape(A_HEADS * HEAD_DIM)


def _exchange(arrays, scatter, *, name):
    n = len(arrays)
    n_peer = N_DEV - 1

    def body(*refs):
        ins, outs = refs[:n], refs[n:2 * n]
        send_sems, recv_sems, loc_sems = refs[2 * n:]
        x, y, c = lax.axis_index("x"), lax.axis_index("y"), lax.axis_index("c")
        me = 4 * x + 2 * y + c
        local = []
        for a in range(n):
            cp = pltpu.make_async_copy(ins[a].at[me] if scatter else ins[a], outs[a].at[me], loc_sems.at[a])
            cp.start()
            local.append(cp)

        def peer(k):
            px = jnp.bitwise_xor(x, (k >> 2) & 1)
            py = jnp.bitwise_xor(y, (k >> 1) & 1)
            pc = jnp.bitwise_xor(c, k & 1)
            return (px, py, pc), 4 * px + 2 * py + pc

        def copy(a, k):
            dev, pid = peer(k)
            return pltpu.make_async_remote_copy(
                src_ref=ins[a].at[pid] if scatter else ins[a], dst_ref=outs[a].at[me],
                send_sem=send_sems.at[a * n_peer + k - 1], recv_sem=recv_sems.at[a * n_peer + k - 1],
                device_id=dev, device_id_type=pl.DeviceIdType.MESH)

        def arrival(a, k):
            dev, pid = peer(k)
            return pltpu.make_async_remote_copy(
                src_ref=ins[a].at[pid] if scatter else ins[a], dst_ref=outs[a].at[pid],
                send_sem=send_sems.at[a * n_peer + k - 1], recv_sem=recv_sems.at[a * n_peer + k - 1],
                device_id=dev, device_id_type=pl.DeviceIdType.MESH)

        sends = [copy(a, k) for k in range(1, N_DEV) for a in range(n)]
        for cp in sends:
            cp.start()
        for k in range(1, N_DEV):
            for a in range(n):
                arrival(a, k).wait_recv()
        for cp in sends:
            cp.wait_send()
        for cp in local:
            cp.wait()

    def out_shape(a):
        blk = a.shape[1:] if scatter else a.shape
        return jax.ShapeDtypeStruct((N_DEV,) + tuple(blk), a.dtype)

    any_spec = pl.BlockSpec(memory_space=pl.ANY)
    return pl.pallas_call(
        body, in_specs=[any_spec] * n, out_specs=[any_spec] * n, out_shape=[out_shape(a) for a in arrays],
        scratch_shapes=[pltpu.SemaphoreType.DMA((n * n_peer,)), pltpu.SemaphoreType.DMA((n * n_peer,)),
                        pltpu.SemaphoreType.DMA((n,))],
        name=name, compiler_params=pltpu.CompilerParams(has_side_effects=True),
    )(*arrays)


N_CHIP = N_DEV // 2
_MESH_ID = pl.DeviceIdType.MESH


def _place():
    x, y, c = lax.axis_index("x"), lax.axis_index("y"), lax.axis_index("c")
    chips = [(1 - x, y), (x, 1 - y), (1 - x, 1 - y)]
    return x, y, c, 2 * x + y, chips


def _gather_blocks(arrays, *, name):
    n = len(arrays)

    def body(*refs):
        ins, outs = refs[:n], refs[n:2 * n]
        send_sems, recv_sems, loc_sems = refs[2 * n:]
        x, y, c, _, chips = _place()
        me = 4 * x + 2 * y + c
        sibling = (x, y, 1 - c)

        def slot(px, py, pc):
            return 4 * px + 2 * py + pc

        def copy(a, k, block, to, src=None):
            dst = outs[a].at[slot(*block)]
            return pltpu.make_async_remote_copy(
                src_ref=dst if src is None else src, dst_ref=dst, send_sem=send_sems.at[7 * a + k],
                recv_sem=recv_sems.at[7 * a + k], device_id=to, device_id_type=_MESH_ID)

        local = [pltpu.make_async_copy(ins[a], outs[a].at[me], loc_sems.at[a]) for a in range(n)]
        for cp in local:
            cp.start()
        first = []
        for a in range(n):
            first.append(copy(a, 0, (x, y, c), sibling, src=ins[a]))
            first += [copy(a, 1 + j, (x, y, c), (*chip, c), src=ins[a]) for j, chip in enumerate(chips)]
        for cp in first:
            cp.start()
        passed = []
        for a in range(n):
            for j, chip in enumerate(chips):
                copy(a, 1 + j, (*chip, c), (x, y, c)).wait_recv()
                fwd = copy(a, 4 + j, (*chip, c), sibling)
                fwd.start()
                passed.append(fwd)
        for a in range(n):
            copy(a, 0, sibling, (x, y, c)).wait_recv()
            for j, chip in enumerate(chips):
                copy(a, 4 + j, (*chip, 1 - c), (x, y, c)).wait_recv()
        for cp in first + passed:
            cp.wait_send()
        for cp in local:
            cp.wait()

    any_spec = pl.BlockSpec(memory_space=pl.ANY)
    return pl.pallas_call(
        body, in_specs=[any_spec] * n, out_specs=[any_spec] * n,
        out_shape=[jax.ShapeDtypeStruct((N_DEV,) + a.shape, a.dtype) for a in arrays],
        scratch_shapes=[pltpu.SemaphoreType.DMA((7 * n,)), pltpu.SemaphoreType.DMA((7 * n,)), pltpu.SemaphoreType.DMA((n,))],
        name=name, compiler_params=pltpu.CompilerParams(has_side_effects=True),
    )(*arrays)


def _sibling_swap(arrays, *, name):
    n = len(arrays)

    def body(*refs):
        ins, outs = refs[:n], refs[n:2 * n]
        send_sems, recv_sems = refs[2 * n:]
        x, y, c, _, _ = _place()
        copies = [pltpu.make_async_remote_copy(
            src_ref=ins[a].at[:, 1 - c], dst_ref=outs[a], send_sem=send_sems.at[a], recv_sem=recv_sems.at[a],
            device_id=(x, y, 1 - c), device_id_type=_MESH_ID) for a in range(n)]
        for cp in copies:
            cp.start()
        for cp in copies:
            cp.wait()

    any_spec = pl.BlockSpec(memory_space=pl.ANY)
    return pl.pallas_call(
        body, in_specs=[any_spec] * n, out_specs=[any_spec] * n,
        out_shape=[jax.ShapeDtypeStruct((N_CHIP,) + a.shape[2:], a.dtype) for a in arrays],
        scratch_shapes=[pltpu.SemaphoreType.DMA((n,)), pltpu.SemaphoreType.DMA((n,))],
        name=name, compiler_params=pltpu.CompilerParams(has_side_effects=True),
    )(*arrays)


def _pair_add(mine, theirs, core, *, name):
    _, _, R, C = mine.shape
    tr = max(t for t in range(16, R + 1, 16) if R % t == 0 and t * C <= 512 * 1024)

    def body(core_ref, m_ref, t_ref, o_ref):
        del core_ref
        o_ref[...] = (m_ref[...].astype(_F32) + t_ref[...].astype(_F32)).astype(o_ref.dtype)

    grid_spec = pltpu.PrefetchScalarGridSpec(
        num_scalar_prefetch=1, grid=(N_CHIP, R // tr),
        in_specs=[pl.BlockSpec((None, None, tr, C), lambda q, i, core: (q, core[0], i, 0)),
                  pl.BlockSpec((None, tr, C), lambda q, i, core: (q, i, 0))],
        out_specs=pl.BlockSpec((None, tr, C), lambda q, i, core: (q, i, 0)))
    return pl.pallas_call(
        body, grid_spec=grid_spec, out_shape=jax.ShapeDtypeStruct((N_CHIP, R, C), mine.dtype), name=name,
        compiler_params=_cparams(dimension_semantics=("parallel", "parallel")),
    )(core.reshape(1), mine, theirs)


def _chip_exchange(arrays, *, name):
    n = len(arrays)

    def body(*refs):
        ins, outs = refs[:n], refs[n:2 * n]
        send_sems, recv_sems, loc_sems = refs[2 * n:]
        x, y, c, p, chips = _place()
        local = [pltpu.make_async_copy(ins[a].at[p], outs[a].at[p], loc_sems.at[a]) for a in range(n)]
        for cp in local:
            cp.start()

        def copy(a, j, landing):
            qx, qy = chips[j]
            q = 2 * qx + qy
            return pltpu.make_async_remote_copy(
                src_ref=ins[a].at[q], dst_ref=outs[a].at[q if landing else p], send_sem=send_sems.at[3 * a + j],
                recv_sem=recv_sems.at[3 * a + j], device_id=(qx, qy, c), device_id_type=_MESH_ID)

        sends = [copy(a, j, False) for a in range(n) for j in range(3)]
        for cp in sends:
            cp.start()
        for a in range(n):
            for j in range(3):
                copy(a, j, True).wait_recv()
        for cp in sends:
            cp.wait_send()
        for cp in local:
            cp.wait()

    any_spec = pl.BlockSpec(memory_space=pl.ANY)
    return pl.pallas_call(
        body, in_specs=[any_spec] * n, out_specs=[any_spec] * n,
        out_shape=[jax.ShapeDtypeStruct(a.shape, a.dtype) for a in arrays],
        scratch_shapes=[pltpu.SemaphoreType.DMA((3 * n,)), pltpu.SemaphoreType.DMA((3 * n,)), pltpu.SemaphoreType.DMA((n,))],
        name=name, compiler_params=pltpu.CompilerParams(has_side_effects=True),
    )(*arrays)


_HBM = pl.BlockSpec(memory_space=pltpu.HBM)
_SEM = pl.BlockSpec(memory_space=pltpu.SEMAPHORE)
_TOKEN = pl.BlockSpec(memory_space=pltpu.VMEM)
_DATAFLOW = pltpu.SideEffectType.DATAFLOW_SIDE_EFFECTING


def _hbm(a):
    return pltpu.HBM(a.shape, a.dtype)


def _token_shape():
    return jax.ShapeDtypeStruct((8, 128), _F32)


def _dev_slot(px, py, pc):
    return 4 * px + 2 * py + pc


def _gather_start(blocks, landings, *, name):
    n = len(blocks)

    def body(*refs):
        ins, lands = refs[:n], refs[n:2 * n]
        send_sems, d2d_sems, ici_sems = refs[2 * n:2 * n + 3]
        token = refs[-1]
        x, y, c, _, chips = _place()
        for a in range(n):
            dst = lands[a].at[_dev_slot(x, y, c)]
            pltpu.make_async_remote_copy(src_ref=ins[a], dst_ref=dst, send_sem=send_sems.at[4 * a], recv_sem=d2d_sems.at[a],
                                         device_id=(x, y, 1 - c), device_id_type=_MESH_ID).start()
            for j, chip in enumerate(chips):
                pltpu.make_async_remote_copy(src_ref=ins[a], dst_ref=dst, send_sem=send_sems.at[4 * a + 1 + j],
                                             recv_sem=ici_sems.at[3 * a + j], device_id=(*chip, c),
                                             device_id_type=_MESH_ID).start()
        token[...] = jnp.zeros_like(token)

    res = pl.pallas_call(
        body, name=name, in_specs=[_HBM] * (2 * n),
        out_shape=(pltpu.SemaphoreType.DMA((4 * n,)), pltpu.SemaphoreType.DMA((n,)), pltpu.SemaphoreType.DMA((3 * n,)),
                   *[_hbm(b) for b in blocks], *[_hbm(b) for b in landings], _token_shape()),
        out_specs=(_SEM, _SEM, _SEM, *[_HBM] * (2 * n), _TOKEN),
        input_output_aliases={i: 3 + i for i in range(2 * n)},
        compiler_params=pltpu.CompilerParams(has_side_effects=_DATAFLOW),
    )(*[pltpu.with_memory_space_constraint(b, pltpu.HBM) for b in blocks],
      *[pltpu.with_memory_space_constraint(b, pltpu.HBM) for b in landings])
    return res[0], res[1], res[2], list(res[3:3 + n]), list(res[3 + n:3 + 2 * n]), res[-1]


def _gather_forward(landings, ici_sems, after, *, name):
    n = len(landings)

    def body(*refs):
        lands = refs[:n]
        ici = refs[n]
        f_send, f_recv = refs[n + 2], refs[n + 3]
        token = refs[-1]
        x, y, c, _, chips = _place()
        for a in range(n):
            for j, chip in enumerate(chips):
                blk = lands[a].at[_dev_slot(*chip, c)]
                pltpu.make_async_remote_copy(src_ref=blk, dst_ref=blk, send_sem=f_send.at[3 * a + j], recv_sem=ici.at[3 * a + j],
                                             device_id=(*chip, c), device_id_type=_MESH_ID).wait_recv()
                pltpu.make_async_remote_copy(src_ref=blk, dst_ref=blk, send_sem=f_send.at[3 * a + j], recv_sem=f_recv.at[3 * a + j],
                                             device_id=(x, y, 1 - c), device_id_type=_MESH_ID).start()
        token[...] = jnp.zeros_like(token)

    res = pl.pallas_call(
        body, name=name, in_specs=[_HBM] * n + [_SEM, pl.BlockSpec(memory_space=pl.ANY)],
        out_shape=(pltpu.SemaphoreType.DMA((3 * n,)), pltpu.SemaphoreType.DMA((3 * n,)), *[_hbm(b) for b in landings], _token_shape()),
        out_specs=(_SEM, _SEM, *[_HBM] * n, _TOKEN),
        input_output_aliases={i: 2 + i for i in range(n)},
        compiler_params=pltpu.CompilerParams(has_side_effects=_DATAFLOW),
    )(*landings, ici_sems, after)
    return res[0], res[1], list(res[2:2 + n]), res[-1]


def _gather_wait(blocks, landings, send_sems, d2d_sems, f_send, f_recv, after, *, name):
    n = len(landings)

    def body(*refs):
        ins, lands = refs[:n], refs[n:2 * n]
        send, d2d, fs, fr = refs[2 * n:2 * n + 4]
        x, y, c, _, chips = _place()
        me = (x, y, c)
        for a in range(n):
            own = lands[a].at[_dev_slot(x, y, 1 - c)]
            pltpu.make_async_remote_copy(src_ref=ins[a], dst_ref=own, send_sem=send.at[4 * a], recv_sem=d2d.at[a],
                                         device_id=me, device_id_type=_MESH_ID).wait_recv()
            for j, chip in enumerate(chips):
                blk = lands[a].at[_dev_slot(*chip, 1 - c)]
                pltpu.make_async_remote_copy(src_ref=blk, dst_ref=blk, send_sem=fs.at[3 * a + j], recv_sem=fr.at[3 * a + j],
                                             device_id=me, device_id_type=_MESH_ID).wait_recv()
            for k in range(4):
                pltpu.make_async_remote_copy(src_ref=ins[a], dst_ref=own, send_sem=send.at[4 * a + k], recv_sem=d2d.at[a],
                                             device_id=me, device_id_type=_MESH_ID).wait_send()
            for j in range(3):
                pltpu.make_async_remote_copy(src_ref=own, dst_ref=own, send_sem=fs.at[3 * a + j], recv_sem=fr.at[3 * a + j],
                                             device_id=me, device_id_type=_MESH_ID).wait_send()

    res = pl.pallas_call(
        body, name=name, in_specs=[_HBM] * (2 * n) + [_SEM] * 4 + [pl.BlockSpec(memory_space=pl.ANY)],
        out_shape=(*[_hbm(b) for b in blocks], *[_hbm(b) for b in landings]), out_specs=tuple([_HBM] * (2 * n)),
        input_output_aliases={i: i for i in range(2 * n)},
        compiler_params=pltpu.CompilerParams(has_side_effects=_DATAFLOW),
    )(*blocks, *landings, send_sems, d2d_sems, f_send, f_recv, after)
    return list(res[n:])


def _chip_exchange_start(sums, landings, *, name):
    n = len(sums)

    def body(*refs):
        ins, lands = refs[:n], refs[n:2 * n]
        send_sems, recv_sems = refs[2 * n:2 * n + 2]
        token = refs[-1]
        _, _, c, p, chips = _place()
        for a in range(n):
            for j, (qx, qy) in enumerate(chips):
                pltpu.make_async_remote_copy(src_ref=ins[a].at[2 * qx + qy], dst_ref=lands[a].at[p], send_sem=send_sems.at[3 * a + j],
                                             recv_sem=recv_sems.at[3 * a + j], device_id=(qx, qy, c), device_id_type=_MESH_ID).start()
        token[...] = jnp.zeros_like(token)

    res = pl.pallas_call(
        body, name=name, in_specs=[_HBM] * (2 * n),
        out_shape=(pltpu.SemaphoreType.DMA((3 * n,)), pltpu.SemaphoreType.DMA((3 * n,)),
                   *[_hbm(b) for b in sums], *[_hbm(b) for b in landings], _token_shape()),
        out_specs=(_SEM, _SEM, *[_HBM] * (2 * n), _TOKEN),
        input_output_aliases={i: 2 + i for i in range(2 * n)},
        compiler_params=pltpu.CompilerParams(has_side_effects=_DATAFLOW),
    )(*[pltpu.with_memory_space_constraint(b, pltpu.HBM) for b in sums],
      *[pltpu.with_memory_space_constraint(b, pltpu.HBM) for b in landings])
    return res[0], res[1], list(res[2:2 + n]), list(res[2 + n:2 + 2 * n]), res[-1]


def _chip_exchange_wait(sums, landings, send_sems, recv_sems, after, *, name):
    n = len(sums)

    def body(*refs):
        ins, lands = refs[:n], refs[n:2 * n]
        send, recv = refs[2 * n:2 * n + 2]
        x, y, c, _, chips = _place()
        for a in range(n):
            for j, (qx, qy) in enumerate(chips):
                q = 2 * qx + qy
                cp = pltpu.make_async_remote_copy(src_ref=ins[a].at[q], dst_ref=lands[a].at[q], send_sem=send.at[3 * a + j],
                                                  recv_sem=recv.at[3 * a + j], device_id=(x, y, c), device_id_type=_MESH_ID)
                cp.wait_recv()
                cp.wait_send()

    res = pl.pallas_call(
        body, name=name, in_specs=[_HBM] * (2 * n) + [_SEM] * 2 + [pl.BlockSpec(memory_space=pl.ANY)],
        out_shape=(*[_hbm(b) for b in sums], *[_hbm(b) for b in landings]), out_specs=tuple([_HBM] * (2 * n)),
        input_output_aliases={i: i for i in range(2 * n)},
        compiler_params=pltpu.CompilerParams(has_side_effects=_DATAFLOW),
    )(*sums, *landings, send_sems, recv_sems, after)
    return list(res[:n]), list(res[n:])


_C1 = 1.0 - ADAM_B1 ** ADAM_STEP
_C2 = 1.0 - ADAM_B2 ** ADAM_STEP


def _adamw_math(g, w, m, v):
    m = ADAM_B1 * m + (1.0 - ADAM_B1) * g
    v = ADAM_B2 * v + (1.0 - ADAM_B2) * (g * g)
    delta = -ADAM_LR * ((m / _C1) / (jnp.sqrt(v / _C2) + ADAM_EPS) + ADAM_WD * w)
    return delta, m, v


def _adamw_reduce(landed, sums, chip, w, m, v, layer, prev, *, name):
    _, R, C = w.shape
    tr = max(t for t in range(16, R + 1, 16) if R % t == 0 and t * C <= 256 * 1024)

    def body(chip_ref, p_ref, own_ref, w_ref, m_ref, v_ref, *rest):
        g_ref, d_ref, nm_ref, nv_ref = rest[-4:]
        own = own_ref[...].astype(_F32)
        g = jnp.where(chip_ref[0] == 0, own, p_ref[0].astype(_F32))
        for q in range(1, N_CHIP):
            g = g + jnp.where(chip_ref[0] == q, own, p_ref[q].astype(_F32))
        d, nm, nv = _adamw_math(g, w_ref[...], m_ref[...], v_ref[...])
        g_ref[...] = g
        d_ref[...] = d
        nm_ref[...] = nm
        nv_ref[...] = nv

    blk = pl.BlockSpec((None, tr, C), lambda i, chip: (layer, i, 0))
    shape = jax.ShapeDtypeStruct((DEPTH, R, C), _F32)
    kept = [] if prev is None else list(prev)
    grid_spec = pltpu.PrefetchScalarGridSpec(
        num_scalar_prefetch=1, grid=(R // tr,),
        in_specs=[pl.BlockSpec((N_CHIP, tr, C), lambda i, chip: (0, i, 0)),
                  pl.BlockSpec((None, tr, C), lambda i, chip: (chip[0], i, 0)), blk, blk, blk]
        + [pl.BlockSpec(memory_space=pl.ANY)] * len(kept),
        out_specs=[blk] * 4)
    return pl.pallas_call(
        body, grid_spec=grid_spec, out_shape=[shape] * 4, name=name,
        input_output_aliases={6 + k: k for k in range(len(kept))},
        compiler_params=_cparams(dimension_semantics=("parallel",)),
    )(chip.reshape(1), landed, sums, w, m, v, *kept)


_PACK_LANES = 128
_LAYER_ROWS = 248
_LB_ROWS = (A_HEADS * HEAD_DIM) // _PACK_LANES


def _small_reduce(parts, lb_logits, *, name):
    rows = DEPTH * _LAYER_ROWS

    def body(p_ref, lg_ref, o_ref):
        g = p_ref[0]
        for s in range(1, N_DEV):
            g = g + p_ref[s]
        o_ref[...] = g
        lg = lg_ref[...]
        e = jnp.exp(lg - jnp.max(lg, axis=0, keepdims=True))
        p = e / jnp.sum(e, axis=0, keepdims=True)
        d1 = g[_LAYER_ROWS:_LAYER_ROWS + _LB_ROWS, :] * p[0] * p[1]
        o_ref[0:_LB_ROWS, :] = -d1
        o_ref[_LAYER_ROWS:_LAYER_ROWS + _LB_ROWS, :] = d1

    return pl.pallas_call(
        body, out_shape=jax.ShapeDtypeStruct((rows, _PACK_LANES), _F32), name=name,
        compiler_params=_cparams(),
    )(parts, lb_logits.reshape(DEPTH, _LB_ROWS, _PACK_LANES))


def _adamw_small(g, w, m, v, *, name):
    def body(g_ref, w_ref, m_ref, v_ref, d_ref, nm_ref, nv_ref):
        d, nm, nv = _adamw_math(g_ref[...], w_ref[...], m_ref[...], v_ref[...])
        d_ref[...] = d
        nm_ref[...] = nm
        nv_ref[...] = nv

    shape = jax.ShapeDtypeStruct(g.shape, _F32)
    return pl.pallas_call(body, out_shape=[shape] * 3, name=name, compiler_params=_cparams())(g, w, m, v)


def _pack(vectors, rows):
    flat = jnp.concatenate([v.reshape(-1).astype(_F32) for v in vectors])
    return jnp.pad(flat, (0, rows * _PACK_LANES - flat.shape[0])).reshape(rows, _PACK_LANES)


def _unpack(packed, shapes):
    flat = packed.reshape(-1)
    out, at = [], 0
    for s in shapes:
        size = int(np.prod(s))
        out.append(flat[at:at + size].reshape(s))
        at += size
    return out


_BIG = ("w_in", "w_gate", "w_up", "w_out", "w_down")


def _full_weight(name, g):
    if name == "w_out":
        return g.reshape(D_MODEL, D_MODEL)
    if name == "w_down":
        return g.reshape(D_FF, D_MODEL)
    if name == "conv_w":
        return g.transpose(1, 0, 2).reshape(g.shape[1], N_DEV * SHARD_COLS)
    return g


class _WeightGather:
    def __init__(self, names, blocks, me, tag):
        self.names, self.tag = names, tag
        landings = [lax.dynamic_update_index_in_dim(lax.empty((N_DEV,) + b.shape, b.dtype), b[None], me, 0) for b in blocks]
        self.send, self.d2d, self.ici, self.blocks, self.lands, self.token = _gather_start(
            blocks, landings, name=f"gather_start_{tag}")

    def forward(self, after):
        self.f_send, self.f_recv, self.lands, token = _gather_forward(self.lands, self.ici, after, name=f"gather_forward_{self.tag}")
        return token

    def wait(self, after):
        got = _gather_wait(self.blocks, self.lands, self.send, self.d2d, self.f_send, self.f_recv, after,
                           name=f"gather_wait_{self.tag}")
        return {n: _full_weight(n, g) for n, g in zip(self.names, got)}


class _LayerWeights:
    def __init__(self, ready, pending=None, forwards=(), tokens=()):
        self.ready, self.pending, self.forwards, self._tokens = dict(ready), pending, list(forwards), list(tokens)

    def at(self, point, after):
        for when, gather in self.forwards:
            if when == point:
                self._tokens.append(gather.forward(after))

    def tokens(self):
        out, self._tokens = self._tokens, []
        return out

    def get(self, name, after):
        if name not in self.ready:
            self.ready.update(self.pending.wait(after))
        return self.ready[name]


def _layer_fwd(x, xb, ws, lb_logits, a_norm_w, c_sink, ln1_g, ln1_b, conv_b, ln2_g, ln2_b, tabs, l):
    proj = _mm_w_slabs(xb, ws.get("w_in", xb), tm=1024, after=ws.tokens(), name=f"proj_{l}")
    qkv = _qkv_prep(proj, tabs, name=f"qkv_prep_{l}")
    o_a, raw, states = _hgrn_fwd(proj, lb_logits, a_norm_w, l, name=f"hgrn_fwd_{l}")
    os_, lses = [], []
    for d in DILATIONS:
        o, lse = _attn_fwd(qkv, d, heads=B_HEADS, q0=QB0, k0=KB0, v0=VB0, kv_group=1, lag_off=0, sink=None,
                           name=f"dilated_fwd_{d}_{l}")
        os_.append(o)
        lses.append(lse)
    ws.at("dilated", os_[-1])
    o_b, lse_b = _dilated_mix(os_, lses, name=f"dilated_mix_{l}")
    o_c, lse_c = _attn_fwd(qkv, 1, heads=C_HEADS, q0=QC0, k0=KC0, v0=VC0, kv_group=C_HEADS // C_KV_HEADS, lag_off=1,
                           sink=c_sink, name=f"swa_fwd_{l}")
    mixed = jnp.concatenate([o_a, o_b, o_c], axis=1).astype(_MXU_DTYPE)
    y = _mm(mixed, ws.get("w_out", mixed), tm=1024, tn=512, after=ws.tokens(), name=f"mix_out_{l}")
    z1, x1, x1b = _ln_fwd(x, y, ln1_g, ln1_b, name=f"ln1_fwd_{l}")
    g = _mm_w_slabs(x1b, ws.get("w_gate", x1b), tm=1024, name=f"ffn_gate_{l}")
    u = _mm_w_slabs(x1b, ws.get("w_up", x1b), tm=1024, name=f"ffn_up_{l}")
    hb = _conv_gate_fwd(g, u, ws.get("conv_w", u), conv_b, name=f"conv_gate_fwd_{l}")
    ws.at("conv", hb)
    y2 = _mm(hb, ws.get("w_down", hb), tm=512, tn=512, after=ws.tokens(), name=f"ffn_down_{l}")
    z2, x2, x2b = _ln_fwd(x1, y2, ln2_g, ln2_b, name=f"ln2_fwd_{l}")
    res = dict(xb=xb, proj=proj, qkv=qkv, raw=raw, states=states, o_b=o_b, lse_b=lse_b, o_c=o_c, lse_c=lse_c,
               mixed=mixed, z1=z1, x1b=x1b, g=g, u=u, hb=hb, z2=z2)
    return x2, x2b, res


class _GradExchange:
    def __init__(self, core, chip):
        self.core, self.chip, self.groups, self._tokens = core, chip, [], []

    def launch(self, names, slabs, l, tag):
        mine = [s.reshape((N_CHIP, 2) + s.shape[1:]) for s in slabs]
        theirs = _sibling_swap(mine, name=f"swap_grads_{tag}")
        sums = [_pair_add(a, b, self.core, name=f"pair_add_{n}_{l}") for n, a, b in zip(names, mine, theirs)]
        landings = [lax.empty(s.shape, s.dtype) for s in sums]
        send, recv, sums, landings, token = _chip_exchange_start(sums, landings, name=f"exchange_start_{tag}")
        self.groups.append((names, l, tag, send, recv, sums, landings))
        self._tokens.append(token)

    def tokens(self):
        out, self._tokens = self._tokens, []
        return out

    def finish(self, weights, mom1, mom2, after):
        out = {}
        after = list(after) + self.tokens()
        for names, l, tag, send, recv, sums, landings in self.groups:
            sums, landings = _chip_exchange_wait(sums, landings, send, recv, after[-1], name=f"exchange_wait_{tag}")
            for n, s, landed in zip(names, sums, landings):
                out[n] = _adamw_reduce(landed, s, self.chip, weights[n], mom1[n], mom2[n], l, out.get(n), name=f"adamw_{n}_{l}")
                after = [out[n][0]]
        return out


def _layer_bwd(dx2, res, w, lb_logits, a_norm_w, c_sink, ln1_g, conv_b, ln2_g, tabs, exchange, l):
    dz2, dz2b, d_ln2_g, d_ln2_b = _ln_bwd(res["z2"], dx2, None, ln2_g, name=f"ln2_bwd_{l}")
    dh = _mm(dz2b, w["w_down"], tb=True, tm=1024, tn=512, after=exchange.tokens(), name=f"ffn_down_dx_{l}")
    d_w_down = _mm(res["hb"], dz2b, ta=True, tm=512, tn=512, out_dtype=_GRAD_DTYPE, name=f"ffn_down_dw_{l}")
    dg, du, d_conv_w, d_conv_b = _conv_gate_bwd(dh, res["g"], res["u"], w["conv_w"], conv_b, name=f"conv_gate_bwd_{l}")
    t = _mm_nt_w_slabs(dg, w["w_gate"], tm=512, tn=512, name=f"ffn_gate_dx_{l}")
    dx1 = _mm_nt_w_slabs(du, w["w_up"], tm=512, tn=512, add=t, name=f"ffn_up_dx_{l}")
    d_w_gate = _mm_tn_slabs(res["x1b"], dg, tm=1024, name=f"ffn_gate_dw_{l}")
    d_w_up = _mm_tn_slabs(res["x1b"], du, tm=1024, name=f"ffn_up_dw_{l}")
    dz1, dz1b, d_ln1_g, d_ln1_b = _ln_bwd(res["z1"], dx1, dz2, ln1_g, name=f"ln1_bwd_{l}")
    d_w_out = _mm(res["mixed"], dz1b, ta=True, tm=1024, tn=512, out_dtype=_GRAD_DTYPE, name=f"mix_out_dw_{l}")
    exchange.launch(("w_down", "w_gate", "w_up", "w_out"),
                    [d_w_down.reshape(N_DEV, D_FF // N_DEV, D_MODEL), d_w_gate, d_w_up,
                     d_w_out.reshape(N_DEV, D_MODEL // N_DEV, D_MODEL)], l, f"ffn_{l}")
    dmixed = _mm(dz1b, w["w_out"], tb=True, tm=1024, tn=512, after=exchange.tokens(), name=f"mix_out_dx_{l}")
    dq_a, df_a, di_a, dg_a, d_norm_w, d_lb = _hgrn_bwd(res["proj"], lb_logits, a_norm_w, res["raw"], res["states"],
                                                      dmixed, l, name=f"hgrn_bwd_{l}")
    grads = []
    for d in DILATIONS:
        grads += list(_attn_bwd(res["qkv"], dmixed, res["o_b"], res["lse_b"], d, heads=B_HEADS, q0=QB0, k0=KB0, v0=VB0,
                                kv_group=1, lag_off=0, do0=A_HEADS, sink=None, name=f"dilated_bwd_{d}_{l}"))
    dq_c, dk_c, dv_c, d_sink = _attn_bwd(res["qkv"], dmixed, res["o_c"], res["lse_c"], 1, heads=C_HEADS, q0=QC0, k0=KC0,
                                         v0=VC0, kv_group=C_HEADS // C_KV_HEADS, lag_off=1, do0=A_HEADS + B_HEADS,
                                         sink=c_sink, name=f"swa_bwd_{l}")
    dqkv = _qkv_grad_finish(*grads, dq_c, dk_c, dv_c, tabs, name=f"qkv_grad_{l}")
    dproj = jnp.concatenate([dq_a, df_a, di_a, dg_a, dqkv], axis=1)
    d_w_in = _mm_tn_slabs(res["xb"], dproj, tm=1024, name=f"proj_dw_{l}")
    exchange.launch(("w_in",), [d_w_in], l, f"mix_{l}")
    dx = _mm_nt_w_slabs(dproj, w["w_in"], tm=512, tn=512, add=dz1, add_scale=ALPHA, after=exchange.tokens(),
                        name=f"proj_dx_{l}")
    small = [d_lb, d_norm_w, jnp.pad(d_sink, (0, _PACK_LANES - C_HEADS)), d_ln1_g, d_ln1_b, d_ln2_g, d_ln2_b, d_conv_b,
             d_conv_w]
    return dx, small


def kernel(x, w_in, lb_logits, a_norm_w, c_sinks, w_out, ln1_g, ln1_b, w_gate, w_up, conv_w, conv_b, w_down, ln2_g, ln2_b, loss_target, m_w_in, m_lb_logits, m_a_norm_w, m_c_sinks, m_w_out, m_ln1_g, m_ln1_b, m_w_gate, m_w_up, m_conv_w, m_conv_b, m_w_down, m_ln2_g, m_ln2_b, v_w_in, v_lb_logits, v_a_norm_w, v_c_sinks, v_w_out, v_ln1_g, v_ln1_b, v_w_gate, v_w_up, v_conv_w, v_conv_b, v_w_down, v_ln2_g, v_ln2_b):
    weights = dict(w_in=w_in, lb_logits=lb_logits, a_norm_w=a_norm_w, c_sinks=c_sinks, w_out=w_out, ln1_g=ln1_g, ln1_b=ln1_b,
                   w_gate=w_gate, w_up=w_up, conv_w=conv_w, conv_b=conv_b, w_down=w_down, ln2_g=ln2_g, ln2_b=ln2_b)
    mom1 = dict(w_in=m_w_in, lb_logits=m_lb_logits, a_norm_w=m_a_norm_w, c_sinks=m_c_sinks, w_out=m_w_out, ln1_g=m_ln1_g,
                ln1_b=m_ln1_b, w_gate=m_w_gate, w_up=m_w_up, conv_w=m_conv_w, conv_b=m_conv_b, w_down=m_w_down, ln2_g=m_ln2_g,
                ln2_b=m_ln2_b)
    mom2 = dict(w_in=v_w_in, lb_logits=v_lb_logits, a_norm_w=v_a_norm_w, c_sinks=v_c_sinks, w_out=v_w_out, ln1_g=v_ln1_g,
                ln1_b=v_ln1_b, w_gate=v_w_gate, w_up=v_w_up, conv_w=v_conv_w, conv_b=v_conv_b, w_down=v_w_down, ln2_g=v_ln2_g,
                ln2_b=v_ln2_b)
    core = lax.axis_index("c").astype(jnp.int32)
    me = 4 * lax.axis_index("x") + 2 * lax.axis_index("y") + core
    tabs = _rope_tables()

    chip = (2 * lax.axis_index("x") + lax.axis_index("y")).astype(jnp.int32)

    def block(n, l):
        return conv_w[l] if n == "conv_w" else weights[n][l].astype(_MXU_DTYPE)

    first, = _gather_blocks([block("w_in", 0)], name="gather_w_in_0")
    rest0 = ("w_out", "w_gate", "w_up", "conv_w", "w_down")
    all1 = ("w_in",) + rest0
    gather0 = _WeightGather(rest0, [block(n, 0) for n in rest0], me, "rest_0")
    gather1 = _WeightGather(all1, [block(n, 1) for n in all1], me, "all_1")
    layer_ws = [_LayerWeights({"w_in": _full_weight("w_in", first)}, gather0, [("dilated", gather0), ("conv", gather1)],
                              [gather0.token, gather1.token]),
                _LayerWeights({}, gather1)]

    xs = x[0]
    xb = xs.astype(_MXU_DTYPE)
    saved = []
    for l in range(DEPTH):
        xs, xb, res = _layer_fwd(xs, xb, layer_ws[l], lb_logits, a_norm_w[l], c_sinks[l], ln1_g[l], ln1_b[l], conv_b[l],
                                 ln2_g[l], ln2_b[l], tabs, l)
        saved.append(res)
    loss_part, dx = _loss_head(xs, loss_target[0], name="loss_head")
    loss = lax.psum(loss_part, ("x", "y", "c"))

    exchange = _GradExchange(core, chip)
    small_parts = [None] * DEPTH
    for l in reversed(range(DEPTH)):
        dx, small = _layer_bwd(dx, saved[l], layer_ws[l].ready, lb_logits, a_norm_w[l], c_sinks[l], ln1_g[l], conv_b[l],
                               ln2_g[l], tabs, exchange, l)
        small_parts[l] = _pack(small, _LAYER_ROWS)
    updated = exchange.finish(weights, mom1, mom2, [dx])
    gathered, = _exchange([jnp.concatenate(small_parts, axis=0)], False, name="gather_small_grads")
    g_small = _small_reduce(gathered, lb_logits, name="small_grads")

    per_layer = [(A_HEADS * HEAD_DIM,), (HEAD_DIM,), (_PACK_LANES,), (D_MODEL,), (D_MODEL,), (D_MODEL,), (D_MODEL,), (D_FF,),
                 (3, D_FF)]
    names = ("lb_logits", "a_norm_w", "c_sinks", "ln1_g", "ln1_b", "ln2_g", "ln2_b", "conv_b", "conv_w")
    grads = {n: [] for n in names}
    for l in range(DEPTH):
        for n, t in zip(names, _unpack(g_small[l * _LAYER_ROWS:(l + 1) * _LAYER_ROWS], per_layer)):
            grads[n].append(t)
    grads = {n: jnp.stack(t) for n, t in grads.items()}
    grads["c_sinks"] = grads["c_sinks"][:, :C_HEADS]
    grads["conv_w"] = lax.dynamic_slice_in_dim(grads["conv_w"], me * SHARD_COLS, SHARD_COLS, axis=2)
    shapes = [grads[n].shape for n in names]
    rows = -(-sum(int(np.prod(s)) for s in shapes) // (8 * _PACK_LANES)) * 8
    d_s, m_s, v_s = _adamw_small(_pack([grads[n] for n in names], rows), _pack([weights[n] for n in names], rows),
                                 _pack([mom1[n] for n in names], rows), _pack([mom2[n] for n in names], rows),
                                 name="adamw_small")
    delta = dict(zip(names, _unpack(d_s, shapes)))
    new_m = dict(zip(names, _unpack(m_s, shapes)))
    new_v = dict(zip(names, _unpack(v_s, shapes)))
    for n in _BIG:
        grads[n], delta[n], new_m[n], new_v[n] = updated[n]

    order = ("w_in", "lb_logits", "a_norm_w", "c_sinks", "w_out", "ln1_g", "ln1_b", "w_gate", "w_up", "conv_w", "conv_b",
             "w_down", "ln2_g", "ln2_b")
    return (loss, dx[None], *[grads[n] for n in order], *[delta[n] for n in order], *[new_m[n] for n in order],
            *[new_v[n] for n in order])
```

```python
import functools

import jax
import jax.numpy as jnp
import numpy as np
from jax import lax
from jax.experimental import pallas as pl
from jax.experimental.pallas import tpu as pltpu

D_MODEL = 2048
SEQ = 2048
DEPTH = 2
HEAD_DIM = 128
A_HEADS = 4
B_HEADS = 6
C_HEADS = 6
C_KV_HEADS = 2
A_CHUNK = 16
DILATIONS = (1, 4, 16)
BLOCK = 128
ROPE_THETA = 500000.0
ROPE_DIM = 32
D_FF = 5632
IN_WIDTH = 5632
LN_EPS = 1e-5
ALPHA = (2 * DEPTH) ** 0.25
N_DEV = 8
SHARD_COLS = IN_WIDTH // N_DEV

ADAM_LR = 0.001
ADAM_B1 = 0.9
ADAM_B2 = 0.999
ADAM_EPS = 1e-08
ADAM_WD = 0.01
ADAM_STEP = 10

A_COLS = 16
QKV_COLS = 28
QB0, KB0, VB0, QC0, KC0, VC0 = 0, 6, 12, 18, 24, 26

_MXU_DTYPE = jnp.bfloat16
_GRAD_DTYPE = jnp.bfloat16
_NEG = -1e30
_VMEM_LIMIT = 56 * 2 ** 20

_F32 = jnp.float32


def _sigmoid(x):
    return 1.0 / (1.0 + jnp.exp(-x))


def _cparams(**kw):
    return pltpu.CompilerParams(vmem_limit_bytes=_VMEM_LIMIT, **kw)


def _mm(a, b, *, ta=False, tb=False, tm, tn, out_dtype=_F32, add=None, add_scale=1.0, after=(), name):
    K = a.shape[0] if ta else a.shape[1]
    M = a.shape[1] if ta else a.shape[0]
    N = b.shape[0] if tb else b.shape[1]
    assert (b.shape[1] if tb else b.shape[0]) == K and M % tm == 0 and N % tn == 0
    dn = (((0 if ta else 1,), (1 if tb else 0,)), ((), ()))

    def body(*refs):
        a_ref, b_ref = refs[:2]
        o_ref = refs[-1]
        r = lax.dot_general(a_ref[...], b_ref[...], dn, preferred_element_type=_F32)
        if add is not None:
            r = r + add_scale * refs[2][...]
        o_ref[...] = r.astype(o_ref.dtype)

    a_spec = pl.BlockSpec((K, tm), lambda i, j: (0, i)) if ta else pl.BlockSpec((tm, K), lambda i, j: (i, 0))
    b_spec = pl.BlockSpec((tn, K), lambda i, j: (j, 0)) if tb else pl.BlockSpec((K, tn), lambda i, j: (0, j))
    o_spec = pl.BlockSpec((tm, tn), lambda i, j: (i, j))
    in_specs = [a_spec, b_spec] + ([o_spec] if add is not None else []) + [pl.BlockSpec(memory_space=pl.ANY)] * len(after)
    args = (a, b) + ((add,) if add is not None else ()) + tuple(after)
    return pl.pallas_call(
        body, grid=(M // tm, N // tn), in_specs=in_specs, out_specs=o_spec,
        out_shape=jax.ShapeDtypeStruct((M, N), out_dtype), name=name,
        compiler_params=_cparams(dimension_semantics=("parallel", "parallel")),
    )(*args)


_PAIR = 2 * SHARD_COLS


def _mm_tn_slabs(a, b, *, tm, name):
    K, M = a.shape
    assert b.shape == (K, N_DEV * SHARD_COLS) and M % tm == 0

    def body(a_ref, b_ref, o_ref):
        a_blk = a_ref[...]
        for s in range(2):
            o_ref[s] = lax.dot_general(a_blk, b_ref[:, s * SHARD_COLS:(s + 1) * SHARD_COLS], _TN,
                                       preferred_element_type=_F32).astype(o_ref.dtype)

    return pl.pallas_call(
        body, grid=(M // tm, N_DEV // 2),
        in_specs=[pl.BlockSpec((K, tm), lambda i, p: (0, i)), pl.BlockSpec((K, _PAIR), lambda i, p: (0, p))],
        out_specs=pl.BlockSpec((2, tm, SHARD_COLS), lambda i, p: (p, i, 0)),
        out_shape=jax.ShapeDtypeStruct((N_DEV, M, SHARD_COLS), _GRAD_DTYPE), name=name,
        compiler_params=_cparams(dimension_semantics=("parallel", "parallel")),
    )(a, b)


def _mm_w_slabs(a, w, *, tm, after=(), name):
    M, K = a.shape
    assert w.shape == (N_DEV, K, SHARD_COLS) and M % tm == 0

    def body(a_ref, w_ref, *rest):
        o_ref = rest[-1]
        a_blk = a_ref[...]
        for s in range(2):
            o_ref[:, s * SHARD_COLS:(s + 1) * SHARD_COLS] = jnp.dot(a_blk, w_ref[s], preferred_element_type=_F32)

    return pl.pallas_call(
        body, grid=(M // tm, N_DEV // 2),
        in_specs=[pl.BlockSpec((tm, K), lambda i, p: (i, 0)), pl.BlockSpec((2, K, SHARD_COLS), lambda i, p: (p, 0, 0))]
        + [pl.BlockSpec(memory_space=pl.ANY)] * len(after),
        out_specs=pl.BlockSpec((tm, _PAIR), lambda i, p: (i, p)),
        out_shape=jax.ShapeDtypeStruct((M, N_DEV * SHARD_COLS), _F32), name=name,
        compiler_params=_cparams(dimension_semantics=("parallel", "parallel")),
    )(a, w, *after)


def _mm_nt_w_slabs(a, w, *, tm, tn, add=None, add_scale=1.0, after=(), name):
    M = a.shape[0]
    N = w.shape[1]
    assert a.shape[1] == N_DEV * SHARD_COLS and w.shape[0] == N_DEV and M % tm == 0 and N % tn == 0

    def body(a_ref, w_ref, *rest):
        o_ref = rest[-1]
        acc = add_scale * rest[0][...] if add is not None else None
        for j in range(N_DEV):
            t = lax.dot_general(a_ref[:, j * SHARD_COLS:(j + 1) * SHARD_COLS], w_ref[j], _NT, preferred_element_type=_F32)
            acc = t if acc is None else acc + t
        o_ref[...] = acc

    o_spec = pl.BlockSpec((tm, tn), lambda i, j: (i, j))
    return pl.pallas_call(
        body, grid=(M // tm, N // tn),
        in_specs=[pl.BlockSpec((tm, N_DEV * SHARD_COLS), lambda i, j: (i, 0)),
                  pl.BlockSpec((N_DEV, tn, SHARD_COLS), lambda i, j: (0, j, 0))]
        + ([o_spec] if add is not None else []) + [pl.BlockSpec(memory_space=pl.ANY)] * len(after),
        out_specs=o_spec, out_shape=jax.ShapeDtypeStruct((M, N), _F32), name=name,
        compiler_params=_cparams(dimension_semantics=("parallel", "parallel")),
    )(a, w, *((add,) if add is not None else ()), *after)


def _ln_fwd(x, y, g, b, *, name):
    tm = 256

    def body(x_ref, y_ref, g_ref, b_ref, z_ref, o_ref, ob_ref):
        z = ALPHA * x_ref[...] + y_ref[...]
        mu = jnp.mean(z, axis=-1, keepdims=True)
        zc = z - mu
        var = jnp.mean(zc * zc, axis=-1, keepdims=True)
        o = zc * lax.rsqrt(var + LN_EPS) * g_ref[...] + b_ref[...]
        z_ref[...] = z
        o_ref[...] = o
        ob_ref[...] = o.astype(ob_ref.dtype)

    row = pl.BlockSpec((tm, D_MODEL), lambda i: (i, 0))
    vec = pl.BlockSpec((1, D_MODEL), lambda i: (0, 0))
    return pl.pallas_call(
        body, grid=(SEQ // tm,), in_specs=[row, row, vec, vec], out_specs=[row, row, row],
        out_shape=[jax.ShapeDtypeStruct((SEQ, D_MODEL), _F32), jax.ShapeDtypeStruct((SEQ, D_MODEL), _F32),
                   jax.ShapeDtypeStruct((SEQ, D_MODEL), _MXU_DTYPE)],
        name=name, compiler_params=_cparams(dimension_semantics=("parallel",)),
    )(x, y, g.reshape(1, D_MODEL), b.reshape(1, D_MODEL))


def _ln_bwd(z, d_a, d_res, g, *, name):
    tm = 256

    def body(*refs):
        if d_res is None:
            z_ref, da_ref, g_ref, dz_ref, dzb_ref, dg_ref, db_ref = refs
            dout = da_ref[...]
        else:
            z_ref, da_ref, dr_ref, g_ref, dz_ref, dzb_ref, dg_ref, db_ref = refs
            dout = da_ref[...] + ALPHA * dr_ref[...]
        z = z_ref[...]
        mu = jnp.mean(z, axis=-1, keepdims=True)
        zc = z - mu
        var = jnp.mean(zc * zc, axis=-1, keepdims=True)
        rstd = lax.rsqrt(var + LN_EPS)
        xh = zc * rstd
        dxh = dout * g_ref[...]
        m1 = jnp.mean(dxh, axis=-1, keepdims=True)
        m2 = jnp.mean(dxh * xh, axis=-1, keepdims=True)
        dz = rstd * (dxh - m1 - xh * m2)
        dz_ref[...] = dz
        dzb_ref[...] = dz.astype(dzb_ref.dtype)

        @pl.when(pl.program_id(0) == 0)
        def _():
            dg_ref[...] = jnp.zeros_like(dg_ref)
            db_ref[...] = jnp.zeros_like(db_ref)

        dg_ref[0:1, :] += jnp.sum(dout * xh, axis=0, keepdims=True)
        db_ref[0:1, :] += jnp.sum(dout, axis=0, keepdims=True)

    row = pl.BlockSpec((tm, D_MODEL), lambda i: (i, 0))
    vec = pl.BlockSpec((1, D_MODEL), lambda i: (0, 0))
    acc = pl.BlockSpec((8, D_MODEL), lambda i: (0, 0))
    ins = [z, d_a] + ([d_res] if d_res is not None else []) + [g.reshape(1, D_MODEL)]
    in_specs = [row, row] + ([row] if d_res is not None else []) + [vec]
    dz, dzb, dg, db = pl.pallas_call(
        body, grid=(SEQ // tm,), in_specs=in_specs, out_specs=[row, row, acc, acc],
        out_shape=[jax.ShapeDtypeStruct((SEQ, D_MODEL), _F32), jax.ShapeDtypeStruct((SEQ, D_MODEL), _MXU_DTYPE),
                   jax.ShapeDtypeStruct((8, D_MODEL), _F32), jax.ShapeDtypeStruct((8, D_MODEL), _F32)],
        name=name, compiler_params=_cparams(dimension_semantics=("arbitrary",)),
    )(*ins)
    return dz, dzb, dg[0], db[0]


def _loss_head(y, target, *, name):
    tm = 256

    def body(y_ref, t_ref, d_ref, l_ref):
        e = y_ref[...] - t_ref[...]
        d_ref[...] = e * (1.0 / D_MODEL)

        @pl.when(pl.program_id(0) == 0)
        def _():
            l_ref[...] = jnp.zeros_like(l_ref)

        l_ref[...] += (0.5 / D_MODEL) * jnp.sum(e * e)

    row = pl.BlockSpec((tm, D_MODEL), lambda i: (i, 0))
    d, l = pl.pallas_call(
        body, grid=(SEQ // tm,), in_specs=[row, row], out_specs=[row, pl.BlockSpec((8, 128), lambda i: (0, 0))],
        out_shape=[jax.ShapeDtypeStruct((SEQ, D_MODEL), _F32), jax.ShapeDtypeStruct((8, 128), _F32)],
        name=name, compiler_params=_cparams(dimension_semantics=("arbitrary",)),
    )(y, target)
    return l[0, 0], d


_CONV_TN = 256


def _shift_down(v, k, rows):
    return jnp.where(rows >= k, pltpu.roll(v, k, axis=0), 0.0)


def _shift_up(v, k, rows):
    return jnp.where(rows < SEQ - k, pltpu.roll(v, SEQ - k, axis=0), 0.0)


def _conv_gate_fwd(g, u, conv_w, conv_b, *, name):
    def body(g_ref, u_ref, w_ref, b_ref, h_ref):
        gv = g_ref[...]
        rows = lax.broadcasted_iota(jnp.int32, gv.shape, 0)
        w = w_ref[...]
        gc = b_ref[...] + w[2:3, :] * gv + w[1:2, :] * _shift_down(gv, 1, rows) + w[0:1, :] * _shift_down(gv, 2, rows)
        h_ref[...] = (gc * _sigmoid(gc) * u_ref[...]).astype(h_ref.dtype)

    col = pl.BlockSpec((SEQ, _CONV_TN), lambda j: (0, j))
    return pl.pallas_call(
        body, grid=(D_FF // _CONV_TN,),
        in_specs=[col, col, pl.BlockSpec((3, _CONV_TN), lambda j: (0, j)), pl.BlockSpec((1, _CONV_TN), lambda j: (0, j))],
        out_specs=col, out_shape=jax.ShapeDtypeStruct((SEQ, D_FF), _MXU_DTYPE), name=name,
        compiler_params=_cparams(dimension_semantics=("parallel",)),
    )(g, u, conv_w, conv_b.reshape(1, D_FF))


def _conv_gate_bwd(dh, g, u, conv_w, conv_b, *, name):
    def body(dh_ref, g_ref, u_ref, w_ref, b_ref, dg_ref, du_ref, dw_ref, db_ref):
        gv = g_ref[...]
        rows = lax.broadcasted_iota(jnp.int32, gv.shape, 0)
        w = w_ref[...]
        g1 = _shift_down(gv, 1, rows)
        g2 = _shift_down(gv, 2, rows)
        gc = b_ref[...] + w[2:3, :] * gv + w[1:2, :] * g1 + w[0:1, :] * g2
        sg = _sigmoid(gc)
        dh = dh_ref[...]
        du_ref[...] = (dh * (gc * sg)).astype(du_ref.dtype)
        dgc = dh * u_ref[...] * (sg * (1.0 + gc * (1.0 - sg)))
        dg = w[2:3, :] * dgc + w[1:2, :] * _shift_up(dgc, 1, rows) + w[0:1, :] * _shift_up(dgc, 2, rows)
        dg_ref[...] = dg.astype(dg_ref.dtype)
        dw_ref[0:1, :] = jnp.sum(dgc * g2, axis=0, keepdims=True)
        dw_ref[1:2, :] = jnp.sum(dgc * g1, axis=0, keepdims=True)
        dw_ref[2:3, :] = jnp.sum(dgc * gv, axis=0, keepdims=True)
        db_ref[...] = jnp.sum(dgc, axis=0, keepdims=True)

    col = pl.BlockSpec((SEQ, _CONV_TN), lambda j: (0, j))
    w3 = pl.BlockSpec((3, _CONV_TN), lambda j: (0, j))
    w1 = pl.BlockSpec((1, _CONV_TN), lambda j: (0, j))
    dg, du, dw, db = pl.pallas_call(
        body, grid=(D_FF // _CONV_TN,), in_specs=[col, col, col, w3, w1], out_specs=[col, col, w3, w1],
        out_shape=[jax.ShapeDtypeStruct((SEQ, D_FF), _MXU_DTYPE), jax.ShapeDtypeStruct((SEQ, D_FF), _MXU_DTYPE),
                   jax.ShapeDtypeStruct((3, D_FF), _F32), jax.ShapeDtypeStruct((1, D_FF), _F32)],
        name=name, compiler_params=_cparams(dimension_semantics=("parallel",)),
    )(dh, g, u, conv_w, conv_b.reshape(1, D_FF))
    return dg, du, dw, db[0]


def _rope_tables():
    half = ROPE_DIM // 2
    inv = ROPE_THETA ** (-jnp.arange(0, ROPE_DIM, 2, dtype=_F32) / ROPE_DIM)
    ang = jnp.arange(SEQ, dtype=_F32)[:, None] * inv[None, :]
    cos, sin = jnp.cos(ang), jnp.sin(ang)
    rest = HEAD_DIM - ROPE_DIM
    c = jnp.concatenate([cos, cos, jnp.ones((SEQ, rest), _F32)], axis=1)
    s1 = jnp.concatenate([-sin, jnp.zeros((SEQ, HEAD_DIM - half), _F32)], axis=1)
    s2 = jnp.concatenate([jnp.zeros((SEQ, half), _F32), sin, jnp.zeros((SEQ, rest), _F32)], axis=1)
    return c, s1, s2


def _rope_apply(x, c, s1, s2):
    return x * c + pltpu.roll(x, HEAD_DIM - ROPE_DIM // 2, axis=1) * s1 + pltpu.roll(x, ROPE_DIM // 2, axis=1) * s2


def _rope_transpose(d, c, s1, s2):
    half = ROPE_DIM // 2
    return d * c + pltpu.roll(d * s1, half, axis=1) + pltpu.roll(d * s2, HEAD_DIM - half, axis=1)


_NT = (((1,), (1,)), ((), ()))
_TN = (((0,), (0,)), ((), ()))
_SCALE = HEAD_DIM ** -0.5


def _band_scores(q, kp, kc, n, lag_off):
    sp = lax.dot_general(q, kp, _NT, preferred_element_type=_F32) * _SCALE
    sc = lax.dot_general(q, kc, _NT, preferred_element_type=_F32) * _SCALE
    row = lax.broadcasted_iota(jnp.int32, (BLOCK, BLOCK), 0)
    col = lax.broadcasted_iota(jnp.int32, (BLOCK, BLOCK), 1)
    sp = jnp.where((col >= row + lag_off) & (n > 0), sp, _NEG)
    sc = jnp.where(col <= row, sc, _NEG)
    return sp, sc


_BAND_STEPS = SEQ // BLOCK


def _rows(start, d):
    if d == 1:
        return pl.ds(pl.multiple_of(start, BLOCK), BLOCK)
    return pl.ds(start, BLOCK, stride=d)


def _band_block(it, d):
    r, n = it % d, it // d
    span = BLOCK * d
    return n, _rows(r + n * span, d), _rows(r + jnp.maximum(n - 1, 0) * span, d)


def _band_fwd(proj, tabs, *, kv_heads, q_per_kv, q0, k0, v0, dilations, lag_off, sink, name):
    heads = kv_heads * q_per_kv

    def body(*refs):
        q_refs = refs[:q_per_kv]
        k_ref, v_ref, c_ref, s1_ref, s2_ref = refs[q_per_kv:q_per_kv + 5]
        rest = refs[q_per_kv + 5:]
        if sink is not None:
            sk_ref, rest = rest[0], rest[1:]
        o_ref, lse_ref, qs, ks, m_s, l_s, acc_s = rest
        c, s1, s2 = c_ref[...], s1_ref[...], s2_ref[...]
        ks[...] = _rope_apply(k_ref[...], c, s1, s2)
        for i in range(q_per_kv):
            qs[...] = _rope_apply(q_refs[i][...], c, s1, s2)
            for pi, d in enumerate(dilations):
                def step(it, carry, d=d, first=(pi == 0)):
                    n, cur, prev = _band_block(it, d)
                    q = qs[cur, :].astype(_MXU_DTYPE)
                    sp, sc = _band_scores(q, ks[prev, :].astype(_MXU_DTYPE), ks[cur, :].astype(_MXU_DTYPE), n, lag_off)
                    m_b = jnp.maximum(jnp.max(sp, axis=1, keepdims=True), jnp.max(sc, axis=1, keepdims=True))
                    m_new = m_b if first else jnp.maximum(m_b, m_s[cur, :][:, 0:1])
                    pp = jnp.exp(sp - m_new)
                    pc = jnp.exp(sc - m_new)
                    l_new = jnp.sum(pp, axis=1, keepdims=True) + jnp.sum(pc, axis=1, keepdims=True)
                    acc = jnp.dot(pp.astype(_MXU_DTYPE), v_ref[prev, :].astype(_MXU_DTYPE), preferred_element_type=_F32)
                    acc = acc + jnp.dot(pc.astype(_MXU_DTYPE), v_ref[cur, :].astype(_MXU_DTYPE), preferred_element_type=_F32)
                    if not first:
                        a = jnp.exp(m_s[cur, :][:, 0:1] - m_new)
                        l_new = l_new + a * l_s[cur, :][:, 0:1]
                        acc = acc + a * acc_s[cur, :]
                    m_s[cur, :] = jnp.broadcast_to(m_new, (BLOCK, HEAD_DIM))
                    l_s[cur, :] = jnp.broadcast_to(l_new, (BLOCK, HEAD_DIM))
                    acc_s[cur, :] = acc
                    return carry

                lax.fori_loop(0, _BAND_STEPS, step, 0)
            m, den = m_s[...], l_s[...]
            if sink is not None:
                sk = sk_ref[i]
                m_f = jnp.maximum(m, sk)
                a = jnp.exp(m - m_f)
                den = den * a + jnp.exp(sk - m_f)
                o = acc_s[...] * a / den
                m = m_f
            else:
                o = acc_s[...] / den
            o_ref[:, i * HEAD_DIM:(i + 1) * HEAD_DIM] = o
            lse_ref[:, i * HEAD_DIM:(i + 1) * HEAD_DIM] = m + jnp.log(den)

    col = (SEQ, HEAD_DIM)
    in_specs = [pl.BlockSpec(col, functools.partial(lambda g, i: (0, A_COLS + q0 + g * q_per_kv + i), i=i)) for i in range(q_per_kv)]
    in_specs += [pl.BlockSpec(col, lambda g: (0, A_COLS + k0 + g)), pl.BlockSpec(col, lambda g: (0, A_COLS + v0 + g))]
    in_specs += [pl.BlockSpec(col, lambda g: (0, 0))] * 3
    args = [proj] * (q_per_kv + 2) + list(tabs)
    if sink is not None:
        in_specs.append(pl.BlockSpec((q_per_kv, 1, HEAD_DIM), lambda g: (g, 0, 0)))
        args.append(jnp.broadcast_to(sink.reshape(heads, 1, 1), (heads, 1, HEAD_DIM)))
    o_spec = pl.BlockSpec((SEQ, q_per_kv * HEAD_DIM), lambda g: (0, g))
    shape = jax.ShapeDtypeStruct((SEQ, heads * HEAD_DIM), _F32)
    return pl.pallas_call(
        body, grid=(kv_heads,), in_specs=in_specs, out_specs=[o_spec, o_spec], out_shape=[shape, shape],
        scratch_shapes=[pltpu.VMEM(col, _F32)] * 5, name=name,
        compiler_params=_cparams(dimension_semantics=("parallel",)),
    )(*args)


def _band_bwd(proj, tabs, dmixed, o, lse, *, kv_heads, q_per_kv, q0, k0, v0, do0, dilations, lag_off, sink, name):
    heads = kv_heads * q_per_kv

    def body(*refs):
        q_refs = refs[:q_per_kv]
        k_ref, v_ref, c_ref, s1_ref, s2_ref = refs[q_per_kv:q_per_kv + 5]
        do_refs = refs[q_per_kv + 5:2 * q_per_kv + 5]
        o_ref, lse_ref = refs[2 * q_per_kv + 5:2 * q_per_kv + 7]
        rest = refs[2 * q_per_kv + 7:]
        if sink is not None:
            sk_ref, rest = rest[0], rest[1:]
            dq_ref, dk_ref, dv_ref, dsk_ref, qs, ks, dq_s, dk_s, dv_s = rest
        else:
            dq_ref, dk_ref, dv_ref, qs, ks, dq_s, dk_s, dv_s = rest
        c, s1, s2 = c_ref[...], s1_ref[...], s2_ref[...]
        ks[...] = _rope_apply(k_ref[...], c, s1, s2)
        dk_s[...] = jnp.zeros_like(dk_s)
        dv_s[...] = jnp.zeros_like(dv_s)
        for i in range(q_per_kv):
            hs = slice(i * HEAD_DIM, (i + 1) * HEAD_DIM)
            qs[...] = _rope_apply(q_refs[i][...], c, s1, s2)
            dq_s[...] = jnp.zeros_like(dq_s)
            do_ref = do_refs[i]
            for d in dilations:
                def step(it, carry, d=d, do_ref=do_ref, hs=hs):
                    n, cur, prev = _band_block(it, d)
                    q = qs[cur, :].astype(_MXU_DTYPE)
                    kp, kc = ks[prev, :].astype(_MXU_DTYPE), ks[cur, :].astype(_MXU_DTYPE)
                    vp, vc = v_ref[prev, :].astype(_MXU_DTYPE), v_ref[cur, :].astype(_MXU_DTYPE)
                    do = do_ref[cur, :]
                    delta = jnp.sum(do * o_ref[cur, hs], axis=1, keepdims=True)
                    lse_c = lse_ref[cur, hs][:, 0:1]
                    sp, sc = _band_scores(q, kp, kc, n, lag_off)
                    pp = jnp.exp(sp - lse_c)
                    pc = jnp.exp(sc - lse_c)
                    dob = do.astype(_MXU_DTYPE)
                    dsp = (pp * (lax.dot_general(dob, vp, _NT, preferred_element_type=_F32) - delta) * _SCALE).astype(_MXU_DTYPE)
                    dsc = (pc * (lax.dot_general(dob, vc, _NT, preferred_element_type=_F32) - delta) * _SCALE).astype(_MXU_DTYPE)
                    dq_s[cur, :] += jnp.dot(dsp, kp, preferred_element_type=_F32) + jnp.dot(dsc, kc, preferred_element_type=_F32)
                    dk_s[prev, :] += lax.dot_general(dsp, q, _TN, preferred_element_type=_F32)
                    dv_s[prev, :] += lax.dot_general(pp.astype(_MXU_DTYPE), dob, _TN, preferred_element_type=_F32)
                    dk_s[cur, :] += lax.dot_general(dsc, q, _TN, preferred_element_type=_F32)
                    dv_s[cur, :] += lax.dot_general(pc.astype(_MXU_DTYPE), dob, _TN, preferred_element_type=_F32)
                    return carry

                lax.fori_loop(0, _BAND_STEPS, step, 0)
            dq_ref[:, hs] = _rope_transpose(dq_s[...], c, s1, s2).astype(dq_ref.dtype)
            if sink is not None:
                delta = jnp.sum(do_ref[...] * o_ref[:, hs], axis=1, keepdims=True)
                w_sink = jnp.exp(sk_ref[i] - lse_ref[:, hs])
                dsk_ref[i] = jnp.broadcast_to(jnp.sum(-delta * w_sink[:, 0:1]), (8, HEAD_DIM))
        dk_ref[...] = _rope_transpose(dk_s[...], c, s1, s2).astype(dk_ref.dtype)
        dv_ref[...] = dv_s[...].astype(dv_ref.dtype)

    col = (SEQ, HEAD_DIM)
    in_specs = [pl.BlockSpec(col, functools.partial(lambda g, i: (0, A_COLS + q0 + g * q_per_kv + i), i=i)) for i in range(q_per_kv)]
    in_specs += [pl.BlockSpec(col, lambda g: (0, A_COLS + k0 + g)), pl.BlockSpec(col, lambda g: (0, A_COLS + v0 + g))]
    in_specs += [pl.BlockSpec(col, lambda g: (0, 0))] * 3
    in_specs += [pl.BlockSpec(col, functools.partial(lambda g, i: (0, do0 + g * q_per_kv + i), i=i)) for i in range(q_per_kv)]
    wide = pl.BlockSpec((SEQ, q_per_kv * HEAD_DIM), lambda g: (0, g))
    in_specs += [wide, wide]
    args = [proj] * (q_per_kv + 2) + list(tabs) + [dmixed] * q_per_kv + [o, lse]
    out_specs = [wide, pl.BlockSpec(col, lambda g: (0, g)), pl.BlockSpec(col, lambda g: (0, g))]
    out_shape = [jax.ShapeDtypeStruct((SEQ, heads * HEAD_DIM), _MXU_DTYPE), jax.ShapeDtypeStruct((SEQ, kv_heads * HEAD_DIM), _MXU_DTYPE),
                 jax.ShapeDtypeStruct((SEQ, kv_heads * HEAD_DIM), _MXU_DTYPE)]
    if sink is not None:
        in_specs.append(pl.BlockSpec((q_per_kv, 1, HEAD_DIM), lambda g: (g, 0, 0)))
        args.append(jnp.broadcast_to(sink.reshape(heads, 1, 1), (heads, 1, HEAD_DIM)))
        out_specs.append(pl.BlockSpec((q_per_kv, 8, HEAD_DIM), lambda g: (g, 0, 0)))
        out_shape.append(jax.ShapeDtypeStruct((heads, 8, HEAD_DIM), _F32))
    res = pl.pallas_call(
        body, grid=(kv_heads,), in_specs=in_specs, out_specs=out_specs, out_shape=out_shape,
        scratch_shapes=[pltpu.VMEM(col, _F32)] * 5, name=name,
        compiler_params=_cparams(dimension_semantics=("parallel",)),
    )(*args)
    if sink is not None:
        return res[0], res[1], res[2], res[3][:, 0, 0]
    return res


_DILATED = dict(kv_heads=B_HEADS, q_per_kv=1, q0=QB0, k0=KB0, v0=VB0, dilations=DILATIONS, lag_off=0, sink=None)
_SWA = dict(kv_heads=C_KV_HEADS, q_per_kv=C_HEADS // C_KV_HEADS, q0=QC0, k0=KC0, v0=VC0, dilations=(1,), lag_off=1)


_HG_TILE = 128
_HG_CHUNKS = _HG_TILE // A_CHUNK
_HG_TILES = SEQ // _HG_TILE
_HI = lax.Precision.HIGHEST


def _chunk_tri():
    i = np.arange(_HG_TILE)
    return jnp.asarray(((i[:, None] // A_CHUNK == i[None, :] // A_CHUNK) & (i[None, :] <= i[:, None])).astype(np.float32))


def _layer_lb(lb_ref, layer):
    if layer == 0:
        return jnp.zeros((1, HEAD_DIM), _F32)
    lg = lb_ref[...]
    m = jnp.max(lg, axis=0, keepdims=True)
    e = jnp.exp(lg - m)
    return e[1:2, :] / jnp.sum(e, axis=0, keepdims=True)


def _hgrn_gates(q, fr, lb):
    sgq = _sigmoid(q)
    sg = _sigmoid(fr)
    f = lb + (1.0 - lb) * sg
    return sgq, q * sgq, sg, f, 1.0 - f


def _hgrn_fwd(proj, lb_logits, norm_w, layer, *, name):
    tri = _chunk_tri()

    def body(q_ref, f_ref, i_ref, g_ref, lb_ref, nw_ref, tri_ref, o_ref, raw_ref, st_ref, state):
        @pl.when(pl.program_id(1) == 0)
        def _():
            state[...] = jnp.zeros_like(state)

        lb = _layer_lb(lb_ref, layer)
        _, qs, _, f, k = _hgrn_gates(q_ref[...], f_ref[...], lb)
        v = i_ref[...]
        b = jnp.dot(tri_ref[...], jnp.log(f), precision=_HI, preferred_element_type=_F32)
        eb = jnp.exp(b)
        ridx = lax.broadcasted_iota(jnp.int32, (A_CHUNK, HEAD_DIM), 0)
        outs = []
        for c in range(_HG_CHUNKS):
            sl = slice(c * A_CHUNK, (c + 1) * A_CHUNK)
            bc, qc, kc, vc = b[sl], qs[sl], k[sl], v[sl]
            bl = bc[A_CHUNK - 1:A_CHUNK]
            st = state[...]
            st_ref[0, c] = st
            o_c = lax.dot_general((qc * eb[sl]).astype(_MXU_DTYPE), st.astype(_MXU_DTYPE), _NT, preferred_element_type=_F32)
            rows = []
            for i in range(A_CHUNK):
                di = jnp.exp(jnp.where(ridx <= i, bc[i:i + 1] - bc, _NEG))
                a = jnp.sum(qc[i:i + 1] * kc * di, axis=1, keepdims=True)
                rows.append(jnp.sum(a * vc, axis=0, keepdims=True))
            outs.append(o_c + jnp.concatenate(rows, axis=0))
            kt = (kc * jnp.exp(bl - bc)).astype(_MXU_DTYPE)
            state[...] = st * jnp.exp(bl) + lax.dot_general(vc.astype(_MXU_DTYPE), kt, _TN, preferred_element_type=_F32)
        o = jnp.concatenate(outs, axis=0)
        raw_ref[...] = o
        r = lax.rsqrt(jnp.mean(o * o, axis=-1, keepdims=True) + LN_EPS)
        g = g_ref[...]
        o_ref[...] = o * r * nw_ref[...] * (g * _sigmoid(g))

    blk = (_HG_TILE, HEAD_DIM)

    def col(base):
        return pl.BlockSpec(blk, lambda h, t: (t, base + h))

    o_spec = pl.BlockSpec(blk, lambda h, t: (t, h))
    o_shape = jax.ShapeDtypeStruct((SEQ, A_HEADS * HEAD_DIM), _F32)
    return pl.pallas_call(
        body, grid=(A_HEADS, _HG_TILES),
        in_specs=[col(0), col(4), col(8), col(12), pl.BlockSpec((DEPTH, HEAD_DIM), lambda h, t: (0, h)),
                  pl.BlockSpec((1, HEAD_DIM), lambda h, t: (0, 0)), pl.BlockSpec(blk, lambda h, t: (0, 0))],
        out_specs=[o_spec, o_spec, pl.BlockSpec((1, _HG_CHUNKS, HEAD_DIM, HEAD_DIM), lambda h, t: (h, t, 0, 0))],
        out_shape=[o_shape, o_shape, jax.ShapeDtypeStruct((A_HEADS, SEQ // A_CHUNK, HEAD_DIM, HEAD_DIM), _F32)],
        scratch_shapes=[pltpu.VMEM((HEAD_DIM, HEAD_DIM), _F32)], name=name,
        compiler_params=_cparams(dimension_semantics=("parallel", "arbitrary")),
    )(proj, proj, proj, proj, lb_logits, norm_w.reshape(1, HEAD_DIM), tri)


def _hgrn_bwd(proj, lb_logits, norm_w, raw, states, dmixed, layer, *, name):
    tri = _chunk_tri()
    triu = tri.T

    def body(q_ref, f_ref, i_ref, g_ref, lb_ref, nw_ref, tri_ref, triu_ref, raw_ref, do_ref, st_ref,
             dq_ref, df_ref, di_ref, dg_ref, dnw_ref, dlb_ref, dstate):
        @pl.when(pl.program_id(1) == 0)
        def _():
            dstate[...] = jnp.zeros_like(dstate)
            dlb_ref[...] = jnp.zeros_like(dlb_ref)

        @pl.when((pl.program_id(0) == 0) & (pl.program_id(1) == 0))
        def _():
            dnw_ref[...] = jnp.zeros_like(dnw_ref)

        lb = _layer_lb(lb_ref, layer)
        q = q_ref[...]
        sgq, qs, sg, f, k = _hgrn_gates(q, f_ref[...], lb)
        v = i_ref[...]
        b = jnp.dot(tri_ref[...], jnp.log(f), precision=_HI, preferred_element_type=_F32)
        eb = jnp.exp(b)
        g = g_ref[...]
        nw = nw_ref[...]
        o = raw_ref[...]
        dout = do_ref[...]
        sgg = _sigmoid(g)
        r = lax.rsqrt(jnp.mean(o * o, axis=-1, keepdims=True) + LN_EPS)
        dg_ref[...] = (dout * (o * r * nw) * (sgg * (1.0 + g * (1.0 - sgg)))).astype(dg_ref.dtype)
        don = dout * (g * sgg)
        dnw_ref[0:1, :] += jnp.sum(don * o * r, axis=0, keepdims=True)
        dy = don * nw
        do_raw = r * dy - o * (r * r * r) * jnp.mean(o * dy, axis=-1, keepdims=True)

        ridx = lax.broadcasted_iota(jnp.int32, (A_CHUNK, HEAD_DIM), 0)
        dqs_t, dk_t, db_t, dv_t = [None] * _HG_CHUNKS, [None] * _HG_CHUNKS, [None] * _HG_CHUNKS, [None] * _HG_CHUNKS
        for c in reversed(range(_HG_CHUNKS)):
            sl = slice(c * A_CHUNK, (c + 1) * A_CHUNK)
            bc, qc, kc, vc, doc = b[sl], qs[sl], k[sl], v[sl], do_raw[sl]
            bl = bc[A_CHUNK - 1:A_CHUNK]
            ebc = eb[sl]
            ebl = jnp.exp(bl - bc)
            lam = jnp.exp(bl)
            qt = qc * ebc
            kt = kc * ebl
            dst = dstate[...]
            stp = st_ref[0, c]
            dob = doc.astype(_MXU_DTYPE)
            dstb = dst.astype(_MXU_DTYPE)
            dqt = jnp.dot(dob, stp.astype(_MXU_DTYPE), preferred_element_type=_F32)
            dkt = jnp.dot(vc.astype(_MXU_DTYPE), dstb, preferred_element_type=_F32)
            dv = lax.dot_general(kt.astype(_MXU_DTYPE), dstb, _NT, preferred_element_type=_F32)
            dlam = jnp.sum(stp * dst, axis=0, keepdims=True)
            dstate[...] = dst * lam + lax.dot_general(dob, qt.astype(_MXU_DTYPE), _TN, preferred_element_type=_F32)
            dqs_rows = []
            dk_in = jnp.zeros((A_CHUNK, HEAD_DIM), _F32)
            for i in range(A_CHUNK):
                di = jnp.exp(jnp.where(ridx <= i, bc[i:i + 1] - bc, _NEG))
                qi = qc[i:i + 1]
                doi = doc[i:i + 1]
                w = kc * di
                a = jnp.sum(qi * w, axis=1, keepdims=True)
                dv = dv + a * doi
                da = jnp.sum(doi * vc, axis=1, keepdims=True)
                dqs_rows.append(jnp.sum(da * w, axis=0, keepdims=True))
                dk_in = dk_in + da * (qi * di)
            dqs_in = jnp.concatenate(dqs_rows, axis=0)
            dbl = jnp.sum(dkt * kt, axis=0, keepdims=True) + dlam * lam
            db = qc * dqs_in - kc * dk_in + dqt * qt - dkt * kt
            db_t[c] = db + jnp.where(ridx == A_CHUNK - 1, dbl, 0.0)
            dqs_t[c] = dqs_in + dqt * ebc
            dk_t[c] = dk_in + dkt * ebl
            dv_t[c] = dv
        dqs = jnp.concatenate(dqs_t, axis=0)
        dk = jnp.concatenate(dk_t, axis=0)
        db = jnp.concatenate(db_t, axis=0)
        di_ref[...] = jnp.concatenate(dv_t, axis=0).astype(di_ref.dtype)
        dlogf = jnp.dot(triu_ref[...], db, precision=_HI, preferred_element_type=_F32)
        df = dlogf / f - dk
        df_ref[...] = (df * (1.0 - lb) * sg * (1.0 - sg)).astype(df_ref.dtype)
        dlb_ref[0, 0:1, :] += jnp.sum(df * (1.0 - sg), axis=0, keepdims=True)
        dq_ref[...] = (dqs * (sgq * (1.0 + q * (1.0 - sgq)))).astype(dq_ref.dtype)

    blk = (_HG_TILE, HEAD_DIM)
    last = _HG_TILES - 1

    def col(base):
        return pl.BlockSpec(blk, lambda h, t: (last - t, base + h))

    tri_spec = pl.BlockSpec(blk, lambda h, t: (0, 0))
    acc_spec = pl.BlockSpec((1, 8, HEAD_DIM), lambda h, t: (h, 0, 0))
    acc_shape = jax.ShapeDtypeStruct((A_HEADS, 8, HEAD_DIM), _F32)
    dq, df, di, dg, dnw, dlb = pl.pallas_call(
        body, grid=(A_HEADS, _HG_TILES),
        in_specs=[col(0), col(4), col(8), col(12), pl.BlockSpec((DEPTH, HEAD_DIM), lambda h, t: (0, h)),
                  pl.BlockSpec((1, HEAD_DIM), lambda h, t: (0, 0)), tri_spec, tri_spec, col(0), col(0),
                  pl.BlockSpec((1, _HG_CHUNKS, HEAD_DIM, HEAD_DIM), lambda h, t: (h, last - t, 0, 0))],
        out_specs=[col(0), col(0), col(0), col(0), pl.BlockSpec((8, HEAD_DIM), lambda h, t: (0, 0)), acc_spec],
        out_shape=[jax.ShapeDtypeStruct((SEQ, A_HEADS * HEAD_DIM), _MXU_DTYPE)] * 4
        + [jax.ShapeDtypeStruct((8, HEAD_DIM), _F32), acc_shape],
        scratch_shapes=[pltpu.VMEM((HEAD_DIM, HEAD_DIM), _F32)], name=name,
        compiler_params=_cparams(dimension_semantics=("arbitrary", "arbitrary")),
    )(proj, proj, proj, proj, lb_logits, norm_w.reshape(1, HEAD_DIM), tri, triu, raw, dmixed, states)
    return dq, df, di, dg, dnw[0], dlb[:, 0, :].reshape(A_HEADS * HEAD_DIM)


def _exchange(arrays, scatter, *, name):
    n = len(arrays)
    n_peer = N_DEV - 1

    def body(*refs):
        ins, outs = refs[:n], refs[n:2 * n]
        send_sems, recv_sems, loc_sems = refs[2 * n:]
        x, y, c = lax.axis_index("x"), lax.axis_index("y"), lax.axis_index("c")
        me = 4 * x + 2 * y + c
        local = []
        for a in range(n):
            cp = pltpu.make_async_copy(ins[a].at[me] if scatter else ins[a], outs[a].at[me], loc_sems.at[a])
            cp.start()
            local.append(cp)

        def peer(k):
            px = jnp.bitwise_xor(x, (k >> 2) & 1)
            py = jnp.bitwise_xor(y, (k >> 1) & 1)
            pc = jnp.bitwise_xor(c, k & 1)
            return (px, py, pc), 4 * px + 2 * py + pc

        def copy(a, k):
            dev, pid = peer(k)
            return pltpu.make_async_remote_copy(
                src_ref=ins[a].at[pid] if scatter else ins[a], dst_ref=outs[a].at[me],
                send_sem=send_sems.at[a * n_peer + k - 1], recv_sem=recv_sems.at[a * n_peer + k - 1],
                device_id=dev, device_id_type=pl.DeviceIdType.MESH)

        def arrival(a, k):
            dev, pid = peer(k)
            return pltpu.make_async_remote_copy(
                src_ref=ins[a].at[pid] if scatter else ins[a], dst_ref=outs[a].at[pid],
                send_sem=send_sems.at[a * n_peer + k - 1], recv_sem=recv_sems.at[a * n_peer + k - 1],
                device_id=dev, device_id_type=pl.DeviceIdType.MESH)

        sends = [copy(a, k) for k in range(1, N_DEV) for a in range(n)]
        for cp in sends:
            cp.start()
        for k in range(1, N_DEV):
            for a in range(n):
                arrival(a, k).wait_recv()
        for cp in sends:
            cp.wait_send()
        for cp in local:
            cp.wait()

    def out_shape(a):
        blk = a.shape[1:] if scatter else a.shape
        return jax.ShapeDtypeStruct((N_DEV,) + tuple(blk), a.dtype)

    any_spec = pl.BlockSpec(memory_space=pl.ANY)
    return pl.pallas_call(
        body, in_specs=[any_spec] * n, out_specs=[any_spec] * n, out_shape=[out_shape(a) for a in arrays],
        scratch_shapes=[pltpu.SemaphoreType.DMA((n * n_peer,)), pltpu.SemaphoreType.DMA((n * n_peer,)),
                        pltpu.SemaphoreType.DMA((n,))],
        name=name, compiler_params=pltpu.CompilerParams(has_side_effects=True),
    )(*arrays)


N_CHIP = N_DEV // 2
_MESH_ID = pl.DeviceIdType.MESH


def _place():
    x, y, c = lax.axis_index("x"), lax.axis_index("y"), lax.axis_index("c")
    chips = [(1 - x, y), (x, 1 - y), (1 - x, 1 - y)]
    return x, y, c, 2 * x + y, chips


def _gather_blocks(arrays, *, name):
    n = len(arrays)

    def body(*refs):
        ins, outs = refs[:n], refs[n:2 * n]
        send_sems, recv_sems, loc_sems = refs[2 * n:]
        x, y, c, _, chips = _place()
        me = 4 * x + 2 * y + c
        sibling = (x, y, 1 - c)

        def slot(px, py, pc):
            return 4 * px + 2 * py + pc

        def copy(a, k, block, to, src=None):
            dst = outs[a].at[slot(*block)]
            return pltpu.make_async_remote_copy(
                src_ref=dst if src is None else src, dst_ref=dst, send_sem=send_sems.at[7 * a + k],
                recv_sem=recv_sems.at[7 * a + k], device_id=to, device_id_type=_MESH_ID)

        local = [pltpu.make_async_copy(ins[a], outs[a].at[me], loc_sems.at[a]) for a in range(n)]
        for cp in local:
            cp.start()
        first = []
        for a in range(n):
            first.append(copy(a, 0, (x, y, c), sibling, src=ins[a]))
            first += [copy(a, 1 + j, (x, y, c), (*chip, c), src=ins[a]) for j, chip in enumerate(chips)]
        for cp in first:
            cp.start()
        passed = []
        for a in range(n):
            for j, chip in enumerate(chips):
                copy(a, 1 + j, (*chip, c), (x, y, c)).wait_recv()
                fwd = copy(a, 4 + j, (*chip, c), sibling)
                fwd.start()
                passed.append(fwd)
        for a in range(n):
            copy(a, 0, sibling, (x, y, c)).wait_recv()
            for j, chip in enumerate(chips):
                copy(a, 4 + j, (*chip, 1 - c), (x, y, c)).wait_recv()
        for cp in first + passed:
            cp.wait_send()
        for cp in local:
            cp.wait()

    any_spec = pl.BlockSpec(memory_space=pl.ANY)
    return pl.pallas_call(
        body, in_specs=[any_spec] * n, out_specs=[any_spec] * n,
        out_shape=[jax.ShapeDtypeStruct((N_DEV,) + a.shape, a.dtype) for a in arrays],
        scratch_shapes=[pltpu.SemaphoreType.DMA((7 * n,)), pltpu.SemaphoreType.DMA((7 * n,)), pltpu.SemaphoreType.DMA((n,))],
        name=name, compiler_params=pltpu.CompilerParams(has_side_effects=True),
    )(*arrays)


def _sibling_swap(arrays, *, name):
    n = len(arrays)

    def body(*refs):
        ins, outs = refs[:n], refs[n:2 * n]
        send_sems, recv_sems = refs[2 * n:]
        x, y, c, _, _ = _place()
        copies = [pltpu.make_async_remote_copy(
            src_ref=ins[a].at[:, 1 - c], dst_ref=outs[a], send_sem=send_sems.at[a], recv_sem=recv_sems.at[a],
            device_id=(x, y, 1 - c), device_id_type=_MESH_ID) for a in range(n)]
        for cp in copies:
            cp.start()
        for cp in copies:
            cp.wait()

    any_spec = pl.BlockSpec(memory_space=pl.ANY)
    return pl.pallas_call(
        body, in_specs=[any_spec] * n, out_specs=[any_spec] * n,
        out_shape=[jax.ShapeDtypeStruct((N_CHIP,) + a.shape[2:], a.dtype) for a in arrays],
        scratch_shapes=[pltpu.SemaphoreType.DMA((n,)), pltpu.SemaphoreType.DMA((n,))],
        name=name, compiler_params=pltpu.CompilerParams(has_side_effects=True),
    )(*arrays)


def _pair_add(mine, theirs, core, *, name):
    _, _, R, C = mine.shape
    tr = max(t for t in range(16, R + 1, 16) if R % t == 0 and t * C <= 512 * 1024)

    def body(core_ref, m_ref, t_ref, o_ref):
        del core_ref
        o_ref[...] = (m_ref[...].astype(_F32) + t_ref[...].astype(_F32)).astype(o_ref.dtype)

    grid_spec = pltpu.PrefetchScalarGridSpec(
        num_scalar_prefetch=1, grid=(N_CHIP, R // tr),
        in_specs=[pl.BlockSpec((None, None, tr, C), lambda q, i, core: (q, core[0], i, 0)),
                  pl.BlockSpec((None, tr, C), lambda q, i, core: (q, i, 0))],
        out_specs=pl.BlockSpec((None, tr, C), lambda q, i, core: (q, i, 0)))
    return pl.pallas_call(
        body, grid_spec=grid_spec, out_shape=jax.ShapeDtypeStruct((N_CHIP, R, C), mine.dtype), name=name,
        compiler_params=_cparams(dimension_semantics=("parallel", "parallel")),
    )(core.reshape(1), mine, theirs)


_HBM = pl.BlockSpec(memory_space=pltpu.HBM)
_SEM = pl.BlockSpec(memory_space=pltpu.SEMAPHORE)
_TOKEN = pl.BlockSpec(memory_space=pltpu.VMEM)
_DATAFLOW = pltpu.SideEffectType.DATAFLOW_SIDE_EFFECTING


def _hbm(a):
    return pltpu.HBM(a.shape, a.dtype)


def _token_shape():
    return jax.ShapeDtypeStruct((8, 128), _F32)


def _dev_slot(px, py, pc):
    return 4 * px + 2 * py + pc


def _gather_start(blocks, landings, *, name):
    n = len(blocks)

    def body(*refs):
        ins, lands = refs[:n], refs[n:2 * n]
        send_sems, d2d_sems, ici_sems = refs[2 * n:2 * n + 3]
        token = refs[-1]
        x, y, c, _, chips = _place()
        for a in range(n):
            dst = lands[a].at[_dev_slot(x, y, c)]
            pltpu.make_async_remote_copy(src_ref=ins[a], dst_ref=dst, send_sem=send_sems.at[4 * a], recv_sem=d2d_sems.at[a],
                                         device_id=(x, y, 1 - c), device_id_type=_MESH_ID).start()
            for j, chip in enumerate(chips):
                pltpu.make_async_remote_copy(src_ref=ins[a], dst_ref=dst, send_sem=send_sems.at[4 * a + 1 + j],
                                             recv_sem=ici_sems.at[3 * a + j], device_id=(*chip, c),
                                             device_id_type=_MESH_ID).start()
        token[...] = jnp.zeros_like(token)

    res = pl.pallas_call(
        body, name=name, in_specs=[_HBM] * (2 * n),
        out_shape=(pltpu.SemaphoreType.DMA((4 * n,)), pltpu.SemaphoreType.DMA((n,)), pltpu.SemaphoreType.DMA((3 * n,)),
                   *[_hbm(b) for b in blocks], *[_hbm(b) for b in landings], _token_shape()),
        out_specs=(_SEM, _SEM, _SEM, *[_HBM] * (2 * n), _TOKEN),
        input_output_aliases={i: 3 + i for i in range(2 * n)},
        compiler_params=pltpu.CompilerParams(has_side_effects=_DATAFLOW),
    )(*[pltpu.with_memory_space_constraint(b, pltpu.HBM) for b in blocks],
      *[pltpu.with_memory_space_constraint(b, pltpu.HBM) for b in landings])
    return res[0], res[1], res[2], list(res[3:3 + n]), list(res[3 + n:3 + 2 * n]), res[-1]


def _gather_forward(landings, ici_sems, after, *, name):
    n = len(landings)

    def body(*refs):
        lands = refs[:n]
        ici = refs[n]
        f_send, f_recv = refs[n + 2], refs[n + 3]
        token = refs[-1]
        x, y, c, _, chips = _place()
        for a in range(n):
            for j, chip in enumerate(chips):
                blk = lands[a].at[_dev_slot(*chip, c)]
                pltpu.make_async_remote_copy(src_ref=blk, dst_ref=blk, send_sem=f_send.at[3 * a + j], recv_sem=ici.at[3 * a + j],
                                             device_id=(*chip, c), device_id_type=_MESH_ID).wait_recv()
                pltpu.make_async_remote_copy(src_ref=blk, dst_ref=blk, send_sem=f_send.at[3 * a + j], recv_sem=f_recv.at[3 * a + j],
                                             device_id=(x, y, 1 - c), device_id_type=_MESH_ID).start()
        token[...] = jnp.zeros_like(token)

    res = pl.pallas_call(
        body, name=name, in_specs=[_HBM] * n + [_SEM, pl.BlockSpec(memory_space=pl.ANY)],
        out_shape=(pltpu.SemaphoreType.DMA((3 * n,)), pltpu.SemaphoreType.DMA((3 * n,)), *[_hbm(b) for b in landings], _token_shape()),
        out_specs=(_SEM, _SEM, *[_HBM] * n, _TOKEN),
        input_output_aliases={i: 2 + i for i in range(n)},
        compiler_params=pltpu.CompilerParams(has_side_effects=_DATAFLOW),
    )(*landings, ici_sems, after)
    return res[0], res[1], list(res[2:2 + n]), res[-1]


def _gather_wait(blocks, landings, send_sems, d2d_sems, f_send, f_recv, after, *, name):
    n = len(landings)

    def body(*refs):
        ins, lands = refs[:n], refs[n:2 * n]
        send, d2d, fs, fr = refs[2 * n:2 * n + 4]
        x, y, c, _, chips = _place()
        me = (x, y, c)
        for a in range(n):
            own = lands[a].at[_dev_slot(x, y, 1 - c)]
            pltpu.make_async_remote_copy(src_ref=ins[a], dst_ref=own, send_sem=send.at[4 * a], recv_sem=d2d.at[a],
                                         device_id=me, device_id_type=_MESH_ID).wait_recv()
            for j, chip in enumerate(chips):
                blk = lands[a].at[_dev_slot(*chip, 1 - c)]
                pltpu.make_async_remote_copy(src_ref=blk, dst_ref=blk, send_sem=fs.at[3 * a + j], recv_sem=fr.at[3 * a + j],
                                             device_id=me, device_id_type=_MESH_ID).wait_recv()
            for k in range(4):
                pltpu.make_async_remote_copy(src_ref=ins[a], dst_ref=own, send_sem=send.at[4 * a + k], recv_sem=d2d.at[a],
                                             device_id=me, device_id_type=_MESH_ID).wait_send()
            for j in range(3):
                pltpu.make_async_remote_copy(src_ref=own, dst_ref=own, send_sem=fs.at[3 * a + j], recv_sem=fr.at[3 * a + j],
                                             device_id=me, device_id_type=_MESH_ID).wait_send()

    res = pl.pallas_call(
        body, name=name, in_specs=[_HBM] * (2 * n) + [_SEM] * 4 + [pl.BlockSpec(memory_space=pl.ANY)],
        out_shape=(*[_hbm(b) for b in blocks], *[_hbm(b) for b in landings]), out_specs=tuple([_HBM] * (2 * n)),
        input_output_aliases={i: i for i in range(2 * n)},
        compiler_params=pltpu.CompilerParams(has_side_effects=_DATAFLOW),
    )(*blocks, *landings, send_sems, d2d_sems, f_send, f_recv, after)
    return list(res[n:])


def _chip_exchange_start(sums, landings, *, name):
    n = len(sums)

    def body(*refs):
        ins, lands = refs[:n], refs[n:2 * n]
        send_sems, recv_sems = refs[2 * n:2 * n + 2]
        token = refs[-1]
        _, _, c, p, chips = _place()
        for a in range(n):
            for j, (qx, qy) in enumerate(chips):
                pltpu.make_async_remote_copy(src_ref=ins[a].at[2 * qx + qy], dst_ref=lands[a].at[p], send_sem=send_sems.at[3 * a + j],
                                             recv_sem=recv_sems.at[3 * a + j], device_id=(qx, qy, c), device_id_type=_MESH_ID).start()
        token[...] = jnp.zeros_like(token)

    res = pl.pallas_call(
        body, name=name, in_specs=[_HBM] * (2 * n),
        out_shape=(pltpu.SemaphoreType.DMA((3 * n,)), pltpu.SemaphoreType.DMA((3 * n,)),
                   *[_hbm(b) for b in sums], *[_hbm(b) for b in landings], _token_shape()),
        out_specs=(_SEM, _SEM, *[_HBM] * (2 * n), _TOKEN),
        input_output_aliases={i: 2 + i for i in range(2 * n)},
        compiler_params=pltpu.CompilerParams(has_side_effects=_DATAFLOW),
    )(*[pltpu.with_memory_space_constraint(b, pltpu.HBM) for b in sums],
      *[pltpu.with_memory_space_constraint(b, pltpu.HBM) for b in landings])
    return res[0], res[1], list(res[2:2 + n]), list(res[2 + n:2 + 2 * n]), res[-1]


def _chip_exchange_wait(sums, landings, send_sems, recv_sems, after, *, name):
    n = len(sums)

    def body(*refs):
        ins, lands = refs[:n], refs[n:2 * n]
        send, recv = refs[2 * n:2 * n + 2]
        x, y, c, _, chips = _place()
        for a in range(n):
            for j, (qx, qy) in enumerate(chips):
                q = 2 * qx + qy
                cp = pltpu.make_async_remote_copy(src_ref=ins[a].at[q], dst_ref=lands[a].at[q], send_sem=send.at[3 * a + j],
                                                  recv_sem=recv.at[3 * a + j], device_id=(x, y, c), device_id_type=_MESH_ID)
                cp.wait_recv()
                cp.wait_send()

    res = pl.pallas_call(
        body, name=name, in_specs=[_HBM] * (2 * n) + [_SEM] * 2 + [pl.BlockSpec(memory_space=pl.ANY)],
        out_shape=(*[_hbm(b) for b in sums], *[_hbm(b) for b in landings]), out_specs=tuple([_HBM] * (2 * n)),
        input_output_aliases={i: i for i in range(2 * n)},
        compiler_params=pltpu.CompilerParams(has_side_effects=_DATAFLOW),
    )(*sums, *landings, send_sems, recv_sems, after)
    return list(res[:n]), list(res[n:])


_C1 = 1.0 - ADAM_B1 ** ADAM_STEP
_C2 = 1.0 - ADAM_B2 ** ADAM_STEP


def _adamw_math(g, w, m, v):
    m = ADAM_B1 * m + (1.0 - ADAM_B1) * g
    v = ADAM_B2 * v + (1.0 - ADAM_B2) * (g * g)
    delta = -ADAM_LR * ((m / _C1) / (jnp.sqrt(v / _C2) + ADAM_EPS) + ADAM_WD * w)
    return delta, m, v


def _adamw_reduce(landed, sums, chip, w, m, v, layer, prev, *, name):
    _, R, C = w.shape
    tr = max(t for t in range(16, R + 1, 16) if R % t == 0 and t * C <= 256 * 1024)

    def body(chip_ref, p_ref, own_ref, w_ref, m_ref, v_ref, *rest):
        g_ref, d_ref, nm_ref, nv_ref = rest[-4:]
        own = own_ref[...].astype(_F32)
        g = jnp.where(chip_ref[0] == 0, own, p_ref[0].astype(_F32))
        for q in range(1, N_CHIP):
            g = g + jnp.where(chip_ref[0] == q, own, p_ref[q].astype(_F32))
        d, nm, nv = _adamw_math(g, w_ref[...], m_ref[...], v_ref[...])
        g_ref[...] = g
        d_ref[...] = d
        nm_ref[...] = nm
        nv_ref[...] = nv

    blk = pl.BlockSpec((None, tr, C), lambda i, chip: (layer, i, 0))
    shape = jax.ShapeDtypeStruct((DEPTH, R, C), _F32)
    kept = [] if prev is None else list(prev)
    grid_spec = pltpu.PrefetchScalarGridSpec(
        num_scalar_prefetch=1, grid=(R // tr,),
        in_specs=[pl.BlockSpec((N_CHIP, tr, C), lambda i, chip: (0, i, 0)),
                  pl.BlockSpec((None, tr, C), lambda i, chip: (chip[0], i, 0)), blk, blk, blk]
        + [pl.BlockSpec(memory_space=pl.ANY)] * len(kept),
        out_specs=[blk] * 4)
    return pl.pallas_call(
        body, grid_spec=grid_spec, out_shape=[shape] * 4, name=name,
        input_output_aliases={6 + k: k for k in range(len(kept))},
        compiler_params=_cparams(dimension_semantics=("parallel",)),
    )(chip.reshape(1), landed, sums, w, m, v, *kept)


_PACK_LANES = 128
_LAYER_ROWS = 248
_LB_ROWS = (A_HEADS * HEAD_DIM) // _PACK_LANES


def _small_reduce(parts, lb_logits, *, name):
    rows = DEPTH * _LAYER_ROWS

    def body(p_ref, lg_ref, o_ref):
        g = p_ref[0]
        for s in range(1, N_DEV):
            g = g + p_ref[s]
        o_ref[...] = g
        lg = lg_ref[...]
        e = jnp.exp(lg - jnp.max(lg, axis=0, keepdims=True))
        p = e / jnp.sum(e, axis=0, keepdims=True)
        d1 = g[_LAYER_ROWS:_LAYER_ROWS + _LB_ROWS, :] * p[0] * p[1]
        o_ref[0:_LB_ROWS, :] = -d1
        o_ref[_LAYER_ROWS:_LAYER_ROWS + _LB_ROWS, :] = d1

    return pl.pallas_call(
        body, out_shape=jax.ShapeDtypeStruct((rows, _PACK_LANES), _F32), name=name,
        compiler_params=_cparams(),
    )(parts, lb_logits.reshape(DEPTH, _LB_ROWS, _PACK_LANES))


def _adamw_small(g, w, m, v, *, name):
    def body(g_ref, w_ref, m_ref, v_ref, d_ref, nm_ref, nv_ref):
        d, nm, nv = _adamw_math(g_ref[...], w_ref[...], m_ref[...], v_ref[...])
        d_ref[...] = d
        nm_ref[...] = nm
        nv_ref[...] = nv

    shape = jax.ShapeDtypeStruct(g.shape, _F32)
    return pl.pallas_call(body, out_shape=[shape] * 3, name=name, compiler_params=_cparams())(g, w, m, v)


def _pack(vectors, rows):
    flat = jnp.concatenate([v.reshape(-1).astype(_F32) for v in vectors])
    return jnp.pad(flat, (0, rows * _PACK_LANES - flat.shape[0])).reshape(rows, _PACK_LANES)


def _unpack(packed, shapes):
    flat = packed.reshape(-1)
    out, at = [], 0
    for s in shapes:
        size = int(np.prod(s))
        out.append(flat[at:at + size].reshape(s))
        at += size
    return out


_BIG = ("w_in", "w_gate", "w_up", "w_out", "w_down")


def _full_weight(name, g):
    if name == "w_out":
        return g.reshape(D_MODEL, D_MODEL)
    if name == "w_down":
        return g.reshape(D_FF, D_MODEL)
    if name == "conv_w":
        return g.transpose(1, 0, 2).reshape(g.shape[1], N_DEV * SHARD_COLS)
    return g


class _WeightGather:
    def __init__(self, names, blocks, me, tag):
        self.names, self.tag = names, tag
        landings = [lax.dynamic_update_index_in_dim(lax.empty((N_DEV,) + b.shape, b.dtype), b[None], me, 0) for b in blocks]
        self.send, self.d2d, self.ici, self.blocks, self.lands, self.token = _gather_start(
            blocks, landings, name=f"gather_start_{tag}")

    def forward(self, after):
        self.f_send, self.f_recv, self.lands, token = _gather_forward(self.lands, self.ici, after, name=f"gather_forward_{self.tag}")
        return token

    def wait(self, after):
        got = _gather_wait(self.blocks, self.lands, self.send, self.d2d, self.f_send, self.f_recv, after,
                           name=f"gather_wait_{self.tag}")
        return {n: _full_weight(n, g) for n, g in zip(self.names, got)}


class _LayerWeights:
    def __init__(self, ready, pending=None, forwards=(), tokens=()):
        self.ready, self.pending, self.forwards, self._tokens = dict(ready), pending, list(forwards), list(tokens)

    def at(self, point, after):
        for when, gather in self.forwards:
            if when == point:
                self._tokens.append(gather.forward(after))

    def tokens(self):
        out, self._tokens = self._tokens, []
        return out

    def get(self, name, after):
        if name not in self.ready:
            self.ready.update(self.pending.wait(after))
        return self.ready[name]


def _layer_fwd(x, xb, ws, lb_logits, a_norm_w, c_sink, ln1_g, ln1_b, conv_b, ln2_g, ln2_b, tabs, l):
    proj = _mm_w_slabs(xb, ws.get("w_in", xb), tm=1024, after=ws.tokens(), name=f"proj_{l}")
    o_a, raw, states = _hgrn_fwd(proj, lb_logits, a_norm_w, l, name=f"hgrn_fwd_{l}")
    o_b, lse_b = _band_fwd(proj, tabs, name=f"dilated_fwd_{l}", **_DILATED)
    ws.at("dilated", o_b)
    o_c, lse_c = _band_fwd(proj, tabs, sink=c_sink, name=f"swa_fwd_{l}", **_SWA)
    mixed = jnp.concatenate([o_a, o_b, o_c], axis=1).astype(_MXU_DTYPE)
    y = _mm(mixed, ws.get("w_out", mixed), tm=1024, tn=512, after=ws.tokens(), name=f"mix_out_{l}")
    z1, x1, x1b = _ln_fwd(x, y, ln1_g, ln1_b, name=f"ln1_fwd_{l}")
    g = _mm_w_slabs(x1b, ws.get("w_gate", x1b), tm=1024, name=f"ffn_gate_{l}")
    u = _mm_w_slabs(x1b, ws.get("w_up", x1b), tm=1024, name=f"ffn_up_{l}")
    hb = _conv_gate_fwd(g, u, ws.get("conv_w", u), conv_b, name=f"conv_gate_fwd_{l}")
    ws.at("conv", hb)
    y2 = _mm(hb, ws.get("w_down", hb), tm=512, tn=512, after=ws.tokens(), name=f"ffn_down_{l}")
    z2, x2, x2b = _ln_fwd(x1, y2, ln2_g, ln2_b, name=f"ln2_fwd_{l}")
    res = dict(xb=xb, proj=proj, raw=raw, states=states, o_b=o_b, lse_b=lse_b, o_c=o_c, lse_c=lse_c,
               mixed=mixed, z1=z1, x1b=x1b, g=g, u=u, hb=hb, z2=z2)
    return x2, x2b, res


class _GradExchange:
    def __init__(self, core, chip):
        self.core, self.chip, self.groups, self._tokens = core, chip, [], []

    def launch(self, names, slabs, l, tag):
        mine = [s.reshape((N_CHIP, 2) + s.shape[1:]) for s in slabs]
        theirs = _sibling_swap(mine, name=f"swap_grads_{tag}")
        sums = [_pair_add(a, b, self.core, name=f"pair_add_{n}_{l}") for n, a, b in zip(names, mine, theirs)]
        landings = [lax.empty(s.shape, s.dtype) for s in sums]
        send, recv, sums, landings, token = _chip_exchange_start(sums, landings, name=f"exchange_start_{tag}")
        self.groups.append((names, l, tag, send, recv, sums, landings))
        self._tokens.append(token)

    def tokens(self):
        out, self._tokens = self._tokens, []
        return out

    def finish(self, weights, mom1, mom2, after):
        out = {}
        after = list(after) + self.tokens()
        for names, l, tag, send, recv, sums, landings in self.groups:
            sums, landings = _chip_exchange_wait(sums, landings, send, recv, after[-1], name=f"exchange_wait_{tag}")
            for n, s, landed in zip(names, sums, landings):
                out[n] = _adamw_reduce(landed, s, self.chip, weights[n], mom1[n], mom2[n], l, out.get(n), name=f"adamw_{n}_{l}")
                after = [out[n][0]]
        return out


def _layer_bwd(dx2, res, w, lb_logits, a_norm_w, c_sink, ln1_g, conv_b, ln2_g, tabs, exchange, l):
    dz2, dz2b, d_ln2_g, d_ln2_b = _ln_bwd(res["z2"], dx2, None, ln2_g, name=f"ln2_bwd_{l}")
    dh = _mm(dz2b, w["w_down"], tb=True, tm=1024, tn=512, after=exchange.tokens(), name=f"ffn_down_dx_{l}")
    d_w_down = _mm(res["hb"], dz2b, ta=True, tm=512, tn=512, out_dtype=_GRAD_DTYPE, name=f"ffn_down_dw_{l}")
    dg, du, d_conv_w, d_conv_b = _conv_gate_bwd(dh, res["g"], res["u"], w["conv_w"], conv_b, name=f"conv_gate_bwd_{l}")
    t = _mm_nt_w_slabs(dg, w["w_gate"], tm=512, tn=512, name=f"ffn_gate_dx_{l}")
    dx1 = _mm_nt_w_slabs(du, w["w_up"], tm=512, tn=512, add=t, name=f"ffn_up_dx_{l}")
    d_w_gate = _mm_tn_slabs(res["x1b"], dg, tm=1024, name=f"ffn_gate_dw_{l}")
    d_w_up = _mm_tn_slabs(res["x1b"], du, tm=1024, name=f"ffn_up_dw_{l}")
    dz1, dz1b, d_ln1_g, d_ln1_b = _ln_bwd(res["z1"], dx1, dz2, ln1_g, name=f"ln1_bwd_{l}")
    d_w_out = _mm(res["mixed"], dz1b, ta=True, tm=1024, tn=512, out_dtype=_GRAD_DTYPE, name=f"mix_out_dw_{l}")
    exchange.launch(("w_down", "w_gate", "w_up", "w_out"),
                    [d_w_down.reshape(N_DEV, D_FF // N_DEV, D_MODEL), d_w_gate, d_w_up,
                     d_w_out.reshape(N_DEV, D_MODEL // N_DEV, D_MODEL)], l, f"ffn_{l}")
    dmixed = _mm(dz1b, w["w_out"], tb=True, tm=1024, tn=512, after=exchange.tokens(), name=f"mix_out_dx_{l}")
    dq_a, df_a, di_a, dg_a, d_norm_w, d_lb = _hgrn_bwd(res["proj"], lb_logits, a_norm_w, res["raw"], res["states"],
                                                      dmixed, l, name=f"hgrn_bwd_{l}")
    dq_b, dk_b, dv_b = _band_bwd(res["proj"], tabs, dmixed, res["o_b"], res["lse_b"], do0=A_HEADS,
                                 name=f"dilated_bwd_{l}", **_DILATED)
    dq_c, dk_c, dv_c, d_sink = _band_bwd(res["proj"], tabs, dmixed, res["o_c"], res["lse_c"], do0=A_HEADS + B_HEADS,
                                         sink=c_sink, name=f"swa_bwd_{l}", **_SWA)
    dproj = jnp.concatenate([dq_a, df_a, di_a, dg_a, dq_b, dk_b, dv_b, dq_c, dk_c, dv_c], axis=1)
    d_w_in = _mm_tn_slabs(res["xb"], dproj, tm=1024, name=f"proj_dw_{l}")
    exchange.launch(("w_in",), [d_w_in], l, f"mix_{l}")
    dx = _mm_nt_w_slabs(dproj, w["w_in"], tm=512, tn=512, add=dz1, add_scale=ALPHA, after=exchange.tokens(),
                        name=f"proj_dx_{l}")
    small = [d_lb, d_norm_w, jnp.pad(d_sink, (0, _PACK_LANES - C_HEADS)), d_ln1_g, d_ln1_b, d_ln2_g, d_ln2_b, d_conv_b,
             d_conv_w]
    return dx, small


def kernel(x, w_in, lb_logits, a_norm_w, c_sinks, w_out, ln1_g, ln1_b, w_gate, w_up, conv_w, conv_b, w_down, ln2_g, ln2_b, loss_target, m_w_in, m_lb_logits, m_a_norm_w, m_c_sinks, m_w_out, m_ln1_g, m_ln1_b, m_w_gate, m_w_up, m_conv_w, m_conv_b, m_w_down, m_ln2_g, m_ln2_b, v_w_in, v_lb_logits, v_a_norm_w, v_c_sinks, v_w_out, v_ln1_g, v_ln1_b, v_w_gate, v_w_up, v_conv_w, v_conv_b, v_w_down, v_ln2_g, v_ln2_b):
    weights = dict(w_in=w_in, lb_logits=lb_logits, a_norm_w=a_norm_w, c_sinks=c_sinks, w_out=w_out, ln1_g=ln1_g, ln1_b=ln1_b,
                   w_gate=w_gate, w_up=w_up, conv_w=conv_w, conv_b=conv_b, w_down=w_down, ln2_g=ln2_g, ln2_b=ln2_b)
    mom1 = dict(w_in=m_w_in, lb_logits=m_lb_logits, a_norm_w=m_a_norm_w, c_sinks=m_c_sinks, w_out=m_w_out, ln1_g=m_ln1_g,
                ln1_b=m_ln1_b, w_gate=m_w_gate, w_up=m_w_up, conv_w=m_conv_w, conv_b=m_conv_b, w_down=m_w_down, ln2_g=m_ln2_g,
                ln2_b=m_ln2_b)
    mom2 = dict(w_in=v_w_in, lb_logits=v_lb_logits, a_norm_w=v_a_norm_w, c_sinks=v_c_sinks, w_out=v_w_out, ln1_g=v_ln1_g,
                ln1_b=v_ln1_b, w_gate=v_w_gate, w_up=v_w_up, conv_w=v_conv_w, conv_b=v_conv_b, w_down=v_w_down, ln2_g=v_ln2_g,
                ln2_b=v_ln2_b)
    core = lax.axis_index("c").astype(jnp.int32)
    me = 4 * lax.axis_index("x") + 2 * lax.axis_index("y") + core
    tabs = _rope_tables()

    chip = (2 * lax.axis_index("x") + lax.axis_index("y")).astype(jnp.int32)

    def block(n, l):
        return conv_w[l] if n == "conv_w" else weights[n][l].astype(_MXU_DTYPE)

    first, = _gather_blocks([block("w_in", 0)], name="gather_w_in_0")
    rest0 = ("w_out", "w_gate", "w_up", "conv_w", "w_down")
    all1 = ("w_in",) + rest0
    gather0 = _WeightGather(rest0, [block(n, 0) for n in rest0], me, "rest_0")
    gather1 = _WeightGather(all1, [block(n, 1) for n in all1], me, "all_1")
    layer_ws = [_LayerWeights({"w_in": _full_weight("w_in", first)}, gather0, [("dilated", gather0), ("conv", gather1)],
                              [gather0.token, gather1.token]),
                _LayerWeights({}, gather1)]

    xs = x[0]
    xb = xs.astype(_MXU_DTYPE)
    saved = []
    for l in range(DEPTH):
        xs, xb, res = _layer_fwd(xs, xb, layer_ws[l], lb_logits, a_norm_w[l], c_sinks[l], ln1_g[l], ln1_b[l], conv_b[l],
                                 ln2_g[l], ln2_b[l], tabs, l)
        saved.append(res)
    loss_part, dx = _loss_head(xs, loss_target[0], name="loss_head")
    loss = lax.psum(loss_part, ("x", "y", "c"))

    exchange = _GradExchange(core, chip)
    small_parts = [None] * DEPTH
    for l in reversed(range(DEPTH)):
        dx, small = _layer_bwd(dx, saved[l], layer_ws[l].ready, lb_logits, a_norm_w[l], c_sinks[l], ln1_g[l], conv_b[l],
                               ln2_g[l], tabs, exchange, l)
        small_parts[l] = _pack(small, _LAYER_ROWS)
    updated = exchange.finish(weights, mom1, mom2, [dx])
    gathered, = _exchange([jnp.concatenate(small_parts, axis=0)], False, name="gather_small_grads")
    g_small = _small_reduce(gathered, lb_logits, name="small_grads")

    per_layer = [(A_HEADS * HEAD_DIM,), (HEAD_DIM,), (_PACK_LANES,), (D_MODEL,), (D_MODEL,), (D_MODEL,), (D_MODEL,), (D_FF,),
                 (3, D_FF)]
    names = ("lb_logits", "a_norm_w", "c_sinks", "ln1_g", "ln1_b", "ln2_g", "ln2_b", "conv_b", "conv_w")
    grads = {n: [] for n in names}
    for l in range(DEPTH):
        for n, t in zip(names, _unpack(g_small[l * _LAYER_ROWS:(l + 1) * _LAYER_ROWS], per_layer)):
            grads[n].append(t)
    grads = {n: jnp.stack(t) for n, t in grads.items()}
    grads["c_sinks"] = grads["c_sinks"][:, :C_HEADS]
    grads["conv_w"] = lax.dynamic_slice_in_dim(grads["conv_w"], me * SHARD_COLS, SHARD_COLS, axis=2)
    shapes = [grads[n].shape for n in names]
    rows = -(-sum(int(np.prod(s)) for s in shapes) // (8 * _PACK_LANES)) * 8
    d_s, m_s, v_s = _adamw_small(_pack([grads[n] for n in names], rows), _pack([weights[n] for n in names], rows),
                                 _pack([mom1[n] for n in names], rows), _pack([mom2[n] for n in names], rows),
                                 name="adamw_small")
    delta = dict(zip(names, _unpack(d_s, shapes)))
    new_m = dict(zip(names, _unpack(m_s, shapes)))
    new_v = dict(zip(names, _unpack(v_s, shapes)))
    for n in _BIG:
        grads[n], delta[n], new_m[n], new_v[n] = updated[n]

    order = ("w_in", "lb_logits", "a_norm_w", "c_sinks", "w_out", "ln1_g", "ln1_b", "w_gate", "w_up", "conv_w", "conv_b",
             "w_down", "ln2_g", "ln2_b")
    return (loss, dx[None], *[grads[n] for n in order], *[delta[n] for n in order], *[new_m[n] for n in order],
            *[new_v[n] for n in order])
```

```python
import functools

import jax
import jax.numpy as jnp
import numpy as np
from jax import lax
from jax.experimental import pallas as pl
from jax.experimental.pallas import tpu as pltpu

D_MODEL = 2048
SEQ = 2048
DEPTH = 2
HEAD_DIM = 128
A_HEADS = 4
B_HEADS = 6
C_HEADS = 6
C_KV_HEADS = 2
A_CHUNK = 16
DILATIONS = (1, 4, 16)
BLOCK = 128
ROPE_THETA = 500000.0
ROPE_DIM = 32
D_FF = 5632
IN_WIDTH = 5632
LN_EPS = 1e-5
ALPHA = (2 * DEPTH) ** 0.25
N_DEV = 8
SHARD_COLS = IN_WIDTH // N_DEV

ADAM_LR = 0.001
ADAM_B1 = 0.9
ADAM_B2 = 0.999
ADAM_EPS = 1e-08
ADAM_WD = 0.01
ADAM_STEP = 10

A_COLS = 16
QKV_COLS = 28
QB0, KB0, VB0, QC0, KC0, VC0 = 0, 6, 12, 18, 24, 26

_MXU_DTYPE = jnp.bfloat16
_GRAD_DTYPE = jnp.bfloat16
_NEG = -1e30
_VMEM_LIMIT = 56 * 2 ** 20

_F32 = jnp.float32


def _sigmoid(x):
    return 1.0 / (1.0 + jnp.exp(-x))


def _cparams(**kw):
    return pltpu.CompilerParams(vmem_limit_bytes=_VMEM_LIMIT, **kw)


def _mm(a, b, *, ta=False, tb=False, tm, tn, out_dtype=_F32, add=None, add_scale=1.0, after=(), name):
    K = a.shape[0] if ta else a.shape[1]
    M = a.shape[1] if ta else a.shape[0]
    N = b.shape[0] if tb else b.shape[1]
    assert (b.shape[1] if tb else b.shape[0]) == K and M % tm == 0 and N % tn == 0
    dn = (((0 if ta else 1,), (1 if tb else 0,)), ((), ()))

    def body(*refs):
        a_ref, b_ref = refs[:2]
        o_ref = refs[-1]
        r = lax.dot_general(a_ref[...], b_ref[...], dn, preferred_element_type=_F32)
        if add is not None:
            r = r + add_scale * refs[2][...]
        o_ref[...] = r.astype(o_ref.dtype)

    a_spec = pl.BlockSpec((K, tm), lambda i, j: (0, i)) if ta else pl.BlockSpec((tm, K), lambda i, j: (i, 0))
    b_spec = pl.BlockSpec((tn, K), lambda i, j: (j, 0)) if tb else pl.BlockSpec((K, tn), lambda i, j: (0, j))
    o_spec = pl.BlockSpec((tm, tn), lambda i, j: (i, j))
    in_specs = [a_spec, b_spec] + ([o_spec] if add is not None else []) + [pl.BlockSpec(memory_space=pl.ANY)] * len(after)
    args = (a, b) + ((add,) if add is not None else ()) + tuple(after)
    return pl.pallas_call(
        body, grid=(M // tm, N // tn), in_specs=in_specs, out_specs=o_spec,
        out_shape=jax.ShapeDtypeStruct((M, N), out_dtype), name=name,
        compiler_params=_cparams(dimension_semantics=("parallel", "parallel")),
    )(*args)


_PAIR = 2 * SHARD_COLS


def _mm_tn_slabs(a, b, *, tm, name):
    K, M = a.shape
    assert b.shape == (K, N_DEV * SHARD_COLS) and M % tm == 0

    def body(a_ref, b_ref, o_ref):
        a_blk = a_ref[...]
        for s in range(2):
            o_ref[s] = lax.dot_general(b_ref[:, s * SHARD_COLS:(s + 1) * SHARD_COLS], a_blk, _TN,
                                       preferred_element_type=_F32).astype(o_ref.dtype)

    return pl.pallas_call(
        body, grid=(M // tm, N_DEV // 2),
        in_specs=[pl.BlockSpec((K, tm), lambda i, p: (0, i)), pl.BlockSpec((K, _PAIR), lambda i, p: (0, p))],
        out_specs=pl.BlockSpec((2, SHARD_COLS, tm), lambda i, p: (p, 0, i)),
        out_shape=jax.ShapeDtypeStruct((N_DEV, SHARD_COLS, M), _GRAD_DTYPE), name=name,
        compiler_params=_cparams(dimension_semantics=("parallel", "parallel")),
    )(a, b)


def _mm_w_slabs(a, w, *, tm, after=(), name):
    M, K = a.shape
    assert w.shape == (N_DEV, K, SHARD_COLS) and M % tm == 0

    def body(a_ref, w_ref, *rest):
        o_ref = rest[-1]
        a_blk = a_ref[...]
        for s in range(2):
            o_ref[:, s * SHARD_COLS:(s + 1) * SHARD_COLS] = jnp.dot(a_blk, w_ref[s], preferred_element_type=_F32)

    return pl.pallas_call(
        body, grid=(M // tm, N_DEV // 2),
        in_specs=[pl.BlockSpec((tm, K), lambda i, p: (i, 0)), pl.BlockSpec((2, K, SHARD_COLS), lambda i, p: (p, 0, 0))]
        + [pl.BlockSpec(memory_space=pl.ANY)] * len(after),
        out_specs=pl.BlockSpec((tm, _PAIR), lambda i, p: (i, p)),
        out_shape=jax.ShapeDtypeStruct((M, N_DEV * SHARD_COLS), _F32), name=name,
        compiler_params=_cparams(dimension_semantics=("parallel", "parallel")),
    )(a, w, *after)


def _mm_nt_w_slabs(a, w, *, tm, tn, add=None, add_scale=1.0, after=(), name):
    M = a.shape[0]
    N = w.shape[1]
    assert a.shape[1] == N_DEV * SHARD_COLS and w.shape[0] == N_DEV and M % tm == 0 and N % tn == 0

    def body(a_ref, w_ref, *rest):
        o_ref = rest[-1]
        acc = add_scale * rest[0][...] if add is not None else None
        for j in range(N_DEV):
            t = lax.dot_general(a_ref[:, j * SHARD_COLS:(j + 1) * SHARD_COLS], w_ref[j], _NT, preferred_element_type=_F32)
            acc = t if acc is None else acc + t
        o_ref[...] = acc

    o_spec = pl.BlockSpec((tm, tn), lambda i, j: (i, j))
    return pl.pallas_call(
        body, grid=(M // tm, N // tn),
        in_specs=[pl.BlockSpec((tm, N_DEV * SHARD_COLS), lambda i, j: (i, 0)),
                  pl.BlockSpec((N_DEV, tn, SHARD_COLS), lambda i, j: (0, j, 0))]
        + ([o_spec] if add is not None else []) + [pl.BlockSpec(memory_space=pl.ANY)] * len(after),
        out_specs=o_spec, out_shape=jax.ShapeDtypeStruct((M, N), _F32), name=name,
        compiler_params=_cparams(dimension_semantics=("parallel", "parallel")),
    )(a, w, *((add,) if add is not None else ()), *after)


def _ln_fwd(x, y, g, b, *, name):
    tm = 256

    def body(x_ref, y_ref, g_ref, b_ref, z_ref, o_ref, ob_ref):
        z = ALPHA * x_ref[...] + y_ref[...]
        mu = jnp.mean(z, axis=-1, keepdims=True)
        zc = z - mu
        var = jnp.mean(zc * zc, axis=-1, keepdims=True)
        o = zc * lax.rsqrt(var + LN_EPS) * g_ref[...] + b_ref[...]
        z_ref[...] = z
        o_ref[...] = o
        ob_ref[...] = o.astype(ob_ref.dtype)

    row = pl.BlockSpec((tm, D_MODEL), lambda i: (i, 0))
    vec = pl.BlockSpec((1, D_MODEL), lambda i: (0, 0))
    return pl.pallas_call(
        body, grid=(SEQ // tm,), in_specs=[row, row, vec, vec], out_specs=[row, row, row],
        out_shape=[jax.ShapeDtypeStruct((SEQ, D_MODEL), _F32), jax.ShapeDtypeStruct((SEQ, D_MODEL), _F32),
                   jax.ShapeDtypeStruct((SEQ, D_MODEL), _MXU_DTYPE)],
        name=name, compiler_params=_cparams(dimension_semantics=("parallel",)),
    )(x, y, g.reshape(1, D_MODEL), b.reshape(1, D_MODEL))


def _ln_bwd(z, d_a, d_res, g, *, name):
    tm = 256

    def body(*refs):
        if d_res is None:
            z_ref, da_ref, g_ref, dz_ref, dzb_ref, dg_ref, db_ref = refs
            dout = da_ref[...]
        else:
            z_ref, da_ref, dr_ref, g_ref, dz_ref, dzb_ref, dg_ref, db_ref = refs
            dout = da_ref[...] + ALPHA * dr_ref[...]
        z = z_ref[...]
        mu = jnp.mean(z, axis=-1, keepdims=True)
        zc = z - mu
        var = jnp.mean(zc * zc, axis=-1, keepdims=True)
        rstd = lax.rsqrt(var + LN_EPS)
        xh = zc * rstd
        dxh = dout * g_ref[...]
        m1 = jnp.mean(dxh, axis=-1, keepdims=True)
        m2 = jnp.mean(dxh * xh, axis=-1, keepdims=True)
        dz = rstd * (dxh - m1 - xh * m2)
        dz_ref[...] = dz
        dzb_ref[...] = dz.astype(dzb_ref.dtype)

        @pl.when(pl.program_id(0) == 0)
        def _():
            dg_ref[...] = jnp.zeros_like(dg_ref)
            db_ref[...] = jnp.zeros_like(db_ref)

        dg_ref[0:1, :] += jnp.sum(dout * xh, axis=0, keepdims=True)
        db_ref[0:1, :] += jnp.sum(dout, axis=0, keepdims=True)

    row = pl.BlockSpec((tm, D_MODEL), lambda i: (i, 0))
    vec = pl.BlockSpec((1, D_MODEL), lambda i: (0, 0))
    acc = pl.BlockSpec((8, D_MODEL), lambda i: (0, 0))
    ins = [z, d_a] + ([d_res] if d_res is not None else []) + [g.reshape(1, D_MODEL)]
    in_specs = [row, row] + ([row] if d_res is not None else []) + [vec]
    dz, dzb, dg, db = pl.pallas_call(
        body, grid=(SEQ // tm,), in_specs=in_specs, out_specs=[row, row, acc, acc],
        out_shape=[jax.ShapeDtypeStruct((SEQ, D_MODEL), _F32), jax.ShapeDtypeStruct((SEQ, D_MODEL), _MXU_DTYPE),
                   jax.ShapeDtypeStruct((8, D_MODEL), _F32), jax.ShapeDtypeStruct((8, D_MODEL), _F32)],
        name=name, compiler_params=_cparams(dimension_semantics=("arbitrary",)),
    )(*ins)
    return dz, dzb, dg[0], db[0]


def _loss_head(y, target, *, name):
    tm = 256

    def body(y_ref, t_ref, d_ref, l_ref):
        e = y_ref[...] - t_ref[...]
        d_ref[...] = e * (1.0 / D_MODEL)

        @pl.when(pl.program_id(0) == 0)
        def _():
            l_ref[...] = jnp.zeros_like(l_ref)

        l_ref[...] += (0.5 / D_MODEL) * jnp.sum(e * e)

    row = pl.BlockSpec((tm, D_MODEL), lambda i: (i, 0))
    d, l = pl.pallas_call(
        body, grid=(SEQ // tm,), in_specs=[row, row], out_specs=[row, pl.BlockSpec((8, 128), lambda i: (0, 0))],
        out_shape=[jax.ShapeDtypeStruct((SEQ, D_MODEL), _F32), jax.ShapeDtypeStruct((8, 128), _F32)],
        name=name, compiler_params=_cparams(dimension_semantics=("arbitrary",)),
    )(y, target)
    return l[0, 0], d


_CONV_TN = 256


def _shift_down(v, k, rows):
    return jnp.where(rows >= k, pltpu.roll(v, k, axis=0), 0.0)


def _shift_up(v, k, rows):
    return jnp.where(rows < SEQ - k, pltpu.roll(v, SEQ - k, axis=0), 0.0)


def _conv_gate_fwd(g, u, conv_w, conv_b, *, name):
    def body(g_ref, u_ref, w_ref, b_ref, h_ref):
        gv = g_ref[...]
        rows = lax.broadcasted_iota(jnp.int32, gv.shape, 0)
        w = w_ref[...]
        gc = b_ref[...] + w[2:3, :] * gv + w[1:2, :] * _shift_down(gv, 1, rows) + w[0:1, :] * _shift_down(gv, 2, rows)
        h_ref[...] = (gc * _sigmoid(gc) * u_ref[...]).astype(h_ref.dtype)

    col = pl.BlockSpec((SEQ, _CONV_TN), lambda j: (0, j))
    return pl.pallas_call(
        body, grid=(D_FF // _CONV_TN,),
        in_specs=[col, col, pl.BlockSpec((3, _CONV_TN), lambda j: (0, j)), pl.BlockSpec((1, _CONV_TN), lambda j: (0, j))],
        out_specs=col, out_shape=jax.ShapeDtypeStruct((SEQ, D_FF), _MXU_DTYPE), name=name,
        compiler_params=_cparams(dimension_semantics=("parallel",)),
    )(g, u, conv_w, conv_b.reshape(1, D_FF))


def _conv_gate_bwd(dh, g, u, conv_w, conv_b, *, name):
    def body(dh_ref, g_ref, u_ref, w_ref, b_ref, dg_ref, du_ref, dw_ref, db_ref):
        gv = g_ref[...]
        rows = lax.broadcasted_iota(jnp.int32, gv.shape, 0)
        w = w_ref[...]
        g1 = _shift_down(gv, 1, rows)
        g2 = _shift_down(gv, 2, rows)
        gc = b_ref[...] + w[2:3, :] * gv + w[1:2, :] * g1 + w[0:1, :] * g2
        sg = _sigmoid(gc)
        dh = dh_ref[...]
        du_ref[...] = (dh * (gc * sg)).astype(du_ref.dtype)
        dgc = dh * u_ref[...] * (sg * (1.0 + gc * (1.0 - sg)))
        dg = w[2:3, :] * dgc + w[1:2, :] * _shift_up(dgc, 1, rows) + w[0:1, :] * _shift_up(dgc, 2, rows)
        dg_ref[...] = dg.astype(dg_ref.dtype)
        dw_ref[0:1, :] = jnp.sum(dgc * g2, axis=0, keepdims=True)
        dw_ref[1:2, :] = jnp.sum(dgc * g1, axis=0, keepdims=True)
        dw_ref[2:3, :] = jnp.sum(dgc * gv, axis=0, keepdims=True)
        db_ref[...] = jnp.sum(dgc, axis=0, keepdims=True)

    col = pl.BlockSpec((SEQ, _CONV_TN), lambda j: (0, j))
    w3 = pl.BlockSpec((3, _CONV_TN), lambda j: (0, j))
    w1 = pl.BlockSpec((1, _CONV_TN), lambda j: (0, j))
    dg, du, dw, db = pl.pallas_call(
        body, grid=(D_FF // _CONV_TN,), in_specs=[col, col, col, w3, w1], out_specs=[col, col, w3, w1],
        out_shape=[jax.ShapeDtypeStruct((SEQ, D_FF), _MXU_DTYPE), jax.ShapeDtypeStruct((SEQ, D_FF), _MXU_DTYPE),
                   jax.ShapeDtypeStruct((3, D_FF), _F32), jax.ShapeDtypeStruct((1, D_FF), _F32)],
        name=name, compiler_params=_cparams(dimension_semantics=("parallel",)),
    )(dh, g, u, conv_w, conv_b.reshape(1, D_FF))
    return dg, du, dw, db[0]


def _rope_tables():
    half = ROPE_DIM // 2
    inv = ROPE_THETA ** (-jnp.arange(0, ROPE_DIM, 2, dtype=_F32) / ROPE_DIM)
    ang = jnp.arange(SEQ, dtype=_F32)[:, None] * inv[None, :]
    cos, sin = jnp.cos(ang), jnp.sin(ang)
    rest = HEAD_DIM - ROPE_DIM
    c = jnp.concatenate([cos, cos, jnp.ones((SEQ, rest), _F32)], axis=1)
    s1 = jnp.concatenate([-sin, jnp.zeros((SEQ, HEAD_DIM - half), _F32)], axis=1)
    s2 = jnp.concatenate([jnp.zeros((SEQ, half), _F32), sin, jnp.zeros((SEQ, rest), _F32)], axis=1)
    return c, s1, s2


def _rope_apply(x, c, s1, s2):
    return x * c + pltpu.roll(x, HEAD_DIM - ROPE_DIM // 2, axis=1) * s1 + pltpu.roll(x, ROPE_DIM // 2, axis=1) * s2


def _rope_transpose(d, c, s1, s2):
    half = ROPE_DIM // 2
    return d * c + pltpu.roll(d * s1, half, axis=1) + pltpu.roll(d * s2, HEAD_DIM - half, axis=1)


_NT = (((1,), (1,)), ((), ()))
_TN = (((0,), (0,)), ((), ()))
_SCALE = HEAD_DIM ** -0.5


def _band_scores(q, kp, kc, n, lag_off):
    sp = lax.dot_general(q, kp, _NT, preferred_element_type=_F32) * _SCALE
    sc = lax.dot_general(q, kc, _NT, preferred_element_type=_F32) * _SCALE
    row = lax.broadcasted_iota(jnp.int32, (BLOCK, BLOCK), 0)
    col = lax.broadcasted_iota(jnp.int32, (BLOCK, BLOCK), 1)
    sp = jnp.where((col >= row + lag_off) & (n > 0), sp, _NEG)
    sc = jnp.where(col <= row, sc, _NEG)
    return sp, sc


_BAND_STEPS = SEQ // BLOCK


def _rows(start, d):
    if d == 1:
        return pl.ds(pl.multiple_of(start, BLOCK), BLOCK)
    return pl.ds(start, BLOCK, stride=d)


def _band_block(it, d):
    r, n = it % d, it // d
    span = BLOCK * d
    return n, _rows(r + n * span, d), _rows(r + jnp.maximum(n - 1, 0) * span, d)


def _band_fwd(proj, tabs, *, kv_heads, q_per_kv, q0, k0, v0, dilations, lag_off, sink, name):
    heads = kv_heads * q_per_kv

    def body(*refs):
        q_refs = refs[:q_per_kv]
        k_ref, v_ref, c_ref, s1_ref, s2_ref = refs[q_per_kv:q_per_kv + 5]
        rest = refs[q_per_kv + 5:]
        if sink is not None:
            sk_ref, rest = rest[0], rest[1:]
        o_ref, lse_ref, qs, ks, m_s, l_s, acc_s = rest
        c, s1, s2 = c_ref[...], s1_ref[...], s2_ref[...]
        ks[...] = _rope_apply(k_ref[...], c, s1, s2)
        for i in range(q_per_kv):
            qs[...] = _rope_apply(q_refs[i][...], c, s1, s2)
            for pi, d in enumerate(dilations):
                def step(it, carry, d=d, first=(pi == 0)):
                    n, cur, prev = _band_block(it, d)
                    q = qs[cur, :].astype(_MXU_DTYPE)
                    sp, sc = _band_scores(q, ks[prev, :].astype(_MXU_DTYPE), ks[cur, :].astype(_MXU_DTYPE), n, lag_off)
                    m_b = jnp.maximum(jnp.max(sp, axis=1, keepdims=True), jnp.max(sc, axis=1, keepdims=True))
                    m_new = m_b if first else jnp.maximum(m_b, m_s[cur, :][:, 0:1])
                    pp = jnp.exp(sp - m_new)
                    pc = jnp.exp(sc - m_new)
                    l_new = jnp.sum(pp, axis=1, keepdims=True) + jnp.sum(pc, axis=1, keepdims=True)
                    acc = jnp.dot(pp.astype(_MXU_DTYPE), v_ref[prev, :].astype(_MXU_DTYPE), preferred_element_type=_F32)
                    acc = acc + jnp.dot(pc.astype(_MXU_DTYPE), v_ref[cur, :].astype(_MXU_DTYPE), preferred_element_type=_F32)
                    if not first:
                        a = jnp.exp(m_s[cur, :][:, 0:1] - m_new)
                        l_new = l_new + a * l_s[cur, :][:, 0:1]
                        acc = acc + a * acc_s[cur, :]
                    m_s[cur, :] = jnp.broadcast_to(m_new, (BLOCK, HEAD_DIM))
                    l_s[cur, :] = jnp.broadcast_to(l_new, (BLOCK, HEAD_DIM))
                    acc_s[cur, :] = acc
                    return carry

                lax.fori_loop(0, _BAND_STEPS, step, 0)
            m, den = m_s[...], l_s[...]
            if sink is not None:
                sk = sk_ref[i]
                m_f = jnp.maximum(m, sk)
                a = jnp.exp(m - m_f)
                den = den * a + jnp.exp(sk - m_f)
                o = acc_s[...] * a / den
                m = m_f
            else:
                o = acc_s[...] / den
            o_ref[:, i * HEAD_DIM:(i + 1) * HEAD_DIM] = o
            lse_ref[:, i * HEAD_DIM:(i + 1) * HEAD_DIM] = m + jnp.log(den)

    col = (SEQ, HEAD_DIM)
    in_specs = [pl.BlockSpec(col, functools.partial(lambda g, i: (0, A_COLS + q0 + g * q_per_kv + i), i=i)) for i in range(q_per_kv)]
    in_specs += [pl.BlockSpec(col, lambda g: (0, A_COLS + k0 + g)), pl.BlockSpec(col, lambda g: (0, A_COLS + v0 + g))]
    in_specs += [pl.BlockSpec(col, lambda g: (0, 0))] * 3
    args = [proj] * (q_per_kv + 2) + list(tabs)
    if sink is not None:
        in_specs.append(pl.BlockSpec((q_per_kv, 1, HEAD_DIM), lambda g: (g, 0, 0)))
        args.append(jnp.broadcast_to(sink.reshape(heads, 1, 1), (heads, 1, HEAD_DIM)))
    o_spec = pl.BlockSpec((SEQ, q_per_kv * HEAD_DIM), lambda g: (0, g))
    shape = jax.ShapeDtypeStruct((SEQ, heads * HEAD_DIM), _F32)
    return pl.pallas_call(
        body, grid=(kv_heads,), in_specs=in_specs, out_specs=[o_spec, o_spec], out_shape=[shape, shape],
        scratch_shapes=[pltpu.VMEM(col, _F32)] * 5, name=name,
        compiler_params=_cparams(dimension_semantics=("parallel",)),
    )(*args)


def _band_bwd(proj, tabs, dmixed, o, lse, *, kv_heads, q_per_kv, q0, k0, v0, do0, dilations, lag_off, sink, name):
    heads = kv_heads * q_per_kv

    def body(*refs):
        q_refs = refs[:q_per_kv]
        k_ref, v_ref, c_ref, s1_ref, s2_ref = refs[q_per_kv:q_per_kv + 5]
        do_refs = refs[q_per_kv + 5:2 * q_per_kv + 5]
        o_ref, lse_ref = refs[2 * q_per_kv + 5:2 * q_per_kv + 7]
        rest = refs[2 * q_per_kv + 7:]
        if sink is not None:
            sk_ref, rest = rest[0], rest[1:]
            dq_ref, dk_ref, dv_ref, dsk_ref, qs, ks, dq_s, dk_s, dv_s = rest
        else:
            dq_ref, dk_ref, dv_ref, qs, ks, dq_s, dk_s, dv_s = rest
        c, s1, s2 = c_ref[...], s1_ref[...], s2_ref[...]
        ks[...] = _rope_apply(k_ref[...], c, s1, s2)
        dk_s[...] = jnp.zeros_like(dk_s)
        dv_s[...] = jnp.zeros_like(dv_s)
        for i in range(q_per_kv):
            hs = slice(i * HEAD_DIM, (i + 1) * HEAD_DIM)
            qs[...] = _rope_apply(q_refs[i][...], c, s1, s2)
            dq_s[...] = jnp.zeros_like(dq_s)
            do_ref = do_refs[i]
            for d in dilations:
                def step(it, carry, d=d, do_ref=do_ref, hs=hs):
                    n, cur, prev = _band_block(it, d)
                    q = qs[cur, :].astype(_MXU_DTYPE)
                    kp, kc = ks[prev, :].astype(_MXU_DTYPE), ks[cur, :].astype(_MXU_DTYPE)
                    vp, vc = v_ref[prev, :].astype(_MXU_DTYPE), v_ref[cur, :].astype(_MXU_DTYPE)
                    do = do_ref[cur, :]
                    delta = jnp.sum(do * o_ref[cur, hs], axis=1, keepdims=True)
                    lse_c = lse_ref[cur, hs][:, 0:1]
                    sp, sc = _band_scores(q, kp, kc, n, lag_off)
                    pp = jnp.exp(sp - lse_c)
                    pc = jnp.exp(sc - lse_c)
                    dob = do.astype(_MXU_DTYPE)
                    dsp = (pp * (lax.dot_general(dob, vp, _NT, preferred_element_type=_F32) - delta) * _SCALE).astype(_MXU_DTYPE)
                    dsc = (pc * (lax.dot_general(dob, vc, _NT, preferred_element_type=_F32) - delta) * _SCALE).astype(_MXU_DTYPE)
                    dq_s[cur, :] += jnp.dot(dsp, kp, preferred_element_type=_F32) + jnp.dot(dsc, kc, preferred_element_type=_F32)
                    dk_s[prev, :] += lax.dot_general(dsp, q, _TN, preferred_element_type=_F32)
                    dv_s[prev, :] += lax.dot_general(pp.astype(_MXU_DTYPE), dob, _TN, preferred_element_type=_F32)
                    dk_s[cur, :] += lax.dot_general(dsc, q, _TN, preferred_element_type=_F32)
                    dv_s[cur, :] += lax.dot_general(pc.astype(_MXU_DTYPE), dob, _TN, preferred_element_type=_F32)
                    return carry

                lax.fori_loop(0, _BAND_STEPS, step, 0)
            dq_ref[:, hs] = _rope_transpose(dq_s[...], c, s1, s2).astype(dq_ref.dtype)
            if sink is not None:
                delta = jnp.sum(do_ref[...] * o_ref[:, hs], axis=1, keepdims=True)
                w_sink = jnp.exp(sk_ref[i] - lse_ref[:, hs])
                dsk_ref[i] = jnp.broadcast_to(jnp.sum(-delta * w_sink[:, 0:1]), (8, HEAD_DIM))
        dk_ref[...] = _rope_transpose(dk_s[...], c, s1, s2).astype(dk_ref.dtype)
        dv_ref[...] = dv_s[...].astype(dv_ref.dtype)

    col = (SEQ, HEAD_DIM)
    in_specs = [pl.BlockSpec(col, functools.partial(lambda g, i: (0, A_COLS + q0 + g * q_per_kv + i), i=i)) for i in range(q_per_kv)]
    in_specs += [pl.BlockSpec(col, lambda g: (0, A_COLS + k0 + g)), pl.BlockSpec(col, lambda g: (0, A_COLS + v0 + g))]
    in_specs += [pl.BlockSpec(col, lambda g: (0, 0))] * 3
    in_specs += [pl.BlockSpec(col, functools.partial(lambda g, i: (0, do0 + g * q_per_kv + i), i=i)) for i in range(q_per_kv)]
    wide = pl.BlockSpec((SEQ, q_per_kv * HEAD_DIM), lambda g: (0, g))
    in_specs += [wide, wide]
    args = [proj] * (q_per_kv + 2) + list(tabs) + [dmixed] * q_per_kv + [o, lse]
    out_specs = [wide, pl.BlockSpec(col, lambda g: (0, g)), pl.BlockSpec(col, lambda g: (0, g))]
    out_shape = [jax.ShapeDtypeStruct((SEQ, heads * HEAD_DIM), _MXU_DTYPE), jax.ShapeDtypeStruct((SEQ, kv_heads * HEAD_DIM), _MXU_DTYPE),
                 jax.ShapeDtypeStruct((SEQ, kv_heads * HEAD_DIM), _MXU_DTYPE)]
    if sink is not None:
        in_specs.append(pl.BlockSpec((q_per_kv, 1, HEAD_DIM), lambda g: (g, 0, 0)))
        args.append(jnp.broadcast_to(sink.reshape(heads, 1, 1), (heads, 1, HEAD_DIM)))
        out_specs.append(pl.BlockSpec((q_per_kv, 8, HEAD_DIM), lambda g: (g, 0, 0)))
        out_shape.append(jax.ShapeDtypeStruct((heads, 8, HEAD_DIM), _F32))
    res = pl.pallas_call(
        body, grid=(kv_heads,), in_specs=in_specs, out_specs=out_specs, out_shape=out_shape,
        scratch_shapes=[pltpu.VMEM(col, _F32)] * 5, name=name,
        compiler_params=_cparams(dimension_semantics=("parallel",)),
    )(*args)
    if sink is not None:
        return res[0], res[1], res[2], res[3][:, 0, 0]
    return res


_DILATED = dict(kv_heads=B_HEADS, q_per_kv=1, q0=QB0, k0=KB0, v0=VB0, dilations=DILATIONS, lag_off=0, sink=None)
_SWA = dict(kv_heads=C_KV_HEADS, q_per_kv=C_HEADS // C_KV_HEADS, q0=QC0, k0=KC0, v0=VC0, dilations=(1,), lag_off=1)


_HG_TILE = 128
_HG_CHUNKS = _HG_TILE // A_CHUNK
_HG_TILES = SEQ // _HG_TILE
_HI = lax.Precision.HIGHEST


def _chunk_tri():
    i = np.arange(_HG_TILE)
    return jnp.asarray(((i[:, None] // A_CHUNK == i[None, :] // A_CHUNK) & (i[None, :] <= i[:, None])).astype(np.float32))


def _layer_lb(lb_ref, layer):
    if layer == 0:
        return jnp.zeros((1, HEAD_DIM), _F32)
    lg = lb_ref[...]
    m = jnp.max(lg, axis=0, keepdims=True)
    e = jnp.exp(lg - m)
    return e[1:2, :] / jnp.sum(e, axis=0, keepdims=True)


def _hgrn_gates(q, fr, lb):
    sgq = _sigmoid(q)
    sg = _sigmoid(fr)
    f = lb + (1.0 - lb) * sg
    return sgq, q * sgq, sg, f, 1.0 - f


def _hgrn_fwd(proj, lb_logits, norm_w, layer, *, name):
    tri = _chunk_tri()

    def body(q_ref, f_ref, i_ref, g_ref, lb_ref, nw_ref, tri_ref, o_ref, raw_ref, st_ref, state):
        @pl.when(pl.program_id(1) == 0)
        def _():
            state[...] = jnp.zeros_like(state)

        lb = _layer_lb(lb_ref, layer)
        _, qs, _, f, k = _hgrn_gates(q_ref[...], f_ref[...], lb)
        v = i_ref[...]
        b = jnp.dot(tri_ref[...], jnp.log(f), precision=_HI, preferred_element_type=_F32)
        eb = jnp.exp(b)
        ridx = lax.broadcasted_iota(jnp.int32, (A_CHUNK, HEAD_DIM), 0)
        outs = []
        for c in range(_HG_CHUNKS):
            sl = slice(c * A_CHUNK, (c + 1) * A_CHUNK)
            bc, qc, kc, vc = b[sl], qs[sl], k[sl], v[sl]
            bl = bc[A_CHUNK - 1:A_CHUNK]
            st = state[...]
            st_ref[0, c] = st
            o_c = lax.dot_general((qc * eb[sl]).astype(_MXU_DTYPE), st.astype(_MXU_DTYPE), _NT, preferred_element_type=_F32)
            rows = []
            for i in range(A_CHUNK):
                di = jnp.exp(jnp.where(ridx <= i, bc[i:i + 1] - bc, _NEG))
                a = jnp.sum(qc[i:i + 1] * kc * di, axis=1, keepdims=True)
                rows.append(jnp.sum(a * vc, axis=0, keepdims=True))
            outs.append(o_c + jnp.concatenate(rows, axis=0))
            kt = (kc * jnp.exp(bl - bc)).astype(_MXU_DTYPE)
            state[...] = st * jnp.exp(bl) + lax.dot_general(vc.astype(_MXU_DTYPE), kt, _TN, preferred_element_type=_F32)
        o = jnp.concatenate(outs, axis=0)
        raw_ref[...] = o
        r = lax.rsqrt(jnp.mean(o * o, axis=-1, keepdims=True) + LN_EPS)
        g = g_ref[...]
        o_ref[...] = o * r * nw_ref[...] * (g * _sigmoid(g))

    blk = (_HG_TILE, HEAD_DIM)

    def col(base):
        return pl.BlockSpec(blk, lambda h, t: (t, base + h))

    o_spec = pl.BlockSpec(blk, lambda h, t: (t, h))
    o_shape = jax.ShapeDtypeStruct((SEQ, A_HEADS * HEAD_DIM), _F32)
    return pl.pallas_call(
        body, grid=(A_HEADS, _HG_TILES),
        in_specs=[col(0), col(4), col(8), col(12), pl.BlockSpec((DEPTH, HEAD_DIM), lambda h, t: (0, h)),
                  pl.BlockSpec((1, HEAD_DIM), lambda h, t: (0, 0)), pl.BlockSpec(blk, lambda h, t: (0, 0))],
        out_specs=[o_spec, o_spec, pl.BlockSpec((1, _HG_CHUNKS, HEAD_DIM, HEAD_DIM), lambda h, t: (h, t, 0, 0))],
        out_shape=[o_shape, o_shape, jax.ShapeDtypeStruct((A_HEADS, SEQ // A_CHUNK, HEAD_DIM, HEAD_DIM), _F32)],
        scratch_shapes=[pltpu.VMEM((HEAD_DIM, HEAD_DIM), _F32)], name=name,
        compiler_params=_cparams(dimension_semantics=("parallel", "arbitrary")),
    )(proj, proj, proj, proj, lb_logits, norm_w.reshape(1, HEAD_DIM), tri)


def _hgrn_bwd(proj, lb_logits, norm_w, raw, states, dmixed, layer, *, name):
    tri = _chunk_tri()
    triu = tri.T

    def body(q_ref, f_ref, i_ref, g_ref, lb_ref, nw_ref, tri_ref, triu_ref, raw_ref, do_ref, st_ref,
             dq_ref, df_ref, di_ref, dg_ref, dnw_ref, dlb_ref, dstate):
        @pl.when(pl.program_id(1) == 0)
        def _():
            dstate[...] = jnp.zeros_like(dstate)
            dlb_ref[...] = jnp.zeros_like(dlb_ref)

        @pl.when((pl.program_id(0) == 0) & (pl.program_id(1) == 0))
        def _():
            dnw_ref[...] = jnp.zeros_like(dnw_ref)

        lb = _layer_lb(lb_ref, layer)
        q = q_ref[...]
        sgq, qs, sg, f, k = _hgrn_gates(q, f_ref[...], lb)
        v = i_ref[...]
        b = jnp.dot(tri_ref[...], jnp.log(f), precision=_HI, preferred_element_type=_F32)
        eb = jnp.exp(b)
        g = g_ref[...]
        nw = nw_ref[...]
        o = raw_ref[...]
        dout = do_ref[...]
        sgg = _sigmoid(g)
        r = lax.rsqrt(jnp.mean(o * o, axis=-1, keepdims=True) + LN_EPS)
        dg_ref[...] = (dout * (o * r * nw) * (sgg * (1.0 + g * (1.0 - sgg)))).astype(dg_ref.dtype)
        don = dout * (g * sgg)
        dnw_ref[0:1, :] += jnp.sum(don * o * r, axis=0, keepdims=True)
        dy = don * nw
        do_raw = r * dy - o * (r * r * r) * jnp.mean(o * dy, axis=-1, keepdims=True)

        ridx = lax.broadcasted_iota(jnp.int32, (A_CHUNK, HEAD_DIM), 0)
        dqs_t, dk_t, db_t, dv_t = [None] * _HG_CHUNKS, [None] * _HG_CHUNKS, [None] * _HG_CHUNKS, [None] * _HG_CHUNKS
        for c in reversed(range(_HG_CHUNKS)):
            sl = slice(c * A_CHUNK, (c + 1) * A_CHUNK)
            bc, qc, kc, vc, doc = b[sl], qs[sl], k[sl], v[sl], do_raw[sl]
            bl = bc[A_CHUNK - 1:A_CHUNK]
            ebc = eb[sl]
            ebl = jnp.exp(bl - bc)
            lam = jnp.exp(bl)
            qt = qc * ebc
            kt = kc * ebl
            dst = dstate[...]
            stp = st_ref[0, c]
            dob = doc.astype(_MXU_DTYPE)
            dstb = dst.astype(_MXU_DTYPE)
            dqt = jnp.dot(dob, stp.astype(_MXU_DTYPE), preferred_element_type=_F32)
            dkt = jnp.dot(vc.astype(_MXU_DTYPE), dstb, preferred_element_type=_F32)
            dv = lax.dot_general(kt.astype(_MXU_DTYPE), dstb, _NT, preferred_element_type=_F32)
            dlam = jnp.sum(stp * dst, axis=0, keepdims=True)
            dstate[...] = dst * lam + lax.dot_general(dob, qt.astype(_MXU_DTYPE), _TN, preferred_element_type=_F32)
            dqs_rows = []
            dk_in = jnp.zeros((A_CHUNK, HEAD_DIM), _F32)
            for i in range(A_CHUNK):
                di = jnp.exp(jnp.where(ridx <= i, bc[i:i + 1] - bc, _NEG))
                qi = qc[i:i + 1]
                doi = doc[i:i + 1]
                w = kc * di
                a = jnp.sum(qi * w, axis=1, keepdims=True)
                dv = dv + a * doi
                da = jnp.sum(doi * vc, axis=1, keepdims=True)
                dqs_rows.append(jnp.sum(da * w, axis=0, keepdims=True))
                dk_in = dk_in + da * (qi * di)
            dqs_in = jnp.concatenate(dqs_rows, axis=0)
            dbl = jnp.sum(dkt * kt, axis=0, keepdims=True) + dlam * lam
            db = qc * dqs_in - kc * dk_in + dqt * qt - dkt * kt
            db_t[c] = db + jnp.where(ridx == A_CHUNK - 1, dbl, 0.0)
            dqs_t[c] = dqs_in + dqt * ebc
            dk_t[c] = dk_in + dkt * ebl
            dv_t[c] = dv
        dqs = jnp.concatenate(dqs_t, axis=0)
        dk = jnp.concatenate(dk_t, axis=0)
        db = jnp.concatenate(db_t, axis=0)
        di_ref[...] = jnp.concatenate(dv_t, axis=0).astype(di_ref.dtype)
        dlogf = jnp.dot(triu_ref[...], db, precision=_HI, preferred_element_type=_F32)
        df = dlogf / f - dk
        df_ref[...] = (df * (1.0 - lb) * sg * (1.0 - sg)).astype(df_ref.dtype)
        dlb_ref[0, 0:1, :] += jnp.sum(df * (1.0 - sg), axis=0, keepdims=True)
        dq_ref[...] = (dqs * (sgq * (1.0 + q * (1.0 - sgq)))).astype(dq_ref.dtype)

    blk = (_HG_TILE, HEAD_DIM)
    last = _HG_TILES - 1

    def col(base):
        return pl.BlockSpec(blk, lambda h, t: (last - t, base + h))

    tri_spec = pl.BlockSpec(blk, lambda h, t: (0, 0))
    acc_spec = pl.BlockSpec((1, 8, HEAD_DIM), lambda h, t: (h, 0, 0))
    acc_shape = jax.ShapeDtypeStruct((A_HEADS, 8, HEAD_DIM), _F32)
    dq, df, di, dg, dnw, dlb = pl.pallas_call(
        body, grid=(A_HEADS, _HG_TILES),
        in_specs=[col(0), col(4), col(8), col(12), pl.BlockSpec((DEPTH, HEAD_DIM), lambda h, t: (0, h)),
                  pl.BlockSpec((1, HEAD_DIM), lambda h, t: (0, 0)), tri_spec, tri_spec, col(0), col(0),
                  pl.BlockSpec((1, _HG_CHUNKS, HEAD_DIM, HEAD_DIM), lambda h, t: (h, last - t, 0, 0))],
        out_specs=[col(0), col(0), col(0), col(0), pl.BlockSpec((8, HEAD_DIM), lambda h, t: (0, 0)), acc_spec],
        out_shape=[jax.ShapeDtypeStruct((SEQ, A_HEADS * HEAD_DIM), _MXU_DTYPE)] * 4
        + [jax.ShapeDtypeStruct((8, HEAD_DIM), _F32), acc_shape],
        scratch_shapes=[pltpu.VMEM((HEAD_DIM, HEAD_DIM), _F32)], name=name,
        compiler_params=_cparams(dimension_semantics=("arbitrary", "arbitrary")),
    )(proj, proj, proj, proj, lb_logits, norm_w.reshape(1, HEAD_DIM), tri, triu, raw, dmixed, states)
    return dq, df, di, dg, dnw[0], dlb[:, 0, :].reshape(A_HEADS * HEAD_DIM)


def _exchange(arrays, scatter, *, name):
    n = len(arrays)
    n_peer = N_DEV - 1

    def body(*refs):
        ins, outs = refs[:n], refs[n:2 * n]
        send_sems, recv_sems, loc_sems = refs[2 * n:]
        x, y, c = lax.axis_index("x"), lax.axis_index("y"), lax.axis_index("c")
        me = 4 * x + 2 * y + c
        local = []
        for a in range(n):
            cp = pltpu.make_async_copy(ins[a].at[me] if scatter else ins[a], outs[a].at[me], loc_sems.at[a])
            cp.start()
            local.append(cp)

        def peer(k):
            px = jnp.bitwise_xor(x, (k >> 2) & 1)
            py = jnp.bitwise_xor(y, (k >> 1) & 1)
            pc = jnp.bitwise_xor(c, k & 1)
            return (px, py, pc), 4 * px + 2 * py + pc

        def copy(a, k):
            dev, pid = peer(k)
            return pltpu.make_async_remote_copy(
                src_ref=ins[a].at[pid] if scatter else ins[a], dst_ref=outs[a].at[me],
                send_sem=send_sems.at[a * n_peer + k - 1], recv_sem=recv_sems.at[a * n_peer + k - 1],
                device_id=dev, device_id_type=pl.DeviceIdType.MESH)

        def arrival(a, k):
            dev, pid = peer(k)
            return pltpu.make_async_remote_copy(
                src_ref=ins[a].at[pid] if scatter else ins[a], dst_ref=outs[a].at[pid],
                send_sem=send_sems.at[a * n_peer + k - 1], recv_sem=recv_sems.at[a * n_peer + k - 1],
                device_id=dev, device_id_type=pl.DeviceIdType.MESH)

        sends = [copy(a, k) for k in range(1, N_DEV) for a in range(n)]
        for cp in sends:
            cp.start()
        for k in range(1, N_DEV):
            for a in range(n):
                arrival(a, k).wait_recv()
        for cp in sends:
            cp.wait_send()
        for cp in local:
            cp.wait()

    def out_shape(a):
        blk = a.shape[1:] if scatter else a.shape
        return jax.ShapeDtypeStruct((N_DEV,) + tuple(blk), a.dtype)

    any_spec = pl.BlockSpec(memory_space=pl.ANY)
    return pl.pallas_call(
        body, in_specs=[any_spec] * n, out_specs=[any_spec] * n, out_shape=[out_shape(a) for a in arrays],
        scratch_shapes=[pltpu.SemaphoreType.DMA((n * n_peer,)), pltpu.SemaphoreType.DMA((n * n_peer,)),
                        pltpu.SemaphoreType.DMA((n,))],
        name=name, compiler_params=pltpu.CompilerParams(has_side_effects=True),
    )(*arrays)


N_CHIP = N_DEV // 2
_MESH_ID = pl.DeviceIdType.MESH


def _place():
    x, y, c = lax.axis_index("x"), lax.axis_index("y"), lax.axis_index("c")
    chips = [(1 - x, y), (x, 1 - y), (1 - x, 1 - y)]
    return x, y, c, 2 * x + y, chips


def _gather_blocks(arrays, *, name):
    n = len(arrays)

    def body(*refs):
        ins, outs = refs[:n], refs[n:2 * n]
        send_sems, recv_sems, loc_sems = refs[2 * n:]
        x, y, c, _, chips = _place()
        me = 4 * x + 2 * y + c
        sibling = (x, y, 1 - c)

        def slot(px, py, pc):
            return 4 * px + 2 * py + pc

        def copy(a, k, block, to, src=None):
            dst = outs[a].at[slot(*block)]
            return pltpu.make_async_remote_copy(
                src_ref=dst if src is None else src, dst_ref=dst, send_sem=send_sems.at[7 * a + k],
                recv_sem=recv_sems.at[7 * a + k], device_id=to, device_id_type=_MESH_ID)

        local = [pltpu.make_async_copy(ins[a], outs[a].at[me], loc_sems.at[a]) for a in range(n)]
        for cp in local:
            cp.start()
        first = []
        for a in range(n):
            first.append(copy(a, 0, (x, y, c), sibling, src=ins[a]))
            first += [copy(a, 1 + j, (x, y, c), (*chip, c), src=ins[a]) for j, chip in enumerate(chips)]
        for cp in first:
            cp.start()
        passed = []
        for a in range(n):
            for j, chip in enumerate(chips):
                copy(a, 1 + j, (*chip, c), (x, y, c)).wait_recv()
                fwd = copy(a, 4 + j, (*chip, c), sibling)
                fwd.start()
                passed.append(fwd)
        for a in range(n):
            copy(a, 0, sibling, (x, y, c)).wait_recv()
            for j, chip in enumerate(chips):
                copy(a, 4 + j, (*chip, 1 - c), (x, y, c)).wait_recv()
        for cp in first + passed:
            cp.wait_send()
        for cp in local:
            cp.wait()

    any_spec = pl.BlockSpec(memory_space=pl.ANY)
    return pl.pallas_call(
        body, in_specs=[any_spec] * n, out_specs=[any_spec] * n,
        out_shape=[jax.ShapeDtypeStruct((N_DEV,) + a.shape, a.dtype) for a in arrays],
        scratch_shapes=[pltpu.SemaphoreType.DMA((7 * n,)), pltpu.SemaphoreType.DMA((7 * n,)), pltpu.SemaphoreType.DMA((n,))],
        name=name, compiler_params=pltpu.CompilerParams(has_side_effects=True),
    )(*arrays)


def _sibling_swap(arrays, *, name):
    n = len(arrays)

    def body(*refs):
        ins, outs = refs[:n], refs[n:2 * n]
        send_sems, recv_sems = refs[2 * n:]
        x, y, c, _, _ = _place()
        copies = [pltpu.make_async_remote_copy(
            src_ref=ins[a].at[:, 1 - c], dst_ref=outs[a], send_sem=send_sems.at[a], recv_sem=recv_sems.at[a],
            device_id=(x, y, 1 - c), device_id_type=_MESH_ID) for a in range(n)]
        for cp in copies:
            cp.start()
        for cp in copies:
            cp.wait()

    any_spec = pl.BlockSpec(memory_space=pl.ANY)
    return pl.pallas_call(
        body, in_specs=[any_spec] * n, out_specs=[any_spec] * n,
        out_shape=[jax.ShapeDtypeStruct((N_CHIP,) + a.shape[2:], a.dtype) for a in arrays],
        scratch_shapes=[pltpu.SemaphoreType.DMA((n,)), pltpu.SemaphoreType.DMA((n,))],
        name=name, compiler_params=pltpu.CompilerParams(has_side_effects=True),
    )(*arrays)


def _pair_add(mine, theirs, core, *, name):
    _, _, R, C = mine.shape
    tr = max(t for t in range(16, R + 1, 16) if R % t == 0 and t * C <= 512 * 1024)

    def body(core_ref, m_ref, t_ref, o_ref):
        del core_ref
        o_ref[...] = (m_ref[...].astype(_F32) + t_ref[...].astype(_F32)).astype(o_ref.dtype)

    grid_spec = pltpu.PrefetchScalarGridSpec(
        num_scalar_prefetch=1, grid=(N_CHIP, R // tr),
        in_specs=[pl.BlockSpec((None, None, tr, C), lambda q, i, core: (q, core[0], i, 0)),
                  pl.BlockSpec((None, tr, C), lambda q, i, core: (q, i, 0))],
        out_specs=pl.BlockSpec((None, tr, C), lambda q, i, core: (q, i, 0)))
    return pl.pallas_call(
        body, grid_spec=grid_spec, out_shape=jax.ShapeDtypeStruct((N_CHIP, R, C), mine.dtype), name=name,
        compiler_params=_cparams(dimension_semantics=("parallel", "parallel")),
    )(core.reshape(1), mine, theirs)


_HBM = pl.BlockSpec(memory_space=pltpu.HBM)
_SEM = pl.BlockSpec(memory_space=pltpu.SEMAPHORE)
_TOKEN = pl.BlockSpec(memory_space=pltpu.VMEM)
_DATAFLOW = pltpu.SideEffectType.DATAFLOW_SIDE_EFFECTING


def _hbm(a):
    return pltpu.HBM(a.shape, a.dtype)


def _token_shape():
    return jax.ShapeDtypeStruct((8, 128), _F32)


def _dev_slot(px, py, pc):
    return 4 * px + 2 * py + pc


def _gather_start(blocks, landings, *, name):
    n = len(blocks)

    def body(*refs):
        ins, lands = refs[:n], refs[n:2 * n]
        send_sems, d2d_sems, ici_sems = refs[2 * n:2 * n + 3]
        token = refs[-1]
        x, y, c, _, chips = _place()
        for a in range(n):
            dst = lands[a].at[_dev_slot(x, y, c)]
            pltpu.make_async_remote_copy(src_ref=ins[a], dst_ref=dst, send_sem=send_sems.at[4 * a], recv_sem=d2d_sems.at[a],
                                         device_id=(x, y, 1 - c), device_id_type=_MESH_ID).start()
            for j, chip in enumerate(chips):
                pltpu.make_async_remote_copy(src_ref=ins[a], dst_ref=dst, send_sem=send_sems.at[4 * a + 1 + j],
                                             recv_sem=ici_sems.at[3 * a + j], device_id=(*chip, c),
                                             device_id_type=_MESH_ID).start()
        token[...] = jnp.zeros_like(token)

    res = pl.pallas_call(
        body, name=name, in_specs=[_HBM] * (2 * n),
        out_shape=(pltpu.SemaphoreType.DMA((4 * n,)), pltpu.SemaphoreType.DMA((n,)), pltpu.SemaphoreType.DMA((3 * n,)),
                   *[_hbm(b) for b in blocks], *[_hbm(b) for b in landings], _token_shape()),
        out_specs=(_SEM, _SEM, _SEM, *[_HBM] * (2 * n), _TOKEN),
        input_output_aliases={i: 3 + i for i in range(2 * n)},
        compiler_params=pltpu.CompilerParams(has_side_effects=_DATAFLOW),
    )(*[pltpu.with_memory_space_constraint(b, pltpu.HBM) for b in blocks],
      *[pltpu.with_memory_space_constraint(b, pltpu.HBM) for b in landings])
    return res[0], res[1], res[2], list(res[3:3 + n]), list(res[3 + n:3 + 2 * n]), res[-1]


def _gather_forward(landings, ici_sems, after, *, name):
    n = len(landings)

    def body(*refs):
        lands = refs[:n]
        ici = refs[n]
        f_send, f_recv = refs[n + 2], refs[n + 3]
        token = refs[-1]
        x, y, c, _, chips = _place()
        for a in range(n):
            for j, chip in enumerate(chips):
                blk = lands[a].at[_dev_slot(*chip, c)]
                pltpu.make_async_remote_copy(src_ref=blk, dst_ref=blk, send_sem=f_send.at[3 * a + j], recv_sem=ici.at[3 * a + j],
                                             device_id=(*chip, c), device_id_type=_MESH_ID).wait_recv()
                pltpu.make_async_remote_copy(src_ref=blk, dst_ref=blk, send_sem=f_send.at[3 * a + j], recv_sem=f_recv.at[3 * a + j],
                                             device_id=(x, y, 1 - c), device_id_type=_MESH_ID).start()
        token[...] = jnp.zeros_like(token)

    res = pl.pallas_call(
        body, name=name, in_specs=[_HBM] * n + [_SEM, pl.BlockSpec(memory_space=pl.ANY)],
        out_shape=(pltpu.SemaphoreType.DMA((3 * n,)), pltpu.SemaphoreType.DMA((3 * n,)), *[_hbm(b) for b in landings], _token_shape()),
        out_specs=(_SEM, _SEM, *[_HBM] * n, _TOKEN),
        input_output_aliases={i: 2 + i for i in range(n)},
        compiler_params=pltpu.CompilerParams(has_side_effects=_DATAFLOW),
    )(*landings, ici_sems, after)
    return res[0], res[1], list(res[2:2 + n]), res[-1]


def _gather_wait(blocks, landings, send_sems, d2d_sems, f_send, f_recv, after, *, name):
    n = len(landings)

    def body(*refs):
        ins, lands = refs[:n], refs[n:2 * n]
        send, d2d, fs, fr = refs[2 * n:2 * n + 4]
        x, y, c, _, chips = _place()
        me = (x, y, c)
        for a in range(n):
            own = lands[a].at[_dev_slot(x, y, 1 - c)]
            pltpu.make_async_remote_copy(src_ref=ins[a], dst_ref=own, send_sem=send.at[4 * a], recv_sem=d2d.at[a],
                                         device_id=me, device_id_type=_MESH_ID).wait_recv()
            for j, chip in enumerate(chips):
                blk = lands[a].at[_dev_slot(*chip, 1 - c)]
                pltpu.make_async_remote_copy(src_ref=blk, dst_ref=blk, send_sem=fs.at[3 * a + j], recv_sem=fr.at[3 * a + j],
                                             device_id=me, device_id_type=_MESH_ID).wait_recv()
            for k in range(4):
                pltpu.make_async_remote_copy(src_ref=ins[a], dst_ref=own, send_sem=send.at[4 * a + k], recv_sem=d2d.at[a],
                                             device_id=me, device_id_type=_MESH_ID).wait_send()
            for j in range(3):
                pltpu.make_async_remote_copy(src_ref=own, dst_ref=own, send_sem=fs.at[3 * a + j], recv_sem=fr.at[3 * a + j],
                                             device_id=me, device_id_type=_MESH_ID).wait_send()

    res = pl.pallas_call(
        body, name=name, in_specs=[_HBM] * (2 * n) + [_SEM] * 4 + [pl.BlockSpec(memory_space=pl.ANY)],
        out_shape=(*[_hbm(b) for b in blocks], *[_hbm(b) for b in landings]), out_specs=tuple([_HBM] * (2 * n)),
        input_output_aliases={i: i for i in range(2 * n)},
        compiler_params=pltpu.CompilerParams(has_side_effects=_DATAFLOW),
    )(*blocks, *landings, send_sems, d2d_sems, f_send, f_recv, after)
    return list(res[n:])


def _chip_exchange_start(sums, landings, *, name):
    n = len(sums)

    def body(*refs):
        ins, lands = refs[:n], refs[n:2 * n]
        send_sems, recv_sems = refs[2 * n:2 * n + 2]
        token = refs[-1]
        _, _, c, p, chips = _place()
        for a in range(n):
            for j, (qx, qy) in enumerate(chips):
                pltpu.make_async_remote_copy(src_ref=ins[a].at[2 * qx + qy], dst_ref=lands[a].at[p], send_sem=send_sems.at[3 * a + j],
                                             recv_sem=recv_sems.at[3 * a + j], device_id=(qx, qy, c), device_id_type=_MESH_ID).start()
        token[...] = jnp.zeros_like(token)

    res = pl.pallas_call(
        body, name=name, in_specs=[_HBM] * (2 * n),
        out_shape=(pltpu.SemaphoreType.DMA((3 * n,)), pltpu.SemaphoreType.DMA((3 * n,)),
                   *[_hbm(b) for b in sums], *[_hbm(b) for b in landings], _token_shape()),
        out_specs=(_SEM, _SEM, *[_HBM] * (2 * n), _TOKEN),
        input_output_aliases={i: 2 + i for i in range(2 * n)},
        compiler_params=pltpu.CompilerParams(has_side_effects=_DATAFLOW),
    )(*[pltpu.with_memory_space_constraint(b, pltpu.HBM) for b in sums],
      *[pltpu.with_memory_space_constraint(b, pltpu.HBM) for b in landings])
    return res[0], res[1], list(res[2:2 + n]), list(res[2 + n:2 + 2 * n]), res[-1]


def _chip_exchange_wait(sums, landings, send_sems, recv_sems, after, *, name):
    n = len(sums)

    def body(*refs):
        ins, lands = refs[:n], refs[n:2 * n]
        send, recv = refs[2 * n:2 * n + 2]
        x, y, c, _, chips = _place()
        for a in range(n):
            for j, (qx, qy) in enumerate(chips):
                q = 2 * qx + qy
                cp = pltpu.make_async_remote_copy(src_ref=ins[a].at[q], dst_ref=lands[a].at[q], send_sem=send.at[3 * a + j],
                                                  recv_sem=recv.at[3 * a + j], device_id=(x, y, c), device_id_type=_MESH_ID)
                cp.wait_recv()
                cp.wait_send()

    res = pl.pallas_call(
        body, name=name, in_specs=[_HBM] * (2 * n) + [_SEM] * 2 + [pl.BlockSpec(memory_space=pl.ANY)],
        out_shape=(*[_hbm(b) for b in sums], *[_hbm(b) for b in landings]), out_specs=tuple([_HBM] * (2 * n)),
        input_output_aliases={i: i for i in range(2 * n)},
        compiler_params=pltpu.CompilerParams(has_side_effects=_DATAFLOW),
    )(*sums, *landings, send_sems, recv_sems, after)
    return list(res[:n]), list(res[n:])


_C1 = 1.0 - ADAM_B1 ** ADAM_STEP
_C2 = 1.0 - ADAM_B2 ** ADAM_STEP


def _adamw_math(g, w, m, v):
    m = ADAM_B1 * m + (1.0 - ADAM_B1) * g
    v = ADAM_B2 * v + (1.0 - ADAM_B2) * (g * g)
    delta = -ADAM_LR * ((m / _C1) / (jnp.sqrt(v / _C2) + ADAM_EPS) + ADAM_WD * w)
    return delta, m, v


def _adamw_reduce(landed, sums, chip, w, m, v, layer, prev, *, name):
    _, R, C = w.shape
    tr = max(t for t in range(16, R + 1, 16) if R % t == 0 and t * C <= 256 * 1024)

    def body(chip_ref, p_ref, own_ref, w_ref, m_ref, v_ref, *rest):
        g_ref, d_ref, nm_ref, nv_ref = rest[-4:]
        own = own_ref[...].astype(_F32)
        g = jnp.where(chip_ref[0] == 0, own, p_ref[0].astype(_F32))
        for q in range(1, N_CHIP):
            g = g + jnp.where(chip_ref[0] == q, own, p_ref[q].astype(_F32))
        d, nm, nv = _adamw_math(g, w_ref[...], m_ref[...], v_ref[...])
        g_ref[...] = g
        d_ref[...] = d
        nm_ref[...] = nm
        nv_ref[...] = nv

    blk = pl.BlockSpec((None, tr, C), lambda i, chip: (layer, i, 0))
    shape = jax.ShapeDtypeStruct((DEPTH, R, C), _F32)
    kept = [] if prev is None else list(prev)
    grid_spec = pltpu.PrefetchScalarGridSpec(
        num_scalar_prefetch=1, grid=(R // tr,),
        in_specs=[pl.BlockSpec((N_CHIP, tr, C), lambda i, chip: (0, i, 0)),
                  pl.BlockSpec((None, tr, C), lambda i, chip: (chip[0], i, 0)), blk, blk, blk]
        + [pl.BlockSpec(memory_space=pl.ANY)] * len(kept),
        out_specs=[blk] * 4)
    return pl.pallas_call(
        body, grid_spec=grid_spec, out_shape=[shape] * 4, name=name,
        input_output_aliases={6 + k: k for k in range(len(kept))},
        compiler_params=_cparams(dimension_semantics=("parallel",)),
    )(chip.reshape(1), landed, sums, w, m, v, *kept)


_PACK_LANES = 128
_LAYER_ROWS = 248
_LB_ROWS = (A_HEADS * HEAD_DIM) // _PACK_LANES


def _small_reduce(parts, lb_logits, *, name):
    rows = DEPTH * _LAYER_ROWS

    def body(p_ref, lg_ref, o_ref):
        g = p_ref[0]
        for s in range(1, N_DEV):
            g = g + p_ref[s]
        o_ref[...] = g
        lg = lg_ref[...]
        e = jnp.exp(lg - jnp.max(lg, axis=0, keepdims=True))
        p = e / jnp.sum(e, axis=0, keepdims=True)
        d1 = g[_LAYER_ROWS:_LAYER_ROWS + _LB_ROWS, :] * p[0] * p[1]
        o_ref[0:_LB_ROWS, :] = -d1
        o_ref[_LAYER_ROWS:_LAYER_ROWS + _LB_ROWS, :] = d1

    return pl.pallas_call(
        body, out_shape=jax.ShapeDtypeStruct((rows, _PACK_LANES), _F32), name=name,
        compiler_params=_cparams(),
    )(parts, lb_logits.reshape(DEPTH, _LB_ROWS, _PACK_LANES))


def _adamw_small(g, w, m, v, *, name):
    def body(g_ref, w_ref, m_ref, v_ref, d_ref, nm_ref, nv_ref):
        d, nm, nv = _adamw_math(g_ref[...], w_ref[...], m_ref[...], v_ref[...])
        d_ref[...] = d
        nm_ref[...] = nm
        nv_ref[...] = nv

    shape = jax.ShapeDtypeStruct(g.shape, _F32)
    return pl.pallas_call(body, out_shape=[shape] * 3, name=name, compiler_params=_cparams())(g, w, m, v)


def _pack(vectors, rows):
    flat = jnp.concatenate([v.reshape(-1).astype(_F32) for v in vectors])
    return jnp.pad(flat, (0, rows * _PACK_LANES - flat.shape[0])).reshape(rows, _PACK_LANES)


def _unpack(packed, shapes):
    flat = packed.reshape(-1)
    out, at = [], 0
    for s in shapes:
        size = int(np.prod(s))
        out.append(flat[at:at + size].reshape(s))
        at += size
    return out


_BIG = ("w_in", "w_gate", "w_up", "w_out", "w_down")
_COLUMN_SHARDED = ("w_in", "w_gate", "w_up")


def _full_weight(name, g):
    if name == "w_out":
        return g.reshape(D_MODEL, D_MODEL)
    if name == "w_down":
        return g.reshape(D_FF, D_MODEL)
    if name == "conv_w":
        return g.transpose(1, 0, 2).reshape(g.shape[1], N_DEV * SHARD_COLS)
    return g


class _WeightGather:
    def __init__(self, names, blocks, me, tag):
        self.names, self.tag = names, tag
        landings = [lax.dynamic_update_index_in_dim(lax.empty((N_DEV,) + b.shape, b.dtype), b[None], me, 0) for b in blocks]
        self.send, self.d2d, self.ici, self.blocks, self.lands, self.token = _gather_start(
            blocks, landings, name=f"gather_start_{tag}")

    def forward(self, after):
        self.f_send, self.f_recv, self.lands, token = _gather_forward(self.lands, self.ici, after, name=f"gather_forward_{self.tag}")
        return token

    def wait(self, after):
        got = _gather_wait(self.blocks, self.lands, self.send, self.d2d, self.f_send, self.f_recv, after,
                           name=f"gather_wait_{self.tag}")
        return {n: _full_weight(n, g) for n, g in zip(self.names, got)}


class _LayerWeights:
    def __init__(self, ready, pending=(), forwards=(), tokens=()):
        self.ready, self.pending, self.forwards, self._tokens = dict(ready), list(pending), list(forwards), list(tokens)

    def at(self, point, after):
        for when, gather in self.forwards:
            if when == point:
                self._tokens.append(gather.forward(after))

    def tokens(self):
        out, self._tokens = self._tokens, []
        return out

    def get(self, name, after):
        if name not in self.ready:
            group, = [g for g in self.pending if name in g.names]
            self.ready.update(group.wait(after))
        return self.ready[name]


def _layer_fwd(x, xb, ws, lb_logits, a_norm_w, c_sink, ln1_g, ln1_b, conv_b, ln2_g, ln2_b, tabs, l):
    proj = _mm_w_slabs(xb, ws.get("w_in", xb), tm=1024, after=ws.tokens(), name=f"proj_{l}")
    o_a, raw, states = _hgrn_fwd(proj, lb_logits, a_norm_w, l, name=f"hgrn_fwd_{l}")
    o_b, lse_b = _band_fwd(proj, tabs, name=f"dilated_fwd_{l}", **_DILATED)
    ws.at("dilated", o_b)
    o_c, lse_c = _band_fwd(proj, tabs, sink=c_sink, name=f"swa_fwd_{l}", **_SWA)
    mixed = jnp.concatenate([o_a, o_b, o_c], axis=1).astype(_MXU_DTYPE)
    y = _mm(mixed, ws.get("w_out", mixed), tm=1024, tn=512, after=ws.tokens(), name=f"mix_out_{l}")
    z1, x1, x1b = _ln_fwd(x, y, ln1_g, ln1_b, name=f"ln1_fwd_{l}")
    g = _mm_w_slabs(x1b, ws.get("w_gate", x1b), tm=1024, name=f"ffn_gate_{l}")
    u = _mm_w_slabs(x1b, ws.get("w_up", x1b), tm=1024, name=f"ffn_up_{l}")
    hb = _conv_gate_fwd(g, u, ws.get("conv_w", u), conv_b, name=f"conv_gate_fwd_{l}")
    ws.at("conv", hb)
    y2 = _mm(hb, ws.get("w_down", hb), tm=512, tn=512, after=ws.tokens(), name=f"ffn_down_{l}")
    z2, x2, x2b = _ln_fwd(x1, y2, ln2_g, ln2_b, name=f"ln2_fwd_{l}")
    res = dict(xb=xb, proj=proj, raw=raw, states=states, o_b=o_b, lse_b=lse_b, o_c=o_c, lse_c=lse_c,
               mixed=mixed, z1=z1, x1b=x1b, g=g, u=u, hb=hb, z2=z2)
    return x2, x2b, res


class _GradExchange:
    def __init__(self, core, chip):
        self.core, self.chip, self.groups, self._tokens = core, chip, [], []

    def launch(self, names, slabs, l, tag):
        mine = [s.reshape((N_CHIP, 2) + s.shape[1:]) for s in slabs]
        theirs = _sibling_swap(mine, name=f"swap_grads_{tag}")
        sums = [_pair_add(a, b, self.core, name=f"pair_add_{n}_{l}") for n, a, b in zip(names, mine, theirs)]
        landings = [lax.empty(s.shape, s.dtype) for s in sums]
        send, recv, sums, landings, token = _chip_exchange_start(sums, landings, name=f"exchange_start_{tag}")
        self.groups.append((names, l, tag, send, recv, sums, landings))
        self._tokens.append(token)

    def tokens(self):
        out, self._tokens = self._tokens, []
        return out

    def finish(self, weights, mom1, mom2, after):
        out = {}
        after = list(after) + self.tokens()
        for names, l, tag, send, recv, sums, landings in self.groups:
            sums, landings = _chip_exchange_wait(sums, landings, send, recv, after[-1], name=f"exchange_wait_{tag}")
            for n, s, landed in zip(names, sums, landings):
                out[n] = _adamw_reduce(landed, s, self.chip, weights[n], mom1[n], mom2[n], l, out.get(n), name=f"adamw_{n}_{l}")
                after = [out[n][0]]
        return out


def _layer_bwd(dx2, res, w, lb_logits, a_norm_w, c_sink, ln1_g, conv_b, ln2_g, tabs, exchange, l):
    dz2, dz2b, d_ln2_g, d_ln2_b = _ln_bwd(res["z2"], dx2, None, ln2_g, name=f"ln2_bwd_{l}")
    dh = _mm(dz2b, w["w_down"], tb=True, tm=1024, tn=512, after=exchange.tokens(), name=f"ffn_down_dx_{l}")
    d_w_down = _mm(res["hb"], dz2b, ta=True, tm=512, tn=512, out_dtype=_GRAD_DTYPE, name=f"ffn_down_dw_{l}")
    dg, du, d_conv_w, d_conv_b = _conv_gate_bwd(dh, res["g"], res["u"], w["conv_w"], conv_b, name=f"conv_gate_bwd_{l}")
    t = _mm_nt_w_slabs(dg, w["w_gate"], tm=512, tn=512, name=f"ffn_gate_dx_{l}")
    dx1 = _mm_nt_w_slabs(du, w["w_up"], tm=512, tn=512, add=t, name=f"ffn_up_dx_{l}")
    d_w_gate = _mm_tn_slabs(res["x1b"], dg, tm=1024, name=f"ffn_gate_dw_{l}")
    d_w_up = _mm_tn_slabs(res["x1b"], du, tm=1024, name=f"ffn_up_dw_{l}")
    dz1, dz1b, d_ln1_g, d_ln1_b = _ln_bwd(res["z1"], dx1, dz2, ln1_g, name=f"ln1_bwd_{l}")
    d_w_out = _mm(res["mixed"], dz1b, ta=True, tm=1024, tn=512, out_dtype=_GRAD_DTYPE, name=f"mix_out_dw_{l}")
    exchange.launch(("w_down", "w_gate", "w_up", "w_out"),
                    [d_w_down.reshape(N_DEV, D_FF // N_DEV, D_MODEL), d_w_gate, d_w_up,
                     d_w_out.reshape(N_DEV, D_MODEL // N_DEV, D_MODEL)], l, f"ffn_{l}")
    dmixed = _mm(dz1b, w["w_out"], tb=True, tm=1024, tn=512, after=exchange.tokens(), name=f"mix_out_dx_{l}")
    dq_a, df_a, di_a, dg_a, d_norm_w, d_lb = _hgrn_bwd(res["proj"], lb_logits, a_norm_w, res["raw"], res["states"],
                                                      dmixed, l, name=f"hgrn_bwd_{l}")
    dq_b, dk_b, dv_b = _band_bwd(res["proj"], tabs, dmixed, res["o_b"], res["lse_b"], do0=A_HEADS,
                                 name=f"dilated_bwd_{l}", **_DILATED)
    dq_c, dk_c, dv_c, d_sink = _band_bwd(res["proj"], tabs, dmixed, res["o_c"], res["lse_c"], do0=A_HEADS + B_HEADS,
                                         sink=c_sink, name=f"swa_bwd_{l}", **_SWA)
    dproj = jnp.concatenate([dq_a, df_a, di_a, dg_a, dq_b, dk_b, dv_b, dq_c, dk_c, dv_c], axis=1)
    d_w_in = _mm_tn_slabs(res["xb"], dproj, tm=1024, name=f"proj_dw_{l}")
    exchange.launch(("w_in",), [d_w_in], l, f"mix_{l}")
    dx = _mm_nt_w_slabs(dproj, w["w_in"], tm=512, tn=512, add=dz1, add_scale=ALPHA, after=exchange.tokens(),
                        name=f"proj_dx_{l}")
    small = [d_lb, d_norm_w, jnp.pad(d_sink, (0, _PACK_LANES - C_HEADS)), d_ln1_g, d_ln1_b, d_ln2_g, d_ln2_b, d_conv_b,
             d_conv_w]
    return dx, small


def kernel(x, w_in, lb_logits, a_norm_w, c_sinks, w_out, ln1_g, ln1_b, w_gate, w_up, conv_w, conv_b, w_down, ln2_g, ln2_b, loss_target, m_w_in, m_lb_logits, m_a_norm_w, m_c_sinks, m_w_out, m_ln1_g, m_ln1_b, m_w_gate, m_w_up, m_conv_w, m_conv_b, m_w_down, m_ln2_g, m_ln2_b, v_w_in, v_lb_logits, v_a_norm_w, v_c_sinks, v_w_out, v_ln1_g, v_ln1_b, v_w_gate, v_w_up, v_conv_w, v_conv_b, v_w_down, v_ln2_g, v_ln2_b):
    weights = dict(w_in=w_in, lb_logits=lb_logits, a_norm_w=a_norm_w, c_sinks=c_sinks, w_out=w_out, ln1_g=ln1_g, ln1_b=ln1_b,
                   w_gate=w_gate, w_up=w_up, conv_w=conv_w, conv_b=conv_b, w_down=w_down, ln2_g=ln2_g, ln2_b=ln2_b)
    mom1 = dict(w_in=m_w_in, lb_logits=m_lb_logits, a_norm_w=m_a_norm_w, c_sinks=m_c_sinks, w_out=m_w_out, ln1_g=m_ln1_g,
                ln1_b=m_ln1_b, w_gate=m_w_gate, w_up=m_w_up, conv_w=m_conv_w, conv_b=m_conv_b, w_down=m_w_down, ln2_g=m_ln2_g,
                ln2_b=m_ln2_b)
    mom2 = dict(w_in=v_w_in, lb_logits=v_lb_logits, a_norm_w=v_a_norm_w, c_sinks=v_c_sinks, w_out=v_w_out, ln1_g=v_ln1_g,
                ln1_b=v_ln1_b, w_gate=v_w_gate, w_up=v_w_up, conv_w=v_conv_w, conv_b=v_conv_b, w_down=v_w_down, ln2_g=v_ln2_g,
                ln2_b=v_ln2_b)
    core = lax.axis_index("c").astype(jnp.int32)
    me = 4 * lax.axis_index("x") + 2 * lax.axis_index("y") + core
    tabs = _rope_tables()

    chip = (2 * lax.axis_index("x") + lax.axis_index("y")).astype(jnp.int32)

    def block(n, l):
        return conv_w[l] if n == "conv_w" else weights[n][l].astype(_MXU_DTYPE)

    first, = _gather_blocks([block("w_in", 0)], name="gather_w_in_0")
    rest = ("w_out", "w_gate", "w_up", "conv_w", "w_down")
    rest0 = _WeightGather(rest, [block(n, 0) for n in rest], me, "rest_0")
    in1 = _WeightGather(("w_in",), [block("w_in", 1)], me, "w_in_1")
    rest1 = _WeightGather(rest, [block(n, 1) for n in rest], me, "rest_1")
    layer_ws = [_LayerWeights({"w_in": _full_weight("w_in", first)}, [rest0], [("dilated", rest0), ("conv", in1)],
                              [rest0.token, in1.token, rest1.token]),
                _LayerWeights({}, [in1, rest1], [("dilated", rest1)])]

    xs = x[0]
    xb = xs.astype(_MXU_DTYPE)
    saved = []
    for l in range(DEPTH):
        xs, xb, res = _layer_fwd(xs, xb, layer_ws[l], lb_logits, a_norm_w[l], c_sinks[l], ln1_g[l], ln1_b[l], conv_b[l],
                                 ln2_g[l], ln2_b[l], tabs, l)
        saved.append(res)
    loss_part, dx = _loss_head(xs, loss_target[0], name="loss_head")
    loss = lax.psum(loss_part, ("x", "y", "c"))

    exchange = _GradExchange(core, chip)
    small_parts = [None] * DEPTH
    for l in reversed(range(DEPTH)):
        dx, small = _layer_bwd(dx, saved[l], layer_ws[l].ready, lb_logits, a_norm_w[l], c_sinks[l], ln1_g[l], conv_b[l],
                               ln2_g[l], tabs, exchange, l)
        small_parts[l] = _pack(small, _LAYER_ROWS)
    def as_slabs(d):
        return {n: jnp.swapaxes(d[n], 1, 2) if n in _COLUMN_SHARDED else d[n] for n in _BIG}

    updated = exchange.finish(as_slabs(weights), as_slabs(mom1), as_slabs(mom2), [dx])
    updated = {n: tuple(jnp.swapaxes(t, 1, 2) for t in u) if n in _COLUMN_SHARDED else u for n, u in updated.items()}
    gathered, = _exchange([jnp.concatenate(small_parts, axis=0)], False, name="gather_small_grads")
    g_small = _small_reduce(gathered, lb_logits, name="small_grads")

    per_layer = [(A_HEADS * HEAD_DIM,), (HEAD_DIM,), (_PACK_LANES,), (D_MODEL,), (D_MODEL,), (D_MODEL,), (D_MODEL,), (D_FF,),
                 (3, D_FF)]
    names = ("lb_logits", "a_norm_w", "c_sinks", "ln1_g", "ln1_b", "ln2_g", "ln2_b", "conv_b", "conv_w")
    grads = {n: [] for n in names}
    for l in range(DEPTH):
        for n, t in zip(names, _unpack(g_small[l * _LAYER_ROWS:(l + 1) * _LAYER_ROWS], per_layer)):
            grads[n].append(t)
    grads = {n: jnp.stack(t) for n, t in grads.items()}
    grads["c_sinks"] = grads["c_sinks"][:, :C_HEADS]
    grads["conv_w"] = lax.dynamic_slice_in_dim(grads["conv_w"], me * SHARD_COLS, SHARD_COLS, axis=2)
    shapes = [grads[n].shape for n in names]
    rows = -(-sum(int(np.prod(s)) for s in shapes) // (8 * _PACK_LANES)) * 8
    d_s, m_s, v_s = _adamw_small(_pack([grads[n] for n in names], rows), _pack([weights[n] for n in names], rows),
                                 _pack([mom1[n] for n in names], rows), _pack([mom2[n] for n in names], rows),
                                 name="adamw_small")
    delta = dict(zip(names, _unpack(d_s, shapes)))
    new_m = dict(zip(names, _unpack(m_s, shapes)))
    new_v = dict(zip(names, _unpack(v_s, shapes)))
    for n in _BIG:
        grads[n], delta[n], new_m[n], new_v[n] = updated[n]

    order = ("w_in", "lb_logits", "a_norm_w", "c_sinks", "w_out", "ln1_g", "ln1_b", "w_gate", "w_up", "conv_w", "conv_b",
             "w_down", "ln2_g", "ln2_b")
    return (loss, dx[None], *[grads[n] for n in order], *[delta[n] for n in order], *[new_m[n] for n in order],
            *[new_v[n] for n in order])
```

```python
import functools

import jax
import jax.numpy as jnp
import numpy as np
from jax import lax
from jax.experimental import pallas as pl
from jax.experimental.pallas import tpu as pltpu

D_MODEL = 2048
SEQ = 2048
DEPTH = 2
HEAD_DIM = 128
A_HEADS = 4
B_HEADS = 6
C_HEADS = 6
C_KV_HEADS = 2
A_CHUNK = 16
DILATIONS = (1, 4, 16)
BLOCK = 128
ROPE_THETA = 500000.0
ROPE_DIM = 32
D_FF = 5632
IN_WIDTH = 5632
LN_EPS = 1e-5
ALPHA = (2 * DEPTH) ** 0.25
N_DEV = 8
SHARD_COLS = IN_WIDTH // N_DEV

ADAM_LR = 0.001
ADAM_B1 = 0.9
ADAM_B2 = 0.999
ADAM_EPS = 1e-08
ADAM_WD = 0.01
ADAM_STEP = 10

A_COLS = 16
QKV_COLS = 28
QB0, KB0, VB0, QC0, KC0, VC0 = 0, 6, 12, 18, 24, 26

_MXU_DTYPE = jnp.bfloat16
_GRAD_DTYPE = jnp.bfloat16
_NEG = -1e30
_VMEM_LIMIT = 56 * 2 ** 20

_F32 = jnp.float32


def _sigmoid(x):
    return 1.0 / (1.0 + jnp.exp(-x))


def _cparams(**kw):
    return pltpu.CompilerParams(vmem_limit_bytes=_VMEM_LIMIT, **kw)


def _mm(a, b, *, ta=False, tb=False, tm, tn, out_dtype=_F32, add=None, add_scale=1.0, after=(), name):
    K = a.shape[0] if ta else a.shape[1]
    M = a.shape[1] if ta else a.shape[0]
    N = b.shape[0] if tb else b.shape[1]
    assert (b.shape[1] if tb else b.shape[0]) == K and M % tm == 0 and N % tn == 0
    dn = (((0 if ta else 1,), (1 if tb else 0,)), ((), ()))

    def body(*refs):
        a_ref, b_ref = refs[:2]
        o_ref = refs[-1]
        r = lax.dot_general(a_ref[...], b_ref[...], dn, preferred_element_type=_F32)
        if add is not None:
            r = r + add_scale * refs[2][...]
        o_ref[...] = r.astype(o_ref.dtype)

    a_spec = pl.BlockSpec((K, tm), lambda i, j: (0, i)) if ta else pl.BlockSpec((tm, K), lambda i, j: (i, 0))
    b_spec = pl.BlockSpec((tn, K), lambda i, j: (j, 0)) if tb else pl.BlockSpec((K, tn), lambda i, j: (0, j))
    o_spec = pl.BlockSpec((tm, tn), lambda i, j: (i, j))
    in_specs = [a_spec, b_spec] + ([o_spec] if add is not None else []) + [pl.BlockSpec(memory_space=pl.ANY)] * len(after)
    args = (a, b) + ((add,) if add is not None else ()) + tuple(after)
    return pl.pallas_call(
        body, grid=(M // tm, N // tn), in_specs=in_specs, out_specs=o_spec,
        out_shape=jax.ShapeDtypeStruct((M, N), out_dtype), name=name,
        compiler_params=_cparams(dimension_semantics=("parallel", "parallel")),
    )(*args)


_PAIR = 2 * SHARD_COLS


def _mm_tn_slabs(a, b, *, tm, name):
    K, M = a.shape
    assert b.shape == (K, N_DEV * SHARD_COLS) and M % tm == 0

    def body(a_ref, b_ref, o_ref):
        a_blk = a_ref[...]
        for s in range(2):
            o_ref[s] = lax.dot_general(b_ref[:, s * SHARD_COLS:(s + 1) * SHARD_COLS], a_blk, _TN,
                                       preferred_element_type=_F32).astype(o_ref.dtype)

    return pl.pallas_call(
        body, grid=(M // tm, N_DEV // 2),
        in_specs=[pl.BlockSpec((K, tm), lambda i, p: (0, i)), pl.BlockSpec((K, _PAIR), lambda i, p: (0, p))],
        out_specs=pl.BlockSpec((2, SHARD_COLS, tm), lambda i, p: (p, 0, i)),
        out_shape=jax.ShapeDtypeStruct((N_DEV, SHARD_COLS, M), _GRAD_DTYPE), name=name,
        compiler_params=_cparams(dimension_semantics=("parallel", "parallel")),
    )(a, b)


def _mm_w_slabs(a, w, *, tm, after=(), name):
    M, K = a.shape
    assert w.shape == (N_DEV, K, SHARD_COLS) and M % tm == 0

    def body(a_ref, w_ref, *rest):
        o_ref = rest[-1]
        a_blk = a_ref[...]
        for s in range(2):
            o_ref[:, s * SHARD_COLS:(s + 1) * SHARD_COLS] = jnp.dot(a_blk, w_ref[s], preferred_element_type=_F32)

    return pl.pallas_call(
        body, grid=(M // tm, N_DEV // 2),
        in_specs=[pl.BlockSpec((tm, K), lambda i, p: (i, 0)), pl.BlockSpec((2, K, SHARD_COLS), lambda i, p: (p, 0, 0))]
        + [pl.BlockSpec(memory_space=pl.ANY)] * len(after),
        out_specs=pl.BlockSpec((tm, _PAIR), lambda i, p: (i, p)),
        out_shape=jax.ShapeDtypeStruct((M, N_DEV * SHARD_COLS), _F32), name=name,
        compiler_params=_cparams(dimension_semantics=("parallel", "parallel")),
    )(a, w, *after)


def _mm_nt_w_slabs(a, w, *, tm, tn, add=None, add_scale=1.0, after=(), name):
    M = a.shape[0]
    N = w.shape[1]
    assert a.shape[1] == N_DEV * SHARD_COLS and w.shape[0] == N_DEV and M % tm == 0 and N % tn == 0

    def body(a_ref, w_ref, *rest):
        o_ref = rest[-1]
        acc = add_scale * rest[0][...] if add is not None else None
        for j in range(N_DEV):
            t = lax.dot_general(a_ref[:, j * SHARD_COLS:(j + 1) * SHARD_COLS], w_ref[j], _NT, preferred_element_type=_F32)
            acc = t if acc is None else acc + t
        o_ref[...] = acc

    o_spec = pl.BlockSpec((tm, tn), lambda i, j: (i, j))
    return pl.pallas_call(
        body, grid=(M // tm, N // tn),
        in_specs=[pl.BlockSpec((tm, N_DEV * SHARD_COLS), lambda i, j: (i, 0)),
                  pl.BlockSpec((N_DEV, tn, SHARD_COLS), lambda i, j: (0, j, 0))]
        + ([o_spec] if add is not None else []) + [pl.BlockSpec(memory_space=pl.ANY)] * len(after),
        out_specs=o_spec, out_shape=jax.ShapeDtypeStruct((M, N), _F32), name=name,
        compiler_params=_cparams(dimension_semantics=("parallel", "parallel")),
    )(a, w, *((add,) if add is not None else ()), *after)


def _ln_fwd(x, y, g, b, *, name):
    tm = 256

    def body(x_ref, y_ref, g_ref, b_ref, z_ref, o_ref, ob_ref):
        z = ALPHA * x_ref[...] + y_ref[...]
        mu = jnp.mean(z, axis=-1, keepdims=True)
        zc = z - mu
        var = jnp.mean(zc * zc, axis=-1, keepdims=True)
        o = zc * lax.rsqrt(var + LN_EPS) * g_ref[...] + b_ref[...]
        z_ref[...] = z
        o_ref[...] = o
        ob_ref[...] = o.astype(ob_ref.dtype)

    row = pl.BlockSpec((tm, D_MODEL), lambda i: (i, 0))
    vec = pl.BlockSpec((1, D_MODEL), lambda i: (0, 0))
    return pl.pallas_call(
        body, grid=(SEQ // tm,), in_specs=[row, row, vec, vec], out_specs=[row, row, row],
        out_shape=[jax.ShapeDtypeStruct((SEQ, D_MODEL), _F32), jax.ShapeDtypeStruct((SEQ, D_MODEL), _F32),
                   jax.ShapeDtypeStruct((SEQ, D_MODEL), _MXU_DTYPE)],
        name=name, compiler_params=_cparams(dimension_semantics=("parallel",)),
    )(x, y, g.reshape(1, D_MODEL), b.reshape(1, D_MODEL))


def _ln_bwd(z, d_a, d_res, g, *, name):
    tm = 256

    def body(*refs):
        if d_res is None:
            z_ref, da_ref, g_ref, dz_ref, dzb_ref, dg_ref, db_ref = refs
            dout = da_ref[...]
        else:
            z_ref, da_ref, dr_ref, g_ref, dz_ref, dzb_ref, dg_ref, db_ref = refs
            dout = da_ref[...] + ALPHA * dr_ref[...]
        z = z_ref[...]
        mu = jnp.mean(z, axis=-1, keepdims=True)
        zc = z - mu
        var = jnp.mean(zc * zc, axis=-1, keepdims=True)
        rstd = lax.rsqrt(var + LN_EPS)
        xh = zc * rstd
        dxh = dout * g_ref[...]
        m1 = jnp.mean(dxh, axis=-1, keepdims=True)
        m2 = jnp.mean(dxh * xh, axis=-1, keepdims=True)
        dz = rstd * (dxh - m1 - xh * m2)
        dz_ref[...] = dz
        dzb_ref[...] = dz.astype(dzb_ref.dtype)

        @pl.when(pl.program_id(0) == 0)
        def _():
            dg_ref[...] = jnp.zeros_like(dg_ref)
            db_ref[...] = jnp.zeros_like(db_ref)

        dg_ref[0:1, :] += jnp.sum(dout * xh, axis=0, keepdims=True)
        db_ref[0:1, :] += jnp.sum(dout, axis=0, keepdims=True)

    row = pl.BlockSpec((tm, D_MODEL), lambda i: (i, 0))
    vec = pl.BlockSpec((1, D_MODEL), lambda i: (0, 0))
    acc = pl.BlockSpec((8, D_MODEL), lambda i: (0, 0))
    ins = [z, d_a] + ([d_res] if d_res is not None else []) + [g.reshape(1, D_MODEL)]
    in_specs = [row, row] + ([row] if d_res is not None else []) + [vec]
    dz, dzb, dg, db = pl.pallas_call(
        body, grid=(SEQ // tm,), in_specs=in_specs, out_specs=[row, row, acc, acc],
        out_shape=[jax.ShapeDtypeStruct((SEQ, D_MODEL), _F32), jax.ShapeDtypeStruct((SEQ, D_MODEL), _MXU_DTYPE),
                   jax.ShapeDtypeStruct((8, D_MODEL), _F32), jax.ShapeDtypeStruct((8, D_MODEL), _F32)],
        name=name, compiler_params=_cparams(dimension_semantics=("arbitrary",)),
    )(*ins)
    return dz, dzb, dg[0], db[0]


def _loss_head(y, target, *, name):
    tm = 256

    def body(y_ref, t_ref, d_ref, l_ref):
        e = y_ref[...] - t_ref[...]
        d_ref[...] = e * (1.0 / D_MODEL)

        @pl.when(pl.program_id(0) == 0)
        def _():
            l_ref[...] = jnp.zeros_like(l_ref)

        l_ref[...] += (0.5 / D_MODEL) * jnp.sum(e * e)

    row = pl.BlockSpec((tm, D_MODEL), lambda i: (i, 0))
    d, l = pl.pallas_call(
        body, grid=(SEQ // tm,), in_specs=[row, row], out_specs=[row, pl.BlockSpec((8, 128), lambda i: (0, 0))],
        out_shape=[jax.ShapeDtypeStruct((SEQ, D_MODEL), _F32), jax.ShapeDtypeStruct((8, 128), _F32)],
        name=name, compiler_params=_cparams(dimension_semantics=("arbitrary",)),
    )(y, target)
    return l[0, 0], d


_CONV_TN = 256


def _shift_down(v, k, rows):
    return jnp.where(rows >= k, pltpu.roll(v, k, axis=0), 0.0)


def _shift_up(v, k, rows):
    return jnp.where(rows < SEQ - k, pltpu.roll(v, SEQ - k, axis=0), 0.0)


def _conv_gate_fwd(g, u, conv_w, conv_b, *, name):
    def body(g_ref, u_ref, w_ref, b_ref, h_ref):
        gv = g_ref[...]
        rows = lax.broadcasted_iota(jnp.int32, gv.shape, 0)
        w = w_ref[...]
        gc = b_ref[...] + w[2:3, :] * gv + w[1:2, :] * _shift_down(gv, 1, rows) + w[0:1, :] * _shift_down(gv, 2, rows)
        h_ref[...] = (gc * _sigmoid(gc) * u_ref[...]).astype(h_ref.dtype)

    col = pl.BlockSpec((SEQ, _CONV_TN), lambda j: (0, j))
    return pl.pallas_call(
        body, grid=(D_FF // _CONV_TN,),
        in_specs=[col, col, pl.BlockSpec((3, _CONV_TN), lambda j: (0, j)), pl.BlockSpec((1, _CONV_TN), lambda j: (0, j))],
        out_specs=col, out_shape=jax.ShapeDtypeStruct((SEQ, D_FF), _MXU_DTYPE), name=name,
        compiler_params=_cparams(dimension_semantics=("parallel",)),
    )(g, u, conv_w, conv_b.reshape(1, D_FF))


def _conv_gate_bwd(dh, g, u, conv_w, conv_b, *, name):
    def body(dh_ref, g_ref, u_ref, w_ref, b_ref, dg_ref, du_ref, dw_ref, db_ref):
        gv = g_ref[...]
        rows = lax.broadcasted_iota(jnp.int32, gv.shape, 0)
        w = w_ref[...]
        g1 = _shift_down(gv, 1, rows)
        g2 = _shift_down(gv, 2, rows)
        gc = b_ref[...] + w[2:3, :] * gv + w[1:2, :] * g1 + w[0:1, :] * g2
        sg = _sigmoid(gc)
        dh = dh_ref[...]
        du_ref[...] = (dh * (gc * sg)).astype(du_ref.dtype)
        dgc = dh * u_ref[...] * (sg * (1.0 + gc * (1.0 - sg)))
        dg = w[2:3, :] * dgc + w[1:2, :] * _shift_up(dgc, 1, rows) + w[0:1, :] * _shift_up(dgc, 2, rows)
        dg_ref[...] = dg.astype(dg_ref.dtype)
        dw_ref[0:1, :] = jnp.sum(dgc * g2, axis=0, keepdims=True)
        dw_ref[1:2, :] = jnp.sum(dgc * g1, axis=0, keepdims=True)
        dw_ref[2:3, :] = jnp.sum(dgc * gv, axis=0, keepdims=True)
        db_ref[...] = jnp.sum(dgc, axis=0, keepdims=True)

    col = pl.BlockSpec((SEQ, _CONV_TN), lambda j: (0, j))
    w3 = pl.BlockSpec((3, _CONV_TN), lambda j: (0, j))
    w1 = pl.BlockSpec((1, _CONV_TN), lambda j: (0, j))
    dg, du, dw, db = pl.pallas_call(
        body, grid=(D_FF // _CONV_TN,), in_specs=[col, col, col, w3, w1], out_specs=[col, col, w3, w1],
        out_shape=[jax.ShapeDtypeStruct((SEQ, D_FF), _MXU_DTYPE), jax.ShapeDtypeStruct((SEQ, D_FF), _MXU_DTYPE),
                   jax.ShapeDtypeStruct((3, D_FF), _F32), jax.ShapeDtypeStruct((1, D_FF), _F32)],
        name=name, compiler_params=_cparams(dimension_semantics=("parallel",)),
    )(dh, g, u, conv_w, conv_b.reshape(1, D_FF))
    return dg, du, dw, db[0]


def _rope_tables():
    half = ROPE_DIM // 2
    inv = ROPE_THETA ** (-jnp.arange(0, ROPE_DIM, 2, dtype=_F32) / ROPE_DIM)
    ang = jnp.arange(SEQ, dtype=_F32)[:, None] * inv[None, :]
    cos, sin = jnp.cos(ang), jnp.sin(ang)
    rest = HEAD_DIM - ROPE_DIM
    c = jnp.concatenate([cos, cos, jnp.ones((SEQ, rest), _F32)], axis=1)
    s1 = jnp.concatenate([-sin, jnp.zeros((SEQ, HEAD_DIM - half), _F32)], axis=1)
    s2 = jnp.concatenate([jnp.zeros((SEQ, half), _F32), sin, jnp.zeros((SEQ, rest), _F32)], axis=1)
    return c, s1, s2


def _rope_apply(x, c, s1, s2):
    return x * c + pltpu.roll(x, HEAD_DIM - ROPE_DIM // 2, axis=1) * s1 + pltpu.roll(x, ROPE_DIM // 2, axis=1) * s2


def _rope_transpose(d, c, s1, s2):
    half = ROPE_DIM // 2
    return d * c + pltpu.roll(d * s1, half, axis=1) + pltpu.roll(d * s2, HEAD_DIM - half, axis=1)


_NT = (((1,), (1,)), ((), ()))
_TN = (((0,), (0,)), ((), ()))
_SCALE = HEAD_DIM ** -0.5


def _band_scores(q, k2, n, lag_off):
    s = lax.dot_general(q, k2, _NT, preferred_element_type=_F32) * _SCALE
    row = lax.broadcasted_iota(jnp.int32, (BLOCK, 2 * BLOCK), 0)
    col = lax.broadcasted_iota(jnp.int32, (BLOCK, 2 * BLOCK), 1)
    front = (col >= row + lag_off) & (col < BLOCK) & (n > 0)
    own = (col >= BLOCK) & (col <= row + BLOCK)
    return jnp.where(front | own, s, _NEG)


_BAND_STEPS = SEQ // BLOCK


def _rows(start, d):
    if d == 1:
        return pl.ds(pl.multiple_of(start, BLOCK), BLOCK)
    return pl.ds(start, BLOCK, stride=d)


def _band_block(it, d):
    r, n = it % d, it // d
    span = BLOCK * d
    return n, _rows(r + n * span, d), _rows(r + jnp.maximum(n - 1, 0) * span, d)


def _band_fwd(proj, tabs, *, kv_heads, q_per_kv, q0, k0, v0, dilations, lag_off, sink, name):
    heads = kv_heads * q_per_kv

    def body(*refs):
        q_refs = refs[:q_per_kv]
        k_ref, v_ref, c_ref, s1_ref, s2_ref = refs[q_per_kv:q_per_kv + 5]
        rest = refs[q_per_kv + 5:]
        if sink is not None:
            sk_ref, rest = rest[0], rest[1:]
        o_ref, lse_ref, qs, ks, m_s, l_s, acc_s = rest
        c, s1, s2 = c_ref[...], s1_ref[...], s2_ref[...]
        ks[...] = _rope_apply(k_ref[...], c, s1, s2)
        for i in range(q_per_kv):
            qs[...] = _rope_apply(q_refs[i][...], c, s1, s2)
            for pi, d in enumerate(dilations):
                def step(it, carry, d=d, first=(pi == 0)):
                    n, cur, prev = _band_block(it, d)
                    q = qs[cur, :].astype(_MXU_DTYPE)
                    k2 = jnp.concatenate([ks[prev, :], ks[cur, :]], axis=0).astype(_MXU_DTYPE)
                    v2 = jnp.concatenate([v_ref[prev, :], v_ref[cur, :]], axis=0).astype(_MXU_DTYPE)
                    s = _band_scores(q, k2, n, lag_off)
                    m_b = jnp.max(s, axis=1, keepdims=True)
                    m_new = m_b if first else jnp.maximum(m_b, m_s[cur, :][:, 0:1])
                    p = jnp.exp(s - m_new)
                    l_new = jnp.sum(p, axis=1, keepdims=True)
                    acc = jnp.dot(p.astype(_MXU_DTYPE), v2, preferred_element_type=_F32)
                    if not first:
                        a = jnp.exp(m_s[cur, :][:, 0:1] - m_new)
                        l_new = l_new + a * l_s[cur, :][:, 0:1]
                        acc = acc + a * acc_s[cur, :]
                    m_s[cur, :] = jnp.broadcast_to(m_new, (BLOCK, HEAD_DIM))
                    l_s[cur, :] = jnp.broadcast_to(l_new, (BLOCK, HEAD_DIM))
                    acc_s[cur, :] = acc
                    return carry

                lax.fori_loop(0, _BAND_STEPS, step, 0, unroll=4)
            m, den = m_s[...], l_s[...]
            if sink is not None:
                sk = sk_ref[i]
                m_f = jnp.maximum(m, sk)
                a = jnp.exp(m - m_f)
                den = den * a + jnp.exp(sk - m_f)
                o = acc_s[...] * a / den
                m = m_f
            else:
                o = acc_s[...] / den
            o_ref[:, i * HEAD_DIM:(i + 1) * HEAD_DIM] = o
            lse_ref[:, i * HEAD_DIM:(i + 1) * HEAD_DIM] = m + jnp.log(den)

    col = (SEQ, HEAD_DIM)
    in_specs = [pl.BlockSpec(col, functools.partial(lambda g, i: (0, A_COLS + q0 + g * q_per_kv + i), i=i)) for i in range(q_per_kv)]
    in_specs += [pl.BlockSpec(col, lambda g: (0, A_COLS + k0 + g)), pl.BlockSpec(col, lambda g: (0, A_COLS + v0 + g))]
    in_specs += [pl.BlockSpec(col, lambda g: (0, 0))] * 3
    args = [proj] * (q_per_kv + 2) + list(tabs)
    if sink is not None:
        in_specs.append(pl.BlockSpec((q_per_kv, 1, HEAD_DIM), lambda g: (g, 0, 0)))
        args.append(jnp.broadcast_to(sink.reshape(heads, 1, 1), (heads, 1, HEAD_DIM)))
    o_spec = pl.BlockSpec((SEQ, q_per_kv * HEAD_DIM), lambda g: (0, g))
    shape = jax.ShapeDtypeStruct((SEQ, heads * HEAD_DIM), _F32)
    return pl.pallas_call(
        body, grid=(kv_heads,), in_specs=in_specs, out_specs=[o_spec, o_spec], out_shape=[shape, shape],
        scratch_shapes=[pltpu.VMEM(col, _F32)] * 5, name=name,
        compiler_params=_cparams(dimension_semantics=("parallel",)),
    )(*args)


def _band_bwd(proj, tabs, dmixed, o, lse, *, kv_heads, q_per_kv, q0, k0, v0, do0, dilations, lag_off, sink, name):
    heads = kv_heads * q_per_kv

    def body(*refs):
        q_refs = refs[:q_per_kv]
        k_ref, v_ref, c_ref, s1_ref, s2_ref = refs[q_per_kv:q_per_kv + 5]
        do_refs = refs[q_per_kv + 5:2 * q_per_kv + 5]
        o_ref, lse_ref = refs[2 * q_per_kv + 5:2 * q_per_kv + 7]
        rest = refs[2 * q_per_kv + 7:]
        if sink is not None:
            sk_ref, rest = rest[0], rest[1:]
            dq_ref, dk_ref, dv_ref, dsk_ref, qs, ks, dq_s, dk_s, dv_s = rest
        else:
            dq_ref, dk_ref, dv_ref, qs, ks, dq_s, dk_s, dv_s = rest
        c, s1, s2 = c_ref[...], s1_ref[...], s2_ref[...]
        ks[...] = _rope_apply(k_ref[...], c, s1, s2)
        dk_s[...] = jnp.zeros_like(dk_s)
        dv_s[...] = jnp.zeros_like(dv_s)
        for i in range(q_per_kv):
            hs = slice(i * HEAD_DIM, (i + 1) * HEAD_DIM)
            qs[...] = _rope_apply(q_refs[i][...], c, s1, s2)
            dq_s[...] = jnp.zeros_like(dq_s)
            do_ref = do_refs[i]
            for d in dilations:
                def step(it, carry, d=d, do_ref=do_ref, hs=hs):
                    n, cur, prev = _band_block(it, d)
                    q = qs[cur, :].astype(_MXU_DTYPE)
                    k2 = jnp.concatenate([ks[prev, :], ks[cur, :]], axis=0).astype(_MXU_DTYPE)
                    v2 = jnp.concatenate([v_ref[prev, :], v_ref[cur, :]], axis=0).astype(_MXU_DTYPE)
                    do = do_ref[cur, :]
                    delta = jnp.sum(do * o_ref[cur, hs], axis=1, keepdims=True)
                    lse_c = lse_ref[cur, hs][:, 0:1]
                    p = jnp.exp(_band_scores(q, k2, n, lag_off) - lse_c)
                    dob = do.astype(_MXU_DTYPE)
                    ds = (p * (lax.dot_general(dob, v2, _NT, preferred_element_type=_F32) - delta) * _SCALE).astype(_MXU_DTYPE)
                    dq_s[cur, :] += jnp.dot(ds, k2, preferred_element_type=_F32)
                    dk2 = lax.dot_general(ds, q, _TN, preferred_element_type=_F32)
                    dv2 = lax.dot_general(p.astype(_MXU_DTYPE), dob, _TN, preferred_element_type=_F32)
                    dk_s[prev, :] += dk2[:BLOCK]
                    dv_s[prev, :] += dv2[:BLOCK]
                    dk_s[cur, :] += dk2[BLOCK:]
                    dv_s[cur, :] += dv2[BLOCK:]
                    return carry

                lax.fori_loop(0, _BAND_STEPS, step, 0, unroll=4)
            dq_ref[:, hs] = _rope_transpose(dq_s[...], c, s1, s2).astype(dq_ref.dtype)
            if sink is not None:
                delta = jnp.sum(do_ref[...] * o_ref[:, hs], axis=1, keepdims=True)
                w_sink = jnp.exp(sk_ref[i] - lse_ref[:, hs])
                dsk_ref[i] = jnp.broadcast_to(jnp.sum(-delta * w_sink[:, 0:1]), (8, HEAD_DIM))
        dk_ref[...] = _rope_transpose(dk_s[...], c, s1, s2).astype(dk_ref.dtype)
        dv_ref[...] = dv_s[...].astype(dv_ref.dtype)

    col = (SEQ, HEAD_DIM)
    in_specs = [pl.BlockSpec(col, functools.partial(lambda g, i: (0, A_COLS + q0 + g * q_per_kv + i), i=i)) for i in range(q_per_kv)]
    in_specs += [pl.BlockSpec(col, lambda g: (0, A_COLS + k0 + g)), pl.BlockSpec(col, lambda g: (0, A_COLS + v0 + g))]
    in_specs += [pl.BlockSpec(col, lambda g: (0, 0))] * 3
    in_specs += [pl.BlockSpec(col, functools.partial(lambda g, i: (0, do0 + g * q_per_kv + i), i=i)) for i in range(q_per_kv)]
    wide = pl.BlockSpec((SEQ, q_per_kv * HEAD_DIM), lambda g: (0, g))
    in_specs += [wide, wide]
    args = [proj] * (q_per_kv + 2) + list(tabs) + [dmixed] * q_per_kv + [o, lse]
    out_specs = [wide, pl.BlockSpec(col, lambda g: (0, g)), pl.BlockSpec(col, lambda g: (0, g))]
    out_shape = [jax.ShapeDtypeStruct((SEQ, heads * HEAD_DIM), _MXU_DTYPE), jax.ShapeDtypeStruct((SEQ, kv_heads * HEAD_DIM), _MXU_DTYPE),
                 jax.ShapeDtypeStruct((SEQ, kv_heads * HEAD_DIM), _MXU_DTYPE)]
    if sink is not None:
        in_specs.append(pl.BlockSpec((q_per_kv, 1, HEAD_DIM), lambda g: (g, 0, 0)))
        args.append(jnp.broadcast_to(sink.reshape(heads, 1, 1), (heads, 1, HEAD_DIM)))
        out_specs.append(pl.BlockSpec((q_per_kv, 8, HEAD_DIM), lambda g: (g, 0, 0)))
        out_shape.append(jax.ShapeDtypeStruct((heads, 8, HEAD_DIM), _F32))
    res = pl.pallas_call(
        body, grid=(kv_heads,), in_specs=in_specs, out_specs=out_specs, out_shape=out_shape,
        scratch_shapes=[pltpu.VMEM(col, _F32)] * 5, name=name,
        compiler_params=_cparams(dimension_semantics=("parallel",)),
    )(*args)
    if sink is not None:
        return res[0], res[1], res[2], res[3][:, 0, 0]
    return res


_DILATED = dict(kv_heads=B_HEADS, q_per_kv=1, q0=QB0, k0=KB0, v0=VB0, dilations=DILATIONS, lag_off=0, sink=None)
_SWA = dict(kv_heads=C_KV_HEADS, q_per_kv=C_HEADS // C_KV_HEADS, q0=QC0, k0=KC0, v0=VC0, dilations=(1,), lag_off=1)


_HG_TILE = 128
_HG_CHUNKS = _HG_TILE // A_CHUNK
_HG_TILES = SEQ // _HG_TILE
_HI = lax.Precision.HIGHEST


def _chunk_tri():
    i = np.arange(_HG_TILE)
    return jnp.asarray(((i[:, None] // A_CHUNK == i[None, :] // A_CHUNK) & (i[None, :] <= i[:, None])).astype(np.float32))


def _layer_lb(lb_ref, layer):
    if layer == 0:
        return jnp.zeros((1, HEAD_DIM), _F32)
    lg = lb_ref[...]
    m = jnp.max(lg, axis=0, keepdims=True)
    e = jnp.exp(lg - m)
    return e[1:2, :] / jnp.sum(e, axis=0, keepdims=True)


def _hgrn_gates(q, fr, lb):
    sgq = _sigmoid(q)
    sg = _sigmoid(fr)
    f = lb + (1.0 - lb) * sg
    return sgq, q * sgq, sg, f, 1.0 - f


def _hgrn_fwd(proj, lb_logits, norm_w, layer, *, name):
    tri = _chunk_tri()

    def body(q_ref, f_ref, i_ref, g_ref, lb_ref, nw_ref, tri_ref, o_ref, raw_ref, st_ref, state):
        @pl.when(pl.program_id(1) == 0)
        def _():
            state[...] = jnp.zeros_like(state)

        lb = _layer_lb(lb_ref, layer)
        _, qs, _, f, k = _hgrn_gates(q_ref[...], f_ref[...], lb)
        v = i_ref[...]
        b = jnp.dot(tri_ref[...], jnp.log(f), precision=_HI, preferred_element_type=_F32)
        eb = jnp.exp(b)
        ridx = lax.broadcasted_iota(jnp.int32, (A_CHUNK, HEAD_DIM), 0)
        outs = []
        for c in range(_HG_CHUNKS):
            sl = slice(c * A_CHUNK, (c + 1) * A_CHUNK)
            bc, qc, kc, vc = b[sl], qs[sl], k[sl], v[sl]
            bl = bc[A_CHUNK - 1:A_CHUNK]
            st = state[...]
            st_ref[0, c] = st
            o_c = lax.dot_general((qc * eb[sl]).astype(_MXU_DTYPE), st.astype(_MXU_DTYPE), _NT, preferred_element_type=_F32)
            rows = []
            for i in range(A_CHUNK):
                di = jnp.exp(jnp.where(ridx <= i, bc[i:i + 1] - bc, _NEG))
                a = jnp.sum(qc[i:i + 1] * kc * di, axis=1, keepdims=True)
                rows.append(jnp.sum(a * vc, axis=0, keepdims=True))
            outs.append(o_c + jnp.concatenate(rows, axis=0))
            kt = (kc * jnp.exp(bl - bc)).astype(_MXU_DTYPE)
            state[...] = st * jnp.exp(bl) + lax.dot_general(vc.astype(_MXU_DTYPE), kt, _TN, preferred_element_type=_F32)
        o = jnp.concatenate(outs, axis=0)
        raw_ref[...] = o
        r = lax.rsqrt(jnp.mean(o * o, axis=-1, keepdims=True) + LN_EPS)
        g = g_ref[...]
        o_ref[...] = o * r * nw_ref[...] * (g * _sigmoid(g))

    blk = (_HG_TILE, HEAD_DIM)

    def col(base):
        return pl.BlockSpec(blk, lambda h, t: (t, base + h))

    o_spec = pl.BlockSpec(blk, lambda h, t: (t, h))
    o_shape = jax.ShapeDtypeStruct((SEQ, A_HEADS * HEAD_DIM), _F32)
    return pl.pallas_call(
        body, grid=(A_HEADS, _HG_TILES),
        in_specs=[col(0), col(4), col(8), col(12), pl.BlockSpec((DEPTH, HEAD_DIM), lambda h, t: (0, h)),
                  pl.BlockSpec((1, HEAD_DIM), lambda h, t: (0, 0)), pl.BlockSpec(blk, lambda h, t: (0, 0))],
        out_specs=[o_spec, o_spec, pl.BlockSpec((1, _HG_CHUNKS, HEAD_DIM, HEAD_DIM), lambda h, t: (h, t, 0, 0))],
        out_shape=[o_shape, o_shape, jax.ShapeDtypeStruct((A_HEADS, SEQ // A_CHUNK, HEAD_DIM, HEAD_DIM), _F32)],
        scratch_shapes=[pltpu.VMEM((HEAD_DIM, HEAD_DIM), _F32)], name=name,
        compiler_params=_cparams(dimension_semantics=("parallel", "arbitrary")),
    )(proj, proj, proj, proj, lb_logits, norm_w.reshape(1, HEAD_DIM), tri)


def _hgrn_bwd(proj, lb_logits, norm_w, raw, states, dmixed, layer, *, name):
    tri = _chunk_tri()
    triu = tri.T

    def body(q_ref, f_ref, i_ref, g_ref, lb_ref, nw_ref, tri_ref, triu_ref, raw_ref, do_ref, st_ref,
             dq_ref, df_ref, di_ref, dg_ref, dnw_ref, dlb_ref, dstate):
        @pl.when(pl.program_id(1) == 0)
        def _():
            dstate[...] = jnp.zeros_like(dstate)
            dlb_ref[...] = jnp.zeros_like(dlb_ref)

        @pl.when((pl.program_id(0) == 0) & (pl.program_id(1) == 0))
        def _():
            dnw_ref[...] = jnp.zeros_like(dnw_ref)

        lb = _layer_lb(lb_ref, layer)
        q = q_ref[...]
        sgq, qs, sg, f, k = _hgrn_gates(q, f_ref[...], lb)
        v = i_ref[...]
        b = jnp.dot(tri_ref[...], jnp.log(f), precision=_HI, preferred_element_type=_F32)
        eb = jnp.exp(b)
        g = g_ref[...]
        nw = nw_ref[...]
        o = raw_ref[...]
        dout = do_ref[...]
        sgg = _sigmoid(g)
        r = lax.rsqrt(jnp.mean(o * o, axis=-1, keepdims=True) + LN_EPS)
        dg_ref[...] = (dout * (o * r * nw) * (sgg * (1.0 + g * (1.0 - sgg)))).astype(dg_ref.dtype)
        don = dout * (g * sgg)
        dnw_ref[0:1, :] += jnp.sum(don * o * r, axis=0, keepdims=True)
        dy = don * nw
        do_raw = r * dy - o * (r * r * r) * jnp.mean(o * dy, axis=-1, keepdims=True)

        ridx = lax.broadcasted_iota(jnp.int32, (A_CHUNK, HEAD_DIM), 0)
        dqs_t, dk_t, db_t, dv_t = [None] * _HG_CHUNKS, [None] * _HG_CHUNKS, [None] * _HG_CHUNKS, [None] * _HG_CHUNKS
        for c in reversed(range(_HG_CHUNKS)):
            sl = slice(c * A_CHUNK, (c + 1) * A_CHUNK)
            bc, qc, kc, vc, doc = b[sl], qs[sl], k[sl], v[sl], do_raw[sl]
            bl = bc[A_CHUNK - 1:A_CHUNK]
            ebc = eb[sl]
            ebl = jnp.exp(bl - bc)
            lam = jnp.exp(bl)
            qt = qc * ebc
            kt = kc * ebl
            dst = dstate[...]
            stp = st_ref[0, c]
            dob = doc.astype(_MXU_DTYPE)
            dstb = dst.astype(_MXU_DTYPE)
            dqt = jnp.dot(dob, stp.astype(_MXU_DTYPE), preferred_element_type=_F32)
            dkt = jnp.dot(vc.astype(_MXU_DTYPE), dstb, preferred_element_type=_F32)
            dv = lax.dot_general(kt.astype(_MXU_DTYPE), dstb, _NT, preferred_element_type=_F32)
            dlam = jnp.sum(stp * dst, axis=0, keepdims=True)
            dstate[...] = dst * lam + lax.dot_general(dob, qt.astype(_MXU_DTYPE), _TN, preferred_element_type=_F32)
            dqs_rows = []
            dk_in = jnp.zeros((A_CHUNK, HEAD_DIM), _F32)
            for i in range(A_CHUNK):
                di = jnp.exp(jnp.where(ridx <= i, bc[i:i + 1] - bc, _NEG))
                qi = qc[i:i + 1]
                doi = doc[i:i + 1]
                w = kc * di
                a = jnp.sum(qi * w, axis=1, keepdims=True)
                dv = dv + a * doi
                da = jnp.sum(doi * vc, axis=1, keepdims=True)
                dqs_rows.append(jnp.sum(da * w, axis=0, keepdims=True))
                dk_in = dk_in + da * (qi * di)
            dqs_in = jnp.concatenate(dqs_rows, axis=0)
            dbl = jnp.sum(dkt * kt, axis=0, keepdims=True) + dlam * lam
            db = qc * dqs_in - kc * dk_in + dqt * qt - dkt * kt
            db_t[c] = db + jnp.where(ridx == A_CHUNK - 1, dbl, 0.0)
            dqs_t[c] = dqs_in + dqt * ebc
            dk_t[c] = dk_in + dkt * ebl
            dv_t[c] = dv
        dqs = jnp.concatenate(dqs_t, axis=0)
        dk = jnp.concatenate(dk_t, axis=0)
        db = jnp.concatenate(db_t, axis=0)
        di_ref[...] = jnp.concatenate(dv_t, axis=0).astype(di_ref.dtype)
        dlogf = jnp.dot(triu_ref[...], db, precision=_HI, preferred_element_type=_F32)
        df = dlogf / f - dk
        df_ref[...] = (df * (1.0 - lb) * sg * (1.0 - sg)).astype(df_ref.dtype)
        dlb_ref[0, 0:1, :] += jnp.sum(df * (1.0 - sg), axis=0, keepdims=True)
        dq_ref[...] = (dqs * (sgq * (1.0 + q * (1.0 - sgq)))).astype(dq_ref.dtype)

    blk = (_HG_TILE, HEAD_DIM)
    last = _HG_TILES - 1

    def col(base):
        return pl.BlockSpec(blk, lambda h, t: (last - t, base + h))

    tri_spec = pl.BlockSpec(blk, lambda h, t: (0, 0))
    acc_spec = pl.BlockSpec((1, 8, HEAD_DIM), lambda h, t: (h, 0, 0))
    acc_shape = jax.ShapeDtypeStruct((A_HEADS, 8, HEAD_DIM), _F32)
    dq, df, di, dg, dnw, dlb = pl.pallas_call(
        body, grid=(A_HEADS, _HG_TILES),
        in_specs=[col(0), col(4), col(8), col(12), pl.BlockSpec((DEPTH, HEAD_DIM), lambda h, t: (0, h)),
                  pl.BlockSpec((1, HEAD_DIM), lambda h, t: (0, 0)), tri_spec, tri_spec, col(0), col(0),
                  pl.BlockSpec((1, _HG_CHUNKS, HEAD_DIM, HEAD_DIM), lambda h, t: (h, last - t, 0, 0))],
        out_specs=[col(0), col(0), col(0), col(0), pl.BlockSpec((8, HEAD_DIM), lambda h, t: (0, 0)), acc_spec],
        out_shape=[jax.ShapeDtypeStruct((SEQ, A_HEADS * HEAD_DIM), _MXU_DTYPE)] * 4
        + [jax.ShapeDtypeStruct((8, HEAD_DIM), _F32), acc_shape],
        scratch_shapes=[pltpu.VMEM((HEAD_DIM, HEAD_DIM), _F32)], name=name,
        compiler_params=_cparams(dimension_semantics=("arbitrary", "arbitrary")),
    )(proj, proj, proj, proj, lb_logits, norm_w.reshape(1, HEAD_DIM), tri, triu, raw, dmixed, states)
    return dq, df, di, dg, dnw[0], dlb[:, 0, :].reshape(A_HEADS * HEAD_DIM)


def _exchange(arrays, scatter, *, name):
    n = len(arrays)
    n_peer = N_DEV - 1

    def body(*refs):
        ins, outs = refs[:n], refs[n:2 * n]
        send_sems, recv_sems, loc_sems = refs[2 * n:]
        x, y, c = lax.axis_index("x"), lax.axis_index("y"), lax.axis_index("c")
        me = 4 * x + 2 * y + c
        local = []
        for a in range(n):
            cp = pltpu.make_async_copy(ins[a].at[me] if scatter else ins[a], outs[a].at[me], loc_sems.at[a])
            cp.start()
            local.append(cp)

        def peer(k):
            px = jnp.bitwise_xor(x, (k >> 2) & 1)
            py = jnp.bitwise_xor(y, (k >> 1) & 1)
            pc = jnp.bitwise_xor(c, k & 1)
            return (px, py, pc), 4 * px + 2 * py + pc

        def copy(a, k):
            dev, pid = peer(k)
            return pltpu.make_async_remote_copy(
                src_ref=ins[a].at[pid] if scatter else ins[a], dst_ref=outs[a].at[me],
                send_sem=send_sems.at[a * n_peer + k - 1], recv_sem=recv_sems.at[a * n_peer + k - 1],
                device_id=dev, device_id_type=pl.DeviceIdType.MESH)

        def arrival(a, k):
            dev, pid = peer(k)
            return pltpu.make_async_remote_copy(
                src_ref=ins[a].at[pid] if scatter else ins[a], dst_ref=outs[a].at[pid],
                send_sem=send_sems.at[a * n_peer + k - 1], recv_sem=recv_sems.at[a * n_peer + k - 1],
                device_id=dev, device_id_type=pl.DeviceIdType.MESH)

        sends = [copy(a, k) for k in range(1, N_DEV) for a in range(n)]
        for cp in sends:
            cp.start()
        for k in range(1, N_DEV):
            for a in range(n):
                arrival(a, k).wait_recv()
        for cp in sends:
            cp.wait_send()
        for cp in local:
            cp.wait()

    def out_shape(a):
        blk = a.shape[1:] if scatter else a.shape
        return jax.ShapeDtypeStruct((N_DEV,) + tuple(blk), a.dtype)

    any_spec = pl.BlockSpec(memory_space=pl.ANY)
    return pl.pallas_call(
        body, in_specs=[any_spec] * n, out_specs=[any_spec] * n, out_shape=[out_shape(a) for a in arrays],
        scratch_shapes=[pltpu.SemaphoreType.DMA((n * n_peer,)), pltpu.SemaphoreType.DMA((n * n_peer,)),
                        pltpu.SemaphoreType.DMA((n,))],
        name=name, compiler_params=pltpu.CompilerParams(has_side_effects=True),
    )(*arrays)


N_CHIP = N_DEV // 2
_MESH_ID = pl.DeviceIdType.MESH


def _place():
    x, y, c = lax.axis_index("x"), lax.axis_index("y"), lax.axis_index("c")
    chips = [(1 - x, y), (x, 1 - y), (1 - x, 1 - y)]
    return x, y, c, 2 * x + y, chips


def _gather_blocks(arrays, *, name):
    n = len(arrays)

    def body(*refs):
        ins, outs = refs[:n], refs[n:2 * n]
        send_sems, recv_sems, loc_sems = refs[2 * n:]
        x, y, c, _, chips = _place()
        me = 4 * x + 2 * y + c
        sibling = (x, y, 1 - c)

        def slot(px, py, pc):
            return 4 * px + 2 * py + pc

        def copy(a, k, block, to, src=None):
            dst = outs[a].at[slot(*block)]
            return pltpu.make_async_remote_copy(
                src_ref=dst if src is None else src, dst_ref=dst, send_sem=send_sems.at[7 * a + k],
                recv_sem=recv_sems.at[7 * a + k], device_id=to, device_id_type=_MESH_ID)

        local = [pltpu.make_async_copy(ins[a], outs[a].at[me], loc_sems.at[a]) for a in range(n)]
        for cp in local:
            cp.start()
        first = []
        for a in range(n):
            first.append(copy(a, 0, (x, y, c), sibling, src=ins[a]))
            first += [copy(a, 1 + j, (x, y, c), (*chip, c), src=ins[a]) for j, chip in enumerate(chips)]
        for cp in first:
            cp.start()
        passed = []
        for a in range(n):
            for j, chip in enumerate(chips):
                copy(a, 1 + j, (*chip, c), (x, y, c)).wait_recv()
                fwd = copy(a, 4 + j, (*chip, c), sibling)
                fwd.start()
                passed.append(fwd)
        for a in range(n):
            copy(a, 0, sibling, (x, y, c)).wait_recv()
            for j, chip in enumerate(chips):
                copy(a, 4 + j, (*chip, 1 - c), (x, y, c)).wait_recv()
        for cp in first + passed:
            cp.wait_send()
        for cp in local:
            cp.wait()

    any_spec = pl.BlockSpec(memory_space=pl.ANY)
    return pl.pallas_call(
        body, in_specs=[any_spec] * n, out_specs=[any_spec] * n,
        out_shape=[jax.ShapeDtypeStruct((N_DEV,) + a.shape, a.dtype) for a in arrays],
        scratch_shapes=[pltpu.SemaphoreType.DMA((7 * n,)), pltpu.SemaphoreType.DMA((7 * n,)), pltpu.SemaphoreType.DMA((n,))],
        name=name, compiler_params=pltpu.CompilerParams(has_side_effects=True),
    )(*arrays)


def _sibling_swap(arrays, *, name):
    n = len(arrays)

    def body(*refs):
        ins, outs = refs[:n], refs[n:2 * n]
        send_sems, recv_sems = refs[2 * n:]
        x, y, c, _, _ = _place()
        copies = [pltpu.make_async_remote_copy(
            src_ref=ins[a].at[:, 1 - c], dst_ref=outs[a], send_sem=send_sems.at[a], recv_sem=recv_sems.at[a],
            device_id=(x, y, 1 - c), device_id_type=_MESH_ID) for a in range(n)]
        for cp in copies:
            cp.start()
        for cp in copies:
            cp.wait()

    any_spec = pl.BlockSpec(memory_space=pl.ANY)
    return pl.pallas_call(
        body, in_specs=[any_spec] * n, out_specs=[any_spec] * n,
        out_shape=[jax.ShapeDtypeStruct((N_CHIP,) + a.shape[2:], a.dtype) for a in arrays],
        scratch_shapes=[pltpu.SemaphoreType.DMA((n,)), pltpu.SemaphoreType.DMA((n,))],
        name=name, compiler_params=pltpu.CompilerParams(has_side_effects=True),
    )(*arrays)


def _pair_add(mine, theirs, core, *, name):
    _, _, R, C = mine.shape
    tr = max(t for t in range(16, R + 1, 16) if R % t == 0 and t * C <= 512 * 1024)

    def body(core_ref, m_ref, t_ref, o_ref):
        del core_ref
        o_ref[...] = (m_ref[...].astype(_F32) + t_ref[...].astype(_F32)).astype(o_ref.dtype)

    grid_spec = pltpu.PrefetchScalarGridSpec(
        num_scalar_prefetch=1, grid=(N_CHIP, R // tr),
        in_specs=[pl.BlockSpec((None, None, tr, C), lambda q, i, core: (q, core[0], i, 0)),
                  pl.BlockSpec((None, tr, C), lambda q, i, core: (q, i, 0))],
        out_specs=pl.BlockSpec((None, tr, C), lambda q, i, core: (q, i, 0)))
    return pl.pallas_call(
        body, grid_spec=grid_spec, out_shape=jax.ShapeDtypeStruct((N_CHIP, R, C), mine.dtype), name=name,
        compiler_params=_cparams(dimension_semantics=("parallel", "parallel")),
    )(core.reshape(1), mine, theirs)


_HBM = pl.BlockSpec(memory_space=pltpu.HBM)
_SEM = pl.BlockSpec(memory_space=pltpu.SEMAPHORE)
_TOKEN = pl.BlockSpec(memory_space=pltpu.VMEM)
_DATAFLOW = pltpu.SideEffectType.DATAFLOW_SIDE_EFFECTING


def _hbm(a):
    return pltpu.HBM(a.shape, a.dtype)


def _token_shape():
    return jax.ShapeDtypeStruct((8, 128), _F32)


def _dev_slot(px, py, pc):
    return 4 * px + 2 * py + pc


def _gather_start(blocks, landings, *, name):
    n = len(blocks)

    def body(*refs):
        ins, lands = refs[:n], refs[n:2 * n]
        send_sems, d2d_sems, ici_sems = refs[2 * n:2 * n + 3]
        token = refs[-1]
        x, y, c, _, chips = _place()
        for a in range(n):
            dst = lands[a].at[_dev_slot(x, y, c)]
            pltpu.make_async_remote_copy(src_ref=ins[a], dst_ref=dst, send_sem=send_sems.at[4 * a], recv_sem=d2d_sems.at[a],
                                         device_id=(x, y, 1 - c), device_id_type=_MESH_ID).start()
            for j, chip in enumerate(chips):
                pltpu.make_async_remote_copy(src_ref=ins[a], dst_ref=dst, send_sem=send_sems.at[4 * a + 1 + j],
                                             recv_sem=ici_sems.at[3 * a + j], device_id=(*chip, c),
                                             device_id_type=_MESH_ID).start()
        token[...] = jnp.zeros_like(token)

    res = pl.pallas_call(
        body, name=name, in_specs=[_HBM] * (2 * n),
        out_shape=(pltpu.SemaphoreType.DMA((4 * n,)), pltpu.SemaphoreType.DMA((n,)), pltpu.SemaphoreType.DMA((3 * n,)),
                   *[_hbm(b) for b in blocks], *[_hbm(b) for b in landings], _token_shape()),
        out_specs=(_SEM, _SEM, _SEM, *[_HBM] * (2 * n), _TOKEN),
        input_output_aliases={i: 3 + i for i in range(2 * n)},
        compiler_params=pltpu.CompilerParams(has_side_effects=_DATAFLOW),
    )(*[pltpu.with_memory_space_constraint(b, pltpu.HBM) for b in blocks],
      *[pltpu.with_memory_space_constraint(b, pltpu.HBM) for b in landings])
    return res[0], res[1], res[2], list(res[3:3 + n]), list(res[3 + n:3 + 2 * n]), res[-1]


def _gather_forward(landings, ici_sems, after, *, name):
    n = len(landings)

    def body(*refs):
        lands = refs[:n]
        ici = refs[n]
        f_send, f_recv = refs[n + 2], refs[n + 3]
        token = refs[-1]
        x, y, c, _, chips = _place()
        for a in range(n):
            for j, chip in enumerate(chips):
                blk = lands[a].at[_dev_slot(*chip, c)]
                pltpu.make_async_remote_copy(src_ref=blk, dst_ref=blk, send_sem=f_send.at[3 * a + j], recv_sem=ici.at[3 * a + j],
                                             device_id=(*chip, c), device_id_type=_MESH_ID).wait_recv()
                pltpu.make_async_remote_copy(src_ref=blk, dst_ref=blk, send_sem=f_send.at[3 * a + j], recv_sem=f_recv.at[3 * a + j],
                                             device_id=(x, y, 1 - c), device_id_type=_MESH_ID).start()
        token[...] = jnp.zeros_like(token)

    res = pl.pallas_call(
        body, name=name, in_specs=[_HBM] * n + [_SEM, pl.BlockSpec(memory_space=pl.ANY)],
        out_shape=(pltpu.SemaphoreType.DMA((3 * n,)), pltpu.SemaphoreType.DMA((3 * n,)), *[_hbm(b) for b in landings], _token_shape()),
        out_specs=(_SEM, _SEM, *[_HBM] * n, _TOKEN),
        input_output_aliases={i: 2 + i for i in range(n)},
        compiler_params=pltpu.CompilerParams(has_side_effects=_DATAFLOW),
    )(*landings, ici_sems, after)
    return res[0], res[1], list(res[2:2 + n]), res[-1]


def _gather_wait(blocks, landings, send_sems, d2d_sems, f_send, f_recv, after, *, name):
    n = len(landings)

    def body(*refs):
        ins, lands = refs[:n], refs[n:2 * n]
        send, d2d, fs, fr = refs[2 * n:2 * n + 4]
        x, y, c, _, chips = _place()
        me = (x, y, c)
        for a in range(n):
            own = lands[a].at[_dev_slot(x, y, 1 - c)]
            pltpu.make_async_remote_copy(src_ref=ins[a], dst_ref=own, send_sem=send.at[4 * a], recv_sem=d2d.at[a],
                                         device_id=me, device_id_type=_MESH_ID).wait_recv()
            for j, chip in enumerate(chips):
                blk = lands[a].at[_dev_slot(*chip, 1 - c)]
                pltpu.make_async_remote_copy(src_ref=blk, dst_ref=blk, send_sem=fs.at[3 * a + j], recv_sem=fr.at[3 * a + j],
                                             device_id=me, device_id_type=_MESH_ID).wait_recv()
            for k in range(4):
                pltpu.make_async_remote_copy(src_ref=ins[a], dst_ref=own, send_sem=send.at[4 * a + k], recv_sem=d2d.at[a],
                                             device_id=me, device_id_type=_MESH_ID).wait_send()
            for j in range(3):
                pltpu.make_async_remote_copy(src_ref=own, dst_ref=own, send_sem=fs.at[3 * a + j], recv_sem=fr.at[3 * a + j],
                                             device_id=me, device_id_type=_MESH_ID).wait_send()

    res = pl.pallas_call(
        body, name=name, in_specs=[_HBM] * (2 * n) + [_SEM] * 4 + [pl.BlockSpec(memory_space=pl.ANY)],
        out_shape=(*[_hbm(b) for b in blocks], *[_hbm(b) for b in landings]), out_specs=tuple([_HBM] * (2 * n)),
        input_output_aliases={i: i for i in range(2 * n)},
        compiler_params=pltpu.CompilerParams(has_side_effects=_DATAFLOW),
    )(*blocks, *landings, send_sems, d2d_sems, f_send, f_recv, after)
    return list(res[n:])


def _chip_exchange_start(sums, landings, *, name):
    n = len(sums)

    def body(*refs):
        ins, lands = refs[:n], refs[n:2 * n]
        send_sems, recv_sems = refs[2 * n:2 * n + 2]
        token = refs[-1]
        _, _, c, p, chips = _place()
        for a in range(n):
            for j, (qx, qy) in enumerate(chips):
                pltpu.make_async_remote_copy(src_ref=ins[a].at[2 * qx + qy], dst_ref=lands[a].at[p], send_sem=send_sems.at[3 * a + j],
                                             recv_sem=recv_sems.at[3 * a + j], device_id=(qx, qy, c), device_id_type=_MESH_ID).start()
        token[...] = jnp.zeros_like(token)

    res = pl.pallas_call(
        body, name=name, in_specs=[_HBM] * (2 * n),
        out_shape=(pltpu.SemaphoreType.DMA((3 * n,)), pltpu.SemaphoreType.DMA((3 * n,)),
                   *[_hbm(b) for b in sums], *[_hbm(b) for b in landings], _token_shape()),
        out_specs=(_SEM, _SEM, *[_HBM] * (2 * n), _TOKEN),
        input_output_aliases={i: 2 + i for i in range(2 * n)},
        compiler_params=pltpu.CompilerParams(has_side_effects=_DATAFLOW),
    )(*[pltpu.with_memory_space_constraint(b, pltpu.HBM) for b in sums],
      *[pltpu.with_memory_space_constraint(b, pltpu.HBM) for b in landings])
    return res[0], res[1], list(res[2:2 + n]), list(res[2 + n:2 + 2 * n]), res[-1]


def _chip_exchange_wait(sums, landings, send_sems, recv_sems, after, *, name):
    n = len(sums)

    def body(*refs):
        ins, lands = refs[:n], refs[n:2 * n]
        send, recv = refs[2 * n:2 * n + 2]
        x, y, c, _, chips = _place()
        for a in range(n):
            for j, (qx, qy) in enumerate(chips):
                q = 2 * qx + qy
                cp = pltpu.make_async_remote_copy(src_ref=ins[a].at[q], dst_ref=lands[a].at[q], send_sem=send.at[3 * a + j],
                                                  recv_sem=recv.at[3 * a + j], device_id=(x, y, c), device_id_type=_MESH_ID)
                cp.wait_recv()
                cp.wait_send()

    res = pl.pallas_call(
        body, name=name, in_specs=[_HBM] * (2 * n) + [_SEM] * 2 + [pl.BlockSpec(memory_space=pl.ANY)],
        out_shape=(*[_hbm(b) for b in sums], *[_hbm(b) for b in landings]), out_specs=tuple([_HBM] * (2 * n)),
        input_output_aliases={i: i for i in range(2 * n)},
        compiler_params=pltpu.CompilerParams(has_side_effects=_DATAFLOW),
    )(*sums, *landings, send_sems, recv_sems, after)
    return list(res[:n]), list(res[n:])


_C1 = 1.0 - ADAM_B1 ** ADAM_STEP
_C2 = 1.0 - ADAM_B2 ** ADAM_STEP


def _adamw_math(g, w, m, v):
    m = ADAM_B1 * m + (1.0 - ADAM_B1) * g
    v = ADAM_B2 * v + (1.0 - ADAM_B2) * (g * g)
    delta = -ADAM_LR * ((m / _C1) / (jnp.sqrt(v / _C2) + ADAM_EPS) + ADAM_WD * w)
    return delta, m, v


def _adamw_reduce(landed, sums, chip, w, m, v, layer, prev, *, name):
    _, R, C = w.shape
    tr = max(t for t in range(16, R + 1, 16) if R % t == 0 and t * C <= 256 * 1024)

    def body(chip_ref, p_ref, own_ref, w_ref, m_ref, v_ref, *rest):
        g_ref, d_ref, nm_ref, nv_ref = rest[-4:]
        own = own_ref[...].astype(_F32)
        g = jnp.where(chip_ref[0] == 0, own, p_ref[0].astype(_F32))
        for q in range(1, N_CHIP):
            g = g + jnp.where(chip_ref[0] == q, own, p_ref[q].astype(_F32))
        d, nm, nv = _adamw_math(g, w_ref[...], m_ref[...], v_ref[...])
        g_ref[...] = g
        d_ref[...] = d
        nm_ref[...] = nm
        nv_ref[...] = nv

    blk = pl.BlockSpec((None, tr, C), lambda i, chip: (layer, i, 0))
    shape = jax.ShapeDtypeStruct((DEPTH, R, C), _F32)
    kept = [] if prev is None else list(prev)
    grid_spec = pltpu.PrefetchScalarGridSpec(
        num_scalar_prefetch=1, grid=(R // tr,),
        in_specs=[pl.BlockSpec((N_CHIP, tr, C), lambda i, chip: (0, i, 0)),
                  pl.BlockSpec((None, tr, C), lambda i, chip: (chip[0], i, 0)), blk, blk, blk]
        + [pl.BlockSpec(memory_space=pl.ANY)] * len(kept),
        out_specs=[blk] * 4)
    return pl.pallas_call(
        body, grid_spec=grid_spec, out_shape=[shape] * 4, name=name,
        input_output_aliases={6 + k: k for k in range(len(kept))},
        compiler_params=_cparams(dimension_semantics=("parallel",)),
    )(chip.reshape(1), landed, sums, w, m, v, *kept)


_PACK_LANES = 128
_LAYER_ROWS = 248
_LB_ROWS = (A_HEADS * HEAD_DIM) // _PACK_LANES


def _small_reduce(parts, lb_logits, *, name):
    rows = DEPTH * _LAYER_ROWS

    def body(p_ref, lg_ref, o_ref):
        g = p_ref[0]
        for s in range(1, N_DEV):
            g = g + p_ref[s]
        o_ref[...] = g
        lg = lg_ref[...]
        e = jnp.exp(lg - jnp.max(lg, axis=0, keepdims=True))
        p = e / jnp.sum(e, axis=0, keepdims=True)
        d1 = g[_LAYER_ROWS:_LAYER_ROWS + _LB_ROWS, :] * p[0] * p[1]
        o_ref[0:_LB_ROWS, :] = -d1
        o_ref[_LAYER_ROWS:_LAYER_ROWS + _LB_ROWS, :] = d1

    return pl.pallas_call(
        body, out_shape=jax.ShapeDtypeStruct((rows, _PACK_LANES), _F32), name=name,
        compiler_params=_cparams(),
    )(parts, lb_logits.reshape(DEPTH, _LB_ROWS, _PACK_LANES))


def _adamw_small(g, w, m, v, *, name):
    def body(g_ref, w_ref, m_ref, v_ref, d_ref, nm_ref, nv_ref):
        d, nm, nv = _adamw_math(g_ref[...], w_ref[...], m_ref[...], v_ref[...])
        d_ref[...] = d
        nm_ref[...] = nm
        nv_ref[...] = nv

    shape = jax.ShapeDtypeStruct(g.shape, _F32)
    return pl.pallas_call(body, out_shape=[shape] * 3, name=name, compiler_params=_cparams())(g, w, m, v)


def _pack(vectors, rows):
    flat = jnp.concatenate([v.reshape(-1).astype(_F32) for v in vectors])
    return jnp.pad(flat, (0, rows * _PACK_LANES - flat.shape[0])).reshape(rows, _PACK_LANES)


def _unpack(packed, shapes):
    flat = packed.reshape(-1)
    out, at = [], 0
    for s in shapes:
        size = int(np.prod(s))
        out.append(flat[at:at + size].reshape(s))
        at += size
    return out


_BIG = ("w_in", "w_gate", "w_up", "w_out", "w_down")
_COLUMN_SHARDED = ("w_in", "w_gate", "w_up")


def _full_weight(name, g):
    if name == "w_out":
        return g.reshape(D_MODEL, D_MODEL)
    if name == "w_down":
        return g.reshape(D_FF, D_MODEL)
    if name == "conv_w":
        return g.transpose(1, 0, 2).reshape(g.shape[1], N_DEV * SHARD_COLS)
    return g


class _WeightGather:
    def __init__(self, names, blocks, me, tag):
        self.names, self.tag = names, tag
        landings = [lax.dynamic_update_index_in_dim(lax.empty((N_DEV,) + b.shape, b.dtype), b[None], me, 0) for b in blocks]
        self.send, self.d2d, self.ici, self.blocks, self.lands, self.token = _gather_start(
            blocks, landings, name=f"gather_start_{tag}")

    def forward(self, after):
        self.f_send, self.f_recv, self.lands, token = _gather_forward(self.lands, self.ici, after, name=f"gather_forward_{self.tag}")
        return token

    def wait(self, after):
        got = _gather_wait(self.blocks, self.lands, self.send, self.d2d, self.f_send, self.f_recv, after,
                           name=f"gather_wait_{self.tag}")
        return {n: _full_weight(n, g) for n, g in zip(self.names, got)}


class _LayerWeights:
    def __init__(self, ready, pending=(), forwards=(), tokens=()):
        self.ready, self.pending, self.forwards, self._tokens = dict(ready), list(pending), list(forwards), list(tokens)

    def at(self, point, after):
        for when, gather in self.forwards:
            if when == point:
                self._tokens.append(gather.forward(after))

    def tokens(self):
        out, self._tokens = self._tokens, []
        return out

    def get(self, name, after):
        if name not in self.ready:
            group, = [g for g in self.pending if name in g.names]
            self.ready.update(group.wait(after))
        return self.ready[name]


def _layer_fwd(x, xb, ws, lb_logits, a_norm_w, c_sink, ln1_g, ln1_b, conv_b, ln2_g, ln2_b, tabs, l):
    proj = _mm_w_slabs(xb, ws.get("w_in", xb), tm=1024, after=ws.tokens(), name=f"proj_{l}")
    o_a, raw, states = _hgrn_fwd(proj, lb_logits, a_norm_w, l, name=f"hgrn_fwd_{l}")
    o_b, lse_b = _band_fwd(proj, tabs, name=f"dilated_fwd_{l}", **_DILATED)
    ws.at("dilated", o_b)
    o_c, lse_c = _band_fwd(proj, tabs, sink=c_sink, name=f"swa_fwd_{l}", **_SWA)
    mixed = jnp.concatenate([o_a, o_b, o_c], axis=1).astype(_MXU_DTYPE)
    y = _mm(mixed, ws.get("w_out", mixed), tm=1024, tn=512, after=ws.tokens(), name=f"mix_out_{l}")
    z1, x1, x1b = _ln_fwd(x, y, ln1_g, ln1_b, name=f"ln1_fwd_{l}")
    g = _mm_w_slabs(x1b, ws.get("w_gate", x1b), tm=1024, name=f"ffn_gate_{l}")
    u = _mm_w_slabs(x1b, ws.get("w_up", x1b), tm=1024, name=f"ffn_up_{l}")
    hb = _conv_gate_fwd(g, u, ws.get("conv_w", u), conv_b, name=f"conv_gate_fwd_{l}")
    ws.at("conv", hb)
    y2 = _mm(hb, ws.get("w_down", hb), tm=512, tn=512, after=ws.tokens(), name=f"ffn_down_{l}")
    z2, x2, x2b = _ln_fwd(x1, y2, ln2_g, ln2_b, name=f"ln2_fwd_{l}")
    res = dict(xb=xb, proj=proj, raw=raw, states=states, o_b=o_b, lse_b=lse_b, o_c=o_c, lse_c=lse_c,
               mixed=mixed, z1=z1, x1b=x1b, g=g, u=u, hb=hb, z2=z2)
    return x2, x2b, res


class _GradExchange:
    def __init__(self, core, chip):
        self.core, self.chip, self.groups, self._tokens = core, chip, [], []

    def launch(self, names, slabs, l, tag):
        mine = [s.reshape((N_CHIP, 2) + s.shape[1:]) for s in slabs]
        theirs = _sibling_swap(mine, name=f"swap_grads_{tag}")
        sums = [_pair_add(a, b, self.core, name=f"pair_add_{n}_{l}") for n, a, b in zip(names, mine, theirs)]
        landings = [lax.empty(s.shape, s.dtype) for s in sums]
        send, recv, sums, landings, token = _chip_exchange_start(sums, landings, name=f"exchange_start_{tag}")
        self.groups.append((names, l, tag, send, recv, sums, landings))
        self._tokens.append(token)

    def tokens(self):
        out, self._tokens = self._tokens, []
        return out

    def finish(self, weights, mom1, mom2, after):
        out = {}
        after = list(after) + self.tokens()
        for names, l, tag, send, recv, sums, landings in self.groups:
            sums, landings = _chip_exchange_wait(sums, landings, send, recv, after[-1], name=f"exchange_wait_{tag}")
            for n, s, landed in zip(names, sums, landings):
                out[n] = _adamw_reduce(landed, s, self.chip, weights[n], mom1[n], mom2[n], l, out.get(n), name=f"adamw_{n}_{l}")
                after = [out[n][0]]
        return out


def _layer_bwd(dx2, res, w, lb_logits, a_norm_w, c_sink, ln1_g, conv_b, ln2_g, tabs, exchange, l):
    dz2, dz2b, d_ln2_g, d_ln2_b = _ln_bwd(res["z2"], dx2, None, ln2_g, name=f"ln2_bwd_{l}")
    dh = _mm(dz2b, w["w_down"], tb=True, tm=1024, tn=512, after=exchange.tokens(), name=f"ffn_down_dx_{l}")
    d_w_down = _mm(res["hb"], dz2b, ta=True, tm=512, tn=512, out_dtype=_GRAD_DTYPE, name=f"ffn_down_dw_{l}")
    dg, du, d_conv_w, d_conv_b = _conv_gate_bwd(dh, res["g"], res["u"], w["conv_w"], conv_b, name=f"conv_gate_bwd_{l}")
    t = _mm_nt_w_slabs(dg, w["w_gate"], tm=512, tn=512, name=f"ffn_gate_dx_{l}")
    dx1 = _mm_nt_w_slabs(du, w["w_up"], tm=512, tn=512, add=t, name=f"ffn_up_dx_{l}")
    d_w_gate = _mm_tn_slabs(res["x1b"], dg, tm=1024, name=f"ffn_gate_dw_{l}")
    d_w_up = _mm_tn_slabs(res["x1b"], du, tm=1024, name=f"ffn_up_dw_{l}")
    dz1, dz1b, d_ln1_g, d_ln1_b = _ln_bwd(res["z1"], dx1, dz2, ln1_g, name=f"ln1_bwd_{l}")
    d_w_out = _mm(res["mixed"], dz1b, ta=True, tm=1024, tn=512, out_dtype=_GRAD_DTYPE, name=f"mix_out_dw_{l}")
    exchange.launch(("w_down", "w_gate", "w_up", "w_out"),
                    [d_w_down.reshape(N_DEV, D_FF // N_DEV, D_MODEL), d_w_gate, d_w_up,
                     d_w_out.reshape(N_DEV, D_MODEL // N_DEV, D_MODEL)], l, f"ffn_{l}")
    dmixed = _mm(dz1b, w["w_out"], tb=True, tm=1024, tn=512, after=exchange.tokens(), name=f"mix_out_dx_{l}")
    dq_a, df_a, di_a, dg_a, d_norm_w, d_lb = _hgrn_bwd(res["proj"], lb_logits, a_norm_w, res["raw"], res["states"],
                                                      dmixed, l, name=f"hgrn_bwd_{l}")
    dq_b, dk_b, dv_b = _band_bwd(res["proj"], tabs, dmixed, res["o_b"], res["lse_b"], do0=A_HEADS,
                                 name=f"dilated_bwd_{l}", **_DILATED)
    dq_c, dk_c, dv_c, d_sink = _band_bwd(res["proj"], tabs, dmixed, res["o_c"], res["lse_c"], do0=A_HEADS + B_HEADS,
                                         sink=c_sink, name=f"swa_bwd_{l}", **_SWA)
    dproj = jnp.concatenate([dq_a, df_a, di_a, dg_a, dq_b, dk_b, dv_b, dq_c, dk_c, dv_c], axis=1)
    d_w_in = _mm_tn_slabs(res["xb"], dproj, tm=1024, name=f"proj_dw_{l}")
    exchange.launch(("w_in",), [d_w_in], l, f"mix_{l}")
    dx = _mm_nt_w_slabs(dproj, w["w_in"], tm=512, tn=512, add=dz1, add_scale=ALPHA, after=exchange.tokens(),
                        name=f"proj_dx_{l}")
    small = [d_lb, d_norm_w, jnp.pad(d_sink, (0, _PACK_LANES - C_HEADS)), d_ln1_g, d_ln1_b, d_ln2_g, d_ln2_b, d_conv_b,
             d_conv_w]
    return dx, small


def kernel(x, w_in, lb_logits, a_norm_w, c_sinks, w_out, ln1_g, ln1_b, w_gate, w_up, conv_w, conv_b, w_down, ln2_g, ln2_b, loss_target, m_w_in, m_lb_logits, m_a_norm_w, m_c_sinks, m_w_out, m_ln1_g, m_ln1_b, m_w_gate, m_w_up, m_conv_w, m_conv_b, m_w_down, m_ln2_g, m_ln2_b, v_w_in, v_lb_logits, v_a_norm_w, v_c_sinks, v_w_out, v_ln1_g, v_ln1_b, v_w_gate, v_w_up, v_conv_w, v_conv_b, v_w_down, v_ln2_g, v_ln2_b):
    weights = dict(w_in=w_in, lb_logits=lb_logits, a_norm_w=a_norm_w, c_sinks=c_sinks, w_out=w_out, ln1_g=ln1_g, ln1_b=ln1_b,
                   w_gate=w_gate, w_up=w_up, conv_w=conv_w, conv_b=conv_b, w_down=w_down, ln2_g=ln2_g, ln2_b=ln2_b)
    mom1 = dict(w_in=m_w_in, lb_logits=m_lb_logits, a_norm_w=m_a_norm_w, c_sinks=m_c_sinks, w_out=m_w_out, ln1_g=m_ln1_g,
                ln1_b=m_ln1_b, w_gate=m_w_gate, w_up=m_w_up, conv_w=m_conv_w, conv_b=m_conv_b, w_down=m_w_down, ln2_g=m_ln2_g,
                ln2_b=m_ln2_b)
    mom2 = dict(w_in=v_w_in, lb_logits=v_lb_logits, a_norm_w=v_a_norm_w, c_sinks=v_c_sinks, w_out=v_w_out, ln1_g=v_ln1_g,
                ln1_b=v_ln1_b, w_gate=v_w_gate, w_up=v_w_up, conv_w=v_conv_w, conv_b=v_conv_b, w_down=v_w_down, ln2_g=v_ln2_g,
                ln2_b=v_ln2_b)
    core = lax.axis_index("c").astype(jnp.int32)
    me = 4 * lax.axis_index("x") + 2 * lax.axis_index("y") + core
    tabs = _rope_tables()

    chip = (2 * lax.axis_index("x") + lax.axis_index("y")).astype(jnp.int32)

    def block(n, l):
        return conv_w[l] if n == "conv_w" else weights[n][l].astype(_MXU_DTYPE)

    first, = _gather_blocks([block("w_in", 0)], name="gather_w_in_0")
    rest = ("w_out", "w_gate", "w_up", "conv_w", "w_down")
    rest0 = _WeightGather(rest, [block(n, 0) for n in rest], me, "rest_0")
    in1 = _WeightGather(("w_in",), [block("w_in", 1)], me, "w_in_1")
    rest1 = _WeightGather(rest, [block(n, 1) for n in rest], me, "rest_1")
    layer_ws = [_LayerWeights({"w_in": _full_weight("w_in", first)}, [rest0], [("dilated", rest0), ("conv", in1)],
                              [rest0.token, in1.token, rest1.token]),
                _LayerWeights({}, [in1, rest1], [("dilated", rest1)])]

    xs = x[0]
    xb = xs.astype(_MXU_DTYPE)
    saved = []
    for l in range(DEPTH):
        xs, xb, res = _layer_fwd(xs, xb, layer_ws[l], lb_logits, a_norm_w[l], c_sinks[l], ln1_g[l], ln1_b[l], conv_b[l],
                                 ln2_g[l], ln2_b[l], tabs, l)
        saved.append(res)
    loss_part, dx = _loss_head(xs, loss_target[0], name="loss_head")
    loss = lax.psum(loss_part, ("x", "y", "c"))

    exchange = _GradExchange(core, chip)
    small_parts = [None] * DEPTH
    for l in reversed(range(DEPTH)):
        dx, small = _layer_bwd(dx, saved[l], layer_ws[l].ready, lb_logits, a_norm_w[l], c_sinks[l], ln1_g[l], conv_b[l],
                               ln2_g[l], tabs, exchange, l)
        small_parts[l] = _pack(small, _LAYER_ROWS)
    def as_slabs(d):
        return {n: jnp.swapaxes(d[n], 1, 2) if n in _COLUMN_SHARDED else d[n] for n in _BIG}

    updated = exchange.finish(as_slabs(weights), as_slabs(mom1), as_slabs(mom2), [dx])
    updated = {n: tuple(jnp.swapaxes(t, 1, 2) for t in u) if n in _COLUMN_SHARDED else u for n, u in updated.items()}
    gathered, = _exchange([jnp.concatenate(small_parts, axis=0)], False, name="gather_small_grads")
    g_small = _small_reduce(gathered, lb_logits, name="small_grads")

    per_layer = [(A_HEADS * HEAD_DIM,), (HEAD_DIM,), (_PACK_LANES,), (D_MODEL,), (D_MODEL,), (D_MODEL,), (D_MODEL,), (D_FF,),
                 (3, D_FF)]
    names = ("lb_logits", "a_norm_w", "c_sinks", "ln1_g", "ln1_b", "ln2_g", "ln2_b", "conv_b", "conv_w")
    grads = {n: [] for n in names}
    for l in range(DEPTH):
        for n, t in zip(names, _unpack(g_small[l * _LAYER_ROWS:(l + 1) * _LAYER_ROWS], per_layer)):
            grads[n].append(t)
    grads = {n: jnp.stack(t) for n, t in grads.items()}
    grads["c_sinks"] = grads["c_sinks"][:, :C_HEADS]
    grads["conv_w"] = lax.dynamic_slice_in_dim(grads["conv_w"], me * SHARD_COLS, SHARD_COLS, axis=2)
    shapes = [grads[n].shape for n in names]
    rows = -(-sum(int(np.prod(s)) for s in shapes) // (8 * _PACK_LANES)) * 8
    d_s, m_s, v_s = _adamw_small(_pack([grads[n] for n in names], rows), _pack([weights[n] for n in names], rows),
                                 _pack([mom1[n] for n in names], rows), _pack([mom2[n] for n in names], rows),
                                 name="adamw_small")
    delta = dict(zip(names, _unpack(d_s, shapes)))
    new_m = dict(zip(names, _unpack(m_s, shapes)))
    new_v = dict(zip(names, _unpack(v_s, shapes)))
    for n in _BIG:
        grads[n], delta[n], new_m[n], new_v[n] = updated[n]

    order = ("w_in", "lb_logits", "a_norm_w", "c_sinks", "w_out", "ln1_g", "ln1_b", "w_gate", "w_up", "conv_w", "conv_b",
             "w_down", "ln2_g", "ln2_b")
    return (loss, dx[None], *[grads[n] for n in order], *[delta[n] for n in order], *[new_m[n] for n in order],
            *[new_v[n] for n in order])
```

```python
import functools

import jax
import jax.numpy as jnp
import numpy as np
from jax import lax
from jax.experimental import pallas as pl
from jax.experimental.pallas import tpu as pltpu

D_MODEL = 2048
SEQ = 2048
DEPTH = 2
HEAD_DIM = 128
A_HEADS = 4
B_HEADS = 6
C_HEADS = 6
C_KV_HEADS = 2
A_CHUNK = 16
DILATIONS = (1, 4, 16)
BLOCK = 128
ROPE_THETA = 500000.0
ROPE_DIM = 32
D_FF = 5632
IN_WIDTH = 5632
LN_EPS = 1e-5
ALPHA = (2 * DEPTH) ** 0.25
N_DEV = 8
SHARD_COLS = IN_WIDTH // N_DEV

ADAM_LR = 0.001
ADAM_B1 = 0.9
ADAM_B2 = 0.999
ADAM_EPS = 1e-08
ADAM_WD = 0.01
ADAM_STEP = 10

A_COLS = 16
QKV_COLS = 28
QB0, KB0, VB0, QC0, KC0, VC0 = 0, 6, 12, 18, 24, 26

_MXU_DTYPE = jnp.bfloat16
_GRAD_DTYPE = jnp.bfloat16
_NEG = -1e30
_VMEM_LIMIT = 56 * 2 ** 20

_F32 = jnp.float32


def _sigmoid(x):
    return 1.0 / (1.0 + jnp.exp(-x))


def _cparams(**kw):
    return pltpu.CompilerParams(vmem_limit_bytes=_VMEM_LIMIT, **kw)


def _mm(a, b, *, ta=False, tb=False, tm, tn, out_dtype=_F32, add=None, add_scale=1.0, after=(), name):
    K = a.shape[0] if ta else a.shape[1]
    M = a.shape[1] if ta else a.shape[0]
    N = b.shape[0] if tb else b.shape[1]
    assert (b.shape[1] if tb else b.shape[0]) == K and M % tm == 0 and N % tn == 0
    dn = (((0 if ta else 1,), (1 if tb else 0,)), ((), ()))

    def body(*refs):
        a_ref, b_ref = refs[:2]
        o_ref = refs[-1]
        r = lax.dot_general(a_ref[...], b_ref[...], dn, preferred_element_type=_F32)
        if add is not None:
            r = r + add_scale * refs[2][...]
        o_ref[...] = r.astype(o_ref.dtype)

    a_spec = pl.BlockSpec((K, tm), lambda i, j: (0, i)) if ta else pl.BlockSpec((tm, K), lambda i, j: (i, 0))
    b_spec = pl.BlockSpec((tn, K), lambda i, j: (j, 0)) if tb else pl.BlockSpec((K, tn), lambda i, j: (0, j))
    o_spec = pl.BlockSpec((tm, tn), lambda i, j: (i, j))
    in_specs = [a_spec, b_spec] + ([o_spec] if add is not None else []) + [pl.BlockSpec(memory_space=pl.ANY)] * len(after)
    args = (a, b) + ((add,) if add is not None else ()) + tuple(after)
    return pl.pallas_call(
        body, grid=(M // tm, N // tn), in_specs=in_specs, out_specs=o_spec,
        out_shape=jax.ShapeDtypeStruct((M, N), out_dtype), name=name,
        compiler_params=_cparams(dimension_semantics=("parallel", "parallel")),
    )(*args)


_PAIR = 2 * SHARD_COLS


def _mm_tn_slabs(a, b, *, tm, name):
    K, M = a.shape
    assert b.shape == (K, N_DEV * SHARD_COLS) and M % tm == 0

    def body(a_ref, b_ref, o_ref):
        a_blk = a_ref[...]
        for s in range(2):
            o_ref[s] = lax.dot_general(b_ref[:, s * SHARD_COLS:(s + 1) * SHARD_COLS], a_blk, _TN,
                                       preferred_element_type=_F32).astype(o_ref.dtype)

    return pl.pallas_call(
        body, grid=(M // tm, N_DEV // 2),
        in_specs=[pl.BlockSpec((K, tm), lambda i, p: (0, i)), pl.BlockSpec((K, _PAIR), lambda i, p: (0, p))],
        out_specs=pl.BlockSpec((2, SHARD_COLS, tm), lambda i, p: (p, 0, i)),
        out_shape=jax.ShapeDtypeStruct((N_DEV, SHARD_COLS, M), _GRAD_DTYPE), name=name,
        compiler_params=_cparams(dimension_semantics=("parallel", "parallel")),
    )(a, b)


def _mm_w_slabs(a, w, *, tm, after=(), name):
    M, K = a.shape
    assert w.shape == (N_DEV, K, SHARD_COLS) and M % tm == 0

    def body(a_ref, w_ref, *rest):
        o_ref = rest[-1]
        a_blk = a_ref[...]
        for s in range(2):
            o_ref[:, s * SHARD_COLS:(s + 1) * SHARD_COLS] = jnp.dot(a_blk, w_ref[s], preferred_element_type=_F32)

    return pl.pallas_call(
        body, grid=(M // tm, N_DEV // 2),
        in_specs=[pl.BlockSpec((tm, K), lambda i, p: (i, 0)), pl.BlockSpec((2, K, SHARD_COLS), lambda i, p: (p, 0, 0))]
        + [pl.BlockSpec(memory_space=pl.ANY)] * len(after),
        out_specs=pl.BlockSpec((tm, _PAIR), lambda i, p: (i, p)),
        out_shape=jax.ShapeDtypeStruct((M, N_DEV * SHARD_COLS), _F32), name=name,
        compiler_params=_cparams(dimension_semantics=("parallel", "parallel")),
    )(a, w, *after)


def _mm_nt_w_slabs(a, w, *, tm, tn, add=None, add_scale=1.0, after=(), name):
    M = a.shape[0]
    N = w.shape[1]
    assert a.shape[1] == N_DEV * SHARD_COLS and w.shape[0] == N_DEV and M % tm == 0 and N % tn == 0

    def body(a_ref, w_ref, *rest):
        o_ref = rest[-1]
        acc = add_scale * rest[0][...] if add is not None else None
        for j in range(N_DEV):
            t = lax.dot_general(a_ref[:, j * SHARD_COLS:(j + 1) * SHARD_COLS], w_ref[j], _NT, preferred_element_type=_F32)
            acc = t if acc is None else acc + t
        o_ref[...] = acc

    o_spec = pl.BlockSpec((tm, tn), lambda i, j: (i, j))
    return pl.pallas_call(
        body, grid=(M // tm, N // tn),
        in_specs=[pl.BlockSpec((tm, N_DEV * SHARD_COLS), lambda i, j: (i, 0)),
                  pl.BlockSpec((N_DEV, tn, SHARD_COLS), lambda i, j: (0, j, 0))]
        + ([o_spec] if add is not None else []) + [pl.BlockSpec(memory_space=pl.ANY)] * len(after),
        out_specs=o_spec, out_shape=jax.ShapeDtypeStruct((M, N), _F32), name=name,
        compiler_params=_cparams(dimension_semantics=("parallel", "parallel")),
    )(a, w, *((add,) if add is not None else ()), *after)


def _ln_fwd(x, y, g, b, *, name):
    tm = 256

    def body(x_ref, y_ref, g_ref, b_ref, z_ref, o_ref, ob_ref):
        z = ALPHA * x_ref[...] + y_ref[...]
        mu = jnp.mean(z, axis=-1, keepdims=True)
        zc = z - mu
        var = jnp.mean(zc * zc, axis=-1, keepdims=True)
        o = zc * lax.rsqrt(var + LN_EPS) * g_ref[...] + b_ref[...]
        z_ref[...] = z
        o_ref[...] = o
        ob_ref[...] = o.astype(ob_ref.dtype)

    row = pl.BlockSpec((tm, D_MODEL), lambda i: (i, 0))
    vec = pl.BlockSpec((1, D_MODEL), lambda i: (0, 0))
    return pl.pallas_call(
        body, grid=(SEQ // tm,), in_specs=[row, row, vec, vec], out_specs=[row, row, row],
        out_shape=[jax.ShapeDtypeStruct((SEQ, D_MODEL), _F32), jax.ShapeDtypeStruct((SEQ, D_MODEL), _F32),
                   jax.ShapeDtypeStruct((SEQ, D_MODEL), _MXU_DTYPE)],
        name=name, compiler_params=_cparams(dimension_semantics=("parallel",)),
    )(x, y, g.reshape(1, D_MODEL), b.reshape(1, D_MODEL))


def _ln_bwd(z, d_a, d_res, g, *, name):
    tm = 256

    def body(*refs):
        if d_res is None:
            z_ref, da_ref, g_ref, dz_ref, dzb_ref, dg_ref, db_ref = refs
            dout = da_ref[...]
        else:
            z_ref, da_ref, dr_ref, g_ref, dz_ref, dzb_ref, dg_ref, db_ref = refs
            dout = da_ref[...] + ALPHA * dr_ref[...]
        z = z_ref[...]
        mu = jnp.mean(z, axis=-1, keepdims=True)
        zc = z - mu
        var = jnp.mean(zc * zc, axis=-1, keepdims=True)
        rstd = lax.rsqrt(var + LN_EPS)
        xh = zc * rstd
        dxh = dout * g_ref[...]
        m1 = jnp.mean(dxh, axis=-1, keepdims=True)
        m2 = jnp.mean(dxh * xh, axis=-1, keepdims=True)
        dz = rstd * (dxh - m1 - xh * m2)
        dz_ref[...] = dz
        dzb_ref[...] = dz.astype(dzb_ref.dtype)

        @pl.when(pl.program_id(0) == 0)
        def _():
            dg_ref[...] = jnp.zeros_like(dg_ref)
            db_ref[...] = jnp.zeros_like(db_ref)

        dg_ref[0:1, :] += jnp.sum(dout * xh, axis=0, keepdims=True)
        db_ref[0:1, :] += jnp.sum(dout, axis=0, keepdims=True)

    row = pl.BlockSpec((tm, D_MODEL), lambda i: (i, 0))
    vec = pl.BlockSpec((1, D_MODEL), lambda i: (0, 0))
    acc = pl.BlockSpec((8, D_MODEL), lambda i: (0, 0))
    ins = [z, d_a] + ([d_res] if d_res is not None else []) + [g.reshape(1, D_MODEL)]
    in_specs = [row, row] + ([row] if d_res is not None else []) + [vec]
    dz, dzb, dg, db = pl.pallas_call(
        body, grid=(SEQ // tm,), in_specs=in_specs, out_specs=[row, row, acc, acc],
        out_shape=[jax.ShapeDtypeStruct((SEQ, D_MODEL), _F32), jax.ShapeDtypeStruct((SEQ, D_MODEL), _MXU_DTYPE),
                   jax.ShapeDtypeStruct((8, D_MODEL), _F32), jax.ShapeDtypeStruct((8, D_MODEL), _F32)],
        name=name, compiler_params=_cparams(dimension_semantics=("arbitrary",)),
    )(*ins)
    return dz, dzb, dg[0], db[0]


def _loss_head(y, target, *, name):
    tm = 256

    def body(y_ref, t_ref, d_ref, l_ref):
        e = y_ref[...] - t_ref[...]
        d_ref[...] = e * (1.0 / D_MODEL)

        @pl.when(pl.program_id(0) == 0)
        def _():
            l_ref[...] = jnp.zeros_like(l_ref)

        l_ref[...] += (0.5 / D_MODEL) * jnp.sum(e * e)

    row = pl.BlockSpec((tm, D_MODEL), lambda i: (i, 0))
    d, l = pl.pallas_call(
        body, grid=(SEQ // tm,), in_specs=[row, row], out_specs=[row, pl.BlockSpec((8, 128), lambda i: (0, 0))],
        out_shape=[jax.ShapeDtypeStruct((SEQ, D_MODEL), _F32), jax.ShapeDtypeStruct((8, 128), _F32)],
        name=name, compiler_params=_cparams(dimension_semantics=("arbitrary",)),
    )(y, target)
    return l[0, 0], d


_CONV_TN = 256


def _shift_down(v, k, rows):
    return jnp.where(rows >= k, pltpu.roll(v, k, axis=0), 0.0)


def _shift_up(v, k, rows):
    return jnp.where(rows < SEQ - k, pltpu.roll(v, SEQ - k, axis=0), 0.0)


def _conv_gate_fwd(g, u, conv_w, conv_b, *, name):
    def body(g_ref, u_ref, w_ref, b_ref, h_ref):
        gv = g_ref[...]
        rows = lax.broadcasted_iota(jnp.int32, gv.shape, 0)
        w = w_ref[...]
        gc = b_ref[...] + w[2:3, :] * gv + w[1:2, :] * _shift_down(gv, 1, rows) + w[0:1, :] * _shift_down(gv, 2, rows)
        h_ref[...] = (gc * _sigmoid(gc) * u_ref[...]).astype(h_ref.dtype)

    col = pl.BlockSpec((SEQ, _CONV_TN), lambda j: (0, j))
    return pl.pallas_call(
        body, grid=(D_FF // _CONV_TN,),
        in_specs=[col, col, pl.BlockSpec((3, _CONV_TN), lambda j: (0, j)), pl.BlockSpec((1, _CONV_TN), lambda j: (0, j))],
        out_specs=col, out_shape=jax.ShapeDtypeStruct((SEQ, D_FF), _MXU_DTYPE), name=name,
        compiler_params=_cparams(dimension_semantics=("parallel",)),
    )(g, u, conv_w, conv_b.reshape(1, D_FF))


def _conv_gate_bwd(dh, g, u, conv_w, conv_b, *, name):
    def body(dh_ref, g_ref, u_ref, w_ref, b_ref, dg_ref, du_ref, dw_ref, db_ref):
        gv = g_ref[...]
        rows = lax.broadcasted_iota(jnp.int32, gv.shape, 0)
        w = w_ref[...]
        g1 = _shift_down(gv, 1, rows)
        g2 = _shift_down(gv, 2, rows)
        gc = b_ref[...] + w[2:3, :] * gv + w[1:2, :] * g1 + w[0:1, :] * g2
        sg = _sigmoid(gc)
        dh = dh_ref[...]
        du_ref[...] = (dh * (gc * sg)).astype(du_ref.dtype)
        dgc = dh * u_ref[...] * (sg * (1.0 + gc * (1.0 - sg)))
        dg = w[2:3, :] * dgc + w[1:2, :] * _shift_up(dgc, 1, rows) + w[0:1, :] * _shift_up(dgc, 2, rows)
        dg_ref[...] = dg.astype(dg_ref.dtype)
        dw_ref[0:1, :] = jnp.sum(dgc * g2, axis=0, keepdims=True)
        dw_ref[1:2, :] = jnp.sum(dgc * g1, axis=0, keepdims=True)
        dw_ref[2:3, :] = jnp.sum(dgc * gv, axis=0, keepdims=True)
        db_ref[...] = jnp.sum(dgc, axis=0, keepdims=True)

    col = pl.BlockSpec((SEQ, _CONV_TN), lambda j: (0, j))
    w3 = pl.BlockSpec((3, _CONV_TN), lambda j: (0, j))
    w1 = pl.BlockSpec((1, _CONV_TN), lambda j: (0, j))
    dg, du, dw, db = pl.pallas_call(
        body, grid=(D_FF // _CONV_TN,), in_specs=[col, col, col, w3, w1], out_specs=[col, col, w3, w1],
        out_shape=[jax.ShapeDtypeStruct((SEQ, D_FF), _MXU_DTYPE), jax.ShapeDtypeStruct((SEQ, D_FF), _MXU_DTYPE),
                   jax.ShapeDtypeStruct((3, D_FF), _F32), jax.ShapeDtypeStruct((1, D_FF), _F32)],
        name=name, compiler_params=_cparams(dimension_semantics=("parallel",)),
    )(dh, g, u, conv_w, conv_b.reshape(1, D_FF))
    return dg, du, dw, db[0]


def _rope_tables():
    half = ROPE_DIM // 2
    inv = ROPE_THETA ** (-jnp.arange(0, ROPE_DIM, 2, dtype=_F32) / ROPE_DIM)
    ang = jnp.arange(SEQ, dtype=_F32)[:, None] * inv[None, :]
    cos, sin = jnp.cos(ang), jnp.sin(ang)
    rest = HEAD_DIM - ROPE_DIM
    c = jnp.concatenate([cos, cos, jnp.ones((SEQ, rest), _F32)], axis=1)
    s1 = jnp.concatenate([-sin, jnp.zeros((SEQ, HEAD_DIM - half), _F32)], axis=1)
    s2 = jnp.concatenate([jnp.zeros((SEQ, half), _F32), sin, jnp.zeros((SEQ, rest), _F32)], axis=1)
    return c, s1, s2


def _rope_apply(x, c, s1, s2):
    return x * c + pltpu.roll(x, HEAD_DIM - ROPE_DIM // 2, axis=1) * s1 + pltpu.roll(x, ROPE_DIM // 2, axis=1) * s2


def _rope_transpose(d, c, s1, s2):
    half = ROPE_DIM // 2
    return d * c + pltpu.roll(d * s1, half, axis=1) + pltpu.roll(d * s2, HEAD_DIM - half, axis=1)


_NT = (((1,), (1,)), ((), ()))
_TN = (((0,), (0,)), ((), ()))
_SCALE = HEAD_DIM ** -0.5


def _band_scores(q, k2, n, lag_off):
    s = lax.dot_general(q, k2, _NT, preferred_element_type=_F32) * _SCALE
    row = lax.broadcasted_iota(jnp.int32, (BLOCK, 2 * BLOCK), 0)
    col = lax.broadcasted_iota(jnp.int32, (BLOCK, 2 * BLOCK), 1)
    front = (col >= row + lag_off) & (col < BLOCK) & (n > 0)
    own = (col >= BLOCK) & (col <= row + BLOCK)
    return jnp.where(front | own, s, _NEG)


_BAND_STEPS = SEQ // BLOCK


def _rows(start, d):
    if d == 1:
        return pl.ds(pl.multiple_of(start, BLOCK), BLOCK)
    return pl.ds(start, BLOCK, stride=d)


def _band_block(it, d):
    r, n = it % d, it // d
    span = BLOCK * d
    return n, _rows(r + n * span, d), _rows(r + jnp.maximum(n - 1, 0) * span, d)


def _band_fwd(proj, tabs, *, kv_heads, q_per_kv, q0, k0, v0, dilations, lag_off, sink, name):
    heads = kv_heads * q_per_kv

    def body(*refs):
        q_refs = refs[:q_per_kv]
        k_ref, v_ref, c_ref, s1_ref, s2_ref = refs[q_per_kv:q_per_kv + 5]
        rest = refs[q_per_kv + 5:]
        if sink is not None:
            sk_ref, rest = rest[0], rest[1:]
        o_ref, lse_ref, qs, ks, m_s, l_s, acc_s = rest
        c, s1, s2 = c_ref[...], s1_ref[...], s2_ref[...]
        ks[...] = _rope_apply(k_ref[...], c, s1, s2)
        for i in range(q_per_kv):
            qs[...] = _rope_apply(q_refs[i][...], c, s1, s2)
            for pi, d in enumerate(dilations):
                def step(it, carry, d=d, first=(pi == 0)):
                    n, cur, prev = _band_block(it, d)
                    q = qs[cur, :].astype(_MXU_DTYPE)
                    k2 = jnp.concatenate([ks[prev, :], ks[cur, :]], axis=0).astype(_MXU_DTYPE)
                    v2 = jnp.concatenate([v_ref[prev, :], v_ref[cur, :]], axis=0).astype(_MXU_DTYPE)
                    s = _band_scores(q, k2, n, lag_off)
                    m_b = jnp.max(s, axis=1, keepdims=True)
                    m_new = m_b if first else jnp.maximum(m_b, m_s[cur, :][:, 0:1])
                    p = jnp.exp(s - m_new)
                    l_new = jnp.sum(p, axis=1, keepdims=True)
                    acc = jnp.dot(p.astype(_MXU_DTYPE), v2, preferred_element_type=_F32)
                    if not first:
                        a = jnp.exp(m_s[cur, :][:, 0:1] - m_new)
                        l_new = l_new + a * l_s[cur, :][:, 0:1]
                        acc = acc + a * acc_s[cur, :]
                    m_s[cur, :] = jnp.broadcast_to(m_new, (BLOCK, HEAD_DIM))
                    l_s[cur, :] = jnp.broadcast_to(l_new, (BLOCK, HEAD_DIM))
                    acc_s[cur, :] = acc
                    return carry

                lax.fori_loop(0, _BAND_STEPS, step, 0, unroll=4)
            m, den = m_s[...], l_s[...]
            if sink is not None:
                sk = sk_ref[i]
                m_f = jnp.maximum(m, sk)
                a = jnp.exp(m - m_f)
                den = den * a + jnp.exp(sk - m_f)
                o = acc_s[...] * a / den
                m = m_f
            else:
                o = acc_s[...] / den
            o_ref[:, i * HEAD_DIM:(i + 1) * HEAD_DIM] = o
            lse_ref[:, i * HEAD_DIM:(i + 1) * HEAD_DIM] = m + jnp.log(den)

    col = (SEQ, HEAD_DIM)
    in_specs = [pl.BlockSpec(col, functools.partial(lambda g, i: (0, A_COLS + q0 + g * q_per_kv + i), i=i)) for i in range(q_per_kv)]
    in_specs += [pl.BlockSpec(col, lambda g: (0, A_COLS + k0 + g)), pl.BlockSpec(col, lambda g: (0, A_COLS + v0 + g))]
    in_specs += [pl.BlockSpec(col, lambda g: (0, 0))] * 3
    args = [proj] * (q_per_kv + 2) + list(tabs)
    if sink is not None:
        in_specs.append(pl.BlockSpec((q_per_kv, 1, HEAD_DIM), lambda g: (g, 0, 0)))
        args.append(jnp.broadcast_to(sink.reshape(heads, 1, 1), (heads, 1, HEAD_DIM)))
    o_spec = pl.BlockSpec((SEQ, q_per_kv * HEAD_DIM), lambda g: (0, g))
    shape = jax.ShapeDtypeStruct((SEQ, heads * HEAD_DIM), _F32)
    return pl.pallas_call(
        body, grid=(kv_heads,), in_specs=in_specs, out_specs=[o_spec, o_spec], out_shape=[shape, shape],
        scratch_shapes=[pltpu.VMEM(col, _F32)] * 5, name=name,
        compiler_params=_cparams(dimension_semantics=("parallel",)),
    )(*args)


def _band_bwd(proj, tabs, dmixed, o, lse, *, kv_heads, q_per_kv, q0, k0, v0, do0, dilations, lag_off, sink, name):
    heads = kv_heads * q_per_kv

    def body(*refs):
        q_refs = refs[:q_per_kv]
        k_ref, v_ref, c_ref, s1_ref, s2_ref = refs[q_per_kv:q_per_kv + 5]
        do_refs = refs[q_per_kv + 5:2 * q_per_kv + 5]
        o_ref, lse_ref = refs[2 * q_per_kv + 5:2 * q_per_kv + 7]
        rest = refs[2 * q_per_kv + 7:]
        if sink is not None:
            sk_ref, rest = rest[0], rest[1:]
            dq_ref, dk_ref, dv_ref, dsk_ref, qs, ks, dq_s, dk_s, dv_s = rest
        else:
            dq_ref, dk_ref, dv_ref, qs, ks, dq_s, dk_s, dv_s = rest
        c, s1, s2 = c_ref[...], s1_ref[...], s2_ref[...]
        ks[...] = _rope_apply(k_ref[...], c, s1, s2)
        dk_s[...] = jnp.zeros_like(dk_s)
        dv_s[...] = jnp.zeros_like(dv_s)
        for i in range(q_per_kv):
            hs = slice(i * HEAD_DIM, (i + 1) * HEAD_DIM)
            qs[...] = _rope_apply(q_refs[i][...], c, s1, s2)
            dq_s[...] = jnp.zeros_like(dq_s)
            do_ref = do_refs[i]
            for d in dilations:
                def step(it, carry, d=d, do_ref=do_ref, hs=hs):
                    n, cur, prev = _band_block(it, d)
                    q = qs[cur, :].astype(_MXU_DTYPE)
                    k2 = jnp.concatenate([ks[prev, :], ks[cur, :]], axis=0).astype(_MXU_DTYPE)
                    v2 = jnp.concatenate([v_ref[prev, :], v_ref[cur, :]], axis=0).astype(_MXU_DTYPE)
                    do = do_ref[cur, :]
                    delta = jnp.sum(do * o_ref[cur, hs], axis=1, keepdims=True)
                    lse_c = lse_ref[cur, hs][:, 0:1]
                    p = jnp.exp(_band_scores(q, k2, n, lag_off) - lse_c)
                    dob = do.astype(_MXU_DTYPE)
                    ds = (p * (lax.dot_general(dob, v2, _NT, preferred_element_type=_F32) - delta) * _SCALE).astype(_MXU_DTYPE)
                    dq_s[cur, :] += jnp.dot(ds, k2, preferred_element_type=_F32)
                    dk2 = lax.dot_general(ds, q, _TN, preferred_element_type=_F32)
                    dv2 = lax.dot_general(p.astype(_MXU_DTYPE), dob, _TN, preferred_element_type=_F32)
                    dk_s[prev, :] += dk2[:BLOCK]
                    dv_s[prev, :] += dv2[:BLOCK]
                    dk_s[cur, :] += dk2[BLOCK:]
                    dv_s[cur, :] += dv2[BLOCK:]
                    return carry

                lax.fori_loop(0, _BAND_STEPS, step, 0, unroll=4)
            dq_ref[:, hs] = _rope_transpose(dq_s[...], c, s1, s2).astype(dq_ref.dtype)
            if sink is not None:
                delta = jnp.sum(do_ref[...] * o_ref[:, hs], axis=1, keepdims=True)
                w_sink = jnp.exp(sk_ref[i] - lse_ref[:, hs])
                dsk_ref[i] = jnp.broadcast_to(jnp.sum(-delta * w_sink[:, 0:1]), (8, HEAD_DIM))
        dk_ref[...] = _rope_transpose(dk_s[...], c, s1, s2).astype(dk_ref.dtype)
        dv_ref[...] = dv_s[...].astype(dv_ref.dtype)

    col = (SEQ, HEAD_DIM)
    in_specs = [pl.BlockSpec(col, functools.partial(lambda g, i: (0, A_COLS + q0 + g * q_per_kv + i), i=i)) for i in range(q_per_kv)]
    in_specs += [pl.BlockSpec(col, lambda g: (0, A_COLS + k0 + g)), pl.BlockSpec(col, lambda g: (0, A_COLS + v0 + g))]
    in_specs += [pl.BlockSpec(col, lambda g: (0, 0))] * 3
    in_specs += [pl.BlockSpec(col, functools.partial(lambda g, i: (0, do0 + g * q_per_kv + i), i=i)) for i in range(q_per_kv)]
    wide = pl.BlockSpec((SEQ, q_per_kv * HEAD_DIM), lambda g: (0, g))
    in_specs += [wide, wide]
    args = [proj] * (q_per_kv + 2) + list(tabs) + [dmixed] * q_per_kv + [o, lse]
    out_specs = [wide, pl.BlockSpec(col, lambda g: (0, g)), pl.BlockSpec(col, lambda g: (0, g))]
    out_shape = [jax.ShapeDtypeStruct((SEQ, heads * HEAD_DIM), _MXU_DTYPE), jax.ShapeDtypeStruct((SEQ, kv_heads * HEAD_DIM), _MXU_DTYPE),
                 jax.ShapeDtypeStruct((SEQ, kv_heads * HEAD_DIM), _MXU_DTYPE)]
    if sink is not None:
        in_specs.append(pl.BlockSpec((q_per_kv, 1, HEAD_DIM), lambda g: (g, 0, 0)))
        args.append(jnp.broadcast_to(sink.reshape(heads, 1, 1), (heads, 1, HEAD_DIM)))
        out_specs.append(pl.BlockSpec((q_per_kv, 8, HEAD_DIM), lambda g: (g, 0, 0)))
        out_shape.append(jax.ShapeDtypeStruct((heads, 8, HEAD_DIM), _F32))
    res = pl.pallas_call(
        body, grid=(kv_heads,), in_specs=in_specs, out_specs=out_specs, out_shape=out_shape,
        scratch_shapes=[pltpu.VMEM(col, _F32)] * 5, name=name,
        compiler_params=_cparams(dimension_semantics=("parallel",)),
    )(*args)
    if sink is not None:
        return res[0], res[1], res[2], res[3][:, 0, 0]
    return res


_DILATED = dict(kv_heads=B_HEADS, q_per_kv=1, q0=QB0, k0=KB0, v0=VB0, dilations=DILATIONS, lag_off=0, sink=None)
_SWA = dict(kv_heads=C_KV_HEADS, q_per_kv=C_HEADS // C_KV_HEADS, q0=QC0, k0=KC0, v0=VC0, dilations=(1,), lag_off=1)


_HG_TILE = 128
_HG_CHUNKS = _HG_TILE // A_CHUNK
_HG_TILES = SEQ // _HG_TILE
_HI = lax.Precision.HIGHEST


def _chunk_tri():
    i = np.arange(_HG_TILE)
    return jnp.asarray(((i[:, None] // A_CHUNK == i[None, :] // A_CHUNK) & (i[None, :] <= i[:, None])).astype(np.float32))


def _layer_lb(lb_ref, layer):
    if layer == 0:
        return jnp.zeros((1, HEAD_DIM), _F32)
    lg = lb_ref[...]
    m = jnp.max(lg, axis=0, keepdims=True)
    e = jnp.exp(lg - m)
    return e[1:2, :] / jnp.sum(e, axis=0, keepdims=True)


def _hgrn_gates(q, fr, lb):
    sgq = _sigmoid(q)
    sg = _sigmoid(fr)
    f = lb + (1.0 - lb) * sg
    return sgq, q * sgq, sg, f, 1.0 - f


def _hgrn_fwd(proj, lb_logits, norm_w, layer, *, name):
    tri = _chunk_tri()

    def body(q_ref, f_ref, i_ref, g_ref, lb_ref, nw_ref, tri_ref, o_ref, raw_ref, st_ref, state):
        @pl.when(pl.program_id(1) == 0)
        def _():
            state[...] = jnp.zeros_like(state)

        lb = _layer_lb(lb_ref, layer)
        _, qs, _, f, k = _hgrn_gates(q_ref[...], f_ref[...], lb)
        v = i_ref[...]
        b = jnp.dot(tri_ref[...], jnp.log(f), precision=_HI, preferred_element_type=_F32)
        eb = jnp.exp(b)
        ridx = lax.broadcasted_iota(jnp.int32, (A_CHUNK, HEAD_DIM), 0)
        outs = []
        for c in range(_HG_CHUNKS):
            sl = slice(c * A_CHUNK, (c + 1) * A_CHUNK)
            bc, qc, kc, vc = b[sl], qs[sl], k[sl], v[sl]
            bl = bc[A_CHUNK - 1:A_CHUNK]
            st = state[...]
            st_ref[0, c] = st
            o_c = lax.dot_general((qc * eb[sl]).astype(_MXU_DTYPE), st.astype(_MXU_DTYPE), _NT, preferred_element_type=_F32)
            rows = []
            for i in range(A_CHUNK):
                di = jnp.exp(jnp.where(ridx <= i, bc[i:i + 1] - bc, _NEG))
                a = jnp.sum(qc[i:i + 1] * kc * di, axis=1, keepdims=True)
                rows.append(jnp.sum(a * vc, axis=0, keepdims=True))
            outs.append(o_c + jnp.concatenate(rows, axis=0))
            kt = (kc * jnp.exp(bl - bc)).astype(_MXU_DTYPE)
            state[...] = st * jnp.exp(bl) + lax.dot_general(vc.astype(_MXU_DTYPE), kt, _TN, preferred_element_type=_F32)
        o = jnp.concatenate(outs, axis=0)
        raw_ref[...] = o
        r = lax.rsqrt(jnp.mean(o * o, axis=-1, keepdims=True) + LN_EPS)
        g = g_ref[...]
        o_ref[...] = o * r * nw_ref[...] * (g * _sigmoid(g))

    blk = (_HG_TILE, HEAD_DIM)

    def col(base):
        return pl.BlockSpec(blk, lambda h, t: (t, base + h))

    o_spec = pl.BlockSpec(blk, lambda h, t: (t, h))
    o_shape = jax.ShapeDtypeStruct((SEQ, A_HEADS * HEAD_DIM), _F32)
    return pl.pallas_call(
        body, grid=(A_HEADS, _HG_TILES),
        in_specs=[col(0), col(4), col(8), col(12), pl.BlockSpec((DEPTH, HEAD_DIM), lambda h, t: (0, h)),
                  pl.BlockSpec((1, HEAD_DIM), lambda h, t: (0, 0)), pl.BlockSpec(blk, lambda h, t: (0, 0))],
        out_specs=[o_spec, o_spec, pl.BlockSpec((1, _HG_CHUNKS, HEAD_DIM, HEAD_DIM), lambda h, t: (h, t, 0, 0))],
        out_shape=[o_shape, o_shape, jax.ShapeDtypeStruct((A_HEADS, SEQ // A_CHUNK, HEAD_DIM, HEAD_DIM), _F32)],
        scratch_shapes=[pltpu.VMEM((HEAD_DIM, HEAD_DIM), _F32)], name=name,
        compiler_params=_cparams(dimension_semantics=("parallel", "arbitrary")),
    )(proj, proj, proj, proj, lb_logits, norm_w.reshape(1, HEAD_DIM), tri)


def _hgrn_bwd(proj, lb_logits, norm_w, raw, states, dmixed, layer, *, name):
    tri = _chunk_tri()
    triu = tri.T

    def body(q_ref, f_ref, i_ref, g_ref, lb_ref, nw_ref, tri_ref, triu_ref, raw_ref, do_ref, st_ref,
             dq_ref, df_ref, di_ref, dg_ref, dnw_ref, dlb_ref, dstate):
        @pl.when(pl.program_id(1) == 0)
        def _():
            dstate[...] = jnp.zeros_like(dstate)
            dlb_ref[...] = jnp.zeros_like(dlb_ref)

        @pl.when((pl.program_id(0) == 0) & (pl.program_id(1) == 0))
        def _():
            dnw_ref[...] = jnp.zeros_like(dnw_ref)

        lb = _layer_lb(lb_ref, layer)
        q = q_ref[...]
        sgq, qs, sg, f, k = _hgrn_gates(q, f_ref[...], lb)
        v = i_ref[...]
        b = jnp.dot(tri_ref[...], jnp.log(f), precision=_HI, preferred_element_type=_F32)
        eb = jnp.exp(b)
        g = g_ref[...]
        nw = nw_ref[...]
        o = raw_ref[...]
        dout = do_ref[...]
        sgg = _sigmoid(g)
        r = lax.rsqrt(jnp.mean(o * o, axis=-1, keepdims=True) + LN_EPS)
        dg_ref[...] = (dout * (o * r * nw) * (sgg * (1.0 + g * (1.0 - sgg)))).astype(dg_ref.dtype)
        don = dout * (g * sgg)
        dnw_ref[0:1, :] += jnp.sum(don * o * r, axis=0, keepdims=True)
        dy = don * nw
        do_raw = r * dy - o * (r * r * r) * jnp.mean(o * dy, axis=-1, keepdims=True)

        ridx = lax.broadcasted_iota(jnp.int32, (A_CHUNK, HEAD_DIM), 0)
        dqs_t, dk_t, db_t, dv_t = [None] * _HG_CHUNKS, [None] * _HG_CHUNKS, [None] * _HG_CHUNKS, [None] * _HG_CHUNKS
        for c in reversed(range(_HG_CHUNKS)):
            sl = slice(c * A_CHUNK, (c + 1) * A_CHUNK)
            bc, qc, kc, vc, doc = b[sl], qs[sl], k[sl], v[sl], do_raw[sl]
            bl = bc[A_CHUNK - 1:A_CHUNK]
            ebc = eb[sl]
            ebl = jnp.exp(bl - bc)
            lam = jnp.exp(bl)
            qt = qc * ebc
            kt = kc * ebl
            dst = dstate[...]
            stp = st_ref[0, c]
            dob = doc.astype(_MXU_DTYPE)
            dstb = dst.astype(_MXU_DTYPE)
            dqt = jnp.dot(dob, stp.astype(_MXU_DTYPE), preferred_element_type=_F32)
            dkt = jnp.dot(vc.astype(_MXU_DTYPE), dstb, preferred_element_type=_F32)
            dv = lax.dot_general(kt.astype(_MXU_DTYPE), dstb, _NT, preferred_element_type=_F32)
            dlam = jnp.sum(stp * dst, axis=0, keepdims=True)
            dstate[...] = dst * lam + lax.dot_general(dob, qt.astype(_MXU_DTYPE), _TN, preferred_element_type=_F32)
            dqs_rows = []
            dk_in = jnp.zeros((A_CHUNK, HEAD_DIM), _F32)
            for i in range(A_CHUNK):
                di = jnp.exp(jnp.where(ridx <= i, bc[i:i + 1] - bc, _NEG))
                qi = qc[i:i + 1]
                doi = doc[i:i + 1]
                w = kc * di
                a = jnp.sum(qi * w, axis=1, keepdims=True)
                dv = dv + a * doi
                da = jnp.sum(doi * vc, axis=1, keepdims=True)
                dqs_rows.append(jnp.sum(da * w, axis=0, keepdims=True))
                dk_in = dk_in + da * (qi * di)
            dqs_in = jnp.concatenate(dqs_rows, axis=0)
            dbl = jnp.sum(dkt * kt, axis=0, keepdims=True) + dlam * lam
            db = qc * dqs_in - kc * dk_in + dqt * qt - dkt * kt
            db_t[c] = db + jnp.where(ridx == A_CHUNK - 1, dbl, 0.0)
            dqs_t[c] = dqs_in + dqt * ebc
            dk_t[c] = dk_in + dkt * ebl
            dv_t[c] = dv
        dqs = jnp.concatenate(dqs_t, axis=0)
        dk = jnp.concatenate(dk_t, axis=0)
        db = jnp.concatenate(db_t, axis=0)
        di_ref[...] = jnp.concatenate(dv_t, axis=0).astype(di_ref.dtype)
        dlogf = jnp.dot(triu_ref[...], db, precision=_HI, preferred_element_type=_F32)
        df = dlogf / f - dk
        df_ref[...] = (df * (1.0 - lb) * sg * (1.0 - sg)).astype(df_ref.dtype)
        dlb_ref[0, 0:1, :] += jnp.sum(df * (1.0 - sg), axis=0, keepdims=True)
        dq_ref[...] = (dqs * (sgq * (1.0 + q * (1.0 - sgq)))).astype(dq_ref.dtype)

    blk = (_HG_TILE, HEAD_DIM)
    last = _HG_TILES - 1

    def col(base):
        return pl.BlockSpec(blk, lambda h, t: (last - t, base + h))

    tri_spec = pl.BlockSpec(blk, lambda h, t: (0, 0))
    acc_spec = pl.BlockSpec((1, 8, HEAD_DIM), lambda h, t: (h, 0, 0))
    acc_shape = jax.ShapeDtypeStruct((A_HEADS, 8, HEAD_DIM), _F32)
    dq, df, di, dg, dnw, dlb = pl.pallas_call(
        body, grid=(A_HEADS, _HG_TILES),
        in_specs=[col(0), col(4), col(8), col(12), pl.BlockSpec((DEPTH, HEAD_DIM), lambda h, t: (0, h)),
                  pl.BlockSpec((1, HEAD_DIM), lambda h, t: (0, 0)), tri_spec, tri_spec, col(0), col(0),
                  pl.BlockSpec((1, _HG_CHUNKS, HEAD_DIM, HEAD_DIM), lambda h, t: (h, last - t, 0, 0))],
        out_specs=[col(0), col(0), col(0), col(0), pl.BlockSpec((8, HEAD_DIM), lambda h, t: (0, 0)), acc_spec],
        out_shape=[jax.ShapeDtypeStruct((SEQ, A_HEADS * HEAD_DIM), _MXU_DTYPE)] * 4
        + [jax.ShapeDtypeStruct((8, HEAD_DIM), _F32), acc_shape],
        scratch_shapes=[pltpu.VMEM((HEAD_DIM, HEAD_DIM), _F32)], name=name,
        compiler_params=_cparams(dimension_semantics=("arbitrary", "arbitrary")),
    )(proj, proj, proj, proj, lb_logits, norm_w.reshape(1, HEAD_DIM), tri, triu, raw, dmixed, states)
    return dq, df, di, dg, dnw[0], dlb[:, 0, :].reshape(A_HEADS * HEAD_DIM)


def _exchange(arrays, scatter, *, name):
    n = len(arrays)
    n_peer = N_DEV - 1

    def body(*refs):
        ins, outs = refs[:n], refs[n:2 * n]
        send_sems, recv_sems, loc_sems = refs[2 * n:]
        x, y, c = lax.axis_index("x"), lax.axis_index("y"), lax.axis_index("c")
        me = 4 * x + 2 * y + c
        local = []
        for a in range(n):
            cp = pltpu.make_async_copy(ins[a].at[me] if scatter else ins[a], outs[a].at[me], loc_sems.at[a])
            cp.start()
            local.append(cp)

        def peer(k):
            px = jnp.bitwise_xor(x, (k >> 2) & 1)
            py = jnp.bitwise_xor(y, (k >> 1) & 1)
            pc = jnp.bitwise_xor(c, k & 1)
            return (px, py, pc), 4 * px + 2 * py + pc

        def copy(a, k):
            dev, pid = peer(k)
            return pltpu.make_async_remote_copy(
                src_ref=ins[a].at[pid] if scatter else ins[a], dst_ref=outs[a].at[me],
                send_sem=send_sems.at[a * n_peer + k - 1], recv_sem=recv_sems.at[a * n_peer + k - 1],
                device_id=dev, device_id_type=pl.DeviceIdType.MESH)

        def arrival(a, k):
            dev, pid = peer(k)
            return pltpu.make_async_remote_copy(
                src_ref=ins[a].at[pid] if scatter else ins[a], dst_ref=outs[a].at[pid],
                send_sem=send_sems.at[a * n_peer + k - 1], recv_sem=recv_sems.at[a * n_peer + k - 1],
                device_id=dev, device_id_type=pl.DeviceIdType.MESH)

        sends = [copy(a, k) for k in range(1, N_DEV) for a in range(n)]
        for cp in sends:
            cp.start()
        for k in range(1, N_DEV):
            for a in range(n):
                arrival(a, k).wait_recv()
        for cp in sends:
            cp.wait_send()
        for cp in local:
            cp.wait()

    def out_shape(a):
        blk = a.shape[1:] if scatter else a.shape
        return jax.ShapeDtypeStruct((N_DEV,) + tuple(blk), a.dtype)

    any_spec = pl.BlockSpec(memory_space=pl.ANY)
    return pl.pallas_call(
        body, in_specs=[any_spec] * n, out_specs=[any_spec] * n, out_shape=[out_shape(a) for a in arrays],
        scratch_shapes=[pltpu.SemaphoreType.DMA((n * n_peer,)), pltpu.SemaphoreType.DMA((n * n_peer,)),
                        pltpu.SemaphoreType.DMA((n,))],
        name=name, compiler_params=pltpu.CompilerParams(has_side_effects=True),
    )(*arrays)


N_CHIP = N_DEV // 2
_MESH_ID = pl.DeviceIdType.MESH


def _place():
    x, y, c = lax.axis_index("x"), lax.axis_index("y"), lax.axis_index("c")
    chips = [(1 - x, y), (x, 1 - y), (1 - x, 1 - y)]
    return x, y, c, 2 * x + y, chips


def _gather_blocks(arrays, *, name):
    n = len(arrays)

    def body(*refs):
        ins, outs = refs[:n], refs[n:2 * n]
        send_sems, recv_sems, loc_sems = refs[2 * n:]
        x, y, c, _, chips = _place()
        me = 4 * x + 2 * y + c
        sibling = (x, y, 1 - c)

        def slot(px, py, pc):
            return 4 * px + 2 * py + pc

        def copy(a, k, block, to, src=None):
            dst = outs[a].at[slot(*block)]
            return pltpu.make_async_remote_copy(
                src_ref=dst if src is None else src, dst_ref=dst, send_sem=send_sems.at[7 * a + k],
                recv_sem=recv_sems.at[7 * a + k], device_id=to, device_id_type=_MESH_ID)

        local = [pltpu.make_async_copy(ins[a], outs[a].at[me], loc_sems.at[a]) for a in range(n)]
        for cp in local:
            cp.start()
        first = []
        for a in range(n):
            first.append(copy(a, 0, (x, y, c), sibling, src=ins[a]))
            first += [copy(a, 1 + j, (x, y, c), (*chip, c), src=ins[a]) for j, chip in enumerate(chips)]
        for cp in first:
            cp.start()
        passed = []
        for a in range(n):
            for j, chip in enumerate(chips):
                copy(a, 1 + j, (*chip, c), (x, y, c)).wait_recv()
                fwd = copy(a, 4 + j, (*chip, c), sibling)
                fwd.start()
                passed.append(fwd)
        for a in range(n):
            copy(a, 0, sibling, (x, y, c)).wait_recv()
            for j, chip in enumerate(chips):
                copy(a, 4 + j, (*chip, 1 - c), (x, y, c)).wait_recv()
        for cp in first + passed:
            cp.wait_send()
        for cp in local:
            cp.wait()

    any_spec = pl.BlockSpec(memory_space=pl.ANY)
    return pl.pallas_call(
        body, in_specs=[any_spec] * n, out_specs=[any_spec] * n,
        out_shape=[jax.ShapeDtypeStruct((N_DEV,) + a.shape, a.dtype) for a in arrays],
        scratch_shapes=[pltpu.SemaphoreType.DMA((7 * n,)), pltpu.SemaphoreType.DMA((7 * n,)), pltpu.SemaphoreType.DMA((n,))],
        name=name, compiler_params=pltpu.CompilerParams(has_side_effects=True),
    )(*arrays)


def _sibling_swap(arrays, *, name):
    n = len(arrays)

    def body(*refs):
        ins, outs = refs[:n], refs[n:2 * n]
        send_sems, recv_sems = refs[2 * n:]
        x, y, c, _, _ = _place()
        copies = [pltpu.make_async_remote_copy(
            src_ref=ins[a].at[:, 1 - c], dst_ref=outs[a], send_sem=send_sems.at[a], recv_sem=recv_sems.at[a],
            device_id=(x, y, 1 - c), device_id_type=_MESH_ID) for a in range(n)]
        for cp in copies:
            cp.start()
        for cp in copies:
            cp.wait()

    any_spec = pl.BlockSpec(memory_space=pl.ANY)
    return pl.pallas_call(
        body, in_specs=[any_spec] * n, out_specs=[any_spec] * n,
        out_shape=[jax.ShapeDtypeStruct((N_CHIP,) + a.shape[2:], a.dtype) for a in arrays],
        scratch_shapes=[pltpu.SemaphoreType.DMA((n,)), pltpu.SemaphoreType.DMA((n,))],
        name=name, compiler_params=pltpu.CompilerParams(has_side_effects=True),
    )(*arrays)


def _pair_add(mine, theirs, core, *, name):
    _, _, R, C = mine.shape
    tr = max(t for t in range(16, R + 1, 16) if R % t == 0 and t * C <= 512 * 1024)

    def body(core_ref, m_ref, t_ref, o_ref):
        del core_ref
        o_ref[...] = (m_ref[...].astype(_F32) + t_ref[...].astype(_F32)).astype(o_ref.dtype)

    grid_spec = pltpu.PrefetchScalarGridSpec(
        num_scalar_prefetch=1, grid=(N_CHIP, R // tr),
        in_specs=[pl.BlockSpec((None, None, tr, C), lambda q, i, core: (q, core[0], i, 0)),
                  pl.BlockSpec((None, tr, C), lambda q, i, core: (q, i, 0))],
        out_specs=pl.BlockSpec((None, tr, C), lambda q, i, core: (q, i, 0)))
    return pl.pallas_call(
        body, grid_spec=grid_spec, out_shape=jax.ShapeDtypeStruct((N_CHIP, R, C), mine.dtype), name=name,
        compiler_params=_cparams(dimension_semantics=("parallel", "parallel")),
    )(core.reshape(1), mine, theirs)


_HBM = pl.BlockSpec(memory_space=pltpu.HBM)
_SEM = pl.BlockSpec(memory_space=pltpu.SEMAPHORE)
_TOKEN = pl.BlockSpec(memory_space=pltpu.VMEM)
_DATAFLOW = pltpu.SideEffectType.DATAFLOW_SIDE_EFFECTING


def _hbm(a):
    return pltpu.HBM(a.shape, a.dtype)


def _token_shape():
    return jax.ShapeDtypeStruct((8, 128), _F32)


def _dev_slot(px, py, pc):
    return 4 * px + 2 * py + pc


def _gather_start(blocks, landings, *, name):
    n = len(blocks)

    def body(*refs):
        ins, lands = refs[:n], refs[n:2 * n]
        send_sems, d2d_sems, ici_sems = refs[2 * n:2 * n + 3]
        token = refs[-1]
        x, y, c, _, chips = _place()
        for a in range(n):
            dst = lands[a].at[_dev_slot(x, y, c)]
            pltpu.make_async_remote_copy(src_ref=ins[a], dst_ref=dst, send_sem=send_sems.at[4 * a], recv_sem=d2d_sems.at[a],
                                         device_id=(x, y, 1 - c), device_id_type=_MESH_ID).start()
            for j, chip in enumerate(chips):
                pltpu.make_async_remote_copy(src_ref=ins[a], dst_ref=dst, send_sem=send_sems.at[4 * a + 1 + j],
                                             recv_sem=ici_sems.at[3 * a + j], device_id=(*chip, c),
                                             device_id_type=_MESH_ID).start()
        token[...] = jnp.zeros_like(token)

    res = pl.pallas_call(
        body, name=name, in_specs=[_HBM] * (2 * n),
        out_shape=(pltpu.SemaphoreType.DMA((4 * n,)), pltpu.SemaphoreType.DMA((n,)), pltpu.SemaphoreType.DMA((3 * n,)),
                   *[_hbm(b) for b in blocks], *[_hbm(b) for b in landings], _token_shape()),
        out_specs=(_SEM, _SEM, _SEM, *[_HBM] * (2 * n), _TOKEN),
        input_output_aliases={i: 3 + i for i in range(2 * n)},
        compiler_params=pltpu.CompilerParams(has_side_effects=_DATAFLOW),
    )(*[pltpu.with_memory_space_constraint(b, pltpu.HBM) for b in blocks],
      *[pltpu.with_memory_space_constraint(b, pltpu.HBM) for b in landings])
    return res[0], res[1], res[2], list(res[3:3 + n]), list(res[3 + n:3 + 2 * n]), res[-1]


def _gather_forward(landings, ici_sems, first, after, *, name):
    n = len(landings)

    def body(*refs):
        lands = refs[:n]
        ici = refs[n]
        f_send, f_recv = refs[n + 2], refs[n + 3]
        token = refs[-1]
        x, y, c, _, chips = _place()
        for a in range(n):
            for j, chip in enumerate(chips):
                blk = lands[a].at[_dev_slot(*chip, c)]
                pltpu.make_async_remote_copy(src_ref=blk, dst_ref=blk, send_sem=f_send.at[3 * a + j],
                                             recv_sem=ici.at[3 * (first + a) + j], device_id=(*chip, c),
                                             device_id_type=_MESH_ID).wait_recv()
                pltpu.make_async_remote_copy(src_ref=blk, dst_ref=blk, send_sem=f_send.at[3 * a + j], recv_sem=f_recv.at[3 * a + j],
                                             device_id=(x, y, 1 - c), device_id_type=_MESH_ID).start()
        token[...] = jnp.zeros_like(token)

    res = pl.pallas_call(
        body, name=name, in_specs=[_HBM] * n + [_SEM, pl.BlockSpec(memory_space=pl.ANY)],
        out_shape=(pltpu.SemaphoreType.DMA((3 * n,)), pltpu.SemaphoreType.DMA((3 * n,)), *[_hbm(b) for b in landings], _token_shape()),
        out_specs=(_SEM, _SEM, *[_HBM] * n, _TOKEN),
        input_output_aliases={i: 2 + i for i in range(n)},
        compiler_params=pltpu.CompilerParams(has_side_effects=_DATAFLOW),
    )(*landings, ici_sems, after)
    return res[0], res[1], list(res[2:2 + n]), res[-1]


def _gather_wait(blocks, landings, send_sems, d2d_sems, first, f_send, f_recv, after, *, name):
    n = len(landings)

    def body(*refs):
        ins, lands = refs[:n], refs[n:2 * n]
        send, d2d, fs, fr = refs[2 * n:2 * n + 4]
        x, y, c, _, chips = _place()
        me = (x, y, c)
        for a in range(n):
            own = lands[a].at[_dev_slot(x, y, 1 - c)]
            g = first + a
            pltpu.make_async_remote_copy(src_ref=ins[a], dst_ref=own, send_sem=send.at[4 * g], recv_sem=d2d.at[g],
                                         device_id=me, device_id_type=_MESH_ID).wait_recv()
            for j, chip in enumerate(chips):
                blk = lands[a].at[_dev_slot(*chip, 1 - c)]
                pltpu.make_async_remote_copy(src_ref=blk, dst_ref=blk, send_sem=fs.at[3 * a + j], recv_sem=fr.at[3 * a + j],
                                             device_id=me, device_id_type=_MESH_ID).wait_recv()
            for k in range(4):
                pltpu.make_async_remote_copy(src_ref=ins[a], dst_ref=own, send_sem=send.at[4 * g + k], recv_sem=d2d.at[g],
                                             device_id=me, device_id_type=_MESH_ID).wait_send()
            for j in range(3):
                pltpu.make_async_remote_copy(src_ref=own, dst_ref=own, send_sem=fs.at[3 * a + j], recv_sem=fr.at[3 * a + j],
                                             device_id=me, device_id_type=_MESH_ID).wait_send()

    res = pl.pallas_call(
        body, name=name, in_specs=[_HBM] * (2 * n) + [_SEM] * 4 + [pl.BlockSpec(memory_space=pl.ANY)],
        out_shape=(*[_hbm(b) for b in blocks], *[_hbm(b) for b in landings]), out_specs=tuple([_HBM] * (2 * n)),
        input_output_aliases={i: i for i in range(2 * n)},
        compiler_params=pltpu.CompilerParams(has_side_effects=_DATAFLOW),
    )(*blocks, *landings, send_sems, d2d_sems, f_send, f_recv, after)
    return list(res[n:])


def _chip_exchange_start(sums, landings, *, name):
    n = len(sums)

    def body(*refs):
        ins, lands = refs[:n], refs[n:2 * n]
        send_sems, recv_sems = refs[2 * n:2 * n + 2]
        token = refs[-1]
        _, _, c, p, chips = _place()
        for a in range(n):
            for j, (qx, qy) in enumerate(chips):
                pltpu.make_async_remote_copy(src_ref=ins[a].at[2 * qx + qy], dst_ref=lands[a].at[p], send_sem=send_sems.at[3 * a + j],
                                             recv_sem=recv_sems.at[3 * a + j], device_id=(qx, qy, c), device_id_type=_MESH_ID).start()
        token[...] = jnp.zeros_like(token)

    res = pl.pallas_call(
        body, name=name, in_specs=[_HBM] * (2 * n),
        out_shape=(pltpu.SemaphoreType.DMA((3 * n,)), pltpu.SemaphoreType.DMA((3 * n,)),
                   *[_hbm(b) for b in sums], *[_hbm(b) for b in landings], _token_shape()),
        out_specs=(_SEM, _SEM, *[_HBM] * (2 * n), _TOKEN),
        input_output_aliases={i: 2 + i for i in range(2 * n)},
        compiler_params=pltpu.CompilerParams(has_side_effects=_DATAFLOW),
    )(*[pltpu.with_memory_space_constraint(b, pltpu.HBM) for b in sums],
      *[pltpu.with_memory_space_constraint(b, pltpu.HBM) for b in landings])
    return res[0], res[1], list(res[2:2 + n]), list(res[2 + n:2 + 2 * n]), res[-1]


def _chip_exchange_wait(sums, landings, send_sems, recv_sems, after, *, name):
    n = len(sums)

    def body(*refs):
        ins, lands = refs[:n], refs[n:2 * n]
        send, recv = refs[2 * n:2 * n + 2]
        x, y, c, _, chips = _place()
        for a in range(n):
            for j, (qx, qy) in enumerate(chips):
                q = 2 * qx + qy
                cp = pltpu.make_async_remote_copy(src_ref=ins[a].at[q], dst_ref=lands[a].at[q], send_sem=send.at[3 * a + j],
                                                  recv_sem=recv.at[3 * a + j], device_id=(x, y, c), device_id_type=_MESH_ID)
                cp.wait_recv()
                cp.wait_send()

    res = pl.pallas_call(
        body, name=name, in_specs=[_HBM] * (2 * n) + [_SEM] * 2 + [pl.BlockSpec(memory_space=pl.ANY)],
        out_shape=(*[_hbm(b) for b in sums], *[_hbm(b) for b in landings]), out_specs=tuple([_HBM] * (2 * n)),
        input_output_aliases={i: i for i in range(2 * n)},
        compiler_params=pltpu.CompilerParams(has_side_effects=_DATAFLOW),
    )(*sums, *landings, send_sems, recv_sems, after)
    return list(res[:n]), list(res[n:])


_C1 = 1.0 - ADAM_B1 ** ADAM_STEP
_C2 = 1.0 - ADAM_B2 ** ADAM_STEP


def _adamw_math(g, w, m, v):
    m = ADAM_B1 * m + (1.0 - ADAM_B1) * g
    v = ADAM_B2 * v + (1.0 - ADAM_B2) * (g * g)
    delta = -ADAM_LR * ((m / _C1) / (jnp.sqrt(v / _C2) + ADAM_EPS) + ADAM_WD * w)
    return delta, m, v


def _adamw_reduce(landed, sums, chip, w, m, v, layer, prev, *, name):
    _, R, C = w.shape
    tr = max(t for t in range(16, R + 1, 16) if R % t == 0 and t * C <= 256 * 1024)

    def body(chip_ref, p_ref, own_ref, w_ref, m_ref, v_ref, *rest):
        g_ref, d_ref, nm_ref, nv_ref = rest[-4:]
        own = own_ref[...].astype(_F32)
        g = jnp.where(chip_ref[0] == 0, own, p_ref[0].astype(_F32))
        for q in range(1, N_CHIP):
            g = g + jnp.where(chip_ref[0] == q, own, p_ref[q].astype(_F32))
        d, nm, nv = _adamw_math(g, w_ref[...], m_ref[...], v_ref[...])
        g_ref[...] = g
        d_ref[...] = d
        nm_ref[...] = nm
        nv_ref[...] = nv

    blk = pl.BlockSpec((None, tr, C), lambda i, chip: (layer, i, 0))
    shape = jax.ShapeDtypeStruct((DEPTH, R, C), _F32)
    kept = [] if prev is None else list(prev)
    grid_spec = pltpu.PrefetchScalarGridSpec(
        num_scalar_prefetch=1, grid=(R // tr,),
        in_specs=[pl.BlockSpec((N_CHIP, tr, C), lambda i, chip: (0, i, 0)),
                  pl.BlockSpec((None, tr, C), lambda i, chip: (chip[0], i, 0)), blk, blk, blk]
        + [pl.BlockSpec(memory_space=pl.ANY)] * len(kept),
        out_specs=[blk] * 4)
    return pl.pallas_call(
        body, grid_spec=grid_spec, out_shape=[shape] * 4, name=name,
        input_output_aliases={6 + k: k for k in range(len(kept))},
        compiler_params=_cparams(dimension_semantics=("parallel",)),
    )(chip.reshape(1), landed, sums, w, m, v, *kept)


_PACK_LANES = 128
_LAYER_ROWS = 248
_LB_ROWS = (A_HEADS * HEAD_DIM) // _PACK_LANES


def _small_reduce(parts, lb_logits, *, name):
    rows = DEPTH * _LAYER_ROWS

    def body(p_ref, lg_ref, o_ref):
        g = p_ref[0]
        for s in range(1, N_DEV):
            g = g + p_ref[s]
        o_ref[...] = g
        lg = lg_ref[...]
        e = jnp.exp(lg - jnp.max(lg, axis=0, keepdims=True))
        p = e / jnp.sum(e, axis=0, keepdims=True)
        d1 = g[_LAYER_ROWS:_LAYER_ROWS + _LB_ROWS, :] * p[0] * p[1]
        o_ref[0:_LB_ROWS, :] = -d1
        o_ref[_LAYER_ROWS:_LAYER_ROWS + _LB_ROWS, :] = d1

    return pl.pallas_call(
        body, out_shape=jax.ShapeDtypeStruct((rows, _PACK_LANES), _F32), name=name,
        compiler_params=_cparams(),
    )(parts, lb_logits.reshape(DEPTH, _LB_ROWS, _PACK_LANES))


def _adamw_small(g, w, m, v, *, name):
    def body(g_ref, w_ref, m_ref, v_ref, d_ref, nm_ref, nv_ref):
        d, nm, nv = _adamw_math(g_ref[...], w_ref[...], m_ref[...], v_ref[...])
        d_ref[...] = d
        nm_ref[...] = nm
        nv_ref[...] = nv

    shape = jax.ShapeDtypeStruct(g.shape, _F32)
    return pl.pallas_call(body, out_shape=[shape] * 3, name=name, compiler_params=_cparams())(g, w, m, v)


def _pack(vectors, rows):
    flat = jnp.concatenate([v.reshape(-1).astype(_F32) for v in vectors])
    return jnp.pad(flat, (0, rows * _PACK_LANES - flat.shape[0])).reshape(rows, _PACK_LANES)


def _unpack(packed, shapes):
    flat = packed.reshape(-1)
    out, at = [], 0
    for s in shapes:
        size = int(np.prod(s))
        out.append(flat[at:at + size].reshape(s))
        at += size
    return out


_BIG = ("w_in", "w_gate", "w_up", "w_out", "w_down")
_COLUMN_SHARDED = ("w_in", "w_gate", "w_up")


def _full_weight(name, g):
    if name == "w_out":
        return g.reshape(D_MODEL, D_MODEL)
    if name == "w_down":
        return g.reshape(D_FF, D_MODEL)
    if name == "conv_w":
        return g.transpose(1, 0, 2).reshape(g.shape[1], N_DEV * SHARD_COLS)
    return g


class _WeightGather:
    def __init__(self, names, first, blocks, lands, sems, tag):
        self.names, self.first, self.blocks, self.lands, self.sems, self.tag = names, first, blocks, lands, sems, tag
        self.forwarded = None

    def forward(self, after):
        f_send, f_recv, self.lands, token = _gather_forward(self.lands, self.sems[2], self.first, after,
                                                            name=f"gather_forward_{self.tag}")
        self.forwarded = (f_send, f_recv)
        return token

    def wait(self, after):
        if self.forwarded is None:
            self.forward(after)
        got = _gather_wait(self.blocks, self.lands, self.sems[0], self.sems[1], self.first, *self.forwarded, after,
                           name=f"gather_wait_{self.tag}")
        return {n: _full_weight(n, g) for n, g in zip(self.names, got)}


def _start_gathers(groups, me):
    blocks = [b for _, _, bs in groups for b in bs]
    landings = [lax.dynamic_update_index_in_dim(lax.empty((N_DEV,) + b.shape, b.dtype), b[None], me, 0) for b in blocks]
    send, d2d, ici, blocks, landings, token = _gather_start(blocks, landings, name="gather_start")
    out, first = [], 0
    for tag, names, bs in groups:
        k = len(bs)
        out.append(_WeightGather(names, first, blocks[first:first + k], landings[first:first + k], (send, d2d, ici), tag))
        first += k
    return out, token


class _LayerWeights:
    def __init__(self, ready, pending=(), forwards=(), tokens=()):
        self.ready, self.pending, self.forwards, self._tokens = dict(ready), list(pending), list(forwards), list(tokens)

    def at(self, point, after):
        for when, gather in self.forwards:
            if when == point:
                self._tokens.append(gather.forward(after))

    def tokens(self):
        out, self._tokens = self._tokens, []
        return out

    def get(self, name, after):
        if name not in self.ready:
            group, = [g for g in self.pending if name in g.names]
            self.ready.update(group.wait(after))
        return self.ready[name]


def _layer_fwd(x, xb, ws, lb_logits, a_norm_w, c_sink, ln1_g, ln1_b, conv_b, ln2_g, ln2_b, tabs, l):
    proj = _mm_w_slabs(xb, ws.get("w_in", xb), tm=1024, after=ws.tokens(), name=f"proj_{l}")
    o_a, raw, states = _hgrn_fwd(proj, lb_logits, a_norm_w, l, name=f"hgrn_fwd_{l}")
    ws.at("hgrn", o_a)
    o_b, lse_b = _band_fwd(proj, tabs, name=f"dilated_fwd_{l}", **_DILATED)
    o_c, lse_c = _band_fwd(proj, tabs, sink=c_sink, name=f"swa_fwd_{l}", **_SWA)
    ws.at("swa", o_c)
    mixed = jnp.concatenate([o_a, o_b, o_c], axis=1).astype(_MXU_DTYPE)
    y = _mm(mixed, ws.get("w_out", mixed), tm=1024, tn=512, after=ws.tokens(), name=f"mix_out_{l}")
    z1, x1, x1b = _ln_fwd(x, y, ln1_g, ln1_b, name=f"ln1_fwd_{l}")
    g = _mm_w_slabs(x1b, ws.get("w_gate", x1b), tm=1024, name=f"ffn_gate_{l}")
    u = _mm_w_slabs(x1b, ws.get("w_up", x1b), tm=1024, name=f"ffn_up_{l}")
    ws.at("up", u)
    hb = _conv_gate_fwd(g, u, ws.get("conv_w", u), conv_b, name=f"conv_gate_fwd_{l}")
    y2 = _mm(hb, ws.get("w_down", hb), tm=512, tn=512, after=ws.tokens(), name=f"ffn_down_{l}")
    ws.at("down", y2)
    z2, x2, x2b = _ln_fwd(x1, y2, ln2_g, ln2_b, name=f"ln2_fwd_{l}")
    res = dict(xb=xb, proj=proj, raw=raw, states=states, o_b=o_b, lse_b=lse_b, o_c=o_c, lse_c=lse_c,
               mixed=mixed, z1=z1, x1b=x1b, g=g, u=u, hb=hb, z2=z2)
    return x2, x2b, res


class _GradExchange:
    def __init__(self, core, chip):
        self.core, self.chip, self.groups, self._tokens = core, chip, [], []

    def launch(self, names, slabs, l, tag):
        mine = [s.reshape((N_CHIP, 2) + s.shape[1:]) for s in slabs]
        theirs = _sibling_swap(mine, name=f"swap_grads_{tag}")
        sums = [_pair_add(a, b, self.core, name=f"pair_add_{n}_{l}") for n, a, b in zip(names, mine, theirs)]
        landings = [lax.empty(s.shape, s.dtype) for s in sums]
        send, recv, sums, landings, token = _chip_exchange_start(sums, landings, name=f"exchange_start_{tag}")
        self.groups.append((names, l, tag, send, recv, sums, landings))
        self._tokens.append(token)

    def tokens(self):
        out, self._tokens = self._tokens, []
        return out

    def finish(self, weights, mom1, mom2, after):
        out = {}
        after = list(after) + self.tokens()
        for names, l, tag, send, recv, sums, landings in self.groups:
            sums, landings = _chip_exchange_wait(sums, landings, send, recv, after[-1], name=f"exchange_wait_{tag}")
            for n, s, landed in zip(names, sums, landings):
                out[n] = _adamw_reduce(landed, s, self.chip, weights[n], mom1[n], mom2[n], l, out.get(n), name=f"adamw_{n}_{l}")
                after = [out[n][0]]
        return out


def _layer_bwd(dx2, res, w, lb_logits, a_norm_w, c_sink, ln1_g, conv_b, ln2_g, tabs, exchange, l):
    dz2, dz2b, d_ln2_g, d_ln2_b = _ln_bwd(res["z2"], dx2, None, ln2_g, name=f"ln2_bwd_{l}")
    dh = _mm(dz2b, w["w_down"], tb=True, tm=1024, tn=512, after=exchange.tokens(), name=f"ffn_down_dx_{l}")
    d_w_down = _mm(res["hb"], dz2b, ta=True, tm=512, tn=512, out_dtype=_GRAD_DTYPE, name=f"ffn_down_dw_{l}")
    dg, du, d_conv_w, d_conv_b = _conv_gate_bwd(dh, res["g"], res["u"], w["conv_w"], conv_b, name=f"conv_gate_bwd_{l}")
    t = _mm_nt_w_slabs(dg, w["w_gate"], tm=512, tn=512, name=f"ffn_gate_dx_{l}")
    dx1 = _mm_nt_w_slabs(du, w["w_up"], tm=512, tn=512, add=t, name=f"ffn_up_dx_{l}")
    d_w_gate = _mm_tn_slabs(res["x1b"], dg, tm=1024, name=f"ffn_gate_dw_{l}")
    d_w_up = _mm_tn_slabs(res["x1b"], du, tm=1024, name=f"ffn_up_dw_{l}")
    dz1, dz1b, d_ln1_g, d_ln1_b = _ln_bwd(res["z1"], dx1, dz2, ln1_g, name=f"ln1_bwd_{l}")
    d_w_out = _mm(res["mixed"], dz1b, ta=True, tm=1024, tn=512, out_dtype=_GRAD_DTYPE, name=f"mix_out_dw_{l}")
    exchange.launch(("w_down", "w_gate", "w_up", "w_out"),
                    [d_w_down.reshape(N_DEV, D_FF // N_DEV, D_MODEL), d_w_gate, d_w_up,
                     d_w_out.reshape(N_DEV, D_MODEL // N_DEV, D_MODEL)], l, f"ffn_{l}")
    dmixed = _mm(dz1b, w["w_out"], tb=True, tm=1024, tn=512, after=exchange.tokens(), name=f"mix_out_dx_{l}")
    dq_a, df_a, di_a, dg_a, d_norm_w, d_lb = _hgrn_bwd(res["proj"], lb_logits, a_norm_w, res["raw"], res["states"],
                                                      dmixed, l, name=f"hgrn_bwd_{l}")
    dq_b, dk_b, dv_b = _band_bwd(res["proj"], tabs, dmixed, res["o_b"], res["lse_b"], do0=A_HEADS,
                                 name=f"dilated_bwd_{l}", **_DILATED)
    dq_c, dk_c, dv_c, d_sink = _band_bwd(res["proj"], tabs, dmixed, res["o_c"], res["lse_c"], do0=A_HEADS + B_HEADS,
                                         sink=c_sink, name=f"swa_bwd_{l}", **_SWA)
    dproj = jnp.concatenate([dq_a, df_a, di_a, dg_a, dq_b, dk_b, dv_b, dq_c, dk_c, dv_c], axis=1)
    d_w_in = _mm_tn_slabs(res["xb"], dproj, tm=1024, name=f"proj_dw_{l}")
    exchange.launch(("w_in",), [d_w_in], l, f"mix_{l}")
    dx = _mm_nt_w_slabs(dproj, w["w_in"], tm=512, tn=512, add=dz1, add_scale=ALPHA, after=exchange.tokens(),
                        name=f"proj_dx_{l}")
    small = [d_lb, d_norm_w, jnp.pad(d_sink, (0, _PACK_LANES - C_HEADS)), d_ln1_g, d_ln1_b, d_ln2_g, d_ln2_b, d_conv_b,
             d_conv_w]
    return dx, small


def kernel(x, w_in, lb_logits, a_norm_w, c_sinks, w_out, ln1_g, ln1_b, w_gate, w_up, conv_w, conv_b, w_down, ln2_g, ln2_b, loss_target, m_w_in, m_lb_logits, m_a_norm_w, m_c_sinks, m_w_out, m_ln1_g, m_ln1_b, m_w_gate, m_w_up, m_conv_w, m_conv_b, m_w_down, m_ln2_g, m_ln2_b, v_w_in, v_lb_logits, v_a_norm_w, v_c_sinks, v_w_out, v_ln1_g, v_ln1_b, v_w_gate, v_w_up, v_conv_w, v_conv_b, v_w_down, v_ln2_g, v_ln2_b):
    weights = dict(w_in=w_in, lb_logits=lb_logits, a_norm_w=a_norm_w, c_sinks=c_sinks, w_out=w_out, ln1_g=ln1_g, ln1_b=ln1_b,
                   w_gate=w_gate, w_up=w_up, conv_w=conv_w, conv_b=conv_b, w_down=w_down, ln2_g=ln2_g, ln2_b=ln2_b)
    mom1 = dict(w_in=m_w_in, lb_logits=m_lb_logits, a_norm_w=m_a_norm_w, c_sinks=m_c_sinks, w_out=m_w_out, ln1_g=m_ln1_g,
                ln1_b=m_ln1_b, w_gate=m_w_gate, w_up=m_w_up, conv_w=m_conv_w, conv_b=m_conv_b, w_down=m_w_down, ln2_g=m_ln2_g,
                ln2_b=m_ln2_b)
    mom2 = dict(w_in=v_w_in, lb_logits=v_lb_logits, a_norm_w=v_a_norm_w, c_sinks=v_c_sinks, w_out=v_w_out, ln1_g=v_ln1_g,
                ln1_b=v_ln1_b, w_gate=v_w_gate, w_up=v_w_up, conv_w=v_conv_w, conv_b=v_conv_b, w_down=v_w_down, ln2_g=v_ln2_g,
                ln2_b=v_ln2_b)
    core = lax.axis_index("c").astype(jnp.int32)
    me = 4 * lax.axis_index("x") + 2 * lax.axis_index("y") + core
    tabs = _rope_tables()

    chip = (2 * lax.axis_index("x") + lax.axis_index("y")).astype(jnp.int32)

    def block(n, l):
        return conv_w[l] if n == "conv_w" else weights[n][l].astype(_MXU_DTYPE)

    first, = _gather_blocks([block("w_in", 0)], name="gather_w_in_0")
    order = [(("w_out",), 0), (("w_gate", "w_up", "conv_w"), 0), (("w_down",), 0),
             (("w_in",), 1), (("w_out",), 1), (("w_gate", "w_up", "conv_w"), 1), (("w_down",), 1)]
    gathers, started = _start_gathers([(f"{names[0]}_{l}", names, [block(n, l) for n in names]) for names, l in order], me)
    out0, ffn0, down0, in1, out1, ffn1, down1 = gathers
    layer_ws = [_LayerWeights({"w_in": _full_weight("w_in", first)}, [out0, ffn0, down0],
                              [("hgrn", out0), ("swa", ffn0), ("up", down0), ("down", in1)], [started]),
                _LayerWeights({}, [in1, out1, ffn1, down1], [("hgrn", out1), ("swa", ffn1), ("up", down1)])]

    xs = x[0]
    xb = xs.astype(_MXU_DTYPE)
    saved = []
    for l in range(DEPTH):
        xs, xb, res = _layer_fwd(xs, xb, layer_ws[l], lb_logits, a_norm_w[l], c_sinks[l], ln1_g[l], ln1_b[l], conv_b[l],
                                 ln2_g[l], ln2_b[l], tabs, l)
        saved.append(res)
    loss_part, dx = _loss_head(xs, loss_target[0], name="loss_head")
    loss = lax.psum(loss_part, ("x", "y", "c"))

    exchange = _GradExchange(core, chip)
    small_parts = [None] * DEPTH
    for l in reversed(range(DEPTH)):
        dx, small = _layer_bwd(dx, saved[l], layer_ws[l].ready, lb_logits, a_norm_w[l], c_sinks[l], ln1_g[l], conv_b[l],
                               ln2_g[l], tabs, exchange, l)
        small_parts[l] = _pack(small, _LAYER_ROWS)
    def as_slabs(d):
        return {n: jnp.swapaxes(d[n], 1, 2) if n in _COLUMN_SHARDED else d[n] for n in _BIG}

    updated = exchange.finish(as_slabs(weights), as_slabs(mom1), as_slabs(mom2), [dx])
    updated = {n: tuple(jnp.swapaxes(t, 1, 2) for t in u) if n in _COLUMN_SHARDED else u for n, u in updated.items()}
    gathered, = _exchange([jnp.concatenate(small_parts, axis=0)], False, name="gather_small_grads")
    g_small = _small_reduce(gathered, lb_logits, name="small_grads")

    per_layer = [(A_HEADS * HEAD_DIM,), (HEAD_DIM,), (_PACK_LANES,), (D_MODEL,), (D_MODEL,), (D_MODEL,), (D_MODEL,), (D_FF,),
                 (3, D_FF)]
    names = ("lb_logits", "a_norm_w", "c_sinks", "ln1_g", "ln1_b", "ln2_g", "ln2_b", "conv_b", "conv_w")
    grads = {n: [] for n in names}
    for l in range(DEPTH):
        for n, t in zip(names, _unpack(g_small[l * _LAYER_ROWS:(l + 1) * _LAYER_ROWS], per_layer)):
            grads[n].append(t)
    grads = {n: jnp.stack(t) for n, t in grads.items()}
    grads["c_sinks"] = grads["c_sinks"][:, :C_HEADS]
    grads["conv_w"] = lax.dynamic_slice_in_dim(grads["conv_w"], me * SHARD_COLS, SHARD_COLS, axis=2)
    shapes = [grads[n].shape for n in names]
    rows = -(-sum(int(np.prod(s)) for s in shapes) // (8 * _PACK_LANES)) * 8
    d_s, m_s, v_s = _adamw_small(_pack([grads[n] for n in names], rows), _pack([weights[n] for n in names], rows),
                                 _pack([mom1[n] for n in names], rows), _pack([mom2[n] for n in names], rows),
                                 name="adamw_small")
    delta = dict(zip(names, _unpack(d_s, shapes)))
    new_m = dict(zip(names, _unpack(m_s, shapes)))
    new_v = dict(zip(names, _unpack(v_s, shapes)))
    for n in _BIG:
        grads[n], delta[n], new_m[n], new_v[n] = updated[n]

    order = ("w_in", "lb_logits", "a_norm_w", "c_sinks", "w_out", "ln1_g", "ln1_b", "w_gate", "w_up", "conv_w", "conv_b",
             "w_down", "ln2_g", "ln2_b")
    return (loss, dx[None], *[grads[n] for n in order], *[delta[n] for n in order], *[new_m[n] for n in order],
            *[new_v[n] for n in order])
```

```python
import functools

import jax
import jax.numpy as jnp
import numpy as np
from jax import lax
from jax.experimental import pallas as pl
from jax.experimental.pallas import tpu as pltpu

D_MODEL = 2048
SEQ = 2048
DEPTH = 2
HEAD_DIM = 128
A_HEADS = 4
B_HEADS = 6
C_HEADS = 6
C_KV_HEADS = 2
A_CHUNK = 16
DILATIONS = (1, 4, 16)
BLOCK = 128
ROPE_THETA = 500000.0
ROPE_DIM = 32
D_FF = 5632
IN_WIDTH = 5632
LN_EPS = 1e-5
ALPHA = (2 * DEPTH) ** 0.25
N_DEV = 8
SHARD_COLS = IN_WIDTH // N_DEV

ADAM_LR = 0.001
ADAM_B1 = 0.9
ADAM_B2 = 0.999
ADAM_EPS = 1e-08
ADAM_WD = 0.01
ADAM_STEP = 10

A_COLS = 16
QKV_COLS = 28
QB0, KB0, VB0, QC0, KC0, VC0 = 0, 6, 12, 18, 24, 26

_MXU_DTYPE = jnp.bfloat16
_GRAD_DTYPE = jnp.bfloat16
_ACT_DTYPE = jnp.bfloat16
_NEG = -1e30
_VMEM_LIMIT = 56 * 2 ** 20

_F32 = jnp.float32


def _sigmoid(x):
    return 1.0 / (1.0 + jnp.exp(-x))


def _cparams(**kw):
    return pltpu.CompilerParams(vmem_limit_bytes=_VMEM_LIMIT, **kw)


def _mm(a, b, *, ta=False, tb=False, tm, tn, out_dtype=_F32, add=None, add_scale=1.0, after=(), name):
    K = a.shape[0] if ta else a.shape[1]
    M = a.shape[1] if ta else a.shape[0]
    N = b.shape[0] if tb else b.shape[1]
    assert (b.shape[1] if tb else b.shape[0]) == K and M % tm == 0 and N % tn == 0
    dn = (((0 if ta else 1,), (1 if tb else 0,)), ((), ()))

    def body(*refs):
        a_ref, b_ref = refs[:2]
        o_ref = refs[-1]
        r = lax.dot_general(a_ref[...], b_ref[...], dn, preferred_element_type=_F32)
        if add is not None:
            r = r + add_scale * refs[2][...]
        o_ref[...] = r.astype(o_ref.dtype)

    a_spec = pl.BlockSpec((K, tm), lambda i, j: (0, i)) if ta else pl.BlockSpec((tm, K), lambda i, j: (i, 0))
    b_spec = pl.BlockSpec((tn, K), lambda i, j: (j, 0)) if tb else pl.BlockSpec((K, tn), lambda i, j: (0, j))
    o_spec = pl.BlockSpec((tm, tn), lambda i, j: (i, j))
    in_specs = [a_spec, b_spec] + ([o_spec] if add is not None else []) + [pl.BlockSpec(memory_space=pl.ANY)] * len(after)
    args = (a, b) + ((add,) if add is not None else ()) + tuple(after)
    return pl.pallas_call(
        body, grid=(M // tm, N // tn), in_specs=in_specs, out_specs=o_spec,
        out_shape=jax.ShapeDtypeStruct((M, N), out_dtype), name=name,
        compiler_params=_cparams(dimension_semantics=("parallel", "parallel")),
    )(*args)


_PAIR = 2 * SHARD_COLS


def _mm_tn_slabs(a, b, *, tm, name):
    K, M = a.shape
    assert b.shape == (K, N_DEV * SHARD_COLS) and M % tm == 0

    def body(a_ref, b_ref, o_ref):
        a_blk = a_ref[...]
        for s in range(2):
            o_ref[s] = lax.dot_general(b_ref[:, s * SHARD_COLS:(s + 1) * SHARD_COLS], a_blk, _TN,
                                       preferred_element_type=_F32).astype(o_ref.dtype)

    return pl.pallas_call(
        body, grid=(M // tm, N_DEV // 2),
        in_specs=[pl.BlockSpec((K, tm), lambda i, p: (0, i)), pl.BlockSpec((K, _PAIR), lambda i, p: (0, p))],
        out_specs=pl.BlockSpec((2, SHARD_COLS, tm), lambda i, p: (p, 0, i)),
        out_shape=jax.ShapeDtypeStruct((N_DEV, SHARD_COLS, M), _GRAD_DTYPE), name=name,
        compiler_params=_cparams(dimension_semantics=("parallel", "parallel")),
    )(a, b)


def _mm_w_slabs(a, w, *, tm, out_dtype=_F32, after=(), name):
    M, K = a.shape
    assert w.shape == (N_DEV, K, SHARD_COLS) and M % tm == 0

    def body(a_ref, w_ref, *rest):
        o_ref = rest[-1]
        a_blk = a_ref[...]
        for s in range(2):
            o_ref[:, s * SHARD_COLS:(s + 1) * SHARD_COLS] = jnp.dot(
                a_blk, w_ref[s], preferred_element_type=_F32).astype(o_ref.dtype)

    return pl.pallas_call(
        body, grid=(M // tm, N_DEV // 2),
        in_specs=[pl.BlockSpec((tm, K), lambda i, p: (i, 0)), pl.BlockSpec((2, K, SHARD_COLS), lambda i, p: (p, 0, 0))]
        + [pl.BlockSpec(memory_space=pl.ANY)] * len(after),
        out_specs=pl.BlockSpec((tm, _PAIR), lambda i, p: (i, p)),
        out_shape=jax.ShapeDtypeStruct((M, N_DEV * SHARD_COLS), out_dtype), name=name,
        compiler_params=_cparams(dimension_semantics=("parallel", "parallel")),
    )(a, w, *after)


def _mm_nt_w_slabs(a, w, *, tm, tn, add=None, add_scale=1.0, after=(), name):
    M = a.shape[0]
    N = w.shape[1]
    assert a.shape[1] == N_DEV * SHARD_COLS and w.shape[0] == N_DEV and M % tm == 0 and N % tn == 0

    def body(a_ref, w_ref, *rest):
        o_ref = rest[-1]
        acc = add_scale * rest[0][...] if add is not None else None
        for j in range(N_DEV):
            t = lax.dot_general(a_ref[:, j * SHARD_COLS:(j + 1) * SHARD_COLS], w_ref[j], _NT, preferred_element_type=_F32)
            acc = t if acc is None else acc + t
        o_ref[...] = acc

    o_spec = pl.BlockSpec((tm, tn), lambda i, j: (i, j))
    return pl.pallas_call(
        body, grid=(M // tm, N // tn),
        in_specs=[pl.BlockSpec((tm, N_DEV * SHARD_COLS), lambda i, j: (i, 0)),
                  pl.BlockSpec((N_DEV, tn, SHARD_COLS), lambda i, j: (0, j, 0))]
        + ([o_spec] if add is not None else []) + [pl.BlockSpec(memory_space=pl.ANY)] * len(after),
        out_specs=o_spec, out_shape=jax.ShapeDtypeStruct((M, N), _F32), name=name,
        compiler_params=_cparams(dimension_semantics=("parallel", "parallel")),
    )(a, w, *((add,) if add is not None else ()), *after)


def _ln_fwd(x, y, g, b, *, name):
    tm = 256

    def body(x_ref, y_ref, g_ref, b_ref, z_ref, o_ref, ob_ref):
        z = ALPHA * x_ref[...] + y_ref[...]
        mu = jnp.mean(z, axis=-1, keepdims=True)
        zc = z - mu
        var = jnp.mean(zc * zc, axis=-1, keepdims=True)
        o = zc * lax.rsqrt(var + LN_EPS) * g_ref[...] + b_ref[...]
        z_ref[...] = z
        o_ref[...] = o
        ob_ref[...] = o.astype(ob_ref.dtype)

    row = pl.BlockSpec((tm, D_MODEL), lambda i: (i, 0))
    vec = pl.BlockSpec((1, D_MODEL), lambda i: (0, 0))
    return pl.pallas_call(
        body, grid=(SEQ // tm,), in_specs=[row, row, vec, vec], out_specs=[row, row, row],
        out_shape=[jax.ShapeDtypeStruct((SEQ, D_MODEL), _F32), jax.ShapeDtypeStruct((SEQ, D_MODEL), _F32),
                   jax.ShapeDtypeStruct((SEQ, D_MODEL), _MXU_DTYPE)],
        name=name, compiler_params=_cparams(dimension_semantics=("parallel",)),
    )(x, y, g.reshape(1, D_MODEL), b.reshape(1, D_MODEL))


def _ln_bwd(z, d_a, d_res, g, *, name):
    tm = 256

    def body(*refs):
        if d_res is None:
            z_ref, da_ref, g_ref, dz_ref, dzb_ref, dg_ref, db_ref = refs
            dout = da_ref[...]
        else:
            z_ref, da_ref, dr_ref, g_ref, dz_ref, dzb_ref, dg_ref, db_ref = refs
            dout = da_ref[...] + ALPHA * dr_ref[...]
        z = z_ref[...]
        mu = jnp.mean(z, axis=-1, keepdims=True)
        zc = z - mu
        var = jnp.mean(zc * zc, axis=-1, keepdims=True)
        rstd = lax.rsqrt(var + LN_EPS)
        xh = zc * rstd
        dxh = dout * g_ref[...]
        m1 = jnp.mean(dxh, axis=-1, keepdims=True)
        m2 = jnp.mean(dxh * xh, axis=-1, keepdims=True)
        dz = rstd * (dxh - m1 - xh * m2)
        dz_ref[...] = dz
        dzb_ref[...] = dz.astype(dzb_ref.dtype)

        @pl.when(pl.program_id(0) == 0)
        def _():
            dg_ref[...] = jnp.zeros_like(dg_ref)
            db_ref[...] = jnp.zeros_like(db_ref)

        dg_ref[0:1, :] += jnp.sum(dout * xh, axis=0, keepdims=True)
        db_ref[0:1, :] += jnp.sum(dout, axis=0, keepdims=True)

    row = pl.BlockSpec((tm, D_MODEL), lambda i: (i, 0))
    vec = pl.BlockSpec((1, D_MODEL), lambda i: (0, 0))
    acc = pl.BlockSpec((8, D_MODEL), lambda i: (0, 0))
    ins = [z, d_a] + ([d_res] if d_res is not None else []) + [g.reshape(1, D_MODEL)]
    in_specs = [row, row] + ([row] if d_res is not None else []) + [vec]
    dz, dzb, dg, db = pl.pallas_call(
        body, grid=(SEQ // tm,), in_specs=in_specs, out_specs=[row, row, acc, acc],
        out_shape=[jax.ShapeDtypeStruct((SEQ, D_MODEL), _F32), jax.ShapeDtypeStruct((SEQ, D_MODEL), _MXU_DTYPE),
                   jax.ShapeDtypeStruct((8, D_MODEL), _F32), jax.ShapeDtypeStruct((8, D_MODEL), _F32)],
        name=name, compiler_params=_cparams(dimension_semantics=("arbitrary",)),
    )(*ins)
    return dz, dzb, dg[0], db[0]


def _loss_head(y, target, *, name):
    tm = 256

    def body(y_ref, t_ref, d_ref, l_ref):
        e = y_ref[...] - t_ref[...]
        d_ref[...] = e * (1.0 / D_MODEL)

        @pl.when(pl.program_id(0) == 0)
        def _():
            l_ref[...] = jnp.zeros_like(l_ref)

        l_ref[...] += (0.5 / D_MODEL) * jnp.sum(e * e)

    row = pl.BlockSpec((tm, D_MODEL), lambda i: (i, 0))
    d, l = pl.pallas_call(
        body, grid=(SEQ // tm,), in_specs=[row, row], out_specs=[row, pl.BlockSpec((8, 128), lambda i: (0, 0))],
        out_shape=[jax.ShapeDtypeStruct((SEQ, D_MODEL), _F32), jax.ShapeDtypeStruct((8, 128), _F32)],
        name=name, compiler_params=_cparams(dimension_semantics=("arbitrary",)),
    )(y, target)
    return l[0, 0], d


_CONV_TN = 256


def _shift_down(v, k, rows):
    return jnp.where(rows >= k, pltpu.roll(v, k, axis=0), 0.0)


def _shift_up(v, k, rows):
    return jnp.where(rows < SEQ - k, pltpu.roll(v, SEQ - k, axis=0), 0.0)


def _conv_gate_fwd(g, u, conv_w, conv_b, *, name):
    def body(g_ref, u_ref, w_ref, b_ref, h_ref):
        gv = g_ref[...].astype(_F32)
        rows = lax.broadcasted_iota(jnp.int32, gv.shape, 0)
        w = w_ref[...]
        gc = b_ref[...] + w[2:3, :] * gv + w[1:2, :] * _shift_down(gv, 1, rows) + w[0:1, :] * _shift_down(gv, 2, rows)
        h_ref[...] = (gc * _sigmoid(gc) * u_ref[...].astype(_F32)).astype(h_ref.dtype)

    col = pl.BlockSpec((SEQ, _CONV_TN), lambda j: (0, j))
    return pl.pallas_call(
        body, grid=(D_FF // _CONV_TN,),
        in_specs=[col, col, pl.BlockSpec((3, _CONV_TN), lambda j: (0, j)), pl.BlockSpec((1, _CONV_TN), lambda j: (0, j))],
        out_specs=col, out_shape=jax.ShapeDtypeStruct((SEQ, D_FF), _MXU_DTYPE), name=name,
        compiler_params=_cparams(dimension_semantics=("parallel",)),
    )(g, u, conv_w, conv_b.reshape(1, D_FF))


def _conv_gate_bwd(dh, g, u, conv_w, conv_b, *, name):
    def body(dh_ref, g_ref, u_ref, w_ref, b_ref, dg_ref, du_ref, dw_ref, db_ref):
        gv = g_ref[...].astype(_F32)
        rows = lax.broadcasted_iota(jnp.int32, gv.shape, 0)
        w = w_ref[...]
        g1 = _shift_down(gv, 1, rows)
        g2 = _shift_down(gv, 2, rows)
        gc = b_ref[...] + w[2:3, :] * gv + w[1:2, :] * g1 + w[0:1, :] * g2
        sg = _sigmoid(gc)
        dh = dh_ref[...].astype(_F32)
        du_ref[...] = (dh * (gc * sg)).astype(du_ref.dtype)
        dgc = dh * u_ref[...].astype(_F32) * (sg * (1.0 + gc * (1.0 - sg)))
        dg = w[2:3, :] * dgc + w[1:2, :] * _shift_up(dgc, 1, rows) + w[0:1, :] * _shift_up(dgc, 2, rows)
        dg_ref[...] = dg.astype(dg_ref.dtype)
        dw_ref[0:1, :] = jnp.sum(dgc * g2, axis=0, keepdims=True)
        dw_ref[1:2, :] = jnp.sum(dgc * g1, axis=0, keepdims=True)
        dw_ref[2:3, :] = jnp.sum(dgc * gv, axis=0, keepdims=True)
        db_ref[...] = jnp.sum(dgc, axis=0, keepdims=True)

    col = pl.BlockSpec((SEQ, _CONV_TN), lambda j: (0, j))
    w3 = pl.BlockSpec((3, _CONV_TN), lambda j: (0, j))
    w1 = pl.BlockSpec((1, _CONV_TN), lambda j: (0, j))
    dg, du, dw, db = pl.pallas_call(
        body, grid=(D_FF // _CONV_TN,), in_specs=[col, col, col, w3, w1], out_specs=[col, col, w3, w1],
        out_shape=[jax.ShapeDtypeStruct((SEQ, D_FF), _MXU_DTYPE), jax.ShapeDtypeStruct((SEQ, D_FF), _MXU_DTYPE),
                   jax.ShapeDtypeStruct((3, D_FF), _F32), jax.ShapeDtypeStruct((1, D_FF), _F32)],
        name=name, compiler_params=_cparams(dimension_semantics=("parallel",)),
    )(dh, g, u, conv_w, conv_b.reshape(1, D_FF))
    return dg, du, dw, db[0]


def _rope_tables():
    half = ROPE_DIM // 2
    inv = ROPE_THETA ** (-jnp.arange(0, ROPE_DIM, 2, dtype=_F32) / ROPE_DIM)
    ang = jnp.arange(SEQ, dtype=_F32)[:, None] * inv[None, :]
    cos, sin = jnp.cos(ang), jnp.sin(ang)
    rest = HEAD_DIM - ROPE_DIM
    c = jnp.concatenate([cos, cos, jnp.ones((SEQ, rest), _F32)], axis=1)
    s1 = jnp.concatenate([-sin, jnp.zeros((SEQ, HEAD_DIM - half), _F32)], axis=1)
    s2 = jnp.concatenate([jnp.zeros((SEQ, half), _F32), sin, jnp.zeros((SEQ, rest), _F32)], axis=1)
    return c, s1, s2


def _rope_apply(x, c, s1, s2):
    return x * c + pltpu.roll(x, HEAD_DIM - ROPE_DIM // 2, axis=1) * s1 + pltpu.roll(x, ROPE_DIM // 2, axis=1) * s2


def _rope_transpose(d, c, s1, s2):
    half = ROPE_DIM // 2
    return d * c + pltpu.roll(d * s1, half, axis=1) + pltpu.roll(d * s2, HEAD_DIM - half, axis=1)


_NT = (((1,), (1,)), ((), ()))
_TN = (((0,), (0,)), ((), ()))
_SCALE = HEAD_DIM ** -0.5


def _band_scores(q, k2, n, lag_off):
    s = lax.dot_general(q, k2, _NT, preferred_element_type=_F32) * _SCALE
    row = lax.broadcasted_iota(jnp.int32, (BLOCK, 2 * BLOCK), 0)
    col = lax.broadcasted_iota(jnp.int32, (BLOCK, 2 * BLOCK), 1)
    front = (col >= row + lag_off) & (col < BLOCK) & (n > 0)
    own = (col >= BLOCK) & (col <= row + BLOCK)
    return jnp.where(front | own, s, _NEG)


_BAND_STEPS = SEQ // BLOCK


def _rows(start, d):
    if d == 1:
        return pl.ds(pl.multiple_of(start, BLOCK), BLOCK)
    return pl.ds(start, BLOCK, stride=d)


def _band_block(it, d):
    r, n = it % d, it // d
    span = BLOCK * d
    return n, _rows(r + n * span, d), _rows(r + jnp.maximum(n - 1, 0) * span, d)


def _band_fwd(proj, tabs, *, kv_heads, q_per_kv, q0, k0, v0, dilations, lag_off, sink, name):
    heads = kv_heads * q_per_kv

    def body(*refs):
        q_refs = refs[:q_per_kv]
        k_ref, v_ref, c_ref, s1_ref, s2_ref = refs[q_per_kv:q_per_kv + 5]
        rest = refs[q_per_kv + 5:]
        if sink is not None:
            sk_ref, rest = rest[0], rest[1:]
        o_ref, lse_ref, qs, ks, m_s, l_s, acc_s = rest
        c, s1, s2 = c_ref[...], s1_ref[...], s2_ref[...]
        ks[...] = _rope_apply(k_ref[...], c, s1, s2)
        for i in range(q_per_kv):
            qs[...] = _rope_apply(q_refs[i][...], c, s1, s2)
            for pi, d in enumerate(dilations):
                def step(it, carry, d=d, first=(pi == 0)):
                    n, cur, prev = _band_block(it, d)
                    q = qs[cur, :].astype(_MXU_DTYPE)
                    k2 = jnp.concatenate([ks[prev, :], ks[cur, :]], axis=0).astype(_MXU_DTYPE)
                    v2 = jnp.concatenate([v_ref[prev, :], v_ref[cur, :]], axis=0).astype(_MXU_DTYPE)
                    s = _band_scores(q, k2, n, lag_off)
                    m_b = jnp.max(s, axis=1, keepdims=True)
                    m_new = m_b if first else jnp.maximum(m_b, m_s[cur, :][:, 0:1])
                    p = jnp.exp(s - m_new)
                    l_new = jnp.sum(p, axis=1, keepdims=True)
                    acc = jnp.dot(p.astype(_MXU_DTYPE), v2, preferred_element_type=_F32)
                    if not first:
                        a = jnp.exp(m_s[cur, :][:, 0:1] - m_new)
                        l_new = l_new + a * l_s[cur, :][:, 0:1]
                        acc = acc + a * acc_s[cur, :]
                    m_s[cur, :] = jnp.broadcast_to(m_new, (BLOCK, HEAD_DIM))
                    l_s[cur, :] = jnp.broadcast_to(l_new, (BLOCK, HEAD_DIM))
                    acc_s[cur, :] = acc
                    return carry

                lax.fori_loop(0, _BAND_STEPS, step, 0, unroll=4)
            m, den = m_s[...], l_s[...]
            if sink is not None:
                sk = sk_ref[i]
                m_f = jnp.maximum(m, sk)
                a = jnp.exp(m - m_f)
                den = den * a + jnp.exp(sk - m_f)
                o = acc_s[...] * a / den
                m = m_f
            else:
                o = acc_s[...] / den
            o_ref[:, i * HEAD_DIM:(i + 1) * HEAD_DIM] = o
            lse_ref[:, i * HEAD_DIM:(i + 1) * HEAD_DIM] = m + jnp.log(den)

    col = (SEQ, HEAD_DIM)
    in_specs = [pl.BlockSpec(col, functools.partial(lambda g, i: (0, A_COLS + q0 + g * q_per_kv + i), i=i)) for i in range(q_per_kv)]
    in_specs += [pl.BlockSpec(col, lambda g: (0, A_COLS + k0 + g)), pl.BlockSpec(col, lambda g: (0, A_COLS + v0 + g))]
    in_specs += [pl.BlockSpec(col, lambda g: (0, 0))] * 3
    args = [proj] * (q_per_kv + 2) + list(tabs)
    if sink is not None:
        in_specs.append(pl.BlockSpec((q_per_kv, 1, HEAD_DIM), lambda g: (g, 0, 0)))
        args.append(jnp.broadcast_to(sink.reshape(heads, 1, 1), (heads, 1, HEAD_DIM)))
    o_spec = pl.BlockSpec((SEQ, q_per_kv * HEAD_DIM), lambda g: (0, g))
    shape = jax.ShapeDtypeStruct((SEQ, heads * HEAD_DIM), _F32)
    return pl.pallas_call(
        body, grid=(kv_heads,), in_specs=in_specs, out_specs=[o_spec, o_spec], out_shape=[shape, shape],
        scratch_shapes=[pltpu.VMEM(col, _F32)] * 5, name=name,
        compiler_params=_cparams(dimension_semantics=("parallel",)),
    )(*args)


def _band_bwd(proj, tabs, dmixed, o, lse, *, kv_heads, q_per_kv, q0, k0, v0, do0, dilations, lag_off, sink, after=(), name):
    heads = kv_heads * q_per_kv

    def body(*refs):
        q_refs = refs[:q_per_kv]
        k_ref, v_ref, c_ref, s1_ref, s2_ref = refs[q_per_kv:q_per_kv + 5]
        do_refs = refs[q_per_kv + 5:2 * q_per_kv + 5]
        o_ref, lse_ref = refs[2 * q_per_kv + 5:2 * q_per_kv + 7]
        rest = refs[2 * q_per_kv + 7:]
        if sink is not None:
            sk_ref, rest = rest[0], rest[1:]
            dq_ref, dk_ref, dv_ref, dsk_ref, qs, ks, dq_s, dk_s, dv_s = rest[len(after):]
        else:
            dq_ref, dk_ref, dv_ref, qs, ks, dq_s, dk_s, dv_s = rest[len(after):]
        c, s1, s2 = c_ref[...], s1_ref[...], s2_ref[...]
        ks[...] = _rope_apply(k_ref[...], c, s1, s2)
        dk_s[...] = jnp.zeros_like(dk_s)
        dv_s[...] = jnp.zeros_like(dv_s)
        for i in range(q_per_kv):
            hs = slice(i * HEAD_DIM, (i + 1) * HEAD_DIM)
            qs[...] = _rope_apply(q_refs[i][...], c, s1, s2)
            dq_s[...] = jnp.zeros_like(dq_s)
            do_ref = do_refs[i]
            for d in dilations:
                def step(it, carry, d=d, do_ref=do_ref, hs=hs):
                    n, cur, prev = _band_block(it, d)
                    q = qs[cur, :].astype(_MXU_DTYPE)
                    k2 = jnp.concatenate([ks[prev, :], ks[cur, :]], axis=0).astype(_MXU_DTYPE)
                    v2 = jnp.concatenate([v_ref[prev, :], v_ref[cur, :]], axis=0).astype(_MXU_DTYPE)
                    do = do_ref[cur, :]
                    delta = jnp.sum(do * o_ref[cur, hs], axis=1, keepdims=True)
                    lse_c = lse_ref[cur, hs][:, 0:1]
                    p = jnp.exp(_band_scores(q, k2, n, lag_off) - lse_c)
                    dob = do.astype(_MXU_DTYPE)
                    ds = (p * (lax.dot_general(dob, v2, _NT, preferred_element_type=_F32) - delta) * _SCALE).astype(_MXU_DTYPE)
                    dq_s[cur, :] += jnp.dot(ds, k2, preferred_element_type=_F32)
                    dk2 = lax.dot_general(ds, q, _TN, preferred_element_type=_F32)
                    dv2 = lax.dot_general(p.astype(_MXU_DTYPE), dob, _TN, preferred_element_type=_F32)
                    dk_s[prev, :] += dk2[:BLOCK]
                    dv_s[prev, :] += dv2[:BLOCK]
                    dk_s[cur, :] += dk2[BLOCK:]
                    dv_s[cur, :] += dv2[BLOCK:]
                    return carry

                lax.fori_loop(0, _BAND_STEPS, step, 0, unroll=4)
            dq_ref[:, hs] = _rope_transpose(dq_s[...], c, s1, s2).astype(dq_ref.dtype)
            if sink is not None:
                delta = jnp.sum(do_ref[...] * o_ref[:, hs], axis=1, keepdims=True)
                w_sink = jnp.exp(sk_ref[i] - lse_ref[:, hs])
                dsk_ref[i] = jnp.broadcast_to(jnp.sum(-delta * w_sink[:, 0:1]), (8, HEAD_DIM))
        dk_ref[...] = _rope_transpose(dk_s[...], c, s1, s2).astype(dk_ref.dtype)
        dv_ref[...] = dv_s[...].astype(dv_ref.dtype)

    col = (SEQ, HEAD_DIM)
    in_specs = [pl.BlockSpec(col, functools.partial(lambda g, i: (0, A_COLS + q0 + g * q_per_kv + i), i=i)) for i in range(q_per_kv)]
    in_specs += [pl.BlockSpec(col, lambda g: (0, A_COLS + k0 + g)), pl.BlockSpec(col, lambda g: (0, A_COLS + v0 + g))]
    in_specs += [pl.BlockSpec(col, lambda g: (0, 0))] * 3
    in_specs += [pl.BlockSpec(col, functools.partial(lambda g, i: (0, do0 + g * q_per_kv + i), i=i)) for i in range(q_per_kv)]
    wide = pl.BlockSpec((SEQ, q_per_kv * HEAD_DIM), lambda g: (0, g))
    in_specs += [wide, wide]
    args = [proj] * (q_per_kv + 2) + list(tabs) + [dmixed] * q_per_kv + [o, lse]
    out_specs = [wide, pl.BlockSpec(col, lambda g: (0, g)), pl.BlockSpec(col, lambda g: (0, g))]
    out_shape = [jax.ShapeDtypeStruct((SEQ, heads * HEAD_DIM), _MXU_DTYPE), jax.ShapeDtypeStruct((SEQ, kv_heads * HEAD_DIM), _MXU_DTYPE),
                 jax.ShapeDtypeStruct((SEQ, kv_heads * HEAD_DIM), _MXU_DTYPE)]
    if sink is not None:
        in_specs.append(pl.BlockSpec((q_per_kv, 1, HEAD_DIM), lambda g: (g, 0, 0)))
        args.append(jnp.broadcast_to(sink.reshape(heads, 1, 1), (heads, 1, HEAD_DIM)))
        out_specs.append(pl.BlockSpec((q_per_kv, 8, HEAD_DIM), lambda g: (g, 0, 0)))
        out_shape.append(jax.ShapeDtypeStruct((heads, 8, HEAD_DIM), _F32))
    in_specs += [pl.BlockSpec(memory_space=pl.ANY)] * len(after)
    args += list(after)
    res = pl.pallas_call(
        body, grid=(kv_heads,), in_specs=in_specs, out_specs=out_specs, out_shape=out_shape,
        scratch_shapes=[pltpu.VMEM(col, _F32)] * 5, name=name,
        compiler_params=_cparams(dimension_semantics=("parallel",)),
    )(*args)
    if sink is not None:
        return res[0], res[1], res[2], res[3][:, 0, 0]
    return res


_DILATED = dict(kv_heads=B_HEADS, q_per_kv=1, q0=QB0, k0=KB0, v0=VB0, dilations=DILATIONS, lag_off=0, sink=None)
_SWA = dict(kv_heads=C_KV_HEADS, q_per_kv=C_HEADS // C_KV_HEADS, q0=QC0, k0=KC0, v0=VC0, dilations=(1,), lag_off=1)


_HG_TILE = 128
_HG_CHUNKS = _HG_TILE // A_CHUNK
_HG_TILES = SEQ // _HG_TILE
_HI = lax.Precision.HIGHEST


def _chunk_tri():
    i = np.arange(_HG_TILE)
    return jnp.asarray(((i[:, None] // A_CHUNK == i[None, :] // A_CHUNK) & (i[None, :] <= i[:, None])).astype(np.float32))


def _layer_lb(lb_ref, layer):
    if layer == 0:
        return jnp.zeros((1, HEAD_DIM), _F32)
    lg = lb_ref[...]
    m = jnp.max(lg, axis=0, keepdims=True)
    e = jnp.exp(lg - m)
    return e[1:2, :] / jnp.sum(e, axis=0, keepdims=True)


def _hgrn_gates(q, fr, lb):
    sgq = _sigmoid(q)
    sg = _sigmoid(fr)
    f = lb + (1.0 - lb) * sg
    return sgq, q * sgq, sg, f, 1.0 - f


def _hgrn_fwd(proj, lb_logits, norm_w, layer, *, name):
    tri = _chunk_tri()

    def body(q_ref, f_ref, i_ref, g_ref, lb_ref, nw_ref, tri_ref, o_ref, raw_ref, st_ref, state):
        @pl.when(pl.program_id(1) == 0)
        def _():
            state[...] = jnp.zeros_like(state)

        lb = _layer_lb(lb_ref, layer)
        _, qs, _, f, k = _hgrn_gates(q_ref[...], f_ref[...], lb)
        v = i_ref[...]
        b = jnp.dot(tri_ref[...], jnp.log(f), precision=_HI, preferred_element_type=_F32)
        eb = jnp.exp(b)
        ridx = lax.broadcasted_iota(jnp.int32, (A_CHUNK, HEAD_DIM), 0)
        outs = []
        for c in range(_HG_CHUNKS):
            sl = slice(c * A_CHUNK, (c + 1) * A_CHUNK)
            bc, qc, kc, vc = b[sl], qs[sl], k[sl], v[sl]
            bl = bc[A_CHUNK - 1:A_CHUNK]
            st = state[...]
            st_ref[0, c] = st
            o_c = lax.dot_general((qc * eb[sl]).astype(_MXU_DTYPE), st.astype(_MXU_DTYPE), _NT, preferred_element_type=_F32)
            rows = []
            for i in range(A_CHUNK):
                di = jnp.exp(jnp.where(ridx <= i, bc[i:i + 1] - bc, _NEG))
                a = jnp.sum(qc[i:i + 1] * kc * di, axis=1, keepdims=True)
                rows.append(jnp.sum(a * vc, axis=0, keepdims=True))
            outs.append(o_c + jnp.concatenate(rows, axis=0))
            kt = (kc * jnp.exp(bl - bc)).astype(_MXU_DTYPE)
            state[...] = st * jnp.exp(bl) + lax.dot_general(vc.astype(_MXU_DTYPE), kt, _TN, preferred_element_type=_F32)
        o = jnp.concatenate(outs, axis=0)
        raw_ref[...] = o
        r = lax.rsqrt(jnp.mean(o * o, axis=-1, keepdims=True) + LN_EPS)
        g = g_ref[...]
        o_ref[...] = o * r * nw_ref[...] * (g * _sigmoid(g))

    blk = (_HG_TILE, HEAD_DIM)

    def col(base):
        return pl.BlockSpec(blk, lambda h, t: (t, base + h))

    o_spec = pl.BlockSpec(blk, lambda h, t: (t, h))
    o_shape = jax.ShapeDtypeStruct((SEQ, A_HEADS * HEAD_DIM), _F32)
    return pl.pallas_call(
        body, grid=(A_HEADS, _HG_TILES),
        in_specs=[col(0), col(4), col(8), col(12), pl.BlockSpec((DEPTH, HEAD_DIM), lambda h, t: (0, h)),
                  pl.BlockSpec((1, HEAD_DIM), lambda h, t: (0, 0)), pl.BlockSpec(blk, lambda h, t: (0, 0))],
        out_specs=[o_spec, o_spec, pl.BlockSpec((1, _HG_CHUNKS, HEAD_DIM, HEAD_DIM), lambda h, t: (h, t, 0, 0))],
        out_shape=[o_shape, o_shape, jax.ShapeDtypeStruct((A_HEADS, SEQ // A_CHUNK, HEAD_DIM, HEAD_DIM), _F32)],
        scratch_shapes=[pltpu.VMEM((HEAD_DIM, HEAD_DIM), _F32)], name=name,
        compiler_params=_cparams(dimension_semantics=("parallel", "arbitrary")),
    )(proj, proj, proj, proj, lb_logits, norm_w.reshape(1, HEAD_DIM), tri)


def _hgrn_bwd(proj, lb_logits, norm_w, raw, states, dmixed, layer, *, name):
    tri = _chunk_tri()
    triu = tri.T

    def body(q_ref, f_ref, i_ref, g_ref, lb_ref, nw_ref, tri_ref, triu_ref, raw_ref, do_ref, st_ref,
             dq_ref, df_ref, di_ref, dg_ref, dnw_ref, dlb_ref, dstate):
        @pl.when(pl.program_id(1) == 0)
        def _():
            dstate[...] = jnp.zeros_like(dstate)
            dlb_ref[...] = jnp.zeros_like(dlb_ref)

        @pl.when((pl.program_id(0) == 0) & (pl.program_id(1) == 0))
        def _():
            dnw_ref[...] = jnp.zeros_like(dnw_ref)

        lb = _layer_lb(lb_ref, layer)
        q = q_ref[...]
        sgq, qs, sg, f, k = _hgrn_gates(q, f_ref[...], lb)
        v = i_ref[...]
        b = jnp.dot(tri_ref[...], jnp.log(f), precision=_HI, preferred_element_type=_F32)
        eb = jnp.exp(b)
        g = g_ref[...]
        nw = nw_ref[...]
        o = raw_ref[...]
        dout = do_ref[...]
        sgg = _sigmoid(g)
        r = lax.rsqrt(jnp.mean(o * o, axis=-1, keepdims=True) + LN_EPS)
        dg_ref[...] = (dout * (o * r * nw) * (sgg * (1.0 + g * (1.0 - sgg)))).astype(dg_ref.dtype)
        don = dout * (g * sgg)
        dnw_ref[0:1, :] += jnp.sum(don * o * r, axis=0, keepdims=True)
        dy = don * nw
        do_raw = r * dy - o * (r * r * r) * jnp.mean(o * dy, axis=-1, keepdims=True)

        ridx = lax.broadcasted_iota(jnp.int32, (A_CHUNK, HEAD_DIM), 0)
        dqs_t, dk_t, db_t, dv_t = [None] * _HG_CHUNKS, [None] * _HG_CHUNKS, [None] * _HG_CHUNKS, [None] * _HG_CHUNKS
        for c in reversed(range(_HG_CHUNKS)):
            sl = slice(c * A_CHUNK, (c + 1) * A_CHUNK)
            bc, qc, kc, vc, doc = b[sl], qs[sl], k[sl], v[sl], do_raw[sl]
            bl = bc[A_CHUNK - 1:A_CHUNK]
            ebc = eb[sl]
            ebl = jnp.exp(bl - bc)
            lam = jnp.exp(bl)
            qt = qc * ebc
            kt = kc * ebl
            dst = dstate[...]
            stp = st_ref[0, c]
            dob = doc.astype(_MXU_DTYPE)
            dstb = dst.astype(_MXU_DTYPE)
            dqt = jnp.dot(dob, stp.astype(_MXU_DTYPE), preferred_element_type=_F32)
            dkt = jnp.dot(vc.astype(_MXU_DTYPE), dstb, preferred_element_type=_F32)
            dv = lax.dot_general(kt.astype(_MXU_DTYPE), dstb, _NT, preferred_element_type=_F32)
            dlam = jnp.sum(stp * dst, axis=0, keepdims=True)
            dstate[...] = dst * lam + lax.dot_general(dob, qt.astype(_MXU_DTYPE), _TN, preferred_element_type=_F32)
            dqs_rows = []
            dk_in = jnp.zeros((A_CHUNK, HEAD_DIM), _F32)
            for i in range(A_CHUNK):
                di = jnp.exp(jnp.where(ridx <= i, bc[i:i + 1] - bc, _NEG))
                qi = qc[i:i + 1]
                doi = doc[i:i + 1]
                w = kc * di
                a = jnp.sum(qi * w, axis=1, keepdims=True)
                dv = dv + a * doi
                da = jnp.sum(doi * vc, axis=1, keepdims=True)
                dqs_rows.append(jnp.sum(da * w, axis=0, keepdims=True))
                dk_in = dk_in + da * (qi * di)
            dqs_in = jnp.concatenate(dqs_rows, axis=0)
            dbl = jnp.sum(dkt * kt, axis=0, keepdims=True) + dlam * lam
            db = qc * dqs_in - kc * dk_in + dqt * qt - dkt * kt
            db_t[c] = db + jnp.where(ridx == A_CHUNK - 1, dbl, 0.0)
            dqs_t[c] = dqs_in + dqt * ebc
            dk_t[c] = dk_in + dkt * ebl
            dv_t[c] = dv
        dqs = jnp.concatenate(dqs_t, axis=0)
        dk = jnp.concatenate(dk_t, axis=0)
        db = jnp.concatenate(db_t, axis=0)
        di_ref[...] = jnp.concatenate(dv_t, axis=0).astype(di_ref.dtype)
        dlogf = jnp.dot(triu_ref[...], db, precision=_HI, preferred_element_type=_F32)
        df = dlogf / f - dk
        df_ref[...] = (df * (1.0 - lb) * sg * (1.0 - sg)).astype(df_ref.dtype)
        dlb_ref[0, 0:1, :] += jnp.sum(df * (1.0 - sg), axis=0, keepdims=True)
        dq_ref[...] = (dqs * (sgq * (1.0 + q * (1.0 - sgq)))).astype(dq_ref.dtype)

    blk = (_HG_TILE, HEAD_DIM)
    last = _HG_TILES - 1

    def col(base):
        return pl.BlockSpec(blk, lambda h, t: (last - t, base + h))

    tri_spec = pl.BlockSpec(blk, lambda h, t: (0, 0))
    acc_spec = pl.BlockSpec((1, 8, HEAD_DIM), lambda h, t: (h, 0, 0))
    acc_shape = jax.ShapeDtypeStruct((A_HEADS, 8, HEAD_DIM), _F32)
    dq, df, di, dg, dnw, dlb = pl.pallas_call(
        body, grid=(A_HEADS, _HG_TILES),
        in_specs=[col(0), col(4), col(8), col(12), pl.BlockSpec((DEPTH, HEAD_DIM), lambda h, t: (0, h)),
                  pl.BlockSpec((1, HEAD_DIM), lambda h, t: (0, 0)), tri_spec, tri_spec, col(0), col(0),
                  pl.BlockSpec((1, _HG_CHUNKS, HEAD_DIM, HEAD_DIM), lambda h, t: (h, last - t, 0, 0))],
        out_specs=[col(0), col(0), col(0), col(0), pl.BlockSpec((8, HEAD_DIM), lambda h, t: (0, 0)), acc_spec],
        out_shape=[jax.ShapeDtypeStruct((SEQ, A_HEADS * HEAD_DIM), _MXU_DTYPE)] * 4
        + [jax.ShapeDtypeStruct((8, HEAD_DIM), _F32), acc_shape],
        scratch_shapes=[pltpu.VMEM((HEAD_DIM, HEAD_DIM), _F32)], name=name,
        compiler_params=_cparams(dimension_semantics=("arbitrary", "arbitrary")),
    )(proj, proj, proj, proj, lb_logits, norm_w.reshape(1, HEAD_DIM), tri, triu, raw, dmixed, states)
    return dq, df, di, dg, dnw[0], dlb[:, 0, :].reshape(A_HEADS * HEAD_DIM)


def _exchange(arrays, scatter, *, name):
    n = len(arrays)
    n_peer = N_DEV - 1

    def body(*refs):
        ins, outs = refs[:n], refs[n:2 * n]
        send_sems, recv_sems, loc_sems = refs[2 * n:]
        x, y, c = lax.axis_index("x"), lax.axis_index("y"), lax.axis_index("c")
        me = 4 * x + 2 * y + c
        local = []
        for a in range(n):
            cp = pltpu.make_async_copy(ins[a].at[me] if scatter else ins[a], outs[a].at[me], loc_sems.at[a])
            cp.start()
            local.append(cp)

        def peer(k):
            px = jnp.bitwise_xor(x, (k >> 2) & 1)
            py = jnp.bitwise_xor(y, (k >> 1) & 1)
            pc = jnp.bitwise_xor(c, k & 1)
            return (px, py, pc), 4 * px + 2 * py + pc

        def copy(a, k):
            dev, pid = peer(k)
            return pltpu.make_async_remote_copy(
                src_ref=ins[a].at[pid] if scatter else ins[a], dst_ref=outs[a].at[me],
                send_sem=send_sems.at[a * n_peer + k - 1], recv_sem=recv_sems.at[a * n_peer + k - 1],
                device_id=dev, device_id_type=pl.DeviceIdType.MESH)

        def arrival(a, k):
            dev, pid = peer(k)
            return pltpu.make_async_remote_copy(
                src_ref=ins[a].at[pid] if scatter else ins[a], dst_ref=outs[a].at[pid],
                send_sem=send_sems.at[a * n_peer + k - 1], recv_sem=recv_sems.at[a * n_peer + k - 1],
                device_id=dev, device_id_type=pl.DeviceIdType.MESH)

        sends = [copy(a, k) for k in range(1, N_DEV) for a in range(n)]
        for cp in sends:
            cp.start()
        for k in range(1, N_DEV):
            for a in range(n):
                arrival(a, k).wait_recv()
        for cp in sends:
            cp.wait_send()
        for cp in local:
            cp.wait()

    def out_shape(a):
        blk = a.shape[1:] if scatter else a.shape
        return jax.ShapeDtypeStruct((N_DEV,) + tuple(blk), a.dtype)

    any_spec = pl.BlockSpec(memory_space=pl.ANY)
    return pl.pallas_call(
        body, in_specs=[any_spec] * n, out_specs=[any_spec] * n, out_shape=[out_shape(a) for a in arrays],
        scratch_shapes=[pltpu.SemaphoreType.DMA((n * n_peer,)), pltpu.SemaphoreType.DMA((n * n_peer,)),
                        pltpu.SemaphoreType.DMA((n,))],
        name=name, compiler_params=pltpu.CompilerParams(has_side_effects=True),
    )(*arrays)


N_CHIP = N_DEV // 2
_MESH_ID = pl.DeviceIdType.MESH


def _place():
    x, y, c = lax.axis_index("x"), lax.axis_index("y"), lax.axis_index("c")
    chips = [(1 - x, y), (x, 1 - y), (1 - x, 1 - y)]
    return x, y, c, 2 * x + y, chips


def _gather_blocks(arrays, *, name):
    n = len(arrays)

    def body(*refs):
        ins, outs = refs[:n], refs[n:2 * n]
        send_sems, recv_sems, loc_sems = refs[2 * n:]
        x, y, c, _, chips = _place()
        me = 4 * x + 2 * y + c
        sibling = (x, y, 1 - c)

        def slot(px, py, pc):
            return 4 * px + 2 * py + pc

        def copy(a, k, block, to, src=None):
            dst = outs[a].at[slot(*block)]
            return pltpu.make_async_remote_copy(
                src_ref=dst if src is None else src, dst_ref=dst, send_sem=send_sems.at[7 * a + k],
                recv_sem=recv_sems.at[7 * a + k], device_id=to, device_id_type=_MESH_ID)

        local = [pltpu.make_async_copy(ins[a], outs[a].at[me], loc_sems.at[a]) for a in range(n)]
        for cp in local:
            cp.start()
        first = []
        for a in range(n):
            first.append(copy(a, 0, (x, y, c), sibling, src=ins[a]))
            first += [copy(a, 1 + j, (x, y, c), (*chip, c), src=ins[a]) for j, chip in enumerate(chips)]
        for cp in first:
            cp.start()
        passed = []
        for a in range(n):
            for j, chip in enumerate(chips):
                copy(a, 1 + j, (*chip, c), (x, y, c)).wait_recv()
                fwd = copy(a, 4 + j, (*chip, c), sibling)
                fwd.start()
                passed.append(fwd)
        for a in range(n):
            copy(a, 0, sibling, (x, y, c)).wait_recv()
            for j, chip in enumerate(chips):
                copy(a, 4 + j, (*chip, 1 - c), (x, y, c)).wait_recv()
        for cp in first + passed:
            cp.wait_send()
        for cp in local:
            cp.wait()

    any_spec = pl.BlockSpec(memory_space=pl.ANY)
    return pl.pallas_call(
        body, in_specs=[any_spec] * n, out_specs=[any_spec] * n,
        out_shape=[jax.ShapeDtypeStruct((N_DEV,) + a.shape, a.dtype) for a in arrays],
        scratch_shapes=[pltpu.SemaphoreType.DMA((7 * n,)), pltpu.SemaphoreType.DMA((7 * n,)), pltpu.SemaphoreType.DMA((n,))],
        name=name, compiler_params=pltpu.CompilerParams(has_side_effects=True),
    )(*arrays)


def _sibling_swap(arrays, *, name):
    n = len(arrays)

    def body(*refs):
        ins, outs = refs[:n], refs[n:2 * n]
        send_sems, recv_sems = refs[2 * n:]
        x, y, c, _, _ = _place()
        copies = [pltpu.make_async_remote_copy(
            src_ref=ins[a].at[:, 1 - c], dst_ref=outs[a], send_sem=send_sems.at[a], recv_sem=recv_sems.at[a],
            device_id=(x, y, 1 - c), device_id_type=_MESH_ID) for a in range(n)]
        for cp in copies:
            cp.start()
        for cp in copies:
            cp.wait()

    any_spec = pl.BlockSpec(memory_space=pl.ANY)
    return pl.pallas_call(
        body, in_specs=[any_spec] * n, out_specs=[any_spec] * n,
        out_shape=[jax.ShapeDtypeStruct((N_CHIP,) + a.shape[2:], a.dtype) for a in arrays],
        scratch_shapes=[pltpu.SemaphoreType.DMA((n,)), pltpu.SemaphoreType.DMA((n,))],
        name=name, compiler_params=pltpu.CompilerParams(has_side_effects=True),
    )(*arrays)


def _pair_add(mine, theirs, core, *, name):
    _, _, R, C = mine.shape
    tr = max(t for t in range(16, R + 1, 16) if R % t == 0 and t * C <= 512 * 1024)

    def body(core_ref, m_ref, t_ref, o_ref):
        del core_ref
        o_ref[...] = (m_ref[...].astype(_F32) + t_ref[...].astype(_F32)).astype(o_ref.dtype)

    grid_spec = pltpu.PrefetchScalarGridSpec(
        num_scalar_prefetch=1, grid=(N_CHIP, R // tr),
        in_specs=[pl.BlockSpec((None, None, tr, C), lambda q, i, core: (q, core[0], i, 0)),
                  pl.BlockSpec((None, tr, C), lambda q, i, core: (q, i, 0))],
        out_specs=pl.BlockSpec((None, tr, C), lambda q, i, core: (q, i, 0)))
    return pl.pallas_call(
        body, grid_spec=grid_spec, out_shape=jax.ShapeDtypeStruct((N_CHIP, R, C), mine.dtype), name=name,
        compiler_params=_cparams(dimension_semantics=("parallel", "parallel")),
    )(core.reshape(1), mine, theirs)


_HBM = pl.BlockSpec(memory_space=pltpu.HBM)
_SEM = pl.BlockSpec(memory_space=pltpu.SEMAPHORE)
_TOKEN = pl.BlockSpec(memory_space=pltpu.VMEM)
_DATAFLOW = pltpu.SideEffectType.DATAFLOW_SIDE_EFFECTING


def _hbm(a):
    return pltpu.HBM(a.shape, a.dtype)


def _token_shape():
    return jax.ShapeDtypeStruct((8, 128), _F32)


def _dev_slot(px, py, pc):
    return 4 * px + 2 * py + pc


def _gather_start(blocks, landings, *, name):
    n = len(blocks)

    def body(*refs):
        ins, lands = refs[:n], refs[n:2 * n]
        send_sems, d2d_sems, ici_sems = refs[2 * n:2 * n + 3]
        token = refs[-1]
        x, y, c, _, chips = _place()
        for a in range(n):
            dst = lands[a].at[_dev_slot(x, y, c)]
            pltpu.make_async_remote_copy(src_ref=ins[a], dst_ref=dst, send_sem=send_sems.at[4 * a], recv_sem=d2d_sems.at[a],
                                         device_id=(x, y, 1 - c), device_id_type=_MESH_ID).start()
            for j, chip in enumerate(chips):
                pltpu.make_async_remote_copy(src_ref=ins[a], dst_ref=dst, send_sem=send_sems.at[4 * a + 1 + j],
                                             recv_sem=ici_sems.at[3 * a + j], device_id=(*chip, c),
                                             device_id_type=_MESH_ID).start()
        token[...] = jnp.zeros_like(token)

    res = pl.pallas_call(
        body, name=name, in_specs=[_HBM] * (2 * n),
        out_shape=(pltpu.SemaphoreType.DMA((4 * n,)), pltpu.SemaphoreType.DMA((n,)), pltpu.SemaphoreType.DMA((3 * n,)),
                   *[_hbm(b) for b in blocks], *[_hbm(b) for b in landings], _token_shape()),
        out_specs=(_SEM, _SEM, _SEM, *[_HBM] * (2 * n), _TOKEN),
        input_output_aliases={i: 3 + i for i in range(2 * n)},
        compiler_params=pltpu.CompilerParams(has_side_effects=_DATAFLOW),
    )(*[pltpu.with_memory_space_constraint(b, pltpu.HBM) for b in blocks],
      *[pltpu.with_memory_space_constraint(b, pltpu.HBM) for b in landings])
    return res[0], res[1], res[2], list(res[3:3 + n]), list(res[3 + n:3 + 2 * n]), res[-1]


def _gather_forward(landings, ici_sems, first, after, *, name):
    n = len(landings)

    def body(*refs):
        lands = refs[:n]
        ici = refs[n]
        f_send, f_recv = refs[n + 2], refs[n + 3]
        token = refs[-1]
        x, y, c, _, chips = _place()
        for a in range(n):
            for j, chip in enumerate(chips):
                blk = lands[a].at[_dev_slot(*chip, c)]
                pltpu.make_async_remote_copy(src_ref=blk, dst_ref=blk, send_sem=f_send.at[3 * a + j],
                                             recv_sem=ici.at[3 * (first + a) + j], device_id=(*chip, c),
                                             device_id_type=_MESH_ID).wait_recv()
                pltpu.make_async_remote_copy(src_ref=blk, dst_ref=blk, send_sem=f_send.at[3 * a + j], recv_sem=f_recv.at[3 * a + j],
                                             device_id=(x, y, 1 - c), device_id_type=_MESH_ID).start()
        token[...] = jnp.zeros_like(token)

    res = pl.pallas_call(
        body, name=name, in_specs=[_HBM] * n + [_SEM, pl.BlockSpec(memory_space=pl.ANY)],
        out_shape=(pltpu.SemaphoreType.DMA((3 * n,)), pltpu.SemaphoreType.DMA((3 * n,)), *[_hbm(b) for b in landings], _token_shape()),
        out_specs=(_SEM, _SEM, *[_HBM] * n, _TOKEN),
        input_output_aliases={i: 2 + i for i in range(n)},
        compiler_params=pltpu.CompilerParams(has_side_effects=_DATAFLOW),
    )(*landings, ici_sems, after)
    return res[0], res[1], list(res[2:2 + n]), res[-1]


def _gather_wait(blocks, landings, send_sems, d2d_sems, first, f_send, f_recv, after, *, name):
    n = len(landings)

    def body(*refs):
        ins, lands = refs[:n], refs[n:2 * n]
        send, d2d, fs, fr = refs[2 * n:2 * n + 4]
        x, y, c, _, chips = _place()
        me = (x, y, c)
        for a in range(n):
            own = lands[a].at[_dev_slot(x, y, 1 - c)]
            g = first + a
            pltpu.make_async_remote_copy(src_ref=ins[a], dst_ref=own, send_sem=send.at[4 * g], recv_sem=d2d.at[g],
                                         device_id=me, device_id_type=_MESH_ID).wait_recv()
            for j, chip in enumerate(chips):
                blk = lands[a].at[_dev_slot(*chip, 1 - c)]
                pltpu.make_async_remote_copy(src_ref=blk, dst_ref=blk, send_sem=fs.at[3 * a + j], recv_sem=fr.at[3 * a + j],
                                             device_id=me, device_id_type=_MESH_ID).wait_recv()
            for k in range(4):
                pltpu.make_async_remote_copy(src_ref=ins[a], dst_ref=own, send_sem=send.at[4 * g + k], recv_sem=d2d.at[g],
                                             device_id=me, device_id_type=_MESH_ID).wait_send()
            for j in range(3):
                pltpu.make_async_remote_copy(src_ref=own, dst_ref=own, send_sem=fs.at[3 * a + j], recv_sem=fr.at[3 * a + j],
                                             device_id=me, device_id_type=_MESH_ID).wait_send()

    res = pl.pallas_call(
        body, name=name, in_specs=[_HBM] * (2 * n) + [_SEM] * 4 + [pl.BlockSpec(memory_space=pl.ANY)],
        out_shape=(*[_hbm(b) for b in blocks], *[_hbm(b) for b in landings]), out_specs=tuple([_HBM] * (2 * n)),
        input_output_aliases={i: i for i in range(2 * n)},
        compiler_params=pltpu.CompilerParams(has_side_effects=_DATAFLOW),
    )(*blocks, *landings, send_sems, d2d_sems, f_send, f_recv, after)
    return list(res[n:])


def _swap_start(mine, landings, *, name):
    n = len(mine)

    def body(*refs):
        ins, lands = refs[:n], refs[n:2 * n]
        send_sems, recv_sems = refs[2 * n:2 * n + 2]
        token = refs[-1]
        x, y, c, _, _ = _place()
        for a in range(n):
            pltpu.make_async_remote_copy(src_ref=ins[a].at[:, 1 - c], dst_ref=lands[a], send_sem=send_sems.at[a],
                                         recv_sem=recv_sems.at[a], device_id=(x, y, 1 - c), device_id_type=_MESH_ID).start()
        token[...] = jnp.zeros_like(token)

    res = pl.pallas_call(
        body, name=name, in_specs=[_HBM] * (2 * n),
        out_shape=(pltpu.SemaphoreType.DMA((n,)), pltpu.SemaphoreType.DMA((n,)),
                   *[_hbm(b) for b in mine], *[_hbm(b) for b in landings], _token_shape()),
        out_specs=(_SEM, _SEM, *[_HBM] * (2 * n), _TOKEN),
        input_output_aliases={i: 2 + i for i in range(2 * n)},
        compiler_params=pltpu.CompilerParams(has_side_effects=_DATAFLOW),
    )(*[pltpu.with_memory_space_constraint(b, pltpu.HBM) for b in mine],
      *[pltpu.with_memory_space_constraint(b, pltpu.HBM) for b in landings])
    return res[0], res[1], list(res[2:2 + n]), list(res[2 + n:2 + 2 * n]), res[-1]


def _swap_wait(mine, landings, send_sems, recv_sems, after, *, name):
    n = len(mine)

    def body(*refs):
        ins, lands = refs[:n], refs[n:2 * n]
        send, recv = refs[2 * n:2 * n + 2]
        x, y, c, _, _ = _place()
        for a in range(n):
            cp = pltpu.make_async_remote_copy(src_ref=ins[a].at[:, 1 - c], dst_ref=lands[a], send_sem=send.at[a],
                                              recv_sem=recv.at[a], device_id=(x, y, c), device_id_type=_MESH_ID)
            cp.wait_recv()
            cp.wait_send()

    res = pl.pallas_call(
        body, name=name, in_specs=[_HBM] * (2 * n) + [_SEM] * 2 + [pl.BlockSpec(memory_space=pl.ANY)],
        out_shape=(*[_hbm(b) for b in mine], *[_hbm(b) for b in landings]), out_specs=tuple([_HBM] * (2 * n)),
        input_output_aliases={i: i for i in range(2 * n)},
        compiler_params=pltpu.CompilerParams(has_side_effects=_DATAFLOW),
    )(*mine, *landings, send_sems, recv_sems, after)
    return list(res[:n]), list(res[n:])


def _chip_exchange_start(sums, landings, *, name):
    n = len(sums)

    def body(*refs):
        ins, lands = refs[:n], refs[n:2 * n]
        send_sems, recv_sems = refs[2 * n:2 * n + 2]
        token = refs[-1]
        _, _, c, p, chips = _place()
        for a in range(n):
            for j, (qx, qy) in enumerate(chips):
                pltpu.make_async_remote_copy(src_ref=ins[a].at[2 * qx + qy], dst_ref=lands[a].at[p], send_sem=send_sems.at[3 * a + j],
                                             recv_sem=recv_sems.at[3 * a + j], device_id=(qx, qy, c), device_id_type=_MESH_ID).start()
        token[...] = jnp.zeros_like(token)

    res = pl.pallas_call(
        body, name=name, in_specs=[_HBM] * (2 * n),
        out_shape=(pltpu.SemaphoreType.DMA((3 * n,)), pltpu.SemaphoreType.DMA((3 * n,)),
                   *[_hbm(b) for b in sums], *[_hbm(b) for b in landings], _token_shape()),
        out_specs=(_SEM, _SEM, *[_HBM] * (2 * n), _TOKEN),
        input_output_aliases={i: 2 + i for i in range(2 * n)},
        compiler_params=pltpu.CompilerParams(has_side_effects=_DATAFLOW),
    )(*[pltpu.with_memory_space_constraint(b, pltpu.HBM) for b in sums],
      *[pltpu.with_memory_space_constraint(b, pltpu.HBM) for b in landings])
    return res[0], res[1], list(res[2:2 + n]), list(res[2 + n:2 + 2 * n]), res[-1]


def _chip_exchange_wait(sums, landings, send_sems, recv_sems, after, *, name):
    n = len(sums)

    def body(*refs):
        ins, lands = refs[:n], refs[n:2 * n]
        send, recv = refs[2 * n:2 * n + 2]
        x, y, c, _, chips = _place()
        for a in range(n):
            for j, (qx, qy) in enumerate(chips):
                q = 2 * qx + qy
                cp = pltpu.make_async_remote_copy(src_ref=ins[a].at[q], dst_ref=lands[a].at[q], send_sem=send.at[3 * a + j],
                                                  recv_sem=recv.at[3 * a + j], device_id=(x, y, c), device_id_type=_MESH_ID)
                cp.wait_recv()
                cp.wait_send()

    res = pl.pallas_call(
        body, name=name, in_specs=[_HBM] * (2 * n) + [_SEM] * 2 + [pl.BlockSpec(memory_space=pl.ANY)],
        out_shape=(*[_hbm(b) for b in sums], *[_hbm(b) for b in landings]), out_specs=tuple([_HBM] * (2 * n)),
        input_output_aliases={i: i for i in range(2 * n)},
        compiler_params=pltpu.CompilerParams(has_side_effects=_DATAFLOW),
    )(*sums, *landings, send_sems, recv_sems, after)
    return list(res[:n]), list(res[n:])


_C1 = 1.0 - ADAM_B1 ** ADAM_STEP
_C2 = 1.0 - ADAM_B2 ** ADAM_STEP


def _adamw_math(g, w, m, v):
    m = ADAM_B1 * m + (1.0 - ADAM_B1) * g
    v = ADAM_B2 * v + (1.0 - ADAM_B2) * (g * g)
    delta = -ADAM_LR * ((m / _C1) / (jnp.sqrt(v / _C2) + ADAM_EPS) + ADAM_WD * w)
    return delta, m, v


def _adamw_reduce(landed, sums, chip, w, m, v, layer, prev, *, name):
    _, R, C = w.shape
    tr = max(t for t in range(16, R + 1, 16) if R % t == 0 and t * C <= 256 * 1024)

    def body(chip_ref, p_ref, own_ref, w_ref, m_ref, v_ref, *rest):
        g_ref, d_ref, nm_ref, nv_ref = rest[-4:]
        own = own_ref[...].astype(_F32)
        g = jnp.where(chip_ref[0] == 0, own, p_ref[0].astype(_F32))
        for q in range(1, N_CHIP):
            g = g + jnp.where(chip_ref[0] == q, own, p_ref[q].astype(_F32))
        d, nm, nv = _adamw_math(g, w_ref[...], m_ref[...], v_ref[...])
        g_ref[...] = g
        d_ref[...] = d
        nm_ref[...] = nm
        nv_ref[...] = nv

    blk = pl.BlockSpec((None, tr, C), lambda i, chip: (layer, i, 0))
    shape = jax.ShapeDtypeStruct((DEPTH, R, C), _F32)
    kept = [] if prev is None else list(prev)
    grid_spec = pltpu.PrefetchScalarGridSpec(
        num_scalar_prefetch=1, grid=(R // tr,),
        in_specs=[pl.BlockSpec((N_CHIP, tr, C), lambda i, chip: (0, i, 0)),
                  pl.BlockSpec((None, tr, C), lambda i, chip: (chip[0], i, 0)), blk, blk, blk]
        + [pl.BlockSpec(memory_space=pl.ANY)] * len(kept),
        out_specs=[blk] * 4)
    return pl.pallas_call(
        body, grid_spec=grid_spec, out_shape=[shape] * 4, name=name,
        input_output_aliases={6 + k: k for k in range(len(kept))},
        compiler_params=_cparams(dimension_semantics=("parallel",)),
    )(chip.reshape(1), landed, sums, w, m, v, *kept)


_PACK_LANES = 128
_LAYER_ROWS = 248
_LB_ROWS = (A_HEADS * HEAD_DIM) // _PACK_LANES


def _small_reduce(parts, lb_logits, *, name):
    rows = DEPTH * _LAYER_ROWS

    def body(p_ref, lg_ref, o_ref):
        g = p_ref[0]
        for s in range(1, N_DEV):
            g = g + p_ref[s]
        o_ref[...] = g
        lg = lg_ref[...]
        e = jnp.exp(lg - jnp.max(lg, axis=0, keepdims=True))
        p = e / jnp.sum(e, axis=0, keepdims=True)
        d1 = g[_LAYER_ROWS:_LAYER_ROWS + _LB_ROWS, :] * p[0] * p[1]
        o_ref[0:_LB_ROWS, :] = -d1
        o_ref[_LAYER_ROWS:_LAYER_ROWS + _LB_ROWS, :] = d1

    return pl.pallas_call(
        body, out_shape=jax.ShapeDtypeStruct((rows, _PACK_LANES), _F32), name=name,
        compiler_params=_cparams(),
    )(parts, lb_logits.reshape(DEPTH, _LB_ROWS, _PACK_LANES))


def _adamw_small(g, w, m, v, *, name):
    def body(g_ref, w_ref, m_ref, v_ref, d_ref, nm_ref, nv_ref):
        d, nm, nv = _adamw_math(g_ref[...], w_ref[...], m_ref[...], v_ref[...])
        d_ref[...] = d
        nm_ref[...] = nm
        nv_ref[...] = nv

    shape = jax.ShapeDtypeStruct(g.shape, _F32)
    return pl.pallas_call(body, out_shape=[shape] * 3, name=name, compiler_params=_cparams())(g, w, m, v)


def _pack(vectors, rows):
    flat = jnp.concatenate([v.reshape(-1).astype(_F32) for v in vectors])
    return jnp.pad(flat, (0, rows * _PACK_LANES - flat.shape[0])).reshape(rows, _PACK_LANES)


def _unpack(packed, shapes):
    flat = packed.reshape(-1)
    out, at = [], 0
    for s in shapes:
        size = int(np.prod(s))
        out.append(flat[at:at + size].reshape(s))
        at += size
    return out


_BIG = ("w_in", "w_gate", "w_up", "w_out", "w_down")
_COLUMN_SHARDED = ("w_in", "w_gate", "w_up")


def _full_weight(name, g):
    if name == "w_out":
        return g.reshape(D_MODEL, D_MODEL)
    if name == "w_down":
        return g.reshape(D_FF, D_MODEL)
    if name == "conv_w":
        return g.transpose(1, 0, 2).reshape(g.shape[1], N_DEV * SHARD_COLS)
    return g


class _WeightGather:
    def __init__(self, names, first, blocks, lands, sems, tag):
        self.names, self.first, self.blocks, self.lands, self.sems, self.tag = names, first, blocks, lands, sems, tag
        self.forwarded = None

    def forward(self, after):
        f_send, f_recv, self.lands, token = _gather_forward(self.lands, self.sems[2], self.first, after,
                                                            name=f"gather_forward_{self.tag}")
        self.forwarded = (f_send, f_recv)
        return token

    def wait(self, after):
        if self.forwarded is None:
            self.forward(after)
        got = _gather_wait(self.blocks, self.lands, self.sems[0], self.sems[1], self.first, *self.forwarded, after,
                           name=f"gather_wait_{self.tag}")
        return {n: _full_weight(n, g) for n, g in zip(self.names, got)}


def _start_gathers(groups, me):
    blocks = [b for _, _, bs in groups for b in bs]
    landings = [lax.dynamic_update_index_in_dim(lax.empty((N_DEV,) + b.shape, b.dtype), b[None], me, 0) for b in blocks]
    send, d2d, ici, blocks, landings, token = _gather_start(blocks, landings, name="gather_start")
    out, first = [], 0
    for tag, names, bs in groups:
        k = len(bs)
        out.append(_WeightGather(names, first, blocks[first:first + k], landings[first:first + k], (send, d2d, ici), tag))
        first += k
    return out, token


class _LayerWeights:
    def __init__(self, ready, pending=(), forwards=(), tokens=()):
        self.ready, self.pending, self.forwards, self._tokens = dict(ready), list(pending), list(forwards), list(tokens)

    def at(self, point, after):
        for when, gather in self.forwards:
            if when == point:
                self._tokens.append(gather.forward(after))

    def tokens(self):
        out, self._tokens = self._tokens, []
        return out

    def get(self, name, after):
        if name not in self.ready:
            group, = [g for g in self.pending if name in g.names]
            self.ready.update(group.wait(after))
        return self.ready[name]


def _layer_fwd(x, xb, ws, lb_logits, a_norm_w, c_sink, ln1_g, ln1_b, conv_b, ln2_g, ln2_b, tabs, l):
    proj = _mm_w_slabs(xb, ws.get("w_in", xb), tm=1024, after=ws.tokens(), name=f"proj_{l}")
    o_a, raw, states = _hgrn_fwd(proj, lb_logits, a_norm_w, l, name=f"hgrn_fwd_{l}")
    ws.at("hgrn", o_a)
    o_b, lse_b = _band_fwd(proj, tabs, name=f"dilated_fwd_{l}", **_DILATED)
    o_c, lse_c = _band_fwd(proj, tabs, sink=c_sink, name=f"swa_fwd_{l}", **_SWA)
    ws.at("swa", o_c)
    mixed = jnp.concatenate([o_a, o_b, o_c], axis=1).astype(_MXU_DTYPE)
    y = _mm(mixed, ws.get("w_out", mixed), tm=1024, tn=512, after=ws.tokens(), name=f"mix_out_{l}")
    z1, x1, x1b = _ln_fwd(x, y, ln1_g, ln1_b, name=f"ln1_fwd_{l}")
    g = _mm_w_slabs(x1b, ws.get("w_gate", x1b), tm=1024, out_dtype=_ACT_DTYPE, name=f"ffn_gate_{l}")
    u = _mm_w_slabs(x1b, ws.get("w_up", x1b), tm=1024, out_dtype=_ACT_DTYPE, name=f"ffn_up_{l}")
    ws.at("up", u)
    hb = _conv_gate_fwd(g, u, ws.get("conv_w", u), conv_b, name=f"conv_gate_fwd_{l}")
    y2 = _mm(hb, ws.get("w_down", hb), tm=512, tn=512, after=ws.tokens(), name=f"ffn_down_{l}")
    ws.at("down", y2)
    z2, x2, x2b = _ln_fwd(x1, y2, ln2_g, ln2_b, name=f"ln2_fwd_{l}")
    res = dict(xb=xb, proj=proj, raw=raw, states=states, o_b=o_b, lse_b=lse_b, o_c=o_c, lse_c=lse_c,
               mixed=mixed, z1=z1, x1b=x1b, g=g, u=u, hb=hb, z2=z2)
    return x2, x2b, res


class _GradExchange:
    def __init__(self, core, chip):
        self.core, self.chip, self.groups, self.swapping, self._tokens = core, chip, [], [], []

    def launch(self, names, slabs, l, tag, behind):
        mine = [s.reshape((N_CHIP, 2) + s.shape[1:]) for s in slabs]
        if behind:
            landings = [lax.empty((N_CHIP,) + m.shape[2:], m.dtype) for m in mine]
            send, recv, mine, landings, token = _swap_start(mine, landings, name=f"swap_start_{tag}")
            self.swapping.append((names, l, tag, send, recv, mine, landings))
            self._tokens.append(token)
        else:
            self._exchange(names, l, tag, mine, _sibling_swap(mine, name=f"swap_grads_{tag}"))

    def advance(self, after):
        for names, l, tag, send, recv, mine, landings in self.swapping:
            mine, theirs = _swap_wait(mine, landings, send, recv, after, name=f"swap_wait_{tag}")
            self._exchange(names, l, tag, mine, theirs)
        self.swapping = []

    def _exchange(self, names, l, tag, mine, theirs):
        sums = [_pair_add(a, b, self.core, name=f"pair_add_{n}_{l}") for n, a, b in zip(names, mine, theirs)]
        landings = [lax.empty(s.shape, s.dtype) for s in sums]
        send, recv, sums, landings, token = _chip_exchange_start(sums, landings, name=f"exchange_start_{tag}")
        self.groups.append((names, l, tag, send, recv, sums, landings))
        self._tokens.append(token)

    def tokens(self):
        out, self._tokens = self._tokens, []
        return out

    def finish(self, weights, mom1, mom2, after):
        out = {}
        after = list(after) + self.tokens()
        for names, l, tag, send, recv, sums, landings in self.groups:
            sums, landings = _chip_exchange_wait(sums, landings, send, recv, after[-1], name=f"exchange_wait_{tag}")
            for n, s, landed in zip(names, sums, landings):
                out[n] = _adamw_reduce(landed, s, self.chip, weights[n], mom1[n], mom2[n], l, out.get(n), name=f"adamw_{n}_{l}")
                after = [out[n][0]]
        return out


def _layer_bwd(dx2, res, w, lb_logits, a_norm_w, c_sink, ln1_g, conv_b, ln2_g, tabs, exchange, l):
    dz2, dz2b, d_ln2_g, d_ln2_b = _ln_bwd(res["z2"], dx2, None, ln2_g, name=f"ln2_bwd_{l}")
    exchange.advance(dz2b)
    dh = _mm(dz2b, w["w_down"], tb=True, tm=1024, tn=512, out_dtype=_ACT_DTYPE, after=exchange.tokens(),
             name=f"ffn_down_dx_{l}")
    d_w_down = _mm(res["hb"], dz2b, ta=True, tm=512, tn=512, out_dtype=_GRAD_DTYPE, name=f"ffn_down_dw_{l}")
    dg, du, d_conv_w, d_conv_b = _conv_gate_bwd(dh, res["g"], res["u"], w["conv_w"], conv_b, name=f"conv_gate_bwd_{l}")
    t = _mm_nt_w_slabs(dg, w["w_gate"], tm=512, tn=512, name=f"ffn_gate_dx_{l}")
    dx1 = _mm_nt_w_slabs(du, w["w_up"], tm=512, tn=512, add=t, name=f"ffn_up_dx_{l}")
    d_w_gate = _mm_tn_slabs(res["x1b"], dg, tm=1024, name=f"ffn_gate_dw_{l}")
    d_w_up = _mm_tn_slabs(res["x1b"], du, tm=1024, name=f"ffn_up_dw_{l}")
    dz1, dz1b, d_ln1_g, d_ln1_b = _ln_bwd(res["z1"], dx1, dz2, ln1_g, name=f"ln1_bwd_{l}")
    d_w_out = _mm(res["mixed"], dz1b, ta=True, tm=1024, tn=512, out_dtype=_GRAD_DTYPE, name=f"mix_out_dw_{l}")
    exchange.launch(("w_down", "w_gate", "w_up", "w_out"),
                    [d_w_down.reshape(N_DEV, D_FF // N_DEV, D_MODEL), d_w_gate, d_w_up,
                     d_w_out.reshape(N_DEV, D_MODEL // N_DEV, D_MODEL)], l, f"ffn_{l}", True)
    dmixed = _mm(dz1b, w["w_out"], tb=True, tm=1024, tn=512, after=exchange.tokens(), name=f"mix_out_dx_{l}")
    dq_a, df_a, di_a, dg_a, d_norm_w, d_lb = _hgrn_bwd(res["proj"], lb_logits, a_norm_w, res["raw"], res["states"],
                                                      dmixed, l, name=f"hgrn_bwd_{l}")
    exchange.advance(dq_a)
    dq_b, dk_b, dv_b = _band_bwd(res["proj"], tabs, dmixed, res["o_b"], res["lse_b"], do0=A_HEADS, after=exchange.tokens(),
                                 name=f"dilated_bwd_{l}", **_DILATED)
    dq_c, dk_c, dv_c, d_sink = _band_bwd(res["proj"], tabs, dmixed, res["o_c"], res["lse_c"], do0=A_HEADS + B_HEADS,
                                         sink=c_sink, name=f"swa_bwd_{l}", **_SWA)
    dproj = jnp.concatenate([dq_a, df_a, di_a, dg_a, dq_b, dk_b, dv_b, dq_c, dk_c, dv_c], axis=1)
    d_w_in = _mm_tn_slabs(res["xb"], dproj, tm=1024, name=f"proj_dw_{l}")
    exchange.launch(("w_in",), [d_w_in], l, f"mix_{l}", l > 0)
    dx = _mm_nt_w_slabs(dproj, w["w_in"], tm=512, tn=512, add=dz1, add_scale=ALPHA, after=exchange.tokens(),
                        name=f"proj_dx_{l}")
    small = [d_lb, d_norm_w, jnp.pad(d_sink, (0, _PACK_LANES - C_HEADS)), d_ln1_g, d_ln1_b, d_ln2_g, d_ln2_b, d_conv_b,
             d_conv_w]
    return dx, small


def kernel(x, w_in, lb_logits, a_norm_w, c_sinks, w_out, ln1_g, ln1_b, w_gate, w_up, conv_w, conv_b, w_down, ln2_g, ln2_b, loss_target, m_w_in, m_lb_logits, m_a_norm_w, m_c_sinks, m_w_out, m_ln1_g, m_ln1_b, m_w_gate, m_w_up, m_conv_w, m_conv_b, m_w_down, m_ln2_g, m_ln2_b, v_w_in, v_lb_logits, v_a_norm_w, v_c_sinks, v_w_out, v_ln1_g, v_ln1_b, v_w_gate, v_w_up, v_conv_w, v_conv_b, v_w_down, v_ln2_g, v_ln2_b):
    weights = dict(w_in=w_in, lb_logits=lb_logits, a_norm_w=a_norm_w, c_sinks=c_sinks, w_out=w_out, ln1_g=ln1_g, ln1_b=ln1_b,
                   w_gate=w_gate, w_up=w_up, conv_w=conv_w, conv_b=conv_b, w_down=w_down, ln2_g=ln2_g, ln2_b=ln2_b)
    mom1 = dict(w_in=m_w_in, lb_logits=m_lb_logits, a_norm_w=m_a_norm_w, c_sinks=m_c_sinks, w_out=m_w_out, ln1_g=m_ln1_g,
                ln1_b=m_ln1_b, w_gate=m_w_gate, w_up=m_w_up, conv_w=m_conv_w, conv_b=m_conv_b, w_down=m_w_down, ln2_g=m_ln2_g,
                ln2_b=m_ln2_b)
    mom2 = dict(w_in=v_w_in, lb_logits=v_lb_logits, a_norm_w=v_a_norm_w, c_sinks=v_c_sinks, w_out=v_w_out, ln1_g=v_ln1_g,
                ln1_b=v_ln1_b, w_gate=v_w_gate, w_up=v_w_up, conv_w=v_conv_w, conv_b=v_conv_b, w_down=v_w_down, ln2_g=v_ln2_g,
                ln2_b=v_ln2_b)
    core = lax.axis_index("c").astype(jnp.int32)
    me = 4 * lax.axis_index("x") + 2 * lax.axis_index("y") + core
    tabs = _rope_tables()

    chip = (2 * lax.axis_index("x") + lax.axis_index("y")).astype(jnp.int32)

    def block(n, l):
        return conv_w[l] if n == "conv_w" else weights[n][l].astype(_MXU_DTYPE)

    first, = _gather_blocks([block("w_in", 0)], name="gather_w_in_0")
    order = [(("w_out",), 0), (("w_gate", "w_up", "conv_w"), 0), (("w_down",), 0),
             (("w_in",), 1), (("w_out",), 1), (("w_gate", "w_up", "conv_w"), 1), (("w_down",), 1)]
    gathers, started = _start_gathers([(f"{names[0]}_{l}", names, [block(n, l) for n in names]) for names, l in order], me)
    out0, ffn0, down0, in1, out1, ffn1, down1 = gathers
    layer_ws = [_LayerWeights({"w_in": _full_weight("w_in", first)}, [out0, ffn0, down0],
                              [("hgrn", out0), ("swa", ffn0), ("up", down0), ("down", in1)], [started]),
                _LayerWeights({}, [in1, out1, ffn1, down1], [("hgrn", out1), ("swa", ffn1), ("up", down1)])]

    xs = x[0]
    xb = xs.astype(_MXU_DTYPE)
    saved = []
    for l in range(DEPTH):
        xs, xb, res = _layer_fwd(xs, xb, layer_ws[l], lb_logits, a_norm_w[l], c_sinks[l], ln1_g[l], ln1_b[l], conv_b[l],
                                 ln2_g[l], ln2_b[l], tabs, l)
        saved.append(res)
    loss_part, dx = _loss_head(xs, loss_target[0], name="loss_head")
    loss = lax.psum(loss_part, ("x", "y", "c"))

    exchange = _GradExchange(core, chip)
    small_parts = [None] * DEPTH
    for l in reversed(range(DEPTH)):
        dx, small = _layer_bwd(dx, saved[l], layer_ws[l].ready, lb_logits, a_norm_w[l], c_sinks[l], ln1_g[l], conv_b[l],
                               ln2_g[l], tabs, exchange, l)
        small_parts[l] = _pack(small, _LAYER_ROWS)
    def as_slabs(d):
        return {n: jnp.swapaxes(d[n], 1, 2) if n in _COLUMN_SHARDED else d[n] for n in _BIG}

    updated = exchange.finish(as_slabs(weights), as_slabs(mom1), as_slabs(mom2), [dx])
    updated = {n: tuple(jnp.swapaxes(t, 1, 2) for t in u) if n in _COLUMN_SHARDED else u for n, u in updated.items()}
    gathered, = _exchange([jnp.concatenate(small_parts, axis=0)], False, name="gather_small_grads")
    g_small = _small_reduce(gathered, lb_logits, name="small_grads")

    per_layer = [(A_HEADS * HEAD_DIM,), (HEAD_DIM,), (_PACK_LANES,), (D_MODEL,), (D_MODEL,), (D_MODEL,), (D_MODEL,), (D_FF,),
                 (3, D_FF)]
    names = ("lb_logits", "a_norm_w", "c_sinks", "ln1_g", "ln1_b", "ln2_g", "ln2_b", "conv_b", "conv_w")
    grads = {n: [] for n in names}
    for l in range(DEPTH):
        for n, t in zip(names, _unpack(g_small[l * _LAYER_ROWS:(l + 1) * _LAYER_ROWS], per_layer)):
            grads[n].append(t)
    grads = {n: jnp.stack(t) for n, t in grads.items()}
    grads["c_sinks"] = grads["c_sinks"][:, :C_HEADS]
    grads["conv_w"] = lax.dynamic_slice_in_dim(grads["conv_w"], me * SHARD_COLS, SHARD_COLS, axis=2)
    shapes = [grads[n].shape for n in names]
    rows = -(-sum(int(np.prod(s)) for s in shapes) // (8 * _PACK_LANES)) * 8
    d_s, m_s, v_s = _adamw_small(_pack([grads[n] for n in names], rows), _pack([weights[n] for n in names], rows),
                                 _pack([mom1[n] for n in names], rows), _pack([mom2[n] for n in names], rows),
                                 name="adamw_small")
    delta = dict(zip(names, _unpack(d_s, shapes)))
    new_m = dict(zip(names, _unpack(m_s, shapes)))
    new_v = dict(zip(names, _unpack(v_s, shapes)))
    for n in _BIG:
        grads[n], delta[n], new_m[n], new_v[n] = updated[n]

    order = ("w_in", "lb_logits", "a_norm_w", "c_sinks", "w_out", "ln1_g", "ln1_b", "w_gate", "w_up", "conv_w", "conv_b",
             "w_down", "ln2_g", "ln2_b")
    return (loss, dx[None], *[grads[n] for n in order], *[delta[n] for n in order], *[new_m[n] for n in order],
            *[new_v[n] for n in order])
```

```python
import functools

import jax
import jax.numpy as jnp
import numpy as np
from jax import lax
from jax.experimental import pallas as pl
from jax.experimental.pallas import tpu as pltpu

D_MODEL = 2048
SEQ = 2048
DEPTH = 2
HEAD_DIM = 128
A_HEADS = 4
B_HEADS = 6
C_HEADS = 6
C_KV_HEADS = 2
A_CHUNK = 16
DILATIONS = (1, 4, 16)
BLOCK = 128
ROPE_THETA = 500000.0
ROPE_DIM = 32
D_FF = 5632
IN_WIDTH = 5632
LN_EPS = 1e-5
ALPHA = (2 * DEPTH) ** 0.25
N_DEV = 8
SHARD_COLS = IN_WIDTH // N_DEV

ADAM_LR = 0.001
ADAM_B1 = 0.9
ADAM_B2 = 0.999
ADAM_EPS = 1e-08
ADAM_WD = 0.01
ADAM_STEP = 10

A_COLS = 16
QKV_COLS = 28
QB0, KB0, VB0, QC0, KC0, VC0 = 0, 6, 12, 18, 24, 26

_MXU_DTYPE = jnp.bfloat16
_GRAD_DTYPE = jnp.bfloat16
_ACT_DTYPE = jnp.bfloat16
_NEG = -1e30
_VMEM_LIMIT = 56 * 2 ** 20

_F32 = jnp.float32


def _sigmoid(x):
    return 1.0 / (1.0 + jnp.exp(-x))


def _cparams(**kw):
    return pltpu.CompilerParams(vmem_limit_bytes=_VMEM_LIMIT, **kw)


def _mm(a, b, *, ta=False, tb=False, tm, tn, out_dtype=_F32, add=None, add_scale=1.0, after=(), name):
    K = a.shape[0] if ta else a.shape[1]
    M = a.shape[1] if ta else a.shape[0]
    N = b.shape[0] if tb else b.shape[1]
    assert (b.shape[1] if tb else b.shape[0]) == K and M % tm == 0 and N % tn == 0
    dn = (((0 if ta else 1,), (1 if tb else 0,)), ((), ()))

    def body(*refs):
        a_ref, b_ref = refs[:2]
        o_ref = refs[-1]
        r = lax.dot_general(a_ref[...], b_ref[...], dn, preferred_element_type=_F32)
        if add is not None:
            r = r + add_scale * refs[2][...]
        o_ref[...] = r.astype(o_ref.dtype)

    a_spec = pl.BlockSpec((K, tm), lambda i, j: (0, i)) if ta else pl.BlockSpec((tm, K), lambda i, j: (i, 0))
    b_spec = pl.BlockSpec((tn, K), lambda i, j: (j, 0)) if tb else pl.BlockSpec((K, tn), lambda i, j: (0, j))
    o_spec = pl.BlockSpec((tm, tn), lambda i, j: (i, j))
    in_specs = [a_spec, b_spec] + ([o_spec] if add is not None else []) + [pl.BlockSpec(memory_space=pl.ANY)] * len(after)
    args = (a, b) + ((add,) if add is not None else ()) + tuple(after)
    return pl.pallas_call(
        body, grid=(M // tm, N // tn), in_specs=in_specs, out_specs=o_spec,
        out_shape=jax.ShapeDtypeStruct((M, N), out_dtype), name=name,
        compiler_params=_cparams(dimension_semantics=("parallel", "parallel")),
    )(*args)


_PAIR = 2 * SHARD_COLS


def _mm_tn_slabs(a, b, *, tm, name):
    K, M = a.shape
    assert b.shape == (K, N_DEV * SHARD_COLS) and M % tm == 0

    def body(a_ref, b_ref, o_ref):
        a_blk = a_ref[...]
        for s in range(2):
            o_ref[s] = lax.dot_general(b_ref[:, s * SHARD_COLS:(s + 1) * SHARD_COLS], a_blk, _TN,
                                       preferred_element_type=_F32).astype(o_ref.dtype)

    return pl.pallas_call(
        body, grid=(M // tm, N_DEV // 2),
        in_specs=[pl.BlockSpec((K, tm), lambda i, p: (0, i)), pl.BlockSpec((K, _PAIR), lambda i, p: (0, p))],
        out_specs=pl.BlockSpec((2, SHARD_COLS, tm), lambda i, p: (p, 0, i)),
        out_shape=jax.ShapeDtypeStruct((N_DEV, SHARD_COLS, M), _GRAD_DTYPE), name=name,
        compiler_params=_cparams(dimension_semantics=("parallel", "parallel")),
    )(a, b)


def _cast_layer(w, layer, *, after=(), name):
    _, R, C = w.shape
    tr = max(t for t in range(16, R + 1, 16) if R % t == 0 and t * C <= 512 * 1024)

    def body(w_ref, *rest):
        o_ref = rest[-1]
        o_ref[...] = w_ref[...].astype(o_ref.dtype)

    return pl.pallas_call(
        body, grid=(R // tr,),
        in_specs=[pl.BlockSpec((None, tr, C), lambda i: (layer, i, 0))] + [pl.BlockSpec(memory_space=pl.ANY)] * len(after),
        out_specs=pl.BlockSpec((tr, C), lambda i: (i, 0)), out_shape=jax.ShapeDtypeStruct((R, C), _MXU_DTYPE), name=name,
        compiler_params=_cparams(dimension_semantics=("parallel",)),
    )(w, *after)


def _mm_w_slabs(a, w, *, tm, out_dtype=_F32, after=(), name):
    M, K = a.shape
    assert w.shape == (N_DEV, SHARD_COLS, K) and M % tm == 0

    def body(a_ref, w_ref, *rest):
        o_ref = rest[-1]
        a_blk = a_ref[...]
        for s in range(2):
            o_ref[:, s * SHARD_COLS:(s + 1) * SHARD_COLS] = lax.dot_general(
                a_blk, w_ref[s], _NT, preferred_element_type=_F32).astype(o_ref.dtype)

    return pl.pallas_call(
        body, grid=(M // tm, N_DEV // 2),
        in_specs=[pl.BlockSpec((tm, K), lambda i, p: (i, 0)), pl.BlockSpec((2, SHARD_COLS, K), lambda i, p: (p, 0, 0))]
        + [pl.BlockSpec(memory_space=pl.ANY)] * len(after),
        out_specs=pl.BlockSpec((tm, _PAIR), lambda i, p: (i, p)),
        out_shape=jax.ShapeDtypeStruct((M, N_DEV * SHARD_COLS), out_dtype), name=name,
        compiler_params=_cparams(dimension_semantics=("parallel", "parallel")),
    )(a, w, *after)


def _mm_nt_w_slabs(a, w, *, tm, tn, add=None, add_scale=1.0, after=(), name):
    M = a.shape[0]
    N = w.shape[2]
    assert a.shape[1] == N_DEV * SHARD_COLS and w.shape[:2] == (N_DEV, SHARD_COLS) and M % tm == 0 and N % tn == 0

    def body(a_ref, w_ref, *rest):
        o_ref = rest[-1]
        acc = add_scale * rest[0][...] if add is not None else None
        for j in range(N_DEV):
            t = jnp.dot(a_ref[:, j * SHARD_COLS:(j + 1) * SHARD_COLS], w_ref[j], preferred_element_type=_F32)
            acc = t if acc is None else acc + t
        o_ref[...] = acc

    o_spec = pl.BlockSpec((tm, tn), lambda i, j: (i, j))
    return pl.pallas_call(
        body, grid=(M // tm, N // tn),
        in_specs=[pl.BlockSpec((tm, N_DEV * SHARD_COLS), lambda i, j: (i, 0)),
                  pl.BlockSpec((N_DEV, SHARD_COLS, tn), lambda i, j: (0, 0, j))]
        + ([o_spec] if add is not None else []) + [pl.BlockSpec(memory_space=pl.ANY)] * len(after),
        out_specs=o_spec, out_shape=jax.ShapeDtypeStruct((M, N), _F32), name=name,
        compiler_params=_cparams(dimension_semantics=("parallel", "parallel")),
    )(a, w, *((add,) if add is not None else ()), *after)


def _ln_fwd(x, y, g, b, *, name):
    tm = 256

    def body(x_ref, y_ref, g_ref, b_ref, z_ref, o_ref, ob_ref):
        z = ALPHA * x_ref[...] + y_ref[...]
        mu = jnp.mean(z, axis=-1, keepdims=True)
        zc = z - mu
        var = jnp.mean(zc * zc, axis=-1, keepdims=True)
        o = zc * lax.rsqrt(var + LN_EPS) * g_ref[...] + b_ref[...]
        z_ref[...] = z
        o_ref[...] = o
        ob_ref[...] = o.astype(ob_ref.dtype)

    row = pl.BlockSpec((tm, D_MODEL), lambda i: (i, 0))
    vec = pl.BlockSpec((1, D_MODEL), lambda i: (0, 0))
    return pl.pallas_call(
        body, grid=(SEQ // tm,), in_specs=[row, row, vec, vec], out_specs=[row, row, row],
        out_shape=[jax.ShapeDtypeStruct((SEQ, D_MODEL), _F32), jax.ShapeDtypeStruct((SEQ, D_MODEL), _F32),
                   jax.ShapeDtypeStruct((SEQ, D_MODEL), _MXU_DTYPE)],
        name=name, compiler_params=_cparams(dimension_semantics=("parallel",)),
    )(x, y, g.reshape(1, D_MODEL), b.reshape(1, D_MODEL))


def _ln_bwd(z, d_a, d_res, g, *, name):
    tm = 256

    def body(*refs):
        if d_res is None:
            z_ref, da_ref, g_ref, dz_ref, dzb_ref, dg_ref, db_ref = refs
            dout = da_ref[...]
        else:
            z_ref, da_ref, dr_ref, g_ref, dz_ref, dzb_ref, dg_ref, db_ref = refs
            dout = da_ref[...] + ALPHA * dr_ref[...]
        z = z_ref[...]
        mu = jnp.mean(z, axis=-1, keepdims=True)
        zc = z - mu
        var = jnp.mean(zc * zc, axis=-1, keepdims=True)
        rstd = lax.rsqrt(var + LN_EPS)
        xh = zc * rstd
        dxh = dout * g_ref[...]
        m1 = jnp.mean(dxh, axis=-1, keepdims=True)
        m2 = jnp.mean(dxh * xh, axis=-1, keepdims=True)
        dz = rstd * (dxh - m1 - xh * m2)
        dz_ref[...] = dz
        dzb_ref[...] = dz.astype(dzb_ref.dtype)

        @pl.when(pl.program_id(0) == 0)
        def _():
            dg_ref[...] = jnp.zeros_like(dg_ref)
            db_ref[...] = jnp.zeros_like(db_ref)

        dg_ref[0:1, :] += jnp.sum(dout * xh, axis=0, keepdims=True)
        db_ref[0:1, :] += jnp.sum(dout, axis=0, keepdims=True)

    row = pl.BlockSpec((tm, D_MODEL), lambda i: (i, 0))
    vec = pl.BlockSpec((1, D_MODEL), lambda i: (0, 0))
    acc = pl.BlockSpec((8, D_MODEL), lambda i: (0, 0))
    ins = [z, d_a] + ([d_res] if d_res is not None else []) + [g.reshape(1, D_MODEL)]
    in_specs = [row, row] + ([row] if d_res is not None else []) + [vec]
    dz, dzb, dg, db = pl.pallas_call(
        body, grid=(SEQ // tm,), in_specs=in_specs, out_specs=[row, row, acc, acc],
        out_shape=[jax.ShapeDtypeStruct((SEQ, D_MODEL), _F32), jax.ShapeDtypeStruct((SEQ, D_MODEL), _MXU_DTYPE),
                   jax.ShapeDtypeStruct((8, D_MODEL), _F32), jax.ShapeDtypeStruct((8, D_MODEL), _F32)],
        name=name, compiler_params=_cparams(dimension_semantics=("arbitrary",)),
    )(*ins)
    return dz, dzb, dg[0], db[0]


def _loss_head(y, target, *, name):
    tm = 256

    def body(y_ref, t_ref, d_ref, l_ref):
        e = y_ref[...] - t_ref[...]
        d_ref[...] = e * (1.0 / D_MODEL)

        @pl.when(pl.program_id(0) == 0)
        def _():
            l_ref[...] = jnp.zeros_like(l_ref)

        l_ref[...] += (0.5 / D_MODEL) * jnp.sum(e * e)

    row = pl.BlockSpec((tm, D_MODEL), lambda i: (i, 0))
    d, l = pl.pallas_call(
        body, grid=(SEQ // tm,), in_specs=[row, row], out_specs=[row, pl.BlockSpec((8, 128), lambda i: (0, 0))],
        out_shape=[jax.ShapeDtypeStruct((SEQ, D_MODEL), _F32), jax.ShapeDtypeStruct((8, 128), _F32)],
        name=name, compiler_params=_cparams(dimension_semantics=("arbitrary",)),
    )(y, target)
    return l[0, 0], d


_CONV_TN = 256


def _shift_down(v, k, rows):
    return jnp.where(rows >= k, pltpu.roll(v, k, axis=0), 0.0)


def _shift_up(v, k, rows):
    return jnp.where(rows < SEQ - k, pltpu.roll(v, SEQ - k, axis=0), 0.0)


def _conv_gate_fwd(g, u, conv_w, conv_b, *, name):
    def body(g_ref, u_ref, w_ref, b_ref, h_ref):
        gv = g_ref[...].astype(_F32)
        rows = lax.broadcasted_iota(jnp.int32, gv.shape, 0)
        w = w_ref[...]
        gc = b_ref[...] + w[2:3, :] * gv + w[1:2, :] * _shift_down(gv, 1, rows) + w[0:1, :] * _shift_down(gv, 2, rows)
        h_ref[...] = (gc * _sigmoid(gc) * u_ref[...].astype(_F32)).astype(h_ref.dtype)

    col = pl.BlockSpec((SEQ, _CONV_TN), lambda j: (0, j))
    return pl.pallas_call(
        body, grid=(D_FF // _CONV_TN,),
        in_specs=[col, col, pl.BlockSpec((3, _CONV_TN), lambda j: (0, j)), pl.BlockSpec((1, _CONV_TN), lambda j: (0, j))],
        out_specs=col, out_shape=jax.ShapeDtypeStruct((SEQ, D_FF), _MXU_DTYPE), name=name,
        compiler_params=_cparams(dimension_semantics=("parallel",)),
    )(g, u, conv_w, conv_b.reshape(1, D_FF))


def _conv_gate_bwd(dh, g, u, conv_w, conv_b, *, name):
    def body(dh_ref, g_ref, u_ref, w_ref, b_ref, dg_ref, du_ref, dw_ref, db_ref):
        gv = g_ref[...].astype(_F32)
        rows = lax.broadcasted_iota(jnp.int32, gv.shape, 0)
        w = w_ref[...]
        g1 = _shift_down(gv, 1, rows)
        g2 = _shift_down(gv, 2, rows)
        gc = b_ref[...] + w[2:3, :] * gv + w[1:2, :] * g1 + w[0:1, :] * g2
        sg = _sigmoid(gc)
        dh = dh_ref[...].astype(_F32)
        du_ref[...] = (dh * (gc * sg)).astype(du_ref.dtype)
        dgc = dh * u_ref[...].astype(_F32) * (sg * (1.0 + gc * (1.0 - sg)))
        dg = w[2:3, :] * dgc + w[1:2, :] * _shift_up(dgc, 1, rows) + w[0:1, :] * _shift_up(dgc, 2, rows)
        dg_ref[...] = dg.astype(dg_ref.dtype)
        dw_ref[0:1, :] = jnp.sum(dgc * g2, axis=0, keepdims=True)
        dw_ref[1:2, :] = jnp.sum(dgc * g1, axis=0, keepdims=True)
        dw_ref[2:3, :] = jnp.sum(dgc * gv, axis=0, keepdims=True)
        db_ref[...] = jnp.sum(dgc, axis=0, keepdims=True)

    col = pl.BlockSpec((SEQ, _CONV_TN), lambda j: (0, j))
    w3 = pl.BlockSpec((3, _CONV_TN), lambda j: (0, j))
    w1 = pl.BlockSpec((1, _CONV_TN), lambda j: (0, j))
    dg, du, dw, db = pl.pallas_call(
        body, grid=(D_FF // _CONV_TN,), in_specs=[col, col, col, w3, w1], out_specs=[col, col, w3, w1],
        out_shape=[jax.ShapeDtypeStruct((SEQ, D_FF), _MXU_DTYPE), jax.ShapeDtypeStruct((SEQ, D_FF), _MXU_DTYPE),
                   jax.ShapeDtypeStruct((3, D_FF), _F32), jax.ShapeDtypeStruct((1, D_FF), _F32)],
        name=name, compiler_params=_cparams(dimension_semantics=("parallel",)),
    )(dh, g, u, conv_w, conv_b.reshape(1, D_FF))
    return dg, du, dw, db[0]


def _rope_tables():
    half = ROPE_DIM // 2
    inv = ROPE_THETA ** (-jnp.arange(0, ROPE_DIM, 2, dtype=_F32) / ROPE_DIM)
    ang = jnp.arange(SEQ, dtype=_F32)[:, None] * inv[None, :]
    cos, sin = jnp.cos(ang), jnp.sin(ang)
    rest = HEAD_DIM - ROPE_DIM
    c = jnp.concatenate([cos, cos, jnp.ones((SEQ, rest), _F32)], axis=1)
    s1 = jnp.concatenate([-sin, jnp.zeros((SEQ, HEAD_DIM - half), _F32)], axis=1)
    s2 = jnp.concatenate([jnp.zeros((SEQ, half), _F32), sin, jnp.zeros((SEQ, rest), _F32)], axis=1)
    return c, s1, s2


def _rope_apply(x, c, s1, s2):
    return x * c + pltpu.roll(x, HEAD_DIM - ROPE_DIM // 2, axis=1) * s1 + pltpu.roll(x, ROPE_DIM // 2, axis=1) * s2


def _rope_transpose(d, c, s1, s2):
    half = ROPE_DIM // 2
    return d * c + pltpu.roll(d * s1, half, axis=1) + pltpu.roll(d * s2, HEAD_DIM - half, axis=1)


_NT = (((1,), (1,)), ((), ()))
_TN = (((0,), (0,)), ((), ()))
_SCALE = HEAD_DIM ** -0.5


def _band_scores(q, k2, n, lag_off):
    s = lax.dot_general(q, k2, _NT, preferred_element_type=_F32) * _SCALE
    row = lax.broadcasted_iota(jnp.int32, (BLOCK, 2 * BLOCK), 0)
    col = lax.broadcasted_iota(jnp.int32, (BLOCK, 2 * BLOCK), 1)
    front = (col >= row + lag_off) & (col < BLOCK) & (n > 0)
    own = (col >= BLOCK) & (col <= row + BLOCK)
    return jnp.where(front | own, s, _NEG)


_BAND_STEPS = SEQ // BLOCK


def _rows(start, d):
    if d == 1:
        return pl.ds(pl.multiple_of(start, BLOCK), BLOCK)
    return pl.ds(start, BLOCK, stride=d)


def _band_block(it, d):
    r, n = it % d, it // d
    span = BLOCK * d
    return n, _rows(r + n * span, d), _rows(r + jnp.maximum(n - 1, 0) * span, d)


def _band_fwd(proj, tabs, *, kv_heads, q_per_kv, q0, k0, v0, dilations, lag_off, sink, name):
    heads = kv_heads * q_per_kv

    def body(*refs):
        q_refs = refs[:q_per_kv]
        k_ref, v_ref, c_ref, s1_ref, s2_ref = refs[q_per_kv:q_per_kv + 5]
        rest = refs[q_per_kv + 5:]
        if sink is not None:
            sk_ref, rest = rest[0], rest[1:]
        o_ref, lse_ref, qs, ks, m_s, l_s, acc_s = rest
        c, s1, s2 = c_ref[...], s1_ref[...], s2_ref[...]
        ks[...] = _rope_apply(k_ref[...], c, s1, s2)
        for i in range(q_per_kv):
            qs[...] = _rope_apply(q_refs[i][...], c, s1, s2)
            for pi, d in enumerate(dilations):
                def step(it, carry, d=d, first=(pi == 0)):
                    n, cur, prev = _band_block(it, d)
                    q = qs[cur, :].astype(_MXU_DTYPE)
                    k2 = jnp.concatenate([ks[prev, :], ks[cur, :]], axis=0).astype(_MXU_DTYPE)
                    v2 = jnp.concatenate([v_ref[prev, :], v_ref[cur, :]], axis=0).astype(_MXU_DTYPE)
                    s = _band_scores(q, k2, n, lag_off)
                    m_b = jnp.max(s, axis=1, keepdims=True)
                    m_new = m_b if first else jnp.maximum(m_b, m_s[cur, :][:, 0:1])
                    p = jnp.exp(s - m_new)
                    l_new = jnp.sum(p, axis=1, keepdims=True)
                    acc = jnp.dot(p.astype(_MXU_DTYPE), v2, preferred_element_type=_F32)
                    if not first:
                        a = jnp.exp(m_s[cur, :][:, 0:1] - m_new)
                        l_new = l_new + a * l_s[cur, :][:, 0:1]
                        acc = acc + a * acc_s[cur, :]
                    m_s[cur, :] = jnp.broadcast_to(m_new, (BLOCK, HEAD_DIM))
                    l_s[cur, :] = jnp.broadcast_to(l_new, (BLOCK, HEAD_DIM))
                    acc_s[cur, :] = acc
                    return carry

                lax.fori_loop(0, _BAND_STEPS, step, 0, unroll=4)
            m, den = m_s[...], l_s[...]
            if sink is not None:
                sk = sk_ref[i]
                m_f = jnp.maximum(m, sk)
                a = jnp.exp(m - m_f)
                den = den * a + jnp.exp(sk - m_f)
                o = acc_s[...] * a / den
                m = m_f
            else:
                o = acc_s[...] / den
            o_ref[:, i * HEAD_DIM:(i + 1) * HEAD_DIM] = o
            lse_ref[:, i * HEAD_DIM:(i + 1) * HEAD_DIM] = m + jnp.log(den)

    col = (SEQ, HEAD_DIM)
    in_specs = [pl.BlockSpec(col, functools.partial(lambda g, i: (0, A_COLS + q0 + g * q_per_kv + i), i=i)) for i in range(q_per_kv)]
    in_specs += [pl.BlockSpec(col, lambda g: (0, A_COLS + k0 + g)), pl.BlockSpec(col, lambda g: (0, A_COLS + v0 + g))]
    in_specs += [pl.BlockSpec(col, lambda g: (0, 0))] * 3
    args = [proj] * (q_per_kv + 2) + list(tabs)
    if sink is not None:
        in_specs.append(pl.BlockSpec((q_per_kv, 1, HEAD_DIM), lambda g: (g, 0, 0)))
        args.append(jnp.broadcast_to(sink.reshape(heads, 1, 1), (heads, 1, HEAD_DIM)))
    o_spec = pl.BlockSpec((SEQ, q_per_kv * HEAD_DIM), lambda g: (0, g))
    shape = jax.ShapeDtypeStruct((SEQ, heads * HEAD_DIM), _F32)
    return pl.pallas_call(
        body, grid=(kv_heads,), in_specs=in_specs, out_specs=[o_spec, o_spec], out_shape=[shape, shape],
        scratch_shapes=[pltpu.VMEM(col, _F32)] * 5, name=name,
        compiler_params=_cparams(dimension_semantics=("parallel",)),
    )(*args)


def _band_bwd(proj, tabs, dmixed, o, lse, *, kv_heads, q_per_kv, q0, k0, v0, do0, dilations, lag_off, sink, after=(), name):
    heads = kv_heads * q_per_kv

    def body(*refs):
        q_refs = refs[:q_per_kv]
        k_ref, v_ref, c_ref, s1_ref, s2_ref = refs[q_per_kv:q_per_kv + 5]
        do_refs = refs[q_per_kv + 5:2 * q_per_kv + 5]
        o_ref, lse_ref = refs[2 * q_per_kv + 5:2 * q_per_kv + 7]
        rest = refs[2 * q_per_kv + 7:]
        if sink is not None:
            sk_ref, rest = rest[0], rest[1:]
            dq_ref, dk_ref, dv_ref, dsk_ref, qs, ks, dq_s, dk_s, dv_s = rest[len(after):]
        else:
            dq_ref, dk_ref, dv_ref, qs, ks, dq_s, dk_s, dv_s = rest[len(after):]
        c, s1, s2 = c_ref[...], s1_ref[...], s2_ref[...]
        ks[...] = _rope_apply(k_ref[...], c, s1, s2)
        dk_s[...] = jnp.zeros_like(dk_s)
        dv_s[...] = jnp.zeros_like(dv_s)
        for i in range(q_per_kv):
            hs = slice(i * HEAD_DIM, (i + 1) * HEAD_DIM)
            qs[...] = _rope_apply(q_refs[i][...], c, s1, s2)
            dq_s[...] = jnp.zeros_like(dq_s)
            do_ref = do_refs[i]
            for d in dilations:
                def step(it, carry, d=d, do_ref=do_ref, hs=hs):
                    n, cur, prev = _band_block(it, d)
                    q = qs[cur, :].astype(_MXU_DTYPE)
                    k2 = jnp.concatenate([ks[prev, :], ks[cur, :]], axis=0).astype(_MXU_DTYPE)
                    v2 = jnp.concatenate([v_ref[prev, :], v_ref[cur, :]], axis=0).astype(_MXU_DTYPE)
                    do = do_ref[cur, :]
                    delta = jnp.sum(do * o_ref[cur, hs], axis=1, keepdims=True)
                    lse_c = lse_ref[cur, hs][:, 0:1]
                    p = jnp.exp(_band_scores(q, k2, n, lag_off) - lse_c)
                    dob = do.astype(_MXU_DTYPE)
                    ds = (p * (lax.dot_general(dob, v2, _NT, preferred_element_type=_F32) - delta) * _SCALE).astype(_MXU_DTYPE)
                    dq_s[cur, :] += jnp.dot(ds, k2, preferred_element_type=_F32)
                    dk2 = lax.dot_general(ds, q, _TN, preferred_element_type=_F32)
                    dv2 = lax.dot_general(p.astype(_MXU_DTYPE), dob, _TN, preferred_element_type=_F32)
                    dk_s[prev, :] += dk2[:BLOCK]
                    dv_s[prev, :] += dv2[:BLOCK]
                    dk_s[cur, :] += dk2[BLOCK:]
                    dv_s[cur, :] += dv2[BLOCK:]
                    return carry

                lax.fori_loop(0, _BAND_STEPS, step, 0, unroll=4)
            dq_ref[:, hs] = _rope_transpose(dq_s[...], c, s1, s2).astype(dq_ref.dtype)
            if sink is not None:
                delta = jnp.sum(do_ref[...] * o_ref[:, hs], axis=1, keepdims=True)
                w_sink = jnp.exp(sk_ref[i] - lse_ref[:, hs])
                dsk_ref[i] = jnp.broadcast_to(jnp.sum(-delta * w_sink[:, 0:1]), (8, HEAD_DIM))
        dk_ref[...] = _rope_transpose(dk_s[...], c, s1, s2).astype(dk_ref.dtype)
        dv_ref[...] = dv_s[...].astype(dv_ref.dtype)

    col = (SEQ, HEAD_DIM)
    in_specs = [pl.BlockSpec(col, functools.partial(lambda g, i: (0, A_COLS + q0 + g * q_per_kv + i), i=i)) for i in range(q_per_kv)]
    in_specs += [pl.BlockSpec(col, lambda g: (0, A_COLS + k0 + g)), pl.BlockSpec(col, lambda g: (0, A_COLS + v0 + g))]
    in_specs += [pl.BlockSpec(col, lambda g: (0, 0))] * 3
    in_specs += [pl.BlockSpec(col, functools.partial(lambda g, i: (0, do0 + g * q_per_kv + i), i=i)) for i in range(q_per_kv)]
    wide = pl.BlockSpec((SEQ, q_per_kv * HEAD_DIM), lambda g: (0, g))
    in_specs += [wide, wide]
    args = [proj] * (q_per_kv + 2) + list(tabs) + [dmixed] * q_per_kv + [o, lse]
    out_specs = [wide, pl.BlockSpec(col, lambda g: (0, g)), pl.BlockSpec(col, lambda g: (0, g))]
    out_shape = [jax.ShapeDtypeStruct((SEQ, heads * HEAD_DIM), _MXU_DTYPE), jax.ShapeDtypeStruct((SEQ, kv_heads * HEAD_DIM), _MXU_DTYPE),
                 jax.ShapeDtypeStruct((SEQ, kv_heads * HEAD_DIM), _MXU_DTYPE)]
    if sink is not None:
        in_specs.append(pl.BlockSpec((q_per_kv, 1, HEAD_DIM), lambda g: (g, 0, 0)))
        args.append(jnp.broadcast_to(sink.reshape(heads, 1, 1), (heads, 1, HEAD_DIM)))
        out_specs.append(pl.BlockSpec((q_per_kv, 8, HEAD_DIM), lambda g: (g, 0, 0)))
        out_shape.append(jax.ShapeDtypeStruct((heads, 8, HEAD_DIM), _F32))
    in_specs += [pl.BlockSpec(memory_space=pl.ANY)] * len(after)
    args += list(after)
    res = pl.pallas_call(
        body, grid=(kv_heads,), in_specs=in_specs, out_specs=out_specs, out_shape=out_shape,
        scratch_shapes=[pltpu.VMEM(col, _F32)] * 5, name=name,
        compiler_params=_cparams(dimension_semantics=("parallel",)),
    )(*args)
    if sink is not None:
        return res[0], res[1], res[2], res[3][:, 0, 0]
    return res


_DILATED = dict(kv_heads=B_HEADS, q_per_kv=1, q0=QB0, k0=KB0, v0=VB0, dilations=DILATIONS, lag_off=0, sink=None)
_SWA = dict(kv_heads=C_KV_HEADS, q_per_kv=C_HEADS // C_KV_HEADS, q0=QC0, k0=KC0, v0=VC0, dilations=(1,), lag_off=1)


_HG_TILE = 128
_HG_CHUNKS = _HG_TILE // A_CHUNK
_HG_TILES = SEQ // _HG_TILE
_HI = lax.Precision.HIGHEST


def _chunk_tri():
    i = np.arange(_HG_TILE)
    return jnp.asarray(((i[:, None] // A_CHUNK == i[None, :] // A_CHUNK) & (i[None, :] <= i[:, None])).astype(np.float32))


def _layer_lb(lb_ref, layer):
    if layer == 0:
        return jnp.zeros((1, HEAD_DIM), _F32)
    lg = lb_ref[...]
    m = jnp.max(lg, axis=0, keepdims=True)
    e = jnp.exp(lg - m)
    return e[1:2, :] / jnp.sum(e, axis=0, keepdims=True)


def _hgrn_gates(q, fr, lb):
    sgq = _sigmoid(q)
    sg = _sigmoid(fr)
    f = lb + (1.0 - lb) * sg
    return sgq, q * sgq, sg, f, 1.0 - f


def _hgrn_fwd(proj, lb_logits, norm_w, layer, *, name):
    tri = _chunk_tri()

    def body(q_ref, f_ref, i_ref, g_ref, lb_ref, nw_ref, tri_ref, o_ref, raw_ref, st_ref, state):
        @pl.when(pl.program_id(1) == 0)
        def _():
            state[...] = jnp.zeros_like(state)

        lb = _layer_lb(lb_ref, layer)
        _, qs, _, f, k = _hgrn_gates(q_ref[...], f_ref[...], lb)
        v = i_ref[...]
        b = jnp.dot(tri_ref[...], jnp.log(f), precision=_HI, preferred_element_type=_F32)
        eb = jnp.exp(b)
        ridx = lax.broadcasted_iota(jnp.int32, (A_CHUNK, HEAD_DIM), 0)
        outs = []
        for c in range(_HG_CHUNKS):
            sl = slice(c * A_CHUNK, (c + 1) * A_CHUNK)
            bc, qc, kc, vc = b[sl], qs[sl], k[sl], v[sl]
            bl = bc[A_CHUNK - 1:A_CHUNK]
            st = state[...]
            st_ref[0, c] = st
            o_c = lax.dot_general((qc * eb[sl]).astype(_MXU_DTYPE), st.astype(_MXU_DTYPE), _NT, preferred_element_type=_F32)
            rows = []
            for i in range(A_CHUNK):
                di = jnp.exp(jnp.where(ridx <= i, bc[i:i + 1] - bc, _NEG))
                a = jnp.sum(qc[i:i + 1] * kc * di, axis=1, keepdims=True)
                rows.append(jnp.sum(a * vc, axis=0, keepdims=True))
            outs.append(o_c + jnp.concatenate(rows, axis=0))
            kt = (kc * jnp.exp(bl - bc)).astype(_MXU_DTYPE)
            state[...] = st * jnp.exp(bl) + lax.dot_general(vc.astype(_MXU_DTYPE), kt, _TN, preferred_element_type=_F32)
        o = jnp.concatenate(outs, axis=0)
        raw_ref[...] = o
        r = lax.rsqrt(jnp.mean(o * o, axis=-1, keepdims=True) + LN_EPS)
        g = g_ref[...]
        o_ref[...] = o * r * nw_ref[...] * (g * _sigmoid(g))

    blk = (_HG_TILE, HEAD_DIM)

    def col(base):
        return pl.BlockSpec(blk, lambda h, t: (t, base + h))

    o_spec = pl.BlockSpec(blk, lambda h, t: (t, h))
    o_shape = jax.ShapeDtypeStruct((SEQ, A_HEADS * HEAD_DIM), _F32)
    return pl.pallas_call(
        body, grid=(A_HEADS, _HG_TILES),
        in_specs=[col(0), col(4), col(8), col(12), pl.BlockSpec((DEPTH, HEAD_DIM), lambda h, t: (0, h)),
                  pl.BlockSpec((1, HEAD_DIM), lambda h, t: (0, 0)), pl.BlockSpec(blk, lambda h, t: (0, 0))],
        out_specs=[o_spec, o_spec, pl.BlockSpec((1, _HG_CHUNKS, HEAD_DIM, HEAD_DIM), lambda h, t: (h, t, 0, 0))],
        out_shape=[o_shape, o_shape, jax.ShapeDtypeStruct((A_HEADS, SEQ // A_CHUNK, HEAD_DIM, HEAD_DIM), _F32)],
        scratch_shapes=[pltpu.VMEM((HEAD_DIM, HEAD_DIM), _F32)], name=name,
        compiler_params=_cparams(dimension_semantics=("parallel", "arbitrary")),
    )(proj, proj, proj, proj, lb_logits, norm_w.reshape(1, HEAD_DIM), tri)


def _hgrn_bwd(proj, lb_logits, norm_w, raw, states, dmixed, layer, *, name):
    tri = _chunk_tri()
    triu = tri.T

    def body(q_ref, f_ref, i_ref, g_ref, lb_ref, nw_ref, tri_ref, triu_ref, raw_ref, do_ref, st_ref,
             dq_ref, df_ref, di_ref, dg_ref, dnw_ref, dlb_ref, dstate):
        @pl.when(pl.program_id(1) == 0)
        def _():
            dstate[...] = jnp.zeros_like(dstate)
            dlb_ref[...] = jnp.zeros_like(dlb_ref)

        @pl.when((pl.program_id(0) == 0) & (pl.program_id(1) == 0))
        def _():
            dnw_ref[...] = jnp.zeros_like(dnw_ref)

        lb = _layer_lb(lb_ref, layer)
        q = q_ref[...]
        sgq, qs, sg, f, k = _hgrn_gates(q, f_ref[...], lb)
        v = i_ref[...]
        b = jnp.dot(tri_ref[...], jnp.log(f), precision=_HI, preferred_element_type=_F32)
        eb = jnp.exp(b)
        g = g_ref[...]
        nw = nw_ref[...]
        o = raw_ref[...]
        dout = do_ref[...]
        sgg = _sigmoid(g)
        r = lax.rsqrt(jnp.mean(o * o, axis=-1, keepdims=True) + LN_EPS)
        dg_ref[...] = (dout * (o * r * nw) * (sgg * (1.0 + g * (1.0 - sgg)))).astype(dg_ref.dtype)
        don = dout * (g * sgg)
        dnw_ref[0:1, :] += jnp.sum(don * o * r, axis=0, keepdims=True)
        dy = don * nw
        do_raw = r * dy - o * (r * r * r) * jnp.mean(o * dy, axis=-1, keepdims=True)

        ridx = lax.broadcasted_iota(jnp.int32, (A_CHUNK, HEAD_DIM), 0)
        dqs_t, dk_t, db_t, dv_t = [None] * _HG_CHUNKS, [None] * _HG_CHUNKS, [None] * _HG_CHUNKS, [None] * _HG_CHUNKS
        for c in reversed(range(_HG_CHUNKS)):
            sl = slice(c * A_CHUNK, (c + 1) * A_CHUNK)
            bc, qc, kc, vc, doc = b[sl], qs[sl], k[sl], v[sl], do_raw[sl]
            bl = bc[A_CHUNK - 1:A_CHUNK]
            ebc = eb[sl]
            ebl = jnp.exp(bl - bc)
            lam = jnp.exp(bl)
            qt = qc * ebc
            kt = kc * ebl
            dst = dstate[...]
            stp = st_ref[0, c]
            dob = doc.astype(_MXU_DTYPE)
            dstb = dst.astype(_MXU_DTYPE)
            dqt = jnp.dot(dob, stp.astype(_MXU_DTYPE), preferred_element_type=_F32)
            dkt = jnp.dot(vc.astype(_MXU_DTYPE), dstb, preferred_element_type=_F32)
            dv = lax.dot_general(kt.astype(_MXU_DTYPE), dstb, _NT, preferred_element_type=_F32)
            dlam = jnp.sum(stp * dst, axis=0, keepdims=True)
            dstate[...] = dst * lam + lax.dot_general(dob, qt.astype(_MXU_DTYPE), _TN, preferred_element_type=_F32)
            dqs_rows = []
            dk_in = jnp.zeros((A_CHUNK, HEAD_DIM), _F32)
            for i in range(A_CHUNK):
                di = jnp.exp(jnp.where(ridx <= i, bc[i:i + 1] - bc, _NEG))
                qi = qc[i:i + 1]
                doi = doc[i:i + 1]
                w = kc * di
                a = jnp.sum(qi * w, axis=1, keepdims=True)
                dv = dv + a * doi
                da = jnp.sum(doi * vc, axis=1, keepdims=True)
                dqs_rows.append(jnp.sum(da * w, axis=0, keepdims=True))
                dk_in = dk_in + da * (qi * di)
            dqs_in = jnp.concatenate(dqs_rows, axis=0)
            dbl = jnp.sum(dkt * kt, axis=0, keepdims=True) + dlam * lam
            db = qc * dqs_in - kc * dk_in + dqt * qt - dkt * kt
            db_t[c] = db + jnp.where(ridx == A_CHUNK - 1, dbl, 0.0)
            dqs_t[c] = dqs_in + dqt * ebc
            dk_t[c] = dk_in + dkt * ebl
            dv_t[c] = dv
        dqs = jnp.concatenate(dqs_t, axis=0)
        dk = jnp.concatenate(dk_t, axis=0)
        db = jnp.concatenate(db_t, axis=0)
        di_ref[...] = jnp.concatenate(dv_t, axis=0).astype(di_ref.dtype)
        dlogf = jnp.dot(triu_ref[...], db, precision=_HI, preferred_element_type=_F32)
        df = dlogf / f - dk
        df_ref[...] = (df * (1.0 - lb) * sg * (1.0 - sg)).astype(df_ref.dtype)
        dlb_ref[0, 0:1, :] += jnp.sum(df * (1.0 - sg), axis=0, keepdims=True)
        dq_ref[...] = (dqs * (sgq * (1.0 + q * (1.0 - sgq)))).astype(dq_ref.dtype)

    blk = (_HG_TILE, HEAD_DIM)
    last = _HG_TILES - 1

    def col(base):
        return pl.BlockSpec(blk, lambda h, t: (last - t, base + h))

    tri_spec = pl.BlockSpec(blk, lambda h, t: (0, 0))
    acc_spec = pl.BlockSpec((1, 8, HEAD_DIM), lambda h, t: (h, 0, 0))
    acc_shape = jax.ShapeDtypeStruct((A_HEADS, 8, HEAD_DIM), _F32)
    dq, df, di, dg, dnw, dlb = pl.pallas_call(
        body, grid=(A_HEADS, _HG_TILES),
        in_specs=[col(0), col(4), col(8), col(12), pl.BlockSpec((DEPTH, HEAD_DIM), lambda h, t: (0, h)),
                  pl.BlockSpec((1, HEAD_DIM), lambda h, t: (0, 0)), tri_spec, tri_spec, col(0), col(0),
                  pl.BlockSpec((1, _HG_CHUNKS, HEAD_DIM, HEAD_DIM), lambda h, t: (h, last - t, 0, 0))],
        out_specs=[col(0), col(0), col(0), col(0), pl.BlockSpec((8, HEAD_DIM), lambda h, t: (0, 0)), acc_spec],
        out_shape=[jax.ShapeDtypeStruct((SEQ, A_HEADS * HEAD_DIM), _MXU_DTYPE)] * 4
        + [jax.ShapeDtypeStruct((8, HEAD_DIM), _F32), acc_shape],
        scratch_shapes=[pltpu.VMEM((HEAD_DIM, HEAD_DIM), _F32)], name=name,
        compiler_params=_cparams(dimension_semantics=("arbitrary", "arbitrary")),
    )(proj, proj, proj, proj, lb_logits, norm_w.reshape(1, HEAD_DIM), tri, triu, raw, dmixed, states)
    return dq, df, di, dg, dnw[0], dlb[:, 0, :].reshape(A_HEADS * HEAD_DIM)


def _exchange(arrays, scatter, *, name):
    n = len(arrays)
    n_peer = N_DEV - 1

    def body(*refs):
        ins, outs = refs[:n], refs[n:2 * n]
        send_sems, recv_sems, loc_sems = refs[2 * n:]
        x, y, c = lax.axis_index("x"), lax.axis_index("y"), lax.axis_index("c")
        me = 4 * x + 2 * y + c
        local = []
        for a in range(n):
            cp = pltpu.make_async_copy(ins[a].at[me] if scatter else ins[a], outs[a].at[me], loc_sems.at[a])
            cp.start()
            local.append(cp)

        def peer(k):
            px = jnp.bitwise_xor(x, (k >> 2) & 1)
            py = jnp.bitwise_xor(y, (k >> 1) & 1)
            pc = jnp.bitwise_xor(c, k & 1)
            return (px, py, pc), 4 * px + 2 * py + pc

        def copy(a, k):
            dev, pid = peer(k)
            return pltpu.make_async_remote_copy(
                src_ref=ins[a].at[pid] if scatter else ins[a], dst_ref=outs[a].at[me],
                send_sem=send_sems.at[a * n_peer + k - 1], recv_sem=recv_sems.at[a * n_peer + k - 1],
                device_id=dev, device_id_type=pl.DeviceIdType.MESH)

        def arrival(a, k):
            dev, pid = peer(k)
            return pltpu.make_async_remote_copy(
                src_ref=ins[a].at[pid] if scatter else ins[a], dst_ref=outs[a].at[pid],
                send_sem=send_sems.at[a * n_peer + k - 1], recv_sem=recv_sems.at[a * n_peer + k - 1],
                device_id=dev, device_id_type=pl.DeviceIdType.MESH)

        sends = [copy(a, k) for k in range(1, N_DEV) for a in range(n)]
        for cp in sends:
            cp.start()
        for k in range(1, N_DEV):
            for a in range(n):
                arrival(a, k).wait_recv()
        for cp in sends:
            cp.wait_send()
        for cp in local:
            cp.wait()

    def out_shape(a):
        blk = a.shape[1:] if scatter else a.shape
        return jax.ShapeDtypeStruct((N_DEV,) + tuple(blk), a.dtype)

    any_spec = pl.BlockSpec(memory_space=pl.ANY)
    return pl.pallas_call(
        body, in_specs=[any_spec] * n, out_specs=[any_spec] * n, out_shape=[out_shape(a) for a in arrays],
        scratch_shapes=[pltpu.SemaphoreType.DMA((n * n_peer,)), pltpu.SemaphoreType.DMA((n * n_peer,)),
                        pltpu.SemaphoreType.DMA((n,))],
        name=name, compiler_params=pltpu.CompilerParams(has_side_effects=True),
    )(*arrays)


N_CHIP = N_DEV // 2
_MESH_ID = pl.DeviceIdType.MESH


def _place():
    x, y, c = lax.axis_index("x"), lax.axis_index("y"), lax.axis_index("c")
    chips = [(1 - x, y), (x, 1 - y), (1 - x, 1 - y)]
    return x, y, c, 2 * x + y, chips


def _sibling_swap(arrays, *, name):
    n = len(arrays)

    def body(*refs):
        ins, outs = refs[:n], refs[n:2 * n]
        send_sems, recv_sems = refs[2 * n:]
        x, y, c, _, _ = _place()
        copies = [pltpu.make_async_remote_copy(
            src_ref=ins[a].at[:, 1 - c], dst_ref=outs[a], send_sem=send_sems.at[a], recv_sem=recv_sems.at[a],
            device_id=(x, y, 1 - c), device_id_type=_MESH_ID) for a in range(n)]
        for cp in copies:
            cp.start()
        for cp in copies:
            cp.wait()

    any_spec = pl.BlockSpec(memory_space=pl.ANY)
    return pl.pallas_call(
        body, in_specs=[any_spec] * n, out_specs=[any_spec] * n,
        out_shape=[jax.ShapeDtypeStruct((N_CHIP,) + a.shape[2:], a.dtype) for a in arrays],
        scratch_shapes=[pltpu.SemaphoreType.DMA((n,)), pltpu.SemaphoreType.DMA((n,))],
        name=name, compiler_params=pltpu.CompilerParams(has_side_effects=True),
    )(*arrays)


def _pair_add(mine, theirs, core, *, name):
    _, _, R, C = mine.shape
    tr = max(t for t in range(16, R + 1, 16) if R % t == 0 and t * C <= 512 * 1024)

    def body(core_ref, m_ref, t_ref, o_ref):
        del core_ref
        o_ref[...] = (m_ref[...].astype(_F32) + t_ref[...].astype(_F32)).astype(o_ref.dtype)

    grid_spec = pltpu.PrefetchScalarGridSpec(
        num_scalar_prefetch=1, grid=(N_CHIP, R // tr),
        in_specs=[pl.BlockSpec((None, None, tr, C), lambda q, i, core: (q, core[0], i, 0)),
                  pl.BlockSpec((None, tr, C), lambda q, i, core: (q, i, 0))],
        out_specs=pl.BlockSpec((None, tr, C), lambda q, i, core: (q, i, 0)))
    return pl.pallas_call(
        body, grid_spec=grid_spec, out_shape=jax.ShapeDtypeStruct((N_CHIP, R, C), mine.dtype), name=name,
        compiler_params=_cparams(dimension_semantics=("parallel", "parallel")),
    )(core.reshape(1), mine, theirs)


_HBM = pl.BlockSpec(memory_space=pltpu.HBM)
_SEM = pl.BlockSpec(memory_space=pltpu.SEMAPHORE)
_TOKEN = pl.BlockSpec(memory_space=pltpu.VMEM)
_DATAFLOW = pltpu.SideEffectType.DATAFLOW_SIDE_EFFECTING


def _hbm(a):
    return pltpu.HBM(a.shape, a.dtype)


def _token_shape():
    return jax.ShapeDtypeStruct((8, 128), _F32)


def _dev_slot(px, py, pc):
    return 4 * px + 2 * py + pc


def _gather_start(blocks, landings, *, name):
    n = len(blocks)

    def body(*refs):
        ins, lands = refs[:n], refs[n:2 * n]
        send_sems, d2d_sems, ici_sems = refs[2 * n:2 * n + 3]
        token = refs[-1]
        x, y, c, _, chips = _place()
        for a in range(n):
            dst = lands[a].at[_dev_slot(x, y, c)]
            pltpu.make_async_remote_copy(src_ref=ins[a], dst_ref=dst, send_sem=send_sems.at[4 * a], recv_sem=d2d_sems.at[a],
                                         device_id=(x, y, 1 - c), device_id_type=_MESH_ID).start()
            for j, chip in enumerate(chips):
                pltpu.make_async_remote_copy(src_ref=ins[a], dst_ref=dst, send_sem=send_sems.at[4 * a + 1 + j],
                                             recv_sem=ici_sems.at[3 * a + j], device_id=(*chip, c),
                                             device_id_type=_MESH_ID).start()
        token[...] = jnp.zeros_like(token)

    res = pl.pallas_call(
        body, name=name, in_specs=[_HBM] * (2 * n),
        out_shape=(pltpu.SemaphoreType.DMA((4 * n,)), pltpu.SemaphoreType.DMA((n,)), pltpu.SemaphoreType.DMA((3 * n,)),
                   *[_hbm(b) for b in blocks], *[_hbm(b) for b in landings], _token_shape()),
        out_specs=(_SEM, _SEM, _SEM, *[_HBM] * (2 * n), _TOKEN),
        input_output_aliases={i: 3 + i for i in range(2 * n)},
        compiler_params=pltpu.CompilerParams(has_side_effects=_DATAFLOW),
    )(*[pltpu.with_memory_space_constraint(b, pltpu.HBM) for b in blocks],
      *[pltpu.with_memory_space_constraint(b, pltpu.HBM) for b in landings])
    return res[0], res[1], res[2], list(res[3:3 + n]), list(res[3 + n:3 + 2 * n]), res[-1]


def _gather_forward(landings, ici_sems, first, after, *, name):
    n = len(landings)

    def body(*refs):
        lands = refs[:n]
        ici = refs[n]
        f_send, f_recv = refs[n + 2], refs[n + 3]
        token = refs[-1]
        x, y, c, _, chips = _place()
        for a in range(n):
            for j, chip in enumerate(chips):
                blk = lands[a].at[_dev_slot(*chip, c)]
                pltpu.make_async_remote_copy(src_ref=blk, dst_ref=blk, send_sem=f_send.at[3 * a + j],
                                             recv_sem=ici.at[3 * (first + a) + j], device_id=(*chip, c),
                                             device_id_type=_MESH_ID).wait_recv()
                pltpu.make_async_remote_copy(src_ref=blk, dst_ref=blk, send_sem=f_send.at[3 * a + j], recv_sem=f_recv.at[3 * a + j],
                                             device_id=(x, y, 1 - c), device_id_type=_MESH_ID).start()
        token[...] = jnp.zeros_like(token)

    res = pl.pallas_call(
        body, name=name, in_specs=[_HBM] * n + [_SEM, pl.BlockSpec(memory_space=pl.ANY)],
        out_shape=(pltpu.SemaphoreType.DMA((3 * n,)), pltpu.SemaphoreType.DMA((3 * n,)), *[_hbm(b) for b in landings], _token_shape()),
        out_specs=(_SEM, _SEM, *[_HBM] * n, _TOKEN),
        input_output_aliases={i: 2 + i for i in range(n)},
        compiler_params=pltpu.CompilerParams(has_side_effects=_DATAFLOW),
    )(*landings, ici_sems, after)
    return res[0], res[1], list(res[2:2 + n]), res[-1]


def _gather_wait(blocks, landings, send_sems, d2d_sems, first, f_send, f_recv, after, *, name):
    n = len(landings)

    def body(*refs):
        ins, lands = refs[:n], refs[n:2 * n]
        send, d2d, fs, fr = refs[2 * n:2 * n + 4]
        x, y, c, _, chips = _place()
        me = (x, y, c)
        for a in range(n):
            own = lands[a].at[_dev_slot(x, y, 1 - c)]
            g = first + a
            pltpu.make_async_remote_copy(src_ref=ins[a], dst_ref=own, send_sem=send.at[4 * g], recv_sem=d2d.at[g],
                                         device_id=me, device_id_type=_MESH_ID).wait_recv()
            for j, chip in enumerate(chips):
                blk = lands[a].at[_dev_slot(*chip, 1 - c)]
                pltpu.make_async_remote_copy(src_ref=blk, dst_ref=blk, send_sem=fs.at[3 * a + j], recv_sem=fr.at[3 * a + j],
                                             device_id=me, device_id_type=_MESH_ID).wait_recv()
            for k in range(4):
                pltpu.make_async_remote_copy(src_ref=ins[a], dst_ref=own, send_sem=send.at[4 * g + k], recv_sem=d2d.at[g],
                                             device_id=me, device_id_type=_MESH_ID).wait_send()
            for j in range(3):
                pltpu.make_async_remote_copy(src_ref=own, dst_ref=own, send_sem=fs.at[3 * a + j], recv_sem=fr.at[3 * a + j],
                                             device_id=me, device_id_type=_MESH_ID).wait_send()

    res = pl.pallas_call(
        body, name=name, in_specs=[_HBM] * (2 * n) + [_SEM] * 4 + [pl.BlockSpec(memory_space=pl.ANY)],
        out_shape=(*[_hbm(b) for b in blocks], *[_hbm(b) for b in landings]), out_specs=tuple([_HBM] * (2 * n)),
        input_output_aliases={i: i for i in range(2 * n)},
        compiler_params=pltpu.CompilerParams(has_side_effects=_DATAFLOW),
    )(*blocks, *landings, send_sems, d2d_sems, f_send, f_recv, after)
    return list(res[n:])


def _swap_start(mine, landings, *, name):
    n = len(mine)

    def body(*refs):
        ins, lands = refs[:n], refs[n:2 * n]
        send_sems, recv_sems = refs[2 * n:2 * n + 2]
        token = refs[-1]
        x, y, c, _, _ = _place()
        for a in range(n):
            pltpu.make_async_remote_copy(src_ref=ins[a].at[:, 1 - c], dst_ref=lands[a], send_sem=send_sems.at[a],
                                         recv_sem=recv_sems.at[a], device_id=(x, y, 1 - c), device_id_type=_MESH_ID).start()
        token[...] = jnp.zeros_like(token)

    res = pl.pallas_call(
        body, name=name, in_specs=[_HBM] * (2 * n),
        out_shape=(pltpu.SemaphoreType.DMA((n,)), pltpu.SemaphoreType.DMA((n,)),
                   *[_hbm(b) for b in mine], *[_hbm(b) for b in landings], _token_shape()),
        out_specs=(_SEM, _SEM, *[_HBM] * (2 * n), _TOKEN),
        input_output_aliases={i: 2 + i for i in range(2 * n)},
        compiler_params=pltpu.CompilerParams(has_side_effects=_DATAFLOW),
    )(*[pltpu.with_memory_space_constraint(b, pltpu.HBM) for b in mine],
      *[pltpu.with_memory_space_constraint(b, pltpu.HBM) for b in landings])
    return res[0], res[1], list(res[2:2 + n]), list(res[2 + n:2 + 2 * n]), res[-1]


def _swap_wait(mine, landings, send_sems, recv_sems, after, *, name):
    n = len(mine)

    def body(*refs):
        ins, lands = refs[:n], refs[n:2 * n]
        send, recv = refs[2 * n:2 * n + 2]
        x, y, c, _, _ = _place()
        for a in range(n):
            cp = pltpu.make_async_remote_copy(src_ref=ins[a].at[:, 1 - c], dst_ref=lands[a], send_sem=send.at[a],
                                              recv_sem=recv.at[a], device_id=(x, y, c), device_id_type=_MESH_ID)
            cp.wait_recv()
            cp.wait_send()

    res = pl.pallas_call(
        body, name=name, in_specs=[_HBM] * (2 * n) + [_SEM] * 2 + [pl.BlockSpec(memory_space=pl.ANY)],
        out_shape=(*[_hbm(b) for b in mine], *[_hbm(b) for b in landings]), out_specs=tuple([_HBM] * (2 * n)),
        input_output_aliases={i: i for i in range(2 * n)},
        compiler_params=pltpu.CompilerParams(has_side_effects=_DATAFLOW),
    )(*mine, *landings, send_sems, recv_sems, after)
    return list(res[:n]), list(res[n:])


def _chip_exchange_start(sums, landings, *, name):
    n = len(sums)

    def body(*refs):
        ins, lands = refs[:n], refs[n:2 * n]
        send_sems, recv_sems = refs[2 * n:2 * n + 2]
        token = refs[-1]
        _, _, c, p, chips = _place()
        for a in range(n):
            for j, (qx, qy) in enumerate(chips):
                pltpu.make_async_remote_copy(src_ref=ins[a].at[2 * qx + qy], dst_ref=lands[a].at[p], send_sem=send_sems.at[3 * a + j],
                                             recv_sem=recv_sems.at[3 * a + j], device_id=(qx, qy, c), device_id_type=_MESH_ID).start()
        token[...] = jnp.zeros_like(token)

    res = pl.pallas_call(
        body, name=name, in_specs=[_HBM] * (2 * n),
        out_shape=(pltpu.SemaphoreType.DMA((3 * n,)), pltpu.SemaphoreType.DMA((3 * n,)),
                   *[_hbm(b) for b in sums], *[_hbm(b) for b in landings], _token_shape()),
        out_specs=(_SEM, _SEM, *[_HBM] * (2 * n), _TOKEN),
        input_output_aliases={i: 2 + i for i in range(2 * n)},
        compiler_params=pltpu.CompilerParams(has_side_effects=_DATAFLOW),
    )(*[pltpu.with_memory_space_constraint(b, pltpu.HBM) for b in sums],
      *[pltpu.with_memory_space_constraint(b, pltpu.HBM) for b in landings])
    return res[0], res[1], list(res[2:2 + n]), list(res[2 + n:2 + 2 * n]), res[-1]


def _chip_exchange_wait(sums, landings, send_sems, recv_sems, after, *, name):
    n = len(sums)

    def body(*refs):
        ins, lands = refs[:n], refs[n:2 * n]
        send, recv = refs[2 * n:2 * n + 2]
        x, y, c, _, chips = _place()
        for a in range(n):
            for j, (qx, qy) in enumerate(chips):
                q = 2 * qx + qy
                cp = pltpu.make_async_remote_copy(src_ref=ins[a].at[q], dst_ref=lands[a].at[q], send_sem=send.at[3 * a + j],
                                                  recv_sem=recv.at[3 * a + j], device_id=(x, y, c), device_id_type=_MESH_ID)
                cp.wait_recv()
                cp.wait_send()

    res = pl.pallas_call(
        body, name=name, in_specs=[_HBM] * (2 * n) + [_SEM] * 2 + [pl.BlockSpec(memory_space=pl.ANY)] * len(after),
        out_shape=(*[_hbm(b) for b in sums], *[_hbm(b) for b in landings]), out_specs=tuple([_HBM] * (2 * n)),
        input_output_aliases={i: i for i in range(2 * n)},
        compiler_params=pltpu.CompilerParams(has_side_effects=_DATAFLOW),
    )(*sums, *landings, send_sems, recv_sems, *after)
    return list(res[:n]), list(res[n:])


_C1 = 1.0 - ADAM_B1 ** ADAM_STEP
_C2 = 1.0 - ADAM_B2 ** ADAM_STEP


def _adamw_math(g, w, m, v):
    m = ADAM_B1 * m + (1.0 - ADAM_B1) * g
    v = ADAM_B2 * v + (1.0 - ADAM_B2) * (g * g)
    delta = -ADAM_LR * ((m / _C1) / (jnp.sqrt(v / _C2) + ADAM_EPS) + ADAM_WD * w)
    return delta, m, v


def _adamw_reduce(landed, sums, chip, w, m, v, layer, prev, *, name):
    _, R, C = w.shape
    tr = max(t for t in range(16, R + 1, 16) if R % t == 0 and t * C <= 256 * 1024)

    def body(chip_ref, p_ref, own_ref, w_ref, m_ref, v_ref, *rest):
        g_ref, d_ref, nm_ref, nv_ref = rest[-4:]
        own = own_ref[...].astype(_F32)
        g = jnp.where(chip_ref[0] == 0, own, p_ref[0].astype(_F32))
        for q in range(1, N_CHIP):
            g = g + jnp.where(chip_ref[0] == q, own, p_ref[q].astype(_F32))
        d, nm, nv = _adamw_math(g, w_ref[...], m_ref[...], v_ref[...])
        g_ref[...] = g
        d_ref[...] = d
        nm_ref[...] = nm
        nv_ref[...] = nv

    blk = pl.BlockSpec((None, tr, C), lambda i, chip: (layer, i, 0))
    shape = jax.ShapeDtypeStruct((DEPTH, R, C), _F32)
    kept = [] if prev is None else list(prev)
    grid_spec = pltpu.PrefetchScalarGridSpec(
        num_scalar_prefetch=1, grid=(R // tr,),
        in_specs=[pl.BlockSpec((N_CHIP, tr, C), lambda i, chip: (0, i, 0)),
                  pl.BlockSpec((None, tr, C), lambda i, chip: (chip[0], i, 0)), blk, blk, blk]
        + [pl.BlockSpec(memory_space=pl.ANY)] * len(kept),
        out_specs=[blk] * 4)
    return pl.pallas_call(
        body, grid_spec=grid_spec, out_shape=[shape] * 4, name=name,
        input_output_aliases={6 + k: k for k in range(len(kept))},
        compiler_params=_cparams(dimension_semantics=("parallel",)),
    )(chip.reshape(1), landed, sums, w, m, v, *kept)


_PACK_LANES = 128
_LAYER_ROWS = 248
_LB_ROWS = (A_HEADS * HEAD_DIM) // _PACK_LANES


def _small_reduce(parts, lb_logits, *, name):
    rows = DEPTH * _LAYER_ROWS

    def body(p_ref, lg_ref, o_ref):
        g = p_ref[0]
        for s in range(1, N_DEV):
            g = g + p_ref[s]
        o_ref[...] = g
        lg = lg_ref[...]
        e = jnp.exp(lg - jnp.max(lg, axis=0, keepdims=True))
        p = e / jnp.sum(e, axis=0, keepdims=True)
        d1 = g[_LAYER_ROWS:_LAYER_ROWS + _LB_ROWS, :] * p[0] * p[1]
        o_ref[0:_LB_ROWS, :] = -d1
        o_ref[_LAYER_ROWS:_LAYER_ROWS + _LB_ROWS, :] = d1

    return pl.pallas_call(
        body, out_shape=jax.ShapeDtypeStruct((rows, _PACK_LANES), _F32), name=name,
        compiler_params=_cparams(),
    )(parts, lb_logits.reshape(DEPTH, _LB_ROWS, _PACK_LANES))


def _adamw_small(g, w, m, v, *, name):
    def body(g_ref, w_ref, m_ref, v_ref, d_ref, nm_ref, nv_ref):
        d, nm, nv = _adamw_math(g_ref[...], w_ref[...], m_ref[...], v_ref[...])
        d_ref[...] = d
        nm_ref[...] = nm
        nv_ref[...] = nv

    shape = jax.ShapeDtypeStruct(g.shape, _F32)
    return pl.pallas_call(body, out_shape=[shape] * 3, name=name, compiler_params=_cparams())(g, w, m, v)


def _pack(vectors, rows):
    flat = jnp.concatenate([v.reshape(-1).astype(_F32) for v in vectors])
    return jnp.pad(flat, (0, rows * _PACK_LANES - flat.shape[0])).reshape(rows, _PACK_LANES)


def _unpack(packed, shapes):
    flat = packed.reshape(-1)
    out, at = [], 0
    for s in shapes:
        size = int(np.prod(s))
        out.append(flat[at:at + size].reshape(s))
        at += size
    return out


_BIG = ("w_in", "w_gate", "w_up", "w_out", "w_down")
_COLUMN_SHARDED = ("w_in", "w_gate", "w_up")


def _full_weight(name, g):
    if name == "w_out":
        return g.reshape(D_MODEL, D_MODEL)
    if name == "w_down":
        return g.reshape(D_FF, D_MODEL)
    if name == "conv_w":
        return g.transpose(1, 0, 2).reshape(g.shape[1], N_DEV * SHARD_COLS)
    return g


class _WeightGather:
    def __init__(self, names, first, blocks, lands, sems, tag):
        self.names, self.first, self.blocks, self.lands, self.sems, self.tag = names, first, blocks, lands, sems, tag
        self.forwarded = None

    def forward(self, after):
        f_send, f_recv, self.lands, token = _gather_forward(self.lands, self.sems[2], self.first, after,
                                                            name=f"gather_forward_{self.tag}")
        self.forwarded = (f_send, f_recv)
        return token

    def wait(self, after):
        if self.forwarded is None:
            self.forward(after)
        got = _gather_wait(self.blocks, self.lands, self.sems[0], self.sems[1], self.first, *self.forwarded, after,
                           name=f"gather_wait_{self.tag}")
        return {n: _full_weight(n, g) for n, g in zip(self.names, got)}


def _start_gathers(groups, me, name):
    blocks = [b for _, _, bs in groups for b in bs]
    landings = [lax.dynamic_update_index_in_dim(lax.empty((N_DEV,) + b.shape, b.dtype), b[None], me, 0) for b in blocks]
    send, d2d, ici, blocks, landings, token = _gather_start(blocks, landings, name=name)
    out, first = [], 0
    for tag, names, bs in groups:
        k = len(bs)
        out.append(_WeightGather(names, first, blocks[first:first + k], landings[first:first + k], (send, d2d, ici), tag))
        first += k
    return out, token


class _LayerWeights:
    def __init__(self, ready, pending=(), forwards=(), tokens=()):
        self.ready, self.pending, self.forwards, self._tokens = dict(ready), list(pending), list(forwards), list(tokens)

    def at(self, point, after):
        for when, gather in self.forwards:
            if when == point:
                self._tokens.append(gather.forward(after))

    def tokens(self):
        out, self._tokens = self._tokens, []
        return out

    def get(self, name, after):
        if name not in self.ready:
            group, = [g for g in self.pending if name in g.names]
            self.ready.update(group.wait(after))
        return self.ready[name]


def _layer_fwd(x, xb, ws, lb_logits, a_norm_w, c_sink, ln1_g, ln1_b, conv_b, ln2_g, ln2_b, tabs, l):
    proj = _mm_w_slabs(xb, ws.get("w_in", xb), tm=1024, after=ws.tokens(), name=f"proj_{l}")
    o_a, raw, states = _hgrn_fwd(proj, lb_logits, a_norm_w, l, name=f"hgrn_fwd_{l}")
    ws.at("hgrn", o_a)
    o_b, lse_b = _band_fwd(proj, tabs, name=f"dilated_fwd_{l}", **_DILATED)
    o_c, lse_c = _band_fwd(proj, tabs, sink=c_sink, name=f"swa_fwd_{l}", **_SWA)
    ws.at("swa", o_c)
    mixed = jnp.concatenate([o_a, o_b, o_c], axis=1).astype(_MXU_DTYPE)
    y = _mm(mixed, ws.get("w_out", mixed), tm=1024, tn=512, after=ws.tokens(), name=f"mix_out_{l}")
    z1, x1, x1b = _ln_fwd(x, y, ln1_g, ln1_b, name=f"ln1_fwd_{l}")
    g = _mm_w_slabs(x1b, ws.get("w_gate", x1b), tm=1024, out_dtype=_ACT_DTYPE, name=f"ffn_gate_{l}")
    u = _mm_w_slabs(x1b, ws.get("w_up", x1b), tm=1024, out_dtype=_ACT_DTYPE, name=f"ffn_up_{l}")
    ws.at("up", u)
    hb = _conv_gate_fwd(g, u, ws.get("conv_w", u), conv_b, name=f"conv_gate_fwd_{l}")
    y2 = _mm(hb, ws.get("w_down", hb), tm=512, tn=512, after=ws.tokens(), name=f"ffn_down_{l}")
    ws.at("down", y2)
    z2, x2, x2b = _ln_fwd(x1, y2, ln2_g, ln2_b, name=f"ln2_fwd_{l}")
    res = dict(xb=xb, proj=proj, raw=raw, states=states, o_b=o_b, lse_b=lse_b, o_c=o_c, lse_c=lse_c,
               mixed=mixed, z1=z1, x1b=x1b, g=g, u=u, hb=hb, z2=z2)
    return x2, x2b, res


class _GradExchange:
    def __init__(self, core, chip):
        self.core, self.chip, self.groups, self.swapping, self._tokens = core, chip, [], [], []

    def launch(self, names, slabs, l, tag, behind):
        mine = [s.reshape((N_CHIP, 2) + s.shape[1:]) for s in slabs]
        if behind:
            landings = [lax.empty((N_CHIP,) + m.shape[2:], m.dtype) for m in mine]
            send, recv, mine, landings, token = _swap_start(mine, landings, name=f"swap_start_{tag}")
            self.swapping.append((names, l, tag, send, recv, mine, landings))
            self._tokens.append(token)
        else:
            self._exchange(names, l, tag, mine, _sibling_swap(mine, name=f"swap_grads_{tag}"))

    def advance(self, after):
        for names, l, tag, send, recv, mine, landings in self.swapping:
            mine, theirs = _swap_wait(mine, landings, send, recv, after, name=f"swap_wait_{tag}")
            self._exchange(names, l, tag, mine, theirs)
        self.swapping = []

    def _exchange(self, names, l, tag, mine, theirs):
        sums = [_pair_add(a, b, self.core, name=f"pair_add_{n}_{l}") for n, a, b in zip(names, mine, theirs)]
        landings = [lax.empty(s.shape, s.dtype) for s in sums]
        send, recv, sums, landings, token = _chip_exchange_start(sums, landings, name=f"exchange_start_{tag}")
        self.groups.append((names, l, tag, send, recv, sums, landings))
        self._tokens.append(token)

    def tokens(self):
        out, self._tokens = self._tokens, []
        return out

    def finish(self, weights, mom1, mom2, after):
        out = {}
        after = list(after) + self.tokens()
        for names, l, tag, send, recv, sums, landings in self.groups:
            sums, landings = _chip_exchange_wait(sums, landings, send, recv, after, name=f"exchange_wait_{tag}")
            for n, s, landed in zip(names, sums, landings):
                out[n] = _adamw_reduce(landed, s, self.chip, weights[n], mom1[n], mom2[n], l, out.get(n), name=f"adamw_{n}_{l}")
            after = [out[n][0] for n in names]
        return out


def _layer_bwd(dx2, res, w, lb_logits, a_norm_w, c_sink, ln1_g, conv_b, ln2_g, tabs, exchange, l):
    dz2, dz2b, d_ln2_g, d_ln2_b = _ln_bwd(res["z2"], dx2, None, ln2_g, name=f"ln2_bwd_{l}")
    exchange.advance(dz2b)
    dh = _mm(dz2b, w["w_down"], tb=True, tm=1024, tn=512, out_dtype=_ACT_DTYPE, after=exchange.tokens(),
             name=f"ffn_down_dx_{l}")
    d_w_down = _mm(res["hb"], dz2b, ta=True, tm=512, tn=512, out_dtype=_GRAD_DTYPE, name=f"ffn_down_dw_{l}")
    dg, du, d_conv_w, d_conv_b = _conv_gate_bwd(dh, res["g"], res["u"], w["conv_w"], conv_b, name=f"conv_gate_bwd_{l}")
    t = _mm_nt_w_slabs(dg, w["w_gate"], tm=512, tn=512, name=f"ffn_gate_dx_{l}")
    dx1 = _mm_nt_w_slabs(du, w["w_up"], tm=512, tn=512, add=t, name=f"ffn_up_dx_{l}")
    d_w_gate = _mm_tn_slabs(res["x1b"], dg, tm=1024, name=f"ffn_gate_dw_{l}")
    d_w_up = _mm_tn_slabs(res["x1b"], du, tm=1024, name=f"ffn_up_dw_{l}")
    dz1, dz1b, d_ln1_g, d_ln1_b = _ln_bwd(res["z1"], dx1, dz2, ln1_g, name=f"ln1_bwd_{l}")
    d_w_out = _mm(res["mixed"], dz1b, ta=True, tm=1024, tn=512, out_dtype=_GRAD_DTYPE, name=f"mix_out_dw_{l}")
    exchange.launch(("w_down", "w_gate", "w_up", "w_out"),
                    [d_w_down.reshape(N_DEV, D_FF // N_DEV, D_MODEL), d_w_gate, d_w_up,
                     d_w_out.reshape(N_DEV, D_MODEL // N_DEV, D_MODEL)], l, f"ffn_{l}", True)
    dmixed = _mm(dz1b, w["w_out"], tb=True, tm=1024, tn=512, after=exchange.tokens(), name=f"mix_out_dx_{l}")
    dq_a, df_a, di_a, dg_a, d_norm_w, d_lb = _hgrn_bwd(res["proj"], lb_logits, a_norm_w, res["raw"], res["states"],
                                                      dmixed, l, name=f"hgrn_bwd_{l}")
    exchange.advance(dq_a)
    dq_b, dk_b, dv_b = _band_bwd(res["proj"], tabs, dmixed, res["o_b"], res["lse_b"], do0=A_HEADS, after=exchange.tokens(),
                                 name=f"dilated_bwd_{l}", **_DILATED)
    dq_c, dk_c, dv_c, d_sink = _band_bwd(res["proj"], tabs, dmixed, res["o_c"], res["lse_c"], do0=A_HEADS + B_HEADS,
                                         sink=c_sink, name=f"swa_bwd_{l}", **_SWA)
    dproj = jnp.concatenate([dq_a, df_a, di_a, dg_a, dq_b, dk_b, dv_b, dq_c, dk_c, dv_c], axis=1)
    d_w_in = _mm_tn_slabs(res["xb"], dproj, tm=1024, name=f"proj_dw_{l}")
    exchange.launch(("w_in",), [d_w_in], l, f"mix_{l}", l > 0)
    dx = _mm_nt_w_slabs(dproj, w["w_in"], tm=512, tn=512, add=dz1, add_scale=ALPHA, after=exchange.tokens(),
                        name=f"proj_dx_{l}")
    small = [d_lb, d_norm_w, jnp.pad(d_sink, (0, _PACK_LANES - C_HEADS)), d_ln1_g, d_ln1_b, d_ln2_g, d_ln2_b, d_conv_b,
             d_conv_w]
    return dx, small


def kernel(x, w_in, lb_logits, a_norm_w, c_sinks, w_out, ln1_g, ln1_b, w_gate, w_up, conv_w, conv_b, w_down, ln2_g, ln2_b, loss_target, m_w_in, m_lb_logits, m_a_norm_w, m_c_sinks, m_w_out, m_ln1_g, m_ln1_b, m_w_gate, m_w_up, m_conv_w, m_conv_b, m_w_down, m_ln2_g, m_ln2_b, v_w_in, v_lb_logits, v_a_norm_w, v_c_sinks, v_w_out, v_ln1_g, v_ln1_b, v_w_gate, v_w_up, v_conv_w, v_conv_b, v_w_down, v_ln2_g, v_ln2_b):
    weights = dict(w_in=w_in, lb_logits=lb_logits, a_norm_w=a_norm_w, c_sinks=c_sinks, w_out=w_out, ln1_g=ln1_g, ln1_b=ln1_b,
                   w_gate=w_gate, w_up=w_up, conv_w=conv_w, conv_b=conv_b, w_down=w_down, ln2_g=ln2_g, ln2_b=ln2_b)
    mom1 = dict(w_in=m_w_in, lb_logits=m_lb_logits, a_norm_w=m_a_norm_w, c_sinks=m_c_sinks, w_out=m_w_out, ln1_g=m_ln1_g,
                ln1_b=m_ln1_b, w_gate=m_w_gate, w_up=m_w_up, conv_w=m_conv_w, conv_b=m_conv_b, w_down=m_w_down, ln2_g=m_ln2_g,
                ln2_b=m_ln2_b)
    mom2 = dict(w_in=v_w_in, lb_logits=v_lb_logits, a_norm_w=v_a_norm_w, c_sinks=v_c_sinks, w_out=v_w_out, ln1_g=v_ln1_g,
                ln1_b=v_ln1_b, w_gate=v_w_gate, w_up=v_w_up, conv_w=v_conv_w, conv_b=v_conv_b, w_down=v_w_down, ln2_g=v_ln2_g,
                ln2_b=v_ln2_b)
    core = lax.axis_index("c").astype(jnp.int32)
    me = 4 * lax.axis_index("x") + 2 * lax.axis_index("y") + core
    tabs = _rope_tables()

    chip = (2 * lax.axis_index("x") + lax.axis_index("y")).astype(jnp.int32)

    def as_slabs(d):
        return {n: jnp.swapaxes(d[n], 1, 2) if n in _COLUMN_SHARDED else d[n] for n in _BIG}

    w_views = as_slabs(weights)

    def block(n, l, after=()):
        return conv_w[l] if n == "conv_w" else _cast_layer(w_views[n], l, after=after, name=f"cast_{n}_{l}")

    (in0,), started_first = _start_gathers([("w_in_0", ("w_in",), [block("w_in", 0)])], me, "gather_start_first")
    order = [(("w_out",), 0), (("w_gate", "w_up", "conv_w"), 0), (("w_down",), 0),
             (("w_in",), 1), (("w_out",), 1), (("w_gate", "w_up", "conv_w"), 1), (("w_down",), 1)]
    gathers, started = _start_gathers([(f"{names[0]}_{l}", names, [block(n, l, [started_first]) for n in names])
                                       for names, l in order], me, "gather_start_rest")
    out0, ffn0, down0, in1, out1, ffn1, down1 = gathers
    layer_ws = [_LayerWeights(in0.wait(started), [out0, ffn0, down0],
                              [("hgrn", out0), ("swa", ffn0), ("up", down0), ("down", in1)]),
                _LayerWeights({}, [in1, out1, ffn1, down1], [("hgrn", out1), ("swa", ffn1), ("up", down1)])]

    xs = x[0]
    xb = xs.astype(_MXU_DTYPE)
    saved = []
    for l in range(DEPTH):
        xs, xb, res = _layer_fwd(xs, xb, layer_ws[l], lb_logits, a_norm_w[l], c_sinks[l], ln1_g[l], ln1_b[l], conv_b[l],
                                 ln2_g[l], ln2_b[l], tabs, l)
        saved.append(res)
    loss_part, dx = _loss_head(xs, loss_target[0], name="loss_head")
    loss = lax.psum(loss_part, ("x", "y", "c"))

    exchange = _GradExchange(core, chip)
    small_parts = [None] * DEPTH
    for l in reversed(range(DEPTH)):
        dx, small = _layer_bwd(dx, saved[l], layer_ws[l].ready, lb_logits, a_norm_w[l], c_sinks[l], ln1_g[l], conv_b[l],
                               ln2_g[l], tabs, exchange, l)
        small_parts[l] = _pack(small, _LAYER_ROWS)
    updated = exchange.finish(w_views, as_slabs(mom1), as_slabs(mom2), [dx])
    updated = {n: tuple(jnp.swapaxes(t, 1, 2) for t in u) if n in _COLUMN_SHARDED else u for n, u in updated.items()}
    gathered, = _exchange([jnp.concatenate(small_parts, axis=0)], False, name="gather_small_grads")
    g_small = _small_reduce(gathered, lb_logits, name="small_grads")

    per_layer = [(A_HEADS * HEAD_DIM,), (HEAD_DIM,), (_PACK_LANES,), (D_MODEL,), (D_MODEL,), (D_MODEL,), (D_MODEL,), (D_FF,),
                 (3, D_FF)]
    names = ("lb_logits", "a_norm_w", "c_sinks", "ln1_g", "ln1_b", "ln2_g", "ln2_b", "conv_b", "conv_w")
    grads = {n: [] for n in names}
    for l in range(DEPTH):
        for n, t in zip(names, _unpack(g_small[l * _LAYER_ROWS:(l + 1) * _LAYER_ROWS], per_layer)):
            grads[n].append(t)
    grads = {n: jnp.stack(t) for n, t in grads.items()}
    grads["c_sinks"] = grads["c_sinks"][:, :C_HEADS]
    grads["conv_w"] = lax.dynamic_slice_in_dim(grads["conv_w"], me * SHARD_COLS, SHARD_COLS, axis=2)
    shapes = [grads[n].shape for n in names]
    rows = -(-sum(int(np.prod(s)) for s in shapes) // (8 * _PACK_LANES)) * 8
    d_s, m_s, v_s = _adamw_small(_pack([grads[n] for n in names], rows), _pack([weights[n] for n in names], rows),
                                 _pack([mom1[n] for n in names], rows), _pack([mom2[n] for n in names], rows),
                                 name="adamw_small")
    delta = dict(zip(names, _unpack(d_s, shapes)))
    new_m = dict(zip(names, _unpack(m_s, shapes)))
    new_v = dict(zip(names, _unpack(v_s, shapes)))
    for n in _BIG:
        grads[n], delta[n], new_m[n], new_v[n] = updated[n]

    order = ("w_in", "lb_logits", "a_norm_w", "c_sinks", "w_out", "ln1_g", "ln1_b", "w_gate", "w_up", "conv_w", "conv_b",
             "w_down", "ln2_g", "ln2_b")
    return (loss, dx[None], *[grads[n] for n in order], *[delta[n] for n in order], *[new_m[n] for n in order],
            *[new_v[n] for n in order])
```

```python
import functools

import jax
import jax.numpy as jnp
import numpy as np
from jax import lax
from jax.experimental import pallas as pl
from jax.experimental.pallas import tpu as pltpu

D_MODEL = 2048
SEQ = 2048
DEPTH = 2
HEAD_DIM = 128
A_HEADS = 4
B_HEADS = 6
C_HEADS = 6
C_KV_HEADS = 2
A_CHUNK = 16
DILATIONS = (1, 4, 16)
BLOCK = 128
ROPE_THETA = 500000.0
ROPE_DIM = 32
D_FF = 5632
IN_WIDTH = 5632
LN_EPS = 1e-5
ALPHA = (2 * DEPTH) ** 0.25
N_DEV = 8
SHARD_COLS = IN_WIDTH // N_DEV

ADAM_LR = 0.001
ADAM_B1 = 0.9
ADAM_B2 = 0.999
ADAM_EPS = 1e-08
ADAM_WD = 0.01
ADAM_STEP = 10

A_COLS = 16
QKV_COLS = 28
QB0, KB0, VB0, QC0, KC0, VC0 = 0, 6, 12, 18, 24, 26

_MXU_DTYPE = jnp.bfloat16
_GRAD_DTYPE = jnp.bfloat16
_ACT_DTYPE = jnp.bfloat16
_NEG = -1e30
_VMEM_LIMIT = 56 * 2 ** 20

_F32 = jnp.float32


def _sigmoid(x):
    return 1.0 / (1.0 + jnp.exp(-x))


def _cparams(**kw):
    return pltpu.CompilerParams(vmem_limit_bytes=_VMEM_LIMIT, **kw)


_TILE_MIX = dict(tm=1024, tn=1024)
_TILE_DOWN = dict(tm=1024, tn=512)
_TILE_DOWN_DX = dict(tm=1024, tn=1408)
_TILE_DOWN_DW = dict(tm=1408, tn=1024)
_TILE_NT_SLABS = dict(tm=1024, tn=512)


def _mm(a, b, *, ta=False, tb=False, tm, tn, out_dtype=_F32, add=None, add_scale=1.0, after=(), name):
    K = a.shape[0] if ta else a.shape[1]
    M = a.shape[1] if ta else a.shape[0]
    N = b.shape[0] if tb else b.shape[1]
    assert (b.shape[1] if tb else b.shape[0]) == K and M % tm == 0 and N % tn == 0
    dn = (((0 if ta else 1,), (1 if tb else 0,)), ((), ()))

    def body(*refs):
        a_ref, b_ref = refs[:2]
        o_ref = refs[-1]
        r = lax.dot_general(a_ref[...], b_ref[...], dn, preferred_element_type=_F32)
        if add is not None:
            r = r + add_scale * refs[2][...]
        o_ref[...] = r.astype(o_ref.dtype)

    a_spec = pl.BlockSpec((K, tm), lambda i, j: (0, i)) if ta else pl.BlockSpec((tm, K), lambda i, j: (i, 0))
    b_spec = pl.BlockSpec((tn, K), lambda i, j: (j, 0)) if tb else pl.BlockSpec((K, tn), lambda i, j: (0, j))
    o_spec = pl.BlockSpec((tm, tn), lambda i, j: (i, j))
    in_specs = [a_spec, b_spec] + ([o_spec] if add is not None else []) + [pl.BlockSpec(memory_space=pl.ANY)] * len(after)
    args = (a, b) + ((add,) if add is not None else ()) + tuple(after)
    return pl.pallas_call(
        body, grid=(M // tm, N // tn), in_specs=in_specs, out_specs=o_spec,
        out_shape=jax.ShapeDtypeStruct((M, N), out_dtype), name=name,
        compiler_params=_cparams(dimension_semantics=("parallel", "parallel")),
    )(*args)


_PAIR = 2 * SHARD_COLS


def _mm_tn_slabs(a, b, *, tm, name):
    K, M = a.shape
    assert b.shape == (K, N_DEV * SHARD_COLS) and M % tm == 0

    def body(a_ref, b_ref, o_ref):
        a_blk = a_ref[...]
        for s in range(2):
            o_ref[s] = lax.dot_general(b_ref[:, s * SHARD_COLS:(s + 1) * SHARD_COLS], a_blk, _TN,
                                       preferred_element_type=_F32).astype(o_ref.dtype)

    return pl.pallas_call(
        body, grid=(M // tm, N_DEV // 2),
        in_specs=[pl.BlockSpec((K, tm), lambda i, p: (0, i)), pl.BlockSpec((K, _PAIR), lambda i, p: (0, p))],
        out_specs=pl.BlockSpec((2, SHARD_COLS, tm), lambda i, p: (p, 0, i)),
        out_shape=jax.ShapeDtypeStruct((N_DEV, SHARD_COLS, M), _GRAD_DTYPE), name=name,
        compiler_params=_cparams(dimension_semantics=("parallel", "parallel")),
    )(a, b)


def _cast_layer(w, layer, *, after=(), name):
    _, R, C = w.shape
    tr = max(t for t in range(16, R + 1, 16) if R % t == 0 and t * C <= 512 * 1024)

    def body(w_ref, *rest):
        o_ref = rest[-1]
        o_ref[...] = w_ref[...].astype(o_ref.dtype)

    return pl.pallas_call(
        body, grid=(R // tr,),
        in_specs=[pl.BlockSpec((None, tr, C), lambda i: (layer, i, 0))] + [pl.BlockSpec(memory_space=pl.ANY)] * len(after),
        out_specs=pl.BlockSpec((tr, C), lambda i: (i, 0)), out_shape=jax.ShapeDtypeStruct((R, C), _MXU_DTYPE), name=name,
        compiler_params=_cparams(dimension_semantics=("parallel",)),
    )(w, *after)


def _mm_w_slabs(a, w, *, tm, out_dtype=_F32, after=(), name):
    M, K = a.shape
    assert w.shape == (N_DEV, SHARD_COLS, K) and M % tm == 0

    def body(a_ref, w_ref, *rest):
        o_ref = rest[-1]
        a_blk = a_ref[...]
        for s in range(2):
            o_ref[:, s * SHARD_COLS:(s + 1) * SHARD_COLS] = lax.dot_general(
                a_blk, w_ref[s], _NT, preferred_element_type=_F32).astype(o_ref.dtype)

    return pl.pallas_call(
        body, grid=(M // tm, N_DEV // 2),
        in_specs=[pl.BlockSpec((tm, K), lambda i, p: (i, 0)), pl.BlockSpec((2, SHARD_COLS, K), lambda i, p: (p, 0, 0))]
        + [pl.BlockSpec(memory_space=pl.ANY)] * len(after),
        out_specs=pl.BlockSpec((tm, _PAIR), lambda i, p: (i, p)),
        out_shape=jax.ShapeDtypeStruct((M, N_DEV * SHARD_COLS), out_dtype), name=name,
        compiler_params=_cparams(dimension_semantics=("parallel", "parallel")),
    )(a, w, *after)


def _mm_nt_w_slabs(a, w, *, tm, tn, add=None, add_scale=1.0, after=(), name):
    M = a.shape[0]
    N = w.shape[2]
    assert a.shape[1] == N_DEV * SHARD_COLS and w.shape[:2] == (N_DEV, SHARD_COLS) and M % tm == 0 and N % tn == 0

    def body(a_ref, w_ref, *rest):
        o_ref = rest[-1]
        acc = add_scale * rest[0][...] if add is not None else None
        for j in range(N_DEV):
            t = jnp.dot(a_ref[:, j * SHARD_COLS:(j + 1) * SHARD_COLS], w_ref[j], preferred_element_type=_F32)
            acc = t if acc is None else acc + t
        o_ref[...] = acc

    o_spec = pl.BlockSpec((tm, tn), lambda i, j: (i, j))
    return pl.pallas_call(
        body, grid=(M // tm, N // tn),
        in_specs=[pl.BlockSpec((tm, N_DEV * SHARD_COLS), lambda i, j: (i, 0)),
                  pl.BlockSpec((N_DEV, SHARD_COLS, tn), lambda i, j: (0, 0, j))]
        + ([o_spec] if add is not None else []) + [pl.BlockSpec(memory_space=pl.ANY)] * len(after),
        out_specs=o_spec, out_shape=jax.ShapeDtypeStruct((M, N), _F32), name=name,
        compiler_params=_cparams(dimension_semantics=("parallel", "parallel")),
    )(a, w, *((add,) if add is not None else ()), *after)


def _ln_fwd(x, y, g, b, *, name):
    tm = 256

    def body(x_ref, y_ref, g_ref, b_ref, z_ref, o_ref, ob_ref):
        z = ALPHA * x_ref[...] + y_ref[...]
        mu = jnp.mean(z, axis=-1, keepdims=True)
        zc = z - mu
        var = jnp.mean(zc * zc, axis=-1, keepdims=True)
        o = zc * lax.rsqrt(var + LN_EPS) * g_ref[...] + b_ref[...]
        z_ref[...] = z
        o_ref[...] = o
        ob_ref[...] = o.astype(ob_ref.dtype)

    row = pl.BlockSpec((tm, D_MODEL), lambda i: (i, 0))
    vec = pl.BlockSpec((1, D_MODEL), lambda i: (0, 0))
    return pl.pallas_call(
        body, grid=(SEQ // tm,), in_specs=[row, row, vec, vec], out_specs=[row, row, row],
        out_shape=[jax.ShapeDtypeStruct((SEQ, D_MODEL), _F32), jax.ShapeDtypeStruct((SEQ, D_MODEL), _F32),
                   jax.ShapeDtypeStruct((SEQ, D_MODEL), _MXU_DTYPE)],
        name=name, compiler_params=_cparams(dimension_semantics=("parallel",)),
    )(x, y, g.reshape(1, D_MODEL), b.reshape(1, D_MODEL))


def _ln_bwd(z, d_a, d_res, g, *, name):
    tm = 256

    def body(*refs):
        if d_res is None:
            z_ref, da_ref, g_ref, dz_ref, dzb_ref, dg_ref, db_ref = refs
            dout = da_ref[...]
        else:
            z_ref, da_ref, dr_ref, g_ref, dz_ref, dzb_ref, dg_ref, db_ref = refs
            dout = da_ref[...] + ALPHA * dr_ref[...]
        z = z_ref[...]
        mu = jnp.mean(z, axis=-1, keepdims=True)
        zc = z - mu
        var = jnp.mean(zc * zc, axis=-1, keepdims=True)
        rstd = lax.rsqrt(var + LN_EPS)
        xh = zc * rstd
        dxh = dout * g_ref[...]
        m1 = jnp.mean(dxh, axis=-1, keepdims=True)
        m2 = jnp.mean(dxh * xh, axis=-1, keepdims=True)
        dz = rstd * (dxh - m1 - xh * m2)
        dz_ref[...] = dz
        dzb_ref[...] = dz.astype(dzb_ref.dtype)

        @pl.when(pl.program_id(0) == 0)
        def _():
            dg_ref[...] = jnp.zeros_like(dg_ref)
            db_ref[...] = jnp.zeros_like(db_ref)

        dg_ref[0:1, :] += jnp.sum(dout * xh, axis=0, keepdims=True)
        db_ref[0:1, :] += jnp.sum(dout, axis=0, keepdims=True)

    row = pl.BlockSpec((tm, D_MODEL), lambda i: (i, 0))
    vec = pl.BlockSpec((1, D_MODEL), lambda i: (0, 0))
    acc = pl.BlockSpec((8, D_MODEL), lambda i: (0, 0))
    ins = [z, d_a] + ([d_res] if d_res is not None else []) + [g.reshape(1, D_MODEL)]
    in_specs = [row, row] + ([row] if d_res is not None else []) + [vec]
    dz, dzb, dg, db = pl.pallas_call(
        body, grid=(SEQ // tm,), in_specs=in_specs, out_specs=[row, row, acc, acc],
        out_shape=[jax.ShapeDtypeStruct((SEQ, D_MODEL), _F32), jax.ShapeDtypeStruct((SEQ, D_MODEL), _MXU_DTYPE),
                   jax.ShapeDtypeStruct((8, D_MODEL), _F32), jax.ShapeDtypeStruct((8, D_MODEL), _F32)],
        name=name, compiler_params=_cparams(dimension_semantics=("arbitrary",)),
    )(*ins)
    return dz, dzb, dg[0], db[0]


def _loss_head(y, target, *, name):
    tm = 256

    def body(y_ref, t_ref, d_ref, l_ref):
        e = y_ref[...] - t_ref[...]
        d_ref[...] = e * (1.0 / D_MODEL)

        @pl.when(pl.program_id(0) == 0)
        def _():
            l_ref[...] = jnp.zeros_like(l_ref)

        l_ref[...] += (0.5 / D_MODEL) * jnp.sum(e * e)

    row = pl.BlockSpec((tm, D_MODEL), lambda i: (i, 0))
    d, l = pl.pallas_call(
        body, grid=(SEQ // tm,), in_specs=[row, row], out_specs=[row, pl.BlockSpec((8, 128), lambda i: (0, 0))],
        out_shape=[jax.ShapeDtypeStruct((SEQ, D_MODEL), _F32), jax.ShapeDtypeStruct((8, 128), _F32)],
        name=name, compiler_params=_cparams(dimension_semantics=("arbitrary",)),
    )(y, target)
    return l[0, 0], d


_CONV_TN = 256


def _shift_down(v, k, rows):
    return jnp.where(rows >= k, pltpu.roll(v, k, axis=0), 0.0)


def _shift_up(v, k, rows):
    return jnp.where(rows < SEQ - k, pltpu.roll(v, SEQ - k, axis=0), 0.0)


def _conv_gate_fwd(g, u, conv_w, conv_b, *, name):
    def body(g_ref, u_ref, w_ref, b_ref, h_ref):
        gv = g_ref[...].astype(_F32)
        rows = lax.broadcasted_iota(jnp.int32, gv.shape, 0)
        w = w_ref[...]
        gc = b_ref[...] + w[2:3, :] * gv + w[1:2, :] * _shift_down(gv, 1, rows) + w[0:1, :] * _shift_down(gv, 2, rows)
        h_ref[...] = (gc * _sigmoid(gc) * u_ref[...].astype(_F32)).astype(h_ref.dtype)

    col = pl.BlockSpec((SEQ, _CONV_TN), lambda j: (0, j))
    return pl.pallas_call(
        body, grid=(D_FF // _CONV_TN,),
        in_specs=[col, col, pl.BlockSpec((3, _CONV_TN), lambda j: (0, j)), pl.BlockSpec((1, _CONV_TN), lambda j: (0, j))],
        out_specs=col, out_shape=jax.ShapeDtypeStruct((SEQ, D_FF), _MXU_DTYPE), name=name,
        compiler_params=_cparams(dimension_semantics=("parallel",)),
    )(g, u, conv_w, conv_b.reshape(1, D_FF))


def _conv_gate_bwd(dh, g, u, conv_w, conv_b, *, name):
    def body(dh_ref, g_ref, u_ref, w_ref, b_ref, dg_ref, du_ref, dw_ref, db_ref):
        gv = g_ref[...].astype(_F32)
        rows = lax.broadcasted_iota(jnp.int32, gv.shape, 0)
        w = w_ref[...]
        g1 = _shift_down(gv, 1, rows)
        g2 = _shift_down(gv, 2, rows)
        gc = b_ref[...] + w[2:3, :] * gv + w[1:2, :] * g1 + w[0:1, :] * g2
        sg = _sigmoid(gc)
        dh = dh_ref[...].astype(_F32)
        du_ref[...] = (dh * (gc * sg)).astype(du_ref.dtype)
        dgc = dh * u_ref[...].astype(_F32) * (sg * (1.0 + gc * (1.0 - sg)))
        dg = w[2:3, :] * dgc + w[1:2, :] * _shift_up(dgc, 1, rows) + w[0:1, :] * _shift_up(dgc, 2, rows)
        dg_ref[...] = dg.astype(dg_ref.dtype)
        dw_ref[0:1, :] = jnp.sum(dgc * g2, axis=0, keepdims=True)
        dw_ref[1:2, :] = jnp.sum(dgc * g1, axis=0, keepdims=True)
        dw_ref[2:3, :] = jnp.sum(dgc * gv, axis=0, keepdims=True)
        db_ref[...] = jnp.sum(dgc, axis=0, keepdims=True)

    col = pl.BlockSpec((SEQ, _CONV_TN), lambda j: (0, j))
    w3 = pl.BlockSpec((3, _CONV_TN), lambda j: (0, j))
    w1 = pl.BlockSpec((1, _CONV_TN), lambda j: (0, j))
    dg, du, dw, db = pl.pallas_call(
        body, grid=(D_FF // _CONV_TN,), in_specs=[col, col, col, w3, w1], out_specs=[col, col, w3, w1],
        out_shape=[jax.ShapeDtypeStruct((SEQ, D_FF), _MXU_DTYPE), jax.ShapeDtypeStruct((SEQ, D_FF), _MXU_DTYPE),
                   jax.ShapeDtypeStruct((3, D_FF), _F32), jax.ShapeDtypeStruct((1, D_FF), _F32)],
        name=name, compiler_params=_cparams(dimension_semantics=("parallel",)),
    )(dh, g, u, conv_w, conv_b.reshape(1, D_FF))
    return dg, du, dw, db[0]


def _rope_tables():
    half = ROPE_DIM // 2
    inv = ROPE_THETA ** (-jnp.arange(0, ROPE_DIM, 2, dtype=_F32) / ROPE_DIM)
    ang = jnp.arange(SEQ, dtype=_F32)[:, None] * inv[None, :]
    cos, sin = jnp.cos(ang), jnp.sin(ang)
    rest = HEAD_DIM - ROPE_DIM
    c = jnp.concatenate([cos, cos, jnp.ones((SEQ, rest), _F32)], axis=1)
    s1 = jnp.concatenate([-sin, jnp.zeros((SEQ, HEAD_DIM - half), _F32)], axis=1)
    s2 = jnp.concatenate([jnp.zeros((SEQ, half), _F32), sin, jnp.zeros((SEQ, rest), _F32)], axis=1)
    return c, s1, s2


def _rope_apply(x, c, s1, s2):
    return x * c + pltpu.roll(x, HEAD_DIM - ROPE_DIM // 2, axis=1) * s1 + pltpu.roll(x, ROPE_DIM // 2, axis=1) * s2


def _rope_transpose(d, c, s1, s2):
    half = ROPE_DIM // 2
    return d * c + pltpu.roll(d * s1, half, axis=1) + pltpu.roll(d * s2, HEAD_DIM - half, axis=1)


_NT = (((1,), (1,)), ((), ()))
_TN = (((0,), (0,)), ((), ()))
_SCALE = HEAD_DIM ** -0.5


def _band_scores(q, k2, n, lag_off):
    s = lax.dot_general(q, k2, _NT, preferred_element_type=_F32) * _SCALE
    row = lax.broadcasted_iota(jnp.int32, (BLOCK, 2 * BLOCK), 0)
    col = lax.broadcasted_iota(jnp.int32, (BLOCK, 2 * BLOCK), 1)
    front = (col >= row + lag_off) & (col < BLOCK) & (n > 0)
    own = (col >= BLOCK) & (col <= row + BLOCK)
    return jnp.where(front | own, s, _NEG)


_BAND_STEPS = SEQ // BLOCK


def _rows(start, d):
    if d == 1:
        return pl.ds(pl.multiple_of(start, BLOCK), BLOCK)
    return pl.ds(start, BLOCK, stride=d)


def _band_block(it, d):
    r, n = it % d, it // d
    span = BLOCK * d
    return n, _rows(r + n * span, d), _rows(r + jnp.maximum(n - 1, 0) * span, d)


def _band_fwd(proj, tabs, *, kv_heads, q_per_kv, q0, k0, v0, dilations, lag_off, sink, name):
    heads = kv_heads * q_per_kv

    def body(*refs):
        q_refs = refs[:q_per_kv]
        k_ref, v_ref, c_ref, s1_ref, s2_ref = refs[q_per_kv:q_per_kv + 5]
        rest = refs[q_per_kv + 5:]
        if sink is not None:
            sk_ref, rest = rest[0], rest[1:]
        o_ref, lse_ref, qs, ks, m_s, l_s, acc_s = rest
        c, s1, s2 = c_ref[...], s1_ref[...], s2_ref[...]
        ks[...] = _rope_apply(k_ref[...], c, s1, s2)
        for i in range(q_per_kv):
            qs[...] = _rope_apply(q_refs[i][...], c, s1, s2)
            for pi, d in enumerate(dilations):
                def step(it, carry, d=d, first=(pi == 0)):
                    n, cur, prev = _band_block(it, d)
                    q = qs[cur, :].astype(_MXU_DTYPE)
                    k2 = jnp.concatenate([ks[prev, :], ks[cur, :]], axis=0).astype(_MXU_DTYPE)
                    v2 = jnp.concatenate([v_ref[prev, :], v_ref[cur, :]], axis=0).astype(_MXU_DTYPE)
                    s = _band_scores(q, k2, n, lag_off)
                    m_b = jnp.max(s, axis=1, keepdims=True)
                    m_new = m_b if first else jnp.maximum(m_b, m_s[cur, :][:, 0:1])
                    p = jnp.exp(s - m_new)
                    l_new = jnp.sum(p, axis=1, keepdims=True)
                    acc = jnp.dot(p.astype(_MXU_DTYPE), v2, preferred_element_type=_F32)
                    if not first:
                        a = jnp.exp(m_s[cur, :][:, 0:1] - m_new)
                        l_new = l_new + a * l_s[cur, :][:, 0:1]
                        acc = acc + a * acc_s[cur, :]
                    m_s[cur, :] = jnp.broadcast_to(m_new, (BLOCK, HEAD_DIM))
                    l_s[cur, :] = jnp.broadcast_to(l_new, (BLOCK, HEAD_DIM))
                    acc_s[cur, :] = acc
                    return carry

                lax.fori_loop(0, _BAND_STEPS, step, 0, unroll=4)
            m, den = m_s[...], l_s[...]
            if sink is not None:
                sk = sk_ref[i]
                m_f = jnp.maximum(m, sk)
                a = jnp.exp(m - m_f)
                den = den * a + jnp.exp(sk - m_f)
                o = acc_s[...] * a / den
                m = m_f
            else:
                o = acc_s[...] / den
            o_ref[:, i * HEAD_DIM:(i + 1) * HEAD_DIM] = o
            lse_ref[:, i * HEAD_DIM:(i + 1) * HEAD_DIM] = m + jnp.log(den)

    col = (SEQ, HEAD_DIM)
    in_specs = [pl.BlockSpec(col, functools.partial(lambda g, i: (0, A_COLS + q0 + g * q_per_kv + i), i=i)) for i in range(q_per_kv)]
    in_specs += [pl.BlockSpec(col, lambda g: (0, A_COLS + k0 + g)), pl.BlockSpec(col, lambda g: (0, A_COLS + v0 + g))]
    in_specs += [pl.BlockSpec(col, lambda g: (0, 0))] * 3
    args = [proj] * (q_per_kv + 2) + list(tabs)
    if sink is not None:
        in_specs.append(pl.BlockSpec((q_per_kv, 1, HEAD_DIM), lambda g: (g, 0, 0)))
        args.append(jnp.broadcast_to(sink.reshape(heads, 1, 1), (heads, 1, HEAD_DIM)))
    o_spec = pl.BlockSpec((SEQ, q_per_kv * HEAD_DIM), lambda g: (0, g))
    shape = jax.ShapeDtypeStruct((SEQ, heads * HEAD_DIM), _F32)
    return pl.pallas_call(
        body, grid=(kv_heads,), in_specs=in_specs, out_specs=[o_spec, o_spec], out_shape=[shape, shape],
        scratch_shapes=[pltpu.VMEM(col, _F32)] * 5, name=name,
        compiler_params=_cparams(dimension_semantics=("parallel",)),
    )(*args)


def _band_bwd(proj, tabs, dmixed, o, lse, *, kv_heads, q_per_kv, q0, k0, v0, do0, dilations, lag_off, sink, after=(), name):
    heads = kv_heads * q_per_kv

    def body(*refs):
        q_refs = refs[:q_per_kv]
        k_ref, v_ref, c_ref, s1_ref, s2_ref = refs[q_per_kv:q_per_kv + 5]
        do_refs = refs[q_per_kv + 5:2 * q_per_kv + 5]
        o_ref, lse_ref = refs[2 * q_per_kv + 5:2 * q_per_kv + 7]
        rest = refs[2 * q_per_kv + 7:]
        if sink is not None:
            sk_ref, rest = rest[0], rest[1:]
            dq_ref, dk_ref, dv_ref, dsk_ref, qs, ks, dq_s, dk_s, dv_s = rest[len(after):]
        else:
            dq_ref, dk_ref, dv_ref, qs, ks, dq_s, dk_s, dv_s = rest[len(after):]
        c, s1, s2 = c_ref[...], s1_ref[...], s2_ref[...]
        ks[...] = _rope_apply(k_ref[...], c, s1, s2)
        dk_s[...] = jnp.zeros_like(dk_s)
        dv_s[...] = jnp.zeros_like(dv_s)
        for i in range(q_per_kv):
            hs = slice(i * HEAD_DIM, (i + 1) * HEAD_DIM)
            qs[...] = _rope_apply(q_refs[i][...], c, s1, s2)
            dq_s[...] = jnp.zeros_like(dq_s)
            do_ref = do_refs[i]
            for d in dilations:
                def step(it, carry, d=d, do_ref=do_ref, hs=hs):
                    n, cur, prev = _band_block(it, d)
                    q = qs[cur, :].astype(_MXU_DTYPE)
                    k2 = jnp.concatenate([ks[prev, :], ks[cur, :]], axis=0).astype(_MXU_DTYPE)
                    v2 = jnp.concatenate([v_ref[prev, :], v_ref[cur, :]], axis=0).astype(_MXU_DTYPE)
                    do = do_ref[cur, :]
                    delta = jnp.sum(do * o_ref[cur, hs], axis=1, keepdims=True)
                    lse_c = lse_ref[cur, hs][:, 0:1]
                    p = jnp.exp(_band_scores(q, k2, n, lag_off) - lse_c)
                    dob = do.astype(_MXU_DTYPE)
                    ds = (p * (lax.dot_general(dob, v2, _NT, preferred_element_type=_F32) - delta) * _SCALE).astype(_MXU_DTYPE)
                    dq_s[cur, :] += jnp.dot(ds, k2, preferred_element_type=_F32)
                    dk2 = lax.dot_general(ds, q, _TN, preferred_element_type=_F32)
                    dv2 = lax.dot_general(p.astype(_MXU_DTYPE), dob, _TN, preferred_element_type=_F32)
                    dk_s[prev, :] += dk2[:BLOCK]
                    dv_s[prev, :] += dv2[:BLOCK]
                    dk_s[cur, :] += dk2[BLOCK:]
                    dv_s[cur, :] += dv2[BLOCK:]
                    return carry

                lax.fori_loop(0, _BAND_STEPS, step, 0, unroll=4)
            dq_ref[:, hs] = _rope_transpose(dq_s[...], c, s1, s2).astype(dq_ref.dtype)
            if sink is not None:
                delta = jnp.sum(do_ref[...] * o_ref[:, hs], axis=1, keepdims=True)
                w_sink = jnp.exp(sk_ref[i] - lse_ref[:, hs])
                dsk_ref[i] = jnp.broadcast_to(jnp.sum(-delta * w_sink[:, 0:1]), (8, HEAD_DIM))
        dk_ref[...] = _rope_transpose(dk_s[...], c, s1, s2).astype(dk_ref.dtype)
        dv_ref[...] = dv_s[...].astype(dv_ref.dtype)

    col = (SEQ, HEAD_DIM)
    in_specs = [pl.BlockSpec(col, functools.partial(lambda g, i: (0, A_COLS + q0 + g * q_per_kv + i), i=i)) for i in range(q_per_kv)]
    in_specs += [pl.BlockSpec(col, lambda g: (0, A_COLS + k0 + g)), pl.BlockSpec(col, lambda g: (0, A_COLS + v0 + g))]
    in_specs += [pl.BlockSpec(col, lambda g: (0, 0))] * 3
    in_specs += [pl.BlockSpec(col, functools.partial(lambda g, i: (0, do0 + g * q_per_kv + i), i=i)) for i in range(q_per_kv)]
    wide = pl.BlockSpec((SEQ, q_per_kv * HEAD_DIM), lambda g: (0, g))
    in_specs += [wide, wide]
    args = [proj] * (q_per_kv + 2) + list(tabs) + [dmixed] * q_per_kv + [o, lse]
    out_specs = [wide, pl.BlockSpec(col, lambda g: (0, g)), pl.BlockSpec(col, lambda g: (0, g))]
    out_shape = [jax.ShapeDtypeStruct((SEQ, heads * HEAD_DIM), _MXU_DTYPE), jax.ShapeDtypeStruct((SEQ, kv_heads * HEAD_DIM), _MXU_DTYPE),
                 jax.ShapeDtypeStruct((SEQ, kv_heads * HEAD_DIM), _MXU_DTYPE)]
    if sink is not None:
        in_specs.append(pl.BlockSpec((q_per_kv, 1, HEAD_DIM), lambda g: (g, 0, 0)))
        args.append(jnp.broadcast_to(sink.reshape(heads, 1, 1), (heads, 1, HEAD_DIM)))
        out_specs.append(pl.BlockSpec((q_per_kv, 8, HEAD_DIM), lambda g: (g, 0, 0)))
        out_shape.append(jax.ShapeDtypeStruct((heads, 8, HEAD_DIM), _F32))
    in_specs += [pl.BlockSpec(memory_space=pl.ANY)] * len(after)
    args += list(after)
    res = pl.pallas_call(
        body, grid=(kv_heads,), in_specs=in_specs, out_specs=out_specs, out_shape=out_shape,
        scratch_shapes=[pltpu.VMEM(col, _F32)] * 5, name=name,
        compiler_params=_cparams(dimension_semantics=("parallel",)),
    )(*args)
    if sink is not None:
        return res[0], res[1], res[2], res[3][:, 0, 0]
    return res


_DILATED = dict(kv_heads=B_HEADS, q_per_kv=1, q0=QB0, k0=KB0, v0=VB0, dilations=DILATIONS, lag_off=0, sink=None)
_SWA = dict(kv_heads=C_KV_HEADS, q_per_kv=C_HEADS // C_KV_HEADS, q0=QC0, k0=KC0, v0=VC0, dilations=(1,), lag_off=1)


_HG_TILE = 128
_HG_CHUNKS = _HG_TILE // A_CHUNK
_HG_TILES = SEQ // _HG_TILE
_HI = lax.Precision.HIGHEST


def _chunk_tri():
    i = np.arange(_HG_TILE)
    return jnp.asarray(((i[:, None] // A_CHUNK == i[None, :] // A_CHUNK) & (i[None, :] <= i[:, None])).astype(np.float32))


def _layer_lb(lb_ref, layer):
    if layer == 0:
        return jnp.zeros((1, HEAD_DIM), _F32)
    lg = lb_ref[...]
    m = jnp.max(lg, axis=0, keepdims=True)
    e = jnp.exp(lg - m)
    return e[1:2, :] / jnp.sum(e, axis=0, keepdims=True)


def _hgrn_gates(q, fr, lb):
    sgq = _sigmoid(q)
    sg = _sigmoid(fr)
    f = lb + (1.0 - lb) * sg
    return sgq, q * sgq, sg, f, 1.0 - f


def _hgrn_fwd(proj, lb_logits, norm_w, layer, *, name):
    tri = _chunk_tri()

    def body(q_ref, f_ref, i_ref, g_ref, lb_ref, nw_ref, tri_ref, o_ref, raw_ref, st_ref, state):
        @pl.when(pl.program_id(1) == 0)
        def _():
            state[...] = jnp.zeros_like(state)

        lb = _layer_lb(lb_ref, layer)
        _, qs, _, f, k = _hgrn_gates(q_ref[...], f_ref[...], lb)
        v = i_ref[...]
        b = jnp.dot(tri_ref[...], jnp.log(f), precision=_HI, preferred_element_type=_F32)
        eb = jnp.exp(b)
        ridx = lax.broadcasted_iota(jnp.int32, (A_CHUNK, HEAD_DIM), 0)
        outs = []
        for c in range(_HG_CHUNKS):
            sl = slice(c * A_CHUNK, (c + 1) * A_CHUNK)
            bc, qc, kc, vc = b[sl], qs[sl], k[sl], v[sl]
            bl = bc[A_CHUNK - 1:A_CHUNK]
            st = state[...]
            st_ref[0, c] = st
            o_c = lax.dot_general((qc * eb[sl]).astype(_MXU_DTYPE), st.astype(_MXU_DTYPE), _NT, preferred_element_type=_F32)
            rows = []
            for i in range(A_CHUNK):
                di = jnp.exp(jnp.where(ridx <= i, bc[i:i + 1] - bc, _NEG))
                a = jnp.sum(qc[i:i + 1] * kc * di, axis=1, keepdims=True)
                rows.append(jnp.sum(a * vc, axis=0, keepdims=True))
            outs.append(o_c + jnp.concatenate(rows, axis=0))
            kt = (kc * jnp.exp(bl - bc)).astype(_MXU_DTYPE)
            state[...] = st * jnp.exp(bl) + lax.dot_general(vc.astype(_MXU_DTYPE), kt, _TN, preferred_element_type=_F32)
        o = jnp.concatenate(outs, axis=0)
        raw_ref[...] = o
        r = lax.rsqrt(jnp.mean(o * o, axis=-1, keepdims=True) + LN_EPS)
        g = g_ref[...]
        o_ref[...] = o * r * nw_ref[...] * (g * _sigmoid(g))

    blk = (_HG_TILE, HEAD_DIM)

    def col(base):
        return pl.BlockSpec(blk, lambda h, t: (t, base + h))

    o_spec = pl.BlockSpec(blk, lambda h, t: (t, h))
    o_shape = jax.ShapeDtypeStruct((SEQ, A_HEADS * HEAD_DIM), _F32)
    return pl.pallas_call(
        body, grid=(A_HEADS, _HG_TILES),
        in_specs=[col(0), col(4), col(8), col(12), pl.BlockSpec((DEPTH, HEAD_DIM), lambda h, t: (0, h)),
                  pl.BlockSpec((1, HEAD_DIM), lambda h, t: (0, 0)), pl.BlockSpec(blk, lambda h, t: (0, 0))],
        out_specs=[o_spec, o_spec, pl.BlockSpec((1, _HG_CHUNKS, HEAD_DIM, HEAD_DIM), lambda h, t: (h, t, 0, 0))],
        out_shape=[o_shape, o_shape, jax.ShapeDtypeStruct((A_HEADS, SEQ // A_CHUNK, HEAD_DIM, HEAD_DIM), _F32)],
        scratch_shapes=[pltpu.VMEM((HEAD_DIM, HEAD_DIM), _F32)], name=name,
        compiler_params=_cparams(dimension_semantics=("parallel", "arbitrary")),
    )(proj, proj, proj, proj, lb_logits, norm_w.reshape(1, HEAD_DIM), tri)


def _hgrn_bwd(proj, lb_logits, norm_w, raw, states, dmixed, layer, *, name):
    tri = _chunk_tri()
    triu = tri.T

    def body(q_ref, f_ref, i_ref, g_ref, lb_ref, nw_ref, tri_ref, triu_ref, raw_ref, do_ref, st_ref,
             dq_ref, df_ref, di_ref, dg_ref, dnw_ref, dlb_ref, dstate):
        @pl.when(pl.program_id(1) == 0)
        def _():
            dstate[...] = jnp.zeros_like(dstate)
            dlb_ref[...] = jnp.zeros_like(dlb_ref)

        @pl.when((pl.program_id(0) == 0) & (pl.program_id(1) == 0))
        def _():
            dnw_ref[...] = jnp.zeros_like(dnw_ref)

        lb = _layer_lb(lb_ref, layer)
        q = q_ref[...]
        sgq, qs, sg, f, k = _hgrn_gates(q, f_ref[...], lb)
        v = i_ref[...]
        b = jnp.dot(tri_ref[...], jnp.log(f), precision=_HI, preferred_element_type=_F32)
        eb = jnp.exp(b)
        g = g_ref[...]
        nw = nw_ref[...]
        o = raw_ref[...]
        dout = do_ref[...]
        sgg = _sigmoid(g)
        r = lax.rsqrt(jnp.mean(o * o, axis=-1, keepdims=True) + LN_EPS)
        dg_ref[...] = (dout * (o * r * nw) * (sgg * (1.0 + g * (1.0 - sgg)))).astype(dg_ref.dtype)
        don = dout * (g * sgg)
        dnw_ref[0:1, :] += jnp.sum(don * o * r, axis=0, keepdims=True)
        dy = don * nw
        do_raw = r * dy - o * (r * r * r) * jnp.mean(o * dy, axis=-1, keepdims=True)

        ridx = lax.broadcasted_iota(jnp.int32, (A_CHUNK, HEAD_DIM), 0)
        dqs_t, dk_t, db_t, dv_t = [None] * _HG_CHUNKS, [None] * _HG_CHUNKS, [None] * _HG_CHUNKS, [None] * _HG_CHUNKS
        for c in reversed(range(_HG_CHUNKS)):
            sl = slice(c * A_CHUNK, (c + 1) * A_CHUNK)
            bc, qc, kc, vc, doc = b[sl], qs[sl], k[sl], v[sl], do_raw[sl]
            bl = bc[A_CHUNK - 1:A_CHUNK]
            ebc = eb[sl]
            ebl = jnp.exp(bl - bc)
            lam = jnp.exp(bl)
            qt = qc * ebc
            kt = kc * ebl
            dst = dstate[...]
            stp = st_ref[0, c]
            dob = doc.astype(_MXU_DTYPE)
            dstb = dst.astype(_MXU_DTYPE)
            dqt = jnp.dot(dob, stp.astype(_MXU_DTYPE), preferred_element_type=_F32)
            dkt = jnp.dot(vc.astype(_MXU_DTYPE), dstb, preferred_element_type=_F32)
            dv = lax.dot_general(kt.astype(_MXU_DTYPE), dstb, _NT, preferred_element_type=_F32)
            dlam = jnp.sum(stp * dst, axis=0, keepdims=True)
            dstate[...] = dst * lam + lax.dot_general(dob, qt.astype(_MXU_DTYPE), _TN, preferred_element_type=_F32)
            dqs_rows = []
            dk_in = jnp.zeros((A_CHUNK, HEAD_DIM), _F32)
            for i in range(A_CHUNK):
                di = jnp.exp(jnp.where(ridx <= i, bc[i:i + 1] - bc, _NEG))
                qi = qc[i:i + 1]
                doi = doc[i:i + 1]
                w = kc * di
                a = jnp.sum(qi * w, axis=1, keepdims=True)
                dv = dv + a * doi
                da = jnp.sum(doi * vc, axis=1, keepdims=True)
                dqs_rows.append(jnp.sum(da * w, axis=0, keepdims=True))
                dk_in = dk_in + da * (qi * di)
            dqs_in = jnp.concatenate(dqs_rows, axis=0)
            dbl = jnp.sum(dkt * kt, axis=0, keepdims=True) + dlam * lam
            db = qc * dqs_in - kc * dk_in + dqt * qt - dkt * kt
            db_t[c] = db + jnp.where(ridx == A_CHUNK - 1, dbl, 0.0)
            dqs_t[c] = dqs_in + dqt * ebc
            dk_t[c] = dk_in + dkt * ebl
            dv_t[c] = dv
        dqs = jnp.concatenate(dqs_t, axis=0)
        dk = jnp.concatenate(dk_t, axis=0)
        db = jnp.concatenate(db_t, axis=0)
        di_ref[...] = jnp.concatenate(dv_t, axis=0).astype(di_ref.dtype)
        dlogf = jnp.dot(triu_ref[...], db, precision=_HI, preferred_element_type=_F32)
        df = dlogf / f - dk
        df_ref[...] = (df * (1.0 - lb) * sg * (1.0 - sg)).astype(df_ref.dtype)
        dlb_ref[0, 0:1, :] += jnp.sum(df * (1.0 - sg), axis=0, keepdims=True)
        dq_ref[...] = (dqs * (sgq * (1.0 + q * (1.0 - sgq)))).astype(dq_ref.dtype)

    blk = (_HG_TILE, HEAD_DIM)
    last = _HG_TILES - 1

    def col(base):
        return pl.BlockSpec(blk, lambda h, t: (last - t, base + h))

    tri_spec = pl.BlockSpec(blk, lambda h, t: (0, 0))
    acc_spec = pl.BlockSpec((1, 8, HEAD_DIM), lambda h, t: (h, 0, 0))
    acc_shape = jax.ShapeDtypeStruct((A_HEADS, 8, HEAD_DIM), _F32)
    dq, df, di, dg, dnw, dlb = pl.pallas_call(
        body, grid=(A_HEADS, _HG_TILES),
        in_specs=[col(0), col(4), col(8), col(12), pl.BlockSpec((DEPTH, HEAD_DIM), lambda h, t: (0, h)),
                  pl.BlockSpec((1, HEAD_DIM), lambda h, t: (0, 0)), tri_spec, tri_spec, col(0), col(0),
                  pl.BlockSpec((1, _HG_CHUNKS, HEAD_DIM, HEAD_DIM), lambda h, t: (h, last - t, 0, 0))],
        out_specs=[col(0), col(0), col(0), col(0), pl.BlockSpec((8, HEAD_DIM), lambda h, t: (0, 0)), acc_spec],
        out_shape=[jax.ShapeDtypeStruct((SEQ, A_HEADS * HEAD_DIM), _MXU_DTYPE)] * 4
        + [jax.ShapeDtypeStruct((8, HEAD_DIM), _F32), acc_shape],
        scratch_shapes=[pltpu.VMEM((HEAD_DIM, HEAD_DIM), _F32)], name=name,
        compiler_params=_cparams(dimension_semantics=("arbitrary", "arbitrary")),
    )(proj, proj, proj, proj, lb_logits, norm_w.reshape(1, HEAD_DIM), tri, triu, raw, dmixed, states)
    return dq, df, di, dg, dnw[0], dlb[:, 0, :].reshape(A_HEADS * HEAD_DIM)


def _exchange(arrays, scatter, *, name):
    n = len(arrays)
    n_peer = N_DEV - 1

    def body(*refs):
        ins, outs = refs[:n], refs[n:2 * n]
        send_sems, recv_sems, loc_sems = refs[2 * n:]
        x, y, c = lax.axis_index("x"), lax.axis_index("y"), lax.axis_index("c")
        me = 4 * x + 2 * y + c
        local = []
        for a in range(n):
            cp = pltpu.make_async_copy(ins[a].at[me] if scatter else ins[a], outs[a].at[me], loc_sems.at[a])
            cp.start()
            local.append(cp)

        def peer(k):
            px = jnp.bitwise_xor(x, (k >> 2) & 1)
            py = jnp.bitwise_xor(y, (k >> 1) & 1)
            pc = jnp.bitwise_xor(c, k & 1)
            return (px, py, pc), 4 * px + 2 * py + pc

        def copy(a, k):
            dev, pid = peer(k)
            return pltpu.make_async_remote_copy(
                src_ref=ins[a].at[pid] if scatter else ins[a], dst_ref=outs[a].at[me],
                send_sem=send_sems.at[a * n_peer + k - 1], recv_sem=recv_sems.at[a * n_peer + k - 1],
                device_id=dev, device_id_type=pl.DeviceIdType.MESH)

        def arrival(a, k):
            dev, pid = peer(k)
            return pltpu.make_async_remote_copy(
                src_ref=ins[a].at[pid] if scatter else ins[a], dst_ref=outs[a].at[pid],
                send_sem=send_sems.at[a * n_peer + k - 1], recv_sem=recv_sems.at[a * n_peer + k - 1],
                device_id=dev, device_id_type=pl.DeviceIdType.MESH)

        sends = [copy(a, k) for k in range(1, N_DEV) for a in range(n)]
        for cp in sends:
            cp.start()
        for k in range(1, N_DEV):
            for a in range(n):
                arrival(a, k).wait_recv()
        for cp in sends:
            cp.wait_send()
        for cp in local:
            cp.wait()

    def out_shape(a):
        blk = a.shape[1:] if scatter else a.shape
        return jax.ShapeDtypeStruct((N_DEV,) + tuple(blk), a.dtype)

    any_spec = pl.BlockSpec(memory_space=pl.ANY)
    return pl.pallas_call(
        body, in_specs=[any_spec] * n, out_specs=[any_spec] * n, out_shape=[out_shape(a) for a in arrays],
        scratch_shapes=[pltpu.SemaphoreType.DMA((n * n_peer,)), pltpu.SemaphoreType.DMA((n * n_peer,)),
                        pltpu.SemaphoreType.DMA((n,))],
        name=name, compiler_params=pltpu.CompilerParams(has_side_effects=True),
    )(*arrays)


N_CHIP = N_DEV // 2
_MESH_ID = pl.DeviceIdType.MESH


def _place():
    x, y, c = lax.axis_index("x"), lax.axis_index("y"), lax.axis_index("c")
    chips = [(1 - x, y), (x, 1 - y), (1 - x, 1 - y)]
    return x, y, c, 2 * x + y, chips


def _sibling_swap(arrays, *, name):
    n = len(arrays)

    def body(*refs):
        ins, outs = refs[:n], refs[n:2 * n]
        send_sems, recv_sems = refs[2 * n:]
        x, y, c, _, _ = _place()
        copies = [pltpu.make_async_remote_copy(
            src_ref=ins[a].at[:, 1 - c], dst_ref=outs[a], send_sem=send_sems.at[a], recv_sem=recv_sems.at[a],
            device_id=(x, y, 1 - c), device_id_type=_MESH_ID) for a in range(n)]
        for cp in copies:
            cp.start()
        for cp in copies:
            cp.wait()

    any_spec = pl.BlockSpec(memory_space=pl.ANY)
    return pl.pallas_call(
        body, in_specs=[any_spec] * n, out_specs=[any_spec] * n,
        out_shape=[jax.ShapeDtypeStruct((N_CHIP,) + a.shape[2:], a.dtype) for a in arrays],
        scratch_shapes=[pltpu.SemaphoreType.DMA((n,)), pltpu.SemaphoreType.DMA((n,))],
        name=name, compiler_params=pltpu.CompilerParams(has_side_effects=True),
    )(*arrays)


def _pair_add(mine, theirs, core, *, name):
    _, _, R, C = mine.shape
    tr = max(t for t in range(16, R + 1, 16) if R % t == 0 and t * C <= 512 * 1024)

    def body(core_ref, m_ref, t_ref, o_ref):
        del core_ref
        o_ref[...] = (m_ref[...].astype(_F32) + t_ref[...].astype(_F32)).astype(o_ref.dtype)

    grid_spec = pltpu.PrefetchScalarGridSpec(
        num_scalar_prefetch=1, grid=(N_CHIP, R // tr),
        in_specs=[pl.BlockSpec((None, None, tr, C), lambda q, i, core: (q, core[0], i, 0)),
                  pl.BlockSpec((None, tr, C), lambda q, i, core: (q, i, 0))],
        out_specs=pl.BlockSpec((None, tr, C), lambda q, i, core: (q, i, 0)))
    return pl.pallas_call(
        body, grid_spec=grid_spec, out_shape=jax.ShapeDtypeStruct((N_CHIP, R, C), mine.dtype), name=name,
        compiler_params=_cparams(dimension_semantics=("parallel", "parallel")),
    )(core.reshape(1), mine, theirs)


_HBM = pl.BlockSpec(memory_space=pltpu.HBM)
_SEM = pl.BlockSpec(memory_space=pltpu.SEMAPHORE)
_TOKEN = pl.BlockSpec(memory_space=pltpu.VMEM)
_DATAFLOW = pltpu.SideEffectType.DATAFLOW_SIDE_EFFECTING


def _hbm(a):
    return pltpu.HBM(a.shape, a.dtype)


def _token_shape():
    return jax.ShapeDtypeStruct((8, 128), _F32)


def _dev_slot(px, py, pc):
    return 4 * px + 2 * py + pc


def _gather_start(blocks, landings, *, name):
    n = len(blocks)

    def body(*refs):
        ins, lands = refs[:n], refs[n:2 * n]
        send_sems, d2d_sems, ici_sems = refs[2 * n:2 * n + 3]
        token = refs[-1]
        x, y, c, _, chips = _place()
        for a in range(n):
            dst = lands[a].at[_dev_slot(x, y, c)]
            pltpu.make_async_remote_copy(src_ref=ins[a], dst_ref=dst, send_sem=send_sems.at[4 * a], recv_sem=d2d_sems.at[a],
                                         device_id=(x, y, 1 - c), device_id_type=_MESH_ID).start()
            for j, chip in enumerate(chips):
                pltpu.make_async_remote_copy(src_ref=ins[a], dst_ref=dst, send_sem=send_sems.at[4 * a + 1 + j],
                                             recv_sem=ici_sems.at[3 * a + j], device_id=(*chip, c),
                                             device_id_type=_MESH_ID).start()
        token[...] = jnp.zeros_like(token)

    res = pl.pallas_call(
        body, name=name, in_specs=[_HBM] * (2 * n),
        out_shape=(pltpu.SemaphoreType.DMA((4 * n,)), pltpu.SemaphoreType.DMA((n,)), pltpu.SemaphoreType.DMA((3 * n,)),
                   *[_hbm(b) for b in blocks], *[_hbm(b) for b in landings], _token_shape()),
        out_specs=(_SEM, _SEM, _SEM, *[_HBM] * (2 * n), _TOKEN),
        input_output_aliases={i: 3 + i for i in range(2 * n)},
        compiler_params=pltpu.CompilerParams(has_side_effects=_DATAFLOW),
    )(*[pltpu.with_memory_space_constraint(b, pltpu.HBM) for b in blocks],
      *[pltpu.with_memory_space_constraint(b, pltpu.HBM) for b in landings])
    return res[0], res[1], res[2], list(res[3:3 + n]), list(res[3 + n:3 + 2 * n]), res[-1]


def _gather_forward(landings, ici_sems, first, after, *, name):
    n = len(landings)

    def body(*refs):
        lands = refs[:n]
        ici = refs[n]
        f_send, f_recv = refs[n + 2], refs[n + 3]
        token = refs[-1]
        x, y, c, _, chips = _place()
        for a in range(n):
            for j, chip in enumerate(chips):
                blk = lands[a].at[_dev_slot(*chip, c)]
                pltpu.make_async_remote_copy(src_ref=blk, dst_ref=blk, send_sem=f_send.at[3 * a + j],
                                             recv_sem=ici.at[3 * (first + a) + j], device_id=(*chip, c),
                                             device_id_type=_MESH_ID).wait_recv()
                pltpu.make_async_remote_copy(src_ref=blk, dst_ref=blk, send_sem=f_send.at[3 * a + j], recv_sem=f_recv.at[3 * a + j],
                                             device_id=(x, y, 1 - c), device_id_type=_MESH_ID).start()
        token[...] = jnp.zeros_like(token)

    res = pl.pallas_call(
        body, name=name, in_specs=[_HBM] * n + [_SEM, pl.BlockSpec(memory_space=pl.ANY)],
        out_shape=(pltpu.SemaphoreType.DMA((3 * n,)), pltpu.SemaphoreType.DMA((3 * n,)), *[_hbm(b) for b in landings], _token_shape()),
        out_specs=(_SEM, _SEM, *[_HBM] * n, _TOKEN),
        input_output_aliases={i: 2 + i for i in range(n)},
        compiler_params=pltpu.CompilerParams(has_side_effects=_DATAFLOW),
    )(*landings, ici_sems, after)
    return res[0], res[1], list(res[2:2 + n]), res[-1]


def _gather_wait(blocks, landings, send_sems, d2d_sems, first, f_send, f_recv, after, *, name):
    n = len(landings)

    def body(*refs):
        ins, lands = refs[:n], refs[n:2 * n]
        send, d2d, fs, fr = refs[2 * n:2 * n + 4]
        x, y, c, _, chips = _place()
        me = (x, y, c)
        for a in range(n):
            own = lands[a].at[_dev_slot(x, y, 1 - c)]
            g = first + a
            pltpu.make_async_remote_copy(src_ref=ins[a], dst_ref=own, send_sem=send.at[4 * g], recv_sem=d2d.at[g],
                                         device_id=me, device_id_type=_MESH_ID).wait_recv()
            for j, chip in enumerate(chips):
                blk = lands[a].at[_dev_slot(*chip, 1 - c)]
                pltpu.make_async_remote_copy(src_ref=blk, dst_ref=blk, send_sem=fs.at[3 * a + j], recv_sem=fr.at[3 * a + j],
                                             device_id=me, device_id_type=_MESH_ID).wait_recv()
            for k in range(4):
                pltpu.make_async_remote_copy(src_ref=ins[a], dst_ref=own, send_sem=send.at[4 * g + k], recv_sem=d2d.at[g],
                                             device_id=me, device_id_type=_MESH_ID).wait_send()
            for j in range(3):
                pltpu.make_async_remote_copy(src_ref=own, dst_ref=own, send_sem=fs.at[3 * a + j], recv_sem=fr.at[3 * a + j],
                                             device_id=me, device_id_type=_MESH_ID).wait_send()

    res = pl.pallas_call(
        body, name=name, in_specs=[_HBM] * (2 * n) + [_SEM] * 4 + [pl.BlockSpec(memory_space=pl.ANY)],
        out_shape=(*[_hbm(b) for b in blocks], *[_hbm(b) for b in landings]), out_specs=tuple([_HBM] * (2 * n)),
        input_output_aliases={i: i for i in range(2 * n)},
        compiler_params=pltpu.CompilerParams(has_side_effects=_DATAFLOW),
    )(*blocks, *landings, send_sems, d2d_sems, f_send, f_recv, after)
    return list(res[n:])


def _swap_start(mine, landings, *, name):
    n = len(mine)

    def body(*refs):
        ins, lands = refs[:n], refs[n:2 * n]
        send_sems, recv_sems = refs[2 * n:2 * n + 2]
        token = refs[-1]
        x, y, c, _, _ = _place()
        for a in range(n):
            pltpu.make_async_remote_copy(src_ref=ins[a].at[:, 1 - c], dst_ref=lands[a], send_sem=send_sems.at[a],
                                         recv_sem=recv_sems.at[a], device_id=(x, y, 1 - c), device_id_type=_MESH_ID).start()
        token[...] = jnp.zeros_like(token)

    res = pl.pallas_call(
        body, name=name, in_specs=[_HBM] * (2 * n),
        out_shape=(pltpu.SemaphoreType.DMA((n,)), pltpu.SemaphoreType.DMA((n,)),
                   *[_hbm(b) for b in mine], *[_hbm(b) for b in landings], _token_shape()),
        out_specs=(_SEM, _SEM, *[_HBM] * (2 * n), _TOKEN),
        input_output_aliases={i: 2 + i for i in range(2 * n)},
        compiler_params=pltpu.CompilerParams(has_side_effects=_DATAFLOW),
    )(*[pltpu.with_memory_space_constraint(b, pltpu.HBM) for b in mine],
      *[pltpu.with_memory_space_constraint(b, pltpu.HBM) for b in landings])
    return res[0], res[1], list(res[2:2 + n]), list(res[2 + n:2 + 2 * n]), res[-1]


def _swap_wait(mine, landings, send_sems, recv_sems, after, *, name):
    n = len(mine)

    def body(*refs):
        ins, lands = refs[:n], refs[n:2 * n]
        send, recv = refs[2 * n:2 * n + 2]
        x, y, c, _, _ = _place()
        for a in range(n):
            cp = pltpu.make_async_remote_copy(src_ref=ins[a].at[:, 1 - c], dst_ref=lands[a], send_sem=send.at[a],
                                              recv_sem=recv.at[a], device_id=(x, y, c), device_id_type=_MESH_ID)
            cp.wait_recv()
            cp.wait_send()

    res = pl.pallas_call(
        body, name=name, in_specs=[_HBM] * (2 * n) + [_SEM] * 2 + [pl.BlockSpec(memory_space=pl.ANY)],
        out_shape=(*[_hbm(b) for b in mine], *[_hbm(b) for b in landings]), out_specs=tuple([_HBM] * (2 * n)),
        input_output_aliases={i: i for i in range(2 * n)},
        compiler_params=pltpu.CompilerParams(has_side_effects=_DATAFLOW),
    )(*mine, *landings, send_sems, recv_sems, after)
    return list(res[:n]), list(res[n:])


def _chip_exchange_start(sums, landings, *, name):
    n = len(sums)

    def body(*refs):
        ins, lands = refs[:n], refs[n:2 * n]
        send_sems, recv_sems = refs[2 * n:2 * n + 2]
        token = refs[-1]
        _, _, c, p, chips = _place()
        for a in range(n):
            for j, (qx, qy) in enumerate(chips):
                pltpu.make_async_remote_copy(src_ref=ins[a].at[2 * qx + qy], dst_ref=lands[a].at[p], send_sem=send_sems.at[3 * a + j],
                                             recv_sem=recv_sems.at[3 * a + j], device_id=(qx, qy, c), device_id_type=_MESH_ID).start()
        token[...] = jnp.zeros_like(token)

    res = pl.pallas_call(
        body, name=name, in_specs=[_HBM] * (2 * n),
        out_shape=(pltpu.SemaphoreType.DMA((3 * n,)), pltpu.SemaphoreType.DMA((3 * n,)),
                   *[_hbm(b) for b in sums], *[_hbm(b) for b in landings], _token_shape()),
        out_specs=(_SEM, _SEM, *[_HBM] * (2 * n), _TOKEN),
        input_output_aliases={i: 2 + i for i in range(2 * n)},
        compiler_params=pltpu.CompilerParams(has_side_effects=_DATAFLOW),
    )(*[pltpu.with_memory_space_constraint(b, pltpu.HBM) for b in sums],
      *[pltpu.with_memory_space_constraint(b, pltpu.HBM) for b in landings])
    return res[0], res[1], list(res[2:2 + n]), list(res[2 + n:2 + 2 * n]), res[-1]


def _chip_exchange_wait(sums, landings, send_sems, recv_sems, after, *, name):
    n = len(sums)

    def body(*refs):
        ins, lands = refs[:n], refs[n:2 * n]
        send, recv = refs[2 * n:2 * n + 2]
        x, y, c, _, chips = _place()
        for a in range(n):
            for j, (qx, qy) in enumerate(chips):
                q = 2 * qx + qy
                cp = pltpu.make_async_remote_copy(src_ref=ins[a].at[q], dst_ref=lands[a].at[q], send_sem=send.at[3 * a + j],
                                                  recv_sem=recv.at[3 * a + j], device_id=(x, y, c), device_id_type=_MESH_ID)
                cp.wait_recv()
                cp.wait_send()

    res = pl.pallas_call(
        body, name=name, in_specs=[_HBM] * (2 * n) + [_SEM] * 2 + [pl.BlockSpec(memory_space=pl.ANY)] * len(after),
        out_shape=(*[_hbm(b) for b in sums], *[_hbm(b) for b in landings]), out_specs=tuple([_HBM] * (2 * n)),
        input_output_aliases={i: i for i in range(2 * n)},
        compiler_params=pltpu.CompilerParams(has_side_effects=_DATAFLOW),
    )(*sums, *landings, send_sems, recv_sems, *after)
    return list(res[:n]), list(res[n:])


_C1 = 1.0 - ADAM_B1 ** ADAM_STEP
_C2 = 1.0 - ADAM_B2 ** ADAM_STEP


def _adamw_math(g, w, m, v):
    m = ADAM_B1 * m + (1.0 - ADAM_B1) * g
    v = ADAM_B2 * v + (1.0 - ADAM_B2) * (g * g)
    delta = -ADAM_LR * ((m / _C1) / (jnp.sqrt(v / _C2) + ADAM_EPS) + ADAM_WD * w)
    return delta, m, v


def _adamw_reduce(landed, sums, chip, w, m, v, layer, prev, *, name):
    _, R, C = w.shape
    tr = max(t for t in range(16, R + 1, 16) if R % t == 0 and t * C <= 256 * 1024)

    def body(chip_ref, p_ref, own_ref, w_ref, m_ref, v_ref, *rest):
        g_ref, d_ref, nm_ref, nv_ref = rest[-4:]
        own = own_ref[...].astype(_F32)
        g = jnp.where(chip_ref[0] == 0, own, p_ref[0].astype(_F32))
        for q in range(1, N_CHIP):
            g = g + jnp.where(chip_ref[0] == q, own, p_ref[q].astype(_F32))
        d, nm, nv = _adamw_math(g, w_ref[...], m_ref[...], v_ref[...])
        g_ref[...] = g
        d_ref[...] = d
        nm_ref[...] = nm
        nv_ref[...] = nv

    blk = pl.BlockSpec((None, tr, C), lambda i, chip: (layer, i, 0))
    shape = jax.ShapeDtypeStruct((DEPTH, R, C), _F32)
    kept = [] if prev is None else list(prev)
    grid_spec = pltpu.PrefetchScalarGridSpec(
        num_scalar_prefetch=1, grid=(R // tr,),
        in_specs=[pl.BlockSpec((N_CHIP, tr, C), lambda i, chip: (0, i, 0)),
                  pl.BlockSpec((None, tr, C), lambda i, chip: (chip[0], i, 0)), blk, blk, blk]
        + [pl.BlockSpec(memory_space=pl.ANY)] * len(kept),
        out_specs=[blk] * 4)
    return pl.pallas_call(
        body, grid_spec=grid_spec, out_shape=[shape] * 4, name=name,
        input_output_aliases={6 + k: k for k in range(len(kept))},
        compiler_params=_cparams(dimension_semantics=("parallel",)),
    )(chip.reshape(1), landed, sums, w, m, v, *kept)


_PACK_LANES = 128
_LAYER_ROWS = 248
_LB_ROWS = (A_HEADS * HEAD_DIM) // _PACK_LANES


def _small_reduce(parts, lb_logits, *, name):
    rows = DEPTH * _LAYER_ROWS

    def body(p_ref, lg_ref, o_ref):
        g = p_ref[0]
        for s in range(1, N_DEV):
            g = g + p_ref[s]
        o_ref[...] = g
        lg = lg_ref[...]
        e = jnp.exp(lg - jnp.max(lg, axis=0, keepdims=True))
        p = e / jnp.sum(e, axis=0, keepdims=True)
        d1 = g[_LAYER_ROWS:_LAYER_ROWS + _LB_ROWS, :] * p[0] * p[1]
        o_ref[0:_LB_ROWS, :] = -d1
        o_ref[_LAYER_ROWS:_LAYER_ROWS + _LB_ROWS, :] = d1

    return pl.pallas_call(
        body, out_shape=jax.ShapeDtypeStruct((rows, _PACK_LANES), _F32), name=name,
        compiler_params=_cparams(),
    )(parts, lb_logits.reshape(DEPTH, _LB_ROWS, _PACK_LANES))


def _adamw_small(g, w, m, v, *, name):
    def body(g_ref, w_ref, m_ref, v_ref, d_ref, nm_ref, nv_ref):
        d, nm, nv = _adamw_math(g_ref[...], w_ref[...], m_ref[...], v_ref[...])
        d_ref[...] = d
        nm_ref[...] = nm
        nv_ref[...] = nv

    shape = jax.ShapeDtypeStruct(g.shape, _F32)
    return pl.pallas_call(body, out_shape=[shape] * 3, name=name, compiler_params=_cparams())(g, w, m, v)


def _pack(vectors, rows):
    flat = jnp.concatenate([v.reshape(-1).astype(_F32) for v in vectors])
    return jnp.pad(flat, (0, rows * _PACK_LANES - flat.shape[0])).reshape(rows, _PACK_LANES)


def _unpack(packed, shapes):
    flat = packed.reshape(-1)
    out, at = [], 0
    for s in shapes:
        size = int(np.prod(s))
        out.append(flat[at:at + size].reshape(s))
        at += size
    return out


_BIG = ("w_in", "w_gate", "w_up", "w_out", "w_down")
_COLUMN_SHARDED = ("w_in", "w_gate", "w_up")


def _full_weight(name, g):
    if name == "w_out":
        return g.reshape(D_MODEL, D_MODEL)
    if name == "w_down":
        return g.reshape(D_FF, D_MODEL)
    if name == "conv_w":
        return g.transpose(1, 0, 2).reshape(g.shape[1], N_DEV * SHARD_COLS)
    return g


class _WeightGather:
    def __init__(self, names, first, blocks, lands, sems, tag):
        self.names, self.first, self.blocks, self.lands, self.sems, self.tag = names, first, blocks, lands, sems, tag
        self.forwarded = None

    def forward(self, after):
        f_send, f_recv, self.lands, token = _gather_forward(self.lands, self.sems[2], self.first, after,
                                                            name=f"gather_forward_{self.tag}")
        self.forwarded = (f_send, f_recv)
        return token

    def wait(self, after):
        if self.forwarded is None:
            self.forward(after)
        got = _gather_wait(self.blocks, self.lands, self.sems[0], self.sems[1], self.first, *self.forwarded, after,
                           name=f"gather_wait_{self.tag}")
        return {n: _full_weight(n, g) for n, g in zip(self.names, got)}


def _start_gathers(groups, me, name):
    blocks = [b for _, _, bs in groups for b in bs]
    landings = [lax.dynamic_update_index_in_dim(lax.empty((N_DEV,) + b.shape, b.dtype), b[None], me, 0) for b in blocks]
    send, d2d, ici, blocks, landings, token = _gather_start(blocks, landings, name=name)
    out, first = [], 0
    for tag, names, bs in groups:
        k = len(bs)
        out.append(_WeightGather(names, first, blocks[first:first + k], landings[first:first + k], (send, d2d, ici), tag))
        first += k
    return out, token


class _LayerWeights:
    def __init__(self, ready, pending=(), forwards=(), tokens=()):
        self.ready, self.pending, self.forwards, self._tokens = dict(ready), list(pending), list(forwards), list(tokens)

    def at(self, point, after):
        for when, gather in self.forwards:
            if when == point:
                self._tokens.append(gather.forward(after))

    def tokens(self):
        out, self._tokens = self._tokens, []
        return out

    def get(self, name, after):
        if name not in self.ready:
            group, = [g for g in self.pending if name in g.names]
            self.ready.update(group.wait(after))
        return self.ready[name]


def _layer_fwd(x, xb, ws, lb_logits, a_norm_w, c_sink, ln1_g, ln1_b, conv_b, ln2_g, ln2_b, tabs, l):
    proj = _mm_w_slabs(xb, ws.get("w_in", xb), tm=1024, after=ws.tokens(), name=f"proj_{l}")
    o_a, raw, states = _hgrn_fwd(proj, lb_logits, a_norm_w, l, name=f"hgrn_fwd_{l}")
    ws.at("hgrn", o_a)
    o_b, lse_b = _band_fwd(proj, tabs, name=f"dilated_fwd_{l}", **_DILATED)
    o_c, lse_c = _band_fwd(proj, tabs, sink=c_sink, name=f"swa_fwd_{l}", **_SWA)
    ws.at("swa", o_c)
    mixed = jnp.concatenate([o_a, o_b, o_c], axis=1).astype(_MXU_DTYPE)
    y = _mm(mixed, ws.get("w_out", mixed), **_TILE_MIX, after=ws.tokens(), name=f"mix_out_{l}")
    z1, x1, x1b = _ln_fwd(x, y, ln1_g, ln1_b, name=f"ln1_fwd_{l}")
    g = _mm_w_slabs(x1b, ws.get("w_gate", x1b), tm=1024, out_dtype=_ACT_DTYPE, name=f"ffn_gate_{l}")
    u = _mm_w_slabs(x1b, ws.get("w_up", x1b), tm=1024, out_dtype=_ACT_DTYPE, name=f"ffn_up_{l}")
    ws.at("up", u)
    hb = _conv_gate_fwd(g, u, ws.get("conv_w", u), conv_b, name=f"conv_gate_fwd_{l}")
    y2 = _mm(hb, ws.get("w_down", hb), **_TILE_DOWN, after=ws.tokens(), name=f"ffn_down_{l}")
    ws.at("down", y2)
    z2, x2, x2b = _ln_fwd(x1, y2, ln2_g, ln2_b, name=f"ln2_fwd_{l}")
    res = dict(xb=xb, proj=proj, raw=raw, states=states, o_b=o_b, lse_b=lse_b, o_c=o_c, lse_c=lse_c,
               mixed=mixed, z1=z1, x1b=x1b, g=g, u=u, hb=hb, z2=z2)
    return x2, x2b, res


class _GradExchange:
    def __init__(self, core, chip):
        self.core, self.chip, self.groups, self.swapping, self._tokens = core, chip, [], [], []

    def launch(self, names, slabs, l, tag, behind):
        mine = [s.reshape((N_CHIP, 2) + s.shape[1:]) for s in slabs]
        if behind:
            landings = [lax.empty((N_CHIP,) + m.shape[2:], m.dtype) for m in mine]
            send, recv, mine, landings, token = _swap_start(mine, landings, name=f"swap_start_{tag}")
            self.swapping.append((names, l, tag, send, recv, mine, landings))
            self._tokens.append(token)
        else:
            self._exchange(names, l, tag, mine, _sibling_swap(mine, name=f"swap_grads_{tag}"))

    def advance(self, after):
        for names, l, tag, send, recv, mine, landings in self.swapping:
            mine, theirs = _swap_wait(mine, landings, send, recv, after, name=f"swap_wait_{tag}")
            self._exchange(names, l, tag, mine, theirs)
        self.swapping = []

    def _exchange(self, names, l, tag, mine, theirs):
        sums = [_pair_add(a, b, self.core, name=f"pair_add_{n}_{l}") for n, a, b in zip(names, mine, theirs)]
        landings = [lax.empty(s.shape, s.dtype) for s in sums]
        send, recv, sums, landings, token = _chip_exchange_start(sums, landings, name=f"exchange_start_{tag}")
        self.groups.append((names, l, tag, send, recv, sums, landings))
        self._tokens.append(token)

    def tokens(self):
        out, self._tokens = self._tokens, []
        return out

    def finish(self, weights, mom1, mom2, after):
        out = {}
        after = list(after) + self.tokens()
        for names, l, tag, send, recv, sums, landings in self.groups:
            sums, landings = _chip_exchange_wait(sums, landings, send, recv, after, name=f"exchange_wait_{tag}")
            for n, s, landed in zip(names, sums, landings):
                out[n] = _adamw_reduce(landed, s, self.chip, weights[n], mom1[n], mom2[n], l, out.get(n), name=f"adamw_{n}_{l}")
            after = [out[n][0] for n in names]
        return out


def _layer_bwd(dx2, res, w, lb_logits, a_norm_w, c_sink, ln1_g, conv_b, ln2_g, tabs, exchange, l):
    dz2, dz2b, d_ln2_g, d_ln2_b = _ln_bwd(res["z2"], dx2, None, ln2_g, name=f"ln2_bwd_{l}")
    exchange.advance(dz2b)
    dh = _mm(dz2b, w["w_down"], tb=True, **_TILE_DOWN_DX, out_dtype=_ACT_DTYPE, after=exchange.tokens(),
             name=f"ffn_down_dx_{l}")
    d_w_down = _mm(res["hb"], dz2b, ta=True, **_TILE_DOWN_DW, out_dtype=_GRAD_DTYPE, name=f"ffn_down_dw_{l}")
    dg, du, d_conv_w, d_conv_b = _conv_gate_bwd(dh, res["g"], res["u"], w["conv_w"], conv_b, name=f"conv_gate_bwd_{l}")
    t = _mm_nt_w_slabs(dg, w["w_gate"], **_TILE_NT_SLABS, name=f"ffn_gate_dx_{l}")
    dx1 = _mm_nt_w_slabs(du, w["w_up"], **_TILE_NT_SLABS, add=t, name=f"ffn_up_dx_{l}")
    d_w_gate = _mm_tn_slabs(res["x1b"], dg, tm=1024, name=f"ffn_gate_dw_{l}")
    d_w_up = _mm_tn_slabs(res["x1b"], du, tm=1024, name=f"ffn_up_dw_{l}")
    dz1, dz1b, d_ln1_g, d_ln1_b = _ln_bwd(res["z1"], dx1, dz2, ln1_g, name=f"ln1_bwd_{l}")
    d_w_out = _mm(res["mixed"], dz1b, ta=True, **_TILE_MIX, out_dtype=_GRAD_DTYPE, name=f"mix_out_dw_{l}")
    exchange.launch(("w_down", "w_gate", "w_up", "w_out"),
                    [d_w_down.reshape(N_DEV, D_FF // N_DEV, D_MODEL), d_w_gate, d_w_up,
                     d_w_out.reshape(N_DEV, D_MODEL // N_DEV, D_MODEL)], l, f"ffn_{l}", True)
    dmixed = _mm(dz1b, w["w_out"], tb=True, **_TILE_MIX, after=exchange.tokens(), name=f"mix_out_dx_{l}")
    dq_a, df_a, di_a, dg_a, d_norm_w, d_lb = _hgrn_bwd(res["proj"], lb_logits, a_norm_w, res["raw"], res["states"],
                                                      dmixed, l, name=f"hgrn_bwd_{l}")
    exchange.advance(dq_a)
    dq_b, dk_b, dv_b = _band_bwd(res["proj"], tabs, dmixed, res["o_b"], res["lse_b"], do0=A_HEADS, after=exchange.tokens(),
                                 name=f"dilated_bwd_{l}", **_DILATED)
    dq_c, dk_c, dv_c, d_sink = _band_bwd(res["proj"], tabs, dmixed, res["o_c"], res["lse_c"], do0=A_HEADS + B_HEADS,
                                         sink=c_sink, name=f"swa_bwd_{l}", **_SWA)
    dproj = jnp.concatenate([dq_a, df_a, di_a, dg_a, dq_b, dk_b, dv_b, dq_c, dk_c, dv_c], axis=1)
    d_w_in = _mm_tn_slabs(res["xb"], dproj, tm=1024, name=f"proj_dw_{l}")
    exchange.launch(("w_in",), [d_w_in], l, f"mix_{l}", l > 0)
    dx = _mm_nt_w_slabs(dproj, w["w_in"], **_TILE_NT_SLABS, add=dz1, add_scale=ALPHA, after=exchange.tokens(),
                        name=f"proj_dx_{l}")
    small = [d_lb, d_norm_w, jnp.pad(d_sink, (0, _PACK_LANES - C_HEADS)), d_ln1_g, d_ln1_b, d_ln2_g, d_ln2_b, d_conv_b,
             d_conv_w]
    return dx, small


def kernel(x, w_in, lb_logits, a_norm_w, c_sinks, w_out, ln1_g, ln1_b, w_gate, w_up, conv_w, conv_b, w_down, ln2_g, ln2_b, loss_target, m_w_in, m_lb_logits, m_a_norm_w, m_c_sinks, m_w_out, m_ln1_g, m_ln1_b, m_w_gate, m_w_up, m_conv_w, m_conv_b, m_w_down, m_ln2_g, m_ln2_b, v_w_in, v_lb_logits, v_a_norm_w, v_c_sinks, v_w_out, v_ln1_g, v_ln1_b, v_w_gate, v_w_up, v_conv_w, v_conv_b, v_w_down, v_ln2_g, v_ln2_b):
    weights = dict(w_in=w_in, lb_logits=lb_logits, a_norm_w=a_norm_w, c_sinks=c_sinks, w_out=w_out, ln1_g=ln1_g, ln1_b=ln1_b,
                   w_gate=w_gate, w_up=w_up, conv_w=conv_w, conv_b=conv_b, w_down=w_down, ln2_g=ln2_g, ln2_b=ln2_b)
    mom1 = dict(w_in=m_w_in, lb_logits=m_lb_logits, a_norm_w=m_a_norm_w, c_sinks=m_c_sinks, w_out=m_w_out, ln1_g=m_ln1_g,
                ln1_b=m_ln1_b, w_gate=m_w_gate, w_up=m_w_up, conv_w=m_conv_w, conv_b=m_conv_b, w_down=m_w_down, ln2_g=m_ln2_g,
                ln2_b=m_ln2_b)
    mom2 = dict(w_in=v_w_in, lb_logits=v_lb_logits, a_norm_w=v_a_norm_w, c_sinks=v_c_sinks, w_out=v_w_out, ln1_g=v_ln1_g,
                ln1_b=v_ln1_b, w_gate=v_w_gate, w_up=v_w_up, conv_w=v_conv_w, conv_b=v_conv_b, w_down=v_w_down, ln2_g=v_ln2_g,
                ln2_b=v_ln2_b)
    core = lax.axis_index("c").astype(jnp.int32)
    me = 4 * lax.axis_index("x") + 2 * lax.axis_index("y") + core
    tabs = _rope_tables()

    chip = (2 * lax.axis_index("x") + lax.axis_index("y")).astype(jnp.int32)

    def as_slabs(d):
        return {n: jnp.swapaxes(d[n], 1, 2) if n in _COLUMN_SHARDED else d[n] for n in _BIG}

    w_views = as_slabs(weights)

    def block(n, l, after=()):
        return conv_w[l] if n == "conv_w" else _cast_layer(w_views[n], l, after=after, name=f"cast_{n}_{l}")

    (in0,), started_first = _start_gathers([("w_in_0", ("w_in",), [block("w_in", 0)])], me, "gather_start_first")
    order = [(("w_out",), 0), (("w_gate", "w_up", "conv_w"), 0), (("w_down",), 0),
             (("w_in",), 1), (("w_out",), 1), (("w_gate", "w_up", "conv_w"), 1), (("w_down",), 1)]
    gathers, started = _start_gathers([(f"{names[0]}_{l}", names, [block(n, l, [started_first]) for n in names])
                                       for names, l in order], me, "gather_start_rest")
    out0, ffn0, down0, in1, out1, ffn1, down1 = gathers
    layer_ws = [_LayerWeights(in0.wait(started), [out0, ffn0, down0],
                              [("hgrn", out0), ("swa", ffn0), ("up", down0), ("down", in1)]),
                _LayerWeights({}, [in1, out1, ffn1, down1], [("hgrn", out1), ("swa", ffn1), ("up", down1)])]

    xs = x[0]
    xb = xs.astype(_MXU_DTYPE)
    saved = []
    for l in range(DEPTH):
        xs, xb, res = _layer_fwd(xs, xb, layer_ws[l], lb_logits, a_norm_w[l], c_sinks[l], ln1_g[l], ln1_b[l], conv_b[l],
                                 ln2_g[l], ln2_b[l], tabs, l)
        saved.append(res)
    loss_part, dx = _loss_head(xs, loss_target[0], name="loss_head")
    loss = lax.psum(loss_part, ("x", "y", "c"))

    exchange = _GradExchange(core, chip)
    small_parts = [None] * DEPTH
    for l in reversed(range(DEPTH)):
        dx, small = _layer_bwd(dx, saved[l], layer_ws[l].ready, lb_logits, a_norm_w[l], c_sinks[l], ln1_g[l], conv_b[l],
                               ln2_g[l], tabs, exchange, l)
        small_parts[l] = _pack(small, _LAYER_ROWS)
    updated = exchange.finish(w_views, as_slabs(mom1), as_slabs(mom2), [dx])
    updated = {n: tuple(jnp.swapaxes(t, 1, 2) for t in u) if n in _COLUMN_SHARDED else u for n, u in updated.items()}
    gathered, = _exchange([jnp.concatenate(small_parts, axis=0)], False, name="gather_small_grads")
    g_small = _small_reduce(gathered, lb_logits, name="small_grads")

    per_layer = [(A_HEADS * HEAD_DIM,), (HEAD_DIM,), (_PACK_LANES,), (D_MODEL,), (D_MODEL,), (D_MODEL,), (D_MODEL,), (D_FF,),
                 (3, D_FF)]
    names = ("lb_logits", "a_norm_w", "c_sinks", "ln1_g", "ln1_b", "ln2_g", "ln2_b", "conv_b", "conv_w")
    grads = {n: [] for n in names}
    for l in range(DEPTH):
        for n, t in zip(names, _unpack(g_small[l * _LAYER_ROWS:(l + 1) * _LAYER_ROWS], per_layer)):
            grads[n].append(t)
    grads = {n: jnp.stack(t) for n, t in grads.items()}
    grads["c_sinks"] = grads["c_sinks"][:, :C_HEADS]
    grads["conv_w"] = lax.dynamic_slice_in_dim(grads["conv_w"], me * SHARD_COLS, SHARD_COLS, axis=2)
    shapes = [grads[n].shape for n in names]
    rows = -(-sum(int(np.prod(s)) for s in shapes) // (8 * _PACK_LANES)) * 8
    d_s, m_s, v_s = _adamw_small(_pack([grads[n] for n in names], rows), _pack([weights[n] for n in names], rows),
                                 _pack([mom1[n] for n in names], rows), _pack([mom2[n] for n in names], rows),
                                 name="adamw_small")
    delta = dict(zip(names, _unpack(d_s, shapes)))
    new_m = dict(zip(names, _unpack(m_s, shapes)))
    new_v = dict(zip(names, _unpack(v_s, shapes)))
    for n in _BIG:
        grads[n], delta[n], new_m[n], new_v[n] = updated[n]

    order = ("w_in", "lb_logits", "a_norm_w", "c_sinks", "w_out", "ln1_g", "ln1_b", "w_gate", "w_up", "conv_w", "conv_b",
             "w_down", "ln2_g", "ln2_b")
    return (loss, dx[None], *[grads[n] for n in order], *[delta[n] for n in order], *[new_m[n] for n in order],
            *[new_v[n] for n in order])
```

```python
import functools

import jax
import jax.numpy as jnp
import numpy as np
from jax import lax
from jax.experimental import pallas as pl
from jax.experimental.pallas import tpu as pltpu

D_MODEL = 2048
SEQ = 2048
DEPTH = 2
HEAD_DIM = 128
A_HEADS = 4
B_HEADS = 6
C_HEADS = 6
C_KV_HEADS = 2
A_CHUNK = 16
DILATIONS = (1, 4, 16)
BLOCK = 128
ROPE_THETA = 500000.0
ROPE_DIM = 32
D_FF = 5632
IN_WIDTH = 5632
LN_EPS = 1e-5
ALPHA = (2 * DEPTH) ** 0.25
N_DEV = 8
SHARD_COLS = IN_WIDTH // N_DEV

ADAM_LR = 0.001
ADAM_B1 = 0.9
ADAM_B2 = 0.999
ADAM_EPS = 1e-08
ADAM_WD = 0.01
ADAM_STEP = 10

A_COLS = 16
QKV_COLS = 28
QB0, KB0, VB0, QC0, KC0, VC0 = 0, 6, 12, 18, 24, 26

_MXU_DTYPE = jnp.bfloat16
_GRAD_DTYPE = jnp.bfloat16
_ACT_DTYPE = jnp.bfloat16
_NEG = -1e30
_VMEM_LIMIT = 56 * 2 ** 20

_F32 = jnp.float32


def _sigmoid(x):
    return 0.5 * jnp.tanh(0.5 * x) + 0.5


def _cparams(**kw):
    return pltpu.CompilerParams(vmem_limit_bytes=_VMEM_LIMIT, **kw)


_TILE_MIX = dict(tm=1024, tn=1024)
_TILE_DOWN = dict(tm=1024, tn=512)
_TILE_DOWN_DX = dict(tm=1024, tn=1408)
_TILE_NT_SLABS = dict(tm=1024, tn=512)


def _mm(a, b, *, ta=False, tb=False, tm, tn, out_dtype=_F32, add=None, add_scale=1.0, after=(), name):
    K = a.shape[0] if ta else a.shape[1]
    M = a.shape[1] if ta else a.shape[0]
    N = b.shape[0] if tb else b.shape[1]
    assert (b.shape[1] if tb else b.shape[0]) == K and M % tm == 0 and N % tn == 0
    dn = (((0 if ta else 1,), (1 if tb else 0,)), ((), ()))

    def body(*refs):
        a_ref, b_ref = refs[:2]
        o_ref = refs[-1]
        r = lax.dot_general(a_ref[...], b_ref[...], dn, preferred_element_type=_F32)
        if add is not None:
            r = r + add_scale * refs[2][...]
        o_ref[...] = r.astype(o_ref.dtype)

    a_spec = pl.BlockSpec((K, tm), lambda i, j: (0, i)) if ta else pl.BlockSpec((tm, K), lambda i, j: (i, 0))
    b_spec = pl.BlockSpec((tn, K), lambda i, j: (j, 0)) if tb else pl.BlockSpec((K, tn), lambda i, j: (0, j))
    o_spec = pl.BlockSpec((tm, tn), lambda i, j: (i, j))
    in_specs = [a_spec, b_spec] + ([o_spec] if add is not None else []) + [pl.BlockSpec(memory_space=pl.ANY)] * len(after)
    args = (a, b) + ((add,) if add is not None else ()) + tuple(after)
    return pl.pallas_call(
        body, grid=(M // tm, N // tn), in_specs=in_specs, out_specs=o_spec,
        out_shape=jax.ShapeDtypeStruct((M, N), out_dtype), name=name,
        compiler_params=_cparams(dimension_semantics=("parallel", "parallel")),
    )(*args)


_PAIR = 2 * SHARD_COLS
_TILE_HALF = 1024


def _mm_tn_half(slabbed, other, core_sel, *, add=None, name):
    K, N = other.shape
    assert slabbed.shape == (K, N_DEV * SHARD_COLS) and N % _TILE_HALF == 0

    def body(sel_ref, s_ref, o_ref, *rest):
        out_ref = rest[-1]
        for half in range(2):
            @pl.when(sel_ref[0] == half)
            def _(half=half):
                r = lax.dot_general(s_ref[:, half * SHARD_COLS:(half + 1) * SHARD_COLS], o_ref[...], _TN,
                                    preferred_element_type=_F32)
                if add is not None:
                    r = r + rest[0][...].astype(_F32)
                out_ref[...] = r.astype(out_ref.dtype)

    slab_spec = pl.BlockSpec((None, SHARD_COLS, _TILE_HALF), lambda q, j, sel: (q, 0, j))
    grid_spec = pltpu.PrefetchScalarGridSpec(
        num_scalar_prefetch=1, grid=(N_CHIP, N // _TILE_HALF),
        in_specs=[pl.BlockSpec((K, _PAIR), lambda q, j, sel: (0, q)), pl.BlockSpec((K, _TILE_HALF), lambda q, j, sel: (0, j))]
        + ([slab_spec] if add is not None else []),
        out_specs=slab_spec)
    return pl.pallas_call(
        body, grid_spec=grid_spec, out_shape=jax.ShapeDtypeStruct((N_CHIP, SHARD_COLS, N), _GRAD_DTYPE), name=name,
        compiler_params=_cparams(dimension_semantics=("parallel", "parallel")),
    )(core_sel, slabbed, other, *((add,) if add is not None else ()))


def _mm_tn_rows_half(a, b, core_sel, *, add=None, name):
    K, M = a.shape
    N = b.shape[1]
    R = M // N_DEV
    assert R % 128 == 0 and N % _TILE_HALF == 0

    def body(sel_ref, a_ref, b_ref, *rest):
        del sel_ref
        r = lax.dot_general(a_ref[...], b_ref[...], _TN, preferred_element_type=_F32)
        if add is not None:
            r = r + rest[0][...].astype(_F32)
        rest[-1][...] = r.astype(rest[-1].dtype)

    blk = pl.BlockSpec((None, R, _TILE_HALF), lambda q, j, sel: (q, 0, j))
    grid_spec = pltpu.PrefetchScalarGridSpec(
        num_scalar_prefetch=1, grid=(N_CHIP, N // _TILE_HALF),
        in_specs=[pl.BlockSpec((K, R), lambda q, j, sel: (0, 2 * q + sel[0])), pl.BlockSpec((K, _TILE_HALF), lambda q, j, sel: (0, j))]
        + ([blk] if add is not None else []),
        out_specs=blk)
    return pl.pallas_call(
        body, grid_spec=grid_spec, out_shape=jax.ShapeDtypeStruct((N_CHIP, R, N), _GRAD_DTYPE), name=name,
        compiler_params=_cparams(dimension_semantics=("parallel", "parallel")),
    )(core_sel, a, b, *((add,) if add is not None else ()))


def _cast_layer(w, layer, *, after=(), name):
    _, R, C = w.shape
    tr = max(t for t in range(16, R + 1, 16) if R % t == 0 and t * C <= 512 * 1024)

    def body(w_ref, *rest):
        o_ref = rest[-1]
        o_ref[...] = w_ref[...].astype(o_ref.dtype)

    return pl.pallas_call(
        body, grid=(R // tr,),
        in_specs=[pl.BlockSpec((None, tr, C), lambda i: (layer, i, 0))] + [pl.BlockSpec(memory_space=pl.ANY)] * len(after),
        out_specs=pl.BlockSpec((tr, C), lambda i: (i, 0)), out_shape=jax.ShapeDtypeStruct((R, C), _MXU_DTYPE), name=name,
        compiler_params=_cparams(dimension_semantics=("parallel",)),
    )(w, *after)


def _mm_w_slabs(a, w, *, tm, out_dtype=_F32, after=(), name):
    M, K = a.shape
    assert w.shape == (N_DEV, SHARD_COLS, K) and M % tm == 0

    def body(a_ref, w_ref, *rest):
        o_ref = rest[-1]
        a_blk = a_ref[...]
        for s in range(2):
            o_ref[:, s * SHARD_COLS:(s + 1) * SHARD_COLS] = lax.dot_general(
                a_blk, w_ref[s], _NT, preferred_element_type=_F32).astype(o_ref.dtype)

    return pl.pallas_call(
        body, grid=(M // tm, N_DEV // 2),
        in_specs=[pl.BlockSpec((tm, K), lambda i, p: (i, 0)), pl.BlockSpec((2, SHARD_COLS, K), lambda i, p: (p, 0, 0))]
        + [pl.BlockSpec(memory_space=pl.ANY)] * len(after),
        out_specs=pl.BlockSpec((tm, _PAIR), lambda i, p: (i, p)),
        out_shape=jax.ShapeDtypeStruct((M, N_DEV * SHARD_COLS), out_dtype), name=name,
        compiler_params=_cparams(dimension_semantics=("parallel", "parallel")),
    )(a, w, *after)


def _mm_nt_w_slabs(a, w, *, tm, tn, add=None, add_scale=1.0, after=(), name):
    M = a.shape[0]
    N = w.shape[2]
    assert a.shape[1] == N_DEV * SHARD_COLS and w.shape[:2] == (N_DEV, SHARD_COLS) and M % tm == 0 and N % tn == 0

    def body(a_ref, w_ref, *rest):
        o_ref = rest[-1]
        acc = add_scale * rest[0][...] if add is not None else None
        for j in range(N_DEV):
            t = jnp.dot(a_ref[:, j * SHARD_COLS:(j + 1) * SHARD_COLS], w_ref[j], preferred_element_type=_F32)
            acc = t if acc is None else acc + t
        o_ref[...] = acc

    o_spec = pl.BlockSpec((tm, tn), lambda i, j: (i, j))
    return pl.pallas_call(
        body, grid=(M // tm, N // tn),
        in_specs=[pl.BlockSpec((tm, N_DEV * SHARD_COLS), lambda i, j: (i, 0)),
                  pl.BlockSpec((N_DEV, SHARD_COLS, tn), lambda i, j: (0, 0, j))]
        + ([o_spec] if add is not None else []) + [pl.BlockSpec(memory_space=pl.ANY)] * len(after),
        out_specs=o_spec, out_shape=jax.ShapeDtypeStruct((M, N), _F32), name=name,
        compiler_params=_cparams(dimension_semantics=("parallel", "parallel")),
    )(a, w, *((add,) if add is not None else ()), *after)


def _ln_fwd(x, y, g, b, *, name):
    tm = 256

    def body(x_ref, y_ref, g_ref, b_ref, z_ref, o_ref, ob_ref):
        z = ALPHA * x_ref[...] + y_ref[...]
        mu = jnp.mean(z, axis=-1, keepdims=True)
        zc = z - mu
        var = jnp.mean(zc * zc, axis=-1, keepdims=True)
        o = zc * lax.rsqrt(var + LN_EPS) * g_ref[...] + b_ref[...]
        z_ref[...] = z
        o_ref[...] = o
        ob_ref[...] = o.astype(ob_ref.dtype)

    row = pl.BlockSpec((tm, D_MODEL), lambda i: (i, 0))
    vec = pl.BlockSpec((1, D_MODEL), lambda i: (0, 0))
    return pl.pallas_call(
        body, grid=(SEQ // tm,), in_specs=[row, row, vec, vec], out_specs=[row, row, row],
        out_shape=[jax.ShapeDtypeStruct((SEQ, D_MODEL), _F32), jax.ShapeDtypeStruct((SEQ, D_MODEL), _F32),
                   jax.ShapeDtypeStruct((SEQ, D_MODEL), _MXU_DTYPE)],
        name=name, compiler_params=_cparams(dimension_semantics=("parallel",)),
    )(x, y, g.reshape(1, D_MODEL), b.reshape(1, D_MODEL))


def _ln_bwd(z, d_a, d_res, g, *, name):
    tm = 256

    def body(*refs):
        if d_res is None:
            z_ref, da_ref, g_ref, dz_ref, dzb_ref, dg_ref, db_ref = refs
        else:
            z_ref, da_ref, dr_ref, g_ref, dz_ref, dzb_ref, dg_ref, db_ref = refs

        @pl.when(pl.program_id(0) == 0)
        def _():
            dg_ref[...] = jnp.zeros_like(dg_ref)
            db_ref[...] = jnp.zeros_like(db_ref)

        dout = da_ref[...]
        if d_res is not None:
            dout = dout + ALPHA * dr_ref[...]
        z = z_ref[...]
        mu = jnp.mean(z, axis=-1, keepdims=True)
        zc = z - mu
        var = jnp.mean(zc * zc, axis=-1, keepdims=True)
        rstd = lax.rsqrt(var + LN_EPS)
        xh = zc * rstd
        dxh = dout * g_ref[...]
        m1 = jnp.mean(dxh, axis=-1, keepdims=True)
        m2 = jnp.mean(dxh * xh, axis=-1, keepdims=True)
        dz = rstd * (dxh - m1 - xh * m2)
        dz_ref[...] = dz
        dzb_ref[...] = dz.astype(dzb_ref.dtype)
        dg_ref[0:1, :] += jnp.sum(dout * xh, axis=0, keepdims=True)
        db_ref[0:1, :] += jnp.sum(dout, axis=0, keepdims=True)

    row = pl.BlockSpec((tm, D_MODEL), lambda i: (i, 0))
    vec = pl.BlockSpec((1, D_MODEL), lambda i: (0, 0))
    acc = pl.BlockSpec((8, D_MODEL), lambda i: (0, 0))
    ins = [z, d_a] + ([d_res] if d_res is not None else []) + [g.reshape(1, D_MODEL)]
    in_specs = [row, row] + ([row] if d_res is not None else []) + [vec]
    dz, dzb, dg, db = pl.pallas_call(
        body, grid=(SEQ // tm,), in_specs=in_specs, out_specs=[row, row, acc, acc],
        out_shape=[jax.ShapeDtypeStruct((SEQ, D_MODEL), _F32), jax.ShapeDtypeStruct((SEQ, D_MODEL), _MXU_DTYPE),
                   jax.ShapeDtypeStruct((8, D_MODEL), _F32), jax.ShapeDtypeStruct((8, D_MODEL), _F32)],
        name=name, compiler_params=_cparams(dimension_semantics=("arbitrary",)),
    )(*ins)
    return dz, dzb, dg[0], db[0]


def _loss_head(y, target, *, name):
    tm = 256

    def body(y_ref, t_ref, d_ref, l_ref):
        e = y_ref[...] - t_ref[...]
        d_ref[...] = e * (1.0 / D_MODEL)

        @pl.when(pl.program_id(0) == 0)
        def _():
            l_ref[...] = jnp.zeros_like(l_ref)

        l_ref[...] += (0.5 / D_MODEL) * jnp.sum(e * e)

    row = pl.BlockSpec((tm, D_MODEL), lambda i: (i, 0))
    d, l = pl.pallas_call(
        body, grid=(SEQ // tm,), in_specs=[row, row], out_specs=[row, pl.BlockSpec((8, 128), lambda i: (0, 0))],
        out_shape=[jax.ShapeDtypeStruct((SEQ, D_MODEL), _F32), jax.ShapeDtypeStruct((8, 128), _F32)],
        name=name, compiler_params=_cparams(dimension_semantics=("arbitrary",)),
    )(y, target)
    return l[0, 0], d


_CONV_TN = 256


def _shift_down(v, k, rows):
    return jnp.where(rows >= k, pltpu.roll(v, k, axis=0), 0.0)


def _shift_up(v, k, rows):
    return jnp.where(rows < SEQ - k, pltpu.roll(v, SEQ - k, axis=0), 0.0)


def _conv_gate_fwd(g, u, conv_w, conv_b, *, name):
    def body(g_ref, u_ref, w_ref, b_ref, h_ref):
        gv = g_ref[...].astype(_F32)
        rows = lax.broadcasted_iota(jnp.int32, gv.shape, 0)
        w = w_ref[...]
        gc = b_ref[...] + w[2:3, :] * gv + w[1:2, :] * _shift_down(gv, 1, rows) + w[0:1, :] * _shift_down(gv, 2, rows)
        h_ref[...] = (gc * _sigmoid(gc) * u_ref[...].astype(_F32)).astype(h_ref.dtype)

    col = pl.BlockSpec((SEQ, _CONV_TN), lambda j: (0, j))
    return pl.pallas_call(
        body, grid=(D_FF // _CONV_TN,),
        in_specs=[col, col, pl.BlockSpec((3, _CONV_TN), lambda j: (0, j)), pl.BlockSpec((1, _CONV_TN), lambda j: (0, j))],
        out_specs=col, out_shape=jax.ShapeDtypeStruct((SEQ, D_FF), _MXU_DTYPE), name=name,
        compiler_params=_cparams(dimension_semantics=("parallel",)),
    )(g, u, conv_w, conv_b.reshape(1, D_FF))


def _conv_gate_bwd(dh, g, u, conv_w, conv_b, *, name):
    def body(dh_ref, g_ref, u_ref, w_ref, b_ref, dg_ref, du_ref, dw_ref, db_ref):
        gv = g_ref[...].astype(_F32)
        rows = lax.broadcasted_iota(jnp.int32, gv.shape, 0)
        w = w_ref[...]
        g1 = _shift_down(gv, 1, rows)
        g2 = _shift_down(gv, 2, rows)
        gc = b_ref[...] + w[2:3, :] * gv + w[1:2, :] * g1 + w[0:1, :] * g2
        sg = _sigmoid(gc)
        dh = dh_ref[...].astype(_F32)
        du_ref[...] = (dh * (gc * sg)).astype(du_ref.dtype)
        dgc = dh * u_ref[...].astype(_F32) * (sg * (1.0 + gc * (1.0 - sg)))
        dg = w[2:3, :] * dgc + w[1:2, :] * _shift_up(dgc, 1, rows) + w[0:1, :] * _shift_up(dgc, 2, rows)
        dg_ref[...] = dg.astype(dg_ref.dtype)
        dw_ref[0:1, :] = jnp.sum(dgc * g2, axis=0, keepdims=True)
        dw_ref[1:2, :] = jnp.sum(dgc * g1, axis=0, keepdims=True)
        dw_ref[2:3, :] = jnp.sum(dgc * gv, axis=0, keepdims=True)
        db_ref[...] = jnp.sum(dgc, axis=0, keepdims=True)

    col = pl.BlockSpec((SEQ, _CONV_TN), lambda j: (0, j))
    w3 = pl.BlockSpec((3, _CONV_TN), lambda j: (0, j))
    w1 = pl.BlockSpec((1, _CONV_TN), lambda j: (0, j))
    dg, du, dw, db = pl.pallas_call(
        body, grid=(D_FF // _CONV_TN,), in_specs=[col, col, col, w3, w1], out_specs=[col, col, w3, w1],
        out_shape=[jax.ShapeDtypeStruct((SEQ, D_FF), _MXU_DTYPE), jax.ShapeDtypeStruct((SEQ, D_FF), _MXU_DTYPE),
                   jax.ShapeDtypeStruct((3, D_FF), _F32), jax.ShapeDtypeStruct((1, D_FF), _F32)],
        name=name, compiler_params=_cparams(dimension_semantics=("parallel",)),
    )(dh, g, u, conv_w, conv_b.reshape(1, D_FF))
    return dg, du, dw, db[0]


def _rope_tables():
    half = ROPE_DIM // 2
    inv = ROPE_THETA ** (-jnp.arange(0, ROPE_DIM, 2, dtype=_F32) / ROPE_DIM)
    ang = jnp.arange(SEQ, dtype=_F32)[:, None] * inv[None, :]
    cos, sin = jnp.cos(ang), jnp.sin(ang)
    rest = HEAD_DIM - ROPE_DIM
    c = jnp.concatenate([cos, cos, jnp.ones((SEQ, rest), _F32)], axis=1)
    s1 = jnp.concatenate([-sin, jnp.zeros((SEQ, HEAD_DIM - half), _F32)], axis=1)
    s2 = jnp.concatenate([jnp.zeros((SEQ, half), _F32), sin, jnp.zeros((SEQ, rest), _F32)], axis=1)
    return c, s1, s2


def _rope_apply(x, c, s1, s2):
    return x * c + pltpu.roll(x, HEAD_DIM - ROPE_DIM // 2, axis=1) * s1 + pltpu.roll(x, ROPE_DIM // 2, axis=1) * s2


def _rope_transpose(d, c, s1, s2):
    half = ROPE_DIM // 2
    return d * c + pltpu.roll(d * s1, half, axis=1) + pltpu.roll(d * s2, HEAD_DIM - half, axis=1)


_NT = (((1,), (1,)), ((), ()))
_TN = (((0,), (0,)), ((), ()))
_SCALE = HEAD_DIM ** -0.5


def _band_scores(q, k2, n, lag_off):
    s = lax.dot_general(q, k2, _NT, preferred_element_type=_F32) * _SCALE
    row = lax.broadcasted_iota(jnp.int32, (BLOCK, 2 * BLOCK), 0)
    col = lax.broadcasted_iota(jnp.int32, (BLOCK, 2 * BLOCK), 1)
    front = (col >= row + lag_off) & (col < BLOCK) & (n > 0)
    own = (col >= BLOCK) & (col <= row + BLOCK)
    return jnp.where(front | own, s, _NEG)


_BAND_STEPS = SEQ // BLOCK


def _rows(start, d):
    if d == 1:
        return pl.ds(pl.multiple_of(start, BLOCK), BLOCK)
    return pl.ds(start, BLOCK, stride=d)


def _band_block(it, d):
    r, n = it % d, it // d
    span = BLOCK * d
    return n, _rows(r + n * span, d), _rows(r + jnp.maximum(n - 1, 0) * span, d)


def _band_fwd(proj, tabs, *, kv_heads, q_per_kv, q0, k0, v0, dilations, lag_off, sink, name):
    heads = kv_heads * q_per_kv

    def body(*refs):
        q_refs = refs[:q_per_kv]
        k_ref, v_ref, c_ref, s1_ref, s2_ref = refs[q_per_kv:q_per_kv + 5]
        rest = refs[q_per_kv + 5:]
        if sink is not None:
            sk_ref, rest = rest[0], rest[1:]
        o_ref, lse_ref, qs, ks, m_s, l_s, acc_s = rest
        c, s1, s2 = c_ref[...], s1_ref[...], s2_ref[...]
        ks[...] = _rope_apply(k_ref[...], c, s1, s2)
        for i in range(q_per_kv):
            qs[...] = _rope_apply(q_refs[i][...], c, s1, s2)
            for pi, d in enumerate(dilations):
                def step(it, carry, d=d, first=(pi == 0)):
                    n, cur, prev = _band_block(it, d)
                    q = qs[cur, :].astype(_MXU_DTYPE)
                    k2 = jnp.concatenate([ks[prev, :], ks[cur, :]], axis=0).astype(_MXU_DTYPE)
                    v2 = jnp.concatenate([v_ref[prev, :], v_ref[cur, :]], axis=0).astype(_MXU_DTYPE)
                    s = _band_scores(q, k2, n, lag_off)
                    m_b = jnp.max(s, axis=1, keepdims=True)
                    m_new = m_b if first else jnp.maximum(m_b, m_s[cur, :][:, 0:1])
                    p = jnp.exp(s - m_new)
                    l_new = jnp.sum(p, axis=1, keepdims=True)
                    acc = jnp.dot(p.astype(_MXU_DTYPE), v2, preferred_element_type=_F32)
                    if not first:
                        a = jnp.exp(m_s[cur, :][:, 0:1] - m_new)
                        l_new = l_new + a * l_s[cur, :][:, 0:1]
                        acc = acc + a * acc_s[cur, :]
                    m_s[cur, :] = jnp.broadcast_to(m_new, (BLOCK, HEAD_DIM))
                    l_s[cur, :] = jnp.broadcast_to(l_new, (BLOCK, HEAD_DIM))
                    acc_s[cur, :] = acc
                    return carry

                lax.fori_loop(0, _BAND_STEPS, step, 0, unroll=4)
            m, den = m_s[...], l_s[...]
            if sink is not None:
                sk = sk_ref[i]
                m_f = jnp.maximum(m, sk)
                a = jnp.exp(m - m_f)
                den = den * a + jnp.exp(sk - m_f)
                o = acc_s[...] * a / den
                m = m_f
            else:
                o = acc_s[...] / den
            o_ref[:, i * HEAD_DIM:(i + 1) * HEAD_DIM] = o
            lse_ref[:, i * HEAD_DIM:(i + 1) * HEAD_DIM] = m + jnp.log(den)

    col = (SEQ, HEAD_DIM)
    in_specs = [pl.BlockSpec(col, functools.partial(lambda g, i: (0, A_COLS + q0 + g * q_per_kv + i), i=i)) for i in range(q_per_kv)]
    in_specs += [pl.BlockSpec(col, lambda g: (0, A_COLS + k0 + g)), pl.BlockSpec(col, lambda g: (0, A_COLS + v0 + g))]
    in_specs += [pl.BlockSpec(col, lambda g: (0, 0))] * 3
    args = [proj] * (q_per_kv + 2) + list(tabs)
    if sink is not None:
        in_specs.append(pl.BlockSpec((q_per_kv, 1, HEAD_DIM), lambda g: (g, 0, 0)))
        args.append(jnp.broadcast_to(sink.reshape(heads, 1, 1), (heads, 1, HEAD_DIM)))
    o_spec = pl.BlockSpec((SEQ, q_per_kv * HEAD_DIM), lambda g: (0, g))
    shape = jax.ShapeDtypeStruct((SEQ, heads * HEAD_DIM), _F32)
    return pl.pallas_call(
        body, grid=(kv_heads,), in_specs=in_specs, out_specs=[o_spec, o_spec], out_shape=[shape, shape],
        scratch_shapes=[pltpu.VMEM(col, _F32)] * 5, name=name,
        compiler_params=_cparams(dimension_semantics=("parallel",)),
    )(*args)


def _band_bwd(proj, tabs, dmixed, o, lse, *, kv_heads, q_per_kv, q0, k0, v0, do0, dilations, lag_off, sink, after=(), name):
    heads = kv_heads * q_per_kv

    def body(*refs):
        q_refs = refs[:q_per_kv]
        k_ref, v_ref, c_ref, s1_ref, s2_ref = refs[q_per_kv:q_per_kv + 5]
        do_refs = refs[q_per_kv + 5:2 * q_per_kv + 5]
        o_ref, lse_ref = refs[2 * q_per_kv + 5:2 * q_per_kv + 7]
        rest = refs[2 * q_per_kv + 7:]
        if sink is not None:
            sk_ref, rest = rest[0], rest[1:]
            dq_ref, dk_ref, dv_ref, dsk_ref, qs, ks, dq_s, dk_s, dv_s = rest[len(after):]
        else:
            dq_ref, dk_ref, dv_ref, qs, ks, dq_s, dk_s, dv_s = rest[len(after):]
        c, s1, s2 = c_ref[...], s1_ref[...], s2_ref[...]
        ks[...] = _rope_apply(k_ref[...], c, s1, s2)
        dk_s[...] = jnp.zeros_like(dk_s)
        dv_s[...] = jnp.zeros_like(dv_s)
        for i in range(q_per_kv):
            hs = slice(i * HEAD_DIM, (i + 1) * HEAD_DIM)
            qs[...] = _rope_apply(q_refs[i][...], c, s1, s2)
            dq_s[...] = jnp.zeros_like(dq_s)
            do_ref = do_refs[i]
            for d in dilations:
                def step(it, carry, d=d, do_ref=do_ref, hs=hs):
                    n, cur, prev = _band_block(it, d)
                    q = qs[cur, :].astype(_MXU_DTYPE)
                    k2 = jnp.concatenate([ks[prev, :], ks[cur, :]], axis=0).astype(_MXU_DTYPE)
                    v2 = jnp.concatenate([v_ref[prev, :], v_ref[cur, :]], axis=0).astype(_MXU_DTYPE)
                    do = do_ref[cur, :]
                    delta = jnp.sum(do * o_ref[cur, hs], axis=1, keepdims=True)
                    lse_c = lse_ref[cur, hs][:, 0:1]
                    p = jnp.exp(_band_scores(q, k2, n, lag_off) - lse_c)
                    dob = do.astype(_MXU_DTYPE)
                    ds = (p * (lax.dot_general(dob, v2, _NT, preferred_element_type=_F32) - delta) * _SCALE).astype(_MXU_DTYPE)
                    dq_s[cur, :] += jnp.dot(ds, k2, preferred_element_type=_F32)
                    dk2 = lax.dot_general(ds, q, _TN, preferred_element_type=_F32)
                    dv2 = lax.dot_general(p.astype(_MXU_DTYPE), dob, _TN, preferred_element_type=_F32)
                    dk_s[prev, :] += dk2[:BLOCK]
                    dv_s[prev, :] += dv2[:BLOCK]
                    dk_s[cur, :] += dk2[BLOCK:]
                    dv_s[cur, :] += dv2[BLOCK:]
                    return carry

                lax.fori_loop(0, _BAND_STEPS, step, 0, unroll=4)
            dq_ref[:, hs] = _rope_transpose(dq_s[...], c, s1, s2).astype(dq_ref.dtype)
            if sink is not None:
                delta = jnp.sum(do_ref[...] * o_ref[:, hs], axis=1, keepdims=True)
                w_sink = jnp.exp(sk_ref[i] - lse_ref[:, hs])
                dsk_ref[i] = jnp.broadcast_to(jnp.sum(-delta * w_sink[:, 0:1]), (8, HEAD_DIM))
        dk_ref[...] = _rope_transpose(dk_s[...], c, s1, s2).astype(dk_ref.dtype)
        dv_ref[...] = dv_s[...].astype(dv_ref.dtype)

    col = (SEQ, HEAD_DIM)
    in_specs = [pl.BlockSpec(col, functools.partial(lambda g, i: (0, A_COLS + q0 + g * q_per_kv + i), i=i)) for i in range(q_per_kv)]
    in_specs += [pl.BlockSpec(col, lambda g: (0, A_COLS + k0 + g)), pl.BlockSpec(col, lambda g: (0, A_COLS + v0 + g))]
    in_specs += [pl.BlockSpec(col, lambda g: (0, 0))] * 3
    in_specs += [pl.BlockSpec(col, functools.partial(lambda g, i: (0, do0 + g * q_per_kv + i), i=i)) for i in range(q_per_kv)]
    wide = pl.BlockSpec((SEQ, q_per_kv * HEAD_DIM), lambda g: (0, g))
    in_specs += [wide, wide]
    args = [proj] * (q_per_kv + 2) + list(tabs) + [dmixed] * q_per_kv + [o, lse]
    out_specs = [wide, pl.BlockSpec(col, lambda g: (0, g)), pl.BlockSpec(col, lambda g: (0, g))]
    out_shape = [jax.ShapeDtypeStruct((SEQ, heads * HEAD_DIM), _MXU_DTYPE), jax.ShapeDtypeStruct((SEQ, kv_heads * HEAD_DIM), _MXU_DTYPE),
                 jax.ShapeDtypeStruct((SEQ, kv_heads * HEAD_DIM), _MXU_DTYPE)]
    if sink is not None:
        in_specs.append(pl.BlockSpec((q_per_kv, 1, HEAD_DIM), lambda g: (g, 0, 0)))
        args.append(jnp.broadcast_to(sink.reshape(heads, 1, 1), (heads, 1, HEAD_DIM)))
        out_specs.append(pl.BlockSpec((q_per_kv, 8, HEAD_DIM), lambda g: (g, 0, 0)))
        out_shape.append(jax.ShapeDtypeStruct((heads, 8, HEAD_DIM), _F32))
    in_specs += [pl.BlockSpec(memory_space=pl.ANY)] * len(after)
    args += list(after)
    res = pl.pallas_call(
        body, grid=(kv_heads,), in_specs=in_specs, out_specs=out_specs, out_shape=out_shape,
        scratch_shapes=[pltpu.VMEM(col, _F32)] * 5, name=name,
        compiler_params=_cparams(dimension_semantics=("parallel",)),
    )(*args)
    if sink is not None:
        return res[0], res[1], res[2], res[3][:, 0, 0]
    return res


_DILATED = dict(kv_heads=B_HEADS, q_per_kv=1, q0=QB0, k0=KB0, v0=VB0, dilations=DILATIONS, lag_off=0, sink=None)
_SWA = dict(kv_heads=C_KV_HEADS, q_per_kv=C_HEADS // C_KV_HEADS, q0=QC0, k0=KC0, v0=VC0, dilations=(1,), lag_off=1)


_HG_TILE = 128
_HG_CHUNKS = _HG_TILE // A_CHUNK
_HG_TILES = SEQ // _HG_TILE
_HI = lax.Precision.HIGHEST


def _chunk_tri():
    i = np.arange(_HG_TILE)
    return jnp.asarray(((i[:, None] // A_CHUNK == i[None, :] // A_CHUNK) & (i[None, :] <= i[:, None])).astype(np.float32))


def _layer_lb(lb_ref, layer):
    if layer == 0:
        return jnp.zeros((1, HEAD_DIM), _F32)
    lg = lb_ref[...]
    m = jnp.max(lg, axis=0, keepdims=True)
    e = jnp.exp(lg - m)
    return e[1:2, :] / jnp.sum(e, axis=0, keepdims=True)


def _hgrn_gates(q, fr, lb):
    sgq = _sigmoid(q)
    sg = _sigmoid(fr)
    f = lb + (1.0 - lb) * sg
    return sgq, q * sgq, sg, f, 1.0 - f


def _hgrn_fwd(proj, lb_logits, norm_w, layer, *, name):
    tri = _chunk_tri()

    def body(q_ref, f_ref, i_ref, g_ref, lb_ref, nw_ref, tri_ref, o_ref, raw_ref, st_ref, state):
        @pl.when(pl.program_id(1) == 0)
        def _():
            state[...] = jnp.zeros_like(state)

        lb = _layer_lb(lb_ref, layer)
        _, qs, _, f, k = _hgrn_gates(q_ref[...], f_ref[...], lb)
        v = i_ref[...]
        b = jnp.dot(tri_ref[...], jnp.log(f), precision=_HI, preferred_element_type=_F32)
        eb = jnp.exp(b)
        ridx = lax.broadcasted_iota(jnp.int32, (A_CHUNK, HEAD_DIM), 0)
        outs = []
        for c in range(_HG_CHUNKS):
            sl = slice(c * A_CHUNK, (c + 1) * A_CHUNK)
            bc, qc, kc, vc = b[sl], qs[sl], k[sl], v[sl]
            bl = bc[A_CHUNK - 1:A_CHUNK]
            st = state[...]
            st_ref[0, c] = st
            o_c = lax.dot_general((qc * eb[sl]).astype(_MXU_DTYPE), st.astype(_MXU_DTYPE), _NT, preferred_element_type=_F32)
            rows = []
            for i in range(A_CHUNK):
                di = jnp.exp(jnp.where(ridx <= i, bc[i:i + 1] - bc, _NEG))
                a = jnp.sum(qc[i:i + 1] * kc * di, axis=1, keepdims=True)
                rows.append(jnp.sum(a * vc, axis=0, keepdims=True))
            outs.append(o_c + jnp.concatenate(rows, axis=0))
            kt = (kc * jnp.exp(bl - bc)).astype(_MXU_DTYPE)
            state[...] = st * jnp.exp(bl) + lax.dot_general(vc.astype(_MXU_DTYPE), kt, _TN, preferred_element_type=_F32)
        o = jnp.concatenate(outs, axis=0)
        raw_ref[...] = o
        r = lax.rsqrt(jnp.mean(o * o, axis=-1, keepdims=True) + LN_EPS)
        g = g_ref[...]
        o_ref[...] = o * r * nw_ref[...] * (g * _sigmoid(g))

    blk = (_HG_TILE, HEAD_DIM)

    def col(base):
        return pl.BlockSpec(blk, lambda h, t: (t, base + h))

    o_spec = pl.BlockSpec(blk, lambda h, t: (t, h))
    o_shape = jax.ShapeDtypeStruct((SEQ, A_HEADS * HEAD_DIM), _F32)
    return pl.pallas_call(
        body, grid=(A_HEADS, _HG_TILES),
        in_specs=[col(0), col(4), col(8), col(12), pl.BlockSpec((DEPTH, HEAD_DIM), lambda h, t: (0, h)),
                  pl.BlockSpec((1, HEAD_DIM), lambda h, t: (0, 0)), pl.BlockSpec(blk, lambda h, t: (0, 0))],
        out_specs=[o_spec, o_spec, pl.BlockSpec((1, _HG_CHUNKS, HEAD_DIM, HEAD_DIM), lambda h, t: (h, t, 0, 0))],
        out_shape=[o_shape, o_shape, jax.ShapeDtypeStruct((A_HEADS, SEQ // A_CHUNK, HEAD_DIM, HEAD_DIM), _F32)],
        scratch_shapes=[pltpu.VMEM((HEAD_DIM, HEAD_DIM), _F32)], name=name,
        compiler_params=_cparams(dimension_semantics=("parallel", "arbitrary")),
    )(proj, proj, proj, proj, lb_logits, norm_w.reshape(1, HEAD_DIM), tri)


def _hgrn_bwd(proj, lb_logits, norm_w, raw, states, dmixed, layer, *, name):
    tri = _chunk_tri()
    triu = tri.T

    def body(q_ref, f_ref, i_ref, g_ref, lb_ref, nw_ref, tri_ref, triu_ref, raw_ref, do_ref, st_ref,
             dq_ref, df_ref, di_ref, dg_ref, dnw_ref, dlb_ref, dstate):
        @pl.when(pl.program_id(1) == 0)
        def _():
            dstate[...] = jnp.zeros_like(dstate)
            dlb_ref[...] = jnp.zeros_like(dlb_ref)

        @pl.when((pl.program_id(0) == 0) & (pl.program_id(1) == 0))
        def _():
            dnw_ref[...] = jnp.zeros_like(dnw_ref)

        lb = _layer_lb(lb_ref, layer)
        q = q_ref[...]
        sgq, qs, sg, f, k = _hgrn_gates(q, f_ref[...], lb)
        v = i_ref[...]
        b = jnp.dot(tri_ref[...], jnp.log(f), precision=_HI, preferred_element_type=_F32)
        eb = jnp.exp(b)
        g = g_ref[...]
        nw = nw_ref[...]
        o = raw_ref[...]
        dout = do_ref[...]
        sgg = _sigmoid(g)
        r = lax.rsqrt(jnp.mean(o * o, axis=-1, keepdims=True) + LN_EPS)
        dg_ref[...] = (dout * (o * r * nw) * (sgg * (1.0 + g * (1.0 - sgg)))).astype(dg_ref.dtype)
        don = dout * (g * sgg)
        dnw_ref[0:1, :] += jnp.sum(don * o * r, axis=0, keepdims=True)
        dy = don * nw
        do_raw = r * dy - o * (r * r * r) * jnp.mean(o * dy, axis=-1, keepdims=True)

        ridx = lax.broadcasted_iota(jnp.int32, (A_CHUNK, HEAD_DIM), 0)
        dqs_t, dk_t, db_t, dv_t = [None] * _HG_CHUNKS, [None] * _HG_CHUNKS, [None] * _HG_CHUNKS, [None] * _HG_CHUNKS
        for c in reversed(range(_HG_CHUNKS)):
            sl = slice(c * A_CHUNK, (c + 1) * A_CHUNK)
            bc, qc, kc, vc, doc = b[sl], qs[sl], k[sl], v[sl], do_raw[sl]
            bl = bc[A_CHUNK - 1:A_CHUNK]
            ebc = eb[sl]
            ebl = jnp.exp(bl - bc)
            lam = jnp.exp(bl)
            qt = qc * ebc
            kt = kc * ebl
            dst = dstate[...]
            stp = st_ref[0, c]
            dob = doc.astype(_MXU_DTYPE)
            dstb = dst.astype(_MXU_DTYPE)
            dqt = jnp.dot(dob, stp.astype(_MXU_DTYPE), preferred_element_type=_F32)
            dkt = jnp.dot(vc.astype(_MXU_DTYPE), dstb, preferred_element_type=_F32)
            dv = lax.dot_general(kt.astype(_MXU_DTYPE), dstb, _NT, preferred_element_type=_F32)
            dlam = jnp.sum(stp * dst, axis=0, keepdims=True)
            dstate[...] = dst * lam + lax.dot_general(dob, qt.astype(_MXU_DTYPE), _TN, preferred_element_type=_F32)
            dqs_rows = []
            dk_in = jnp.zeros((A_CHUNK, HEAD_DIM), _F32)
            for i in range(A_CHUNK):
                di = jnp.exp(jnp.where(ridx <= i, bc[i:i + 1] - bc, _NEG))
                qi = qc[i:i + 1]
                doi = doc[i:i + 1]
                w = kc * di
                a = jnp.sum(qi * w, axis=1, keepdims=True)
                dv = dv + a * doi
                da = jnp.sum(doi * vc, axis=1, keepdims=True)
                dqs_rows.append(jnp.sum(da * w, axis=0, keepdims=True))
                dk_in = dk_in + da * (qi * di)
            dqs_in = jnp.concatenate(dqs_rows, axis=0)
            dbl = jnp.sum(dkt * kt, axis=0, keepdims=True) + dlam * lam
            db = qc * dqs_in - kc * dk_in + dqt * qt - dkt * kt
            db_t[c] = db + jnp.where(ridx == A_CHUNK - 1, dbl, 0.0)
            dqs_t[c] = dqs_in + dqt * ebc
            dk_t[c] = dk_in + dkt * ebl
            dv_t[c] = dv
        dqs = jnp.concatenate(dqs_t, axis=0)
        dk = jnp.concatenate(dk_t, axis=0)
        db = jnp.concatenate(db_t, axis=0)
        di_ref[...] = jnp.concatenate(dv_t, axis=0).astype(di_ref.dtype)
        dlogf = jnp.dot(triu_ref[...], db, precision=_HI, preferred_element_type=_F32)
        df = dlogf / f - dk
        df_ref[...] = (df * (1.0 - lb) * sg * (1.0 - sg)).astype(df_ref.dtype)
        dlb_ref[0, 0:1, :] += jnp.sum(df * (1.0 - sg), axis=0, keepdims=True)
        dq_ref[...] = (dqs * (sgq * (1.0 + q * (1.0 - sgq)))).astype(dq_ref.dtype)

    blk = (_HG_TILE, HEAD_DIM)
    last = _HG_TILES - 1

    def col(base):
        return pl.BlockSpec(blk, lambda h, t: (last - t, base + h))

    tri_spec = pl.BlockSpec(blk, lambda h, t: (0, 0))
    acc_spec = pl.BlockSpec((1, 8, HEAD_DIM), lambda h, t: (h, 0, 0))
    acc_shape = jax.ShapeDtypeStruct((A_HEADS, 8, HEAD_DIM), _F32)
    dq, df, di, dg, dnw, dlb = pl.pallas_call(
        body, grid=(A_HEADS, _HG_TILES),
        in_specs=[col(0), col(4), col(8), col(12), pl.BlockSpec((DEPTH, HEAD_DIM), lambda h, t: (0, h)),
                  pl.BlockSpec((1, HEAD_DIM), lambda h, t: (0, 0)), tri_spec, tri_spec, col(0), col(0),
                  pl.BlockSpec((1, _HG_CHUNKS, HEAD_DIM, HEAD_DIM), lambda h, t: (h, last - t, 0, 0))],
        out_specs=[col(0), col(0), col(0), col(0), pl.BlockSpec((8, HEAD_DIM), lambda h, t: (0, 0)), acc_spec],
        out_shape=[jax.ShapeDtypeStruct((SEQ, A_HEADS * HEAD_DIM), _MXU_DTYPE)] * 4
        + [jax.ShapeDtypeStruct((8, HEAD_DIM), _F32), acc_shape],
        scratch_shapes=[pltpu.VMEM((HEAD_DIM, HEAD_DIM), _F32)], name=name,
        compiler_params=_cparams(dimension_semantics=("arbitrary", "arbitrary")),
    )(proj, proj, proj, proj, lb_logits, norm_w.reshape(1, HEAD_DIM), tri, triu, raw, dmixed, states)
    return dq, df, di, dg, dnw[0], dlb[:, 0, :].reshape(A_HEADS * HEAD_DIM)


N_CHIP = N_DEV // 2
_MESH_ID = pl.DeviceIdType.MESH


def _place():
    x, y, c = lax.axis_index("x"), lax.axis_index("y"), lax.axis_index("c")
    chips = [(1 - x, y), (x, 1 - y), (1 - x, 1 - y)]
    return x, y, c, 2 * x + y, chips


def _sibling_swap(arrays, *, name):
    n = len(arrays)

    def body(*refs):
        ins, outs = refs[:n], refs[n:2 * n]
        send_sems, recv_sems = refs[2 * n:]
        x, y, c, _, _ = _place()
        copies = [pltpu.make_async_remote_copy(
            src_ref=ins[a], dst_ref=outs[a], send_sem=send_sems.at[a], recv_sem=recv_sems.at[a],
            device_id=(x, y, 1 - c), device_id_type=_MESH_ID) for a in range(n)]
        for cp in copies:
            cp.start()
        for cp in copies:
            cp.wait()

    any_spec = pl.BlockSpec(memory_space=pl.ANY)
    return pl.pallas_call(
        body, in_specs=[any_spec] * n, out_specs=[any_spec] * n,
        out_shape=[jax.ShapeDtypeStruct(a.shape, a.dtype) for a in arrays],
        scratch_shapes=[pltpu.SemaphoreType.DMA((n,)), pltpu.SemaphoreType.DMA((n,))],
        name=name, compiler_params=pltpu.CompilerParams(has_side_effects=True),
    )(*arrays)


_HBM = pl.BlockSpec(memory_space=pltpu.HBM)
_SEM = pl.BlockSpec(memory_space=pltpu.SEMAPHORE)
_TOKEN = pl.BlockSpec(memory_space=pltpu.VMEM)
_DATAFLOW = pltpu.SideEffectType.DATAFLOW_SIDE_EFFECTING


def _hbm(a):
    return pltpu.HBM(a.shape, a.dtype)


def _token_shape():
    return jax.ShapeDtypeStruct((8, 128), _F32)


def _dev_slot(px, py, pc):
    return 4 * px + 2 * py + pc


def _gather_start(blocks, landings, *, name):
    n = len(blocks)

    def body(*refs):
        ins, lands = refs[:n], refs[n:2 * n]
        send_sems, d2d_sems, ici_sems = refs[2 * n:2 * n + 3]
        token = refs[-1]
        x, y, c, _, chips = _place()
        for a in range(n):
            dst = lands[a].at[_dev_slot(x, y, c)]
            pltpu.make_async_remote_copy(src_ref=ins[a], dst_ref=dst, send_sem=send_sems.at[4 * a], recv_sem=d2d_sems.at[a],
                                         device_id=(x, y, 1 - c), device_id_type=_MESH_ID).start()
            for j, chip in enumerate(chips):
                pltpu.make_async_remote_copy(src_ref=ins[a], dst_ref=dst, send_sem=send_sems.at[4 * a + 1 + j],
                                             recv_sem=ici_sems.at[3 * a + j], device_id=(*chip, c),
                                             device_id_type=_MESH_ID).start()
        token[...] = jnp.zeros_like(token)

    res = pl.pallas_call(
        body, name=name, in_specs=[_HBM] * (2 * n),
        out_shape=(pltpu.SemaphoreType.DMA((4 * n,)), pltpu.SemaphoreType.DMA((n,)), pltpu.SemaphoreType.DMA((3 * n,)),
                   *[_hbm(b) for b in blocks], *[_hbm(b) for b in landings], _token_shape()),
        out_specs=(_SEM, _SEM, _SEM, *[_HBM] * (2 * n), _TOKEN),
        input_output_aliases={i: 3 + i for i in range(2 * n)},
        compiler_params=pltpu.CompilerParams(has_side_effects=_DATAFLOW),
    )(*[pltpu.with_memory_space_constraint(b, pltpu.HBM) for b in blocks],
      *[pltpu.with_memory_space_constraint(b, pltpu.HBM) for b in landings])
    return res[0], res[1], res[2], list(res[3:3 + n]), list(res[3 + n:3 + 2 * n]), res[-1]


def _gather_forward(landings, ici_sems, first, after, *, name):
    n = len(landings)

    def body(*refs):
        lands = refs[:n]
        ici = refs[n]
        f_send, f_recv = refs[n + 2], refs[n + 3]
        token = refs[-1]
        x, y, c, _, chips = _place()
        for a in range(n):
            for j, chip in enumerate(chips):
                blk = lands[a].at[_dev_slot(*chip, c)]
                pltpu.make_async_remote_copy(src_ref=blk, dst_ref=blk, send_sem=f_send.at[3 * a + j],
                                             recv_sem=ici.at[3 * (first + a) + j], device_id=(*chip, c),
                                             device_id_type=_MESH_ID).wait_recv()
                pltpu.make_async_remote_copy(src_ref=blk, dst_ref=blk, send_sem=f_send.at[3 * a + j], recv_sem=f_recv.at[3 * a + j],
                                             device_id=(x, y, 1 - c), device_id_type=_MESH_ID).start()
        token[...] = jnp.zeros_like(token)

    res = pl.pallas_call(
        body, name=name, in_specs=[_HBM] * n + [_SEM, pl.BlockSpec(memory_space=pl.ANY)],
        out_shape=(pltpu.SemaphoreType.DMA((3 * n,)), pltpu.SemaphoreType.DMA((3 * n,)), *[_hbm(b) for b in landings], _token_shape()),
        out_specs=(_SEM, _SEM, *[_HBM] * n, _TOKEN),
        input_output_aliases={i: 2 + i for i in range(n)},
        compiler_params=pltpu.CompilerParams(has_side_effects=_DATAFLOW),
    )(*landings, ici_sems, after)
    return res[0], res[1], list(res[2:2 + n]), res[-1]


def _gather_wait(blocks, landings, send_sems, d2d_sems, first, f_send, f_recv, after, *, name):
    n = len(landings)

    def body(*refs):
        ins, lands = refs[:n], refs[n:2 * n]
        send, d2d, fs, fr = refs[2 * n:2 * n + 4]
        x, y, c, _, chips = _place()
        me = (x, y, c)
        for a in range(n):
            own = lands[a].at[_dev_slot(x, y, 1 - c)]
            g = first + a
            pltpu.make_async_remote_copy(src_ref=ins[a], dst_ref=own, send_sem=send.at[4 * g], recv_sem=d2d.at[g],
                                         device_id=me, device_id_type=_MESH_ID).wait_recv()
            for j, chip in enumerate(chips):
                blk = lands[a].at[_dev_slot(*chip, 1 - c)]
                pltpu.make_async_remote_copy(src_ref=blk, dst_ref=blk, send_sem=fs.at[3 * a + j], recv_sem=fr.at[3 * a + j],
                                             device_id=me, device_id_type=_MESH_ID).wait_recv()
            for k in range(4):
                pltpu.make_async_remote_copy(src_ref=ins[a], dst_ref=own, send_sem=send.at[4 * g + k], recv_sem=d2d.at[g],
                                             device_id=me, device_id_type=_MESH_ID).wait_send()
            for j in range(3):
                pltpu.make_async_remote_copy(src_ref=own, dst_ref=own, send_sem=fs.at[3 * a + j], recv_sem=fr.at[3 * a + j],
                                             device_id=me, device_id_type=_MESH_ID).wait_send()

    res = pl.pallas_call(
        body, name=name, in_specs=[_HBM] * (2 * n) + [_SEM] * 4 + [pl.BlockSpec(memory_space=pl.ANY)],
        out_shape=(*[_hbm(b) for b in blocks], *[_hbm(b) for b in landings]), out_specs=tuple([_HBM] * (2 * n)),
        input_output_aliases={i: i for i in range(2 * n)},
        compiler_params=pltpu.CompilerParams(has_side_effects=_DATAFLOW),
    )(*blocks, *landings, send_sems, d2d_sems, f_send, f_recv, after)
    return list(res[n:])


def _swap_start(mine, landings, *, name):
    n = len(mine)

    def body(*refs):
        ins, lands = refs[:n], refs[n:2 * n]
        send_sems, recv_sems = refs[2 * n:2 * n + 2]
        token = refs[-1]
        x, y, c, _, _ = _place()
        for a in range(n):
            pltpu.make_async_remote_copy(src_ref=ins[a], dst_ref=lands[a], send_sem=send_sems.at[a],
                                         recv_sem=recv_sems.at[a], device_id=(x, y, 1 - c), device_id_type=_MESH_ID).start()
        token[...] = jnp.zeros_like(token)

    res = pl.pallas_call(
        body, name=name, in_specs=[_HBM] * (2 * n),
        out_shape=(pltpu.SemaphoreType.DMA((n,)), pltpu.SemaphoreType.DMA((n,)),
                   *[_hbm(b) for b in mine], *[_hbm(b) for b in landings], _token_shape()),
        out_specs=(_SEM, _SEM, *[_HBM] * (2 * n), _TOKEN),
        input_output_aliases={i: 2 + i for i in range(2 * n)},
        compiler_params=pltpu.CompilerParams(has_side_effects=_DATAFLOW),
    )(*[pltpu.with_memory_space_constraint(b, pltpu.HBM) for b in mine],
      *[pltpu.with_memory_space_constraint(b, pltpu.HBM) for b in landings])
    return res[0], res[1], list(res[2:2 + n]), list(res[2 + n:2 + 2 * n]), res[-1]


def _swap_wait(mine, landings, send_sems, recv_sems, after, *, name):
    n = len(mine)

    def body(*refs):
        ins, lands = refs[:n], refs[n:2 * n]
        send, recv = refs[2 * n:2 * n + 2]
        x, y, c, _, _ = _place()
        for a in range(n):
            cp = pltpu.make_async_remote_copy(src_ref=ins[a], dst_ref=lands[a], send_sem=send.at[a],
                                              recv_sem=recv.at[a], device_id=(x, y, c), device_id_type=_MESH_ID)
            cp.wait_recv()
            cp.wait_send()

    res = pl.pallas_call(
        body, name=name, in_specs=[_HBM] * (2 * n) + [_SEM] * 2 + [pl.BlockSpec(memory_space=pl.ANY)],
        out_shape=(*[_hbm(b) for b in mine], *[_hbm(b) for b in landings]), out_specs=tuple([_HBM] * (2 * n)),
        input_output_aliases={i: i for i in range(2 * n)},
        compiler_params=pltpu.CompilerParams(has_side_effects=_DATAFLOW),
    )(*mine, *landings, send_sems, recv_sems, after)
    return list(res[:n]), list(res[n:])


def _chip_exchange_start(sums, landings, *, name):
    n = len(sums)

    def body(*refs):
        ins, lands = refs[:n], refs[n:2 * n]
        send_sems, recv_sems = refs[2 * n:2 * n + 2]
        token = refs[-1]
        _, _, c, p, chips = _place()
        for a in range(n):
            for j, (qx, qy) in enumerate(chips):
                pltpu.make_async_remote_copy(src_ref=ins[a].at[2 * qx + qy], dst_ref=lands[a].at[p], send_sem=send_sems.at[3 * a + j],
                                             recv_sem=recv_sems.at[3 * a + j], device_id=(qx, qy, c), device_id_type=_MESH_ID).start()
        token[...] = jnp.zeros_like(token)

    res = pl.pallas_call(
        body, name=name, in_specs=[_HBM] * (2 * n),
        out_shape=(pltpu.SemaphoreType.DMA((3 * n,)), pltpu.SemaphoreType.DMA((3 * n,)),
                   *[_hbm(b) for b in sums], *[_hbm(b) for b in landings], _token_shape()),
        out_specs=(_SEM, _SEM, *[_HBM] * (2 * n), _TOKEN),
        input_output_aliases={i: 2 + i for i in range(2 * n)},
        compiler_params=pltpu.CompilerParams(has_side_effects=_DATAFLOW),
    )(*[pltpu.with_memory_space_constraint(b, pltpu.HBM) for b in sums],
      *[pltpu.with_memory_space_constraint(b, pltpu.HBM) for b in landings])
    return res[0], res[1], list(res[2:2 + n]), list(res[2 + n:2 + 2 * n]), res[-1]


def _chip_exchange_wait(sums, landings, send_sems, recv_sems, after, *, name):
    n = len(sums)

    def body(*refs):
        ins, lands = refs[:n], refs[n:2 * n]
        send, recv = refs[2 * n:2 * n + 2]
        x, y, c, _, chips = _place()
        for a in range(n):
            for j, (qx, qy) in enumerate(chips):
                q = 2 * qx + qy
                cp = pltpu.make_async_remote_copy(src_ref=ins[a].at[q], dst_ref=lands[a].at[q], send_sem=send.at[3 * a + j],
                                                  recv_sem=recv.at[3 * a + j], device_id=(x, y, c), device_id_type=_MESH_ID)
                cp.wait_recv()
                cp.wait_send()

    res = pl.pallas_call(
        body, name=name, in_specs=[_HBM] * (2 * n) + [_SEM] * 2 + [pl.BlockSpec(memory_space=pl.ANY)] * len(after),
        out_shape=(*[_hbm(b) for b in sums], *[_hbm(b) for b in landings]), out_specs=tuple([_HBM] * (2 * n)),
        input_output_aliases={i: i for i in range(2 * n)},
        compiler_params=pltpu.CompilerParams(has_side_effects=_DATAFLOW),
    )(*sums, *landings, send_sems, recv_sems, *after)
    return list(res[:n]), list(res[n:])


_C1 = 1.0 - ADAM_B1 ** ADAM_STEP
_C2 = 1.0 - ADAM_B2 ** ADAM_STEP


def _adamw_math(g, w, m, v):
    m = ADAM_B1 * m + (1.0 - ADAM_B1) * g
    v = ADAM_B2 * v + (1.0 - ADAM_B2) * (g * g)
    delta = -ADAM_LR * ((m / _C1) / (jnp.sqrt(v / _C2) + ADAM_EPS) + ADAM_WD * w)
    return delta, m, v


def _adamw_reduce(landed, sums, chip, w, m, v, layer, prev, *, name):
    _, R, C = w.shape
    tr = max(t for t in range(16, R + 1, 16) if R % t == 0 and t * C <= 256 * 1024)

    def body(chip_ref, p_ref, own_ref, w_ref, m_ref, v_ref, *rest):
        g_ref, d_ref, nm_ref, nv_ref = rest[-4:]
        own = own_ref[...].astype(_F32)
        g = jnp.where(chip_ref[0] == 0, own, p_ref[0].astype(_F32))
        for q in range(1, N_CHIP):
            g = g + jnp.where(chip_ref[0] == q, own, p_ref[q].astype(_F32))
        d, nm, nv = _adamw_math(g, w_ref[...], m_ref[...], v_ref[...])
        g_ref[...] = g
        d_ref[...] = d
        nm_ref[...] = nm
        nv_ref[...] = nv

    blk = pl.BlockSpec((None, tr, C), lambda i, chip: (layer, i, 0))
    shape = jax.ShapeDtypeStruct((DEPTH, R, C), _F32)
    kept = [] if prev is None else list(prev)
    grid_spec = pltpu.PrefetchScalarGridSpec(
        num_scalar_prefetch=1, grid=(R // tr,),
        in_specs=[pl.BlockSpec((N_CHIP, tr, C), lambda i, chip: (0, i, 0)),
                  pl.BlockSpec((None, tr, C), lambda i, chip: (chip[0], i, 0)), blk, blk, blk]
        + [pl.BlockSpec(memory_space=pl.ANY)] * len(kept),
        out_specs=[blk] * 4)
    return pl.pallas_call(
        body, grid_spec=grid_spec, out_shape=[shape] * 4, name=name,
        input_output_aliases={6 + k: k for k in range(len(kept))},
        compiler_params=_cparams(dimension_semantics=("parallel",)),
    )(chip.reshape(1), landed, sums, w, m, v, *kept)


_PACK_LANES = 128
_LAYER_ROWS = 248
_LB_ROWS = (A_HEADS * HEAD_DIM) // _PACK_LANES


def _small_reduce(parts, lb_logits, *, name):
    rows = DEPTH * _LAYER_ROWS

    def body(p_ref, lg_ref, o_ref):
        g = p_ref[0]
        for s in range(1, N_DEV):
            g = g + p_ref[s]
        o_ref[...] = g
        lg = lg_ref[...]
        e = jnp.exp(lg - jnp.max(lg, axis=0, keepdims=True))
        p = e / jnp.sum(e, axis=0, keepdims=True)
        d1 = g[_LAYER_ROWS:_LAYER_ROWS + _LB_ROWS, :] * p[0] * p[1]
        o_ref[0:_LB_ROWS, :] = -d1
        o_ref[_LAYER_ROWS:_LAYER_ROWS + _LB_ROWS, :] = d1

    return pl.pallas_call(
        body, out_shape=jax.ShapeDtypeStruct((rows, _PACK_LANES), _F32), name=name,
        compiler_params=_cparams(),
    )(parts, lb_logits.reshape(DEPTH, _LB_ROWS, _PACK_LANES))


def _adamw_small(g, w, m, v, *, name):
    def body(g_ref, w_ref, m_ref, v_ref, d_ref, nm_ref, nv_ref):
        d, nm, nv = _adamw_math(g_ref[...], w_ref[...], m_ref[...], v_ref[...])
        d_ref[...] = d
        nm_ref[...] = nm
        nv_ref[...] = nv

    shape = jax.ShapeDtypeStruct(g.shape, _F32)
    return pl.pallas_call(body, out_shape=[shape] * 3, name=name, compiler_params=_cparams())(g, w, m, v)


def _pack(vectors, rows):
    flat = jnp.concatenate([v.reshape(-1).astype(_F32) for v in vectors])
    return jnp.pad(flat, (0, rows * _PACK_LANES - flat.shape[0])).reshape(rows, _PACK_LANES)


def _unpack(packed, shapes):
    flat = packed.reshape(-1)
    out, at = [], 0
    for s in shapes:
        size = int(np.prod(s))
        out.append(flat[at:at + size].reshape(s))
        at += size
    return out


_BIG = ("w_in", "w_gate", "w_up", "w_out", "w_down")
_COLUMN_SHARDED = ("w_in", "w_gate", "w_up")


def _full_weight(name, g):
    if name == "w_out":
        return g.reshape(D_MODEL, D_MODEL)
    if name == "w_down":
        return g.reshape(D_FF, D_MODEL)
    if name == "conv_w":
        return g.transpose(1, 0, 2).reshape(g.shape[1], N_DEV * SHARD_COLS)
    return g


class _WeightGather:
    def __init__(self, names, first, blocks, lands, sems, tag):
        self.names, self.first, self.blocks, self.lands, self.sems, self.tag = names, first, blocks, lands, sems, tag
        self.forwarded = None

    def forward(self, after):
        f_send, f_recv, self.lands, token = _gather_forward(self.lands, self.sems[2], self.first, after,
                                                            name=f"gather_forward_{self.tag}")
        self.forwarded = (f_send, f_recv)
        return token

    def wait(self, after):
        if self.forwarded is None:
            self.forward(after)
        got = _gather_wait(self.blocks, self.lands, self.sems[0], self.sems[1], self.first, *self.forwarded, after,
                           name=f"gather_wait_{self.tag}")
        return {n: _full_weight(n, g) for n, g in zip(self.names, got)}


def _start_gathers(groups, me, name):
    blocks = [b for _, _, bs in groups for b in bs]
    landings = [lax.dynamic_update_index_in_dim(lax.empty((N_DEV,) + b.shape, b.dtype), b[None], me, 0) for b in blocks]
    send, d2d, ici, blocks, landings, token = _gather_start(blocks, landings, name=name)
    out, first = [], 0
    for tag, names, bs in groups:
        k = len(bs)
        out.append(_WeightGather(names, first, blocks[first:first + k], landings[first:first + k], (send, d2d, ici), tag))
        first += k
    return out, token


class _LayerWeights:
    def __init__(self, ready, pending=(), forwards=(), tokens=()):
        self.ready, self.pending, self.forwards, self._tokens = dict(ready), list(pending), list(forwards), list(tokens)

    def at(self, point, after):
        for when, gather in self.forwards:
            if when == point:
                self._tokens.append(gather.forward(after))

    def tokens(self):
        out, self._tokens = self._tokens, []
        return out

    def get(self, name, after):
        if name not in self.ready:
            group, = [g for g in self.pending if name in g.names]
            self.ready.update(group.wait(after))
        return self.ready[name]


def _layer_fwd(x, xb, ws, lb_logits, a_norm_w, c_sink, ln1_g, ln1_b, conv_b, ln2_g, ln2_b, tabs, l):
    proj = _mm_w_slabs(xb, ws.get("w_in", xb), tm=1024, after=ws.tokens(), name=f"proj_{l}")
    o_a, raw, states = _hgrn_fwd(proj, lb_logits, a_norm_w, l, name=f"hgrn_fwd_{l}")
    ws.at("hgrn", o_a)
    o_b, lse_b = _band_fwd(proj, tabs, name=f"dilated_fwd_{l}", **_DILATED)
    o_c, lse_c = _band_fwd(proj, tabs, sink=c_sink, name=f"swa_fwd_{l}", **_SWA)
    ws.at("swa", o_c)
    mixed = jnp.concatenate([o_a, o_b, o_c], axis=1).astype(_MXU_DTYPE)
    y = _mm(mixed, ws.get("w_out", mixed), **_TILE_MIX, after=ws.tokens(), name=f"mix_out_{l}")
    z1, x1, x1b = _ln_fwd(x, y, ln1_g, ln1_b, name=f"ln1_fwd_{l}")
    g = _mm_w_slabs(x1b, ws.get("w_gate", x1b), tm=1024, out_dtype=_ACT_DTYPE, name=f"ffn_gate_{l}")
    u = _mm_w_slabs(x1b, ws.get("w_up", x1b), tm=1024, out_dtype=_ACT_DTYPE, name=f"ffn_up_{l}")
    ws.at("up", u)
    hb = _conv_gate_fwd(g, u, ws.get("conv_w", u), conv_b, name=f"conv_gate_fwd_{l}")
    y2 = _mm(hb, ws.get("w_down", hb), **_TILE_DOWN, after=ws.tokens(), name=f"ffn_down_{l}")
    ws.at("down", y2)
    z2, x2, x2b = _ln_fwd(x1, y2, ln2_g, ln2_b, name=f"ln2_fwd_{l}")
    res = dict(xb=xb, proj=proj, raw=raw, states=states, o_b=o_b, lse_b=lse_b, o_c=o_c, lse_c=lse_c,
               mixed=mixed, z1=z1, x1b=x1b, g=g, u=u, hb=hb, z2=z2)
    return x2, x2b, res


class _GradExchange:
    def __init__(self, core, chip):
        self.core, self.other = core.reshape(1), (1 - core).reshape(1)
        self.chip, self.groups, self.swapping, self._tokens = chip, [], [], []

    def launch(self, halves, l, tag, behind):
        mine = [fn(self.other, None, "sibling") for _, fn in halves]
        if behind:
            landings = [lax.empty(m.shape, m.dtype) for m in mine]
            send, recv, mine, landings, token = _swap_start(mine, landings, name=f"swap_start_{tag}")
            self.swapping.append((halves, l, tag, send, recv, mine, landings))
            self._tokens.append(token)
        else:
            self._exchange(halves, l, tag, _sibling_swap(mine, name=f"swap_grads_{tag}"))

    def advance(self, after):
        for halves, l, tag, send, recv, mine, landings in self.swapping:
            _, theirs = _swap_wait(mine, landings, send, recv, after, name=f"swap_wait_{tag}")
            self._exchange(halves, l, tag, theirs)
        self.swapping = []

    def _exchange(self, halves, l, tag, theirs):
        names = [n for n, _ in halves]
        sums = [fn(self.core, t, "chip") for (_, fn), t in zip(halves, theirs)]
        landings = [lax.empty(s.shape, s.dtype) for s in sums]
        send, recv, sums, landings, token = _chip_exchange_start(sums, landings, name=f"exchange_start_{tag}")
        self.groups.append((names, l, tag, send, recv, sums, landings))
        self._tokens.append(token)

    def tokens(self):
        out, self._tokens = self._tokens, []
        return out

    def finish(self, weights, mom1, mom2, after):
        out = {}
        after = list(after) + self.tokens()
        for names, l, tag, send, recv, sums, landings in self.groups:
            sums, landings = _chip_exchange_wait(sums, landings, send, recv, after, name=f"exchange_wait_{tag}")
            for n, s, landed in zip(names, sums, landings):
                out[n] = _adamw_reduce(landed, s, self.chip, weights[n], mom1[n], mom2[n], l, out.get(n), name=f"adamw_{n}_{l}")
            after = [out[n][0] for n in names]
        return out


def _layer_bwd(dx2, res, w, lb_logits, a_norm_w, c_sink, ln1_g, conv_b, ln2_g, tabs, exchange, l):
    dz2, dz2b, d_ln2_g, d_ln2_b = _ln_bwd(res["z2"], dx2, None, ln2_g, name=f"ln2_bwd_{l}")
    exchange.advance(dz2b)
    dh = _mm(dz2b, w["w_down"], tb=True, **_TILE_DOWN_DX, out_dtype=_ACT_DTYPE, after=exchange.tokens(),
             name=f"ffn_down_dx_{l}")
    dg, du, d_conv_w, d_conv_b = _conv_gate_bwd(dh, res["g"], res["u"], w["conv_w"], conv_b, name=f"conv_gate_bwd_{l}")
    t = _mm_nt_w_slabs(dg, w["w_gate"], **_TILE_NT_SLABS, name=f"ffn_gate_dx_{l}")
    dx1 = _mm_nt_w_slabs(du, w["w_up"], **_TILE_NT_SLABS, add=t, name=f"ffn_up_dx_{l}")
    dz1, dz1b, d_ln1_g, d_ln1_b = _ln_bwd(res["z1"], dx1, dz2, ln1_g, name=f"ln1_bwd_{l}")

    def slab_half(slabbed, other, tag):
        return lambda sel, add, phase: _mm_tn_half(slabbed, other, sel, add=add, name=f"{tag}_{phase}_{l}")

    exchange.launch([("w_down", slab_half(res["hb"], dz2b, "ffn_down_dw")), ("w_gate", slab_half(dg, res["x1b"], "ffn_gate_dw")),
                     ("w_up", slab_half(du, res["x1b"], "ffn_up_dw")),
                     ("w_out", lambda sel, add, phase: _mm_tn_rows_half(res["mixed"], dz1b, sel, add=add,
                                                                        name=f"mix_out_dw_{phase}_{l}"))],
                    l, f"ffn_{l}", True)
    dmixed = _mm(dz1b, w["w_out"], tb=True, **_TILE_MIX, after=exchange.tokens(), name=f"mix_out_dx_{l}")
    dq_a, df_a, di_a, dg_a, d_norm_w, d_lb = _hgrn_bwd(res["proj"], lb_logits, a_norm_w, res["raw"], res["states"],
                                                      dmixed, l, name=f"hgrn_bwd_{l}")
    exchange.advance(dq_a)
    dq_b, dk_b, dv_b = _band_bwd(res["proj"], tabs, dmixed, res["o_b"], res["lse_b"], do0=A_HEADS, after=exchange.tokens(),
                                 name=f"dilated_bwd_{l}", **_DILATED)
    dq_c, dk_c, dv_c, d_sink = _band_bwd(res["proj"], tabs, dmixed, res["o_c"], res["lse_c"], do0=A_HEADS + B_HEADS,
                                         sink=c_sink, name=f"swa_bwd_{l}", **_SWA)
    dproj = jnp.concatenate([dq_a, df_a, di_a, dg_a, dq_b, dk_b, dv_b, dq_c, dk_c, dv_c], axis=1)
    exchange.launch([("w_in", slab_half(dproj, res["xb"], "proj_dw"))], l, f"mix_{l}", l > 0)
    dx = _mm_nt_w_slabs(dproj, w["w_in"], **_TILE_NT_SLABS, add=dz1, add_scale=ALPHA, after=exchange.tokens(),
                        name=f"proj_dx_{l}")
    small = [d_lb, d_norm_w, jnp.pad(d_sink, (0, _PACK_LANES - C_HEADS)), d_ln1_g, d_ln1_b, d_ln2_g, d_ln2_b, d_conv_b,
             d_conv_w]
    return dx, small


def kernel(x, w_in, lb_logits, a_norm_w, c_sinks, w_out, ln1_g, ln1_b, w_gate, w_up, conv_w, conv_b, w_down, ln2_g, ln2_b, loss_target, m_w_in, m_lb_logits, m_a_norm_w, m_c_sinks, m_w_out, m_ln1_g, m_ln1_b, m_w_gate, m_w_up, m_conv_w, m_conv_b, m_w_down, m_ln2_g, m_ln2_b, v_w_in, v_lb_logits, v_a_norm_w, v_c_sinks, v_w_out, v_ln1_g, v_ln1_b, v_w_gate, v_w_up, v_conv_w, v_conv_b, v_w_down, v_ln2_g, v_ln2_b):
    weights = dict(w_in=w_in, lb_logits=lb_logits, a_norm_w=a_norm_w, c_sinks=c_sinks, w_out=w_out, ln1_g=ln1_g, ln1_b=ln1_b,
                   w_gate=w_gate, w_up=w_up, conv_w=conv_w, conv_b=conv_b, w_down=w_down, ln2_g=ln2_g, ln2_b=ln2_b)
    mom1 = dict(w_in=m_w_in, lb_logits=m_lb_logits, a_norm_w=m_a_norm_w, c_sinks=m_c_sinks, w_out=m_w_out, ln1_g=m_ln1_g,
                ln1_b=m_ln1_b, w_gate=m_w_gate, w_up=m_w_up, conv_w=m_conv_w, conv_b=m_conv_b, w_down=m_w_down, ln2_g=m_ln2_g,
                ln2_b=m_ln2_b)
    mom2 = dict(w_in=v_w_in, lb_logits=v_lb_logits, a_norm_w=v_a_norm_w, c_sinks=v_c_sinks, w_out=v_w_out, ln1_g=v_ln1_g,
                ln1_b=v_ln1_b, w_gate=v_w_gate, w_up=v_w_up, conv_w=v_conv_w, conv_b=v_conv_b, w_down=v_w_down, ln2_g=v_ln2_g,
                ln2_b=v_ln2_b)
    core = lax.axis_index("c").astype(jnp.int32)
    me = 4 * lax.axis_index("x") + 2 * lax.axis_index("y") + core
    tabs = _rope_tables()

    chip = (2 * lax.axis_index("x") + lax.axis_index("y")).astype(jnp.int32)

    def as_slabs(d):
        return {n: jnp.swapaxes(d[n], 1, 2) if n in _COLUMN_SHARDED else d[n] for n in _BIG}

    w_views = as_slabs(weights)

    def block(n, l, after=()):
        return conv_w[l] if n == "conv_w" else _cast_layer(w_views[n], l, after=after, name=f"cast_{n}_{l}")

    (in0,), started_first = _start_gathers([("w_in_0", ("w_in",), [block("w_in", 0)])], me, "gather_start_first")
    order = [(("w_out",), 0), (("w_gate", "w_up", "conv_w"), 0), (("w_down",), 0),
             (("w_in",), 1), (("w_out",), 1), (("w_gate", "w_up", "conv_w"), 1), (("w_down",), 1)]
    gathers, started = _start_gathers([(f"{names[0]}_{l}", names, [block(n, l, [started_first]) for n in names])
                                       for names, l in order], me, "gather_start_rest")
    out0, ffn0, down0, in1, out1, ffn1, down1 = gathers
    layer_ws = [_LayerWeights(in0.wait(started), [out0, ffn0, down0],
                              [("hgrn", out0), ("swa", ffn0), ("up", down0), ("down", in1)]),
                _LayerWeights({}, [in1, out1, ffn1, down1], [("hgrn", out1), ("swa", ffn1), ("up", down1)])]

    xs = x[0]
    xb = xs.astype(_MXU_DTYPE)
    saved = []
    for l in range(DEPTH):
        xs, xb, res = _layer_fwd(xs, xb, layer_ws[l], lb_logits, a_norm_w[l], c_sinks[l], ln1_g[l], ln1_b[l], conv_b[l],
                                 ln2_g[l], ln2_b[l], tabs, l)
        saved.append(res)
    loss_part, dx = _loss_head(xs, loss_target[0], name="loss_head")
    loss = lax.psum(loss_part, ("x", "y", "c"))

    exchange = _GradExchange(core, chip)
    small_parts = [None] * DEPTH
    for l in reversed(range(DEPTH)):
        dx, small = _layer_bwd(dx, saved[l], layer_ws[l].ready, lb_logits, a_norm_w[l], c_sinks[l], ln1_g[l], conv_b[l],
                               ln2_g[l], tabs, exchange, l)
        small_parts[l] = _pack(small, _LAYER_ROWS)
    (small_gather,), small_started = _start_gathers(
        [("small_grads", ("small",), [jnp.concatenate(small_parts, axis=0)])], me, "gather_start_small")
    updated = exchange.finish(w_views, as_slabs(mom1), as_slabs(mom2), [dx, small_started])
    gathered = small_gather.wait(updated["w_in"][0])["small"]
    updated = {n: tuple(jnp.swapaxes(t, 1, 2) for t in u) if n in _COLUMN_SHARDED else u for n, u in updated.items()}
    g_small = _small_reduce(gathered, lb_logits, name="small_grads")

    per_layer = [(A_HEADS * HEAD_DIM,), (HEAD_DIM,), (_PACK_LANES,), (D_MODEL,), (D_MODEL,), (D_MODEL,), (D_MODEL,), (D_FF,),
                 (3, D_FF)]
    names = ("lb_logits", "a_norm_w", "c_sinks", "ln1_g", "ln1_b", "ln2_g", "ln2_b", "conv_b", "conv_w")
    grads = {n: [] for n in names}
    for l in range(DEPTH):
        for n, t in zip(names, _unpack(g_small[l * _LAYER_ROWS:(l + 1) * _LAYER_ROWS], per_layer)):
            grads[n].append(t)
    grads = {n: jnp.stack(t) for n, t in grads.items()}
    grads["c_sinks"] = grads["c_sinks"][:, :C_HEADS]
    grads["conv_w"] = lax.dynamic_slice_in_dim(grads["conv_w"], me * SHARD_COLS, SHARD_COLS, axis=2)
    shapes = [grads[n].shape for n in names]
    rows = -(-sum(int(np.prod(s)) for s in shapes) // (8 * _PACK_LANES)) * 8
    d_s, m_s, v_s = _adamw_small(_pack([grads[n] for n in names], rows), _pack([weights[n] for n in names], rows),
                                 _pack([mom1[n] for n in names], rows), _pack([mom2[n] for n in names], rows),
                                 name="adamw_small")
    delta = dict(zip(names, _unpack(d_s, shapes)))
    new_m = dict(zip(names, _unpack(m_s, shapes)))
    new_v = dict(zip(names, _unpack(v_s, shapes)))
    for n in _BIG:
        grads[n], delta[n], new_m[n], new_v[n] = updated[n]

    order = ("w_in", "lb_logits", "a_norm_w", "c_sinks", "w_out", "ln1_g", "ln1_b", "w_gate", "w_up", "conv_w", "conv_b",
             "w_down", "ln2_g", "ln2_b")
    return (loss, dx[None], *[grads[n] for n in order], *[delta[n] for n in order], *[new_m[n] for n in order],
            *[new_v[n] for n in order])
```

```python
import functools

import jax
import jax.numpy as jnp
import numpy as np
from jax import lax
from jax.experimental import pallas as pl
from jax.experimental.pallas import tpu as pltpu

D_MODEL = 2048
SEQ = 2048
DEPTH = 2
HEAD_DIM = 128
A_HEADS = 4
B_HEADS = 6
C_HEADS = 6
C_KV_HEADS = 2
A_CHUNK = 16
DILATIONS = (1, 4, 16)
BLOCK = 128
ROPE_THETA = 500000.0
ROPE_DIM = 32
D_FF = 5632
IN_WIDTH = 5632
LN_EPS = 1e-5
ALPHA = (2 * DEPTH) ** 0.25
N_DEV = 8
SHARD_COLS = IN_WIDTH // N_DEV

ADAM_LR = 0.001
ADAM_B1 = 0.9
ADAM_B2 = 0.999
ADAM_EPS = 1e-08
ADAM_WD = 0.01
ADAM_STEP = 10

A_COLS = 16
QKV_COLS = 28
QB0, KB0, VB0, QC0, KC0, VC0 = 0, 6, 12, 18, 24, 26

_MXU_DTYPE = jnp.bfloat16
_GRAD_DTYPE = jnp.bfloat16
_ACT_DTYPE = jnp.bfloat16
_NEG = -1e30
_VMEM_LIMIT = 56 * 2 ** 20

_F32 = jnp.float32


def _sigmoid(x):
    return 0.5 * jnp.tanh(0.5 * x) + 0.5


def _cparams(**kw):
    return pltpu.CompilerParams(vmem_limit_bytes=_VMEM_LIMIT, **kw)


_TILE_MIX = dict(tm=1024, tn=1024)
_TILE_DOWN = dict(tm=1024, tn=512)
_TILE_DOWN_DX = dict(tm=1024, tn=1408)
_TILE_DOWN_DW = dict(tm=1408, tn=1024)
_TILE_NT_SLABS = dict(tm=1024, tn=512)


def _mm(a, b, *, ta=False, tb=False, tm, tn, out_dtype=_F32, add=None, add_scale=1.0, after=(), name):
    K = a.shape[0] if ta else a.shape[1]
    M = a.shape[1] if ta else a.shape[0]
    N = b.shape[0] if tb else b.shape[1]
    assert (b.shape[1] if tb else b.shape[0]) == K and M % tm == 0 and N % tn == 0
    dn = (((0 if ta else 1,), (1 if tb else 0,)), ((), ()))

    def body(*refs):
        a_ref, b_ref = refs[:2]
        o_ref = refs[-1]
        r = lax.dot_general(a_ref[...], b_ref[...], dn, preferred_element_type=_F32)
        if add is not None:
            r = r + add_scale * refs[2][...]
        o_ref[...] = r.astype(o_ref.dtype)

    a_spec = pl.BlockSpec((K, tm), lambda i, j: (0, i)) if ta else pl.BlockSpec((tm, K), lambda i, j: (i, 0))
    b_spec = pl.BlockSpec((tn, K), lambda i, j: (j, 0)) if tb else pl.BlockSpec((K, tn), lambda i, j: (0, j))
    o_spec = pl.BlockSpec((tm, tn), lambda i, j: (i, j))
    in_specs = [a_spec, b_spec] + ([o_spec] if add is not None else []) + [pl.BlockSpec(memory_space=pl.ANY)] * len(after)
    args = (a, b) + ((add,) if add is not None else ()) + tuple(after)
    return pl.pallas_call(
        body, grid=(M // tm, N // tn), in_specs=in_specs, out_specs=o_spec,
        out_shape=jax.ShapeDtypeStruct((M, N), out_dtype), name=name,
        compiler_params=_cparams(dimension_semantics=("parallel", "parallel")),
    )(*args)


_PAIR = 2 * SHARD_COLS


def _mm_tn_slabs(a, b, *, tm, name):
    K, M = a.shape
    assert b.shape == (K, N_DEV * SHARD_COLS) and M % tm == 0

    def body(a_ref, b_ref, o_ref):
        a_blk = a_ref[...]
        for s in range(2):
            o_ref[s] = lax.dot_general(b_ref[:, s * SHARD_COLS:(s + 1) * SHARD_COLS], a_blk, _TN,
                                       preferred_element_type=_F32).astype(o_ref.dtype)

    return pl.pallas_call(
        body, grid=(M // tm, N_DEV // 2),
        in_specs=[pl.BlockSpec((K, tm), lambda i, p: (0, i)), pl.BlockSpec((K, _PAIR), lambda i, p: (0, p))],
        out_specs=pl.BlockSpec((2, SHARD_COLS, tm), lambda i, p: (p, 0, i)),
        out_shape=jax.ShapeDtypeStruct((N_DEV, SHARD_COLS, M), _GRAD_DTYPE), name=name,
        compiler_params=_cparams(dimension_semantics=("parallel", "parallel")),
    )(a, b)


def _cast_layer(w, layer, *, after=(), name):
    _, R, C = w.shape
    tr = max(t for t in range(16, R + 1, 16) if R % t == 0 and t * C <= 512 * 1024)

    def body(w_ref, *rest):
        o_ref = rest[-1]
        o_ref[...] = w_ref[...].astype(o_ref.dtype)

    return pl.pallas_call(
        body, grid=(R // tr,),
        in_specs=[pl.BlockSpec((None, tr, C), lambda i: (layer, i, 0))] + [pl.BlockSpec(memory_space=pl.ANY)] * len(after),
        out_specs=pl.BlockSpec((tr, C), lambda i: (i, 0)), out_shape=jax.ShapeDtypeStruct((R, C), _MXU_DTYPE), name=name,
        compiler_params=_cparams(dimension_semantics=("parallel",)),
    )(w, *after)


def _mm_w_slabs(a, w, *, tm, out_dtype=_F32, after=(), name):
    M, K = a.shape
    assert w.shape == (N_DEV, SHARD_COLS, K) and M % tm == 0

    def body(a_ref, w_ref, *rest):
        o_ref = rest[-1]
        a_blk = a_ref[...]
        for s in range(2):
            o_ref[:, s * SHARD_COLS:(s + 1) * SHARD_COLS] = lax.dot_general(
                a_blk, w_ref[s], _NT, preferred_element_type=_F32).astype(o_ref.dtype)

    return pl.pallas_call(
        body, grid=(M // tm, N_DEV // 2),
        in_specs=[pl.BlockSpec((tm, K), lambda i, p: (i, 0)), pl.BlockSpec((2, SHARD_COLS, K), lambda i, p: (p, 0, 0))]
        + [pl.BlockSpec(memory_space=pl.ANY)] * len(after),
        out_specs=pl.BlockSpec((tm, _PAIR), lambda i, p: (i, p)),
        out_shape=jax.ShapeDtypeStruct((M, N_DEV * SHARD_COLS), out_dtype), name=name,
        compiler_params=_cparams(dimension_semantics=("parallel", "parallel")),
    )(a, w, *after)


def _mm_nt_w_slabs(a, w, *, tm, tn, add=None, add_scale=1.0, after=(), name):
    M = a.shape[0]
    N = w.shape[2]
    assert a.shape[1] == N_DEV * SHARD_COLS and w.shape[:2] == (N_DEV, SHARD_COLS) and M % tm == 0 and N % tn == 0

    def body(a_ref, w_ref, *rest):
        o_ref = rest[-1]
        acc = add_scale * rest[0][...] if add is not None else None
        for j in range(N_DEV):
            t = jnp.dot(a_ref[:, j * SHARD_COLS:(j + 1) * SHARD_COLS], w_ref[j], preferred_element_type=_F32)
            acc = t if acc is None else acc + t
        o_ref[...] = acc

    o_spec = pl.BlockSpec((tm, tn), lambda i, j: (i, j))
    return pl.pallas_call(
        body, grid=(M // tm, N // tn),
        in_specs=[pl.BlockSpec((tm, N_DEV * SHARD_COLS), lambda i, j: (i, 0)),
                  pl.BlockSpec((N_DEV, SHARD_COLS, tn), lambda i, j: (0, 0, j))]
        + ([o_spec] if add is not None else []) + [pl.BlockSpec(memory_space=pl.ANY)] * len(after),
        out_specs=o_spec, out_shape=jax.ShapeDtypeStruct((M, N), _F32), name=name,
        compiler_params=_cparams(dimension_semantics=("parallel", "parallel")),
    )(a, w, *((add,) if add is not None else ()), *after)


def _ln_fwd(x, y, g, b, *, name):
    tm = 256

    def body(x_ref, y_ref, g_ref, b_ref, z_ref, o_ref, ob_ref):
        z = ALPHA * x_ref[...] + y_ref[...]
        mu = jnp.mean(z, axis=-1, keepdims=True)
        zc = z - mu
        var = jnp.mean(zc * zc, axis=-1, keepdims=True)
        o = zc * lax.rsqrt(var + LN_EPS) * g_ref[...] + b_ref[...]
        z_ref[...] = z
        o_ref[...] = o
        ob_ref[...] = o.astype(ob_ref.dtype)

    row = pl.BlockSpec((tm, D_MODEL), lambda i: (i, 0))
    vec = pl.BlockSpec((1, D_MODEL), lambda i: (0, 0))
    return pl.pallas_call(
        body, grid=(SEQ // tm,), in_specs=[row, row, vec, vec], out_specs=[row, row, row],
        out_shape=[jax.ShapeDtypeStruct((SEQ, D_MODEL), _F32), jax.ShapeDtypeStruct((SEQ, D_MODEL), _F32),
                   jax.ShapeDtypeStruct((SEQ, D_MODEL), _MXU_DTYPE)],
        name=name, compiler_params=_cparams(dimension_semantics=("parallel",)),
    )(x, y, g.reshape(1, D_MODEL), b.reshape(1, D_MODEL))


def _ln_bwd(z, d_a, d_res, g, *, name):
    tm = 256

    def body(*refs):
        if d_res is None:
            z_ref, da_ref, g_ref, dz_ref, dzb_ref, dg_ref, db_ref = refs
        else:
            z_ref, da_ref, dr_ref, g_ref, dz_ref, dzb_ref, dg_ref, db_ref = refs

        @pl.when(pl.program_id(0) == 0)
        def _():
            dg_ref[...] = jnp.zeros_like(dg_ref)
            db_ref[...] = jnp.zeros_like(db_ref)

        dout = da_ref[...]
        if d_res is not None:
            dout = dout + ALPHA * dr_ref[...]
        z = z_ref[...]
        mu = jnp.mean(z, axis=-1, keepdims=True)
        zc = z - mu
        var = jnp.mean(zc * zc, axis=-1, keepdims=True)
        rstd = lax.rsqrt(var + LN_EPS)
        xh = zc * rstd
        dxh = dout * g_ref[...]
        m1 = jnp.mean(dxh, axis=-1, keepdims=True)
        m2 = jnp.mean(dxh * xh, axis=-1, keepdims=True)
        dz = rstd * (dxh - m1 - xh * m2)
        dz_ref[...] = dz
        dzb_ref[...] = dz.astype(dzb_ref.dtype)
        dg_ref[0:1, :] += jnp.sum(dout * xh, axis=0, keepdims=True)
        db_ref[0:1, :] += jnp.sum(dout, axis=0, keepdims=True)

    row = pl.BlockSpec((tm, D_MODEL), lambda i: (i, 0))
    vec = pl.BlockSpec((1, D_MODEL), lambda i: (0, 0))
    acc = pl.BlockSpec((8, D_MODEL), lambda i: (0, 0))
    ins = [z, d_a] + ([d_res] if d_res is not None else []) + [g.reshape(1, D_MODEL)]
    in_specs = [row, row] + ([row] if d_res is not None else []) + [vec]
    dz, dzb, dg, db = pl.pallas_call(
        body, grid=(SEQ // tm,), in_specs=in_specs, out_specs=[row, row, acc, acc],
        out_shape=[jax.ShapeDtypeStruct((SEQ, D_MODEL), _F32), jax.ShapeDtypeStruct((SEQ, D_MODEL), _MXU_DTYPE),
                   jax.ShapeDtypeStruct((8, D_MODEL), _F32), jax.ShapeDtypeStruct((8, D_MODEL), _F32)],
        name=name, compiler_params=_cparams(dimension_semantics=("arbitrary",)),
    )(*ins)
    return dz, dzb, dg[0], db[0]


def _loss_head(y, target, *, name):
    tm = 256

    def body(y_ref, t_ref, d_ref, l_ref):
        e = y_ref[...] - t_ref[...]
        d_ref[...] = e * (1.0 / D_MODEL)

        @pl.when(pl.program_id(0) == 0)
        def _():
            l_ref[...] = jnp.zeros_like(l_ref)

        l_ref[...] += (0.5 / D_MODEL) * jnp.sum(e * e)

    row = pl.BlockSpec((tm, D_MODEL), lambda i: (i, 0))
    d, l = pl.pallas_call(
        body, grid=(SEQ // tm,), in_specs=[row, row], out_specs=[row, pl.BlockSpec((8, 128), lambda i: (0, 0))],
        out_shape=[jax.ShapeDtypeStruct((SEQ, D_MODEL), _F32), jax.ShapeDtypeStruct((8, 128), _F32)],
        name=name, compiler_params=_cparams(dimension_semantics=("arbitrary",)),
    )(y, target)
    return l[0, 0], d


_CONV_TN = 256


def _shift_down(v, k, rows):
    return jnp.where(rows >= k, pltpu.roll(v, k, axis=0), 0.0)


def _shift_up(v, k, rows):
    return jnp.where(rows < SEQ - k, pltpu.roll(v, SEQ - k, axis=0), 0.0)


def _conv_gate_fwd(g, u, conv_w, conv_b, *, name):
    def body(g_ref, u_ref, w_ref, b_ref, h_ref):
        gv = g_ref[...].astype(_F32)
        rows = lax.broadcasted_iota(jnp.int32, gv.shape, 0)
        w = w_ref[...]
        gc = b_ref[...] + w[2:3, :] * gv + w[1:2, :] * _shift_down(gv, 1, rows) + w[0:1, :] * _shift_down(gv, 2, rows)
        h_ref[...] = (gc * _sigmoid(gc) * u_ref[...].astype(_F32)).astype(h_ref.dtype)

    col = pl.BlockSpec((SEQ, _CONV_TN), lambda j: (0, j))
    return pl.pallas_call(
        body, grid=(D_FF // _CONV_TN,),
        in_specs=[col, col, pl.BlockSpec((3, _CONV_TN), lambda j: (0, j)), pl.BlockSpec((1, _CONV_TN), lambda j: (0, j))],
        out_specs=col, out_shape=jax.ShapeDtypeStruct((SEQ, D_FF), _MXU_DTYPE), name=name,
        compiler_params=_cparams(dimension_semantics=("parallel",)),
    )(g, u, conv_w, conv_b.reshape(1, D_FF))


def _conv_gate_bwd(dh, g, u, conv_w, conv_b, *, name):
    def body(dh_ref, g_ref, u_ref, w_ref, b_ref, dg_ref, du_ref, dw_ref, db_ref):
        gv = g_ref[...].astype(_F32)
        rows = lax.broadcasted_iota(jnp.int32, gv.shape, 0)
        w = w_ref[...]
        g1 = _shift_down(gv, 1, rows)
        g2 = _shift_down(gv, 2, rows)
        gc = b_ref[...] + w[2:3, :] * gv + w[1:2, :] * g1 + w[0:1, :] * g2
        sg = _sigmoid(gc)
        dh = dh_ref[...].astype(_F32)
        du_ref[...] = (dh * (gc * sg)).astype(du_ref.dtype)
        dgc = dh * u_ref[...].astype(_F32) * (sg * (1.0 + gc * (1.0 - sg)))
        dg = w[2:3, :] * dgc + w[1:2, :] * _shift_up(dgc, 1, rows) + w[0:1, :] * _shift_up(dgc, 2, rows)
        dg_ref[...] = dg.astype(dg_ref.dtype)
        dw_ref[0:1, :] = jnp.sum(dgc * g2, axis=0, keepdims=True)
        dw_ref[1:2, :] = jnp.sum(dgc * g1, axis=0, keepdims=True)
        dw_ref[2:3, :] = jnp.sum(dgc * gv, axis=0, keepdims=True)
        db_ref[...] = jnp.sum(dgc, axis=0, keepdims=True)

    col = pl.BlockSpec((SEQ, _CONV_TN), lambda j: (0, j))
    w3 = pl.BlockSpec((3, _CONV_TN), lambda j: (0, j))
    w1 = pl.BlockSpec((1, _CONV_TN), lambda j: (0, j))
    dg, du, dw, db = pl.pallas_call(
        body, grid=(D_FF // _CONV_TN,), in_specs=[col, col, col, w3, w1], out_specs=[col, col, w3, w1],
        out_shape=[jax.ShapeDtypeStruct((SEQ, D_FF), _MXU_DTYPE), jax.ShapeDtypeStruct((SEQ, D_FF), _MXU_DTYPE),
                   jax.ShapeDtypeStruct((3, D_FF), _F32), jax.ShapeDtypeStruct((1, D_FF), _F32)],
        name=name, compiler_params=_cparams(dimension_semantics=("parallel",)),
    )(dh, g, u, conv_w, conv_b.reshape(1, D_FF))
    return dg, du, dw, db[0]


def _rope_tables():
    half = ROPE_DIM // 2
    inv = ROPE_THETA ** (-jnp.arange(0, ROPE_DIM, 2, dtype=_F32) / ROPE_DIM)
    ang = jnp.arange(SEQ, dtype=_F32)[:, None] * inv[None, :]
    cos, sin = jnp.cos(ang), jnp.sin(ang)
    rest = HEAD_DIM - ROPE_DIM
    c = jnp.concatenate([cos, cos, jnp.ones((SEQ, rest), _F32)], axis=1)
    s1 = jnp.concatenate([-sin, jnp.zeros((SEQ, HEAD_DIM - half), _F32)], axis=1)
    s2 = jnp.concatenate([jnp.zeros((SEQ, half), _F32), sin, jnp.zeros((SEQ, rest), _F32)], axis=1)
    return c, s1, s2


def _rope_apply(x, c, s1, s2):
    return x * c + pltpu.roll(x, HEAD_DIM - ROPE_DIM // 2, axis=1) * s1 + pltpu.roll(x, ROPE_DIM // 2, axis=1) * s2


def _rope_transpose(d, c, s1, s2):
    half = ROPE_DIM // 2
    return d * c + pltpu.roll(d * s1, half, axis=1) + pltpu.roll(d * s2, HEAD_DIM - half, axis=1)


_NT = (((1,), (1,)), ((), ()))
_TN = (((0,), (0,)), ((), ()))
_SCALE = HEAD_DIM ** -0.5


def _band_scores(q, k2, n, lag_off):
    s = lax.dot_general(q, k2, _NT, preferred_element_type=_F32) * _SCALE
    row = lax.broadcasted_iota(jnp.int32, (BLOCK, 2 * BLOCK), 0)
    col = lax.broadcasted_iota(jnp.int32, (BLOCK, 2 * BLOCK), 1)
    front = (col >= row + lag_off) & (col < BLOCK) & (n > 0)
    own = (col >= BLOCK) & (col <= row + BLOCK)
    return jnp.where(front | own, s, _NEG)


_BAND_STEPS = SEQ // BLOCK


def _rows(start, d):
    if d == 1:
        return pl.ds(pl.multiple_of(start, BLOCK), BLOCK)
    return pl.ds(start, BLOCK, stride=d)


def _band_block(it, d):
    r, n = it % d, it // d
    span = BLOCK * d
    return n, _rows(r + n * span, d), _rows(r + jnp.maximum(n - 1, 0) * span, d)


def _band_fwd(proj, tabs, *, kv_heads, q_per_kv, q0, k0, v0, dilations, lag_off, sink, name):
    heads = kv_heads * q_per_kv

    def body(*refs):
        q_refs = refs[:q_per_kv]
        k_ref, v_ref, c_ref, s1_ref, s2_ref = refs[q_per_kv:q_per_kv + 5]
        rest = refs[q_per_kv + 5:]
        if sink is not None:
            sk_ref, rest = rest[0], rest[1:]
        o_ref, lse_ref, qs, ks, m_s, l_s, acc_s = rest
        c, s1, s2 = c_ref[...], s1_ref[...], s2_ref[...]
        ks[...] = _rope_apply(k_ref[...], c, s1, s2)
        for i in range(q_per_kv):
            qs[...] = _rope_apply(q_refs[i][...], c, s1, s2)
            for pi, d in enumerate(dilations):
                def step(it, carry, d=d, first=(pi == 0)):
                    n, cur, prev = _band_block(it, d)
                    q = qs[cur, :].astype(_MXU_DTYPE)
                    k2 = jnp.concatenate([ks[prev, :], ks[cur, :]], axis=0).astype(_MXU_DTYPE)
                    v2 = jnp.concatenate([v_ref[prev, :], v_ref[cur, :]], axis=0).astype(_MXU_DTYPE)
                    s = _band_scores(q, k2, n, lag_off)
                    m_b = jnp.max(s, axis=1, keepdims=True)
                    m_new = m_b if first else jnp.maximum(m_b, m_s[cur, :][:, 0:1])
                    p = jnp.exp(s - m_new)
                    l_new = jnp.sum(p, axis=1, keepdims=True)
                    acc = jnp.dot(p.astype(_MXU_DTYPE), v2, preferred_element_type=_F32)
                    if not first:
                        a = jnp.exp(m_s[cur, :][:, 0:1] - m_new)
                        l_new = l_new + a * l_s[cur, :][:, 0:1]
                        acc = acc + a * acc_s[cur, :]
                    m_s[cur, :] = jnp.broadcast_to(m_new, (BLOCK, HEAD_DIM))
                    l_s[cur, :] = jnp.broadcast_to(l_new, (BLOCK, HEAD_DIM))
                    acc_s[cur, :] = acc
                    return carry

                lax.fori_loop(0, _BAND_STEPS, step, 0, unroll=4)
            m, den = m_s[...], l_s[...]
            if sink is not None:
                sk = sk_ref[i]
                m_f = jnp.maximum(m, sk)
                a = jnp.exp(m - m_f)
                den = den * a + jnp.exp(sk - m_f)
                o = acc_s[...] * a / den
                m = m_f
            else:
                o = acc_s[...] / den
            o_ref[:, i * HEAD_DIM:(i + 1) * HEAD_DIM] = o
            lse_ref[:, i * HEAD_DIM:(i + 1) * HEAD_DIM] = m + jnp.log(den)

    col = (SEQ, HEAD_DIM)
    in_specs = [pl.BlockSpec(col, functools.partial(lambda g, i: (0, A_COLS + q0 + g * q_per_kv + i), i=i)) for i in range(q_per_kv)]
    in_specs += [pl.BlockSpec(col, lambda g: (0, A_COLS + k0 + g)), pl.BlockSpec(col, lambda g: (0, A_COLS + v0 + g))]
    in_specs += [pl.BlockSpec(col, lambda g: (0, 0))] * 3
    args = [proj] * (q_per_kv + 2) + list(tabs)
    if sink is not None:
        in_specs.append(pl.BlockSpec((q_per_kv, 1, HEAD_DIM), lambda g: (g, 0, 0)))
        args.append(jnp.broadcast_to(sink.reshape(heads, 1, 1), (heads, 1, HEAD_DIM)))
    o_spec = pl.BlockSpec((SEQ, q_per_kv * HEAD_DIM), lambda g: (0, g))
    shape = jax.ShapeDtypeStruct((SEQ, heads * HEAD_DIM), _F32)
    return pl.pallas_call(
        body, grid=(kv_heads,), in_specs=in_specs, out_specs=[o_spec, o_spec], out_shape=[shape, shape],
        scratch_shapes=[pltpu.VMEM(col, _F32)] * 5, name=name,
        compiler_params=_cparams(dimension_semantics=("parallel",)),
    )(*args)


def _band_bwd(proj, tabs, dmixed, o, lse, *, kv_heads, q_per_kv, q0, k0, v0, do0, dilations, lag_off, sink, after=(), name):
    heads = kv_heads * q_per_kv

    def body(*refs):
        q_refs = refs[:q_per_kv]
        k_ref, v_ref, c_ref, s1_ref, s2_ref = refs[q_per_kv:q_per_kv + 5]
        do_refs = refs[q_per_kv + 5:2 * q_per_kv + 5]
        o_ref, lse_ref = refs[2 * q_per_kv + 5:2 * q_per_kv + 7]
        rest = refs[2 * q_per_kv + 7:]
        if sink is not None:
            sk_ref, rest = rest[0], rest[1:]
            dq_ref, dk_ref, dv_ref, dsk_ref, qs, ks, dq_s, dk_s, dv_s = rest[len(after):]
        else:
            dq_ref, dk_ref, dv_ref, qs, ks, dq_s, dk_s, dv_s = rest[len(after):]
        c, s1, s2 = c_ref[...], s1_ref[...], s2_ref[...]
        ks[...] = _rope_apply(k_ref[...], c, s1, s2)
        dk_s[...] = jnp.zeros_like(dk_s)
        dv_s[...] = jnp.zeros_like(dv_s)
        for i in range(q_per_kv):
            hs = slice(i * HEAD_DIM, (i + 1) * HEAD_DIM)
            qs[...] = _rope_apply(q_refs[i][...], c, s1, s2)
            dq_s[...] = jnp.zeros_like(dq_s)
            do_ref = do_refs[i]
            for d in dilations:
                def step(it, carry, d=d, do_ref=do_ref, hs=hs):
                    n, cur, prev = _band_block(it, d)
                    q = qs[cur, :].astype(_MXU_DTYPE)
                    k2 = jnp.concatenate([ks[prev, :], ks[cur, :]], axis=0).astype(_MXU_DTYPE)
                    v2 = jnp.concatenate([v_ref[prev, :], v_ref[cur, :]], axis=0).astype(_MXU_DTYPE)
                    do = do_ref[cur, :]
                    delta = jnp.sum(do * o_ref[cur, hs], axis=1, keepdims=True)
                    lse_c = lse_ref[cur, hs][:, 0:1]
                    p = jnp.exp(_band_scores(q, k2, n, lag_off) - lse_c)
                    dob = do.astype(_MXU_DTYPE)
                    ds = (p * (lax.dot_general(dob, v2, _NT, preferred_element_type=_F32) - delta) * _SCALE).astype(_MXU_DTYPE)
                    dq_s[cur, :] += jnp.dot(ds, k2, preferred_element_type=_F32)
                    dk2 = lax.dot_general(ds, q, _TN, preferred_element_type=_F32)
                    dv2 = lax.dot_general(p.astype(_MXU_DTYPE), dob, _TN, preferred_element_type=_F32)
                    dk_s[prev, :] += dk2[:BLOCK]
                    dv_s[prev, :] += dv2[:BLOCK]
                    dk_s[cur, :] += dk2[BLOCK:]
                    dv_s[cur, :] += dv2[BLOCK:]
                    return carry

                lax.fori_loop(0, _BAND_STEPS, step, 0, unroll=4)
            dq_ref[:, hs] = _rope_transpose(dq_s[...], c, s1, s2).astype(dq_ref.dtype)
            if sink is not None:
                delta = jnp.sum(do_ref[...] * o_ref[:, hs], axis=1, keepdims=True)
                w_sink = jnp.exp(sk_ref[i] - lse_ref[:, hs])
                dsk_ref[i] = jnp.broadcast_to(jnp.sum(-delta * w_sink[:, 0:1]), (8, HEAD_DIM))
        dk_ref[...] = _rope_transpose(dk_s[...], c, s1, s2).astype(dk_ref.dtype)
        dv_ref[...] = dv_s[...].astype(dv_ref.dtype)

    col = (SEQ, HEAD_DIM)
    in_specs = [pl.BlockSpec(col, functools.partial(lambda g, i: (0, A_COLS + q0 + g * q_per_kv + i), i=i)) for i in range(q_per_kv)]
    in_specs += [pl.BlockSpec(col, lambda g: (0, A_COLS + k0 + g)), pl.BlockSpec(col, lambda g: (0, A_COLS + v0 + g))]
    in_specs += [pl.BlockSpec(col, lambda g: (0, 0))] * 3
    in_specs += [pl.BlockSpec(col, functools.partial(lambda g, i: (0, do0 + g * q_per_kv + i), i=i)) for i in range(q_per_kv)]
    wide = pl.BlockSpec((SEQ, q_per_kv * HEAD_DIM), lambda g: (0, g))
    in_specs += [wide, wide]
    args = [proj] * (q_per_kv + 2) + list(tabs) + [dmixed] * q_per_kv + [o, lse]
    out_specs = [wide, pl.BlockSpec(col, lambda g: (0, g)), pl.BlockSpec(col, lambda g: (0, g))]
    out_shape = [jax.ShapeDtypeStruct((SEQ, heads * HEAD_DIM), _MXU_DTYPE), jax.ShapeDtypeStruct((SEQ, kv_heads * HEAD_DIM), _MXU_DTYPE),
                 jax.ShapeDtypeStruct((SEQ, kv_heads * HEAD_DIM), _MXU_DTYPE)]
    if sink is not None:
        in_specs.append(pl.BlockSpec((q_per_kv, 1, HEAD_DIM), lambda g: (g, 0, 0)))
        args.append(jnp.broadcast_to(sink.reshape(heads, 1, 1), (heads, 1, HEAD_DIM)))
        out_specs.append(pl.BlockSpec((q_per_kv, 8, HEAD_DIM), lambda g: (g, 0, 0)))
        out_shape.append(jax.ShapeDtypeStruct((heads, 8, HEAD_DIM), _F32))
    in_specs += [pl.BlockSpec(memory_space=pl.ANY)] * len(after)
    args += list(after)
    res = pl.pallas_call(
        body, grid=(kv_heads,), in_specs=in_specs, out_specs=out_specs, out_shape=out_shape,
        scratch_shapes=[pltpu.VMEM(col, _F32)] * 5, name=name,
        compiler_params=_cparams(dimension_semantics=("parallel",)),
    )(*args)
    if sink is not None:
        return res[0], res[1], res[2], res[3][:, 0, 0]
    return res


_DILATED = dict(kv_heads=B_HEADS, q_per_kv=1, q0=QB0, k0=KB0, v0=VB0, dilations=DILATIONS, lag_off=0, sink=None)
_SWA = dict(kv_heads=C_KV_HEADS, q_per_kv=C_HEADS // C_KV_HEADS, q0=QC0, k0=KC0, v0=VC0, dilations=(1,), lag_off=1)


_HG_TILE = 128
_HG_CHUNKS = _HG_TILE // A_CHUNK
_HG_TILES = SEQ // _HG_TILE
_HI = lax.Precision.HIGHEST


def _chunk_tri():
    i = np.arange(_HG_TILE)
    return jnp.asarray(((i[:, None] // A_CHUNK == i[None, :] // A_CHUNK) & (i[None, :] <= i[:, None])).astype(np.float32))


def _layer_lb(lb_ref, layer):
    if layer == 0:
        return jnp.zeros((1, HEAD_DIM), _F32)
    lg = lb_ref[...]
    m = jnp.max(lg, axis=0, keepdims=True)
    e = jnp.exp(lg - m)
    return e[1:2, :] / jnp.sum(e, axis=0, keepdims=True)


def _hgrn_gates(q, fr, lb):
    sgq = _sigmoid(q)
    sg = _sigmoid(fr)
    f = lb + (1.0 - lb) * sg
    return sgq, q * sgq, sg, f, 1.0 - f


def _hgrn_fwd(proj, lb_logits, norm_w, layer, *, name):
    tri = _chunk_tri()

    def body(q_ref, f_ref, i_ref, g_ref, lb_ref, nw_ref, tri_ref, o_ref, raw_ref, st_ref, state):
        @pl.when(pl.program_id(1) == 0)
        def _():
            state[...] = jnp.zeros_like(state)

        lb = _layer_lb(lb_ref, layer)
        _, qs, _, f, k = _hgrn_gates(q_ref[...], f_ref[...], lb)
        v = i_ref[...]
        b = jnp.dot(tri_ref[...], jnp.log(f), precision=_HI, preferred_element_type=_F32)
        eb = jnp.exp(b)
        ridx = lax.broadcasted_iota(jnp.int32, (A_CHUNK, HEAD_DIM), 0)
        outs = []
        for c in range(_HG_CHUNKS):
            sl = slice(c * A_CHUNK, (c + 1) * A_CHUNK)
            bc, qc, kc, vc = b[sl], qs[sl], k[sl], v[sl]
            bl = bc[A_CHUNK - 1:A_CHUNK]
            st = state[...]
            st_ref[0, c] = st
            o_c = lax.dot_general((qc * eb[sl]).astype(_MXU_DTYPE), st.astype(_MXU_DTYPE), _NT, preferred_element_type=_F32)
            rows = []
            for i in range(A_CHUNK):
                di = jnp.exp(jnp.where(ridx <= i, bc[i:i + 1] - bc, _NEG))
                a = jnp.sum(qc[i:i + 1] * kc * di, axis=1, keepdims=True)
                rows.append(jnp.sum(a * vc, axis=0, keepdims=True))
            outs.append(o_c + jnp.concatenate(rows, axis=0))
            kt = (kc * jnp.exp(bl - bc)).astype(_MXU_DTYPE)
            state[...] = st * jnp.exp(bl) + lax.dot_general(vc.astype(_MXU_DTYPE), kt, _TN, preferred_element_type=_F32)
        o = jnp.concatenate(outs, axis=0)
        raw_ref[...] = o
        r = lax.rsqrt(jnp.mean(o * o, axis=-1, keepdims=True) + LN_EPS)
        g = g_ref[...]
        o_ref[...] = o * r * nw_ref[...] * (g * _sigmoid(g))

    blk = (_HG_TILE, HEAD_DIM)

    def col(base):
        return pl.BlockSpec(blk, lambda h, t: (t, base + h))

    o_spec = pl.BlockSpec(blk, lambda h, t: (t, h))
    o_shape = jax.ShapeDtypeStruct((SEQ, A_HEADS * HEAD_DIM), _F32)
    return pl.pallas_call(
        body, grid=(A_HEADS, _HG_TILES),
        in_specs=[col(0), col(4), col(8), col(12), pl.BlockSpec((DEPTH, HEAD_DIM), lambda h, t: (0, h)),
                  pl.BlockSpec((1, HEAD_DIM), lambda h, t: (0, 0)), pl.BlockSpec(blk, lambda h, t: (0, 0))],
        out_specs=[o_spec, o_spec, pl.BlockSpec((1, _HG_CHUNKS, HEAD_DIM, HEAD_DIM), lambda h, t: (h, t, 0, 0))],
        out_shape=[o_shape, o_shape, jax.ShapeDtypeStruct((A_HEADS, SEQ // A_CHUNK, HEAD_DIM, HEAD_DIM), _F32)],
        scratch_shapes=[pltpu.VMEM((HEAD_DIM, HEAD_DIM), _F32)], name=name,
        compiler_params=_cparams(dimension_semantics=("parallel", "arbitrary")),
    )(proj, proj, proj, proj, lb_logits, norm_w.reshape(1, HEAD_DIM), tri)


def _hgrn_bwd(proj, lb_logits, norm_w, raw, states, dmixed, layer, *, name):
    tri = _chunk_tri()
    triu = tri.T

    def body(q_ref, f_ref, i_ref, g_ref, lb_ref, nw_ref, tri_ref, triu_ref, raw_ref, do_ref, st_ref,
             dq_ref, df_ref, di_ref, dg_ref, dnw_ref, dlb_ref, dstate):
        @pl.when(pl.program_id(1) == 0)
        def _():
            dstate[...] = jnp.zeros_like(dstate)
            dlb_ref[...] = jnp.zeros_like(dlb_ref)

        @pl.when((pl.program_id(0) == 0) & (pl.program_id(1) == 0))
        def _():
            dnw_ref[...] = jnp.zeros_like(dnw_ref)

        lb = _layer_lb(lb_ref, layer)
        q = q_ref[...]
        sgq, qs, sg, f, k = _hgrn_gates(q, f_ref[...], lb)
        v = i_ref[...]
        b = jnp.dot(tri_ref[...], jnp.log(f), precision=_HI, preferred_element_type=_F32)
        eb = jnp.exp(b)
        g = g_ref[...]
        nw = nw_ref[...]
        o = raw_ref[...]
        dout = do_ref[...]
        sgg = _sigmoid(g)
        r = lax.rsqrt(jnp.mean(o * o, axis=-1, keepdims=True) + LN_EPS)
        dg_ref[...] = (dout * (o * r * nw) * (sgg * (1.0 + g * (1.0 - sgg)))).astype(dg_ref.dtype)
        don = dout * (g * sgg)
        dnw_ref[0:1, :] += jnp.sum(don * o * r, axis=0, keepdims=True)
        dy = don * nw
        do_raw = r * dy - o * (r * r * r) * jnp.mean(o * dy, axis=-1, keepdims=True)

        ridx = lax.broadcasted_iota(jnp.int32, (A_CHUNK, HEAD_DIM), 0)
        dqs_t, dk_t, db_t, dv_t = [None] * _HG_CHUNKS, [None] * _HG_CHUNKS, [None] * _HG_CHUNKS, [None] * _HG_CHUNKS
        for c in reversed(range(_HG_CHUNKS)):
            sl = slice(c * A_CHUNK, (c + 1) * A_CHUNK)
            bc, qc, kc, vc, doc = b[sl], qs[sl], k[sl], v[sl], do_raw[sl]
            bl = bc[A_CHUNK - 1:A_CHUNK]
            ebc = eb[sl]
            ebl = jnp.exp(bl - bc)
            lam = jnp.exp(bl)
            qt = qc * ebc
            kt = kc * ebl
            dst = dstate[...]
            stp = st_ref[0, c]
            dob = doc.astype(_MXU_DTYPE)
            dstb = dst.astype(_MXU_DTYPE)
            dqt = jnp.dot(dob, stp.astype(_MXU_DTYPE), preferred_element_type=_F32)
            dkt = jnp.dot(vc.astype(_MXU_DTYPE), dstb, preferred_element_type=_F32)
            dv = lax.dot_general(kt.astype(_MXU_DTYPE), dstb, _NT, preferred_element_type=_F32)
            dlam = jnp.sum(stp * dst, axis=0, keepdims=True)
            dstate[...] = dst * lam + lax.dot_general(dob, qt.astype(_MXU_DTYPE), _TN, preferred_element_type=_F32)
            dqs_rows = []
            dk_in = jnp.zeros((A_CHUNK, HEAD_DIM), _F32)
            for i in range(A_CHUNK):
                di = jnp.exp(jnp.where(ridx <= i, bc[i:i + 1] - bc, _NEG))
                qi = qc[i:i + 1]
                doi = doc[i:i + 1]
                w = kc * di
                a = jnp.sum(qi * w, axis=1, keepdims=True)
                dv = dv + a * doi
                da = jnp.sum(doi * vc, axis=1, keepdims=True)
                dqs_rows.append(jnp.sum(da * w, axis=0, keepdims=True))
                dk_in = dk_in + da * (qi * di)
            dqs_in = jnp.concatenate(dqs_rows, axis=0)
            dbl = jnp.sum(dkt * kt, axis=0, keepdims=True) + dlam * lam
            db = qc * dqs_in - kc * dk_in + dqt * qt - dkt * kt
            db_t[c] = db + jnp.where(ridx == A_CHUNK - 1, dbl, 0.0)
            dqs_t[c] = dqs_in + dqt * ebc
            dk_t[c] = dk_in + dkt * ebl
            dv_t[c] = dv
        dqs = jnp.concatenate(dqs_t, axis=0)
        dk = jnp.concatenate(dk_t, axis=0)
        db = jnp.concatenate(db_t, axis=0)
        di_ref[...] = jnp.concatenate(dv_t, axis=0).astype(di_ref.dtype)
        dlogf = jnp.dot(triu_ref[...], db, precision=_HI, preferred_element_type=_F32)
        df = dlogf / f - dk
        df_ref[...] = (df * (1.0 - lb) * sg * (1.0 - sg)).astype(df_ref.dtype)
        dlb_ref[0, 0:1, :] += jnp.sum(df * (1.0 - sg), axis=0, keepdims=True)
        dq_ref[...] = (dqs * (sgq * (1.0 + q * (1.0 - sgq)))).astype(dq_ref.dtype)

    blk = (_HG_TILE, HEAD_DIM)
    last = _HG_TILES - 1

    def col(base):
        return pl.BlockSpec(blk, lambda h, t: (last - t, base + h))

    tri_spec = pl.BlockSpec(blk, lambda h, t: (0, 0))
    acc_spec = pl.BlockSpec((1, 8, HEAD_DIM), lambda h, t: (h, 0, 0))
    acc_shape = jax.ShapeDtypeStruct((A_HEADS, 8, HEAD_DIM), _F32)
    dq, df, di, dg, dnw, dlb = pl.pallas_call(
        body, grid=(A_HEADS, _HG_TILES),
        in_specs=[col(0), col(4), col(8), col(12), pl.BlockSpec((DEPTH, HEAD_DIM), lambda h, t: (0, h)),
                  pl.BlockSpec((1, HEAD_DIM), lambda h, t: (0, 0)), tri_spec, tri_spec, col(0), col(0),
                  pl.BlockSpec((1, _HG_CHUNKS, HEAD_DIM, HEAD_DIM), lambda h, t: (h, last - t, 0, 0))],
        out_specs=[col(0), col(0), col(0), col(0), pl.BlockSpec((8, HEAD_DIM), lambda h, t: (0, 0)), acc_spec],
        out_shape=[jax.ShapeDtypeStruct((SEQ, A_HEADS * HEAD_DIM), _MXU_DTYPE)] * 4
        + [jax.ShapeDtypeStruct((8, HEAD_DIM), _F32), acc_shape],
        scratch_shapes=[pltpu.VMEM((HEAD_DIM, HEAD_DIM), _F32)], name=name,
        compiler_params=_cparams(dimension_semantics=("arbitrary", "arbitrary")),
    )(proj, proj, proj, proj, lb_logits, norm_w.reshape(1, HEAD_DIM), tri, triu, raw, dmixed, states)
    return dq, df, di, dg, dnw[0], dlb[:, 0, :].reshape(A_HEADS * HEAD_DIM)


N_CHIP = N_DEV // 2
_MESH_ID = pl.DeviceIdType.MESH


def _place():
    x, y, c = lax.axis_index("x"), lax.axis_index("y"), lax.axis_index("c")
    chips = [(1 - x, y), (x, 1 - y), (1 - x, 1 - y)]
    return x, y, c, 2 * x + y, chips


def _sibling_swap(arrays, *, name):
    n = len(arrays)

    def body(*refs):
        ins, outs = refs[:n], refs[n:2 * n]
        send_sems, recv_sems = refs[2 * n:]
        x, y, c, _, _ = _place()
        copies = [pltpu.make_async_remote_copy(
            src_ref=ins[a].at[:, 1 - c], dst_ref=outs[a], send_sem=send_sems.at[a], recv_sem=recv_sems.at[a],
            device_id=(x, y, 1 - c), device_id_type=_MESH_ID) for a in range(n)]
        for cp in copies:
            cp.start()
        for cp in copies:
            cp.wait()

    any_spec = pl.BlockSpec(memory_space=pl.ANY)
    return pl.pallas_call(
        body, in_specs=[any_spec] * n, out_specs=[any_spec] * n,
        out_shape=[jax.ShapeDtypeStruct((N_CHIP,) + a.shape[2:], a.dtype) for a in arrays],
        scratch_shapes=[pltpu.SemaphoreType.DMA((n,)), pltpu.SemaphoreType.DMA((n,))],
        name=name, compiler_params=pltpu.CompilerParams(has_side_effects=True),
    )(*arrays)


def _pair_add(mine, theirs, core, *, name):
    _, _, R, C = mine.shape
    tr = max(t for t in range(16, R + 1, 16) if R % t == 0 and t * C <= 512 * 1024)

    def body(core_ref, m_ref, t_ref, o_ref):
        del core_ref
        o_ref[...] = (m_ref[...].astype(_F32) + t_ref[...].astype(_F32)).astype(o_ref.dtype)

    grid_spec = pltpu.PrefetchScalarGridSpec(
        num_scalar_prefetch=1, grid=(N_CHIP, R // tr),
        in_specs=[pl.BlockSpec((None, None, tr, C), lambda q, i, core: (q, core[0], i, 0)),
                  pl.BlockSpec((None, tr, C), lambda q, i, core: (q, i, 0))],
        out_specs=pl.BlockSpec((None, tr, C), lambda q, i, core: (q, i, 0)))
    return pl.pallas_call(
        body, grid_spec=grid_spec, out_shape=jax.ShapeDtypeStruct((N_CHIP, R, C), mine.dtype), name=name,
        compiler_params=_cparams(dimension_semantics=("parallel", "parallel")),
    )(core.reshape(1), mine, theirs)


_HBM = pl.BlockSpec(memory_space=pltpu.HBM)
_SEM = pl.BlockSpec(memory_space=pltpu.SEMAPHORE)
_TOKEN = pl.BlockSpec(memory_space=pltpu.VMEM)
_DATAFLOW = pltpu.SideEffectType.DATAFLOW_SIDE_EFFECTING


def _hbm(a):
    return pltpu.HBM(a.shape, a.dtype)


def _token_shape():
    return jax.ShapeDtypeStruct((8, 128), _F32)


def _dev_slot(px, py, pc):
    return 4 * px + 2 * py + pc


def _gather_start(blocks, landings, *, name):
    n = len(blocks)

    def body(*refs):
        ins, lands = refs[:n], refs[n:2 * n]
        send_sems, d2d_sems, ici_sems = refs[2 * n:2 * n + 3]
        token = refs[-1]
        x, y, c, _, chips = _place()
        for a in range(n):
            dst = lands[a].at[_dev_slot(x, y, c)]
            pltpu.make_async_remote_copy(src_ref=ins[a], dst_ref=dst, send_sem=send_sems.at[4 * a], recv_sem=d2d_sems.at[a],
                                         device_id=(x, y, 1 - c), device_id_type=_MESH_ID).start()
            for j, chip in enumerate(chips):
                pltpu.make_async_remote_copy(src_ref=ins[a], dst_ref=dst, send_sem=send_sems.at[4 * a + 1 + j],
                                             recv_sem=ici_sems.at[3 * a + j], device_id=(*chip, c),
                                             device_id_type=_MESH_ID).start()
        token[...] = jnp.zeros_like(token)

    res = pl.pallas_call(
        body, name=name, in_specs=[_HBM] * (2 * n),
        out_shape=(pltpu.SemaphoreType.DMA((4 * n,)), pltpu.SemaphoreType.DMA((n,)), pltpu.SemaphoreType.DMA((3 * n,)),
                   *[_hbm(b) for b in blocks], *[_hbm(b) for b in landings], _token_shape()),
        out_specs=(_SEM, _SEM, _SEM, *[_HBM] * (2 * n), _TOKEN),
        input_output_aliases={i: 3 + i for i in range(2 * n)},
        compiler_params=pltpu.CompilerParams(has_side_effects=_DATAFLOW),
    )(*[pltpu.with_memory_space_constraint(b, pltpu.HBM) for b in blocks],
      *[pltpu.with_memory_space_constraint(b, pltpu.HBM) for b in landings])
    return res[0], res[1], res[2], list(res[3:3 + n]), list(res[3 + n:3 + 2 * n]), res[-1]


def _gather_forward(landings, ici_sems, first, after, *, name):
    n = len(landings)

    def body(*refs):
        lands = refs[:n]
        ici = refs[n]
        f_send, f_recv = refs[n + 2], refs[n + 3]
        token = refs[-1]
        x, y, c, _, chips = _place()
        for a in range(n):
            for j, chip in enumerate(chips):
                blk = lands[a].at[_dev_slot(*chip, c)]
                pltpu.make_async_remote_copy(src_ref=blk, dst_ref=blk, send_sem=f_send.at[3 * a + j],
                                             recv_sem=ici.at[3 * (first + a) + j], device_id=(*chip, c),
                                             device_id_type=_MESH_ID).wait_recv()
                pltpu.make_async_remote_copy(src_ref=blk, dst_ref=blk, send_sem=f_send.at[3 * a + j], recv_sem=f_recv.at[3 * a + j],
                                             device_id=(x, y, 1 - c), device_id_type=_MESH_ID).start()
        token[...] = jnp.zeros_like(token)

    res = pl.pallas_call(
        body, name=name, in_specs=[_HBM] * n + [_SEM, pl.BlockSpec(memory_space=pl.ANY)],
        out_shape=(pltpu.SemaphoreType.DMA((3 * n,)), pltpu.SemaphoreType.DMA((3 * n,)), *[_hbm(b) for b in landings], _token_shape()),
        out_specs=(_SEM, _SEM, *[_HBM] * n, _TOKEN),
        input_output_aliases={i: 2 + i for i in range(n)},
        compiler_params=pltpu.CompilerParams(has_side_effects=_DATAFLOW),
    )(*landings, ici_sems, after)
    return res[0], res[1], list(res[2:2 + n]), res[-1]


def _gather_wait(blocks, landings, send_sems, d2d_sems, first, f_send, f_recv, after, *, name):
    n = len(landings)

    def body(*refs):
        ins, lands = refs[:n], refs[n:2 * n]
        send, d2d, fs, fr = refs[2 * n:2 * n + 4]
        x, y, c, _, chips = _place()
        me = (x, y, c)
        for a in range(n):
            own = lands[a].at[_dev_slot(x, y, 1 - c)]
            g = first + a
            pltpu.make_async_remote_copy(src_ref=ins[a], dst_ref=own, send_sem=send.at[4 * g], recv_sem=d2d.at[g],
                                         device_id=me, device_id_type=_MESH_ID).wait_recv()
            for j, chip in enumerate(chips):
                blk = lands[a].at[_dev_slot(*chip, 1 - c)]
                pltpu.make_async_remote_copy(src_ref=blk, dst_ref=blk, send_sem=fs.at[3 * a + j], recv_sem=fr.at[3 * a + j],
                                             device_id=me, device_id_type=_MESH_ID).wait_recv()
            for k in range(4):
                pltpu.make_async_remote_copy(src_ref=ins[a], dst_ref=own, send_sem=send.at[4 * g + k], recv_sem=d2d.at[g],
                                             device_id=me, device_id_type=_MESH_ID).wait_send()
            for j in range(3):
                pltpu.make_async_remote_copy(src_ref=own, dst_ref=own, send_sem=fs.at[3 * a + j], recv_sem=fr.at[3 * a + j],
                                             device_id=me, device_id_type=_MESH_ID).wait_send()

    res = pl.pallas_call(
        body, name=name, in_specs=[_HBM] * (2 * n) + [_SEM] * 4 + [pl.BlockSpec(memory_space=pl.ANY)],
        out_shape=(*[_hbm(b) for b in blocks], *[_hbm(b) for b in landings]), out_specs=tuple([_HBM] * (2 * n)),
        input_output_aliases={i: i for i in range(2 * n)},
        compiler_params=pltpu.CompilerParams(has_side_effects=_DATAFLOW),
    )(*blocks, *landings, send_sems, d2d_sems, f_send, f_recv, after)
    return list(res[n:])


def _swap_start(mine, landings, *, name):
    n = len(mine)

    def body(*refs):
        ins, lands = refs[:n], refs[n:2 * n]
        send_sems, recv_sems = refs[2 * n:2 * n + 2]
        token = refs[-1]
        x, y, c, _, _ = _place()
        for a in range(n):
            pltpu.make_async_remote_copy(src_ref=ins[a].at[:, 1 - c], dst_ref=lands[a], send_sem=send_sems.at[a],
                                         recv_sem=recv_sems.at[a], device_id=(x, y, 1 - c), device_id_type=_MESH_ID).start()
        token[...] = jnp.zeros_like(token)

    res = pl.pallas_call(
        body, name=name, in_specs=[_HBM] * (2 * n),
        out_shape=(pltpu.SemaphoreType.DMA((n,)), pltpu.SemaphoreType.DMA((n,)),
                   *[_hbm(b) for b in mine], *[_hbm(b) for b in landings], _token_shape()),
        out_specs=(_SEM, _SEM, *[_HBM] * (2 * n), _TOKEN),
        input_output_aliases={i: 2 + i for i in range(2 * n)},
        compiler_params=pltpu.CompilerParams(has_side_effects=_DATAFLOW),
    )(*[pltpu.with_memory_space_constraint(b, pltpu.HBM) for b in mine],
      *[pltpu.with_memory_space_constraint(b, pltpu.HBM) for b in landings])
    return res[0], res[1], list(res[2:2 + n]), list(res[2 + n:2 + 2 * n]), res[-1]


def _swap_wait(mine, landings, send_sems, recv_sems, after, *, name):
    n = len(mine)

    def body(*refs):
        ins, lands = refs[:n], refs[n:2 * n]
        send, recv = refs[2 * n:2 * n + 2]
        x, y, c, _, _ = _place()
        for a in range(n):
            cp = pltpu.make_async_remote_copy(src_ref=ins[a].at[:, 1 - c], dst_ref=lands[a], send_sem=send.at[a],
                                              recv_sem=recv.at[a], device_id=(x, y, c), device_id_type=_MESH_ID)
            cp.wait_recv()
            cp.wait_send()

    res = pl.pallas_call(
        body, name=name, in_specs=[_HBM] * (2 * n) + [_SEM] * 2 + [pl.BlockSpec(memory_space=pl.ANY)],
        out_shape=(*[_hbm(b) for b in mine], *[_hbm(b) for b in landings]), out_specs=tuple([_HBM] * (2 * n)),
        input_output_aliases={i: i for i in range(2 * n)},
        compiler_params=pltpu.CompilerParams(has_side_effects=_DATAFLOW),
    )(*mine, *landings, send_sems, recv_sems, after)
    return list(res[:n]), list(res[n:])


def _chip_exchange_start(sums, landings, *, name):
    n = len(sums)

    def body(*refs):
        ins, lands = refs[:n], refs[n:2 * n]
        send_sems, recv_sems = refs[2 * n:2 * n + 2]
        token = refs[-1]
        _, _, c, p, chips = _place()
        for a in range(n):
            for j, (qx, qy) in enumerate(chips):
                pltpu.make_async_remote_copy(src_ref=ins[a].at[2 * qx + qy], dst_ref=lands[a].at[p], send_sem=send_sems.at[3 * a + j],
                                             recv_sem=recv_sems.at[3 * a + j], device_id=(qx, qy, c), device_id_type=_MESH_ID).start()
        token[...] = jnp.zeros_like(token)

    res = pl.pallas_call(
        body, name=name, in_specs=[_HBM] * (2 * n),
        out_shape=(pltpu.SemaphoreType.DMA((3 * n,)), pltpu.SemaphoreType.DMA((3 * n,)),
                   *[_hbm(b) for b in sums], *[_hbm(b) for b in landings], _token_shape()),
        out_specs=(_SEM, _SEM, *[_HBM] * (2 * n), _TOKEN),
        input_output_aliases={i: 2 + i for i in range(2 * n)},
        compiler_params=pltpu.CompilerParams(has_side_effects=_DATAFLOW),
    )(*[pltpu.with_memory_space_constraint(b, pltpu.HBM) for b in sums],
      *[pltpu.with_memory_space_constraint(b, pltpu.HBM) for b in landings])
    return res[0], res[1], list(res[2:2 + n]), list(res[2 + n:2 + 2 * n]), res[-1]


def _chip_exchange_wait(sums, landings, send_sems, recv_sems, after, *, name):
    n = len(sums)

    def body(*refs):
        ins, lands = refs[:n], refs[n:2 * n]
        send, recv = refs[2 * n:2 * n + 2]
        x, y, c, _, chips = _place()
        for a in range(n):
            for j, (qx, qy) in enumerate(chips):
                q = 2 * qx + qy
                cp = pltpu.make_async_remote_copy(src_ref=ins[a].at[q], dst_ref=lands[a].at[q], send_sem=send.at[3 * a + j],
                                                  recv_sem=recv.at[3 * a + j], device_id=(x, y, c), device_id_type=_MESH_ID)
                cp.wait_recv()
                cp.wait_send()

    res = pl.pallas_call(
        body, name=name, in_specs=[_HBM] * (2 * n) + [_SEM] * 2 + [pl.BlockSpec(memory_space=pl.ANY)] * len(after),
        out_shape=(*[_hbm(b) for b in sums], *[_hbm(b) for b in landings]), out_specs=tuple([_HBM] * (2 * n)),
        input_output_aliases={i: i for i in range(2 * n)},
        compiler_params=pltpu.CompilerParams(has_side_effects=_DATAFLOW),
    )(*sums, *landings, send_sems, recv_sems, *after)
    return list(res[:n]), list(res[n:])


_C1 = 1.0 - ADAM_B1 ** ADAM_STEP
_C2 = 1.0 - ADAM_B2 ** ADAM_STEP


def _adamw_math(g, w, m, v):
    m = ADAM_B1 * m + (1.0 - ADAM_B1) * g
    v = ADAM_B2 * v + (1.0 - ADAM_B2) * (g * g)
    delta = -ADAM_LR * ((m / _C1) / (jnp.sqrt(v / _C2) + ADAM_EPS) + ADAM_WD * w)
    return delta, m, v


def _adamw_reduce(landed, sums, chip, w, m, v, layer, prev, *, name):
    _, R, C = w.shape
    tr = max(t for t in range(16, R + 1, 16) if R % t == 0 and t * C <= 256 * 1024)

    def body(chip_ref, p_ref, own_ref, w_ref, m_ref, v_ref, *rest):
        g_ref, d_ref, nm_ref, nv_ref = rest[-4:]
        own = own_ref[...].astype(_F32)
        g = jnp.where(chip_ref[0] == 0, own, p_ref[0].astype(_F32))
        for q in range(1, N_CHIP):
            g = g + jnp.where(chip_ref[0] == q, own, p_ref[q].astype(_F32))
        d, nm, nv = _adamw_math(g, w_ref[...], m_ref[...], v_ref[...])
        g_ref[...] = g
        d_ref[...] = d
        nm_ref[...] = nm
        nv_ref[...] = nv

    blk = pl.BlockSpec((None, tr, C), lambda i, chip: (layer, i, 0))
    shape = jax.ShapeDtypeStruct((DEPTH, R, C), _F32)
    kept = [] if prev is None else list(prev)
    grid_spec = pltpu.PrefetchScalarGridSpec(
        num_scalar_prefetch=1, grid=(R // tr,),
        in_specs=[pl.BlockSpec((N_CHIP, tr, C), lambda i, chip: (0, i, 0)),
                  pl.BlockSpec((None, tr, C), lambda i, chip: (chip[0], i, 0)), blk, blk, blk]
        + [pl.BlockSpec(memory_space=pl.ANY)] * len(kept),
        out_specs=[blk] * 4)
    return pl.pallas_call(
        body, grid_spec=grid_spec, out_shape=[shape] * 4, name=name,
        input_output_aliases={6 + k: k for k in range(len(kept))},
        compiler_params=_cparams(dimension_semantics=("parallel",)),
    )(chip.reshape(1), landed, sums, w, m, v, *kept)


_PACK_LANES = 128
_LAYER_ROWS = 248
_LB_ROWS = (A_HEADS * HEAD_DIM) // _PACK_LANES


def _small_reduce(parts, lb_logits, *, name):
    rows = DEPTH * _LAYER_ROWS

    def body(p_ref, lg_ref, o_ref):
        g = p_ref[0]
        for s in range(1, N_DEV):
            g = g + p_ref[s]
        o_ref[...] = g
        lg = lg_ref[...]
        e = jnp.exp(lg - jnp.max(lg, axis=0, keepdims=True))
        p = e / jnp.sum(e, axis=0, keepdims=True)
        d1 = g[_LAYER_ROWS:_LAYER_ROWS + _LB_ROWS, :] * p[0] * p[1]
        o_ref[0:_LB_ROWS, :] = -d1
        o_ref[_LAYER_ROWS:_LAYER_ROWS + _LB_ROWS, :] = d1

    return pl.pallas_call(
        body, out_shape=jax.ShapeDtypeStruct((rows, _PACK_LANES), _F32), name=name,
        compiler_params=_cparams(),
    )(parts, lb_logits.reshape(DEPTH, _LB_ROWS, _PACK_LANES))


def _adamw_small(g, w, m, v, *, name):
    def body(g_ref, w_ref, m_ref, v_ref, d_ref, nm_ref, nv_ref):
        d, nm, nv = _adamw_math(g_ref[...], w_ref[...], m_ref[...], v_ref[...])
        d_ref[...] = d
        nm_ref[...] = nm
        nv_ref[...] = nv

    shape = jax.ShapeDtypeStruct(g.shape, _F32)
    return pl.pallas_call(body, out_shape=[shape] * 3, name=name, compiler_params=_cparams())(g, w, m, v)


def _pack(vectors, rows):
    flat = jnp.concatenate([v.reshape(-1).astype(_F32) for v in vectors])
    return jnp.pad(flat, (0, rows * _PACK_LANES - flat.shape[0])).reshape(rows, _PACK_LANES)


def _unpack(packed, shapes):
    flat = packed.reshape(-1)
    out, at = [], 0
    for s in shapes:
        size = int(np.prod(s))
        out.append(flat[at:at + size].reshape(s))
        at += size
    return out


_BIG = ("w_in", "w_gate", "w_up", "w_out", "w_down")
_COLUMN_SHARDED = ("w_in", "w_gate", "w_up")


def _full_weight(name, g):
    if name == "w_out":
        return g.reshape(D_MODEL, D_MODEL)
    if name == "w_down":
        return g.reshape(D_FF, D_MODEL)
    if name == "conv_w":
        return g.transpose(1, 0, 2).reshape(g.shape[1], N_DEV * SHARD_COLS)
    return g


class _WeightGather:
    def __init__(self, names, first, blocks, lands, sems, tag):
        self.names, self.first, self.blocks, self.lands, self.sems, self.tag = names, first, blocks, lands, sems, tag
        self.forwarded = None

    def forward(self, after):
        f_send, f_recv, self.lands, token = _gather_forward(self.lands, self.sems[2], self.first, after,
                                                            name=f"gather_forward_{self.tag}")
        self.forwarded = (f_send, f_recv)
        return token

    def wait(self, after):
        if self.forwarded is None:
            self.forward(after)
        got = _gather_wait(self.blocks, self.lands, self.sems[0], self.sems[1], self.first, *self.forwarded, after,
                           name=f"gather_wait_{self.tag}")
        return {n: _full_weight(n, g) for n, g in zip(self.names, got)}


def _start_gathers(groups, me, name):
    blocks = [b for _, _, bs in groups for b in bs]
    landings = [lax.dynamic_update_index_in_dim(lax.empty((N_DEV,) + b.shape, b.dtype), b[None], me, 0) for b in blocks]
    send, d2d, ici, blocks, landings, token = _gather_start(blocks, landings, name=name)
    out, first = [], 0
    for tag, names, bs in groups:
        k = len(bs)
        out.append(_WeightGather(names, first, blocks[first:first + k], landings[first:first + k], (send, d2d, ici), tag))
        first += k
    return out, token


class _LayerWeights:
    def __init__(self, ready, pending=(), forwards=(), tokens=()):
        self.ready, self.pending, self.forwards, self._tokens = dict(ready), list(pending), list(forwards), list(tokens)

    def at(self, point, after):
        for when, gather in self.forwards:
            if when == point:
                self._tokens.append(gather.forward(after))

    def tokens(self):
        out, self._tokens = self._tokens, []
        return out

    def get(self, name, after):
        if name not in self.ready:
            group, = [g for g in self.pending if name in g.names]
            self.ready.update(group.wait(after))
        return self.ready[name]


def _layer_fwd(x, xb, ws, lb_logits, a_norm_w, c_sink, ln1_g, ln1_b, conv_b, ln2_g, ln2_b, tabs, l):
    proj = _mm_w_slabs(xb, ws.get("w_in", xb), tm=1024, after=ws.tokens(), name=f"proj_{l}")
    o_a, raw, states = _hgrn_fwd(proj, lb_logits, a_norm_w, l, name=f"hgrn_fwd_{l}")
    ws.at("hgrn", o_a)
    o_b, lse_b = _band_fwd(proj, tabs, name=f"dilated_fwd_{l}", **_DILATED)
    o_c, lse_c = _band_fwd(proj, tabs, sink=c_sink, name=f"swa_fwd_{l}", **_SWA)
    ws.at("swa", o_c)
    mixed = jnp.concatenate([o_a, o_b, o_c], axis=1).astype(_MXU_DTYPE)
    y = _mm(mixed, ws.get("w_out", mixed), **_TILE_MIX, after=ws.tokens(), name=f"mix_out_{l}")
    z1, x1, x1b = _ln_fwd(x, y, ln1_g, ln1_b, name=f"ln1_fwd_{l}")
    g = _mm_w_slabs(x1b, ws.get("w_gate", x1b), tm=1024, out_dtype=_ACT_DTYPE, name=f"ffn_gate_{l}")
    u = _mm_w_slabs(x1b, ws.get("w_up", x1b), tm=1024, out_dtype=_ACT_DTYPE, name=f"ffn_up_{l}")
    ws.at("up", u)
    hb = _conv_gate_fwd(g, u, ws.get("conv_w", u), conv_b, name=f"conv_gate_fwd_{l}")
    y2 = _mm(hb, ws.get("w_down", hb), **_TILE_DOWN, after=ws.tokens(), name=f"ffn_down_{l}")
    ws.at("down", y2)
    z2, x2, x2b = _ln_fwd(x1, y2, ln2_g, ln2_b, name=f"ln2_fwd_{l}")
    res = dict(xb=xb, proj=proj, raw=raw, states=states, o_b=o_b, lse_b=lse_b, o_c=o_c, lse_c=lse_c,
               mixed=mixed, z1=z1, x1b=x1b, g=g, u=u, hb=hb, z2=z2)
    return x2, x2b, res


class _GradExchange:
    def __init__(self, core, chip):
        self.core, self.chip, self.groups, self.swapping, self._tokens = core, chip, [], [], []

    def launch(self, names, slabs, l, tag, behind):
        mine = [s.reshape((N_CHIP, 2) + s.shape[1:]) for s in slabs]
        if behind:
            landings = [lax.empty((N_CHIP,) + m.shape[2:], m.dtype) for m in mine]
            send, recv, mine, landings, token = _swap_start(mine, landings, name=f"swap_start_{tag}")
            self.swapping.append((names, l, tag, send, recv, mine, landings))
            self._tokens.append(token)
        else:
            self._exchange(names, l, tag, mine, _sibling_swap(mine, name=f"swap_grads_{tag}"))

    def advance(self, after):
        for names, l, tag, send, recv, mine, landings in self.swapping:
            mine, theirs = _swap_wait(mine, landings, send, recv, after, name=f"swap_wait_{tag}")
            self._exchange(names, l, tag, mine, theirs)
        self.swapping = []

    def _exchange(self, names, l, tag, mine, theirs):
        sums = [_pair_add(a, b, self.core, name=f"pair_add_{n}_{l}") for n, a, b in zip(names, mine, theirs)]
        landings = [lax.empty(s.shape, s.dtype) for s in sums]
        send, recv, sums, landings, token = _chip_exchange_start(sums, landings, name=f"exchange_start_{tag}")
        self.groups.append((names, l, tag, send, recv, sums, landings))
        self._tokens.append(token)

    def tokens(self):
        out, self._tokens = self._tokens, []
        return out

    def finish(self, weights, mom1, mom2, after):
        out = {}
        after = list(after) + self.tokens()
        for names, l, tag, send, recv, sums, landings in self.groups:
            sums, landings = _chip_exchange_wait(sums, landings, send, recv, after, name=f"exchange_wait_{tag}")
            for n, s, landed in zip(names, sums, landings):
                out[n] = _adamw_reduce(landed, s, self.chip, weights[n], mom1[n], mom2[n], l, out.get(n), name=f"adamw_{n}_{l}")
            after = [out[n][0] for n in names]
        return out


def _layer_bwd(dx2, res, w, lb_logits, a_norm_w, c_sink, ln1_g, conv_b, ln2_g, tabs, exchange, l):
    dz2, dz2b, d_ln2_g, d_ln2_b = _ln_bwd(res["z2"], dx2, None, ln2_g, name=f"ln2_bwd_{l}")
    exchange.advance(dz2b)
    dh = _mm(dz2b, w["w_down"], tb=True, **_TILE_DOWN_DX, out_dtype=_ACT_DTYPE, after=exchange.tokens(),
             name=f"ffn_down_dx_{l}")
    d_w_down = _mm(res["hb"], dz2b, ta=True, **_TILE_DOWN_DW, out_dtype=_GRAD_DTYPE, name=f"ffn_down_dw_{l}")
    dg, du, d_conv_w, d_conv_b = _conv_gate_bwd(dh, res["g"], res["u"], w["conv_w"], conv_b, name=f"conv_gate_bwd_{l}")
    t = _mm_nt_w_slabs(dg, w["w_gate"], **_TILE_NT_SLABS, name=f"ffn_gate_dx_{l}")
    dx1 = _mm_nt_w_slabs(du, w["w_up"], **_TILE_NT_SLABS, add=t, name=f"ffn_up_dx_{l}")
    d_w_gate = _mm_tn_slabs(res["x1b"], dg, tm=1024, name=f"ffn_gate_dw_{l}")
    d_w_up = _mm_tn_slabs(res["x1b"], du, tm=1024, name=f"ffn_up_dw_{l}")
    dz1, dz1b, d_ln1_g, d_ln1_b = _ln_bwd(res["z1"], dx1, dz2, ln1_g, name=f"ln1_bwd_{l}")
    d_w_out = _mm(res["mixed"], dz1b, ta=True, **_TILE_MIX, out_dtype=_GRAD_DTYPE, name=f"mix_out_dw_{l}")
    exchange.launch(("w_down", "w_gate", "w_up", "w_out"),
                    [d_w_down.reshape(N_DEV, D_FF // N_DEV, D_MODEL), d_w_gate, d_w_up,
                     d_w_out.reshape(N_DEV, D_MODEL // N_DEV, D_MODEL)], l, f"ffn_{l}", True)
    dmixed = _mm(dz1b, w["w_out"], tb=True, **_TILE_MIX, after=exchange.tokens(), name=f"mix_out_dx_{l}")
    dq_a, df_a, di_a, dg_a, d_norm_w, d_lb = _hgrn_bwd(res["proj"], lb_logits, a_norm_w, res["raw"], res["states"],
                                                      dmixed, l, name=f"hgrn_bwd_{l}")
    exchange.advance(dq_a)
    dq_b, dk_b, dv_b = _band_bwd(res["proj"], tabs, dmixed, res["o_b"], res["lse_b"], do0=A_HEADS, after=exchange.tokens(),
                                 name=f"dilated_bwd_{l}", **_DILATED)
    dq_c, dk_c, dv_c, d_sink = _band_bwd(res["proj"], tabs, dmixed, res["o_c"], res["lse_c"], do0=A_HEADS + B_HEADS,
                                         sink=c_sink, name=f"swa_bwd_{l}", **_SWA)
    dproj = jnp.concatenate([dq_a, df_a, di_a, dg_a, dq_b, dk_b, dv_b, dq_c, dk_c, dv_c], axis=1)
    d_w_in = _mm_tn_slabs(res["xb"], dproj, tm=1024, name=f"proj_dw_{l}")
    exchange.launch(("w_in",), [d_w_in], l, f"mix_{l}", l > 0)
    dx = _mm_nt_w_slabs(dproj, w["w_in"], **_TILE_NT_SLABS, add=dz1, add_scale=ALPHA, after=exchange.tokens(),
                        name=f"proj_dx_{l}")
    small = [d_lb, d_norm_w, jnp.pad(d_sink, (0, _PACK_LANES - C_HEADS)), d_ln1_g, d_ln1_b, d_ln2_g, d_ln2_b, d_conv_b,
             d_conv_w]
    return dx, small


def kernel(x, w_in, lb_logits, a_norm_w, c_sinks, w_out, ln1_g, ln1_b, w_gate, w_up, conv_w, conv_b, w_down, ln2_g, ln2_b, loss_target, m_w_in, m_lb_logits, m_a_norm_w, m_c_sinks, m_w_out, m_ln1_g, m_ln1_b, m_w_gate, m_w_up, m_conv_w, m_conv_b, m_w_down, m_ln2_g, m_ln2_b, v_w_in, v_lb_logits, v_a_norm_w, v_c_sinks, v_w_out, v_ln1_g, v_ln1_b, v_w_gate, v_w_up, v_conv_w, v_conv_b, v_w_down, v_ln2_g, v_ln2_b):
    weights = dict(w_in=w_in, lb_logits=lb_logits, a_norm_w=a_norm_w, c_sinks=c_sinks, w_out=w_out, ln1_g=ln1_g, ln1_b=ln1_b,
                   w_gate=w_gate, w_up=w_up, conv_w=conv_w, conv_b=conv_b, w_down=w_down, ln2_g=ln2_g, ln2_b=ln2_b)
    mom1 = dict(w_in=m_w_in, lb_logits=m_lb_logits, a_norm_w=m_a_norm_w, c_sinks=m_c_sinks, w_out=m_w_out, ln1_g=m_ln1_g,
                ln1_b=m_ln1_b, w_gate=m_w_gate, w_up=m_w_up, conv_w=m_conv_w, conv_b=m_conv_b, w_down=m_w_down, ln2_g=m_ln2_g,
                ln2_b=m_ln2_b)
    mom2 = dict(w_in=v_w_in, lb_logits=v_lb_logits, a_norm_w=v_a_norm_w, c_sinks=v_c_sinks, w_out=v_w_out, ln1_g=v_ln1_g,
                ln1_b=v_ln1_b, w_gate=v_w_gate, w_up=v_w_up, conv_w=v_conv_w, conv_b=v_conv_b, w_down=v_w_down, ln2_g=v_ln2_g,
                ln2_b=v_ln2_b)
    core = lax.axis_index("c").astype(jnp.int32)
    me = 4 * lax.axis_index("x") + 2 * lax.axis_index("y") + core
    tabs = _rope_tables()

    chip = (2 * lax.axis_index("x") + lax.axis_index("y")).astype(jnp.int32)

    def as_slabs(d):
        return {n: jnp.swapaxes(d[n], 1, 2) if n in _COLUMN_SHARDED else d[n] for n in _BIG}

    w_views = as_slabs(weights)

    def block(n, l, after=()):
        return conv_w[l] if n == "conv_w" else _cast_layer(w_views[n], l, after=after, name=f"cast_{n}_{l}")

    (in0,), started_first = _start_gathers([("w_in_0", ("w_in",), [block("w_in", 0)])], me, "gather_start_first")
    order = [(("w_out",), 0), (("w_gate", "w_up", "conv_w"), 0), (("w_down",), 0),
             (("w_in",), 1), (("w_out",), 1), (("w_gate", "w_up", "conv_w"), 1), (("w_down",), 1)]
    gathers, started = _start_gathers([(f"{names[0]}_{l}", names, [block(n, l, [started_first]) for n in names])
                                       for names, l in order], me, "gather_start_rest")
    out0, ffn0, down0, in1, out1, ffn1, down1 = gathers
    layer_ws = [_LayerWeights(in0.wait(started), [out0, ffn0, down0],
                              [("hgrn", out0), ("swa", ffn0), ("up", down0), ("down", in1)]),
                _LayerWeights({}, [in1, out1, ffn1, down1], [("hgrn", out1), ("swa", ffn1), ("up", down1)])]

    xs = x[0]
    xb = xs.astype(_MXU_DTYPE)
    saved = []
    for l in range(DEPTH):
        xs, xb, res = _layer_fwd(xs, xb, layer_ws[l], lb_logits, a_norm_w[l], c_sinks[l], ln1_g[l], ln1_b[l], conv_b[l],
                                 ln2_g[l], ln2_b[l], tabs, l)
        saved.append(res)
    loss_part, dx = _loss_head(xs, loss_target[0], name="loss_head")
    loss = lax.psum(loss_part, ("x", "y", "c"))

    exchange = _GradExchange(core, chip)
    small_parts = [None] * DEPTH
    for l in reversed(range(DEPTH)):
        dx, small = _layer_bwd(dx, saved[l], layer_ws[l].ready, lb_logits, a_norm_w[l], c_sinks[l], ln1_g[l], conv_b[l],
                               ln2_g[l], tabs, exchange, l)
        small_parts[l] = _pack(small, _LAYER_ROWS)
    (small_gather,), small_started = _start_gathers(
        [("small_grads", ("small",), [jnp.concatenate(small_parts, axis=0)])], me, "gather_start_small")
    updated = exchange.finish(w_views, as_slabs(mom1), as_slabs(mom2), [dx, small_started])
    gathered = small_gather.wait(updated["w_in"][0])["small"]
    updated = {n: tuple(jnp.swapaxes(t, 1, 2) for t in u) if n in _COLUMN_SHARDED else u for n, u in updated.items()}
    g_small = _small_reduce(gathered, lb_logits, name="small_grads")

    per_layer = [(A_HEADS * HEAD_DIM,), (HEAD_DIM,), (_PACK_LANES,), (D_MODEL,), (D_MODEL,), (D_MODEL,), (D_MODEL,), (D_FF,),
                 (3, D_FF)]
    names = ("lb_logits", "a_norm_w", "c_sinks", "ln1_g", "ln1_b", "ln2_g", "ln2_b", "conv_b", "conv_w")
    grads = {n: [] for n in names}
    for l in range(DEPTH):
        for n, t in zip(names, _unpack(g_small[l * _LAYER_ROWS:(l + 1) * _LAYER_ROWS], per_layer)):
            grads[n].append(t)
    grads = {n: jnp.stack(t) for n, t in grads.items()}
    grads["c_sinks"] = grads["c_sinks"][:, :C_HEADS]
    grads["conv_w"] = lax.dynamic_slice_in_dim(grads["conv_w"], me * SHARD_COLS, SHARD_COLS, axis=2)
    shapes = [grads[n].shape for n in names]
    rows = -(-sum(int(np.prod(s)) for s in shapes) // (8 * _PACK_LANES)) * 8
    d_s, m_s, v_s = _adamw_small(_pack([grads[n] for n in names], rows), _pack([weights[n] for n in names], rows),
                                 _pack([mom1[n] for n in names], rows), _pack([mom2[n] for n in names], rows),
                                 name="adamw_small")
    delta = dict(zip(names, _unpack(d_s, shapes)))
    new_m = dict(zip(names, _unpack(m_s, shapes)))
    new_v = dict(zip(names, _unpack(v_s, shapes)))
    for n in _BIG:
        grads[n], delta[n], new_m[n], new_v[n] = updated[n]

    order = ("w_in", "lb_logits", "a_norm_w", "c_sinks", "w_out", "ln1_g", "ln1_b", "w_gate", "w_up", "conv_w", "conv_b",
             "w_down", "ln2_g", "ln2_b")
    return (loss, dx[None], *[grads[n] for n in order], *[delta[n] for n in order], *[new_m[n] for n in order],
            *[new_v[n] for n in order])
```

```python
import functools

import jax
import jax.numpy as jnp
import numpy as np
from jax import lax
from jax.experimental import pallas as pl
from jax.experimental.pallas import tpu as pltpu

D_MODEL = 2048
SEQ = 2048
DEPTH = 2
HEAD_DIM = 128
A_HEADS = 4
B_HEADS = 6
C_HEADS = 6
C_KV_HEADS = 2
A_CHUNK = 16
DILATIONS = (1, 4, 16)
BLOCK = 128
ROPE_THETA = 500000.0
ROPE_DIM = 32
D_FF = 5632
IN_WIDTH = 5632
LN_EPS = 1e-5
ALPHA = (2 * DEPTH) ** 0.25
N_DEV = 8
SHARD_COLS = IN_WIDTH // N_DEV

ADAM_LR = 0.001
ADAM_B1 = 0.9
ADAM_B2 = 0.999
ADAM_EPS = 1e-08
ADAM_WD = 0.01
ADAM_STEP = 10

A_COLS = 16
QKV_COLS = 28
QB0, KB0, VB0, QC0, KC0, VC0 = 0, 6, 12, 18, 24, 26

_MXU_DTYPE = jnp.bfloat16
_GRAD_DTYPE = jnp.bfloat16
_ACT_DTYPE = jnp.bfloat16
_NEG = -1e30
_VMEM_LIMIT = 56 * 2 ** 20

_F32 = jnp.float32


def _sigmoid(x):
    return 0.5 * jnp.tanh(0.5 * x) + 0.5


def _cparams(**kw):
    return pltpu.CompilerParams(vmem_limit_bytes=_VMEM_LIMIT, **kw)


_TILE_MIX = dict(tm=1024, tn=1024)
_TILE_WIDE_K = dict(tm=1024, tn=512)
_TILE_WIDE_N = dict(tm=1024, tn=1408)
_TILE_WIDE_M = dict(tm=1408, tn=1024)


def _mm(a, b, *, ta=False, tb=False, tm, tn, out_dtype=_F32, add=None, add_scale=1.0, after=(), name):
    K = a.shape[0] if ta else a.shape[1]
    M = a.shape[1] if ta else a.shape[0]
    N = b.shape[0] if tb else b.shape[1]
    assert (b.shape[1] if tb else b.shape[0]) == K and M % tm == 0 and N % tn == 0
    dn = (((0 if ta else 1,), (1 if tb else 0,)), ((), ()))

    def body(*refs):
        a_ref, b_ref = refs[:2]
        o_ref = refs[-1]
        r = lax.dot_general(a_ref[...], b_ref[...], dn, preferred_element_type=_F32)
        if add is not None:
            r = r + add_scale * refs[2][...]
        o_ref[...] = r.astype(o_ref.dtype)

    a_spec = pl.BlockSpec((K, tm), lambda i, j: (0, i)) if ta else pl.BlockSpec((tm, K), lambda i, j: (i, 0))
    b_spec = pl.BlockSpec((tn, K), lambda i, j: (j, 0)) if tb else pl.BlockSpec((K, tn), lambda i, j: (0, j))
    o_spec = pl.BlockSpec((tm, tn), lambda i, j: (i, j))
    in_specs = [a_spec, b_spec] + ([o_spec] if add is not None else []) + [pl.BlockSpec(memory_space=pl.ANY)] * len(after)
    args = (a, b) + ((add,) if add is not None else ()) + tuple(after)
    return pl.pallas_call(
        body, grid=(M // tm, N // tn), in_specs=in_specs, out_specs=o_spec,
        out_shape=jax.ShapeDtypeStruct((M, N), out_dtype), name=name,
        compiler_params=_cparams(dimension_semantics=("parallel", "parallel")),
    )(*args)


def _cast_layer(w, layer, *, after=(), name):
    _, R, C = w.shape
    tr = max(t for t in range(16, R + 1, 16) if R % t == 0 and t * C <= 512 * 1024)

    def body(w_ref, *rest):
        o_ref = rest[-1]
        o_ref[...] = w_ref[...].astype(o_ref.dtype)

    return pl.pallas_call(
        body, grid=(R // tr,),
        in_specs=[pl.BlockSpec((None, tr, C), lambda i: (layer, i, 0))] + [pl.BlockSpec(memory_space=pl.ANY)] * len(after),
        out_specs=pl.BlockSpec((tr, C), lambda i: (i, 0)), out_shape=jax.ShapeDtypeStruct((R, C), _MXU_DTYPE), name=name,
        compiler_params=_cparams(dimension_semantics=("parallel",)),
    )(w, *after)


def _ln_fwd(x, y, g, b, *, name):
    tm = 256

    def body(x_ref, y_ref, g_ref, b_ref, z_ref, o_ref, ob_ref):
        z = ALPHA * x_ref[...] + y_ref[...]
        mu = jnp.mean(z, axis=-1, keepdims=True)
        zc = z - mu
        var = jnp.mean(zc * zc, axis=-1, keepdims=True)
        o = zc * lax.rsqrt(var + LN_EPS) * g_ref[...] + b_ref[...]
        z_ref[...] = z
        o_ref[...] = o
        ob_ref[...] = o.astype(ob_ref.dtype)

    row = pl.BlockSpec((tm, D_MODEL), lambda i: (i, 0))
    vec = pl.BlockSpec((1, D_MODEL), lambda i: (0, 0))
    return pl.pallas_call(
        body, grid=(SEQ // tm,), in_specs=[row, row, vec, vec], out_specs=[row, row, row],
        out_shape=[jax.ShapeDtypeStruct((SEQ, D_MODEL), _F32), jax.ShapeDtypeStruct((SEQ, D_MODEL), _F32),
                   jax.ShapeDtypeStruct((SEQ, D_MODEL), _MXU_DTYPE)],
        name=name, compiler_params=_cparams(dimension_semantics=("parallel",)),
    )(x, y, g.reshape(1, D_MODEL), b.reshape(1, D_MODEL))


def _ln_bwd(z, d_a, d_res, g, *, name):
    tm = 256

    def body(*refs):
        if d_res is None:
            z_ref, da_ref, g_ref, dz_ref, dzb_ref, dg_ref, db_ref = refs
        else:
            z_ref, da_ref, dr_ref, g_ref, dz_ref, dzb_ref, dg_ref, db_ref = refs

        @pl.when(pl.program_id(0) == 0)
        def _():
            dg_ref[...] = jnp.zeros_like(dg_ref)
            db_ref[...] = jnp.zeros_like(db_ref)

        dout = da_ref[...]
        if d_res is not None:
            dout = dout + ALPHA * dr_ref[...]
        z = z_ref[...]
        mu = jnp.mean(z, axis=-1, keepdims=True)
        zc = z - mu
        var = jnp.mean(zc * zc, axis=-1, keepdims=True)
        rstd = lax.rsqrt(var + LN_EPS)
        xh = zc * rstd
        dxh = dout * g_ref[...]
        m1 = jnp.mean(dxh, axis=-1, keepdims=True)
        m2 = jnp.mean(dxh * xh, axis=-1, keepdims=True)
        dz = rstd * (dxh - m1 - xh * m2)
        dz_ref[...] = dz
        dzb_ref[...] = dz.astype(dzb_ref.dtype)
        dg_ref[0:1, :] += jnp.sum(dout * xh, axis=0, keepdims=True)
        db_ref[0:1, :] += jnp.sum(dout, axis=0, keepdims=True)

    row = pl.BlockSpec((tm, D_MODEL), lambda i: (i, 0))
    vec = pl.BlockSpec((1, D_MODEL), lambda i: (0, 0))
    acc = pl.BlockSpec((8, D_MODEL), lambda i: (0, 0))
    ins = [z, d_a] + ([d_res] if d_res is not None else []) + [g.reshape(1, D_MODEL)]
    in_specs = [row, row] + ([row] if d_res is not None else []) + [vec]
    dz, dzb, dg, db = pl.pallas_call(
        body, grid=(SEQ // tm,), in_specs=in_specs, out_specs=[row, row, acc, acc],
        out_shape=[jax.ShapeDtypeStruct((SEQ, D_MODEL), _F32), jax.ShapeDtypeStruct((SEQ, D_MODEL), _MXU_DTYPE),
                   jax.ShapeDtypeStruct((8, D_MODEL), _F32), jax.ShapeDtypeStruct((8, D_MODEL), _F32)],
        name=name, compiler_params=_cparams(dimension_semantics=("arbitrary",)),
    )(*ins)
    return dz, dzb, dg[0], db[0]


def _loss_head(y, target, *, name):
    tm = 256

    def body(y_ref, t_ref, d_ref, l_ref):
        e = y_ref[...] - t_ref[...]
        d_ref[...] = e * (1.0 / D_MODEL)

        @pl.when(pl.program_id(0) == 0)
        def _():
            l_ref[...] = jnp.zeros_like(l_ref)

        l_ref[...] += (0.5 / D_MODEL) * jnp.sum(e * e)

    row = pl.BlockSpec((tm, D_MODEL), lambda i: (i, 0))
    d, l = pl.pallas_call(
        body, grid=(SEQ // tm,), in_specs=[row, row], out_specs=[row, pl.BlockSpec((8, 128), lambda i: (0, 0))],
        out_shape=[jax.ShapeDtypeStruct((SEQ, D_MODEL), _F32), jax.ShapeDtypeStruct((8, 128), _F32)],
        name=name, compiler_params=_cparams(dimension_semantics=("arbitrary",)),
    )(y, target)
    return l[0, 0], d


_CONV_TN = 256


def _shift_down(v, k, rows):
    return jnp.where(rows >= k, pltpu.roll(v, k, axis=0), 0.0)


def _shift_up(v, k, rows):
    return jnp.where(rows < SEQ - k, pltpu.roll(v, SEQ - k, axis=0), 0.0)


def _conv_gate_fwd(g, u, conv_w, conv_b, *, name):
    def body(g_ref, u_ref, w_ref, b_ref, h_ref):
        gv = g_ref[...].astype(_F32)
        rows = lax.broadcasted_iota(jnp.int32, gv.shape, 0)
        w = w_ref[...]
        gc = b_ref[...] + w[2:3, :] * gv + w[1:2, :] * _shift_down(gv, 1, rows) + w[0:1, :] * _shift_down(gv, 2, rows)
        h_ref[...] = (gc * _sigmoid(gc) * u_ref[...].astype(_F32)).astype(h_ref.dtype)

    col = pl.BlockSpec((SEQ, _CONV_TN), lambda j: (0, j))
    return pl.pallas_call(
        body, grid=(D_FF // _CONV_TN,),
        in_specs=[col, col, pl.BlockSpec((3, _CONV_TN), lambda j: (0, j)), pl.BlockSpec((1, _CONV_TN), lambda j: (0, j))],
        out_specs=col, out_shape=jax.ShapeDtypeStruct((SEQ, D_FF), _MXU_DTYPE), name=name,
        compiler_params=_cparams(dimension_semantics=("parallel",)),
    )(g, u, conv_w, conv_b.reshape(1, D_FF))


def _conv_gate_bwd(dh, g, u, conv_w, conv_b, *, name):
    def body(dh_ref, g_ref, u_ref, w_ref, b_ref, dg_ref, du_ref, dw_ref, db_ref):
        gv = g_ref[...].astype(_F32)
        rows = lax.broadcasted_iota(jnp.int32, gv.shape, 0)
        w = w_ref[...]
        g1 = _shift_down(gv, 1, rows)
        g2 = _shift_down(gv, 2, rows)
        gc = b_ref[...] + w[2:3, :] * gv + w[1:2, :] * g1 + w[0:1, :] * g2
        sg = _sigmoid(gc)
        dh = dh_ref[...].astype(_F32)
        du_ref[...] = (dh * (gc * sg)).astype(du_ref.dtype)
        dgc = dh * u_ref[...].astype(_F32) * (sg * (1.0 + gc * (1.0 - sg)))
        dg = w[2:3, :] * dgc + w[1:2, :] * _shift_up(dgc, 1, rows) + w[0:1, :] * _shift_up(dgc, 2, rows)
        dg_ref[...] = dg.astype(dg_ref.dtype)
        dw_ref[0:1, :] = jnp.sum(dgc * g2, axis=0, keepdims=True)
        dw_ref[1:2, :] = jnp.sum(dgc * g1, axis=0, keepdims=True)
        dw_ref[2:3, :] = jnp.sum(dgc * gv, axis=0, keepdims=True)
        db_ref[...] = jnp.sum(dgc, axis=0, keepdims=True)

    col = pl.BlockSpec((SEQ, _CONV_TN), lambda j: (0, j))
    w3 = pl.BlockSpec((3, _CONV_TN), lambda j: (0, j))
    w1 = pl.BlockSpec((1, _CONV_TN), lambda j: (0, j))
    dg, du, dw, db = pl.pallas_call(
        body, grid=(D_FF // _CONV_TN,), in_specs=[col, col, col, w3, w1], out_specs=[col, col, w3, w1],
        out_shape=[jax.ShapeDtypeStruct((SEQ, D_FF), _MXU_DTYPE), jax.ShapeDtypeStruct((SEQ, D_FF), _MXU_DTYPE),
                   jax.ShapeDtypeStruct((3, D_FF), _F32), jax.ShapeDtypeStruct((1, D_FF), _F32)],
        name=name, compiler_params=_cparams(dimension_semantics=("parallel",)),
    )(dh, g, u, conv_w, conv_b.reshape(1, D_FF))
    return dg, du, dw, db[0]


def _rope_tables():
    half = ROPE_DIM // 2
    inv = ROPE_THETA ** (-jnp.arange(0, ROPE_DIM, 2, dtype=_F32) / ROPE_DIM)
    ang = jnp.arange(SEQ, dtype=_F32)[:, None] * inv[None, :]
    cos, sin = jnp.cos(ang), jnp.sin(ang)
    rest = HEAD_DIM - ROPE_DIM
    c = jnp.concatenate([cos, cos, jnp.ones((SEQ, rest), _F32)], axis=1)
    s1 = jnp.concatenate([-sin, jnp.zeros((SEQ, HEAD_DIM - half), _F32)], axis=1)
    s2 = jnp.concatenate([jnp.zeros((SEQ, half), _F32), sin, jnp.zeros((SEQ, rest), _F32)], axis=1)
    return c, s1, s2


def _rope_apply(x, c, s1, s2):
    return x * c + pltpu.roll(x, HEAD_DIM - ROPE_DIM // 2, axis=1) * s1 + pltpu.roll(x, ROPE_DIM // 2, axis=1) * s2


def _rope_transpose(d, c, s1, s2):
    half = ROPE_DIM // 2
    return d * c + pltpu.roll(d * s1, half, axis=1) + pltpu.roll(d * s2, HEAD_DIM - half, axis=1)


_NT = (((1,), (1,)), ((), ()))
_TN = (((0,), (0,)), ((), ()))
_SCALE = HEAD_DIM ** -0.5


def _band_scores(q, k2, n, lag_off):
    s = lax.dot_general(q, k2, _NT, preferred_element_type=_F32) * _SCALE
    row = lax.broadcasted_iota(jnp.int32, (BLOCK, 2 * BLOCK), 0)
    col = lax.broadcasted_iota(jnp.int32, (BLOCK, 2 * BLOCK), 1)
    front = (col >= row + lag_off) & (col < BLOCK) & (n > 0)
    own = (col >= BLOCK) & (col <= row + BLOCK)
    return jnp.where(front | own, s, _NEG)


_BAND_STEPS = SEQ // BLOCK


def _rows(start, d):
    if d == 1:
        return pl.ds(pl.multiple_of(start, BLOCK), BLOCK)
    return pl.ds(start, BLOCK, stride=d)


def _band_block(it, d):
    r, n = it % d, it // d
    span = BLOCK * d
    return n, _rows(r + n * span, d), _rows(r + jnp.maximum(n - 1, 0) * span, d)


def _band_fwd(proj, tabs, *, kv_heads, q_per_kv, q0, k0, v0, dilations, lag_off, sink, name):
    heads = kv_heads * q_per_kv

    def body(*refs):
        q_refs = refs[:q_per_kv]
        k_ref, v_ref, c_ref, s1_ref, s2_ref = refs[q_per_kv:q_per_kv + 5]
        rest = refs[q_per_kv + 5:]
        if sink is not None:
            sk_ref, rest = rest[0], rest[1:]
        o_ref, lse_ref, qs, ks, m_s, l_s, acc_s = rest
        c, s1, s2 = c_ref[...], s1_ref[...], s2_ref[...]
        ks[...] = _rope_apply(k_ref[...], c, s1, s2)
        for i in range(q_per_kv):
            qs[...] = _rope_apply(q_refs[i][...], c, s1, s2)
            for pi, d in enumerate(dilations):
                def step(it, carry, d=d, first=(pi == 0)):
                    n, cur, prev = _band_block(it, d)
                    q = qs[cur, :].astype(_MXU_DTYPE)
                    k2 = jnp.concatenate([ks[prev, :], ks[cur, :]], axis=0).astype(_MXU_DTYPE)
                    v2 = jnp.concatenate([v_ref[prev, :], v_ref[cur, :]], axis=0).astype(_MXU_DTYPE)
                    s = _band_scores(q, k2, n, lag_off)
                    m_b = jnp.max(s, axis=1, keepdims=True)
                    m_new = m_b if first else jnp.maximum(m_b, m_s[cur, :][:, 0:1])
                    p = jnp.exp(s - m_new)
                    l_new = jnp.sum(p, axis=1, keepdims=True)
                    acc = jnp.dot(p.astype(_MXU_DTYPE), v2, preferred_element_type=_F32)
                    if not first:
                        a = jnp.exp(m_s[cur, :][:, 0:1] - m_new)
                        l_new = l_new + a * l_s[cur, :][:, 0:1]
                        acc = acc + a * acc_s[cur, :]
                    m_s[cur, :] = jnp.broadcast_to(m_new, (BLOCK, HEAD_DIM))
                    l_s[cur, :] = jnp.broadcast_to(l_new, (BLOCK, HEAD_DIM))
                    acc_s[cur, :] = acc
                    return carry

                lax.fori_loop(0, _BAND_STEPS, step, 0, unroll=4)
            m, den = m_s[...], l_s[...]
            if sink is not None:
                sk = sk_ref[i]
                m_f = jnp.maximum(m, sk)
                a = jnp.exp(m - m_f)
                den = den * a + jnp.exp(sk - m_f)
                o = acc_s[...] * a / den
                m = m_f
            else:
                o = acc_s[...] / den
            o_ref[:, i * HEAD_DIM:(i + 1) * HEAD_DIM] = o
            lse_ref[:, i * HEAD_DIM:(i + 1) * HEAD_DIM] = m + jnp.log(den)

    col = (SEQ, HEAD_DIM)
    in_specs = [pl.BlockSpec(col, functools.partial(lambda g, i: (0, A_COLS + q0 + g * q_per_kv + i), i=i)) for i in range(q_per_kv)]
    in_specs += [pl.BlockSpec(col, lambda g: (0, A_COLS + k0 + g)), pl.BlockSpec(col, lambda g: (0, A_COLS + v0 + g))]
    in_specs += [pl.BlockSpec(col, lambda g: (0, 0))] * 3
    args = [proj] * (q_per_kv + 2) + list(tabs)
    if sink is not None:
        in_specs.append(pl.BlockSpec((q_per_kv, 1, HEAD_DIM), lambda g: (g, 0, 0)))
        args.append(jnp.broadcast_to(sink.reshape(heads, 1, 1), (heads, 1, HEAD_DIM)))
    o_spec = pl.BlockSpec((SEQ, q_per_kv * HEAD_DIM), lambda g: (0, g))
    shape = jax.ShapeDtypeStruct((SEQ, heads * HEAD_DIM), _F32)
    return pl.pallas_call(
        body, grid=(kv_heads,), in_specs=in_specs, out_specs=[o_spec, o_spec], out_shape=[shape, shape],
        scratch_shapes=[pltpu.VMEM(col, _F32)] * 5, name=name,
        compiler_params=_cparams(dimension_semantics=("parallel",)),
    )(*args)


def _band_bwd(proj, tabs, dmixed, o, lse, *, kv_heads, q_per_kv, q0, k0, v0, do0, dilations, lag_off, sink, after=(), name):
    heads = kv_heads * q_per_kv

    def body(*refs):
        q_refs = refs[:q_per_kv]
        k_ref, v_ref, c_ref, s1_ref, s2_ref = refs[q_per_kv:q_per_kv + 5]
        do_refs = refs[q_per_kv + 5:2 * q_per_kv + 5]
        o_ref, lse_ref = refs[2 * q_per_kv + 5:2 * q_per_kv + 7]
        rest = refs[2 * q_per_kv + 7:]
        if sink is not None:
            sk_ref, rest = rest[0], rest[1:]
            dq_ref, dk_ref, dv_ref, dsk_ref, qs, ks, dq_s, dk_s, dv_s = rest[len(after):]
        else:
            dq_ref, dk_ref, dv_ref, qs, ks, dq_s, dk_s, dv_s = rest[len(after):]
        c, s1, s2 = c_ref[...], s1_ref[...], s2_ref[...]
        ks[...] = _rope_apply(k_ref[...], c, s1, s2)
        dk_s[...] = jnp.zeros_like(dk_s)
        dv_s[...] = jnp.zeros_like(dv_s)
        for i in range(q_per_kv):
            hs = slice(i * HEAD_DIM, (i + 1) * HEAD_DIM)
            qs[...] = _rope_apply(q_refs[i][...], c, s1, s2)
            dq_s[...] = jnp.zeros_like(dq_s)
            do_ref = do_refs[i]
            for d in dilations:
                def step(it, carry, d=d, do_ref=do_ref, hs=hs):
                    n, cur, prev = _band_block(it, d)
                    q = qs[cur, :].astype(_MXU_DTYPE)
                    k2 = jnp.concatenate([ks[prev, :], ks[cur, :]], axis=0).astype(_MXU_DTYPE)
                    v2 = jnp.concatenate([v_ref[prev, :], v_ref[cur, :]], axis=0).astype(_MXU_DTYPE)
                    do = do_ref[cur, :]
                    delta = jnp.sum(do * o_ref[cur, hs], axis=1, keepdims=True)
                    lse_c = lse_ref[cur, hs][:, 0:1]
                    p = jnp.exp(_band_scores(q, k2, n, lag_off) - lse_c)
                    dob = do.astype(_MXU_DTYPE)
                    ds = (p * (lax.dot_general(dob, v2, _NT, preferred_element_type=_F32) - delta) * _SCALE).astype(_MXU_DTYPE)
                    dq_s[cur, :] += jnp.dot(ds, k2, preferred_element_type=_F32)
                    dk2 = lax.dot_general(ds, q, _TN, preferred_element_type=_F32)
                    dv2 = lax.dot_general(p.astype(_MXU_DTYPE), dob, _TN, preferred_element_type=_F32)
                    dk_s[prev, :] += dk2[:BLOCK]
                    dv_s[prev, :] += dv2[:BLOCK]
                    dk_s[cur, :] += dk2[BLOCK:]
                    dv_s[cur, :] += dv2[BLOCK:]
                    return carry

                lax.fori_loop(0, _BAND_STEPS, step, 0, unroll=4)
            dq_ref[:, hs] = _rope_transpose(dq_s[...], c, s1, s2).astype(dq_ref.dtype)
            if sink is not None:
                delta = jnp.sum(do_ref[...] * o_ref[:, hs], axis=1, keepdims=True)
                w_sink = jnp.exp(sk_ref[i] - lse_ref[:, hs])
                dsk_ref[i] = jnp.broadcast_to(jnp.sum(-delta * w_sink[:, 0:1]), (8, HEAD_DIM))
        dk_ref[...] = _rope_transpose(dk_s[...], c, s1, s2).astype(dk_ref.dtype)
        dv_ref[...] = dv_s[...].astype(dv_ref.dtype)

    col = (SEQ, HEAD_DIM)
    in_specs = [pl.BlockSpec(col, functools.partial(lambda g, i: (0, A_COLS + q0 + g * q_per_kv + i), i=i)) for i in range(q_per_kv)]
    in_specs += [pl.BlockSpec(col, lambda g: (0, A_COLS + k0 + g)), pl.BlockSpec(col, lambda g: (0, A_COLS + v0 + g))]
    in_specs += [pl.BlockSpec(col, lambda g: (0, 0))] * 3
    in_specs += [pl.BlockSpec(col, functools.partial(lambda g, i: (0, do0 + g * q_per_kv + i), i=i)) for i in range(q_per_kv)]
    wide = pl.BlockSpec((SEQ, q_per_kv * HEAD_DIM), lambda g: (0, g))
    in_specs += [wide, wide]
    args = [proj] * (q_per_kv + 2) + list(tabs) + [dmixed] * q_per_kv + [o, lse]
    out_specs = [wide, pl.BlockSpec(col, lambda g: (0, g)), pl.BlockSpec(col, lambda g: (0, g))]
    out_shape = [jax.ShapeDtypeStruct((SEQ, heads * HEAD_DIM), _MXU_DTYPE), jax.ShapeDtypeStruct((SEQ, kv_heads * HEAD_DIM), _MXU_DTYPE),
                 jax.ShapeDtypeStruct((SEQ, kv_heads * HEAD_DIM), _MXU_DTYPE)]
    if sink is not None:
        in_specs.append(pl.BlockSpec((q_per_kv, 1, HEAD_DIM), lambda g: (g, 0, 0)))
        args.append(jnp.broadcast_to(sink.reshape(heads, 1, 1), (heads, 1, HEAD_DIM)))
        out_specs.append(pl.BlockSpec((q_per_kv, 8, HEAD_DIM), lambda g: (g, 0, 0)))
        out_shape.append(jax.ShapeDtypeStruct((heads, 8, HEAD_DIM), _F32))
    in_specs += [pl.BlockSpec(memory_space=pl.ANY)] * len(after)
    args += list(after)
    res = pl.pallas_call(
        body, grid=(kv_heads,), in_specs=in_specs, out_specs=out_specs, out_shape=out_shape,
        scratch_shapes=[pltpu.VMEM(col, _F32)] * 5, name=name,
        compiler_params=_cparams(dimension_semantics=("parallel",)),
    )(*args)
    if sink is not None:
        return res[0], res[1], res[2], res[3][:, 0, 0]
    return res


_DILATED = dict(kv_heads=B_HEADS, q_per_kv=1, q0=QB0, k0=KB0, v0=VB0, dilations=DILATIONS, lag_off=0, sink=None)
_SWA = dict(kv_heads=C_KV_HEADS, q_per_kv=C_HEADS // C_KV_HEADS, q0=QC0, k0=KC0, v0=VC0, dilations=(1,), lag_off=1)


_HG_TILE = 128
_HG_CHUNKS = _HG_TILE // A_CHUNK
_HG_TILES = SEQ // _HG_TILE
_HI = lax.Precision.HIGHEST


def _chunk_tri():
    i = np.arange(_HG_TILE)
    return jnp.asarray(((i[:, None] // A_CHUNK == i[None, :] // A_CHUNK) & (i[None, :] <= i[:, None])).astype(np.float32))


def _layer_lb(lb_ref, layer):
    if layer == 0:
        return jnp.zeros((1, HEAD_DIM), _F32)
    lg = lb_ref[...]
    m = jnp.max(lg, axis=0, keepdims=True)
    e = jnp.exp(lg - m)
    return e[1:2, :] / jnp.sum(e, axis=0, keepdims=True)


def _hgrn_gates(q, fr, lb):
    sgq = _sigmoid(q)
    sg = _sigmoid(fr)
    f = lb + (1.0 - lb) * sg
    return sgq, q * sgq, sg, f, 1.0 - f


def _hgrn_fwd(proj, lb_logits, norm_w, layer, *, name):
    tri = _chunk_tri()

    def body(q_ref, f_ref, i_ref, g_ref, lb_ref, nw_ref, tri_ref, o_ref, raw_ref, st_ref, state):
        @pl.when(pl.program_id(1) == 0)
        def _():
            state[...] = jnp.zeros_like(state)

        lb = _layer_lb(lb_ref, layer)
        _, qs, _, f, k = _hgrn_gates(q_ref[...], f_ref[...], lb)
        v = i_ref[...]
        b = jnp.dot(tri_ref[...], jnp.log(f), precision=_HI, preferred_element_type=_F32)
        eb = jnp.exp(b)
        ridx = lax.broadcasted_iota(jnp.int32, (A_CHUNK, HEAD_DIM), 0)
        outs = []
        for c in range(_HG_CHUNKS):
            sl = slice(c * A_CHUNK, (c + 1) * A_CHUNK)
            bc, qc, kc, vc = b[sl], qs[sl], k[sl], v[sl]
            bl = bc[A_CHUNK - 1:A_CHUNK]
            st = state[...]
            st_ref[0, c] = st
            o_c = lax.dot_general((qc * eb[sl]).astype(_MXU_DTYPE), st.astype(_MXU_DTYPE), _NT, preferred_element_type=_F32)
            rows = []
            for i in range(A_CHUNK):
                di = jnp.exp(jnp.where(ridx <= i, bc[i:i + 1] - bc, _NEG))
                a = jnp.sum(qc[i:i + 1] * kc * di, axis=1, keepdims=True)
                rows.append(jnp.sum(a * vc, axis=0, keepdims=True))
            outs.append(o_c + jnp.concatenate(rows, axis=0))
            kt = (kc * jnp.exp(bl - bc)).astype(_MXU_DTYPE)
            state[...] = st * jnp.exp(bl) + lax.dot_general(vc.astype(_MXU_DTYPE), kt, _TN, preferred_element_type=_F32)
        o = jnp.concatenate(outs, axis=0)
        raw_ref[...] = o
        r = lax.rsqrt(jnp.mean(o * o, axis=-1, keepdims=True) + LN_EPS)
        g = g_ref[...]
        o_ref[...] = o * r * nw_ref[...] * (g * _sigmoid(g))

    blk = (_HG_TILE, HEAD_DIM)

    def col(base):
        return pl.BlockSpec(blk, lambda h, t: (t, base + h))

    o_spec = pl.BlockSpec(blk, lambda h, t: (t, h))
    o_shape = jax.ShapeDtypeStruct((SEQ, A_HEADS * HEAD_DIM), _F32)
    return pl.pallas_call(
        body, grid=(A_HEADS, _HG_TILES),
        in_specs=[col(0), col(4), col(8), col(12), pl.BlockSpec((DEPTH, HEAD_DIM), lambda h, t: (0, h)),
                  pl.BlockSpec((1, HEAD_DIM), lambda h, t: (0, 0)), pl.BlockSpec((_HG_TILE, _HG_TILE), lambda h, t: (0, 0))],
        out_specs=[o_spec, o_spec, pl.BlockSpec((1, _HG_CHUNKS, HEAD_DIM, HEAD_DIM), lambda h, t: (h, t, 0, 0))],
        out_shape=[o_shape, o_shape, jax.ShapeDtypeStruct((A_HEADS, SEQ // A_CHUNK, HEAD_DIM, HEAD_DIM), _F32)],
        scratch_shapes=[pltpu.VMEM((HEAD_DIM, HEAD_DIM), _F32)], name=name,
        compiler_params=_cparams(dimension_semantics=("parallel", "arbitrary")),
    )(proj, proj, proj, proj, lb_logits, norm_w.reshape(1, HEAD_DIM), tri)


def _hgrn_bwd(proj, lb_logits, norm_w, raw, states, dmixed, layer, *, name):
    tri = _chunk_tri()
    triu = tri.T

    def body(q_ref, f_ref, i_ref, g_ref, lb_ref, nw_ref, tri_ref, triu_ref, raw_ref, do_ref, st_ref,
             dq_ref, df_ref, di_ref, dg_ref, dnw_ref, dlb_ref, dstate):
        @pl.when(pl.program_id(1) == 0)
        def _():
            dstate[...] = jnp.zeros_like(dstate)
            dlb_ref[...] = jnp.zeros_like(dlb_ref)

        @pl.when((pl.program_id(0) == 0) & (pl.program_id(1) == 0))
        def _():
            dnw_ref[...] = jnp.zeros_like(dnw_ref)

        lb = _layer_lb(lb_ref, layer)
        q = q_ref[...]
        sgq, qs, sg, f, k = _hgrn_gates(q, f_ref[...], lb)
        v = i_ref[...]
        b = jnp.dot(tri_ref[...], jnp.log(f), precision=_HI, preferred_element_type=_F32)
        eb = jnp.exp(b)
        g = g_ref[...]
        nw = nw_ref[...]
        o = raw_ref[...]
        dout = do_ref[...]
        sgg = _sigmoid(g)
        r = lax.rsqrt(jnp.mean(o * o, axis=-1, keepdims=True) + LN_EPS)
        dg_ref[...] = (dout * (o * r * nw) * (sgg * (1.0 + g * (1.0 - sgg)))).astype(dg_ref.dtype)
        don = dout * (g * sgg)
        dnw_ref[0:1, :] += jnp.sum(don * o * r, axis=0, keepdims=True)
        dy = don * nw
        do_raw = r * dy - o * (r * r * r) * jnp.mean(o * dy, axis=-1, keepdims=True)

        ridx = lax.broadcasted_iota(jnp.int32, (A_CHUNK, HEAD_DIM), 0)
        dqs_t, dk_t, db_t, dv_t = [None] * _HG_CHUNKS, [None] * _HG_CHUNKS, [None] * _HG_CHUNKS, [None] * _HG_CHUNKS
        for c in reversed(range(_HG_CHUNKS)):
            sl = slice(c * A_CHUNK, (c + 1) * A_CHUNK)
            bc, qc, kc, vc, doc = b[sl], qs[sl], k[sl], v[sl], do_raw[sl]
            bl = bc[A_CHUNK - 1:A_CHUNK]
            ebc = eb[sl]
            ebl = jnp.exp(bl - bc)
            lam = jnp.exp(bl)
            qt = qc * ebc
            kt = kc * ebl
            dst = dstate[...]
            stp = st_ref[0, c]
            dob = doc.astype(_MXU_DTYPE)
            dstb = dst.astype(_MXU_DTYPE)
            dqt = jnp.dot(dob, stp.astype(_MXU_DTYPE), preferred_element_type=_F32)
            dkt = jnp.dot(vc.astype(_MXU_DTYPE), dstb, preferred_element_type=_F32)
            dv = lax.dot_general(kt.astype(_MXU_DTYPE), dstb, _NT, preferred_element_type=_F32)
            dlam = jnp.sum(stp * dst, axis=0, keepdims=True)
            dstate[...] = dst * lam + lax.dot_general(dob, qt.astype(_MXU_DTYPE), _TN, preferred_element_type=_F32)
            dqs_rows = []
            dk_in = jnp.zeros((A_CHUNK, HEAD_DIM), _F32)
            for i in range(A_CHUNK):
                di = jnp.exp(jnp.where(ridx <= i, bc[i:i + 1] - bc, _NEG))
                qi = qc[i:i + 1]
                doi = doc[i:i + 1]
                w = kc * di
                a = jnp.sum(qi * w, axis=1, keepdims=True)
                dv = dv + a * doi
                da = jnp.sum(doi * vc, axis=1, keepdims=True)
                dqs_rows.append(jnp.sum(da * w, axis=0, keepdims=True))
                dk_in = dk_in + da * (qi * di)
            dqs_in = jnp.concatenate(dqs_rows, axis=0)
            dbl = jnp.sum(dkt * kt, axis=0, keepdims=True) + dlam * lam
            db = qc * dqs_in - kc * dk_in + dqt * qt - dkt * kt
            db_t[c] = db + jnp.where(ridx == A_CHUNK - 1, dbl, 0.0)
            dqs_t[c] = dqs_in + dqt * ebc
            dk_t[c] = dk_in + dkt * ebl
            dv_t[c] = dv
        dqs = jnp.concatenate(dqs_t, axis=0)
        dk = jnp.concatenate(dk_t, axis=0)
        db = jnp.concatenate(db_t, axis=0)
        di_ref[...] = jnp.concatenate(dv_t, axis=0).astype(di_ref.dtype)
        dlogf = jnp.dot(triu_ref[...], db, precision=_HI, preferred_element_type=_F32)
        df = dlogf / f - dk
        df_ref[...] = (df * (1.0 - lb) * sg * (1.0 - sg)).astype(df_ref.dtype)
        dlb_ref[0, 0:1, :] += jnp.sum(df * (1.0 - sg), axis=0, keepdims=True)
        dq_ref[...] = (dqs * (sgq * (1.0 + q * (1.0 - sgq)))).astype(dq_ref.dtype)

    blk = (_HG_TILE, HEAD_DIM)
    last = _HG_TILES - 1

    def col(base):
        return pl.BlockSpec(blk, lambda h, t: (last - t, base + h))

    tri_spec = pl.BlockSpec((_HG_TILE, _HG_TILE), lambda h, t: (0, 0))
    acc_spec = pl.BlockSpec((1, 8, HEAD_DIM), lambda h, t: (h, 0, 0))
    acc_shape = jax.ShapeDtypeStruct((A_HEADS, 8, HEAD_DIM), _F32)
    dq, df, di, dg, dnw, dlb = pl.pallas_call(
        body, grid=(A_HEADS, _HG_TILES),
        in_specs=[col(0), col(4), col(8), col(12), pl.BlockSpec((DEPTH, HEAD_DIM), lambda h, t: (0, h)),
                  pl.BlockSpec((1, HEAD_DIM), lambda h, t: (0, 0)), tri_spec, tri_spec, col(0), col(0),
                  pl.BlockSpec((1, _HG_CHUNKS, HEAD_DIM, HEAD_DIM), lambda h, t: (h, last - t, 0, 0))],
        out_specs=[col(0), col(0), col(0), col(0), pl.BlockSpec((8, HEAD_DIM), lambda h, t: (0, 0)), acc_spec],
        out_shape=[jax.ShapeDtypeStruct((SEQ, A_HEADS * HEAD_DIM), _MXU_DTYPE)] * 4
        + [jax.ShapeDtypeStruct((8, HEAD_DIM), _F32), acc_shape],
        scratch_shapes=[pltpu.VMEM((HEAD_DIM, HEAD_DIM), _F32)], name=name,
        compiler_params=_cparams(dimension_semantics=("arbitrary", "arbitrary")),
    )(proj, proj, proj, proj, lb_logits, norm_w.reshape(1, HEAD_DIM), tri, triu, raw, dmixed, states)
    return dq, df, di, dg, dnw[0], dlb[:, 0, :].reshape(A_HEADS * HEAD_DIM)


N_CHIP = N_DEV // 2
_MESH_ID = pl.DeviceIdType.MESH


def _place():
    x, y, c = lax.axis_index("x"), lax.axis_index("y"), lax.axis_index("c")
    chips = [(1 - x, y), (x, 1 - y), (1 - x, 1 - y)]
    return x, y, c, 2 * x + y, chips


def _sibling_swap(arrays, *, name):
    n = len(arrays)

    def body(*refs):
        ins, outs = refs[:n], refs[n:2 * n]
        send_sems, recv_sems = refs[2 * n:]
        x, y, c, _, _ = _place()
        copies = [pltpu.make_async_remote_copy(
            src_ref=ins[a].at[:, 1 - c], dst_ref=outs[a], send_sem=send_sems.at[a], recv_sem=recv_sems.at[a],
            device_id=(x, y, 1 - c), device_id_type=_MESH_ID) for a in range(n)]
        for cp in copies:
            cp.start()
        for cp in copies:
            cp.wait()

    any_spec = pl.BlockSpec(memory_space=pl.ANY)
    return pl.pallas_call(
        body, in_specs=[any_spec] * n, out_specs=[any_spec] * n,
        out_shape=[jax.ShapeDtypeStruct((N_CHIP,) + a.shape[2:], a.dtype) for a in arrays],
        scratch_shapes=[pltpu.SemaphoreType.DMA((n,)), pltpu.SemaphoreType.DMA((n,))],
        name=name, compiler_params=pltpu.CompilerParams(has_side_effects=True),
    )(*arrays)


def _pair_add(mine, theirs, core, *, name):
    _, _, R, C = mine.shape
    tr = max(t for t in range(16, R + 1, 16) if R % t == 0 and t * C <= 512 * 1024)

    def body(core_ref, m_ref, t_ref, o_ref):
        del core_ref
        o_ref[...] = (m_ref[...].astype(_F32) + t_ref[...].astype(_F32)).astype(o_ref.dtype)

    grid_spec = pltpu.PrefetchScalarGridSpec(
        num_scalar_prefetch=1, grid=(N_CHIP, R // tr),
        in_specs=[pl.BlockSpec((None, None, tr, C), lambda q, i, core: (q, core[0], i, 0)),
                  pl.BlockSpec((None, tr, C), lambda q, i, core: (q, i, 0))],
        out_specs=pl.BlockSpec((None, tr, C), lambda q, i, core: (q, i, 0)))
    return pl.pallas_call(
        body, grid_spec=grid_spec, out_shape=jax.ShapeDtypeStruct((N_CHIP, R, C), mine.dtype), name=name,
        compiler_params=_cparams(dimension_semantics=("parallel", "parallel")),
    )(core.reshape(1), mine, theirs)


_HBM = pl.BlockSpec(memory_space=pltpu.HBM)
_SEM = pl.BlockSpec(memory_space=pltpu.SEMAPHORE)
_TOKEN = pl.BlockSpec(memory_space=pltpu.VMEM)
_DATAFLOW = pltpu.SideEffectType.DATAFLOW_SIDE_EFFECTING


def _hbm(a):
    return pltpu.HBM(a.shape, a.dtype)


def _token_shape():
    return jax.ShapeDtypeStruct((8, 128), _F32)


def _dev_slot(px, py, pc):
    return 4 * px + 2 * py + pc


def _gather_start(blocks, landings, *, name):
    n = len(blocks)

    def body(*refs):
        ins, lands = refs[:n], refs[n:2 * n]
        send_sems, d2d_sems, ici_sems = refs[2 * n:2 * n + 3]
        token = refs[-1]
        x, y, c, _, chips = _place()
        for a in range(n):
            dst = lands[a].at[_dev_slot(x, y, c)]
            pltpu.make_async_remote_copy(src_ref=ins[a], dst_ref=dst, send_sem=send_sems.at[4 * a], recv_sem=d2d_sems.at[a],
                                         device_id=(x, y, 1 - c), device_id_type=_MESH_ID).start()
            for j, chip in enumerate(chips):
                pltpu.make_async_remote_copy(src_ref=ins[a], dst_ref=dst, send_sem=send_sems.at[4 * a + 1 + j],
                                             recv_sem=ici_sems.at[3 * a + j], device_id=(*chip, c),
                                             device_id_type=_MESH_ID).start()
        token[...] = jnp.zeros_like(token)

    res = pl.pallas_call(
        body, name=name, in_specs=[_HBM] * (2 * n),
        out_shape=(pltpu.SemaphoreType.DMA((4 * n,)), pltpu.SemaphoreType.DMA((n,)), pltpu.SemaphoreType.DMA((3 * n,)),
                   *[_hbm(b) for b in blocks], *[_hbm(b) for b in landings], _token_shape()),
        out_specs=(_SEM, _SEM, _SEM, *[_HBM] * (2 * n), _TOKEN),
        input_output_aliases={i: 3 + i for i in range(2 * n)},
        compiler_params=pltpu.CompilerParams(has_side_effects=_DATAFLOW),
    )(*[pltpu.with_memory_space_constraint(b, pltpu.HBM) for b in blocks],
      *[pltpu.with_memory_space_constraint(b, pltpu.HBM) for b in landings])
    return res[0], res[1], res[2], list(res[3:3 + n]), list(res[3 + n:3 + 2 * n]), res[-1]


def _gather_forward(landings, ici_sems, first, after, *, name):
    n = len(landings)

    def body(*refs):
        lands = refs[:n]
        ici = refs[n]
        f_send, f_recv = refs[n + 2], refs[n + 3]
        token = refs[-1]
        x, y, c, _, chips = _place()
        for a in range(n):
            for j, chip in enumerate(chips):
                blk = lands[a].at[_dev_slot(*chip, c)]
                pltpu.make_async_remote_copy(src_ref=blk, dst_ref=blk, send_sem=f_send.at[3 * a + j],
                                             recv_sem=ici.at[3 * (first + a) + j], device_id=(*chip, c),
                                             device_id_type=_MESH_ID).wait_recv()
                pltpu.make_async_remote_copy(src_ref=blk, dst_ref=blk, send_sem=f_send.at[3 * a + j], recv_sem=f_recv.at[3 * a + j],
                                             device_id=(x, y, 1 - c), device_id_type=_MESH_ID).start()
        token[...] = jnp.zeros_like(token)

    res = pl.pallas_call(
        body, name=name, in_specs=[_HBM] * n + [_SEM, pl.BlockSpec(memory_space=pl.ANY)],
        out_shape=(pltpu.SemaphoreType.DMA((3 * n,)), pltpu.SemaphoreType.DMA((3 * n,)), *[_hbm(b) for b in landings], _token_shape()),
        out_specs=(_SEM, _SEM, *[_HBM] * n, _TOKEN),
        input_output_aliases={i: 2 + i for i in range(n)},
        compiler_params=pltpu.CompilerParams(has_side_effects=_DATAFLOW),
    )(*landings, ici_sems, after)
    return res[0], res[1], list(res[2:2 + n]), res[-1]


def _gather_wait(blocks, landings, send_sems, d2d_sems, first, f_send, f_recv, after, *, name):
    n = len(landings)

    def body(*refs):
        ins, lands = refs[:n], refs[n:2 * n]
        send, d2d, fs, fr = refs[2 * n:2 * n + 4]
        x, y, c, _, chips = _place()
        me = (x, y, c)
        for a in range(n):
            own = lands[a].at[_dev_slot(x, y, 1 - c)]
            g = first + a
            pltpu.make_async_remote_copy(src_ref=ins[a], dst_ref=own, send_sem=send.at[4 * g], recv_sem=d2d.at[g],
                                         device_id=me, device_id_type=_MESH_ID).wait_recv()
            for j, chip in enumerate(chips):
                blk = lands[a].at[_dev_slot(*chip, 1 - c)]
                pltpu.make_async_remote_copy(src_ref=blk, dst_ref=blk, send_sem=fs.at[3 * a + j], recv_sem=fr.at[3 * a + j],
                                             device_id=me, device_id_type=_MESH_ID).wait_recv()
            for k in range(4):
                pltpu.make_async_remote_copy(src_ref=ins[a], dst_ref=own, send_sem=send.at[4 * g + k], recv_sem=d2d.at[g],
                                             device_id=me, device_id_type=_MESH_ID).wait_send()
            for j in range(3):
                pltpu.make_async_remote_copy(src_ref=own, dst_ref=own, send_sem=fs.at[3 * a + j], recv_sem=fr.at[3 * a + j],
                                             device_id=me, device_id_type=_MESH_ID).wait_send()

    res = pl.pallas_call(
        body, name=name, in_specs=[_HBM] * (2 * n) + [_SEM] * 4 + [pl.BlockSpec(memory_space=pl.ANY)],
        out_shape=(*[_hbm(b) for b in blocks], *[_hbm(b) for b in landings]), out_specs=tuple([_HBM] * (2 * n)),
        input_output_aliases={i: i for i in range(2 * n)},
        compiler_params=pltpu.CompilerParams(has_side_effects=_DATAFLOW),
    )(*blocks, *landings, send_sems, d2d_sems, f_send, f_recv, after)
    return list(res[n:])


def _swap_start(mine, landings, *, name):
    n = len(mine)

    def body(*refs):
        ins, lands = refs[:n], refs[n:2 * n]
        send_sems, recv_sems = refs[2 * n:2 * n + 2]
        token = refs[-1]
        x, y, c, _, _ = _place()
        for a in range(n):
            pltpu.make_async_remote_copy(src_ref=ins[a].at[:, 1 - c], dst_ref=lands[a], send_sem=send_sems.at[a],
                                         recv_sem=recv_sems.at[a], device_id=(x, y, 1 - c), device_id_type=_MESH_ID).start()
        token[...] = jnp.zeros_like(token)

    res = pl.pallas_call(
        body, name=name, in_specs=[_HBM] * (2 * n),
        out_shape=(pltpu.SemaphoreType.DMA((n,)), pltpu.SemaphoreType.DMA((n,)),
                   *[_hbm(b) for b in mine], *[_hbm(b) for b in landings], _token_shape()),
        out_specs=(_SEM, _SEM, *[_HBM] * (2 * n), _TOKEN),
        input_output_aliases={i: 2 + i for i in range(2 * n)},
        compiler_params=pltpu.CompilerParams(has_side_effects=_DATAFLOW),
    )(*[pltpu.with_memory_space_constraint(b, pltpu.HBM) for b in mine],
      *[pltpu.with_memory_space_constraint(b, pltpu.HBM) for b in landings])
    return res[0], res[1], list(res[2:2 + n]), list(res[2 + n:2 + 2 * n]), res[-1]


def _swap_wait(mine, landings, send_sems, recv_sems, after, *, name):
    n = len(mine)

    def body(*refs):
        ins, lands = refs[:n], refs[n:2 * n]
        send, recv = refs[2 * n:2 * n + 2]
        x, y, c, _, _ = _place()
        for a in range(n):
            cp = pltpu.make_async_remote_copy(src_ref=ins[a].at[:, 1 - c], dst_ref=lands[a], send_sem=send.at[a],
                                              recv_sem=recv.at[a], device_id=(x, y, c), device_id_type=_MESH_ID)
            cp.wait_recv()
            cp.wait_send()

    res = pl.pallas_call(
        body, name=name, in_specs=[_HBM] * (2 * n) + [_SEM] * 2 + [pl.BlockSpec(memory_space=pl.ANY)],
        out_shape=(*[_hbm(b) for b in mine], *[_hbm(b) for b in landings]), out_specs=tuple([_HBM] * (2 * n)),
        input_output_aliases={i: i for i in range(2 * n)},
        compiler_params=pltpu.CompilerParams(has_side_effects=_DATAFLOW),
    )(*mine, *landings, send_sems, recv_sems, after)
    return list(res[:n]), list(res[n:])


def _chip_exchange_start(sums, landings, *, name):
    n = len(sums)

    def body(*refs):
        ins, lands = refs[:n], refs[n:2 * n]
        send_sems, recv_sems = refs[2 * n:2 * n + 2]
        token = refs[-1]
        _, _, c, p, chips = _place()
        for a in range(n):
            for j, (qx, qy) in enumerate(chips):
                pltpu.make_async_remote_copy(src_ref=ins[a].at[2 * qx + qy], dst_ref=lands[a].at[p], send_sem=send_sems.at[3 * a + j],
                                             recv_sem=recv_sems.at[3 * a + j], device_id=(qx, qy, c), device_id_type=_MESH_ID).start()
        token[...] = jnp.zeros_like(token)

    res = pl.pallas_call(
        body, name=name, in_specs=[_HBM] * (2 * n),
        out_shape=(pltpu.SemaphoreType.DMA((3 * n,)), pltpu.SemaphoreType.DMA((3 * n,)),
                   *[_hbm(b) for b in sums], *[_hbm(b) for b in landings], _token_shape()),
        out_specs=(_SEM, _SEM, *[_HBM] * (2 * n), _TOKEN),
        input_output_aliases={i: 2 + i for i in range(2 * n)},
        compiler_params=pltpu.CompilerParams(has_side_effects=_DATAFLOW),
    )(*[pltpu.with_memory_space_constraint(b, pltpu.HBM) for b in sums],
      *[pltpu.with_memory_space_constraint(b, pltpu.HBM) for b in landings])
    return res[0], res[1], list(res[2:2 + n]), list(res[2 + n:2 + 2 * n]), res[-1]


def _chip_exchange_wait(sums, landings, send_sems, recv_sems, after, *, name):
    n = len(sums)

    def body(*refs):
        ins, lands = refs[:n], refs[n:2 * n]
        send, recv = refs[2 * n:2 * n + 2]
        x, y, c, _, chips = _place()
        for a in range(n):
            for j, (qx, qy) in enumerate(chips):
                q = 2 * qx + qy
                cp = pltpu.make_async_remote_copy(src_ref=ins[a].at[q], dst_ref=lands[a].at[q], send_sem=send.at[3 * a + j],
                                                  recv_sem=recv.at[3 * a + j], device_id=(x, y, c), device_id_type=_MESH_ID)
                cp.wait_recv()
                cp.wait_send()

    res = pl.pallas_call(
        body, name=name, in_specs=[_HBM] * (2 * n) + [_SEM] * 2 + [pl.BlockSpec(memory_space=pl.ANY)] * len(after),
        out_shape=(*[_hbm(b) for b in sums], *[_hbm(b) for b in landings]), out_specs=tuple([_HBM] * (2 * n)),
        input_output_aliases={i: i for i in range(2 * n)},
        compiler_params=pltpu.CompilerParams(has_side_effects=_DATAFLOW),
    )(*sums, *landings, send_sems, recv_sems, *after)
    return list(res[:n]), list(res[n:])


_C1 = 1.0 - ADAM_B1 ** ADAM_STEP
_C2 = 1.0 - ADAM_B2 ** ADAM_STEP


def _adamw_math(g, w, m, v):
    m = ADAM_B1 * m + (1.0 - ADAM_B1) * g
    v = ADAM_B2 * v + (1.0 - ADAM_B2) * (g * g)
    delta = -ADAM_LR * ((m / _C1) / (jnp.sqrt(v / _C2) + ADAM_EPS) + ADAM_WD * w)
    return delta, m, v


def _adamw_reduce(landed, sums, chip, w, m, v, layer, prev, *, name):
    _, R, C = w.shape
    tr = max(t for t in range(16, R + 1, 16) if R % t == 0 and t * C <= 256 * 1024)

    def body(chip_ref, p_ref, own_ref, w_ref, m_ref, v_ref, *rest):
        g_ref, d_ref, nm_ref, nv_ref = rest[-4:]
        own = own_ref[...].astype(_F32)
        g = jnp.where(chip_ref[0] == 0, own, p_ref[0].astype(_F32))
        for q in range(1, N_CHIP):
            g = g + jnp.where(chip_ref[0] == q, own, p_ref[q].astype(_F32))
        d, nm, nv = _adamw_math(g, w_ref[...], m_ref[...], v_ref[...])
        g_ref[...] = g
        d_ref[...] = d
        nm_ref[...] = nm
        nv_ref[...] = nv

    blk = pl.BlockSpec((None, tr, C), lambda i, chip: (layer, i, 0))
    shape = jax.ShapeDtypeStruct((DEPTH, R, C), _F32)
    kept = [] if prev is None else list(prev)
    grid_spec = pltpu.PrefetchScalarGridSpec(
        num_scalar_prefetch=1, grid=(R // tr,),
        in_specs=[pl.BlockSpec((N_CHIP, tr, C), lambda i, chip: (0, i, 0)),
                  pl.BlockSpec((None, tr, C), lambda i, chip: (chip[0], i, 0)), blk, blk, blk]
        + [pl.BlockSpec(memory_space=pl.ANY)] * len(kept),
        out_specs=[blk] * 4)
    return pl.pallas_call(
        body, grid_spec=grid_spec, out_shape=[shape] * 4, name=name,
        input_output_aliases={6 + k: k for k in range(len(kept))},
        compiler_params=_cparams(dimension_semantics=("parallel",)),
    )(chip.reshape(1), landed, sums, w, m, v, *kept)


_PACK_LANES = 128
_LAYER_ROWS = 248
_LB_ROWS = (A_HEADS * HEAD_DIM) // _PACK_LANES


def _small_reduce(parts, lb_logits, *, name):
    rows = DEPTH * _LAYER_ROWS

    def body(p_ref, lg_ref, o_ref):
        g = p_ref[0]
        for s in range(1, N_DEV):
            g = g + p_ref[s]
        o_ref[...] = g
        lg = lg_ref[...]
        e = jnp.exp(lg - jnp.max(lg, axis=0, keepdims=True))
        p = e / jnp.sum(e, axis=0, keepdims=True)
        d1 = g[_LAYER_ROWS:_LAYER_ROWS + _LB_ROWS, :] * p[0] * p[1]
        o_ref[0:_LB_ROWS, :] = -d1
        o_ref[_LAYER_ROWS:_LAYER_ROWS + _LB_ROWS, :] = d1

    return pl.pallas_call(
        body, out_shape=jax.ShapeDtypeStruct((rows, _PACK_LANES), _F32), name=name,
        compiler_params=_cparams(),
    )(parts, lb_logits.reshape(DEPTH, _LB_ROWS, _PACK_LANES))


def _adamw_small(g, w, m, v, *, name):
    def body(g_ref, w_ref, m_ref, v_ref, d_ref, nm_ref, nv_ref):
        d, nm, nv = _adamw_math(g_ref[...], w_ref[...], m_ref[...], v_ref[...])
        d_ref[...] = d
        nm_ref[...] = nm
        nv_ref[...] = nv

    shape = jax.ShapeDtypeStruct(g.shape, _F32)
    return pl.pallas_call(body, out_shape=[shape] * 3, name=name, compiler_params=_cparams())(g, w, m, v)


def _pack(vectors, rows):
    flat = jnp.concatenate([v.reshape(-1).astype(_F32) for v in vectors])
    return jnp.pad(flat, (0, rows * _PACK_LANES - flat.shape[0])).reshape(rows, _PACK_LANES)


def _unpack(packed, shapes):
    flat = packed.reshape(-1)
    out, at = [], 0
    for s in shapes:
        size = int(np.prod(s))
        out.append(flat[at:at + size].reshape(s))
        at += size
    return out


_BIG = ("w_in", "w_gate", "w_up", "w_out", "w_down")
_COLUMN_SHARDED = ("w_in", "w_gate", "w_up")


def _full_weight(name, g):
    if name == "conv_w":
        return g.transpose(1, 0, 2).reshape(g.shape[1], N_DEV * SHARD_COLS)
    if name in _BIG:
        return g.reshape(N_DEV * g.shape[1], g.shape[2])
    return g


class _WeightGather:
    def __init__(self, names, first, blocks, lands, sems, tag):
        self.names, self.first, self.blocks, self.lands, self.sems, self.tag = names, first, blocks, lands, sems, tag
        self.forwarded = None

    def forward(self, after):
        f_send, f_recv, self.lands, token = _gather_forward(self.lands, self.sems[2], self.first, after,
                                                            name=f"gather_forward_{self.tag}")
        self.forwarded = (f_send, f_recv)
        return token

    def wait(self, after):
        if self.forwarded is None:
            self.forward(after)
        got = _gather_wait(self.blocks, self.lands, self.sems[0], self.sems[1], self.first, *self.forwarded, after,
                           name=f"gather_wait_{self.tag}")
        return {n: _full_weight(n, g) for n, g in zip(self.names, got)}


def _start_gathers(groups, me, name):
    blocks = [b for _, _, bs in groups for b in bs]
    landings = [lax.dynamic_update_index_in_dim(lax.empty((N_DEV,) + b.shape, b.dtype), b[None], me, 0) for b in blocks]
    send, d2d, ici, blocks, landings, token = _gather_start(blocks, landings, name=name)
    out, first = [], 0
    for tag, names, bs in groups:
        k = len(bs)
        out.append(_WeightGather(names, first, blocks[first:first + k], landings[first:first + k], (send, d2d, ici), tag))
        first += k
    return out, token


class _LayerWeights:
    def __init__(self, ready, pending=(), forwards=(), tokens=()):
        self.ready, self.pending, self.forwards, self._tokens = dict(ready), list(pending), list(forwards), list(tokens)

    def at(self, point, after):
        for when, gather in self.forwards:
            if when == point:
                self._tokens.append(gather.forward(after))

    def tokens(self):
        out, self._tokens = self._tokens, []
        return out

    def get(self, name, after):
        if name not in self.ready:
            group, = [g for g in self.pending if name in g.names]
            self.ready.update(group.wait(after))
        return self.ready[name]


def _layer_fwd(x, xb, ws, lb_logits, a_norm_w, c_sink, ln1_g, ln1_b, conv_b, ln2_g, ln2_b, tabs, l):
    proj = _mm(xb, ws.get("w_in", xb), tb=True, **_TILE_WIDE_N, after=ws.tokens(), name=f"proj_{l}")
    o_a, raw, states = _hgrn_fwd(proj, lb_logits, a_norm_w, l, name=f"hgrn_fwd_{l}")
    ws.at("hgrn", o_a)
    o_b, lse_b = _band_fwd(proj, tabs, name=f"dilated_fwd_{l}", **_DILATED)
    o_c, lse_c = _band_fwd(proj, tabs, sink=c_sink, name=f"swa_fwd_{l}", **_SWA)
    ws.at("swa", o_c)
    mixed = jnp.concatenate([o_a, o_b, o_c], axis=1).astype(_MXU_DTYPE)
    y = _mm(mixed, ws.get("w_out", mixed), **_TILE_MIX, after=ws.tokens(), name=f"mix_out_{l}")
    z1, x1, x1b = _ln_fwd(x, y, ln1_g, ln1_b, name=f"ln1_fwd_{l}")
    g = _mm(x1b, ws.get("w_gate", x1b), tb=True, **_TILE_WIDE_N, out_dtype=_ACT_DTYPE, name=f"ffn_gate_{l}")
    u = _mm(x1b, ws.get("w_up", x1b), tb=True, **_TILE_WIDE_N, out_dtype=_ACT_DTYPE, name=f"ffn_up_{l}")
    ws.at("up", u)
    hb = _conv_gate_fwd(g, u, ws.get("conv_w", u), conv_b, name=f"conv_gate_fwd_{l}")
    y2 = _mm(hb, ws.get("w_down", hb), **_TILE_WIDE_K, after=ws.tokens(), name=f"ffn_down_{l}")
    ws.at("down", y2)
    z2, x2, x2b = _ln_fwd(x1, y2, ln2_g, ln2_b, name=f"ln2_fwd_{l}")
    res = dict(xb=xb, proj=proj, raw=raw, states=states, o_b=o_b, lse_b=lse_b, o_c=o_c, lse_c=lse_c,
               mixed=mixed, z1=z1, x1b=x1b, g=g, u=u, hb=hb, z2=z2)
    return x2, x2b, res


class _GradExchange:
    def __init__(self, core, chip):
        self.core, self.chip, self.groups, self.swapping, self._tokens = core, chip, [], [], []

    def launch(self, names, slabs, l, tag, behind):
        mine = [s.reshape((N_CHIP, 2) + s.shape[1:]) for s in slabs]
        if behind:
            landings = [lax.empty((N_CHIP,) + m.shape[2:], m.dtype) for m in mine]
            send, recv, mine, landings, token = _swap_start(mine, landings, name=f"swap_start_{tag}")
            self.swapping.append((names, l, tag, send, recv, mine, landings))
            self._tokens.append(token)
        else:
            self._exchange(names, l, tag, mine, _sibling_swap(mine, name=f"swap_grads_{tag}"))

    def advance(self, after):
        for names, l, tag, send, recv, mine, landings in self.swapping:
            mine, theirs = _swap_wait(mine, landings, send, recv, after, name=f"swap_wait_{tag}")
            self._exchange(names, l, tag, mine, theirs)
        self.swapping = []

    def _exchange(self, names, l, tag, mine, theirs):
        sums = [_pair_add(a, b, self.core, name=f"pair_add_{n}_{l}") for n, a, b in zip(names, mine, theirs)]
        landings = [lax.empty(s.shape, s.dtype) for s in sums]
        send, recv, sums, landings, token = _chip_exchange_start(sums, landings, name=f"exchange_start_{tag}")
        self.groups.append((names, l, tag, send, recv, sums, landings))
        self._tokens.append(token)

    def tokens(self):
        out, self._tokens = self._tokens, []
        return out

    def finish(self, weights, mom1, mom2, after):
        out = {}
        after = list(after) + self.tokens()
        for names, l, tag, send, recv, sums, landings in self.groups:
            sums, landings = _chip_exchange_wait(sums, landings, send, recv, after, name=f"exchange_wait_{tag}")
            for n, s, landed in zip(names, sums, landings):
                out[n] = _adamw_reduce(landed, s, self.chip, weights[n], mom1[n], mom2[n], l, out.get(n), name=f"adamw_{n}_{l}")
            after = [out[n][0] for n in names]
        return out


def _layer_bwd(dx2, res, w, lb_logits, a_norm_w, c_sink, ln1_g, conv_b, ln2_g, tabs, exchange, l):
    dz2, dz2b, d_ln2_g, d_ln2_b = _ln_bwd(res["z2"], dx2, None, ln2_g, name=f"ln2_bwd_{l}")
    exchange.advance(dz2b)
    dh = _mm(dz2b, w["w_down"], tb=True, **_TILE_WIDE_N, out_dtype=_ACT_DTYPE, after=exchange.tokens(),
             name=f"ffn_down_dx_{l}")
    d_w_down = _mm(res["hb"], dz2b, ta=True, **_TILE_WIDE_M, out_dtype=_GRAD_DTYPE, name=f"ffn_down_dw_{l}")
    dg, du, d_conv_w, d_conv_b = _conv_gate_bwd(dh, res["g"], res["u"], w["conv_w"], conv_b, name=f"conv_gate_bwd_{l}")
    t = _mm(dg, w["w_gate"], **_TILE_WIDE_K, name=f"ffn_gate_dx_{l}")
    dx1 = _mm(du, w["w_up"], **_TILE_WIDE_K, add=t, name=f"ffn_up_dx_{l}")
    d_w_gate = _mm(dg, res["x1b"], ta=True, **_TILE_WIDE_M, out_dtype=_GRAD_DTYPE, name=f"ffn_gate_dw_{l}")
    d_w_up = _mm(du, res["x1b"], ta=True, **_TILE_WIDE_M, out_dtype=_GRAD_DTYPE, name=f"ffn_up_dw_{l}")
    dz1, dz1b, d_ln1_g, d_ln1_b = _ln_bwd(res["z1"], dx1, dz2, ln1_g, name=f"ln1_bwd_{l}")
    d_w_out = _mm(res["mixed"], dz1b, ta=True, **_TILE_MIX, out_dtype=_GRAD_DTYPE, name=f"mix_out_dw_{l}")
    exchange.launch(("w_down", "w_gate", "w_up", "w_out"),
                    [d.reshape(N_DEV, d.shape[0] // N_DEV, D_MODEL) for d in (d_w_down, d_w_gate, d_w_up, d_w_out)],
                    l, f"ffn_{l}", True)
    dmixed = _mm(dz1b, w["w_out"], tb=True, **_TILE_MIX, after=exchange.tokens(), name=f"mix_out_dx_{l}")
    dq_a, df_a, di_a, dg_a, d_norm_w, d_lb = _hgrn_bwd(res["proj"], lb_logits, a_norm_w, res["raw"], res["states"],
                                                      dmixed, l, name=f"hgrn_bwd_{l}")
    exchange.advance(dq_a)
    dq_b, dk_b, dv_b = _band_bwd(res["proj"], tabs, dmixed, res["o_b"], res["lse_b"], do0=A_HEADS, after=exchange.tokens(),
                                 name=f"dilated_bwd_{l}", **_DILATED)
    dq_c, dk_c, dv_c, d_sink = _band_bwd(res["proj"], tabs, dmixed, res["o_c"], res["lse_c"], do0=A_HEADS + B_HEADS,
                                         sink=c_sink, name=f"swa_bwd_{l}", **_SWA)
    dproj = jnp.concatenate([dq_a, df_a, di_a, dg_a, dq_b, dk_b, dv_b, dq_c, dk_c, dv_c], axis=1)
    d_w_in = _mm(dproj, res["xb"], ta=True, **_TILE_WIDE_M, out_dtype=_GRAD_DTYPE, name=f"proj_dw_{l}")
    exchange.launch(("w_in",), [d_w_in.reshape(N_DEV, SHARD_COLS, D_MODEL)], l, f"mix_{l}", l > 0)
    dx = _mm(dproj, w["w_in"], **_TILE_WIDE_K, add=dz1, add_scale=ALPHA, after=exchange.tokens(), name=f"proj_dx_{l}")
    small = [d_lb, d_norm_w, jnp.pad(d_sink, (0, _PACK_LANES - C_HEADS)), d_ln1_g, d_ln1_b, d_ln2_g, d_ln2_b, d_conv_b,
             d_conv_w]
    return dx, small


def kernel(x, w_in, lb_logits, a_norm_w, c_sinks, w_out, ln1_g, ln1_b, w_gate, w_up, conv_w, conv_b, w_down, ln2_g, ln2_b, loss_target, m_w_in, m_lb_logits, m_a_norm_w, m_c_sinks, m_w_out, m_ln1_g, m_ln1_b, m_w_gate, m_w_up, m_conv_w, m_conv_b, m_w_down, m_ln2_g, m_ln2_b, v_w_in, v_lb_logits, v_a_norm_w, v_c_sinks, v_w_out, v_ln1_g, v_ln1_b, v_w_gate, v_w_up, v_conv_w, v_conv_b, v_w_down, v_ln2_g, v_ln2_b):
    weights = dict(w_in=w_in, lb_logits=lb_logits, a_norm_w=a_norm_w, c_sinks=c_sinks, w_out=w_out, ln1_g=ln1_g, ln1_b=ln1_b,
                   w_gate=w_gate, w_up=w_up, conv_w=conv_w, conv_b=conv_b, w_down=w_down, ln2_g=ln2_g, ln2_b=ln2_b)
    mom1 = dict(w_in=m_w_in, lb_logits=m_lb_logits, a_norm_w=m_a_norm_w, c_sinks=m_c_sinks, w_out=m_w_out, ln1_g=m_ln1_g,
                ln1_b=m_ln1_b, w_gate=m_w_gate, w_up=m_w_up, conv_w=m_conv_w, conv_b=m_conv_b, w_down=m_w_down, ln2_g=m_ln2_g,
                ln2_b=m_ln2_b)
    mom2 = dict(w_in=v_w_in, lb_logits=v_lb_logits, a_norm_w=v_a_norm_w, c_sinks=v_c_sinks, w_out=v_w_out, ln1_g=v_ln1_g,
                ln1_b=v_ln1_b, w_gate=v_w_gate, w_up=v_w_up, conv_w=v_conv_w, conv_b=v_conv_b, w_down=v_w_down, ln2_g=v_ln2_g,
                ln2_b=v_ln2_b)
    core = lax.axis_index("c").astype(jnp.int32)
    me = 4 * lax.axis_index("x") + 2 * lax.axis_index("y") + core
    tabs = _rope_tables()

    chip = (2 * lax.axis_index("x") + lax.axis_index("y")).astype(jnp.int32)

    def as_slabs(d):
        return {n: jnp.swapaxes(d[n], 1, 2) if n in _COLUMN_SHARDED else d[n] for n in _BIG}

    w_views = as_slabs(weights)

    def block(n, l, after=()):
        return conv_w[l] if n == "conv_w" else _cast_layer(w_views[n], l, after=after, name=f"cast_{n}_{l}")

    (in0,), started_first = _start_gathers([("w_in_0", ("w_in",), [block("w_in", 0)])], me, "gather_start_first")
    order = [(("w_out",), 0), (("w_gate", "w_up", "conv_w"), 0), (("w_down",), 0),
             (("w_in",), 1), (("w_out",), 1), (("w_gate", "w_up", "conv_w"), 1), (("w_down",), 1)]
    gathers, started = _start_gathers([(f"{names[0]}_{l}", names, [block(n, l, [started_first]) for n in names])
                                       for names, l in order], me, "gather_start_rest")
    out0, ffn0, down0, in1, out1, ffn1, down1 = gathers
    layer_ws = [_LayerWeights(in0.wait(started), [out0, ffn0, down0],
                              [("hgrn", out0), ("swa", ffn0), ("up", down0), ("down", in1)]),
                _LayerWeights({}, [in1, out1, ffn1, down1], [("hgrn", out1), ("swa", ffn1), ("up", down1)])]

    xs = x[0]
    xb = xs.astype(_MXU_DTYPE)
    saved = []
    for l in range(DEPTH):
        xs, xb, res = _layer_fwd(xs, xb, layer_ws[l], lb_logits, a_norm_w[l], c_sinks[l], ln1_g[l], ln1_b[l], conv_b[l],
                                 ln2_g[l], ln2_b[l], tabs, l)
        saved.append(res)
    loss_part, dx = _loss_head(xs, loss_target[0], name="loss_head")
    loss = lax.psum(loss_part, ("x", "y", "c"))

    exchange = _GradExchange(core, chip)
    small_parts = [None] * DEPTH
    for l in reversed(range(DEPTH)):
        dx, small = _layer_bwd(dx, saved[l], layer_ws[l].ready, lb_logits, a_norm_w[l], c_sinks[l], ln1_g[l], conv_b[l],
                               ln2_g[l], tabs, exchange, l)
        small_parts[l] = _pack(small, _LAYER_ROWS)
    (small_gather,), small_started = _start_gathers(
        [("small_grads", ("small",), [jnp.concatenate(small_parts, axis=0)])], me, "gather_start_small")
    updated = exchange.finish(w_views, as_slabs(mom1), as_slabs(mom2), [dx, small_started])
    gathered = small_gather.wait(updated["w_in"][0])["small"]
    updated = {n: tuple(jnp.swapaxes(t, 1, 2) for t in u) if n in _COLUMN_SHARDED else u for n, u in updated.items()}
    g_small = _small_reduce(gathered, lb_logits, name="small_grads")

    per_layer = [(A_HEADS * HEAD_DIM,), (HEAD_DIM,), (_PACK_LANES,), (D_MODEL,), (D_MODEL,), (D_MODEL,), (D_MODEL,), (D_FF,),
                 (3, D_FF)]
    names = ("lb_logits", "a_norm_w", "c_sinks", "ln1_g", "ln1_b", "ln2_g", "ln2_b", "conv_b", "conv_w")
    grads = {n: [] for n in names}
    for l in range(DEPTH):
        for n, t in zip(names, _unpack(g_small[l * _LAYER_ROWS:(l + 1) * _LAYER_ROWS], per_layer)):
            grads[n].append(t)
    grads = {n: jnp.stack(t) for n, t in grads.items()}
    grads["c_sinks"] = grads["c_sinks"][:, :C_HEADS]
    grads["conv_w"] = lax.dynamic_slice_in_dim(grads["conv_w"], me * SHARD_COLS, SHARD_COLS, axis=2)
    shapes = [grads[n].shape for n in names]
    rows = -(-sum(int(np.prod(s)) for s in shapes) // (8 * _PACK_LANES)) * 8
    d_s, m_s, v_s = _adamw_small(_pack([grads[n] for n in names], rows), _pack([weights[n] for n in names], rows),
                                 _pack([mom1[n] for n in names], rows), _pack([mom2[n] for n in names], rows),
                                 name="adamw_small")
    delta = dict(zip(names, _unpack(d_s, shapes)))
    new_m = dict(zip(names, _unpack(m_s, shapes)))
    new_v = dict(zip(names, _unpack(v_s, shapes)))
    for n in _BIG:
        grads[n], delta[n], new_m[n], new_v[n] = updated[n]

    order = ("w_in", "lb_logits", "a_norm_w", "c_sinks", "w_out", "ln1_g", "ln1_b", "w_gate", "w_up", "conv_w", "conv_b",
             "w_down", "ln2_g", "ln2_b")
    return (loss, dx[None], *[grads[n] for n in order], *[delta[n] for n in order], *[new_m[n] for n in order],
            *[new_v[n] for n in order])
```

```python
import functools

import jax
import jax.numpy as jnp
import numpy as np
from jax import lax
from jax.experimental import pallas as pl
from jax.experimental.pallas import tpu as pltpu

D_MODEL = 2048
SEQ = 2048
DEPTH = 2
HEAD_DIM = 128
A_HEADS = 4
B_HEADS = 6
C_HEADS = 6
C_KV_HEADS = 2
A_CHUNK = 16
DILATIONS = (1, 4, 16)
BLOCK = 128
ROPE_THETA = 500000.0
ROPE_DIM = 32
D_FF = 5632
IN_WIDTH = 5632
LN_EPS = 1e-5
ALPHA = (2 * DEPTH) ** 0.25
N_DEV = 8
SHARD_COLS = IN_WIDTH // N_DEV

ADAM_LR = 0.001
ADAM_B1 = 0.9
ADAM_B2 = 0.999
ADAM_EPS = 1e-08
ADAM_WD = 0.01
ADAM_STEP = 10

A_COLS = 16
QKV_COLS = 28
QB0, KB0, VB0, QC0, KC0, VC0 = 0, 6, 12, 18, 24, 26

_MXU_DTYPE = jnp.bfloat16
_GRAD_DTYPE = jnp.bfloat16
_ACT_DTYPE = jnp.bfloat16
_NEG = -1e30
_VMEM_LIMIT = 56 * 2 ** 20

_F32 = jnp.float32


def _sigmoid(x):
    return 0.5 * jnp.tanh(0.5 * x) + 0.5


def _cparams(**kw):
    return pltpu.CompilerParams(vmem_limit_bytes=_VMEM_LIMIT, **kw)


_TILE_MIX = dict(tm=1024, tn=1024)
_TILE_WIDE_K = dict(tm=1024, tn=512)
_TILE_WIDE_N = dict(tm=1024, tn=1408)
_TILE_WIDE_M = dict(tm=1408, tn=1024)


def _mm(a, b, *, ta=False, tb=False, tm, tn, out_dtype=_F32, add=None, add_scale=1.0, after=(), name):
    K = a.shape[0] if ta else a.shape[1]
    M = a.shape[1] if ta else a.shape[0]
    N = b.shape[0] if tb else b.shape[1]
    assert (b.shape[1] if tb else b.shape[0]) == K and M % tm == 0 and N % tn == 0
    dn = (((0 if ta else 1,), (1 if tb else 0,)), ((), ()))

    def body(*refs):
        a_ref, b_ref = refs[:2]
        o_ref = refs[-1]
        r = lax.dot_general(a_ref[...], b_ref[...], dn, preferred_element_type=_F32)
        if add is not None:
            r = r + add_scale * refs[2][...]
        o_ref[...] = r.astype(o_ref.dtype)

    a_spec = pl.BlockSpec((K, tm), lambda i, j: (0, i)) if ta else pl.BlockSpec((tm, K), lambda i, j: (i, 0))
    b_spec = pl.BlockSpec((tn, K), lambda i, j: (j, 0)) if tb else pl.BlockSpec((K, tn), lambda i, j: (0, j))
    o_spec = pl.BlockSpec((tm, tn), lambda i, j: (i, j))
    in_specs = [a_spec, b_spec] + ([o_spec] if add is not None else []) + [pl.BlockSpec(memory_space=pl.ANY)] * len(after)
    args = (a, b) + ((add,) if add is not None else ()) + tuple(after)
    return pl.pallas_call(
        body, grid=(M // tm, N // tn), in_specs=in_specs, out_specs=o_spec,
        out_shape=jax.ShapeDtypeStruct((M, N), out_dtype), name=name,
        compiler_params=_cparams(dimension_semantics=("parallel", "parallel")),
    )(*args)


def _cast_layer(w, layer, *, after=(), name):
    _, R, C = w.shape
    tr = max(t for t in range(16, R + 1, 16) if R % t == 0 and t * C <= 512 * 1024)

    def body(w_ref, *rest):
        o_ref = rest[-1]
        o_ref[...] = w_ref[...].astype(o_ref.dtype)

    return pl.pallas_call(
        body, grid=(R // tr,),
        in_specs=[pl.BlockSpec((None, tr, C), lambda i: (layer, i, 0))] + [pl.BlockSpec(memory_space=pl.ANY)] * len(after),
        out_specs=pl.BlockSpec((tr, C), lambda i: (i, 0)), out_shape=jax.ShapeDtypeStruct((R, C), _MXU_DTYPE), name=name,
        compiler_params=_cparams(dimension_semantics=("parallel",)),
    )(w, *after)


def _concat_cols(pieces, *, name):
    tm = 512
    widths = [p.shape[1] for p in pieces]
    offs = np.cumsum([0] + widths)

    def body(*refs):
        o_ref = refs[-1]
        for p_ref, off, w in zip(refs[:-1], offs, widths):
            o_ref[:, off:off + w] = p_ref[...].astype(o_ref.dtype)

    return pl.pallas_call(
        body, grid=(SEQ // tm,), in_specs=[pl.BlockSpec((tm, w), lambda i: (i, 0)) for w in widths],
        out_specs=pl.BlockSpec((tm, int(offs[-1])), lambda i: (i, 0)),
        out_shape=jax.ShapeDtypeStruct((SEQ, int(offs[-1])), _MXU_DTYPE), name=name,
        compiler_params=_cparams(dimension_semantics=("parallel",)),
    )(*pieces)


def _ln_fwd(x, y, g, b, *, name):
    tm = 256

    def body(x_ref, y_ref, g_ref, b_ref, z_ref, o_ref, ob_ref):
        z = ALPHA * x_ref[...] + y_ref[...]
        mu = jnp.mean(z, axis=-1, keepdims=True)
        zc = z - mu
        var = jnp.mean(zc * zc, axis=-1, keepdims=True)
        o = zc * lax.rsqrt(var + LN_EPS) * g_ref[...] + b_ref[...]
        z_ref[...] = z
        o_ref[...] = o
        ob_ref[...] = o.astype(ob_ref.dtype)

    row = pl.BlockSpec((tm, D_MODEL), lambda i: (i, 0))
    vec = pl.BlockSpec((1, D_MODEL), lambda i: (0, 0))
    return pl.pallas_call(
        body, grid=(SEQ // tm,), in_specs=[row, row, vec, vec], out_specs=[row, row, row],
        out_shape=[jax.ShapeDtypeStruct((SEQ, D_MODEL), _F32), jax.ShapeDtypeStruct((SEQ, D_MODEL), _F32),
                   jax.ShapeDtypeStruct((SEQ, D_MODEL), _MXU_DTYPE)],
        name=name, compiler_params=_cparams(dimension_semantics=("parallel",)),
    )(x, y, g.reshape(1, D_MODEL), b.reshape(1, D_MODEL))


def _ln_bwd(z, d_a, d_res, g, *, name):
    tm = 256

    def body(*refs):
        if d_res is None:
            z_ref, da_ref, g_ref, dz_ref, dzb_ref, dg_ref, db_ref = refs
        else:
            z_ref, da_ref, dr_ref, g_ref, dz_ref, dzb_ref, dg_ref, db_ref = refs

        @pl.when(pl.program_id(0) == 0)
        def _():
            dg_ref[...] = jnp.zeros_like(dg_ref)
            db_ref[...] = jnp.zeros_like(db_ref)

        dout = da_ref[...]
        if d_res is not None:
            dout = dout + ALPHA * dr_ref[...]
        z = z_ref[...]
        mu = jnp.mean(z, axis=-1, keepdims=True)
        zc = z - mu
        var = jnp.mean(zc * zc, axis=-1, keepdims=True)
        rstd = lax.rsqrt(var + LN_EPS)
        xh = zc * rstd
        dxh = dout * g_ref[...]
        m1 = jnp.mean(dxh, axis=-1, keepdims=True)
        m2 = jnp.mean(dxh * xh, axis=-1, keepdims=True)
        dz = rstd * (dxh - m1 - xh * m2)
        dz_ref[...] = dz
        dzb_ref[...] = dz.astype(dzb_ref.dtype)
        dg_ref[0:1, :] += jnp.sum(dout * xh, axis=0, keepdims=True)
        db_ref[0:1, :] += jnp.sum(dout, axis=0, keepdims=True)

    row = pl.BlockSpec((tm, D_MODEL), lambda i: (i, 0))
    vec = pl.BlockSpec((1, D_MODEL), lambda i: (0, 0))
    acc = pl.BlockSpec((8, D_MODEL), lambda i: (0, 0))
    ins = [z, d_a] + ([d_res] if d_res is not None else []) + [g.reshape(1, D_MODEL)]
    in_specs = [row, row] + ([row] if d_res is not None else []) + [vec]
    dz, dzb, dg, db = pl.pallas_call(
        body, grid=(SEQ // tm,), in_specs=in_specs, out_specs=[row, row, acc, acc],
        out_shape=[jax.ShapeDtypeStruct((SEQ, D_MODEL), _F32), jax.ShapeDtypeStruct((SEQ, D_MODEL), _MXU_DTYPE),
                   jax.ShapeDtypeStruct((8, D_MODEL), _F32), jax.ShapeDtypeStruct((8, D_MODEL), _F32)],
        name=name, compiler_params=_cparams(dimension_semantics=("arbitrary",)),
    )(*ins)
    return dz, dzb, dg[0], db[0]


def _loss_head(y, target, *, name):
    tm = 256

    def body(y_ref, t_ref, d_ref, l_ref):
        e = y_ref[...] - t_ref[...]
        d_ref[...] = e * (1.0 / D_MODEL)

        @pl.when(pl.program_id(0) == 0)
        def _():
            l_ref[...] = jnp.zeros_like(l_ref)

        l_ref[...] += (0.5 / D_MODEL) * jnp.sum(e * e)

    row = pl.BlockSpec((tm, D_MODEL), lambda i: (i, 0))
    d, l = pl.pallas_call(
        body, grid=(SEQ // tm,), in_specs=[row, row], out_specs=[row, pl.BlockSpec((8, 128), lambda i: (0, 0))],
        out_shape=[jax.ShapeDtypeStruct((SEQ, D_MODEL), _F32), jax.ShapeDtypeStruct((8, 128), _F32)],
        name=name, compiler_params=_cparams(dimension_semantics=("arbitrary",)),
    )(y, target)
    return l[0, 0], d


_CONV_TN = 256


def _shift_down(v, k, rows):
    return jnp.where(rows >= k, pltpu.roll(v, k, axis=0), 0.0)


def _shift_up(v, k, rows):
    return jnp.where(rows < SEQ - k, pltpu.roll(v, SEQ - k, axis=0), 0.0)


def _conv_gate_fwd(g, u, conv_w, conv_b, *, name):
    def body(g_ref, u_ref, w_ref, b_ref, h_ref):
        gv = g_ref[...].astype(_F32)
        rows = lax.broadcasted_iota(jnp.int32, gv.shape, 0)
        w = w_ref[...]
        gc = b_ref[...] + w[2:3, :] * gv + w[1:2, :] * _shift_down(gv, 1, rows) + w[0:1, :] * _shift_down(gv, 2, rows)
        h_ref[...] = (gc * _sigmoid(gc) * u_ref[...].astype(_F32)).astype(h_ref.dtype)

    col = pl.BlockSpec((SEQ, _CONV_TN), lambda j: (0, j))
    return pl.pallas_call(
        body, grid=(D_FF // _CONV_TN,),
        in_specs=[col, col, pl.BlockSpec((3, _CONV_TN), lambda j: (0, j)), pl.BlockSpec((1, _CONV_TN), lambda j: (0, j))],
        out_specs=col, out_shape=jax.ShapeDtypeStruct((SEQ, D_FF), _MXU_DTYPE), name=name,
        compiler_params=_cparams(dimension_semantics=("parallel",)),
    )(g, u, conv_w, conv_b.reshape(1, D_FF))


def _conv_gate_bwd(dh, g, u, conv_w, conv_b, *, name):
    def body(dh_ref, g_ref, u_ref, w_ref, b_ref, dg_ref, du_ref, dw_ref, db_ref):
        gv = g_ref[...].astype(_F32)
        rows = lax.broadcasted_iota(jnp.int32, gv.shape, 0)
        w = w_ref[...]
        g1 = _shift_down(gv, 1, rows)
        g2 = _shift_down(gv, 2, rows)
        gc = b_ref[...] + w[2:3, :] * gv + w[1:2, :] * g1 + w[0:1, :] * g2
        sg = _sigmoid(gc)
        dh = dh_ref[...].astype(_F32)
        du_ref[...] = (dh * (gc * sg)).astype(du_ref.dtype)
        dgc = dh * u_ref[...].astype(_F32) * (sg * (1.0 + gc * (1.0 - sg)))
        dg = w[2:3, :] * dgc + w[1:2, :] * _shift_up(dgc, 1, rows) + w[0:1, :] * _shift_up(dgc, 2, rows)
        dg_ref[...] = dg.astype(dg_ref.dtype)
        dw_ref[0:1, :] = jnp.sum(dgc * g2, axis=0, keepdims=True)
        dw_ref[1:2, :] = jnp.sum(dgc * g1, axis=0, keepdims=True)
        dw_ref[2:3, :] = jnp.sum(dgc * gv, axis=0, keepdims=True)
        db_ref[...] = jnp.sum(dgc, axis=0, keepdims=True)

    col = pl.BlockSpec((SEQ, _CONV_TN), lambda j: (0, j))
    w3 = pl.BlockSpec((3, _CONV_TN), lambda j: (0, j))
    w1 = pl.BlockSpec((1, _CONV_TN), lambda j: (0, j))
    dg, du, dw, db = pl.pallas_call(
        body, grid=(D_FF // _CONV_TN,), in_specs=[col, col, col, w3, w1], out_specs=[col, col, w3, w1],
        out_shape=[jax.ShapeDtypeStruct((SEQ, D_FF), _MXU_DTYPE), jax.ShapeDtypeStruct((SEQ, D_FF), _MXU_DTYPE),
                   jax.ShapeDtypeStruct((3, D_FF), _F32), jax.ShapeDtypeStruct((1, D_FF), _F32)],
        name=name, compiler_params=_cparams(dimension_semantics=("parallel",)),
    )(dh, g, u, conv_w, conv_b.reshape(1, D_FF))
    return dg, du, dw, db[0]


def _rope_tables():
    half = ROPE_DIM // 2
    inv = ROPE_THETA ** (-jnp.arange(0, ROPE_DIM, 2, dtype=_F32) / ROPE_DIM)
    ang = jnp.arange(SEQ, dtype=_F32)[:, None] * inv[None, :]
    cos, sin = jnp.cos(ang), jnp.sin(ang)
    rest = HEAD_DIM - ROPE_DIM
    c = jnp.concatenate([cos, cos, jnp.ones((SEQ, rest), _F32)], axis=1)
    s1 = jnp.concatenate([-sin, jnp.zeros((SEQ, HEAD_DIM - half), _F32)], axis=1)
    s2 = jnp.concatenate([jnp.zeros((SEQ, half), _F32), sin, jnp.zeros((SEQ, rest), _F32)], axis=1)
    return c, s1, s2


def _rope_apply(x, c, s1, s2):
    return x * c + pltpu.roll(x, HEAD_DIM - ROPE_DIM // 2, axis=1) * s1 + pltpu.roll(x, ROPE_DIM // 2, axis=1) * s2


def _rope_transpose(d, c, s1, s2):
    half = ROPE_DIM // 2
    return d * c + pltpu.roll(d * s1, half, axis=1) + pltpu.roll(d * s2, HEAD_DIM - half, axis=1)


_NT = (((1,), (1,)), ((), ()))
_TN = (((0,), (0,)), ((), ()))
_SCALE = HEAD_DIM ** -0.5


def _band_scores(q, k2, n, lag_off):
    s = lax.dot_general(q, k2, _NT, preferred_element_type=_F32) * _SCALE
    row = lax.broadcasted_iota(jnp.int32, (BLOCK, 2 * BLOCK), 0)
    col = lax.broadcasted_iota(jnp.int32, (BLOCK, 2 * BLOCK), 1)
    front = (col >= row + lag_off) & (col < BLOCK) & (n > 0)
    own = (col >= BLOCK) & (col <= row + BLOCK)
    return jnp.where(front | own, s, _NEG)


_BAND_STEPS = SEQ // BLOCK


def _rows(start, d):
    if d == 1:
        return pl.ds(pl.multiple_of(start, BLOCK), BLOCK)
    return pl.ds(start, BLOCK, stride=d)


def _band_block(it, d):
    r, n = it % d, it // d
    span = BLOCK * d
    return n, _rows(r + n * span, d), _rows(r + jnp.maximum(n - 1, 0) * span, d)


def _band_fwd(proj, tabs, *, kv_heads, q_per_kv, q0, k0, v0, dilations, lag_off, sink, name):
    heads = kv_heads * q_per_kv

    def body(*refs):
        q_refs = refs[:q_per_kv]
        k_ref, v_ref, c_ref, s1_ref, s2_ref = refs[q_per_kv:q_per_kv + 5]
        rest = refs[q_per_kv + 5:]
        if sink is not None:
            sk_ref, rest = rest[0], rest[1:]
        o_ref, lse_ref, qs, ks, m_s, l_s, acc_s = rest
        c, s1, s2 = c_ref[...], s1_ref[...], s2_ref[...]
        ks[...] = _rope_apply(k_ref[...], c, s1, s2)
        for i in range(q_per_kv):
            qs[...] = _rope_apply(q_refs[i][...], c, s1, s2)
            for pi, d in enumerate(dilations):
                def step(it, carry, d=d, first=(pi == 0)):
                    n, cur, prev = _band_block(it, d)
                    q = qs[cur, :].astype(_MXU_DTYPE)
                    k2 = jnp.concatenate([ks[prev, :], ks[cur, :]], axis=0).astype(_MXU_DTYPE)
                    v2 = jnp.concatenate([v_ref[prev, :], v_ref[cur, :]], axis=0).astype(_MXU_DTYPE)
                    s = _band_scores(q, k2, n, lag_off)
                    m_b = jnp.max(s, axis=1, keepdims=True)
                    m_new = m_b if first else jnp.maximum(m_b, m_s[cur, :][:, 0:1])
                    p = jnp.exp(s - m_new)
                    l_new = jnp.sum(p, axis=1, keepdims=True)
                    acc = jnp.dot(p.astype(_MXU_DTYPE), v2, preferred_element_type=_F32)
                    if not first:
                        a = jnp.exp(m_s[cur, :][:, 0:1] - m_new)
                        l_new = l_new + a * l_s[cur, :][:, 0:1]
                        acc = acc + a * acc_s[cur, :]
                    m_s[cur, :] = jnp.broadcast_to(m_new, (BLOCK, HEAD_DIM))
                    l_s[cur, :] = jnp.broadcast_to(l_new, (BLOCK, HEAD_DIM))
                    acc_s[cur, :] = acc
                    return carry

                lax.fori_loop(0, _BAND_STEPS, step, 0, unroll=16)
            m, den = m_s[...], l_s[...]
            if sink is not None:
                sk = sk_ref[i]
                m_f = jnp.maximum(m, sk)
                a = jnp.exp(m - m_f)
                den = den * a + jnp.exp(sk - m_f)
                o = acc_s[...] * a / den
                m = m_f
            else:
                o = acc_s[...] / den
            o_ref[:, i * HEAD_DIM:(i + 1) * HEAD_DIM] = o
            lse_ref[:, i * HEAD_DIM:(i + 1) * HEAD_DIM] = m + jnp.log(den)

    col = (SEQ, HEAD_DIM)
    in_specs = [pl.BlockSpec(col, functools.partial(lambda g, i: (0, A_COLS + q0 + g * q_per_kv + i), i=i)) for i in range(q_per_kv)]
    in_specs += [pl.BlockSpec(col, lambda g: (0, A_COLS + k0 + g)), pl.BlockSpec(col, lambda g: (0, A_COLS + v0 + g))]
    in_specs += [pl.BlockSpec(col, lambda g: (0, 0))] * 3
    args = [proj] * (q_per_kv + 2) + list(tabs)
    if sink is not None:
        in_specs.append(pl.BlockSpec((q_per_kv, 1, HEAD_DIM), lambda g: (g, 0, 0)))
        args.append(jnp.broadcast_to(sink.reshape(heads, 1, 1), (heads, 1, HEAD_DIM)))
    o_spec = pl.BlockSpec((SEQ, q_per_kv * HEAD_DIM), lambda g: (0, g))
    shape = jax.ShapeDtypeStruct((SEQ, heads * HEAD_DIM), _F32)
    return pl.pallas_call(
        body, grid=(kv_heads,), in_specs=in_specs, out_specs=[o_spec, o_spec], out_shape=[shape, shape],
        scratch_shapes=[pltpu.VMEM(col, _F32)] * 5, name=name,
        compiler_params=_cparams(dimension_semantics=("parallel",)),
    )(*args)


def _band_bwd(proj, tabs, dmixed, o, lse, *, kv_heads, q_per_kv, q0, k0, v0, do0, dilations, lag_off, sink, after=(), name):
    heads = kv_heads * q_per_kv

    def body(*refs):
        q_refs = refs[:q_per_kv]
        k_ref, v_ref, c_ref, s1_ref, s2_ref = refs[q_per_kv:q_per_kv + 5]
        do_refs = refs[q_per_kv + 5:2 * q_per_kv + 5]
        o_ref, lse_ref = refs[2 * q_per_kv + 5:2 * q_per_kv + 7]
        rest = refs[2 * q_per_kv + 7:]
        if sink is not None:
            sk_ref, rest = rest[0], rest[1:]
            dq_ref, dk_ref, dv_ref, dsk_ref, qs, ks, dq_s, dk_s, dv_s = rest[len(after):]
        else:
            dq_ref, dk_ref, dv_ref, qs, ks, dq_s, dk_s, dv_s = rest[len(after):]
        c, s1, s2 = c_ref[...], s1_ref[...], s2_ref[...]
        ks[...] = _rope_apply(k_ref[...], c, s1, s2)
        dk_s[...] = jnp.zeros_like(dk_s)
        dv_s[...] = jnp.zeros_like(dv_s)
        for i in range(q_per_kv):
            hs = slice(i * HEAD_DIM, (i + 1) * HEAD_DIM)
            qs[...] = _rope_apply(q_refs[i][...], c, s1, s2)
            dq_s[...] = jnp.zeros_like(dq_s)
            do_ref = do_refs[i]
            for d in dilations:
                def step(it, carry, d=d, do_ref=do_ref, hs=hs):
                    n, cur, prev = _band_block(it, d)
                    q = qs[cur, :].astype(_MXU_DTYPE)
                    k2 = jnp.concatenate([ks[prev, :], ks[cur, :]], axis=0).astype(_MXU_DTYPE)
                    v2 = jnp.concatenate([v_ref[prev, :], v_ref[cur, :]], axis=0).astype(_MXU_DTYPE)
                    do = do_ref[cur, :]
                    delta = jnp.sum(do * o_ref[cur, hs], axis=1, keepdims=True)
                    lse_c = lse_ref[cur, hs][:, 0:1]
                    p = jnp.exp(_band_scores(q, k2, n, lag_off) - lse_c)
                    dob = do.astype(_MXU_DTYPE)
                    ds = (p * (lax.dot_general(dob, v2, _NT, preferred_element_type=_F32) - delta) * _SCALE).astype(_MXU_DTYPE)
                    dq_s[cur, :] += jnp.dot(ds, k2, preferred_element_type=_F32)
                    dk2 = lax.dot_general(ds, q, _TN, preferred_element_type=_F32)
                    dv2 = lax.dot_general(p.astype(_MXU_DTYPE), dob, _TN, preferred_element_type=_F32)
                    dk_s[prev, :] += dk2[:BLOCK]
                    dv_s[prev, :] += dv2[:BLOCK]
                    dk_s[cur, :] += dk2[BLOCK:]
                    dv_s[cur, :] += dv2[BLOCK:]
                    return carry

                lax.fori_loop(0, _BAND_STEPS, step, 0, unroll=16)
            dq_ref[:, hs] = _rope_transpose(dq_s[...], c, s1, s2).astype(dq_ref.dtype)
            if sink is not None:
                delta = jnp.sum(do_ref[...] * o_ref[:, hs], axis=1, keepdims=True)
                w_sink = jnp.exp(sk_ref[i] - lse_ref[:, hs])
                dsk_ref[i] = jnp.broadcast_to(jnp.sum(-delta * w_sink[:, 0:1]), (8, HEAD_DIM))
        dk_ref[...] = _rope_transpose(dk_s[...], c, s1, s2).astype(dk_ref.dtype)
        dv_ref[...] = dv_s[...].astype(dv_ref.dtype)

    col = (SEQ, HEAD_DIM)
    in_specs = [pl.BlockSpec(col, functools.partial(lambda g, i: (0, A_COLS + q0 + g * q_per_kv + i), i=i)) for i in range(q_per_kv)]
    in_specs += [pl.BlockSpec(col, lambda g: (0, A_COLS + k0 + g)), pl.BlockSpec(col, lambda g: (0, A_COLS + v0 + g))]
    in_specs += [pl.BlockSpec(col, lambda g: (0, 0))] * 3
    in_specs += [pl.BlockSpec(col, functools.partial(lambda g, i: (0, do0 + g * q_per_kv + i), i=i)) for i in range(q_per_kv)]
    wide = pl.BlockSpec((SEQ, q_per_kv * HEAD_DIM), lambda g: (0, g))
    in_specs += [wide, wide]
    args = [proj] * (q_per_kv + 2) + list(tabs) + [dmixed] * q_per_kv + [o, lse]
    out_specs = [wide, pl.BlockSpec(col, lambda g: (0, g)), pl.BlockSpec(col, lambda g: (0, g))]
    out_shape = [jax.ShapeDtypeStruct((SEQ, heads * HEAD_DIM), _MXU_DTYPE), jax.ShapeDtypeStruct((SEQ, kv_heads * HEAD_DIM), _MXU_DTYPE),
                 jax.ShapeDtypeStruct((SEQ, kv_heads * HEAD_DIM), _MXU_DTYPE)]
    if sink is not None:
        in_specs.append(pl.BlockSpec((q_per_kv, 1, HEAD_DIM), lambda g: (g, 0, 0)))
        args.append(jnp.broadcast_to(sink.reshape(heads, 1, 1), (heads, 1, HEAD_DIM)))
        out_specs.append(pl.BlockSpec((q_per_kv, 8, HEAD_DIM), lambda g: (g, 0, 0)))
        out_shape.append(jax.ShapeDtypeStruct((heads, 8, HEAD_DIM), _F32))
    in_specs += [pl.BlockSpec(memory_space=pl.ANY)] * len(after)
    args += list(after)
    res = pl.pallas_call(
        body, grid=(kv_heads,), in_specs=in_specs, out_specs=out_specs, out_shape=out_shape,
        scratch_shapes=[pltpu.VMEM(col, _F32)] * 5, name=name,
        compiler_params=_cparams(dimension_semantics=("parallel",)),
    )(*args)
    if sink is not None:
        return res[0], res[1], res[2], res[3][:, 0, 0]
    return res


_DILATED = dict(kv_heads=B_HEADS, q_per_kv=1, q0=QB0, k0=KB0, v0=VB0, dilations=DILATIONS, lag_off=0, sink=None)
_SWA = dict(kv_heads=C_KV_HEADS, q_per_kv=C_HEADS // C_KV_HEADS, q0=QC0, k0=KC0, v0=VC0, dilations=(1,), lag_off=1)


_HG_TILE = 128
_HG_CHUNKS = _HG_TILE // A_CHUNK
_HG_TILES = SEQ // _HG_TILE
_HI = lax.Precision.HIGHEST


def _chunk_tri():
    i = np.arange(_HG_TILE)
    return jnp.asarray(((i[:, None] // A_CHUNK == i[None, :] // A_CHUNK) & (i[None, :] <= i[:, None])).astype(np.float32))


def _layer_lb(lb_ref, layer):
    if layer == 0:
        return jnp.zeros((1, HEAD_DIM), _F32)
    lg = lb_ref[...]
    m = jnp.max(lg, axis=0, keepdims=True)
    e = jnp.exp(lg - m)
    return e[1:2, :] / jnp.sum(e, axis=0, keepdims=True)


def _hgrn_gates(q, fr, lb):
    sgq = _sigmoid(q)
    sg = _sigmoid(fr)
    f = lb + (1.0 - lb) * sg
    return sgq, q * sgq, sg, f, 1.0 - f


def _hgrn_fwd(proj, lb_logits, norm_w, layer, *, name):
    tri = _chunk_tri()

    def body(q_ref, f_ref, i_ref, g_ref, lb_ref, nw_ref, tri_ref, o_ref, raw_ref, st_ref, state):
        @pl.when(pl.program_id(1) == 0)
        def _():
            state[...] = jnp.zeros_like(state)

        lb = _layer_lb(lb_ref, layer)
        _, qs, _, f, k = _hgrn_gates(q_ref[...], f_ref[...], lb)
        v = i_ref[...]
        b = jnp.dot(tri_ref[...], jnp.log(f), precision=_HI, preferred_element_type=_F32)
        eb = jnp.exp(b)
        ridx = lax.broadcasted_iota(jnp.int32, (A_CHUNK, HEAD_DIM), 0)
        outs = []
        for c in range(_HG_CHUNKS):
            sl = slice(c * A_CHUNK, (c + 1) * A_CHUNK)
            bc, qc, kc, vc = b[sl], qs[sl], k[sl], v[sl]
            bl = bc[A_CHUNK - 1:A_CHUNK]
            st = state[...]
            st_ref[0, c] = st
            o_c = lax.dot_general((qc * eb[sl]).astype(_MXU_DTYPE), st.astype(_MXU_DTYPE), _NT, preferred_element_type=_F32)
            rows = []
            for i in range(A_CHUNK):
                di = jnp.exp(jnp.where(ridx <= i, bc[i:i + 1] - bc, _NEG))
                a = jnp.sum(qc[i:i + 1] * kc * di, axis=1, keepdims=True)
                rows.append(jnp.sum(a * vc, axis=0, keepdims=True))
            outs.append(o_c + jnp.concatenate(rows, axis=0))
            kt = (kc * jnp.exp(bl - bc)).astype(_MXU_DTYPE)
            state[...] = st * jnp.exp(bl) + lax.dot_general(vc.astype(_MXU_DTYPE), kt, _TN, preferred_element_type=_F32)
        o = jnp.concatenate(outs, axis=0)
        raw_ref[...] = o
        r = lax.rsqrt(jnp.mean(o * o, axis=-1, keepdims=True) + LN_EPS)
        g = g_ref[...]
        o_ref[...] = o * r * nw_ref[...] * (g * _sigmoid(g))

    blk = (_HG_TILE, HEAD_DIM)

    def col(base):
        return pl.BlockSpec(blk, lambda h, t: (t, base + h))

    o_spec = pl.BlockSpec(blk, lambda h, t: (t, h))
    o_shape = jax.ShapeDtypeStruct((SEQ, A_HEADS * HEAD_DIM), _F32)
    return pl.pallas_call(
        body, grid=(A_HEADS, _HG_TILES),
        in_specs=[col(0), col(4), col(8), col(12), pl.BlockSpec((DEPTH, HEAD_DIM), lambda h, t: (0, h)),
                  pl.BlockSpec((1, HEAD_DIM), lambda h, t: (0, 0)), pl.BlockSpec((_HG_TILE, _HG_TILE), lambda h, t: (0, 0))],
        out_specs=[o_spec, o_spec, pl.BlockSpec((1, _HG_CHUNKS, HEAD_DIM, HEAD_DIM), lambda h, t: (h, t, 0, 0))],
        out_shape=[o_shape, o_shape, jax.ShapeDtypeStruct((A_HEADS, SEQ // A_CHUNK, HEAD_DIM, HEAD_DIM), _F32)],
        scratch_shapes=[pltpu.VMEM((HEAD_DIM, HEAD_DIM), _F32)], name=name,
        compiler_params=_cparams(dimension_semantics=("parallel", "arbitrary")),
    )(proj, proj, proj, proj, lb_logits, norm_w.reshape(1, HEAD_DIM), tri)


def _hgrn_bwd(proj, lb_logits, norm_w, raw, states, dmixed, layer, *, name):
    tri = _chunk_tri()
    triu = tri.T

    def body(q_ref, f_ref, i_ref, g_ref, lb_ref, nw_ref, tri_ref, triu_ref, raw_ref, do_ref, st_ref,
             dq_ref, df_ref, di_ref, dg_ref, dnw_ref, dlb_ref, dstate):
        @pl.when(pl.program_id(1) == 0)
        def _():
            dstate[...] = jnp.zeros_like(dstate)
            dlb_ref[...] = jnp.zeros_like(dlb_ref)

        @pl.when((pl.program_id(0) == 0) & (pl.program_id(1) == 0))
        def _():
            dnw_ref[...] = jnp.zeros_like(dnw_ref)

        lb = _layer_lb(lb_ref, layer)
        q = q_ref[...]
        sgq, qs, sg, f, k = _hgrn_gates(q, f_ref[...], lb)
        v = i_ref[...]
        b = jnp.dot(tri_ref[...], jnp.log(f), precision=_HI, preferred_element_type=_F32)
        eb = jnp.exp(b)
        g = g_ref[...]
        nw = nw_ref[...]
        o = raw_ref[...]
        dout = do_ref[...]
        sgg = _sigmoid(g)
        r = lax.rsqrt(jnp.mean(o * o, axis=-1, keepdims=True) + LN_EPS)
        dg_ref[...] = (dout * (o * r * nw) * (sgg * (1.0 + g * (1.0 - sgg)))).astype(dg_ref.dtype)
        don = dout * (g * sgg)
        dnw_ref[0:1, :] += jnp.sum(don * o * r, axis=0, keepdims=True)
        dy = don * nw
        do_raw = r * dy - o * (r * r * r) * jnp.mean(o * dy, axis=-1, keepdims=True)

        ridx = lax.broadcasted_iota(jnp.int32, (A_CHUNK, HEAD_DIM), 0)
        dqs_t, dk_t, db_t, dv_t = [None] * _HG_CHUNKS, [None] * _HG_CHUNKS, [None] * _HG_CHUNKS, [None] * _HG_CHUNKS
        for c in reversed(range(_HG_CHUNKS)):
            sl = slice(c * A_CHUNK, (c + 1) * A_CHUNK)
            bc, qc, kc, vc, doc = b[sl], qs[sl], k[sl], v[sl], do_raw[sl]
            bl = bc[A_CHUNK - 1:A_CHUNK]
            ebc = eb[sl]
            ebl = jnp.exp(bl - bc)
            lam = jnp.exp(bl)
            qt = qc * ebc
            kt = kc * ebl
            dst = dstate[...]
            stp = st_ref[0, c]
            dob = doc.astype(_MXU_DTYPE)
            dstb = dst.astype(_MXU_DTYPE)
            dqt = jnp.dot(dob, stp.astype(_MXU_DTYPE), preferred_element_type=_F32)
            dkt = jnp.dot(vc.astype(_MXU_DTYPE), dstb, preferred_element_type=_F32)
            dv = lax.dot_general(kt.astype(_MXU_DTYPE), dstb, _NT, preferred_element_type=_F32)
            dlam = jnp.sum(stp * dst, axis=0, keepdims=True)
            dstate[...] = dst * lam + lax.dot_general(dob, qt.astype(_MXU_DTYPE), _TN, preferred_element_type=_F32)
            dqs_rows = []
            dk_in = jnp.zeros((A_CHUNK, HEAD_DIM), _F32)
            for i in range(A_CHUNK):
                di = jnp.exp(jnp.where(ridx <= i, bc[i:i + 1] - bc, _NEG))
                qi = qc[i:i + 1]
                doi = doc[i:i + 1]
                w = kc * di
                a = jnp.sum(qi * w, axis=1, keepdims=True)
                dv = dv + a * doi
                da = jnp.sum(doi * vc, axis=1, keepdims=True)
                dqs_rows.append(jnp.sum(da * w, axis=0, keepdims=True))
                dk_in = dk_in + da * (qi * di)
            dqs_in = jnp.concatenate(dqs_rows, axis=0)
            dbl = jnp.sum(dkt * kt, axis=0, keepdims=True) + dlam * lam
            db = qc * dqs_in - kc * dk_in + dqt * qt - dkt * kt
            db_t[c] = db + jnp.where(ridx == A_CHUNK - 1, dbl, 0.0)
            dqs_t[c] = dqs_in + dqt * ebc
            dk_t[c] = dk_in + dkt * ebl
            dv_t[c] = dv
        dqs = jnp.concatenate(dqs_t, axis=0)
        dk = jnp.concatenate(dk_t, axis=0)
        db = jnp.concatenate(db_t, axis=0)
        di_ref[...] = jnp.concatenate(dv_t, axis=0).astype(di_ref.dtype)
        dlogf = jnp.dot(triu_ref[...], db, precision=_HI, preferred_element_type=_F32)
        df = dlogf / f - dk
        df_ref[...] = (df * (1.0 - lb) * sg * (1.0 - sg)).astype(df_ref.dtype)
        dlb_ref[0, 0:1, :] += jnp.sum(df * (1.0 - sg), axis=0, keepdims=True)
        dq_ref[...] = (dqs * (sgq * (1.0 + q * (1.0 - sgq)))).astype(dq_ref.dtype)

    blk = (_HG_TILE, HEAD_DIM)
    last = _HG_TILES - 1

    def col(base):
        return pl.BlockSpec(blk, lambda h, t: (last - t, base + h))

    tri_spec = pl.BlockSpec((_HG_TILE, _HG_TILE), lambda h, t: (0, 0))
    acc_spec = pl.BlockSpec((1, 8, HEAD_DIM), lambda h, t: (h, 0, 0))
    acc_shape = jax.ShapeDtypeStruct((A_HEADS, 8, HEAD_DIM), _F32)
    dq, df, di, dg, dnw, dlb = pl.pallas_call(
        body, grid=(A_HEADS, _HG_TILES),
        in_specs=[col(0), col(4), col(8), col(12), pl.BlockSpec((DEPTH, HEAD_DIM), lambda h, t: (0, h)),
                  pl.BlockSpec((1, HEAD_DIM), lambda h, t: (0, 0)), tri_spec, tri_spec, col(0), col(0),
                  pl.BlockSpec((1, _HG_CHUNKS, HEAD_DIM, HEAD_DIM), lambda h, t: (h, last - t, 0, 0))],
        out_specs=[col(0), col(0), col(0), col(0), pl.BlockSpec((8, HEAD_DIM), lambda h, t: (0, 0)), acc_spec],
        out_shape=[jax.ShapeDtypeStruct((SEQ, A_HEADS * HEAD_DIM), _MXU_DTYPE)] * 4
        + [jax.ShapeDtypeStruct((8, HEAD_DIM), _F32), acc_shape],
        scratch_shapes=[pltpu.VMEM((HEAD_DIM, HEAD_DIM), _F32)], name=name,
        compiler_params=_cparams(dimension_semantics=("arbitrary", "arbitrary")),
    )(proj, proj, proj, proj, lb_logits, norm_w.reshape(1, HEAD_DIM), tri, triu, raw, dmixed, states)
    return dq, df, di, dg, dnw[0], dlb[:, 0, :].reshape(A_HEADS * HEAD_DIM)


N_CHIP = N_DEV // 2
_MESH_ID = pl.DeviceIdType.MESH


def _place():
    x, y, c = lax.axis_index("x"), lax.axis_index("y"), lax.axis_index("c")
    chips = [(1 - x, y), (x, 1 - y), (1 - x, 1 - y)]
    return x, y, c, 2 * x + y, chips


def _sibling_swap(arrays, *, name):
    n = len(arrays)

    def body(*refs):
        ins, outs = refs[:n], refs[n:2 * n]
        send_sems, recv_sems = refs[2 * n:]
        x, y, c, _, _ = _place()
        copies = [pltpu.make_async_remote_copy(
            src_ref=ins[a].at[:, 1 - c], dst_ref=outs[a], send_sem=send_sems.at[a], recv_sem=recv_sems.at[a],
            device_id=(x, y, 1 - c), device_id_type=_MESH_ID) for a in range(n)]
        for cp in copies:
            cp.start()
        for cp in copies:
            cp.wait()

    any_spec = pl.BlockSpec(memory_space=pl.ANY)
    return pl.pallas_call(
        body, in_specs=[any_spec] * n, out_specs=[any_spec] * n,
        out_shape=[jax.ShapeDtypeStruct((N_CHIP,) + a.shape[2:], a.dtype) for a in arrays],
        scratch_shapes=[pltpu.SemaphoreType.DMA((n,)), pltpu.SemaphoreType.DMA((n,))],
        name=name, compiler_params=pltpu.CompilerParams(has_side_effects=True),
    )(*arrays)


def _pair_add(mine, theirs, core, *, name):
    _, _, R, C = mine.shape
    tr = max(t for t in range(16, R + 1, 16) if R % t == 0 and t * C <= 512 * 1024)

    def body(core_ref, m_ref, t_ref, o_ref):
        del core_ref
        o_ref[...] = (m_ref[...].astype(_F32) + t_ref[...].astype(_F32)).astype(o_ref.dtype)

    grid_spec = pltpu.PrefetchScalarGridSpec(
        num_scalar_prefetch=1, grid=(N_CHIP, R // tr),
        in_specs=[pl.BlockSpec((None, None, tr, C), lambda q, i, core: (q, core[0], i, 0)),
                  pl.BlockSpec((None, tr, C), lambda q, i, core: (q, i, 0))],
        out_specs=pl.BlockSpec((None, tr, C), lambda q, i, core: (q, i, 0)))
    return pl.pallas_call(
        body, grid_spec=grid_spec, out_shape=jax.ShapeDtypeStruct((N_CHIP, R, C), mine.dtype), name=name,
        compiler_params=_cparams(dimension_semantics=("parallel", "parallel")),
    )(core.reshape(1), mine, theirs)


_HBM = pl.BlockSpec(memory_space=pltpu.HBM)
_SEM = pl.BlockSpec(memory_space=pltpu.SEMAPHORE)
_TOKEN = pl.BlockSpec(memory_space=pltpu.VMEM)
_DATAFLOW = pltpu.SideEffectType.DATAFLOW_SIDE_EFFECTING


def _hbm(a):
    return pltpu.HBM(a.shape, a.dtype)


def _token_shape():
    return jax.ShapeDtypeStruct((8, 128), _F32)


def _dev_slot(px, py, pc):
    return 4 * px + 2 * py + pc


def _gather_start(blocks, landings, *, name):
    n = len(blocks)

    def body(*refs):
        ins, lands = refs[:n], refs[n:2 * n]
        send_sems, d2d_sems, ici_sems = refs[2 * n:2 * n + 3]
        token = refs[-1]
        x, y, c, _, chips = _place()
        for a in range(n):
            dst = lands[a].at[_dev_slot(x, y, c)]
            pltpu.make_async_remote_copy(src_ref=ins[a], dst_ref=dst, send_sem=send_sems.at[4 * a], recv_sem=d2d_sems.at[a],
                                         device_id=(x, y, 1 - c), device_id_type=_MESH_ID).start()
            for j, chip in enumerate(chips):
                pltpu.make_async_remote_copy(src_ref=ins[a], dst_ref=dst, send_sem=send_sems.at[4 * a + 1 + j],
                                             recv_sem=ici_sems.at[3 * a + j], device_id=(*chip, c),
                                             device_id_type=_MESH_ID).start()
        token[...] = jnp.zeros_like(token)

    res = pl.pallas_call(
        body, name=name, in_specs=[_HBM] * (2 * n),
        out_shape=(pltpu.SemaphoreType.DMA((4 * n,)), pltpu.SemaphoreType.DMA((n,)), pltpu.SemaphoreType.DMA((3 * n,)),
                   *[_hbm(b) for b in blocks], *[_hbm(b) for b in landings], _token_shape()),
        out_specs=(_SEM, _SEM, _SEM, *[_HBM] * (2 * n), _TOKEN),
        input_output_aliases={i: 3 + i for i in range(2 * n)},
        compiler_params=pltpu.CompilerParams(has_side_effects=_DATAFLOW),
    )(*[pltpu.with_memory_space_constraint(b, pltpu.HBM) for b in blocks],
      *[pltpu.with_memory_space_constraint(b, pltpu.HBM) for b in landings])
    return res[0], res[1], res[2], list(res[3:3 + n]), list(res[3 + n:3 + 2 * n]), res[-1]


def _gather_forward(landings, ici_sems, first, after, *, name):
    n = len(landings)

    def body(*refs):
        lands = refs[:n]
        ici = refs[n]
        f_send, f_recv = refs[n + 2], refs[n + 3]
        token = refs[-1]
        x, y, c, _, chips = _place()
        for a in range(n):
            for j, chip in enumerate(chips):
                blk = lands[a].at[_dev_slot(*chip, c)]
                pltpu.make_async_remote_copy(src_ref=blk, dst_ref=blk, send_sem=f_send.at[3 * a + j],
                                             recv_sem=ici.at[3 * (first + a) + j], device_id=(*chip, c),
                                             device_id_type=_MESH_ID).wait_recv()
                pltpu.make_async_remote_copy(src_ref=blk, dst_ref=blk, send_sem=f_send.at[3 * a + j], recv_sem=f_recv.at[3 * a + j],
                                             device_id=(x, y, 1 - c), device_id_type=_MESH_ID).start()
        token[...] = jnp.zeros_like(token)

    res = pl.pallas_call(
        body, name=name, in_specs=[_HBM] * n + [_SEM, pl.BlockSpec(memory_space=pl.ANY)],
        out_shape=(pltpu.SemaphoreType.DMA((3 * n,)), pltpu.SemaphoreType.DMA((3 * n,)), *[_hbm(b) for b in landings], _token_shape()),
        out_specs=(_SEM, _SEM, *[_HBM] * n, _TOKEN),
        input_output_aliases={i: 2 + i for i in range(n)},
        compiler_params=pltpu.CompilerParams(has_side_effects=_DATAFLOW),
    )(*landings, ici_sems, after)
    return res[0], res[1], list(res[2:2 + n]), res[-1]


def _gather_wait(blocks, landings, send_sems, d2d_sems, first, f_send, f_recv, after, *, name):
    n = len(landings)

    def body(*refs):
        ins, lands = refs[:n], refs[n:2 * n]
        send, d2d, fs, fr = refs[2 * n:2 * n + 4]
        x, y, c, _, chips = _place()
        me = (x, y, c)
        for a in range(n):
            own = lands[a].at[_dev_slot(x, y, 1 - c)]
            g = first + a
            pltpu.make_async_remote_copy(src_ref=ins[a], dst_ref=own, send_sem=send.at[4 * g], recv_sem=d2d.at[g],
                                         device_id=me, device_id_type=_MESH_ID).wait_recv()
            for j, chip in enumerate(chips):
                blk = lands[a].at[_dev_slot(*chip, 1 - c)]
                pltpu.make_async_remote_copy(src_ref=blk, dst_ref=blk, send_sem=fs.at[3 * a + j], recv_sem=fr.at[3 * a + j],
                                             device_id=me, device_id_type=_MESH_ID).wait_recv()
            for k in range(4):
                pltpu.make_async_remote_copy(src_ref=ins[a], dst_ref=own, send_sem=send.at[4 * g + k], recv_sem=d2d.at[g],
                                             device_id=me, device_id_type=_MESH_ID).wait_send()
            for j in range(3):
                pltpu.make_async_remote_copy(src_ref=own, dst_ref=own, send_sem=fs.at[3 * a + j], recv_sem=fr.at[3 * a + j],
                                             device_id=me, device_id_type=_MESH_ID).wait_send()

    res = pl.pallas_call(
        body, name=name, in_specs=[_HBM] * (2 * n) + [_SEM] * 4 + [pl.BlockSpec(memory_space=pl.ANY)],
        out_shape=(*[_hbm(b) for b in blocks], *[_hbm(b) for b in landings]), out_specs=tuple([_HBM] * (2 * n)),
        input_output_aliases={i: i for i in range(2 * n)},
        compiler_params=pltpu.CompilerParams(has_side_effects=_DATAFLOW),
    )(*blocks, *landings, send_sems, d2d_sems, f_send, f_recv, after)
    return list(res[n:])


def _swap_start(mine, landings, *, name):
    n = len(mine)

    def body(*refs):
        ins, lands = refs[:n], refs[n:2 * n]
        send_sems, recv_sems = refs[2 * n:2 * n + 2]
        token = refs[-1]
        x, y, c, _, _ = _place()
        for a in range(n):
            pltpu.make_async_remote_copy(src_ref=ins[a].at[:, 1 - c], dst_ref=lands[a], send_sem=send_sems.at[a],
                                         recv_sem=recv_sems.at[a], device_id=(x, y, 1 - c), device_id_type=_MESH_ID).start()
        token[...] = jnp.zeros_like(token)

    res = pl.pallas_call(
        body, name=name, in_specs=[_HBM] * (2 * n),
        out_shape=(pltpu.SemaphoreType.DMA((n,)), pltpu.SemaphoreType.DMA((n,)),
                   *[_hbm(b) for b in mine], *[_hbm(b) for b in landings], _token_shape()),
        out_specs=(_SEM, _SEM, *[_HBM] * (2 * n), _TOKEN),
        input_output_aliases={i: 2 + i for i in range(2 * n)},
        compiler_params=pltpu.CompilerParams(has_side_effects=_DATAFLOW),
    )(*[pltpu.with_memory_space_constraint(b, pltpu.HBM) for b in mine],
      *[pltpu.with_memory_space_constraint(b, pltpu.HBM) for b in landings])
    return res[0], res[1], list(res[2:2 + n]), list(res[2 + n:2 + 2 * n]), res[-1]


def _swap_wait(mine, landings, send_sems, recv_sems, after, *, name):
    n = len(mine)

    def body(*refs):
        ins, lands = refs[:n], refs[n:2 * n]
        send, recv = refs[2 * n:2 * n + 2]
        x, y, c, _, _ = _place()
        for a in range(n):
            cp = pltpu.make_async_remote_copy(src_ref=ins[a].at[:, 1 - c], dst_ref=lands[a], send_sem=send.at[a],
                                              recv_sem=recv.at[a], device_id=(x, y, c), device_id_type=_MESH_ID)
            cp.wait_recv()
            cp.wait_send()

    res = pl.pallas_call(
        body, name=name, in_specs=[_HBM] * (2 * n) + [_SEM] * 2 + [pl.BlockSpec(memory_space=pl.ANY)],
        out_shape=(*[_hbm(b) for b in mine], *[_hbm(b) for b in landings]), out_specs=tuple([_HBM] * (2 * n)),
        input_output_aliases={i: i for i in range(2 * n)},
        compiler_params=pltpu.CompilerParams(has_side_effects=_DATAFLOW),
    )(*mine, *landings, send_sems, recv_sems, after)
    return list(res[:n]), list(res[n:])


def _chip_exchange_start(sums, landings, *, name):
    n = len(sums)

    def body(*refs):
        ins, lands = refs[:n], refs[n:2 * n]
        send_sems, recv_sems = refs[2 * n:2 * n + 2]
        token = refs[-1]
        _, _, c, p, chips = _place()
        for a in range(n):
            for j, (qx, qy) in enumerate(chips):
                pltpu.make_async_remote_copy(src_ref=ins[a].at[2 * qx + qy], dst_ref=lands[a].at[p], send_sem=send_sems.at[3 * a + j],
                                             recv_sem=recv_sems.at[3 * a + j], device_id=(qx, qy, c), device_id_type=_MESH_ID).start()
        token[...] = jnp.zeros_like(token)

    res = pl.pallas_call(
        body, name=name, in_specs=[_HBM] * (2 * n),
        out_shape=(pltpu.SemaphoreType.DMA((3 * n,)), pltpu.SemaphoreType.DMA((3 * n,)),
                   *[_hbm(b) for b in sums], *[_hbm(b) for b in landings], _token_shape()),
        out_specs=(_SEM, _SEM, *[_HBM] * (2 * n), _TOKEN),
        input_output_aliases={i: 2 + i for i in range(2 * n)},
        compiler_params=pltpu.CompilerParams(has_side_effects=_DATAFLOW),
    )(*[pltpu.with_memory_space_constraint(b, pltpu.HBM) for b in sums],
      *[pltpu.with_memory_space_constraint(b, pltpu.HBM) for b in landings])
    return res[0], res[1], list(res[2:2 + n]), list(res[2 + n:2 + 2 * n]), res[-1]


def _chip_exchange_wait(sums, landings, send_sems, recv_sems, after, *, name):
    n = len(sums)

    def body(*refs):
        ins, lands = refs[:n], refs[n:2 * n]
        send, recv = refs[2 * n:2 * n + 2]
        x, y, c, _, chips = _place()
        for a in range(n):
            for j, (qx, qy) in enumerate(chips):
                q = 2 * qx + qy
                cp = pltpu.make_async_remote_copy(src_ref=ins[a].at[q], dst_ref=lands[a].at[q], send_sem=send.at[3 * a + j],
                                                  recv_sem=recv.at[3 * a + j], device_id=(x, y, c), device_id_type=_MESH_ID)
                cp.wait_recv()
                cp.wait_send()

    res = pl.pallas_call(
        body, name=name, in_specs=[_HBM] * (2 * n) + [_SEM] * 2 + [pl.BlockSpec(memory_space=pl.ANY)] * len(after),
        out_shape=(*[_hbm(b) for b in sums], *[_hbm(b) for b in landings]), out_specs=tuple([_HBM] * (2 * n)),
        input_output_aliases={i: i for i in range(2 * n)},
        compiler_params=pltpu.CompilerParams(has_side_effects=_DATAFLOW),
    )(*sums, *landings, send_sems, recv_sems, *after)
    return list(res[:n]), list(res[n:])


_C1 = 1.0 - ADAM_B1 ** ADAM_STEP
_C2 = 1.0 - ADAM_B2 ** ADAM_STEP


def _adamw_math(g, w, m, v):
    m = ADAM_B1 * m + (1.0 - ADAM_B1) * g
    v = ADAM_B2 * v + (1.0 - ADAM_B2) * (g * g)
    delta = -ADAM_LR * ((m / _C1) / (jnp.sqrt(v / _C2) + ADAM_EPS) + ADAM_WD * w)
    return delta, m, v


def _adamw_reduce(landed, sums, chip, w, m, v, layer, prev, *, name):
    _, R, C = w.shape
    tr = max(t for t in range(16, R + 1, 16) if R % t == 0 and t * C <= 256 * 1024)

    def body(chip_ref, p_ref, own_ref, w_ref, m_ref, v_ref, *rest):
        g_ref, d_ref, nm_ref, nv_ref = rest[-4:]
        own = own_ref[...].astype(_F32)
        g = jnp.where(chip_ref[0] == 0, own, p_ref[0].astype(_F32))
        for q in range(1, N_CHIP):
            g = g + jnp.where(chip_ref[0] == q, own, p_ref[q].astype(_F32))
        d, nm, nv = _adamw_math(g, w_ref[...], m_ref[...], v_ref[...])
        g_ref[...] = g
        d_ref[...] = d
        nm_ref[...] = nm
        nv_ref[...] = nv

    blk = pl.BlockSpec((None, tr, C), lambda i, chip: (layer, i, 0))
    shape = jax.ShapeDtypeStruct((DEPTH, R, C), _F32)
    kept = [] if prev is None else list(prev)
    grid_spec = pltpu.PrefetchScalarGridSpec(
        num_scalar_prefetch=1, grid=(R // tr,),
        in_specs=[pl.BlockSpec((N_CHIP, tr, C), lambda i, chip: (0, i, 0)),
                  pl.BlockSpec((None, tr, C), lambda i, chip: (chip[0], i, 0)), blk, blk, blk]
        + [pl.BlockSpec(memory_space=pl.ANY)] * len(kept),
        out_specs=[blk] * 4)
    return pl.pallas_call(
        body, grid_spec=grid_spec, out_shape=[shape] * 4, name=name,
        input_output_aliases={6 + k: k for k in range(len(kept))},
        compiler_params=_cparams(dimension_semantics=("parallel",)),
    )(chip.reshape(1), landed, sums, w, m, v, *kept)


_PACK_LANES = 128
_LAYER_ROWS = 248
_LB_ROWS = (A_HEADS * HEAD_DIM) // _PACK_LANES


def _small_reduce(parts, lb_logits, *, name):
    rows = DEPTH * _LAYER_ROWS

    def body(p_ref, lg_ref, o_ref):
        g = p_ref[0]
        for s in range(1, N_DEV):
            g = g + p_ref[s]
        o_ref[...] = g
        lg = lg_ref[...]
        e = jnp.exp(lg - jnp.max(lg, axis=0, keepdims=True))
        p = e / jnp.sum(e, axis=0, keepdims=True)
        d1 = g[_LAYER_ROWS:_LAYER_ROWS + _LB_ROWS, :] * p[0] * p[1]
        o_ref[0:_LB_ROWS, :] = -d1
        o_ref[_LAYER_ROWS:_LAYER_ROWS + _LB_ROWS, :] = d1

    return pl.pallas_call(
        body, out_shape=jax.ShapeDtypeStruct((rows, _PACK_LANES), _F32), name=name,
        compiler_params=_cparams(),
    )(parts, lb_logits.reshape(DEPTH, _LB_ROWS, _PACK_LANES))


def _adamw_small(g, w, m, v, *, name):
    def body(g_ref, w_ref, m_ref, v_ref, d_ref, nm_ref, nv_ref):
        d, nm, nv = _adamw_math(g_ref[...], w_ref[...], m_ref[...], v_ref[...])
        d_ref[...] = d
        nm_ref[...] = nm
        nv_ref[...] = nv

    shape = jax.ShapeDtypeStruct(g.shape, _F32)
    return pl.pallas_call(body, out_shape=[shape] * 3, name=name, compiler_params=_cparams())(g, w, m, v)


def _pack(vectors, rows):
    flat = jnp.concatenate([v.reshape(-1).astype(_F32) for v in vectors])
    return jnp.pad(flat, (0, rows * _PACK_LANES - flat.shape[0])).reshape(rows, _PACK_LANES)


def _unpack(packed, shapes):
    flat = packed.reshape(-1)
    out, at = [], 0
    for s in shapes:
        size = int(np.prod(s))
        out.append(flat[at:at + size].reshape(s))
        at += size
    return out


_BIG = ("w_in", "w_gate", "w_up", "w_out", "w_down")
_COLUMN_SHARDED = ("w_in", "w_gate", "w_up")


def _full_weight(name, g):
    if name == "conv_w":
        return g.transpose(1, 0, 2).reshape(g.shape[1], N_DEV * SHARD_COLS)
    if name in _BIG:
        return g.reshape(N_DEV * g.shape[1], g.shape[2])
    return g


class _WeightGather:
    def __init__(self, names, first, blocks, lands, sems, tag):
        self.names, self.first, self.blocks, self.lands, self.sems, self.tag = names, first, blocks, lands, sems, tag
        self.forwarded = None

    def forward(self, after):
        f_send, f_recv, self.lands, token = _gather_forward(self.lands, self.sems[2], self.first, after,
                                                            name=f"gather_forward_{self.tag}")
        self.forwarded = (f_send, f_recv)
        return token

    def wait(self, after):
        if self.forwarded is None:
            self.forward(after)
        got = _gather_wait(self.blocks, self.lands, self.sems[0], self.sems[1], self.first, *self.forwarded, after,
                           name=f"gather_wait_{self.tag}")
        return {n: _full_weight(n, g) for n, g in zip(self.names, got)}


def _start_gathers(groups, me, name):
    blocks = [b for _, _, bs in groups for b in bs]
    landings = [lax.dynamic_update_index_in_dim(lax.empty((N_DEV,) + b.shape, b.dtype), b[None], me, 0) for b in blocks]
    send, d2d, ici, blocks, landings, token = _gather_start(blocks, landings, name=name)
    out, first = [], 0
    for tag, names, bs in groups:
        k = len(bs)
        out.append(_WeightGather(names, first, blocks[first:first + k], landings[first:first + k], (send, d2d, ici), tag))
        first += k
    return out, token


class _LayerWeights:
    def __init__(self, ready, pending=(), forwards=(), tokens=()):
        self.ready, self.pending, self.forwards, self._tokens = dict(ready), list(pending), list(forwards), list(tokens)

    def at(self, point, after):
        for when, gather in self.forwards:
            if when == point:
                self._tokens.append(gather.forward(after))

    def tokens(self):
        out, self._tokens = self._tokens, []
        return out

    def get(self, name, after):
        if name not in self.ready:
            group, = [g for g in self.pending if name in g.names]
            self.ready.update(group.wait(after))
        return self.ready[name]


def _layer_fwd(x, xb, ws, lb_logits, a_norm_w, c_sink, ln1_g, ln1_b, conv_b, ln2_g, ln2_b, tabs, l):
    proj = _mm(xb, ws.get("w_in", xb), tb=True, **_TILE_WIDE_N, after=ws.tokens(), name=f"proj_{l}")
    o_a, raw, states = _hgrn_fwd(proj, lb_logits, a_norm_w, l, name=f"hgrn_fwd_{l}")
    ws.at("hgrn", o_a)
    o_b, lse_b = _band_fwd(proj, tabs, name=f"dilated_fwd_{l}", **_DILATED)
    o_c, lse_c = _band_fwd(proj, tabs, sink=c_sink, name=f"swa_fwd_{l}", **_SWA)
    ws.at("swa", o_c)
    mixed = _concat_cols([o_a, o_b, o_c], name=f"mixed_{l}")
    y = _mm(mixed, ws.get("w_out", mixed), **_TILE_MIX, after=ws.tokens(), name=f"mix_out_{l}")
    z1, x1, x1b = _ln_fwd(x, y, ln1_g, ln1_b, name=f"ln1_fwd_{l}")
    g = _mm(x1b, ws.get("w_gate", x1b), tb=True, **_TILE_WIDE_N, out_dtype=_ACT_DTYPE, name=f"ffn_gate_{l}")
    u = _mm(x1b, ws.get("w_up", x1b), tb=True, **_TILE_WIDE_N, out_dtype=_ACT_DTYPE, name=f"ffn_up_{l}")
    ws.at("up", u)
    hb = _conv_gate_fwd(g, u, ws.get("conv_w", u), conv_b, name=f"conv_gate_fwd_{l}")
    y2 = _mm(hb, ws.get("w_down", hb), **_TILE_WIDE_K, after=ws.tokens(), name=f"ffn_down_{l}")
    ws.at("down", y2)
    z2, x2, x2b = _ln_fwd(x1, y2, ln2_g, ln2_b, name=f"ln2_fwd_{l}")
    res = dict(xb=xb, proj=proj, raw=raw, states=states, o_b=o_b, lse_b=lse_b, o_c=o_c, lse_c=lse_c,
               mixed=mixed, z1=z1, x1b=x1b, g=g, u=u, hb=hb, z2=z2)
    return x2, x2b, res


class _GradExchange:
    def __init__(self, core, chip):
        self.core, self.chip, self.groups, self.swapping, self._tokens = core, chip, [], [], []

    def launch(self, names, slabs, l, tag, behind):
        mine = [s.reshape((N_CHIP, 2) + s.shape[1:]) for s in slabs]
        if behind:
            landings = [lax.empty((N_CHIP,) + m.shape[2:], m.dtype) for m in mine]
            send, recv, mine, landings, token = _swap_start(mine, landings, name=f"swap_start_{tag}")
            self.swapping.append((names, l, tag, send, recv, mine, landings))
            self._tokens.append(token)
        else:
            self._exchange(names, l, tag, mine, _sibling_swap(mine, name=f"swap_grads_{tag}"))

    def advance(self, after):
        for names, l, tag, send, recv, mine, landings in self.swapping:
            mine, theirs = _swap_wait(mine, landings, send, recv, after, name=f"swap_wait_{tag}")
            self._exchange(names, l, tag, mine, theirs)
        self.swapping = []

    def _exchange(self, names, l, tag, mine, theirs):
        sums = [_pair_add(a, b, self.core, name=f"pair_add_{n}_{l}") for n, a, b in zip(names, mine, theirs)]
        landings = [lax.empty(s.shape, s.dtype) for s in sums]
        send, recv, sums, landings, token = _chip_exchange_start(sums, landings, name=f"exchange_start_{tag}")
        self.groups.append((names, l, tag, send, recv, sums, landings))
        self._tokens.append(token)

    def tokens(self):
        out, self._tokens = self._tokens, []
        return out

    def finish(self, weights, mom1, mom2, after):
        out = {}
        after = list(after) + self.tokens()
        for names, l, tag, send, recv, sums, landings in self.groups:
            sums, landings = _chip_exchange_wait(sums, landings, send, recv, after, name=f"exchange_wait_{tag}")
            for n, s, landed in zip(names, sums, landings):
                out[n] = _adamw_reduce(landed, s, self.chip, weights[n], mom1[n], mom2[n], l, out.get(n), name=f"adamw_{n}_{l}")
            after = [out[n][0] for n in names]
        return out


def _layer_bwd(dx2, res, w, lb_logits, a_norm_w, c_sink, ln1_g, conv_b, ln2_g, tabs, exchange, l):
    dz2, dz2b, d_ln2_g, d_ln2_b = _ln_bwd(res["z2"], dx2, None, ln2_g, name=f"ln2_bwd_{l}")
    exchange.advance(dz2b)
    dh = _mm(dz2b, w["w_down"], tb=True, **_TILE_WIDE_N, out_dtype=_ACT_DTYPE, after=exchange.tokens(),
             name=f"ffn_down_dx_{l}")
    d_w_down = _mm(res["hb"], dz2b, ta=True, **_TILE_WIDE_M, out_dtype=_GRAD_DTYPE, name=f"ffn_down_dw_{l}")
    dg, du, d_conv_w, d_conv_b = _conv_gate_bwd(dh, res["g"], res["u"], w["conv_w"], conv_b, name=f"conv_gate_bwd_{l}")
    t = _mm(dg, w["w_gate"], **_TILE_WIDE_K, name=f"ffn_gate_dx_{l}")
    dx1 = _mm(du, w["w_up"], **_TILE_WIDE_K, add=t, name=f"ffn_up_dx_{l}")
    d_w_gate = _mm(dg, res["x1b"], ta=True, **_TILE_WIDE_M, out_dtype=_GRAD_DTYPE, name=f"ffn_gate_dw_{l}")
    d_w_up = _mm(du, res["x1b"], ta=True, **_TILE_WIDE_M, out_dtype=_GRAD_DTYPE, name=f"ffn_up_dw_{l}")
    dz1, dz1b, d_ln1_g, d_ln1_b = _ln_bwd(res["z1"], dx1, dz2, ln1_g, name=f"ln1_bwd_{l}")
    d_w_out = _mm(res["mixed"], dz1b, ta=True, **_TILE_MIX, out_dtype=_GRAD_DTYPE, name=f"mix_out_dw_{l}")
    exchange.launch(("w_down", "w_gate", "w_up", "w_out"),
                    [d.reshape(N_DEV, d.shape[0] // N_DEV, D_MODEL) for d in (d_w_down, d_w_gate, d_w_up, d_w_out)],
                    l, f"ffn_{l}", True)
    dmixed = _mm(dz1b, w["w_out"], tb=True, **_TILE_MIX, after=exchange.tokens(), name=f"mix_out_dx_{l}")
    dq_a, df_a, di_a, dg_a, d_norm_w, d_lb = _hgrn_bwd(res["proj"], lb_logits, a_norm_w, res["raw"], res["states"],
                                                      dmixed, l, name=f"hgrn_bwd_{l}")
    exchange.advance(dq_a)
    dq_b, dk_b, dv_b = _band_bwd(res["proj"], tabs, dmixed, res["o_b"], res["lse_b"], do0=A_HEADS, after=exchange.tokens(),
                                 name=f"dilated_bwd_{l}", **_DILATED)
    dq_c, dk_c, dv_c, d_sink = _band_bwd(res["proj"], tabs, dmixed, res["o_c"], res["lse_c"], do0=A_HEADS + B_HEADS,
                                         sink=c_sink, name=f"swa_bwd_{l}", **_SWA)
    dproj = _concat_cols([dq_a, df_a, di_a, dg_a, dq_b, dk_b, dv_b, dq_c, dk_c, dv_c], name=f"dproj_{l}")
    d_w_in = _mm(dproj, res["xb"], ta=True, **_TILE_WIDE_M, out_dtype=_GRAD_DTYPE, name=f"proj_dw_{l}")
    exchange.launch(("w_in",), [d_w_in.reshape(N_DEV, SHARD_COLS, D_MODEL)], l, f"mix_{l}", l > 0)
    dx = _mm(dproj, w["w_in"], **_TILE_WIDE_K, add=dz1, add_scale=ALPHA, after=exchange.tokens(), name=f"proj_dx_{l}")
    small = [d_lb, d_norm_w, jnp.pad(d_sink, (0, _PACK_LANES - C_HEADS)), d_ln1_g, d_ln1_b, d_ln2_g, d_ln2_b, d_conv_b,
             d_conv_w]
    return dx, small


def kernel(x, w_in, lb_logits, a_norm_w, c_sinks, w_out, ln1_g, ln1_b, w_gate, w_up, conv_w, conv_b, w_down, ln2_g, ln2_b, loss_target, m_w_in, m_lb_logits, m_a_norm_w, m_c_sinks, m_w_out, m_ln1_g, m_ln1_b, m_w_gate, m_w_up, m_conv_w, m_conv_b, m_w_down, m_ln2_g, m_ln2_b, v_w_in, v_lb_logits, v_a_norm_w, v_c_sinks, v_w_out, v_ln1_g, v_ln1_b, v_w_gate, v_w_up, v_conv_w, v_conv_b, v_w_down, v_ln2_g, v_ln2_b):
    weights = dict(w_in=w_in, lb_logits=lb_logits, a_norm_w=a_norm_w, c_sinks=c_sinks, w_out=w_out, ln1_g=ln1_g, ln1_b=ln1_b,
                   w_gate=w_gate, w_up=w_up, conv_w=conv_w, conv_b=conv_b, w_down=w_down, ln2_g=ln2_g, ln2_b=ln2_b)
    mom1 = dict(w_in=m_w_in, lb_logits=m_lb_logits, a_norm_w=m_a_norm_w, c_sinks=m_c_sinks, w_out=m_w_out, ln1_g=m_ln1_g,
                ln1_b=m_ln1_b, w_gate=m_w_gate, w_up=m_w_up, conv_w=m_conv_w, conv_b=m_conv_b, w_down=m_w_down, ln2_g=m_ln2_g,
                ln2_b=m_ln2_b)
    mom2 = dict(w_in=v_w_in, lb_logits=v_lb_logits, a_norm_w=v_a_norm_w, c_sinks=v_c_sinks, w_out=v_w_out, ln1_g=v_ln1_g,
                ln1_b=v_ln1_b, w_gate=v_w_gate, w_up=v_w_up, conv_w=v_conv_w, conv_b=v_conv_b, w_down=v_w_down, ln2_g=v_ln2_g,
                ln2_b=v_ln2_b)
    core = lax.axis_index("c").astype(jnp.int32)
    me = 4 * lax.axis_index("x") + 2 * lax.axis_index("y") + core
    tabs = _rope_tables()

    chip = (2 * lax.axis_index("x") + lax.axis_index("y")).astype(jnp.int32)

    def as_slabs(d):
        return {n: jnp.swapaxes(d[n], 1, 2) if n in _COLUMN_SHARDED else d[n] for n in _BIG}

    w_views = as_slabs(weights)

    def block(n, l, after=()):
        return conv_w[l] if n == "conv_w" else _cast_layer(w_views[n], l, after=after, name=f"cast_{n}_{l}")

    (in0,), started_first = _start_gathers([("w_in_0", ("w_in",), [block("w_in", 0)])], me, "gather_start_first")
    order = [(("w_out",), 0), (("w_gate", "w_up", "conv_w"), 0), (("w_down",), 0),
             (("w_in",), 1), (("w_out",), 1), (("w_gate", "w_up", "conv_w"), 1), (("w_down",), 1)]
    gathers, started = _start_gathers([(f"{names[0]}_{l}", names, [block(n, l, [started_first]) for n in names])
                                       for names, l in order], me, "gather_start_rest")
    out0, ffn0, down0, in1, out1, ffn1, down1 = gathers
    layer_ws = [_LayerWeights(in0.wait(started), [out0, ffn0, down0],
                              [("hgrn", out0), ("swa", ffn0), ("up", down0), ("down", in1)]),
                _LayerWeights({}, [in1, out1, ffn1, down1], [("hgrn", out1), ("swa", ffn1), ("up", down1)])]

    xs = x[0]
    xb = xs.astype(_MXU_DTYPE)
    saved = []
    for l in range(DEPTH):
        xs, xb, res = _layer_fwd(xs, xb, layer_ws[l], lb_logits, a_norm_w[l], c_sinks[l], ln1_g[l], ln1_b[l], conv_b[l],
                                 ln2_g[l], ln2_b[l], tabs, l)
        saved.append(res)
    loss_part, dx = _loss_head(xs, loss_target[0], name="loss_head")
    loss = lax.psum(loss_part, ("x", "y", "c"))

    exchange = _GradExchange(core, chip)
    small_parts = [None] * DEPTH
    for l in reversed(range(DEPTH)):
        dx, small = _layer_bwd(dx, saved[l], layer_ws[l].ready, lb_logits, a_norm_w[l], c_sinks[l], ln1_g[l], conv_b[l],
                               ln2_g[l], tabs, exchange, l)
        small_parts[l] = _pack(small, _LAYER_ROWS)
    (small_gather,), small_started = _start_gathers(
        [("small_grads", ("small",), [jnp.concatenate(small_parts, axis=0)])], me, "gather_start_small")
    updated = exchange.finish(w_views, as_slabs(mom1), as_slabs(mom2), [dx, small_started])
    gathered = small_gather.wait(updated["w_in"][0])["small"]
    updated = {n: tuple(jnp.swapaxes(t, 1, 2) for t in u) if n in _COLUMN_SHARDED else u for n, u in updated.items()}
    g_small = _small_reduce(gathered, lb_logits, name="small_grads")

    per_layer = [(A_HEADS * HEAD_DIM,), (HEAD_DIM,), (_PACK_LANES,), (D_MODEL,), (D_MODEL,), (D_MODEL,), (D_MODEL,), (D_FF,),
                 (3, D_FF)]
    names = ("lb_logits", "a_norm_w", "c_sinks", "ln1_g", "ln1_b", "ln2_g", "ln2_b", "conv_b", "conv_w")
    grads = {n: [] for n in names}
    for l in range(DEPTH):
        for n, t in zip(names, _unpack(g_small[l * _LAYER_ROWS:(l + 1) * _LAYER_ROWS], per_layer)):
            grads[n].append(t)
    grads = {n: jnp.stack(t) for n, t in grads.items()}
    grads["c_sinks"] = grads["c_sinks"][:, :C_HEADS]
    grads["conv_w"] = lax.dynamic_slice_in_dim(grads["conv_w"], me * SHARD_COLS, SHARD_COLS, axis=2)
    shapes = [grads[n].shape for n in names]
    rows = -(-sum(int(np.prod(s)) for s in shapes) // (8 * _PACK_LANES)) * 8
    d_s, m_s, v_s = _adamw_small(_pack([grads[n] for n in names], rows), _pack([weights[n] for n in names], rows),
                                 _pack([mom1[n] for n in names], rows), _pack([mom2[n] for n in names], rows),
                                 name="adamw_small")
    delta = dict(zip(names, _unpack(d_s, shapes)))
    new_m = dict(zip(names, _unpack(m_s, shapes)))
    new_v = dict(zip(names, _unpack(v_s, shapes)))
    for n in _BIG:
        grads[n], delta[n], new_m[n], new_v[n] = updated[n]

    order = ("w_in", "lb_logits", "a_norm_w", "c_sinks", "w_out", "ln1_g", "ln1_b", "w_gate", "w_up", "conv_w", "conv_b",
             "w_down", "ln2_g", "ln2_b")
    return (loss, dx[None], *[grads[n] for n in order], *[delta[n] for n in order], *[new_m[n] for n in order],
            *[new_v[n] for n in order])
```

```python
import functools

import jax
import jax.numpy as jnp
import numpy as np
from jax import lax
from jax.experimental import pallas as pl
from jax.experimental.pallas import tpu as pltpu

D_MODEL = 2048
SEQ = 2048
DEPTH = 2
HEAD_DIM = 128
A_HEADS = 4
B_HEADS = 6
C_HEADS = 6
C_KV_HEADS = 2
A_CHUNK = 16
DILATIONS = (1, 4, 16)
BLOCK = 128
ROPE_THETA = 500000.0
ROPE_DIM = 32
D_FF = 5632
IN_WIDTH = 5632
LN_EPS = 1e-5
ALPHA = (2 * DEPTH) ** 0.25
N_DEV = 8
SHARD_COLS = IN_WIDTH // N_DEV

ADAM_LR = 0.001
ADAM_B1 = 0.9
ADAM_B2 = 0.999
ADAM_EPS = 1e-08
ADAM_WD = 0.01
ADAM_STEP = 10

A_COLS = 16
QKV_COLS = 28
QB0, KB0, VB0, QC0, KC0, VC0 = 0, 6, 12, 18, 24, 26

_MXU_DTYPE = jnp.bfloat16
_GRAD_DTYPE = jnp.bfloat16
_ACT_DTYPE = jnp.bfloat16
_NEG = -1e30
_VMEM_LIMIT = 56 * 2 ** 20

_F32 = jnp.float32


def _sigmoid(x):
    return 0.5 * jnp.tanh(0.5 * x) + 0.5


def _cparams(**kw):
    return pltpu.CompilerParams(vmem_limit_bytes=_VMEM_LIMIT, **kw)


_TILE_MIX = dict(tm=1024, tn=1024)
_TILE_WIDE_K = dict(tm=1024, tn=512)
_TILE_WIDE_N = dict(tm=1024, tn=1408)
_TILE_WIDE_M = dict(tm=1408, tn=1024)


def _mm(a, b, *, ta=False, tb=False, tm, tn, out_dtype=_F32, add=None, add_scale=1.0, after=(), name):
    K = a.shape[0] if ta else a.shape[1]
    M = a.shape[1] if ta else a.shape[0]
    N = b.shape[0] if tb else b.shape[1]
    assert (b.shape[1] if tb else b.shape[0]) == K and M % tm == 0 and N % tn == 0
    dn = (((0 if ta else 1,), (1 if tb else 0,)), ((), ()))

    def body(*refs):
        a_ref, b_ref = refs[:2]
        o_ref = refs[-1]
        r = lax.dot_general(a_ref[...], b_ref[...], dn, preferred_element_type=_F32)
        if add is not None:
            r = r + add_scale * refs[2][...]
        o_ref[...] = r.astype(o_ref.dtype)

    a_spec = pl.BlockSpec((K, tm), lambda i, j: (0, i)) if ta else pl.BlockSpec((tm, K), lambda i, j: (i, 0))
    b_spec = pl.BlockSpec((tn, K), lambda i, j: (j, 0)) if tb else pl.BlockSpec((K, tn), lambda i, j: (0, j))
    o_spec = pl.BlockSpec((tm, tn), lambda i, j: (i, j))
    in_specs = [a_spec, b_spec] + ([o_spec] if add is not None else []) + [pl.BlockSpec(memory_space=pl.ANY)] * len(after)
    args = (a, b) + ((add,) if add is not None else ()) + tuple(after)
    return pl.pallas_call(
        body, grid=(M // tm, N // tn), in_specs=in_specs, out_specs=o_spec,
        out_shape=jax.ShapeDtypeStruct((M, N), out_dtype), name=name,
        compiler_params=_cparams(dimension_semantics=("parallel", "parallel")),
    )(*args)


def _cast_layer(w, layer, *, after=(), name):
    _, R, C = w.shape
    tr = max(t for t in range(16, R + 1, 16) if R % t == 0 and t * C <= 512 * 1024)

    def body(w_ref, *rest):
        o_ref = rest[-1]
        o_ref[...] = w_ref[...].astype(o_ref.dtype)

    return pl.pallas_call(
        body, grid=(R // tr,),
        in_specs=[pl.BlockSpec((None, tr, C), lambda i: (layer, i, 0))] + [pl.BlockSpec(memory_space=pl.ANY)] * len(after),
        out_specs=pl.BlockSpec((tr, C), lambda i: (i, 0)), out_shape=jax.ShapeDtypeStruct((R, C), _MXU_DTYPE), name=name,
        compiler_params=_cparams(dimension_semantics=("parallel",)),
    )(w, *after)


def _concat_cols(pieces, *, name):
    tm = 512
    widths = [p.shape[1] for p in pieces]
    offs = np.cumsum([0] + widths)

    def body(*refs):
        o_ref = refs[-1]
        for p_ref, off, w in zip(refs[:-1], offs, widths):
            o_ref[:, off:off + w] = p_ref[...].astype(o_ref.dtype)

    return pl.pallas_call(
        body, grid=(SEQ // tm,), in_specs=[pl.BlockSpec((tm, w), lambda i: (i, 0)) for w in widths],
        out_specs=pl.BlockSpec((tm, int(offs[-1])), lambda i: (i, 0)),
        out_shape=jax.ShapeDtypeStruct((SEQ, int(offs[-1])), _MXU_DTYPE), name=name,
        compiler_params=_cparams(dimension_semantics=("parallel",)),
    )(*pieces)


def _ln_fwd(x, y, g, b, *, name):
    tm = 256

    def body(x_ref, y_ref, g_ref, b_ref, z_ref, o_ref, ob_ref):
        z = ALPHA * x_ref[...] + y_ref[...]
        mu = jnp.mean(z, axis=-1, keepdims=True)
        zc = z - mu
        var = jnp.mean(zc * zc, axis=-1, keepdims=True)
        o = zc * lax.rsqrt(var + LN_EPS) * g_ref[...] + b_ref[...]
        z_ref[...] = z
        o_ref[...] = o
        ob_ref[...] = o.astype(ob_ref.dtype)

    row = pl.BlockSpec((tm, D_MODEL), lambda i: (i, 0))
    vec = pl.BlockSpec((1, D_MODEL), lambda i: (0, 0))
    return pl.pallas_call(
        body, grid=(SEQ // tm,), in_specs=[row, row, vec, vec], out_specs=[row, row, row],
        out_shape=[jax.ShapeDtypeStruct((SEQ, D_MODEL), _F32), jax.ShapeDtypeStruct((SEQ, D_MODEL), _F32),
                   jax.ShapeDtypeStruct((SEQ, D_MODEL), _MXU_DTYPE)],
        name=name, compiler_params=_cparams(dimension_semantics=("parallel",)),
    )(x, y, g.reshape(1, D_MODEL), b.reshape(1, D_MODEL))


def _ln_bwd(z, d_a, d_res, g, *, name):
    tm = 256

    def body(*refs):
        if d_res is None:
            z_ref, da_ref, g_ref, dz_ref, dzb_ref, dg_ref, db_ref = refs
        else:
            z_ref, da_ref, dr_ref, g_ref, dz_ref, dzb_ref, dg_ref, db_ref = refs

        @pl.when(pl.program_id(0) == 0)
        def _():
            dg_ref[...] = jnp.zeros_like(dg_ref)
            db_ref[...] = jnp.zeros_like(db_ref)

        dout = da_ref[...]
        if d_res is not None:
            dout = dout + ALPHA * dr_ref[...]
        z = z_ref[...]
        mu = jnp.mean(z, axis=-1, keepdims=True)
        zc = z - mu
        var = jnp.mean(zc * zc, axis=-1, keepdims=True)
        rstd = lax.rsqrt(var + LN_EPS)
        xh = zc * rstd
        dxh = dout * g_ref[...]
        m1 = jnp.mean(dxh, axis=-1, keepdims=True)
        m2 = jnp.mean(dxh * xh, axis=-1, keepdims=True)
        dz = rstd * (dxh - m1 - xh * m2)
        dz_ref[...] = dz
        dzb_ref[...] = dz.astype(dzb_ref.dtype)
        dg_ref[0:1, :] += jnp.sum(dout * xh, axis=0, keepdims=True)
        db_ref[0:1, :] += jnp.sum(dout, axis=0, keepdims=True)

    row = pl.BlockSpec((tm, D_MODEL), lambda i: (i, 0))
    vec = pl.BlockSpec((1, D_MODEL), lambda i: (0, 0))
    acc = pl.BlockSpec((8, D_MODEL), lambda i: (0, 0))
    ins = [z, d_a] + ([d_res] if d_res is not None else []) + [g.reshape(1, D_MODEL)]
    in_specs = [row, row] + ([row] if d_res is not None else []) + [vec]
    dz, dzb, dg, db = pl.pallas_call(
        body, grid=(SEQ // tm,), in_specs=in_specs, out_specs=[row, row, acc, acc],
        out_shape=[jax.ShapeDtypeStruct((SEQ, D_MODEL), _F32), jax.ShapeDtypeStruct((SEQ, D_MODEL), _MXU_DTYPE),
                   jax.ShapeDtypeStruct((8, D_MODEL), _F32), jax.ShapeDtypeStruct((8, D_MODEL), _F32)],
        name=name, compiler_params=_cparams(dimension_semantics=("arbitrary",)),
    )(*ins)
    return dz, dzb, dg[0], db[0]


def _ln_loss_bwd(x, y, g, b, target, *, name):
    tm = 256

    def body(x_ref, y_ref, g_ref, b_ref, t_ref, dz_ref, dzb_ref, dg_ref, db_ref, l_ref):
        @pl.when(pl.program_id(0) == 0)
        def _():
            dg_ref[...] = jnp.zeros_like(dg_ref)
            db_ref[...] = jnp.zeros_like(db_ref)
            l_ref[...] = jnp.zeros_like(l_ref)

        z = ALPHA * x_ref[...] + y_ref[...]
        mu = jnp.mean(z, axis=-1, keepdims=True)
        zc = z - mu
        var = jnp.mean(zc * zc, axis=-1, keepdims=True)
        rstd = lax.rsqrt(var + LN_EPS)
        xh = zc * rstd
        e = xh * g_ref[...] + b_ref[...] - t_ref[...]
        l_ref[...] += (0.5 / D_MODEL) * jnp.sum(e * e)
        dout = e * (1.0 / D_MODEL)
        dxh = dout * g_ref[...]
        m1 = jnp.mean(dxh, axis=-1, keepdims=True)
        m2 = jnp.mean(dxh * xh, axis=-1, keepdims=True)
        dz = rstd * (dxh - m1 - xh * m2)
        dz_ref[...] = dz
        dzb_ref[...] = dz.astype(dzb_ref.dtype)
        dg_ref[0:1, :] += jnp.sum(dout * xh, axis=0, keepdims=True)
        db_ref[0:1, :] += jnp.sum(dout, axis=0, keepdims=True)

    row = pl.BlockSpec((tm, D_MODEL), lambda i: (i, 0))
    vec = pl.BlockSpec((1, D_MODEL), lambda i: (0, 0))
    acc = pl.BlockSpec((8, D_MODEL), lambda i: (0, 0))
    dz, dzb, dg, db, part = pl.pallas_call(
        body, grid=(SEQ // tm,), in_specs=[row, row, vec, vec, row],
        out_specs=[row, row, acc, acc, pl.BlockSpec((8, 128), lambda i: (0, 0))],
        out_shape=[jax.ShapeDtypeStruct((SEQ, D_MODEL), _F32), jax.ShapeDtypeStruct((SEQ, D_MODEL), _MXU_DTYPE),
                   jax.ShapeDtypeStruct((8, D_MODEL), _F32), jax.ShapeDtypeStruct((8, D_MODEL), _F32),
                   jax.ShapeDtypeStruct((8, 128), _F32)],
        name=name, compiler_params=_cparams(dimension_semantics=("arbitrary",)),
    )(x, y, g.reshape(1, D_MODEL), b.reshape(1, D_MODEL), target)
    return part[0, 0], dz, dzb, dg[0], db[0]


_CONV_TN = 256


def _shift_down(v, k, rows):
    return jnp.where(rows >= k, pltpu.roll(v, k, axis=0), 0.0)


def _shift_up(v, k, rows):
    return jnp.where(rows < SEQ - k, pltpu.roll(v, SEQ - k, axis=0), 0.0)


def _conv_gate_fwd(g, u, conv_w, conv_b, *, name):
    def body(g_ref, u_ref, w_ref, b_ref, h_ref):
        gv = g_ref[...].astype(_F32)
        rows = lax.broadcasted_iota(jnp.int32, gv.shape, 0)
        w = w_ref[...]
        gc = b_ref[...] + w[2:3, :] * gv + w[1:2, :] * _shift_down(gv, 1, rows) + w[0:1, :] * _shift_down(gv, 2, rows)
        h_ref[...] = (gc * _sigmoid(gc) * u_ref[...].astype(_F32)).astype(h_ref.dtype)

    col = pl.BlockSpec((SEQ, _CONV_TN), lambda j: (0, j))
    return pl.pallas_call(
        body, grid=(D_FF // _CONV_TN,),
        in_specs=[col, col, pl.BlockSpec((3, _CONV_TN), lambda j: (0, j)), pl.BlockSpec((1, _CONV_TN), lambda j: (0, j))],
        out_specs=col, out_shape=jax.ShapeDtypeStruct((SEQ, D_FF), _MXU_DTYPE), name=name,
        compiler_params=_cparams(dimension_semantics=("parallel",)),
    )(g, u, conv_w, conv_b.reshape(1, D_FF))


def _conv_gate_bwd(dh, g, u, conv_w, conv_b, *, name):
    def body(dh_ref, g_ref, u_ref, w_ref, b_ref, dg_ref, du_ref, dw_ref, db_ref):
        gv = g_ref[...].astype(_F32)
        rows = lax.broadcasted_iota(jnp.int32, gv.shape, 0)
        w = w_ref[...]
        g1 = _shift_down(gv, 1, rows)
        g2 = _shift_down(gv, 2, rows)
        gc = b_ref[...] + w[2:3, :] * gv + w[1:2, :] * g1 + w[0:1, :] * g2
        sg = _sigmoid(gc)
        dh = dh_ref[...].astype(_F32)
        du_ref[...] = (dh * (gc * sg)).astype(du_ref.dtype)
        dgc = dh * u_ref[...].astype(_F32) * (sg * (1.0 + gc * (1.0 - sg)))
        dg = w[2:3, :] * dgc + w[1:2, :] * _shift_up(dgc, 1, rows) + w[0:1, :] * _shift_up(dgc, 2, rows)
        dg_ref[...] = dg.astype(dg_ref.dtype)
        dw_ref[0:1, :] = jnp.sum(dgc * g2, axis=0, keepdims=True)
        dw_ref[1:2, :] = jnp.sum(dgc * g1, axis=0, keepdims=True)
        dw_ref[2:3, :] = jnp.sum(dgc * gv, axis=0, keepdims=True)
        db_ref[...] = jnp.sum(dgc, axis=0, keepdims=True)

    col = pl.BlockSpec((SEQ, _CONV_TN), lambda j: (0, j))
    w3 = pl.BlockSpec((3, _CONV_TN), lambda j: (0, j))
    w1 = pl.BlockSpec((1, _CONV_TN), lambda j: (0, j))
    dg, du, dw, db = pl.pallas_call(
        body, grid=(D_FF // _CONV_TN,), in_specs=[col, col, col, w3, w1], out_specs=[col, col, w3, w1],
        out_shape=[jax.ShapeDtypeStruct((SEQ, D_FF), _MXU_DTYPE), jax.ShapeDtypeStruct((SEQ, D_FF), _MXU_DTYPE),
                   jax.ShapeDtypeStruct((3, D_FF), _F32), jax.ShapeDtypeStruct((1, D_FF), _F32)],
        name=name, compiler_params=_cparams(dimension_semantics=("parallel",)),
    )(dh, g, u, conv_w, conv_b.reshape(1, D_FF))
    return dg, du, dw, db[0]


def _rope_tables():
    half = ROPE_DIM // 2
    inv = ROPE_THETA ** (-jnp.arange(0, ROPE_DIM, 2, dtype=_F32) / ROPE_DIM)
    ang = jnp.arange(SEQ, dtype=_F32)[:, None] * inv[None, :]
    cos, sin = jnp.cos(ang), jnp.sin(ang)
    rest = HEAD_DIM - ROPE_DIM
    c = jnp.concatenate([cos, cos, jnp.ones((SEQ, rest), _F32)], axis=1)
    s1 = jnp.concatenate([-sin, jnp.zeros((SEQ, HEAD_DIM - half), _F32)], axis=1)
    s2 = jnp.concatenate([jnp.zeros((SEQ, half), _F32), sin, jnp.zeros((SEQ, rest), _F32)], axis=1)
    return c, s1, s2


def _rope_apply(x, c, s1, s2):
    return x * c + pltpu.roll(x, HEAD_DIM - ROPE_DIM // 2, axis=1) * s1 + pltpu.roll(x, ROPE_DIM // 2, axis=1) * s2


def _rope_transpose(d, c, s1, s2):
    half = ROPE_DIM // 2
    return d * c + pltpu.roll(d * s1, half, axis=1) + pltpu.roll(d * s2, HEAD_DIM - half, axis=1)


_NT = (((1,), (1,)), ((), ()))
_TN = (((0,), (0,)), ((), ()))
_SCALE = HEAD_DIM ** -0.5


def _band_scores(q, k2, n, lag_off):
    s = lax.dot_general(q, k2, _NT, preferred_element_type=_F32) * _SCALE
    row = lax.broadcasted_iota(jnp.int32, (BLOCK, 2 * BLOCK), 0)
    col = lax.broadcasted_iota(jnp.int32, (BLOCK, 2 * BLOCK), 1)
    front = (col >= row + lag_off) & (col < BLOCK) & (n > 0)
    own = (col >= BLOCK) & (col <= row + BLOCK)
    return jnp.where(front | own, s, _NEG)


_BAND_STEPS = SEQ // BLOCK


def _rows(start, d):
    if d == 1:
        return pl.ds(pl.multiple_of(start, BLOCK), BLOCK)
    return pl.ds(start, BLOCK, stride=d)


def _band_block(it, d):
    r, n = it % d, it // d
    span = BLOCK * d
    return n, _rows(r + n * span, d), _rows(r + jnp.maximum(n - 1, 0) * span, d)


def _band_fwd(proj, tabs, *, kv_heads, q_per_kv, q0, k0, v0, dilations, lag_off, sink, name):
    heads = kv_heads * q_per_kv

    def body(*refs):
        q_refs = refs[:q_per_kv]
        k_ref, v_ref, c_ref, s1_ref, s2_ref = refs[q_per_kv:q_per_kv + 5]
        rest = refs[q_per_kv + 5:]
        if sink is not None:
            sk_ref, rest = rest[0], rest[1:]
        o_ref, lse_ref, qs, ks, m_s, l_s, acc_s = rest
        c, s1, s2 = c_ref[...], s1_ref[...], s2_ref[...]
        ks[...] = _rope_apply(k_ref[...], c, s1, s2)
        for i in range(q_per_kv):
            qs[...] = _rope_apply(q_refs[i][...], c, s1, s2)
            for pi, d in enumerate(dilations):
                def step(it, carry, d=d, first=(pi == 0)):
                    n, cur, prev = _band_block(it, d)
                    q = qs[cur, :].astype(_MXU_DTYPE)
                    k2 = jnp.concatenate([ks[prev, :], ks[cur, :]], axis=0).astype(_MXU_DTYPE)
                    v2 = jnp.concatenate([v_ref[prev, :], v_ref[cur, :]], axis=0).astype(_MXU_DTYPE)
                    s = _band_scores(q, k2, n, lag_off)
                    m_b = jnp.max(s, axis=1, keepdims=True)
                    m_new = m_b if first else jnp.maximum(m_b, m_s[cur, :][:, 0:1])
                    p = jnp.exp(s - m_new)
                    l_new = jnp.sum(p, axis=1, keepdims=True)
                    acc = jnp.dot(p.astype(_MXU_DTYPE), v2, preferred_element_type=_F32)
                    if not first:
                        a = jnp.exp(m_s[cur, :][:, 0:1] - m_new)
                        l_new = l_new + a * l_s[cur, :][:, 0:1]
                        acc = acc + a * acc_s[cur, :]
                    m_s[cur, :] = jnp.broadcast_to(m_new, (BLOCK, HEAD_DIM))
                    l_s[cur, :] = jnp.broadcast_to(l_new, (BLOCK, HEAD_DIM))
                    acc_s[cur, :] = acc
                    return carry

                lax.fori_loop(0, _BAND_STEPS, step, 0, unroll=16)
            m, den = m_s[...], l_s[...]
            if sink is not None:
                sk = sk_ref[i]
                m_f = jnp.maximum(m, sk)
                a = jnp.exp(m - m_f)
                den = den * a + jnp.exp(sk - m_f)
                o = acc_s[...] * a / den
                m = m_f
            else:
                o = acc_s[...] / den
            o_ref[:, i * HEAD_DIM:(i + 1) * HEAD_DIM] = o
            lse_ref[:, i * HEAD_DIM:(i + 1) * HEAD_DIM] = m + jnp.log(den)

    col = (SEQ, HEAD_DIM)
    in_specs = [pl.BlockSpec(col, functools.partial(lambda g, i: (0, A_COLS + q0 + g * q_per_kv + i), i=i)) for i in range(q_per_kv)]
    in_specs += [pl.BlockSpec(col, lambda g: (0, A_COLS + k0 + g)), pl.BlockSpec(col, lambda g: (0, A_COLS + v0 + g))]
    in_specs += [pl.BlockSpec(col, lambda g: (0, 0))] * 3
    args = [proj] * (q_per_kv + 2) + list(tabs)
    if sink is not None:
        in_specs.append(pl.BlockSpec((q_per_kv, 1, HEAD_DIM), lambda g: (g, 0, 0)))
        args.append(jnp.broadcast_to(sink.reshape(heads, 1, 1), (heads, 1, HEAD_DIM)))
    o_spec = pl.BlockSpec((SEQ, q_per_kv * HEAD_DIM), lambda g: (0, g))
    shape = jax.ShapeDtypeStruct((SEQ, heads * HEAD_DIM), _F32)
    return pl.pallas_call(
        body, grid=(kv_heads,), in_specs=in_specs, out_specs=[o_spec, o_spec], out_shape=[shape, shape],
        scratch_shapes=[pltpu.VMEM(col, _F32)] * 5, name=name,
        compiler_params=_cparams(dimension_semantics=("parallel",)),
    )(*args)


def _band_bwd(proj, tabs, dmixed, o, lse, *, kv_heads, q_per_kv, q0, k0, v0, do0, dilations, lag_off, sink, after=(), name):
    heads = kv_heads * q_per_kv

    def body(*refs):
        q_refs = refs[:q_per_kv]
        k_ref, v_ref, c_ref, s1_ref, s2_ref = refs[q_per_kv:q_per_kv + 5]
        do_refs = refs[q_per_kv + 5:2 * q_per_kv + 5]
        o_ref, lse_ref = refs[2 * q_per_kv + 5:2 * q_per_kv + 7]
        rest = refs[2 * q_per_kv + 7:]
        if sink is not None:
            sk_ref, rest = rest[0], rest[1:]
            dq_ref, dk_ref, dv_ref, dsk_ref, qs, ks, dq_s, dk_s, dv_s = rest[len(after):]
        else:
            dq_ref, dk_ref, dv_ref, qs, ks, dq_s, dk_s, dv_s = rest[len(after):]
        c, s1, s2 = c_ref[...], s1_ref[...], s2_ref[...]
        ks[...] = _rope_apply(k_ref[...], c, s1, s2)
        dk_s[...] = jnp.zeros_like(dk_s)
        dv_s[...] = jnp.zeros_like(dv_s)
        for i in range(q_per_kv):
            hs = slice(i * HEAD_DIM, (i + 1) * HEAD_DIM)
            qs[...] = _rope_apply(q_refs[i][...], c, s1, s2)
            dq_s[...] = jnp.zeros_like(dq_s)
            do_ref = do_refs[i]
            for d in dilations:
                def step(it, carry, d=d, do_ref=do_ref, hs=hs):
                    n, cur, prev = _band_block(it, d)
                    q = qs[cur, :].astype(_MXU_DTYPE)
                    k2 = jnp.concatenate([ks[prev, :], ks[cur, :]], axis=0).astype(_MXU_DTYPE)
                    v2 = jnp.concatenate([v_ref[prev, :], v_ref[cur, :]], axis=0).astype(_MXU_DTYPE)
                    do = do_ref[cur, :]
                    delta = jnp.sum(do * o_ref[cur, hs], axis=1, keepdims=True)
                    lse_c = lse_ref[cur, hs][:, 0:1]
                    p = jnp.exp(_band_scores(q, k2, n, lag_off) - lse_c)
                    dob = do.astype(_MXU_DTYPE)
                    ds = (p * (lax.dot_general(dob, v2, _NT, preferred_element_type=_F32) - delta) * _SCALE).astype(_MXU_DTYPE)
                    dq_s[cur, :] += jnp.dot(ds, k2, preferred_element_type=_F32)
                    dk2 = lax.dot_general(ds, q, _TN, preferred_element_type=_F32)
                    dv2 = lax.dot_general(p.astype(_MXU_DTYPE), dob, _TN, preferred_element_type=_F32)
                    dk_s[prev, :] += dk2[:BLOCK]
                    dv_s[prev, :] += dv2[:BLOCK]
                    dk_s[cur, :] += dk2[BLOCK:]
                    dv_s[cur, :] += dv2[BLOCK:]
                    return carry

                lax.fori_loop(0, _BAND_STEPS, step, 0, unroll=16)
            dq_ref[:, hs] = _rope_transpose(dq_s[...], c, s1, s2).astype(dq_ref.dtype)
            if sink is not None:
                delta = jnp.sum(do_ref[...] * o_ref[:, hs], axis=1, keepdims=True)
                w_sink = jnp.exp(sk_ref[i] - lse_ref[:, hs])
                dsk_ref[i] = jnp.broadcast_to(jnp.sum(-delta * w_sink[:, 0:1]), (8, HEAD_DIM))
        dk_ref[...] = _rope_transpose(dk_s[...], c, s1, s2).astype(dk_ref.dtype)
        dv_ref[...] = dv_s[...].astype(dv_ref.dtype)

    col = (SEQ, HEAD_DIM)
    in_specs = [pl.BlockSpec(col, functools.partial(lambda g, i: (0, A_COLS + q0 + g * q_per_kv + i), i=i)) for i in range(q_per_kv)]
    in_specs += [pl.BlockSpec(col, lambda g: (0, A_COLS + k0 + g)), pl.BlockSpec(col, lambda g: (0, A_COLS + v0 + g))]
    in_specs += [pl.BlockSpec(col, lambda g: (0, 0))] * 3
    in_specs += [pl.BlockSpec(col, functools.partial(lambda g, i: (0, do0 + g * q_per_kv + i), i=i)) for i in range(q_per_kv)]
    wide = pl.BlockSpec((SEQ, q_per_kv * HEAD_DIM), lambda g: (0, g))
    in_specs += [wide, wide]
    args = [proj] * (q_per_kv + 2) + list(tabs) + [dmixed] * q_per_kv + [o, lse]
    out_specs = [wide, pl.BlockSpec(col, lambda g: (0, g)), pl.BlockSpec(col, lambda g: (0, g))]
    out_shape = [jax.ShapeDtypeStruct((SEQ, heads * HEAD_DIM), _MXU_DTYPE), jax.ShapeDtypeStruct((SEQ, kv_heads * HEAD_DIM), _MXU_DTYPE),
                 jax.ShapeDtypeStruct((SEQ, kv_heads * HEAD_DIM), _MXU_DTYPE)]
    if sink is not None:
        in_specs.append(pl.BlockSpec((q_per_kv, 1, HEAD_DIM), lambda g: (g, 0, 0)))
        args.append(jnp.broadcast_to(sink.reshape(heads, 1, 1), (heads, 1, HEAD_DIM)))
        out_specs.append(pl.BlockSpec((q_per_kv, 8, HEAD_DIM), lambda g: (g, 0, 0)))
        out_shape.append(jax.ShapeDtypeStruct((heads, 8, HEAD_DIM), _F32))
    in_specs += [pl.BlockSpec(memory_space=pl.ANY)] * len(after)
    args += list(after)
    res = pl.pallas_call(
        body, grid=(kv_heads,), in_specs=in_specs, out_specs=out_specs, out_shape=out_shape,
        scratch_shapes=[pltpu.VMEM(col, _F32)] * 5, name=name,
        compiler_params=_cparams(dimension_semantics=("parallel",)),
    )(*args)
    if sink is not None:
        return res[0], res[1], res[2], res[3][:, 0, 0]
    return res


_DILATED = dict(kv_heads=B_HEADS, q_per_kv=1, q0=QB0, k0=KB0, v0=VB0, dilations=DILATIONS, lag_off=0, sink=None)
_SWA = dict(kv_heads=C_KV_HEADS, q_per_kv=C_HEADS // C_KV_HEADS, q0=QC0, k0=KC0, v0=VC0, dilations=(1,), lag_off=1)


_HG_TILE = 128
_HG_CHUNKS = _HG_TILE // A_CHUNK
_HG_TILES = SEQ // _HG_TILE
_HI = lax.Precision.HIGHEST


def _chunk_tri():
    i = np.arange(_HG_TILE)
    return jnp.asarray(((i[:, None] // A_CHUNK == i[None, :] // A_CHUNK) & (i[None, :] <= i[:, None])).astype(np.float32))


def _layer_lb(lb_ref, layer):
    if layer == 0:
        return jnp.zeros((1, HEAD_DIM), _F32)
    lg = lb_ref[...]
    m = jnp.max(lg, axis=0, keepdims=True)
    e = jnp.exp(lg - m)
    return e[1:2, :] / jnp.sum(e, axis=0, keepdims=True)


def _hgrn_gates(q, fr, lb):
    sgq = _sigmoid(q)
    sg = _sigmoid(fr)
    f = lb + (1.0 - lb) * sg
    return sgq, q * sgq, sg, f, 1.0 - f


def _hgrn_fwd(proj, lb_logits, norm_w, layer, *, name):
    tri = _chunk_tri()

    def body(q_ref, f_ref, i_ref, g_ref, lb_ref, nw_ref, tri_ref, o_ref, raw_ref, st_ref, state):
        @pl.when(pl.program_id(1) == 0)
        def _():
            state[...] = jnp.zeros_like(state)

        lb = _layer_lb(lb_ref, layer)
        _, qs, _, f, k = _hgrn_gates(q_ref[...], f_ref[...], lb)
        v = i_ref[...]
        b = jnp.dot(tri_ref[...], jnp.log(f), precision=_HI, preferred_element_type=_F32)
        eb = jnp.exp(b)
        ridx = lax.broadcasted_iota(jnp.int32, (A_CHUNK, HEAD_DIM), 0)
        outs = []
        st = state[...]
        for c in range(_HG_CHUNKS):
            sl = slice(c * A_CHUNK, (c + 1) * A_CHUNK)
            bc, qc, kc, vc = b[sl], qs[sl], k[sl], v[sl]
            bl = bc[A_CHUNK - 1:A_CHUNK]
            st_ref[0, c] = st
            o_c = lax.dot_general((qc * eb[sl]).astype(_MXU_DTYPE), st.astype(_MXU_DTYPE), _NT, preferred_element_type=_F32)
            rows = []
            for i in range(A_CHUNK):
                di = jnp.exp(jnp.where(ridx <= i, bc[i:i + 1] - bc, _NEG))
                a = jnp.sum(qc[i:i + 1] * kc * di, axis=1, keepdims=True)
                rows.append(jnp.sum(a * vc, axis=0, keepdims=True))
            outs.append(o_c + jnp.concatenate(rows, axis=0))
            kt = (kc * jnp.exp(bl - bc)).astype(_MXU_DTYPE)
            st = st * jnp.exp(bl) + lax.dot_general(vc.astype(_MXU_DTYPE), kt, _TN, preferred_element_type=_F32)
        state[...] = st
        o = jnp.concatenate(outs, axis=0)
        raw_ref[...] = o
        r = lax.rsqrt(jnp.mean(o * o, axis=-1, keepdims=True) + LN_EPS)
        g = g_ref[...]
        o_ref[...] = o * r * nw_ref[...] * (g * _sigmoid(g))

    blk = (_HG_TILE, HEAD_DIM)

    def col(base):
        return pl.BlockSpec(blk, lambda h, t: (t, base + h))

    o_spec = pl.BlockSpec(blk, lambda h, t: (t, h))
    o_shape = jax.ShapeDtypeStruct((SEQ, A_HEADS * HEAD_DIM), _F32)
    return pl.pallas_call(
        body, grid=(A_HEADS, _HG_TILES),
        in_specs=[col(0), col(4), col(8), col(12), pl.BlockSpec((DEPTH, HEAD_DIM), lambda h, t: (0, h)),
                  pl.BlockSpec((1, HEAD_DIM), lambda h, t: (0, 0)), pl.BlockSpec((_HG_TILE, _HG_TILE), lambda h, t: (0, 0))],
        out_specs=[o_spec, o_spec, pl.BlockSpec((1, _HG_CHUNKS, HEAD_DIM, HEAD_DIM), lambda h, t: (h, t, 0, 0))],
        out_shape=[o_shape, o_shape, jax.ShapeDtypeStruct((A_HEADS, SEQ // A_CHUNK, HEAD_DIM, HEAD_DIM), _F32)],
        scratch_shapes=[pltpu.VMEM((HEAD_DIM, HEAD_DIM), _F32)], name=name,
        compiler_params=_cparams(dimension_semantics=("parallel", "arbitrary")),
    )(proj, proj, proj, proj, lb_logits, norm_w.reshape(1, HEAD_DIM), tri)


def _hgrn_bwd(proj, lb_logits, norm_w, raw, states, dmixed, layer, *, name):
    tri = _chunk_tri()
    triu = tri.T

    def body(q_ref, f_ref, i_ref, g_ref, lb_ref, nw_ref, tri_ref, triu_ref, raw_ref, do_ref, st_ref,
             dq_ref, df_ref, di_ref, dg_ref, dnw_ref, dlb_ref, dstate):
        @pl.when(pl.program_id(1) == 0)
        def _():
            dstate[...] = jnp.zeros_like(dstate)
            dlb_ref[...] = jnp.zeros_like(dlb_ref)

        @pl.when((pl.program_id(0) == 0) & (pl.program_id(1) == 0))
        def _():
            dnw_ref[...] = jnp.zeros_like(dnw_ref)

        lb = _layer_lb(lb_ref, layer)
        q = q_ref[...]
        sgq, qs, sg, f, k = _hgrn_gates(q, f_ref[...], lb)
        v = i_ref[...]
        b = jnp.dot(tri_ref[...], jnp.log(f), precision=_HI, preferred_element_type=_F32)
        eb = jnp.exp(b)
        g = g_ref[...]
        nw = nw_ref[...]
        o = raw_ref[...]
        dout = do_ref[...]
        sgg = _sigmoid(g)
        r = lax.rsqrt(jnp.mean(o * o, axis=-1, keepdims=True) + LN_EPS)
        dg_ref[...] = (dout * (o * r * nw) * (sgg * (1.0 + g * (1.0 - sgg)))).astype(dg_ref.dtype)
        don = dout * (g * sgg)
        dnw_ref[0:1, :] += jnp.sum(don * o * r, axis=0, keepdims=True)
        dy = don * nw
        do_raw = r * dy - o * (r * r * r) * jnp.mean(o * dy, axis=-1, keepdims=True)

        ridx = lax.broadcasted_iota(jnp.int32, (A_CHUNK, HEAD_DIM), 0)
        dqs_t, dk_t, db_t, dv_t = [None] * _HG_CHUNKS, [None] * _HG_CHUNKS, [None] * _HG_CHUNKS, [None] * _HG_CHUNKS
        dst = dstate[...]
        for c in reversed(range(_HG_CHUNKS)):
            sl = slice(c * A_CHUNK, (c + 1) * A_CHUNK)
            bc, qc, kc, vc, doc = b[sl], qs[sl], k[sl], v[sl], do_raw[sl]
            bl = bc[A_CHUNK - 1:A_CHUNK]
            ebc = eb[sl]
            ebl = jnp.exp(bl - bc)
            lam = jnp.exp(bl)
            qt = qc * ebc
            kt = kc * ebl
            stp = st_ref[0, c]
            dob = doc.astype(_MXU_DTYPE)
            dstb = dst.astype(_MXU_DTYPE)
            dqt = jnp.dot(dob, stp.astype(_MXU_DTYPE), preferred_element_type=_F32)
            dkt = jnp.dot(vc.astype(_MXU_DTYPE), dstb, preferred_element_type=_F32)
            dv = lax.dot_general(kt.astype(_MXU_DTYPE), dstb, _NT, preferred_element_type=_F32)
            dlam = jnp.sum(stp * dst, axis=0, keepdims=True)
            dst = dst * lam + lax.dot_general(dob, qt.astype(_MXU_DTYPE), _TN, preferred_element_type=_F32)
            dqs_rows = []
            dk_in = jnp.zeros((A_CHUNK, HEAD_DIM), _F32)
            for i in range(A_CHUNK):
                di = jnp.exp(jnp.where(ridx <= i, bc[i:i + 1] - bc, _NEG))
                qi = qc[i:i + 1]
                doi = doc[i:i + 1]
                w = kc * di
                a = jnp.sum(qi * w, axis=1, keepdims=True)
                dv = dv + a * doi
                da = jnp.sum(doi * vc, axis=1, keepdims=True)
                dqs_rows.append(jnp.sum(da * w, axis=0, keepdims=True))
                dk_in = dk_in + da * (qi * di)
            dqs_in = jnp.concatenate(dqs_rows, axis=0)
            dbl = jnp.sum(dkt * kt, axis=0, keepdims=True) + dlam * lam
            db = qc * dqs_in - kc * dk_in + dqt * qt - dkt * kt
            db_t[c] = db + jnp.where(ridx == A_CHUNK - 1, dbl, 0.0)
            dqs_t[c] = dqs_in + dqt * ebc
            dk_t[c] = dk_in + dkt * ebl
            dv_t[c] = dv
        dstate[...] = dst
        dqs = jnp.concatenate(dqs_t, axis=0)
        dk = jnp.concatenate(dk_t, axis=0)
        db = jnp.concatenate(db_t, axis=0)
        di_ref[...] = jnp.concatenate(dv_t, axis=0).astype(di_ref.dtype)
        dlogf = jnp.dot(triu_ref[...], db, precision=_HI, preferred_element_type=_F32)
        df = dlogf / f - dk
        df_ref[...] = (df * (1.0 - lb) * sg * (1.0 - sg)).astype(df_ref.dtype)
        dlb_ref[0, 0:1, :] += jnp.sum(df * (1.0 - sg), axis=0, keepdims=True)
        dq_ref[...] = (dqs * (sgq * (1.0 + q * (1.0 - sgq)))).astype(dq_ref.dtype)

    blk = (_HG_TILE, HEAD_DIM)
    last = _HG_TILES - 1

    def col(base):
        return pl.BlockSpec(blk, lambda h, t: (last - t, base + h))

    tri_spec = pl.BlockSpec((_HG_TILE, _HG_TILE), lambda h, t: (0, 0))
    acc_spec = pl.BlockSpec((1, 8, HEAD_DIM), lambda h, t: (h, 0, 0))
    acc_shape = jax.ShapeDtypeStruct((A_HEADS, 8, HEAD_DIM), _F32)
    dq, df, di, dg, dnw, dlb = pl.pallas_call(
        body, grid=(A_HEADS, _HG_TILES),
        in_specs=[col(0), col(4), col(8), col(12), pl.BlockSpec((DEPTH, HEAD_DIM), lambda h, t: (0, h)),
                  pl.BlockSpec((1, HEAD_DIM), lambda h, t: (0, 0)), tri_spec, tri_spec, col(0), col(0),
                  pl.BlockSpec((1, _HG_CHUNKS, HEAD_DIM, HEAD_DIM), lambda h, t: (h, last - t, 0, 0))],
        out_specs=[col(0), col(0), col(0), col(0), pl.BlockSpec((8, HEAD_DIM), lambda h, t: (0, 0)), acc_spec],
        out_shape=[jax.ShapeDtypeStruct((SEQ, A_HEADS * HEAD_DIM), _MXU_DTYPE)] * 4
        + [jax.ShapeDtypeStruct((8, HEAD_DIM), _F32), acc_shape],
        scratch_shapes=[pltpu.VMEM((HEAD_DIM, HEAD_DIM), _F32)], name=name,
        compiler_params=_cparams(dimension_semantics=("arbitrary", "arbitrary")),
    )(proj, proj, proj, proj, lb_logits, norm_w.reshape(1, HEAD_DIM), tri, triu, raw, dmixed, states)
    return dq, df, di, dg, dnw[0], dlb[:, 0, :].reshape(A_HEADS * HEAD_DIM)


N_CHIP = N_DEV // 2
_MESH_ID = pl.DeviceIdType.MESH


def _place():
    x, y, c = lax.axis_index("x"), lax.axis_index("y"), lax.axis_index("c")
    chips = [(1 - x, y), (x, 1 - y), (1 - x, 1 - y)]
    return x, y, c, 2 * x + y, chips


def _sibling_swap(arrays, *, name):
    n = len(arrays)

    def body(*refs):
        ins, outs = refs[:n], refs[n:2 * n]
        send_sems, recv_sems = refs[2 * n:]
        x, y, c, _, _ = _place()
        copies = [pltpu.make_async_remote_copy(
            src_ref=ins[a].at[:, 1 - c], dst_ref=outs[a], send_sem=send_sems.at[a], recv_sem=recv_sems.at[a],
            device_id=(x, y, 1 - c), device_id_type=_MESH_ID) for a in range(n)]
        for cp in copies:
            cp.start()
        for cp in copies:
            cp.wait()

    any_spec = pl.BlockSpec(memory_space=pl.ANY)
    return pl.pallas_call(
        body, in_specs=[any_spec] * n, out_specs=[any_spec] * n,
        out_shape=[jax.ShapeDtypeStruct((N_CHIP,) + a.shape[2:], a.dtype) for a in arrays],
        scratch_shapes=[pltpu.SemaphoreType.DMA((n,)), pltpu.SemaphoreType.DMA((n,))],
        name=name, compiler_params=pltpu.CompilerParams(has_side_effects=True),
    )(*arrays)


def _pair_add(mine, theirs, core, *, name):
    _, _, R, C = mine.shape
    tr = max(t for t in range(16, R + 1, 16) if R % t == 0 and t * C <= 512 * 1024)

    def body(core_ref, m_ref, t_ref, o_ref):
        del core_ref
        o_ref[...] = (m_ref[...].astype(_F32) + t_ref[...].astype(_F32)).astype(o_ref.dtype)

    grid_spec = pltpu.PrefetchScalarGridSpec(
        num_scalar_prefetch=1, grid=(N_CHIP, R // tr),
        in_specs=[pl.BlockSpec((None, None, tr, C), lambda q, i, core: (q, core[0], i, 0)),
                  pl.BlockSpec((None, tr, C), lambda q, i, core: (q, i, 0))],
        out_specs=pl.BlockSpec((None, tr, C), lambda q, i, core: (q, i, 0)))
    return pl.pallas_call(
        body, grid_spec=grid_spec, out_shape=jax.ShapeDtypeStruct((N_CHIP, R, C), mine.dtype), name=name,
        compiler_params=_cparams(dimension_semantics=("parallel", "parallel")),
    )(core.reshape(1), mine, theirs)


_HBM = pl.BlockSpec(memory_space=pltpu.HBM)
_SEM = pl.BlockSpec(memory_space=pltpu.SEMAPHORE)
_TOKEN = pl.BlockSpec(memory_space=pltpu.VMEM)
_DATAFLOW = pltpu.SideEffectType.DATAFLOW_SIDE_EFFECTING


def _hbm(a):
    return pltpu.HBM(a.shape, a.dtype)


def _token_shape():
    return jax.ShapeDtypeStruct((8, 128), _F32)


def _dev_slot(px, py, pc):
    return 4 * px + 2 * py + pc


def _gather_start(blocks, landings, *, name):
    n = len(blocks)

    def body(*refs):
        ins, lands = refs[:n], refs[n:2 * n]
        send_sems, d2d_sems, ici_sems = refs[2 * n:2 * n + 3]
        token = refs[-1]
        x, y, c, _, chips = _place()
        for a in range(n):
            dst = lands[a].at[_dev_slot(x, y, c)]
            pltpu.make_async_remote_copy(src_ref=ins[a], dst_ref=dst, send_sem=send_sems.at[4 * a], recv_sem=d2d_sems.at[a],
                                         device_id=(x, y, 1 - c), device_id_type=_MESH_ID).start()
            for j, chip in enumerate(chips):
                pltpu.make_async_remote_copy(src_ref=ins[a], dst_ref=dst, send_sem=send_sems.at[4 * a + 1 + j],
                                             recv_sem=ici_sems.at[3 * a + j], device_id=(*chip, c),
                                             device_id_type=_MESH_ID).start()
        token[...] = jnp.zeros_like(token)

    res = pl.pallas_call(
        body, name=name, in_specs=[_HBM] * (2 * n),
        out_shape=(pltpu.SemaphoreType.DMA((4 * n,)), pltpu.SemaphoreType.DMA((n,)), pltpu.SemaphoreType.DMA((3 * n,)),
                   *[_hbm(b) for b in blocks], *[_hbm(b) for b in landings], _token_shape()),
        out_specs=(_SEM, _SEM, _SEM, *[_HBM] * (2 * n), _TOKEN),
        input_output_aliases={i: 3 + i for i in range(2 * n)},
        compiler_params=pltpu.CompilerParams(has_side_effects=_DATAFLOW),
    )(*[pltpu.with_memory_space_constraint(b, pltpu.HBM) for b in blocks],
      *[pltpu.with_memory_space_constraint(b, pltpu.HBM) for b in landings])
    return res[0], res[1], res[2], list(res[3:3 + n]), list(res[3 + n:3 + 2 * n]), res[-1]


def _gather_forward(landings, ici_sems, first, after, *, name):
    n = len(landings)

    def body(*refs):
        lands = refs[:n]
        ici = refs[n]
        f_send, f_recv = refs[n + 2], refs[n + 3]
        token = refs[-1]
        x, y, c, _, chips = _place()
        for a in range(n):
            for j, chip in enumerate(chips):
                blk = lands[a].at[_dev_slot(*chip, c)]
                pltpu.make_async_remote_copy(src_ref=blk, dst_ref=blk, send_sem=f_send.at[3 * a + j],
                                             recv_sem=ici.at[3 * (first + a) + j], device_id=(*chip, c),
                                             device_id_type=_MESH_ID).wait_recv()
                pltpu.make_async_remote_copy(src_ref=blk, dst_ref=blk, send_sem=f_send.at[3 * a + j], recv_sem=f_recv.at[3 * a + j],
                                             device_id=(x, y, 1 - c), device_id_type=_MESH_ID).start()
        token[...] = jnp.zeros_like(token)

    res = pl.pallas_call(
        body, name=name, in_specs=[_HBM] * n + [_SEM, pl.BlockSpec(memory_space=pl.ANY)],
        out_shape=(pltpu.SemaphoreType.DMA((3 * n,)), pltpu.SemaphoreType.DMA((3 * n,)), *[_hbm(b) for b in landings], _token_shape()),
        out_specs=(_SEM, _SEM, *[_HBM] * n, _TOKEN),
        input_output_aliases={i: 2 + i for i in range(n)},
        compiler_params=pltpu.CompilerParams(has_side_effects=_DATAFLOW),
    )(*landings, ici_sems, after)
    return res[0], res[1], list(res[2:2 + n]), res[-1]


def _gather_wait(blocks, landings, send_sems, d2d_sems, first, f_send, f_recv, after, *, name):
    n = len(landings)

    def body(*refs):
        ins, lands = refs[:n], refs[n:2 * n]
        send, d2d, fs, fr = refs[2 * n:2 * n + 4]
        x, y, c, _, chips = _place()
        me = (x, y, c)
        for a in range(n):
            own = lands[a].at[_dev_slot(x, y, 1 - c)]
            g = first + a
            pltpu.make_async_remote_copy(src_ref=ins[a], dst_ref=own, send_sem=send.at[4 * g], recv_sem=d2d.at[g],
                                         device_id=me, device_id_type=_MESH_ID).wait_recv()
            for j, chip in enumerate(chips):
                blk = lands[a].at[_dev_slot(*chip, 1 - c)]
                pltpu.make_async_remote_copy(src_ref=blk, dst_ref=blk, send_sem=fs.at[3 * a + j], recv_sem=fr.at[3 * a + j],
                                             device_id=me, device_id_type=_MESH_ID).wait_recv()
            for k in range(4):
                pltpu.make_async_remote_copy(src_ref=ins[a], dst_ref=own, send_sem=send.at[4 * g + k], recv_sem=d2d.at[g],
                                             device_id=me, device_id_type=_MESH_ID).wait_send()
            for j in range(3):
                pltpu.make_async_remote_copy(src_ref=own, dst_ref=own, send_sem=fs.at[3 * a + j], recv_sem=fr.at[3 * a + j],
                                             device_id=me, device_id_type=_MESH_ID).wait_send()

    res = pl.pallas_call(
        body, name=name, in_specs=[_HBM] * (2 * n) + [_SEM] * 4 + [pl.BlockSpec(memory_space=pl.ANY)],
        out_shape=(*[_hbm(b) for b in blocks], *[_hbm(b) for b in landings]), out_specs=tuple([_HBM] * (2 * n)),
        input_output_aliases={i: i for i in range(2 * n)},
        compiler_params=pltpu.CompilerParams(has_side_effects=_DATAFLOW),
    )(*blocks, *landings, send_sems, d2d_sems, f_send, f_recv, after)
    return list(res[n:])


def _swap_start(mine, landings, *, name):
    n = len(mine)

    def body(*refs):
        ins, lands = refs[:n], refs[n:2 * n]
        send_sems, recv_sems = refs[2 * n:2 * n + 2]
        token = refs[-1]
        x, y, c, _, _ = _place()
        for a in range(n):
            pltpu.make_async_remote_copy(src_ref=ins[a].at[:, 1 - c], dst_ref=lands[a], send_sem=send_sems.at[a],
                                         recv_sem=recv_sems.at[a], device_id=(x, y, 1 - c), device_id_type=_MESH_ID).start()
        token[...] = jnp.zeros_like(token)

    res = pl.pallas_call(
        body, name=name, in_specs=[_HBM] * (2 * n),
        out_shape=(pltpu.SemaphoreType.DMA((n,)), pltpu.SemaphoreType.DMA((n,)),
                   *[_hbm(b) for b in mine], *[_hbm(b) for b in landings], _token_shape()),
        out_specs=(_SEM, _SEM, *[_HBM] * (2 * n), _TOKEN),
        input_output_aliases={i: 2 + i for i in range(2 * n)},
        compiler_params=pltpu.CompilerParams(has_side_effects=_DATAFLOW),
    )(*[pltpu.with_memory_space_constraint(b, pltpu.HBM) for b in mine],
      *[pltpu.with_memory_space_constraint(b, pltpu.HBM) for b in landings])
    return res[0], res[1], list(res[2:2 + n]), list(res[2 + n:2 + 2 * n]), res[-1]


def _swap_wait(mine, landings, send_sems, recv_sems, after, *, name):
    n = len(mine)

    def body(*refs):
        ins, lands = refs[:n], refs[n:2 * n]
        send, recv = refs[2 * n:2 * n + 2]
        x, y, c, _, _ = _place()
        for a in range(n):
            cp = pltpu.make_async_remote_copy(src_ref=ins[a].at[:, 1 - c], dst_ref=lands[a], send_sem=send.at[a],
                                              recv_sem=recv.at[a], device_id=(x, y, c), device_id_type=_MESH_ID)
            cp.wait_recv()
            cp.wait_send()

    res = pl.pallas_call(
        body, name=name, in_specs=[_HBM] * (2 * n) + [_SEM] * 2 + [pl.BlockSpec(memory_space=pl.ANY)],
        out_shape=(*[_hbm(b) for b in mine], *[_hbm(b) for b in landings]), out_specs=tuple([_HBM] * (2 * n)),
        input_output_aliases={i: i for i in range(2 * n)},
        compiler_params=pltpu.CompilerParams(has_side_effects=_DATAFLOW),
    )(*mine, *landings, send_sems, recv_sems, after)
    return list(res[:n]), list(res[n:])


def _chip_exchange_start(sums, landings, *, name):
    n = len(sums)

    def body(*refs):
        ins, lands = refs[:n], refs[n:2 * n]
        send_sems, recv_sems = refs[2 * n:2 * n + 2]
        token = refs[-1]
        _, _, c, p, chips = _place()
        for a in range(n):
            for j, (qx, qy) in enumerate(chips):
                pltpu.make_async_remote_copy(src_ref=ins[a].at[2 * qx + qy], dst_ref=lands[a].at[p], send_sem=send_sems.at[3 * a + j],
                                             recv_sem=recv_sems.at[3 * a + j], device_id=(qx, qy, c), device_id_type=_MESH_ID).start()
        token[...] = jnp.zeros_like(token)

    res = pl.pallas_call(
        body, name=name, in_specs=[_HBM] * (2 * n),
        out_shape=(pltpu.SemaphoreType.DMA((3 * n,)), pltpu.SemaphoreType.DMA((3 * n,)),
                   *[_hbm(b) for b in sums], *[_hbm(b) for b in landings], _token_shape()),
        out_specs=(_SEM, _SEM, *[_HBM] * (2 * n), _TOKEN),
        input_output_aliases={i: 2 + i for i in range(2 * n)},
        compiler_params=pltpu.CompilerParams(has_side_effects=_DATAFLOW),
    )(*[pltpu.with_memory_space_constraint(b, pltpu.HBM) for b in sums],
      *[pltpu.with_memory_space_constraint(b, pltpu.HBM) for b in landings])
    return res[0], res[1], list(res[2:2 + n]), list(res[2 + n:2 + 2 * n]), res[-1]


def _chip_exchange_wait(sums, landings, send_sems, recv_sems, after, *, name):
    n = len(sums)

    def body(*refs):
        ins, lands = refs[:n], refs[n:2 * n]
        send, recv = refs[2 * n:2 * n + 2]
        x, y, c, _, chips = _place()
        for a in range(n):
            for j, (qx, qy) in enumerate(chips):
                q = 2 * qx + qy
                cp = pltpu.make_async_remote_copy(src_ref=ins[a].at[q], dst_ref=lands[a].at[q], send_sem=send.at[3 * a + j],
                                                  recv_sem=recv.at[3 * a + j], device_id=(x, y, c), device_id_type=_MESH_ID)
                cp.wait_recv()
                cp.wait_send()

    res = pl.pallas_call(
        body, name=name, in_specs=[_HBM] * (2 * n) + [_SEM] * 2 + [pl.BlockSpec(memory_space=pl.ANY)] * len(after),
        out_shape=(*[_hbm(b) for b in sums], *[_hbm(b) for b in landings]), out_specs=tuple([_HBM] * (2 * n)),
        input_output_aliases={i: i for i in range(2 * n)},
        compiler_params=pltpu.CompilerParams(has_side_effects=_DATAFLOW),
    )(*sums, *landings, send_sems, recv_sems, *after)
    return list(res[:n]), list(res[n:])


_C1 = 1.0 - ADAM_B1 ** ADAM_STEP
_C2 = 1.0 - ADAM_B2 ** ADAM_STEP


def _adamw_math(g, w, m, v):
    m = ADAM_B1 * m + (1.0 - ADAM_B1) * g
    v = ADAM_B2 * v + (1.0 - ADAM_B2) * (g * g)
    delta = -ADAM_LR * ((m / _C1) / (jnp.sqrt(v / _C2) + ADAM_EPS) + ADAM_WD * w)
    return delta, m, v


def _adamw_reduce(landed, sums, chip, w, m, v, layer, prev, *, name):
    _, R, C = w.shape
    tr = max(t for t in range(16, R + 1, 16) if R % t == 0 and t * C <= 256 * 1024)

    def body(chip_ref, p_ref, own_ref, w_ref, m_ref, v_ref, *rest):
        g_ref, d_ref, nm_ref, nv_ref = rest[-4:]
        own = own_ref[...].astype(_F32)
        g = jnp.where(chip_ref[0] == 0, own, p_ref[0].astype(_F32))
        for q in range(1, N_CHIP):
            g = g + jnp.where(chip_ref[0] == q, own, p_ref[q].astype(_F32))
        d, nm, nv = _adamw_math(g, w_ref[...], m_ref[...], v_ref[...])
        g_ref[...] = g
        d_ref[...] = d
        nm_ref[...] = nm
        nv_ref[...] = nv

    blk = pl.BlockSpec((None, tr, C), lambda i, chip: (layer, i, 0))
    shape = jax.ShapeDtypeStruct((DEPTH, R, C), _F32)
    kept = [] if prev is None else list(prev)
    grid_spec = pltpu.PrefetchScalarGridSpec(
        num_scalar_prefetch=1, grid=(R // tr,),
        in_specs=[pl.BlockSpec((N_CHIP, tr, C), lambda i, chip: (0, i, 0)),
                  pl.BlockSpec((None, tr, C), lambda i, chip: (chip[0], i, 0)), blk, blk, blk]
        + [pl.BlockSpec(memory_space=pl.ANY)] * len(kept),
        out_specs=[blk] * 4)
    return pl.pallas_call(
        body, grid_spec=grid_spec, out_shape=[shape] * 4, name=name,
        input_output_aliases={6 + k: k for k in range(len(kept))},
        compiler_params=_cparams(dimension_semantics=("parallel",)),
    )(chip.reshape(1), landed, sums, w, m, v, *kept)


_PACK_LANES = 128
_LAYER_ROWS = 248
_LB_ROWS = (A_HEADS * HEAD_DIM) // _PACK_LANES


def _small_reduce(parts, lb_logits, *, name):
    rows = DEPTH * _LAYER_ROWS

    def body(p_ref, lg_ref, o_ref):
        g = p_ref[0]
        for s in range(1, N_DEV):
            g = g + p_ref[s]
        o_ref[...] = g
        lg = lg_ref[...]
        e = jnp.exp(lg - jnp.max(lg, axis=0, keepdims=True))
        p = e / jnp.sum(e, axis=0, keepdims=True)
        d1 = g[_LAYER_ROWS:_LAYER_ROWS + _LB_ROWS, :] * p[0] * p[1]
        o_ref[0:_LB_ROWS, :] = -d1
        o_ref[_LAYER_ROWS:_LAYER_ROWS + _LB_ROWS, :] = d1

    return pl.pallas_call(
        body, out_shape=jax.ShapeDtypeStruct((rows, _PACK_LANES), _F32), name=name,
        compiler_params=_cparams(),
    )(parts, lb_logits.reshape(DEPTH, _LB_ROWS, _PACK_LANES))


def _adamw_small(g, w, m, v, *, name):
    def body(g_ref, w_ref, m_ref, v_ref, d_ref, nm_ref, nv_ref):
        d, nm, nv = _adamw_math(g_ref[...], w_ref[...], m_ref[...], v_ref[...])
        d_ref[...] = d
        nm_ref[...] = nm
        nv_ref[...] = nv

    shape = jax.ShapeDtypeStruct(g.shape, _F32)
    return pl.pallas_call(body, out_shape=[shape] * 3, name=name, compiler_params=_cparams())(g, w, m, v)


def _pack(vectors, rows):
    flat = jnp.concatenate([v.reshape(-1).astype(_F32) for v in vectors])
    return jnp.pad(flat, (0, rows * _PACK_LANES - flat.shape[0])).reshape(rows, _PACK_LANES)


def _unpack(packed, shapes):
    flat = packed.reshape(-1)
    out, at = [], 0
    for s in shapes:
        size = int(np.prod(s))
        out.append(flat[at:at + size].reshape(s))
        at += size
    return out


_BIG = ("w_in", "w_gate", "w_up", "w_out", "w_down")
_COLUMN_SHARDED = ("w_in", "w_gate", "w_up")


def _full_weight(name, g):
    if name == "conv_w":
        return g.transpose(1, 0, 2).reshape(g.shape[1], N_DEV * SHARD_COLS)
    if name in _BIG:
        return g.reshape(N_DEV * g.shape[1], g.shape[2])
    return g


class _WeightGather:
    def __init__(self, names, first, blocks, lands, sems, tag):
        self.names, self.first, self.blocks, self.lands, self.sems, self.tag = names, first, blocks, lands, sems, tag
        self.forwarded = None

    def forward(self, after):
        f_send, f_recv, self.lands, token = _gather_forward(self.lands, self.sems[2], self.first, after,
                                                            name=f"gather_forward_{self.tag}")
        self.forwarded = (f_send, f_recv)
        return token

    def wait(self, after):
        if self.forwarded is None:
            self.forward(after)
        got = _gather_wait(self.blocks, self.lands, self.sems[0], self.sems[1], self.first, *self.forwarded, after,
                           name=f"gather_wait_{self.tag}")
        return {n: _full_weight(n, g) for n, g in zip(self.names, got)}


def _start_gathers(groups, me, name):
    blocks = [b for _, _, bs in groups for b in bs]
    landings = [lax.dynamic_update_index_in_dim(lax.empty((N_DEV,) + b.shape, b.dtype), b[None], me, 0) for b in blocks]
    send, d2d, ici, blocks, landings, token = _gather_start(blocks, landings, name=name)
    out, first = [], 0
    for tag, names, bs in groups:
        k = len(bs)
        out.append(_WeightGather(names, first, blocks[first:first + k], landings[first:first + k], (send, d2d, ici), tag))
        first += k
    return out, token


class _LayerWeights:
    def __init__(self, ready, pending=(), forwards=(), tokens=()):
        self.ready, self.pending, self.forwards, self._tokens = dict(ready), list(pending), list(forwards), list(tokens)

    def at(self, point, after):
        for when, gather in self.forwards:
            if when == point:
                self._tokens.append(gather.forward(after))

    def tokens(self):
        out, self._tokens = self._tokens, []
        return out

    def get(self, name, after):
        if name not in self.ready:
            group, = [g for g in self.pending if name in g.names]
            self.ready.update(group.wait(after))
        return self.ready[name]


def _layer_fwd(x, xb, ws, lb_logits, a_norm_w, c_sink, ln1_g, ln1_b, conv_b, ln2_g, ln2_b, tabs, l, target=None):
    proj = _mm(xb, ws.get("w_in", xb), tb=True, **_TILE_WIDE_N, after=ws.tokens(), name=f"proj_{l}")
    o_a, raw, states = _hgrn_fwd(proj, lb_logits, a_norm_w, l, name=f"hgrn_fwd_{l}")
    ws.at("hgrn", o_a)
    o_b, lse_b = _band_fwd(proj, tabs, name=f"dilated_fwd_{l}", **_DILATED)
    o_c, lse_c = _band_fwd(proj, tabs, sink=c_sink, name=f"swa_fwd_{l}", **_SWA)
    ws.at("swa", o_c)
    mixed = _concat_cols([o_a, o_b, o_c], name=f"mixed_{l}")
    y = _mm(mixed, ws.get("w_out", mixed), **_TILE_MIX, after=ws.tokens(), name=f"mix_out_{l}")
    z1, x1, x1b = _ln_fwd(x, y, ln1_g, ln1_b, name=f"ln1_fwd_{l}")
    g = _mm(x1b, ws.get("w_gate", x1b), tb=True, **_TILE_WIDE_N, out_dtype=_ACT_DTYPE, name=f"ffn_gate_{l}")
    u = _mm(x1b, ws.get("w_up", x1b), tb=True, **_TILE_WIDE_N, out_dtype=_ACT_DTYPE, name=f"ffn_up_{l}")
    ws.at("up", u)
    hb = _conv_gate_fwd(g, u, ws.get("conv_w", u), conv_b, name=f"conv_gate_fwd_{l}")
    y2 = _mm(hb, ws.get("w_down", hb), **_TILE_WIDE_K, after=ws.tokens(), name=f"ffn_down_{l}")
    ws.at("down", y2)
    res = dict(xb=xb, proj=proj, raw=raw, states=states, o_b=o_b, lse_b=lse_b, o_c=o_c, lse_c=lse_c,
               mixed=mixed, z1=z1, x1b=x1b, g=g, u=u, hb=hb)
    if target is not None:
        loss_part, *res["ln2_bwd"] = _ln_loss_bwd(x1, y2, ln2_g, ln2_b, target, name=f"ln2_loss_{l}")
        return loss_part, None, res
    res["z2"], x2, x2b = _ln_fwd(x1, y2, ln2_g, ln2_b, name=f"ln2_fwd_{l}")
    return x2, x2b, res


class _GradExchange:
    def __init__(self, core, chip):
        self.core, self.chip, self.groups, self.swapping, self._tokens = core, chip, [], [], []

    def launch(self, names, slabs, l, tag, behind):
        mine = [s.reshape((N_CHIP, 2) + s.shape[1:]) for s in slabs]
        if behind:
            landings = [lax.empty((N_CHIP,) + m.shape[2:], m.dtype) for m in mine]
            send, recv, mine, landings, token = _swap_start(mine, landings, name=f"swap_start_{tag}")
            self.swapping.append((names, l, tag, send, recv, mine, landings))
            self._tokens.append(token)
        else:
            self._exchange(names, l, tag, mine, _sibling_swap(mine, name=f"swap_grads_{tag}"))

    def advance(self, after):
        for names, l, tag, send, recv, mine, landings in self.swapping:
            mine, theirs = _swap_wait(mine, landings, send, recv, after, name=f"swap_wait_{tag}")
            self._exchange(names, l, tag, mine, theirs)
        self.swapping = []

    def _exchange(self, names, l, tag, mine, theirs):
        sums = [_pair_add(a, b, self.core, name=f"pair_add_{n}_{l}") for n, a, b in zip(names, mine, theirs)]
        landings = [lax.empty(s.shape, s.dtype) for s in sums]
        send, recv, sums, landings, token = _chip_exchange_start(sums, landings, name=f"exchange_start_{tag}")
        self.groups.append((names, l, tag, send, recv, sums, landings))
        self._tokens.append(token)

    def tokens(self):
        out, self._tokens = self._tokens, []
        return out

    def finish(self, weights, mom1, mom2, after):
        out = {}
        after = list(after) + self.tokens()
        for names, l, tag, send, recv, sums, landings in self.groups:
            sums, landings = _chip_exchange_wait(sums, landings, send, recv, after, name=f"exchange_wait_{tag}")
            for n, s, landed in zip(names, sums, landings):
                out[n] = _adamw_reduce(landed, s, self.chip, weights[n], mom1[n], mom2[n], l, out.get(n), name=f"adamw_{n}_{l}")
            after = [out[n][0] for n in names]
        return out


def _layer_bwd(dx2, res, w, lb_logits, a_norm_w, c_sink, ln1_g, conv_b, ln2_g, tabs, exchange, l):
    if "ln2_bwd" in res:
        dz2, dz2b, d_ln2_g, d_ln2_b = res["ln2_bwd"]
    else:
        dz2, dz2b, d_ln2_g, d_ln2_b = _ln_bwd(res["z2"], dx2, None, ln2_g, name=f"ln2_bwd_{l}")
    exchange.advance(dz2b)
    dh = _mm(dz2b, w["w_down"], tb=True, **_TILE_WIDE_N, out_dtype=_ACT_DTYPE, after=exchange.tokens(),
             name=f"ffn_down_dx_{l}")
    d_w_down = _mm(res["hb"], dz2b, ta=True, **_TILE_WIDE_M, out_dtype=_GRAD_DTYPE, name=f"ffn_down_dw_{l}")
    dg, du, d_conv_w, d_conv_b = _conv_gate_bwd(dh, res["g"], res["u"], w["conv_w"], conv_b, name=f"conv_gate_bwd_{l}")
    t = _mm(dg, w["w_gate"], **_TILE_WIDE_K, name=f"ffn_gate_dx_{l}")
    dx1 = _mm(du, w["w_up"], **_TILE_WIDE_K, add=t, name=f"ffn_up_dx_{l}")
    d_w_gate = _mm(dg, res["x1b"], ta=True, **_TILE_WIDE_M, out_dtype=_GRAD_DTYPE, name=f"ffn_gate_dw_{l}")
    d_w_up = _mm(du, res["x1b"], ta=True, **_TILE_WIDE_M, out_dtype=_GRAD_DTYPE, name=f"ffn_up_dw_{l}")
    dz1, dz1b, d_ln1_g, d_ln1_b = _ln_bwd(res["z1"], dx1, dz2, ln1_g, name=f"ln1_bwd_{l}")
    d_w_out = _mm(res["mixed"], dz1b, ta=True, **_TILE_MIX, out_dtype=_GRAD_DTYPE, name=f"mix_out_dw_{l}")
    exchange.launch(("w_down", "w_gate", "w_up", "w_out"),
                    [d.reshape(N_DEV, d.shape[0] // N_DEV, D_MODEL) for d in (d_w_down, d_w_gate, d_w_up, d_w_out)],
                    l, f"ffn_{l}", True)
    dmixed = _mm(dz1b, w["w_out"], tb=True, **_TILE_MIX, after=exchange.tokens(), name=f"mix_out_dx_{l}")
    dq_a, df_a, di_a, dg_a, d_norm_w, d_lb = _hgrn_bwd(res["proj"], lb_logits, a_norm_w, res["raw"], res["states"],
                                                      dmixed, l, name=f"hgrn_bwd_{l}")
    exchange.advance(dq_a)
    dq_b, dk_b, dv_b = _band_bwd(res["proj"], tabs, dmixed, res["o_b"], res["lse_b"], do0=A_HEADS, after=exchange.tokens(),
                                 name=f"dilated_bwd_{l}", **_DILATED)
    dq_c, dk_c, dv_c, d_sink = _band_bwd(res["proj"], tabs, dmixed, res["o_c"], res["lse_c"], do0=A_HEADS + B_HEADS,
                                         sink=c_sink, name=f"swa_bwd_{l}", **_SWA)
    dproj = _concat_cols([dq_a, df_a, di_a, dg_a, dq_b, dk_b, dv_b, dq_c, dk_c, dv_c], name=f"dproj_{l}")
    d_w_in = _mm(dproj, res["xb"], ta=True, **_TILE_WIDE_M, out_dtype=_GRAD_DTYPE, name=f"proj_dw_{l}")
    exchange.launch(("w_in",), [d_w_in.reshape(N_DEV, SHARD_COLS, D_MODEL)], l, f"mix_{l}", l > 0)
    dx = _mm(dproj, w["w_in"], **_TILE_WIDE_K, add=dz1, add_scale=ALPHA, after=exchange.tokens(), name=f"proj_dx_{l}")
    small = [d_lb, d_norm_w, jnp.pad(d_sink, (0, _PACK_LANES - C_HEADS)), d_ln1_g, d_ln1_b, d_ln2_g, d_ln2_b, d_conv_b,
             d_conv_w]
    return dx, small


def kernel(x, w_in, lb_logits, a_norm_w, c_sinks, w_out, ln1_g, ln1_b, w_gate, w_up, conv_w, conv_b, w_down, ln2_g, ln2_b, loss_target, m_w_in, m_lb_logits, m_a_norm_w, m_c_sinks, m_w_out, m_ln1_g, m_ln1_b, m_w_gate, m_w_up, m_conv_w, m_conv_b, m_w_down, m_ln2_g, m_ln2_b, v_w_in, v_lb_logits, v_a_norm_w, v_c_sinks, v_w_out, v_ln1_g, v_ln1_b, v_w_gate, v_w_up, v_conv_w, v_conv_b, v_w_down, v_ln2_g, v_ln2_b):
    weights = dict(w_in=w_in, lb_logits=lb_logits, a_norm_w=a_norm_w, c_sinks=c_sinks, w_out=w_out, ln1_g=ln1_g, ln1_b=ln1_b,
                   w_gate=w_gate, w_up=w_up, conv_w=conv_w, conv_b=conv_b, w_down=w_down, ln2_g=ln2_g, ln2_b=ln2_b)
    mom1 = dict(w_in=m_w_in, lb_logits=m_lb_logits, a_norm_w=m_a_norm_w, c_sinks=m_c_sinks, w_out=m_w_out, ln1_g=m_ln1_g,
                ln1_b=m_ln1_b, w_gate=m_w_gate, w_up=m_w_up, conv_w=m_conv_w, conv_b=m_conv_b, w_down=m_w_down, ln2_g=m_ln2_g,
                ln2_b=m_ln2_b)
    mom2 = dict(w_in=v_w_in, lb_logits=v_lb_logits, a_norm_w=v_a_norm_w, c_sinks=v_c_sinks, w_out=v_w_out, ln1_g=v_ln1_g,
                ln1_b=v_ln1_b, w_gate=v_w_gate, w_up=v_w_up, conv_w=v_conv_w, conv_b=v_conv_b, w_down=v_w_down, ln2_g=v_ln2_g,
                ln2_b=v_ln2_b)
    core = lax.axis_index("c").astype(jnp.int32)
    me = 4 * lax.axis_index("x") + 2 * lax.axis_index("y") + core
    tabs = _rope_tables()

    chip = (2 * lax.axis_index("x") + lax.axis_index("y")).astype(jnp.int32)

    def as_slabs(d):
        return {n: jnp.swapaxes(d[n], 1, 2) if n in _COLUMN_SHARDED else d[n] for n in _BIG}

    w_views = as_slabs(weights)

    def block(n, l, after=()):
        return conv_w[l] if n == "conv_w" else _cast_layer(w_views[n], l, after=after, name=f"cast_{n}_{l}")

    (in0,), started_first = _start_gathers([("w_in_0", ("w_in",), [block("w_in", 0)])], me, "gather_start_first")
    order = [(("w_out",), 0), (("w_gate", "w_up", "conv_w"), 0), (("w_down",), 0),
             (("w_in",), 1), (("w_out",), 1), (("w_gate", "w_up", "conv_w"), 1), (("w_down",), 1)]
    gathers, started = _start_gathers([(f"{names[0]}_{l}", names, [block(n, l, [started_first]) for n in names])
                                       for names, l in order], me, "gather_start_rest")
    out0, ffn0, down0, in1, out1, ffn1, down1 = gathers
    layer_ws = [_LayerWeights(in0.wait(started), [out0, ffn0, down0],
                              [("hgrn", out0), ("swa", ffn0), ("up", down0), ("down", in1)]),
                _LayerWeights({}, [in1, out1, ffn1, down1], [("hgrn", out1), ("swa", ffn1), ("up", down1)])]

    xs = x[0]
    xb = xs.astype(_MXU_DTYPE)
    saved = []
    for l in range(DEPTH):
        xs, xb, res = _layer_fwd(xs, xb, layer_ws[l], lb_logits, a_norm_w[l], c_sinks[l], ln1_g[l], ln1_b[l], conv_b[l],
                                 ln2_g[l], ln2_b[l], tabs, l, loss_target[0] if l == DEPTH - 1 else None)
        saved.append(res)
    loss = lax.psum(xs, ("x", "y", "c"))
    dx = None

    exchange = _GradExchange(core, chip)
    small_parts = [None] * DEPTH
    for l in reversed(range(DEPTH)):
        dx, small = _layer_bwd(dx, saved[l], layer_ws[l].ready, lb_logits, a_norm_w[l], c_sinks[l], ln1_g[l], conv_b[l],
                               ln2_g[l], tabs, exchange, l)
        small_parts[l] = _pack(small, _LAYER_ROWS)
    (small_gather,), small_started = _start_gathers(
        [("small_grads", ("small",), [jnp.concatenate(small_parts, axis=0)])], me, "gather_start_small")
    updated = exchange.finish(w_views, as_slabs(mom1), as_slabs(mom2), [dx, small_started])
    gathered = small_gather.wait(updated["w_in"][0])["small"]
    updated = {n: tuple(jnp.swapaxes(t, 1, 2) for t in u) if n in _COLUMN_SHARDED else u for n, u in updated.items()}
    g_small = _small_reduce(gathered, lb_logits, name="small_grads")

    per_layer = [(A_HEADS * HEAD_DIM,), (HEAD_DIM,), (_PACK_LANES,), (D_MODEL,), (D_MODEL,), (D_MODEL,), (D_MODEL,), (D_FF,),
                 (3, D_FF)]
    names = ("lb_logits", "a_norm_w", "c_sinks", "ln1_g", "ln1_b", "ln2_g", "ln2_b", "conv_b", "conv_w")
    grads = {n: [] for n in names}
    for l in range(DEPTH):
        for n, t in zip(names, _unpack(g_small[l * _LAYER_ROWS:(l + 1) * _LAYER_ROWS], per_layer)):
            grads[n].append(t)
    grads = {n: jnp.stack(t) for n, t in grads.items()}
    grads["c_sinks"] = grads["c_sinks"][:, :C_HEADS]
    grads["conv_w"] = lax.dynamic_slice_in_dim(grads["conv_w"], me * SHARD_COLS, SHARD_COLS, axis=2)
    shapes = [grads[n].shape for n in names]
    rows = -(-sum(int(np.prod(s)) for s in shapes) // (8 * _PACK_LANES)) * 8
    d_s, m_s, v_s = _adamw_small(_pack([grads[n] for n in names], rows), _pack([weights[n] for n in names], rows),
                                 _pack([mom1[n] for n in names], rows), _pack([mom2[n] for n in names], rows),
                                 name="adamw_small")
    delta = dict(zip(names, _unpack(d_s, shapes)))
    new_m = dict(zip(names, _unpack(m_s, shapes)))
    new_v = dict(zip(names, _unpack(v_s, shapes)))
    for n in _BIG:
        grads[n], delta[n], new_m[n], new_v[n] = updated[n]

    order = ("w_in", "lb_logits", "a_norm_w", "c_sinks", "w_out", "ln1_g", "ln1_b", "w_gate", "w_up", "conv_w", "conv_b",
             "w_down", "ln2_g", "ln2_b")
    return (loss, dx[None], *[grads[n] for n in order], *[delta[n] for n in order], *[new_m[n] for n in order],
            *[new_v[n] for n in order])
```

```python
import functools

import jax
import jax.numpy as jnp
import numpy as np
from jax import lax
from jax.experimental import pallas as pl
from jax.experimental.pallas import tpu as pltpu

D_MODEL = 2048
SEQ = 2048
DEPTH = 2
HEAD_DIM = 128
A_HEADS = 4
B_HEADS = 6
C_HEADS = 6
C_KV_HEADS = 2
A_CHUNK = 16
DILATIONS = (1, 4, 16)
BLOCK = 128
ROPE_THETA = 500000.0
ROPE_DIM = 32
D_FF = 5632
IN_WIDTH = 5632
LN_EPS = 1e-5
ALPHA = (2 * DEPTH) ** 0.25
N_DEV = 8
SHARD_COLS = IN_WIDTH // N_DEV

ADAM_LR = 0.001
ADAM_B1 = 0.9
ADAM_B2 = 0.999
ADAM_EPS = 1e-08
ADAM_WD = 0.01
ADAM_STEP = 10

A_COLS = 16
QKV_COLS = 28
QB0, KB0, VB0, QC0, KC0, VC0 = 0, 6, 12, 18, 24, 26

_MXU_DTYPE = jnp.bfloat16
_GRAD_DTYPE = jnp.bfloat16
_ACT_DTYPE = jnp.bfloat16
_NEG = -1e30
_VMEM_LIMIT = 56 * 2 ** 20

_F32 = jnp.float32


def _sigmoid(x):
    return 0.5 * jnp.tanh(0.5 * x) + 0.5


def _cparams(**kw):
    return pltpu.CompilerParams(vmem_limit_bytes=_VMEM_LIMIT, **kw)


_TILE_MIX = dict(tm=1024, tn=2048)
_TILE_WIDE_K = dict(tm=1024, tn=512)
_TILE_WIDE_N = dict(tm=1024, tn=1408)
_TILE_WIDE_M = dict(tm=1408, tn=1024)


def _mm(a, b, *, ta=False, tb=False, tm, tn, out_dtype=_F32, add=None, add_scale=1.0, after=(), name):
    K = a.shape[0] if ta else a.shape[1]
    M = a.shape[1] if ta else a.shape[0]
    N = b.shape[0] if tb else b.shape[1]
    assert (b.shape[1] if tb else b.shape[0]) == K and M % tm == 0 and N % tn == 0
    dn = (((0 if ta else 1,), (1 if tb else 0,)), ((), ()))

    def body(*refs):
        a_ref, b_ref = refs[:2]
        o_ref = refs[-1]
        r = lax.dot_general(a_ref[...], b_ref[...], dn, preferred_element_type=_F32)
        if add is not None:
            r = r + add_scale * refs[2][...]
        o_ref[...] = r.astype(o_ref.dtype)

    a_spec = pl.BlockSpec((K, tm), lambda i, j: (0, i)) if ta else pl.BlockSpec((tm, K), lambda i, j: (i, 0))
    b_spec = pl.BlockSpec((tn, K), lambda i, j: (j, 0)) if tb else pl.BlockSpec((K, tn), lambda i, j: (0, j))
    o_spec = pl.BlockSpec((tm, tn), lambda i, j: (i, j))
    in_specs = [a_spec, b_spec] + ([o_spec] if add is not None else []) + [pl.BlockSpec(memory_space=pl.ANY)] * len(after)
    args = (a, b) + ((add,) if add is not None else ()) + tuple(after)
    return pl.pallas_call(
        body, grid=(M // tm, N // tn), in_specs=in_specs, out_specs=o_spec,
        out_shape=jax.ShapeDtypeStruct((M, N), out_dtype), name=name,
        compiler_params=_cparams(dimension_semantics=("parallel", "parallel")),
    )(*args)


def _cast_layer(w, layer, *, after=(), name):
    _, R, C = w.shape
    tr = max(t for t in range(16, R + 1, 16) if R % t == 0 and t * C <= 512 * 1024)

    def body(w_ref, *rest):
        o_ref = rest[-1]
        o_ref[...] = w_ref[...].astype(o_ref.dtype)

    return pl.pallas_call(
        body, grid=(R // tr,),
        in_specs=[pl.BlockSpec((None, tr, C), lambda i: (layer, i, 0))] + [pl.BlockSpec(memory_space=pl.ANY)] * len(after),
        out_specs=pl.BlockSpec((tr, C), lambda i: (i, 0)), out_shape=jax.ShapeDtypeStruct((R, C), _MXU_DTYPE), name=name,
        compiler_params=_cparams(dimension_semantics=("parallel",)),
    )(w, *after)


def _concat_cols(pieces, *, name):
    tm = 512
    widths = [p.shape[1] for p in pieces]
    offs = np.cumsum([0] + widths)

    def body(*refs):
        o_ref = refs[-1]
        for p_ref, off, w in zip(refs[:-1], offs, widths):
            o_ref[:, off:off + w] = p_ref[...].astype(o_ref.dtype)

    return pl.pallas_call(
        body, grid=(SEQ // tm,), in_specs=[pl.BlockSpec((tm, w), lambda i: (i, 0)) for w in widths],
        out_specs=pl.BlockSpec((tm, int(offs[-1])), lambda i: (i, 0)),
        out_shape=jax.ShapeDtypeStruct((SEQ, int(offs[-1])), _MXU_DTYPE), name=name,
        compiler_params=_cparams(dimension_semantics=("parallel",)),
    )(*pieces)


def _ln_fwd(x, y, g, b, *, name):
    tm = 256

    def body(x_ref, y_ref, g_ref, b_ref, z_ref, o_ref, ob_ref):
        z = ALPHA * x_ref[...] + y_ref[...]
        mu = jnp.mean(z, axis=-1, keepdims=True)
        zc = z - mu
        var = jnp.mean(zc * zc, axis=-1, keepdims=True)
        o = zc * lax.rsqrt(var + LN_EPS) * g_ref[...] + b_ref[...]
        z_ref[...] = z
        o_ref[...] = o
        ob_ref[...] = o.astype(ob_ref.dtype)

    row = pl.BlockSpec((tm, D_MODEL), lambda i: (i, 0))
    vec = pl.BlockSpec((1, D_MODEL), lambda i: (0, 0))
    return pl.pallas_call(
        body, grid=(SEQ // tm,), in_specs=[row, row, vec, vec], out_specs=[row, row, row],
        out_shape=[jax.ShapeDtypeStruct((SEQ, D_MODEL), _F32), jax.ShapeDtypeStruct((SEQ, D_MODEL), _F32),
                   jax.ShapeDtypeStruct((SEQ, D_MODEL), _MXU_DTYPE)],
        name=name, compiler_params=_cparams(dimension_semantics=("parallel",)),
    )(x, y, g.reshape(1, D_MODEL), b.reshape(1, D_MODEL))


def _ln_bwd(z, d_a, d_res, g, *, name):
    tm = 256

    def body(*refs):
        if d_res is None:
            z_ref, da_ref, g_ref, dz_ref, dzb_ref, dg_ref, db_ref = refs
        else:
            z_ref, da_ref, dr_ref, g_ref, dz_ref, dzb_ref, dg_ref, db_ref = refs

        @pl.when(pl.program_id(0) == 0)
        def _():
            dg_ref[...] = jnp.zeros_like(dg_ref)
            db_ref[...] = jnp.zeros_like(db_ref)

        dout = da_ref[...]
        if d_res is not None:
            dout = dout + ALPHA * dr_ref[...]
        z = z_ref[...]
        mu = jnp.mean(z, axis=-1, keepdims=True)
        zc = z - mu
        var = jnp.mean(zc * zc, axis=-1, keepdims=True)
        rstd = lax.rsqrt(var + LN_EPS)
        xh = zc * rstd
        dxh = dout * g_ref[...]
        m1 = jnp.mean(dxh, axis=-1, keepdims=True)
        m2 = jnp.mean(dxh * xh, axis=-1, keepdims=True)
        dz = rstd * (dxh - m1 - xh * m2)
        dz_ref[...] = dz
        dzb_ref[...] = dz.astype(dzb_ref.dtype)
        dg_ref[0:1, :] += jnp.sum(dout * xh, axis=0, keepdims=True)
        db_ref[0:1, :] += jnp.sum(dout, axis=0, keepdims=True)

    row = pl.BlockSpec((tm, D_MODEL), lambda i: (i, 0))
    vec = pl.BlockSpec((1, D_MODEL), lambda i: (0, 0))
    acc = pl.BlockSpec((8, D_MODEL), lambda i: (0, 0))
    ins = [z, d_a] + ([d_res] if d_res is not None else []) + [g.reshape(1, D_MODEL)]
    in_specs = [row, row] + ([row] if d_res is not None else []) + [vec]
    dz, dzb, dg, db = pl.pallas_call(
        body, grid=(SEQ // tm,), in_specs=in_specs, out_specs=[row, row, acc, acc],
        out_shape=[jax.ShapeDtypeStruct((SEQ, D_MODEL), _F32), jax.ShapeDtypeStruct((SEQ, D_MODEL), _MXU_DTYPE),
                   jax.ShapeDtypeStruct((8, D_MODEL), _F32), jax.ShapeDtypeStruct((8, D_MODEL), _F32)],
        name=name, compiler_params=_cparams(dimension_semantics=("arbitrary",)),
    )(*ins)
    return dz, dzb, dg[0], db[0]


def _ln_loss_bwd(x, y, g, b, target, *, name):
    tm = 256

    def body(x_ref, y_ref, g_ref, b_ref, t_ref, dz_ref, dzb_ref, dg_ref, db_ref, l_ref):
        @pl.when(pl.program_id(0) == 0)
        def _():
            dg_ref[...] = jnp.zeros_like(dg_ref)
            db_ref[...] = jnp.zeros_like(db_ref)
            l_ref[...] = jnp.zeros_like(l_ref)

        z = ALPHA * x_ref[...] + y_ref[...]
        mu = jnp.mean(z, axis=-1, keepdims=True)
        zc = z - mu
        var = jnp.mean(zc * zc, axis=-1, keepdims=True)
        rstd = lax.rsqrt(var + LN_EPS)
        xh = zc * rstd
        e = xh * g_ref[...] + b_ref[...] - t_ref[...]
        l_ref[...] += (0.5 / D_MODEL) * jnp.sum(e * e)
        dout = e * (1.0 / D_MODEL)
        dxh = dout * g_ref[...]
        m1 = jnp.mean(dxh, axis=-1, keepdims=True)
        m2 = jnp.mean(dxh * xh, axis=-1, keepdims=True)
        dz = rstd * (dxh - m1 - xh * m2)
        dz_ref[...] = dz
        dzb_ref[...] = dz.astype(dzb_ref.dtype)
        dg_ref[0:1, :] += jnp.sum(dout * xh, axis=0, keepdims=True)
        db_ref[0:1, :] += jnp.sum(dout, axis=0, keepdims=True)

    row = pl.BlockSpec((tm, D_MODEL), lambda i: (i, 0))
    vec = pl.BlockSpec((1, D_MODEL), lambda i: (0, 0))
    acc = pl.BlockSpec((8, D_MODEL), lambda i: (0, 0))
    dz, dzb, dg, db, part = pl.pallas_call(
        body, grid=(SEQ // tm,), in_specs=[row, row, vec, vec, row],
        out_specs=[row, row, acc, acc, pl.BlockSpec((8, 128), lambda i: (0, 0))],
        out_shape=[jax.ShapeDtypeStruct((SEQ, D_MODEL), _F32), jax.ShapeDtypeStruct((SEQ, D_MODEL), _MXU_DTYPE),
                   jax.ShapeDtypeStruct((8, D_MODEL), _F32), jax.ShapeDtypeStruct((8, D_MODEL), _F32),
                   jax.ShapeDtypeStruct((8, 128), _F32)],
        name=name, compiler_params=_cparams(dimension_semantics=("arbitrary",)),
    )(x, y, g.reshape(1, D_MODEL), b.reshape(1, D_MODEL), target)
    return part[0, 0], dz, dzb, dg[0], db[0]


_CONV_TN = 256


def _shift_down(v, k, rows):
    return jnp.where(rows >= k, pltpu.roll(v, k, axis=0), 0.0)


def _shift_up(v, k, rows):
    return jnp.where(rows < SEQ - k, pltpu.roll(v, SEQ - k, axis=0), 0.0)


def _conv_gate_fwd(g, u, conv_w, conv_b, *, name):
    def body(g_ref, u_ref, w_ref, b_ref, h_ref):
        gv = g_ref[...].astype(_F32)
        rows = lax.broadcasted_iota(jnp.int32, gv.shape, 0)
        w = w_ref[...]
        gc = b_ref[...] + w[2:3, :] * gv + w[1:2, :] * _shift_down(gv, 1, rows) + w[0:1, :] * _shift_down(gv, 2, rows)
        h_ref[...] = (gc * _sigmoid(gc) * u_ref[...].astype(_F32)).astype(h_ref.dtype)

    col = pl.BlockSpec((SEQ, _CONV_TN), lambda j: (0, j))
    return pl.pallas_call(
        body, grid=(D_FF // _CONV_TN,),
        in_specs=[col, col, pl.BlockSpec((3, _CONV_TN), lambda j: (0, j)), pl.BlockSpec((1, _CONV_TN), lambda j: (0, j))],
        out_specs=col, out_shape=jax.ShapeDtypeStruct((SEQ, D_FF), _MXU_DTYPE), name=name,
        compiler_params=_cparams(dimension_semantics=("parallel",)),
    )(g, u, conv_w, conv_b.reshape(1, D_FF))


def _conv_gate_bwd(dh, g, u, conv_w, conv_b, *, name):
    def body(dh_ref, g_ref, u_ref, w_ref, b_ref, dg_ref, du_ref, dw_ref, db_ref):
        gv = g_ref[...].astype(_F32)
        rows = lax.broadcasted_iota(jnp.int32, gv.shape, 0)
        w = w_ref[...]
        g1 = _shift_down(gv, 1, rows)
        g2 = _shift_down(gv, 2, rows)
        gc = b_ref[...] + w[2:3, :] * gv + w[1:2, :] * g1 + w[0:1, :] * g2
        sg = _sigmoid(gc)
        dh = dh_ref[...].astype(_F32)
        du_ref[...] = (dh * (gc * sg)).astype(du_ref.dtype)
        dgc = dh * u_ref[...].astype(_F32) * (sg * (1.0 + gc * (1.0 - sg)))
        dg = w[2:3, :] * dgc + w[1:2, :] * _shift_up(dgc, 1, rows) + w[0:1, :] * _shift_up(dgc, 2, rows)
        dg_ref[...] = dg.astype(dg_ref.dtype)
        dw_ref[0:1, :] = jnp.sum(dgc * g2, axis=0, keepdims=True)
        dw_ref[1:2, :] = jnp.sum(dgc * g1, axis=0, keepdims=True)
        dw_ref[2:3, :] = jnp.sum(dgc * gv, axis=0, keepdims=True)
        db_ref[...] = jnp.sum(dgc, axis=0, keepdims=True)

    col = pl.BlockSpec((SEQ, _CONV_TN), lambda j: (0, j))
    w3 = pl.BlockSpec((3, _CONV_TN), lambda j: (0, j))
    w1 = pl.BlockSpec((1, _CONV_TN), lambda j: (0, j))
    dg, du, dw, db = pl.pallas_call(
        body, grid=(D_FF // _CONV_TN,), in_specs=[col, col, col, w3, w1], out_specs=[col, col, w3, w1],
        out_shape=[jax.ShapeDtypeStruct((SEQ, D_FF), _MXU_DTYPE), jax.ShapeDtypeStruct((SEQ, D_FF), _MXU_DTYPE),
                   jax.ShapeDtypeStruct((3, D_FF), _F32), jax.ShapeDtypeStruct((1, D_FF), _F32)],
        name=name, compiler_params=_cparams(dimension_semantics=("parallel",)),
    )(dh, g, u, conv_w, conv_b.reshape(1, D_FF))
    return dg, du, dw, db[0]


def _rope_tables():
    half = ROPE_DIM // 2
    inv = ROPE_THETA ** (-jnp.arange(0, ROPE_DIM, 2, dtype=_F32) / ROPE_DIM)
    ang = jnp.arange(SEQ, dtype=_F32)[:, None] * inv[None, :]
    cos, sin = jnp.cos(ang), jnp.sin(ang)
    rest = HEAD_DIM - ROPE_DIM
    c = jnp.concatenate([cos, cos, jnp.ones((SEQ, rest), _F32)], axis=1)
    s1 = jnp.concatenate([-sin, jnp.zeros((SEQ, HEAD_DIM - half), _F32)], axis=1)
    s2 = jnp.concatenate([jnp.zeros((SEQ, half), _F32), sin, jnp.zeros((SEQ, rest), _F32)], axis=1)
    return c, s1, s2


def _rope_apply(x, c, s1, s2):
    return x * c + pltpu.roll(x, HEAD_DIM - ROPE_DIM // 2, axis=1) * s1 + pltpu.roll(x, ROPE_DIM // 2, axis=1) * s2


def _rope_transpose(d, c, s1, s2):
    half = ROPE_DIM // 2
    return d * c + pltpu.roll(d * s1, half, axis=1) + pltpu.roll(d * s2, HEAD_DIM - half, axis=1)


_NT = (((1,), (1,)), ((), ()))
_TN = (((0,), (0,)), ((), ()))
_SCALE = HEAD_DIM ** -0.5


def _band_scores(q, k2, n, lag_off):
    s = lax.dot_general(q, k2, _NT, preferred_element_type=_F32) * _SCALE
    row = lax.broadcasted_iota(jnp.int32, (BLOCK, 2 * BLOCK), 0)
    col = lax.broadcasted_iota(jnp.int32, (BLOCK, 2 * BLOCK), 1)
    front = (col >= row + lag_off) & (col < BLOCK) & (n > 0)
    own = (col >= BLOCK) & (col <= row + BLOCK)
    return jnp.where(front | own, s, _NEG)


_BAND_STEPS = SEQ // BLOCK


def _rows(start, d):
    if d == 1:
        return pl.ds(pl.multiple_of(start, BLOCK), BLOCK)
    return pl.ds(start, BLOCK, stride=d)


def _band_block(it, d):
    r, n = it % d, it // d
    span = BLOCK * d
    return n, _rows(r + n * span, d), _rows(r + jnp.maximum(n - 1, 0) * span, d)


def _band_fwd(proj, tabs, *, kv_heads, q_per_kv, q0, k0, v0, dilations, lag_off, sink, name):
    heads = kv_heads * q_per_kv

    def body(*refs):
        q_refs = refs[:q_per_kv]
        k_ref, v_ref, c_ref, s1_ref, s2_ref = refs[q_per_kv:q_per_kv + 5]
        rest = refs[q_per_kv + 5:]
        if sink is not None:
            sk_ref, rest = rest[0], rest[1:]
        o_ref, lse_ref, qs, ks, m_s, l_s, acc_s = rest
        c, s1, s2 = c_ref[...], s1_ref[...], s2_ref[...]
        ks[...] = _rope_apply(k_ref[...], c, s1, s2)
        for i in range(q_per_kv):
            qs[...] = _rope_apply(q_refs[i][...], c, s1, s2)
            for pi, d in enumerate(dilations):
                def step(it, carry, d=d, first=(pi == 0)):
                    n, cur, prev = _band_block(it, d)
                    q = qs[cur, :].astype(_MXU_DTYPE)
                    k2 = jnp.concatenate([ks[prev, :], ks[cur, :]], axis=0).astype(_MXU_DTYPE)
                    v2 = jnp.concatenate([v_ref[prev, :], v_ref[cur, :]], axis=0).astype(_MXU_DTYPE)
                    s = _band_scores(q, k2, n, lag_off)
                    m_b = jnp.max(s, axis=1, keepdims=True)
                    m_new = m_b if first else jnp.maximum(m_b, m_s[cur, :][:, 0:1])
                    p = jnp.exp(s - m_new)
                    l_new = jnp.sum(p, axis=1, keepdims=True)
                    acc = jnp.dot(p.astype(_MXU_DTYPE), v2, preferred_element_type=_F32)
                    if not first:
                        a = jnp.exp(m_s[cur, :][:, 0:1] - m_new)
                        l_new = l_new + a * l_s[cur, :][:, 0:1]
                        acc = acc + a * acc_s[cur, :]
                    m_s[cur, :] = jnp.broadcast_to(m_new, (BLOCK, HEAD_DIM))
                    l_s[cur, :] = jnp.broadcast_to(l_new, (BLOCK, HEAD_DIM))
                    acc_s[cur, :] = acc
                    return carry

                lax.fori_loop(0, _BAND_STEPS, step, 0, unroll=16)
            m, den = m_s[...], l_s[...]
            if sink is not None:
                sk = sk_ref[i]
                m_f = jnp.maximum(m, sk)
                a = jnp.exp(m - m_f)
                den = den * a + jnp.exp(sk - m_f)
                o = acc_s[...] * a / den
                m = m_f
            else:
                o = acc_s[...] / den
            o_ref[:, i * HEAD_DIM:(i + 1) * HEAD_DIM] = o
            lse_ref[:, i * HEAD_DIM:(i + 1) * HEAD_DIM] = m + jnp.log(den)

    col = (SEQ, HEAD_DIM)
    in_specs = [pl.BlockSpec(col, functools.partial(lambda g, i: (0, A_COLS + q0 + g * q_per_kv + i), i=i)) for i in range(q_per_kv)]
    in_specs += [pl.BlockSpec(col, lambda g: (0, A_COLS + k0 + g)), pl.BlockSpec(col, lambda g: (0, A_COLS + v0 + g))]
    in_specs += [pl.BlockSpec(col, lambda g: (0, 0))] * 3
    args = [proj] * (q_per_kv + 2) + list(tabs)
    if sink is not None:
        in_specs.append(pl.BlockSpec((q_per_kv, 1, HEAD_DIM), lambda g: (g, 0, 0)))
        args.append(jnp.broadcast_to(sink.reshape(heads, 1, 1), (heads, 1, HEAD_DIM)))
    o_spec = pl.BlockSpec((SEQ, q_per_kv * HEAD_DIM), lambda g: (0, g))
    shape = jax.ShapeDtypeStruct((SEQ, heads * HEAD_DIM), _F32)
    return pl.pallas_call(
        body, grid=(kv_heads,), in_specs=in_specs, out_specs=[o_spec, o_spec], out_shape=[shape, shape],
        scratch_shapes=[pltpu.VMEM(col, _F32)] * 5, name=name,
        compiler_params=_cparams(dimension_semantics=("parallel",)),
    )(*args)


def _band_bwd(proj, tabs, dmixed, o, lse, *, kv_heads, q_per_kv, q0, k0, v0, do0, dilations, lag_off, sink, after=(), name):
    heads = kv_heads * q_per_kv

    def body(*refs):
        q_refs = refs[:q_per_kv]
        k_ref, v_ref, c_ref, s1_ref, s2_ref = refs[q_per_kv:q_per_kv + 5]
        do_refs = refs[q_per_kv + 5:2 * q_per_kv + 5]
        o_ref, lse_ref = refs[2 * q_per_kv + 5:2 * q_per_kv + 7]
        rest = refs[2 * q_per_kv + 7:]
        if sink is not None:
            sk_ref, rest = rest[0], rest[1:]
            dq_ref, dk_ref, dv_ref, dsk_ref, qs, ks, dq_s, dk_s, dv_s = rest[len(after):]
        else:
            dq_ref, dk_ref, dv_ref, qs, ks, dq_s, dk_s, dv_s = rest[len(after):]
        c, s1, s2 = c_ref[...], s1_ref[...], s2_ref[...]
        ks[...] = _rope_apply(k_ref[...], c, s1, s2)
        dk_s[...] = jnp.zeros_like(dk_s)
        dv_s[...] = jnp.zeros_like(dv_s)
        for i in range(q_per_kv):
            hs = slice(i * HEAD_DIM, (i + 1) * HEAD_DIM)
            qs[...] = _rope_apply(q_refs[i][...], c, s1, s2)
            dq_s[...] = jnp.zeros_like(dq_s)
            do_ref = do_refs[i]
            for d in dilations:
                def step(it, carry, d=d, do_ref=do_ref, hs=hs):
                    n, cur, prev = _band_block(it, d)
                    q = qs[cur, :].astype(_MXU_DTYPE)
                    k2 = jnp.concatenate([ks[prev, :], ks[cur, :]], axis=0).astype(_MXU_DTYPE)
                    v2 = jnp.concatenate([v_ref[prev, :], v_ref[cur, :]], axis=0).astype(_MXU_DTYPE)
                    do = do_ref[cur, :]
                    delta = jnp.sum(do * o_ref[cur, hs], axis=1, keepdims=True)
                    lse_c = lse_ref[cur, hs][:, 0:1]
                    p = jnp.exp(_band_scores(q, k2, n, lag_off) - lse_c)
                    dob = do.astype(_MXU_DTYPE)
                    ds = (p * (lax.dot_general(dob, v2, _NT, preferred_element_type=_F32) - delta) * _SCALE).astype(_MXU_DTYPE)
                    dq_s[cur, :] += jnp.dot(ds, k2, preferred_element_type=_F32)
                    dk2 = lax.dot_general(ds, q, _TN, preferred_element_type=_F32)
                    dv2 = lax.dot_general(p.astype(_MXU_DTYPE), dob, _TN, preferred_element_type=_F32)
                    dk_s[prev, :] += dk2[:BLOCK]
                    dv_s[prev, :] += dv2[:BLOCK]
                    dk_s[cur, :] += dk2[BLOCK:]
                    dv_s[cur, :] += dv2[BLOCK:]
                    return carry

                lax.fori_loop(0, _BAND_STEPS, step, 0, unroll=16)
            dq_ref[:, hs] = _rope_transpose(dq_s[...], c, s1, s2).astype(dq_ref.dtype)
            if sink is not None:
                delta = jnp.sum(do_ref[...] * o_ref[:, hs], axis=1, keepdims=True)
                w_sink = jnp.exp(sk_ref[i] - lse_ref[:, hs])
                dsk_ref[i] = jnp.broadcast_to(jnp.sum(-delta * w_sink[:, 0:1]), (8, HEAD_DIM))
        dk_ref[...] = _rope_transpose(dk_s[...], c, s1, s2).astype(dk_ref.dtype)
        dv_ref[...] = dv_s[...].astype(dv_ref.dtype)

    col = (SEQ, HEAD_DIM)
    in_specs = [pl.BlockSpec(col, functools.partial(lambda g, i: (0, A_COLS + q0 + g * q_per_kv + i), i=i)) for i in range(q_per_kv)]
    in_specs += [pl.BlockSpec(col, lambda g: (0, A_COLS + k0 + g)), pl.BlockSpec(col, lambda g: (0, A_COLS + v0 + g))]
    in_specs += [pl.BlockSpec(col, lambda g: (0, 0))] * 3
    in_specs += [pl.BlockSpec(col, functools.partial(lambda g, i: (0, do0 + g * q_per_kv + i), i=i)) for i in range(q_per_kv)]
    wide = pl.BlockSpec((SEQ, q_per_kv * HEAD_DIM), lambda g: (0, g))
    in_specs += [wide, wide]
    args = [proj] * (q_per_kv + 2) + list(tabs) + [dmixed] * q_per_kv + [o, lse]
    out_specs = [wide, pl.BlockSpec(col, lambda g: (0, g)), pl.BlockSpec(col, lambda g: (0, g))]
    out_shape = [jax.ShapeDtypeStruct((SEQ, heads * HEAD_DIM), _MXU_DTYPE), jax.ShapeDtypeStruct((SEQ, kv_heads * HEAD_DIM), _MXU_DTYPE),
                 jax.ShapeDtypeStruct((SEQ, kv_heads * HEAD_DIM), _MXU_DTYPE)]
    if sink is not None:
        in_specs.append(pl.BlockSpec((q_per_kv, 1, HEAD_DIM), lambda g: (g, 0, 0)))
        args.append(jnp.broadcast_to(sink.reshape(heads, 1, 1), (heads, 1, HEAD_DIM)))
        out_specs.append(pl.BlockSpec((q_per_kv, 8, HEAD_DIM), lambda g: (g, 0, 0)))
        out_shape.append(jax.ShapeDtypeStruct((heads, 8, HEAD_DIM), _F32))
    in_specs += [pl.BlockSpec(memory_space=pl.ANY)] * len(after)
    args += list(after)
    res = pl.pallas_call(
        body, grid=(kv_heads,), in_specs=in_specs, out_specs=out_specs, out_shape=out_shape,
        scratch_shapes=[pltpu.VMEM(col, _F32)] * 5, name=name,
        compiler_params=_cparams(dimension_semantics=("parallel",)),
    )(*args)
    if sink is not None:
        return res[0], res[1], res[2], res[3][:, 0, 0]
    return res


_DILATED = dict(kv_heads=B_HEADS, q_per_kv=1, q0=QB0, k0=KB0, v0=VB0, dilations=DILATIONS, lag_off=0, sink=None)
_SWA = dict(kv_heads=C_KV_HEADS, q_per_kv=C_HEADS // C_KV_HEADS, q0=QC0, k0=KC0, v0=VC0, dilations=(1,), lag_off=1)


_HG_TILE = 128
_HG_CHUNKS = _HG_TILE // A_CHUNK
_HG_TILES = SEQ // _HG_TILE
_HI = lax.Precision.HIGHEST


def _chunk_tri():
    i = np.arange(_HG_TILE)
    return jnp.asarray(((i[:, None] // A_CHUNK == i[None, :] // A_CHUNK) & (i[None, :] <= i[:, None])).astype(np.float32))


def _layer_lb(lb_ref, layer):
    if layer == 0:
        return jnp.zeros((1, HEAD_DIM), _F32)
    lg = lb_ref[...]
    m = jnp.max(lg, axis=0, keepdims=True)
    e = jnp.exp(lg - m)
    return e[1:2, :] / jnp.sum(e, axis=0, keepdims=True)


def _hgrn_gates(q, fr, lb):
    sgq = _sigmoid(q)
    sg = _sigmoid(fr)
    f = lb + (1.0 - lb) * sg
    return sgq, q * sgq, sg, f, 1.0 - f


def _hgrn_fwd(proj, lb_logits, norm_w, layer, *, name):
    tri = _chunk_tri()

    def body(q_ref, f_ref, i_ref, g_ref, lb_ref, nw_ref, tri_ref, o_ref, raw_ref, st_ref, state):
        @pl.when(pl.program_id(1) == 0)
        def _():
            state[...] = jnp.zeros_like(state)

        lb = _layer_lb(lb_ref, layer)
        _, qs, _, f, k = _hgrn_gates(q_ref[...], f_ref[...], lb)
        v = i_ref[...]
        b = jnp.dot(tri_ref[...], jnp.log(f), precision=_HI, preferred_element_type=_F32)
        eb = jnp.exp(b)
        ridx = lax.broadcasted_iota(jnp.int32, (A_CHUNK, HEAD_DIM), 0)
        outs = []
        st = state[...]
        for c in range(_HG_CHUNKS):
            sl = slice(c * A_CHUNK, (c + 1) * A_CHUNK)
            bc, qc, kc, vc = b[sl], qs[sl], k[sl], v[sl]
            bl = bc[A_CHUNK - 1:A_CHUNK]
            st_ref[0, c] = st
            o_c = lax.dot_general((qc * eb[sl]).astype(_MXU_DTYPE), st.astype(_MXU_DTYPE), _NT, preferred_element_type=_F32)
            rows = []
            for i in range(A_CHUNK):
                di = jnp.exp(jnp.where(ridx <= i, bc[i:i + 1] - bc, _NEG))
                a = jnp.sum(qc[i:i + 1] * kc * di, axis=1, keepdims=True)
                rows.append(jnp.sum(a * vc, axis=0, keepdims=True))
            outs.append(o_c + jnp.concatenate(rows, axis=0))
            kt = (kc * jnp.exp(bl - bc)).astype(_MXU_DTYPE)
            st = st * jnp.exp(bl) + lax.dot_general(vc.astype(_MXU_DTYPE), kt, _TN, preferred_element_type=_F32)
        state[...] = st
        o = jnp.concatenate(outs, axis=0)
        raw_ref[...] = o
        r = lax.rsqrt(jnp.mean(o * o, axis=-1, keepdims=True) + LN_EPS)
        g = g_ref[...]
        o_ref[...] = o * r * nw_ref[...] * (g * _sigmoid(g))

    blk = (_HG_TILE, HEAD_DIM)

    def col(base):
        return pl.BlockSpec(blk, lambda h, t: (t, base + h))

    o_spec = pl.BlockSpec(blk, lambda h, t: (t, h))
    o_shape = jax.ShapeDtypeStruct((SEQ, A_HEADS * HEAD_DIM), _F32)
    return pl.pallas_call(
        body, grid=(A_HEADS, _HG_TILES),
        in_specs=[col(0), col(4), col(8), col(12), pl.BlockSpec((DEPTH, HEAD_DIM), lambda h, t: (0, h)),
                  pl.BlockSpec((1, HEAD_DIM), lambda h, t: (0, 0)), pl.BlockSpec((_HG_TILE, _HG_TILE), lambda h, t: (0, 0))],
        out_specs=[o_spec, o_spec, pl.BlockSpec((1, _HG_CHUNKS, HEAD_DIM, HEAD_DIM), lambda h, t: (h, t, 0, 0))],
        out_shape=[o_shape, o_shape, jax.ShapeDtypeStruct((A_HEADS, SEQ // A_CHUNK, HEAD_DIM, HEAD_DIM), _F32)],
        scratch_shapes=[pltpu.VMEM((HEAD_DIM, HEAD_DIM), _F32)], name=name,
        compiler_params=_cparams(dimension_semantics=("parallel", "arbitrary")),
    )(proj, proj, proj, proj, lb_logits, norm_w.reshape(1, HEAD_DIM), tri)


def _hgrn_bwd(proj, lb_logits, norm_w, raw, states, dmixed, layer, *, name):
    tri = _chunk_tri()
    triu = tri.T

    def body(q_ref, f_ref, i_ref, g_ref, lb_ref, nw_ref, tri_ref, triu_ref, raw_ref, do_ref, st_ref,
             dq_ref, df_ref, di_ref, dg_ref, dnw_ref, dlb_ref, dstate):
        @pl.when(pl.program_id(1) == 0)
        def _():
            dstate[...] = jnp.zeros_like(dstate)
            dlb_ref[...] = jnp.zeros_like(dlb_ref)

        @pl.when((pl.program_id(0) == 0) & (pl.program_id(1) == 0))
        def _():
            dnw_ref[...] = jnp.zeros_like(dnw_ref)

        lb = _layer_lb(lb_ref, layer)
        q = q_ref[...]
        sgq, qs, sg, f, k = _hgrn_gates(q, f_ref[...], lb)
        v = i_ref[...]
        b = jnp.dot(tri_ref[...], jnp.log(f), precision=_HI, preferred_element_type=_F32)
        eb = jnp.exp(b)
        g = g_ref[...]
        nw = nw_ref[...]
        o = raw_ref[...]
        dout = do_ref[...]
        sgg = _sigmoid(g)
        r = lax.rsqrt(jnp.mean(o * o, axis=-1, keepdims=True) + LN_EPS)
        dg_ref[...] = (dout * (o * r * nw) * (sgg * (1.0 + g * (1.0 - sgg)))).astype(dg_ref.dtype)
        don = dout * (g * sgg)
        dnw_ref[0:1, :] += jnp.sum(don * o * r, axis=0, keepdims=True)
        dy = don * nw
        do_raw = r * dy - o * (r * r * r) * jnp.mean(o * dy, axis=-1, keepdims=True)

        ridx = lax.broadcasted_iota(jnp.int32, (A_CHUNK, HEAD_DIM), 0)
        dqs_t, dk_t, db_t, dv_t = [None] * _HG_CHUNKS, [None] * _HG_CHUNKS, [None] * _HG_CHUNKS, [None] * _HG_CHUNKS
        dst = dstate[...]
        for c in reversed(range(_HG_CHUNKS)):
            sl = slice(c * A_CHUNK, (c + 1) * A_CHUNK)
            bc, qc, kc, vc, doc = b[sl], qs[sl], k[sl], v[sl], do_raw[sl]
            bl = bc[A_CHUNK - 1:A_CHUNK]
            ebc = eb[sl]
            ebl = jnp.exp(bl - bc)
            lam = jnp.exp(bl)
            qt = qc * ebc
            kt = kc * ebl
            stp = st_ref[0, c]
            dob = doc.astype(_MXU_DTYPE)
            dstb = dst.astype(_MXU_DTYPE)
            dqt = jnp.dot(dob, stp.astype(_MXU_DTYPE), preferred_element_type=_F32)
            dkt = jnp.dot(vc.astype(_MXU_DTYPE), dstb, preferred_element_type=_F32)
            dv = lax.dot_general(kt.astype(_MXU_DTYPE), dstb, _NT, preferred_element_type=_F32)
            dlam = jnp.sum(stp * dst, axis=0, keepdims=True)
            dst = dst * lam + lax.dot_general(dob, qt.astype(_MXU_DTYPE), _TN, preferred_element_type=_F32)
            dqs_rows = []
            dk_in = jnp.zeros((A_CHUNK, HEAD_DIM), _F32)
            for i in range(A_CHUNK):
                di = jnp.exp(jnp.where(ridx <= i, bc[i:i + 1] - bc, _NEG))
                qi = qc[i:i + 1]
                doi = doc[i:i + 1]
                w = kc * di
                a = jnp.sum(qi * w, axis=1, keepdims=True)
                dv = dv + a * doi
                da = jnp.sum(doi * vc, axis=1, keepdims=True)
                dqs_rows.append(jnp.sum(da * w, axis=0, keepdims=True))
                dk_in = dk_in + da * (qi * di)
            dqs_in = jnp.concatenate(dqs_rows, axis=0)
            dbl = jnp.sum(dkt * kt, axis=0, keepdims=True) + dlam * lam
            db = qc * dqs_in - kc * dk_in + dqt * qt - dkt * kt
            db_t[c] = db + jnp.where(ridx == A_CHUNK - 1, dbl, 0.0)
            dqs_t[c] = dqs_in + dqt * ebc
            dk_t[c] = dk_in + dkt * ebl
            dv_t[c] = dv
        dstate[...] = dst
        dqs = jnp.concatenate(dqs_t, axis=0)
        dk = jnp.concatenate(dk_t, axis=0)
        db = jnp.concatenate(db_t, axis=0)
        di_ref[...] = jnp.concatenate(dv_t, axis=0).astype(di_ref.dtype)
        dlogf = jnp.dot(triu_ref[...], db, precision=_HI, preferred_element_type=_F32)
        df = dlogf / f - dk
        df_ref[...] = (df * (1.0 - lb) * sg * (1.0 - sg)).astype(df_ref.dtype)
        dlb_ref[0, 0:1, :] += jnp.sum(df * (1.0 - sg), axis=0, keepdims=True)
        dq_ref[...] = (dqs * (sgq * (1.0 + q * (1.0 - sgq)))).astype(dq_ref.dtype)

    blk = (_HG_TILE, HEAD_DIM)
    last = _HG_TILES - 1

    def col(base):
        return pl.BlockSpec(blk, lambda h, t: (last - t, base + h))

    tri_spec = pl.BlockSpec((_HG_TILE, _HG_TILE), lambda h, t: (0, 0))
    acc_spec = pl.BlockSpec((1, 8, HEAD_DIM), lambda h, t: (h, 0, 0))
    acc_shape = jax.ShapeDtypeStruct((A_HEADS, 8, HEAD_DIM), _F32)
    dq, df, di, dg, dnw, dlb = pl.pallas_call(
        body, grid=(A_HEADS, _HG_TILES),
        in_specs=[col(0), col(4), col(8), col(12), pl.BlockSpec((DEPTH, HEAD_DIM), lambda h, t: (0, h)),
                  pl.BlockSpec((1, HEAD_DIM), lambda h, t: (0, 0)), tri_spec, tri_spec, col(0), col(0),
                  pl.BlockSpec((1, _HG_CHUNKS, HEAD_DIM, HEAD_DIM), lambda h, t: (h, last - t, 0, 0))],
        out_specs=[col(0), col(0), col(0), col(0), pl.BlockSpec((8, HEAD_DIM), lambda h, t: (0, 0)), acc_spec],
        out_shape=[jax.ShapeDtypeStruct((SEQ, A_HEADS * HEAD_DIM), _MXU_DTYPE)] * 4
        + [jax.ShapeDtypeStruct((8, HEAD_DIM), _F32), acc_shape],
        scratch_shapes=[pltpu.VMEM((HEAD_DIM, HEAD_DIM), _F32)], name=name,
        compiler_params=_cparams(dimension_semantics=("arbitrary", "arbitrary")),
    )(proj, proj, proj, proj, lb_logits, norm_w.reshape(1, HEAD_DIM), tri, triu, raw, dmixed, states)
    return dq, df, di, dg, dnw[0], dlb[:, 0, :].reshape(A_HEADS * HEAD_DIM)


N_CHIP = N_DEV // 2
_MESH_ID = pl.DeviceIdType.MESH


def _place():
    x, y, c = lax.axis_index("x"), lax.axis_index("y"), lax.axis_index("c")
    chips = [(1 - x, y), (x, 1 - y), (1 - x, 1 - y)]
    return x, y, c, 2 * x + y, chips


def _sibling_swap(arrays, *, name):
    n = len(arrays)

    def body(*refs):
        ins, outs = refs[:n], refs[n:2 * n]
        send_sems, recv_sems = refs[2 * n:]
        x, y, c, _, _ = _place()
        copies = [pltpu.make_async_remote_copy(
            src_ref=ins[a].at[:, 1 - c], dst_ref=outs[a], send_sem=send_sems.at[a], recv_sem=recv_sems.at[a],
            device_id=(x, y, 1 - c), device_id_type=_MESH_ID) for a in range(n)]
        for cp in copies:
            cp.start()
        for cp in copies:
            cp.wait()

    any_spec = pl.BlockSpec(memory_space=pl.ANY)
    return pl.pallas_call(
        body, in_specs=[any_spec] * n, out_specs=[any_spec] * n,
        out_shape=[jax.ShapeDtypeStruct((N_CHIP,) + a.shape[2:], a.dtype) for a in arrays],
        scratch_shapes=[pltpu.SemaphoreType.DMA((n,)), pltpu.SemaphoreType.DMA((n,))],
        name=name, compiler_params=pltpu.CompilerParams(has_side_effects=True),
    )(*arrays)


def _pair_add(mine, theirs, core, *, name):
    _, _, R, C = mine.shape
    tr = max(t for t in range(16, R + 1, 16) if R % t == 0 and t * C <= 512 * 1024)

    def body(core_ref, m_ref, t_ref, o_ref):
        del core_ref
        o_ref[...] = (m_ref[...].astype(_F32) + t_ref[...].astype(_F32)).astype(o_ref.dtype)

    grid_spec = pltpu.PrefetchScalarGridSpec(
        num_scalar_prefetch=1, grid=(N_CHIP, R // tr),
        in_specs=[pl.BlockSpec((None, None, tr, C), lambda q, i, core: (q, core[0], i, 0)),
                  pl.BlockSpec((None, tr, C), lambda q, i, core: (q, i, 0))],
        out_specs=pl.BlockSpec((None, tr, C), lambda q, i, core: (q, i, 0)))
    return pl.pallas_call(
        body, grid_spec=grid_spec, out_shape=jax.ShapeDtypeStruct((N_CHIP, R, C), mine.dtype), name=name,
        compiler_params=_cparams(dimension_semantics=("parallel", "parallel")),
    )(core.reshape(1), mine, theirs)


_HBM = pl.BlockSpec(memory_space=pltpu.HBM)
_SEM = pl.BlockSpec(memory_space=pltpu.SEMAPHORE)
_TOKEN = pl.BlockSpec(memory_space=pltpu.VMEM)
_DATAFLOW = pltpu.SideEffectType.DATAFLOW_SIDE_EFFECTING


def _hbm(a):
    return pltpu.HBM(a.shape, a.dtype)


def _token_shape():
    return jax.ShapeDtypeStruct((8, 128), _F32)


def _dev_slot(px, py, pc):
    return 4 * px + 2 * py + pc


def _gather_start(blocks, landings, *, name):
    n = len(blocks)

    def body(*refs):
        ins, lands = refs[:n], refs[n:2 * n]
        send_sems, d2d_sems, ici_sems = refs[2 * n:2 * n + 3]
        token = refs[-1]
        x, y, c, _, chips = _place()
        for a in range(n):
            dst = lands[a].at[_dev_slot(x, y, c)]
            pltpu.make_async_remote_copy(src_ref=ins[a], dst_ref=dst, send_sem=send_sems.at[4 * a], recv_sem=d2d_sems.at[a],
                                         device_id=(x, y, 1 - c), device_id_type=_MESH_ID).start()
            for j, chip in enumerate(chips):
                pltpu.make_async_remote_copy(src_ref=ins[a], dst_ref=dst, send_sem=send_sems.at[4 * a + 1 + j],
                                             recv_sem=ici_sems.at[3 * a + j], device_id=(*chip, c),
                                             device_id_type=_MESH_ID).start()
        token[...] = jnp.zeros_like(token)

    res = pl.pallas_call(
        body, name=name, in_specs=[_HBM] * (2 * n),
        out_shape=(pltpu.SemaphoreType.DMA((4 * n,)), pltpu.SemaphoreType.DMA((n,)), pltpu.SemaphoreType.DMA((3 * n,)),
                   *[_hbm(b) for b in blocks], *[_hbm(b) for b in landings], _token_shape()),
        out_specs=(_SEM, _SEM, _SEM, *[_HBM] * (2 * n), _TOKEN),
        input_output_aliases={i: 3 + i for i in range(2 * n)},
        compiler_params=pltpu.CompilerParams(has_side_effects=_DATAFLOW),
    )(*[pltpu.with_memory_space_constraint(b, pltpu.HBM) for b in blocks],
      *[pltpu.with_memory_space_constraint(b, pltpu.HBM) for b in landings])
    return res[0], res[1], res[2], list(res[3:3 + n]), list(res[3 + n:3 + 2 * n]), res[-1]


def _gather_forward(landings, ici_sems, first, after, *, name):
    n = len(landings)

    def body(*refs):
        lands = refs[:n]
        ici = refs[n]
        f_send, f_recv = refs[n + 2], refs[n + 3]
        token = refs[-1]
        x, y, c, _, chips = _place()
        for a in range(n):
            for j, chip in enumerate(chips):
                blk = lands[a].at[_dev_slot(*chip, c)]
                pltpu.make_async_remote_copy(src_ref=blk, dst_ref=blk, send_sem=f_send.at[3 * a + j],
                                             recv_sem=ici.at[3 * (first + a) + j], device_id=(*chip, c),
                                             device_id_type=_MESH_ID).wait_recv()
                pltpu.make_async_remote_copy(src_ref=blk, dst_ref=blk, send_sem=f_send.at[3 * a + j], recv_sem=f_recv.at[3 * a + j],
                                             device_id=(x, y, 1 - c), device_id_type=_MESH_ID).start()
        token[...] = jnp.zeros_like(token)

    res = pl.pallas_call(
        body, name=name, in_specs=[_HBM] * n + [_SEM, pl.BlockSpec(memory_space=pl.ANY)],
        out_shape=(pltpu.SemaphoreType.DMA((3 * n,)), pltpu.SemaphoreType.DMA((3 * n,)), *[_hbm(b) for b in landings], _token_shape()),
        out_specs=(_SEM, _SEM, *[_HBM] * n, _TOKEN),
        input_output_aliases={i: 2 + i for i in range(n)},
        compiler_params=pltpu.CompilerParams(has_side_effects=_DATAFLOW),
    )(*landings, ici_sems, after)
    return res[0], res[1], list(res[2:2 + n]), res[-1]


def _gather_wait(blocks, landings, send_sems, d2d_sems, first, f_send, f_recv, after, *, name):
    n = len(landings)

    def body(*refs):
        ins, lands = refs[:n], refs[n:2 * n]
        send, d2d, fs, fr = refs[2 * n:2 * n + 4]
        x, y, c, _, chips = _place()
        me = (x, y, c)
        for a in range(n):
            own = lands[a].at[_dev_slot(x, y, 1 - c)]
            g = first + a
            pltpu.make_async_remote_copy(src_ref=ins[a], dst_ref=own, send_sem=send.at[4 * g], recv_sem=d2d.at[g],
                                         device_id=me, device_id_type=_MESH_ID).wait_recv()
            for j, chip in enumerate(chips):
                blk = lands[a].at[_dev_slot(*chip, 1 - c)]
                pltpu.make_async_remote_copy(src_ref=blk, dst_ref=blk, send_sem=fs.at[3 * a + j], recv_sem=fr.at[3 * a + j],
                                             device_id=me, device_id_type=_MESH_ID).wait_recv()
            for k in range(4):
                pltpu.make_async_remote_copy(src_ref=ins[a], dst_ref=own, send_sem=send.at[4 * g + k], recv_sem=d2d.at[g],
                                             device_id=me, device_id_type=_MESH_ID).wait_send()
            for j in range(3):
                pltpu.make_async_remote_copy(src_ref=own, dst_ref=own, send_sem=fs.at[3 * a + j], recv_sem=fr.at[3 * a + j],
                                             device_id=me, device_id_type=_MESH_ID).wait_send()

    res = pl.pallas_call(
        body, name=name, in_specs=[_HBM] * (2 * n) + [_SEM] * 4 + [pl.BlockSpec(memory_space=pl.ANY)],
        out_shape=(*[_hbm(b) for b in blocks], *[_hbm(b) for b in landings]), out_specs=tuple([_HBM] * (2 * n)),
        input_output_aliases={i: i for i in range(2 * n)},
        compiler_params=pltpu.CompilerParams(has_side_effects=_DATAFLOW),
    )(*blocks, *landings, send_sems, d2d_sems, f_send, f_recv, after)
    return list(res[n:])


def _swap_start(mine, landings, *, name):
    n = len(mine)

    def body(*refs):
        ins, lands = refs[:n], refs[n:2 * n]
        send_sems, recv_sems = refs[2 * n:2 * n + 2]
        token = refs[-1]
        x, y, c, _, _ = _place()
        for a in range(n):
            pltpu.make_async_remote_copy(src_ref=ins[a].at[:, 1 - c], dst_ref=lands[a], send_sem=send_sems.at[a],
                                         recv_sem=recv_sems.at[a], device_id=(x, y, 1 - c), device_id_type=_MESH_ID).start()
        token[...] = jnp.zeros_like(token)

    res = pl.pallas_call(
        body, name=name, in_specs=[_HBM] * (2 * n),
        out_shape=(pltpu.SemaphoreType.DMA((n,)), pltpu.SemaphoreType.DMA((n,)),
                   *[_hbm(b) for b in mine], *[_hbm(b) for b in landings], _token_shape()),
        out_specs=(_SEM, _SEM, *[_HBM] * (2 * n), _TOKEN),
        input_output_aliases={i: 2 + i for i in range(2 * n)},
        compiler_params=pltpu.CompilerParams(has_side_effects=_DATAFLOW),
    )(*[pltpu.with_memory_space_constraint(b, pltpu.HBM) for b in mine],
      *[pltpu.with_memory_space_constraint(b, pltpu.HBM) for b in landings])
    return res[0], res[1], list(res[2:2 + n]), list(res[2 + n:2 + 2 * n]), res[-1]


def _swap_wait(mine, landings, send_sems, recv_sems, after, *, name):
    n = len(mine)

    def body(*refs):
        ins, lands = refs[:n], refs[n:2 * n]
        send, recv = refs[2 * n:2 * n + 2]
        x, y, c, _, _ = _place()
        for a in range(n):
            cp = pltpu.make_async_remote_copy(src_ref=ins[a].at[:, 1 - c], dst_ref=lands[a], send_sem=send.at[a],
                                              recv_sem=recv.at[a], device_id=(x, y, c), device_id_type=_MESH_ID)
            cp.wait_recv()
            cp.wait_send()

    res = pl.pallas_call(
        body, name=name, in_specs=[_HBM] * (2 * n) + [_SEM] * 2 + [pl.BlockSpec(memory_space=pl.ANY)],
        out_shape=(*[_hbm(b) for b in mine], *[_hbm(b) for b in landings]), out_specs=tuple([_HBM] * (2 * n)),
        input_output_aliases={i: i for i in range(2 * n)},
        compiler_params=pltpu.CompilerParams(has_side_effects=_DATAFLOW),
    )(*mine, *landings, send_sems, recv_sems, after)
    return list(res[:n]), list(res[n:])


def _chip_exchange_start(sums, landings, *, name):
    n = len(sums)

    def body(*refs):
        ins, lands = refs[:n], refs[n:2 * n]
        send_sems, recv_sems = refs[2 * n:2 * n + 2]
        token = refs[-1]
        _, _, c, p, chips = _place()
        for a in range(n):
            for j, (qx, qy) in enumerate(chips):
                pltpu.make_async_remote_copy(src_ref=ins[a].at[2 * qx + qy], dst_ref=lands[a].at[p], send_sem=send_sems.at[3 * a + j],
                                             recv_sem=recv_sems.at[3 * a + j], device_id=(qx, qy, c), device_id_type=_MESH_ID).start()
        token[...] = jnp.zeros_like(token)

    res = pl.pallas_call(
        body, name=name, in_specs=[_HBM] * (2 * n),
        out_shape=(pltpu.SemaphoreType.DMA((3 * n,)), pltpu.SemaphoreType.DMA((3 * n,)),
                   *[_hbm(b) for b in sums], *[_hbm(b) for b in landings], _token_shape()),
        out_specs=(_SEM, _SEM, *[_HBM] * (2 * n), _TOKEN),
        input_output_aliases={i: 2 + i for i in range(2 * n)},
        compiler_params=pltpu.CompilerParams(has_side_effects=_DATAFLOW),
    )(*[pltpu.with_memory_space_constraint(b, pltpu.HBM) for b in sums],
      *[pltpu.with_memory_space_constraint(b, pltpu.HBM) for b in landings])
    return res[0], res[1], list(res[2:2 + n]), list(res[2 + n:2 + 2 * n]), res[-1]


def _chip_exchange_wait(sums, landings, send_sems, recv_sems, after, *, name):
    n = len(sums)

    def body(*refs):
        ins, lands = refs[:n], refs[n:2 * n]
        send, recv = refs[2 * n:2 * n + 2]
        x, y, c, _, chips = _place()
        for a in range(n):
            for j, (qx, qy) in enumerate(chips):
                q = 2 * qx + qy
                cp = pltpu.make_async_remote_copy(src_ref=ins[a].at[q], dst_ref=lands[a].at[q], send_sem=send.at[3 * a + j],
                                                  recv_sem=recv.at[3 * a + j], device_id=(x, y, c), device_id_type=_MESH_ID)
                cp.wait_recv()
                cp.wait_send()

    res = pl.pallas_call(
        body, name=name, in_specs=[_HBM] * (2 * n) + [_SEM] * 2 + [pl.BlockSpec(memory_space=pl.ANY)] * len(after),
        out_shape=(*[_hbm(b) for b in sums], *[_hbm(b) for b in landings]), out_specs=tuple([_HBM] * (2 * n)),
        input_output_aliases={i: i for i in range(2 * n)},
        compiler_params=pltpu.CompilerParams(has_side_effects=_DATAFLOW),
    )(*sums, *landings, send_sems, recv_sems, *after)
    return list(res[:n]), list(res[n:])


_C1 = 1.0 - ADAM_B1 ** ADAM_STEP
_C2 = 1.0 - ADAM_B2 ** ADAM_STEP


def _adamw_math(g, w, m, v):
    m = ADAM_B1 * m + (1.0 - ADAM_B1) * g
    v = ADAM_B2 * v + (1.0 - ADAM_B2) * (g * g)
    delta = -ADAM_LR * ((m / _C1) / (jnp.sqrt(v / _C2) + ADAM_EPS) + ADAM_WD * w)
    return delta, m, v


def _adamw_reduce(landed, sums, chip, w, m, v, layer, prev, *, name):
    _, R, C = w.shape
    tr = max(t for t in range(16, R + 1, 16) if R % t == 0 and t * C <= 256 * 1024)

    def body(chip_ref, p_ref, own_ref, w_ref, m_ref, v_ref, *rest):
        g_ref, d_ref, nm_ref, nv_ref = rest[-4:]
        own = own_ref[...].astype(_F32)
        g = jnp.where(chip_ref[0] == 0, own, p_ref[0].astype(_F32))
        for q in range(1, N_CHIP):
            g = g + jnp.where(chip_ref[0] == q, own, p_ref[q].astype(_F32))
        d, nm, nv = _adamw_math(g, w_ref[...], m_ref[...], v_ref[...])
        g_ref[...] = g
        d_ref[...] = d
        nm_ref[...] = nm
        nv_ref[...] = nv

    blk = pl.BlockSpec((None, tr, C), lambda i, chip: (layer, i, 0))
    shape = jax.ShapeDtypeStruct((DEPTH, R, C), _F32)
    kept = [] if prev is None else list(prev)
    grid_spec = pltpu.PrefetchScalarGridSpec(
        num_scalar_prefetch=1, grid=(R // tr,),
        in_specs=[pl.BlockSpec((N_CHIP, tr, C), lambda i, chip: (0, i, 0)),
                  pl.BlockSpec((None, tr, C), lambda i, chip: (chip[0], i, 0)), blk, blk, blk]
        + [pl.BlockSpec(memory_space=pl.ANY)] * len(kept),
        out_specs=[blk] * 4)
    return pl.pallas_call(
        body, grid_spec=grid_spec, out_shape=[shape] * 4, name=name,
        input_output_aliases={6 + k: k for k in range(len(kept))},
        compiler_params=_cparams(dimension_semantics=("parallel",)),
    )(chip.reshape(1), landed, sums, w, m, v, *kept)


_PACK_LANES = 128
_LAYER_ROWS = 248
_LB_ROWS = (A_HEADS * HEAD_DIM) // _PACK_LANES


def _small_reduce(parts, lb_logits, *, name):
    rows = DEPTH * _LAYER_ROWS

    def body(p_ref, lg_ref, o_ref):
        g = p_ref[0]
        for s in range(1, N_DEV):
            g = g + p_ref[s]
        o_ref[...] = g
        lg = lg_ref[...]
        e = jnp.exp(lg - jnp.max(lg, axis=0, keepdims=True))
        p = e / jnp.sum(e, axis=0, keepdims=True)
        d1 = g[_LAYER_ROWS:_LAYER_ROWS + _LB_ROWS, :] * p[0] * p[1]
        o_ref[0:_LB_ROWS, :] = -d1
        o_ref[_LAYER_ROWS:_LAYER_ROWS + _LB_ROWS, :] = d1

    return pl.pallas_call(
        body, out_shape=jax.ShapeDtypeStruct((rows, _PACK_LANES), _F32), name=name,
        compiler_params=_cparams(),
    )(parts, lb_logits.reshape(DEPTH, _LB_ROWS, _PACK_LANES))


def _adamw_small(g, w, m, v, *, name):
    def body(g_ref, w_ref, m_ref, v_ref, d_ref, nm_ref, nv_ref):
        d, nm, nv = _adamw_math(g_ref[...], w_ref[...], m_ref[...], v_ref[...])
        d_ref[...] = d
        nm_ref[...] = nm
        nv_ref[...] = nv

    shape = jax.ShapeDtypeStruct(g.shape, _F32)
    return pl.pallas_call(body, out_shape=[shape] * 3, name=name, compiler_params=_cparams())(g, w, m, v)


def _pack(vectors, rows):
    flat = jnp.concatenate([v.reshape(-1).astype(_F32) for v in vectors])
    return jnp.pad(flat, (0, rows * _PACK_LANES - flat.shape[0])).reshape(rows, _PACK_LANES)


def _unpack(packed, shapes):
    flat = packed.reshape(-1)
    out, at = [], 0
    for s in shapes:
        size = int(np.prod(s))
        out.append(flat[at:at + size].reshape(s))
        at += size
    return out


_BIG = ("w_in", "w_gate", "w_up", "w_out", "w_down")
_COLUMN_SHARDED = ("w_in", "w_gate", "w_up")


def _full_weight(name, g):
    if name == "conv_w":
        return g.transpose(1, 0, 2).reshape(g.shape[1], N_DEV * SHARD_COLS)
    if name in _BIG:
        return g.reshape(N_DEV * g.shape[1], g.shape[2])
    return g


class _WeightGather:
    def __init__(self, names, first, blocks, lands, sems, tag):
        self.names, self.first, self.blocks, self.lands, self.sems, self.tag = names, first, blocks, lands, sems, tag
        self.forwarded = None

    def forward(self, after):
        f_send, f_recv, self.lands, token = _gather_forward(self.lands, self.sems[2], self.first, after,
                                                            name=f"gather_forward_{self.tag}")
        self.forwarded = (f_send, f_recv)
        return token

    def wait(self, after):
        if self.forwarded is None:
            self.forward(after)
        got = _gather_wait(self.blocks, self.lands, self.sems[0], self.sems[1], self.first, *self.forwarded, after,
                           name=f"gather_wait_{self.tag}")
        return {n: _full_weight(n, g) for n, g in zip(self.names, got)}


def _start_gathers(groups, me, name):
    blocks = [b for _, _, bs in groups for b in bs]
    landings = [lax.dynamic_update_index_in_dim(lax.empty((N_DEV,) + b.shape, b.dtype), b[None], me, 0) for b in blocks]
    send, d2d, ici, blocks, landings, token = _gather_start(blocks, landings, name=name)
    out, first = [], 0
    for tag, names, bs in groups:
        k = len(bs)
        out.append(_WeightGather(names, first, blocks[first:first + k], landings[first:first + k], (send, d2d, ici), tag))
        first += k
    return out, token


class _LayerWeights:
    def __init__(self, ready, pending=(), forwards=(), tokens=()):
        self.ready, self.pending, self.forwards, self._tokens = dict(ready), list(pending), list(forwards), list(tokens)

    def at(self, point, after):
        for when, gather in self.forwards:
            if when == point:
                self._tokens.append(gather.forward(after))

    def tokens(self):
        out, self._tokens = self._tokens, []
        return out

    def get(self, name, after):
        if name not in self.ready:
            group, = [g for g in self.pending if name in g.names]
            self.ready.update(group.wait(after))
        return self.ready[name]


def _layer_fwd(x, xb, ws, lb_logits, a_norm_w, c_sink, ln1_g, ln1_b, conv_b, ln2_g, ln2_b, tabs, l, target=None):
    proj = _mm(xb, ws.get("w_in", xb), tb=True, **_TILE_WIDE_N, after=ws.tokens(), name=f"proj_{l}")
    o_a, raw, states = _hgrn_fwd(proj, lb_logits, a_norm_w, l, name=f"hgrn_fwd_{l}")
    ws.at("hgrn", o_a)
    o_b, lse_b = _band_fwd(proj, tabs, name=f"dilated_fwd_{l}", **_DILATED)
    o_c, lse_c = _band_fwd(proj, tabs, sink=c_sink, name=f"swa_fwd_{l}", **_SWA)
    ws.at("swa", o_c)
    mixed = _concat_cols([o_a, o_b, o_c], name=f"mixed_{l}")
    y = _mm(mixed, ws.get("w_out", mixed), **_TILE_MIX, after=ws.tokens(), name=f"mix_out_{l}")
    z1, x1, x1b = _ln_fwd(x, y, ln1_g, ln1_b, name=f"ln1_fwd_{l}")
    g = _mm(x1b, ws.get("w_gate", x1b), tb=True, **_TILE_WIDE_N, out_dtype=_ACT_DTYPE, name=f"ffn_gate_{l}")
    u = _mm(x1b, ws.get("w_up", x1b), tb=True, **_TILE_WIDE_N, out_dtype=_ACT_DTYPE, name=f"ffn_up_{l}")
    ws.at("up", u)
    hb = _conv_gate_fwd(g, u, ws.get("conv_w", u), conv_b, name=f"conv_gate_fwd_{l}")
    y2 = _mm(hb, ws.get("w_down", hb), **_TILE_WIDE_K, after=ws.tokens(), name=f"ffn_down_{l}")
    ws.at("down", y2)
    res = dict(xb=xb, proj=proj, raw=raw, states=states, o_b=o_b, lse_b=lse_b, o_c=o_c, lse_c=lse_c,
               mixed=mixed, z1=z1, x1b=x1b, g=g, u=u, hb=hb)
    if target is not None:
        loss_part, *res["ln2_bwd"] = _ln_loss_bwd(x1, y2, ln2_g, ln2_b, target, name=f"ln2_loss_{l}")
        return loss_part, None, res
    res["z2"], x2, x2b = _ln_fwd(x1, y2, ln2_g, ln2_b, name=f"ln2_fwd_{l}")
    return x2, x2b, res


class _GradExchange:
    def __init__(self, core, chip):
        self.core, self.chip, self.groups, self.swapping, self._tokens = core, chip, [], [], []

    def launch(self, names, slabs, l, tag, behind):
        mine = [s.reshape((N_CHIP, 2) + s.shape[1:]) for s in slabs]
        if behind:
            landings = [lax.empty((N_CHIP,) + m.shape[2:], m.dtype) for m in mine]
            send, recv, mine, landings, token = _swap_start(mine, landings, name=f"swap_start_{tag}")
            self.swapping.append((names, l, tag, send, recv, mine, landings))
            self._tokens.append(token)
        else:
            self._exchange(names, l, tag, mine, _sibling_swap(mine, name=f"swap_grads_{tag}"))

    def advance(self, after):
        for names, l, tag, send, recv, mine, landings in self.swapping:
            mine, theirs = _swap_wait(mine, landings, send, recv, after, name=f"swap_wait_{tag}")
            self._exchange(names, l, tag, mine, theirs)
        self.swapping = []

    def _exchange(self, names, l, tag, mine, theirs):
        sums = [_pair_add(a, b, self.core, name=f"pair_add_{n}_{l}") for n, a, b in zip(names, mine, theirs)]
        landings = [lax.empty(s.shape, s.dtype) for s in sums]
        send, recv, sums, landings, token = _chip_exchange_start(sums, landings, name=f"exchange_start_{tag}")
        self.groups.append((names, l, tag, send, recv, sums, landings))
        self._tokens.append(token)

    def tokens(self):
        out, self._tokens = self._tokens, []
        return out

    def finish(self, weights, mom1, mom2, after):
        out = {}
        after = list(after) + self.tokens()
        for names, l, tag, send, recv, sums, landings in self.groups:
            sums, landings = _chip_exchange_wait(sums, landings, send, recv, after, name=f"exchange_wait_{tag}")
            for n, s, landed in zip(names, sums, landings):
                out[n] = _adamw_reduce(landed, s, self.chip, weights[n], mom1[n], mom2[n], l, out.get(n), name=f"adamw_{n}_{l}")
            after = [out[n][0] for n in names]
        return out


def _layer_bwd(dx2, res, w, lb_logits, a_norm_w, c_sink, ln1_g, conv_b, ln2_g, tabs, exchange, l):
    if "ln2_bwd" in res:
        dz2, dz2b, d_ln2_g, d_ln2_b = res["ln2_bwd"]
    else:
        dz2, dz2b, d_ln2_g, d_ln2_b = _ln_bwd(res["z2"], dx2, None, ln2_g, name=f"ln2_bwd_{l}")
    exchange.advance(dz2b)
    dh = _mm(dz2b, w["w_down"], tb=True, **_TILE_WIDE_N, out_dtype=_ACT_DTYPE, after=exchange.tokens(),
             name=f"ffn_down_dx_{l}")
    d_w_down = _mm(res["hb"], dz2b, ta=True, **_TILE_WIDE_M, out_dtype=_GRAD_DTYPE, name=f"ffn_down_dw_{l}")
    dg, du, d_conv_w, d_conv_b = _conv_gate_bwd(dh, res["g"], res["u"], w["conv_w"], conv_b, name=f"conv_gate_bwd_{l}")
    t = _mm(dg, w["w_gate"], **_TILE_WIDE_K, name=f"ffn_gate_dx_{l}")
    dx1 = _mm(du, w["w_up"], **_TILE_WIDE_K, add=t, name=f"ffn_up_dx_{l}")
    d_w_gate = _mm(dg, res["x1b"], ta=True, **_TILE_WIDE_M, out_dtype=_GRAD_DTYPE, name=f"ffn_gate_dw_{l}")
    d_w_up = _mm(du, res["x1b"], ta=True, **_TILE_WIDE_M, out_dtype=_GRAD_DTYPE, name=f"ffn_up_dw_{l}")
    dz1, dz1b, d_ln1_g, d_ln1_b = _ln_bwd(res["z1"], dx1, dz2, ln1_g, name=f"ln1_bwd_{l}")
    d_w_out = _mm(res["mixed"], dz1b, ta=True, **_TILE_MIX, out_dtype=_GRAD_DTYPE, name=f"mix_out_dw_{l}")
    exchange.launch(("w_down", "w_gate", "w_up", "w_out"),
                    [d.reshape(N_DEV, d.shape[0] // N_DEV, D_MODEL) for d in (d_w_down, d_w_gate, d_w_up, d_w_out)],
                    l, f"ffn_{l}", True)
    dmixed = _mm(dz1b, w["w_out"], tb=True, **_TILE_MIX, after=exchange.tokens(), name=f"mix_out_dx_{l}")
    dq_a, df_a, di_a, dg_a, d_norm_w, d_lb = _hgrn_bwd(res["proj"], lb_logits, a_norm_w, res["raw"], res["states"],
                                                      dmixed, l, name=f"hgrn_bwd_{l}")
    exchange.advance(dq_a)
    dq_b, dk_b, dv_b = _band_bwd(res["proj"], tabs, dmixed, res["o_b"], res["lse_b"], do0=A_HEADS, after=exchange.tokens(),
                                 name=f"dilated_bwd_{l}", **_DILATED)
    dq_c, dk_c, dv_c, d_sink = _band_bwd(res["proj"], tabs, dmixed, res["o_c"], res["lse_c"], do0=A_HEADS + B_HEADS,
                                         sink=c_sink, name=f"swa_bwd_{l}", **_SWA)
    dproj = _concat_cols([dq_a, df_a, di_a, dg_a, dq_b, dk_b, dv_b, dq_c, dk_c, dv_c], name=f"dproj_{l}")
    d_w_in = _mm(dproj, res["xb"], ta=True, **_TILE_WIDE_M, out_dtype=_GRAD_DTYPE, name=f"proj_dw_{l}")
    exchange.launch(("w_in",), [d_w_in.reshape(N_DEV, SHARD_COLS, D_MODEL)], l, f"mix_{l}", l > 0)
    dx = _mm(dproj, w["w_in"], **_TILE_WIDE_K, add=dz1, add_scale=ALPHA, after=exchange.tokens(), name=f"proj_dx_{l}")
    small = [d_lb, d_norm_w, jnp.pad(d_sink, (0, _PACK_LANES - C_HEADS)), d_ln1_g, d_ln1_b, d_ln2_g, d_ln2_b, d_conv_b,
             d_conv_w]
    return dx, small


def kernel(x, w_in, lb_logits, a_norm_w, c_sinks, w_out, ln1_g, ln1_b, w_gate, w_up, conv_w, conv_b, w_down, ln2_g, ln2_b, loss_target, m_w_in, m_lb_logits, m_a_norm_w, m_c_sinks, m_w_out, m_ln1_g, m_ln1_b, m_w_gate, m_w_up, m_conv_w, m_conv_b, m_w_down, m_ln2_g, m_ln2_b, v_w_in, v_lb_logits, v_a_norm_w, v_c_sinks, v_w_out, v_ln1_g, v_ln1_b, v_w_gate, v_w_up, v_conv_w, v_conv_b, v_w_down, v_ln2_g, v_ln2_b):
    weights = dict(w_in=w_in, lb_logits=lb_logits, a_norm_w=a_norm_w, c_sinks=c_sinks, w_out=w_out, ln1_g=ln1_g, ln1_b=ln1_b,
                   w_gate=w_gate, w_up=w_up, conv_w=conv_w, conv_b=conv_b, w_down=w_down, ln2_g=ln2_g, ln2_b=ln2_b)
    mom1 = dict(w_in=m_w_in, lb_logits=m_lb_logits, a_norm_w=m_a_norm_w, c_sinks=m_c_sinks, w_out=m_w_out, ln1_g=m_ln1_g,
                ln1_b=m_ln1_b, w_gate=m_w_gate, w_up=m_w_up, conv_w=m_conv_w, conv_b=m_conv_b, w_down=m_w_down, ln2_g=m_ln2_g,
                ln2_b=m_ln2_b)
    mom2 = dict(w_in=v_w_in, lb_logits=v_lb_logits, a_norm_w=v_a_norm_w, c_sinks=v_c_sinks, w_out=v_w_out, ln1_g=v_ln1_g,
                ln1_b=v_ln1_b, w_gate=v_w_gate, w_up=v_w_up, conv_w=v_conv_w, conv_b=v_conv_b, w_down=v_w_down, ln2_g=v_ln2_g,
                ln2_b=v_ln2_b)
    core = lax.axis_index("c").astype(jnp.int32)
    me = 4 * lax.axis_index("x") + 2 * lax.axis_index("y") + core
    tabs = _rope_tables()

    chip = (2 * lax.axis_index("x") + lax.axis_index("y")).astype(jnp.int32)

    def as_slabs(d):
        return {n: jnp.swapaxes(d[n], 1, 2) if n in _COLUMN_SHARDED else d[n] for n in _BIG}

    w_views = as_slabs(weights)

    def block(n, l, after=()):
        return conv_w[l] if n == "conv_w" else _cast_layer(w_views[n], l, after=after, name=f"cast_{n}_{l}")

    (in0,), started_first = _start_gathers([("w_in_0", ("w_in",), [block("w_in", 0)])], me, "gather_start_first")
    order = [(("w_out",), 0), (("w_gate", "w_up", "conv_w"), 0), (("w_down",), 0),
             (("w_in",), 1), (("w_out",), 1), (("w_gate", "w_up", "conv_w"), 1), (("w_down",), 1)]
    gathers, started = _start_gathers([(f"{names[0]}_{l}", names, [block(n, l, [started_first]) for n in names])
                                       for names, l in order], me, "gather_start_rest")
    out0, ffn0, down0, in1, out1, ffn1, down1 = gathers
    layer_ws = [_LayerWeights(in0.wait(started), [out0, ffn0, down0],
                              [("hgrn", out0), ("swa", ffn0), ("up", down0), ("down", in1)]),
                _LayerWeights({}, [in1, out1, ffn1, down1], [("hgrn", out1), ("swa", ffn1), ("up", down1)])]

    xs = x[0]
    xb = xs.astype(_MXU_DTYPE)
    saved = []
    for l in range(DEPTH):
        xs, xb, res = _layer_fwd(xs, xb, layer_ws[l], lb_logits, a_norm_w[l], c_sinks[l], ln1_g[l], ln1_b[l], conv_b[l],
                                 ln2_g[l], ln2_b[l], tabs, l, loss_target[0] if l == DEPTH - 1 else None)
        saved.append(res)
    loss = lax.psum(xs, ("x", "y", "c"))
    dx = None

    exchange = _GradExchange(core, chip)
    small_parts = [None] * DEPTH
    for l in reversed(range(DEPTH)):
        dx, small = _layer_bwd(dx, saved[l], layer_ws[l].ready, lb_logits, a_norm_w[l], c_sinks[l], ln1_g[l], conv_b[l],
                               ln2_g[l], tabs, exchange, l)
        small_parts[l] = _pack(small, _LAYER_ROWS)
    (small_gather,), small_started = _start_gathers(
        [("small_grads", ("small",), [jnp.concatenate(small_parts, axis=0)])], me, "gather_start_small")
    updated = exchange.finish(w_views, as_slabs(mom1), as_slabs(mom2), [dx, small_started])
    gathered = small_gather.wait(updated["w_in"][0])["small"]
    updated = {n: tuple(jnp.swapaxes(t, 1, 2) for t in u) if n in _COLUMN_SHARDED else u for n, u in updated.items()}
    g_small = _small_reduce(gathered, lb_logits, name="small_grads")

    per_layer = [(A_HEADS * HEAD_DIM,), (HEAD_DIM,), (_PACK_LANES,), (D_MODEL,), (D_MODEL,), (D_MODEL,), (D_MODEL,), (D_FF,),
                 (3, D_FF)]
    names = ("lb_logits", "a_norm_w", "c_sinks", "ln1_g", "ln1_b", "ln2_g", "ln2_b", "conv_b", "conv_w")
    grads = {n: [] for n in names}
    for l in range(DEPTH):
        for n, t in zip(names, _unpack(g_small[l * _LAYER_ROWS:(l + 1) * _LAYER_ROWS], per_layer)):
            grads[n].append(t)
    grads = {n: jnp.stack(t) for n, t in grads.items()}
    grads["c_sinks"] = grads["c_sinks"][:, :C_HEADS]
    grads["conv_w"] = lax.dynamic_slice_in_dim(grads["conv_w"], me * SHARD_COLS, SHARD_COLS, axis=2)
    shapes = [grads[n].shape for n in names]
    rows = -(-sum(int(np.prod(s)) for s in shapes) // (8 * _PACK_LANES)) * 8
    d_s, m_s, v_s = _adamw_small(_pack([grads[n] for n in names], rows), _pack([weights[n] for n in names], rows),
                                 _pack([mom1[n] for n in names], rows), _pack([mom2[n] for n in names], rows),
                                 name="adamw_small")
    delta = dict(zip(names, _unpack(d_s, shapes)))
    new_m = dict(zip(names, _unpack(m_s, shapes)))
    new_v = dict(zip(names, _unpack(v_s, shapes)))
    for n in _BIG:
        grads[n], delta[n], new_m[n], new_v[n] = updated[n]

    order = ("w_in", "lb_logits", "a_norm_w", "c_sinks", "w_out", "ln1_g", "ln1_b", "w_gate", "w_up", "conv_w", "conv_b",
             "w_down", "ln2_g", "ln2_b")
    return (loss, dx[None], *[grads[n] for n in order], *[delta[n] for n in order], *[new_m[n] for n in order],
            *[new_v[n] for n in order])
```

```python
import functools

import jax
import jax.numpy as jnp
import numpy as np
from jax import lax
from jax.experimental import pallas as pl
from jax.experimental.pallas import tpu as pltpu

D_MODEL = 2048
SEQ = 2048
DEPTH = 2
HEAD_DIM = 128
A_HEADS = 4
B_HEADS = 6
C_HEADS = 6
C_KV_HEADS = 2
A_CHUNK = 16
DILATIONS = (1, 4, 16)
BLOCK = 128
ROPE_THETA = 500000.0
ROPE_DIM = 32
D_FF = 5632
IN_WIDTH = 5632
LN_EPS = 1e-5
ALPHA = (2 * DEPTH) ** 0.25
N_DEV = 8
SHARD_COLS = IN_WIDTH // N_DEV

ADAM_LR = 0.001
ADAM_B1 = 0.9
ADAM_B2 = 0.999
ADAM_EPS = 1e-08
ADAM_WD = 0.01
ADAM_STEP = 10

A_COLS = 16
QKV_COLS = 28
QB0, KB0, VB0, QC0, KC0, VC0 = 0, 6, 12, 18, 24, 26

_MXU_DTYPE = jnp.bfloat16
_GRAD_DTYPE = jnp.bfloat16
_ACT_DTYPE = jnp.bfloat16
_NEG = -1e30
_VMEM_LIMIT = 56 * 2 ** 20

_F32 = jnp.float32


def _sigmoid(x):
    return 0.5 * jnp.tanh(0.5 * x) + 0.5


def _cparams(**kw):
    return pltpu.CompilerParams(vmem_limit_bytes=_VMEM_LIMIT, **kw)


_TILE_MIX = dict(tm=1024, tn=1024)
_TILE_WIDE_K = dict(tm=1024, tn=512)
_TILE_SUM2 = dict(tm=512, tn=512)
_TILE_WIDE_N = dict(tm=1024, tn=1408)
_TILE_WIDE_M = dict(tm=1408, tn=1024)


def _mm(a, b, *, ta=False, tb=False, tm, tn, out_dtype=_F32, add=None, add_scale=1.0, after=(), name):
    K = a.shape[0] if ta else a.shape[1]
    M = a.shape[1] if ta else a.shape[0]
    N = b.shape[0] if tb else b.shape[1]
    assert (b.shape[1] if tb else b.shape[0]) == K and M % tm == 0 and N % tn == 0
    dn = (((0 if ta else 1,), (1 if tb else 0,)), ((), ()))

    def body(*refs):
        a_ref, b_ref = refs[:2]
        o_ref = refs[-1]
        r = lax.dot_general(a_ref[...], b_ref[...], dn, preferred_element_type=_F32)
        if add is not None:
            r = r + add_scale * refs[2][...]
        o_ref[...] = r.astype(o_ref.dtype)

    a_spec = pl.BlockSpec((K, tm), lambda i, j: (0, i)) if ta else pl.BlockSpec((tm, K), lambda i, j: (i, 0))
    b_spec = pl.BlockSpec((tn, K), lambda i, j: (j, 0)) if tb else pl.BlockSpec((K, tn), lambda i, j: (0, j))
    o_spec = pl.BlockSpec((tm, tn), lambda i, j: (i, j))
    in_specs = [a_spec, b_spec] + ([o_spec] if add is not None else []) + [pl.BlockSpec(memory_space=pl.ANY)] * len(after)
    args = (a, b) + ((add,) if add is not None else ()) + tuple(after)
    return pl.pallas_call(
        body, grid=(M // tm, N // tn), in_specs=in_specs, out_specs=o_spec,
        out_shape=jax.ShapeDtypeStruct((M, N), out_dtype), name=name,
        compiler_params=_cparams(dimension_semantics=("parallel", "parallel")),
    )(*args)


def _mm_sum2(a1, b1, a2, b2, *, tm, tn, name):
    M, K = a1.shape
    N = b1.shape[1]
    assert a2.shape == a1.shape and b1.shape == b2.shape == (K, N) and M % tm == 0 and N % tn == 0

    def body(a1_ref, b1_ref, a2_ref, b2_ref, o_ref):
        o_ref[...] = (jnp.dot(a1_ref[...], b1_ref[...], preferred_element_type=_F32)
                      + jnp.dot(a2_ref[...], b2_ref[...], preferred_element_type=_F32))

    a_spec = pl.BlockSpec((tm, K), lambda i, j: (i, 0))
    b_spec = pl.BlockSpec((K, tn), lambda i, j: (0, j))
    return pl.pallas_call(
        body, grid=(M // tm, N // tn), in_specs=[a_spec, b_spec, a_spec, b_spec],
        out_specs=pl.BlockSpec((tm, tn), lambda i, j: (i, j)), out_shape=jax.ShapeDtypeStruct((M, N), _F32), name=name,
        compiler_params=_cparams(dimension_semantics=("parallel", "parallel")),
    )(a1, b1, a2, b2)


def _cast_layer(w, layer, *, after=(), name):
    _, R, C = w.shape
    tr = max(t for t in range(16, R + 1, 16) if R % t == 0 and t * C <= 512 * 1024)

    def body(w_ref, *rest):
        o_ref = rest[-1]
        o_ref[...] = w_ref[...].astype(o_ref.dtype)

    return pl.pallas_call(
        body, grid=(R // tr,),
        in_specs=[pl.BlockSpec((None, tr, C), lambda i: (layer, i, 0))] + [pl.BlockSpec(memory_space=pl.ANY)] * len(after),
        out_specs=pl.BlockSpec((tr, C), lambda i: (i, 0)), out_shape=jax.ShapeDtypeStruct((R, C), _MXU_DTYPE), name=name,
        compiler_params=_cparams(dimension_semantics=("parallel",)),
    )(w, *after)


def _concat_cols(pieces, *, name):
    tm = 512
    widths = [p.shape[1] for p in pieces]
    offs = np.cumsum([0] + widths)

    def body(*refs):
        o_ref = refs[-1]
        for p_ref, off, w in zip(refs[:-1], offs, widths):
            o_ref[:, off:off + w] = p_ref[...].astype(o_ref.dtype)

    return pl.pallas_call(
        body, grid=(SEQ // tm,), in_specs=[pl.BlockSpec((tm, w), lambda i: (i, 0)) for w in widths],
        out_specs=pl.BlockSpec((tm, int(offs[-1])), lambda i: (i, 0)),
        out_shape=jax.ShapeDtypeStruct((SEQ, int(offs[-1])), _MXU_DTYPE), name=name,
        compiler_params=_cparams(dimension_semantics=("parallel",)),
    )(*pieces)


def _ln_fwd(x, y, g, b, *, name):
    tm = 256

    def body(x_ref, y_ref, g_ref, b_ref, z_ref, o_ref, ob_ref):
        z = ALPHA * x_ref[...] + y_ref[...]
        mu = jnp.mean(z, axis=-1, keepdims=True)
        zc = z - mu
        var = jnp.mean(zc * zc, axis=-1, keepdims=True)
        o = zc * lax.rsqrt(var + LN_EPS) * g_ref[...] + b_ref[...]
        z_ref[...] = z
        o_ref[...] = o
        ob_ref[...] = o.astype(ob_ref.dtype)

    row = pl.BlockSpec((tm, D_MODEL), lambda i: (i, 0))
    vec = pl.BlockSpec((1, D_MODEL), lambda i: (0, 0))
    return pl.pallas_call(
        body, grid=(SEQ // tm,), in_specs=[row, row, vec, vec], out_specs=[row, row, row],
        out_shape=[jax.ShapeDtypeStruct((SEQ, D_MODEL), _F32), jax.ShapeDtypeStruct((SEQ, D_MODEL), _F32),
                   jax.ShapeDtypeStruct((SEQ, D_MODEL), _MXU_DTYPE)],
        name=name, compiler_params=_cparams(dimension_semantics=("parallel",)),
    )(x, y, g.reshape(1, D_MODEL), b.reshape(1, D_MODEL))


def _ln_bwd(z, d_a, d_res, g, *, name):
    tm = 256

    def body(*refs):
        if d_res is None:
            z_ref, da_ref, g_ref, dz_ref, dzb_ref, dg_ref, db_ref = refs
        else:
            z_ref, da_ref, dr_ref, g_ref, dz_ref, dzb_ref, dg_ref, db_ref = refs

        @pl.when(pl.program_id(0) == 0)
        def _():
            dg_ref[...] = jnp.zeros_like(dg_ref)
            db_ref[...] = jnp.zeros_like(db_ref)

        dout = da_ref[...]
        if d_res is not None:
            dout = dout + ALPHA * dr_ref[...]
        z = z_ref[...]
        mu = jnp.mean(z, axis=-1, keepdims=True)
        zc = z - mu
        var = jnp.mean(zc * zc, axis=-1, keepdims=True)
        rstd = lax.rsqrt(var + LN_EPS)
        xh = zc * rstd
        dxh = dout * g_ref[...]
        m1 = jnp.mean(dxh, axis=-1, keepdims=True)
        m2 = jnp.mean(dxh * xh, axis=-1, keepdims=True)
        dz = rstd * (dxh - m1 - xh * m2)
        dz_ref[...] = dz
        dzb_ref[...] = dz.astype(dzb_ref.dtype)
        dg_ref[0:1, :] += jnp.sum(dout * xh, axis=0, keepdims=True)
        db_ref[0:1, :] += jnp.sum(dout, axis=0, keepdims=True)

    row = pl.BlockSpec((tm, D_MODEL), lambda i: (i, 0))
    vec = pl.BlockSpec((1, D_MODEL), lambda i: (0, 0))
    acc = pl.BlockSpec((8, D_MODEL), lambda i: (0, 0))
    ins = [z, d_a] + ([d_res] if d_res is not None else []) + [g.reshape(1, D_MODEL)]
    in_specs = [row, row] + ([row] if d_res is not None else []) + [vec]
    dz, dzb, dg, db = pl.pallas_call(
        body, grid=(SEQ // tm,), in_specs=in_specs, out_specs=[row, row, acc, acc],
        out_shape=[jax.ShapeDtypeStruct((SEQ, D_MODEL), _F32), jax.ShapeDtypeStruct((SEQ, D_MODEL), _MXU_DTYPE),
                   jax.ShapeDtypeStruct((8, D_MODEL), _F32), jax.ShapeDtypeStruct((8, D_MODEL), _F32)],
        name=name, compiler_params=_cparams(dimension_semantics=("arbitrary",)),
    )(*ins)
    return dz, dzb, dg[0], db[0]


def _ln_loss_bwd(x, y, g, b, target, *, name):
    tm = 256

    def body(x_ref, y_ref, g_ref, b_ref, t_ref, dz_ref, dzb_ref, dg_ref, db_ref, l_ref):
        @pl.when(pl.program_id(0) == 0)
        def _():
            dg_ref[...] = jnp.zeros_like(dg_ref)
            db_ref[...] = jnp.zeros_like(db_ref)
            l_ref[...] = jnp.zeros_like(l_ref)

        z = ALPHA * x_ref[...] + y_ref[...]
        mu = jnp.mean(z, axis=-1, keepdims=True)
        zc = z - mu
        var = jnp.mean(zc * zc, axis=-1, keepdims=True)
        rstd = lax.rsqrt(var + LN_EPS)
        xh = zc * rstd
        e = xh * g_ref[...] + b_ref[...] - t_ref[...]
        l_ref[...] += (0.5 / D_MODEL) * jnp.sum(e * e)
        dout = e * (1.0 / D_MODEL)
        dxh = dout * g_ref[...]
        m1 = jnp.mean(dxh, axis=-1, keepdims=True)
        m2 = jnp.mean(dxh * xh, axis=-1, keepdims=True)
        dz = rstd * (dxh - m1 - xh * m2)
        dz_ref[...] = dz
        dzb_ref[...] = dz.astype(dzb_ref.dtype)
        dg_ref[0:1, :] += jnp.sum(dout * xh, axis=0, keepdims=True)
        db_ref[0:1, :] += jnp.sum(dout, axis=0, keepdims=True)

    row = pl.BlockSpec((tm, D_MODEL), lambda i: (i, 0))
    vec = pl.BlockSpec((1, D_MODEL), lambda i: (0, 0))
    acc = pl.BlockSpec((8, D_MODEL), lambda i: (0, 0))
    dz, dzb, dg, db, part = pl.pallas_call(
        body, grid=(SEQ // tm,), in_specs=[row, row, vec, vec, row],
        out_specs=[row, row, acc, acc, pl.BlockSpec((8, 128), lambda i: (0, 0))],
        out_shape=[jax.ShapeDtypeStruct((SEQ, D_MODEL), _F32), jax.ShapeDtypeStruct((SEQ, D_MODEL), _MXU_DTYPE),
                   jax.ShapeDtypeStruct((8, D_MODEL), _F32), jax.ShapeDtypeStruct((8, D_MODEL), _F32),
                   jax.ShapeDtypeStruct((8, 128), _F32)],
        name=name, compiler_params=_cparams(dimension_semantics=("arbitrary",)),
    )(x, y, g.reshape(1, D_MODEL), b.reshape(1, D_MODEL), target)
    return part[0, 0], dz, dzb, dg[0], db[0]


_CONV_TN = 256


def _shift_down(v, k, rows):
    return jnp.where(rows >= k, pltpu.roll(v, k, axis=0), 0.0)


def _shift_up(v, k, rows):
    return jnp.where(rows < SEQ - k, pltpu.roll(v, SEQ - k, axis=0), 0.0)


def _conv_gate_fwd(g, u, conv_w, conv_b, *, name):
    def body(g_ref, u_ref, w_ref, b_ref, h_ref):
        gv = g_ref[...].astype(_F32)
        rows = lax.broadcasted_iota(jnp.int32, gv.shape, 0)
        w = w_ref[...]
        gc = b_ref[...] + w[2:3, :] * gv + w[1:2, :] * _shift_down(gv, 1, rows) + w[0:1, :] * _shift_down(gv, 2, rows)
        h_ref[...] = (gc * _sigmoid(gc) * u_ref[...].astype(_F32)).astype(h_ref.dtype)

    col = pl.BlockSpec((SEQ, _CONV_TN), lambda j: (0, j))
    return pl.pallas_call(
        body, grid=(D_FF // _CONV_TN,),
        in_specs=[col, col, pl.BlockSpec((3, _CONV_TN), lambda j: (0, j)), pl.BlockSpec((1, _CONV_TN), lambda j: (0, j))],
        out_specs=col, out_shape=jax.ShapeDtypeStruct((SEQ, D_FF), _MXU_DTYPE), name=name,
        compiler_params=_cparams(dimension_semantics=("parallel",)),
    )(g, u, conv_w, conv_b.reshape(1, D_FF))


def _conv_gate_bwd(dh, g, u, conv_w, conv_b, *, name):
    def body(dh_ref, g_ref, u_ref, w_ref, b_ref, dg_ref, du_ref, dw_ref, db_ref):
        gv = g_ref[...].astype(_F32)
        rows = lax.broadcasted_iota(jnp.int32, gv.shape, 0)
        w = w_ref[...]
        g1 = _shift_down(gv, 1, rows)
        g2 = _shift_down(gv, 2, rows)
        gc = b_ref[...] + w[2:3, :] * gv + w[1:2, :] * g1 + w[0:1, :] * g2
        sg = _sigmoid(gc)
        dh = dh_ref[...].astype(_F32)
        du_ref[...] = (dh * (gc * sg)).astype(du_ref.dtype)
        dgc = dh * u_ref[...].astype(_F32) * (sg * (1.0 + gc * (1.0 - sg)))
        dg = w[2:3, :] * dgc + w[1:2, :] * _shift_up(dgc, 1, rows) + w[0:1, :] * _shift_up(dgc, 2, rows)
        dg_ref[...] = dg.astype(dg_ref.dtype)
        dw_ref[0:1, :] = jnp.sum(dgc * g2, axis=0, keepdims=True)
        dw_ref[1:2, :] = jnp.sum(dgc * g1, axis=0, keepdims=True)
        dw_ref[2:3, :] = jnp.sum(dgc * gv, axis=0, keepdims=True)
        db_ref[...] = jnp.sum(dgc, axis=0, keepdims=True)

    col = pl.BlockSpec((SEQ, _CONV_TN), lambda j: (0, j))
    w3 = pl.BlockSpec((3, _CONV_TN), lambda j: (0, j))
    w1 = pl.BlockSpec((1, _CONV_TN), lambda j: (0, j))
    dg, du, dw, db = pl.pallas_call(
        body, grid=(D_FF // _CONV_TN,), in_specs=[col, col, col, w3, w1], out_specs=[col, col, w3, w1],
        out_shape=[jax.ShapeDtypeStruct((SEQ, D_FF), _MXU_DTYPE), jax.ShapeDtypeStruct((SEQ, D_FF), _MXU_DTYPE),
                   jax.ShapeDtypeStruct((3, D_FF), _F32), jax.ShapeDtypeStruct((1, D_FF), _F32)],
        name=name, compiler_params=_cparams(dimension_semantics=("parallel",)),
    )(dh, g, u, conv_w, conv_b.reshape(1, D_FF))
    return dg, du, dw, db[0]


def _rope_tables():
    half = ROPE_DIM // 2
    inv = ROPE_THETA ** (-jnp.arange(0, ROPE_DIM, 2, dtype=_F32) / ROPE_DIM)
    ang = jnp.arange(SEQ, dtype=_F32)[:, None] * inv[None, :]
    cos, sin = jnp.cos(ang), jnp.sin(ang)
    rest = HEAD_DIM - ROPE_DIM
    c = jnp.concatenate([cos, cos, jnp.ones((SEQ, rest), _F32)], axis=1)
    s1 = jnp.concatenate([-sin, jnp.zeros((SEQ, HEAD_DIM - half), _F32)], axis=1)
    s2 = jnp.concatenate([jnp.zeros((SEQ, half), _F32), sin, jnp.zeros((SEQ, rest), _F32)], axis=1)
    return c, s1, s2


def _rope_apply(x, c, s1, s2):
    return x * c + pltpu.roll(x, HEAD_DIM - ROPE_DIM // 2, axis=1) * s1 + pltpu.roll(x, ROPE_DIM // 2, axis=1) * s2


def _rope_transpose(d, c, s1, s2):
    half = ROPE_DIM // 2
    return d * c + pltpu.roll(d * s1, half, axis=1) + pltpu.roll(d * s2, HEAD_DIM - half, axis=1)


_NT = (((1,), (1,)), ((), ()))
_TN = (((0,), (0,)), ((), ()))
_SCALE = HEAD_DIM ** -0.5


def _band_scores(q, k2, n, lag_off):
    s = lax.dot_general(q, k2, _NT, preferred_element_type=_F32) * _SCALE
    row = lax.broadcasted_iota(jnp.int32, (BLOCK, 2 * BLOCK), 0)
    col = lax.broadcasted_iota(jnp.int32, (BLOCK, 2 * BLOCK), 1)
    front = (col >= row + lag_off) & (col < BLOCK) & (n > 0)
    own = (col >= BLOCK) & (col <= row + BLOCK)
    return jnp.where(front | own, s, _NEG)


_BAND_STEPS = SEQ // BLOCK


def _rows(start, d):
    if d == 1:
        return pl.ds(pl.multiple_of(start, BLOCK), BLOCK)
    return pl.ds(start, BLOCK, stride=d)


def _band_block(it, d):
    r, n = it % d, it // d
    span = BLOCK * d
    return n, _rows(r + n * span, d), _rows(r + jnp.maximum(n - 1, 0) * span, d)


def _band_fwd(proj, tabs, *, kv_heads, q_per_kv, q0, k0, v0, dilations, lag_off, sink, name):
    heads = kv_heads * q_per_kv

    def body(*refs):
        q_refs = refs[:q_per_kv]
        k_ref, v_ref, c_ref, s1_ref, s2_ref = refs[q_per_kv:q_per_kv + 5]
        rest = refs[q_per_kv + 5:]
        if sink is not None:
            sk_ref, rest = rest[0], rest[1:]
        o_ref, lse_ref, qs, ks, m_s, l_s, acc_s = rest
        c, s1, s2 = c_ref[...], s1_ref[...], s2_ref[...]
        ks[...] = _rope_apply(k_ref[...], c, s1, s2)
        for i in range(q_per_kv):
            qs[...] = _rope_apply(q_refs[i][...], c, s1, s2)
            for pi, d in enumerate(dilations):
                def step(it, carry, d=d, first=(pi == 0)):
                    n, cur, prev = _band_block(it, d)
                    q = qs[cur, :].astype(_MXU_DTYPE)
                    k2 = jnp.concatenate([ks[prev, :], ks[cur, :]], axis=0).astype(_MXU_DTYPE)
                    v2 = jnp.concatenate([v_ref[prev, :], v_ref[cur, :]], axis=0).astype(_MXU_DTYPE)
                    s = _band_scores(q, k2, n, lag_off)
                    m_b = jnp.max(s, axis=1, keepdims=True)
                    m_new = m_b if first else jnp.maximum(m_b, m_s[cur, :][:, 0:1])
                    p = jnp.exp(s - m_new)
                    l_new = jnp.sum(p, axis=1, keepdims=True)
                    acc = jnp.dot(p.astype(_MXU_DTYPE), v2, preferred_element_type=_F32)
                    if not first:
                        a = jnp.exp(m_s[cur, :][:, 0:1] - m_new)
                        l_new = l_new + a * l_s[cur, :][:, 0:1]
                        acc = acc + a * acc_s[cur, :]
                    m_s[cur, :] = jnp.broadcast_to(m_new, (BLOCK, HEAD_DIM))
                    l_s[cur, :] = jnp.broadcast_to(l_new, (BLOCK, HEAD_DIM))
                    acc_s[cur, :] = acc
                    return carry

                lax.fori_loop(0, _BAND_STEPS, step, 0, unroll=16)
            m, den = m_s[...], l_s[...]
            if sink is not None:
                sk = sk_ref[i]
                m_f = jnp.maximum(m, sk)
                a = jnp.exp(m - m_f)
                den = den * a + jnp.exp(sk - m_f)
                o = acc_s[...] * a / den
                m = m_f
            else:
                o = acc_s[...] / den
            o_ref[:, i * HEAD_DIM:(i + 1) * HEAD_DIM] = o
            lse_ref[:, i * HEAD_DIM:(i + 1) * HEAD_DIM] = m + jnp.log(den)

    col = (SEQ, HEAD_DIM)
    in_specs = [pl.BlockSpec(col, functools.partial(lambda g, i: (0, A_COLS + q0 + g * q_per_kv + i), i=i)) for i in range(q_per_kv)]
    in_specs += [pl.BlockSpec(col, lambda g: (0, A_COLS + k0 + g)), pl.BlockSpec(col, lambda g: (0, A_COLS + v0 + g))]
    in_specs += [pl.BlockSpec(col, lambda g: (0, 0))] * 3
    args = [proj] * (q_per_kv + 2) + list(tabs)
    if sink is not None:
        in_specs.append(pl.BlockSpec((q_per_kv, 1, HEAD_DIM), lambda g: (g, 0, 0)))
        args.append(jnp.broadcast_to(sink.reshape(heads, 1, 1), (heads, 1, HEAD_DIM)))
    o_spec = pl.BlockSpec((SEQ, q_per_kv * HEAD_DIM), lambda g: (0, g))
    shape = jax.ShapeDtypeStruct((SEQ, heads * HEAD_DIM), _F32)
    return pl.pallas_call(
        body, grid=(kv_heads,), in_specs=in_specs, out_specs=[o_spec, o_spec], out_shape=[shape, shape],
        scratch_shapes=[pltpu.VMEM(col, _F32)] * 5, name=name,
        compiler_params=_cparams(dimension_semantics=("parallel",)),
    )(*args)


def _band_bwd(proj, tabs, dmixed, o, lse, *, kv_heads, q_per_kv, q0, k0, v0, do0, dilations, lag_off, sink, after=(), name):
    heads = kv_heads * q_per_kv

    def body(*refs):
        q_refs = refs[:q_per_kv]
        k_ref, v_ref, c_ref, s1_ref, s2_ref = refs[q_per_kv:q_per_kv + 5]
        do_refs = refs[q_per_kv + 5:2 * q_per_kv + 5]
        o_ref, lse_ref = refs[2 * q_per_kv + 5:2 * q_per_kv + 7]
        rest = refs[2 * q_per_kv + 7:]
        if sink is not None:
            sk_ref, rest = rest[0], rest[1:]
            dq_ref, dk_ref, dv_ref, dsk_ref, qs, ks, dq_s, dk_s, dv_s = rest[len(after):]
        else:
            dq_ref, dk_ref, dv_ref, qs, ks, dq_s, dk_s, dv_s = rest[len(after):]
        c, s1, s2 = c_ref[...], s1_ref[...], s2_ref[...]
        ks[...] = _rope_apply(k_ref[...], c, s1, s2)
        dk_s[...] = jnp.zeros_like(dk_s)
        dv_s[...] = jnp.zeros_like(dv_s)
        for i in range(q_per_kv):
            hs = slice(i * HEAD_DIM, (i + 1) * HEAD_DIM)
            qs[...] = _rope_apply(q_refs[i][...], c, s1, s2)
            dq_s[...] = jnp.zeros_like(dq_s)
            do_ref = do_refs[i]
            for d in dilations:
                def step(it, carry, d=d, do_ref=do_ref, hs=hs):
                    n, cur, prev = _band_block(it, d)
                    q = qs[cur, :].astype(_MXU_DTYPE)
                    k2 = jnp.concatenate([ks[prev, :], ks[cur, :]], axis=0).astype(_MXU_DTYPE)
                    v2 = jnp.concatenate([v_ref[prev, :], v_ref[cur, :]], axis=0).astype(_MXU_DTYPE)
                    do = do_ref[cur, :]
                    delta = jnp.sum(do * o_ref[cur, hs], axis=1, keepdims=True)
                    lse_c = lse_ref[cur, hs][:, 0:1]
                    p = jnp.exp(_band_scores(q, k2, n, lag_off) - lse_c)
                    dob = do.astype(_MXU_DTYPE)
                    ds = (p * (lax.dot_general(dob, v2, _NT, preferred_element_type=_F32) - delta) * _SCALE).astype(_MXU_DTYPE)
                    dq_s[cur, :] += jnp.dot(ds, k2, preferred_element_type=_F32)
                    dk2 = lax.dot_general(ds, q, _TN, preferred_element_type=_F32)
                    dv2 = lax.dot_general(p.astype(_MXU_DTYPE), dob, _TN, preferred_element_type=_F32)
                    dk_s[prev, :] += dk2[:BLOCK]
                    dv_s[prev, :] += dv2[:BLOCK]
                    dk_s[cur, :] += dk2[BLOCK:]
                    dv_s[cur, :] += dv2[BLOCK:]
                    return carry

                lax.fori_loop(0, _BAND_STEPS, step, 0, unroll=16)
            dq_ref[:, hs] = _rope_transpose(dq_s[...], c, s1, s2).astype(dq_ref.dtype)
            if sink is not None:
                delta = jnp.sum(do_ref[...] * o_ref[:, hs], axis=1, keepdims=True)
                w_sink = jnp.exp(sk_ref[i] - lse_ref[:, hs])
                dsk_ref[i] = jnp.broadcast_to(jnp.sum(-delta * w_sink[:, 0:1]), (8, HEAD_DIM))
        dk_ref[...] = _rope_transpose(dk_s[...], c, s1, s2).astype(dk_ref.dtype)
        dv_ref[...] = dv_s[...].astype(dv_ref.dtype)

    col = (SEQ, HEAD_DIM)
    in_specs = [pl.BlockSpec(col, functools.partial(lambda g, i: (0, A_COLS + q0 + g * q_per_kv + i), i=i)) for i in range(q_per_kv)]
    in_specs += [pl.BlockSpec(col, lambda g: (0, A_COLS + k0 + g)), pl.BlockSpec(col, lambda g: (0, A_COLS + v0 + g))]
    in_specs += [pl.BlockSpec(col, lambda g: (0, 0))] * 3
    in_specs += [pl.BlockSpec(col, functools.partial(lambda g, i: (0, do0 + g * q_per_kv + i), i=i)) for i in range(q_per_kv)]
    wide = pl.BlockSpec((SEQ, q_per_kv * HEAD_DIM), lambda g: (0, g))
    in_specs += [wide, wide]
    args = [proj] * (q_per_kv + 2) + list(tabs) + [dmixed] * q_per_kv + [o, lse]
    out_specs = [wide, pl.BlockSpec(col, lambda g: (0, g)), pl.BlockSpec(col, lambda g: (0, g))]
    out_shape = [jax.ShapeDtypeStruct((SEQ, heads * HEAD_DIM), _MXU_DTYPE), jax.ShapeDtypeStruct((SEQ, kv_heads * HEAD_DIM), _MXU_DTYPE),
                 jax.ShapeDtypeStruct((SEQ, kv_heads * HEAD_DIM), _MXU_DTYPE)]
    if sink is not None:
        in_specs.append(pl.BlockSpec((q_per_kv, 1, HEAD_DIM), lambda g: (g, 0, 0)))
        args.append(jnp.broadcast_to(sink.reshape(heads, 1, 1), (heads, 1, HEAD_DIM)))
        out_specs.append(pl.BlockSpec((q_per_kv, 8, HEAD_DIM), lambda g: (g, 0, 0)))
        out_shape.append(jax.ShapeDtypeStruct((heads, 8, HEAD_DIM), _F32))
    in_specs += [pl.BlockSpec(memory_space=pl.ANY)] * len(after)
    args += list(after)
    res = pl.pallas_call(
        body, grid=(kv_heads,), in_specs=in_specs, out_specs=out_specs, out_shape=out_shape,
        scratch_shapes=[pltpu.VMEM(col, _F32)] * 5, name=name,
        compiler_params=_cparams(dimension_semantics=("parallel",)),
    )(*args)
    if sink is not None:
        return res[0], res[1], res[2], res[3][:, 0, 0]
    return res


_DILATED = dict(kv_heads=B_HEADS, q_per_kv=1, q0=QB0, k0=KB0, v0=VB0, dilations=DILATIONS, lag_off=0, sink=None)
_SWA = dict(kv_heads=C_KV_HEADS, q_per_kv=C_HEADS // C_KV_HEADS, q0=QC0, k0=KC0, v0=VC0, dilations=(1,), lag_off=1)


_HG_TILE = 128
_HG_CHUNKS = _HG_TILE // A_CHUNK
_HG_TILES = SEQ // _HG_TILE
_HI = lax.Precision.HIGHEST


def _chunk_tri():
    i = np.arange(_HG_TILE)
    return jnp.asarray(((i[:, None] // A_CHUNK == i[None, :] // A_CHUNK) & (i[None, :] <= i[:, None])).astype(np.float32))


def _layer_lb(lb_ref, layer):
    if layer == 0:
        return jnp.zeros((1, HEAD_DIM), _F32)
    lg = lb_ref[...]
    m = jnp.max(lg, axis=0, keepdims=True)
    e = jnp.exp(lg - m)
    return e[1:2, :] / jnp.sum(e, axis=0, keepdims=True)


def _hgrn_gates(q, fr, lb):
    sgq = _sigmoid(q)
    sg = _sigmoid(fr)
    f = lb + (1.0 - lb) * sg
    return sgq, q * sgq, sg, f, 1.0 - f


def _hgrn_fwd(proj, lb_logits, norm_w, layer, *, name):
    tri = _chunk_tri()

    def body(q_ref, f_ref, i_ref, g_ref, lb_ref, nw_ref, tri_ref, o_ref, raw_ref, st_ref, state):
        @pl.when(pl.program_id(1) == 0)
        def _():
            state[...] = jnp.zeros_like(state)

        lb = _layer_lb(lb_ref, layer)
        _, qs, _, f, k = _hgrn_gates(q_ref[...], f_ref[...], lb)
        v = i_ref[...]
        b = jnp.dot(tri_ref[...], jnp.log(f), precision=_HI, preferred_element_type=_F32)
        eb = jnp.exp(b)
        ridx = lax.broadcasted_iota(jnp.int32, (A_CHUNK, HEAD_DIM), 0)
        outs = []
        st = state[...]
        for c in range(_HG_CHUNKS):
            sl = slice(c * A_CHUNK, (c + 1) * A_CHUNK)
            bc, qc, kc, vc = b[sl], qs[sl], k[sl], v[sl]
            bl = bc[A_CHUNK - 1:A_CHUNK]
            st_ref[0, c] = st
            o_c = lax.dot_general((qc * eb[sl]).astype(_MXU_DTYPE), st.astype(_MXU_DTYPE), _NT, preferred_element_type=_F32)
            rows = []
            for i in range(A_CHUNK):
                di = jnp.exp(jnp.where(ridx <= i, bc[i:i + 1] - bc, _NEG))
                a = jnp.sum(qc[i:i + 1] * kc * di, axis=1, keepdims=True)
                rows.append(jnp.sum(a * vc, axis=0, keepdims=True))
            outs.append(o_c + jnp.concatenate(rows, axis=0))
            kt = (kc * jnp.exp(bl - bc)).astype(_MXU_DTYPE)
            st = st * jnp.exp(bl) + lax.dot_general(vc.astype(_MXU_DTYPE), kt, _TN, preferred_element_type=_F32)
        state[...] = st
        o = jnp.concatenate(outs, axis=0)
        raw_ref[...] = o
        r = lax.rsqrt(jnp.mean(o * o, axis=-1, keepdims=True) + LN_EPS)
        g = g_ref[...]
        o_ref[...] = o * r * nw_ref[...] * (g * _sigmoid(g))

    blk = (_HG_TILE, HEAD_DIM)

    def col(base):
        return pl.BlockSpec(blk, lambda h, t: (t, base + h))

    o_spec = pl.BlockSpec(blk, lambda h, t: (t, h))
    o_shape = jax.ShapeDtypeStruct((SEQ, A_HEADS * HEAD_DIM), _F32)
    return pl.pallas_call(
        body, grid=(A_HEADS, _HG_TILES),
        in_specs=[col(0), col(4), col(8), col(12), pl.BlockSpec((DEPTH, HEAD_DIM), lambda h, t: (0, h)),
                  pl.BlockSpec((1, HEAD_DIM), lambda h, t: (0, 0)), pl.BlockSpec((_HG_TILE, _HG_TILE), lambda h, t: (0, 0))],
        out_specs=[o_spec, o_spec, pl.BlockSpec((1, _HG_CHUNKS, HEAD_DIM, HEAD_DIM), lambda h, t: (h, t, 0, 0))],
        out_shape=[o_shape, o_shape, jax.ShapeDtypeStruct((A_HEADS, SEQ // A_CHUNK, HEAD_DIM, HEAD_DIM), _F32)],
        scratch_shapes=[pltpu.VMEM((HEAD_DIM, HEAD_DIM), _F32)], name=name,
        compiler_params=_cparams(dimension_semantics=("parallel", "arbitrary")),
    )(proj, proj, proj, proj, lb_logits, norm_w.reshape(1, HEAD_DIM), tri)


def _hgrn_bwd(proj, lb_logits, norm_w, raw, states, dmixed, layer, *, name):
    tri = _chunk_tri()
    triu = tri.T

    def body(q_ref, f_ref, i_ref, g_ref, lb_ref, nw_ref, tri_ref, triu_ref, raw_ref, do_ref, st_ref,
             dq_ref, df_ref, di_ref, dg_ref, dnw_ref, dlb_ref, dstate):
        @pl.when(pl.program_id(1) == 0)
        def _():
            dstate[...] = jnp.zeros_like(dstate)
            dlb_ref[...] = jnp.zeros_like(dlb_ref)

        @pl.when((pl.program_id(0) == 0) & (pl.program_id(1) == 0))
        def _():
            dnw_ref[...] = jnp.zeros_like(dnw_ref)

        lb = _layer_lb(lb_ref, layer)
        q = q_ref[...]
        sgq, qs, sg, f, k = _hgrn_gates(q, f_ref[...], lb)
        v = i_ref[...]
        b = jnp.dot(tri_ref[...], jnp.log(f), precision=_HI, preferred_element_type=_F32)
        eb = jnp.exp(b)
        g = g_ref[...]
        nw = nw_ref[...]
        o = raw_ref[...]
        dout = do_ref[...]
        sgg = _sigmoid(g)
        r = lax.rsqrt(jnp.mean(o * o, axis=-1, keepdims=True) + LN_EPS)
        dg_ref[...] = (dout * (o * r * nw) * (sgg * (1.0 + g * (1.0 - sgg)))).astype(dg_ref.dtype)
        don = dout * (g * sgg)
        dnw_ref[0:1, :] += jnp.sum(don * o * r, axis=0, keepdims=True)
        dy = don * nw
        do_raw = r * dy - o * (r * r * r) * jnp.mean(o * dy, axis=-1, keepdims=True)

        ridx = lax.broadcasted_iota(jnp.int32, (A_CHUNK, HEAD_DIM), 0)
        dqs_t, dk_t, db_t, dv_t = [None] * _HG_CHUNKS, [None] * _HG_CHUNKS, [None] * _HG_CHUNKS, [None] * _HG_CHUNKS
        dst = dstate[...]
        for c in reversed(range(_HG_CHUNKS)):
            sl = slice(c * A_CHUNK, (c + 1) * A_CHUNK)
            bc, qc, kc, vc, doc = b[sl], qs[sl], k[sl], v[sl], do_raw[sl]
            bl = bc[A_CHUNK - 1:A_CHUNK]
            ebc = eb[sl]
            ebl = jnp.exp(bl - bc)
            lam = jnp.exp(bl)
            qt = qc * ebc
            kt = kc * ebl
            stp = st_ref[0, c]
            dob = doc.astype(_MXU_DTYPE)
            dstb = dst.astype(_MXU_DTYPE)
            dqt = jnp.dot(dob, stp.astype(_MXU_DTYPE), preferred_element_type=_F32)
            dkt = jnp.dot(vc.astype(_MXU_DTYPE), dstb, preferred_element_type=_F32)
            dv = lax.dot_general(kt.astype(_MXU_DTYPE), dstb, _NT, preferred_element_type=_F32)
            dlam = jnp.sum(stp * dst, axis=0, keepdims=True)
            dst = dst * lam + lax.dot_general(dob, qt.astype(_MXU_DTYPE), _TN, preferred_element_type=_F32)
            dqs_rows = []
            dk_in = jnp.zeros((A_CHUNK, HEAD_DIM), _F32)
            for i in range(A_CHUNK):
                di = jnp.exp(jnp.where(ridx <= i, bc[i:i + 1] - bc, _NEG))
                qi = qc[i:i + 1]
                doi = doc[i:i + 1]
                w = kc * di
                a = jnp.sum(qi * w, axis=1, keepdims=True)
                dv = dv + a * doi
                da = jnp.sum(doi * vc, axis=1, keepdims=True)
                dqs_rows.append(jnp.sum(da * w, axis=0, keepdims=True))
                dk_in = dk_in + da * (qi * di)
            dqs_in = jnp.concatenate(dqs_rows, axis=0)
            dbl = jnp.sum(dkt * kt, axis=0, keepdims=True) + dlam * lam
            db = qc * dqs_in - kc * dk_in + dqt * qt - dkt * kt
            db_t[c] = db + jnp.where(ridx == A_CHUNK - 1, dbl, 0.0)
            dqs_t[c] = dqs_in + dqt * ebc
            dk_t[c] = dk_in + dkt * ebl
            dv_t[c] = dv
        dstate[...] = dst
        dqs = jnp.concatenate(dqs_t, axis=0)
        dk = jnp.concatenate(dk_t, axis=0)
        db = jnp.concatenate(db_t, axis=0)
        di_ref[...] = jnp.concatenate(dv_t, axis=0).astype(di_ref.dtype)
        dlogf = jnp.dot(triu_ref[...], db, precision=_HI, preferred_element_type=_F32)
        df = dlogf / f - dk
        df_ref[...] = (df * (1.0 - lb) * sg * (1.0 - sg)).astype(df_ref.dtype)
        dlb_ref[0, 0:1, :] += jnp.sum(df * (1.0 - sg), axis=0, keepdims=True)
        dq_ref[...] = (dqs * (sgq * (1.0 + q * (1.0 - sgq)))).astype(dq_ref.dtype)

    blk = (_HG_TILE, HEAD_DIM)
    last = _HG_TILES - 1

    def col(base):
        return pl.BlockSpec(blk, lambda h, t: (last - t, base + h))

    tri_spec = pl.BlockSpec((_HG_TILE, _HG_TILE), lambda h, t: (0, 0))
    acc_spec = pl.BlockSpec((1, 8, HEAD_DIM), lambda h, t: (h, 0, 0))
    acc_shape = jax.ShapeDtypeStruct((A_HEADS, 8, HEAD_DIM), _F32)
    dq, df, di, dg, dnw, dlb = pl.pallas_call(
        body, grid=(A_HEADS, _HG_TILES),
        in_specs=[col(0), col(4), col(8), col(12), pl.BlockSpec((DEPTH, HEAD_DIM), lambda h, t: (0, h)),
                  pl.BlockSpec((1, HEAD_DIM), lambda h, t: (0, 0)), tri_spec, tri_spec, col(0), col(0),
                  pl.BlockSpec((1, _HG_CHUNKS, HEAD_DIM, HEAD_DIM), lambda h, t: (h, last - t, 0, 0))],
        out_specs=[col(0), col(0), col(0), col(0), pl.BlockSpec((8, HEAD_DIM), lambda h, t: (0, 0)), acc_spec],
        out_shape=[jax.ShapeDtypeStruct((SEQ, A_HEADS * HEAD_DIM), _MXU_DTYPE)] * 4
        + [jax.ShapeDtypeStruct((8, HEAD_DIM), _F32), acc_shape],
        scratch_shapes=[pltpu.VMEM((HEAD_DIM, HEAD_DIM), _F32)], name=name,
        compiler_params=_cparams(dimension_semantics=("arbitrary", "arbitrary")),
    )(proj, proj, proj, proj, lb_logits, norm_w.reshape(1, HEAD_DIM), tri, triu, raw, dmixed, states)
    return dq, df, di, dg, dnw[0], dlb[:, 0, :].reshape(A_HEADS * HEAD_DIM)


N_CHIP = N_DEV // 2
_MESH_ID = pl.DeviceIdType.MESH


def _place():
    x, y, c = lax.axis_index("x"), lax.axis_index("y"), lax.axis_index("c")
    chips = [(1 - x, y), (x, 1 - y), (1 - x, 1 - y)]
    return x, y, c, 2 * x + y, chips


def _sibling_swap(arrays, *, name):
    n = len(arrays)

    def body(*refs):
        ins, outs = refs[:n], refs[n:2 * n]
        send_sems, recv_sems = refs[2 * n:]
        x, y, c, _, _ = _place()
        copies = [pltpu.make_async_remote_copy(
            src_ref=ins[a].at[:, 1 - c], dst_ref=outs[a], send_sem=send_sems.at[a], recv_sem=recv_sems.at[a],
            device_id=(x, y, 1 - c), device_id_type=_MESH_ID) for a in range(n)]
        for cp in copies:
            cp.start()
        for cp in copies:
            cp.wait()

    any_spec = pl.BlockSpec(memory_space=pl.ANY)
    return pl.pallas_call(
        body, in_specs=[any_spec] * n, out_specs=[any_spec] * n,
        out_shape=[jax.ShapeDtypeStruct((N_CHIP,) + a.shape[2:], a.dtype) for a in arrays],
        scratch_shapes=[pltpu.SemaphoreType.DMA((n,)), pltpu.SemaphoreType.DMA((n,))],
        name=name, compiler_params=pltpu.CompilerParams(has_side_effects=True),
    )(*arrays)


def _pair_add(mine, theirs, core, *, name):
    _, _, R, C = mine.shape
    tr = max(t for t in range(16, R + 1, 16) if R % t == 0 and t * C <= 512 * 1024)

    def body(core_ref, m_ref, t_ref, o_ref):
        del core_ref
        o_ref[...] = (m_ref[...].astype(_F32) + t_ref[...].astype(_F32)).astype(o_ref.dtype)

    grid_spec = pltpu.PrefetchScalarGridSpec(
        num_scalar_prefetch=1, grid=(N_CHIP, R // tr),
        in_specs=[pl.BlockSpec((None, None, tr, C), lambda q, i, core: (q, core[0], i, 0)),
                  pl.BlockSpec((None, tr, C), lambda q, i, core: (q, i, 0))],
        out_specs=pl.BlockSpec((None, tr, C), lambda q, i, core: (q, i, 0)))
    return pl.pallas_call(
        body, grid_spec=grid_spec, out_shape=jax.ShapeDtypeStruct((N_CHIP, R, C), mine.dtype), name=name,
        compiler_params=_cparams(dimension_semantics=("parallel", "parallel")),
    )(core.reshape(1), mine, theirs)


_HBM = pl.BlockSpec(memory_space=pltpu.HBM)
_SEM = pl.BlockSpec(memory_space=pltpu.SEMAPHORE)
_TOKEN = pl.BlockSpec(memory_space=pltpu.VMEM)
_DATAFLOW = pltpu.SideEffectType.DATAFLOW_SIDE_EFFECTING


def _hbm(a):
    return pltpu.HBM(a.shape, a.dtype)


def _token_shape():
    return jax.ShapeDtypeStruct((8, 128), _F32)


def _dev_slot(px, py, pc):
    return 4 * px + 2 * py + pc


def _gather_start(blocks, landings, *, name):
    n = len(blocks)

    def body(*refs):
        ins, lands = refs[:n], refs[n:2 * n]
        send_sems, d2d_sems, ici_sems = refs[2 * n:2 * n + 3]
        token = refs[-1]
        x, y, c, _, chips = _place()
        for a in range(n):
            dst = lands[a].at[_dev_slot(x, y, c)]
            pltpu.make_async_remote_copy(src_ref=ins[a], dst_ref=dst, send_sem=send_sems.at[4 * a], recv_sem=d2d_sems.at[a],
                                         device_id=(x, y, 1 - c), device_id_type=_MESH_ID).start()
            for j, chip in enumerate(chips):
                pltpu.make_async_remote_copy(src_ref=ins[a], dst_ref=dst, send_sem=send_sems.at[4 * a + 1 + j],
                                             recv_sem=ici_sems.at[3 * a + j], device_id=(*chip, c),
                                             device_id_type=_MESH_ID).start()
        token[...] = jnp.zeros_like(token)

    res = pl.pallas_call(
        body, name=name, in_specs=[_HBM] * (2 * n),
        out_shape=(pltpu.SemaphoreType.DMA((4 * n,)), pltpu.SemaphoreType.DMA((n,)), pltpu.SemaphoreType.DMA((3 * n,)),
                   *[_hbm(b) for b in blocks], *[_hbm(b) for b in landings], _token_shape()),
        out_specs=(_SEM, _SEM, _SEM, *[_HBM] * (2 * n), _TOKEN),
        input_output_aliases={i: 3 + i for i in range(2 * n)},
        compiler_params=pltpu.CompilerParams(has_side_effects=_DATAFLOW),
    )(*[pltpu.with_memory_space_constraint(b, pltpu.HBM) for b in blocks],
      *[pltpu.with_memory_space_constraint(b, pltpu.HBM) for b in landings])
    return res[0], res[1], res[2], list(res[3:3 + n]), list(res[3 + n:3 + 2 * n]), res[-1]


def _gather_forward(landings, ici_sems, first, after, *, name):
    n = len(landings)

    def body(*refs):
        lands = refs[:n]
        ici = refs[n]
        f_send, f_recv = refs[n + 2], refs[n + 3]
        token = refs[-1]
        x, y, c, _, chips = _place()
        for a in range(n):
            for j, chip in enumerate(chips):
                blk = lands[a].at[_dev_slot(*chip, c)]
                pltpu.make_async_remote_copy(src_ref=blk, dst_ref=blk, send_sem=f_send.at[3 * a + j],
                                             recv_sem=ici.at[3 * (first + a) + j], device_id=(*chip, c),
                                             device_id_type=_MESH_ID).wait_recv()
                pltpu.make_async_remote_copy(src_ref=blk, dst_ref=blk, send_sem=f_send.at[3 * a + j], recv_sem=f_recv.at[3 * a + j],
                                             device_id=(x, y, 1 - c), device_id_type=_MESH_ID).start()
        token[...] = jnp.zeros_like(token)

    res = pl.pallas_call(
        body, name=name, in_specs=[_HBM] * n + [_SEM, pl.BlockSpec(memory_space=pl.ANY)],
        out_shape=(pltpu.SemaphoreType.DMA((3 * n,)), pltpu.SemaphoreType.DMA((3 * n,)), *[_hbm(b) for b in landings], _token_shape()),
        out_specs=(_SEM, _SEM, *[_HBM] * n, _TOKEN),
        input_output_aliases={i: 2 + i for i in range(n)},
        compiler_params=pltpu.CompilerParams(has_side_effects=_DATAFLOW),
    )(*landings, ici_sems, after)
    return res[0], res[1], list(res[2:2 + n]), res[-1]


def _gather_wait(blocks, landings, send_sems, d2d_sems, first, f_send, f_recv, after, *, name):
    n = len(landings)

    def body(*refs):
        ins, lands = refs[:n], refs[n:2 * n]
        send, d2d, fs, fr = refs[2 * n:2 * n + 4]
        x, y, c, _, chips = _place()
        me = (x, y, c)
        for a in range(n):
            own = lands[a].at[_dev_slot(x, y, 1 - c)]
            g = first + a
            pltpu.make_async_remote_copy(src_ref=ins[a], dst_ref=own, send_sem=send.at[4 * g], recv_sem=d2d.at[g],
                                         device_id=me, device_id_type=_MESH_ID).wait_recv()
            for j, chip in enumerate(chips):
                blk = lands[a].at[_dev_slot(*chip, 1 - c)]
                pltpu.make_async_remote_copy(src_ref=blk, dst_ref=blk, send_sem=fs.at[3 * a + j], recv_sem=fr.at[3 * a + j],
                                             device_id=me, device_id_type=_MESH_ID).wait_recv()
            for k in range(4):
                pltpu.make_async_remote_copy(src_ref=ins[a], dst_ref=own, send_sem=send.at[4 * g + k], recv_sem=d2d.at[g],
                                             device_id=me, device_id_type=_MESH_ID).wait_send()
            for j in range(3):
                pltpu.make_async_remote_copy(src_ref=own, dst_ref=own, send_sem=fs.at[3 * a + j], recv_sem=fr.at[3 * a + j],
                                             device_id=me, device_id_type=_MESH_ID).wait_send()

    res = pl.pallas_call(
        body, name=name, in_specs=[_HBM] * (2 * n) + [_SEM] * 4 + [pl.BlockSpec(memory_space=pl.ANY)],
        out_shape=(*[_hbm(b) for b in blocks], *[_hbm(b) for b in landings]), out_specs=tuple([_HBM] * (2 * n)),
        input_output_aliases={i: i for i in range(2 * n)},
        compiler_params=pltpu.CompilerParams(has_side_effects=_DATAFLOW),
    )(*blocks, *landings, send_sems, d2d_sems, f_send, f_recv, after)
    return list(res[n:])


def _swap_start(mine, landings, *, name):
    n = len(mine)

    def body(*refs):
        ins, lands = refs[:n], refs[n:2 * n]
        send_sems, recv_sems = refs[2 * n:2 * n + 2]
        token = refs[-1]
        x, y, c, _, _ = _place()
        for a in range(n):
            pltpu.make_async_remote_copy(src_ref=ins[a].at[:, 1 - c], dst_ref=lands[a], send_sem=send_sems.at[a],
                                         recv_sem=recv_sems.at[a], device_id=(x, y, 1 - c), device_id_type=_MESH_ID).start()
        token[...] = jnp.zeros_like(token)

    res = pl.pallas_call(
        body, name=name, in_specs=[_HBM] * (2 * n),
        out_shape=(pltpu.SemaphoreType.DMA((n,)), pltpu.SemaphoreType.DMA((n,)),
                   *[_hbm(b) for b in mine], *[_hbm(b) for b in landings], _token_shape()),
        out_specs=(_SEM, _SEM, *[_HBM] * (2 * n), _TOKEN),
        input_output_aliases={i: 2 + i for i in range(2 * n)},
        compiler_params=pltpu.CompilerParams(has_side_effects=_DATAFLOW),
    )(*[pltpu.with_memory_space_constraint(b, pltpu.HBM) for b in mine],
      *[pltpu.with_memory_space_constraint(b, pltpu.HBM) for b in landings])
    return res[0], res[1], list(res[2:2 + n]), list(res[2 + n:2 + 2 * n]), res[-1]


def _swap_wait(mine, landings, send_sems, recv_sems, after, *, name):
    n = len(mine)

    def body(*refs):
        ins, lands = refs[:n], refs[n:2 * n]
        send, recv = refs[2 * n:2 * n + 2]
        x, y, c, _, _ = _place()
        for a in range(n):
            cp = pltpu.make_async_remote_copy(src_ref=ins[a].at[:, 1 - c], dst_ref=lands[a], send_sem=send.at[a],
                                              recv_sem=recv.at[a], device_id=(x, y, c), device_id_type=_MESH_ID)
            cp.wait_recv()
            cp.wait_send()

    res = pl.pallas_call(
        body, name=name, in_specs=[_HBM] * (2 * n) + [_SEM] * 2 + [pl.BlockSpec(memory_space=pl.ANY)],
        out_shape=(*[_hbm(b) for b in mine], *[_hbm(b) for b in landings]), out_specs=tuple([_HBM] * (2 * n)),
        input_output_aliases={i: i for i in range(2 * n)},
        compiler_params=pltpu.CompilerParams(has_side_effects=_DATAFLOW),
    )(*mine, *landings, send_sems, recv_sems, after)
    return list(res[:n]), list(res[n:])


def _chip_exchange_start(sums, landings, *, name):
    n = len(sums)

    def body(*refs):
        ins, lands = refs[:n], refs[n:2 * n]
        send_sems, recv_sems = refs[2 * n:2 * n + 2]
        token = refs[-1]
        _, _, c, p, chips = _place()
        for a in range(n):
            for j, (qx, qy) in enumerate(chips):
                pltpu.make_async_remote_copy(src_ref=ins[a].at[2 * qx + qy], dst_ref=lands[a].at[p], send_sem=send_sems.at[3 * a + j],
                                             recv_sem=recv_sems.at[3 * a + j], device_id=(qx, qy, c), device_id_type=_MESH_ID).start()
        token[...] = jnp.zeros_like(token)

    res = pl.pallas_call(
        body, name=name, in_specs=[_HBM] * (2 * n),
        out_shape=(pltpu.SemaphoreType.DMA((3 * n,)), pltpu.SemaphoreType.DMA((3 * n,)),
                   *[_hbm(b) for b in sums], *[_hbm(b) for b in landings], _token_shape()),
        out_specs=(_SEM, _SEM, *[_HBM] * (2 * n), _TOKEN),
        input_output_aliases={i: 2 + i for i in range(2 * n)},
        compiler_params=pltpu.CompilerParams(has_side_effects=_DATAFLOW),
    )(*[pltpu.with_memory_space_constraint(b, pltpu.HBM) for b in sums],
      *[pltpu.with_memory_space_constraint(b, pltpu.HBM) for b in landings])
    return res[0], res[1], list(res[2:2 + n]), list(res[2 + n:2 + 2 * n]), res[-1]


def _chip_exchange_wait(sums, landings, send_sems, recv_sems, after, *, name):
    n = len(sums)

    def body(*refs):
        ins, lands = refs[:n], refs[n:2 * n]
        send, recv = refs[2 * n:2 * n + 2]
        x, y, c, _, chips = _place()
        for a in range(n):
            for j, (qx, qy) in enumerate(chips):
                q = 2 * qx + qy
                cp = pltpu.make_async_remote_copy(src_ref=ins[a].at[q], dst_ref=lands[a].at[q], send_sem=send.at[3 * a + j],
                                                  recv_sem=recv.at[3 * a + j], device_id=(x, y, c), device_id_type=_MESH_ID)
                cp.wait_recv()
                cp.wait_send()

    res = pl.pallas_call(
        body, name=name, in_specs=[_HBM] * (2 * n) + [_SEM] * 2 + [pl.BlockSpec(memory_space=pl.ANY)] * len(after),
        out_shape=(*[_hbm(b) for b in sums], *[_hbm(b) for b in landings]), out_specs=tuple([_HBM] * (2 * n)),
        input_output_aliases={i: i for i in range(2 * n)},
        compiler_params=pltpu.CompilerParams(has_side_effects=_DATAFLOW),
    )(*sums, *landings, send_sems, recv_sems, *after)
    return list(res[:n]), list(res[n:])


_C1 = 1.0 - ADAM_B1 ** ADAM_STEP
_C2 = 1.0 - ADAM_B2 ** ADAM_STEP


def _adamw_math(g, w, m, v):
    m = ADAM_B1 * m + (1.0 - ADAM_B1) * g
    v = ADAM_B2 * v + (1.0 - ADAM_B2) * (g * g)
    delta = -ADAM_LR * ((m / _C1) / (jnp.sqrt(v / _C2) + ADAM_EPS) + ADAM_WD * w)
    return delta, m, v


def _adamw_reduce(landed, sums, chip, w, m, v, layer, prev, *, name):
    _, R, C = w.shape
    tr = max(t for t in range(16, R + 1, 16) if R % t == 0 and t * C <= 256 * 1024)

    def body(chip_ref, p_ref, own_ref, w_ref, m_ref, v_ref, *rest):
        g_ref, d_ref, nm_ref, nv_ref = rest[-4:]
        own = own_ref[...].astype(_F32)
        g = jnp.where(chip_ref[0] == 0, own, p_ref[0].astype(_F32))
        for q in range(1, N_CHIP):
            g = g + jnp.where(chip_ref[0] == q, own, p_ref[q].astype(_F32))
        d, nm, nv = _adamw_math(g, w_ref[...], m_ref[...], v_ref[...])
        g_ref[...] = g
        d_ref[...] = d
        nm_ref[...] = nm
        nv_ref[...] = nv

    blk = pl.BlockSpec((None, tr, C), lambda i, chip: (layer, i, 0))
    shape = jax.ShapeDtypeStruct((DEPTH, R, C), _F32)
    kept = [] if prev is None else list(prev)
    grid_spec = pltpu.PrefetchScalarGridSpec(
        num_scalar_prefetch=1, grid=(R // tr,),
        in_specs=[pl.BlockSpec((N_CHIP, tr, C), lambda i, chip: (0, i, 0)),
                  pl.BlockSpec((None, tr, C), lambda i, chip: (chip[0], i, 0)), blk, blk, blk]
        + [pl.BlockSpec(memory_space=pl.ANY)] * len(kept),
        out_specs=[blk] * 4)
    return pl.pallas_call(
        body, grid_spec=grid_spec, out_shape=[shape] * 4, name=name,
        input_output_aliases={6 + k: k for k in range(len(kept))},
        compiler_params=_cparams(dimension_semantics=("parallel",)),
    )(chip.reshape(1), landed, sums, w, m, v, *kept)


_PACK_LANES = 128
_LAYER_ROWS = 248
_LB_ROWS = (A_HEADS * HEAD_DIM) // _PACK_LANES


def _small_reduce(parts, lb_logits, *, name):
    rows = DEPTH * _LAYER_ROWS

    def body(p_ref, lg_ref, o_ref):
        g = p_ref[0]
        for s in range(1, N_DEV):
            g = g + p_ref[s]
        o_ref[...] = g
        lg = lg_ref[...]
        e = jnp.exp(lg - jnp.max(lg, axis=0, keepdims=True))
        p = e / jnp.sum(e, axis=0, keepdims=True)
        d1 = g[_LAYER_ROWS:_LAYER_ROWS + _LB_ROWS, :] * p[0] * p[1]
        o_ref[0:_LB_ROWS, :] = -d1
        o_ref[_LAYER_ROWS:_LAYER_ROWS + _LB_ROWS, :] = d1

    return pl.pallas_call(
        body, out_shape=jax.ShapeDtypeStruct((rows, _PACK_LANES), _F32), name=name,
        compiler_params=_cparams(),
    )(parts, lb_logits.reshape(DEPTH, _LB_ROWS, _PACK_LANES))


def _adamw_small(g, w, m, v, *, name):
    def body(g_ref, w_ref, m_ref, v_ref, d_ref, nm_ref, nv_ref):
        d, nm, nv = _adamw_math(g_ref[...], w_ref[...], m_ref[...], v_ref[...])
        d_ref[...] = d
        nm_ref[...] = nm
        nv_ref[...] = nv

    shape = jax.ShapeDtypeStruct(g.shape, _F32)
    return pl.pallas_call(body, out_shape=[shape] * 3, name=name, compiler_params=_cparams())(g, w, m, v)


def _pack(vectors, rows):
    flat = jnp.concatenate([v.reshape(-1).astype(_F32) for v in vectors])
    return jnp.pad(flat, (0, rows * _PACK_LANES - flat.shape[0])).reshape(rows, _PACK_LANES)


def _unpack(packed, shapes):
    flat = packed.reshape(-1)
    out, at = [], 0
    for s in shapes:
        size = int(np.prod(s))
        out.append(flat[at:at + size].reshape(s))
        at += size
    return out


_BIG = ("w_in", "w_gate", "w_up", "w_out", "w_down")
_COLUMN_SHARDED = ("w_in", "w_gate", "w_up")


def _full_weight(name, g):
    if name == "conv_w":
        return g.transpose(1, 0, 2).reshape(g.shape[1], N_DEV * SHARD_COLS)
    if name in _BIG:
        return g.reshape(N_DEV * g.shape[1], g.shape[2])
    return g


class _WeightGather:
    def __init__(self, names, first, blocks, lands, sems, tag):
        self.names, self.first, self.blocks, self.lands, self.sems, self.tag = names, first, blocks, lands, sems, tag
        self.forwarded = None

    def forward(self, after):
        f_send, f_recv, self.lands, token = _gather_forward(self.lands, self.sems[2], self.first, after,
                                                            name=f"gather_forward_{self.tag}")
        self.forwarded = (f_send, f_recv)
        return token

    def wait(self, after):
        if self.forwarded is None:
            self.forward(after)
        got = _gather_wait(self.blocks, self.lands, self.sems[0], self.sems[1], self.first, *self.forwarded, after,
                           name=f"gather_wait_{self.tag}")
        return {n: _full_weight(n, g) for n, g in zip(self.names, got)}


def _start_gathers(groups, me, name):
    blocks = [b for _, _, bs in groups for b in bs]
    landings = [lax.dynamic_update_index_in_dim(lax.empty((N_DEV,) + b.shape, b.dtype), b[None], me, 0) for b in blocks]
    send, d2d, ici, blocks, landings, token = _gather_start(blocks, landings, name=name)
    out, first = [], 0
    for tag, names, bs in groups:
        k = len(bs)
        out.append(_WeightGather(names, first, blocks[first:first + k], landings[first:first + k], (send, d2d, ici), tag))
        first += k
    return out, token


class _LayerWeights:
    def __init__(self, ready, pending=(), forwards=(), tokens=()):
        self.ready, self.pending, self.forwards, self._tokens = dict(ready), list(pending), list(forwards), list(tokens)

    def at(self, point, after):
        for when, gather in self.forwards:
            if when == point:
                self._tokens.append(gather.forward(after))

    def tokens(self):
        out, self._tokens = self._tokens, []
        return out

    def get(self, name, after):
        if name not in self.ready:
            group, = [g for g in self.pending if name in g.names]
            self.ready.update(group.wait(after))
        return self.ready[name]


def _layer_fwd(x, xb, ws, lb_logits, a_norm_w, c_sink, ln1_g, ln1_b, conv_b, ln2_g, ln2_b, tabs, l, target=None):
    proj = _mm(xb, ws.get("w_in", xb), tb=True, **_TILE_WIDE_N, after=ws.tokens(), name=f"proj_{l}")
    o_a, raw, states = _hgrn_fwd(proj, lb_logits, a_norm_w, l, name=f"hgrn_fwd_{l}")
    ws.at("hgrn", o_a)
    o_b, lse_b = _band_fwd(proj, tabs, name=f"dilated_fwd_{l}", **_DILATED)
    o_c, lse_c = _band_fwd(proj, tabs, sink=c_sink, name=f"swa_fwd_{l}", **_SWA)
    ws.at("swa", o_c)
    mixed = _concat_cols([o_a, o_b, o_c], name=f"mixed_{l}")
    y = _mm(mixed, ws.get("w_out", mixed), **_TILE_MIX, after=ws.tokens(), name=f"mix_out_{l}")
    z1, x1, x1b = _ln_fwd(x, y, ln1_g, ln1_b, name=f"ln1_fwd_{l}")
    g = _mm(x1b, ws.get("w_gate", x1b), tb=True, **_TILE_WIDE_N, out_dtype=_ACT_DTYPE, name=f"ffn_gate_{l}")
    u = _mm(x1b, ws.get("w_up", x1b), tb=True, **_TILE_WIDE_N, out_dtype=_ACT_DTYPE, name=f"ffn_up_{l}")
    ws.at("up", u)
    hb = _conv_gate_fwd(g, u, ws.get("conv_w", u), conv_b, name=f"conv_gate_fwd_{l}")
    y2 = _mm(hb, ws.get("w_down", hb), **_TILE_WIDE_K, after=ws.tokens(), name=f"ffn_down_{l}")
    ws.at("down", y2)
    res = dict(xb=xb, proj=proj, raw=raw, states=states, o_b=o_b, lse_b=lse_b, o_c=o_c, lse_c=lse_c,
               mixed=mixed, z1=z1, x1b=x1b, g=g, u=u, hb=hb)
    if target is not None:
        loss_part, *res["ln2_bwd"] = _ln_loss_bwd(x1, y2, ln2_g, ln2_b, target, name=f"ln2_loss_{l}")
        return loss_part, None, res
    res["z2"], x2, x2b = _ln_fwd(x1, y2, ln2_g, ln2_b, name=f"ln2_fwd_{l}")
    return x2, x2b, res


class _GradExchange:
    def __init__(self, core, chip):
        self.core, self.chip, self.groups, self.swapping, self._tokens = core, chip, [], [], []

    def launch(self, names, slabs, l, tag, behind):
        mine = [s.reshape((N_CHIP, 2) + s.shape[1:]) for s in slabs]
        if behind:
            landings = [lax.empty((N_CHIP,) + m.shape[2:], m.dtype) for m in mine]
            send, recv, mine, landings, token = _swap_start(mine, landings, name=f"swap_start_{tag}")
            self.swapping.append((names, l, tag, send, recv, mine, landings))
            self._tokens.append(token)
        else:
            self._exchange(names, l, tag, mine, _sibling_swap(mine, name=f"swap_grads_{tag}"))

    def advance(self, after):
        for names, l, tag, send, recv, mine, landings in self.swapping:
            mine, theirs = _swap_wait(mine, landings, send, recv, after, name=f"swap_wait_{tag}")
            self._exchange(names, l, tag, mine, theirs)
        self.swapping = []

    def _exchange(self, names, l, tag, mine, theirs):
        sums = [_pair_add(a, b, self.core, name=f"pair_add_{n}_{l}") for n, a, b in zip(names, mine, theirs)]
        landings = [lax.empty(s.shape, s.dtype) for s in sums]
        send, recv, sums, landings, token = _chip_exchange_start(sums, landings, name=f"exchange_start_{tag}")
        self.groups.append((names, l, tag, send, recv, sums, landings))
        self._tokens.append(token)

    def tokens(self):
        out, self._tokens = self._tokens, []
        return out

    def finish(self, weights, mom1, mom2, after):
        out = {}
        after = list(after) + self.tokens()
        for names, l, tag, send, recv, sums, landings in self.groups:
            sums, landings = _chip_exchange_wait(sums, landings, send, recv, after, name=f"exchange_wait_{tag}")
            for n, s, landed in zip(names, sums, landings):
                out[n] = _adamw_reduce(landed, s, self.chip, weights[n], mom1[n], mom2[n], l, out.get(n), name=f"adamw_{n}_{l}")
            after = [out[n][0] for n in names]
        return out


def _layer_bwd(dx2, res, w, lb_logits, a_norm_w, c_sink, ln1_g, conv_b, ln2_g, tabs, exchange, l):
    if "ln2_bwd" in res:
        dz2, dz2b, d_ln2_g, d_ln2_b = res["ln2_bwd"]
    else:
        dz2, dz2b, d_ln2_g, d_ln2_b = _ln_bwd(res["z2"], dx2, None, ln2_g, name=f"ln2_bwd_{l}")
    exchange.advance(dz2b)
    dh = _mm(dz2b, w["w_down"], tb=True, **_TILE_WIDE_N, out_dtype=_ACT_DTYPE, after=exchange.tokens(),
             name=f"ffn_down_dx_{l}")
    d_w_down = _mm(res["hb"], dz2b, ta=True, **_TILE_WIDE_M, out_dtype=_GRAD_DTYPE, name=f"ffn_down_dw_{l}")
    dg, du, d_conv_w, d_conv_b = _conv_gate_bwd(dh, res["g"], res["u"], w["conv_w"], conv_b, name=f"conv_gate_bwd_{l}")
    dx1 = _mm_sum2(dg, w["w_gate"], du, w["w_up"], **_TILE_SUM2, name=f"ffn_gate_up_dx_{l}")
    d_w_gate = _mm(dg, res["x1b"], ta=True, **_TILE_WIDE_M, out_dtype=_GRAD_DTYPE, name=f"ffn_gate_dw_{l}")
    d_w_up = _mm(du, res["x1b"], ta=True, **_TILE_WIDE_M, out_dtype=_GRAD_DTYPE, name=f"ffn_up_dw_{l}")
    dz1, dz1b, d_ln1_g, d_ln1_b = _ln_bwd(res["z1"], dx1, dz2, ln1_g, name=f"ln1_bwd_{l}")
    d_w_out = _mm(res["mixed"], dz1b, ta=True, **_TILE_MIX, out_dtype=_GRAD_DTYPE, name=f"mix_out_dw_{l}")
    exchange.launch(("w_down", "w_gate", "w_up", "w_out"),
                    [d.reshape(N_DEV, d.shape[0] // N_DEV, D_MODEL) for d in (d_w_down, d_w_gate, d_w_up, d_w_out)],
                    l, f"ffn_{l}", True)
    dmixed = _mm(dz1b, w["w_out"], tb=True, **_TILE_MIX, after=exchange.tokens(), name=f"mix_out_dx_{l}")
    dq_a, df_a, di_a, dg_a, d_norm_w, d_lb = _hgrn_bwd(res["proj"], lb_logits, a_norm_w, res["raw"], res["states"],
                                                      dmixed, l, name=f"hgrn_bwd_{l}")
    exchange.advance(dq_a)
    dq_b, dk_b, dv_b = _band_bwd(res["proj"], tabs, dmixed, res["o_b"], res["lse_b"], do0=A_HEADS, after=exchange.tokens(),
                                 name=f"dilated_bwd_{l}", **_DILATED)
    dq_c, dk_c, dv_c, d_sink = _band_bwd(res["proj"], tabs, dmixed, res["o_c"], res["lse_c"], do0=A_HEADS + B_HEADS,
                                         sink=c_sink, name=f"swa_bwd_{l}", **_SWA)
    dproj = _concat_cols([dq_a, df_a, di_a, dg_a, dq_b, dk_b, dv_b, dq_c, dk_c, dv_c], name=f"dproj_{l}")
    d_w_in = _mm(dproj, res["xb"], ta=True, **_TILE_WIDE_M, out_dtype=_GRAD_DTYPE, name=f"proj_dw_{l}")
    exchange.launch(("w_in",), [d_w_in.reshape(N_DEV, SHARD_COLS, D_MODEL)], l, f"mix_{l}", l > 0)
    dx = _mm(dproj, w["w_in"], **_TILE_WIDE_K, add=dz1, add_scale=ALPHA, after=exchange.tokens(), name=f"proj_dx_{l}")
    small = [d_lb, d_norm_w, jnp.pad(d_sink, (0, _PACK_LANES - C_HEADS)), d_ln1_g, d_ln1_b, d_ln2_g, d_ln2_b, d_conv_b,
             d_conv_w]
    return dx, small


def kernel(x, w_in, lb_logits, a_norm_w, c_sinks, w_out, ln1_g, ln1_b, w_gate, w_up, conv_w, conv_b, w_down, ln2_g, ln2_b, loss_target, m_w_in, m_lb_logits, m_a_norm_w, m_c_sinks, m_w_out, m_ln1_g, m_ln1_b, m_w_gate, m_w_up, m_conv_w, m_conv_b, m_w_down, m_ln2_g, m_ln2_b, v_w_in, v_lb_logits, v_a_norm_w, v_c_sinks, v_w_out, v_ln1_g, v_ln1_b, v_w_gate, v_w_up, v_conv_w, v_conv_b, v_w_down, v_ln2_g, v_ln2_b):
    weights = dict(w_in=w_in, lb_logits=lb_logits, a_norm_w=a_norm_w, c_sinks=c_sinks, w_out=w_out, ln1_g=ln1_g, ln1_b=ln1_b,
                   w_gate=w_gate, w_up=w_up, conv_w=conv_w, conv_b=conv_b, w_down=w_down, ln2_g=ln2_g, ln2_b=ln2_b)
    mom1 = dict(w_in=m_w_in, lb_logits=m_lb_logits, a_norm_w=m_a_norm_w, c_sinks=m_c_sinks, w_out=m_w_out, ln1_g=m_ln1_g,
                ln1_b=m_ln1_b, w_gate=m_w_gate, w_up=m_w_up, conv_w=m_conv_w, conv_b=m_conv_b, w_down=m_w_down, ln2_g=m_ln2_g,
                ln2_b=m_ln2_b)
    mom2 = dict(w_in=v_w_in, lb_logits=v_lb_logits, a_norm_w=v_a_norm_w, c_sinks=v_c_sinks, w_out=v_w_out, ln1_g=v_ln1_g,
                ln1_b=v_ln1_b, w_gate=v_w_gate, w_up=v_w_up, conv_w=v_conv_w, conv_b=v_conv_b, w_down=v_w_down, ln2_g=v_ln2_g,
                ln2_b=v_ln2_b)
    core = lax.axis_index("c").astype(jnp.int32)
    me = 4 * lax.axis_index("x") + 2 * lax.axis_index("y") + core
    tabs = _rope_tables()

    chip = (2 * lax.axis_index("x") + lax.axis_index("y")).astype(jnp.int32)

    def as_slabs(d):
        return {n: jnp.swapaxes(d[n], 1, 2) if n in _COLUMN_SHARDED else d[n] for n in _BIG}

    w_views = as_slabs(weights)

    def block(n, l, after=()):
        return conv_w[l] if n == "conv_w" else _cast_layer(w_views[n], l, after=after, name=f"cast_{n}_{l}")

    (in0,), started_first = _start_gathers([("w_in_0", ("w_in",), [block("w_in", 0)])], me, "gather_start_first")
    order = [(("w_out",), 0), (("w_gate", "w_up", "conv_w"), 0), (("w_down",), 0),
             (("w_in",), 1), (("w_out",), 1), (("w_gate", "w_up", "conv_w"), 1), (("w_down",), 1)]
    gathers, started = _start_gathers([(f"{names[0]}_{l}", names, [block(n, l, [started_first]) for n in names])
                                       for names, l in order], me, "gather_start_rest")
    out0, ffn0, down0, in1, out1, ffn1, down1 = gathers
    layer_ws = [_LayerWeights(in0.wait(started), [out0, ffn0, down0],
                              [("hgrn", out0), ("swa", ffn0), ("up", down0), ("down", in1)]),
                _LayerWeights({}, [in1, out1, ffn1, down1], [("hgrn", out1), ("swa", ffn1), ("up", down1)])]

    xs = x[0]
    xb = xs.astype(_MXU_DTYPE)
    saved = []
    for l in range(DEPTH):
        xs, xb, res = _layer_fwd(xs, xb, layer_ws[l], lb_logits, a_norm_w[l], c_sinks[l], ln1_g[l], ln1_b[l], conv_b[l],
                                 ln2_g[l], ln2_b[l], tabs, l, loss_target[0] if l == DEPTH - 1 else None)
        saved.append(res)
    loss = lax.psum(xs, ("x", "y", "c"))
    dx = None

    exchange = _GradExchange(core, chip)
    small_parts = [None] * DEPTH
    for l in reversed(range(DEPTH)):
        dx, small = _layer_bwd(dx, saved[l], layer_ws[l].ready, lb_logits, a_norm_w[l], c_sinks[l], ln1_g[l], conv_b[l],
                               ln2_g[l], tabs, exchange, l)
        small_parts[l] = _pack(small, _LAYER_ROWS)
    (small_gather,), small_started = _start_gathers(
        [("small_grads", ("small",), [jnp.concatenate(small_parts, axis=0)])], me, "gather_start_small")
    updated = exchange.finish(w_views, as_slabs(mom1), as_slabs(mom2), [dx, small_started])
    gathered = small_gather.wait(updated["w_in"][0])["small"]
    updated = {n: tuple(jnp.swapaxes(t, 1, 2) for t in u) if n in _COLUMN_SHARDED else u for n, u in updated.items()}
    g_small = _small_reduce(gathered, lb_logits, name="small_grads")

    per_layer = [(A_HEADS * HEAD_DIM,), (HEAD_DIM,), (_PACK_LANES,), (D_MODEL,), (D_MODEL,), (D_MODEL,), (D_MODEL,), (D_FF,),
                 (3, D_FF)]
    names = ("lb_logits", "a_norm_w", "c_sinks", "ln1_g", "ln1_b", "ln2_g", "ln2_b", "conv_b", "conv_w")
    grads = {n: [] for n in names}
    for l in range(DEPTH):
        for n, t in zip(names, _unpack(g_small[l * _LAYER_ROWS:(l + 1) * _LAYER_ROWS], per_layer)):
            grads[n].append(t)
    grads = {n: jnp.stack(t) for n, t in grads.items()}
    grads["c_sinks"] = grads["c_sinks"][:, :C_HEADS]
    grads["conv_w"] = lax.dynamic_slice_in_dim(grads["conv_w"], me * SHARD_COLS, SHARD_COLS, axis=2)
    shapes = [grads[n].shape for n in names]
    rows = -(-sum(int(np.prod(s)) for s in shapes) // (8 * _PACK_LANES)) * 8
    d_s, m_s, v_s = _adamw_small(_pack([grads[n] for n in names], rows), _pack([weights[n] for n in names], rows),
                                 _pack([mom1[n] for n in names], rows), _pack([mom2[n] for n in names], rows),
                                 name="adamw_small")
    delta = dict(zip(names, _unpack(d_s, shapes)))
    new_m = dict(zip(names, _unpack(m_s, shapes)))
    new_v = dict(zip(names, _unpack(v_s, shapes)))
    for n in _BIG:
        grads[n], delta[n], new_m[n], new_v[n] = updated[n]

    order = ("w_in", "lb_logits", "a_norm_w", "c_sinks", "w_out", "ln1_g", "ln1_b", "w_gate", "w_up", "conv_w", "conv_b",
             "w_down", "ln2_g", "ln2_b")
    return (loss, dx[None], *[grads[n] for n in order], *[delta[n] for n in order], *[new_m[n] for n in order],
            *[new_v[n] for n in order])
```

```python
import functools

import jax
import jax.numpy as jnp
import numpy as np
from jax import lax
from jax.experimental import pallas as pl
from jax.experimental.pallas import tpu as pltpu

D_MODEL = 2048
SEQ = 2048
DEPTH = 2
HEAD_DIM = 128
A_HEADS = 4
B_HEADS = 6
C_HEADS = 6
C_KV_HEADS = 2
A_CHUNK = 16
DILATIONS = (1, 4, 16)
BLOCK = 128
ROPE_THETA = 500000.0
ROPE_DIM = 32
D_FF = 5632
IN_WIDTH = 5632
LN_EPS = 1e-5
ALPHA = (2 * DEPTH) ** 0.25
N_DEV = 8
SHARD_COLS = IN_WIDTH // N_DEV

ADAM_LR = 0.001
ADAM_B1 = 0.9
ADAM_B2 = 0.999
ADAM_EPS = 1e-08
ADAM_WD = 0.01
ADAM_STEP = 10

A_COLS = 16
QKV_COLS = 28
QB0, KB0, VB0, QC0, KC0, VC0 = 0, 6, 12, 18, 24, 26

_MXU_DTYPE = jnp.bfloat16
_GRAD_DTYPE = jnp.bfloat16
_ACT_DTYPE = jnp.bfloat16
_NEG = -1e30
_VMEM_LIMIT = 56 * 2 ** 20

_F32 = jnp.float32


def _sigmoid(x):
    return 0.5 * jnp.tanh(0.5 * x) + 0.5


def _cparams(**kw):
    return pltpu.CompilerParams(vmem_limit_bytes=_VMEM_LIMIT, **kw)


_TILE_MIX = dict(tm=1024, tn=1024)
_TILE_WIDE_K = dict(tm=1024, tn=512)
_TILE_WIDE_N = dict(tm=1024, tn=1408)
_TILE_WIDE_M = dict(tm=1408, tn=1024)


def _mm(a, b, *, ta=False, tb=False, tm, tn, out_dtype=_F32, add=None, add_scale=1.0, after=(), name):
    K = a.shape[0] if ta else a.shape[1]
    M = a.shape[1] if ta else a.shape[0]
    N = b.shape[0] if tb else b.shape[1]
    assert (b.shape[1] if tb else b.shape[0]) == K and M % tm == 0 and N % tn == 0
    dn = (((0 if ta else 1,), (1 if tb else 0,)), ((), ()))

    def body(*refs):
        a_ref, b_ref = refs[:2]
        o_ref = refs[-1]
        r = lax.dot_general(a_ref[...], b_ref[...], dn, preferred_element_type=_F32)
        if add is not None:
            r = r + add_scale * refs[2][...]
        o_ref[...] = r.astype(o_ref.dtype)

    a_spec = pl.BlockSpec((K, tm), lambda i, j: (0, i)) if ta else pl.BlockSpec((tm, K), lambda i, j: (i, 0))
    b_spec = pl.BlockSpec((tn, K), lambda i, j: (j, 0)) if tb else pl.BlockSpec((K, tn), lambda i, j: (0, j))
    o_spec = pl.BlockSpec((tm, tn), lambda i, j: (i, j))
    in_specs = [a_spec, b_spec] + ([o_spec] if add is not None else []) + [pl.BlockSpec(memory_space=pl.ANY)] * len(after)
    args = (a, b) + ((add,) if add is not None else ()) + tuple(after)
    return pl.pallas_call(
        body, grid=(M // tm, N // tn), in_specs=in_specs, out_specs=o_spec,
        out_shape=jax.ShapeDtypeStruct((M, N), out_dtype), name=name,
        compiler_params=_cparams(dimension_semantics=("parallel", "parallel")),
    )(*args)


def _cast_layer(w, layer, *, after=(), name):
    _, R, C = w.shape
    tr = max(t for t in range(16, R + 1, 16) if R % t == 0 and t * C <= 512 * 1024)

    def body(w_ref, *rest):
        o_ref = rest[-1]
        o_ref[...] = w_ref[...].astype(o_ref.dtype)

    return pl.pallas_call(
        body, grid=(R // tr,),
        in_specs=[pl.BlockSpec((None, tr, C), lambda i: (layer, i, 0))] + [pl.BlockSpec(memory_space=pl.ANY)] * len(after),
        out_specs=pl.BlockSpec((tr, C), lambda i: (i, 0)), out_shape=jax.ShapeDtypeStruct((R, C), _MXU_DTYPE), name=name,
        compiler_params=_cparams(dimension_semantics=("parallel",)),
    )(w, *after)


def _concat_cols(pieces, *, name):
    tm = 512
    widths = [p.shape[1] for p in pieces]
    offs = np.cumsum([0] + widths)

    def body(*refs):
        o_ref = refs[-1]
        for p_ref, off, w in zip(refs[:-1], offs, widths):
            o_ref[:, off:off + w] = p_ref[...].astype(o_ref.dtype)

    return pl.pallas_call(
        body, grid=(SEQ // tm,), in_specs=[pl.BlockSpec((tm, w), lambda i: (i, 0)) for w in widths],
        out_specs=pl.BlockSpec((tm, int(offs[-1])), lambda i: (i, 0)),
        out_shape=jax.ShapeDtypeStruct((SEQ, int(offs[-1])), _MXU_DTYPE), name=name,
        compiler_params=_cparams(dimension_semantics=("parallel",)),
    )(*pieces)


def _ln_fwd(x, y, g, b, *, name):
    tm = 256

    def body(x_ref, y_ref, g_ref, b_ref, z_ref, o_ref, ob_ref):
        z = ALPHA * x_ref[...] + y_ref[...]
        mu = jnp.mean(z, axis=-1, keepdims=True)
        zc = z - mu
        var = jnp.mean(zc * zc, axis=-1, keepdims=True)
        o = zc * lax.rsqrt(var + LN_EPS) * g_ref[...] + b_ref[...]
        z_ref[...] = z
        o_ref[...] = o
        ob_ref[...] = o.astype(ob_ref.dtype)

    row = pl.BlockSpec((tm, D_MODEL), lambda i: (i, 0))
    vec = pl.BlockSpec((1, D_MODEL), lambda i: (0, 0))
    return pl.pallas_call(
        body, grid=(SEQ // tm,), in_specs=[row, row, vec, vec], out_specs=[row, row, row],
        out_shape=[jax.ShapeDtypeStruct((SEQ, D_MODEL), _F32), jax.ShapeDtypeStruct((SEQ, D_MODEL), _F32),
                   jax.ShapeDtypeStruct((SEQ, D_MODEL), _MXU_DTYPE)],
        name=name, compiler_params=_cparams(dimension_semantics=("parallel",)),
    )(x, y, g.reshape(1, D_MODEL), b.reshape(1, D_MODEL))


def _ln_bwd(z, d_a, d_res, g, *, name):
    tm = 256

    def body(*refs):
        if d_res is None:
            z_ref, da_ref, g_ref, dz_ref, dzb_ref, dg_ref, db_ref = refs
        else:
            z_ref, da_ref, dr_ref, g_ref, dz_ref, dzb_ref, dg_ref, db_ref = refs

        @pl.when(pl.program_id(0) == 0)
        def _():
            dg_ref[...] = jnp.zeros_like(dg_ref)
            db_ref[...] = jnp.zeros_like(db_ref)

        dout = da_ref[...]
        if d_res is not None:
            dout = dout + ALPHA * dr_ref[...]
        z = z_ref[...]
        mu = jnp.mean(z, axis=-1, keepdims=True)
        zc = z - mu
        var = jnp.mean(zc * zc, axis=-1, keepdims=True)
        rstd = lax.rsqrt(var + LN_EPS)
        xh = zc * rstd
        dxh = dout * g_ref[...]
        m1 = jnp.mean(dxh, axis=-1, keepdims=True)
        m2 = jnp.mean(dxh * xh, axis=-1, keepdims=True)
        dz = rstd * (dxh - m1 - xh * m2)
        dz_ref[...] = dz
        dzb_ref[...] = dz.astype(dzb_ref.dtype)
        dg_ref[0:1, :] += jnp.sum(dout * xh, axis=0, keepdims=True)
        db_ref[0:1, :] += jnp.sum(dout, axis=0, keepdims=True)

    row = pl.BlockSpec((tm, D_MODEL), lambda i: (i, 0))
    vec = pl.BlockSpec((1, D_MODEL), lambda i: (0, 0))
    acc = pl.BlockSpec((8, D_MODEL), lambda i: (0, 0))
    ins = [z, d_a] + ([d_res] if d_res is not None else []) + [g.reshape(1, D_MODEL)]
    in_specs = [row, row] + ([row] if d_res is not None else []) + [vec]
    dz, dzb, dg, db = pl.pallas_call(
        body, grid=(SEQ // tm,), in_specs=in_specs, out_specs=[row, row, acc, acc],
        out_shape=[jax.ShapeDtypeStruct((SEQ, D_MODEL), _F32), jax.ShapeDtypeStruct((SEQ, D_MODEL), _MXU_DTYPE),
                   jax.ShapeDtypeStruct((8, D_MODEL), _F32), jax.ShapeDtypeStruct((8, D_MODEL), _F32)],
        name=name, compiler_params=_cparams(dimension_semantics=("arbitrary",)),
    )(*ins)
    return dz, dzb, dg[0], db[0]


def _ln_loss_bwd(x, y, g, b, target, *, name):
    tm = 256

    def body(x_ref, y_ref, g_ref, b_ref, t_ref, dz_ref, dzb_ref, dg_ref, db_ref, l_ref):
        @pl.when(pl.program_id(0) == 0)
        def _():
            dg_ref[...] = jnp.zeros_like(dg_ref)
            db_ref[...] = jnp.zeros_like(db_ref)
            l_ref[...] = jnp.zeros_like(l_ref)

        z = ALPHA * x_ref[...] + y_ref[...]
        mu = jnp.mean(z, axis=-1, keepdims=True)
        zc = z - mu
        var = jnp.mean(zc * zc, axis=-1, keepdims=True)
        rstd = lax.rsqrt(var + LN_EPS)
        xh = zc * rstd
        e = xh * g_ref[...] + b_ref[...] - t_ref[...]
        l_ref[...] += (0.5 / D_MODEL) * jnp.sum(e * e)
        dout = e * (1.0 / D_MODEL)
        dxh = dout * g_ref[...]
        m1 = jnp.mean(dxh, axis=-1, keepdims=True)
        m2 = jnp.mean(dxh * xh, axis=-1, keepdims=True)
        dz = rstd * (dxh - m1 - xh * m2)
        dz_ref[...] = dz
        dzb_ref[...] = dz.astype(dzb_ref.dtype)
        dg_ref[0:1, :] += jnp.sum(dout * xh, axis=0, keepdims=True)
        db_ref[0:1, :] += jnp.sum(dout, axis=0, keepdims=True)

    row = pl.BlockSpec((tm, D_MODEL), lambda i: (i, 0))
    vec = pl.BlockSpec((1, D_MODEL), lambda i: (0, 0))
    acc = pl.BlockSpec((8, D_MODEL), lambda i: (0, 0))
    dz, dzb, dg, db, part = pl.pallas_call(
        body, grid=(SEQ // tm,), in_specs=[row, row, vec, vec, row],
        out_specs=[row, row, acc, acc, pl.BlockSpec((8, 128), lambda i: (0, 0))],
        out_shape=[jax.ShapeDtypeStruct((SEQ, D_MODEL), _F32), jax.ShapeDtypeStruct((SEQ, D_MODEL), _MXU_DTYPE),
                   jax.ShapeDtypeStruct((8, D_MODEL), _F32), jax.ShapeDtypeStruct((8, D_MODEL), _F32),
                   jax.ShapeDtypeStruct((8, 128), _F32)],
        name=name, compiler_params=_cparams(dimension_semantics=("arbitrary",)),
    )(x, y, g.reshape(1, D_MODEL), b.reshape(1, D_MODEL), target)
    return part[0, 0], dz, dzb, dg[0], db[0]


_CONV_TN = 256


def _shift_down(v, k, rows):
    return jnp.where(rows >= k, pltpu.roll(v, k, axis=0), 0.0)


def _shift_up(v, k, rows):
    return jnp.where(rows < SEQ - k, pltpu.roll(v, SEQ - k, axis=0), 0.0)


def _conv_gate_fwd(g, u, conv_w, conv_b, *, name):
    def body(g_ref, u_ref, w_ref, b_ref, h_ref):
        gv = g_ref[...].astype(_F32)
        rows = lax.broadcasted_iota(jnp.int32, gv.shape, 0)
        w = w_ref[...]
        gc = b_ref[...] + w[2:3, :] * gv + w[1:2, :] * _shift_down(gv, 1, rows) + w[0:1, :] * _shift_down(gv, 2, rows)
        h_ref[...] = (gc * _sigmoid(gc) * u_ref[...].astype(_F32)).astype(h_ref.dtype)

    col = pl.BlockSpec((SEQ, _CONV_TN), lambda j: (0, j))
    return pl.pallas_call(
        body, grid=(D_FF // _CONV_TN,),
        in_specs=[col, col, pl.BlockSpec((3, _CONV_TN), lambda j: (0, j)), pl.BlockSpec((1, _CONV_TN), lambda j: (0, j))],
        out_specs=col, out_shape=jax.ShapeDtypeStruct((SEQ, D_FF), _MXU_DTYPE), name=name,
        compiler_params=_cparams(dimension_semantics=("parallel",)),
    )(g, u, conv_w, conv_b.reshape(1, D_FF))


def _conv_gate_bwd(dh, g, u, conv_w, conv_b, *, name):
    def body(dh_ref, g_ref, u_ref, w_ref, b_ref, dg_ref, du_ref, dw_ref, db_ref):
        gv = g_ref[...].astype(_F32)
        rows = lax.broadcasted_iota(jnp.int32, gv.shape, 0)
        w = w_ref[...]
        g1 = _shift_down(gv, 1, rows)
        g2 = _shift_down(gv, 2, rows)
        gc = b_ref[...] + w[2:3, :] * gv + w[1:2, :] * g1 + w[0:1, :] * g2
        sg = _sigmoid(gc)
        dh = dh_ref[...].astype(_F32)
        du_ref[...] = (dh * (gc * sg)).astype(du_ref.dtype)
        dgc = dh * u_ref[...].astype(_F32) * (sg * (1.0 + gc * (1.0 - sg)))
        dg = w[2:3, :] * dgc + w[1:2, :] * _shift_up(dgc, 1, rows) + w[0:1, :] * _shift_up(dgc, 2, rows)
        dg_ref[...] = dg.astype(dg_ref.dtype)
        dw_ref[0:1, :] = jnp.sum(dgc * g2, axis=0, keepdims=True)
        dw_ref[1:2, :] = jnp.sum(dgc * g1, axis=0, keepdims=True)
        dw_ref[2:3, :] = jnp.sum(dgc * gv, axis=0, keepdims=True)
        db_ref[...] = jnp.sum(dgc, axis=0, keepdims=True)

    col = pl.BlockSpec((SEQ, _CONV_TN), lambda j: (0, j))
    w3 = pl.BlockSpec((3, _CONV_TN), lambda j: (0, j))
    w1 = pl.BlockSpec((1, _CONV_TN), lambda j: (0, j))
    dg, du, dw, db = pl.pallas_call(
        body, grid=(D_FF // _CONV_TN,), in_specs=[col, col, col, w3, w1], out_specs=[col, col, w3, w1],
        out_shape=[jax.ShapeDtypeStruct((SEQ, D_FF), _MXU_DTYPE), jax.ShapeDtypeStruct((SEQ, D_FF), _MXU_DTYPE),
                   jax.ShapeDtypeStruct((3, D_FF), _F32), jax.ShapeDtypeStruct((1, D_FF), _F32)],
        name=name, compiler_params=_cparams(dimension_semantics=("parallel",)),
    )(dh, g, u, conv_w, conv_b.reshape(1, D_FF))
    return dg, du, dw, db[0]


def _rope_tables():
    half = ROPE_DIM // 2
    inv = ROPE_THETA ** (-jnp.arange(0, ROPE_DIM, 2, dtype=_F32) / ROPE_DIM)
    ang = jnp.arange(SEQ, dtype=_F32)[:, None] * inv[None, :]
    cos, sin = jnp.cos(ang), jnp.sin(ang)
    rest = HEAD_DIM - ROPE_DIM
    c = jnp.concatenate([cos, cos, jnp.ones((SEQ, rest), _F32)], axis=1)
    s1 = jnp.concatenate([-sin, jnp.zeros((SEQ, HEAD_DIM - half), _F32)], axis=1)
    s2 = jnp.concatenate([jnp.zeros((SEQ, half), _F32), sin, jnp.zeros((SEQ, rest), _F32)], axis=1)
    return c, s1, s2


def _rope_apply(x, c, s1, s2):
    return x * c + pltpu.roll(x, HEAD_DIM - ROPE_DIM // 2, axis=1) * s1 + pltpu.roll(x, ROPE_DIM // 2, axis=1) * s2


def _rope_transpose(d, c, s1, s2):
    half = ROPE_DIM // 2
    return d * c + pltpu.roll(d * s1, half, axis=1) + pltpu.roll(d * s2, HEAD_DIM - half, axis=1)


_NT = (((1,), (1,)), ((), ()))
_TN = (((0,), (0,)), ((), ()))
_SCALE = HEAD_DIM ** -0.5


def _band_scores(q, k2, n, lag_off):
    s = lax.dot_general(q, k2, _NT, preferred_element_type=_F32) * _SCALE
    row = lax.broadcasted_iota(jnp.int32, (BLOCK, 2 * BLOCK), 0)
    col = lax.broadcasted_iota(jnp.int32, (BLOCK, 2 * BLOCK), 1)
    front = (col >= row + lag_off) & (col < BLOCK) & (n > 0)
    own = (col >= BLOCK) & (col <= row + BLOCK)
    return jnp.where(front | own, s, _NEG)


_BAND_STEPS = SEQ // BLOCK


def _rows(start, d):
    if d == 1:
        return pl.ds(pl.multiple_of(start, BLOCK), BLOCK)
    return pl.ds(start, BLOCK, stride=d)


def _band_block(it, d):
    r, n = it % d, it // d
    span = BLOCK * d
    return n, _rows(r + n * span, d), _rows(r + jnp.maximum(n - 1, 0) * span, d)


def _band_fwd(proj, tabs, *, kv_heads, q_per_kv, q0, k0, v0, dilations, lag_off, sink, name):
    heads = kv_heads * q_per_kv

    def body(*refs):
        q_refs = refs[:q_per_kv]
        k_ref, v_ref, c_ref, s1_ref, s2_ref = refs[q_per_kv:q_per_kv + 5]
        rest = refs[q_per_kv + 5:]
        if sink is not None:
            sk_ref, rest = rest[0], rest[1:]
        o_ref, lse_ref, qs, ks, m_s, l_s, acc_s = rest
        c, s1, s2 = c_ref[...], s1_ref[...], s2_ref[...]
        ks[...] = _rope_apply(k_ref[...], c, s1, s2)
        for i in range(q_per_kv):
            qs[...] = _rope_apply(q_refs[i][...], c, s1, s2)
            for pi, d in enumerate(dilations):
                def step(it, carry, d=d, first=(pi == 0)):
                    n, cur, prev = _band_block(it, d)
                    q = qs[cur, :].astype(_MXU_DTYPE)
                    k2 = jnp.concatenate([ks[prev, :], ks[cur, :]], axis=0).astype(_MXU_DTYPE)
                    v2 = jnp.concatenate([v_ref[prev, :], v_ref[cur, :]], axis=0).astype(_MXU_DTYPE)
                    s = _band_scores(q, k2, n, lag_off)
                    m_b = jnp.max(s, axis=1, keepdims=True)
                    m_new = m_b if first else jnp.maximum(m_b, m_s[cur, :][:, 0:1])
                    p = jnp.exp(s - m_new)
                    l_new = jnp.sum(p, axis=1, keepdims=True)
                    acc = jnp.dot(p.astype(_MXU_DTYPE), v2, preferred_element_type=_F32)
                    if not first:
                        a = jnp.exp(m_s[cur, :][:, 0:1] - m_new)
                        l_new = l_new + a * l_s[cur, :][:, 0:1]
                        acc = acc + a * acc_s[cur, :]
                    m_s[cur, :] = jnp.broadcast_to(m_new, (BLOCK, HEAD_DIM))
                    l_s[cur, :] = jnp.broadcast_to(l_new, (BLOCK, HEAD_DIM))
                    acc_s[cur, :] = acc
                    return carry

                lax.fori_loop(0, _BAND_STEPS, step, 0, unroll=16)
            m, den = m_s[...], l_s[...]
            if sink is not None:
                sk = sk_ref[i]
                m_f = jnp.maximum(m, sk)
                a = jnp.exp(m - m_f)
                den = den * a + jnp.exp(sk - m_f)
                o = acc_s[...] * a / den
                m = m_f
            else:
                o = acc_s[...] / den
            o_ref[:, i * HEAD_DIM:(i + 1) * HEAD_DIM] = o
            lse_ref[:, i * HEAD_DIM:(i + 1) * HEAD_DIM] = m + jnp.log(den)

    col = (SEQ, HEAD_DIM)
    in_specs = [pl.BlockSpec(col, functools.partial(lambda g, i: (0, A_COLS + q0 + g * q_per_kv + i), i=i)) for i in range(q_per_kv)]
    in_specs += [pl.BlockSpec(col, lambda g: (0, A_COLS + k0 + g)), pl.BlockSpec(col, lambda g: (0, A_COLS + v0 + g))]
    in_specs += [pl.BlockSpec(col, lambda g: (0, 0))] * 3
    args = [proj] * (q_per_kv + 2) + list(tabs)
    if sink is not None:
        in_specs.append(pl.BlockSpec((q_per_kv, 1, HEAD_DIM), lambda g: (g, 0, 0)))
        args.append(jnp.broadcast_to(sink.reshape(heads, 1, 1), (heads, 1, HEAD_DIM)))
    o_spec = pl.BlockSpec((SEQ, q_per_kv * HEAD_DIM), lambda g: (0, g))
    shape = jax.ShapeDtypeStruct((SEQ, heads * HEAD_DIM), _F32)
    return pl.pallas_call(
        body, grid=(kv_heads,), in_specs=in_specs, out_specs=[o_spec, o_spec], out_shape=[shape, shape],
        scratch_shapes=[pltpu.VMEM(col, _F32)] * 5, name=name,
        compiler_params=_cparams(dimension_semantics=("parallel",)),
    )(*args)


def _band_bwd(proj, tabs, dmixed, o, lse, *, kv_heads, q_per_kv, q0, k0, v0, do0, dilations, lag_off, sink, after=(), name):
    heads = kv_heads * q_per_kv

    def body(*refs):
        q_refs = refs[:q_per_kv]
        k_ref, v_ref, c_ref, s1_ref, s2_ref = refs[q_per_kv:q_per_kv + 5]
        do_refs = refs[q_per_kv + 5:2 * q_per_kv + 5]
        o_ref, lse_ref = refs[2 * q_per_kv + 5:2 * q_per_kv + 7]
        rest = refs[2 * q_per_kv + 7:]
        if sink is not None:
            sk_ref, rest = rest[0], rest[1:]
            dq_ref, dk_ref, dv_ref, dsk_ref, qs, ks, dq_s, dk_s, dv_s = rest[len(after):]
        else:
            dq_ref, dk_ref, dv_ref, qs, ks, dq_s, dk_s, dv_s = rest[len(after):]
        c, s1, s2 = c_ref[...], s1_ref[...], s2_ref[...]
        ks[...] = _rope_apply(k_ref[...], c, s1, s2)
        dk_s[...] = jnp.zeros_like(dk_s)
        dv_s[...] = jnp.zeros_like(dv_s)
        for i in range(q_per_kv):
            hs = slice(i * HEAD_DIM, (i + 1) * HEAD_DIM)
            qs[...] = _rope_apply(q_refs[i][...], c, s1, s2)
            dq_s[...] = jnp.zeros_like(dq_s)
            do_ref = do_refs[i]
            for d in dilations:
                def step(it, carry, d=d, do_ref=do_ref, hs=hs):
                    n, cur, prev = _band_block(it, d)
                    q = qs[cur, :].astype(_MXU_DTYPE)
                    k2 = jnp.concatenate([ks[prev, :], ks[cur, :]], axis=0).astype(_MXU_DTYPE)
                    v2 = jnp.concatenate([v_ref[prev, :], v_ref[cur, :]], axis=0).astype(_MXU_DTYPE)
                    do = do_ref[cur, :]
                    delta = jnp.sum(do * o_ref[cur, hs], axis=1, keepdims=True)
                    lse_c = lse_ref[cur, hs][:, 0:1]
                    p = jnp.exp(_band_scores(q, k2, n, lag_off) - lse_c)
                    dob = do.astype(_MXU_DTYPE)
                    ds = (p * (lax.dot_general(dob, v2, _NT, preferred_element_type=_F32) - delta) * _SCALE).astype(_MXU_DTYPE)
                    dq_s[cur, :] += jnp.dot(ds, k2, preferred_element_type=_F32)
                    dk2 = lax.dot_general(ds, q, _TN, preferred_element_type=_F32)
                    dv2 = lax.dot_general(p.astype(_MXU_DTYPE), dob, _TN, preferred_element_type=_F32)
                    dk_s[prev, :] += dk2[:BLOCK]
                    dv_s[prev, :] += dv2[:BLOCK]
                    dk_s[cur, :] += dk2[BLOCK:]
                    dv_s[cur, :] += dv2[BLOCK:]
                    return carry

                lax.fori_loop(0, _BAND_STEPS, step, 0, unroll=16)
            dq_ref[:, hs] = _rope_transpose(dq_s[...], c, s1, s2).astype(dq_ref.dtype)
            if sink is not None:
                delta = jnp.sum(do_ref[...] * o_ref[:, hs], axis=1, keepdims=True)
                w_sink = jnp.exp(sk_ref[i] - lse_ref[:, hs])
                dsk_ref[i] = jnp.broadcast_to(jnp.sum(-delta * w_sink[:, 0:1]), (8, HEAD_DIM))
        dk_ref[...] = _rope_transpose(dk_s[...], c, s1, s2).astype(dk_ref.dtype)
        dv_ref[...] = dv_s[...].astype(dv_ref.dtype)

    col = (SEQ, HEAD_DIM)
    in_specs = [pl.BlockSpec(col, functools.partial(lambda g, i: (0, A_COLS + q0 + g * q_per_kv + i), i=i)) for i in range(q_per_kv)]
    in_specs += [pl.BlockSpec(col, lambda g: (0, A_COLS + k0 + g)), pl.BlockSpec(col, lambda g: (0, A_COLS + v0 + g))]
    in_specs += [pl.BlockSpec(col, lambda g: (0, 0))] * 3
    in_specs += [pl.BlockSpec(col, functools.partial(lambda g, i: (0, do0 + g * q_per_kv + i), i=i)) for i in range(q_per_kv)]
    wide = pl.BlockSpec((SEQ, q_per_kv * HEAD_DIM), lambda g: (0, g))
    in_specs += [wide, wide]
    args = [proj] * (q_per_kv + 2) + list(tabs) + [dmixed] * q_per_kv + [o, lse]
    out_specs = [wide, pl.BlockSpec(col, lambda g: (0, g)), pl.BlockSpec(col, lambda g: (0, g))]
    out_shape = [jax.ShapeDtypeStruct((SEQ, heads * HEAD_DIM), _MXU_DTYPE), jax.ShapeDtypeStruct((SEQ, kv_heads * HEAD_DIM), _MXU_DTYPE),
                 jax.ShapeDtypeStruct((SEQ, kv_heads * HEAD_DIM), _MXU_DTYPE)]
    if sink is not None:
        in_specs.append(pl.BlockSpec((q_per_kv, 1, HEAD_DIM), lambda g: (g, 0, 0)))
        args.append(jnp.broadcast_to(sink.reshape(heads, 1, 1), (heads, 1, HEAD_DIM)))
        out_specs.append(pl.BlockSpec((q_per_kv, 8, HEAD_DIM), lambda g: (g, 0, 0)))
        out_shape.append(jax.ShapeDtypeStruct((heads, 8, HEAD_DIM), _F32))
    in_specs += [pl.BlockSpec(memory_space=pl.ANY)] * len(after)
    args += list(after)
    res = pl.pallas_call(
        body, grid=(kv_heads,), in_specs=in_specs, out_specs=out_specs, out_shape=out_shape,
        scratch_shapes=[pltpu.VMEM(col, _F32)] * 5, name=name,
        compiler_params=_cparams(dimension_semantics=("parallel",)),
    )(*args)
    if sink is not None:
        return res[0], res[1], res[2], res[3][:, 0, 0]
    return res


_DILATED = dict(kv_heads=B_HEADS, q_per_kv=1, q0=QB0, k0=KB0, v0=VB0, dilations=DILATIONS, lag_off=0, sink=None)
_SWA = dict(kv_heads=C_KV_HEADS, q_per_kv=C_HEADS // C_KV_HEADS, q0=QC0, k0=KC0, v0=VC0, dilations=(1,), lag_off=1)


_HG_TILE = 128
_HG_CHUNKS = _HG_TILE // A_CHUNK
_HG_TILES = SEQ // _HG_TILE
_HI = lax.Precision.HIGHEST


def _chunk_tri():
    i = np.arange(_HG_TILE)
    return jnp.asarray(((i[:, None] // A_CHUNK == i[None, :] // A_CHUNK) & (i[None, :] <= i[:, None])).astype(np.float32))


def _layer_lb(lb_ref, layer):
    if layer == 0:
        return jnp.zeros((1, HEAD_DIM), _F32)
    lg = lb_ref[...]
    m = jnp.max(lg, axis=0, keepdims=True)
    e = jnp.exp(lg - m)
    return e[1:2, :] / jnp.sum(e, axis=0, keepdims=True)


def _hgrn_gates(q, fr, lb):
    sgq = _sigmoid(q)
    sg = _sigmoid(fr)
    f = lb + (1.0 - lb) * sg
    return sgq, q * sgq, sg, f, 1.0 - f


def _hgrn_fwd(proj, lb_logits, norm_w, layer, *, name):
    tri = _chunk_tri()

    def body(q_ref, f_ref, i_ref, g_ref, lb_ref, nw_ref, tri_ref, o_ref, raw_ref, st_ref, state):
        @pl.when(pl.program_id(1) == 0)
        def _():
            state[...] = jnp.zeros_like(state)

        lb = _layer_lb(lb_ref, layer)
        _, qs, _, f, k = _hgrn_gates(q_ref[...], f_ref[...], lb)
        v = i_ref[...]
        b = jnp.dot(tri_ref[...], jnp.log(f), precision=_HI, preferred_element_type=_F32)
        eb = jnp.exp(b)
        ridx = lax.broadcasted_iota(jnp.int32, (A_CHUNK, HEAD_DIM), 0)
        outs = []
        st = state[...]
        for c in range(_HG_CHUNKS):
            sl = slice(c * A_CHUNK, (c + 1) * A_CHUNK)
            bc, qc, kc, vc = b[sl], qs[sl], k[sl], v[sl]
            bl = bc[A_CHUNK - 1:A_CHUNK]
            st_ref[0, c] = st
            o_c = lax.dot_general((qc * eb[sl]).astype(_MXU_DTYPE), st.astype(_MXU_DTYPE), _NT, preferred_element_type=_F32)
            rows = []
            for i in range(A_CHUNK):
                di = jnp.exp(jnp.where(ridx <= i, bc[i:i + 1] - bc, _NEG))
                a = jnp.sum(qc[i:i + 1] * kc * di, axis=1, keepdims=True)
                rows.append(jnp.sum(a * vc, axis=0, keepdims=True))
            outs.append(o_c + jnp.concatenate(rows, axis=0))
            kt = (kc * jnp.exp(bl - bc)).astype(_MXU_DTYPE)
            st = st * jnp.exp(bl) + lax.dot_general(vc.astype(_MXU_DTYPE), kt, _TN, preferred_element_type=_F32)
        state[...] = st
        o = jnp.concatenate(outs, axis=0)
        raw_ref[...] = o
        r = lax.rsqrt(jnp.mean(o * o, axis=-1, keepdims=True) + LN_EPS)
        g = g_ref[...]
        o_ref[...] = o * r * nw_ref[...] * (g * _sigmoid(g))

    blk = (_HG_TILE, HEAD_DIM)

    def col(base):
        return pl.BlockSpec(blk, lambda h, t: (t, base + h))

    o_spec = pl.BlockSpec(blk, lambda h, t: (t, h))
    o_shape = jax.ShapeDtypeStruct((SEQ, A_HEADS * HEAD_DIM), _F32)
    return pl.pallas_call(
        body, grid=(A_HEADS, _HG_TILES),
        in_specs=[col(0), col(4), col(8), col(12), pl.BlockSpec((DEPTH, HEAD_DIM), lambda h, t: (0, h)),
                  pl.BlockSpec((1, HEAD_DIM), lambda h, t: (0, 0)), pl.BlockSpec((_HG_TILE, _HG_TILE), lambda h, t: (0, 0))],
        out_specs=[o_spec, o_spec, pl.BlockSpec((1, _HG_CHUNKS, HEAD_DIM, HEAD_DIM), lambda h, t: (h, t, 0, 0))],
        out_shape=[o_shape, o_shape, jax.ShapeDtypeStruct((A_HEADS, SEQ // A_CHUNK, HEAD_DIM, HEAD_DIM), _F32)],
        scratch_shapes=[pltpu.VMEM((HEAD_DIM, HEAD_DIM), _F32)], name=name,
        compiler_params=_cparams(dimension_semantics=("parallel", "arbitrary")),
    )(proj, proj, proj, proj, lb_logits, norm_w.reshape(1, HEAD_DIM), tri)


def _hgrn_bwd(proj, lb_logits, norm_w, raw, states, dmixed, layer, *, name):
    tri = _chunk_tri()
    triu = tri.T

    def body(q_ref, f_ref, i_ref, g_ref, lb_ref, nw_ref, tri_ref, triu_ref, raw_ref, do_ref, st_ref,
             dq_ref, df_ref, di_ref, dg_ref, dnw_ref, dlb_ref, dstate):
        @pl.when(pl.program_id(1) == 0)
        def _():
            dstate[...] = jnp.zeros_like(dstate)
            dlb_ref[...] = jnp.zeros_like(dlb_ref)

        @pl.when((pl.program_id(0) == 0) & (pl.program_id(1) == 0))
        def _():
            dnw_ref[...] = jnp.zeros_like(dnw_ref)

        lb = _layer_lb(lb_ref, layer)
        q = q_ref[...]
        sgq, qs, sg, f, k = _hgrn_gates(q, f_ref[...], lb)
        v = i_ref[...]
        b = jnp.dot(tri_ref[...], jnp.log(f), precision=_HI, preferred_element_type=_F32)
        eb = jnp.exp(b)
        g = g_ref[...]
        nw = nw_ref[...]
        o = raw_ref[...]
        dout = do_ref[...]
        sgg = _sigmoid(g)
        r = lax.rsqrt(jnp.mean(o * o, axis=-1, keepdims=True) + LN_EPS)
        dg_ref[...] = (dout * (o * r * nw) * (sgg * (1.0 + g * (1.0 - sgg)))).astype(dg_ref.dtype)
        don = dout * (g * sgg)
        dnw_ref[0:1, :] += jnp.sum(don * o * r, axis=0, keepdims=True)
        dy = don * nw
        do_raw = r * dy - o * (r * r * r) * jnp.mean(o * dy, axis=-1, keepdims=True)

        ridx = lax.broadcasted_iota(jnp.int32, (A_CHUNK, HEAD_DIM), 0)
        dqs_t, dk_t, db_t, dv_t = [None] * _HG_CHUNKS, [None] * _HG_CHUNKS, [None] * _HG_CHUNKS, [None] * _HG_CHUNKS
        dst = dstate[...]
        for c in reversed(range(_HG_CHUNKS)):
            sl = slice(c * A_CHUNK, (c + 1) * A_CHUNK)
            bc, qc, kc, vc, doc = b[sl], qs[sl], k[sl], v[sl], do_raw[sl]
            bl = bc[A_CHUNK - 1:A_CHUNK]
            ebc = eb[sl]
            ebl = jnp.exp(bl - bc)
            lam = jnp.exp(bl)
            qt = qc * ebc
            kt = kc * ebl
            stp = st_ref[0, c]
            dob = doc.astype(_MXU_DTYPE)
            dstb = dst.astype(_MXU_DTYPE)
            dqt = jnp.dot(dob, stp.astype(_MXU_DTYPE), preferred_element_type=_F32)
            dkt = jnp.dot(vc.astype(_MXU_DTYPE), dstb, preferred_element_type=_F32)
            dv = lax.dot_general(kt.astype(_MXU_DTYPE), dstb, _NT, preferred_element_type=_F32)
            dlam = jnp.sum(stp * dst, axis=0, keepdims=True)
            dst = dst * lam + lax.dot_general(dob, qt.astype(_MXU_DTYPE), _TN, preferred_element_type=_F32)
            dqs_rows = []
            dk_in = jnp.zeros((A_CHUNK, HEAD_DIM), _F32)
            for i in range(A_CHUNK):
                di = jnp.exp(jnp.where(ridx <= i, bc[i:i + 1] - bc, _NEG))
                qi = qc[i:i + 1]
                doi = doc[i:i + 1]
                w = kc * di
                a = jnp.sum(qi * w, axis=1, keepdims=True)
                dv = dv + a * doi
                da = jnp.sum(doi * vc, axis=1, keepdims=True)
                dqs_rows.append(jnp.sum(da * w, axis=0, keepdims=True))
                dk_in = dk_in + da * (qi * di)
            dqs_in = jnp.concatenate(dqs_rows, axis=0)
            dbl = jnp.sum(dkt * kt, axis=0, keepdims=True) + dlam * lam
            db = qc * dqs_in - kc * dk_in + dqt * qt - dkt * kt
            db_t[c] = db + jnp.where(ridx == A_CHUNK - 1, dbl, 0.0)
            dqs_t[c] = dqs_in + dqt * ebc
            dk_t[c] = dk_in + dkt * ebl
            dv_t[c] = dv
        dstate[...] = dst
        dqs = jnp.concatenate(dqs_t, axis=0)
        dk = jnp.concatenate(dk_t, axis=0)
        db = jnp.concatenate(db_t, axis=0)
        di_ref[...] = jnp.concatenate(dv_t, axis=0).astype(di_ref.dtype)
        dlogf = jnp.dot(triu_ref[...], db, precision=_HI, preferred_element_type=_F32)
        df = dlogf / f - dk
        df_ref[...] = (df * (1.0 - lb) * sg * (1.0 - sg)).astype(df_ref.dtype)
        dlb_ref[0, 0:1, :] += jnp.sum(df * (1.0 - sg), axis=0, keepdims=True)
        dq_ref[...] = (dqs * (sgq * (1.0 + q * (1.0 - sgq)))).astype(dq_ref.dtype)

    blk = (_HG_TILE, HEAD_DIM)
    last = _HG_TILES - 1

    def col(base):
        return pl.BlockSpec(blk, lambda h, t: (last - t, base + h))

    tri_spec = pl.BlockSpec((_HG_TILE, _HG_TILE), lambda h, t: (0, 0))
    acc_spec = pl.BlockSpec((1, 8, HEAD_DIM), lambda h, t: (h, 0, 0))
    acc_shape = jax.ShapeDtypeStruct((A_HEADS, 8, HEAD_DIM), _F32)
    dq, df, di, dg, dnw, dlb = pl.pallas_call(
        body, grid=(A_HEADS, _HG_TILES),
        in_specs=[col(0), col(4), col(8), col(12), pl.BlockSpec((DEPTH, HEAD_DIM), lambda h, t: (0, h)),
                  pl.BlockSpec((1, HEAD_DIM), lambda h, t: (0, 0)), tri_spec, tri_spec, col(0), col(0),
                  pl.BlockSpec((1, _HG_CHUNKS, HEAD_DIM, HEAD_DIM), lambda h, t: (h, last - t, 0, 0))],
        out_specs=[col(0), col(0), col(0), col(0), pl.BlockSpec((8, HEAD_DIM), lambda h, t: (0, 0)), acc_spec],
        out_shape=[jax.ShapeDtypeStruct((SEQ, A_HEADS * HEAD_DIM), _MXU_DTYPE)] * 4
        + [jax.ShapeDtypeStruct((8, HEAD_DIM), _F32), acc_shape],
        scratch_shapes=[pltpu.VMEM((HEAD_DIM, HEAD_DIM), _F32)], name=name,
        compiler_params=_cparams(dimension_semantics=("arbitrary", "arbitrary")),
    )(proj, proj, proj, proj, lb_logits, norm_w.reshape(1, HEAD_DIM), tri, triu, raw, dmixed, states)
    return dq, df, di, dg, dnw[0], dlb[:, 0, :].reshape(A_HEADS * HEAD_DIM)


N_CHIP = N_DEV // 2
_MESH_ID = pl.DeviceIdType.MESH


def _place():
    x, y, c = lax.axis_index("x"), lax.axis_index("y"), lax.axis_index("c")
    chips = [(1 - x, y), (x, 1 - y), (1 - x, 1 - y)]
    return x, y, c, 2 * x + y, chips


def _sibling_swap(arrays, *, name):
    n = len(arrays)

    def body(*refs):
        ins, outs = refs[:n], refs[n:2 * n]
        send_sems, recv_sems = refs[2 * n:]
        x, y, c, _, _ = _place()
        copies = [pltpu.make_async_remote_copy(
            src_ref=ins[a].at[:, 1 - c], dst_ref=outs[a], send_sem=send_sems.at[a], recv_sem=recv_sems.at[a],
            device_id=(x, y, 1 - c), device_id_type=_MESH_ID) for a in range(n)]
        for cp in copies:
            cp.start()
        for cp in copies:
            cp.wait()

    any_spec = pl.BlockSpec(memory_space=pl.ANY)
    return pl.pallas_call(
        body, in_specs=[any_spec] * n, out_specs=[any_spec] * n,
        out_shape=[jax.ShapeDtypeStruct((N_CHIP,) + a.shape[2:], a.dtype) for a in arrays],
        scratch_shapes=[pltpu.SemaphoreType.DMA((n,)), pltpu.SemaphoreType.DMA((n,))],
        name=name, compiler_params=pltpu.CompilerParams(has_side_effects=True),
    )(*arrays)


def _pair_add(mine, theirs, core, *, name):
    _, _, R, C = mine.shape
    tr = max(t for t in range(16, R + 1, 16) if R % t == 0 and t * C <= 512 * 1024)

    def body(core_ref, m_ref, t_ref, o_ref):
        del core_ref
        o_ref[...] = (m_ref[...].astype(_F32) + t_ref[...].astype(_F32)).astype(o_ref.dtype)

    grid_spec = pltpu.PrefetchScalarGridSpec(
        num_scalar_prefetch=1, grid=(N_CHIP, R // tr),
        in_specs=[pl.BlockSpec((None, None, tr, C), lambda q, i, core: (q, core[0], i, 0)),
                  pl.BlockSpec((None, tr, C), lambda q, i, core: (q, i, 0))],
        out_specs=pl.BlockSpec((None, tr, C), lambda q, i, core: (q, i, 0)))
    return pl.pallas_call(
        body, grid_spec=grid_spec, out_shape=jax.ShapeDtypeStruct((N_CHIP, R, C), mine.dtype), name=name,
        compiler_params=_cparams(dimension_semantics=("parallel", "parallel")),
    )(core.reshape(1), mine, theirs)


_HBM = pl.BlockSpec(memory_space=pltpu.HBM)
_SEM = pl.BlockSpec(memory_space=pltpu.SEMAPHORE)
_TOKEN = pl.BlockSpec(memory_space=pltpu.VMEM)
_DATAFLOW = pltpu.SideEffectType.DATAFLOW_SIDE_EFFECTING


def _hbm(a):
    return pltpu.HBM(a.shape, a.dtype)


def _token_shape():
    return jax.ShapeDtypeStruct((8, 128), _F32)


def _dev_slot(px, py, pc):
    return 4 * px + 2 * py + pc


def _gather_start(blocks, landings, *, name):
    n = len(blocks)

    def body(*refs):
        ins, lands = refs[:n], refs[n:2 * n]
        send_sems, d2d_sems, ici_sems = refs[2 * n:2 * n + 3]
        token = refs[-1]
        x, y, c, _, chips = _place()
        for a in range(n):
            dst = lands[a].at[_dev_slot(x, y, c)]
            pltpu.make_async_remote_copy(src_ref=ins[a], dst_ref=dst, send_sem=send_sems.at[4 * a], recv_sem=d2d_sems.at[a],
                                         device_id=(x, y, 1 - c), device_id_type=_MESH_ID).start()
            for j, chip in enumerate(chips):
                pltpu.make_async_remote_copy(src_ref=ins[a], dst_ref=dst, send_sem=send_sems.at[4 * a + 1 + j],
                                             recv_sem=ici_sems.at[3 * a + j], device_id=(*chip, c),
                                             device_id_type=_MESH_ID).start()
        token[...] = jnp.zeros_like(token)

    res = pl.pallas_call(
        body, name=name, in_specs=[_HBM] * (2 * n),
        out_shape=(pltpu.SemaphoreType.DMA((4 * n,)), pltpu.SemaphoreType.DMA((n,)), pltpu.SemaphoreType.DMA((3 * n,)),
                   *[_hbm(b) for b in blocks], *[_hbm(b) for b in landings], _token_shape()),
        out_specs=(_SEM, _SEM, _SEM, *[_HBM] * (2 * n), _TOKEN),
        input_output_aliases={i: 3 + i for i in range(2 * n)},
        compiler_params=pltpu.CompilerParams(has_side_effects=_DATAFLOW),
    )(*[pltpu.with_memory_space_constraint(b, pltpu.HBM) for b in blocks],
      *[pltpu.with_memory_space_constraint(b, pltpu.HBM) for b in landings])
    return res[0], res[1], res[2], list(res[3:3 + n]), list(res[3 + n:3 + 2 * n]), res[-1]


def _gather_forward(landings, ici_sems, first, after, *, name):
    n = len(landings)

    def body(*refs):
        lands = refs[:n]
        ici = refs[n]
        f_send, f_recv = refs[n + 2], refs[n + 3]
        token = refs[-1]
        x, y, c, _, chips = _place()
        for a in range(n):
            for j, chip in enumerate(chips):
                blk = lands[a].at[_dev_slot(*chip, c)]
                pltpu.make_async_remote_copy(src_ref=blk, dst_ref=blk, send_sem=f_send.at[3 * a + j],
                                             recv_sem=ici.at[3 * (first + a) + j], device_id=(*chip, c),
                                             device_id_type=_MESH_ID).wait_recv()
                pltpu.make_async_remote_copy(src_ref=blk, dst_ref=blk, send_sem=f_send.at[3 * a + j], recv_sem=f_recv.at[3 * a + j],
                                             device_id=(x, y, 1 - c), device_id_type=_MESH_ID).start()
        token[...] = jnp.zeros_like(token)

    res = pl.pallas_call(
        body, name=name, in_specs=[_HBM] * n + [_SEM, pl.BlockSpec(memory_space=pl.ANY)],
        out_shape=(pltpu.SemaphoreType.DMA((3 * n,)), pltpu.SemaphoreType.DMA((3 * n,)), *[_hbm(b) for b in landings], _token_shape()),
        out_specs=(_SEM, _SEM, *[_HBM] * n, _TOKEN),
        input_output_aliases={i: 2 + i for i in range(n)},
        compiler_params=pltpu.CompilerParams(has_side_effects=_DATAFLOW),
    )(*landings, ici_sems, after)
    return res[0], res[1], list(res[2:2 + n]), res[-1]


def _gather_wait(blocks, landings, send_sems, d2d_sems, first, f_send, f_recv, after, *, name):
    n = len(landings)

    def body(*refs):
        ins, lands = refs[:n], refs[n:2 * n]
        send, d2d, fs, fr = refs[2 * n:2 * n + 4]
        x, y, c, _, chips = _place()
        me = (x, y, c)
        for a in range(n):
            own = lands[a].at[_dev_slot(x, y, 1 - c)]
            g = first + a
            pltpu.make_async_remote_copy(src_ref=ins[a], dst_ref=own, send_sem=send.at[4 * g], recv_sem=d2d.at[g],
                                         device_id=me, device_id_type=_MESH_ID).wait_recv()
            for j, chip in enumerate(chips):
                blk = lands[a].at[_dev_slot(*chip, 1 - c)]
                pltpu.make_async_remote_copy(src_ref=blk, dst_ref=blk, send_sem=fs.at[3 * a + j], recv_sem=fr.at[3 * a + j],
                                             device_id=me, device_id_type=_MESH_ID).wait_recv()
            for k in range(4):
                pltpu.make_async_remote_copy(src_ref=ins[a], dst_ref=own, send_sem=send.at[4 * g + k], recv_sem=d2d.at[g],
                                             device_id=me, device_id_type=_MESH_ID).wait_send()
            for j in range(3):
                pltpu.make_async_remote_copy(src_ref=own, dst_ref=own, send_sem=fs.at[3 * a + j], recv_sem=fr.at[3 * a + j],
                                             device_id=me, device_id_type=_MESH_ID).wait_send()

    res = pl.pallas_call(
        body, name=name, in_specs=[_HBM] * (2 * n) + [_SEM] * 4 + [pl.BlockSpec(memory_space=pl.ANY)],
        out_shape=(*[_hbm(b) for b in blocks], *[_hbm(b) for b in landings]), out_specs=tuple([_HBM] * (2 * n)),
        input_output_aliases={i: i for i in range(2 * n)},
        compiler_params=pltpu.CompilerParams(has_side_effects=_DATAFLOW),
    )(*blocks, *landings, send_sems, d2d_sems, f_send, f_recv, after)
    return list(res[n:])


def _swap_start(mine, landings, *, name):
    n = len(mine)

    def body(*refs):
        ins, lands = refs[:n], refs[n:2 * n]
        send_sems, recv_sems = refs[2 * n:2 * n + 2]
        token = refs[-1]
        x, y, c, _, _ = _place()
        for a in range(n):
            pltpu.make_async_remote_copy(src_ref=ins[a].at[:, 1 - c], dst_ref=lands[a], send_sem=send_sems.at[a],
                                         recv_sem=recv_sems.at[a], device_id=(x, y, 1 - c), device_id_type=_MESH_ID).start()
        token[...] = jnp.zeros_like(token)

    res = pl.pallas_call(
        body, name=name, in_specs=[_HBM] * (2 * n),
        out_shape=(pltpu.SemaphoreType.DMA((n,)), pltpu.SemaphoreType.DMA((n,)),
                   *[_hbm(b) for b in mine], *[_hbm(b) for b in landings], _token_shape()),
        out_specs=(_SEM, _SEM, *[_HBM] * (2 * n), _TOKEN),
        input_output_aliases={i: 2 + i for i in range(2 * n)},
        compiler_params=pltpu.CompilerParams(has_side_effects=_DATAFLOW),
    )(*[pltpu.with_memory_space_constraint(b, pltpu.HBM) for b in mine],
      *[pltpu.with_memory_space_constraint(b, pltpu.HBM) for b in landings])
    return res[0], res[1], list(res[2:2 + n]), list(res[2 + n:2 + 2 * n]), res[-1]


def _swap_wait(mine, landings, send_sems, recv_sems, after, *, name):
    n = len(mine)

    def body(*refs):
        ins, lands = refs[:n], refs[n:2 * n]
        send, recv = refs[2 * n:2 * n + 2]
        x, y, c, _, _ = _place()
        for a in range(n):
            cp = pltpu.make_async_remote_copy(src_ref=ins[a].at[:, 1 - c], dst_ref=lands[a], send_sem=send.at[a],
                                              recv_sem=recv.at[a], device_id=(x, y, c), device_id_type=_MESH_ID)
            cp.wait_recv()
            cp.wait_send()

    res = pl.pallas_call(
        body, name=name, in_specs=[_HBM] * (2 * n) + [_SEM] * 2 + [pl.BlockSpec(memory_space=pl.ANY)],
        out_shape=(*[_hbm(b) for b in mine], *[_hbm(b) for b in landings]), out_specs=tuple([_HBM] * (2 * n)),
        input_output_aliases={i: i for i in range(2 * n)},
        compiler_params=pltpu.CompilerParams(has_side_effects=_DATAFLOW),
    )(*mine, *landings, send_sems, recv_sems, after)
    return list(res[:n]), list(res[n:])


def _chip_exchange_start(sums, landings, *, name):
    n = len(sums)

    def body(*refs):
        ins, lands = refs[:n], refs[n:2 * n]
        send_sems, recv_sems = refs[2 * n:2 * n + 2]
        token = refs[-1]
        _, _, c, p, chips = _place()
        for a in range(n):
            for j, (qx, qy) in enumerate(chips):
                pltpu.make_async_remote_copy(src_ref=ins[a].at[2 * qx + qy], dst_ref=lands[a].at[p], send_sem=send_sems.at[3 * a + j],
                                             recv_sem=recv_sems.at[3 * a + j], device_id=(qx, qy, c), device_id_type=_MESH_ID).start()
        token[...] = jnp.zeros_like(token)

    res = pl.pallas_call(
        body, name=name, in_specs=[_HBM] * (2 * n),
        out_shape=(pltpu.SemaphoreType.DMA((3 * n,)), pltpu.SemaphoreType.DMA((3 * n,)),
                   *[_hbm(b) for b in sums], *[_hbm(b) for b in landings], _token_shape()),
        out_specs=(_SEM, _SEM, *[_HBM] * (2 * n), _TOKEN),
        input_output_aliases={i: 2 + i for i in range(2 * n)},
        compiler_params=pltpu.CompilerParams(has_side_effects=_DATAFLOW),
    )(*[pltpu.with_memory_space_constraint(b, pltpu.HBM) for b in sums],
      *[pltpu.with_memory_space_constraint(b, pltpu.HBM) for b in landings])
    return res[0], res[1], list(res[2:2 + n]), list(res[2 + n:2 + 2 * n]), res[-1]


def _chip_exchange_wait(sums, landings, send_sems, recv_sems, after, *, name):
    n = len(sums)

    def body(*refs):
        ins, lands = refs[:n], refs[n:2 * n]
        send, recv = refs[2 * n:2 * n + 2]
        x, y, c, _, chips = _place()
        for a in range(n):
            for j, (qx, qy) in enumerate(chips):
                q = 2 * qx + qy
                cp = pltpu.make_async_remote_copy(src_ref=ins[a].at[q], dst_ref=lands[a].at[q], send_sem=send.at[3 * a + j],
                                                  recv_sem=recv.at[3 * a + j], device_id=(x, y, c), device_id_type=_MESH_ID)
                cp.wait_recv()
                cp.wait_send()

    res = pl.pallas_call(
        body, name=name, in_specs=[_HBM] * (2 * n) + [_SEM] * 2 + [pl.BlockSpec(memory_space=pl.ANY)] * len(after),
        out_shape=(*[_hbm(b) for b in sums], *[_hbm(b) for b in landings]), out_specs=tuple([_HBM] * (2 * n)),
        input_output_aliases={i: i for i in range(2 * n)},
        compiler_params=pltpu.CompilerParams(has_side_effects=_DATAFLOW),
    )(*sums, *landings, send_sems, recv_sems, *after)
    return list(res[:n]), list(res[n:])


_C1 = 1.0 - ADAM_B1 ** ADAM_STEP
_C2 = 1.0 - ADAM_B2 ** ADAM_STEP


def _adamw_math(g, w, m, v):
    m = ADAM_B1 * m + (1.0 - ADAM_B1) * g
    v = ADAM_B2 * v + (1.0 - ADAM_B2) * (g * g)
    delta = -ADAM_LR * ((m / _C1) / (jnp.sqrt(v / _C2) + ADAM_EPS) + ADAM_WD * w)
    return delta, m, v


def _adamw_reduce(landed, sums, chip, w, m, v, layer, prev, *, name):
    _, R, C = w.shape
    tr = max(t for t in range(16, R + 1, 16) if R % t == 0 and t * C <= 512 * 1024)

    def body(chip_ref, p_ref, own_ref, w_ref, m_ref, v_ref, *rest):
        g_ref, d_ref, nm_ref, nv_ref = rest[-4:]
        own = own_ref[...].astype(_F32)
        g = jnp.where(chip_ref[0] == 0, own, p_ref[0].astype(_F32))
        for q in range(1, N_CHIP):
            g = g + jnp.where(chip_ref[0] == q, own, p_ref[q].astype(_F32))
        d, nm, nv = _adamw_math(g, w_ref[...], m_ref[...], v_ref[...])
        g_ref[...] = g
        d_ref[...] = d
        nm_ref[...] = nm
        nv_ref[...] = nv

    blk = pl.BlockSpec((None, tr, C), lambda i, chip: (layer, i, 0))
    shape = jax.ShapeDtypeStruct((DEPTH, R, C), _F32)
    kept = [] if prev is None else list(prev)
    grid_spec = pltpu.PrefetchScalarGridSpec(
        num_scalar_prefetch=1, grid=(R // tr,),
        in_specs=[pl.BlockSpec((N_CHIP, tr, C), lambda i, chip: (0, i, 0)),
                  pl.BlockSpec((None, tr, C), lambda i, chip: (chip[0], i, 0)), blk, blk, blk]
        + [pl.BlockSpec(memory_space=pl.ANY)] * len(kept),
        out_specs=[blk] * 4)
    return pl.pallas_call(
        body, grid_spec=grid_spec, out_shape=[shape] * 4, name=name,
        input_output_aliases={6 + k: k for k in range(len(kept))},
        compiler_params=_cparams(dimension_semantics=("parallel",)),
    )(chip.reshape(1), landed, sums, w, m, v, *kept)


_PACK_LANES = 128
_LAYER_ROWS = 248
_LB_ROWS = (A_HEADS * HEAD_DIM) // _PACK_LANES


def _small_reduce(parts, lb_logits, *, name):
    rows = DEPTH * _LAYER_ROWS

    def body(p_ref, lg_ref, o_ref):
        g = p_ref[0]
        for s in range(1, N_DEV):
            g = g + p_ref[s]
        o_ref[...] = g
        lg = lg_ref[...]
        e = jnp.exp(lg - jnp.max(lg, axis=0, keepdims=True))
        p = e / jnp.sum(e, axis=0, keepdims=True)
        d1 = g[_LAYER_ROWS:_LAYER_ROWS + _LB_ROWS, :] * p[0] * p[1]
        o_ref[0:_LB_ROWS, :] = -d1
        o_ref[_LAYER_ROWS:_LAYER_ROWS + _LB_ROWS, :] = d1

    return pl.pallas_call(
        body, out_shape=jax.ShapeDtypeStruct((rows, _PACK_LANES), _F32), name=name,
        compiler_params=_cparams(),
    )(parts, lb_logits.reshape(DEPTH, _LB_ROWS, _PACK_LANES))


def _adamw_small(g, w, m, v, *, name):
    def body(g_ref, w_ref, m_ref, v_ref, d_ref, nm_ref, nv_ref):
        d, nm, nv = _adamw_math(g_ref[...], w_ref[...], m_ref[...], v_ref[...])
        d_ref[...] = d
        nm_ref[...] = nm
        nv_ref[...] = nv

    shape = jax.ShapeDtypeStruct(g.shape, _F32)
    return pl.pallas_call(body, out_shape=[shape] * 3, name=name, compiler_params=_cparams())(g, w, m, v)


def _pack(vectors, rows):
    flat = jnp.concatenate([v.reshape(-1).astype(_F32) for v in vectors])
    return jnp.pad(flat, (0, rows * _PACK_LANES - flat.shape[0])).reshape(rows, _PACK_LANES)


def _unpack(packed, shapes):
    flat = packed.reshape(-1)
    out, at = [], 0
    for s in shapes:
        size = int(np.prod(s))
        out.append(flat[at:at + size].reshape(s))
        at += size
    return out


_BIG = ("w_in", "w_gate", "w_up", "w_out", "w_down")
_COLUMN_SHARDED = ("w_in", "w_gate", "w_up")


def _full_weight(name, g):
    if name == "conv_w":
        return g.transpose(1, 0, 2).reshape(g.shape[1], N_DEV * SHARD_COLS)
    if name in _BIG:
        return g.reshape(N_DEV * g.shape[1], g.shape[2])
    return g


class _WeightGather:
    def __init__(self, names, first, blocks, lands, sems, tag):
        self.names, self.first, self.blocks, self.lands, self.sems, self.tag = names, first, blocks, lands, sems, tag
        self.forwarded = None

    def forward(self, after):
        f_send, f_recv, self.lands, token = _gather_forward(self.lands, self.sems[2], self.first, after,
                                                            name=f"gather_forward_{self.tag}")
        self.forwarded = (f_send, f_recv)
        return token

    def wait(self, after):
        if self.forwarded is None:
            self.forward(after)
        got = _gather_wait(self.blocks, self.lands, self.sems[0], self.sems[1], self.first, *self.forwarded, after,
                           name=f"gather_wait_{self.tag}")
        return {n: _full_weight(n, g) for n, g in zip(self.names, got)}


def _start_gathers(groups, me, name):
    blocks = [b for _, _, bs in groups for b in bs]
    landings = [lax.dynamic_update_index_in_dim(lax.empty((N_DEV,) + b.shape, b.dtype), b[None], me, 0) for b in blocks]
    send, d2d, ici, blocks, landings, token = _gather_start(blocks, landings, name=name)
    out, first = [], 0
    for tag, names, bs in groups:
        k = len(bs)
        out.append(_WeightGather(names, first, blocks[first:first + k], landings[first:first + k], (send, d2d, ici), tag))
        first += k
    return out, token


class _LayerWeights:
    def __init__(self, ready, pending=(), forwards=(), tokens=()):
        self.ready, self.pending, self.forwards, self._tokens = dict(ready), list(pending), list(forwards), list(tokens)

    def at(self, point, after):
        for when, gather in self.forwards:
            if when == point:
                self._tokens.append(gather.forward(after))

    def tokens(self):
        out, self._tokens = self._tokens, []
        return out

    def get(self, name, after):
        if name not in self.ready:
            group, = [g for g in self.pending if name in g.names]
            self.ready.update(group.wait(after))
        return self.ready[name]


def _layer_fwd(x, xb, ws, lb_logits, a_norm_w, c_sink, ln1_g, ln1_b, conv_b, ln2_g, ln2_b, tabs, l, target=None):
    proj = _mm(xb, ws.get("w_in", xb), tb=True, **_TILE_WIDE_N, after=ws.tokens(), name=f"proj_{l}")
    o_a, raw, states = _hgrn_fwd(proj, lb_logits, a_norm_w, l, name=f"hgrn_fwd_{l}")
    ws.at("hgrn", o_a)
    o_b, lse_b = _band_fwd(proj, tabs, name=f"dilated_fwd_{l}", **_DILATED)
    o_c, lse_c = _band_fwd(proj, tabs, sink=c_sink, name=f"swa_fwd_{l}", **_SWA)
    ws.at("swa", o_c)
    mixed = _concat_cols([o_a, o_b, o_c], name=f"mixed_{l}")
    y = _mm(mixed, ws.get("w_out", mixed), **_TILE_MIX, after=ws.tokens(), name=f"mix_out_{l}")
    z1, x1, x1b = _ln_fwd(x, y, ln1_g, ln1_b, name=f"ln1_fwd_{l}")
    g = _mm(x1b, ws.get("w_gate", x1b), tb=True, **_TILE_WIDE_N, out_dtype=_ACT_DTYPE, name=f"ffn_gate_{l}")
    u = _mm(x1b, ws.get("w_up", x1b), tb=True, **_TILE_WIDE_N, out_dtype=_ACT_DTYPE, name=f"ffn_up_{l}")
    ws.at("up", u)
    hb = _conv_gate_fwd(g, u, ws.get("conv_w", u), conv_b, name=f"conv_gate_fwd_{l}")
    y2 = _mm(hb, ws.get("w_down", hb), **_TILE_WIDE_K, after=ws.tokens(), name=f"ffn_down_{l}")
    ws.at("down", y2)
    res = dict(xb=xb, proj=proj, raw=raw, states=states, o_b=o_b, lse_b=lse_b, o_c=o_c, lse_c=lse_c,
               mixed=mixed, z1=z1, x1b=x1b, g=g, u=u, hb=hb)
    if target is not None:
        loss_part, *res["ln2_bwd"] = _ln_loss_bwd(x1, y2, ln2_g, ln2_b, target, name=f"ln2_loss_{l}")
        return loss_part, None, res
    res["z2"], x2, x2b = _ln_fwd(x1, y2, ln2_g, ln2_b, name=f"ln2_fwd_{l}")
    return x2, x2b, res


class _GradExchange:
    def __init__(self, core, chip):
        self.core, self.chip, self.groups, self.swapping, self._tokens = core, chip, [], [], []

    def launch(self, names, slabs, l, tag, behind):
        mine = [s.reshape((N_CHIP, 2) + s.shape[1:]) for s in slabs]
        if behind:
            landings = [lax.empty((N_CHIP,) + m.shape[2:], m.dtype) for m in mine]
            send, recv, mine, landings, token = _swap_start(mine, landings, name=f"swap_start_{tag}")
            self.swapping.append((names, l, tag, send, recv, mine, landings))
            self._tokens.append(token)
        else:
            self._exchange(names, l, tag, mine, _sibling_swap(mine, name=f"swap_grads_{tag}"))

    def advance(self, after):
        for names, l, tag, send, recv, mine, landings in self.swapping:
            mine, theirs = _swap_wait(mine, landings, send, recv, after, name=f"swap_wait_{tag}")
            self._exchange(names, l, tag, mine, theirs)
        self.swapping = []

    def _exchange(self, names, l, tag, mine, theirs):
        sums = [_pair_add(a, b, self.core, name=f"pair_add_{n}_{l}") for n, a, b in zip(names, mine, theirs)]
        landings = [lax.empty(s.shape, s.dtype) for s in sums]
        send, recv, sums, landings, token = _chip_exchange_start(sums, landings, name=f"exchange_start_{tag}")
        self.groups.append((names, l, tag, send, recv, sums, landings))
        self._tokens.append(token)

    def tokens(self):
        out, self._tokens = self._tokens, []
        return out

    def finish(self, weights, mom1, mom2, after):
        out = {}
        after = list(after) + self.tokens()
        for names, l, tag, send, recv, sums, landings in self.groups:
            sums, landings = _chip_exchange_wait(sums, landings, send, recv, after, name=f"exchange_wait_{tag}")
            for n, s, landed in zip(names, sums, landings):
                out[n] = _adamw_reduce(landed, s, self.chip, weights[n], mom1[n], mom2[n], l, out.get(n), name=f"adamw_{n}_{l}")
            after = [out[n][0] for n in names]
        return out


def _layer_bwd(dx2, res, w, lb_logits, a_norm_w, c_sink, ln1_g, conv_b, ln2_g, tabs, exchange, l):
    if "ln2_bwd" in res:
        dz2, dz2b, d_ln2_g, d_ln2_b = res["ln2_bwd"]
    else:
        dz2, dz2b, d_ln2_g, d_ln2_b = _ln_bwd(res["z2"], dx2, None, ln2_g, name=f"ln2_bwd_{l}")
    exchange.advance(dz2b)
    dh = _mm(dz2b, w["w_down"], tb=True, **_TILE_WIDE_N, out_dtype=_ACT_DTYPE, after=exchange.tokens(),
             name=f"ffn_down_dx_{l}")
    d_w_down = _mm(res["hb"], dz2b, ta=True, **_TILE_WIDE_M, out_dtype=_GRAD_DTYPE, name=f"ffn_down_dw_{l}")
    dg, du, d_conv_w, d_conv_b = _conv_gate_bwd(dh, res["g"], res["u"], w["conv_w"], conv_b, name=f"conv_gate_bwd_{l}")
    t = _mm(dg, w["w_gate"], **_TILE_WIDE_K, name=f"ffn_gate_dx_{l}")
    dx1 = _mm(du, w["w_up"], **_TILE_WIDE_K, add=t, name=f"ffn_up_dx_{l}")
    d_w_gate = _mm(dg, res["x1b"], ta=True, **_TILE_WIDE_M, out_dtype=_GRAD_DTYPE, name=f"ffn_gate_dw_{l}")
    d_w_up = _mm(du, res["x1b"], ta=True, **_TILE_WIDE_M, out_dtype=_GRAD_DTYPE, name=f"ffn_up_dw_{l}")
    dz1, dz1b, d_ln1_g, d_ln1_b = _ln_bwd(res["z1"], dx1, dz2, ln1_g, name=f"ln1_bwd_{l}")
    d_w_out = _mm(res["mixed"], dz1b, ta=True, **_TILE_MIX, out_dtype=_GRAD_DTYPE, name=f"mix_out_dw_{l}")
    exchange.launch(("w_down", "w_gate", "w_up", "w_out"),
                    [d.reshape(N_DEV, d.shape[0] // N_DEV, D_MODEL) for d in (d_w_down, d_w_gate, d_w_up, d_w_out)],
                    l, f"ffn_{l}", True)
    dmixed = _mm(dz1b, w["w_out"], tb=True, **_TILE_MIX, after=exchange.tokens(), name=f"mix_out_dx_{l}")
    dq_a, df_a, di_a, dg_a, d_norm_w, d_lb = _hgrn_bwd(res["proj"], lb_logits, a_norm_w, res["raw"], res["states"],
                                                      dmixed, l, name=f"hgrn_bwd_{l}")
    exchange.advance(dq_a)
    dq_b, dk_b, dv_b = _band_bwd(res["proj"], tabs, dmixed, res["o_b"], res["lse_b"], do0=A_HEADS, after=exchange.tokens(),
                                 name=f"dilated_bwd_{l}", **_DILATED)
    dq_c, dk_c, dv_c, d_sink = _band_bwd(res["proj"], tabs, dmixed, res["o_c"], res["lse_c"], do0=A_HEADS + B_HEADS,
                                         sink=c_sink, name=f"swa_bwd_{l}", **_SWA)
    dproj = _concat_cols([dq_a, df_a, di_a, dg_a, dq_b, dk_b, dv_b, dq_c, dk_c, dv_c], name=f"dproj_{l}")
    d_w_in = _mm(dproj, res["xb"], ta=True, **_TILE_WIDE_M, out_dtype=_GRAD_DTYPE, name=f"proj_dw_{l}")
    exchange.launch(("w_in",), [d_w_in.reshape(N_DEV, SHARD_COLS, D_MODEL)], l, f"mix_{l}", l > 0)
    dx = _mm(dproj, w["w_in"], **_TILE_WIDE_K, add=dz1, add_scale=ALPHA, after=exchange.tokens(), name=f"proj_dx_{l}")
    small = [d_lb, d_norm_w, jnp.pad(d_sink, (0, _PACK_LANES - C_HEADS)), d_ln1_g, d_ln1_b, d_ln2_g, d_ln2_b, d_conv_b,
             d_conv_w]
    return dx, small


def kernel(x, w_in, lb_logits, a_norm_w, c_sinks, w_out, ln1_g, ln1_b, w_gate, w_up, conv_w, conv_b, w_down, ln2_g, ln2_b, loss_target, m_w_in, m_lb_logits, m_a_norm_w, m_c_sinks, m_w_out, m_ln1_g, m_ln1_b, m_w_gate, m_w_up, m_conv_w, m_conv_b, m_w_down, m_ln2_g, m_ln2_b, v_w_in, v_lb_logits, v_a_norm_w, v_c_sinks, v_w_out, v_ln1_g, v_ln1_b, v_w_gate, v_w_up, v_conv_w, v_conv_b, v_w_down, v_ln2_g, v_ln2_b):
    weights = dict(w_in=w_in, lb_logits=lb_logits, a_norm_w=a_norm_w, c_sinks=c_sinks, w_out=w_out, ln1_g=ln1_g, ln1_b=ln1_b,
                   w_gate=w_gate, w_up=w_up, conv_w=conv_w, conv_b=conv_b, w_down=w_down, ln2_g=ln2_g, ln2_b=ln2_b)
    mom1 = dict(w_in=m_w_in, lb_logits=m_lb_logits, a_norm_w=m_a_norm_w, c_sinks=m_c_sinks, w_out=m_w_out, ln1_g=m_ln1_g,
                ln1_b=m_ln1_b, w_gate=m_w_gate, w_up=m_w_up, conv_w=m_conv_w, conv_b=m_conv_b, w_down=m_w_down, ln2_g=m_ln2_g,
                ln2_b=m_ln2_b)
    mom2 = dict(w_in=v_w_in, lb_logits=v_lb_logits, a_norm_w=v_a_norm_w, c_sinks=v_c_sinks, w_out=v_w_out, ln1_g=v_ln1_g,
                ln1_b=v_ln1_b, w_gate=v_w_gate, w_up=v_w_up, conv_w=v_conv_w, conv_b=v_conv_b, w_down=v_w_down, ln2_g=v_ln2_g,
                ln2_b=v_ln2_b)
    core = lax.axis_index("c").astype(jnp.int32)
    me = 4 * lax.axis_index("x") + 2 * lax.axis_index("y") + core
    tabs = _rope_tables()

    chip = (2 * lax.axis_index("x") + lax.axis_index("y")).astype(jnp.int32)

    def as_slabs(d):
        return {n: jnp.swapaxes(d[n], 1, 2) if n in _COLUMN_SHARDED else d[n] for n in _BIG}

    w_views = as_slabs(weights)

    def block(n, l, after=()):
        return conv_w[l] if n == "conv_w" else _cast_layer(w_views[n], l, after=after, name=f"cast_{n}_{l}")

    (in0,), started_first = _start_gathers([("w_in_0", ("w_in",), [block("w_in", 0)])], me, "gather_start_first")
    order = [(("w_out",), 0), (("w_gate", "w_up", "conv_w"), 0), (("w_down",), 0),
             (("w_in",), 1), (("w_out",), 1), (("w_gate", "w_up", "conv_w"), 1), (("w_down",), 1)]
    gathers, started = _start_gathers([(f"{names[0]}_{l}", names, [block(n, l, [started_first]) for n in names])
                                       for names, l in order], me, "gather_start_rest")
    out0, ffn0, down0, in1, out1, ffn1, down1 = gathers
    layer_ws = [_LayerWeights(in0.wait(started), [out0, ffn0, down0],
                              [("hgrn", out0), ("swa", ffn0), ("up", down0), ("down", in1)]),
                _LayerWeights({}, [in1, out1, ffn1, down1], [("hgrn", out1), ("swa", ffn1), ("up", down1)])]

    xs = x[0]
    xb = xs.astype(_MXU_DTYPE)
    saved = []
    for l in range(DEPTH):
        xs, xb, res = _layer_fwd(xs, xb, layer_ws[l], lb_logits, a_norm_w[l], c_sinks[l], ln1_g[l], ln1_b[l], conv_b[l],
                                 ln2_g[l], ln2_b[l], tabs, l, loss_target[0] if l == DEPTH - 1 else None)
        saved.append(res)
    loss = lax.psum(xs, ("x", "y", "c"))
    dx = None

    exchange = _GradExchange(core, chip)
    small_parts = [None] * DEPTH
    for l in reversed(range(DEPTH)):
        dx, small = _layer_bwd(dx, saved[l], layer_ws[l].ready, lb_logits, a_norm_w[l], c_sinks[l], ln1_g[l], conv_b[l],
                               ln2_g[l], tabs, exchange, l)
        small_parts[l] = _pack(small, _LAYER_ROWS)
    (small_gather,), small_started = _start_gathers(
        [("small_grads", ("small",), [jnp.concatenate(small_parts, axis=0)])], me, "gather_start_small")
    updated = exchange.finish(w_views, as_slabs(mom1), as_slabs(mom2), [dx, small_started])
    gathered = small_gather.wait(updated["w_in"][0])["small"]
    updated = {n: tuple(jnp.swapaxes(t, 1, 2) for t in u) if n in _COLUMN_SHARDED else u for n, u in updated.items()}
    g_small = _small_reduce(gathered, lb_logits, name="small_grads")

    per_layer = [(A_HEADS * HEAD_DIM,), (HEAD_DIM,), (_PACK_LANES,), (D_MODEL,), (D_MODEL,), (D_MODEL,), (D_MODEL,), (D_FF,),
                 (3, D_FF)]
    names = ("lb_logits", "a_norm_w", "c_sinks", "ln1_g", "ln1_b", "ln2_g", "ln2_b", "conv_b", "conv_w")
    grads = {n: [] for n in names}
    for l in range(DEPTH):
        for n, t in zip(names, _unpack(g_small[l * _LAYER_ROWS:(l + 1) * _LAYER_ROWS], per_layer)):
            grads[n].append(t)
    grads = {n: jnp.stack(t) for n, t in grads.items()}
    grads["c_sinks"] = grads["c_sinks"][:, :C_HEADS]
    grads["conv_w"] = lax.dynamic_slice_in_dim(grads["conv_w"], me * SHARD_COLS, SHARD_COLS, axis=2)
    shapes = [grads[n].shape for n in names]
    rows = -(-sum(int(np.prod(s)) for s in shapes) // (8 * _PACK_LANES)) * 8
    d_s, m_s, v_s = _adamw_small(_pack([grads[n] for n in names], rows), _pack([weights[n] for n in names], rows),
                                 _pack([mom1[n] for n in names], rows), _pack([mom2[n] for n in names], rows),
                                 name="adamw_small")
    delta = dict(zip(names, _unpack(d_s, shapes)))
    new_m = dict(zip(names, _unpack(m_s, shapes)))
    new_v = dict(zip(names, _unpack(v_s, shapes)))
    for n in _BIG:
        grads[n], delta[n], new_m[n], new_v[n] = updated[n]

    order = ("w_in", "lb_logits", "a_norm_w", "c_sinks", "w_out", "ln1_g", "ln1_b", "w_gate", "w_up", "conv_w", "conv_b",
             "w_down", "ln2_g", "ln2_b")
    return (loss, dx[None], *[grads[n] for n in order], *[delta[n] for n in order], *[new_m[n] for n in order],
            *[new_v[n] for n in order])
```

```python
import functools

import jax
import jax.numpy as jnp
import numpy as np
from jax import lax
from jax.experimental import pallas as pl
from jax.experimental.pallas import tpu as pltpu

D_MODEL = 2048
SEQ = 2048
DEPTH = 2
HEAD_DIM = 128
A_HEADS = 4
B_HEADS = 6
C_HEADS = 6
C_KV_HEADS = 2
A_CHUNK = 16
DILATIONS = (1, 4, 16)
BLOCK = 128
ROPE_THETA = 500000.0
ROPE_DIM = 32
D_FF = 5632
IN_WIDTH = 5632
LN_EPS = 1e-5
ALPHA = (2 * DEPTH) ** 0.25
N_DEV = 8
SHARD_COLS = IN_WIDTH // N_DEV

ADAM_LR = 0.001
ADAM_B1 = 0.9
ADAM_B2 = 0.999
ADAM_EPS = 1e-08
ADAM_WD = 0.01
ADAM_STEP = 10

A_COLS = 16
QKV_COLS = 28
QB0, KB0, VB0, QC0, KC0, VC0 = 0, 6, 12, 18, 24, 26

_MXU_DTYPE = jnp.bfloat16
_GRAD_DTYPE = jnp.bfloat16
_ACT_DTYPE = jnp.bfloat16
_NEG = -1e30
_VMEM_LIMIT = 56 * 2 ** 20

_F32 = jnp.float32


def _sigmoid(x):
    return 0.5 * jnp.tanh(0.5 * x) + 0.5


def _cparams(**kw):
    return pltpu.CompilerParams(vmem_limit_bytes=_VMEM_LIMIT, **kw)


_TILE_MIX = dict(tm=1024, tn=1024)
_TILE_WIDE_K = dict(tm=1024, tn=512)
_TILE_WIDE_N = dict(tm=1024, tn=1408)
_TILE_WIDE_M = dict(tm=1408, tn=1024)


def _mm(a, b, *, ta=False, tb=False, tm, tn, out_dtype=_F32, add=None, add_scale=1.0, after=(), name):
    K = a.shape[0] if ta else a.shape[1]
    M = a.shape[1] if ta else a.shape[0]
    N = b.shape[0] if tb else b.shape[1]
    assert (b.shape[1] if tb else b.shape[0]) == K and M % tm == 0 and N % tn == 0
    dn = (((0 if ta else 1,), (1 if tb else 0,)), ((), ()))

    def body(*refs):
        a_ref, b_ref = refs[:2]
        o_ref = refs[-1]
        r = lax.dot_general(a_ref[...], b_ref[...], dn, preferred_element_type=_F32)
        if add is not None:
            r = r + add_scale * refs[2][...]
        o_ref[...] = r.astype(o_ref.dtype)

    a_spec = pl.BlockSpec((K, tm), lambda i, j: (0, i)) if ta else pl.BlockSpec((tm, K), lambda i, j: (i, 0))
    b_spec = pl.BlockSpec((tn, K), lambda i, j: (j, 0)) if tb else pl.BlockSpec((K, tn), lambda i, j: (0, j))
    o_spec = pl.BlockSpec((tm, tn), lambda i, j: (i, j))
    in_specs = [a_spec, b_spec] + ([o_spec] if add is not None else []) + [pl.BlockSpec(memory_space=pl.ANY)] * len(after)
    args = (a, b) + ((add,) if add is not None else ()) + tuple(after)
    return pl.pallas_call(
        body, grid=(M // tm, N // tn), in_specs=in_specs, out_specs=o_spec,
        out_shape=jax.ShapeDtypeStruct((M, N), out_dtype), name=name,
        compiler_params=_cparams(dimension_semantics=("parallel", "parallel")),
    )(*args)


def _cast_layer(w, layer, *, after=(), name):
    _, R, C = w.shape
    tr = max(t for t in range(16, R + 1, 16) if R % t == 0 and t * C <= 512 * 1024)

    def body(w_ref, *rest):
        o_ref = rest[-1]
        o_ref[...] = w_ref[...].astype(o_ref.dtype)

    return pl.pallas_call(
        body, grid=(R // tr,),
        in_specs=[pl.BlockSpec((None, tr, C), lambda i: (layer, i, 0))] + [pl.BlockSpec(memory_space=pl.ANY)] * len(after),
        out_specs=pl.BlockSpec((tr, C), lambda i: (i, 0)), out_shape=jax.ShapeDtypeStruct((R, C), _MXU_DTYPE), name=name,
        compiler_params=_cparams(dimension_semantics=("parallel",)),
    )(w, *after)


def _concat_cols(pieces, *, name):
    tm = 512
    widths = [p.shape[1] for p in pieces]
    offs = np.cumsum([0] + widths)

    def body(*refs):
        o_ref = refs[-1]
        for p_ref, off, w in zip(refs[:-1], offs, widths):
            o_ref[:, off:off + w] = p_ref[...].astype(o_ref.dtype)

    return pl.pallas_call(
        body, grid=(SEQ // tm,), in_specs=[pl.BlockSpec((tm, w), lambda i: (i, 0)) for w in widths],
        out_specs=pl.BlockSpec((tm, int(offs[-1])), lambda i: (i, 0)),
        out_shape=jax.ShapeDtypeStruct((SEQ, int(offs[-1])), _MXU_DTYPE), name=name,
        compiler_params=_cparams(dimension_semantics=("parallel",)),
    )(*pieces)


def _ln_fwd(x, y, g, b, *, name):
    tm = 256

    def body(x_ref, y_ref, g_ref, b_ref, z_ref, o_ref, ob_ref):
        z = ALPHA * x_ref[...] + y_ref[...]
        mu = jnp.mean(z, axis=-1, keepdims=True)
        zc = z - mu
        var = jnp.mean(zc * zc, axis=-1, keepdims=True)
        o = zc * lax.rsqrt(var + LN_EPS) * g_ref[...] + b_ref[...]
        z_ref[...] = z
        o_ref[...] = o
        ob_ref[...] = o.astype(ob_ref.dtype)

    row = pl.BlockSpec((tm, D_MODEL), lambda i: (i, 0))
    vec = pl.BlockSpec((1, D_MODEL), lambda i: (0, 0))
    return pl.pallas_call(
        body, grid=(SEQ // tm,), in_specs=[row, row, vec, vec], out_specs=[row, row, row],
        out_shape=[jax.ShapeDtypeStruct((SEQ, D_MODEL), _F32), jax.ShapeDtypeStruct((SEQ, D_MODEL), _F32),
                   jax.ShapeDtypeStruct((SEQ, D_MODEL), _MXU_DTYPE)],
        name=name, compiler_params=_cparams(dimension_semantics=("parallel",)),
    )(x, y, g.reshape(1, D_MODEL), b.reshape(1, D_MODEL))


def _ln_bwd(z, d_a, d_res, g, *, name):
    tm = 256

    def body(*refs):
        if d_res is None:
            z_ref, da_ref, g_ref, dz_ref, dzb_ref, dg_ref, db_ref = refs
        else:
            z_ref, da_ref, dr_ref, g_ref, dz_ref, dzb_ref, dg_ref, db_ref = refs

        @pl.when(pl.program_id(0) == 0)
        def _():
            dg_ref[...] = jnp.zeros_like(dg_ref)
            db_ref[...] = jnp.zeros_like(db_ref)

        dout = da_ref[...]
        if d_res is not None:
            dout = dout + ALPHA * dr_ref[...]
        z = z_ref[...]
        mu = jnp.mean(z, axis=-1, keepdims=True)
        zc = z - mu
        var = jnp.mean(zc * zc, axis=-1, keepdims=True)
        rstd = lax.rsqrt(var + LN_EPS)
        xh = zc * rstd
        dxh = dout * g_ref[...]
        m1 = jnp.mean(dxh, axis=-1, keepdims=True)
        m2 = jnp.mean(dxh * xh, axis=-1, keepdims=True)
        dz = rstd * (dxh - m1 - xh * m2)
        dz_ref[...] = dz
        dzb_ref[...] = dz.astype(dzb_ref.dtype)
        dg_ref[0:1, :] += jnp.sum(dout * xh, axis=0, keepdims=True)
        db_ref[0:1, :] += jnp.sum(dout, axis=0, keepdims=True)

    row = pl.BlockSpec((tm, D_MODEL), lambda i: (i, 0))
    vec = pl.BlockSpec((1, D_MODEL), lambda i: (0, 0))
    acc = pl.BlockSpec((8, D_MODEL), lambda i: (0, 0))
    ins = [z, d_a] + ([d_res] if d_res is not None else []) + [g.reshape(1, D_MODEL)]
    in_specs = [row, row] + ([row] if d_res is not None else []) + [vec]
    dz, dzb, dg, db = pl.pallas_call(
        body, grid=(SEQ // tm,), in_specs=in_specs, out_specs=[row, row, acc, acc],
        out_shape=[jax.ShapeDtypeStruct((SEQ, D_MODEL), _F32), jax.ShapeDtypeStruct((SEQ, D_MODEL), _MXU_DTYPE),
                   jax.ShapeDtypeStruct((8, D_MODEL), _F32), jax.ShapeDtypeStruct((8, D_MODEL), _F32)],
        name=name, compiler_params=_cparams(dimension_semantics=("arbitrary",)),
    )(*ins)
    return dz, dzb, dg[0], db[0]


def _ln_loss_bwd(x, y, g, b, target, *, name):
    tm = 256

    def body(x_ref, y_ref, g_ref, b_ref, t_ref, dz_ref, dzb_ref, dg_ref, db_ref, l_ref):
        @pl.when(pl.program_id(0) == 0)
        def _():
            dg_ref[...] = jnp.zeros_like(dg_ref)
            db_ref[...] = jnp.zeros_like(db_ref)
            l_ref[...] = jnp.zeros_like(l_ref)

        z = ALPHA * x_ref[...] + y_ref[...]
        mu = jnp.mean(z, axis=-1, keepdims=True)
        zc = z - mu
        var = jnp.mean(zc * zc, axis=-1, keepdims=True)
        rstd = lax.rsqrt(var + LN_EPS)
        xh = zc * rstd
        e = xh * g_ref[...] + b_ref[...] - t_ref[...]
        l_ref[...] += (0.5 / D_MODEL) * jnp.sum(e * e)
        dout = e * (1.0 / D_MODEL)
        dxh = dout * g_ref[...]
        m1 = jnp.mean(dxh, axis=-1, keepdims=True)
        m2 = jnp.mean(dxh * xh, axis=-1, keepdims=True)
        dz = rstd * (dxh - m1 - xh * m2)
        dz_ref[...] = dz
        dzb_ref[...] = dz.astype(dzb_ref.dtype)
        dg_ref[0:1, :] += jnp.sum(dout * xh, axis=0, keepdims=True)
        db_ref[0:1, :] += jnp.sum(dout, axis=0, keepdims=True)

    row = pl.BlockSpec((tm, D_MODEL), lambda i: (i, 0))
    vec = pl.BlockSpec((1, D_MODEL), lambda i: (0, 0))
    acc = pl.BlockSpec((8, D_MODEL), lambda i: (0, 0))
    dz, dzb, dg, db, part = pl.pallas_call(
        body, grid=(SEQ // tm,), in_specs=[row, row, vec, vec, row],
        out_specs=[row, row, acc, acc, pl.BlockSpec((8, 128), lambda i: (0, 0))],
        out_shape=[jax.ShapeDtypeStruct((SEQ, D_MODEL), _F32), jax.ShapeDtypeStruct((SEQ, D_MODEL), _MXU_DTYPE),
                   jax.ShapeDtypeStruct((8, D_MODEL), _F32), jax.ShapeDtypeStruct((8, D_MODEL), _F32),
                   jax.ShapeDtypeStruct((8, 128), _F32)],
        name=name, compiler_params=_cparams(dimension_semantics=("arbitrary",)),
    )(x, y, g.reshape(1, D_MODEL), b.reshape(1, D_MODEL), target)
    return part[0, 0], dz, dzb, dg[0], db[0]


_CONV_TN = 256


def _shift_down(v, k, rows):
    return jnp.where(rows >= k, pltpu.roll(v, k, axis=0), 0.0)


def _shift_up(v, k, rows):
    return jnp.where(rows < SEQ - k, pltpu.roll(v, SEQ - k, axis=0), 0.0)


def _conv_gate_fwd(g, u, conv_w, conv_b, *, name):
    def body(g_ref, u_ref, w_ref, b_ref, h_ref):
        gv = g_ref[...].astype(_F32)
        rows = lax.broadcasted_iota(jnp.int32, gv.shape, 0)
        w = w_ref[...]
        gc = b_ref[...] + w[2:3, :] * gv + w[1:2, :] * _shift_down(gv, 1, rows) + w[0:1, :] * _shift_down(gv, 2, rows)
        h_ref[...] = (gc * _sigmoid(gc) * u_ref[...].astype(_F32)).astype(h_ref.dtype)

    col = pl.BlockSpec((SEQ, _CONV_TN), lambda j: (0, j))
    return pl.pallas_call(
        body, grid=(D_FF // _CONV_TN,),
        in_specs=[col, col, pl.BlockSpec((3, _CONV_TN), lambda j: (0, j)), pl.BlockSpec((1, _CONV_TN), lambda j: (0, j))],
        out_specs=col, out_shape=jax.ShapeDtypeStruct((SEQ, D_FF), _MXU_DTYPE), name=name,
        compiler_params=_cparams(dimension_semantics=("parallel",)),
    )(g, u, conv_w, conv_b.reshape(1, D_FF))


def _conv_gate_bwd(dh, g, u, conv_w, conv_b, *, name):
    def body(dh_ref, g_ref, u_ref, w_ref, b_ref, dg_ref, du_ref, dw_ref, db_ref):
        gv = g_ref[...].astype(_F32)
        rows = lax.broadcasted_iota(jnp.int32, gv.shape, 0)
        w = w_ref[...]
        g1 = _shift_down(gv, 1, rows)
        g2 = _shift_down(gv, 2, rows)
        gc = b_ref[...] + w[2:3, :] * gv + w[1:2, :] * g1 + w[0:1, :] * g2
        sg = _sigmoid(gc)
        dh = dh_ref[...].astype(_F32)
        du_ref[...] = (dh * (gc * sg)).astype(du_ref.dtype)
        dgc = dh * u_ref[...].astype(_F32) * (sg * (1.0 + gc * (1.0 - sg)))
        dg = w[2:3, :] * dgc + w[1:2, :] * _shift_up(dgc, 1, rows) + w[0:1, :] * _shift_up(dgc, 2, rows)
        dg_ref[...] = dg.astype(dg_ref.dtype)
        dw_ref[0:1, :] = jnp.sum(dgc * g2, axis=0, keepdims=True)
        dw_ref[1:2, :] = jnp.sum(dgc * g1, axis=0, keepdims=True)
        dw_ref[2:3, :] = jnp.sum(dgc * gv, axis=0, keepdims=True)
        db_ref[...] = jnp.sum(dgc, axis=0, keepdims=True)

    col = pl.BlockSpec((SEQ, _CONV_TN), lambda j: (0, j))
    w3 = pl.BlockSpec((3, _CONV_TN), lambda j: (0, j))
    w1 = pl.BlockSpec((1, _CONV_TN), lambda j: (0, j))
    dg, du, dw, db = pl.pallas_call(
        body, grid=(D_FF // _CONV_TN,), in_specs=[col, col, col, w3, w1], out_specs=[col, col, w3, w1],
        out_shape=[jax.ShapeDtypeStruct((SEQ, D_FF), _MXU_DTYPE), jax.ShapeDtypeStruct((SEQ, D_FF), _MXU_DTYPE),
                   jax.ShapeDtypeStruct((3, D_FF), _F32), jax.ShapeDtypeStruct((1, D_FF), _F32)],
        name=name, compiler_params=_cparams(dimension_semantics=("parallel",)),
    )(dh, g, u, conv_w, conv_b.reshape(1, D_FF))
    return dg, du, dw, db[0]


def _rope_tables():
    half = ROPE_DIM // 2
    inv = ROPE_THETA ** (-jnp.arange(0, ROPE_DIM, 2, dtype=_F32) / ROPE_DIM)
    ang = jnp.arange(SEQ, dtype=_F32)[:, None] * inv[None, :]
    cos, sin = jnp.cos(ang), jnp.sin(ang)
    rest = HEAD_DIM - ROPE_DIM
    c = jnp.concatenate([cos, cos, jnp.ones((SEQ, rest), _F32)], axis=1)
    s1 = jnp.concatenate([-sin, jnp.zeros((SEQ, HEAD_DIM - half), _F32)], axis=1)
    s2 = jnp.concatenate([jnp.zeros((SEQ, half), _F32), sin, jnp.zeros((SEQ, rest), _F32)], axis=1)
    return c, s1, s2


def _rope_apply(x, c, s1, s2):
    return x * c + pltpu.roll(x, HEAD_DIM - ROPE_DIM // 2, axis=1) * s1 + pltpu.roll(x, ROPE_DIM // 2, axis=1) * s2


def _rope_transpose(d, c, s1, s2):
    half = ROPE_DIM // 2
    return d * c + pltpu.roll(d * s1, half, axis=1) + pltpu.roll(d * s2, HEAD_DIM - half, axis=1)


_NT = (((1,), (1,)), ((), ()))
_TN = (((0,), (0,)), ((), ()))
_SCALE = HEAD_DIM ** -0.5


def _band_scores(q, k2, n, lag_off):
    s = lax.dot_general(q, k2, _NT, preferred_element_type=_F32) * _SCALE
    row = lax.broadcasted_iota(jnp.int32, (BLOCK, 2 * BLOCK), 0)
    col = lax.broadcasted_iota(jnp.int32, (BLOCK, 2 * BLOCK), 1)
    front = (col >= row + lag_off) & (col < BLOCK) & (n > 0)
    own = (col >= BLOCK) & (col <= row + BLOCK)
    return jnp.where(front | own, s, _NEG)


_BAND_STEPS = SEQ // BLOCK


def _rows(start, d):
    if d == 1:
        return pl.ds(pl.multiple_of(start, BLOCK), BLOCK)
    return pl.ds(start, BLOCK, stride=d)


def _band_block(it, d):
    r, n = it % d, it // d
    span = BLOCK * d
    return n, _rows(r + n * span, d), _rows(r + jnp.maximum(n - 1, 0) * span, d)


def _band_fwd(proj, tabs, *, kv_heads, q_per_kv, q0, k0, v0, dilations, lag_off, sink, name):
    heads = kv_heads * q_per_kv

    def body(*refs):
        q_refs = refs[:q_per_kv]
        k_ref, v_ref, c_ref, s1_ref, s2_ref = refs[q_per_kv:q_per_kv + 5]
        rest = refs[q_per_kv + 5:]
        if sink is not None:
            sk_ref, rest = rest[0], rest[1:]
        o_ref, lse_ref, qs, ks, m_s, l_s, acc_s = rest
        c, s1, s2 = c_ref[...], s1_ref[...], s2_ref[...]
        ks[...] = _rope_apply(k_ref[...], c, s1, s2)
        for i in range(q_per_kv):
            qs[...] = _rope_apply(q_refs[i][...], c, s1, s2)
            for pi, d in enumerate(dilations):
                def step(it, carry, d=d, first=(pi == 0)):
                    n, cur, prev = _band_block(it, d)
                    q = qs[cur, :].astype(_MXU_DTYPE)
                    k2 = jnp.concatenate([ks[prev, :], ks[cur, :]], axis=0).astype(_MXU_DTYPE)
                    v2 = jnp.concatenate([v_ref[prev, :], v_ref[cur, :]], axis=0).astype(_MXU_DTYPE)
                    s = _band_scores(q, k2, n, lag_off)
                    m_b = jnp.max(s, axis=1, keepdims=True)
                    m_new = m_b if first else jnp.maximum(m_b, m_s[cur, :][:, 0:1])
                    p = jnp.exp(s - m_new)
                    l_new = jnp.sum(p, axis=1, keepdims=True)
                    acc = jnp.dot(p.astype(_MXU_DTYPE), v2, preferred_element_type=_F32)
                    if not first:
                        a = jnp.exp(m_s[cur, :][:, 0:1] - m_new)
                        l_new = l_new + a * l_s[cur, :][:, 0:1]
                        acc = acc + a * acc_s[cur, :]
                    m_s[cur, :] = jnp.broadcast_to(m_new, (BLOCK, HEAD_DIM))
                    l_s[cur, :] = jnp.broadcast_to(l_new, (BLOCK, HEAD_DIM))
                    acc_s[cur, :] = acc
                    return carry

                lax.fori_loop(0, _BAND_STEPS, step, 0, unroll=16)
            m, den = m_s[...], l_s[...]
            if sink is not None:
                sk = sk_ref[i]
                m_f = jnp.maximum(m, sk)
                a = jnp.exp(m - m_f)
                den = den * a + jnp.exp(sk - m_f)
                o = acc_s[...] * a / den
                m = m_f
            else:
                o = acc_s[...] / den
            o_ref[:, i * HEAD_DIM:(i + 1) * HEAD_DIM] = o
            lse_ref[:, i * HEAD_DIM:(i + 1) * HEAD_DIM] = m + jnp.log(den)

    col = (SEQ, HEAD_DIM)
    in_specs = [pl.BlockSpec(col, functools.partial(lambda g, i: (0, A_COLS + q0 + g * q_per_kv + i), i=i)) for i in range(q_per_kv)]
    in_specs += [pl.BlockSpec(col, lambda g: (0, A_COLS + k0 + g)), pl.BlockSpec(col, lambda g: (0, A_COLS + v0 + g))]
    in_specs += [pl.BlockSpec(col, lambda g: (0, 0))] * 3
    args = [proj] * (q_per_kv + 2) + list(tabs)
    if sink is not None:
        in_specs.append(pl.BlockSpec((q_per_kv, 1, HEAD_DIM), lambda g: (g, 0, 0)))
        args.append(jnp.broadcast_to(sink.reshape(heads, 1, 1), (heads, 1, HEAD_DIM)))
    o_spec = pl.BlockSpec((SEQ, q_per_kv * HEAD_DIM), lambda g: (0, g))
    shape = jax.ShapeDtypeStruct((SEQ, heads * HEAD_DIM), _F32)
    return pl.pallas_call(
        body, grid=(kv_heads,), in_specs=in_specs, out_specs=[o_spec, o_spec], out_shape=[shape, shape],
        scratch_shapes=[pltpu.VMEM(col, _F32)] * 5, name=name,
        compiler_params=_cparams(dimension_semantics=("parallel",)),
    )(*args)


def _band_bwd(proj, tabs, dmixed, o, lse, *, kv_heads, q_per_kv, q0, k0, v0, do0, dilations, lag_off, sink, after=(), name):
    heads = kv_heads * q_per_kv

    def body(*refs):
        q_refs = refs[:q_per_kv]
        k_ref, v_ref, c_ref, s1_ref, s2_ref = refs[q_per_kv:q_per_kv + 5]
        do_refs = refs[q_per_kv + 5:2 * q_per_kv + 5]
        o_ref, lse_ref = refs[2 * q_per_kv + 5:2 * q_per_kv + 7]
        rest = refs[2 * q_per_kv + 7:]
        if sink is not None:
            sk_ref, rest = rest[0], rest[1:]
            dq_ref, dk_ref, dv_ref, dsk_ref, qs, ks, dq_s, dk_s, dv_s = rest[len(after):]
        else:
            dq_ref, dk_ref, dv_ref, qs, ks, dq_s, dk_s, dv_s = rest[len(after):]
        c, s1, s2 = c_ref[...], s1_ref[...], s2_ref[...]
        ks[...] = _rope_apply(k_ref[...], c, s1, s2)
        dk_s[...] = jnp.zeros_like(dk_s)
        dv_s[...] = jnp.zeros_like(dv_s)
        for i in range(q_per_kv):
            hs = slice(i * HEAD_DIM, (i + 1) * HEAD_DIM)
            qs[...] = _rope_apply(q_refs[i][...], c, s1, s2)
            dq_s[...] = jnp.zeros_like(dq_s)
            do_ref = do_refs[i]
            for d in dilations:
                def step(it, carry, d=d, do_ref=do_ref, hs=hs):
                    n, cur, prev = _band_block(it, d)
                    q = qs[cur, :].astype(_MXU_DTYPE)
                    k2 = jnp.concatenate([ks[prev, :], ks[cur, :]], axis=0).astype(_MXU_DTYPE)
                    v2 = jnp.concatenate([v_ref[prev, :], v_ref[cur, :]], axis=0).astype(_MXU_DTYPE)
                    do = do_ref[cur, :]
                    delta = jnp.sum(do * o_ref[cur, hs], axis=1, keepdims=True)
                    lse_c = lse_ref[cur, hs][:, 0:1]
                    p = jnp.exp(_band_scores(q, k2, n, lag_off) - lse_c)
                    dob = do.astype(_MXU_DTYPE)
                    ds = (p * (lax.dot_general(dob, v2, _NT, preferred_element_type=_F32) - delta) * _SCALE).astype(_MXU_DTYPE)
                    dq_s[cur, :] += jnp.dot(ds, k2, preferred_element_type=_F32)
                    dk2 = lax.dot_general(ds, q, _TN, preferred_element_type=_F32)
                    dv2 = lax.dot_general(p.astype(_MXU_DTYPE), dob, _TN, preferred_element_type=_F32)
                    dk_s[prev, :] += dk2[:BLOCK]
                    dv_s[prev, :] += dv2[:BLOCK]
                    dk_s[cur, :] += dk2[BLOCK:]
                    dv_s[cur, :] += dv2[BLOCK:]
                    return carry

                lax.fori_loop(0, _BAND_STEPS, step, 0, unroll=16)
            dq_ref[:, hs] = _rope_transpose(dq_s[...], c, s1, s2).astype(dq_ref.dtype)
            if sink is not None:
                delta = jnp.sum(do_ref[...] * o_ref[:, hs], axis=1, keepdims=True)
                w_sink = jnp.exp(sk_ref[i] - lse_ref[:, hs])
                dsk_ref[i] = jnp.broadcast_to(jnp.sum(-delta * w_sink[:, 0:1]), (8, HEAD_DIM))
        dk_ref[...] = _rope_transpose(dk_s[...], c, s1, s2).astype(dk_ref.dtype)
        dv_ref[...] = dv_s[...].astype(dv_ref.dtype)

    col = (SEQ, HEAD_DIM)
    in_specs = [pl.BlockSpec(col, functools.partial(lambda g, i: (0, A_COLS + q0 + g * q_per_kv + i), i=i)) for i in range(q_per_kv)]
    in_specs += [pl.BlockSpec(col, lambda g: (0, A_COLS + k0 + g)), pl.BlockSpec(col, lambda g: (0, A_COLS + v0 + g))]
    in_specs += [pl.BlockSpec(col, lambda g: (0, 0))] * 3
    in_specs += [pl.BlockSpec(col, functools.partial(lambda g, i: (0, do0 + g * q_per_kv + i), i=i)) for i in range(q_per_kv)]
    wide = pl.BlockSpec((SEQ, q_per_kv * HEAD_DIM), lambda g: (0, g))
    in_specs += [wide, wide]
    args = [proj] * (q_per_kv + 2) + list(tabs) + [dmixed] * q_per_kv + [o, lse]
    out_specs = [wide, pl.BlockSpec(col, lambda g: (0, g)), pl.BlockSpec(col, lambda g: (0, g))]
    out_shape = [jax.ShapeDtypeStruct((SEQ, heads * HEAD_DIM), _MXU_DTYPE), jax.ShapeDtypeStruct((SEQ, kv_heads * HEAD_DIM), _MXU_DTYPE),
                 jax.ShapeDtypeStruct((SEQ, kv_heads * HEAD_DIM), _MXU_DTYPE)]
    if sink is not None:
        in_specs.append(pl.BlockSpec((q_per_kv, 1, HEAD_DIM), lambda g: (g, 0, 0)))
        args.append(jnp.broadcast_to(sink.reshape(heads, 1, 1), (heads, 1, HEAD_DIM)))
        out_specs.append(pl.BlockSpec((q_per_kv, 8, HEAD_DIM), lambda g: (g, 0, 0)))
        out_shape.append(jax.ShapeDtypeStruct((heads, 8, HEAD_DIM), _F32))
    in_specs += [pl.BlockSpec(memory_space=pl.ANY)] * len(after)
    args += list(after)
    res = pl.pallas_call(
        body, grid=(kv_heads,), in_specs=in_specs, out_specs=out_specs, out_shape=out_shape,
        scratch_shapes=[pltpu.VMEM(col, _F32)] * 5, name=name,
        compiler_params=_cparams(dimension_semantics=("parallel",)),
    )(*args)
    if sink is not None:
        return res[0], res[1], res[2], res[3][:, 0, 0]
    return res


_DILATED = dict(kv_heads=B_HEADS, q_per_kv=1, q0=QB0, k0=KB0, v0=VB0, dilations=DILATIONS, lag_off=0, sink=None)
_SWA = dict(kv_heads=C_KV_HEADS, q_per_kv=C_HEADS // C_KV_HEADS, q0=QC0, k0=KC0, v0=VC0, dilations=(1,), lag_off=1)


_HG_TILE = 128
_HG_CHUNKS = _HG_TILE // A_CHUNK
_HG_TILES = SEQ // _HG_TILE
_HI = lax.Precision.HIGHEST


def _chunk_tri():
    i = np.arange(_HG_TILE)
    return jnp.asarray(((i[:, None] // A_CHUNK == i[None, :] // A_CHUNK) & (i[None, :] <= i[:, None])).astype(np.float32))


def _layer_lb(lb_ref, layer):
    if layer == 0:
        return jnp.zeros((1, HEAD_DIM), _F32)
    lg = lb_ref[...]
    m = jnp.max(lg, axis=0, keepdims=True)
    e = jnp.exp(lg - m)
    return e[1:2, :] / jnp.sum(e, axis=0, keepdims=True)


def _hgrn_gates(q, fr, lb):
    sgq = _sigmoid(q)
    sg = _sigmoid(fr)
    f = lb + (1.0 - lb) * sg
    return sgq, q * sgq, sg, f, 1.0 - f


def _hgrn_fwd(proj, lb_logits, norm_w, layer, *, name):
    tri = _chunk_tri()

    def body(q_ref, f_ref, i_ref, g_ref, lb_ref, nw_ref, tri_ref, o_ref, raw_ref, st_ref, state):
        @pl.when(pl.program_id(1) == 0)
        def _():
            state[...] = jnp.zeros_like(state)

        lb = _layer_lb(lb_ref, layer)
        _, qs, _, f, k = _hgrn_gates(q_ref[...], f_ref[...], lb)
        v = i_ref[...]
        b = jnp.dot(tri_ref[...], jnp.log(f), precision=_HI, preferred_element_type=_F32)
        eb = jnp.exp(b)
        ridx = lax.broadcasted_iota(jnp.int32, (A_CHUNK, HEAD_DIM), 0)
        ups, lams = [], []
        for c in range(_HG_CHUNKS):
            sl = slice(c * A_CHUNK, (c + 1) * A_CHUNK)
            bc = b[sl]
            bl = bc[A_CHUNK - 1:A_CHUNK]
            kt = (k[sl] * jnp.exp(bl - bc)).astype(_MXU_DTYPE)
            ups.append(lax.dot_general(v[sl].astype(_MXU_DTYPE), kt, _TN, preferred_element_type=_F32))
            lams.append(jnp.exp(bl))
        outs = []
        st = state[...]
        for c in range(_HG_CHUNKS):
            sl = slice(c * A_CHUNK, (c + 1) * A_CHUNK)
            bc, qc, kc, vc = b[sl], qs[sl], k[sl], v[sl]
            st_ref[0, c] = st
            o_c = lax.dot_general((qc * eb[sl]).astype(_MXU_DTYPE), st.astype(_MXU_DTYPE), _NT, preferred_element_type=_F32)
            for j in range(A_CHUNK):
                dj = jnp.exp(jnp.where(ridx >= j, bc - bc[j:j + 1], _NEG))
                a = jnp.sum(qc * kc[j:j + 1] * dj, axis=1, keepdims=True)
                o_c = o_c + a * vc[j:j + 1]
            outs.append(o_c)
            st = st * lams[c] + ups[c]
        state[...] = st
        o = jnp.concatenate(outs, axis=0)
        raw_ref[...] = o
        r = lax.rsqrt(jnp.mean(o * o, axis=-1, keepdims=True) + LN_EPS)
        g = g_ref[...]
        o_ref[...] = o * r * nw_ref[...] * (g * _sigmoid(g))

    blk = (_HG_TILE, HEAD_DIM)

    def col(base):
        return pl.BlockSpec(blk, lambda h, t: (t, base + h))

    o_spec = pl.BlockSpec(blk, lambda h, t: (t, h))
    o_shape = jax.ShapeDtypeStruct((SEQ, A_HEADS * HEAD_DIM), _F32)
    return pl.pallas_call(
        body, grid=(A_HEADS, _HG_TILES),
        in_specs=[col(0), col(4), col(8), col(12), pl.BlockSpec((DEPTH, HEAD_DIM), lambda h, t: (0, h)),
                  pl.BlockSpec((1, HEAD_DIM), lambda h, t: (0, 0)), pl.BlockSpec((_HG_TILE, _HG_TILE), lambda h, t: (0, 0))],
        out_specs=[o_spec, o_spec, pl.BlockSpec((1, _HG_CHUNKS, HEAD_DIM, HEAD_DIM), lambda h, t: (h, t, 0, 0))],
        out_shape=[o_shape, o_shape, jax.ShapeDtypeStruct((A_HEADS, SEQ // A_CHUNK, HEAD_DIM, HEAD_DIM), _F32)],
        scratch_shapes=[pltpu.VMEM((HEAD_DIM, HEAD_DIM), _F32)], name=name,
        compiler_params=_cparams(dimension_semantics=("parallel", "arbitrary")),
    )(proj, proj, proj, proj, lb_logits, norm_w.reshape(1, HEAD_DIM), tri)


def _hgrn_bwd(proj, lb_logits, norm_w, raw, states, dmixed, layer, *, name):
    tri = _chunk_tri()
    triu = tri.T

    def body(q_ref, f_ref, i_ref, g_ref, lb_ref, nw_ref, tri_ref, triu_ref, raw_ref, do_ref, st_ref,
             dq_ref, df_ref, di_ref, dg_ref, dnw_ref, dlb_ref, dstate):
        @pl.when(pl.program_id(1) == 0)
        def _():
            dstate[...] = jnp.zeros_like(dstate)
            dlb_ref[...] = jnp.zeros_like(dlb_ref)

        @pl.when((pl.program_id(0) == 0) & (pl.program_id(1) == 0))
        def _():
            dnw_ref[...] = jnp.zeros_like(dnw_ref)

        lb = _layer_lb(lb_ref, layer)
        q = q_ref[...]
        sgq, qs, sg, f, k = _hgrn_gates(q, f_ref[...], lb)
        v = i_ref[...]
        b = jnp.dot(tri_ref[...], jnp.log(f), precision=_HI, preferred_element_type=_F32)
        eb = jnp.exp(b)
        g = g_ref[...]
        nw = nw_ref[...]
        o = raw_ref[...]
        dout = do_ref[...]
        sgg = _sigmoid(g)
        r = lax.rsqrt(jnp.mean(o * o, axis=-1, keepdims=True) + LN_EPS)
        dg_ref[...] = (dout * (o * r * nw) * (sgg * (1.0 + g * (1.0 - sgg)))).astype(dg_ref.dtype)
        don = dout * (g * sgg)
        dnw_ref[0:1, :] += jnp.sum(don * o * r, axis=0, keepdims=True)
        dy = don * nw
        do_raw = r * dy - o * (r * r * r) * jnp.mean(o * dy, axis=-1, keepdims=True)

        ridx = lax.broadcasted_iota(jnp.int32, (A_CHUNK, HEAD_DIM), 0)
        dqs_t, dk_t, db_t, dv_t = [None] * _HG_CHUNKS, [None] * _HG_CHUNKS, [None] * _HG_CHUNKS, [None] * _HG_CHUNKS
        dqts, dups = [None] * _HG_CHUNKS, [None] * _HG_CHUNKS
        for c in range(_HG_CHUNKS):
            sl = slice(c * A_CHUNK, (c + 1) * A_CHUNK)
            dob = do_raw[sl].astype(_MXU_DTYPE)
            dqts[c] = jnp.dot(dob, st_ref[0, c].astype(_MXU_DTYPE), preferred_element_type=_F32)
            dups[c] = lax.dot_general(dob, (qs[sl] * eb[sl]).astype(_MXU_DTYPE), _TN, preferred_element_type=_F32)
        dst = dstate[...]
        for c in reversed(range(_HG_CHUNKS)):
            sl = slice(c * A_CHUNK, (c + 1) * A_CHUNK)
            bc, qc, kc, vc, doc = b[sl], qs[sl], k[sl], v[sl], do_raw[sl]
            bl = bc[A_CHUNK - 1:A_CHUNK]
            ebc = eb[sl]
            ebl = jnp.exp(bl - bc)
            lam = jnp.exp(bl)
            qt = qc * ebc
            kt = kc * ebl
            stp = st_ref[0, c]
            dstb = dst.astype(_MXU_DTYPE)
            dqt = dqts[c]
            dkt = jnp.dot(vc.astype(_MXU_DTYPE), dstb, preferred_element_type=_F32)
            dv = lax.dot_general(kt.astype(_MXU_DTYPE), dstb, _NT, preferred_element_type=_F32)
            dlam = jnp.sum(stp * dst, axis=0, keepdims=True)
            dst = dst * lam + dups[c]
            dqs_rows = []
            dk_in = jnp.zeros((A_CHUNK, HEAD_DIM), _F32)
            for i in range(A_CHUNK):
                di = jnp.exp(jnp.where(ridx <= i, bc[i:i + 1] - bc, _NEG))
                qi = qc[i:i + 1]
                doi = doc[i:i + 1]
                w = kc * di
                a = jnp.sum(qi * w, axis=1, keepdims=True)
                dv = dv + a * doi
                da = jnp.sum(doi * vc, axis=1, keepdims=True)
                dqs_rows.append(jnp.sum(da * w, axis=0, keepdims=True))
                dk_in = dk_in + da * (qi * di)
            dqs_in = jnp.concatenate(dqs_rows, axis=0)
            dbl = jnp.sum(dkt * kt, axis=0, keepdims=True) + dlam * lam
            db = qc * dqs_in - kc * dk_in + dqt * qt - dkt * kt
            db_t[c] = db + jnp.where(ridx == A_CHUNK - 1, dbl, 0.0)
            dqs_t[c] = dqs_in + dqt * ebc
            dk_t[c] = dk_in + dkt * ebl
            dv_t[c] = dv
        dstate[...] = dst
        dqs = jnp.concatenate(dqs_t, axis=0)
        dk = jnp.concatenate(dk_t, axis=0)
        db = jnp.concatenate(db_t, axis=0)
        di_ref[...] = jnp.concatenate(dv_t, axis=0).astype(di_ref.dtype)
        dlogf = jnp.dot(triu_ref[...], db, precision=_HI, preferred_element_type=_F32)
        df = dlogf / f - dk
        df_ref[...] = (df * (1.0 - lb) * sg * (1.0 - sg)).astype(df_ref.dtype)
        dlb_ref[0, 0:1, :] += jnp.sum(df * (1.0 - sg), axis=0, keepdims=True)
        dq_ref[...] = (dqs * (sgq * (1.0 + q * (1.0 - sgq)))).astype(dq_ref.dtype)

    blk = (_HG_TILE, HEAD_DIM)
    last = _HG_TILES - 1

    def col(base):
        return pl.BlockSpec(blk, lambda h, t: (last - t, base + h))

    tri_spec = pl.BlockSpec((_HG_TILE, _HG_TILE), lambda h, t: (0, 0))
    acc_spec = pl.BlockSpec((1, 8, HEAD_DIM), lambda h, t: (h, 0, 0))
    acc_shape = jax.ShapeDtypeStruct((A_HEADS, 8, HEAD_DIM), _F32)
    dq, df, di, dg, dnw, dlb = pl.pallas_call(
        body, grid=(A_HEADS, _HG_TILES),
        in_specs=[col(0), col(4), col(8), col(12), pl.BlockSpec((DEPTH, HEAD_DIM), lambda h, t: (0, h)),
                  pl.BlockSpec((1, HEAD_DIM), lambda h, t: (0, 0)), tri_spec, tri_spec, col(0), col(0),
                  pl.BlockSpec((1, _HG_CHUNKS, HEAD_DIM, HEAD_DIM), lambda h, t: (h, last - t, 0, 0))],
        out_specs=[col(0), col(0), col(0), col(0), pl.BlockSpec((8, HEAD_DIM), lambda h, t: (0, 0)), acc_spec],
        out_shape=[jax.ShapeDtypeStruct((SEQ, A_HEADS * HEAD_DIM), _MXU_DTYPE)] * 4
        + [jax.ShapeDtypeStruct((8, HEAD_DIM), _F32), acc_shape],
        scratch_shapes=[pltpu.VMEM((HEAD_DIM, HEAD_DIM), _F32)], name=name,
        compiler_params=_cparams(dimension_semantics=("arbitrary", "arbitrary")),
    )(proj, proj, proj, proj, lb_logits, norm_w.reshape(1, HEAD_DIM), tri, triu, raw, dmixed, states)
    return dq, df, di, dg, dnw[0], dlb[:, 0, :].reshape(A_HEADS * HEAD_DIM)


N_CHIP = N_DEV // 2
_MESH_ID = pl.DeviceIdType.MESH


def _place():
    x, y, c = lax.axis_index("x"), lax.axis_index("y"), lax.axis_index("c")
    chips = [(1 - x, y), (x, 1 - y), (1 - x, 1 - y)]
    return x, y, c, 2 * x + y, chips


def _sibling_swap(arrays, *, name):
    n = len(arrays)

    def body(*refs):
        ins, outs = refs[:n], refs[n:2 * n]
        send_sems, recv_sems = refs[2 * n:]
        x, y, c, _, _ = _place()
        copies = [pltpu.make_async_remote_copy(
            src_ref=ins[a].at[:, 1 - c], dst_ref=outs[a], send_sem=send_sems.at[a], recv_sem=recv_sems.at[a],
            device_id=(x, y, 1 - c), device_id_type=_MESH_ID) for a in range(n)]
        for cp in copies:
            cp.start()
        for cp in copies:
            cp.wait()

    any_spec = pl.BlockSpec(memory_space=pl.ANY)
    return pl.pallas_call(
        body, in_specs=[any_spec] * n, out_specs=[any_spec] * n,
        out_shape=[jax.ShapeDtypeStruct((N_CHIP,) + a.shape[2:], a.dtype) for a in arrays],
        scratch_shapes=[pltpu.SemaphoreType.DMA((n,)), pltpu.SemaphoreType.DMA((n,))],
        name=name, compiler_params=pltpu.CompilerParams(has_side_effects=True),
    )(*arrays)


def _pair_add(mine, theirs, core, *, name):
    _, _, R, C = mine.shape
    tr = max(t for t in range(16, R + 1, 16) if R % t == 0 and t * C <= 512 * 1024)

    def body(core_ref, m_ref, t_ref, o_ref):
        del core_ref
        o_ref[...] = (m_ref[...].astype(_F32) + t_ref[...].astype(_F32)).astype(o_ref.dtype)

    grid_spec = pltpu.PrefetchScalarGridSpec(
        num_scalar_prefetch=1, grid=(N_CHIP, R // tr),
        in_specs=[pl.BlockSpec((None, None, tr, C), lambda q, i, core: (q, core[0], i, 0)),
                  pl.BlockSpec((None, tr, C), lambda q, i, core: (q, i, 0))],
        out_specs=pl.BlockSpec((None, tr, C), lambda q, i, core: (q, i, 0)))
    return pl.pallas_call(
        body, grid_spec=grid_spec, out_shape=jax.ShapeDtypeStruct((N_CHIP, R, C), mine.dtype), name=name,
        compiler_params=_cparams(dimension_semantics=("parallel", "parallel")),
    )(core.reshape(1), mine, theirs)


_HBM = pl.BlockSpec(memory_space=pltpu.HBM)
_SEM = pl.BlockSpec(memory_space=pltpu.SEMAPHORE)
_TOKEN = pl.BlockSpec(memory_space=pltpu.VMEM)
_DATAFLOW = pltpu.SideEffectType.DATAFLOW_SIDE_EFFECTING


def _hbm(a):
    return pltpu.HBM(a.shape, a.dtype)


def _token_shape():
    return jax.ShapeDtypeStruct((8, 128), _F32)


def _dev_slot(px, py, pc):
    return 4 * px + 2 * py + pc


def _gather_start(blocks, landings, *, name):
    n = len(blocks)

    def body(*refs):
        ins, lands = refs[:n], refs[n:2 * n]
        send_sems, d2d_sems, ici_sems = refs[2 * n:2 * n + 3]
        token = refs[-1]
        x, y, c, _, chips = _place()
        for a in range(n):
            dst = lands[a].at[_dev_slot(x, y, c)]
            pltpu.make_async_remote_copy(src_ref=ins[a], dst_ref=dst, send_sem=send_sems.at[4 * a], recv_sem=d2d_sems.at[a],
                                         device_id=(x, y, 1 - c), device_id_type=_MESH_ID).start()
            for j, chip in enumerate(chips):
                pltpu.make_async_remote_copy(src_ref=ins[a], dst_ref=dst, send_sem=send_sems.at[4 * a + 1 + j],
                                             recv_sem=ici_sems.at[3 * a + j], device_id=(*chip, c),
                                             device_id_type=_MESH_ID).start()
        token[...] = jnp.zeros_like(token)

    res = pl.pallas_call(
        body, name=name, in_specs=[_HBM] * (2 * n),
        out_shape=(pltpu.SemaphoreType.DMA((4 * n,)), pltpu.SemaphoreType.DMA((n,)), pltpu.SemaphoreType.DMA((3 * n,)),
                   *[_hbm(b) for b in blocks], *[_hbm(b) for b in landings], _token_shape()),
        out_specs=(_SEM, _SEM, _SEM, *[_HBM] * (2 * n), _TOKEN),
        input_output_aliases={i: 3 + i for i in range(2 * n)},
        compiler_params=pltpu.CompilerParams(has_side_effects=_DATAFLOW),
    )(*[pltpu.with_memory_space_constraint(b, pltpu.HBM) for b in blocks],
      *[pltpu.with_memory_space_constraint(b, pltpu.HBM) for b in landings])
    return res[0], res[1], res[2], list(res[3:3 + n]), list(res[3 + n:3 + 2 * n]), res[-1]


def _gather_forward(landings, ici_sems, first, after, *, name):
    n = len(landings)

    def body(*refs):
        lands = refs[:n]
        ici = refs[n]
        f_send, f_recv = refs[n + 2], refs[n + 3]
        token = refs[-1]
        x, y, c, _, chips = _place()
        for a in range(n):
            for j, chip in enumerate(chips):
                blk = lands[a].at[_dev_slot(*chip, c)]
                pltpu.make_async_remote_copy(src_ref=blk, dst_ref=blk, send_sem=f_send.at[3 * a + j],
                                             recv_sem=ici.at[3 * (first + a) + j], device_id=(*chip, c),
                                             device_id_type=_MESH_ID).wait_recv()
                pltpu.make_async_remote_copy(src_ref=blk, dst_ref=blk, send_sem=f_send.at[3 * a + j], recv_sem=f_recv.at[3 * a + j],
                                             device_id=(x, y, 1 - c), device_id_type=_MESH_ID).start()
        token[...] = jnp.zeros_like(token)

    res = pl.pallas_call(
        body, name=name, in_specs=[_HBM] * n + [_SEM, pl.BlockSpec(memory_space=pl.ANY)],
        out_shape=(pltpu.SemaphoreType.DMA((3 * n,)), pltpu.SemaphoreType.DMA((3 * n,)), *[_hbm(b) for b in landings], _token_shape()),
        out_specs=(_SEM, _SEM, *[_HBM] * n, _TOKEN),
        input_output_aliases={i: 2 + i for i in range(n)},
        compiler_params=pltpu.CompilerParams(has_side_effects=_DATAFLOW),
    )(*landings, ici_sems, after)
    return res[0], res[1], list(res[2:2 + n]), res[-1]


def _gather_wait(blocks, landings, send_sems, d2d_sems, first, f_send, f_recv, after, *, name):
    n = len(landings)

    def body(*refs):
        ins, lands = refs[:n], refs[n:2 * n]
        send, d2d, fs, fr = refs[2 * n:2 * n + 4]
        x, y, c, _, chips = _place()
        me = (x, y, c)
        for a in range(n):
            own = lands[a].at[_dev_slot(x, y, 1 - c)]
            g = first + a
            pltpu.make_async_remote_copy(src_ref=ins[a], dst_ref=own, send_sem=send.at[4 * g], recv_sem=d2d.at[g],
                                         device_id=me, device_id_type=_MESH_ID).wait_recv()
            for j, chip in enumerate(chips):
                blk = lands[a].at[_dev_slot(*chip, 1 - c)]
                pltpu.make_async_remote_copy(src_ref=blk, dst_ref=blk, send_sem=fs.at[3 * a + j], recv_sem=fr.at[3 * a + j],
                                             device_id=me, device_id_type=_MESH_ID).wait_recv()
            for k in range(4):
                pltpu.make_async_remote_copy(src_ref=ins[a], dst_ref=own, send_sem=send.at[4 * g + k], recv_sem=d2d.at[g],
                                             device_id=me, device_id_type=_MESH_ID).wait_send()
            for j in range(3):
                pltpu.make_async_remote_copy(src_ref=own, dst_ref=own, send_sem=fs.at[3 * a + j], recv_sem=fr.at[3 * a + j],
                                             device_id=me, device_id_type=_MESH_ID).wait_send()

    res = pl.pallas_call(
        body, name=name, in_specs=[_HBM] * (2 * n) + [_SEM] * 4 + [pl.BlockSpec(memory_space=pl.ANY)],
        out_shape=(*[_hbm(b) for b in blocks], *[_hbm(b) for b in landings]), out_specs=tuple([_HBM] * (2 * n)),
        input_output_aliases={i: i for i in range(2 * n)},
        compiler_params=pltpu.CompilerParams(has_side_effects=_DATAFLOW),
    )(*blocks, *landings, send_sems, d2d_sems, f_send, f_recv, after)
    return list(res[n:])


def _swap_start(mine, landings, *, name):
    n = len(mine)

    def body(*refs):
        ins, lands = refs[:n], refs[n:2 * n]
        send_sems, recv_sems = refs[2 * n:2 * n + 2]
        token = refs[-1]
        x, y, c, _, _ = _place()
        for a in range(n):
            pltpu.make_async_remote_copy(src_ref=ins[a].at[:, 1 - c], dst_ref=lands[a], send_sem=send_sems.at[a],
                                         recv_sem=recv_sems.at[a], device_id=(x, y, 1 - c), device_id_type=_MESH_ID).start()
        token[...] = jnp.zeros_like(token)

    res = pl.pallas_call(
        body, name=name, in_specs=[_HBM] * (2 * n),
        out_shape=(pltpu.SemaphoreType.DMA((n,)), pltpu.SemaphoreType.DMA((n,)),
                   *[_hbm(b) for b in mine], *[_hbm(b) for b in landings], _token_shape()),
        out_specs=(_SEM, _SEM, *[_HBM] * (2 * n), _TOKEN),
        input_output_aliases={i: 2 + i for i in range(2 * n)},
        compiler_params=pltpu.CompilerParams(has_side_effects=_DATAFLOW),
    )(*[pltpu.with_memory_space_constraint(b, pltpu.HBM) for b in mine],
      *[pltpu.with_memory_space_constraint(b, pltpu.HBM) for b in landings])
    return res[0], res[1], list(res[2:2 + n]), list(res[2 + n:2 + 2 * n]), res[-1]


def _swap_wait(mine, landings, send_sems, recv_sems, after, *, name):
    n = len(mine)

    def body(*refs):
        ins, lands = refs[:n], refs[n:2 * n]
        send, recv = refs[2 * n:2 * n + 2]
        x, y, c, _, _ = _place()
        for a in range(n):
            cp = pltpu.make_async_remote_copy(src_ref=ins[a].at[:, 1 - c], dst_ref=lands[a], send_sem=send.at[a],
                                              recv_sem=recv.at[a], device_id=(x, y, c), device_id_type=_MESH_ID)
            cp.wait_recv()
            cp.wait_send()

    res = pl.pallas_call(
        body, name=name, in_specs=[_HBM] * (2 * n) + [_SEM] * 2 + [pl.BlockSpec(memory_space=pl.ANY)],
        out_shape=(*[_hbm(b) for b in mine], *[_hbm(b) for b in landings]), out_specs=tuple([_HBM] * (2 * n)),
        input_output_aliases={i: i for i in range(2 * n)},
        compiler_params=pltpu.CompilerParams(has_side_effects=_DATAFLOW),
    )(*mine, *landings, send_sems, recv_sems, after)
    return list(res[:n]), list(res[n:])


def _chip_exchange_start(sums, landings, *, name):
    n = len(sums)

    def body(*refs):
        ins, lands = refs[:n], refs[n:2 * n]
        send_sems, recv_sems = refs[2 * n:2 * n + 2]
        token = refs[-1]
        _, _, c, p, chips = _place()
        for a in range(n):
            for j, (qx, qy) in enumerate(chips):
                pltpu.make_async_remote_copy(src_ref=ins[a].at[2 * qx + qy], dst_ref=lands[a].at[p], send_sem=send_sems.at[3 * a + j],
                                             recv_sem=recv_sems.at[3 * a + j], device_id=(qx, qy, c), device_id_type=_MESH_ID).start()
        token[...] = jnp.zeros_like(token)

    res = pl.pallas_call(
        body, name=name, in_specs=[_HBM] * (2 * n),
        out_shape=(pltpu.SemaphoreType.DMA((3 * n,)), pltpu.SemaphoreType.DMA((3 * n,)),
                   *[_hbm(b) for b in sums], *[_hbm(b) for b in landings], _token_shape()),
        out_specs=(_SEM, _SEM, *[_HBM] * (2 * n), _TOKEN),
        input_output_aliases={i: 2 + i for i in range(2 * n)},
        compiler_params=pltpu.CompilerParams(has_side_effects=_DATAFLOW),
    )(*[pltpu.with_memory_space_constraint(b, pltpu.HBM) for b in sums],
      *[pltpu.with_memory_space_constraint(b, pltpu.HBM) for b in landings])
    return res[0], res[1], list(res[2:2 + n]), list(res[2 + n:2 + 2 * n]), res[-1]


def _chip_exchange_wait(sums, landings, send_sems, recv_sems, after, *, name):
    n = len(sums)

    def body(*refs):
        ins, lands = refs[:n], refs[n:2 * n]
        send, recv = refs[2 * n:2 * n + 2]
        x, y, c, _, chips = _place()
        for a in range(n):
            for j, (qx, qy) in enumerate(chips):
                q = 2 * qx + qy
                cp = pltpu.make_async_remote_copy(src_ref=ins[a].at[q], dst_ref=lands[a].at[q], send_sem=send.at[3 * a + j],
                                                  recv_sem=recv.at[3 * a + j], device_id=(x, y, c), device_id_type=_MESH_ID)
                cp.wait_recv()
                cp.wait_send()

    res = pl.pallas_call(
        body, name=name, in_specs=[_HBM] * (2 * n) + [_SEM] * 2 + [pl.BlockSpec(memory_space=pl.ANY)] * len(after),
        out_shape=(*[_hbm(b) for b in sums], *[_hbm(b) for b in landings]), out_specs=tuple([_HBM] * (2 * n)),
        input_output_aliases={i: i for i in range(2 * n)},
        compiler_params=pltpu.CompilerParams(has_side_effects=_DATAFLOW),
    )(*sums, *landings, send_sems, recv_sems, *after)
    return list(res[:n]), list(res[n:])


_C1 = 1.0 - ADAM_B1 ** ADAM_STEP
_C2 = 1.0 - ADAM_B2 ** ADAM_STEP


def _adamw_math(g, w, m, v):
    m = ADAM_B1 * m + (1.0 - ADAM_B1) * g
    v = ADAM_B2 * v + (1.0 - ADAM_B2) * (g * g)
    delta = -ADAM_LR * ((m / _C1) / (jnp.sqrt(v / _C2) + ADAM_EPS) + ADAM_WD * w)
    return delta, m, v


def _adamw_reduce(landed, sums, chip, w, m, v, layer, prev, *, name):
    _, R, C = w.shape
    tr = max(t for t in range(16, R + 1, 16) if R % t == 0 and t * C <= 256 * 1024)

    def body(chip_ref, p_ref, own_ref, w_ref, m_ref, v_ref, *rest):
        g_ref, d_ref, nm_ref, nv_ref = rest[-4:]
        own = own_ref[...].astype(_F32)
        g = jnp.where(chip_ref[0] == 0, own, p_ref[0].astype(_F32))
        for q in range(1, N_CHIP):
            g = g + jnp.where(chip_ref[0] == q, own, p_ref[q].astype(_F32))
        d, nm, nv = _adamw_math(g, w_ref[...], m_ref[...], v_ref[...])
        g_ref[...] = g
        d_ref[...] = d
        nm_ref[...] = nm
        nv_ref[...] = nv

    blk = pl.BlockSpec((None, tr, C), lambda i, chip: (layer, i, 0))
    shape = jax.ShapeDtypeStruct((DEPTH, R, C), _F32)
    kept = [] if prev is None else list(prev)
    grid_spec = pltpu.PrefetchScalarGridSpec(
        num_scalar_prefetch=1, grid=(R // tr,),
        in_specs=[pl.BlockSpec((N_CHIP, tr, C), lambda i, chip: (0, i, 0)),
                  pl.BlockSpec((None, tr, C), lambda i, chip: (chip[0], i, 0)), blk, blk, blk]
        + [pl.BlockSpec(memory_space=pl.ANY)] * len(kept),
        out_specs=[blk] * 4)
    return pl.pallas_call(
        body, grid_spec=grid_spec, out_shape=[shape] * 4, name=name,
        input_output_aliases={6 + k: k for k in range(len(kept))},
        compiler_params=_cparams(dimension_semantics=("parallel",)),
    )(chip.reshape(1), landed, sums, w, m, v, *kept)


_PACK_LANES = 128
_LAYER_ROWS = 248
_LB_ROWS = (A_HEADS * HEAD_DIM) // _PACK_LANES


def _small_reduce(parts, lb_logits, *, name):
    rows = DEPTH * _LAYER_ROWS

    def body(p_ref, lg_ref, o_ref):
        g = p_ref[0]
        for s in range(1, N_DEV):
            g = g + p_ref[s]
        o_ref[...] = g
        lg = lg_ref[...]
        e = jnp.exp(lg - jnp.max(lg, axis=0, keepdims=True))
        p = e / jnp.sum(e, axis=0, keepdims=True)
        d1 = g[_LAYER_ROWS:_LAYER_ROWS + _LB_ROWS, :] * p[0] * p[1]
        o_ref[0:_LB_ROWS, :] = -d1
        o_ref[_LAYER_ROWS:_LAYER_ROWS + _LB_ROWS, :] = d1

    return pl.pallas_call(
        body, out_shape=jax.ShapeDtypeStruct((rows, _PACK_LANES), _F32), name=name,
        compiler_params=_cparams(),
    )(parts, lb_logits.reshape(DEPTH, _LB_ROWS, _PACK_LANES))


def _adamw_small(g, w, m, v, *, name):
    def body(g_ref, w_ref, m_ref, v_ref, d_ref, nm_ref, nv_ref):
        d, nm, nv = _adamw_math(g_ref[...], w_ref[...], m_ref[...], v_ref[...])
        d_ref[...] = d
        nm_ref[...] = nm
        nv_ref[...] = nv

    shape = jax.ShapeDtypeStruct(g.shape, _F32)
    return pl.pallas_call(body, out_shape=[shape] * 3, name=name, compiler_params=_cparams())(g, w, m, v)


def _pack(vectors, rows):
    flat = jnp.concatenate([v.reshape(-1).astype(_F32) for v in vectors])
    return jnp.pad(flat, (0, rows * _PACK_LANES - flat.shape[0])).reshape(rows, _PACK_LANES)


def _unpack(packed, shapes):
    flat = packed.reshape(-1)
    out, at = [], 0
    for s in shapes:
        size = int(np.prod(s))
        out.append(flat[at:at + size].reshape(s))
        at += size
    return out


_BIG = ("w_in", "w_gate", "w_up", "w_out", "w_down")
_COLUMN_SHARDED = ("w_in", "w_gate", "w_up")


def _full_weight(name, g):
    if name == "conv_w":
        return g.transpose(1, 0, 2).reshape(g.shape[1], N_DEV * SHARD_COLS)
    if name in _BIG:
        return g.reshape(N_DEV * g.shape[1], g.shape[2])
    return g


class _WeightGather:
    def __init__(self, names, first, blocks, lands, sems, tag):
        self.names, self.first, self.blocks, self.lands, self.sems, self.tag = names, first, blocks, lands, sems, tag
        self.forwarded = None

    def forward(self, after):
        f_send, f_recv, self.lands, token = _gather_forward(self.lands, self.sems[2], self.first, after,
                                                            name=f"gather_forward_{self.tag}")
        self.forwarded = (f_send, f_recv)
        return token

    def wait(self, after):
        if self.forwarded is None:
            self.forward(after)
        got = _gather_wait(self.blocks, self.lands, self.sems[0], self.sems[1], self.first, *self.forwarded, after,
                           name=f"gather_wait_{self.tag}")
        return {n: _full_weight(n, g) for n, g in zip(self.names, got)}


def _start_gathers(groups, me, name):
    blocks = [b for _, _, bs in groups for b in bs]
    landings = [lax.dynamic_update_index_in_dim(lax.empty((N_DEV,) + b.shape, b.dtype), b[None], me, 0) for b in blocks]
    send, d2d, ici, blocks, landings, token = _gather_start(blocks, landings, name=name)
    out, first = [], 0
    for tag, names, bs in groups:
        k = len(bs)
        out.append(_WeightGather(names, first, blocks[first:first + k], landings[first:first + k], (send, d2d, ici), tag))
        first += k
    return out, token


class _LayerWeights:
    def __init__(self, ready, pending=(), forwards=(), tokens=()):
        self.ready, self.pending, self.forwards, self._tokens = dict(ready), list(pending), list(forwards), list(tokens)

    def at(self, point, after):
        for when, gather in self.forwards:
            if when == point:
                self._tokens.append(gather.forward(after))

    def tokens(self):
        out, self._tokens = self._tokens, []
        return out

    def get(self, name, after):
        if name not in self.ready:
            group, = [g for g in self.pending if name in g.names]
            self.ready.update(group.wait(after))
        return self.ready[name]


def _layer_fwd(x, xb, ws, lb_logits, a_norm_w, c_sink, ln1_g, ln1_b, conv_b, ln2_g, ln2_b, tabs, l, target=None):
    proj = _mm(xb, ws.get("w_in", xb), tb=True, **_TILE_WIDE_N, after=ws.tokens(), name=f"proj_{l}")
    o_a, raw, states = _hgrn_fwd(proj, lb_logits, a_norm_w, l, name=f"hgrn_fwd_{l}")
    ws.at("hgrn", o_a)
    o_b, lse_b = _band_fwd(proj, tabs, name=f"dilated_fwd_{l}", **_DILATED)
    o_c, lse_c = _band_fwd(proj, tabs, sink=c_sink, name=f"swa_fwd_{l}", **_SWA)
    ws.at("swa", o_c)
    mixed = _concat_cols([o_a, o_b, o_c], name=f"mixed_{l}")
    y = _mm(mixed, ws.get("w_out", mixed), **_TILE_MIX, after=ws.tokens(), name=f"mix_out_{l}")
    z1, x1, x1b = _ln_fwd(x, y, ln1_g, ln1_b, name=f"ln1_fwd_{l}")
    g = _mm(x1b, ws.get("w_gate", x1b), tb=True, **_TILE_WIDE_N, out_dtype=_ACT_DTYPE, name=f"ffn_gate_{l}")
    u = _mm(x1b, ws.get("w_up", x1b), tb=True, **_TILE_WIDE_N, out_dtype=_ACT_DTYPE, name=f"ffn_up_{l}")
    ws.at("up", u)
    hb = _conv_gate_fwd(g, u, ws.get("conv_w", u), conv_b, name=f"conv_gate_fwd_{l}")
    y2 = _mm(hb, ws.get("w_down", hb), **_TILE_WIDE_K, after=ws.tokens(), name=f"ffn_down_{l}")
    ws.at("down", y2)
    res = dict(xb=xb, proj=proj, raw=raw, states=states, o_b=o_b, lse_b=lse_b, o_c=o_c, lse_c=lse_c,
               mixed=mixed, z1=z1, x1b=x1b, g=g, u=u, hb=hb)
    if target is not None:
        loss_part, *res["ln2_bwd"] = _ln_loss_bwd(x1, y2, ln2_g, ln2_b, target, name=f"ln2_loss_{l}")
        return loss_part, None, res
    res["z2"], x2, x2b = _ln_fwd(x1, y2, ln2_g, ln2_b, name=f"ln2_fwd_{l}")
    return x2, x2b, res


class _GradExchange:
    def __init__(self, core, chip):
        self.core, self.chip, self.groups, self.swapping, self._tokens = core, chip, [], [], []

    def launch(self, names, slabs, l, tag, behind):
        mine = [s.reshape((N_CHIP, 2) + s.shape[1:]) for s in slabs]
        if behind:
            landings = [lax.empty((N_CHIP,) + m.shape[2:], m.dtype) for m in mine]
            send, recv, mine, landings, token = _swap_start(mine, landings, name=f"swap_start_{tag}")
            self.swapping.append((names, l, tag, send, recv, mine, landings))
            self._tokens.append(token)
        else:
            self._exchange(names, l, tag, mine, _sibling_swap(mine, name=f"swap_grads_{tag}"))

    def advance(self, after):
        for names, l, tag, send, recv, mine, landings in self.swapping:
            mine, theirs = _swap_wait(mine, landings, send, recv, after, name=f"swap_wait_{tag}")
            self._exchange(names, l, tag, mine, theirs)
        self.swapping = []

    def _exchange(self, names, l, tag, mine, theirs):
        sums = [_pair_add(a, b, self.core, name=f"pair_add_{n}_{l}") for n, a, b in zip(names, mine, theirs)]
        landings = [lax.empty(s.shape, s.dtype) for s in sums]
        send, recv, sums, landings, token = _chip_exchange_start(sums, landings, name=f"exchange_start_{tag}")
        self.groups.append((names, l, tag, send, recv, sums, landings))
        self._tokens.append(token)

    def tokens(self):
        out, self._tokens = self._tokens, []
        return out

    def finish(self, weights, mom1, mom2, after):
        out = {}
        after = list(after) + self.tokens()
        for names, l, tag, send, recv, sums, landings in self.groups:
            sums, landings = _chip_exchange_wait(sums, landings, send, recv, after, name=f"exchange_wait_{tag}")
            for n, s, landed in zip(names, sums, landings):
                out[n] = _adamw_reduce(landed, s, self.chip, weights[n], mom1[n], mom2[n], l, out.get(n), name=f"adamw_{n}_{l}")
            after = [out[n][0] for n in names]
        return out


def _layer_bwd(dx2, res, w, lb_logits, a_norm_w, c_sink, ln1_g, conv_b, ln2_g, tabs, exchange, l):
    if "ln2_bwd" in res:
        dz2, dz2b, d_ln2_g, d_ln2_b = res["ln2_bwd"]
    else:
        dz2, dz2b, d_ln2_g, d_ln2_b = _ln_bwd(res["z2"], dx2, None, ln2_g, name=f"ln2_bwd_{l}")
    exchange.advance(dz2b)
    dh = _mm(dz2b, w["w_down"], tb=True, **_TILE_WIDE_N, out_dtype=_ACT_DTYPE, after=exchange.tokens(),
             name=f"ffn_down_dx_{l}")
    d_w_down = _mm(res["hb"], dz2b, ta=True, **_TILE_WIDE_M, out_dtype=_GRAD_DTYPE, name=f"ffn_down_dw_{l}")
    dg, du, d_conv_w, d_conv_b = _conv_gate_bwd(dh, res["g"], res["u"], w["conv_w"], conv_b, name=f"conv_gate_bwd_{l}")
    t = _mm(dg, w["w_gate"], **_TILE_WIDE_K, name=f"ffn_gate_dx_{l}")
    dx1 = _mm(du, w["w_up"], **_TILE_WIDE_K, add=t, name=f"ffn_up_dx_{l}")
    d_w_gate = _mm(dg, res["x1b"], ta=True, **_TILE_WIDE_M, out_dtype=_GRAD_DTYPE, name=f"ffn_gate_dw_{l}")
    d_w_up = _mm(du, res["x1b"], ta=True, **_TILE_WIDE_M, out_dtype=_GRAD_DTYPE, name=f"ffn_up_dw_{l}")
    dz1, dz1b, d_ln1_g, d_ln1_b = _ln_bwd(res["z1"], dx1, dz2, ln1_g, name=f"ln1_bwd_{l}")
    d_w_out = _mm(res["mixed"], dz1b, ta=True, **_TILE_MIX, out_dtype=_GRAD_DTYPE, name=f"mix_out_dw_{l}")
    exchange.launch(("w_down", "w_gate", "w_up", "w_out"),
                    [d.reshape(N_DEV, d.shape[0] // N_DEV, D_MODEL) for d in (d_w_down, d_w_gate, d_w_up, d_w_out)],
                    l, f"ffn_{l}", True)
    dmixed = _mm(dz1b, w["w_out"], tb=True, **_TILE_MIX, after=exchange.tokens(), name=f"mix_out_dx_{l}")
    dq_a, df_a, di_a, dg_a, d_norm_w, d_lb = _hgrn_bwd(res["proj"], lb_logits, a_norm_w, res["raw"], res["states"],
                                                      dmixed, l, name=f"hgrn_bwd_{l}")
    exchange.advance(dq_a)
    dq_b, dk_b, dv_b = _band_bwd(res["proj"], tabs, dmixed, res["o_b"], res["lse_b"], do0=A_HEADS, after=exchange.tokens(),
                                 name=f"dilated_bwd_{l}", **_DILATED)
    dq_c, dk_c, dv_c, d_sink = _band_bwd(res["proj"], tabs, dmixed, res["o_c"], res["lse_c"], do0=A_HEADS + B_HEADS,
                                         sink=c_sink, name=f"swa_bwd_{l}", **_SWA)
    dproj = _concat_cols([dq_a, df_a, di_a, dg_a, dq_b, dk_b, dv_b, dq_c, dk_c, dv_c], name=f"dproj_{l}")
    d_w_in = _mm(dproj, res["xb"], ta=True, **_TILE_WIDE_M, out_dtype=_GRAD_DTYPE, name=f"proj_dw_{l}")
    exchange.launch(("w_in",), [d_w_in.reshape(N_DEV, SHARD_COLS, D_MODEL)], l, f"mix_{l}", l > 0)
    dx = _mm(dproj, w["w_in"], **_TILE_WIDE_K, add=dz1, add_scale=ALPHA, after=exchange.tokens(), name=f"proj_dx_{l}")
    small = [d_lb, d_norm_w, jnp.pad(d_sink, (0, _PACK_LANES - C_HEADS)), d_ln1_g, d_ln1_b, d_ln2_g, d_ln2_b, d_conv_b,
             d_conv_w]
    return dx, small


def kernel(x, w_in, lb_logits, a_norm_w, c_sinks, w_out, ln1_g, ln1_b, w_gate, w_up, conv_w, conv_b, w_down, ln2_g, ln2_b, loss_target, m_w_in, m_lb_logits, m_a_norm_w, m_c_sinks, m_w_out, m_ln1_g, m_ln1_b, m_w_gate, m_w_up, m_conv_w, m_conv_b, m_w_down, m_ln2_g, m_ln2_b, v_w_in, v_lb_logits, v_a_norm_w, v_c_sinks, v_w_out, v_ln1_g, v_ln1_b, v_w_gate, v_w_up, v_conv_w, v_conv_b, v_w_down, v_ln2_g, v_ln2_b):
    weights = dict(w_in=w_in, lb_logits=lb_logits, a_norm_w=a_norm_w, c_sinks=c_sinks, w_out=w_out, ln1_g=ln1_g, ln1_b=ln1_b,
                   w_gate=w_gate, w_up=w_up, conv_w=conv_w, conv_b=conv_b, w_down=w_down, ln2_g=ln2_g, ln2_b=ln2_b)
    mom1 = dict(w_in=m_w_in, lb_logits=m_lb_logits, a_norm_w=m_a_norm_w, c_sinks=m_c_sinks, w_out=m_w_out, ln1_g=m_ln1_g,
                ln1_b=m_ln1_b, w_gate=m_w_gate, w_up=m_w_up, conv_w=m_conv_w, conv_b=m_conv_b, w_down=m_w_down, ln2_g=m_ln2_g,
                ln2_b=m_ln2_b)
    mom2 = dict(w_in=v_w_in, lb_logits=v_lb_logits, a_norm_w=v_a_norm_w, c_sinks=v_c_sinks, w_out=v_w_out, ln1_g=v_ln1_g,
                ln1_b=v_ln1_b, w_gate=v_w_gate, w_up=v_w_up, conv_w=v_conv_w, conv_b=v_conv_b, w_down=v_w_down, ln2_g=v_ln2_g,
                ln2_b=v_ln2_b)
    core = lax.axis_index("c").astype(jnp.int32)
    me = 4 * lax.axis_index("x") + 2 * lax.axis_index("y") + core
    tabs = _rope_tables()

    chip = (2 * lax.axis_index("x") + lax.axis_index("y")).astype(jnp.int32)

    def as_slabs(d):
        return {n: jnp.swapaxes(d[n], 1, 2) if n in _COLUMN_SHARDED else d[n] for n in _BIG}

    w_views = as_slabs(weights)

    def block(n, l, after=()):
        return conv_w[l] if n == "conv_w" else _cast_layer(w_views[n], l, after=after, name=f"cast_{n}_{l}")

    (in0,), started_first = _start_gathers([("w_in_0", ("w_in",), [block("w_in", 0)])], me, "gather_start_first")
    order = [(("w_out",), 0), (("w_gate", "w_up", "conv_w"), 0), (("w_down",), 0),
             (("w_in",), 1), (("w_out",), 1), (("w_gate", "w_up", "conv_w"), 1), (("w_down",), 1)]
    gathers, started = _start_gathers([(f"{names[0]}_{l}", names, [block(n, l, [started_first]) for n in names])
                                       for names, l in order], me, "gather_start_rest")
    out0, ffn0, down0, in1, out1, ffn1, down1 = gathers
    layer_ws = [_LayerWeights(in0.wait(started), [out0, ffn0, down0],
                              [("hgrn", out0), ("swa", ffn0), ("up", down0), ("down", in1)]),
                _LayerWeights({}, [in1, out1, ffn1, down1], [("hgrn", out1), ("swa", ffn1), ("up", down1)])]

    xs = x[0]
    xb = xs.astype(_MXU_DTYPE)
    saved = []
    for l in range(DEPTH):
        xs, xb, res = _layer_fwd(xs, xb, layer_ws[l], lb_logits, a_norm_w[l], c_sinks[l], ln1_g[l], ln1_b[l], conv_b[l],
                                 ln2_g[l], ln2_b[l], tabs, l, loss_target[0] if l == DEPTH - 1 else None)
        saved.append(res)
    loss = lax.psum(xs, ("x", "y", "c"))
    dx = None

    exchange = _GradExchange(core, chip)
    small_parts = [None] * DEPTH
    for l in reversed(range(DEPTH)):
        dx, small = _layer_bwd(dx, saved[l], layer_ws[l].ready, lb_logits, a_norm_w[l], c_sinks[l], ln1_g[l], conv_b[l],
                               ln2_g[l], tabs, exchange, l)
        small_parts[l] = _pack(small, _LAYER_ROWS)
    (small_gather,), small_started = _start_gathers(
        [("small_grads", ("small",), [jnp.concatenate(small_parts, axis=0)])], me, "gather_start_small")
    updated = exchange.finish(w_views, as_slabs(mom1), as_slabs(mom2), [dx, small_started])
    gathered = small_gather.wait(updated["w_in"][0])["small"]
    updated = {n: tuple(jnp.swapaxes(t, 1, 2) for t in u) if n in _COLUMN_SHARDED else u for n, u in updated.items()}
    g_small = _small_reduce(gathered, lb_logits, name="small_grads")

    per_layer = [(A_HEADS * HEAD_DIM,), (HEAD_DIM,), (_PACK_LANES,), (D_MODEL,), (D_MODEL,), (D_MODEL,), (D_MODEL,), (D_FF,),
                 (3, D_FF)]
    names = ("lb_logits", "a_norm_w", "c_sinks", "ln1_g", "ln1_b", "ln2_g", "ln2_b", "conv_b", "conv_w")
    grads = {n: [] for n in names}
    for l in range(DEPTH):
        for n, t in zip(names, _unpack(g_small[l * _LAYER_ROWS:(l + 1) * _LAYER_ROWS], per_layer)):
            grads[n].append(t)
    grads = {n: jnp.stack(t) for n, t in grads.items()}
    grads["c_sinks"] = grads["c_sinks"][:, :C_HEADS]
    grads["conv_w"] = lax.dynamic_slice_in_dim(grads["conv_w"], me * SHARD_COLS, SHARD_COLS, axis=2)
    shapes = [grads[n].shape for n in names]
    rows = -(-sum(int(np.prod(s)) for s in shapes) // (8 * _PACK_LANES)) * 8
    d_s, m_s, v_s = _adamw_small(_pack([grads[n] for n in names], rows), _pack([weights[n] for n in names], rows),
                                 _pack([mom1[n] for n in names], rows), _pack([mom2[n] for n in names], rows),
                                 name="adamw_small")
    delta = dict(zip(names, _unpack(d_s, shapes)))
    new_m = dict(zip(names, _unpack(m_s, shapes)))
    new_v = dict(zip(names, _unpack(v_s, shapes)))
    for n in _BIG:
        grads[n], delta[n], new_m[n], new_v[n] = updated[n]

    order = ("w_in", "lb_logits", "a_norm_w", "c_sinks", "w_out", "ln1_g", "ln1_b", "w_gate", "w_up", "conv_w", "conv_b",
             "w_down", "ln2_g", "ln2_b")
    return (loss, dx[None], *[grads[n] for n in order], *[delta[n] for n in order], *[new_m[n] for n in order],
            *[new_v[n] for n in order])
```

```python
import functools

import jax
import jax.numpy as jnp
import numpy as np
from jax import lax
from jax.experimental import pallas as pl
from jax.experimental.pallas import tpu as pltpu

D_MODEL = 2048
SEQ = 2048
DEPTH = 2
HEAD_DIM = 128
A_HEADS = 4
B_HEADS = 6
C_HEADS = 6
C_KV_HEADS = 2
A_CHUNK = 16
DILATIONS = (1, 4, 16)
BLOCK = 128
ROPE_THETA = 500000.0
ROPE_DIM = 32
D_FF = 5632
IN_WIDTH = 5632
LN_EPS = 1e-5
ALPHA = (2 * DEPTH) ** 0.25
N_DEV = 8
SHARD_COLS = IN_WIDTH // N_DEV

ADAM_LR = 0.001
ADAM_B1 = 0.9
ADAM_B2 = 0.999
ADAM_EPS = 1e-08
ADAM_WD = 0.01
ADAM_STEP = 10

A_COLS = 16
QKV_COLS = 28
QB0, KB0, VB0, QC0, KC0, VC0 = 0, 6, 12, 18, 24, 26

_MXU_DTYPE = jnp.bfloat16
_GRAD_DTYPE = jnp.bfloat16
_ACT_DTYPE = jnp.bfloat16
_NEG = -1e30
_VMEM_LIMIT = 56 * 2 ** 20

_F32 = jnp.float32


def _sigmoid(x):
    return 0.5 * jnp.tanh(0.5 * x) + 0.5


def _cparams(**kw):
    return pltpu.CompilerParams(vmem_limit_bytes=_VMEM_LIMIT, **kw)


_TILE_MIX = dict(tm=1024, tn=1024)
_TILE_WIDE_K = dict(tm=1024, tn=512)
_TILE_WIDE_N = dict(tm=1024, tn=1408)
_TILE_WIDE_M = dict(tm=1408, tn=1024)


def _mm(a, b, *, ta=False, tb=False, tm, tn, out_dtype=_F32, add=None, add_scale=1.0, after=(), name):
    K = a.shape[0] if ta else a.shape[1]
    M = a.shape[1] if ta else a.shape[0]
    N = b.shape[0] if tb else b.shape[1]
    assert (b.shape[1] if tb else b.shape[0]) == K and M % tm == 0 and N % tn == 0
    dn = (((0 if ta else 1,), (1 if tb else 0,)), ((), ()))

    def body(*refs):
        a_ref, b_ref = refs[:2]
        o_ref = refs[-1]
        r = lax.dot_general(a_ref[...], b_ref[...], dn, preferred_element_type=_F32)
        if add is not None:
            r = r + add_scale * refs[2][...]
        o_ref[...] = r.astype(o_ref.dtype)

    a_spec = pl.BlockSpec((K, tm), lambda i, j: (0, i)) if ta else pl.BlockSpec((tm, K), lambda i, j: (i, 0))
    b_spec = pl.BlockSpec((tn, K), lambda i, j: (j, 0)) if tb else pl.BlockSpec((K, tn), lambda i, j: (0, j))
    o_spec = pl.BlockSpec((tm, tn), lambda i, j: (i, j))
    in_specs = [a_spec, b_spec] + ([o_spec] if add is not None else []) + [pl.BlockSpec(memory_space=pl.ANY)] * len(after)
    args = (a, b) + ((add,) if add is not None else ()) + tuple(after)
    return pl.pallas_call(
        body, grid=(M // tm, N // tn), in_specs=in_specs, out_specs=o_spec,
        out_shape=jax.ShapeDtypeStruct((M, N), out_dtype), name=name,
        compiler_params=_cparams(dimension_semantics=("parallel", "parallel")),
    )(*args)


def _cast_layer(w, layer, *, after=(), name):
    _, R, C = w.shape
    tr = max(t for t in range(16, R + 1, 16) if R % t == 0 and t * C <= 512 * 1024)

    def body(w_ref, *rest):
        o_ref = rest[-1]
        o_ref[...] = w_ref[...].astype(o_ref.dtype)

    return pl.pallas_call(
        body, grid=(R // tr,),
        in_specs=[pl.BlockSpec((None, tr, C), lambda i: (layer, i, 0))] + [pl.BlockSpec(memory_space=pl.ANY)] * len(after),
        out_specs=pl.BlockSpec((tr, C), lambda i: (i, 0)), out_shape=jax.ShapeDtypeStruct((R, C), _MXU_DTYPE), name=name,
        compiler_params=_cparams(dimension_semantics=("parallel",)),
    )(w, *after)


def _concat_cols(pieces, *, name):
    tm = 512
    widths = [p.shape[1] for p in pieces]
    offs = np.cumsum([0] + widths)

    def body(*refs):
        o_ref = refs[-1]
        for p_ref, off, w in zip(refs[:-1], offs, widths):
            o_ref[:, off:off + w] = p_ref[...].astype(o_ref.dtype)

    return pl.pallas_call(
        body, grid=(SEQ // tm,), in_specs=[pl.BlockSpec((tm, w), lambda i: (i, 0)) for w in widths],
        out_specs=pl.BlockSpec((tm, int(offs[-1])), lambda i: (i, 0)),
        out_shape=jax.ShapeDtypeStruct((SEQ, int(offs[-1])), _MXU_DTYPE), name=name,
        compiler_params=_cparams(dimension_semantics=("parallel",)),
    )(*pieces)


def _ln_fwd(x, y, g, b, *, name):
    tm = 256

    def body(x_ref, y_ref, g_ref, b_ref, z_ref, o_ref, ob_ref):
        z = ALPHA * x_ref[...] + y_ref[...]
        mu = jnp.mean(z, axis=-1, keepdims=True)
        zc = z - mu
        var = jnp.mean(zc * zc, axis=-1, keepdims=True)
        o = zc * lax.rsqrt(var + LN_EPS) * g_ref[...] + b_ref[...]
        z_ref[...] = z
        o_ref[...] = o
        ob_ref[...] = o.astype(ob_ref.dtype)

    row = pl.BlockSpec((tm, D_MODEL), lambda i: (i, 0))
    vec = pl.BlockSpec((1, D_MODEL), lambda i: (0, 0))
    return pl.pallas_call(
        body, grid=(SEQ // tm,), in_specs=[row, row, vec, vec], out_specs=[row, row, row],
        out_shape=[jax.ShapeDtypeStruct((SEQ, D_MODEL), _F32), jax.ShapeDtypeStruct((SEQ, D_MODEL), _F32),
                   jax.ShapeDtypeStruct((SEQ, D_MODEL), _MXU_DTYPE)],
        name=name, compiler_params=_cparams(dimension_semantics=("parallel",)),
    )(x, y, g.reshape(1, D_MODEL), b.reshape(1, D_MODEL))


def _ln_bwd(z, d_a, d_res, g, *, name):
    tm = 256

    def body(*refs):
        if d_res is None:
            z_ref, da_ref, g_ref, dz_ref, dzb_ref, dg_ref, db_ref = refs
        else:
            z_ref, da_ref, dr_ref, g_ref, dz_ref, dzb_ref, dg_ref, db_ref = refs

        @pl.when(pl.program_id(0) == 0)
        def _():
            dg_ref[...] = jnp.zeros_like(dg_ref)
            db_ref[...] = jnp.zeros_like(db_ref)

        dout = da_ref[...]
        if d_res is not None:
            dout = dout + ALPHA * dr_ref[...]
        z = z_ref[...]
        mu = jnp.mean(z, axis=-1, keepdims=True)
        zc = z - mu
        var = jnp.mean(zc * zc, axis=-1, keepdims=True)
        rstd = lax.rsqrt(var + LN_EPS)
        xh = zc * rstd
        dxh = dout * g_ref[...]
        m1 = jnp.mean(dxh, axis=-1, keepdims=True)
        m2 = jnp.mean(dxh * xh, axis=-1, keepdims=True)
        dz = rstd * (dxh - m1 - xh * m2)
        dz_ref[...] = dz
        dzb_ref[...] = dz.astype(dzb_ref.dtype)
        dg_ref[0:1, :] += jnp.sum(dout * xh, axis=0, keepdims=True)
        db_ref[0:1, :] += jnp.sum(dout, axis=0, keepdims=True)

    row = pl.BlockSpec((tm, D_MODEL), lambda i: (i, 0))
    vec = pl.BlockSpec((1, D_MODEL), lambda i: (0, 0))
    acc = pl.BlockSpec((8, D_MODEL), lambda i: (0, 0))
    ins = [z, d_a] + ([d_res] if d_res is not None else []) + [g.reshape(1, D_MODEL)]
    in_specs = [row, row] + ([row] if d_res is not None else []) + [vec]
    dz, dzb, dg, db = pl.pallas_call(
        body, grid=(SEQ // tm,), in_specs=in_specs, out_specs=[row, row, acc, acc],
        out_shape=[jax.ShapeDtypeStruct((SEQ, D_MODEL), _F32), jax.ShapeDtypeStruct((SEQ, D_MODEL), _MXU_DTYPE),
                   jax.ShapeDtypeStruct((8, D_MODEL), _F32), jax.ShapeDtypeStruct((8, D_MODEL), _F32)],
        name=name, compiler_params=_cparams(dimension_semantics=("arbitrary",)),
    )(*ins)
    return dz, dzb, dg[0], db[0]


def _ln_loss_bwd(x, y, g, b, target, *, name):
    tm = 256

    def body(x_ref, y_ref, g_ref, b_ref, t_ref, dz_ref, dzb_ref, dg_ref, db_ref, l_ref):
        @pl.when(pl.program_id(0) == 0)
        def _():
            dg_ref[...] = jnp.zeros_like(dg_ref)
            db_ref[...] = jnp.zeros_like(db_ref)
            l_ref[...] = jnp.zeros_like(l_ref)

        z = ALPHA * x_ref[...] + y_ref[...]
        mu = jnp.mean(z, axis=-1, keepdims=True)
        zc = z - mu
        var = jnp.mean(zc * zc, axis=-1, keepdims=True)
        rstd = lax.rsqrt(var + LN_EPS)
        xh = zc * rstd
        e = xh * g_ref[...] + b_ref[...] - t_ref[...]
        l_ref[...] += (0.5 / D_MODEL) * jnp.sum(e * e)
        dout = e * (1.0 / D_MODEL)
        dxh = dout * g_ref[...]
        m1 = jnp.mean(dxh, axis=-1, keepdims=True)
        m2 = jnp.mean(dxh * xh, axis=-1, keepdims=True)
        dz = rstd * (dxh - m1 - xh * m2)
        dz_ref[...] = dz
        dzb_ref[...] = dz.astype(dzb_ref.dtype)
        dg_ref[0:1, :] += jnp.sum(dout * xh, axis=0, keepdims=True)
        db_ref[0:1, :] += jnp.sum(dout, axis=0, keepdims=True)

    row = pl.BlockSpec((tm, D_MODEL), lambda i: (i, 0))
    vec = pl.BlockSpec((1, D_MODEL), lambda i: (0, 0))
    acc = pl.BlockSpec((8, D_MODEL), lambda i: (0, 0))
    dz, dzb, dg, db, part = pl.pallas_call(
        body, grid=(SEQ // tm,), in_specs=[row, row, vec, vec, row],
        out_specs=[row, row, acc, acc, pl.BlockSpec((8, 128), lambda i: (0, 0))],
        out_shape=[jax.ShapeDtypeStruct((SEQ, D_MODEL), _F32), jax.ShapeDtypeStruct((SEQ, D_MODEL), _MXU_DTYPE),
                   jax.ShapeDtypeStruct((8, D_MODEL), _F32), jax.ShapeDtypeStruct((8, D_MODEL), _F32),
                   jax.ShapeDtypeStruct((8, 128), _F32)],
        name=name, compiler_params=_cparams(dimension_semantics=("arbitrary",)),
    )(x, y, g.reshape(1, D_MODEL), b.reshape(1, D_MODEL), target)
    return part[0, 0], dz, dzb, dg[0], db[0]


_CONV_TN = 256


def _shift_down(v, k, rows):
    return jnp.where(rows >= k, pltpu.roll(v, k, axis=0), 0.0)


def _shift_up(v, k, rows):
    return jnp.where(rows < SEQ - k, pltpu.roll(v, SEQ - k, axis=0), 0.0)


def _conv_gate_fwd(g, u, conv_w, conv_b, *, name):
    def body(g_ref, u_ref, w_ref, b_ref, h_ref):
        gv = g_ref[...].astype(_F32)
        rows = lax.broadcasted_iota(jnp.int32, gv.shape, 0)
        w = w_ref[...]
        gc = b_ref[...] + w[2:3, :] * gv + w[1:2, :] * _shift_down(gv, 1, rows) + w[0:1, :] * _shift_down(gv, 2, rows)
        h_ref[...] = (gc * _sigmoid(gc) * u_ref[...].astype(_F32)).astype(h_ref.dtype)

    col = pl.BlockSpec((SEQ, _CONV_TN), lambda j: (0, j))
    return pl.pallas_call(
        body, grid=(D_FF // _CONV_TN,),
        in_specs=[col, col, pl.BlockSpec((3, _CONV_TN), lambda j: (0, j)), pl.BlockSpec((1, _CONV_TN), lambda j: (0, j))],
        out_specs=col, out_shape=jax.ShapeDtypeStruct((SEQ, D_FF), _MXU_DTYPE), name=name,
        compiler_params=_cparams(dimension_semantics=("parallel",)),
    )(g, u, conv_w, conv_b.reshape(1, D_FF))


def _conv_gate_bwd(dh, g, u, conv_w, conv_b, *, name):
    def body(dh_ref, g_ref, u_ref, w_ref, b_ref, dg_ref, du_ref, dw_ref, db_ref):
        gv = g_ref[...].astype(_F32)
        rows = lax.broadcasted_iota(jnp.int32, gv.shape, 0)
        w = w_ref[...]
        g1 = _shift_down(gv, 1, rows)
        g2 = _shift_down(gv, 2, rows)
        gc = b_ref[...] + w[2:3, :] * gv + w[1:2, :] * g1 + w[0:1, :] * g2
        sg = _sigmoid(gc)
        dh = dh_ref[...].astype(_F32)
        du_ref[...] = (dh * (gc * sg)).astype(du_ref.dtype)
        dgc = dh * u_ref[...].astype(_F32) * (sg * (1.0 + gc * (1.0 - sg)))
        dg = w[2:3, :] * dgc + w[1:2, :] * _shift_up(dgc, 1, rows) + w[0:1, :] * _shift_up(dgc, 2, rows)
        dg_ref[...] = dg.astype(dg_ref.dtype)
        dw_ref[0:1, :] = jnp.sum(dgc * g2, axis=0, keepdims=True)
        dw_ref[1:2, :] = jnp.sum(dgc * g1, axis=0, keepdims=True)
        dw_ref[2:3, :] = jnp.sum(dgc * gv, axis=0, keepdims=True)
        db_ref[...] = jnp.sum(dgc, axis=0, keepdims=True)

    col = pl.BlockSpec((SEQ, _CONV_TN), lambda j: (0, j))
    w3 = pl.BlockSpec((3, _CONV_TN), lambda j: (0, j))
    w1 = pl.BlockSpec((1, _CONV_TN), lambda j: (0, j))
    dg, du, dw, db = pl.pallas_call(
        body, grid=(D_FF // _CONV_TN,), in_specs=[col, col, col, w3, w1], out_specs=[col, col, w3, w1],
        out_shape=[jax.ShapeDtypeStruct((SEQ, D_FF), _MXU_DTYPE), jax.ShapeDtypeStruct((SEQ, D_FF), _MXU_DTYPE),
                   jax.ShapeDtypeStruct((3, D_FF), _F32), jax.ShapeDtypeStruct((1, D_FF), _F32)],
        name=name, compiler_params=_cparams(dimension_semantics=("parallel",)),
    )(dh, g, u, conv_w, conv_b.reshape(1, D_FF))
    return dg, du, dw, db[0]


def _rope_tables():
    half = ROPE_DIM // 2
    inv = ROPE_THETA ** (-jnp.arange(0, ROPE_DIM, 2, dtype=_F32) / ROPE_DIM)
    ang = jnp.arange(SEQ, dtype=_F32)[:, None] * inv[None, :]
    cos, sin = jnp.cos(ang), jnp.sin(ang)
    rest = HEAD_DIM - ROPE_DIM
    c = jnp.concatenate([cos, cos, jnp.ones((SEQ, rest), _F32)], axis=1)
    s1 = jnp.concatenate([-sin, jnp.zeros((SEQ, HEAD_DIM - half), _F32)], axis=1)
    s2 = jnp.concatenate([jnp.zeros((SEQ, half), _F32), sin, jnp.zeros((SEQ, rest), _F32)], axis=1)
    return c, s1, s2


def _rope_apply(x, c, s1, s2):
    return x * c + pltpu.roll(x, HEAD_DIM - ROPE_DIM // 2, axis=1) * s1 + pltpu.roll(x, ROPE_DIM // 2, axis=1) * s2


def _rope_transpose(d, c, s1, s2):
    half = ROPE_DIM // 2
    return d * c + pltpu.roll(d * s1, half, axis=1) + pltpu.roll(d * s2, HEAD_DIM - half, axis=1)


_NT = (((1,), (1,)), ((), ()))
_TN = (((0,), (0,)), ((), ()))
_SCALE = HEAD_DIM ** -0.5


def _band_scores(q, k2, n, lag_off):
    s = lax.dot_general(q, k2, _NT, preferred_element_type=_F32) * _SCALE
    row = lax.broadcasted_iota(jnp.int32, (BLOCK, 2 * BLOCK), 0)
    col = lax.broadcasted_iota(jnp.int32, (BLOCK, 2 * BLOCK), 1)
    front = (col >= row + lag_off) & (col < BLOCK) & (n > 0)
    own = (col >= BLOCK) & (col <= row + BLOCK)
    return jnp.where(front | own, s, _NEG)


_BAND_STEPS = SEQ // BLOCK


def _rows(start, d):
    if d == 1:
        return pl.ds(pl.multiple_of(start, BLOCK), BLOCK)
    return pl.ds(start, BLOCK, stride=d)


def _band_block(it, d):
    r, n = it % d, it // d
    span = BLOCK * d
    return n, _rows(r + n * span, d), _rows(r + jnp.maximum(n - 1, 0) * span, d)


def _band_fwd(proj, tabs, *, kv_heads, q_per_kv, q0, k0, v0, dilations, lag_off, sink, name):
    heads = kv_heads * q_per_kv

    def body(*refs):
        q_refs = refs[:q_per_kv]
        k_ref, v_ref, c_ref, s1_ref, s2_ref = refs[q_per_kv:q_per_kv + 5]
        rest = refs[q_per_kv + 5:]
        if sink is not None:
            sk_ref, rest = rest[0], rest[1:]
        o_ref, lse_ref, qs, ks, m_s, l_s, acc_s = rest
        c, s1, s2 = c_ref[...], s1_ref[...], s2_ref[...]
        ks[...] = _rope_apply(k_ref[...], c, s1, s2)
        for i in range(q_per_kv):
            qs[...] = _rope_apply(q_refs[i][...], c, s1, s2)
            for pi, d in enumerate(dilations):
                def step(it, carry, d=d, first=(pi == 0)):
                    n, cur, prev = _band_block(it, d)
                    q = qs[cur, :].astype(_MXU_DTYPE)
                    k2 = jnp.concatenate([ks[prev, :], ks[cur, :]], axis=0).astype(_MXU_DTYPE)
                    v2 = jnp.concatenate([v_ref[prev, :], v_ref[cur, :]], axis=0).astype(_MXU_DTYPE)
                    s = _band_scores(q, k2, n, lag_off)
                    m_b = jnp.max(s, axis=1, keepdims=True)
                    m_new = m_b if first else jnp.maximum(m_b, m_s[cur, :][:, 0:1])
                    p = jnp.exp(s - m_new)
                    l_new = jnp.sum(p, axis=1, keepdims=True)
                    acc = jnp.dot(p.astype(_MXU_DTYPE), v2, preferred_element_type=_F32)
                    if not first:
                        a = jnp.exp(m_s[cur, :][:, 0:1] - m_new)
                        l_new = l_new + a * l_s[cur, :][:, 0:1]
                        acc = acc + a * acc_s[cur, :]
                    m_s[cur, :] = jnp.broadcast_to(m_new, (BLOCK, HEAD_DIM))
                    l_s[cur, :] = jnp.broadcast_to(l_new, (BLOCK, HEAD_DIM))
                    acc_s[cur, :] = acc
                    return carry

                lax.fori_loop(0, _BAND_STEPS, step, 0, unroll=16)
            m, den = m_s[...], l_s[...]
            if sink is not None:
                sk = sk_ref[i]
                m_f = jnp.maximum(m, sk)
                a = jnp.exp(m - m_f)
                den = den * a + jnp.exp(sk - m_f)
                o = acc_s[...] * a / den
                m = m_f
            else:
                o = acc_s[...] / den
            o_ref[:, i * HEAD_DIM:(i + 1) * HEAD_DIM] = o
            lse_ref[:, i * HEAD_DIM:(i + 1) * HEAD_DIM] = m + jnp.log(den)

    col = (SEQ, HEAD_DIM)
    in_specs = [pl.BlockSpec(col, functools.partial(lambda g, i: (0, A_COLS + q0 + g * q_per_kv + i), i=i)) for i in range(q_per_kv)]
    in_specs += [pl.BlockSpec(col, lambda g: (0, A_COLS + k0 + g)), pl.BlockSpec(col, lambda g: (0, A_COLS + v0 + g))]
    in_specs += [pl.BlockSpec(col, lambda g: (0, 0))] * 3
    args = [proj] * (q_per_kv + 2) + list(tabs)
    if sink is not None:
        in_specs.append(pl.BlockSpec((q_per_kv, 1, HEAD_DIM), lambda g: (g, 0, 0)))
        args.append(jnp.broadcast_to(sink.reshape(heads, 1, 1), (heads, 1, HEAD_DIM)))
    o_spec = pl.BlockSpec((SEQ, q_per_kv * HEAD_DIM), lambda g: (0, g))
    shape = jax.ShapeDtypeStruct((SEQ, heads * HEAD_DIM), _F32)
    return pl.pallas_call(
        body, grid=(kv_heads,), in_specs=in_specs, out_specs=[o_spec, o_spec], out_shape=[shape, shape],
        scratch_shapes=[pltpu.VMEM(col, _F32)] * 5, name=name,
        compiler_params=_cparams(dimension_semantics=("parallel",)),
    )(*args)


def _band_bwd(proj, tabs, dmixed, o, lse, *, kv_heads, q_per_kv, q0, k0, v0, do0, dilations, lag_off, sink, after=(), name):
    heads = kv_heads * q_per_kv

    def body(*refs):
        q_refs = refs[:q_per_kv]
        k_ref, v_ref, c_ref, s1_ref, s2_ref = refs[q_per_kv:q_per_kv + 5]
        do_refs = refs[q_per_kv + 5:2 * q_per_kv + 5]
        o_ref, lse_ref = refs[2 * q_per_kv + 5:2 * q_per_kv + 7]
        rest = refs[2 * q_per_kv + 7:]
        if sink is not None:
            sk_ref, rest = rest[0], rest[1:]
            dq_ref, dk_ref, dv_ref, dsk_ref, qs, ks, dq_s, dk_s, dv_s = rest[len(after):]
        else:
            dq_ref, dk_ref, dv_ref, qs, ks, dq_s, dk_s, dv_s = rest[len(after):]
        c, s1, s2 = c_ref[...], s1_ref[...], s2_ref[...]
        ks[...] = _rope_apply(k_ref[...], c, s1, s2)
        dk_s[...] = jnp.zeros_like(dk_s)
        dv_s[...] = jnp.zeros_like(dv_s)
        for i in range(q_per_kv):
            hs = slice(i * HEAD_DIM, (i + 1) * HEAD_DIM)
            qs[...] = _rope_apply(q_refs[i][...], c, s1, s2)
            dq_s[...] = jnp.zeros_like(dq_s)
            do_ref = do_refs[i]
            for d in dilations:
                def step(it, carry, d=d, do_ref=do_ref, hs=hs):
                    n, cur, prev = _band_block(it, d)
                    q = qs[cur, :].astype(_MXU_DTYPE)
                    k2 = jnp.concatenate([ks[prev, :], ks[cur, :]], axis=0).astype(_MXU_DTYPE)
                    v2 = jnp.concatenate([v_ref[prev, :], v_ref[cur, :]], axis=0).astype(_MXU_DTYPE)
                    do = do_ref[cur, :]
                    delta = jnp.sum(do * o_ref[cur, hs], axis=1, keepdims=True)
                    lse_c = lse_ref[cur, hs][:, 0:1]
                    p = jnp.exp(_band_scores(q, k2, n, lag_off) - lse_c)
                    dob = do.astype(_MXU_DTYPE)
                    ds = (p * (lax.dot_general(dob, v2, _NT, preferred_element_type=_F32) - delta) * _SCALE).astype(_MXU_DTYPE)
                    dq_s[cur, :] += jnp.dot(ds, k2, preferred_element_type=_F32)
                    dk2 = lax.dot_general(ds, q, _TN, preferred_element_type=_F32)
                    dv2 = lax.dot_general(p.astype(_MXU_DTYPE), dob, _TN, preferred_element_type=_F32)
                    dk_s[prev, :] += dk2[:BLOCK]
                    dv_s[prev, :] += dv2[:BLOCK]
                    dk_s[cur, :] += dk2[BLOCK:]
                    dv_s[cur, :] += dv2[BLOCK:]
                    return carry

                lax.fori_loop(0, _BAND_STEPS, step, 0, unroll=16)
            dq_ref[:, hs] = _rope_transpose(dq_s[...], c, s1, s2).astype(dq_ref.dtype)
            if sink is not None:
                delta = jnp.sum(do_ref[...] * o_ref[:, hs], axis=1, keepdims=True)
                w_sink = jnp.exp(sk_ref[i] - lse_ref[:, hs])
                dsk_ref[i] = jnp.broadcast_to(jnp.sum(-delta * w_sink[:, 0:1]), (8, HEAD_DIM))
        dk_ref[...] = _rope_transpose(dk_s[...], c, s1, s2).astype(dk_ref.dtype)
        dv_ref[...] = dv_s[...].astype(dv_ref.dtype)

    col = (SEQ, HEAD_DIM)
    in_specs = [pl.BlockSpec(col, functools.partial(lambda g, i: (0, A_COLS + q0 + g * q_per_kv + i), i=i)) for i in range(q_per_kv)]
    in_specs += [pl.BlockSpec(col, lambda g: (0, A_COLS + k0 + g)), pl.BlockSpec(col, lambda g: (0, A_COLS + v0 + g))]
    in_specs += [pl.BlockSpec(col, lambda g: (0, 0))] * 3
    in_specs += [pl.BlockSpec(col, functools.partial(lambda g, i: (0, do0 + g * q_per_kv + i), i=i)) for i in range(q_per_kv)]
    wide = pl.BlockSpec((SEQ, q_per_kv * HEAD_DIM), lambda g: (0, g))
    in_specs += [wide, wide]
    args = [proj] * (q_per_kv + 2) + list(tabs) + [dmixed] * q_per_kv + [o, lse]
    out_specs = [wide, pl.BlockSpec(col, lambda g: (0, g)), pl.BlockSpec(col, lambda g: (0, g))]
    out_shape = [jax.ShapeDtypeStruct((SEQ, heads * HEAD_DIM), _MXU_DTYPE), jax.ShapeDtypeStruct((SEQ, kv_heads * HEAD_DIM), _MXU_DTYPE),
                 jax.ShapeDtypeStruct((SEQ, kv_heads * HEAD_DIM), _MXU_DTYPE)]
    if sink is not None:
        in_specs.append(pl.BlockSpec((q_per_kv, 1, HEAD_DIM), lambda g: (g, 0, 0)))
        args.append(jnp.broadcast_to(sink.reshape(heads, 1, 1), (heads, 1, HEAD_DIM)))
        out_specs.append(pl.BlockSpec((q_per_kv, 8, HEAD_DIM), lambda g: (g, 0, 0)))
        out_shape.append(jax.ShapeDtypeStruct((heads, 8, HEAD_DIM), _F32))
    in_specs += [pl.BlockSpec(memory_space=pl.ANY)] * len(after)
    args += list(after)
    res = pl.pallas_call(
        body, grid=(kv_heads,), in_specs=in_specs, out_specs=out_specs, out_shape=out_shape,
        scratch_shapes=[pltpu.VMEM(col, _F32)] * 5, name=name,
        compiler_params=_cparams(dimension_semantics=("parallel",)),
    )(*args)
    if sink is not None:
        return res[0], res[1], res[2], res[3][:, 0, 0]
    return res


_DILATED = dict(kv_heads=B_HEADS, q_per_kv=1, q0=QB0, k0=KB0, v0=VB0, dilations=DILATIONS, lag_off=0, sink=None)
_SWA = dict(kv_heads=C_KV_HEADS, q_per_kv=C_HEADS // C_KV_HEADS, q0=QC0, k0=KC0, v0=VC0, dilations=(1,), lag_off=1)


_HG_TILE = 256
_HG_CHUNKS = _HG_TILE // A_CHUNK
_HG_TILES = SEQ // _HG_TILE
_HI = lax.Precision.HIGHEST


def _chunk_tri():
    i = np.arange(_HG_TILE)
    return jnp.asarray(((i[:, None] // A_CHUNK == i[None, :] // A_CHUNK) & (i[None, :] <= i[:, None])).astype(np.float32))


def _layer_lb(lb_ref, layer):
    if layer == 0:
        return jnp.zeros((1, HEAD_DIM), _F32)
    lg = lb_ref[...]
    m = jnp.max(lg, axis=0, keepdims=True)
    e = jnp.exp(lg - m)
    return e[1:2, :] / jnp.sum(e, axis=0, keepdims=True)


def _hgrn_gates(q, fr, lb):
    sgq = _sigmoid(q)
    sg = _sigmoid(fr)
    f = lb + (1.0 - lb) * sg
    return sgq, q * sgq, sg, f, 1.0 - f


def _hgrn_fwd(proj, lb_logits, norm_w, layer, *, name):
    tri = _chunk_tri()

    def body(q_ref, f_ref, i_ref, g_ref, lb_ref, nw_ref, tri_ref, o_ref, raw_ref, st_ref, state):
        @pl.when(pl.program_id(1) == 0)
        def _():
            state[...] = jnp.zeros_like(state)

        lb = _layer_lb(lb_ref, layer)
        _, qs, _, f, k = _hgrn_gates(q_ref[...], f_ref[...], lb)
        v = i_ref[...]
        b = jnp.dot(tri_ref[...], jnp.log(f), precision=_HI, preferred_element_type=_F32)
        eb = jnp.exp(b)
        ridx = lax.broadcasted_iota(jnp.int32, (A_CHUNK, HEAD_DIM), 0)
        ups, lams = [], []
        for c in range(_HG_CHUNKS):
            sl = slice(c * A_CHUNK, (c + 1) * A_CHUNK)
            bc = b[sl]
            bl = bc[A_CHUNK - 1:A_CHUNK]
            kt = (k[sl] * jnp.exp(bl - bc)).astype(_MXU_DTYPE)
            ups.append(lax.dot_general(v[sl].astype(_MXU_DTYPE), kt, _TN, preferred_element_type=_F32))
            lams.append(jnp.exp(bl))
        outs = []
        st = state[...]
        for c in range(_HG_CHUNKS):
            sl = slice(c * A_CHUNK, (c + 1) * A_CHUNK)
            bc, qc, kc, vc = b[sl], qs[sl], k[sl], v[sl]
            st_ref[0, c] = st
            o_c = lax.dot_general((qc * eb[sl]).astype(_MXU_DTYPE), st.astype(_MXU_DTYPE), _NT, preferred_element_type=_F32)
            for j in range(A_CHUNK):
                dj = jnp.exp(jnp.where(ridx >= j, bc - bc[j:j + 1], _NEG))
                a = jnp.sum(qc * kc[j:j + 1] * dj, axis=1, keepdims=True)
                o_c = o_c + a * vc[j:j + 1]
            outs.append(o_c)
            st = st * lams[c] + ups[c]
        state[...] = st
        o = jnp.concatenate(outs, axis=0)
        raw_ref[...] = o
        r = lax.rsqrt(jnp.mean(o * o, axis=-1, keepdims=True) + LN_EPS)
        g = g_ref[...]
        o_ref[...] = o * r * nw_ref[...] * (g * _sigmoid(g))

    blk = (_HG_TILE, HEAD_DIM)

    def col(base):
        return pl.BlockSpec(blk, lambda h, t: (t, base + h))

    o_spec = pl.BlockSpec(blk, lambda h, t: (t, h))
    o_shape = jax.ShapeDtypeStruct((SEQ, A_HEADS * HEAD_DIM), _F32)
    return pl.pallas_call(
        body, grid=(A_HEADS, _HG_TILES),
        in_specs=[col(0), col(4), col(8), col(12), pl.BlockSpec((DEPTH, HEAD_DIM), lambda h, t: (0, h)),
                  pl.BlockSpec((1, HEAD_DIM), lambda h, t: (0, 0)), pl.BlockSpec((_HG_TILE, _HG_TILE), lambda h, t: (0, 0))],
        out_specs=[o_spec, o_spec, pl.BlockSpec((1, _HG_CHUNKS, HEAD_DIM, HEAD_DIM), lambda h, t: (h, t, 0, 0))],
        out_shape=[o_shape, o_shape, jax.ShapeDtypeStruct((A_HEADS, SEQ // A_CHUNK, HEAD_DIM, HEAD_DIM), _F32)],
        scratch_shapes=[pltpu.VMEM((HEAD_DIM, HEAD_DIM), _F32)], name=name,
        compiler_params=_cparams(dimension_semantics=("parallel", "arbitrary")),
    )(proj, proj, proj, proj, lb_logits, norm_w.reshape(1, HEAD_DIM), tri)


def _hgrn_bwd(proj, lb_logits, norm_w, raw, states, dmixed, layer, *, name):
    tri = _chunk_tri()
    triu = tri.T

    def body(q_ref, f_ref, i_ref, g_ref, lb_ref, nw_ref, tri_ref, triu_ref, raw_ref, do_ref, st_ref,
             dq_ref, df_ref, di_ref, dg_ref, dnw_ref, dlb_ref, dstate):
        @pl.when(pl.program_id(1) == 0)
        def _():
            dstate[...] = jnp.zeros_like(dstate)
            dlb_ref[...] = jnp.zeros_like(dlb_ref)

        @pl.when((pl.program_id(0) == 0) & (pl.program_id(1) == 0))
        def _():
            dnw_ref[...] = jnp.zeros_like(dnw_ref)

        lb = _layer_lb(lb_ref, layer)
        q = q_ref[...]
        sgq, qs, sg, f, k = _hgrn_gates(q, f_ref[...], lb)
        v = i_ref[...]
        b = jnp.dot(tri_ref[...], jnp.log(f), precision=_HI, preferred_element_type=_F32)
        eb = jnp.exp(b)
        g = g_ref[...]
        nw = nw_ref[...]
        o = raw_ref[...]
        dout = do_ref[...]
        sgg = _sigmoid(g)
        r = lax.rsqrt(jnp.mean(o * o, axis=-1, keepdims=True) + LN_EPS)
        dg_ref[...] = (dout * (o * r * nw) * (sgg * (1.0 + g * (1.0 - sgg)))).astype(dg_ref.dtype)
        don = dout * (g * sgg)
        dnw_ref[0:1, :] += jnp.sum(don * o * r, axis=0, keepdims=True)
        dy = don * nw
        do_raw = r * dy - o * (r * r * r) * jnp.mean(o * dy, axis=-1, keepdims=True)

        ridx = lax.broadcasted_iota(jnp.int32, (A_CHUNK, HEAD_DIM), 0)
        dqs_t, dk_t, db_t, dv_t = [None] * _HG_CHUNKS, [None] * _HG_CHUNKS, [None] * _HG_CHUNKS, [None] * _HG_CHUNKS
        dqts, dups = [None] * _HG_CHUNKS, [None] * _HG_CHUNKS
        for c in range(_HG_CHUNKS):
            sl = slice(c * A_CHUNK, (c + 1) * A_CHUNK)
            dob = do_raw[sl].astype(_MXU_DTYPE)
            dqts[c] = jnp.dot(dob, st_ref[0, c].astype(_MXU_DTYPE), preferred_element_type=_F32)
            dups[c] = lax.dot_general(dob, (qs[sl] * eb[sl]).astype(_MXU_DTYPE), _TN, preferred_element_type=_F32)
        dst = dstate[...]
        for c in reversed(range(_HG_CHUNKS)):
            sl = slice(c * A_CHUNK, (c + 1) * A_CHUNK)
            bc, qc, kc, vc, doc = b[sl], qs[sl], k[sl], v[sl], do_raw[sl]
            bl = bc[A_CHUNK - 1:A_CHUNK]
            ebc = eb[sl]
            ebl = jnp.exp(bl - bc)
            lam = jnp.exp(bl)
            qt = qc * ebc
            kt = kc * ebl
            stp = st_ref[0, c]
            dstb = dst.astype(_MXU_DTYPE)
            dqt = dqts[c]
            dkt = jnp.dot(vc.astype(_MXU_DTYPE), dstb, preferred_element_type=_F32)
            dv = lax.dot_general(kt.astype(_MXU_DTYPE), dstb, _NT, preferred_element_type=_F32)
            dlam = jnp.sum(stp * dst, axis=0, keepdims=True)
            dst = dst * lam + dups[c]
            dqs_rows = []
            dk_in = jnp.zeros((A_CHUNK, HEAD_DIM), _F32)
            for i in range(A_CHUNK):
                di = jnp.exp(jnp.where(ridx <= i, bc[i:i + 1] - bc, _NEG))
                qi = qc[i:i + 1]
                doi = doc[i:i + 1]
                w = kc * di
                a = jnp.sum(qi * w, axis=1, keepdims=True)
                dv = dv + a * doi
                da = jnp.sum(doi * vc, axis=1, keepdims=True)
                dqs_rows.append(jnp.sum(da * w, axis=0, keepdims=True))
                dk_in = dk_in + da * (qi * di)
            dqs_in = jnp.concatenate(dqs_rows, axis=0)
            dbl = jnp.sum(dkt * kt, axis=0, keepdims=True) + dlam * lam
            db = qc * dqs_in - kc * dk_in + dqt * qt - dkt * kt
            db_t[c] = db + jnp.where(ridx == A_CHUNK - 1, dbl, 0.0)
            dqs_t[c] = dqs_in + dqt * ebc
            dk_t[c] = dk_in + dkt * ebl
            dv_t[c] = dv
        dstate[...] = dst
        dqs = jnp.concatenate(dqs_t, axis=0)
        dk = jnp.concatenate(dk_t, axis=0)
        db = jnp.concatenate(db_t, axis=0)
        di_ref[...] = jnp.concatenate(dv_t, axis=0).astype(di_ref.dtype)
        dlogf = jnp.dot(triu_ref[...], db, precision=_HI, preferred_element_type=_F32)
        df = dlogf / f - dk
        df_ref[...] = (df * (1.0 - lb) * sg * (1.0 - sg)).astype(df_ref.dtype)
        dlb_ref[0, 0:1, :] += jnp.sum(df * (1.0 - sg), axis=0, keepdims=True)
        dq_ref[...] = (dqs * (sgq * (1.0 + q * (1.0 - sgq)))).astype(dq_ref.dtype)

    blk = (_HG_TILE, HEAD_DIM)
    last = _HG_TILES - 1

    def col(base):
        return pl.BlockSpec(blk, lambda h, t: (last - t, base + h))

    tri_spec = pl.BlockSpec((_HG_TILE, _HG_TILE), lambda h, t: (0, 0))
    acc_spec = pl.BlockSpec((1, 8, HEAD_DIM), lambda h, t: (h, 0, 0))
    acc_shape = jax.ShapeDtypeStruct((A_HEADS, 8, HEAD_DIM), _F32)
    dq, df, di, dg, dnw, dlb = pl.pallas_call(
        body, grid=(A_HEADS, _HG_TILES),
        in_specs=[col(0), col(4), col(8), col(12), pl.BlockSpec((DEPTH, HEAD_DIM), lambda h, t: (0, h)),
                  pl.BlockSpec((1, HEAD_DIM), lambda h, t: (0, 0)), tri_spec, tri_spec, col(0), col(0),
                  pl.BlockSpec((1, _HG_CHUNKS, HEAD_DIM, HEAD_DIM), lambda h, t: (h, last - t, 0, 0))],
        out_specs=[col(0), col(0), col(0), col(0), pl.BlockSpec((8, HEAD_DIM), lambda h, t: (0, 0)), acc_spec],
        out_shape=[jax.ShapeDtypeStruct((SEQ, A_HEADS * HEAD_DIM), _MXU_DTYPE)] * 4
        + [jax.ShapeDtypeStruct((8, HEAD_DIM), _F32), acc_shape],
        scratch_shapes=[pltpu.VMEM((HEAD_DIM, HEAD_DIM), _F32)], name=name,
        compiler_params=_cparams(dimension_semantics=("arbitrary", "arbitrary")),
    )(proj, proj, proj, proj, lb_logits, norm_w.reshape(1, HEAD_DIM), tri, triu, raw, dmixed, states)
    return dq, df, di, dg, dnw[0], dlb[:, 0, :].reshape(A_HEADS * HEAD_DIM)


N_CHIP = N_DEV // 2
_MESH_ID = pl.DeviceIdType.MESH


def _place():
    x, y, c = lax.axis_index("x"), lax.axis_index("y"), lax.axis_index("c")
    chips = [(1 - x, y), (x, 1 - y), (1 - x, 1 - y)]
    return x, y, c, 2 * x + y, chips


def _sibling_swap(arrays, *, name):
    n = len(arrays)

    def body(*refs):
        ins, outs = refs[:n], refs[n:2 * n]
        send_sems, recv_sems = refs[2 * n:]
        x, y, c, _, _ = _place()
        copies = [pltpu.make_async_remote_copy(
            src_ref=ins[a].at[:, 1 - c], dst_ref=outs[a], send_sem=send_sems.at[a], recv_sem=recv_sems.at[a],
            device_id=(x, y, 1 - c), device_id_type=_MESH_ID) for a in range(n)]
        for cp in copies:
            cp.start()
        for cp in copies:
            cp.wait()

    any_spec = pl.BlockSpec(memory_space=pl.ANY)
    return pl.pallas_call(
        body, in_specs=[any_spec] * n, out_specs=[any_spec] * n,
        out_shape=[jax.ShapeDtypeStruct((N_CHIP,) + a.shape[2:], a.dtype) for a in arrays],
        scratch_shapes=[pltpu.SemaphoreType.DMA((n,)), pltpu.SemaphoreType.DMA((n,))],
        name=name, compiler_params=pltpu.CompilerParams(has_side_effects=True),
    )(*arrays)


def _pair_add(mine, theirs, core, *, name):
    _, _, R, C = mine.shape
    tr = max(t for t in range(16, R + 1, 16) if R % t == 0 and t * C <= 512 * 1024)

    def body(core_ref, m_ref, t_ref, o_ref):
        del core_ref
        o_ref[...] = (m_ref[...].astype(_F32) + t_ref[...].astype(_F32)).astype(o_ref.dtype)

    grid_spec = pltpu.PrefetchScalarGridSpec(
        num_scalar_prefetch=1, grid=(N_CHIP, R // tr),
        in_specs=[pl.BlockSpec((None, None, tr, C), lambda q, i, core: (q, core[0], i, 0)),
                  pl.BlockSpec((None, tr, C), lambda q, i, core: (q, i, 0))],
        out_specs=pl.BlockSpec((None, tr, C), lambda q, i, core: (q, i, 0)))
    return pl.pallas_call(
        body, grid_spec=grid_spec, out_shape=jax.ShapeDtypeStruct((N_CHIP, R, C), mine.dtype), name=name,
        compiler_params=_cparams(dimension_semantics=("parallel", "parallel")),
    )(core.reshape(1), mine, theirs)


_HBM = pl.BlockSpec(memory_space=pltpu.HBM)
_SEM = pl.BlockSpec(memory_space=pltpu.SEMAPHORE)
_TOKEN = pl.BlockSpec(memory_space=pltpu.VMEM)
_DATAFLOW = pltpu.SideEffectType.DATAFLOW_SIDE_EFFECTING


def _hbm(a):
    return pltpu.HBM(a.shape, a.dtype)


def _token_shape():
    return jax.ShapeDtypeStruct((8, 128), _F32)


def _dev_slot(px, py, pc):
    return 4 * px + 2 * py + pc


def _gather_start(blocks, landings, *, name):
    n = len(blocks)

    def body(*refs):
        ins, lands = refs[:n], refs[n:2 * n]
        send_sems, d2d_sems, ici_sems = refs[2 * n:2 * n + 3]
        token = refs[-1]
        x, y, c, _, chips = _place()
        for a in range(n):
            dst = lands[a].at[_dev_slot(x, y, c)]
            pltpu.make_async_remote_copy(src_ref=ins[a], dst_ref=dst, send_sem=send_sems.at[4 * a], recv_sem=d2d_sems.at[a],
                                         device_id=(x, y, 1 - c), device_id_type=_MESH_ID).start()
            for j, chip in enumerate(chips):
                pltpu.make_async_remote_copy(src_ref=ins[a], dst_ref=dst, send_sem=send_sems.at[4 * a + 1 + j],
                                             recv_sem=ici_sems.at[3 * a + j], device_id=(*chip, c),
                                             device_id_type=_MESH_ID).start()
        token[...] = jnp.zeros_like(token)

    res = pl.pallas_call(
        body, name=name, in_specs=[_HBM] * (2 * n),
        out_shape=(pltpu.SemaphoreType.DMA((4 * n,)), pltpu.SemaphoreType.DMA((n,)), pltpu.SemaphoreType.DMA((3 * n,)),
                   *[_hbm(b) for b in blocks], *[_hbm(b) for b in landings], _token_shape()),
        out_specs=(_SEM, _SEM, _SEM, *[_HBM] * (2 * n), _TOKEN),
        input_output_aliases={i: 3 + i for i in range(2 * n)},
        compiler_params=pltpu.CompilerParams(has_side_effects=_DATAFLOW),
    )(*[pltpu.with_memory_space_constraint(b, pltpu.HBM) for b in blocks],
      *[pltpu.with_memory_space_constraint(b, pltpu.HBM) for b in landings])
    return res[0], res[1], res[2], list(res[3:3 + n]), list(res[3 + n:3 + 2 * n]), res[-1]


def _gather_forward(landings, ici_sems, first, after, *, name):
    n = len(landings)

    def body(*refs):
        lands = refs[:n]
        ici = refs[n]
        f_send, f_recv = refs[n + 2], refs[n + 3]
        token = refs[-1]
        x, y, c, _, chips = _place()
        for a in range(n):
            for j, chip in enumerate(chips):
                blk = lands[a].at[_dev_slot(*chip, c)]
                pltpu.make_async_remote_copy(src_ref=blk, dst_ref=blk, send_sem=f_send.at[3 * a + j],
                                             recv_sem=ici.at[3 * (first + a) + j], device_id=(*chip, c),
                                             device_id_type=_MESH_ID).wait_recv()
                pltpu.make_async_remote_copy(src_ref=blk, dst_ref=blk, send_sem=f_send.at[3 * a + j], recv_sem=f_recv.at[3 * a + j],
                                             device_id=(x, y, 1 - c), device_id_type=_MESH_ID).start()
        token[...] = jnp.zeros_like(token)

    res = pl.pallas_call(
        body, name=name, in_specs=[_HBM] * n + [_SEM, pl.BlockSpec(memory_space=pl.ANY)],
        out_shape=(pltpu.SemaphoreType.DMA((3 * n,)), pltpu.SemaphoreType.DMA((3 * n,)), *[_hbm(b) for b in landings], _token_shape()),
        out_specs=(_SEM, _SEM, *[_HBM] * n, _TOKEN),
        input_output_aliases={i: 2 + i for i in range(n)},
        compiler_params=pltpu.CompilerParams(has_side_effects=_DATAFLOW),
    )(*landings, ici_sems, after)
    return res[0], res[1], list(res[2:2 + n]), res[-1]


def _gather_wait(blocks, landings, send_sems, d2d_sems, first, f_send, f_recv, after, *, name):
    n = len(landings)

    def body(*refs):
        ins, lands = refs[:n], refs[n:2 * n]
        send, d2d, fs, fr = refs[2 * n:2 * n + 4]
        x, y, c, _, chips = _place()
        me = (x, y, c)
        for a in range(n):
            own = lands[a].at[_dev_slot(x, y, 1 - c)]
            g = first + a
            pltpu.make_async_remote_copy(src_ref=ins[a], dst_ref=own, send_sem=send.at[4 * g], recv_sem=d2d.at[g],
                                         device_id=me, device_id_type=_MESH_ID).wait_recv()
            for j, chip in enumerate(chips):
                blk = lands[a].at[_dev_slot(*chip, 1 - c)]
                pltpu.make_async_remote_copy(src_ref=blk, dst_ref=blk, send_sem=fs.at[3 * a + j], recv_sem=fr.at[3 * a + j],
                                             device_id=me, device_id_type=_MESH_ID).wait_recv()
            for k in range(4):
                pltpu.make_async_remote_copy(src_ref=ins[a], dst_ref=own, send_sem=send.at[4 * g + k], recv_sem=d2d.at[g],
                                             device_id=me, device_id_type=_MESH_ID).wait_send()
            for j in range(3):
                pltpu.make_async_remote_copy(src_ref=own, dst_ref=own, send_sem=fs.at[3 * a + j], recv_sem=fr.at[3 * a + j],
                                             device_id=me, device_id_type=_MESH_ID).wait_send()

    res = pl.pallas_call(
        body, name=name, in_specs=[_HBM] * (2 * n) + [_SEM] * 4 + [pl.BlockSpec(memory_space=pl.ANY)],
        out_shape=(*[_hbm(b) for b in blocks], *[_hbm(b) for b in landings]), out_specs=tuple([_HBM] * (2 * n)),
        input_output_aliases={i: i for i in range(2 * n)},
        compiler_params=pltpu.CompilerParams(has_side_effects=_DATAFLOW),
    )(*blocks, *landings, send_sems, d2d_sems, f_send, f_recv, after)
    return list(res[n:])


def _swap_start(mine, landings, *, name):
    n = len(mine)

    def body(*refs):
        ins, lands = refs[:n], refs[n:2 * n]
        send_sems, recv_sems = refs[2 * n:2 * n + 2]
        token = refs[-1]
        x, y, c, _, _ = _place()
        for a in range(n):
            pltpu.make_async_remote_copy(src_ref=ins[a].at[:, 1 - c], dst_ref=lands[a], send_sem=send_sems.at[a],
                                         recv_sem=recv_sems.at[a], device_id=(x, y, 1 - c), device_id_type=_MESH_ID).start()
        token[...] = jnp.zeros_like(token)

    res = pl.pallas_call(
        body, name=name, in_specs=[_HBM] * (2 * n),
        out_shape=(pltpu.SemaphoreType.DMA((n,)), pltpu.SemaphoreType.DMA((n,)),
                   *[_hbm(b) for b in mine], *[_hbm(b) for b in landings], _token_shape()),
        out_specs=(_SEM, _SEM, *[_HBM] * (2 * n), _TOKEN),
        input_output_aliases={i: 2 + i for i in range(2 * n)},
        compiler_params=pltpu.CompilerParams(has_side_effects=_DATAFLOW),
    )(*[pltpu.with_memory_space_constraint(b, pltpu.HBM) for b in mine],
      *[pltpu.with_memory_space_constraint(b, pltpu.HBM) for b in landings])
    return res[0], res[1], list(res[2:2 + n]), list(res[2 + n:2 + 2 * n]), res[-1]


def _swap_wait(mine, landings, send_sems, recv_sems, after, *, name):
    n = len(mine)

    def body(*refs):
        ins, lands = refs[:n], refs[n:2 * n]
        send, recv = refs[2 * n:2 * n + 2]
        x, y, c, _, _ = _place()
        for a in range(n):
            cp = pltpu.make_async_remote_copy(src_ref=ins[a].at[:, 1 - c], dst_ref=lands[a], send_sem=send.at[a],
                                              recv_sem=recv.at[a], device_id=(x, y, c), device_id_type=_MESH_ID)
            cp.wait_recv()
            cp.wait_send()

    res = pl.pallas_call(
        body, name=name, in_specs=[_HBM] * (2 * n) + [_SEM] * 2 + [pl.BlockSpec(memory_space=pl.ANY)],
        out_shape=(*[_hbm(b) for b in mine], *[_hbm(b) for b in landings]), out_specs=tuple([_HBM] * (2 * n)),
        input_output_aliases={i: i for i in range(2 * n)},
        compiler_params=pltpu.CompilerParams(has_side_effects=_DATAFLOW),
    )(*mine, *landings, send_sems, recv_sems, after)
    return list(res[:n]), list(res[n:])


def _chip_exchange_start(sums, landings, *, name):
    n = len(sums)

    def body(*refs):
        ins, lands = refs[:n], refs[n:2 * n]
        send_sems, recv_sems = refs[2 * n:2 * n + 2]
        token = refs[-1]
        _, _, c, p, chips = _place()
        for a in range(n):
            for j, (qx, qy) in enumerate(chips):
                pltpu.make_async_remote_copy(src_ref=ins[a].at[2 * qx + qy], dst_ref=lands[a].at[p], send_sem=send_sems.at[3 * a + j],
                                             recv_sem=recv_sems.at[3 * a + j], device_id=(qx, qy, c), device_id_type=_MESH_ID).start()
        token[...] = jnp.zeros_like(token)

    res = pl.pallas_call(
        body, name=name, in_specs=[_HBM] * (2 * n),
        out_shape=(pltpu.SemaphoreType.DMA((3 * n,)), pltpu.SemaphoreType.DMA((3 * n,)),
                   *[_hbm(b) for b in sums], *[_hbm(b) for b in landings], _token_shape()),
        out_specs=(_SEM, _SEM, *[_HBM] * (2 * n), _TOKEN),
        input_output_aliases={i: 2 + i for i in range(2 * n)},
        compiler_params=pltpu.CompilerParams(has_side_effects=_DATAFLOW),
    )(*[pltpu.with_memory_space_constraint(b, pltpu.HBM) for b in sums],
      *[pltpu.with_memory_space_constraint(b, pltpu.HBM) for b in landings])
    return res[0], res[1], list(res[2:2 + n]), list(res[2 + n:2 + 2 * n]), res[-1]


def _chip_exchange_wait(sums, landings, send_sems, recv_sems, after, *, name):
    n = len(sums)

    def body(*refs):
        ins, lands = refs[:n], refs[n:2 * n]
        send, recv = refs[2 * n:2 * n + 2]
        x, y, c, _, chips = _place()
        for a in range(n):
            for j, (qx, qy) in enumerate(chips):
                q = 2 * qx + qy
                cp = pltpu.make_async_remote_copy(src_ref=ins[a].at[q], dst_ref=lands[a].at[q], send_sem=send.at[3 * a + j],
                                                  recv_sem=recv.at[3 * a + j], device_id=(x, y, c), device_id_type=_MESH_ID)
                cp.wait_recv()
                cp.wait_send()

    res = pl.pallas_call(
        body, name=name, in_specs=[_HBM] * (2 * n) + [_SEM] * 2 + [pl.BlockSpec(memory_space=pl.ANY)] * len(after),
        out_shape=(*[_hbm(b) for b in sums], *[_hbm(b) for b in landings]), out_specs=tuple([_HBM] * (2 * n)),
        input_output_aliases={i: i for i in range(2 * n)},
        compiler_params=pltpu.CompilerParams(has_side_effects=_DATAFLOW),
    )(*sums, *landings, send_sems, recv_sems, *after)
    return list(res[:n]), list(res[n:])


_C1 = 1.0 - ADAM_B1 ** ADAM_STEP
_C2 = 1.0 - ADAM_B2 ** ADAM_STEP


def _adamw_math(g, w, m, v):
    m = ADAM_B1 * m + (1.0 - ADAM_B1) * g
    v = ADAM_B2 * v + (1.0 - ADAM_B2) * (g * g)
    delta = -ADAM_LR * ((m / _C1) / (jnp.sqrt(v / _C2) + ADAM_EPS) + ADAM_WD * w)
    return delta, m, v


def _adamw_reduce(landed, sums, chip, w, m, v, layer, prev, *, name):
    _, R, C = w.shape
    tr = max(t for t in range(16, R + 1, 16) if R % t == 0 and t * C <= 256 * 1024)

    def body(chip_ref, p_ref, own_ref, w_ref, m_ref, v_ref, *rest):
        g_ref, d_ref, nm_ref, nv_ref = rest[-4:]
        own = own_ref[...].astype(_F32)
        g = jnp.where(chip_ref[0] == 0, own, p_ref[0].astype(_F32))
        for q in range(1, N_CHIP):
            g = g + jnp.where(chip_ref[0] == q, own, p_ref[q].astype(_F32))
        d, nm, nv = _adamw_math(g, w_ref[...], m_ref[...], v_ref[...])
        g_ref[...] = g
        d_ref[...] = d
        nm_ref[...] = nm
        nv_ref[...] = nv

    blk = pl.BlockSpec((None, tr, C), lambda i, chip: (layer, i, 0))
    shape = jax.ShapeDtypeStruct((DEPTH, R, C), _F32)
    kept = [] if prev is None else list(prev)
    grid_spec = pltpu.PrefetchScalarGridSpec(
        num_scalar_prefetch=1, grid=(R // tr,),
        in_specs=[pl.BlockSpec((N_CHIP, tr, C), lambda i, chip: (0, i, 0)),
                  pl.BlockSpec((None, tr, C), lambda i, chip: (chip[0], i, 0)), blk, blk, blk]
        + [pl.BlockSpec(memory_space=pl.ANY)] * len(kept),
        out_specs=[blk] * 4)
    return pl.pallas_call(
        body, grid_spec=grid_spec, out_shape=[shape] * 4, name=name,
        input_output_aliases={6 + k: k for k in range(len(kept))},
        compiler_params=_cparams(dimension_semantics=("parallel",)),
    )(chip.reshape(1), landed, sums, w, m, v, *kept)


_PACK_LANES = 128
_LAYER_ROWS = 248
_LB_ROWS = (A_HEADS * HEAD_DIM) // _PACK_LANES


def _small_reduce(parts, lb_logits, *, name):
    rows = DEPTH * _LAYER_ROWS

    def body(p_ref, lg_ref, o_ref):
        g = p_ref[0]
        for s in range(1, N_DEV):
            g = g + p_ref[s]
        o_ref[...] = g
        lg = lg_ref[...]
        e = jnp.exp(lg - jnp.max(lg, axis=0, keepdims=True))
        p = e / jnp.sum(e, axis=0, keepdims=True)
        d1 = g[_LAYER_ROWS:_LAYER_ROWS + _LB_ROWS, :] * p[0] * p[1]
        o_ref[0:_LB_ROWS, :] = -d1
        o_ref[_LAYER_ROWS:_LAYER_ROWS + _LB_ROWS, :] = d1

    return pl.pallas_call(
        body, out_shape=jax.ShapeDtypeStruct((rows, _PACK_LANES), _F32), name=name,
        compiler_params=_cparams(),
    )(parts, lb_logits.reshape(DEPTH, _LB_ROWS, _PACK_LANES))


def _adamw_small(g, w, m, v, *, name):
    def body(g_ref, w_ref, m_ref, v_ref, d_ref, nm_ref, nv_ref):
        d, nm, nv = _adamw_math(g_ref[...], w_ref[...], m_ref[...], v_ref[...])
        d_ref[...] = d
        nm_ref[...] = nm
        nv_ref[...] = nv

    shape = jax.ShapeDtypeStruct(g.shape, _F32)
    return pl.pallas_call(body, out_shape=[shape] * 3, name=name, compiler_params=_cparams())(g, w, m, v)


def _pack(vectors, rows):
    flat = jnp.concatenate([v.reshape(-1).astype(_F32) for v in vectors])
    return jnp.pad(flat, (0, rows * _PACK_LANES - flat.shape[0])).reshape(rows, _PACK_LANES)


def _unpack(packed, shapes):
    flat = packed.reshape(-1)
    out, at = [], 0
    for s in shapes:
        size = int(np.prod(s))
        out.append(flat[at:at + size].reshape(s))
        at += size
    return out


_BIG = ("w_in", "w_gate", "w_up", "w_out", "w_down")
_COLUMN_SHARDED = ("w_in", "w_gate", "w_up")


def _full_weight(name, g):
    if name == "conv_w":
        return g.transpose(1, 0, 2).reshape(g.shape[1], N_DEV * SHARD_COLS)
    if name in _BIG:
        return g.reshape(N_DEV * g.shape[1], g.shape[2])
    return g


class _WeightGather:
    def __init__(self, names, first, blocks, lands, sems, tag):
        self.names, self.first, self.blocks, self.lands, self.sems, self.tag = names, first, blocks, lands, sems, tag
        self.forwarded = None

    def forward(self, after):
        f_send, f_recv, self.lands, token = _gather_forward(self.lands, self.sems[2], self.first, after,
                                                            name=f"gather_forward_{self.tag}")
        self.forwarded = (f_send, f_recv)
        return token

    def wait(self, after):
        if self.forwarded is None:
            self.forward(after)
        got = _gather_wait(self.blocks, self.lands, self.sems[0], self.sems[1], self.first, *self.forwarded, after,
                           name=f"gather_wait_{self.tag}")
        return {n: _full_weight(n, g) for n, g in zip(self.names, got)}


def _start_gathers(groups, me, name):
    blocks = [b for _, _, bs in groups for b in bs]
    landings = [lax.dynamic_update_index_in_dim(lax.empty((N_DEV,) + b.shape, b.dtype), b[None], me, 0) for b in blocks]
    send, d2d, ici, blocks, landings, token = _gather_start(blocks, landings, name=name)
    out, first = [], 0
    for tag, names, bs in groups:
        k = len(bs)
        out.append(_WeightGather(names, first, blocks[first:first + k], landings[first:first + k], (send, d2d, ici), tag))
        first += k
    return out, token


class _LayerWeights:
    def __init__(self, ready, pending=(), forwards=(), tokens=()):
        self.ready, self.pending, self.forwards, self._tokens = dict(ready), list(pending), list(forwards), list(tokens)

    def at(self, point, after):
        for when, gather in self.forwards:
            if when == point:
                self._tokens.append(gather.forward(after))

    def tokens(self):
        out, self._tokens = self._tokens, []
        return out

    def get(self, name, after):
        if name not in self.ready:
            group, = [g for g in self.pending if name in g.names]
            self.ready.update(group.wait(after))
        return self.ready[name]


def _layer_fwd(x, xb, ws, lb_logits, a_norm_w, c_sink, ln1_g, ln1_b, conv_b, ln2_g, ln2_b, tabs, l, target=None):
    proj = _mm(xb, ws.get("w_in", xb), tb=True, **_TILE_WIDE_N, after=ws.tokens(), name=f"proj_{l}")
    o_a, raw, states = _hgrn_fwd(proj, lb_logits, a_norm_w, l, name=f"hgrn_fwd_{l}")
    ws.at("hgrn", o_a)
    o_b, lse_b = _band_fwd(proj, tabs, name=f"dilated_fwd_{l}", **_DILATED)
    o_c, lse_c = _band_fwd(proj, tabs, sink=c_sink, name=f"swa_fwd_{l}", **_SWA)
    ws.at("swa", o_c)
    mixed = _concat_cols([o_a, o_b, o_c], name=f"mixed_{l}")
    y = _mm(mixed, ws.get("w_out", mixed), **_TILE_MIX, after=ws.tokens(), name=f"mix_out_{l}")
    z1, x1, x1b = _ln_fwd(x, y, ln1_g, ln1_b, name=f"ln1_fwd_{l}")
    g = _mm(x1b, ws.get("w_gate", x1b), tb=True, **_TILE_WIDE_N, out_dtype=_ACT_DTYPE, name=f"ffn_gate_{l}")
    u = _mm(x1b, ws.get("w_up", x1b), tb=True, **_TILE_WIDE_N, out_dtype=_ACT_DTYPE, name=f"ffn_up_{l}")
    ws.at("up", u)
    hb = _conv_gate_fwd(g, u, ws.get("conv_w", u), conv_b, name=f"conv_gate_fwd_{l}")
    y2 = _mm(hb, ws.get("w_down", hb), **_TILE_WIDE_K, after=ws.tokens(), name=f"ffn_down_{l}")
    ws.at("down", y2)
    res = dict(xb=xb, proj=proj, raw=raw, states=states, o_b=o_b, lse_b=lse_b, o_c=o_c, lse_c=lse_c,
               mixed=mixed, z1=z1, x1b=x1b, g=g, u=u, hb=hb)
    if target is not None:
        loss_part, *res["ln2_bwd"] = _ln_loss_bwd(x1, y2, ln2_g, ln2_b, target, name=f"ln2_loss_{l}")
        return loss_part, None, res
    res["z2"], x2, x2b = _ln_fwd(x1, y2, ln2_g, ln2_b, name=f"ln2_fwd_{l}")
    return x2, x2b, res


class _GradExchange:
    def __init__(self, core, chip):
        self.core, self.chip, self.groups, self.swapping, self._tokens = core, chip, [], [], []

    def launch(self, names, slabs, l, tag, behind):
        mine = [s.reshape((N_CHIP, 2) + s.shape[1:]) for s in slabs]
        if behind:
            landings = [lax.empty((N_CHIP,) + m.shape[2:], m.dtype) for m in mine]
            send, recv, mine, landings, token = _swap_start(mine, landings, name=f"swap_start_{tag}")
            self.swapping.append((names, l, tag, send, recv, mine, landings))
            self._tokens.append(token)
        else:
            self._exchange(names, l, tag, mine, _sibling_swap(mine, name=f"swap_grads_{tag}"))

    def advance(self, after):
        for names, l, tag, send, recv, mine, landings in self.swapping:
            mine, theirs = _swap_wait(mine, landings, send, recv, after, name=f"swap_wait_{tag}")
            self._exchange(names, l, tag, mine, theirs)
        self.swapping = []

    def _exchange(self, names, l, tag, mine, theirs):
        sums = [_pair_add(a, b, self.core, name=f"pair_add_{n}_{l}") for n, a, b in zip(names, mine, theirs)]
        landings = [lax.empty(s.shape, s.dtype) for s in sums]
        send, recv, sums, landings, token = _chip_exchange_start(sums, landings, name=f"exchange_start_{tag}")
        self.groups.append((names, l, tag, send, recv, sums, landings))
        self._tokens.append(token)

    def tokens(self):
        out, self._tokens = self._tokens, []
        return out

    def finish(self, weights, mom1, mom2, after):
        out = {}
        after = list(after) + self.tokens()
        for names, l, tag, send, recv, sums, landings in self.groups:
            sums, landings = _chip_exchange_wait(sums, landings, send, recv, after, name=f"exchange_wait_{tag}")
            for n, s, landed in zip(names, sums, landings):
                out[n] = _adamw_reduce(landed, s, self.chip, weights[n], mom1[n], mom2[n], l, out.get(n), name=f"adamw_{n}_{l}")
            after = [out[n][0] for n in names]
        return out


def _layer_bwd(dx2, res, w, lb_logits, a_norm_w, c_sink, ln1_g, conv_b, ln2_g, tabs, exchange, l):
    if "ln2_bwd" in res:
        dz2, dz2b, d_ln2_g, d_ln2_b = res["ln2_bwd"]
    else:
        dz2, dz2b, d_ln2_g, d_ln2_b = _ln_bwd(res["z2"], dx2, None, ln2_g, name=f"ln2_bwd_{l}")
    exchange.advance(dz2b)
    dh = _mm(dz2b, w["w_down"], tb=True, **_TILE_WIDE_N, out_dtype=_ACT_DTYPE, after=exchange.tokens(),
             name=f"ffn_down_dx_{l}")
    d_w_down = _mm(res["hb"], dz2b, ta=True, **_TILE_WIDE_M, out_dtype=_GRAD_DTYPE, name=f"ffn_down_dw_{l}")
    dg, du, d_conv_w, d_conv_b = _conv_gate_bwd(dh, res["g"], res["u"], w["conv_w"], conv_b, name=f"conv_gate_bwd_{l}")
    t = _mm(dg, w["w_gate"], **_TILE_WIDE_K, name=f"ffn_gate_dx_{l}")
    dx1 = _mm(du, w["w_up"], **_TILE_WIDE_K, add=t, name=f"ffn_up_dx_{l}")
    d_w_gate = _mm(dg, res["x1b"], ta=True, **_TILE_WIDE_M, out_dtype=_GRAD_DTYPE, name=f"ffn_gate_dw_{l}")
    d_w_up = _mm(du, res["x1b"], ta=True, **_TILE_WIDE_M, out_dtype=_GRAD_DTYPE, name=f"ffn_up_dw_{l}")
    dz1, dz1b, d_ln1_g, d_ln1_b = _ln_bwd(res["z1"], dx1, dz2, ln1_g, name=f"ln1_bwd_{l}")
    d_w_out = _mm(res["mixed"], dz1b, ta=True, **_TILE_MIX, out_dtype=_GRAD_DTYPE, name=f"mix_out_dw_{l}")
    exchange.launch(("w_down", "w_gate", "w_up", "w_out"),
                    [d.reshape(N_DEV, d.shape[0] // N_DEV, D_MODEL) for d in (d_w_down, d_w_gate, d_w_up, d_w_out)],
                    l, f"ffn_{l}", True)
    dmixed = _mm(dz1b, w["w_out"], tb=True, **_TILE_MIX, after=exchange.tokens(), name=f"mix_out_dx_{l}")
    dq_a, df_a, di_a, dg_a, d_norm_w, d_lb = _hgrn_bwd(res["proj"], lb_logits, a_norm_w, res["raw"], res["states"],
                                                      dmixed, l, name=f"hgrn_bwd_{l}")
    exchange.advance(dq_a)
    dq_b, dk_b, dv_b = _band_bwd(res["proj"], tabs, dmixed, res["o_b"], res["lse_b"], do0=A_HEADS, after=exchange.tokens(),
                                 name=f"dilated_bwd_{l}", **_DILATED)
    dq_c, dk_c, dv_c, d_sink = _band_bwd(res["proj"], tabs, dmixed, res["o_c"], res["lse_c"], do0=A_HEADS + B_HEADS,
                                         sink=c_sink, name=f"swa_bwd_{l}", **_SWA)
    dproj = _concat_cols([dq_a, df_a, di_a, dg_a, dq_b, dk_b, dv_b, dq_c, dk_c, dv_c], name=f"dproj_{l}")
    d_w_in = _mm(dproj, res["xb"], ta=True, **_TILE_WIDE_M, out_dtype=_GRAD_DTYPE, name=f"proj_dw_{l}")
    exchange.launch(("w_in",), [d_w_in.reshape(N_DEV, SHARD_COLS, D_MODEL)], l, f"mix_{l}", l > 0)
    dx = _mm(dproj, w["w_in"], **_TILE_WIDE_K, add=dz1, add_scale=ALPHA, after=exchange.tokens(), name=f"proj_dx_{l}")
    small = [d_lb, d_norm_w, jnp.pad(d_sink, (0, _PACK_LANES - C_HEADS)), d_ln1_g, d_ln1_b, d_ln2_g, d_ln2_b, d_conv_b,
             d_conv_w]
    return dx, small


def kernel(x, w_in, lb_logits, a_norm_w, c_sinks, w_out, ln1_g, ln1_b, w_gate, w_up, conv_w, conv_b, w_down, ln2_g, ln2_b, loss_target, m_w_in, m_lb_logits, m_a_norm_w, m_c_sinks, m_w_out, m_ln1_g, m_ln1_b, m_w_gate, m_w_up, m_conv_w, m_conv_b, m_w_down, m_ln2_g, m_ln2_b, v_w_in, v_lb_logits, v_a_norm_w, v_c_sinks, v_w_out, v_ln1_g, v_ln1_b, v_w_gate, v_w_up, v_conv_w, v_conv_b, v_w_down, v_ln2_g, v_ln2_b):
    weights = dict(w_in=w_in, lb_logits=lb_logits, a_norm_w=a_norm_w, c_sinks=c_sinks, w_out=w_out, ln1_g=ln1_g, ln1_b=ln1_b,
                   w_gate=w_gate, w_up=w_up, conv_w=conv_w, conv_b=conv_b, w_down=w_down, ln2_g=ln2_g, ln2_b=ln2_b)
    mom1 = dict(w_in=m_w_in, lb_logits=m_lb_logits, a_norm_w=m_a_norm_w, c_sinks=m_c_sinks, w_out=m_w_out, ln1_g=m_ln1_g,
                ln1_b=m_ln1_b, w_gate=m_w_gate, w_up=m_w_up, conv_w=m_conv_w, conv_b=m_conv_b, w_down=m_w_down, ln2_g=m_ln2_g,
                ln2_b=m_ln2_b)
    mom2 = dict(w_in=v_w_in, lb_logits=v_lb_logits, a_norm_w=v_a_norm_w, c_sinks=v_c_sinks, w_out=v_w_out, ln1_g=v_ln1_g,
                ln1_b=v_ln1_b, w_gate=v_w_gate, w_up=v_w_up, conv_w=v_conv_w, conv_b=v_conv_b, w_down=v_w_down, ln2_g=v_ln2_g,
                ln2_b=v_ln2_b)
    core = lax.axis_index("c").astype(jnp.int32)
    me = 4 * lax.axis_index("x") + 2 * lax.axis_index("y") + core
    tabs = _rope_tables()

    chip = (2 * lax.axis_index("x") + lax.axis_index("y")).astype(jnp.int32)

    def as_slabs(d):
        return {n: jnp.swapaxes(d[n], 1, 2) if n in _COLUMN_SHARDED else d[n] for n in _BIG}

    w_views = as_slabs(weights)

    def block(n, l, after=()):
        return conv_w[l] if n == "conv_w" else _cast_layer(w_views[n], l, after=after, name=f"cast_{n}_{l}")

    (in0,), started_first = _start_gathers([("w_in_0", ("w_in",), [block("w_in", 0)])], me, "gather_start_first")
    order = [(("w_out",), 0), (("w_gate", "w_up", "conv_w"), 0), (("w_down",), 0),
             (("w_in",), 1), (("w_out",), 1), (("w_gate", "w_up", "conv_w"), 1), (("w_down",), 1)]
    gathers, started = _start_gathers([(f"{names[0]}_{l}", names, [block(n, l, [started_first]) for n in names])
                                       for names, l in order], me, "gather_start_rest")
    out0, ffn0, down0, in1, out1, ffn1, down1 = gathers
    layer_ws = [_LayerWeights(in0.wait(started), [out0, ffn0, down0],
                              [("hgrn", out0), ("swa", ffn0), ("up", down0), ("down", in1)]),
                _LayerWeights({}, [in1, out1, ffn1, down1], [("hgrn", out1), ("swa", ffn1), ("up", down1)])]

    xs = x[0]
    xb = xs.astype(_MXU_DTYPE)
    saved = []
    for l in range(DEPTH):
        xs, xb, res = _layer_fwd(xs, xb, layer_ws[l], lb_logits, a_norm_w[l], c_sinks[l], ln1_g[l], ln1_b[l], conv_b[l],
                                 ln2_g[l], ln2_b[l], tabs, l, loss_target[0] if l == DEPTH - 1 else None)
        saved.append(res)
    loss = lax.psum(xs, ("x", "y", "c"))
    dx = None

    exchange = _GradExchange(core, chip)
    small_parts = [None] * DEPTH
    for l in reversed(range(DEPTH)):
        dx, small = _layer_bwd(dx, saved[l], layer_ws[l].ready, lb_logits, a_norm_w[l], c_sinks[l], ln1_g[l], conv_b[l],
                               ln2_g[l], tabs, exchange, l)
        small_parts[l] = _pack(small, _LAYER_ROWS)
    (small_gather,), small_started = _start_gathers(
        [("small_grads", ("small",), [jnp.concatenate(small_parts, axis=0)])], me, "gather_start_small")
    updated = exchange.finish(w_views, as_slabs(mom1), as_slabs(mom2), [dx, small_started])
    gathered = small_gather.wait(updated["w_in"][0])["small"]
    updated = {n: tuple(jnp.swapaxes(t, 1, 2) for t in u) if n in _COLUMN_SHARDED else u for n, u in updated.items()}
    g_small = _small_reduce(gathered, lb_logits, name="small_grads")

    per_layer = [(A_HEADS * HEAD_DIM,), (HEAD_DIM,), (_PACK_LANES,), (D_MODEL,), (D_MODEL,), (D_MODEL,), (D_MODEL,), (D_FF,),
                 (3, D_FF)]
    names = ("lb_logits", "a_norm_w", "c_sinks", "ln1_g", "ln1_b", "ln2_g", "ln2_b", "conv_b", "conv_w")
    grads = {n: [] for n in names}
    for l in range(DEPTH):
        for n, t in zip(names, _unpack(g_small[l * _LAYER_ROWS:(l + 1) * _LAYER_ROWS], per_layer)):
            grads[n].append(t)
    grads = {n: jnp.stack(t) for n, t in grads.items()}
    grads["c_sinks"] = grads["c_sinks"][:, :C_HEADS]
    grads["conv_w"] = lax.dynamic_slice_in_dim(grads["conv_w"], me * SHARD_COLS, SHARD_COLS, axis=2)
    shapes = [grads[n].shape for n in names]
    rows = -(-sum(int(np.prod(s)) for s in shapes) // (8 * _PACK_LANES)) * 8
    d_s, m_s, v_s = _adamw_small(_pack([grads[n] for n in names], rows), _pack([weights[n] for n in names], rows),
                                 _pack([mom1[n] for n in names], rows), _pack([mom2[n] for n in names], rows),
                                 name="adamw_small")
    delta = dict(zip(names, _unpack(d_s, shapes)))
    new_m = dict(zip(names, _unpack(m_s, shapes)))
    new_v = dict(zip(names, _unpack(v_s, shapes)))
    for n in _BIG:
        grads[n], delta[n], new_m[n], new_v[n] = updated[n]

    order = ("w_in", "lb_logits", "a_norm_w", "c_sinks", "w_out", "ln1_g", "ln1_b", "w_gate", "w_up", "conv_w", "conv_b",
             "w_down", "ln2_g", "ln2_b")
    return (loss, dx[None], *[grads[n] for n in order], *[delta[n] for n in order], *[new_m[n] for n in order],
            *[new_v[n] for n in order])
```
